```python
import jax, jax.numpy as jnp
from jax import lax
import numpy as np

D_MODEL = 1024
BATCH = 8
SEQ = 2048
DEPTH = 4

EPS = 1e-6
F_FLOOR = 1e-30
GDN_HEADS = 4
GDN_HEAD_DIM = 128
GDN_WIDTH = GDN_HEADS * GDN_HEAD_DIM
GDN_CONV = 4
GDN_CHUNK = 64
HGRN_HEADS = 4
HGRN_STATE = 128
HGRN_HEAD_DIM = 128
HGRN_KEY_WIDTH = HGRN_HEADS * HGRN_STATE
HGRN_VAL_WIDTH = HGRN_HEADS * HGRN_HEAD_DIM
HGRN_CHUNK = 16
MIX_WIDTH = GDN_WIDTH + HGRN_VAL_WIDTH
AB_SIZES = (3 * GDN_WIDTH, GDN_WIDTH, GDN_HEADS, GDN_HEADS,
            HGRN_KEY_WIDTH, HGRN_KEY_WIDTH, HGRN_VAL_WIDTH, HGRN_VAL_WIDTH)
AB_COLS = sum(AB_SIZES)
LRU_WIDTH = D_MODEL
LRU_HEADS = 4
LRU_BLOCK = LRU_WIDTH // LRU_HEADS
LRU_CONV = 4
RG_C = 8.0
D_FF = 2816
FFN_CONV = 3
N_EVEN = (DEPTH + 1) // 2
N_ODD = DEPTH // 2

kernel_name = 'hybrid_gdn_hgrn2_rglru_convffn'


def _rmsnorm(x, gain):
    x32 = x.astype(jnp.float32)
    y = x32 * lax.rsqrt(jnp.mean(x32 * x32, axis=-1, keepdims=True) + EPS)
    return (y * gain.astype(jnp.float32)).astype(x.dtype)


def _l2norm(x):
    return x * lax.rsqrt(jnp.sum(x * x, axis=-1, keepdims=True) + EPS)


def _causal_dwconv(x, w):
    width, ch = w.shape
    return lax.conv_general_dilated(
        x, w[:, None, :].astype(x.dtype), window_strides=(1,), padding=[(width - 1, 0)],
        dimension_numbers=('NWC', 'WIO', 'NWC'), feature_group_count=ch)


def _heads(t, nh):
    b, s, _ = t.shape
    return t.reshape(b, s, nh, -1).transpose(0, 2, 1, 3)


def _masked_exp(logits, mask):
    return jnp.where(mask, jnp.exp(jnp.where(mask, logits, 0.0)), 0.0)


def _gated_delta_rule(q, k, v, g, beta):
    bsz, nh, seq, dk = q.shape
    dv = v.shape[-1]
    c = GDN_CHUNK
    n = seq // c
    q, k = (t.reshape(bsz, nh, n, c, dk) for t in (q, k))
    v = v.reshape(bsz, nh, n, c, dv)
    g, beta = (t.reshape(bsz, nh, n, c) for t in (g, beta))
    gc = jnp.cumsum(g, axis=-1)
    causal = jnp.tril(jnp.ones((c, c), dtype=bool))
    decay = _masked_exp(gc[..., :, None] - gc[..., None, :], causal)
    k_beta = k * beta[..., None]
    lower = jnp.tril(jnp.einsum('bhnik,bhnjk->bhnij', k_beta, k) * decay, -1)
    rhs = jnp.concatenate([v * beta[..., None], k_beta * jnp.exp(gc)[..., None]], axis=-1)
    sol = lax.linalg.triangular_solve(lower + jnp.eye(c, dtype=q.dtype), rhs, left_side=True,
                                      lower=True, unit_diagonal=True)
    u, w = sol[..., :dv], sol[..., dv:]
    attn = jnp.einsum('bhnik,bhnjk->bhnij', q, k) * decay
    q_dec = q * jnp.exp(gc)[..., None]
    k_dec = k * jnp.exp(gc[..., -1:] - gc)[..., None]
    chunk_decay = jnp.exp(gc[..., -1])

    def step(state, xs):
        u_c, w_c, attn_c, q_c, k_c, d_c = xs
        v_new = u_c - jnp.einsum('bhck,bhkv->bhcv', w_c, state)
        o_c = (jnp.einsum('bhck,bhkv->bhcv', q_c, state)
               + jnp.einsum('bhij,bhjv->bhiv', attn_c, v_new))
        state = state * d_c[..., None, None] + jnp.einsum('bhck,bhcv->bhkv', k_c, v_new)
        return state, o_c

    xs = tuple(jnp.moveaxis(t, 2, 0) for t in (u, w, attn, q_dec, k_dec, chunk_decay))
    _, o = lax.scan(step, jnp.zeros((bsz, nh, dk, dv), q.dtype), xs)
    return jnp.moveaxis(o, 0, 2).reshape(bsz, nh, seq, dv)


def _chunk_gla(q, k, v, log_f):
    bsz, nh, seq, dk = q.shape
    dv = v.shape[-1]
    c = HGRN_CHUNK
    n = seq // c
    q, k, log_f = (t.reshape(bsz, nh, n, c, dk) for t in (q, k, log_f))
    v = v.reshape(bsz, nh, n, c, dv)
    b = jnp.cumsum(log_f, axis=3)
    causal = jnp.tril(jnp.ones((c, c), dtype=bool))[:, :, None]
    rel = _masked_exp(b[..., :, None, :] - b[..., None, :, :], causal)
    scores = jnp.sum(q[..., :, None, :] * k[..., None, :, :] * rel, axis=-1)
    o_intra = jnp.einsum('bhnij,bhnjv->bhniv', scores, v)
    b_last = b[..., -1:, :]
    q_dec = q * jnp.exp(b)
    k_dec = k * jnp.exp(b_last - b)
    chunk_decay = jnp.exp(b_last[..., 0, :])

    def step(state, xs):
        q_c, k_c, v_c, d_c = xs
        o_c = jnp.einsum('bhck,bhkv->bhcv', q_c, state)
        state = state * d_c[..., None] + jnp.einsum('bhck,bhcv->bhkv', k_c, v_c)
        return state, o_c

    xs = tuple(jnp.moveaxis(t, 2, 0) for t in (q_dec, k_dec, v, chunk_decay))
    _, o_inter = lax.scan(step, jnp.zeros((bsz, nh, dk, dv), q.dtype), xs)
    return (o_intra + jnp.moveaxis(o_inter, 0, 2)).reshape(bsz, nh, seq, dv)


def _even_mixer(h, w_in, conv_w, a_log, dt_bias, gdn_gain, lower_bound, hgrn_gain, w_out):
    f32 = jnp.float32
    bsz, seq, _ = h.shape
    p = h @ w_in
    offsets = [int(o) for o in np.cumsum(AB_SIZES)[:-1]]
    qkv_a, z_a, beta_a, alpha_a, q_b, f_b, i_b, g_b = jnp.split(p, offsets, axis=-1)
    qkv = jax.nn.silu(_causal_dwconv(qkv_a, conv_w))
    q, k, v = (_heads(t, GDN_HEADS).astype(f32) for t in jnp.split(qkv, 3, axis=-1))
    q = _l2norm(q) * (GDN_HEAD_DIM ** -0.5)
    k = _l2norm(k)
    beta = jax.nn.sigmoid(beta_a.astype(f32)).transpose(0, 2, 1)
    g = (-jnp.exp(a_log.astype(f32))
         * jax.nn.softplus(alpha_a.astype(f32) + dt_bias.astype(f32))).transpose(0, 2, 1)
    o_a = _gated_delta_rule(q, k, v, g, beta).transpose(0, 2, 1, 3)
    z = jax.nn.silu(z_a.astype(f32)).reshape(bsz, seq, GDN_HEADS, GDN_HEAD_DIM)
    o_a = _rmsnorm(o_a, gdn_gain) * z
    f = lower_bound + (1.0 - lower_bound) * jax.nn.sigmoid(f_b.astype(f32))
    log_f = jnp.log(jnp.maximum(f, F_FLOOR))
    k_b = 1.0 - f
    q_b = jax.nn.silu(q_b.astype(f32))
    o_b = _chunk_gla(_heads(q_b, HGRN_HEADS), _heads(k_b, HGRN_HEADS),
                     _heads(i_b.astype(f32), HGRN_HEADS), _heads(log_f, HGRN_HEADS))
    o_b = o_b.transpose(0, 2, 1, 3)
    gate_b = jax.nn.silu(g_b.astype(f32)).reshape(bsz, seq, HGRN_HEADS, HGRN_HEAD_DIM)
    o_b = _rmsnorm(o_b, hgrn_gain) * gate_b
    o = jnp.concatenate([o_a.reshape(bsz, seq, GDN_WIDTH),
                         o_b.reshape(bsz, seq, HGRN_VAL_WIDTH)], axis=-1).astype(h.dtype)
    return o @ w_out


def _rglru_block(h, w_in, conv_w, conv_b, gate_a_w, gate_a_b, gate_x_w, gate_x_b, lam, w_out):
    f32 = jnp.float32
    bsz, seq, _ = h.shape
    y_branch, x_branch = jnp.split(h @ w_in, 2, axis=-1)
    gate = jax.nn.gelu(y_branch.astype(f32), approximate=True)
    xc = (_causal_dwconv(x_branch, conv_w) + conv_b).astype(f32)
    xb = xc.reshape(bsz, seq, LRU_HEADS, LRU_BLOCK)
    r = jax.nn.sigmoid(jnp.einsum('bthi,hij->bthj', xb, gate_a_w.astype(f32)).reshape(bsz, seq, LRU_WIDTH)
                       + gate_a_b.astype(f32))
    i = jax.nn.sigmoid(jnp.einsum('bthi,hij->bthj', xb, gate_x_w.astype(f32)).reshape(bsz, seq, LRU_WIDTH)
                       + gate_x_b.astype(f32))
    log_a = -RG_C * r * jax.nn.softplus(-lam.astype(f32))
    a = jnp.exp(log_a)
    u = jnp.sqrt(jnp.maximum(-jnp.expm1(2.0 * log_a), 0.0)) * (i * xc)

    def combine(left, right):
        a_l, b_l = left
        a_r, b_r = right
        return a_l * a_r, a_r * b_l + b_r

    _, hs = lax.associative_scan(combine, (a, u), axis=1)
    return (hs * gate).astype(h.dtype) @ w_out


def _conv_ffn(h, w_up, conv_w, conv_b, w_down):
    gate, val = jnp.split(h @ w_up, 2, axis=-1)
    gate = _causal_dwconv(gate, conv_w) + conv_b
    return (jax.nn.silu(gate) * val) @ w_down


def _fwd_setup_inputs(seed: int = 0) -> dict:
    key = jax.random.key(seed)
    ks = iter(jax.random.split(key, 40))
    f32 = jnp.float32

    def nrm(shape, scale):
        return jax.random.normal(next(ks), shape, f32) * scale

    def uni(shape, lo, hi):
        return jax.random.uniform(next(ks), shape, f32, lo, hi)

    x = nrm((BATCH, SEQ, D_MODEL), 1.0)
    norm_mix = 1.0 + nrm((DEPTH, D_MODEL), 0.02)
    norm_ffn = 1.0 + nrm((DEPTH, D_MODEL), 0.02)
    norm_final = 1.0 + nrm((D_MODEL,), 0.02)
    ab_w_in = nrm((N_EVEN, D_MODEL, AB_COLS), D_MODEL ** -0.5)
    gdn_conv_w = nrm((N_EVEN, GDN_CONV, 3 * GDN_WIDTH), GDN_CONV ** -0.5)
    gdn_a_log = jnp.log(uni((N_EVEN, GDN_HEADS), 1.0, 16.0))
    dt = jnp.exp(uni((N_EVEN, GDN_HEADS), float(np.log(1e-3)), float(np.log(1e-1))))
    gdn_dt_bias = dt + jnp.log(-jnp.expm1(-dt))
    gdn_norm = 1.0 + nrm((N_EVEN, GDN_HEAD_DIM), 0.02)
    hgrn_lower_bounds = 1.0 + nrm((N_EVEN, HGRN_KEY_WIDTH), 0.1)
    hgrn_norm = 1.0 + nrm((N_EVEN, HGRN_HEAD_DIM), 0.02)
    ab_w_out = nrm((N_EVEN, MIX_WIDTH, D_MODEL), MIX_WIDTH ** -0.5)
    c_w_in = nrm((N_ODD, D_MODEL, 2 * LRU_WIDTH), D_MODEL ** -0.5)
    c_conv_w = nrm((N_ODD, LRU_CONV, LRU_WIDTH), LRU_CONV ** -0.5)
    c_conv_b = nrm((N_ODD, LRU_WIDTH), 0.01)
    c_gate_a_w = nrm((N_ODD, LRU_HEADS, LRU_BLOCK, LRU_BLOCK), LRU_BLOCK ** -0.5)
    c_gate_a_b = nrm((N_ODD, LRU_WIDTH), 0.01)
    c_gate_x_w = nrm((N_ODD, LRU_HEADS, LRU_BLOCK, LRU_BLOCK), LRU_BLOCK ** -0.5)
    c_gate_x_b = nrm((N_ODD, LRU_WIDTH), 0.01)
    a_c = uni((N_ODD, LRU_WIDTH), 0.9, 0.999) ** (1.0 / RG_C)
    c_lambda = jnp.log(a_c) - jnp.log1p(-a_c)
    c_w_out = nrm((N_ODD, LRU_WIDTH, D_MODEL), LRU_WIDTH ** -0.5)
    ffn_w_up = nrm((DEPTH, D_MODEL, 2 * D_FF), D_MODEL ** -0.5)
    ffn_conv_w = nrm((DEPTH, FFN_CONV, D_FF), FFN_CONV ** -0.5)
    ffn_conv_b = nrm((DEPTH, D_FF), 0.01)
    ffn_w_down = nrm((DEPTH, D_FF, D_MODEL), D_FF ** -0.5)
    return {'x': x, 'norm_mix': norm_mix, 'norm_ffn': norm_ffn, 'norm_final': norm_final,
            'ab_w_in': ab_w_in, 'gdn_conv_w': gdn_conv_w, 'gdn_a_log': gdn_a_log,
            'gdn_dt_bias': gdn_dt_bias, 'gdn_norm': gdn_norm, 'hgrn_lower_bounds': hgrn_lower_bounds,
            'hgrn_norm': hgrn_norm, 'ab_w_out': ab_w_out, 'c_w_in': c_w_in, 'c_conv_w': c_conv_w,
            'c_conv_b': c_conv_b, 'c_gate_a_w': c_gate_a_w, 'c_gate_a_b': c_gate_a_b,
            'c_gate_x_w': c_gate_x_w, 'c_gate_x_b': c_gate_x_b, 'c_lambda': c_lambda,
            'c_w_out': c_w_out, 'ffn_w_up': ffn_w_up, 'ffn_conv_w': ffn_conv_w,
            'ffn_conv_b': ffn_conv_b, 'ffn_w_down': ffn_w_down}


def _fwd_reference(x, norm_mix, norm_ffn, norm_final, ab_w_in, gdn_conv_w, gdn_a_log, gdn_dt_bias,
              gdn_norm, hgrn_lower_bounds, hgrn_norm, ab_w_out, c_w_in, c_conv_w, c_conv_b,
              c_gate_a_w, c_gate_a_b, c_gate_x_w, c_gate_x_b, c_lambda, c_w_out,
              ffn_w_up, ffn_conv_w, ffn_conv_b, ffn_w_down):
    lb_p = jax.nn.softmax(hgrn_lower_bounds.astype(jnp.float32), axis=0)
    lower_bounds = jnp.cumsum(lb_p, axis=0) - lb_p[0]
    for layer in range(DEPTH):
        j = layer // 2
        h = _rmsnorm(x, norm_mix[layer])
        if layer % 2 == 0:
            mix = _even_mixer(h, ab_w_in[j], gdn_conv_w[j], gdn_a_log[j], gdn_dt_bias[j],
                              gdn_norm[j], lower_bounds[j], hgrn_norm[j], ab_w_out[j])
        else:
            mix = _rglru_block(h, c_w_in[j], c_conv_w[j], c_conv_b[j], c_gate_a_w[j],
                               c_gate_a_b[j], c_gate_x_w[j], c_gate_x_b[j], c_lambda[j], c_w_out[j])
        x = x + mix
        h = _rmsnorm(x, norm_ffn[layer])
        x = x + _conv_ffn(h, ffn_w_up[layer], ffn_conv_w[layer], ffn_conv_b[layer], ffn_w_down[layer])
    return _rmsnorm(x, norm_final)


import jax as _jax
import jax.numpy as _jnp

TWIN_FORMAT = 'train_step'
FWD_PARAMS = ['x', 'norm_mix', 'norm_ffn', 'norm_final', 'ab_w_in', 'gdn_conv_w', 'gdn_a_log', 'gdn_dt_bias', 'gdn_norm', 'hgrn_lower_bounds', 'hgrn_norm', 'ab_w_out', 'c_w_in', 'c_conv_w', 'c_conv_b', 'c_gate_a_w', 'c_gate_a_b', 'c_gate_x_w', 'c_gate_x_b', 'c_lambda', 'c_w_out', 'ffn_w_up', 'ffn_conv_w', 'ffn_conv_b', 'ffn_w_down']
TWIN_WEIGHTS = ['norm_mix', 'norm_ffn', 'norm_final', 'ab_w_in', 'gdn_conv_w', 'gdn_a_log', 'gdn_dt_bias', 'gdn_norm', 'hgrn_lower_bounds', 'hgrn_norm', 'ab_w_out', 'c_w_in', 'c_conv_w', 'c_conv_b', 'c_gate_a_w', 'c_gate_a_b', 'c_gate_x_w', 'c_gate_x_b', 'c_lambda', 'c_w_out', 'ffn_w_up', 'ffn_conv_w', 'ffn_conv_b', 'ffn_w_down']
TWIN_DIFF_INPUT = 'x'
TWIN_INPUTS = ['x', 'norm_mix', 'norm_ffn', 'norm_final', 'ab_w_in', 'gdn_conv_w', 'gdn_a_log', 'gdn_dt_bias', 'gdn_norm', 'hgrn_lower_bounds', 'hgrn_norm', 'ab_w_out', 'c_w_in', 'c_conv_w', 'c_conv_b', 'c_gate_a_w', 'c_gate_a_b', 'c_gate_x_w', 'c_gate_x_b', 'c_lambda', 'c_w_out', 'ffn_w_up', 'ffn_conv_w', 'ffn_conv_b', 'ffn_w_down', 'loss_target', 'm_norm_mix', 'm_norm_ffn', 'm_norm_final', 'm_ab_w_in', 'm_gdn_conv_w', 'm_gdn_a_log', 'm_gdn_dt_bias', 'm_gdn_norm', 'm_hgrn_lower_bounds', 'm_hgrn_norm', 'm_ab_w_out', 'm_c_w_in', 'm_c_conv_w', 'm_c_conv_b', 'm_c_gate_a_w', 'm_c_gate_a_b', 'm_c_gate_x_w', 'm_c_gate_x_b', 'm_c_lambda', 'm_c_w_out', 'm_ffn_w_up', 'm_ffn_conv_w', 'm_ffn_conv_b', 'm_ffn_w_down', 'v_norm_mix', 'v_norm_ffn', 'v_norm_final', 'v_ab_w_in', 'v_gdn_conv_w', 'v_gdn_a_log', 'v_gdn_dt_bias', 'v_gdn_norm', 'v_hgrn_lower_bounds', 'v_hgrn_norm', 'v_ab_w_out', 'v_c_w_in', 'v_c_conv_w', 'v_c_conv_b', 'v_c_gate_a_w', 'v_c_gate_a_b', 'v_c_gate_x_w', 'v_c_gate_x_b', 'v_c_lambda', 'v_c_w_out', 'v_ffn_w_up', 'v_ffn_conv_w', 'v_ffn_conv_b', 'v_ffn_w_down']
TWIN_OUTPUTS = ['loss', 'grad_x', 'grad_norm_mix', 'grad_norm_ffn', 'grad_norm_final', 'grad_ab_w_in', 'grad_gdn_conv_w', 'grad_gdn_a_log', 'grad_gdn_dt_bias', 'grad_gdn_norm', 'grad_hgrn_lower_bounds', 'grad_hgrn_norm', 'grad_ab_w_out', 'grad_c_w_in', 'grad_c_conv_w', 'grad_c_conv_b', 'grad_c_gate_a_w', 'grad_c_gate_a_b', 'grad_c_gate_x_w', 'grad_c_gate_x_b', 'grad_c_lambda', 'grad_c_w_out', 'grad_ffn_w_up', 'grad_ffn_conv_w', 'grad_ffn_conv_b', 'grad_ffn_w_down', 'delta_norm_mix', 'delta_norm_ffn', 'delta_norm_final', 'delta_ab_w_in', 'delta_gdn_conv_w', 'delta_gdn_a_log', 'delta_gdn_dt_bias', 'delta_gdn_norm', 'delta_hgrn_lower_bounds', 'delta_hgrn_norm', 'delta_ab_w_out', 'delta_c_w_in', 'delta_c_conv_w', 'delta_c_conv_b', 'delta_c_gate_a_w', 'delta_c_gate_a_b', 'delta_c_gate_x_w', 'delta_c_gate_x_b', 'delta_c_lambda', 'delta_c_w_out', 'delta_ffn_w_up', 'delta_ffn_conv_w', 'delta_ffn_conv_b', 'delta_ffn_w_down', 'new_m_norm_mix', 'new_m_norm_ffn', 'new_m_norm_final', 'new_m_ab_w_in', 'new_m_gdn_conv_w', 'new_m_gdn_a_log', 'new_m_gdn_dt_bias', 'new_m_gdn_norm', 'new_m_hgrn_lower_bounds', 'new_m_hgrn_norm', 'new_m_ab_w_out', 'new_m_c_w_in', 'new_m_c_conv_w', 'new_m_c_conv_b', 'new_m_c_gate_a_w', 'new_m_c_gate_a_b', 'new_m_c_gate_x_w', 'new_m_c_gate_x_b', 'new_m_c_lambda', 'new_m_c_w_out', 'new_m_ffn_w_up', 'new_m_ffn_conv_w', 'new_m_ffn_conv_b', 'new_m_ffn_w_down', 'new_v_norm_mix', 'new_v_norm_ffn', 'new_v_norm_final', 'new_v_ab_w_in', 'new_v_gdn_conv_w', 'new_v_gdn_a_log', 'new_v_gdn_dt_bias', 'new_v_gdn_norm', 'new_v_hgrn_lower_bounds', 'new_v_hgrn_norm', 'new_v_ab_w_out', 'new_v_c_w_in', 'new_v_c_conv_w', 'new_v_c_conv_b', 'new_v_c_gate_a_w', 'new_v_c_gate_a_b', 'new_v_c_gate_x_w', 'new_v_c_gate_x_b', 'new_v_c_lambda', 'new_v_c_w_out', 'new_v_ffn_w_up', 'new_v_ffn_conv_w', 'new_v_ffn_conv_b', 'new_v_ffn_w_down']
TWIN_LEAF_KINDS = {'loss': 'loss', 'grad_x': 'grad_x', 'grad_norm_mix': 'grad_w', 'grad_norm_ffn': 'grad_w', 'grad_norm_final': 'grad_w', 'grad_ab_w_in': 'grad_w', 'grad_gdn_conv_w': 'grad_w', 'grad_gdn_a_log': 'grad_w', 'grad_gdn_dt_bias': 'grad_w', 'grad_gdn_norm': 'grad_w', 'grad_hgrn_lower_bounds': 'grad_w', 'grad_hgrn_norm': 'grad_w', 'grad_ab_w_out': 'grad_w', 'grad_c_w_in': 'grad_w', 'grad_c_conv_w': 'grad_w', 'grad_c_conv_b': 'grad_w', 'grad_c_gate_a_w': 'grad_w', 'grad_c_gate_a_b': 'grad_w', 'grad_c_gate_x_w': 'grad_w', 'grad_c_gate_x_b': 'grad_w', 'grad_c_lambda': 'grad_w', 'grad_c_w_out': 'grad_w', 'grad_ffn_w_up': 'grad_w', 'grad_ffn_conv_w': 'grad_w', 'grad_ffn_conv_b': 'grad_w', 'grad_ffn_w_down': 'grad_w', 'delta_norm_mix': 'delta_w', 'delta_norm_ffn': 'delta_w', 'delta_norm_final': 'delta_w', 'delta_ab_w_in': 'delta_w', 'delta_gdn_conv_w': 'delta_w', 'delta_gdn_a_log': 'delta_w', 'delta_gdn_dt_bias': 'delta_w', 'delta_gdn_norm': 'delta_w', 'delta_hgrn_lower_bounds': 'delta_w', 'delta_hgrn_norm': 'delta_w', 'delta_ab_w_out': 'delta_w', 'delta_c_w_in': 'delta_w', 'delta_c_conv_w': 'delta_w', 'delta_c_conv_b': 'delta_w', 'delta_c_gate_a_w': 'delta_w', 'delta_c_gate_a_b': 'delta_w', 'delta_c_gate_x_w': 'delta_w', 'delta_c_gate_x_b': 'delta_w', 'delta_c_lambda': 'delta_w', 'delta_c_w_out': 'delta_w', 'delta_ffn_w_up': 'delta_w', 'delta_ffn_conv_w': 'delta_w', 'delta_ffn_conv_b': 'delta_w', 'delta_ffn_w_down': 'delta_w', 'new_m_norm_mix': 'new_m', 'new_m_norm_ffn': 'new_m', 'new_m_norm_final': 'new_m', 'new_m_ab_w_in': 'new_m', 'new_m_gdn_conv_w': 'new_m', 'new_m_gdn_a_log': 'new_m', 'new_m_gdn_dt_bias': 'new_m', 'new_m_gdn_norm': 'new_m', 'new_m_hgrn_lower_bounds': 'new_m', 'new_m_hgrn_norm': 'new_m', 'new_m_ab_w_out': 'new_m', 'new_m_c_w_in': 'new_m', 'new_m_c_conv_w': 'new_m', 'new_m_c_conv_b': 'new_m', 'new_m_c_gate_a_w': 'new_m', 'new_m_c_gate_a_b': 'new_m', 'new_m_c_gate_x_w': 'new_m', 'new_m_c_gate_x_b': 'new_m', 'new_m_c_lambda': 'new_m', 'new_m_c_w_out': 'new_m', 'new_m_ffn_w_up': 'new_m', 'new_m_ffn_conv_w': 'new_m', 'new_m_ffn_conv_b': 'new_m', 'new_m_ffn_w_down': 'new_m', 'new_v_norm_mix': 'new_v', 'new_v_norm_ffn': 'new_v', 'new_v_norm_final': 'new_v', 'new_v_ab_w_in': 'new_v', 'new_v_gdn_conv_w': 'new_v', 'new_v_gdn_a_log': 'new_v', 'new_v_gdn_dt_bias': 'new_v', 'new_v_gdn_norm': 'new_v', 'new_v_hgrn_lower_bounds': 'new_v', 'new_v_hgrn_norm': 'new_v', 'new_v_ab_w_out': 'new_v', 'new_v_c_w_in': 'new_v', 'new_v_c_conv_w': 'new_v', 'new_v_c_conv_b': 'new_v', 'new_v_c_gate_a_w': 'new_v', 'new_v_c_gate_a_b': 'new_v', 'new_v_c_gate_x_w': 'new_v', 'new_v_c_gate_x_b': 'new_v', 'new_v_c_lambda': 'new_v', 'new_v_c_w_out': 'new_v', 'new_v_ffn_w_up': 'new_v', 'new_v_ffn_conv_w': 'new_v', 'new_v_ffn_conv_b': 'new_v', 'new_v_ffn_w_down': 'new_v'}


def _forward(args):
    return _fwd_reference(*[args[k] for k in FWD_PARAMS])


def _output_shape():
    out = _jax.eval_shape(lambda: _forward(_fwd_setup_inputs(0)))
    return out.shape, out.dtype

N_MICROBATCH = 1
ADAM_LR = 0.001
ADAM_B1 = 0.9
ADAM_B2 = 0.999
ADAM_EPS = 1e-08
ADAM_WD = 0.01
ADAM_STEP = 10
PER_EXAMPLE_BATCH_AXIS = {'x': 0, 'loss_target': 0}
SHARED_INPUTS = []
_WEIGHT_DTYPES = {'norm_mix': _jnp.float32, 'norm_ffn': _jnp.float32, 'norm_final': _jnp.float32, 'ab_w_in': _jnp.float32, 'gdn_conv_w': _jnp.float32, 'gdn_a_log': _jnp.float32, 'gdn_dt_bias': _jnp.float32, 'gdn_norm': _jnp.float32, 'hgrn_lower_bounds': _jnp.float32, 'hgrn_norm': _jnp.float32, 'ab_w_out': _jnp.float32, 'c_w_in': _jnp.float32, 'c_conv_w': _jnp.float32, 'c_conv_b': _jnp.float32, 'c_gate_a_w': _jnp.float32, 'c_gate_a_b': _jnp.float32, 'c_gate_x_w': _jnp.float32, 'c_gate_x_b': _jnp.float32, 'c_lambda': _jnp.float32, 'c_w_out': _jnp.float32, 'ffn_w_up': _jnp.float32, 'ffn_conv_w': _jnp.float32, 'ffn_conv_b': _jnp.float32, 'ffn_w_down': _jnp.float32}
MOMENT_SCALE = {'norm_mix': 1.072662e-01, 'norm_ffn': 8.782801e-02, 'norm_final': 1.599136e+01, 'ab_w_in': 6.312873e-02, 'gdn_conv_w': 6.340060e-02, 'gdn_a_log': 5.029636e-01, 'gdn_dt_bias': 4.945357e-01, 'gdn_norm': 1.771318e-01, 'hgrn_lower_bounds': 5.504404e-03, 'hgrn_norm': 1.738203e-01, 'ab_w_out': 8.619006e-02, 'c_w_in': 5.872390e-02, 'c_conv_w': 6.120109e-02, 'c_conv_b': 5.121939e-01, 'c_gate_a_w': 1.188048e-02, 'c_gate_a_b': 1.322122e-02, 'c_gate_x_w': 2.117475e-02, 'c_gate_x_b': 2.471942e-02, 'c_lambda': 3.029860e-02, 'c_w_out': 6.075590e-02, 'ffn_w_up': 3.747011e-02, 'ffn_conv_w': 3.822100e-02, 'ffn_conv_b': 3.683148e-02, 'ffn_w_down': 6.117079e-02}


def _to_microbatches(a, axis):
    t = _jnp.moveaxis(a, axis, 0)
    t = t.reshape((N_MICROBATCH, t.shape[0] // N_MICROBATCH) + t.shape[1:])
    return _jnp.moveaxis(t, 1, axis + 1)


def setup_inputs(seed: int = 0) -> dict:
    inp = _fwd_setup_inputs(seed)
    key = _jax.random.fold_in(_jax.random.key(seed), 7919)
    shape, _ = _output_shape()
    out = dict(inp)
    out["loss_target"] = _jax.random.normal(_jax.random.fold_in(key, 0), shape, _jnp.float32)
    for i, name in enumerate(TWIN_WEIGHTS):
        w = inp[name].astype(_jnp.float32)
        if MOMENT_SCALE is None:
            s = _jnp.sqrt(_jnp.mean(_jnp.square(w)) + 1e-30)
        else:
            s = MOMENT_SCALE[name]
        km, kv = _jax.random.split(_jax.random.fold_in(key, i + 1))
        out[name] = w
        out["m_" + name] = s * _jax.random.normal(km, w.shape, _jnp.float32)
        out["v_" + name] = (s * s) * _jax.random.uniform(kv, w.shape, _jnp.float32, 0.5, 1.5)
    if N_MICROBATCH > 1:
        for name, axis in PER_EXAMPLE_BATCH_AXIS.items():
            out[name] = _to_microbatches(out[name], axis)
    return {'x': out['x'], 'norm_mix': out['norm_mix'], 'norm_ffn': out['norm_ffn'], 'norm_final': out['norm_final'], 'ab_w_in': out['ab_w_in'], 'gdn_conv_w': out['gdn_conv_w'], 'gdn_a_log': out['gdn_a_log'], 'gdn_dt_bias': out['gdn_dt_bias'], 'gdn_norm': out['gdn_norm'], 'hgrn_lower_bounds': out['hgrn_lower_bounds'], 'hgrn_norm': out['hgrn_norm'], 'ab_w_out': out['ab_w_out'], 'c_w_in': out['c_w_in'], 'c_conv_w': out['c_conv_w'], 'c_conv_b': out['c_conv_b'], 'c_gate_a_w': out['c_gate_a_w'], 'c_gate_a_b': out['c_gate_a_b'], 'c_gate_x_w': out['c_gate_x_w'], 'c_gate_x_b': out['c_gate_x_b'], 'c_lambda': out['c_lambda'], 'c_w_out': out['c_w_out'], 'ffn_w_up': out['ffn_w_up'], 'ffn_conv_w': out['ffn_conv_w'], 'ffn_conv_b': out['ffn_conv_b'], 'ffn_w_down': out['ffn_w_down'], 'loss_target': out['loss_target'], 'm_norm_mix': out['m_norm_mix'], 'm_norm_ffn': out['m_norm_ffn'], 'm_norm_final': out['m_norm_final'], 'm_ab_w_in': out['m_ab_w_in'], 'm_gdn_conv_w': out['m_gdn_conv_w'], 'm_gdn_a_log': out['m_gdn_a_log'], 'm_gdn_dt_bias': out['m_gdn_dt_bias'], 'm_gdn_norm': out['m_gdn_norm'], 'm_hgrn_lower_bounds': out['m_hgrn_lower_bounds'], 'm_hgrn_norm': out['m_hgrn_norm'], 'm_ab_w_out': out['m_ab_w_out'], 'm_c_w_in': out['m_c_w_in'], 'm_c_conv_w': out['m_c_conv_w'], 'm_c_conv_b': out['m_c_conv_b'], 'm_c_gate_a_w': out['m_c_gate_a_w'], 'm_c_gate_a_b': out['m_c_gate_a_b'], 'm_c_gate_x_w': out['m_c_gate_x_w'], 'm_c_gate_x_b': out['m_c_gate_x_b'], 'm_c_lambda': out['m_c_lambda'], 'm_c_w_out': out['m_c_w_out'], 'm_ffn_w_up': out['m_ffn_w_up'], 'm_ffn_conv_w': out['m_ffn_conv_w'], 'm_ffn_conv_b': out['m_ffn_conv_b'], 'm_ffn_w_down': out['m_ffn_w_down'], 'v_norm_mix': out['v_norm_mix'], 'v_norm_ffn': out['v_norm_ffn'], 'v_norm_final': out['v_norm_final'], 'v_ab_w_in': out['v_ab_w_in'], 'v_gdn_conv_w': out['v_gdn_conv_w'], 'v_gdn_a_log': out['v_gdn_a_log'], 'v_gdn_dt_bias': out['v_gdn_dt_bias'], 'v_gdn_norm': out['v_gdn_norm'], 'v_hgrn_lower_bounds': out['v_hgrn_lower_bounds'], 'v_hgrn_norm': out['v_hgrn_norm'], 'v_ab_w_out': out['v_ab_w_out'], 'v_c_w_in': out['v_c_w_in'], 'v_c_conv_w': out['v_c_conv_w'], 'v_c_conv_b': out['v_c_conv_b'], 'v_c_gate_a_w': out['v_c_gate_a_w'], 'v_c_gate_a_b': out['v_c_gate_a_b'], 'v_c_gate_x_w': out['v_c_gate_x_w'], 'v_c_gate_x_b': out['v_c_gate_x_b'], 'v_c_lambda': out['v_c_lambda'], 'v_c_w_out': out['v_c_w_out'], 'v_ffn_w_up': out['v_ffn_w_up'], 'v_ffn_conv_w': out['v_ffn_conv_w'], 'v_ffn_conv_b': out['v_ffn_conv_b'], 'v_ffn_w_down': out['v_ffn_w_down']}


def _loss(weights, diff, rest, loss_target):
    with _jax.named_scope("forward"):
        args = {**rest, TWIN_DIFF_INPUT: diff, **{k: w.astype(_WEIGHT_DTYPES[k]) for k, w in weights.items()}}
        y = _forward(args)
    with _jax.named_scope("loss_head"):
        err = _jnp.square(y.astype(_jnp.float32) - loss_target)
        return 0.5 * _jnp.sum(_jnp.mean(err, axis=-1)) if err.ndim else 0.5 * err


def _adamw(w, g, m, v):
    m = ADAM_B1 * m + (1.0 - ADAM_B1) * g
    v = ADAM_B2 * v + (1.0 - ADAM_B2) * _jnp.square(g)
    m_hat = m / (1.0 - ADAM_B1 ** ADAM_STEP)
    v_hat = v / (1.0 - ADAM_B2 ** ADAM_STEP)
    delta = -ADAM_LR * (m_hat / (_jnp.sqrt(v_hat) + ADAM_EPS) + ADAM_WD * w)
    return delta, m, v


def reference(x, norm_mix, norm_ffn, norm_final, ab_w_in, gdn_conv_w, gdn_a_log, gdn_dt_bias, gdn_norm, hgrn_lower_bounds, hgrn_norm, ab_w_out, c_w_in, c_conv_w, c_conv_b, c_gate_a_w, c_gate_a_b, c_gate_x_w, c_gate_x_b, c_lambda, c_w_out, ffn_w_up, ffn_conv_w, ffn_conv_b, ffn_w_down, loss_target, m_norm_mix, m_norm_ffn, m_norm_final, m_ab_w_in, m_gdn_conv_w, m_gdn_a_log, m_gdn_dt_bias, m_gdn_norm, m_hgrn_lower_bounds, m_hgrn_norm, m_ab_w_out, m_c_w_in, m_c_conv_w, m_c_conv_b, m_c_gate_a_w, m_c_gate_a_b, m_c_gate_x_w, m_c_gate_x_b, m_c_lambda, m_c_w_out, m_ffn_w_up, m_ffn_conv_w, m_ffn_conv_b, m_ffn_w_down, v_norm_mix, v_norm_ffn, v_norm_final, v_ab_w_in, v_gdn_conv_w, v_gdn_a_log, v_gdn_dt_bias, v_gdn_norm, v_hgrn_lower_bounds, v_hgrn_norm, v_ab_w_out, v_c_w_in, v_c_conv_w, v_c_conv_b, v_c_gate_a_w, v_c_gate_a_b, v_c_gate_x_w, v_c_gate_x_b, v_c_lambda, v_c_w_out, v_ffn_w_up, v_ffn_conv_w, v_ffn_conv_b, v_ffn_w_down):
    given = dict(x=x, norm_mix=norm_mix, norm_ffn=norm_ffn, norm_final=norm_final, ab_w_in=ab_w_in, gdn_conv_w=gdn_conv_w, gdn_a_log=gdn_a_log, gdn_dt_bias=gdn_dt_bias, gdn_norm=gdn_norm, hgrn_lower_bounds=hgrn_lower_bounds, hgrn_norm=hgrn_norm, ab_w_out=ab_w_out, c_w_in=c_w_in, c_conv_w=c_conv_w, c_conv_b=c_conv_b, c_gate_a_w=c_gate_a_w, c_gate_a_b=c_gate_a_b, c_gate_x_w=c_gate_x_w, c_gate_x_b=c_gate_x_b, c_lambda=c_lambda, c_w_out=c_w_out, ffn_w_up=ffn_w_up, ffn_conv_w=ffn_conv_w, ffn_conv_b=ffn_conv_b, ffn_w_down=ffn_w_down, loss_target=loss_target, m_norm_mix=m_norm_mix, m_norm_ffn=m_norm_ffn, m_norm_final=m_norm_final, m_ab_w_in=m_ab_w_in, m_gdn_conv_w=m_gdn_conv_w, m_gdn_a_log=m_gdn_a_log, m_gdn_dt_bias=m_gdn_dt_bias, m_gdn_norm=m_gdn_norm, m_hgrn_lower_bounds=m_hgrn_lower_bounds, m_hgrn_norm=m_hgrn_norm, m_ab_w_out=m_ab_w_out, m_c_w_in=m_c_w_in, m_c_conv_w=m_c_conv_w, m_c_conv_b=m_c_conv_b, m_c_gate_a_w=m_c_gate_a_w, m_c_gate_a_b=m_c_gate_a_b, m_c_gate_x_w=m_c_gate_x_w, m_c_gate_x_b=m_c_gate_x_b, m_c_lambda=m_c_lambda, m_c_w_out=m_c_w_out, m_ffn_w_up=m_ffn_w_up, m_ffn_conv_w=m_ffn_conv_w, m_ffn_conv_b=m_ffn_conv_b, m_ffn_w_down=m_ffn_w_down, v_norm_mix=v_norm_mix, v_norm_ffn=v_norm_ffn, v_norm_final=v_norm_final, v_ab_w_in=v_ab_w_in, v_gdn_conv_w=v_gdn_conv_w, v_gdn_a_log=v_gdn_a_log, v_gdn_dt_bias=v_gdn_dt_bias, v_gdn_norm=v_gdn_norm, v_hgrn_lower_bounds=v_hgrn_lower_bounds, v_hgrn_norm=v_hgrn_norm, v_ab_w_out=v_ab_w_out, v_c_w_in=v_c_w_in, v_c_conv_w=v_c_conv_w, v_c_conv_b=v_c_conv_b, v_c_gate_a_w=v_c_gate_a_w, v_c_gate_a_b=v_c_gate_a_b, v_c_gate_x_w=v_c_gate_x_w, v_c_gate_x_b=v_c_gate_x_b, v_c_lambda=v_c_lambda, v_c_w_out=v_c_w_out, v_ffn_w_up=v_ffn_w_up, v_ffn_conv_w=v_ffn_conv_w, v_ffn_conv_b=v_ffn_conv_b, v_ffn_w_down=v_ffn_w_down)
    weights = {n: given[n] for n in TWIN_WEIGHTS}
    shared = {n: given[n] for n in SHARED_INPUTS}
    per_example = {n: given[n] for n in ['x']}
    grad_fn = _jax.value_and_grad(_loss, argnums=(0, 1))

    def one_microbatch(ex, loss_target):
        ex = dict(ex)
        diff = ex.pop(TWIN_DIFF_INPUT)
        return grad_fn(weights, diff, {**shared, **ex}, loss_target)

    if N_MICROBATCH == 1:
        loss, (grad_w, grad_x) = one_microbatch(per_example, given["loss_target"])
    else:
        def body(carry, xs):
            loss_sum, grad_sum = carry
            l_k, (gw_k, gx_k) = one_microbatch(xs[0], xs[1])
            with _jax.named_scope("update"):
                return (loss_sum + l_k, _jax.tree.map(_jnp.add, grad_sum, gw_k)), gx_k

        init = (_jnp.zeros((), _jnp.float32), _jax.tree.map(_jnp.zeros_like, weights))
        (loss, grad_w), grad_x = _jax.lax.scan(body, init, (per_example, given["loss_target"]))
    with _jax.named_scope("update"):
        delta_w, new_m, new_v = {}, {}, {}
        for n in TWIN_WEIGHTS:
            delta_w[n], new_m[n], new_v[n] = _adamw(weights[n], grad_w[n], given["m_" + n], given["v_" + n])
    return (loss, grad_x, *[grad_w[n] for n in TWIN_WEIGHTS], *[delta_w[n] for n in TWIN_WEIGHTS],
            *[new_m[n] for n in TWIN_WEIGHTS], *[new_v[n] for n in TWIN_WEIGHTS])
```

```python
import functools

import numpy as np
import jax
import jax.numpy as jnp
from jax import lax
from jax.experimental import pallas as pl
from jax.experimental.pallas import tpu as pltpu

f32 = jnp.float32
bf16 = jnp.bfloat16
HI = lax.Precision.HIGHEST
MESH = pl.DeviceIdType.MESH

N_DEV = 8
D_MODEL = 1024
DEPTH = 4
EPS = 1e-6
F_FLOOR = 1e-30
HEADS = 4
HEAD_DIM = 128
GDN_WIDTH = 512
GDN_CONV = 4
GDN_CHUNK = 64
HGRN_CHUNK = 16
HGRN_STEP = 128
MIX_WIDTH = 1024
AB_COLS = 4104
AB_PAD = 4224
LRU_WIDTH = 1024
LRU_BLOCK = 256
LRU_CONV = 4
RG_C = 8.0
D_FF = 2816
FF_SHARD = 704
FF_PAD = 768
D_FFP = 4 * FF_PAD
FF_ROWS = 352
AB_SHARD, AB_SHARD_PAD = 513, 640
FFN_CONV = 3
ADAM_LR, ADAM_B1, ADAM_B2, ADAM_EPS, ADAM_WD, ADAM_STEP = 0.001, 0.9, 0.999, 1e-08, 0.01, 10
VMEM_LIMIT = 56 * 1024 * 1024
PACK_LANES = 512
PACK_ROWS = 256

OFF_Q, OFF_K, OFF_V, OFF_Z, OFF_QB, OFF_FB, OFF_IB, OFF_GB, OFF_BA = 0, 512, 1024, 1536, 2048, 2560, 3072, 3584, 4096

WEIGHTS = ['norm_mix', 'norm_ffn', 'norm_final', 'ab_w_in', 'gdn_conv_w', 'gdn_a_log', 'gdn_dt_bias', 'gdn_norm',
           'hgrn_lower_bounds', 'hgrn_norm', 'ab_w_out', 'c_w_in', 'c_conv_w', 'c_conv_b', 'c_gate_a_w', 'c_gate_a_b',
           'c_gate_x_w', 'c_gate_x_b', 'c_lambda', 'c_w_out', 'ffn_w_up', 'ffn_conv_w', 'ffn_conv_b', 'ffn_w_down']
SHARD_AXIS = {'norm_mix': None, 'norm_ffn': None, 'norm_final': None, 'ab_w_in': 2, 'gdn_conv_w': 2, 'gdn_a_log': None,
              'gdn_dt_bias': None, 'gdn_norm': None, 'hgrn_lower_bounds': None, 'hgrn_norm': None, 'ab_w_out': 1,
              'c_w_in': 2, 'c_conv_w': 2, 'c_conv_b': 1, 'c_gate_a_w': 2, 'c_gate_a_b': 1, 'c_gate_x_w': 2,
              'c_gate_x_b': 1, 'c_lambda': 1, 'c_w_out': 1, 'ffn_w_up': 2, 'ffn_conv_w': 2, 'ffn_conv_b': None,
              'ffn_w_down': 1}
MATMUL_WEIGHTS = ('ab_w_in', 'ab_w_out', 'c_w_in', 'c_gate_a_w', 'c_gate_x_w', 'c_w_out', 'ffn_w_up', 'ffn_w_down')
SHARDED = [n for n in WEIGHTS if SHARD_AXIS[n] is not None]
REPLICATED = [n for n in WEIGHTS if SHARD_AXIS[n] is None]


def _tile(n, prefs=(512, 384, 256, 128)):
    for p in prefs:
        if n % p == 0:
            return p
    return n


def _cparams(sem=None):
    kw = dict(vmem_limit_bytes=VMEM_LIMIT)
    if sem is not None:
        kw['dimension_semantics'] = sem
    return pltpu.CompilerParams(**kw)


def _sds(shape, dtype):
    return jax.ShapeDtypeStruct(tuple(shape), dtype)


def _sigmoid(x):
    return 1.0 / (1.0 + jnp.exp(-x))


def _silu(x):
    return x * _sigmoid(x)


def _log1p(x):
    u = 1.0 + x
    return jnp.where(u == 1.0, x, jnp.log(u) * (x / jnp.where(u == 1.0, 1.0, u - 1.0)))


def _softplus(x):
    return jnp.maximum(x, 0.0) + _log1p(jnp.exp(-jnp.abs(x)))


def _expm1(x):
    small = jnp.abs(x) < 0.05
    xs = jnp.where(small, x, 0.0)
    series = xs * (1.0 + xs * (0.5 + xs * (1.0 / 6.0 + xs * (1.0 / 24.0 + xs * (1.0 / 120.0)))))
    return jnp.where(small, series, jnp.exp(x) - 1.0)


def _gelu(x):
    return 0.5 * x * (1.0 + jnp.tanh(0.7978845608028654 * (x + 0.044715 * x * x * x)))


def _rms(x, gain):
    return x * lax.rsqrt(jnp.mean(x * x, axis=-1, keepdims=True) + EPS) * gain


def _dot(a, b, dims=((1,), (0,)), precision=None):
    return lax.dot_general(a, b, (dims, ((), ())), precision=precision, preferred_element_type=f32)


def _bdot(a, b, dims=((1,), (0,))):
    return _dot(a.astype(bf16), b.astype(bf16), dims)


NT = ((1,), (1,))
TN = ((0,), (0,))


def _shift_down(x, k):
    if k == 0:
        return x
    row = lax.broadcasted_iota(jnp.int32, x.shape, 0)
    return jnp.where(row >= k, pltpu.roll(x, k, 0), 0.0)


def _shift_up(x, k, fill=0.0):
    if k == 0:
        return x
    n = x.shape[0]
    row = lax.broadcasted_iota(jnp.int32, x.shape, 0)
    return jnp.where(row < n - k, pltpu.roll(x, n - k, 0), fill)


def _conv_fwd(x, w_ref, width):
    acc = w_ref[width - 1:width, :] * x
    for k in range(width - 1):
        acc = acc + w_ref[k:k + 1, :] * _shift_down(x, width - 1 - k)
    return acc


def _conv_bwd(x, dout, w_ref, dw_ref, width):
    dx = w_ref[width - 1:width, :] * dout
    dw_ref[width - 1:width, :] = jnp.sum(dout * x, axis=0, keepdims=True)
    for k in range(width - 1):
        s = width - 1 - k
        dx = dx + w_ref[k:k + 1, :] * _shift_up(dout, s)
        dw_ref[k:k + 1, :] = jnp.sum(dout * _shift_down(x, s), axis=0, keepdims=True)
    return dx


def _mm(a, b, *, ta=False, tb=False, add=None, out_dtype=f32, name):
    m, k = (a.shape[1], a.shape[0]) if ta else a.shape
    n = b.shape[0] if tb else b.shape[1]
    tm, tn = _tile(m), _tile(n)
    dims = ((0 if ta else 1,), (1 if tb else 0,))

    def body(*refs):
        a_ref, b_ref = refs[0], refs[1]
        o_ref = refs[-1]
        r = _dot(a_ref[...], b_ref[...], dims)
        if add is not None:
            r = r + refs[2][...]
        o_ref[...] = r.astype(out_dtype)

    a_spec = pl.BlockSpec((k, tm), lambda j, i: (0, i)) if ta else pl.BlockSpec((tm, k), lambda j, i: (i, 0))
    b_spec = pl.BlockSpec((tn, k), lambda j, i: (j, 0)) if tb else pl.BlockSpec((k, tn), lambda j, i: (0, j))
    o_spec = pl.BlockSpec((tm, tn), lambda j, i: (i, j))
    ins, specs = [a, b], [a_spec, b_spec]
    if add is not None:
        ins.append(add)
        specs.append(o_spec)
    return pl.pallas_call(body, name=name, grid=(n // tn, m // tm), in_specs=specs, out_specs=o_spec,
                          out_shape=_sds((m, n), out_dtype), compiler_params=_cparams(("parallel", "parallel")))(*ins)


def _rms_fwd(x, gain, name):
    t, d = x.shape
    tr = _tile(t, (256, 128))

    def body(x_ref, g_ref, h_ref):
        h_ref[...] = _rms(x_ref[...], g_ref[...]).astype(bf16)

    return pl.pallas_call(body, name=name, grid=(t // tr,),
                          in_specs=[pl.BlockSpec((tr, d), lambda i: (i, 0)), pl.BlockSpec((1, d), lambda i: (0, 0))],
                          out_specs=pl.BlockSpec((tr, d), lambda i: (i, 0)), out_shape=_sds((t, d), bf16),
                          compiler_params=_cparams(("parallel",)))(x, gain)


def _rms_bwd(x, gain, dh, dres, name):
    t, d = x.shape
    tr = _tile(t, (256, 128))

    def body(x_ref, g_ref, dh_ref, dres_ref, dx_ref, dxb_ref, dg_ref):
        _, vjp = jax.vjp(_rms, x_ref[...], g_ref[...])
        dx, dg = vjp(dh_ref[...])
        dx = dx + dres_ref[...]
        dx_ref[...] = dx
        dxb_ref[...] = dx.astype(bf16)

        @pl.when(pl.program_id(0) == 0)
        def _():
            dg_ref[...] = jnp.zeros_like(dg_ref)

        dg_ref[...] += dg

    row = pl.BlockSpec((tr, d), lambda i: (i, 0))
    vec = pl.BlockSpec((1, d), lambda i: (0, 0))
    return pl.pallas_call(body, name=name, grid=(t // tr,), in_specs=[row, vec, row, row], out_specs=[row, row, vec],
                          out_shape=[_sds((t, d), f32), _sds((t, d), bf16), _sds((1, d), f32)],
                          compiler_params=_cparams(("arbitrary",)))(x, gain, dh, dres)


def _loss_head(x, gain, target):
    t, d = x.shape
    tr = _tile(t, (256, 128))

    def f(xv, g, tgt):
        err = _rms(xv, g) - tgt
        return 0.5 * jnp.sum(jnp.mean(err * err, axis=-1, keepdims=True), axis=0, keepdims=True)

    def body(x_ref, g_ref, t_ref, loss_ref, dx_ref, dxb_ref, dg_ref):
        loss, vjp = jax.vjp(lambda xv, g: f(xv, g, t_ref[...]), x_ref[...], g_ref[...])
        dx, dg = vjp(jnp.ones((1, 1), f32))
        dx_ref[...] = dx
        dxb_ref[...] = dx.astype(bf16)

        @pl.when(pl.program_id(0) == 0)
        def _():
            dg_ref[...] = jnp.zeros_like(dg_ref)
            loss_ref[...] = jnp.zeros_like(loss_ref)

        dg_ref[...] += dg
        loss_ref[...] += jnp.broadcast_to(loss, loss_ref.shape)

    row = pl.BlockSpec((tr, d), lambda i: (i, 0))
    vec = pl.BlockSpec((1, d), lambda i: (0, 0))
    one = pl.BlockSpec((8, 128), lambda i: (0, 0))
    return pl.pallas_call(body, name="loss_head", grid=(t // tr,), in_specs=[row, vec, row],
                          out_specs=[one, row, row, vec],
                          out_shape=[_sds((8, 128), f32), _sds((t, d), f32), _sds((t, d), bf16), _sds((1, d), f32)],
                          compiler_params=_cparams(("arbitrary",)))(x, gain, target)


def _ffn_act_fwd(u, conv_w, conv_b):
    t = u.shape[0]
    tc = FF_PAD // 2
    nb = D_FFP // tc

    def body(g_ref, v_ref, w_ref, b_ref, a_ref):
        gc = _conv_fwd(g_ref[...], w_ref, FFN_CONV) + b_ref[...]
        a_ref[...] = (_silu(gc) * v_ref[...]).astype(bf16)

    return pl.pallas_call(
        body, name="ffn_act_fwd", grid=(nb,),
        in_specs=[pl.BlockSpec((t, tc), lambda j: (0, j)), pl.BlockSpec((t, tc), lambda j: (0, j + nb)),
                  pl.BlockSpec((FFN_CONV, tc), lambda j: (0, j)), pl.BlockSpec((1, tc), lambda j: (0, j))],
        out_specs=pl.BlockSpec((t, tc), lambda j: (0, j)), out_shape=_sds((t, D_FFP), bf16),
        compiler_params=_cparams(("parallel",)))(u, u, conv_w, conv_b)


def _ffn_act_bwd(u, conv_w, conv_b, da):
    t = u.shape[0]
    tc = FF_PAD // 2
    nb = D_FFP // tc

    def act(gc, val):
        return _silu(gc) * val

    def body(g_ref, v_ref, w_ref, b_ref, da_ref, a_ref, dg_ref, dv_ref, dw_ref, db_ref):
        gp = g_ref[...]
        gc = _conv_fwd(gp, w_ref, FFN_CONV) + b_ref[...]
        a, vjp = jax.vjp(act, gc, v_ref[...])
        dgc, dval = vjp(da_ref[...])
        a_ref[...] = a.astype(bf16)
        dv_ref[...] = dval.astype(bf16)
        db_ref[...] = jnp.sum(dgc, axis=0, keepdims=True)
        dg_ref[...] = _conv_bwd(gp, dgc, w_ref, dw_ref, FFN_CONV).astype(bf16)

    col = pl.BlockSpec((t, tc), lambda j: (0, j))
    return pl.pallas_call(
        body, name="ffn_act_bwd", grid=(nb,),
        in_specs=[col, pl.BlockSpec((t, tc), lambda j: (0, j + nb)), pl.BlockSpec((FFN_CONV, tc), lambda j: (0, j)),
                  pl.BlockSpec((1, tc), lambda j: (0, j)), col],
        out_specs=[col, col, col, pl.BlockSpec((FFN_CONV, tc), lambda j: (0, j)), pl.BlockSpec((1, tc), lambda j: (0, j))],
        out_shape=[_sds((t, D_FFP), bf16), _sds((t, D_FFP), bf16), _sds((t, D_FFP), bf16), _sds((FFN_CONV, D_FFP), f32),
                   _sds((1, D_FFP), f32)],
        compiler_params=_cparams(("parallel",)))(u, u, conv_w, conv_b, da)


def _lru_gates(xc, ra, ia, lam):
    r = _sigmoid(ra)
    i = _sigmoid(ia)
    log_a = -RG_C * r * _softplus(-lam)
    a = jnp.exp(log_a)
    u = jnp.sqrt(jnp.maximum(-_expm1(2.0 * log_a), 0.0)) * (i * xc)
    return a, u


def _lin_scan(a, u):
    n = a.shape[0]
    row = lax.broadcasted_iota(jnp.int32, a.shape, 0)
    s = 1
    while s < n:
        keep = row >= s
        u = a * jnp.where(keep, pltpu.roll(u, s, 0), 0.0) + u
        a = a * jnp.where(keep, pltpu.roll(a, s, 0), 1.0)
        s *= 2
    return u


def _rev_scan(a_next, d):
    n = d.shape[0]
    row = lax.broadcasted_iota(jnp.int32, d.shape, 0)
    a = a_next
    s = 1
    while s < n:
        keep = row < n - s
        d = a * jnp.where(keep, pltpu.roll(d, n - s, 0), 0.0) + d
        a = a * jnp.where(keep, pltpu.roll(a, n - s, 0), 1.0)
        s *= 2
    return d


def _col_conv_fwd(p, col_off, conv_w, conv_b, width, tc, name):
    t = p.shape[0]
    c = conv_w.shape[1]
    ob = col_off // tc

    def body(x_ref, w_ref, b_ref, o_ref):
        o_ref[...] = _conv_fwd(x_ref[...], w_ref, width) + b_ref[...]

    return pl.pallas_call(
        body, name=name, grid=(c // tc,),
        in_specs=[pl.BlockSpec((t, tc), lambda j: (0, j + ob)), pl.BlockSpec((width, tc), lambda j: (0, j)),
                  pl.BlockSpec((1, tc), lambda j: (0, j))],
        out_specs=pl.BlockSpec((t, tc), lambda j: (0, j)), out_shape=_sds((t, c), f32),
        compiler_params=_cparams(("parallel",)))(p, conv_w, conv_b)


def _col_conv_bwd(p, col_off, conv_w, dxc, width, tc, name):
    t = p.shape[0]
    c = conv_w.shape[1]
    ob = col_off // tc

    def body(x_ref, w_ref, d_ref, dx_ref, dw_ref, db_ref):
        d = d_ref[...]
        db_ref[...] = jnp.sum(d, axis=0, keepdims=True)
        dx_ref[...] = _conv_bwd(x_ref[...], d, w_ref, dw_ref, width).astype(bf16)

    col = pl.BlockSpec((t, tc), lambda j: (0, j))
    return pl.pallas_call(
        body, name=name, grid=(c // tc,),
        in_specs=[pl.BlockSpec((t, tc), lambda j: (0, j + ob)), pl.BlockSpec((width, tc), lambda j: (0, j)), col],
        out_specs=[col, pl.BlockSpec((width, tc), lambda j: (0, j)), pl.BlockSpec((1, tc), lambda j: (0, j))],
        out_shape=[_sds((t, c), bf16), _sds((width, c), f32), _sds((1, c), f32)],
        compiler_params=_cparams(("parallel",)))(p, conv_w, dxc)


def _lru_fwd(p, xc, wa, ba, wx, bx, lam):
    t = p.shape[0]
    bw = LRU_BLOCK

    def body(y_ref, xc_ref, wa_ref, ba_ref, wx_ref, bx_ref, lam_ref, out_ref, hs_ref, a_ref):
        xc_v = xc_ref[...]
        xb = xc_v.astype(bf16)
        ra = _dot(xb, wa_ref[0]) + ba_ref[...]
        ia = _dot(xb, wx_ref[0]) + bx_ref[...]
        a, u = _lru_gates(xc_v, ra, ia, lam_ref[...])
        a_ref[...] = a
        hs = _lin_scan(a, u)
        hs_ref[...] = hs
        out_ref[...] = (hs * _gelu(y_ref[...])).astype(bf16)

    col = pl.BlockSpec((t, bw), lambda h: (0, h))
    vec = pl.BlockSpec((1, bw), lambda h: (0, h))
    mat = pl.BlockSpec((1, bw, bw), lambda h: (h, 0, 0))
    return pl.pallas_call(
        body, name="lru_fwd", grid=(HEADS,), in_specs=[col, col, mat, vec, mat, vec, vec], out_specs=[col, col, col],
        out_shape=[_sds((t, LRU_WIDTH), bf16), _sds((t, LRU_WIDTH), f32), _sds((t, LRU_WIDTH), f32)],
        compiler_params=_cparams(("parallel",)))(p, xc, wa, ba, wx, bx, lam)


def _lru_bwd_scan(p, a, hs, dout):
    t = p.shape[0]
    bw = LRU_BLOCK

    def body(y_ref, a_ref, hs_ref, do_ref, dy_ref, da_ref, du_ref):
        hs_v = hs_ref[...]
        do = do_ref[...]
        gate, vjp = jax.vjp(_gelu, y_ref[...])
        dy_ref[...] = vjp(do * hs_v)[0].astype(bf16)
        g = _rev_scan(_shift_up(a_ref[...], 1), do * gate)
        du_ref[...] = g
        da_ref[...] = g * _shift_down(hs_v, 1)

    col = pl.BlockSpec((t, bw), lambda h: (0, h))
    return pl.pallas_call(
        body, name="lru_bwd_scan", grid=(HEADS,), in_specs=[col, col, col, col], out_specs=[col, col, col],
        out_shape=[_sds((t, LRU_WIDTH), bf16), _sds((t, LRU_WIDTH), f32), _sds((t, LRU_WIDTH), f32)],
        compiler_params=_cparams(("parallel",)))(p, a, hs, dout)


def _lru_bwd_gates(xc, da, du, wa, ba, wx, bx, lam):
    t = xc.shape[0]
    bw = LRU_BLOCK
    tr = _tile(t, (512, 256, 128))

    def body(xc_ref, da_ref, du_ref, wa_ref, ba_ref, wx_ref, bx_ref, lam_ref,
             dxc_ref, dwa_ref, dwx_ref, dba_ref, dbx_ref, dlam_ref):
        xc_v = xc_ref[...]
        xb = xc_v.astype(bf16)
        ra = _dot(xb, wa_ref[0]) + ba_ref[...]
        ia = _dot(xb, wx_ref[0]) + bx_ref[...]
        _, vjp = jax.vjp(_lru_gates, xc_v, ra, ia, lam_ref[...])
        dxc, dra, dia, dlam = vjp((da_ref[...], du_ref[...]))
        drb, dib = dra.astype(bf16), dia.astype(bf16)
        dxc_ref[...] = dxc + _dot(drb, wa_ref[0], NT) + _dot(dib, wx_ref[0], NT)

        @pl.when(pl.program_id(1) == 0)
        def _():
            dwa_ref[...] = jnp.zeros_like(dwa_ref)
            dwx_ref[...] = jnp.zeros_like(dwx_ref)
            dba_ref[...] = jnp.zeros_like(dba_ref)
            dbx_ref[...] = jnp.zeros_like(dbx_ref)
            dlam_ref[...] = jnp.zeros_like(dlam_ref)

        dwa_ref[0] += _dot(xb, drb, TN)
        dwx_ref[0] += _dot(xb, dib, TN)
        dba_ref[...] += jnp.sum(dra, axis=0, keepdims=True)
        dbx_ref[...] += jnp.sum(dia, axis=0, keepdims=True)
        dlam_ref[...] += dlam

    tile = pl.BlockSpec((tr, bw), lambda h, i: (i, h))
    vec = pl.BlockSpec((1, bw), lambda h, i: (0, h))
    mat = pl.BlockSpec((1, bw, bw), lambda h, i: (h, 0, 0))
    return pl.pallas_call(
        body, name="lru_bwd_gates", grid=(HEADS, t // tr), in_specs=[tile, tile, tile, mat, vec, mat, vec, vec],
        out_specs=[tile, mat, mat, vec, vec, vec],
        out_shape=[_sds((t, LRU_WIDTH), f32), _sds((HEADS, bw, bw), f32), _sds((HEADS, bw, bw), f32),
                   _sds((1, LRU_WIDTH), f32), _sds((1, LRU_WIDTH), f32), _sds((1, LRU_WIDTH), f32)],
        compiler_params=_cparams(("parallel", "arbitrary")))(xc, da, du, wa, ba, wx, bx, lam)


def _gdn_pre_fn(cq, ck, cv, ba, alog, dtb, h):
    q, k, v = _silu(cq), _silu(ck), _silu(cv)
    q = q * lax.rsqrt(jnp.sum(q * q, axis=-1, keepdims=True) + EPS) * (HEAD_DIM ** -0.5)
    k = k * lax.rsqrt(jnp.sum(k * k, axis=-1, keepdims=True) + EPS)
    lane = lax.broadcasted_iota(jnp.int32, (1, HEAD_DIM), 1)
    mb = (lane == h).astype(f32)
    ma = (lane == HEADS + h).astype(f32)
    beta_raw = jnp.sum(ba * mb, axis=-1, keepdims=True)
    alpha = jnp.sum(ba * ma, axis=-1, keepdims=True)
    al = jnp.sum(alog * mb, axis=-1, keepdims=True)
    db = jnp.sum(dtb * mb, axis=-1, keepdims=True)
    beta = _sigmoid(beta_raw)
    g = -jnp.exp(al) * _softplus(alpha + db)
    return q, k, v, jnp.broadcast_to(beta, q.shape), jnp.broadcast_to(g, q.shape)


def _gdn_pre_fwd(p, conv_w, alog, dtb):
    t = p.shape[0]
    hd = HEAD_DIM

    def body(pq_ref, pk_ref, pv_ref, ba_ref, wq_ref, wk_ref, wv_ref, al_ref, dt_ref, q_ref, k_ref, v_ref, b_ref, g_ref):
        h = pl.program_id(0)
        cq = _conv_fwd(pq_ref[...], wq_ref, GDN_CONV)
        ck = _conv_fwd(pk_ref[...], wk_ref, GDN_CONV)
        cv = _conv_fwd(pv_ref[...], wv_ref, GDN_CONV)
        q, k, v, be, ge = _gdn_pre_fn(cq, ck, cv, ba_ref[...], al_ref[...], dt_ref[...], h)
        q_ref[...], k_ref[...], v_ref[...], b_ref[...], g_ref[...] = q, k, v, be, ge

    def pcol(off):
        return pl.BlockSpec((t, hd), lambda h: (0, h + off // hd))

    def wcol(off):
        return pl.BlockSpec((GDN_CONV, hd), lambda h: (0, h + off // hd))

    vec = pl.BlockSpec((1, hd), lambda h: (0, 0))
    out = pl.BlockSpec((t, hd), lambda h: (0, h))
    return pl.pallas_call(
        body, name="gdn_pre_fwd", grid=(HEADS,),
        in_specs=[pcol(OFF_Q), pcol(OFF_K), pcol(OFF_V), pl.BlockSpec((t, hd), lambda h: (0, OFF_BA // hd)),
                  wcol(0), wcol(GDN_WIDTH), wcol(2 * GDN_WIDTH), vec, vec],
        out_specs=[out] * 5, out_shape=[_sds((t, GDN_WIDTH), f32)] * 5,
        compiler_params=_cparams(("parallel",)))(p, p, p, p, conv_w, conv_w, conv_w, alog, dtb)


def _gdn_pre_bwd(p, conv_w, alog, dtb, dq, dk, dv, dbe, dge):
    t = p.shape[0]
    hd = HEAD_DIM

    def body(pq_ref, pk_ref, pv_ref, ba_ref, wq_ref, wk_ref, wv_ref, al_ref, dt_ref,
             dq_ref, dk_ref, dv_ref, dbe_ref, dge_ref,
             opq_ref, opk_ref, opv_ref, dba_ref, dwq_ref, dwk_ref, dwv_ref, dal_ref, ddt_ref):
        h = pl.program_id(0)
        pq, pk, pv = pq_ref[...], pk_ref[...], pv_ref[...]
        cq = _conv_fwd(pq, wq_ref, GDN_CONV)
        ck = _conv_fwd(pk, wk_ref, GDN_CONV)
        cv = _conv_fwd(pv, wv_ref, GDN_CONV)
        _, vjp = jax.vjp(functools.partial(_gdn_pre_fn, h=h), cq, ck, cv, ba_ref[...], al_ref[...], dt_ref[...])
        dcq, dck, dcv, dba, dal, ddt = vjp((dq_ref[...], dk_ref[...], dv_ref[...], dbe_ref[...], dge_ref[...]))
        opq_ref[...] = _conv_bwd(pq, dcq, wq_ref, dwq_ref, GDN_CONV).astype(bf16)
        opk_ref[...] = _conv_bwd(pk, dck, wk_ref, dwk_ref, GDN_CONV).astype(bf16)
        opv_ref[...] = _conv_bwd(pv, dcv, wv_ref, dwv_ref, GDN_CONV).astype(bf16)

        @pl.when(h == 0)
        def _():
            dba_ref[...] = jnp.zeros_like(dba_ref)
            dal_ref[...] = jnp.zeros_like(dal_ref)
            ddt_ref[...] = jnp.zeros_like(ddt_ref)

        dba_ref[...] += dba
        dal_ref[...] += dal
        ddt_ref[...] += ddt

    def pcol(off):
        return pl.BlockSpec((t, hd), lambda h: (0, h + off // hd))

    def wcol(off):
        return pl.BlockSpec((GDN_CONV, hd), lambda h: (0, h + off // hd))

    vec = pl.BlockSpec((1, hd), lambda h: (0, 0))
    col = pl.BlockSpec((t, hd), lambda h: (0, h))
    full = pl.BlockSpec((t, hd), lambda h: (0, 0))
    wout = pl.BlockSpec((GDN_CONV, hd), lambda h: (0, h))
    return pl.pallas_call(
        body, name="gdn_pre_bwd", grid=(HEADS,),
        in_specs=[pcol(OFF_Q), pcol(OFF_K), pcol(OFF_V), pl.BlockSpec((t, hd), lambda h: (0, OFF_BA // hd)),
                  wcol(0), wcol(GDN_WIDTH), wcol(2 * GDN_WIDTH), vec, vec, col, col, col, col, col],
        out_specs=[col, col, col, full, wout, wout, wout, vec, vec],
        out_shape=[_sds((t, GDN_WIDTH), bf16)] * 3 + [_sds((t, hd), f32)] + [_sds((GDN_CONV, GDN_WIDTH), f32)] * 3
        + [_sds((1, hd), f32)] * 2,
        compiler_params=_cparams(("arbitrary",)))(p, p, p, p, conv_w, conv_w, conv_w, alog, dtb, dq, dk, dv, dbe, dge)


def _tri_inverse(a):
    c = a.shape[0]
    r = lax.broadcasted_iota(jnp.int32, (c, c), 0)
    col = lax.broadcasted_iota(jnp.int32, (c, c), 1)
    m = -a
    inv = jnp.where(r == col, 1.0, 0.0) + m
    s = 2
    while s < c:
        m = _dot(m, m, precision=HI)
        inv = inv + _dot(inv, m, precision=HI)
        s *= 2
    return inv


def _gdn_chunk(s, q, k, v, ge, be):
    c = q.shape[0]
    r = lax.broadcasted_iota(jnp.int32, (c, c), 0)
    col = lax.broadcasted_iota(jnp.int32, (c, c), 1)
    causal = r >= col
    gc = _dot(causal.astype(f32), ge, precision=HI)
    gcc = gc[:, :c]
    gcr = gc.T[:c, :]
    decay = jnp.where(causal, jnp.exp(jnp.where(causal, gcc - gcr, 0.0)), 0.0)
    kb = k * be
    lower = jnp.where(r > col, _bdot(kb, k, NT) * decay, 0.0)
    tinv = _tri_inverse(lower)
    egc = jnp.exp(gc)
    u = _dot(tinv, v * be, precision=HI)
    w = _dot(tinv, kb * egc, precision=HI)
    attn = _bdot(q, k, NT) * decay
    gl = gc[c - 1:c, :]
    v_new = u - _bdot(w, s)
    o = _bdot(q * egc, s) + _bdot(attn, v_new)
    s_new = s * jnp.exp(gl) + _bdot(k * jnp.exp(gl - gc), v_new, TN)
    return o, s_new


def _gdn_core_fwd(q, k, v, ge, be):
    t = q.shape[0]
    c, hd = GDN_CHUNK, HEAD_DIM
    n = t // c

    def body(q_ref, k_ref, v_ref, g_ref, b_ref, o_ref, st_ref, s_ref):
        @pl.when(pl.program_id(1) == 0)
        def _():
            s_ref[...] = jnp.zeros_like(s_ref)

        s = s_ref[...]
        st_ref[0, 0] = s
        o, s_new = _gdn_chunk(s, q_ref[...], k_ref[...], v_ref[...], g_ref[...], b_ref[...])
        o_ref[...] = o
        s_ref[...] = s_new

    tile = pl.BlockSpec((c, hd), lambda h, i: (i, h))
    return pl.pallas_call(
        body, name="gdn_core_fwd", grid=(HEADS, n), in_specs=[tile] * 5,
        out_specs=[tile, pl.BlockSpec((1, 1, hd, hd), lambda h, i: (h, i, 0, 0))],
        out_shape=[_sds((t, GDN_WIDTH), f32), _sds((HEADS, n, hd, hd), f32)],
        scratch_shapes=[pltpu.VMEM((hd, hd), f32)],
        compiler_params=_cparams(("parallel", "arbitrary")))(q, k, v, ge, be)


def _gdn_core_bwd(q, k, v, ge, be, states, do):
    t = q.shape[0]
    c, hd = GDN_CHUNK, HEAD_DIM
    n = t // c

    def body(q_ref, k_ref, v_ref, g_ref, b_ref, st_ref, do_ref, dq_ref, dk_ref, dv_ref, dg_ref, db_ref, ds_ref):
        @pl.when(pl.program_id(1) == 0)
        def _():
            ds_ref[...] = jnp.zeros_like(ds_ref)

        _, vjp = jax.vjp(_gdn_chunk, st_ref[0, 0], q_ref[...], k_ref[...], v_ref[...], g_ref[...], b_ref[...])
        ds, dq, dk, dv, dg, db = vjp((do_ref[...], ds_ref[...]))
        ds_ref[...] = ds
        dq_ref[...], dk_ref[...], dv_ref[...], dg_ref[...], db_ref[...] = dq, dk, dv, dg, db

    tile = pl.BlockSpec((c, hd), lambda h, i: (n - 1 - i, h))
    return pl.pallas_call(
        body, name="gdn_core_bwd", grid=(HEADS, n),
        in_specs=[tile] * 5 + [pl.BlockSpec((1, 1, hd, hd), lambda h, i: (h, n - 1 - i, 0, 0)), tile],
        out_specs=[tile] * 5, out_shape=[_sds((t, GDN_WIDTH), f32)] * 5,
        scratch_shapes=[pltpu.VMEM((hd, hd), f32)],
        compiler_params=_cparams(("parallel", "arbitrary")))(q, k, v, ge, be, states, do)


def _post_fn(o, z, gain):
    return _rms(o, gain) * _silu(z)


def _post_fwd(o, p, z_off, gain, name):
    t = o.shape[0]
    hd = HEAD_DIM

    def body(o_ref, z_ref, g_ref, y_ref):
        y_ref[...] = _post_fn(o_ref[...], z_ref[...], g_ref[...]).astype(bf16)

    col = pl.BlockSpec((t, hd), lambda h: (0, h))
    return pl.pallas_call(
        body, name=name, grid=(HEADS,),
        in_specs=[col, pl.BlockSpec((t, hd), lambda h: (0, h + z_off // hd)), pl.BlockSpec((1, hd), lambda h: (0, 0))],
        out_specs=col, out_shape=_sds((t, HEADS * hd), bf16), compiler_params=_cparams(("parallel",)))(o, p, gain)


def _post_bwd(o, p, z_off, gain, dmix, mix_off, name):
    t = o.shape[0]
    hd = HEAD_DIM

    def body(o_ref, z_ref, g_ref, dy_ref, do_ref, dz_ref, dg_ref):
        _, vjp = jax.vjp(_post_fn, o_ref[...], z_ref[...], g_ref[...])
        do, dz, dg = vjp(dy_ref[...])
        do_ref[...] = do
        dz_ref[...] = dz.astype(bf16)

        @pl.when(pl.program_id(0) == 0)
        def _():
            dg_ref[...] = jnp.zeros_like(dg_ref)

        dg_ref[...] += dg

    col = pl.BlockSpec((t, hd), lambda h: (0, h))
    vec = pl.BlockSpec((1, hd), lambda h: (0, 0))
    return pl.pallas_call(
        body, name=name, grid=(HEADS,),
        in_specs=[col, pl.BlockSpec((t, hd), lambda h: (0, h + z_off // hd)), vec,
                  pl.BlockSpec((t, hd), lambda h: (0, h + mix_off // hd))],
        out_specs=[col, col, vec], out_shape=[_sds((t, HEADS * hd), f32), _sds((t, HEADS * hd), bf16), _sds((1, hd), f32)],
        compiler_params=_cparams(("arbitrary",)))(o, p, gain, dmix)


def _hgrn_pre_fn(qb, fb, lbw, layer):
    l0, l1 = lbw[0:1, :], lbw[1:2, :]
    m = jnp.maximum(l0, l1)
    e0, e1 = jnp.exp(l0 - m), jnp.exp(l1 - m)
    p0, p1 = e0 / (e0 + e1), e1 / (e0 + e1)
    lb = (p0 - p0) if layer == 0 else ((p0 + p1) - p0)
    f = lb + (1.0 - lb) * _sigmoid(fb)
    return _silu(qb), 1.0 - f, jnp.log(jnp.maximum(f, F_FLOOR))


def _hgrn_pre_fwd(p, lbw, layer):
    t = p.shape[0]
    tc = HEAD_DIM

    def body(qb_ref, fb_ref, lb_ref, q_ref, k_ref, lf_ref):
        q_ref[...], k_ref[...], lf_ref[...] = _hgrn_pre_fn(qb_ref[...], fb_ref[...], lb_ref[...], layer)

    col = pl.BlockSpec((t, tc), lambda j: (0, j))
    return pl.pallas_call(
        body, name="hgrn_pre_fwd", grid=(GDN_WIDTH // tc,),
        in_specs=[pl.BlockSpec((t, tc), lambda j: (0, j + OFF_QB // tc)), pl.BlockSpec((t, tc), lambda j: (0, j + OFF_FB // tc)),
                  pl.BlockSpec((2, tc), lambda j: (0, j))],
        out_specs=[col] * 3, out_shape=[_sds((t, GDN_WIDTH), f32)] * 3,
        compiler_params=_cparams(("parallel",)))(p, p, lbw)


def _hgrn_pre_bwd(p, lbw, layer, dq, dk, dlf):
    t = p.shape[0]
    tc = HEAD_DIM

    def body(qb_ref, fb_ref, lb_ref, dq_ref, dk_ref, dlf_ref, dqb_ref, dfb_ref, dlb_ref):
        _, vjp = jax.vjp(functools.partial(_hgrn_pre_fn, layer=layer), qb_ref[...], fb_ref[...], lb_ref[...])
        dqb, dfb, dlb = vjp((dq_ref[...], dk_ref[...], dlf_ref[...]))
        dqb_ref[...] = dqb.astype(bf16)
        dfb_ref[...] = dfb.astype(bf16)
        dlb_ref[...] = dlb

    col = pl.BlockSpec((t, tc), lambda j: (0, j))
    lb = pl.BlockSpec((2, tc), lambda j: (0, j))
    return pl.pallas_call(
        body, name="hgrn_pre_bwd", grid=(GDN_WIDTH // tc,),
        in_specs=[pl.BlockSpec((t, tc), lambda j: (0, j + OFF_QB // tc)), pl.BlockSpec((t, tc), lambda j: (0, j + OFF_FB // tc)),
                  lb, col, col, col],
        out_specs=[col, col, lb], out_shape=[_sds((t, GDN_WIDTH), bf16)] * 2 + [_sds((2, GDN_WIDTH), f32)],
        compiler_params=_cparams(("parallel",)))(p, p, lbw, dq, dk, dlf)


def _hgrn_step(st, q, k, lf, v):
    c = HGRN_CHUNK
    r2 = lax.broadcasted_iota(jnp.int32, (c, c), 0)
    c2 = lax.broadcasted_iota(jnp.int32, (c, c), 1)
    tri = (r2 >= c2).astype(f32)
    i3 = lax.broadcasted_iota(jnp.int32, (c, c, HEAD_DIM), 0)
    j3 = lax.broadcasted_iota(jnp.int32, (c, c, HEAD_DIM), 1)
    mask = i3 >= j3
    outs = []
    for n in range(q.shape[0] // c):
        sl = slice(n * c, (n + 1) * c)
        qc, kc, lc, vc = q[sl], k[sl], lf[sl], v[sl]
        b = _dot(tri, lc, precision=HI)
        rel = jnp.where(mask, jnp.exp(jnp.where(mask, b[:, None, :] - b[None, :, :], 0.0)), 0.0)
        scores = jnp.sum(qc[:, None, :] * kc[None, :, :] * rel, axis=-1)
        bl = b[c - 1:c, :]
        o = _bdot(scores, vc) + _bdot(qc * jnp.exp(b), st, NT)
        st = st * jnp.exp(bl) + _bdot(vc, kc * jnp.exp(bl - b), TN)
        outs.append(o)
    return jnp.concatenate(outs, axis=0), st


def _hgrn_core_fwd(q, k, lf, p):
    t = q.shape[0]
    hd = HEAD_DIM
    rs = min(HGRN_STEP, t)
    n = t // rs

    def body(q_ref, k_ref, lf_ref, v_ref, o_ref, st_ref, s_ref):
        @pl.when(pl.program_id(1) == 0)
        def _():
            s_ref[...] = jnp.zeros_like(s_ref)

        s = s_ref[...]
        st_ref[0, 0] = s
        o, s_new = _hgrn_step(s, q_ref[...], k_ref[...], lf_ref[...], v_ref[...])
        o_ref[...] = o
        s_ref[...] = s_new

    tile = pl.BlockSpec((rs, hd), lambda h, i: (i, h))
    return pl.pallas_call(
        body, name="hgrn_core_fwd", grid=(HEADS, n),
        in_specs=[tile, tile, tile, pl.BlockSpec((rs, hd), lambda h, i: (i, h + OFF_IB // hd))],
        out_specs=[tile, pl.BlockSpec((1, 1, hd, hd), lambda h, i: (h, i, 0, 0))],
        out_shape=[_sds((t, GDN_WIDTH), f32), _sds((HEADS, n, hd, hd), f32)],
        scratch_shapes=[pltpu.VMEM((hd, hd), f32)],
        compiler_params=_cparams(("parallel", "arbitrary")))(q, k, lf, p)


def _hgrn_core_bwd(q, k, lf, p, states, do):
    t = q.shape[0]
    hd = HEAD_DIM
    rs = min(HGRN_STEP, t)
    n = t // rs

    def body(q_ref, k_ref, lf_ref, v_ref, st_ref, do_ref, dq_ref, dk_ref, dlf_ref, dv_ref, ds_ref):
        @pl.when(pl.program_id(1) == 0)
        def _():
            ds_ref[...] = jnp.zeros_like(ds_ref)

        _, vjp = jax.vjp(_hgrn_step, st_ref[0, 0], q_ref[...], k_ref[...], lf_ref[...], v_ref[...])
        ds, dq, dk, dlf, dv = vjp((do_ref[...], ds_ref[...]))
        ds_ref[...] = ds
        dq_ref[...], dk_ref[...], dlf_ref[...] = dq, dk, dlf
        dv_ref[...] = dv.astype(bf16)

    tile = pl.BlockSpec((rs, hd), lambda h, i: (n - 1 - i, h))
    return pl.pallas_call(
        body, name="hgrn_core_bwd", grid=(HEADS, n),
        in_specs=[tile, tile, tile, pl.BlockSpec((rs, hd), lambda h, i: (n - 1 - i, h + OFF_IB // hd)),
                  pl.BlockSpec((1, 1, hd, hd), lambda h, i: (h, n - 1 - i, 0, 0)), tile],
        out_specs=[tile] * 4, out_shape=[_sds((t, GDN_WIDTH), f32)] * 3 + [_sds((t, GDN_WIDTH), bf16)],
        scratch_shapes=[pltpu.VMEM((hd, hd), f32)],
        compiler_params=_cparams(("parallel", "arbitrary")))(q, k, lf, p, states, do)


def _row(v):
    return v.reshape(1, -1)


def _pad_lanes(v, n=HEAD_DIM):
    return jnp.pad(v.reshape(1, -1), ((0, 0), (0, n - v.shape[-1])))


def _ffn_fwd(x, w, l):
    h = _rms_fwd(x, _row(w['norm_ffn'][l]), "ffn_norm")
    u = _mm(h, w['ffn_w_up'][l], name="ffn_up")
    a = _ffn_act_fwd(u, w['ffn_conv_w'][l], _row(w['ffn_conv_b'][l]))
    y = _mm(a, w['ffn_w_down'][l], add=x, name="ffn_down")
    return y, (x, h, u)


def _ffn_bwd(saved, w, l, dy, dyb, grads):
    x, h, u = saved
    da = _mm(dyb, w['ffn_w_down'][l], tb=True, name="ffn_down_dx")
    a, dg, dv, dcw, dcb = _ffn_act_bwd(u, w['ffn_conv_w'][l], _row(w['ffn_conv_b'][l]), da)
    grads['ffn_w_down'][l] = _mm(a, dyb, ta=True, out_dtype=bf16, name="ffn_down_dw")
    du = jnp.concatenate([dg, dv], axis=1)
    grads['ffn_w_up'][l] = _mm(h, du, ta=True, out_dtype=bf16, name="ffn_up_dw")
    dh = _mm(du, w['ffn_w_up'][l], tb=True, name="ffn_up_dx")
    dx, dxb, dgain = _rms_bwd(x, _row(w['norm_ffn'][l]), dh, dy, "ffn_norm_bwd")
    grads['ffn_conv_w'][l] = dcw
    grads['ffn_conv_b'][l] = dcb[0]
    grads['norm_ffn'][l] = dgain[0]
    return dx, dxb


def _odd_fwd(x, w, l, j):
    h = _rms_fwd(x, _row(w['norm_mix'][l]), "mix_norm")
    p = _mm(h, w['c_w_in'][j], name="lru_in")
    xc = _col_conv_fwd(p, LRU_WIDTH, w['c_conv_w'][j], _row(w['c_conv_b'][j]), LRU_CONV, 256, "lru_conv_fwd")
    out, hs, a = _lru_fwd(p, xc, w['c_gate_a_w'][j], _row(w['c_gate_a_b'][j]), w['c_gate_x_w'][j],
                          _row(w['c_gate_x_b'][j]), _row(w['c_lambda'][j]))
    y = _mm(out, w['c_w_out'][j], add=x, name="lru_out")
    return y, (x, h, p, xc, out, hs, a)


def _odd_bwd(saved, w, l, j, dy, dyb, grads):
    x, h, p, xc, out, hs, a = saved
    dout = _mm(dyb, w['c_w_out'][j], tb=True, name="lru_out_dx")
    grads['c_w_out'][j] = _mm(out, dyb, ta=True, out_dtype=bf16, name="lru_out_dw")
    dyb_, da, du = _lru_bwd_scan(p, a, hs, dout)
    dxc, dwa, dwx, dba, dbx, dlam = _lru_bwd_gates(xc, da, du, w['c_gate_a_w'][j], _row(w['c_gate_a_b'][j]),
                                                   w['c_gate_x_w'][j], _row(w['c_gate_x_b'][j]), _row(w['c_lambda'][j]))
    dxb_, dcw, dcb = _col_conv_bwd(p, LRU_WIDTH, w['c_conv_w'][j], dxc, LRU_CONV, 256, "lru_conv_bwd")
    dp = jnp.concatenate([dyb_, dxb_], axis=1)
    grads['c_w_in'][j] = _mm(h, dp, ta=True, out_dtype=bf16, name="lru_in_dw")
    dh = _mm(dp, w['c_w_in'][j], tb=True, name="lru_in_dx")
    dx, dxb, dgain = _rms_bwd(x, _row(w['norm_mix'][l]), dh, dy, "mix_norm_bwd")
    grads['c_gate_a_w'][j], grads['c_gate_x_w'][j] = dwa, dwx
    grads['c_gate_a_b'][j], grads['c_gate_x_b'][j], grads['c_lambda'][j] = dba[0], dbx[0], dlam[0]
    grads['c_conv_w'][j], grads['c_conv_b'][j] = dcw, dcb[0]
    grads['norm_mix'][l] = dgain[0]
    return dx, dxb


def _even_fwd(x, w, l, j):
    h = _rms_fwd(x, _row(w['norm_mix'][l]), "mix_norm")
    p = _mm(h, w['ab_w_in'][j], name="ab_in")
    alog, dtb = _pad_lanes(w['gdn_a_log'][j]), _pad_lanes(w['gdn_dt_bias'][j])
    q, k, v, be, ge = _gdn_pre_fwd(p, w['gdn_conv_w'][j], alog, dtb)
    oa, sa = _gdn_core_fwd(q, k, v, ge, be)
    ya = _post_fwd(oa, p, OFF_Z, _row(w['gdn_norm'][j]), "gdn_post_fwd")
    qq, kk, lf = _hgrn_pre_fwd(p, w['hgrn_lower_bounds'], j)
    ob, sb = _hgrn_core_fwd(qq, kk, lf, p)
    yb = _post_fwd(ob, p, OFF_GB, _row(w['hgrn_norm'][j]), "hgrn_post_fwd")
    mix = jnp.concatenate([ya, yb], axis=1)
    y = _mm(mix, w['ab_w_out'][j], add=x, name="ab_out")
    return y, (x, h, p, q, k, v, be, ge, oa, sa, qq, kk, lf, ob, sb, mix)


def _even_bwd(saved, w, l, j, dy, dyb, grads):
    x, h, p, q, k, v, be, ge, oa, sa, qq, kk, lf, ob, sb, mix = saved
    alog, dtb = _pad_lanes(w['gdn_a_log'][j]), _pad_lanes(w['gdn_dt_bias'][j])
    dmix = _mm(dyb, w['ab_w_out'][j], tb=True, name="ab_out_dx")
    grads['ab_w_out'][j] = _mm(mix, dyb, ta=True, out_dtype=bf16, name="ab_out_dw")
    doa, dz, dgn = _post_bwd(oa, p, OFF_Z, _row(w['gdn_norm'][j]), dmix, 0, "gdn_post_bwd")
    dob, dgb, dhn = _post_bwd(ob, p, OFF_GB, _row(w['hgrn_norm'][j]), dmix, GDN_WIDTH, "hgrn_post_bwd")
    dq, dk, dv, dge, dbe = _gdn_core_bwd(q, k, v, ge, be, sa, doa)
    dpq, dpk, dpv, dba, dwq, dwk, dwv, dal, ddt = _gdn_pre_bwd(p, w['gdn_conv_w'][j], alog, dtb, dq, dk, dv, dbe, dge)
    dqq, dkk, dlf, dib = _hgrn_core_bwd(qq, kk, lf, p, sb, dob)
    dqb, dfb, dlb = _hgrn_pre_bwd(p, w['hgrn_lower_bounds'], j, dqq, dkk, dlf)
    dp = jnp.concatenate([dpq, dpk, dpv, dz, dqb, dfb, dib, dgb, dba.astype(bf16)], axis=1)
    grads['ab_w_in'][j] = _mm(h, dp, ta=True, out_dtype=bf16, name="ab_in_dw")
    dh = _mm(dp, w['ab_w_in'][j], tb=True, name="ab_in_dx")
    dx, dxb, dgain = _rms_bwd(x, _row(w['norm_mix'][l]), dh, dy, "mix_norm_bwd")
    grads['gdn_conv_w'][j] = jnp.concatenate([dwq, dwk, dwv], axis=1)
    grads['gdn_a_log'][j], grads['gdn_dt_bias'][j] = dal[0, :HEADS], ddt[0, :HEADS]
    grads['gdn_norm'][j], grads['hgrn_norm'][j] = dgn[0], dhn[0]
    grads['hgrn_lower_bounds'].append(dlb)
    grads['norm_mix'][l] = dgain[0]
    return dx, dxb


def _ab_permute(w_in):
    n = w_in.shape[0]
    pad = jnp.zeros((n, D_MODEL, AB_PAD - AB_COLS), w_in.dtype)
    return jnp.concatenate([w_in[..., :2048], w_in[..., 2056:], w_in[..., 2048:2056], pad], axis=-1)


def _ab_unpermute(g):
    return jnp.concatenate([g[..., :2048], g[..., 4096:4104], g[..., 2048:4096]], axis=-1)


def _block_pad(a, axis, nblk, padded):
    axis = axis % a.ndim
    s = a.shape
    a = a.reshape(s[:axis] + (nblk, s[axis] // nblk) + s[axis + 1:])
    pad = [(0, 0)] * a.ndim
    pad[axis + 1] = (0, padded - s[axis] // nblk)
    return jnp.pad(a, pad).reshape(s[:axis] + (nblk * padded,) + s[axis + 1:])


def _block_unpad(a, axis, nblk, width):
    axis = axis % a.ndim
    s = a.shape
    a = a.reshape(s[:axis] + (nblk, s[axis] // nblk) + s[axis + 1:])
    a = lax.slice_in_dim(a, 0, width, axis=axis + 1)
    return a.reshape(s[:axis] + (nblk * width,) + s[axis + 1:])


def _kernel_layout(w):
    w = dict(w)
    w['ab_w_in'] = _ab_permute(w['ab_w_in'])
    w['ffn_w_up'] = _block_pad(w['ffn_w_up'], 2, N_DEV, FF_PAD)
    w['ffn_w_down'] = _block_pad(w['ffn_w_down'], 1, 4, FF_PAD)
    w['ffn_conv_w'] = _block_pad(w['ffn_conv_w'], 2, 4, FF_PAD)
    w['ffn_conv_b'] = _block_pad(w['ffn_conv_b'], 1, 4, FF_PAD)
    return w


def _natural_grads(g):
    g = dict(g)
    g['ab_w_in'] = _ab_unpermute(g['ab_w_in'])
    g['ffn_w_up'] = _block_unpad(g['ffn_w_up'], 2, N_DEV, FF_SHARD)
    g['ffn_w_down'] = _block_unpad(g['ffn_w_down'], 1, 4, FF_SHARD)
    g['ffn_conv_w'] = _block_unpad(g['ffn_conv_w'], 2, 4, FF_SHARD)
    g['ffn_conv_b'] = _block_unpad(g['ffn_conv_b'], 1, 4, FF_SHARD)
    return g


def _local_step(x, target, w):
    grads = {n: [None] * w[n].shape[0] for n in WEIGHTS if n not in ('norm_final', 'hgrn_lower_bounds')}
    grads['hgrn_lower_bounds'] = []
    saved = []
    for l in range(DEPTH):
        j = l // 2
        x, s_mix = (_even_fwd if l % 2 == 0 else _odd_fwd)(x, w, l, j)
        x, s_ffn = _ffn_fwd(x, w, l)
        saved.append((s_mix, s_ffn))
    loss, dx, dxb, dgf = _loss_head(x, _row(w['norm_final']), target)
    for l in reversed(range(DEPTH)):
        j = l // 2
        s_mix, s_ffn = saved[l]
        dx, dxb = _ffn_bwd(s_ffn, w, l, dx, dxb, grads)
        dx, dxb = (_even_bwd if l % 2 == 0 else _odd_bwd)(s_mix, w, l, j, dx, dxb, grads)
    out = {n: jnp.stack(g) for n, g in grads.items() if n != 'hgrn_lower_bounds'}
    out['hgrn_lower_bounds'] = grads['hgrn_lower_bounds'][0] + grads['hgrn_lower_bounds'][1]
    out['norm_final'] = dgf[0]
    return loss[0, 0], dx, out


def _position():
    return lax.axis_index("x"), lax.axis_index("y"), lax.axis_index("c")


BLOCK_LAYOUT = {
    'ab_w_in': ((2, D_MODEL, N_DEV * AB_SHARD_PAD), (2, D_MODEL, AB_SHARD_PAD)),
    'ab_w_out': ((2, N_DEV, 128, D_MODEL), (2, 128, D_MODEL)),
    'c_w_in': ((2, D_MODEL, 2 * LRU_WIDTH), (2, D_MODEL, 256)),
    'c_w_out': ((2, N_DEV, 128, D_MODEL), (2, 128, D_MODEL)),
    'c_gate_a_w': ((2, HEADS, N_DEV, 32, LRU_BLOCK), (2, HEADS, 32, LRU_BLOCK)),
    'c_gate_x_w': ((2, HEADS, N_DEV, 32, LRU_BLOCK), (2, HEADS, 32, LRU_BLOCK)),
    'ffn_w_up': ((DEPTH, D_MODEL, N_DEV * FF_PAD), (DEPTH, D_MODEL, FF_PAD)),
    'ffn_w_down': ((DEPTH, 4, FF_PAD, D_MODEL), (DEPTH, FF_ROWS, D_MODEL)),
}


def _block_of(name, ref, p):
    d = 4 * p[0] + 2 * p[1] + p[2]
    if name == 'ab_w_in':
        return ref.at[:, :, pl.ds(pl.multiple_of(d * AB_SHARD_PAD, 128), AB_SHARD_PAD)]
    if name == 'c_w_in':
        return ref.at[:, :, pl.ds(pl.multiple_of(d * 256, 128), 256)]
    if name == 'ffn_w_up':
        return ref.at[:, :, pl.ds(pl.multiple_of(d * FF_PAD, 128), FF_PAD)]
    if name == 'ffn_w_down':
        return ref.at[:, 2 * p[0] + p[1], pl.ds(pl.multiple_of(p[2] * FF_ROWS, 16), FF_ROWS), :]
    if name in ('c_gate_a_w', 'c_gate_x_w'):
        return ref.at[:, :, d]
    if name in ('ab_w_out', 'c_w_out'):
        return ref.at[:, d]
    return ref.at[d]


def _gather_weights(shards):
    names = list(shards)
    n = len(names)
    shapes = {nm: (BLOCK_LAYOUT[nm][0] if nm in BLOCK_LAYOUT else (N_DEV,) + shards[nm].shape) for nm in names}

    def body(*refs):
        ins = dict(zip(names, refs[:n]))
        outs = dict(zip(names, refs[n + 1:2 * n + 1]))
        send_sems, recv_sems, local_sems = refs[2 * n + 1:]
        x, y, c = _position()
        me, sibling = (x, y, c), (x, y, 1 - c)
        chips = [(1 - x, y), (x, 1 - y), (1 - x, 1 - y)]

        def copy(i, k, block, to, src=None):
            dst = _block_of(names[i], outs[names[i]], block)
            return pltpu.make_async_remote_copy(
                src_ref=dst if src is None else src, dst_ref=dst, send_sem=send_sems.at[7 * i + k],
                recv_sem=recv_sems.at[7 * i + k], device_id=to, device_id_type=MESH)

        local = [pltpu.make_async_copy(ins[nm], _block_of(nm, outs[nm], me), local_sems.at[i])
                 for i, nm in enumerate(names)]
        for cp in local:
            cp.start()
        first = []
        for i, nm in enumerate(names):
            first.append(copy(i, 0, me, sibling, src=ins[nm]))
            first += [copy(i, 1 + j, me, (*chip, c), src=ins[nm]) for j, chip in enumerate(chips)]
        for cp in first:
            cp.start()
        passed = []
        for j, chip in enumerate(chips):
            for i in range(n):
                copy(i, 1 + j, (*chip, c), me).wait_recv()
                fwd = copy(i, 4 + j, (*chip, c), sibling)
                fwd.start()
                passed.append(fwd)
        for i in range(n):
            copy(i, 0, sibling, me).wait_recv()
        for j, chip in enumerate(chips):
            for i in range(n):
                copy(i, 4 + j, (*chip, 1 - c), me).wait_recv()
        for cp in first + passed:
            cp.wait_send()
        for cp in local:
            cp.wait()

    any_spec = pl.BlockSpec(memory_space=pl.ANY)
    down = names.index('ffn_w_down')
    zeros = jnp.zeros(shapes['ffn_w_down'], shards['ffn_w_down'].dtype)
    outs = pl.pallas_call(
        body, name="gather_weights", out_shape=[_sds(shapes[nm], shards[nm].dtype) for nm in names],
        in_specs=[any_spec] * (n + 1), out_specs=[any_spec] * n, input_output_aliases={n: down},
        scratch_shapes=[pltpu.SemaphoreType.DMA((7 * n,)), pltpu.SemaphoreType.DMA((7 * n,)),
                        pltpu.SemaphoreType.DMA((n,))],
    )(*[shards[nm] for nm in names], zeros)
    return dict(zip(names, outs))


def _exchange_grads(fulls, rep):
    names = list(fulls)
    n = len(names)
    shard_shape = {nm: (BLOCK_LAYOUT[nm][1] if nm in BLOCK_LAYOUT else fulls[nm].shape[1:]) for nm in names}

    def body(*refs):
        ins = dict(zip(names, refs[:n]))
        rep_ref = refs[n]
        outs = dict(zip(names, refs[n + 1:2 * n + 1]))
        rrep_ref = refs[2 * n + 1]
        send_sems, recv_sems, local_sems = refs[2 * n + 2:]
        x, y, c = _position()
        me = (x, y, c)
        slot = 4 * x + 2 * y + c
        copies = []
        for k in range(1, N_DEV):
            peer = (1 - x if (k >> 2) & 1 else x, 1 - y if (k >> 1) & 1 else y, 1 - c if k & 1 else c)
            for i, nm in enumerate(names + ['']):
                src = rep_ref if i == n else _block_of(nm, ins[nm], peer)
                dst = (rrep_ref if i == n else outs[nm]).at[slot]
                copies.append(pltpu.make_async_remote_copy(
                    src_ref=src, dst_ref=dst, send_sem=send_sems.at[7 * i + k - 1], recv_sem=recv_sems.at[7 * i + k - 1],
                    device_id=peer, device_id_type=MESH))
        own = [pltpu.make_async_copy(_block_of(nm, ins[nm], me), outs[nm].at[slot], local_sems.at[i])
               for i, nm in enumerate(names)]
        own.append(pltpu.make_async_copy(rep_ref, rrep_ref.at[slot], local_sems.at[n]))
        for cp in own + copies:
            cp.start()
        for cp in copies:
            cp.wait_recv()
        for cp in copies:
            cp.wait_send()
        for cp in own:
            cp.wait()

    any_spec = pl.BlockSpec(memory_space=pl.ANY)
    outs = pl.pallas_call(
        body, name="grad_exchange",
        out_shape=[_sds((N_DEV,) + tuple(shard_shape[nm]), fulls[nm].dtype) for nm in names]
        + [_sds((N_DEV,) + rep.shape, rep.dtype)],
        in_specs=[any_spec] * (n + 1), out_specs=[any_spec] * (n + 1),
        scratch_shapes=[pltpu.SemaphoreType.DMA((7 * (n + 1),)), pltpu.SemaphoreType.DMA((7 * (n + 1),)),
                        pltpu.SemaphoreType.DMA((n + 1,))],
    )(*[fulls[nm] for nm in names], rep)
    return dict(zip(names, outs[:n])), outs[n]


def _sum_adamw(parts, w, m, v, name):
    r, l = w.shape
    lp = parts.shape[2]
    tr = _tile(r, (256, 128, 64, 32, 16, 8))
    c1 = 1.0 / (1.0 - ADAM_B1 ** ADAM_STEP)
    c2 = 1.0 / (1.0 - ADAM_B2 ** ADAM_STEP)

    def body(p_ref, w_ref, m_ref, v_ref, g_ref, d_ref, nm_ref, nv_ref):
        g = p_ref[0].astype(f32)
        for s in range(1, N_DEV):
            g = g + p_ref[s].astype(f32)
        if lp != l:
            g = g[:, :l]
        m_new = ADAM_B1 * m_ref[...] + (1.0 - ADAM_B1) * g
        v_new = ADAM_B2 * v_ref[...] + (1.0 - ADAM_B2) * (g * g)
        g_ref[...] = g
        nm_ref[...] = m_new
        nv_ref[...] = v_new
        d_ref[...] = -ADAM_LR * ((m_new * c1) / (jnp.sqrt(v_new * c2) + ADAM_EPS) + ADAM_WD * w_ref[...])

    tile = pl.BlockSpec((tr, l), lambda i: (i, 0))
    return pl.pallas_call(
        body, name=name, grid=(r // tr,), in_specs=[pl.BlockSpec((N_DEV, tr, lp), lambda i: (0, i, 0)), tile, tile, tile],
        out_specs=[tile] * 4, out_shape=[_sds((r, l), f32)] * 4, compiler_params=_cparams(("parallel",)))(parts, w, m, v)


def _pack(arrs, lead=None):
    if lead is None:
        flat = jnp.concatenate([a.reshape(-1).astype(f32) for a in arrs])
        n = flat.shape[0]
    else:
        flat = jnp.concatenate([a.reshape(lead, -1).astype(f32) for a in arrs], axis=1)
        n = flat.shape[1]
    tot = -(-n // 1024) * 1024
    if lead is None:
        return jnp.pad(flat, (0, tot - n)).reshape(tot // 128, 128)
    return jnp.pad(flat, ((0, 0), (0, tot - n))).reshape(lead, tot // 128, 128)


def _unpack(packed, shapes, lead=False):
    flat = packed.reshape(packed.shape[0], -1) if lead else packed.reshape(-1)
    out, off = [], 0
    for s in shapes:
        n = int(np.prod(s))
        out.append(flat[:, off:off + n].reshape((packed.shape[0],) + tuple(s)) if lead else flat[off:off + n].reshape(s))
        off += n
    return out


def _merge_shards(g, axis):
    g = jnp.moveaxis(g, 0, axis)
    s = g.shape
    return g.reshape(s[:axis] + (s[axis] * s[axis + 1],) + s[axis + 2:])


def _split_shards(full, axis):
    s = full.shape
    g = full.reshape(s[:axis] + (N_DEV, s[axis] // N_DEV) + s[axis + 1:])
    return jnp.moveaxis(g, axis, 0)


def kernel(x, norm_mix, norm_ffn, norm_final, ab_w_in, gdn_conv_w, gdn_a_log, gdn_dt_bias, gdn_norm, hgrn_lower_bounds, hgrn_norm, ab_w_out, c_w_in, c_conv_w, c_conv_b, c_gate_a_w, c_gate_a_b, c_gate_x_w, c_gate_x_b, c_lambda, c_w_out, ffn_w_up, ffn_conv_w, ffn_conv_b, ffn_w_down, loss_target, m_norm_mix, m_norm_ffn, m_norm_final, m_ab_w_in, m_gdn_conv_w, m_gdn_a_log, m_gdn_dt_bias, m_gdn_norm, m_hgrn_lower_bounds, m_hgrn_norm, m_ab_w_out, m_c_w_in, m_c_conv_w, m_c_conv_b, m_c_gate_a_w, m_c_gate_a_b, m_c_gate_x_w, m_c_gate_x_b, m_c_lambda, m_c_w_out, m_ffn_w_up, m_ffn_conv_w, m_ffn_conv_b, m_ffn_w_down, v_norm_mix, v_norm_ffn, v_norm_final, v_ab_w_in, v_gdn_conv_w, v_gdn_a_log, v_gdn_dt_bias, v_gdn_norm, v_hgrn_lower_bounds, v_hgrn_norm, v_ab_w_out, v_c_w_in, v_c_conv_w, v_c_conv_b, v_c_gate_a_w, v_c_gate_a_b, v_c_gate_x_w, v_c_gate_x_b, v_c_lambda, v_c_w_out, v_ffn_w_up, v_ffn_conv_w, v_ffn_conv_b, v_ffn_w_down):
    wl = dict(zip(WEIGHTS, (norm_mix, norm_ffn, norm_final, ab_w_in, gdn_conv_w, gdn_a_log, gdn_dt_bias, gdn_norm, hgrn_lower_bounds, hgrn_norm, ab_w_out, c_w_in, c_conv_w, c_conv_b, c_gate_a_w, c_gate_a_b, c_gate_x_w, c_gate_x_b, c_lambda, c_w_out, ffn_w_up, ffn_conv_w, ffn_conv_b, ffn_w_down)))
    ml = dict(zip(WEIGHTS, (m_norm_mix, m_norm_ffn, m_norm_final, m_ab_w_in, m_gdn_conv_w, m_gdn_a_log, m_gdn_dt_bias, m_gdn_norm, m_hgrn_lower_bounds, m_hgrn_norm, m_ab_w_out, m_c_w_in, m_c_conv_w, m_c_conv_b, m_c_gate_a_w, m_c_gate_a_b, m_c_gate_x_w, m_c_gate_x_b, m_c_lambda, m_c_w_out, m_ffn_w_up, m_ffn_conv_w, m_ffn_conv_b, m_ffn_w_down)))
    vl = dict(zip(WEIGHTS, (v_norm_mix, v_norm_ffn, v_norm_final, v_ab_w_in, v_gdn_conv_w, v_gdn_a_log, v_gdn_dt_bias, v_gdn_norm, v_hgrn_lower_bounds, v_hgrn_norm, v_ab_w_out, v_c_w_in, v_c_conv_w, v_c_conv_b, v_c_gate_a_w, v_c_gate_a_b, v_c_gate_x_w, v_c_gate_x_b, v_c_lambda, v_c_w_out, v_ffn_w_up, v_ffn_conv_w, v_ffn_conv_b, v_ffn_w_down)))

    big = [n for n in SHARDED if n in MATMUL_WEIGHTS]
    vec = [n for n in SHARDED if n not in MATMUL_WEIGHTS]
    shards = {n: wl[n].astype(bf16) for n in big}
    shards['ab_w_in'] = jnp.pad(shards['ab_w_in'], ((0, 0), (0, 0), (0, AB_SHARD_PAD - AB_SHARD)))
    shards['ffn_w_up'] = jnp.pad(shards['ffn_w_up'], ((0, 0), (0, 0), (0, FF_PAD - FF_SHARD)))
    shards['vec'] = _pack([wl[n] for n in vec])
    got = _gather_weights(shards)
    full = {n: wl[n] for n in REPLICATED}
    for n, a in zip(vec, _unpack(got['vec'], [wl[n].shape for n in vec], lead=True)):
        full[n] = _merge_shards(a, SHARD_AXIS[n])
    full['ffn_conv_w'] = _block_pad(full['ffn_conv_w'], 2, 4, FF_PAD)
    full['ffn_conv_b'] = _block_pad(full['ffn_conv_b'], 1, 4, FF_PAD)
    full['ab_w_in'] = _ab_permute(_block_unpad(got['ab_w_in'], 2, N_DEV, AB_SHARD))
    full['c_w_in'] = got['c_w_in']
    full['ffn_w_up'] = got['ffn_w_up']
    full['ffn_w_down'] = got['ffn_w_down'].reshape(DEPTH, D_FFP, D_MODEL)
    for n in ('ab_w_out', 'c_w_out'):
        full[n] = got[n].reshape(2, D_MODEL, D_MODEL)
    for n in ('c_gate_a_w', 'c_gate_x_w'):
        full[n] = got[n].reshape(2, HEADS, LRU_BLOCK, LRU_BLOCK)

    loss, dx, grads = _local_step(x[0], loss_target[0], full)

    grads['ffn_conv_w'] = _block_unpad(grads['ffn_conv_w'], 2, 4, FF_SHARD)
    grads['ffn_conv_b'] = _block_unpad(grads['ffn_conv_b'], 1, 4, FF_SHARD)
    fulls = {n: grads[n].astype(bf16).reshape(BLOCK_LAYOUT[n][0]) for n in big if n != 'ab_w_in'}
    fulls['ab_w_in'] = _block_pad(_ab_unpermute(grads['ab_w_in']), 2, N_DEV, AB_SHARD_PAD)
    fulls = {n: fulls[n] for n in big}
    fulls['vec'] = _pack([_split_shards(grads[n], SHARD_AXIS[n]) for n in vec], lead=N_DEV)
    recv, rrep = _exchange_grads(fulls, _pack([grads[n] for n in REPLICATED]))
    res = {}
    for n in big:
        shp = wl[n].shape
        r, c = int(np.prod(shp[:-1])), shp[-1]
        outs = _sum_adamw(recv[n].reshape(N_DEV, r, -1), wl[n].reshape(r, c), ml[n].reshape(r, c), vl[n].reshape(r, c),
                          "adamw_" + n)
        for kind, o in zip(("grad", "delta", "new_m", "new_v"), outs):
            res[kind, n] = o.reshape(shp)
    for names, parts, tag in ((vec, recv['vec'], "adamw_vectors"), (REPLICATED, rrep, "adamw_replicated")):
        outs = _sum_adamw(parts, _pack([wl[n] for n in names]), _pack([ml[n] for n in names]),
                          _pack([vl[n] for n in names]), tag)
        for kind, o in zip(("grad", "delta", "new_m", "new_v"), outs):
            for n, a in zip(names, _unpack(o, [wl[n].shape for n in names])):
                res[kind, n] = a

    loss = lax.psum(loss, ("x", "y", "c"))
    return (loss, dx[None], *[res[kind, n] for kind in ("grad", "delta", "new_m", "new_v") for n in WEIGHTS])
```

```python
import functools

import numpy as np
import jax
import jax.numpy as jnp
from jax import lax
from jax.experimental import pallas as pl
from jax.experimental.pallas import tpu as pltpu

f32 = jnp.float32
bf16 = jnp.bfloat16
HI = lax.Precision.HIGHEST
MESH = pl.DeviceIdType.MESH

N_DEV = 8
D_MODEL = 1024
DEPTH = 4
EPS = 1e-6
F_FLOOR = 1e-30
HEADS = 4
HEAD_DIM = 128
GDN_WIDTH = 512
GDN_CONV = 4
GDN_CHUNK = 64
HGRN_CHUNK = 16
HGRN_STEP = 128
MIX_WIDTH = 1024
AB_COLS = 4104
AB_PAD = 4224
LRU_WIDTH = 1024
LRU_BLOCK = 256
LRU_CONV = 4
RG_C = 8.0
D_FF = 2816
FF_SHARD = 704
FF_PAD = 768
D_FFP = 4 * FF_PAD
FF_ROWS = 352
AB_SHARD, AB_SHARD_PAD = 513, 640
FFN_CONV = 3
ADAM_LR, ADAM_B1, ADAM_B2, ADAM_EPS, ADAM_WD, ADAM_STEP = 0.001, 0.9, 0.999, 1e-08, 0.01, 10
VMEM_LIMIT = 56 * 1024 * 1024
PACK_LANES = 512
PACK_ROWS = 256

OFF_Q, OFF_K, OFF_V, OFF_Z, OFF_QB, OFF_FB, OFF_IB, OFF_GB, OFF_BA = 0, 512, 1024, 1536, 2048, 2560, 3072, 3584, 4096

WEIGHTS = ['norm_mix', 'norm_ffn', 'norm_final', 'ab_w_in', 'gdn_conv_w', 'gdn_a_log', 'gdn_dt_bias', 'gdn_norm',
           'hgrn_lower_bounds', 'hgrn_norm', 'ab_w_out', 'c_w_in', 'c_conv_w', 'c_conv_b', 'c_gate_a_w', 'c_gate_a_b',
           'c_gate_x_w', 'c_gate_x_b', 'c_lambda', 'c_w_out', 'ffn_w_up', 'ffn_conv_w', 'ffn_conv_b', 'ffn_w_down']
SHARD_AXIS = {'norm_mix': None, 'norm_ffn': None, 'norm_final': None, 'ab_w_in': 2, 'gdn_conv_w': 2, 'gdn_a_log': None,
              'gdn_dt_bias': None, 'gdn_norm': None, 'hgrn_lower_bounds': None, 'hgrn_norm': None, 'ab_w_out': 1,
              'c_w_in': 2, 'c_conv_w': 2, 'c_conv_b': 1, 'c_gate_a_w': 2, 'c_gate_a_b': 1, 'c_gate_x_w': 2,
              'c_gate_x_b': 1, 'c_lambda': 1, 'c_w_out': 1, 'ffn_w_up': 2, 'ffn_conv_w': 2, 'ffn_conv_b': None,
              'ffn_w_down': 1}
MATMUL_WEIGHTS = ('ab_w_in', 'ab_w_out', 'c_w_in', 'c_gate_a_w', 'c_gate_x_w', 'c_w_out', 'ffn_w_up', 'ffn_w_down')
SHARDED = [n for n in WEIGHTS if SHARD_AXIS[n] is not None]
REPLICATED = [n for n in WEIGHTS if SHARD_AXIS[n] is None]


def _tile(n, prefs=(512, 384, 256, 128)):
    for p in prefs:
        if n % p == 0:
            return p
    return n


def _cparams(sem=None):
    kw = dict(vmem_limit_bytes=VMEM_LIMIT)
    if sem is not None:
        kw['dimension_semantics'] = sem
    return pltpu.CompilerParams(**kw)


def _sds(shape, dtype):
    return jax.ShapeDtypeStruct(tuple(shape), dtype)


def _sigmoid(x):
    return 1.0 / (1.0 + jnp.exp(-x))


def _silu(x):
    return x * _sigmoid(x)


def _log1p(x):
    u = 1.0 + x
    return jnp.where(u == 1.0, x, jnp.log(u) * (x / jnp.where(u == 1.0, 1.0, u - 1.0)))


def _softplus(x):
    return jnp.maximum(x, 0.0) + _log1p(jnp.exp(-jnp.abs(x)))


def _expm1(x):
    small = jnp.abs(x) < 0.05
    xs = jnp.where(small, x, 0.0)
    series = xs * (1.0 + xs * (0.5 + xs * (1.0 / 6.0 + xs * (1.0 / 24.0 + xs * (1.0 / 120.0)))))
    return jnp.where(small, series, jnp.exp(x) - 1.0)


def _gelu(x):
    return 0.5 * x * (1.0 + jnp.tanh(0.7978845608028654 * (x + 0.044715 * x * x * x)))


def _rms(x, gain):
    return x * lax.rsqrt(jnp.mean(x * x, axis=-1, keepdims=True) + EPS) * gain


def _dot(a, b, dims=((1,), (0,)), precision=None):
    return lax.dot_general(a, b, (dims, ((), ())), precision=precision, preferred_element_type=f32)


def _bdot(a, b, dims=((1,), (0,))):
    return _dot(a.astype(bf16), b.astype(bf16), dims)


NT = ((1,), (1,))
TN = ((0,), (0,))


def _shift_down(x, k):
    if k == 0:
        return x
    row = lax.broadcasted_iota(jnp.int32, x.shape, 0)
    return jnp.where(row >= k, pltpu.roll(x, k, 0), 0.0)


def _shift_up(x, k, fill=0.0):
    if k == 0:
        return x
    n = x.shape[0]
    row = lax.broadcasted_iota(jnp.int32, x.shape, 0)
    return jnp.where(row < n - k, pltpu.roll(x, n - k, 0), fill)


def _conv_fwd(x, w_ref, width):
    acc = w_ref[width - 1:width, :] * x
    for k in range(width - 1):
        acc = acc + w_ref[k:k + 1, :] * _shift_down(x, width - 1 - k)
    return acc


def _conv_bwd(x, dout, w_ref, dw_ref, width):
    dx = w_ref[width - 1:width, :] * dout
    dw_ref[width - 1:width, :] = jnp.sum(dout * x, axis=0, keepdims=True)
    for k in range(width - 1):
        s = width - 1 - k
        dx = dx + w_ref[k:k + 1, :] * _shift_up(dout, s)
        dw_ref[k:k + 1, :] = jnp.sum(dout * _shift_down(x, s), axis=0, keepdims=True)
    return dx


def _mm(a, b, *, ta=False, tb=False, add=None, out_dtype=f32, name):
    m, k = (a.shape[1], a.shape[0]) if ta else a.shape
    n = b.shape[0] if tb else b.shape[1]
    tm, tn = _tile(m), _tile(n)
    dims = ((0 if ta else 1,), (1 if tb else 0,))

    def body(*refs):
        a_ref, b_ref = refs[0], refs[1]
        o_ref = refs[-1]
        r = _dot(a_ref[...], b_ref[...], dims)
        if add is not None:
            r = r + refs[2][...]
        o_ref[...] = r.astype(out_dtype)

    a_spec = pl.BlockSpec((k, tm), lambda j, i: (0, i)) if ta else pl.BlockSpec((tm, k), lambda j, i: (i, 0))
    b_spec = pl.BlockSpec((tn, k), lambda j, i: (j, 0)) if tb else pl.BlockSpec((k, tn), lambda j, i: (0, j))
    o_spec = pl.BlockSpec((tm, tn), lambda j, i: (i, j))
    ins, specs = [a, b], [a_spec, b_spec]
    if add is not None:
        ins.append(add)
        specs.append(o_spec)
    return pl.pallas_call(body, name=name, grid=(n // tn, m // tm), in_specs=specs, out_specs=o_spec,
                          out_shape=_sds((m, n), out_dtype), compiler_params=_cparams(("parallel", "parallel")))(*ins)


def _rms_fwd(x, gain, name):
    t, d = x.shape
    tr = _tile(t, (256, 128))

    def body(x_ref, g_ref, h_ref):
        h_ref[...] = _rms(x_ref[...], g_ref[...]).astype(bf16)

    return pl.pallas_call(body, name=name, grid=(t // tr,),
                          in_specs=[pl.BlockSpec((tr, d), lambda i: (i, 0)), pl.BlockSpec((1, d), lambda i: (0, 0))],
                          out_specs=pl.BlockSpec((tr, d), lambda i: (i, 0)), out_shape=_sds((t, d), bf16),
                          compiler_params=_cparams(("parallel",)))(x, gain)


def _rms_bwd(x, gain, dh, dres, name):
    t, d = x.shape
    tr = _tile(t, (256, 128))

    def body(x_ref, g_ref, dh_ref, dres_ref, dx_ref, dxb_ref, dg_ref):
        _, vjp = jax.vjp(_rms, x_ref[...], g_ref[...])
        dx, dg = vjp(dh_ref[...])
        dx = dx + dres_ref[...]
        dx_ref[...] = dx
        dxb_ref[...] = dx.astype(bf16)

        @pl.when(pl.program_id(0) == 0)
        def _():
            dg_ref[...] = jnp.zeros_like(dg_ref)

        dg_ref[...] += dg

    row = pl.BlockSpec((tr, d), lambda i: (i, 0))
    vec = pl.BlockSpec((1, d), lambda i: (0, 0))
    return pl.pallas_call(body, name=name, grid=(t // tr,), in_specs=[row, vec, row, row], out_specs=[row, row, vec],
                          out_shape=[_sds((t, d), f32), _sds((t, d), bf16), _sds((1, d), f32)],
                          compiler_params=_cparams(("arbitrary",)))(x, gain, dh, dres)


def _loss_head(x, gain, target):
    t, d = x.shape
    tr = _tile(t, (256, 128))

    def f(xv, g, tgt):
        err = _rms(xv, g) - tgt
        return 0.5 * jnp.sum(jnp.mean(err * err, axis=-1, keepdims=True), axis=0, keepdims=True)

    def body(x_ref, g_ref, t_ref, loss_ref, dx_ref, dxb_ref, dg_ref):
        loss, vjp = jax.vjp(lambda xv, g: f(xv, g, t_ref[...]), x_ref[...], g_ref[...])
        dx, dg = vjp(jnp.ones((1, 1), f32))
        dx_ref[...] = dx
        dxb_ref[...] = dx.astype(bf16)

        @pl.when(pl.program_id(0) == 0)
        def _():
            dg_ref[...] = jnp.zeros_like(dg_ref)
            loss_ref[...] = jnp.zeros_like(loss_ref)

        dg_ref[...] += dg
        loss_ref[...] += jnp.broadcast_to(loss, loss_ref.shape)

    row = pl.BlockSpec((tr, d), lambda i: (i, 0))
    vec = pl.BlockSpec((1, d), lambda i: (0, 0))
    one = pl.BlockSpec((8, 128), lambda i: (0, 0))
    return pl.pallas_call(body, name="loss_head", grid=(t // tr,), in_specs=[row, vec, row],
                          out_specs=[one, row, row, vec],
                          out_shape=[_sds((8, 128), f32), _sds((t, d), f32), _sds((t, d), bf16), _sds((1, d), f32)],
                          compiler_params=_cparams(("arbitrary",)))(x, gain, target)


def _ffn_act_fwd(u, conv_w, conv_b):
    t = u.shape[0]
    tc = FF_PAD // 2
    nb = D_FFP // tc

    def body(g_ref, v_ref, w_ref, b_ref, a_ref):
        gc = _conv_fwd(g_ref[...], w_ref, FFN_CONV) + b_ref[...]
        a_ref[...] = (_silu(gc) * v_ref[...]).astype(bf16)

    return pl.pallas_call(
        body, name="ffn_act_fwd", grid=(nb,),
        in_specs=[pl.BlockSpec((t, tc), lambda j: (0, j)), pl.BlockSpec((t, tc), lambda j: (0, j + nb)),
                  pl.BlockSpec((FFN_CONV, tc), lambda j: (0, j)), pl.BlockSpec((1, tc), lambda j: (0, j))],
        out_specs=pl.BlockSpec((t, tc), lambda j: (0, j)), out_shape=_sds((t, D_FFP), bf16),
        compiler_params=_cparams(("parallel",)))(u, u, conv_w, conv_b)


def _ffn_act_bwd(u, conv_w, conv_b, da):
    t = u.shape[0]
    tc = FF_PAD // 2
    nb = D_FFP // tc

    def act(gc, val):
        return _silu(gc) * val

    def body(g_ref, v_ref, w_ref, b_ref, da_ref, a_ref, dg_ref, dv_ref, dw_ref, db_ref):
        gp = g_ref[...]
        gc = _conv_fwd(gp, w_ref, FFN_CONV) + b_ref[...]
        a, vjp = jax.vjp(act, gc, v_ref[...])
        dgc, dval = vjp(da_ref[...])
        a_ref[...] = a.astype(bf16)
        dv_ref[...] = dval.astype(bf16)
        db_ref[...] = jnp.sum(dgc, axis=0, keepdims=True)
        dg_ref[...] = _conv_bwd(gp, dgc, w_ref, dw_ref, FFN_CONV).astype(bf16)

    col = pl.BlockSpec((t, tc), lambda j: (0, j))
    return pl.pallas_call(
        body, name="ffn_act_bwd", grid=(nb,),
        in_specs=[col, pl.BlockSpec((t, tc), lambda j: (0, j + nb)), pl.BlockSpec((FFN_CONV, tc), lambda j: (0, j)),
                  pl.BlockSpec((1, tc), lambda j: (0, j)), col],
        out_specs=[col, col, col, pl.BlockSpec((FFN_CONV, tc), lambda j: (0, j)), pl.BlockSpec((1, tc), lambda j: (0, j))],
        out_shape=[_sds((t, D_FFP), bf16), _sds((t, D_FFP), bf16), _sds((t, D_FFP), bf16), _sds((FFN_CONV, D_FFP), f32),
                   _sds((1, D_FFP), f32)],
        compiler_params=_cparams(("parallel",)))(u, u, conv_w, conv_b, da)


def _lru_gates(xc, ra, ia, lam):
    r = _sigmoid(ra)
    i = _sigmoid(ia)
    log_a = -RG_C * r * _softplus(-lam)
    a = jnp.exp(log_a)
    u = jnp.sqrt(jnp.maximum(-_expm1(2.0 * log_a), 0.0)) * (i * xc)
    return a, u


def _lin_scan(a, u):
    n = a.shape[0]
    row = lax.broadcasted_iota(jnp.int32, a.shape, 0)
    s = 1
    while s < n:
        keep = row >= s
        u = a * jnp.where(keep, pltpu.roll(u, s, 0), 0.0) + u
        a = a * jnp.where(keep, pltpu.roll(a, s, 0), 1.0)
        s *= 2
    return u


def _rev_scan(a_next, d):
    n = d.shape[0]
    row = lax.broadcasted_iota(jnp.int32, d.shape, 0)
    a = a_next
    s = 1
    while s < n:
        keep = row < n - s
        d = a * jnp.where(keep, pltpu.roll(d, n - s, 0), 0.0) + d
        a = a * jnp.where(keep, pltpu.roll(a, n - s, 0), 1.0)
        s *= 2
    return d


def _col_conv_fwd(p, col_off, conv_w, conv_b, width, tc, name):
    t = p.shape[0]
    c = conv_w.shape[1]
    ob = col_off // tc

    def body(x_ref, w_ref, b_ref, o_ref):
        o_ref[...] = _conv_fwd(x_ref[...], w_ref, width) + b_ref[...]

    return pl.pallas_call(
        body, name=name, grid=(c // tc,),
        in_specs=[pl.BlockSpec((t, tc), lambda j: (0, j + ob)), pl.BlockSpec((width, tc), lambda j: (0, j)),
                  pl.BlockSpec((1, tc), lambda j: (0, j))],
        out_specs=pl.BlockSpec((t, tc), lambda j: (0, j)), out_shape=_sds((t, c), f32),
        compiler_params=_cparams(("parallel",)))(p, conv_w, conv_b)


def _col_conv_bwd(p, col_off, conv_w, dxc, width, tc, name):
    t = p.shape[0]
    c = conv_w.shape[1]
    ob = col_off // tc

    def body(x_ref, w_ref, d_ref, dx_ref, dw_ref, db_ref):
        d = d_ref[...]
        db_ref[...] = jnp.sum(d, axis=0, keepdims=True)
        dx_ref[...] = _conv_bwd(x_ref[...], d, w_ref, dw_ref, width).astype(bf16)

    col = pl.BlockSpec((t, tc), lambda j: (0, j))
    return pl.pallas_call(
        body, name=name, grid=(c // tc,),
        in_specs=[pl.BlockSpec((t, tc), lambda j: (0, j + ob)), pl.BlockSpec((width, tc), lambda j: (0, j)), col],
        out_specs=[col, pl.BlockSpec((width, tc), lambda j: (0, j)), pl.BlockSpec((1, tc), lambda j: (0, j))],
        out_shape=[_sds((t, c), bf16), _sds((width, c), f32), _sds((1, c), f32)],
        compiler_params=_cparams(("parallel",)))(p, conv_w, dxc)


def _lru_fwd(p, xc, wa, ba, wx, bx, lam):
    t = p.shape[0]
    bw = LRU_BLOCK

    def body(y_ref, xc_ref, wa_ref, ba_ref, wx_ref, bx_ref, lam_ref, out_ref, hs_ref, a_ref):
        xc_v = xc_ref[...]
        xb = xc_v.astype(bf16)
        ra = _dot(xb, wa_ref[0]) + ba_ref[...]
        ia = _dot(xb, wx_ref[0]) + bx_ref[...]
        a, u = _lru_gates(xc_v, ra, ia, lam_ref[...])
        a_ref[...] = a
        hs = _lin_scan(a, u)
        hs_ref[...] = hs
        out_ref[...] = (hs * _gelu(y_ref[...])).astype(bf16)

    col = pl.BlockSpec((t, bw), lambda h: (0, h))
    vec = pl.BlockSpec((1, bw), lambda h: (0, h))
    mat = pl.BlockSpec((1, bw, bw), lambda h: (h, 0, 0))
    return pl.pallas_call(
        body, name="lru_fwd", grid=(HEADS,), in_specs=[col, col, mat, vec, mat, vec, vec], out_specs=[col, col, col],
        out_shape=[_sds((t, LRU_WIDTH), bf16), _sds((t, LRU_WIDTH), f32), _sds((t, LRU_WIDTH), f32)],
        compiler_params=_cparams(("parallel",)))(p, xc, wa, ba, wx, bx, lam)


def _lru_bwd_scan(p, a, hs, dout):
    t = p.shape[0]
    bw = LRU_BLOCK

    def body(y_ref, a_ref, hs_ref, do_ref, dy_ref, da_ref, du_ref):
        hs_v = hs_ref[...]
        do = do_ref[...]
        gate, vjp = jax.vjp(_gelu, y_ref[...])
        dy_ref[...] = vjp(do * hs_v)[0].astype(bf16)
        g = _rev_scan(_shift_up(a_ref[...], 1), do * gate)
        du_ref[...] = g
        da_ref[...] = g * _shift_down(hs_v, 1)

    col = pl.BlockSpec((t, bw), lambda h: (0, h))
    return pl.pallas_call(
        body, name="lru_bwd_scan", grid=(HEADS,), in_specs=[col, col, col, col], out_specs=[col, col, col],
        out_shape=[_sds((t, LRU_WIDTH), bf16), _sds((t, LRU_WIDTH), f32), _sds((t, LRU_WIDTH), f32)],
        compiler_params=_cparams(("parallel",)))(p, a, hs, dout)


def _lru_bwd_gates(xc, da, du, wa, ba, wx, bx, lam):
    t = xc.shape[0]
    bw = LRU_BLOCK
    tr = _tile(t, (512, 256, 128))

    def body(xc_ref, da_ref, du_ref, wa_ref, ba_ref, wx_ref, bx_ref, lam_ref,
             dxc_ref, dwa_ref, dwx_ref, dba_ref, dbx_ref, dlam_ref):
        xc_v = xc_ref[...]
        xb = xc_v.astype(bf16)
        ra = _dot(xb, wa_ref[0]) + ba_ref[...]
        ia = _dot(xb, wx_ref[0]) + bx_ref[...]
        _, vjp = jax.vjp(_lru_gates, xc_v, ra, ia, lam_ref[...])
        dxc, dra, dia, dlam = vjp((da_ref[...], du_ref[...]))
        drb, dib = dra.astype(bf16), dia.astype(bf16)
        dxc_ref[...] = dxc + _dot(drb, wa_ref[0], NT) + _dot(dib, wx_ref[0], NT)

        @pl.when(pl.program_id(1) == 0)
        def _():
            dwa_ref[...] = jnp.zeros_like(dwa_ref)
            dwx_ref[...] = jnp.zeros_like(dwx_ref)
            dba_ref[...] = jnp.zeros_like(dba_ref)
            dbx_ref[...] = jnp.zeros_like(dbx_ref)
            dlam_ref[...] = jnp.zeros_like(dlam_ref)

        dwa_ref[0] += _dot(xb, drb, TN)
        dwx_ref[0] += _dot(xb, dib, TN)
        dba_ref[...] += jnp.sum(dra, axis=0, keepdims=True)
        dbx_ref[...] += jnp.sum(dia, axis=0, keepdims=True)
        dlam_ref[...] += dlam

    tile = pl.BlockSpec((tr, bw), lambda h, i: (i, h))
    vec = pl.BlockSpec((1, bw), lambda h, i: (0, h))
    mat = pl.BlockSpec((1, bw, bw), lambda h, i: (h, 0, 0))
    return pl.pallas_call(
        body, name="lru_bwd_gates", grid=(HEADS, t // tr), in_specs=[tile, tile, tile, mat, vec, mat, vec, vec],
        out_specs=[tile, mat, mat, vec, vec, vec],
        out_shape=[_sds((t, LRU_WIDTH), f32), _sds((HEADS, bw, bw), f32), _sds((HEADS, bw, bw), f32),
                   _sds((1, LRU_WIDTH), f32), _sds((1, LRU_WIDTH), f32), _sds((1, LRU_WIDTH), f32)],
        compiler_params=_cparams(("parallel", "arbitrary")))(xc, da, du, wa, ba, wx, bx, lam)


def _gdn_pre_fn(cq, ck, cv, ba, alog, dtb, h):
    q, k, v = _silu(cq), _silu(ck), _silu(cv)
    q = q * lax.rsqrt(jnp.sum(q * q, axis=-1, keepdims=True) + EPS) * (HEAD_DIM ** -0.5)
    k = k * lax.rsqrt(jnp.sum(k * k, axis=-1, keepdims=True) + EPS)
    lane = lax.broadcasted_iota(jnp.int32, (1, HEAD_DIM), 1)
    mb = (lane == h).astype(f32)
    ma = (lane == HEADS + h).astype(f32)
    beta_raw = jnp.sum(ba * mb, axis=-1, keepdims=True)
    alpha = jnp.sum(ba * ma, axis=-1, keepdims=True)
    al = jnp.sum(alog * mb, axis=-1, keepdims=True)
    db = jnp.sum(dtb * mb, axis=-1, keepdims=True)
    beta = _sigmoid(beta_raw)
    g = -jnp.exp(al) * _softplus(alpha + db)
    return q, k, v, jnp.broadcast_to(beta, q.shape), jnp.broadcast_to(g, q.shape)


def _gdn_pre_fwd(p, conv_w, alog, dtb):
    t = p.shape[0]
    hd = HEAD_DIM

    def body(pq_ref, pk_ref, pv_ref, ba_ref, wq_ref, wk_ref, wv_ref, al_ref, dt_ref, q_ref, k_ref, v_ref, b_ref, g_ref):
        h = pl.program_id(0)
        cq = _conv_fwd(pq_ref[...], wq_ref, GDN_CONV)
        ck = _conv_fwd(pk_ref[...], wk_ref, GDN_CONV)
        cv = _conv_fwd(pv_ref[...], wv_ref, GDN_CONV)
        q, k, v, be, ge = _gdn_pre_fn(cq, ck, cv, ba_ref[...], al_ref[...], dt_ref[...], h)
        q_ref[...], k_ref[...], v_ref[...], b_ref[...], g_ref[...] = q, k, v, be, ge

    def pcol(off):
        return pl.BlockSpec((t, hd), lambda h: (0, h + off // hd))

    def wcol(off):
        return pl.BlockSpec((GDN_CONV, hd), lambda h: (0, h + off // hd))

    vec = pl.BlockSpec((1, hd), lambda h: (0, 0))
    out = pl.BlockSpec((t, hd), lambda h: (0, h))
    return pl.pallas_call(
        body, name="gdn_pre_fwd", grid=(HEADS,),
        in_specs=[pcol(OFF_Q), pcol(OFF_K), pcol(OFF_V), pl.BlockSpec((t, hd), lambda h: (0, OFF_BA // hd)),
                  wcol(0), wcol(GDN_WIDTH), wcol(2 * GDN_WIDTH), vec, vec],
        out_specs=[out] * 5, out_shape=[_sds((t, GDN_WIDTH), f32)] * 5,
        compiler_params=_cparams(("parallel",)))(p, p, p, p, conv_w, conv_w, conv_w, alog, dtb)


def _gdn_pre_bwd(p, conv_w, alog, dtb, dq, dk, dv, dbe, dge):
    t = p.shape[0]
    hd = HEAD_DIM

    def body(pq_ref, pk_ref, pv_ref, ba_ref, wq_ref, wk_ref, wv_ref, al_ref, dt_ref,
             dq_ref, dk_ref, dv_ref, dbe_ref, dge_ref,
             opq_ref, opk_ref, opv_ref, dba_ref, dwq_ref, dwk_ref, dwv_ref, dal_ref, ddt_ref):
        h = pl.program_id(0)
        pq, pk, pv = pq_ref[...], pk_ref[...], pv_ref[...]
        cq = _conv_fwd(pq, wq_ref, GDN_CONV)
        ck = _conv_fwd(pk, wk_ref, GDN_CONV)
        cv = _conv_fwd(pv, wv_ref, GDN_CONV)
        _, vjp = jax.vjp(functools.partial(_gdn_pre_fn, h=h), cq, ck, cv, ba_ref[...], al_ref[...], dt_ref[...])
        dcq, dck, dcv, dba, dal, ddt = vjp((dq_ref[...], dk_ref[...], dv_ref[...], dbe_ref[...], dge_ref[...]))
        opq_ref[...] = _conv_bwd(pq, dcq, wq_ref, dwq_ref, GDN_CONV).astype(bf16)
        opk_ref[...] = _conv_bwd(pk, dck, wk_ref, dwk_ref, GDN_CONV).astype(bf16)
        opv_ref[...] = _conv_bwd(pv, dcv, wv_ref, dwv_ref, GDN_CONV).astype(bf16)

        @pl.when(h == 0)
        def _():
            dba_ref[...] = jnp.zeros_like(dba_ref)
            dal_ref[...] = jnp.zeros_like(dal_ref)
            ddt_ref[...] = jnp.zeros_like(ddt_ref)

        dba_ref[...] += dba
        dal_ref[...] += dal
        ddt_ref[...] += ddt

    def pcol(off):
        return pl.BlockSpec((t, hd), lambda h: (0, h + off // hd))

    def wcol(off):
        return pl.BlockSpec((GDN_CONV, hd), lambda h: (0, h + off // hd))

    vec = pl.BlockSpec((1, hd), lambda h: (0, 0))
    col = pl.BlockSpec((t, hd), lambda h: (0, h))
    full = pl.BlockSpec((t, hd), lambda h: (0, 0))
    wout = pl.BlockSpec((GDN_CONV, hd), lambda h: (0, h))
    return pl.pallas_call(
        body, name="gdn_pre_bwd", grid=(HEADS,),
        in_specs=[pcol(OFF_Q), pcol(OFF_K), pcol(OFF_V), pl.BlockSpec((t, hd), lambda h: (0, OFF_BA // hd)),
                  wcol(0), wcol(GDN_WIDTH), wcol(2 * GDN_WIDTH), vec, vec, col, col, col, col, col],
        out_specs=[col, col, col, full, wout, wout, wout, vec, vec],
        out_shape=[_sds((t, GDN_WIDTH), bf16)] * 3 + [_sds((t, hd), f32)] + [_sds((GDN_CONV, GDN_WIDTH), f32)] * 3
        + [_sds((1, hd), f32)] * 2,
        compiler_params=_cparams(("arbitrary",)))(p, p, p, p, conv_w, conv_w, conv_w, alog, dtb, dq, dk, dv, dbe, dge)


def _tri_inverse(a):
    c = a.shape[0]
    r = lax.broadcasted_iota(jnp.int32, (c, c), 0)
    col = lax.broadcasted_iota(jnp.int32, (c, c), 1)
    m = -a
    inv = jnp.where(r == col, 1.0, 0.0) + m
    s = 2
    while s < c:
        m = _dot(m, m, precision=HI)
        inv = inv + _dot(inv, m, precision=HI)
        s *= 2
    return inv


def _gdn_chunk(s, q, k, v, ge, be):
    c = q.shape[0]
    r = lax.broadcasted_iota(jnp.int32, (c, c), 0)
    col = lax.broadcasted_iota(jnp.int32, (c, c), 1)
    causal = r >= col
    gc = _dot(causal.astype(f32), ge, precision=HI)
    gcc = gc[:, :c]
    gcr = gc.T[:c, :]
    decay = jnp.where(causal, jnp.exp(jnp.where(causal, gcc - gcr, 0.0)), 0.0)
    kb = k * be
    lower = jnp.where(r > col, _bdot(kb, k, NT) * decay, 0.0)
    tinv = _tri_inverse(lower)
    egc = jnp.exp(gc)
    u = _dot(tinv, v * be, precision=HI)
    w = _dot(tinv, kb * egc, precision=HI)
    attn = _bdot(q, k, NT) * decay
    gl = gc[c - 1:c, :]
    v_new = u - _bdot(w, s)
    o = _bdot(q * egc, s) + _bdot(attn, v_new)
    s_new = s * jnp.exp(gl) + _bdot(k * jnp.exp(gl - gc), v_new, TN)
    return o, s_new


def _gdn_core_fwd(q, k, v, ge, be):
    t = q.shape[0]
    c, hd = GDN_CHUNK, HEAD_DIM
    n = t // c

    def body(q_ref, k_ref, v_ref, g_ref, b_ref, o_ref, st_ref, s_ref):
        @pl.when(pl.program_id(0) == 0)
        def _():
            s_ref[...] = jnp.zeros_like(s_ref)

        for h in range(HEADS):
            hs = slice(h * hd, (h + 1) * hd)
            s = s_ref[h]
            st_ref[h, 0] = s
            o, s_new = _gdn_chunk(s, q_ref[:, hs], k_ref[:, hs], v_ref[:, hs], g_ref[:, hs], b_ref[:, hs])
            o_ref[:, hs] = o
            s_ref[h] = s_new

    tile = pl.BlockSpec((c, GDN_WIDTH), lambda i: (i, 0))
    return pl.pallas_call(
        body, name="gdn_core_fwd", grid=(n,), in_specs=[tile] * 5,
        out_specs=[tile, pl.BlockSpec((HEADS, 1, hd, hd), lambda i: (0, i, 0, 0))],
        out_shape=[_sds((t, GDN_WIDTH), f32), _sds((HEADS, n, hd, hd), f32)],
        scratch_shapes=[pltpu.VMEM((HEADS, hd, hd), f32)],
        compiler_params=_cparams(("arbitrary",)))(q, k, v, ge, be)


def _gdn_core_bwd(q, k, v, ge, be, states, do):
    t = q.shape[0]
    c, hd = GDN_CHUNK, HEAD_DIM
    n = t // c

    def body(q_ref, k_ref, v_ref, g_ref, b_ref, st_ref, do_ref, dq_ref, dk_ref, dv_ref, dg_ref, db_ref, ds_ref):
        @pl.when(pl.program_id(0) == 0)
        def _():
            ds_ref[...] = jnp.zeros_like(ds_ref)

        for h in range(HEADS):
            hs = slice(h * hd, (h + 1) * hd)
            _, vjp = jax.vjp(_gdn_chunk, st_ref[h, 0], q_ref[:, hs], k_ref[:, hs], v_ref[:, hs], g_ref[:, hs], b_ref[:, hs])
            ds, dq, dk, dv, dg, db = vjp((do_ref[:, hs], ds_ref[h]))
            ds_ref[h] = ds
            dq_ref[:, hs], dk_ref[:, hs], dv_ref[:, hs], dg_ref[:, hs], db_ref[:, hs] = dq, dk, dv, dg, db

    tile = pl.BlockSpec((c, GDN_WIDTH), lambda i: (n - 1 - i, 0))
    return pl.pallas_call(
        body, name="gdn_core_bwd", grid=(n,),
        in_specs=[tile] * 5 + [pl.BlockSpec((HEADS, 1, hd, hd), lambda i: (0, n - 1 - i, 0, 0)), tile],
        out_specs=[tile] * 5, out_shape=[_sds((t, GDN_WIDTH), f32)] * 5,
        scratch_shapes=[pltpu.VMEM((HEADS, hd, hd), f32)],
        compiler_params=_cparams(("arbitrary",)))(q, k, v, ge, be, states, do)


def _post_fn(o, z, gain):
    return _rms(o, gain) * _silu(z)


def _post_fwd(o, p, z_off, gain, name):
    t = o.shape[0]
    hd = HEAD_DIM

    def body(o_ref, z_ref, g_ref, y_ref):
        y_ref[...] = _post_fn(o_ref[...], z_ref[...], g_ref[...]).astype(bf16)

    col = pl.BlockSpec((t, hd), lambda h: (0, h))
    return pl.pallas_call(
        body, name=name, grid=(HEADS,),
        in_specs=[col, pl.BlockSpec((t, hd), lambda h: (0, h + z_off // hd)), pl.BlockSpec((1, hd), lambda h: (0, 0))],
        out_specs=col, out_shape=_sds((t, HEADS * hd), bf16), compiler_params=_cparams(("parallel",)))(o, p, gain)


def _post_bwd(o, p, z_off, gain, dmix, mix_off, name):
    t = o.shape[0]
    hd = HEAD_DIM

    def body(o_ref, z_ref, g_ref, dy_ref, do_ref, dz_ref, dg_ref):
        _, vjp = jax.vjp(_post_fn, o_ref[...], z_ref[...], g_ref[...])
        do, dz, dg = vjp(dy_ref[...])
        do_ref[...] = do
        dz_ref[...] = dz.astype(bf16)

        @pl.when(pl.program_id(0) == 0)
        def _():
            dg_ref[...] = jnp.zeros_like(dg_ref)

        dg_ref[...] += dg

    col = pl.BlockSpec((t, hd), lambda h: (0, h))
    vec = pl.BlockSpec((1, hd), lambda h: (0, 0))
    return pl.pallas_call(
        body, name=name, grid=(HEADS,),
        in_specs=[col, pl.BlockSpec((t, hd), lambda h: (0, h + z_off // hd)), vec,
                  pl.BlockSpec((t, hd), lambda h: (0, h + mix_off // hd))],
        out_specs=[col, col, vec], out_shape=[_sds((t, HEADS * hd), f32), _sds((t, HEADS * hd), bf16), _sds((1, hd), f32)],
        compiler_params=_cparams(("arbitrary",)))(o, p, gain, dmix)


def _hgrn_pre_fn(qb, fb, lbw, layer):
    l0, l1 = lbw[0:1, :], lbw[1:2, :]
    m = jnp.maximum(l0, l1)
    e0, e1 = jnp.exp(l0 - m), jnp.exp(l1 - m)
    p0, p1 = e0 / (e0 + e1), e1 / (e0 + e1)
    lb = (p0 - p0) if layer == 0 else ((p0 + p1) - p0)
    f = lb + (1.0 - lb) * _sigmoid(fb)
    return _silu(qb), 1.0 - f, jnp.log(jnp.maximum(f, F_FLOOR))


def _hgrn_pre_fwd(p, lbw, layer):
    t = p.shape[0]
    tc = HEAD_DIM

    def body(qb_ref, fb_ref, lb_ref, q_ref, k_ref, lf_ref):
        q_ref[...], k_ref[...], lf_ref[...] = _hgrn_pre_fn(qb_ref[...], fb_ref[...], lb_ref[...], layer)

    col = pl.BlockSpec((t, tc), lambda j: (0, j))
    return pl.pallas_call(
        body, name="hgrn_pre_fwd", grid=(GDN_WIDTH // tc,),
        in_specs=[pl.BlockSpec((t, tc), lambda j: (0, j + OFF_QB // tc)), pl.BlockSpec((t, tc), lambda j: (0, j + OFF_FB // tc)),
                  pl.BlockSpec((2, tc), lambda j: (0, j))],
        out_specs=[col] * 3, out_shape=[_sds((t, GDN_WIDTH), f32)] * 3,
        compiler_params=_cparams(("parallel",)))(p, p, lbw)


def _hgrn_pre_bwd(p, lbw, layer, dq, dk, dlf):
    t = p.shape[0]
    tc = HEAD_DIM

    def body(qb_ref, fb_ref, lb_ref, dq_ref, dk_ref, dlf_ref, dqb_ref, dfb_ref, dlb_ref):
        _, vjp = jax.vjp(functools.partial(_hgrn_pre_fn, layer=layer), qb_ref[...], fb_ref[...], lb_ref[...])
        dqb, dfb, dlb = vjp((dq_ref[...], dk_ref[...], dlf_ref[...]))
        dqb_ref[...] = dqb.astype(bf16)
        dfb_ref[...] = dfb.astype(bf16)
        dlb_ref[...] = dlb

    col = pl.BlockSpec((t, tc), lambda j: (0, j))
    lb = pl.BlockSpec((2, tc), lambda j: (0, j))
    return pl.pallas_call(
        body, name="hgrn_pre_bwd", grid=(GDN_WIDTH // tc,),
        in_specs=[pl.BlockSpec((t, tc), lambda j: (0, j + OFF_QB // tc)), pl.BlockSpec((t, tc), lambda j: (0, j + OFF_FB // tc)),
                  lb, col, col, col],
        out_specs=[col, col, lb], out_shape=[_sds((t, GDN_WIDTH), bf16)] * 2 + [_sds((2, GDN_WIDTH), f32)],
        compiler_params=_cparams(("parallel",)))(p, p, lbw, dq, dk, dlf)


def _hgrn_step(st, q, k, lf, v):
    c = HGRN_CHUNK
    r2 = lax.broadcasted_iota(jnp.int32, (c, c), 0)
    c2 = lax.broadcasted_iota(jnp.int32, (c, c), 1)
    tri = (r2 >= c2).astype(f32)
    i3 = lax.broadcasted_iota(jnp.int32, (c, c, HEAD_DIM), 0)
    j3 = lax.broadcasted_iota(jnp.int32, (c, c, HEAD_DIM), 1)
    mask = i3 >= j3
    outs = []
    for n in range(q.shape[0] // c):
        sl = slice(n * c, (n + 1) * c)
        qc, kc, lc, vc = q[sl], k[sl], lf[sl], v[sl]
        b = _dot(tri, lc, precision=HI)
        rel = jnp.where(mask, jnp.exp(jnp.where(mask, b[:, None, :] - b[None, :, :], 0.0)), 0.0)
        scores = jnp.sum(qc[:, None, :] * kc[None, :, :] * rel, axis=-1)
        bl = b[c - 1:c, :]
        o = _bdot(scores, vc) + _bdot(qc * jnp.exp(b), st, NT)
        st = st * jnp.exp(bl) + _bdot(vc, kc * jnp.exp(bl - b), TN)
        outs.append(o)
    return jnp.concatenate(outs, axis=0), st


def _hgrn_core_fwd(q, k, lf, p):
    t = q.shape[0]
    hd = HEAD_DIM
    rs = min(HGRN_STEP, t)
    n = t // rs

    def body(q_ref, k_ref, lf_ref, v_ref, o_ref, st_ref, s_ref):
        @pl.when(pl.program_id(0) == 0)
        def _():
            s_ref[...] = jnp.zeros_like(s_ref)

        for h in range(HEADS):
            hs = slice(h * hd, (h + 1) * hd)
            s = s_ref[h]
            st_ref[h, 0] = s
            o, s_new = _hgrn_step(s, q_ref[:, hs], k_ref[:, hs], lf_ref[:, hs], v_ref[:, hs])
            o_ref[:, hs] = o
            s_ref[h] = s_new

    tile = pl.BlockSpec((rs, GDN_WIDTH), lambda i: (i, 0))
    return pl.pallas_call(
        body, name="hgrn_core_fwd", grid=(n,),
        in_specs=[tile, tile, tile, pl.BlockSpec((rs, GDN_WIDTH), lambda i: (i, OFF_IB // GDN_WIDTH))],
        out_specs=[tile, pl.BlockSpec((HEADS, 1, hd, hd), lambda i: (0, i, 0, 0))],
        out_shape=[_sds((t, GDN_WIDTH), f32), _sds((HEADS, n, hd, hd), f32)],
        scratch_shapes=[pltpu.VMEM((HEADS, hd, hd), f32)],
        compiler_params=_cparams(("arbitrary",)))(q, k, lf, p)


def _hgrn_core_bwd(q, k, lf, p, states, do):
    t = q.shape[0]
    hd = HEAD_DIM
    rs = min(HGRN_STEP, t)
    n = t // rs

    def body(q_ref, k_ref, lf_ref, v_ref, st_ref, do_ref, dq_ref, dk_ref, dlf_ref, dv_ref, ds_ref):
        @pl.when(pl.program_id(0) == 0)
        def _():
            ds_ref[...] = jnp.zeros_like(ds_ref)

        for h in range(HEADS):
            hs = slice(h * hd, (h + 1) * hd)
            _, vjp = jax.vjp(_hgrn_step, st_ref[h, 0], q_ref[:, hs], k_ref[:, hs], lf_ref[:, hs], v_ref[:, hs])
            ds, dq, dk, dlf, dv = vjp((do_ref[:, hs], ds_ref[h]))
            ds_ref[h] = ds
            dq_ref[:, hs], dk_ref[:, hs], dlf_ref[:, hs] = dq, dk, dlf
            dv_ref[:, hs] = dv.astype(bf16)

    tile = pl.BlockSpec((rs, GDN_WIDTH), lambda i: (n - 1 - i, 0))
    return pl.pallas_call(
        body, name="hgrn_core_bwd", grid=(n,),
        in_specs=[tile, tile, tile, pl.BlockSpec((rs, GDN_WIDTH), lambda i: (n - 1 - i, OFF_IB // GDN_WIDTH)),
                  pl.BlockSpec((HEADS, 1, hd, hd), lambda i: (0, n - 1 - i, 0, 0)), tile],
        out_specs=[tile] * 4, out_shape=[_sds((t, GDN_WIDTH), f32)] * 3 + [_sds((t, GDN_WIDTH), bf16)],
        scratch_shapes=[pltpu.VMEM((HEADS, hd, hd), f32)],
        compiler_params=_cparams(("arbitrary",)))(q, k, lf, p, states, do)


def _row(v):
    return v.reshape(1, -1)


def _pad_lanes(v, n=HEAD_DIM):
    return jnp.pad(v.reshape(1, -1), ((0, 0), (0, n - v.shape[-1])))


def _ffn_fwd(x, w, l):
    h = _rms_fwd(x, _row(w['norm_ffn'][l]), "ffn_norm")
    u = _mm(h, w['ffn_w_up'][l], name="ffn_up")
    a = _ffn_act_fwd(u, w['ffn_conv_w'][l], _row(w['ffn_conv_b'][l]))
    y = _mm(a, w['ffn_w_down'][l], add=x, name="ffn_down")
    return y, (x, h, u)


def _ffn_bwd(saved, w, l, dy, dyb, grads):
    x, h, u = saved
    da = _mm(dyb, w['ffn_w_down'][l], tb=True, name="ffn_down_dx")
    a, dg, dv, dcw, dcb = _ffn_act_bwd(u, w['ffn_conv_w'][l], _row(w['ffn_conv_b'][l]), da)
    grads['ffn_w_down'][l] = _mm(a, dyb, ta=True, out_dtype=bf16, name="ffn_down_dw")
    du = jnp.concatenate([dg, dv], axis=1)
    grads['ffn_w_up'][l] = _mm(h, du, ta=True, out_dtype=bf16, name="ffn_up_dw")
    dh = _mm(du, w['ffn_w_up'][l], tb=True, name="ffn_up_dx")
    dx, dxb, dgain = _rms_bwd(x, _row(w['norm_ffn'][l]), dh, dy, "ffn_norm_bwd")
    grads['ffn_conv_w'][l] = dcw
    grads['ffn_conv_b'][l] = dcb[0]
    grads['norm_ffn'][l] = dgain[0]
    return dx, dxb


def _odd_fwd(x, w, l, j):
    h = _rms_fwd(x, _row(w['norm_mix'][l]), "mix_norm")
    p = _mm(h, w['c_w_in'][j], name="lru_in")
    xc = _col_conv_fwd(p, LRU_WIDTH, w['c_conv_w'][j], _row(w['c_conv_b'][j]), LRU_CONV, 256, "lru_conv_fwd")
    out, hs, a = _lru_fwd(p, xc, w['c_gate_a_w'][j], _row(w['c_gate_a_b'][j]), w['c_gate_x_w'][j],
                          _row(w['c_gate_x_b'][j]), _row(w['c_lambda'][j]))
    y = _mm(out, w['c_w_out'][j], add=x, name="lru_out")
    return y, (x, h, p, xc, out, hs, a)


def _odd_bwd(saved, w, l, j, dy, dyb, grads):
    x, h, p, xc, out, hs, a = saved
    dout = _mm(dyb, w['c_w_out'][j], tb=True, name="lru_out_dx")
    grads['c_w_out'][j] = _mm(out, dyb, ta=True, out_dtype=bf16, name="lru_out_dw")
    dyb_, da, du = _lru_bwd_scan(p, a, hs, dout)
    dxc, dwa, dwx, dba, dbx, dlam = _lru_bwd_gates(xc, da, du, w['c_gate_a_w'][j], _row(w['c_gate_a_b'][j]),
                                                   w['c_gate_x_w'][j], _row(w['c_gate_x_b'][j]), _row(w['c_lambda'][j]))
    dxb_, dcw, dcb = _col_conv_bwd(p, LRU_WIDTH, w['c_conv_w'][j], dxc, LRU_CONV, 256, "lru_conv_bwd")
    dp = jnp.concatenate([dyb_, dxb_], axis=1)
    grads['c_w_in'][j] = _mm(h, dp, ta=True, out_dtype=bf16, name="lru_in_dw")
    dh = _mm(dp, w['c_w_in'][j], tb=True, name="lru_in_dx")
    dx, dxb, dgain = _rms_bwd(x, _row(w['norm_mix'][l]), dh, dy, "mix_norm_bwd")
    grads['c_gate_a_w'][j], grads['c_gate_x_w'][j] = dwa, dwx
    grads['c_gate_a_b'][j], grads['c_gate_x_b'][j], grads['c_lambda'][j] = dba[0], dbx[0], dlam[0]
    grads['c_conv_w'][j], grads['c_conv_b'][j] = dcw, dcb[0]
    grads['norm_mix'][l] = dgain[0]
    return dx, dxb


def _even_fwd(x, w, l, j):
    h = _rms_fwd(x, _row(w['norm_mix'][l]), "mix_norm")
    p = _mm(h, w['ab_w_in'][j], name="ab_in")
    alog, dtb = _pad_lanes(w['gdn_a_log'][j]), _pad_lanes(w['gdn_dt_bias'][j])
    q, k, v, be, ge = _gdn_pre_fwd(p, w['gdn_conv_w'][j], alog, dtb)
    oa, sa = _gdn_core_fwd(q, k, v, ge, be)
    ya = _post_fwd(oa, p, OFF_Z, _row(w['gdn_norm'][j]), "gdn_post_fwd")
    qq, kk, lf = _hgrn_pre_fwd(p, w['hgrn_lower_bounds'], j)
    ob, sb = _hgrn_core_fwd(qq, kk, lf, p)
    yb = _post_fwd(ob, p, OFF_GB, _row(w['hgrn_norm'][j]), "hgrn_post_fwd")
    mix = jnp.concatenate([ya, yb], axis=1)
    y = _mm(mix, w['ab_w_out'][j], add=x, name="ab_out")
    return y, (x, h, p, q, k, v, be, ge, oa, sa, qq, kk, lf, ob, sb, mix)


def _even_bwd(saved, w, l, j, dy, dyb, grads):
    x, h, p, q, k, v, be, ge, oa, sa, qq, kk, lf, ob, sb, mix = saved
    alog, dtb = _pad_lanes(w['gdn_a_log'][j]), _pad_lanes(w['gdn_dt_bias'][j])
    dmix = _mm(dyb, w['ab_w_out'][j], tb=True, name="ab_out_dx")
    grads['ab_w_out'][j] = _mm(mix, dyb, ta=True, out_dtype=bf16, name="ab_out_dw")
    doa, dz, dgn = _post_bwd(oa, p, OFF_Z, _row(w['gdn_norm'][j]), dmix, 0, "gdn_post_bwd")
    dob, dgb, dhn = _post_bwd(ob, p, OFF_GB, _row(w['hgrn_norm'][j]), dmix, GDN_WIDTH, "hgrn_post_bwd")
    dq, dk, dv, dge, dbe = _gdn_core_bwd(q, k, v, ge, be, sa, doa)
    dpq, dpk, dpv, dba, dwq, dwk, dwv, dal, ddt = _gdn_pre_bwd(p, w['gdn_conv_w'][j], alog, dtb, dq, dk, dv, dbe, dge)
    dqq, dkk, dlf, dib = _hgrn_core_bwd(qq, kk, lf, p, sb, dob)
    dqb, dfb, dlb = _hgrn_pre_bwd(p, w['hgrn_lower_bounds'], j, dqq, dkk, dlf)
    dp = jnp.concatenate([dpq, dpk, dpv, dz, dqb, dfb, dib, dgb, dba.astype(bf16)], axis=1)
    grads['ab_w_in'][j] = _mm(h, dp, ta=True, out_dtype=bf16, name="ab_in_dw")
    dh = _mm(dp, w['ab_w_in'][j], tb=True, name="ab_in_dx")
    dx, dxb, dgain = _rms_bwd(x, _row(w['norm_mix'][l]), dh, dy, "mix_norm_bwd")
    grads['gdn_conv_w'][j] = jnp.concatenate([dwq, dwk, dwv], axis=1)
    grads['gdn_a_log'][j], grads['gdn_dt_bias'][j] = dal[0, :HEADS], ddt[0, :HEADS]
    grads['gdn_norm'][j], grads['hgrn_norm'][j] = dgn[0], dhn[0]
    grads['hgrn_lower_bounds'].append(dlb)
    grads['norm_mix'][l] = dgain[0]
    return dx, dxb


def _ab_permute(w_in):
    n = w_in.shape[0]
    pad = jnp.zeros((n, D_MODEL, AB_PAD - AB_COLS), w_in.dtype)
    return jnp.concatenate([w_in[..., :2048], w_in[..., 2056:], w_in[..., 2048:2056], pad], axis=-1)


def _ab_unpermute(g):
    return jnp.concatenate([g[..., :2048], g[..., 4096:4104], g[..., 2048:4096]], axis=-1)


def _block_pad(a, axis, nblk, padded):
    axis = axis % a.ndim
    s = a.shape
    a = a.reshape(s[:axis] + (nblk, s[axis] // nblk) + s[axis + 1:])
    pad = [(0, 0)] * a.ndim
    pad[axis + 1] = (0, padded - s[axis] // nblk)
    return jnp.pad(a, pad).reshape(s[:axis] + (nblk * padded,) + s[axis + 1:])


def _block_unpad(a, axis, nblk, width):
    axis = axis % a.ndim
    s = a.shape
    a = a.reshape(s[:axis] + (nblk, s[axis] // nblk) + s[axis + 1:])
    a = lax.slice_in_dim(a, 0, width, axis=axis + 1)
    return a.reshape(s[:axis] + (nblk * width,) + s[axis + 1:])


def _kernel_layout(w):
    w = dict(w)
    w['ab_w_in'] = _ab_permute(w['ab_w_in'])
    w['ffn_w_up'] = _block_pad(w['ffn_w_up'], 2, N_DEV, FF_PAD)
    w['ffn_w_down'] = _block_pad(w['ffn_w_down'], 1, 4, FF_PAD)
    w['ffn_conv_w'] = _block_pad(w['ffn_conv_w'], 2, 4, FF_PAD)
    w['ffn_conv_b'] = _block_pad(w['ffn_conv_b'], 1, 4, FF_PAD)
    return w


def _natural_grads(g):
    g = dict(g)
    g['ab_w_in'] = _ab_unpermute(g['ab_w_in'])
    g['ffn_w_up'] = _block_unpad(g['ffn_w_up'], 2, N_DEV, FF_SHARD)
    g['ffn_w_down'] = _block_unpad(g['ffn_w_down'], 1, 4, FF_SHARD)
    g['ffn_conv_w'] = _block_unpad(g['ffn_conv_w'], 2, 4, FF_SHARD)
    g['ffn_conv_b'] = _block_unpad(g['ffn_conv_b'], 1, 4, FF_SHARD)
    return g


def _local_step(x, target, w):
    grads = {n: [None] * w[n].shape[0] for n in WEIGHTS if n not in ('norm_final', 'hgrn_lower_bounds')}
    grads['hgrn_lower_bounds'] = []
    saved = []
    for l in range(DEPTH):
        j = l // 2
        x, s_mix = (_even_fwd if l % 2 == 0 else _odd_fwd)(x, w, l, j)
        x, s_ffn = _ffn_fwd(x, w, l)
        saved.append((s_mix, s_ffn))
    loss, dx, dxb, dgf = _loss_head(x, _row(w['norm_final']), target)
    for l in reversed(range(DEPTH)):
        j = l // 2
        s_mix, s_ffn = saved[l]
        dx, dxb = _ffn_bwd(s_ffn, w, l, dx, dxb, grads)
        dx, dxb = (_even_bwd if l % 2 == 0 else _odd_bwd)(s_mix, w, l, j, dx, dxb, grads)
    out = {n: jnp.stack(g) for n, g in grads.items() if n != 'hgrn_lower_bounds'}
    out['hgrn_lower_bounds'] = grads['hgrn_lower_bounds'][0] + grads['hgrn_lower_bounds'][1]
    out['norm_final'] = dgf[0]
    return loss[0, 0], dx, out


def _position():
    return lax.axis_index("x"), lax.axis_index("y"), lax.axis_index("c")


BLOCK_LAYOUT = {
    'ab_w_in': ((2, D_MODEL, N_DEV * AB_SHARD_PAD), (2, D_MODEL, AB_SHARD_PAD)),
    'ab_w_out': ((2, N_DEV, 128, D_MODEL), (2, 128, D_MODEL)),
    'c_w_in': ((2, D_MODEL, 2 * LRU_WIDTH), (2, D_MODEL, 256)),
    'c_w_out': ((2, N_DEV, 128, D_MODEL), (2, 128, D_MODEL)),
    'c_gate_a_w': ((2, HEADS, N_DEV, 32, LRU_BLOCK), (2, HEADS, 32, LRU_BLOCK)),
    'c_gate_x_w': ((2, HEADS, N_DEV, 32, LRU_BLOCK), (2, HEADS, 32, LRU_BLOCK)),
    'ffn_w_up': ((DEPTH, D_MODEL, N_DEV * FF_PAD), (DEPTH, D_MODEL, FF_PAD)),
    'ffn_w_down': ((DEPTH, 4, FF_PAD, D_MODEL), (DEPTH, FF_ROWS, D_MODEL)),
}


def _block_of(name, ref, p):
    d = 4 * p[0] + 2 * p[1] + p[2]
    if name == 'ab_w_in':
        return ref.at[:, :, pl.ds(pl.multiple_of(d * AB_SHARD_PAD, 128), AB_SHARD_PAD)]
    if name == 'c_w_in':
        return ref.at[:, :, pl.ds(pl.multiple_of(d * 256, 128), 256)]
    if name == 'ffn_w_up':
        return ref.at[:, :, pl.ds(pl.multiple_of(d * FF_PAD, 128), FF_PAD)]
    if name == 'ffn_w_down':
        return ref.at[:, 2 * p[0] + p[1], pl.ds(pl.multiple_of(p[2] * FF_ROWS, 16), FF_ROWS), :]
    if name in ('c_gate_a_w', 'c_gate_x_w'):
        return ref.at[:, :, d]
    if name in ('ab_w_out', 'c_w_out'):
        return ref.at[:, d]
    return ref.at[d]


def _gather_weights(shards):
    names = list(shards)
    n = len(names)
    shapes = {nm: (BLOCK_LAYOUT[nm][0] if nm in BLOCK_LAYOUT else (N_DEV,) + shards[nm].shape) for nm in names}

    def body(*refs):
        ins = dict(zip(names, refs[:n]))
        outs = dict(zip(names, refs[n + 1:2 * n + 1]))
        send_sems, recv_sems, local_sems = refs[2 * n + 1:]
        x, y, c = _position()
        me, sibling = (x, y, c), (x, y, 1 - c)
        chips = [(1 - x, y), (x, 1 - y), (1 - x, 1 - y)]

        def copy(i, k, block, to, src=None):
            dst = _block_of(names[i], outs[names[i]], block)
            return pltpu.make_async_remote_copy(
                src_ref=dst if src is None else src, dst_ref=dst, send_sem=send_sems.at[7 * i + k],
                recv_sem=recv_sems.at[7 * i + k], device_id=to, device_id_type=MESH)

        local = [pltpu.make_async_copy(ins[nm], _block_of(nm, outs[nm], me), local_sems.at[i])
                 for i, nm in enumerate(names)]
        for cp in local:
            cp.start()
        first = []
        for i, nm in enumerate(names):
            first.append(copy(i, 0, me, sibling, src=ins[nm]))
            first += [copy(i, 1 + j, me, (*chip, c), src=ins[nm]) for j, chip in enumerate(chips)]
        for cp in first:
            cp.start()
        passed = []
        for j, chip in enumerate(chips):
            for i in range(n):
                copy(i, 1 + j, (*chip, c), me).wait_recv()
                fwd = copy(i, 4 + j, (*chip, c), sibling)
                fwd.start()
                passed.append(fwd)
        for i in range(n):
            copy(i, 0, sibling, me).wait_recv()
        for j, chip in enumerate(chips):
            for i in range(n):
                copy(i, 4 + j, (*chip, 1 - c), me).wait_recv()
        for cp in first + passed:
            cp.wait_send()
        for cp in local:
            cp.wait()

    any_spec = pl.BlockSpec(memory_space=pl.ANY)
    down = names.index('ffn_w_down')
    zeros = jnp.zeros(shapes['ffn_w_down'], shards['ffn_w_down'].dtype)
    outs = pl.pallas_call(
        body, name="gather_weights", out_shape=[_sds(shapes[nm], shards[nm].dtype) for nm in names],
        in_specs=[any_spec] * (n + 1), out_specs=[any_spec] * n, input_output_aliases={n: down},
        scratch_shapes=[pltpu.SemaphoreType.DMA((7 * n,)), pltpu.SemaphoreType.DMA((7 * n,)),
                        pltpu.SemaphoreType.DMA((n,))],
    )(*[shards[nm] for nm in names], zeros)
    return dict(zip(names, outs))


def _exchange_grads(fulls, rep):
    own, pair, rep_pair = _pair_exchange(fulls, rep)
    chip = {nm: _add_pair(own[nm], pair[nm], "chip_sum_" + nm) for nm in own}
    rep_chip = _add_pair(rep, rep_pair, "chip_sum_replicated")
    return _cross_exchange(chip, rep_chip)


def _pair_exchange(fulls, rep):
    names = list(fulls)
    n = len(names)
    shard_shape = {nm: (BLOCK_LAYOUT[nm][1] if nm in BLOCK_LAYOUT else fulls[nm].shape[1:]) for nm in names}

    def body(*refs):
        ins = dict(zip(names, refs[:n]))
        rep_ref = refs[n]
        own = dict(zip(names, refs[n + 1:2 * n + 1]))
        pair = dict(zip(names, refs[2 * n + 1:3 * n + 1]))
        rpair_ref = refs[3 * n + 1]
        send_sems, recv_sems, local_sems = refs[3 * n + 2:]
        x, y, c = _position()
        sibling = (x, y, 1 - c)
        local, remote = [], []
        for i, nm in enumerate(names):
            for q in range(4):
                local.append(pltpu.make_async_copy(_block_of(nm, ins[nm], (q >> 1, q & 1, c)), own[nm].at[q],
                                                   local_sems.at[4 * i + q]))
                remote.append(pltpu.make_async_remote_copy(
                    src_ref=_block_of(nm, ins[nm], (q >> 1, q & 1, 1 - c)), dst_ref=pair[nm].at[q],
                    send_sem=send_sems.at[4 * i + q], recv_sem=recv_sems.at[4 * i + q], device_id=sibling,
                    device_id_type=MESH))
        remote.append(pltpu.make_async_remote_copy(
            src_ref=rep_ref, dst_ref=rpair_ref, send_sem=send_sems.at[4 * n], recv_sem=recv_sems.at[4 * n],
            device_id=sibling, device_id_type=MESH))
        for cp in remote + local:
            cp.start()
        for cp in remote:
            cp.wait_recv()
        for cp in remote:
            cp.wait_send()
        for cp in local:
            cp.wait()

    any_spec = pl.BlockSpec(memory_space=pl.ANY)
    four = [_sds((4,) + tuple(shard_shape[nm]), fulls[nm].dtype) for nm in names]
    outs = pl.pallas_call(
        body, name="grad_pair_exchange", out_shape=four + four + [_sds(rep.shape, rep.dtype)],
        in_specs=[any_spec] * (n + 1), out_specs=[any_spec] * (2 * n + 1),
        scratch_shapes=[pltpu.SemaphoreType.DMA((4 * n + 1,)), pltpu.SemaphoreType.DMA((4 * n + 1,)),
                        pltpu.SemaphoreType.DMA((4 * n,))],
    )(*[fulls[nm] for nm in names], rep)
    return dict(zip(names, outs[:n])), dict(zip(names, outs[n:2 * n])), outs[2 * n]


def _add_pair(a, b, name):
    shp = a.shape
    r, c = int(np.prod(shp[:-1])), shp[-1]
    tr = _tile(r, (512, 256, 128, 64, 32, 16, 8))

    def body(a_ref, b_ref, o_ref):
        o_ref[...] = (a_ref[...].astype(f32) + b_ref[...].astype(f32)).astype(o_ref.dtype)

    tile = pl.BlockSpec((tr, c), lambda i: (i, 0))
    return pl.pallas_call(body, name=name, grid=(r // tr,), in_specs=[tile, tile], out_specs=tile,
                          out_shape=_sds((r, c), a.dtype), compiler_params=_cparams(("parallel",)))(
        a.reshape(r, c), b.reshape(r, c)).reshape(shp)


def _cross_exchange(chip, rep_chip):
    names = list(chip)
    n = len(names)

    def body(*refs):
        ins = dict(zip(names, refs[:n]))
        rep_ref = refs[n]
        outs = dict(zip(names, refs[n + 1:2 * n + 1]))
        rrep_ref = refs[2 * n + 1]
        send_sems, recv_sems, local_sems = refs[2 * n + 2:]
        x, y, c = _position()
        mine = 2 * x + y
        copies = []
        for k in range(1, 4):
            px, py = (1 - x if (k >> 1) & 1 else x), (1 - y if k & 1 else y)
            for i, nm in enumerate(names + ['']):
                src = rep_ref if i == n else ins[nm].at[2 * px + py]
                dst = (rrep_ref if i == n else outs[nm]).at[mine]
                copies.append(pltpu.make_async_remote_copy(
                    src_ref=src, dst_ref=dst, send_sem=send_sems.at[3 * i + k - 1], recv_sem=recv_sems.at[3 * i + k - 1],
                    device_id=(px, py, c), device_id_type=MESH))
        own = [pltpu.make_async_copy(ins[nm].at[mine], outs[nm].at[mine], local_sems.at[i]) for i, nm in enumerate(names)]
        own.append(pltpu.make_async_copy(rep_ref, rrep_ref.at[mine], local_sems.at[n]))
        for cp in own + copies:
            cp.start()
        for cp in copies:
            cp.wait_recv()
        for cp in copies:
            cp.wait_send()
        for cp in own:
            cp.wait()

    any_spec = pl.BlockSpec(memory_space=pl.ANY)
    outs = pl.pallas_call(
        body, name="grad_cross_exchange",
        out_shape=[_sds(chip[nm].shape, chip[nm].dtype) for nm in names] + [_sds((4,) + rep_chip.shape, rep_chip.dtype)],
        in_specs=[any_spec] * (n + 1), out_specs=[any_spec] * (n + 1),
        scratch_shapes=[pltpu.SemaphoreType.DMA((3 * (n + 1),)), pltpu.SemaphoreType.DMA((3 * (n + 1),)),
                        pltpu.SemaphoreType.DMA((n + 1,))],
    )(*[chip[nm] for nm in names], rep_chip)
    return dict(zip(names, outs[:n])), outs[n]


def _sum_adamw(parts, w, m, v, name):
    r, l = w.shape
    lp = parts.shape[2]
    tr = _tile(r, (256, 128, 64, 32, 16, 8))
    c1 = 1.0 / (1.0 - ADAM_B1 ** ADAM_STEP)
    c2 = 1.0 / (1.0 - ADAM_B2 ** ADAM_STEP)

    def body(p_ref, w_ref, m_ref, v_ref, g_ref, d_ref, nm_ref, nv_ref):
        g = p_ref[0].astype(f32)
        for s in range(1, parts.shape[0]):
            g = g + p_ref[s].astype(f32)
        if lp != l:
            g = g[:, :l]
        m_new = ADAM_B1 * m_ref[...] + (1.0 - ADAM_B1) * g
        v_new = ADAM_B2 * v_ref[...] + (1.0 - ADAM_B2) * (g * g)
        g_ref[...] = g
        nm_ref[...] = m_new
        nv_ref[...] = v_new
        d_ref[...] = -ADAM_LR * ((m_new * c1) / (jnp.sqrt(v_new * c2) + ADAM_EPS) + ADAM_WD * w_ref[...])

    tile = pl.BlockSpec((tr, l), lambda i: (i, 0))
    return pl.pallas_call(
        body, name=name, grid=(r // tr,), in_specs=[pl.BlockSpec((parts.shape[0], tr, lp), lambda i: (0, i, 0)), tile, tile, tile],
        out_specs=[tile] * 4, out_shape=[_sds((r, l), f32)] * 4, compiler_params=_cparams(("parallel",)))(parts, w, m, v)


def _pack(arrs, lead=None):
    if lead is None:
        flat = jnp.concatenate([a.reshape(-1).astype(f32) for a in arrs])
        n = flat.shape[0]
    else:
        flat = jnp.concatenate([a.reshape(lead, -1).astype(f32) for a in arrs], axis=1)
        n = flat.shape[1]
    tot = -(-n // 1024) * 1024
    if lead is None:
        return jnp.pad(flat, (0, tot - n)).reshape(tot // 128, 128)
    return jnp.pad(flat, ((0, 0), (0, tot - n))).reshape(lead, tot // 128, 128)


def _unpack(packed, shapes, lead=False):
    flat = packed.reshape(packed.shape[0], -1) if lead else packed.reshape(-1)
    out, off = [], 0
    for s in shapes:
        n = int(np.prod(s))
        out.append(flat[:, off:off + n].reshape((packed.shape[0],) + tuple(s)) if lead else flat[off:off + n].reshape(s))
        off += n
    return out


def _merge_shards(g, axis):
    g = jnp.moveaxis(g, 0, axis)
    s = g.shape
    return g.reshape(s[:axis] + (s[axis] * s[axis + 1],) + s[axis + 2:])


def _split_shards(full, axis):
    s = full.shape
    g = full.reshape(s[:axis] + (N_DEV, s[axis] // N_DEV) + s[axis + 1:])
    return jnp.moveaxis(g, axis, 0)


def kernel(x, norm_mix, norm_ffn, norm_final, ab_w_in, gdn_conv_w, gdn_a_log, gdn_dt_bias, gdn_norm, hgrn_lower_bounds, hgrn_norm, ab_w_out, c_w_in, c_conv_w, c_conv_b, c_gate_a_w, c_gate_a_b, c_gate_x_w, c_gate_x_b, c_lambda, c_w_out, ffn_w_up, ffn_conv_w, ffn_conv_b, ffn_w_down, loss_target, m_norm_mix, m_norm_ffn, m_norm_final, m_ab_w_in, m_gdn_conv_w, m_gdn_a_log, m_gdn_dt_bias, m_gdn_norm, m_hgrn_lower_bounds, m_hgrn_norm, m_ab_w_out, m_c_w_in, m_c_conv_w, m_c_conv_b, m_c_gate_a_w, m_c_gate_a_b, m_c_gate_x_w, m_c_gate_x_b, m_c_lambda, m_c_w_out, m_ffn_w_up, m_ffn_conv_w, m_ffn_conv_b, m_ffn_w_down, v_norm_mix, v_norm_ffn, v_norm_final, v_ab_w_in, v_gdn_conv_w, v_gdn_a_log, v_gdn_dt_bias, v_gdn_norm, v_hgrn_lower_bounds, v_hgrn_norm, v_ab_w_out, v_c_w_in, v_c_conv_w, v_c_conv_b, v_c_gate_a_w, v_c_gate_a_b, v_c_gate_x_w, v_c_gate_x_b, v_c_lambda, v_c_w_out, v_ffn_w_up, v_ffn_conv_w, v_ffn_conv_b, v_ffn_w_down):
    wl = dict(zip(WEIGHTS, (norm_mix, norm_ffn, norm_final, ab_w_in, gdn_conv_w, gdn_a_log, gdn_dt_bias, gdn_norm, hgrn_lower_bounds, hgrn_norm, ab_w_out, c_w_in, c_conv_w, c_conv_b, c_gate_a_w, c_gate_a_b, c_gate_x_w, c_gate_x_b, c_lambda, c_w_out, ffn_w_up, ffn_conv_w, ffn_conv_b, ffn_w_down)))
    ml = dict(zip(WEIGHTS, (m_norm_mix, m_norm_ffn, m_norm_final, m_ab_w_in, m_gdn_conv_w, m_gdn_a_log, m_gdn_dt_bias, m_gdn_norm, m_hgrn_lower_bounds, m_hgrn_norm, m_ab_w_out, m_c_w_in, m_c_conv_w, m_c_conv_b, m_c_gate_a_w, m_c_gate_a_b, m_c_gate_x_w, m_c_gate_x_b, m_c_lambda, m_c_w_out, m_ffn_w_up, m_ffn_conv_w, m_ffn_conv_b, m_ffn_w_down)))
    vl = dict(zip(WEIGHTS, (v_norm_mix, v_norm_ffn, v_norm_final, v_ab_w_in, v_gdn_conv_w, v_gdn_a_log, v_gdn_dt_bias, v_gdn_norm, v_hgrn_lower_bounds, v_hgrn_norm, v_ab_w_out, v_c_w_in, v_c_conv_w, v_c_conv_b, v_c_gate_a_w, v_c_gate_a_b, v_c_gate_x_w, v_c_gate_x_b, v_c_lambda, v_c_w_out, v_ffn_w_up, v_ffn_conv_w, v_ffn_conv_b, v_ffn_w_down)))

    big = [n for n in SHARDED if n in MATMUL_WEIGHTS]
    vec = [n for n in SHARDED if n not in MATMUL_WEIGHTS]
    shards = {n: wl[n].astype(bf16) for n in big}
    shards['ab_w_in'] = jnp.pad(shards['ab_w_in'], ((0, 0), (0, 0), (0, AB_SHARD_PAD - AB_SHARD)))
    shards['ffn_w_up'] = jnp.pad(shards['ffn_w_up'], ((0, 0), (0, 0), (0, FF_PAD - FF_SHARD)))
    shards['vec'] = _pack([wl[n] for n in vec])
    got = _gather_weights(shards)
    full = {n: wl[n] for n in REPLICATED}
    for n, a in zip(vec, _unpack(got['vec'], [wl[n].shape for n in vec], lead=True)):
        full[n] = _merge_shards(a, SHARD_AXIS[n])
    full['ffn_conv_w'] = _block_pad(full['ffn_conv_w'], 2, 4, FF_PAD)
    full['ffn_conv_b'] = _block_pad(full['ffn_conv_b'], 1, 4, FF_PAD)
    full['ab_w_in'] = _ab_permute(_block_unpad(got['ab_w_in'], 2, N_DEV, AB_SHARD))
    full['c_w_in'] = got['c_w_in']
    full['ffn_w_up'] = got['ffn_w_up']
    full['ffn_w_down'] = got['ffn_w_down'].reshape(DEPTH, D_FFP, D_MODEL)
    for n in ('ab_w_out', 'c_w_out'):
        full[n] = got[n].reshape(2, D_MODEL, D_MODEL)
    for n in ('c_gate_a_w', 'c_gate_x_w'):
        full[n] = got[n].reshape(2, HEADS, LRU_BLOCK, LRU_BLOCK)

    loss, dx, grads = _local_step(x[0], loss_target[0], full)

    grads['ffn_conv_w'] = _block_unpad(grads['ffn_conv_w'], 2, 4, FF_SHARD)
    grads['ffn_conv_b'] = _block_unpad(grads['ffn_conv_b'], 1, 4, FF_SHARD)
    fulls = {n: grads[n].astype(bf16).reshape(BLOCK_LAYOUT[n][0]) for n in big if n != 'ab_w_in'}
    fulls['ab_w_in'] = _block_pad(_ab_unpermute(grads['ab_w_in']), 2, N_DEV, AB_SHARD_PAD)
    fulls = {n: fulls[n] for n in big}
    fulls['vec'] = _pack([_split_shards(grads[n], SHARD_AXIS[n]) for n in vec], lead=N_DEV)
    recv, rrep = _exchange_grads(fulls, _pack([grads[n] for n in REPLICATED]))
    res = {}
    for n in big:
        shp = wl[n].shape
        r, c = int(np.prod(shp[:-1])), shp[-1]
        outs = _sum_adamw(recv[n].reshape(4, r, -1), wl[n].reshape(r, c), ml[n].reshape(r, c), vl[n].reshape(r, c),
                          "adamw_" + n)
        for kind, o in zip(("grad", "delta", "new_m", "new_v"), outs):
            res[kind, n] = o.reshape(shp)
    for names, parts, tag in ((vec, recv['vec'], "adamw_vectors"), (REPLICATED, rrep, "adamw_replicated")):
        outs = _sum_adamw(parts, _pack([wl[n] for n in names]), _pack([ml[n] for n in names]),
                          _pack([vl[n] for n in names]), tag)
        for kind, o in zip(("grad", "delta", "new_m", "new_v"), outs):
            for n, a in zip(names, _unpack(o, [wl[n].shape for n in names])):
                res[kind, n] = a

    loss = lax.psum(loss, ("x", "y", "c"))
    return (loss, dx[None], *[res[kind, n] for kind in ("grad", "delta", "new_m", "new_v") for n in WEIGHTS])
```

```python
import functools

import numpy as np
import jax
import jax.numpy as jnp
from jax import lax
from jax.experimental import pallas as pl
from jax.experimental.pallas import tpu as pltpu

f32 = jnp.float32
bf16 = jnp.bfloat16
HI = lax.Precision.HIGHEST
MESH = pl.DeviceIdType.MESH

N_DEV = 8
D_MODEL = 1024
DEPTH = 4
EPS = 1e-6
F_FLOOR = 1e-30
HEADS = 4
HEAD_DIM = 128
GDN_WIDTH = 512
GDN_CONV = 4
GDN_CHUNK = 64
HGRN_CHUNK = 16
HGRN_STEP = 128
MIX_WIDTH = 1024
AB_COLS = 4104
AB_PAD = 4224
LRU_WIDTH = 1024
LRU_BLOCK = 256
LRU_CONV = 4
RG_C = 8.0
D_FF = 2816
FF_SHARD = 704
FF_PAD = 768
D_FFP = 4 * FF_PAD
FF_ROWS = 352
AB_SHARD, AB_SHARD_PAD = 513, 640
FFN_CONV = 3
ADAM_LR, ADAM_B1, ADAM_B2, ADAM_EPS, ADAM_WD, ADAM_STEP = 0.001, 0.9, 0.999, 1e-08, 0.01, 10
VMEM_LIMIT = 56 * 1024 * 1024
PACK_LANES = 512
PACK_ROWS = 256

OFF_Q, OFF_K, OFF_V, OFF_Z, OFF_QB, OFF_FB, OFF_IB, OFF_GB, OFF_BA = 0, 512, 1024, 1536, 2048, 2560, 3072, 3584, 4096

WEIGHTS = ['norm_mix', 'norm_ffn', 'norm_final', 'ab_w_in', 'gdn_conv_w', 'gdn_a_log', 'gdn_dt_bias', 'gdn_norm',
           'hgrn_lower_bounds', 'hgrn_norm', 'ab_w_out', 'c_w_in', 'c_conv_w', 'c_conv_b', 'c_gate_a_w', 'c_gate_a_b',
           'c_gate_x_w', 'c_gate_x_b', 'c_lambda', 'c_w_out', 'ffn_w_up', 'ffn_conv_w', 'ffn_conv_b', 'ffn_w_down']
SHARD_AXIS = {'norm_mix': None, 'norm_ffn': None, 'norm_final': None, 'ab_w_in': 2, 'gdn_conv_w': 2, 'gdn_a_log': None,
              'gdn_dt_bias': None, 'gdn_norm': None, 'hgrn_lower_bounds': None, 'hgrn_norm': None, 'ab_w_out': 1,
              'c_w_in': 2, 'c_conv_w': 2, 'c_conv_b': 1, 'c_gate_a_w': 2, 'c_gate_a_b': 1, 'c_gate_x_w': 2,
              'c_gate_x_b': 1, 'c_lambda': 1, 'c_w_out': 1, 'ffn_w_up': 2, 'ffn_conv_w': 2, 'ffn_conv_b': None,
              'ffn_w_down': 1}
MATMUL_WEIGHTS = ('ab_w_in', 'ab_w_out', 'c_w_in', 'c_gate_a_w', 'c_gate_x_w', 'c_w_out', 'ffn_w_up', 'ffn_w_down')
SHARDED = [n for n in WEIGHTS if SHARD_AXIS[n] is not None]
REPLICATED = [n for n in WEIGHTS if SHARD_AXIS[n] is None]


def _tile(n, prefs=(512, 384, 256, 128)):
    for p in prefs:
        if n % p == 0:
            return p
    return n


def _cparams(sem=None):
    kw = dict(vmem_limit_bytes=VMEM_LIMIT)
    if sem is not None:
        kw['dimension_semantics'] = sem
    return pltpu.CompilerParams(**kw)


def _sds(shape, dtype):
    return jax.ShapeDtypeStruct(tuple(shape), dtype)


def _sigmoid(x):
    return 1.0 / (1.0 + jnp.exp(-x))


def _silu(x):
    return x * _sigmoid(x)


def _log1p(x):
    u = 1.0 + x
    return jnp.where(u == 1.0, x, jnp.log(u) * (x / jnp.where(u == 1.0, 1.0, u - 1.0)))


def _softplus(x):
    return jnp.maximum(x, 0.0) + _log1p(jnp.exp(-jnp.abs(x)))


def _expm1(x):
    small = jnp.abs(x) < 0.05
    xs = jnp.where(small, x, 0.0)
    series = xs * (1.0 + xs * (0.5 + xs * (1.0 / 6.0 + xs * (1.0 / 24.0 + xs * (1.0 / 120.0)))))
    return jnp.where(small, series, jnp.exp(x) - 1.0)


def _gelu(x):
    return 0.5 * x * (1.0 + jnp.tanh(0.7978845608028654 * (x + 0.044715 * x * x * x)))


def _rms(x, gain):
    return x * lax.rsqrt(jnp.mean(x * x, axis=-1, keepdims=True) + EPS) * gain


def _dot(a, b, dims=((1,), (0,)), precision=None):
    return lax.dot_general(a, b, (dims, ((), ())), precision=precision, preferred_element_type=f32)


def _bdot(a, b, dims=((1,), (0,))):
    return _dot(a.astype(bf16), b.astype(bf16), dims)


NT = ((1,), (1,))
TN = ((0,), (0,))


def _shift_down(x, k):
    if k == 0:
        return x
    row = lax.broadcasted_iota(jnp.int32, x.shape, 0)
    return jnp.where(row >= k, pltpu.roll(x, k, 0), 0.0)


def _shift_up(x, k, fill=0.0):
    if k == 0:
        return x
    n = x.shape[0]
    row = lax.broadcasted_iota(jnp.int32, x.shape, 0)
    return jnp.where(row < n - k, pltpu.roll(x, n - k, 0), fill)


def _conv_fwd(x, w_ref, width):
    acc = w_ref[width - 1:width, :] * x
    for k in range(width - 1):
        acc = acc + w_ref[k:k + 1, :] * _shift_down(x, width - 1 - k)
    return acc


def _conv_bwd(x, dout, w_ref, dw_ref, width):
    dx = w_ref[width - 1:width, :] * dout
    dw_ref[width - 1:width, :] = jnp.sum(dout * x, axis=0, keepdims=True)
    for k in range(width - 1):
        s = width - 1 - k
        dx = dx + w_ref[k:k + 1, :] * _shift_up(dout, s)
        dw_ref[k:k + 1, :] = jnp.sum(dout * _shift_down(x, s), axis=0, keepdims=True)
    return dx


def _mm(a, b, *, ta=False, tb=False, add=None, out_dtype=f32, name):
    m, k = (a.shape[1], a.shape[0]) if ta else a.shape
    n = b.shape[0] if tb else b.shape[1]
    tm, tn = _tile(m), _tile(n)
    dims = ((0 if ta else 1,), (1 if tb else 0,))

    def body(*refs):
        a_ref, b_ref = refs[0], refs[1]
        o_ref = refs[-1]
        r = _dot(a_ref[...], b_ref[...], dims)
        if add is not None:
            r = r + refs[2][...]
        o_ref[...] = r.astype(out_dtype)

    a_spec = pl.BlockSpec((k, tm), lambda j, i: (0, i)) if ta else pl.BlockSpec((tm, k), lambda j, i: (i, 0))
    b_spec = pl.BlockSpec((tn, k), lambda j, i: (j, 0)) if tb else pl.BlockSpec((k, tn), lambda j, i: (0, j))
    o_spec = pl.BlockSpec((tm, tn), lambda j, i: (i, j))
    ins, specs = [a, b], [a_spec, b_spec]
    if add is not None:
        ins.append(add)
        specs.append(o_spec)
    return pl.pallas_call(body, name=name, grid=(n // tn, m // tm), in_specs=specs, out_specs=o_spec,
                          out_shape=_sds((m, n), out_dtype), compiler_params=_cparams(("parallel", "parallel")))(*ins)


def _rms_fwd(x, gain, name):
    t, d = x.shape
    tr = _tile(t, (256, 128))

    def body(x_ref, g_ref, h_ref):
        h_ref[...] = _rms(x_ref[...], g_ref[...]).astype(bf16)

    return pl.pallas_call(body, name=name, grid=(t // tr,),
                          in_specs=[pl.BlockSpec((tr, d), lambda i: (i, 0)), pl.BlockSpec((1, d), lambda i: (0, 0))],
                          out_specs=pl.BlockSpec((tr, d), lambda i: (i, 0)), out_shape=_sds((t, d), bf16),
                          compiler_params=_cparams(("parallel",)))(x, gain)


def _rms_bwd(x, gain, dh, dres, name):
    t, d = x.shape
    tr = _tile(t, (256, 128))

    def body(x_ref, g_ref, dh_ref, dres_ref, dx_ref, dxb_ref, dg_ref):
        _, vjp = jax.vjp(_rms, x_ref[...], g_ref[...])
        dx, dg = vjp(dh_ref[...])
        dx = dx + dres_ref[...]
        dx_ref[...] = dx
        dxb_ref[...] = dx.astype(bf16)

        @pl.when(pl.program_id(0) == 0)
        def _():
            dg_ref[...] = jnp.zeros_like(dg_ref)

        dg_ref[...] += dg

    row = pl.BlockSpec((tr, d), lambda i: (i, 0))
    vec = pl.BlockSpec((1, d), lambda i: (0, 0))
    return pl.pallas_call(body, name=name, grid=(t // tr,), in_specs=[row, vec, row, row], out_specs=[row, row, vec],
                          out_shape=[_sds((t, d), f32), _sds((t, d), bf16), _sds((1, d), f32)],
                          compiler_params=_cparams(("arbitrary",)))(x, gain, dh, dres)


def _loss_head(x, gain, target):
    t, d = x.shape
    tr = _tile(t, (256, 128))

    def f(xv, g, tgt):
        err = _rms(xv, g) - tgt
        return 0.5 * jnp.sum(jnp.mean(err * err, axis=-1, keepdims=True), axis=0, keepdims=True)

    def body(x_ref, g_ref, t_ref, loss_ref, dx_ref, dxb_ref, dg_ref):
        loss, vjp = jax.vjp(lambda xv, g: f(xv, g, t_ref[...]), x_ref[...], g_ref[...])
        dx, dg = vjp(jnp.ones((1, 1), f32))
        dx_ref[...] = dx
        dxb_ref[...] = dx.astype(bf16)

        @pl.when(pl.program_id(0) == 0)
        def _():
            dg_ref[...] = jnp.zeros_like(dg_ref)
            loss_ref[...] = jnp.zeros_like(loss_ref)

        dg_ref[...] += dg
        loss_ref[...] += jnp.broadcast_to(loss, loss_ref.shape)

    row = pl.BlockSpec((tr, d), lambda i: (i, 0))
    vec = pl.BlockSpec((1, d), lambda i: (0, 0))
    one = pl.BlockSpec((8, 128), lambda i: (0, 0))
    return pl.pallas_call(body, name="loss_head", grid=(t // tr,), in_specs=[row, vec, row],
                          out_specs=[one, row, row, vec],
                          out_shape=[_sds((8, 128), f32), _sds((t, d), f32), _sds((t, d), bf16), _sds((1, d), f32)],
                          compiler_params=_cparams(("arbitrary",)))(x, gain, target)


def _ffn_act_fwd(u, conv_w, conv_b):
    t = u.shape[0]
    tc = FF_PAD // 2
    nb = D_FFP // tc

    def body(g_ref, v_ref, w_ref, b_ref, a_ref):
        gc = _conv_fwd(g_ref[...], w_ref, FFN_CONV) + b_ref[...]
        a_ref[...] = (_silu(gc) * v_ref[...]).astype(bf16)

    return pl.pallas_call(
        body, name="ffn_act_fwd", grid=(nb,),
        in_specs=[pl.BlockSpec((t, tc), lambda j: (0, j)), pl.BlockSpec((t, tc), lambda j: (0, j + nb)),
                  pl.BlockSpec((FFN_CONV, tc), lambda j: (0, j)), pl.BlockSpec((1, tc), lambda j: (0, j))],
        out_specs=pl.BlockSpec((t, tc), lambda j: (0, j)), out_shape=_sds((t, D_FFP), bf16),
        compiler_params=_cparams(("parallel",)))(u, u, conv_w, conv_b)


def _ffn_act_bwd(u, conv_w, conv_b, da):
    t = u.shape[0]
    tc = FF_PAD // 2
    nb = D_FFP // tc

    def act(gc, val):
        return _silu(gc) * val

    def body(g_ref, v_ref, w_ref, b_ref, da_ref, a_ref, dg_ref, dv_ref, dw_ref, db_ref):
        gp = g_ref[...]
        gc = _conv_fwd(gp, w_ref, FFN_CONV) + b_ref[...]
        a, vjp = jax.vjp(act, gc, v_ref[...])
        dgc, dval = vjp(da_ref[...])
        a_ref[...] = a.astype(bf16)
        dv_ref[...] = dval.astype(bf16)
        db_ref[...] = jnp.sum(dgc, axis=0, keepdims=True)
        dg_ref[...] = _conv_bwd(gp, dgc, w_ref, dw_ref, FFN_CONV).astype(bf16)

    col = pl.BlockSpec((t, tc), lambda j: (0, j))
    return pl.pallas_call(
        body, name="ffn_act_bwd", grid=(nb,),
        in_specs=[col, pl.BlockSpec((t, tc), lambda j: (0, j + nb)), pl.BlockSpec((FFN_CONV, tc), lambda j: (0, j)),
                  pl.BlockSpec((1, tc), lambda j: (0, j)), col],
        out_specs=[col, col, col, pl.BlockSpec((FFN_CONV, tc), lambda j: (0, j)), pl.BlockSpec((1, tc), lambda j: (0, j))],
        out_shape=[_sds((t, D_FFP), bf16), _sds((t, D_FFP), bf16), _sds((t, D_FFP), bf16), _sds((FFN_CONV, D_FFP), f32),
                   _sds((1, D_FFP), f32)],
        compiler_params=_cparams(("parallel",)))(u, u, conv_w, conv_b, da)


def _lru_gates(xc, ra, ia, lam):
    r = _sigmoid(ra)
    i = _sigmoid(ia)
    log_a = -RG_C * r * _softplus(-lam)
    a = jnp.exp(log_a)
    u = jnp.sqrt(jnp.maximum(-_expm1(2.0 * log_a), 0.0)) * (i * xc)
    return a, u


def _lin_scan(a, u):
    n = a.shape[0]
    row = lax.broadcasted_iota(jnp.int32, a.shape, 0)
    s = 1
    while s < n:
        keep = row >= s
        u = a * jnp.where(keep, pltpu.roll(u, s, 0), 0.0) + u
        a = a * jnp.where(keep, pltpu.roll(a, s, 0), 1.0)
        s *= 2
    return u


def _rev_scan(a_next, d):
    n = d.shape[0]
    row = lax.broadcasted_iota(jnp.int32, d.shape, 0)
    a = a_next
    s = 1
    while s < n:
        keep = row < n - s
        d = a * jnp.where(keep, pltpu.roll(d, n - s, 0), 0.0) + d
        a = a * jnp.where(keep, pltpu.roll(a, n - s, 0), 1.0)
        s *= 2
    return d


def _col_conv_fwd(p, col_off, conv_w, conv_b, width, tc, name):
    t = p.shape[0]
    c = conv_w.shape[1]
    ob = col_off // tc

    def body(x_ref, w_ref, b_ref, o_ref):
        o_ref[...] = _conv_fwd(x_ref[...], w_ref, width) + b_ref[...]

    return pl.pallas_call(
        body, name=name, grid=(c // tc,),
        in_specs=[pl.BlockSpec((t, tc), lambda j: (0, j + ob)), pl.BlockSpec((width, tc), lambda j: (0, j)),
                  pl.BlockSpec((1, tc), lambda j: (0, j))],
        out_specs=pl.BlockSpec((t, tc), lambda j: (0, j)), out_shape=_sds((t, c), f32),
        compiler_params=_cparams(("parallel",)))(p, conv_w, conv_b)


def _col_conv_bwd(p, col_off, conv_w, dxc, width, tc, name):
    t = p.shape[0]
    c = conv_w.shape[1]
    ob = col_off // tc

    def body(x_ref, w_ref, d_ref, dx_ref, dw_ref, db_ref):
        d = d_ref[...]
        db_ref[...] = jnp.sum(d, axis=0, keepdims=True)
        dx_ref[...] = _conv_bwd(x_ref[...], d, w_ref, dw_ref, width).astype(bf16)

    col = pl.BlockSpec((t, tc), lambda j: (0, j))
    return pl.pallas_call(
        body, name=name, grid=(c // tc,),
        in_specs=[pl.BlockSpec((t, tc), lambda j: (0, j + ob)), pl.BlockSpec((width, tc), lambda j: (0, j)), col],
        out_specs=[col, pl.BlockSpec((width, tc), lambda j: (0, j)), pl.BlockSpec((1, tc), lambda j: (0, j))],
        out_shape=[_sds((t, c), bf16), _sds((width, c), f32), _sds((1, c), f32)],
        compiler_params=_cparams(("parallel",)))(p, conv_w, dxc)


def _lru_fwd(p, xc, wa, ba, wx, bx, lam):
    t = p.shape[0]
    bw = LRU_BLOCK

    def body(y_ref, xc_ref, wa_ref, ba_ref, wx_ref, bx_ref, lam_ref, out_ref, hs_ref, a_ref):
        xc_v = xc_ref[...]
        xb = xc_v.astype(bf16)
        ra = _dot(xb, wa_ref[0]) + ba_ref[...]
        ia = _dot(xb, wx_ref[0]) + bx_ref[...]
        a, u = _lru_gates(xc_v, ra, ia, lam_ref[...])
        a_ref[...] = a
        hs = _lin_scan(a, u)
        hs_ref[...] = hs
        out_ref[...] = (hs * _gelu(y_ref[...])).astype(bf16)

    col = pl.BlockSpec((t, bw), lambda h: (0, h))
    vec = pl.BlockSpec((1, bw), lambda h: (0, h))
    mat = pl.BlockSpec((1, bw, bw), lambda h: (h, 0, 0))
    return pl.pallas_call(
        body, name="lru_fwd", grid=(HEADS,), in_specs=[col, col, mat, vec, mat, vec, vec], out_specs=[col, col, col],
        out_shape=[_sds((t, LRU_WIDTH), bf16), _sds((t, LRU_WIDTH), f32), _sds((t, LRU_WIDTH), f32)],
        compiler_params=_cparams(("parallel",)))(p, xc, wa, ba, wx, bx, lam)


def _lru_bwd_scan(p, a, hs, dout):
    t = p.shape[0]
    bw = LRU_BLOCK

    def body(y_ref, a_ref, hs_ref, do_ref, dy_ref, da_ref, du_ref):
        hs_v = hs_ref[...]
        do = do_ref[...]
        gate, vjp = jax.vjp(_gelu, y_ref[...])
        dy_ref[...] = vjp(do * hs_v)[0].astype(bf16)
        g = _rev_scan(_shift_up(a_ref[...], 1), do * gate)
        du_ref[...] = g
        da_ref[...] = g * _shift_down(hs_v, 1)

    col = pl.BlockSpec((t, bw), lambda h: (0, h))
    return pl.pallas_call(
        body, name="lru_bwd_scan", grid=(HEADS,), in_specs=[col, col, col, col], out_specs=[col, col, col],
        out_shape=[_sds((t, LRU_WIDTH), bf16), _sds((t, LRU_WIDTH), f32), _sds((t, LRU_WIDTH), f32)],
        compiler_params=_cparams(("parallel",)))(p, a, hs, dout)


def _lru_bwd_gates(xc, da, du, wa, ba, wx, bx, lam):
    t = xc.shape[0]
    bw = LRU_BLOCK
    tr = _tile(t, (512, 256, 128))

    def body(xc_ref, da_ref, du_ref, wa_ref, ba_ref, wx_ref, bx_ref, lam_ref,
             dxc_ref, dwa_ref, dwx_ref, dba_ref, dbx_ref, dlam_ref):
        xc_v = xc_ref[...]
        xb = xc_v.astype(bf16)
        ra = _dot(xb, wa_ref[0]) + ba_ref[...]
        ia = _dot(xb, wx_ref[0]) + bx_ref[...]
        _, vjp = jax.vjp(_lru_gates, xc_v, ra, ia, lam_ref[...])
        dxc, dra, dia, dlam = vjp((da_ref[...], du_ref[...]))
        drb, dib = dra.astype(bf16), dia.astype(bf16)
        dxc_ref[...] = dxc + _dot(drb, wa_ref[0], NT) + _dot(dib, wx_ref[0], NT)

        @pl.when(pl.program_id(1) == 0)
        def _():
            dwa_ref[...] = jnp.zeros_like(dwa_ref)
            dwx_ref[...] = jnp.zeros_like(dwx_ref)
            dba_ref[...] = jnp.zeros_like(dba_ref)
            dbx_ref[...] = jnp.zeros_like(dbx_ref)
            dlam_ref[...] = jnp.zeros_like(dlam_ref)

        dwa_ref[0] += _dot(xb, drb, TN)
        dwx_ref[0] += _dot(xb, dib, TN)
        dba_ref[...] += jnp.sum(dra, axis=0, keepdims=True)
        dbx_ref[...] += jnp.sum(dia, axis=0, keepdims=True)
        dlam_ref[...] += dlam

    tile = pl.BlockSpec((tr, bw), lambda h, i: (i, h))
    vec = pl.BlockSpec((1, bw), lambda h, i: (0, h))
    mat = pl.BlockSpec((1, bw, bw), lambda h, i: (h, 0, 0))
    return pl.pallas_call(
        body, name="lru_bwd_gates", grid=(HEADS, t // tr), in_specs=[tile, tile, tile, mat, vec, mat, vec, vec],
        out_specs=[tile, mat, mat, vec, vec, vec],
        out_shape=[_sds((t, LRU_WIDTH), f32), _sds((HEADS, bw, bw), f32), _sds((HEADS, bw, bw), f32),
                   _sds((1, LRU_WIDTH), f32), _sds((1, LRU_WIDTH), f32), _sds((1, LRU_WIDTH), f32)],
        compiler_params=_cparams(("parallel", "arbitrary")))(xc, da, du, wa, ba, wx, bx, lam)


def _gdn_pre_fn(cq, ck, cv, ba, alog, dtb, h):
    q, k, v = _silu(cq), _silu(ck), _silu(cv)
    q = q * lax.rsqrt(jnp.sum(q * q, axis=-1, keepdims=True) + EPS) * (HEAD_DIM ** -0.5)
    k = k * lax.rsqrt(jnp.sum(k * k, axis=-1, keepdims=True) + EPS)
    lane = lax.broadcasted_iota(jnp.int32, (1, HEAD_DIM), 1)
    mb = (lane == h).astype(f32)
    ma = (lane == HEADS + h).astype(f32)
    beta_raw = jnp.sum(ba * mb, axis=-1, keepdims=True)
    alpha = jnp.sum(ba * ma, axis=-1, keepdims=True)
    al = jnp.sum(alog * mb, axis=-1, keepdims=True)
    db = jnp.sum(dtb * mb, axis=-1, keepdims=True)
    beta = _sigmoid(beta_raw)
    g = -jnp.exp(al) * _softplus(alpha + db)
    return q, k, v, jnp.broadcast_to(beta, q.shape), jnp.broadcast_to(g, q.shape)


def _gdn_pre_fwd(p, conv_w, alog, dtb):
    t = p.shape[0]
    hd = HEAD_DIM

    def body(pq_ref, pk_ref, pv_ref, ba_ref, wq_ref, wk_ref, wv_ref, al_ref, dt_ref, q_ref, k_ref, v_ref, b_ref, g_ref):
        h = pl.program_id(0)
        cq = _conv_fwd(pq_ref[...], wq_ref, GDN_CONV)
        ck = _conv_fwd(pk_ref[...], wk_ref, GDN_CONV)
        cv = _conv_fwd(pv_ref[...], wv_ref, GDN_CONV)
        q, k, v, be, ge = _gdn_pre_fn(cq, ck, cv, ba_ref[...], al_ref[...], dt_ref[...], h)
        q_ref[...], k_ref[...], v_ref[...], b_ref[...], g_ref[...] = q, k, v, be, ge

    def pcol(off):
        return pl.BlockSpec((t, hd), lambda h: (0, h + off // hd))

    def wcol(off):
        return pl.BlockSpec((GDN_CONV, hd), lambda h: (0, h + off // hd))

    vec = pl.BlockSpec((1, hd), lambda h: (0, 0))
    out = pl.BlockSpec((t, hd), lambda h: (0, h))
    return pl.pallas_call(
        body, name="gdn_pre_fwd", grid=(HEADS,),
        in_specs=[pcol(OFF_Q), pcol(OFF_K), pcol(OFF_V), pl.BlockSpec((t, hd), lambda h: (0, OFF_BA // hd)),
                  wcol(0), wcol(GDN_WIDTH), wcol(2 * GDN_WIDTH), vec, vec],
        out_specs=[out] * 5, out_shape=[_sds((t, GDN_WIDTH), f32)] * 5,
        compiler_params=_cparams(("parallel",)))(p, p, p, p, conv_w, conv_w, conv_w, alog, dtb)


def _gdn_pre_bwd(p, conv_w, alog, dtb, dq, dk, dv, dbe, dge):
    t = p.shape[0]
    hd = HEAD_DIM

    def body(pq_ref, pk_ref, pv_ref, ba_ref, wq_ref, wk_ref, wv_ref, al_ref, dt_ref,
             dq_ref, dk_ref, dv_ref, dbe_ref, dge_ref,
             opq_ref, opk_ref, opv_ref, dba_ref, dwq_ref, dwk_ref, dwv_ref, dal_ref, ddt_ref):
        h = pl.program_id(0)
        pq, pk, pv = pq_ref[...], pk_ref[...], pv_ref[...]
        cq = _conv_fwd(pq, wq_ref, GDN_CONV)
        ck = _conv_fwd(pk, wk_ref, GDN_CONV)
        cv = _conv_fwd(pv, wv_ref, GDN_CONV)
        _, vjp = jax.vjp(functools.partial(_gdn_pre_fn, h=h), cq, ck, cv, ba_ref[...], al_ref[...], dt_ref[...])
        dcq, dck, dcv, dba, dal, ddt = vjp((dq_ref[...], dk_ref[...], dv_ref[...], dbe_ref[...], dge_ref[...]))
        opq_ref[...] = _conv_bwd(pq, dcq, wq_ref, dwq_ref, GDN_CONV).astype(bf16)
        opk_ref[...] = _conv_bwd(pk, dck, wk_ref, dwk_ref, GDN_CONV).astype(bf16)
        opv_ref[...] = _conv_bwd(pv, dcv, wv_ref, dwv_ref, GDN_CONV).astype(bf16)

        @pl.when(h == 0)
        def _():
            dba_ref[...] = jnp.zeros_like(dba_ref)
            dal_ref[...] = jnp.zeros_like(dal_ref)
            ddt_ref[...] = jnp.zeros_like(ddt_ref)

        dba_ref[...] += dba
        dal_ref[...] += dal
        ddt_ref[...] += ddt

    def pcol(off):
        return pl.BlockSpec((t, hd), lambda h: (0, h + off // hd))

    def wcol(off):
        return pl.BlockSpec((GDN_CONV, hd), lambda h: (0, h + off // hd))

    vec = pl.BlockSpec((1, hd), lambda h: (0, 0))
    col = pl.BlockSpec((t, hd), lambda h: (0, h))
    full = pl.BlockSpec((t, hd), lambda h: (0, 0))
    wout = pl.BlockSpec((GDN_CONV, hd), lambda h: (0, h))
    return pl.pallas_call(
        body, name="gdn_pre_bwd", grid=(HEADS,),
        in_specs=[pcol(OFF_Q), pcol(OFF_K), pcol(OFF_V), pl.BlockSpec((t, hd), lambda h: (0, OFF_BA // hd)),
                  wcol(0), wcol(GDN_WIDTH), wcol(2 * GDN_WIDTH), vec, vec, col, col, col, col, col],
        out_specs=[col, col, col, full, wout, wout, wout, vec, vec],
        out_shape=[_sds((t, GDN_WIDTH), bf16)] * 3 + [_sds((t, hd), f32)] + [_sds((GDN_CONV, GDN_WIDTH), f32)] * 3
        + [_sds((1, hd), f32)] * 2,
        compiler_params=_cparams(("arbitrary",)))(p, p, p, p, conv_w, conv_w, conv_w, alog, dtb, dq, dk, dv, dbe, dge)


BNN = (((2,), (1,)), ((0,), (0,)))
BNT = (((2,), (2,)), ((0,), (0,)))
BTN = (((1,), (1,)), ((0,), (0,)))


def _hdot(a, b, dn=BNN, precision=None):
    return lax.dot_general(a, b, dn, precision=precision, preferred_element_type=f32)


def _hbdot(a, b, dn=BNN):
    return _hdot(a.astype(bf16), b.astype(bf16), dn)


def _tri_inverse(a):
    c = a.shape[-1]
    r = lax.broadcasted_iota(jnp.int32, (c, c), 0)
    col = lax.broadcasted_iota(jnp.int32, (c, c), 1)
    m = -a
    inv = jnp.where(r == col, 1.0, 0.0) + m
    s = 2
    while s < c:
        m = _hdot(m, m, precision=HI)
        inv = inv + _hdot(inv, m, precision=HI)
        s *= 2
    return inv


def _gdn_chunk(s, q, k, v, ge, be):
    nh, c, _ = q.shape
    r = lax.broadcasted_iota(jnp.int32, (c, c), 0)
    col = lax.broadcasted_iota(jnp.int32, (c, c), 1)
    causal = r >= col
    tri = jnp.broadcast_to(causal.astype(f32), (nh, c, c))
    gc = _hdot(tri, ge, precision=HI)
    gcc = gc[:, :, :c]
    gcr = jnp.swapaxes(gc, 1, 2)[:, :c, :]
    decay = jnp.where(causal, jnp.exp(jnp.where(causal, gcc - gcr, 0.0)), 0.0)
    kb = k * be
    lower = jnp.where(r > col, _hbdot(kb, k, BNT) * decay, 0.0)
    tinv = _tri_inverse(lower)
    egc = jnp.exp(gc)
    u = _hdot(tinv, v * be, precision=HI)
    w = _hdot(tinv, kb * egc, precision=HI)
    attn = _hbdot(q, k, BNT) * decay
    gl = gc[:, c - 1:c, :]
    v_new = u - _hbdot(w, s)
    o = _hbdot(q * egc, s) + _hbdot(attn, v_new)
    s_new = s * jnp.exp(gl) + _hbdot(k * jnp.exp(gl - gc), v_new, BTN)
    return o, s_new


def _heads_major(ref):
    return jnp.stack([ref[:, h * HEAD_DIM:(h + 1) * HEAD_DIM] for h in range(HEADS)])


def _gdn_core_fwd(q, k, v, ge, be):
    t = q.shape[0]
    c, hd = GDN_CHUNK, HEAD_DIM
    n = t // c

    def body(q_ref, k_ref, v_ref, g_ref, b_ref, o_ref, st_ref, s_ref):
        @pl.when(pl.program_id(0) == 0)
        def _():
            s_ref[...] = jnp.zeros_like(s_ref)

        s = s_ref[...]
        st_ref[:, 0] = s
        o, s_new = _gdn_chunk(s, *[_heads_major(r) for r in (q_ref, k_ref, v_ref, g_ref, b_ref)])
        for h in range(HEADS):
            o_ref[:, h * hd:(h + 1) * hd] = o[h]
        s_ref[...] = s_new

    tile = pl.BlockSpec((c, GDN_WIDTH), lambda i: (i, 0))
    return pl.pallas_call(
        body, name="gdn_core_fwd", grid=(n,), in_specs=[tile] * 5,
        out_specs=[tile, pl.BlockSpec((HEADS, 1, hd, hd), lambda i: (0, i, 0, 0))],
        out_shape=[_sds((t, GDN_WIDTH), f32), _sds((HEADS, n, hd, hd), f32)],
        scratch_shapes=[pltpu.VMEM((HEADS, hd, hd), f32)],
        compiler_params=_cparams(("arbitrary",)))(q, k, v, ge, be)


def _gdn_core_bwd(q, k, v, ge, be, states, do):
    t = q.shape[0]
    c, hd = GDN_CHUNK, HEAD_DIM
    n = t // c

    def body(q_ref, k_ref, v_ref, g_ref, b_ref, st_ref, do_ref, dq_ref, dk_ref, dv_ref, dg_ref, db_ref, ds_ref):
        @pl.when(pl.program_id(0) == 0)
        def _():
            ds_ref[...] = jnp.zeros_like(ds_ref)

        _, vjp = jax.vjp(_gdn_chunk, st_ref[:, 0], *[_heads_major(r) for r in (q_ref, k_ref, v_ref, g_ref, b_ref)])
        ds, *dins = vjp((_heads_major(do_ref), ds_ref[...]))
        ds_ref[...] = ds
        for d_ref, d in zip((dq_ref, dk_ref, dv_ref, dg_ref, db_ref), dins):
            for h in range(HEADS):
                d_ref[:, h * hd:(h + 1) * hd] = d[h]

    tile = pl.BlockSpec((c, GDN_WIDTH), lambda i: (n - 1 - i, 0))
    return pl.pallas_call(
        body, name="gdn_core_bwd", grid=(n,),
        in_specs=[tile] * 5 + [pl.BlockSpec((HEADS, 1, hd, hd), lambda i: (0, n - 1 - i, 0, 0)), tile],
        out_specs=[tile] * 5, out_shape=[_sds((t, GDN_WIDTH), f32)] * 5,
        scratch_shapes=[pltpu.VMEM((HEADS, hd, hd), f32)],
        compiler_params=_cparams(("arbitrary",)))(q, k, v, ge, be, states, do)


def _post_fn(o, z, gain):
    return _rms(o, gain) * _silu(z)


def _post_fwd(o, p, z_off, gain, name):
    t = o.shape[0]
    hd = HEAD_DIM

    def body(o_ref, z_ref, g_ref, y_ref):
        y_ref[...] = _post_fn(o_ref[...], z_ref[...], g_ref[...]).astype(bf16)

    col = pl.BlockSpec((t, hd), lambda h: (0, h))
    return pl.pallas_call(
        body, name=name, grid=(HEADS,),
        in_specs=[col, pl.BlockSpec((t, hd), lambda h: (0, h + z_off // hd)), pl.BlockSpec((1, hd), lambda h: (0, 0))],
        out_specs=col, out_shape=_sds((t, HEADS * hd), bf16), compiler_params=_cparams(("parallel",)))(o, p, gain)


def _post_bwd(o, p, z_off, gain, dmix, mix_off, name):
    t = o.shape[0]
    hd = HEAD_DIM

    def body(o_ref, z_ref, g_ref, dy_ref, do_ref, dz_ref, dg_ref):
        _, vjp = jax.vjp(_post_fn, o_ref[...], z_ref[...], g_ref[...])
        do, dz, dg = vjp(dy_ref[...])
        do_ref[...] = do
        dz_ref[...] = dz.astype(bf16)

        @pl.when(pl.program_id(0) == 0)
        def _():
            dg_ref[...] = jnp.zeros_like(dg_ref)

        dg_ref[...] += dg

    col = pl.BlockSpec((t, hd), lambda h: (0, h))
    vec = pl.BlockSpec((1, hd), lambda h: (0, 0))
    return pl.pallas_call(
        body, name=name, grid=(HEADS,),
        in_specs=[col, pl.BlockSpec((t, hd), lambda h: (0, h + z_off // hd)), vec,
                  pl.BlockSpec((t, hd), lambda h: (0, h + mix_off // hd))],
        out_specs=[col, col, vec], out_shape=[_sds((t, HEADS * hd), f32), _sds((t, HEADS * hd), bf16), _sds((1, hd), f32)],
        compiler_params=_cparams(("arbitrary",)))(o, p, gain, dmix)


def _hgrn_pre_fn(qb, fb, lbw, layer):
    l0, l1 = lbw[0:1, :], lbw[1:2, :]
    m = jnp.maximum(l0, l1)
    e0, e1 = jnp.exp(l0 - m), jnp.exp(l1 - m)
    p0, p1 = e0 / (e0 + e1), e1 / (e0 + e1)
    lb = (p0 - p0) if layer == 0 else ((p0 + p1) - p0)
    f = lb + (1.0 - lb) * _sigmoid(fb)
    return _silu(qb), 1.0 - f, jnp.log(jnp.maximum(f, F_FLOOR))


def _hgrn_pre_fwd(p, lbw, layer):
    t = p.shape[0]
    tc = HEAD_DIM

    def body(qb_ref, fb_ref, lb_ref, q_ref, k_ref, lf_ref):
        q_ref[...], k_ref[...], lf_ref[...] = _hgrn_pre_fn(qb_ref[...], fb_ref[...], lb_ref[...], layer)

    col = pl.BlockSpec((t, tc), lambda j: (0, j))
    return pl.pallas_call(
        body, name="hgrn_pre_fwd", grid=(GDN_WIDTH // tc,),
        in_specs=[pl.BlockSpec((t, tc), lambda j: (0, j + OFF_QB // tc)), pl.BlockSpec((t, tc), lambda j: (0, j + OFF_FB // tc)),
                  pl.BlockSpec((2, tc), lambda j: (0, j))],
        out_specs=[col] * 3, out_shape=[_sds((t, GDN_WIDTH), f32)] * 3,
        compiler_params=_cparams(("parallel",)))(p, p, lbw)


def _hgrn_pre_bwd(p, lbw, layer, dq, dk, dlf):
    t = p.shape[0]
    tc = HEAD_DIM

    def body(qb_ref, fb_ref, lb_ref, dq_ref, dk_ref, dlf_ref, dqb_ref, dfb_ref, dlb_ref):
        _, vjp = jax.vjp(functools.partial(_hgrn_pre_fn, layer=layer), qb_ref[...], fb_ref[...], lb_ref[...])
        dqb, dfb, dlb = vjp((dq_ref[...], dk_ref[...], dlf_ref[...]))
        dqb_ref[...] = dqb.astype(bf16)
        dfb_ref[...] = dfb.astype(bf16)
        dlb_ref[...] = dlb

    col = pl.BlockSpec((t, tc), lambda j: (0, j))
    lb = pl.BlockSpec((2, tc), lambda j: (0, j))
    return pl.pallas_call(
        body, name="hgrn_pre_bwd", grid=(GDN_WIDTH // tc,),
        in_specs=[pl.BlockSpec((t, tc), lambda j: (0, j + OFF_QB // tc)), pl.BlockSpec((t, tc), lambda j: (0, j + OFF_FB // tc)),
                  lb, col, col, col],
        out_specs=[col, col, lb], out_shape=[_sds((t, GDN_WIDTH), bf16)] * 2 + [_sds((2, GDN_WIDTH), f32)],
        compiler_params=_cparams(("parallel",)))(p, p, lbw, dq, dk, dlf)


def _hgrn_step(st, q, k, lf, v):
    c = HGRN_CHUNK
    r2 = lax.broadcasted_iota(jnp.int32, (c, c), 0)
    c2 = lax.broadcasted_iota(jnp.int32, (c, c), 1)
    tri = (r2 >= c2).astype(f32)
    i3 = lax.broadcasted_iota(jnp.int32, (c, c, HEAD_DIM), 0)
    j3 = lax.broadcasted_iota(jnp.int32, (c, c, HEAD_DIM), 1)
    mask = i3 >= j3
    outs = []
    for n in range(q.shape[0] // c):
        sl = slice(n * c, (n + 1) * c)
        qc, kc, lc, vc = q[sl], k[sl], lf[sl], v[sl]
        b = _dot(tri, lc, precision=HI)
        rel = jnp.where(mask, jnp.exp(jnp.where(mask, b[:, None, :] - b[None, :, :], 0.0)), 0.0)
        scores = jnp.sum(qc[:, None, :] * kc[None, :, :] * rel, axis=-1)
        bl = b[c - 1:c, :]
        o = _bdot(scores, vc) + _bdot(qc * jnp.exp(b), st, NT)
        st = st * jnp.exp(bl) + _bdot(vc, kc * jnp.exp(bl - b), TN)
        outs.append(o)
    return jnp.concatenate(outs, axis=0), st


def _hgrn_core_fwd(q, k, lf, p):
    t = q.shape[0]
    hd = HEAD_DIM
    rs = min(HGRN_STEP, t)
    n = t // rs

    def body(q_ref, k_ref, lf_ref, v_ref, o_ref, st_ref, s_ref):
        @pl.when(pl.program_id(0) == 0)
        def _():
            s_ref[...] = jnp.zeros_like(s_ref)

        for h in range(HEADS):
            hs = slice(h * hd, (h + 1) * hd)
            s = s_ref[h]
            st_ref[h, 0] = s
            o, s_new = _hgrn_step(s, q_ref[:, hs], k_ref[:, hs], lf_ref[:, hs], v_ref[:, hs])
            o_ref[:, hs] = o
            s_ref[h] = s_new

    tile = pl.BlockSpec((rs, GDN_WIDTH), lambda i: (i, 0))
    return pl.pallas_call(
        body, name="hgrn_core_fwd", grid=(n,),
        in_specs=[tile, tile, tile, pl.BlockSpec((rs, GDN_WIDTH), lambda i: (i, OFF_IB // GDN_WIDTH))],
        out_specs=[tile, pl.BlockSpec((HEADS, 1, hd, hd), lambda i: (0, i, 0, 0))],
        out_shape=[_sds((t, GDN_WIDTH), f32), _sds((HEADS, n, hd, hd), f32)],
        scratch_shapes=[pltpu.VMEM((HEADS, hd, hd), f32)],
        compiler_params=_cparams(("arbitrary",)))(q, k, lf, p)


def _hgrn_core_bwd(q, k, lf, p, states, do):
    t = q.shape[0]
    hd = HEAD_DIM
    rs = min(HGRN_STEP, t)
    n = t // rs

    def body(q_ref, k_ref, lf_ref, v_ref, st_ref, do_ref, dq_ref, dk_ref, dlf_ref, dv_ref, ds_ref):
        @pl.when(pl.program_id(0) == 0)
        def _():
            ds_ref[...] = jnp.zeros_like(ds_ref)

        for h in range(HEADS):
            hs = slice(h * hd, (h + 1) * hd)
            _, vjp = jax.vjp(_hgrn_step, st_ref[h, 0], q_ref[:, hs], k_ref[:, hs], lf_ref[:, hs], v_ref[:, hs])
            ds, dq, dk, dlf, dv = vjp((do_ref[:, hs], ds_ref[h]))
            ds_ref[h] = ds
            dq_ref[:, hs], dk_ref[:, hs], dlf_ref[:, hs] = dq, dk, dlf
            dv_ref[:, hs] = dv.astype(bf16)

    tile = pl.BlockSpec((rs, GDN_WIDTH), lambda i: (n - 1 - i, 0))
    return pl.pallas_call(
        body, name="hgrn_core_bwd", grid=(n,),
        in_specs=[tile, tile, tile, pl.BlockSpec((rs, GDN_WIDTH), lambda i: (n - 1 - i, OFF_IB // GDN_WIDTH)),
                  pl.BlockSpec((HEADS, 1, hd, hd), lambda i: (0, n - 1 - i, 0, 0)), tile],
        out_specs=[tile] * 4, out_shape=[_sds((t, GDN_WIDTH), f32)] * 3 + [_sds((t, GDN_WIDTH), bf16)],
        scratch_shapes=[pltpu.VMEM((HEADS, hd, hd), f32)],
        compiler_params=_cparams(("arbitrary",)))(q, k, lf, p, states, do)


def _row(v):
    return v.reshape(1, -1)


def _pad_lanes(v, n=HEAD_DIM):
    return jnp.pad(v.reshape(1, -1), ((0, 0), (0, n - v.shape[-1])))


def _ffn_fwd(x, w, l):
    h = _rms_fwd(x, _row(w['norm_ffn'][l]), "ffn_norm")
    u = _mm(h, w['ffn_w_up'][l], name="ffn_up")
    a = _ffn_act_fwd(u, w['ffn_conv_w'][l], _row(w['ffn_conv_b'][l]))
    y = _mm(a, w['ffn_w_down'][l], add=x, name="ffn_down")
    return y, (x, h, u)


def _ffn_bwd(saved, w, l, dy, dyb, grads):
    x, h, u = saved
    da = _mm(dyb, w['ffn_w_down'][l], tb=True, name="ffn_down_dx")
    a, dg, dv, dcw, dcb = _ffn_act_bwd(u, w['ffn_conv_w'][l], _row(w['ffn_conv_b'][l]), da)
    grads['ffn_w_down'][l] = _mm(a, dyb, ta=True, out_dtype=bf16, name="ffn_down_dw")
    du = jnp.concatenate([dg, dv], axis=1)
    grads['ffn_w_up'][l] = _mm(h, du, ta=True, out_dtype=bf16, name="ffn_up_dw")
    dh = _mm(du, w['ffn_w_up'][l], tb=True, name="ffn_up_dx")
    dx, dxb, dgain = _rms_bwd(x, _row(w['norm_ffn'][l]), dh, dy, "ffn_norm_bwd")
    grads['ffn_conv_w'][l] = dcw
    grads['ffn_conv_b'][l] = dcb[0]
    grads['norm_ffn'][l] = dgain[0]
    return dx, dxb


def _odd_fwd(x, w, l, j):
    h = _rms_fwd(x, _row(w['norm_mix'][l]), "mix_norm")
    p = _mm(h, w['c_w_in'][j], name="lru_in")
    xc = _col_conv_fwd(p, LRU_WIDTH, w['c_conv_w'][j], _row(w['c_conv_b'][j]), LRU_CONV, 256, "lru_conv_fwd")
    out, hs, a = _lru_fwd(p, xc, w['c_gate_a_w'][j], _row(w['c_gate_a_b'][j]), w['c_gate_x_w'][j],
                          _row(w['c_gate_x_b'][j]), _row(w['c_lambda'][j]))
    y = _mm(out, w['c_w_out'][j], add=x, name="lru_out")
    return y, (x, h, p, xc, out, hs, a)


def _odd_bwd(saved, w, l, j, dy, dyb, grads):
    x, h, p, xc, out, hs, a = saved
    dout = _mm(dyb, w['c_w_out'][j], tb=True, name="lru_out_dx")
    grads['c_w_out'][j] = _mm(out, dyb, ta=True, out_dtype=bf16, name="lru_out_dw")
    dyb_, da, du = _lru_bwd_scan(p, a, hs, dout)
    dxc, dwa, dwx, dba, dbx, dlam = _lru_bwd_gates(xc, da, du, w['c_gate_a_w'][j], _row(w['c_gate_a_b'][j]),
                                                   w['c_gate_x_w'][j], _row(w['c_gate_x_b'][j]), _row(w['c_lambda'][j]))
    dxb_, dcw, dcb = _col_conv_bwd(p, LRU_WIDTH, w['c_conv_w'][j], dxc, LRU_CONV, 256, "lru_conv_bwd")
    dp = jnp.concatenate([dyb_, dxb_], axis=1)
    grads['c_w_in'][j] = _mm(h, dp, ta=True, out_dtype=bf16, name="lru_in_dw")
    dh = _mm(dp, w['c_w_in'][j], tb=True, name="lru_in_dx")
    dx, dxb, dgain = _rms_bwd(x, _row(w['norm_mix'][l]), dh, dy, "mix_norm_bwd")
    grads['c_gate_a_w'][j], grads['c_gate_x_w'][j] = dwa, dwx
    grads['c_gate_a_b'][j], grads['c_gate_x_b'][j], grads['c_lambda'][j] = dba[0], dbx[0], dlam[0]
    grads['c_conv_w'][j], grads['c_conv_b'][j] = dcw, dcb[0]
    grads['norm_mix'][l] = dgain[0]
    return dx, dxb


def _even_fwd(x, w, l, j):
    h = _rms_fwd(x, _row(w['norm_mix'][l]), "mix_norm")
    p = _mm(h, w['ab_w_in'][j], name="ab_in")
    alog, dtb = _pad_lanes(w['gdn_a_log'][j]), _pad_lanes(w['gdn_dt_bias'][j])
    q, k, v, be, ge = _gdn_pre_fwd(p, w['gdn_conv_w'][j], alog, dtb)
    oa, sa = _gdn_core_fwd(q, k, v, ge, be)
    ya = _post_fwd(oa, p, OFF_Z, _row(w['gdn_norm'][j]), "gdn_post_fwd")
    qq, kk, lf = _hgrn_pre_fwd(p, w['hgrn_lower_bounds'], j)
    ob, sb = _hgrn_core_fwd(qq, kk, lf, p)
    yb = _post_fwd(ob, p, OFF_GB, _row(w['hgrn_norm'][j]), "hgrn_post_fwd")
    mix = jnp.concatenate([ya, yb], axis=1)
    y = _mm(mix, w['ab_w_out'][j], add=x, name="ab_out")
    return y, (x, h, p, q, k, v, be, ge, oa, sa, qq, kk, lf, ob, sb, mix)


def _even_bwd(saved, w, l, j, dy, dyb, grads):
    x, h, p, q, k, v, be, ge, oa, sa, qq, kk, lf, ob, sb, mix = saved
    alog, dtb = _pad_lanes(w['gdn_a_log'][j]), _pad_lanes(w['gdn_dt_bias'][j])
    dmix = _mm(dyb, w['ab_w_out'][j], tb=True, name="ab_out_dx")
    grads['ab_w_out'][j] = _mm(mix, dyb, ta=True, out_dtype=bf16, name="ab_out_dw")
    doa, dz, dgn = _post_bwd(oa, p, OFF_Z, _row(w['gdn_norm'][j]), dmix, 0, "gdn_post_bwd")
    dob, dgb, dhn = _post_bwd(ob, p, OFF_GB, _row(w['hgrn_norm'][j]), dmix, GDN_WIDTH, "hgrn_post_bwd")
    dq, dk, dv, dge, dbe = _gdn_core_bwd(q, k, v, ge, be, sa, doa)
    dpq, dpk, dpv, dba, dwq, dwk, dwv, dal, ddt = _gdn_pre_bwd(p, w['gdn_conv_w'][j], alog, dtb, dq, dk, dv, dbe, dge)
    dqq, dkk, dlf, dib = _hgrn_core_bwd(qq, kk, lf, p, sb, dob)
    dqb, dfb, dlb = _hgrn_pre_bwd(p, w['hgrn_lower_bounds'], j, dqq, dkk, dlf)
    dp = jnp.concatenate([dpq, dpk, dpv, dz, dqb, dfb, dib, dgb, dba.astype(bf16)], axis=1)
    grads['ab_w_in'][j] = _mm(h, dp, ta=True, out_dtype=bf16, name="ab_in_dw")
    dh = _mm(dp, w['ab_w_in'][j], tb=True, name="ab_in_dx")
    dx, dxb, dgain = _rms_bwd(x, _row(w['norm_mix'][l]), dh, dy, "mix_norm_bwd")
    grads['gdn_conv_w'][j] = jnp.concatenate([dwq, dwk, dwv], axis=1)
    grads['gdn_a_log'][j], grads['gdn_dt_bias'][j] = dal[0, :HEADS], ddt[0, :HEADS]
    grads['gdn_norm'][j], grads['hgrn_norm'][j] = dgn[0], dhn[0]
    grads['hgrn_lower_bounds'].append(dlb)
    grads['norm_mix'][l] = dgain[0]
    return dx, dxb


def _ab_permute(w_in):
    n = w_in.shape[0]
    pad = jnp.zeros((n, D_MODEL, AB_PAD - AB_COLS), w_in.dtype)
    return jnp.concatenate([w_in[..., :2048], w_in[..., 2056:], w_in[..., 2048:2056], pad], axis=-1)


def _ab_unpermute(g):
    return jnp.concatenate([g[..., :2048], g[..., 4096:4104], g[..., 2048:4096]], axis=-1)


def _block_pad(a, axis, nblk, padded):
    axis = axis % a.ndim
    s = a.shape
    a = a.reshape(s[:axis] + (nblk, s[axis] // nblk) + s[axis + 1:])
    pad = [(0, 0)] * a.ndim
    pad[axis + 1] = (0, padded - s[axis] // nblk)
    return jnp.pad(a, pad).reshape(s[:axis] + (nblk * padded,) + s[axis + 1:])


def _block_unpad(a, axis, nblk, width):
    axis = axis % a.ndim
    s = a.shape
    a = a.reshape(s[:axis] + (nblk, s[axis] // nblk) + s[axis + 1:])
    a = lax.slice_in_dim(a, 0, width, axis=axis + 1)
    return a.reshape(s[:axis] + (nblk * width,) + s[axis + 1:])


def _kernel_layout(w):
    w = dict(w)
    w['ab_w_in'] = _ab_permute(w['ab_w_in'])
    w['ffn_w_up'] = _block_pad(w['ffn_w_up'], 2, N_DEV, FF_PAD)
    w['ffn_w_down'] = _block_pad(w['ffn_w_down'], 1, 4, FF_PAD)
    w['ffn_conv_w'] = _block_pad(w['ffn_conv_w'], 2, 4, FF_PAD)
    w['ffn_conv_b'] = _block_pad(w['ffn_conv_b'], 1, 4, FF_PAD)
    return w


def _natural_grads(g):
    g = dict(g)
    g['ab_w_in'] = _ab_unpermute(g['ab_w_in'])
    g['ffn_w_up'] = _block_unpad(g['ffn_w_up'], 2, N_DEV, FF_SHARD)
    g['ffn_w_down'] = _block_unpad(g['ffn_w_down'], 1, 4, FF_SHARD)
    g['ffn_conv_w'] = _block_unpad(g['ffn_conv_w'], 2, 4, FF_SHARD)
    g['ffn_conv_b'] = _block_unpad(g['ffn_conv_b'], 1, 4, FF_SHARD)
    return g


def _local_step(x, target, w):
    grads = {n: [None] * w[n].shape[0] for n in WEIGHTS if n not in ('norm_final', 'hgrn_lower_bounds')}
    grads['hgrn_lower_bounds'] = []
    saved = []
    for l in range(DEPTH):
        j = l // 2
        x, s_mix = (_even_fwd if l % 2 == 0 else _odd_fwd)(x, w, l, j)
        x, s_ffn = _ffn_fwd(x, w, l)
        saved.append((s_mix, s_ffn))
    loss, dx, dxb, dgf = _loss_head(x, _row(w['norm_final']), target)
    for l in reversed(range(DEPTH)):
        j = l // 2
        s_mix, s_ffn = saved[l]
        dx, dxb = _ffn_bwd(s_ffn, w, l, dx, dxb, grads)
        dx, dxb = (_even_bwd if l % 2 == 0 else _odd_bwd)(s_mix, w, l, j, dx, dxb, grads)
    out = {n: jnp.stack(g) for n, g in grads.items() if n != 'hgrn_lower_bounds'}
    out['hgrn_lower_bounds'] = grads['hgrn_lower_bounds'][0] + grads['hgrn_lower_bounds'][1]
    out['norm_final'] = dgf[0]
    return loss[0, 0], dx, out


def _position():
    return lax.axis_index("x"), lax.axis_index("y"), lax.axis_index("c")


BLOCK_LAYOUT = {
    'ab_w_in': ((2, D_MODEL, N_DEV * AB_SHARD_PAD), (2, D_MODEL, AB_SHARD_PAD)),
    'ab_w_out': ((2, N_DEV, 128, D_MODEL), (2, 128, D_MODEL)),
    'c_w_in': ((2, D_MODEL, 2 * LRU_WIDTH), (2, D_MODEL, 256)),
    'c_w_out': ((2, N_DEV, 128, D_MODEL), (2, 128, D_MODEL)),
    'c_gate_a_w': ((2, HEADS, N_DEV, 32, LRU_BLOCK), (2, HEADS, 32, LRU_BLOCK)),
    'c_gate_x_w': ((2, HEADS, N_DEV, 32, LRU_BLOCK), (2, HEADS, 32, LRU_BLOCK)),
    'ffn_w_up': ((DEPTH, D_MODEL, N_DEV * FF_PAD), (DEPTH, D_MODEL, FF_PAD)),
    'ffn_w_down': ((DEPTH, 4, FF_PAD, D_MODEL), (DEPTH, FF_ROWS, D_MODEL)),
}


def _block_of(name, ref, p):
    d = 4 * p[0] + 2 * p[1] + p[2]
    if name == 'ab_w_in':
        return ref.at[:, :, pl.ds(pl.multiple_of(d * AB_SHARD_PAD, 128), AB_SHARD_PAD)]
    if name == 'c_w_in':
        return ref.at[:, :, pl.ds(pl.multiple_of(d * 256, 128), 256)]
    if name == 'ffn_w_up':
        return ref.at[:, :, pl.ds(pl.multiple_of(d * FF_PAD, 128), FF_PAD)]
    if name == 'ffn_w_down':
        return ref.at[:, 2 * p[0] + p[1], pl.ds(pl.multiple_of(p[2] * FF_ROWS, 16), FF_ROWS), :]
    if name in ('c_gate_a_w', 'c_gate_x_w'):
        return ref.at[:, :, d]
    if name in ('ab_w_out', 'c_w_out'):
        return ref.at[:, d]
    return ref.at[d]


def _gather_weights(shards):
    names = list(shards)
    n = len(names)
    shapes = {nm: (BLOCK_LAYOUT[nm][0] if nm in BLOCK_LAYOUT else (N_DEV,) + shards[nm].shape) for nm in names}

    def body(*refs):
        ins = dict(zip(names, refs[:n]))
        outs = dict(zip(names, refs[n + 1:2 * n + 1]))
        send_sems, recv_sems, local_sems = refs[2 * n + 1:]
        x, y, c = _position()
        me, sibling = (x, y, c), (x, y, 1 - c)
        chips = [(1 - x, y), (x, 1 - y), (1 - x, 1 - y)]

        def copy(i, k, block, to, src=None):
            dst = _block_of(names[i], outs[names[i]], block)
            return pltpu.make_async_remote_copy(
                src_ref=dst if src is None else src, dst_ref=dst, send_sem=send_sems.at[7 * i + k],
                recv_sem=recv_sems.at[7 * i + k], device_id=to, device_id_type=MESH)

        local = [pltpu.make_async_copy(ins[nm], _block_of(nm, outs[nm], me), local_sems.at[i])
                 for i, nm in enumerate(names)]
        for cp in local:
            cp.start()
        first = []
        for i, nm in enumerate(names):
            first.append(copy(i, 0, me, sibling, src=ins[nm]))
            first += [copy(i, 1 + j, me, (*chip, c), src=ins[nm]) for j, chip in enumerate(chips)]
        for cp in first:
            cp.start()
        passed = []
        for j, chip in enumerate(chips):
            for i in range(n):
                copy(i, 1 + j, (*chip, c), me).wait_recv()
                fwd = copy(i, 4 + j, (*chip, c), sibling)
                fwd.start()
                passed.append(fwd)
        for i in range(n):
            copy(i, 0, sibling, me).wait_recv()
        for j, chip in enumerate(chips):
            for i in range(n):
                copy(i, 4 + j, (*chip, 1 - c), me).wait_recv()
        for cp in first + passed:
            cp.wait_send()
        for cp in local:
            cp.wait()

    any_spec = pl.BlockSpec(memory_space=pl.ANY)
    down = names.index('ffn_w_down')
    zeros = jnp.zeros(shapes['ffn_w_down'], shards['ffn_w_down'].dtype)
    outs = pl.pallas_call(
        body, name="gather_weights", out_shape=[_sds(shapes[nm], shards[nm].dtype) for nm in names],
        in_specs=[any_spec] * (n + 1), out_specs=[any_spec] * n, input_output_aliases={n: down},
        scratch_shapes=[pltpu.SemaphoreType.DMA((7 * n,)), pltpu.SemaphoreType.DMA((7 * n,)),
                        pltpu.SemaphoreType.DMA((n,))],
    )(*[shards[nm] for nm in names], zeros)
    return dict(zip(names, outs))


def _exchange_grads(fulls, rep):
    cpos = lax.axis_index("c").astype(jnp.int32).reshape(1)
    pair, rep_pair = _pair_exchange(fulls, rep)
    chip = {nm: _chip_sum(nm, fulls[nm], pair[nm], cpos) for nm in fulls}
    rep_chip = _add_pair(rep, rep_pair, "chip_sum_replicated")
    cross, cross_rep = _cross_exchange(chip, rep_chip)
    return chip, rep_chip, cross, cross_rep


def _pair_exchange(fulls, rep):
    names = list(fulls)
    n = len(names)
    shard_shape = {nm: (BLOCK_LAYOUT[nm][1] if nm in BLOCK_LAYOUT else fulls[nm].shape[1:]) for nm in names}

    def body(*refs):
        ins = dict(zip(names, refs[:n]))
        rep_ref = refs[n]
        pair = dict(zip(names, refs[n + 1:2 * n + 1]))
        rpair_ref = refs[2 * n + 1]
        send_sems, recv_sems = refs[2 * n + 2:]
        x, y, c = _position()
        sibling = (x, y, 1 - c)
        remote = []
        for i, nm in enumerate(names):
            for q in range(4):
                remote.append(pltpu.make_async_remote_copy(
                    src_ref=_block_of(nm, ins[nm], (q >> 1, q & 1, 1 - c)), dst_ref=pair[nm].at[q],
                    send_sem=send_sems.at[4 * i + q], recv_sem=recv_sems.at[4 * i + q], device_id=sibling,
                    device_id_type=MESH))
        remote.append(pltpu.make_async_remote_copy(
            src_ref=rep_ref, dst_ref=rpair_ref, send_sem=send_sems.at[4 * n], recv_sem=recv_sems.at[4 * n],
            device_id=sibling, device_id_type=MESH))
        for cp in remote:
            cp.start()
        for cp in remote:
            cp.wait_recv()
        for cp in remote:
            cp.wait_send()

    any_spec = pl.BlockSpec(memory_space=pl.ANY)
    four = [_sds((4,) + tuple(shard_shape[nm]), fulls[nm].dtype) for nm in names]
    outs = pl.pallas_call(
        body, name="grad_pair_exchange", out_shape=four + [_sds(rep.shape, rep.dtype)],
        in_specs=[any_spec] * (n + 1), out_specs=[any_spec] * (n + 1),
        scratch_shapes=[pltpu.SemaphoreType.DMA((4 * n + 1,)), pltpu.SemaphoreType.DMA((4 * n + 1,))],
    )(*[fulls[nm] for nm in names], rep)
    return dict(zip(names, outs[:n])), outs[n]


def _chip_sum(name, full, pair, cpos):
    if name in ('ab_w_in', 'c_w_in', 'ffn_w_up'):
        width = BLOCK_LAYOUT[name][1][-1]
        rows = full.shape[0] * full.shape[1]
        tr = 512

        def body(c_ref, f_ref, p_ref, o_ref):
            o_ref[0] = (f_ref[...].astype(f32) + p_ref[0].astype(f32)).astype(o_ref.dtype)

        slot = pl.BlockSpec((1, tr, width), lambda q, i, c: (q, i, 0))
        out = pl.pallas_call(
            body, name="chip_sum_" + name, out_shape=_sds((4, rows, width), full.dtype),
            grid_spec=pltpu.PrefetchScalarGridSpec(
                num_scalar_prefetch=1, grid=(4, rows // tr),
                in_specs=[pl.BlockSpec((tr, width), lambda q, i, c: (i, 2 * q + c[0])), slot], out_specs=slot),
            compiler_params=_cparams(("parallel", "parallel")))(
            cpos, full.reshape(rows, N_DEV * width), pair.reshape(4, rows, width))
        return out.reshape(pair.shape)

    if name == 'ffn_w_down':
        f4, p4 = full, pair
        fspec = pl.BlockSpec((DEPTH, 1, FF_ROWS, D_MODEL), lambda q, c: (0, q, c[0], 0))
    else:
        shard = pair.shape[1:]
        lead = int(np.prod(shard[:-2]))
        f4 = full.reshape((lead, N_DEV) + shard[-2:])
        p4 = pair.reshape((4, lead) + shard[-2:])
        fspec = pl.BlockSpec((lead, 1) + shard[-2:], lambda q, c: (0, 2 * q + c[0], 0, 0))

    def body4(c_ref, f_ref, p_ref, o_ref):
        o_ref[0] = (f_ref[:, 0].astype(f32) + p_ref[0].astype(f32)).astype(o_ref.dtype)

    slot = pl.BlockSpec((1,) + p4.shape[1:], lambda q, c: (q, 0, 0, 0))
    out = pl.pallas_call(
        body4, name="chip_sum_" + name, out_shape=_sds(p4.shape, full.dtype),
        grid_spec=pltpu.PrefetchScalarGridSpec(num_scalar_prefetch=1, grid=(4,), in_specs=[fspec, slot], out_specs=slot),
        compiler_params=_cparams(("parallel",)))(cpos, f4, p4)
    return out.reshape(pair.shape)


def _add_pair(a, b, name):
    shp = a.shape
    r, c = int(np.prod(shp[:-1])), shp[-1]
    tr = _tile(r, (512, 256, 128, 64, 32, 16, 8))

    def body(a_ref, b_ref, o_ref):
        o_ref[...] = (a_ref[...].astype(f32) + b_ref[...].astype(f32)).astype(o_ref.dtype)

    tile = pl.BlockSpec((tr, c), lambda i: (i, 0))
    return pl.pallas_call(body, name=name, grid=(r // tr,), in_specs=[tile, tile], out_specs=tile,
                          out_shape=_sds((r, c), a.dtype), compiler_params=_cparams(("parallel",)))(
        a.reshape(r, c), b.reshape(r, c)).reshape(shp)


def _cross_exchange(chip, rep_chip):
    names = list(chip)
    n = len(names)

    def body(*refs):
        ins = dict(zip(names, refs[:n]))
        rep_ref = refs[n]
        outs = dict(zip(names, refs[2 * n + 2:3 * n + 2]))
        rrep_ref = refs[3 * n + 2]
        send_sems, recv_sems = refs[3 * n + 3:]
        x, y, c = _position()
        mine = 2 * x + y
        copies = []
        for k in range(1, 4):
            px, py = (1 - x if (k >> 1) & 1 else x), (1 - y if k & 1 else y)
            for i, nm in enumerate(names + ['']):
                src = rep_ref if i == n else ins[nm].at[2 * px + py]
                dst = (rrep_ref if i == n else outs[nm]).at[mine]
                copies.append(pltpu.make_async_remote_copy(
                    src_ref=src, dst_ref=dst, send_sem=send_sems.at[3 * i + k - 1], recv_sem=recv_sems.at[3 * i + k - 1],
                    device_id=(px, py, c), device_id_type=MESH))
        for cp in copies:
            cp.start()
        for cp in copies:
            cp.wait_recv()
        for cp in copies:
            cp.wait_send()

    any_spec = pl.BlockSpec(memory_space=pl.ANY)
    shapes = [_sds(chip[nm].shape, chip[nm].dtype) for nm in names] + [_sds((4,) + rep_chip.shape, rep_chip.dtype)]
    zeros = [jnp.zeros(s.shape, s.dtype) for s in shapes]
    outs = pl.pallas_call(
        body, name="grad_cross_exchange", out_shape=shapes,
        in_specs=[any_spec] * (2 * n + 2), out_specs=[any_spec] * (n + 1),
        input_output_aliases={n + 1 + i: i for i in range(n + 1)},
        scratch_shapes=[pltpu.SemaphoreType.DMA((3 * (n + 1),)), pltpu.SemaphoreType.DMA((3 * (n + 1),))],
    )(*[chip[nm] for nm in names], rep_chip, *zeros)
    return dict(zip(names, outs[:n])), outs[n]


def _sum_adamw(parts, own, mine, w, m, v, name):
    r, l = w.shape
    lp = parts.shape[2]
    tr = _tile(r, (256, 128, 64, 32, 16, 8))
    c1 = 1.0 / (1.0 - ADAM_B1 ** ADAM_STEP)
    c2 = 1.0 / (1.0 - ADAM_B2 ** ADAM_STEP)

    def body(mine_ref, p_ref, o_ref, w_ref, m_ref, v_ref, g_ref, d_ref, nm_ref, nv_ref):
        mine_v = (o_ref[0] if own.ndim == 3 else o_ref[...]).astype(f32)
        g = jnp.where(mine_ref[0] == 0, mine_v, p_ref[0].astype(f32))
        for s in range(1, parts.shape[0]):
            g = g + jnp.where(mine_ref[0] == s, mine_v, p_ref[s].astype(f32))
        if lp != l:
            g = g[:, :l]
        m_new = ADAM_B1 * m_ref[...] + (1.0 - ADAM_B1) * g
        v_new = ADAM_B2 * v_ref[...] + (1.0 - ADAM_B2) * (g * g)
        g_ref[...] = g
        nm_ref[...] = m_new
        nv_ref[...] = v_new
        d_ref[...] = -ADAM_LR * ((m_new * c1) / (jnp.sqrt(v_new * c2) + ADAM_EPS) + ADAM_WD * w_ref[...])

    tile = pl.BlockSpec((tr, l), lambda i, mn: (i, 0))
    own_spec = (pl.BlockSpec((1, tr, lp), lambda i, mn: (mn[0], i, 0)) if own.ndim == 3
                else pl.BlockSpec((tr, lp), lambda i, mn: (i, 0)))
    return pl.pallas_call(
        body, name=name, out_shape=[_sds((r, l), f32)] * 4,
        grid_spec=pltpu.PrefetchScalarGridSpec(
            num_scalar_prefetch=1, grid=(r // tr,),
            in_specs=[pl.BlockSpec((parts.shape[0], tr, lp), lambda i, mn: (0, i, 0)), own_spec, tile, tile, tile],
            out_specs=[tile] * 4),
        compiler_params=_cparams(("parallel",)))(mine, parts, own, w, m, v)


def _pack(arrs, lead=None):
    if lead is None:
        flat = jnp.concatenate([a.reshape(-1).astype(f32) for a in arrs])
        n = flat.shape[0]
    else:
        flat = jnp.concatenate([a.reshape(lead, -1).astype(f32) for a in arrs], axis=1)
        n = flat.shape[1]
    tot = -(-n // 1024) * 1024
    if lead is None:
        return jnp.pad(flat, (0, tot - n)).reshape(tot // 128, 128)
    return jnp.pad(flat, ((0, 0), (0, tot - n))).reshape(lead, tot // 128, 128)


def _unpack(packed, shapes, lead=False):
    flat = packed.reshape(packed.shape[0], -1) if lead else packed.reshape(-1)
    out, off = [], 0
    for s in shapes:
        n = int(np.prod(s))
        out.append(flat[:, off:off + n].reshape((packed.shape[0],) + tuple(s)) if lead else flat[off:off + n].reshape(s))
        off += n
    return out


def _merge_shards(g, axis):
    g = jnp.moveaxis(g, 0, axis)
    s = g.shape
    return g.reshape(s[:axis] + (s[axis] * s[axis + 1],) + s[axis + 2:])


def _split_shards(full, axis):
    s = full.shape
    g = full.reshape(s[:axis] + (N_DEV, s[axis] // N_DEV) + s[axis + 1:])
    return jnp.moveaxis(g, axis, 0)


def kernel(x, norm_mix, norm_ffn, norm_final, ab_w_in, gdn_conv_w, gdn_a_log, gdn_dt_bias, gdn_norm, hgrn_lower_bounds, hgrn_norm, ab_w_out, c_w_in, c_conv_w, c_conv_b, c_gate_a_w, c_gate_a_b, c_gate_x_w, c_gate_x_b, c_lambda, c_w_out, ffn_w_up, ffn_conv_w, ffn_conv_b, ffn_w_down, loss_target, m_norm_mix, m_norm_ffn, m_norm_final, m_ab_w_in, m_gdn_conv_w, m_gdn_a_log, m_gdn_dt_bias, m_gdn_norm, m_hgrn_lower_bounds, m_hgrn_norm, m_ab_w_out, m_c_w_in, m_c_conv_w, m_c_conv_b, m_c_gate_a_w, m_c_gate_a_b, m_c_gate_x_w, m_c_gate_x_b, m_c_lambda, m_c_w_out, m_ffn_w_up, m_ffn_conv_w, m_ffn_conv_b, m_ffn_w_down, v_norm_mix, v_norm_ffn, v_norm_final, v_ab_w_in, v_gdn_conv_w, v_gdn_a_log, v_gdn_dt_bias, v_gdn_norm, v_hgrn_lower_bounds, v_hgrn_norm, v_ab_w_out, v_c_w_in, v_c_conv_w, v_c_conv_b, v_c_gate_a_w, v_c_gate_a_b, v_c_gate_x_w, v_c_gate_x_b, v_c_lambda, v_c_w_out, v_ffn_w_up, v_ffn_conv_w, v_ffn_conv_b, v_ffn_w_down):
    wl = dict(zip(WEIGHTS, (norm_mix, norm_ffn, norm_final, ab_w_in, gdn_conv_w, gdn_a_log, gdn_dt_bias, gdn_norm, hgrn_lower_bounds, hgrn_norm, ab_w_out, c_w_in, c_conv_w, c_conv_b, c_gate_a_w, c_gate_a_b, c_gate_x_w, c_gate_x_b, c_lambda, c_w_out, ffn_w_up, ffn_conv_w, ffn_conv_b, ffn_w_down)))
    ml = dict(zip(WEIGHTS, (m_norm_mix, m_norm_ffn, m_norm_final, m_ab_w_in, m_gdn_conv_w, m_gdn_a_log, m_gdn_dt_bias, m_gdn_norm, m_hgrn_lower_bounds, m_hgrn_norm, m_ab_w_out, m_c_w_in, m_c_conv_w, m_c_conv_b, m_c_gate_a_w, m_c_gate_a_b, m_c_gate_x_w, m_c_gate_x_b, m_c_lambda, m_c_w_out, m_ffn_w_up, m_ffn_conv_w, m_ffn_conv_b, m_ffn_w_down)))
    vl = dict(zip(WEIGHTS, (v_norm_mix, v_norm_ffn, v_norm_final, v_ab_w_in, v_gdn_conv_w, v_gdn_a_log, v_gdn_dt_bias, v_gdn_norm, v_hgrn_lower_bounds, v_hgrn_norm, v_ab_w_out, v_c_w_in, v_c_conv_w, v_c_conv_b, v_c_gate_a_w, v_c_gate_a_b, v_c_gate_x_w, v_c_gate_x_b, v_c_lambda, v_c_w_out, v_ffn_w_up, v_ffn_conv_w, v_ffn_conv_b, v_ffn_w_down)))

    big = [n for n in SHARDED if n in MATMUL_WEIGHTS]
    vec = [n for n in SHARDED if n not in MATMUL_WEIGHTS]
    shards = {n: wl[n].astype(bf16) for n in big}
    shards['ab_w_in'] = jnp.pad(shards['ab_w_in'], ((0, 0), (0, 0), (0, AB_SHARD_PAD - AB_SHARD)))
    shards['ffn_w_up'] = jnp.pad(shards['ffn_w_up'], ((0, 0), (0, 0), (0, FF_PAD - FF_SHARD)))
    shards['vec'] = _pack([wl[n] for n in vec])
    got = _gather_weights(shards)
    full = {n: wl[n] for n in REPLICATED}
    for n, a in zip(vec, _unpack(got['vec'], [wl[n].shape for n in vec], lead=True)):
        full[n] = _merge_shards(a, SHARD_AXIS[n])
    full['ffn_conv_w'] = _block_pad(full['ffn_conv_w'], 2, 4, FF_PAD)
    full['ffn_conv_b'] = _block_pad(full['ffn_conv_b'], 1, 4, FF_PAD)
    full['ab_w_in'] = _ab_permute(_block_unpad(got['ab_w_in'], 2, N_DEV, AB_SHARD))
    full['c_w_in'] = got['c_w_in']
    full['ffn_w_up'] = got['ffn_w_up']
    full['ffn_w_down'] = got['ffn_w_down'].reshape(DEPTH, D_FFP, D_MODEL)
    for n in ('ab_w_out', 'c_w_out'):
        full[n] = got[n].reshape(2, D_MODEL, D_MODEL)
    for n in ('c_gate_a_w', 'c_gate_x_w'):
        full[n] = got[n].reshape(2, HEADS, LRU_BLOCK, LRU_BLOCK)

    loss, dx, grads = _local_step(x[0], loss_target[0], full)

    grads['ffn_conv_w'] = _block_unpad(grads['ffn_conv_w'], 2, 4, FF_SHARD)
    grads['ffn_conv_b'] = _block_unpad(grads['ffn_conv_b'], 1, 4, FF_SHARD)
    fulls = {n: grads[n].astype(bf16).reshape(BLOCK_LAYOUT[n][0]) for n in big if n != 'ab_w_in'}
    fulls['ab_w_in'] = _block_pad(_ab_unpermute(grads['ab_w_in']), 2, N_DEV, AB_SHARD_PAD)
    fulls = {n: fulls[n] for n in big}
    fulls['vec'] = _pack([_split_shards(grads[n], SHARD_AXIS[n]) for n in vec], lead=N_DEV)
    chip, rep_chip, recv, rrep = _exchange_grads(fulls, _pack([grads[n] for n in REPLICATED]))
    mine = (2 * lax.axis_index("x") + lax.axis_index("y")).astype(jnp.int32).reshape(1)
    res = {}
    for n in big:
        shp = wl[n].shape
        r, c = int(np.prod(shp[:-1])), shp[-1]
        outs = _sum_adamw(recv[n].reshape(4, r, -1), chip[n].reshape(4, r, -1), mine, wl[n].reshape(r, c),
                          ml[n].reshape(r, c), vl[n].reshape(r, c), "adamw_" + n)
        for kind, o in zip(("grad", "delta", "new_m", "new_v"), outs):
            res[kind, n] = o.reshape(shp)
    for names, parts, own, tag in ((vec, recv['vec'], chip['vec'], "adamw_vectors"),
                                   (REPLICATED, rrep, rep_chip, "adamw_replicated")):
        outs = _sum_adamw(parts, own, mine, _pack([wl[n] for n in names]), _pack([ml[n] for n in names]),
                          _pack([vl[n] for n in names]), tag)
        for kind, o in zip(("grad", "delta", "new_m", "new_v"), outs):
            for n, a in zip(names, _unpack(o, [wl[n].shape for n in names])):
                res[kind, n] = a

    loss = lax.psum(loss, ("x", "y", "c"))
    return (loss, dx[None], *[res[kind, n] for kind in ("grad", "delta", "new_m", "new_v") for n in WEIGHTS])
```

```python
import functools

import numpy as np
import jax
import jax.numpy as jnp
from jax import lax
from jax.experimental import pallas as pl
from jax.experimental.pallas import tpu as pltpu

f32 = jnp.float32
bf16 = jnp.bfloat16
HI = lax.Precision.HIGHEST
MESH = pl.DeviceIdType.MESH

N_DEV = 8
D_MODEL = 1024
DEPTH = 4
EPS = 1e-6
F_FLOOR = 1e-30
HEADS = 4
HEAD_DIM = 128
GDN_WIDTH = 512
GDN_CONV = 4
GDN_CHUNK = 64
HGRN_CHUNK = 16
HGRN_STEP = 128
MIX_WIDTH = 1024
AB_COLS = 4104
AB_PAD = 4224
LRU_WIDTH = 1024
LRU_BLOCK = 256
LRU_CONV = 4
RG_C = 8.0
D_FF = 2816
FF_SHARD = 704
FF_PAD = 768
D_FFP = 4 * FF_PAD
FF_ROWS = 352
AB_SHARD, AB_SHARD_PAD = 513, 640
FFN_CONV = 3
ADAM_LR, ADAM_B1, ADAM_B2, ADAM_EPS, ADAM_WD, ADAM_STEP = 0.001, 0.9, 0.999, 1e-08, 0.01, 10
VMEM_LIMIT = 56 * 1024 * 1024
PACK_LANES = 512
PACK_ROWS = 256

OFF_Q, OFF_K, OFF_V, OFF_Z, OFF_QB, OFF_FB, OFF_IB, OFF_GB, OFF_BA = 0, 512, 1024, 1536, 2048, 2560, 3072, 3584, 4096

WEIGHTS = ['norm_mix', 'norm_ffn', 'norm_final', 'ab_w_in', 'gdn_conv_w', 'gdn_a_log', 'gdn_dt_bias', 'gdn_norm',
           'hgrn_lower_bounds', 'hgrn_norm', 'ab_w_out', 'c_w_in', 'c_conv_w', 'c_conv_b', 'c_gate_a_w', 'c_gate_a_b',
           'c_gate_x_w', 'c_gate_x_b', 'c_lambda', 'c_w_out', 'ffn_w_up', 'ffn_conv_w', 'ffn_conv_b', 'ffn_w_down']
SHARD_AXIS = {'norm_mix': None, 'norm_ffn': None, 'norm_final': None, 'ab_w_in': 2, 'gdn_conv_w': 2, 'gdn_a_log': None,
              'gdn_dt_bias': None, 'gdn_norm': None, 'hgrn_lower_bounds': None, 'hgrn_norm': None, 'ab_w_out': 1,
              'c_w_in': 2, 'c_conv_w': 2, 'c_conv_b': 1, 'c_gate_a_w': 2, 'c_gate_a_b': 1, 'c_gate_x_w': 2,
              'c_gate_x_b': 1, 'c_lambda': 1, 'c_w_out': 1, 'ffn_w_up': 2, 'ffn_conv_w': 2, 'ffn_conv_b': None,
              'ffn_w_down': 1}
MATMUL_WEIGHTS = ('ab_w_in', 'ab_w_out', 'c_w_in', 'c_gate_a_w', 'c_gate_x_w', 'c_w_out', 'ffn_w_up', 'ffn_w_down')
SHARDED = [n for n in WEIGHTS if SHARD_AXIS[n] is not None]
REPLICATED = [n for n in WEIGHTS if SHARD_AXIS[n] is None]


def _tile(n, prefs=(512, 384, 256, 128)):
    for p in prefs:
        if n % p == 0:
            return p
    return n


def _cparams(sem=None):
    kw = dict(vmem_limit_bytes=VMEM_LIMIT)
    if sem is not None:
        kw['dimension_semantics'] = sem
    return pltpu.CompilerParams(**kw)


def _sds(shape, dtype):
    return jax.ShapeDtypeStruct(tuple(shape), dtype)


def _sigmoid(x):
    return 1.0 / (1.0 + jnp.exp(-x))


def _silu(x):
    return x * _sigmoid(x)


def _log1p(x):
    u = 1.0 + x
    return jnp.where(u == 1.0, x, jnp.log(u) * (x / jnp.where(u == 1.0, 1.0, u - 1.0)))


def _softplus(x):
    return jnp.maximum(x, 0.0) + _log1p(jnp.exp(-jnp.abs(x)))


def _expm1(x):
    small = jnp.abs(x) < 0.05
    xs = jnp.where(small, x, 0.0)
    series = xs * (1.0 + xs * (0.5 + xs * (1.0 / 6.0 + xs * (1.0 / 24.0 + xs * (1.0 / 120.0)))))
    return jnp.where(small, series, jnp.exp(x) - 1.0)


def _gelu(x):
    return 0.5 * x * (1.0 + jnp.tanh(0.7978845608028654 * (x + 0.044715 * x * x * x)))


def _rms(x, gain):
    return x * lax.rsqrt(jnp.mean(x * x, axis=-1, keepdims=True) + EPS) * gain


def _dot(a, b, dims=((1,), (0,)), precision=None):
    return lax.dot_general(a, b, (dims, ((), ())), precision=precision, preferred_element_type=f32)


def _bdot(a, b, dims=((1,), (0,))):
    return _dot(a.astype(bf16), b.astype(bf16), dims)


NT = ((1,), (1,))
TN = ((0,), (0,))


def _shift_down(x, k):
    if k == 0:
        return x
    row = lax.broadcasted_iota(jnp.int32, x.shape, 0)
    return jnp.where(row >= k, pltpu.roll(x, k, 0), 0.0)


def _shift_up(x, k, fill=0.0):
    if k == 0:
        return x
    n = x.shape[0]
    row = lax.broadcasted_iota(jnp.int32, x.shape, 0)
    return jnp.where(row < n - k, pltpu.roll(x, n - k, 0), fill)


def _conv_fwd(x, w_ref, width):
    acc = w_ref[width - 1:width, :] * x
    for k in range(width - 1):
        acc = acc + w_ref[k:k + 1, :] * _shift_down(x, width - 1 - k)
    return acc


def _conv_bwd(x, dout, w_ref, dw_ref, width):
    dx = w_ref[width - 1:width, :] * dout
    dw_ref[width - 1:width, :] = jnp.sum(dout * x, axis=0, keepdims=True)
    for k in range(width - 1):
        s = width - 1 - k
        dx = dx + w_ref[k:k + 1, :] * _shift_up(dout, s)
        dw_ref[k:k + 1, :] = jnp.sum(dout * _shift_down(x, s), axis=0, keepdims=True)
    return dx


MM_VMEM_BUDGET = 36 * 1024 * 1024
MM_MAX_TILE = 1024 * 1024


def _mm_tiles(m, n, k, out_bytes):
    best = None
    for tm in (1024, 512, 384, 256, 128):
        if m % tm:
            continue
        for tn in range(1536, 0, -128):
            if n % tn or tm * tn > MM_MAX_TILE:
                continue
            if 2 * (tm * k * 2 + k * tn * 2 + tm * tn * out_bytes) <= MM_VMEM_BUDGET and (best is None or tm * tn > best[0]):
                best = (tm * tn, tm, tn)
    return (best[1], best[2]) if best else (_tile(m), _tile(n))


def _mm(a, b, *, ta=False, tb=False, add=None, out_dtype=f32, name):
    m, k = (a.shape[1], a.shape[0]) if ta else a.shape
    n = b.shape[0] if tb else b.shape[1]
    tm, tn = _mm_tiles(m, n, k, jnp.dtype(out_dtype).itemsize + (4 if add is not None else 0))
    dims = ((0 if ta else 1,), (1 if tb else 0,))

    def body(*refs):
        a_ref, b_ref = refs[0], refs[1]
        o_ref = refs[-1]
        r = _dot(a_ref[...], b_ref[...], dims)
        if add is not None:
            r = r + refs[2][...]
        o_ref[...] = r.astype(out_dtype)

    a_spec = pl.BlockSpec((k, tm), lambda j, i: (0, i)) if ta else pl.BlockSpec((tm, k), lambda j, i: (i, 0))
    b_spec = pl.BlockSpec((tn, k), lambda j, i: (j, 0)) if tb else pl.BlockSpec((k, tn), lambda j, i: (0, j))
    o_spec = pl.BlockSpec((tm, tn), lambda j, i: (i, j))
    ins, specs = [a, b], [a_spec, b_spec]
    if add is not None:
        ins.append(add)
        specs.append(o_spec)
    return pl.pallas_call(body, name=name, grid=(n // tn, m // tm), in_specs=specs, out_specs=o_spec,
                          out_shape=_sds((m, n), out_dtype), compiler_params=_cparams(("parallel", "parallel")))(*ins)


def _rms_fwd(x, gain, name):
    t, d = x.shape
    tr = _tile(t, (256, 128))

    def body(x_ref, g_ref, h_ref):
        h_ref[...] = _rms(x_ref[...], g_ref[...]).astype(bf16)

    return pl.pallas_call(body, name=name, grid=(t // tr,),
                          in_specs=[pl.BlockSpec((tr, d), lambda i: (i, 0)), pl.BlockSpec((1, d), lambda i: (0, 0))],
                          out_specs=pl.BlockSpec((tr, d), lambda i: (i, 0)), out_shape=_sds((t, d), bf16),
                          compiler_params=_cparams(("parallel",)))(x, gain)


def _rms_bwd(x, gain, dh, dres, name):
    t, d = x.shape
    tr = _tile(t, (256, 128))

    def body(x_ref, g_ref, dh_ref, dres_ref, dx_ref, dxb_ref, dg_ref):
        _, vjp = jax.vjp(_rms, x_ref[...], g_ref[...])
        dx, dg = vjp(dh_ref[...])
        dx = dx + dres_ref[...]
        dx_ref[...] = dx
        dxb_ref[...] = dx.astype(bf16)

        @pl.when(pl.program_id(0) == 0)
        def _():
            dg_ref[...] = jnp.zeros_like(dg_ref)

        dg_ref[...] += dg

    row = pl.BlockSpec((tr, d), lambda i: (i, 0))
    vec = pl.BlockSpec((1, d), lambda i: (0, 0))
    return pl.pallas_call(body, name=name, grid=(t // tr,), in_specs=[row, vec, row, row], out_specs=[row, row, vec],
                          out_shape=[_sds((t, d), f32), _sds((t, d), bf16), _sds((1, d), f32)],
                          compiler_params=_cparams(("arbitrary",)))(x, gain, dh, dres)


def _loss_head(x, gain, target):
    t, d = x.shape
    tr = _tile(t, (256, 128))

    def f(xv, g, tgt):
        err = _rms(xv, g) - tgt
        return 0.5 * jnp.sum(jnp.mean(err * err, axis=-1, keepdims=True), axis=0, keepdims=True)

    def body(x_ref, g_ref, t_ref, loss_ref, dx_ref, dxb_ref, dg_ref):
        loss, vjp = jax.vjp(lambda xv, g: f(xv, g, t_ref[...]), x_ref[...], g_ref[...])
        dx, dg = vjp(jnp.ones((1, 1), f32))
        dx_ref[...] = dx
        dxb_ref[...] = dx.astype(bf16)

        @pl.when(pl.program_id(0) == 0)
        def _():
            dg_ref[...] = jnp.zeros_like(dg_ref)
            loss_ref[...] = jnp.zeros_like(loss_ref)

        dg_ref[...] += dg
        loss_ref[...] += jnp.broadcast_to(loss, loss_ref.shape)

    row = pl.BlockSpec((tr, d), lambda i: (i, 0))
    vec = pl.BlockSpec((1, d), lambda i: (0, 0))
    one = pl.BlockSpec((8, 128), lambda i: (0, 0))
    return pl.pallas_call(body, name="loss_head", grid=(t // tr,), in_specs=[row, vec, row],
                          out_specs=[one, row, row, vec],
                          out_shape=[_sds((8, 128), f32), _sds((t, d), f32), _sds((t, d), bf16), _sds((1, d), f32)],
                          compiler_params=_cparams(("arbitrary",)))(x, gain, target)


def _ffn_act_fwd(u, conv_w, conv_b):
    t = u.shape[0]
    tc = FF_PAD // 2
    nb = D_FFP // tc

    def body(g_ref, v_ref, w_ref, b_ref, a_ref):
        gc = _conv_fwd(g_ref[...], w_ref, FFN_CONV) + b_ref[...]
        a_ref[...] = (_silu(gc) * v_ref[...]).astype(bf16)

    return pl.pallas_call(
        body, name="ffn_act_fwd", grid=(nb,),
        in_specs=[pl.BlockSpec((t, tc), lambda j: (0, j)), pl.BlockSpec((t, tc), lambda j: (0, j + nb)),
                  pl.BlockSpec((FFN_CONV, tc), lambda j: (0, j)), pl.BlockSpec((1, tc), lambda j: (0, j))],
        out_specs=pl.BlockSpec((t, tc), lambda j: (0, j)), out_shape=_sds((t, D_FFP), bf16),
        compiler_params=_cparams(("parallel",)))(u, u, conv_w, conv_b)


def _ffn_act_bwd(u, conv_w, conv_b, da):
    t = u.shape[0]
    tc = FF_PAD // 2
    nb = D_FFP // tc

    def act(gc, val):
        return _silu(gc) * val

    def body(g_ref, v_ref, w_ref, b_ref, da_ref, a_ref, dg_ref, dv_ref, dw_ref, db_ref):
        gp = g_ref[...]
        gc = _conv_fwd(gp, w_ref, FFN_CONV) + b_ref[...]
        a, vjp = jax.vjp(act, gc, v_ref[...])
        dgc, dval = vjp(da_ref[...])
        a_ref[...] = a.astype(bf16)
        dv_ref[...] = dval.astype(bf16)
        db_ref[...] = jnp.sum(dgc, axis=0, keepdims=True)
        dg_ref[...] = _conv_bwd(gp, dgc, w_ref, dw_ref, FFN_CONV).astype(bf16)

    col = pl.BlockSpec((t, tc), lambda j: (0, j))
    return pl.pallas_call(
        body, name="ffn_act_bwd", grid=(nb,),
        in_specs=[col, pl.BlockSpec((t, tc), lambda j: (0, j + nb)), pl.BlockSpec((FFN_CONV, tc), lambda j: (0, j)),
                  pl.BlockSpec((1, tc), lambda j: (0, j)), col],
        out_specs=[col, col, col, pl.BlockSpec((FFN_CONV, tc), lambda j: (0, j)), pl.BlockSpec((1, tc), lambda j: (0, j))],
        out_shape=[_sds((t, D_FFP), bf16), _sds((t, D_FFP), bf16), _sds((t, D_FFP), bf16), _sds((FFN_CONV, D_FFP), f32),
                   _sds((1, D_FFP), f32)],
        compiler_params=_cparams(("parallel",)))(u, u, conv_w, conv_b, da)


def _lru_gates(xc, ra, ia, lam):
    r = _sigmoid(ra)
    i = _sigmoid(ia)
    log_a = -RG_C * r * _softplus(-lam)
    a = jnp.exp(log_a)
    u = jnp.sqrt(jnp.maximum(-_expm1(2.0 * log_a), 0.0)) * (i * xc)
    return a, u


def _lin_scan(a, u):
    n = a.shape[0]
    row = lax.broadcasted_iota(jnp.int32, a.shape, 0)
    s = 1
    while s < n:
        keep = row >= s
        u = a * jnp.where(keep, pltpu.roll(u, s, 0), 0.0) + u
        a = a * jnp.where(keep, pltpu.roll(a, s, 0), 1.0)
        s *= 2
    return u


def _rev_scan(a_next, d):
    n = d.shape[0]
    row = lax.broadcasted_iota(jnp.int32, d.shape, 0)
    a = a_next
    s = 1
    while s < n:
        keep = row < n - s
        d = a * jnp.where(keep, pltpu.roll(d, n - s, 0), 0.0) + d
        a = a * jnp.where(keep, pltpu.roll(a, n - s, 0), 1.0)
        s *= 2
    return d


def _col_conv_fwd(p, col_off, conv_w, conv_b, width, tc, name):
    t = p.shape[0]
    c = conv_w.shape[1]
    ob = col_off // tc

    def body(x_ref, w_ref, b_ref, o_ref):
        o_ref[...] = _conv_fwd(x_ref[...], w_ref, width) + b_ref[...]

    return pl.pallas_call(
        body, name=name, grid=(c // tc,),
        in_specs=[pl.BlockSpec((t, tc), lambda j: (0, j + ob)), pl.BlockSpec((width, tc), lambda j: (0, j)),
                  pl.BlockSpec((1, tc), lambda j: (0, j))],
        out_specs=pl.BlockSpec((t, tc), lambda j: (0, j)), out_shape=_sds((t, c), f32),
        compiler_params=_cparams(("parallel",)))(p, conv_w, conv_b)


def _col_conv_bwd(p, col_off, conv_w, dxc, width, tc, name):
    t = p.shape[0]
    c = conv_w.shape[1]
    ob = col_off // tc

    def body(x_ref, w_ref, d_ref, dx_ref, dw_ref, db_ref):
        d = d_ref[...]
        db_ref[...] = jnp.sum(d, axis=0, keepdims=True)
        dx_ref[...] = _conv_bwd(x_ref[...], d, w_ref, dw_ref, width).astype(bf16)

    col = pl.BlockSpec((t, tc), lambda j: (0, j))
    return pl.pallas_call(
        body, name=name, grid=(c // tc,),
        in_specs=[pl.BlockSpec((t, tc), lambda j: (0, j + ob)), pl.BlockSpec((width, tc), lambda j: (0, j)), col],
        out_specs=[col, pl.BlockSpec((width, tc), lambda j: (0, j)), pl.BlockSpec((1, tc), lambda j: (0, j))],
        out_shape=[_sds((t, c), bf16), _sds((width, c), f32), _sds((1, c), f32)],
        compiler_params=_cparams(("parallel",)))(p, conv_w, dxc)


def _lru_fwd(p, xc, wa, ba, wx, bx, lam):
    t = p.shape[0]
    bw = LRU_BLOCK

    def body(y_ref, xc_ref, wa_ref, ba_ref, wx_ref, bx_ref, lam_ref, out_ref, hs_ref, a_ref):
        xc_v = xc_ref[...]
        xb = xc_v.astype(bf16)
        ra = _dot(xb, wa_ref[0]) + ba_ref[...]
        ia = _dot(xb, wx_ref[0]) + bx_ref[...]
        a, u = _lru_gates(xc_v, ra, ia, lam_ref[...])
        a_ref[...] = a
        hs = _lin_scan(a, u)
        hs_ref[...] = hs
        out_ref[...] = (hs * _gelu(y_ref[...])).astype(bf16)

    col = pl.BlockSpec((t, bw), lambda h: (0, h))
    vec = pl.BlockSpec((1, bw), lambda h: (0, h))
    mat = pl.BlockSpec((1, bw, bw), lambda h: (h, 0, 0))
    return pl.pallas_call(
        body, name="lru_fwd", grid=(HEADS,), in_specs=[col, col, mat, vec, mat, vec, vec], out_specs=[col, col, col],
        out_shape=[_sds((t, LRU_WIDTH), bf16), _sds((t, LRU_WIDTH), f32), _sds((t, LRU_WIDTH), f32)],
        compiler_params=_cparams(("parallel",)))(p, xc, wa, ba, wx, bx, lam)


def _lru_bwd_scan(p, a, hs, dout):
    t = p.shape[0]
    bw = LRU_BLOCK

    def body(y_ref, a_ref, hs_ref, do_ref, dy_ref, da_ref, du_ref):
        hs_v = hs_ref[...]
        do = do_ref[...]
        gate, vjp = jax.vjp(_gelu, y_ref[...])
        dy_ref[...] = vjp(do * hs_v)[0].astype(bf16)
        g = _rev_scan(_shift_up(a_ref[...], 1), do * gate)
        du_ref[...] = g
        da_ref[...] = g * _shift_down(hs_v, 1)

    col = pl.BlockSpec((t, bw), lambda h: (0, h))
    return pl.pallas_call(
        body, name="lru_bwd_scan", grid=(HEADS,), in_specs=[col, col, col, col], out_specs=[col, col, col],
        out_shape=[_sds((t, LRU_WIDTH), bf16), _sds((t, LRU_WIDTH), f32), _sds((t, LRU_WIDTH), f32)],
        compiler_params=_cparams(("parallel",)))(p, a, hs, dout)


def _lru_bwd_gates(xc, da, du, wa, ba, wx, bx, lam):
    t = xc.shape[0]
    bw = LRU_BLOCK
    tr = _tile(t, (512, 256, 128))

    def body(xc_ref, da_ref, du_ref, wa_ref, ba_ref, wx_ref, bx_ref, lam_ref,
             dxc_ref, dwa_ref, dwx_ref, dba_ref, dbx_ref, dlam_ref):
        xc_v = xc_ref[...]
        xb = xc_v.astype(bf16)
        ra = _dot(xb, wa_ref[0]) + ba_ref[...]
        ia = _dot(xb, wx_ref[0]) + bx_ref[...]
        _, vjp = jax.vjp(_lru_gates, xc_v, ra, ia, lam_ref[...])
        dxc, dra, dia, dlam = vjp((da_ref[...], du_ref[...]))
        drb, dib = dra.astype(bf16), dia.astype(bf16)
        dxc_ref[...] = dxc + _dot(drb, wa_ref[0], NT) + _dot(dib, wx_ref[0], NT)

        @pl.when(pl.program_id(1) == 0)
        def _():
            dwa_ref[...] = jnp.zeros_like(dwa_ref)
            dwx_ref[...] = jnp.zeros_like(dwx_ref)
            dba_ref[...] = jnp.zeros_like(dba_ref)
            dbx_ref[...] = jnp.zeros_like(dbx_ref)
            dlam_ref[...] = jnp.zeros_like(dlam_ref)

        dwa_ref[0] += _dot(xb, drb, TN)
        dwx_ref[0] += _dot(xb, dib, TN)
        dba_ref[...] += jnp.sum(dra, axis=0, keepdims=True)
        dbx_ref[...] += jnp.sum(dia, axis=0, keepdims=True)
        dlam_ref[...] += dlam

    tile = pl.BlockSpec((tr, bw), lambda h, i: (i, h))
    vec = pl.BlockSpec((1, bw), lambda h, i: (0, h))
    mat = pl.BlockSpec((1, bw, bw), lambda h, i: (h, 0, 0))
    return pl.pallas_call(
        body, name="lru_bwd_gates", grid=(HEADS, t // tr), in_specs=[tile, tile, tile, mat, vec, mat, vec, vec],
        out_specs=[tile, mat, mat, vec, vec, vec],
        out_shape=[_sds((t, LRU_WIDTH), f32), _sds((HEADS, bw, bw), f32), _sds((HEADS, bw, bw), f32),
                   _sds((1, LRU_WIDTH), f32), _sds((1, LRU_WIDTH), f32), _sds((1, LRU_WIDTH), f32)],
        compiler_params=_cparams(("parallel", "arbitrary")))(xc, da, du, wa, ba, wx, bx, lam)


def _gdn_pre_fn(cq, ck, cv, ba, alog, dtb, h):
    q, k, v = _silu(cq), _silu(ck), _silu(cv)
    q = q * lax.rsqrt(jnp.sum(q * q, axis=-1, keepdims=True) + EPS) * (HEAD_DIM ** -0.5)
    k = k * lax.rsqrt(jnp.sum(k * k, axis=-1, keepdims=True) + EPS)
    lane = lax.broadcasted_iota(jnp.int32, (1, HEAD_DIM), 1)
    mb = (lane == h).astype(f32)
    ma = (lane == HEADS + h).astype(f32)
    beta_raw = jnp.sum(ba * mb, axis=-1, keepdims=True)
    alpha = jnp.sum(ba * ma, axis=-1, keepdims=True)
    al = jnp.sum(alog * mb, axis=-1, keepdims=True)
    db = jnp.sum(dtb * mb, axis=-1, keepdims=True)
    beta = _sigmoid(beta_raw)
    g = -jnp.exp(al) * _softplus(alpha + db)
    return q, k, v, jnp.broadcast_to(beta, q.shape), jnp.broadcast_to(g, q.shape)


def _gdn_pre_fwd(p, conv_w, alog, dtb):
    t = p.shape[0]
    hd = HEAD_DIM

    def body(pq_ref, pk_ref, pv_ref, ba_ref, wq_ref, wk_ref, wv_ref, al_ref, dt_ref, q_ref, k_ref, v_ref, b_ref, g_ref):
        h = pl.program_id(0)
        cq = _conv_fwd(pq_ref[...], wq_ref, GDN_CONV)
        ck = _conv_fwd(pk_ref[...], wk_ref, GDN_CONV)
        cv = _conv_fwd(pv_ref[...], wv_ref, GDN_CONV)
        q, k, v, be, ge = _gdn_pre_fn(cq, ck, cv, ba_ref[...], al_ref[...], dt_ref[...], h)
        q_ref[...], k_ref[...], v_ref[...], b_ref[...], g_ref[...] = q, k, v, be, ge

    def pcol(off):
        return pl.BlockSpec((t, hd), lambda h: (0, h + off // hd))

    def wcol(off):
        return pl.BlockSpec((GDN_CONV, hd), lambda h: (0, h + off // hd))

    vec = pl.BlockSpec((1, hd), lambda h: (0, 0))
    out = pl.BlockSpec((t, hd), lambda h: (0, h))
    return pl.pallas_call(
        body, name="gdn_pre_fwd", grid=(HEADS,),
        in_specs=[pcol(OFF_Q), pcol(OFF_K), pcol(OFF_V), pl.BlockSpec((t, hd), lambda h: (0, OFF_BA // hd)),
                  wcol(0), wcol(GDN_WIDTH), wcol(2 * GDN_WIDTH), vec, vec],
        out_specs=[out] * 5, out_shape=[_sds((t, GDN_WIDTH), f32)] * 5,
        compiler_params=_cparams(("parallel",)))(p, p, p, p, conv_w, conv_w, conv_w, alog, dtb)


def _gdn_pre_bwd(p, conv_w, alog, dtb, dq, dk, dv, dbe, dge):
    t = p.shape[0]
    hd = HEAD_DIM

    def body(pq_ref, pk_ref, pv_ref, ba_ref, wq_ref, wk_ref, wv_ref, al_ref, dt_ref,
             dq_ref, dk_ref, dv_ref, dbe_ref, dge_ref,
             opq_ref, opk_ref, opv_ref, dba_ref, dwq_ref, dwk_ref, dwv_ref, dal_ref, ddt_ref):
        h = pl.program_id(0)
        pq, pk, pv = pq_ref[...], pk_ref[...], pv_ref[...]
        cq = _conv_fwd(pq, wq_ref, GDN_CONV)
        ck = _conv_fwd(pk, wk_ref, GDN_CONV)
        cv = _conv_fwd(pv, wv_ref, GDN_CONV)
        _, vjp = jax.vjp(functools.partial(_gdn_pre_fn, h=h), cq, ck, cv, ba_ref[...], al_ref[...], dt_ref[...])
        dcq, dck, dcv, dba, dal, ddt = vjp((dq_ref[...], dk_ref[...], dv_ref[...], dbe_ref[...], dge_ref[...]))
        opq_ref[...] = _conv_bwd(pq, dcq, wq_ref, dwq_ref, GDN_CONV).astype(bf16)
        opk_ref[...] = _conv_bwd(pk, dck, wk_ref, dwk_ref, GDN_CONV).astype(bf16)
        opv_ref[...] = _conv_bwd(pv, dcv, wv_ref, dwv_ref, GDN_CONV).astype(bf16)

        @pl.when(h == 0)
        def _():
            dba_ref[...] = jnp.zeros_like(dba_ref)
            dal_ref[...] = jnp.zeros_like(dal_ref)
            ddt_ref[...] = jnp.zeros_like(ddt_ref)

        dba_ref[...] += dba
        dal_ref[...] += dal
        ddt_ref[...] += ddt

    def pcol(off):
        return pl.BlockSpec((t, hd), lambda h: (0, h + off // hd))

    def wcol(off):
        return pl.BlockSpec((GDN_CONV, hd), lambda h: (0, h + off // hd))

    vec = pl.BlockSpec((1, hd), lambda h: (0, 0))
    col = pl.BlockSpec((t, hd), lambda h: (0, h))
    full = pl.BlockSpec((t, hd), lambda h: (0, 0))
    wout = pl.BlockSpec((GDN_CONV, hd), lambda h: (0, h))
    return pl.pallas_call(
        body, name="gdn_pre_bwd", grid=(HEADS,),
        in_specs=[pcol(OFF_Q), pcol(OFF_K), pcol(OFF_V), pl.BlockSpec((t, hd), lambda h: (0, OFF_BA // hd)),
                  wcol(0), wcol(GDN_WIDTH), wcol(2 * GDN_WIDTH), vec, vec, col, col, col, col, col],
        out_specs=[col, col, col, full, wout, wout, wout, vec, vec],
        out_shape=[_sds((t, GDN_WIDTH), bf16)] * 3 + [_sds((t, hd), f32)] + [_sds((GDN_CONV, GDN_WIDTH), f32)] * 3
        + [_sds((1, hd), f32)] * 2,
        compiler_params=_cparams(("arbitrary",)))(p, p, p, p, conv_w, conv_w, conv_w, alog, dtb, dq, dk, dv, dbe, dge)


BNN = (((2,), (1,)), ((0,), (0,)))
BNT = (((2,), (2,)), ((0,), (0,)))
BTN = (((1,), (1,)), ((0,), (0,)))


def _hdot(a, b, dn=BNN, precision=None):
    return lax.dot_general(a, b, dn, precision=precision, preferred_element_type=f32)


def _hbdot(a, b, dn=BNN):
    return _hdot(a.astype(bf16), b.astype(bf16), dn)


def _tri_inverse(a):
    c = a.shape[-1]
    r = lax.broadcasted_iota(jnp.int32, (c, c), 0)
    col = lax.broadcasted_iota(jnp.int32, (c, c), 1)
    m = -a
    inv = jnp.where(r == col, 1.0, 0.0) + m
    s = 2
    while s < c:
        m = _hdot(m, m, precision=HI)
        inv = inv + _hdot(inv, m, precision=HI)
        s *= 2
    return inv


def _gdn_chunk(s, q, k, v, ge, be):
    nh, c, _ = q.shape
    r = lax.broadcasted_iota(jnp.int32, (c, c), 0)
    col = lax.broadcasted_iota(jnp.int32, (c, c), 1)
    causal = r >= col
    tri = jnp.broadcast_to(causal.astype(f32), (nh, c, c))
    gc = _hdot(tri, ge, precision=HI)
    gcc = gc[:, :, :c]
    gcr = jnp.swapaxes(gc, 1, 2)[:, :c, :]
    decay = jnp.where(causal, jnp.exp(jnp.where(causal, gcc - gcr, 0.0)), 0.0)
    kb = k * be
    lower = jnp.where(r > col, _hbdot(kb, k, BNT) * decay, 0.0)
    tinv = _tri_inverse(lower)
    egc = jnp.exp(gc)
    u = _hdot(tinv, v * be, precision=HI)
    w = _hdot(tinv, kb * egc, precision=HI)
    attn = _hbdot(q, k, BNT) * decay
    gl = gc[:, c - 1:c, :]
    v_new = u - _hbdot(w, s)
    o = _hbdot(q * egc, s) + _hbdot(attn, v_new)
    s_new = s * jnp.exp(gl) + _hbdot(k * jnp.exp(gl - gc), v_new, BTN)
    return o, s_new


def _heads_major(ref):
    return jnp.stack([ref[:, h * HEAD_DIM:(h + 1) * HEAD_DIM] for h in range(HEADS)])


def _gdn_core_fwd(q, k, v, ge, be):
    t = q.shape[0]
    c, hd = GDN_CHUNK, HEAD_DIM
    n = t // c

    def body(q_ref, k_ref, v_ref, g_ref, b_ref, o_ref, st_ref, s_ref):
        @pl.when(pl.program_id(0) == 0)
        def _():
            s_ref[...] = jnp.zeros_like(s_ref)

        s = s_ref[...]
        st_ref[:, 0] = s
        o, s_new = _gdn_chunk(s, *[_heads_major(r) for r in (q_ref, k_ref, v_ref, g_ref, b_ref)])
        for h in range(HEADS):
            o_ref[:, h * hd:(h + 1) * hd] = o[h]
        s_ref[...] = s_new

    tile = pl.BlockSpec((c, GDN_WIDTH), lambda i: (i, 0))
    return pl.pallas_call(
        body, name="gdn_core_fwd", grid=(n,), in_specs=[tile] * 5,
        out_specs=[tile, pl.BlockSpec((HEADS, 1, hd, hd), lambda i: (0, i, 0, 0))],
        out_shape=[_sds((t, GDN_WIDTH), f32), _sds((HEADS, n, hd, hd), f32)],
        scratch_shapes=[pltpu.VMEM((HEADS, hd, hd), f32)],
        compiler_params=_cparams(("arbitrary",)))(q, k, v, ge, be)


def _gdn_core_bwd(q, k, v, ge, be, states, do):
    t = q.shape[0]
    c, hd = GDN_CHUNK, HEAD_DIM
    n = t // c

    def body(q_ref, k_ref, v_ref, g_ref, b_ref, st_ref, do_ref, dq_ref, dk_ref, dv_ref, dg_ref, db_ref, ds_ref):
        @pl.when(pl.program_id(0) == 0)
        def _():
            ds_ref[...] = jnp.zeros_like(ds_ref)

        _, vjp = jax.vjp(_gdn_chunk, st_ref[:, 0], *[_heads_major(r) for r in (q_ref, k_ref, v_ref, g_ref, b_ref)])
        ds, *dins = vjp((_heads_major(do_ref), ds_ref[...]))
        ds_ref[...] = ds
        for d_ref, d in zip((dq_ref, dk_ref, dv_ref, dg_ref, db_ref), dins):
            for h in range(HEADS):
                d_ref[:, h * hd:(h + 1) * hd] = d[h]

    tile = pl.BlockSpec((c, GDN_WIDTH), lambda i: (n - 1 - i, 0))
    return pl.pallas_call(
        body, name="gdn_core_bwd", grid=(n,),
        in_specs=[tile] * 5 + [pl.BlockSpec((HEADS, 1, hd, hd), lambda i: (0, n - 1 - i, 0, 0)), tile],
        out_specs=[tile] * 5, out_shape=[_sds((t, GDN_WIDTH), f32)] * 5,
        scratch_shapes=[pltpu.VMEM((HEADS, hd, hd), f32)],
        compiler_params=_cparams(("arbitrary",)))(q, k, v, ge, be, states, do)


def _post_fn(o, z, gain):
    return _rms(o, gain) * _silu(z)


def _post_fwd(o, p, z_off, gain, name):
    t = o.shape[0]
    hd = HEAD_DIM

    def body(o_ref, z_ref, g_ref, y_ref):
        y_ref[...] = _post_fn(o_ref[...], z_ref[...], g_ref[...]).astype(bf16)

    col = pl.BlockSpec((t, hd), lambda h: (0, h))
    return pl.pallas_call(
        body, name=name, grid=(HEADS,),
        in_specs=[col, pl.BlockSpec((t, hd), lambda h: (0, h + z_off // hd)), pl.BlockSpec((1, hd), lambda h: (0, 0))],
        out_specs=col, out_shape=_sds((t, HEADS * hd), bf16), compiler_params=_cparams(("parallel",)))(o, p, gain)


def _post_bwd(o, p, z_off, gain, dmix, mix_off, name):
    t = o.shape[0]
    hd = HEAD_DIM

    def body(o_ref, z_ref, g_ref, dy_ref, do_ref, dz_ref, dg_ref):
        _, vjp = jax.vjp(_post_fn, o_ref[...], z_ref[...], g_ref[...])
        do, dz, dg = vjp(dy_ref[...])
        do_ref[...] = do
        dz_ref[...] = dz.astype(bf16)

        @pl.when(pl.program_id(0) == 0)
        def _():
            dg_ref[...] = jnp.zeros_like(dg_ref)

        dg_ref[...] += dg

    col = pl.BlockSpec((t, hd), lambda h: (0, h))
    vec = pl.BlockSpec((1, hd), lambda h: (0, 0))
    return pl.pallas_call(
        body, name=name, grid=(HEADS,),
        in_specs=[col, pl.BlockSpec((t, hd), lambda h: (0, h + z_off // hd)), vec,
                  pl.BlockSpec((t, hd), lambda h: (0, h + mix_off // hd))],
        out_specs=[col, col, vec], out_shape=[_sds((t, HEADS * hd), f32), _sds((t, HEADS * hd), bf16), _sds((1, hd), f32)],
        compiler_params=_cparams(("arbitrary",)))(o, p, gain, dmix)


def _hgrn_pre_fn(qb, fb, lbw, layer):
    l0, l1 = lbw[0:1, :], lbw[1:2, :]
    m = jnp.maximum(l0, l1)
    e0, e1 = jnp.exp(l0 - m), jnp.exp(l1 - m)
    p0, p1 = e0 / (e0 + e1), e1 / (e0 + e1)
    lb = (p0 - p0) if layer == 0 else ((p0 + p1) - p0)
    f = lb + (1.0 - lb) * _sigmoid(fb)
    return _silu(qb), 1.0 - f, jnp.log(jnp.maximum(f, F_FLOOR))


def _hgrn_pre_fwd(p, lbw, layer):
    t = p.shape[0]
    tc = HEAD_DIM

    def body(qb_ref, fb_ref, lb_ref, q_ref, k_ref, lf_ref):
        q_ref[...], k_ref[...], lf_ref[...] = _hgrn_pre_fn(qb_ref[...], fb_ref[...], lb_ref[...], layer)

    col = pl.BlockSpec((t, tc), lambda j: (0, j))
    return pl.pallas_call(
        body, name="hgrn_pre_fwd", grid=(GDN_WIDTH // tc,),
        in_specs=[pl.BlockSpec((t, tc), lambda j: (0, j + OFF_QB // tc)), pl.BlockSpec((t, tc), lambda j: (0, j + OFF_FB // tc)),
                  pl.BlockSpec((2, tc), lambda j: (0, j))],
        out_specs=[col] * 3, out_shape=[_sds((t, GDN_WIDTH), f32)] * 3,
        compiler_params=_cparams(("parallel",)))(p, p, lbw)


def _hgrn_pre_bwd(p, lbw, layer, dq, dk, dlf):
    t = p.shape[0]
    tc = HEAD_DIM

    def body(qb_ref, fb_ref, lb_ref, dq_ref, dk_ref, dlf_ref, dqb_ref, dfb_ref, dlb_ref):
        _, vjp = jax.vjp(functools.partial(_hgrn_pre_fn, layer=layer), qb_ref[...], fb_ref[...], lb_ref[...])
        dqb, dfb, dlb = vjp((dq_ref[...], dk_ref[...], dlf_ref[...]))
        dqb_ref[...] = dqb.astype(bf16)
        dfb_ref[...] = dfb.astype(bf16)
        dlb_ref[...] = dlb

    col = pl.BlockSpec((t, tc), lambda j: (0, j))
    lb = pl.BlockSpec((2, tc), lambda j: (0, j))
    return pl.pallas_call(
        body, name="hgrn_pre_bwd", grid=(GDN_WIDTH // tc,),
        in_specs=[pl.BlockSpec((t, tc), lambda j: (0, j + OFF_QB // tc)), pl.BlockSpec((t, tc), lambda j: (0, j + OFF_FB // tc)),
                  lb, col, col, col],
        out_specs=[col, col, lb], out_shape=[_sds((t, GDN_WIDTH), bf16)] * 2 + [_sds((2, GDN_WIDTH), f32)],
        compiler_params=_cparams(("parallel",)))(p, p, lbw, dq, dk, dlf)


def _hgrn_step(st, q, k, lf, v):
    c = HGRN_CHUNK
    nh = q.shape[0]
    r2 = lax.broadcasted_iota(jnp.int32, (c, c), 0)
    c2 = lax.broadcasted_iota(jnp.int32, (c, c), 1)
    tri = jnp.broadcast_to((r2 >= c2).astype(f32), (nh, c, c))
    i3 = lax.broadcasted_iota(jnp.int32, (c, c, HEAD_DIM), 0)
    j3 = lax.broadcasted_iota(jnp.int32, (c, c, HEAD_DIM), 1)
    mask = i3 >= j3
    outs = []
    for n in range(q.shape[1] // c):
        sl = slice(n * c, (n + 1) * c)
        qc, kc, lc, vc = q[:, sl], k[:, sl], lf[:, sl], v[:, sl]
        b = _hdot(tri, lc, precision=HI)
        rel = jnp.where(mask, jnp.exp(jnp.where(mask, b[:, :, None, :] - b[:, None, :, :], 0.0)), 0.0)
        scores = jnp.sum(qc[:, :, None, :] * kc[:, None, :, :] * rel, axis=-1)
        bl = b[:, c - 1:c, :]
        o = _hbdot(scores, vc) + _hbdot(qc * jnp.exp(b), st, BNT)
        st = st * jnp.exp(bl) + _hbdot(vc, kc * jnp.exp(bl - b), BTN)
        outs.append(o)
    return jnp.concatenate(outs, axis=1), st


def _hgrn_core_fwd(q, k, lf, p):
    t = q.shape[0]
    hd = HEAD_DIM
    rs = min(HGRN_STEP, t)
    n = t // rs

    def body(q_ref, k_ref, lf_ref, v_ref, o_ref, st_ref, s_ref):
        @pl.when(pl.program_id(0) == 0)
        def _():
            s_ref[...] = jnp.zeros_like(s_ref)

        s = s_ref[...]
        st_ref[:, 0] = s
        o, s_new = _hgrn_step(s, *[_heads_major(r) for r in (q_ref, k_ref, lf_ref, v_ref)])
        for h in range(HEADS):
            o_ref[:, h * hd:(h + 1) * hd] = o[h]
        s_ref[...] = s_new

    tile = pl.BlockSpec((rs, GDN_WIDTH), lambda i: (i, 0))
    return pl.pallas_call(
        body, name="hgrn_core_fwd", grid=(n,),
        in_specs=[tile, tile, tile, pl.BlockSpec((rs, GDN_WIDTH), lambda i: (i, OFF_IB // GDN_WIDTH))],
        out_specs=[tile, pl.BlockSpec((HEADS, 1, hd, hd), lambda i: (0, i, 0, 0))],
        out_shape=[_sds((t, GDN_WIDTH), f32), _sds((HEADS, n, hd, hd), f32)],
        scratch_shapes=[pltpu.VMEM((HEADS, hd, hd), f32)],
        compiler_params=_cparams(("arbitrary",)))(q, k, lf, p)


def _hgrn_core_bwd(q, k, lf, p, states, do):
    t = q.shape[0]
    hd = HEAD_DIM
    rs = min(HGRN_STEP, t)
    n = t // rs

    def body(q_ref, k_ref, lf_ref, v_ref, st_ref, do_ref, dq_ref, dk_ref, dlf_ref, dv_ref, ds_ref):
        @pl.when(pl.program_id(0) == 0)
        def _():
            ds_ref[...] = jnp.zeros_like(ds_ref)

        _, vjp = jax.vjp(_hgrn_step, st_ref[:, 0], *[_heads_major(r) for r in (q_ref, k_ref, lf_ref, v_ref)])
        ds, *dins = vjp((_heads_major(do_ref), ds_ref[...]))
        ds_ref[...] = ds
        for d_ref, d in zip((dq_ref, dk_ref, dlf_ref, dv_ref), dins):
            for h in range(HEADS):
                d_ref[:, h * hd:(h + 1) * hd] = d[h].astype(d_ref.dtype)

    tile = pl.BlockSpec((rs, GDN_WIDTH), lambda i: (n - 1 - i, 0))
    return pl.pallas_call(
        body, name="hgrn_core_bwd", grid=(n,),
        in_specs=[tile, tile, tile, pl.BlockSpec((rs, GDN_WIDTH), lambda i: (n - 1 - i, OFF_IB // GDN_WIDTH)),
                  pl.BlockSpec((HEADS, 1, hd, hd), lambda i: (0, n - 1 - i, 0, 0)), tile],
        out_specs=[tile] * 4, out_shape=[_sds((t, GDN_WIDTH), f32)] * 3 + [_sds((t, GDN_WIDTH), bf16)],
        scratch_shapes=[pltpu.VMEM((HEADS, hd, hd), f32)],
        compiler_params=_cparams(("arbitrary",)))(q, k, lf, p, states, do)


def _row(v):
    return v.reshape(1, -1)


def _pad_lanes(v, n=HEAD_DIM):
    return jnp.pad(v.reshape(1, -1), ((0, 0), (0, n - v.shape[-1])))


def _ffn_fwd(x, w, l):
    h = _rms_fwd(x, _row(w['norm_ffn'][l]), "ffn_norm")
    u = _mm(h, w['ffn_w_up'][l], name="ffn_up")
    a = _ffn_act_fwd(u, w['ffn_conv_w'][l], _row(w['ffn_conv_b'][l]))
    y = _mm(a, w['ffn_w_down'][l], add=x, name="ffn_down")
    return y, (x, h, u)


def _ffn_bwd(saved, w, l, dy, dyb, grads):
    x, h, u = saved
    da = _mm(dyb, w['ffn_w_down'][l], tb=True, name="ffn_down_dx")
    a, dg, dv, dcw, dcb = _ffn_act_bwd(u, w['ffn_conv_w'][l], _row(w['ffn_conv_b'][l]), da)
    grads['ffn_w_down'][l] = _mm(a, dyb, ta=True, out_dtype=bf16, name="ffn_down_dw")
    du = jnp.concatenate([dg, dv], axis=1)
    grads['ffn_w_up'][l] = _mm(h, du, ta=True, out_dtype=bf16, name="ffn_up_dw")
    dh = _mm(du, w['ffn_w_up'][l], tb=True, name="ffn_up_dx")
    dx, dxb, dgain = _rms_bwd(x, _row(w['norm_ffn'][l]), dh, dy, "ffn_norm_bwd")
    grads['ffn_conv_w'][l] = dcw
    grads['ffn_conv_b'][l] = dcb[0]
    grads['norm_ffn'][l] = dgain[0]
    return dx, dxb


def _odd_fwd(x, w, l, j):
    h = _rms_fwd(x, _row(w['norm_mix'][l]), "mix_norm")
    p = _mm(h, w['c_w_in'][j], name="lru_in")
    xc = _col_conv_fwd(p, LRU_WIDTH, w['c_conv_w'][j], _row(w['c_conv_b'][j]), LRU_CONV, 256, "lru_conv_fwd")
    out, hs, a = _lru_fwd(p, xc, w['c_gate_a_w'][j], _row(w['c_gate_a_b'][j]), w['c_gate_x_w'][j],
                          _row(w['c_gate_x_b'][j]), _row(w['c_lambda'][j]))
    y = _mm(out, w['c_w_out'][j], add=x, name="lru_out")
    return y, (x, h, p, xc, out, hs, a)


def _odd_bwd(saved, w, l, j, dy, dyb, grads):
    x, h, p, xc, out, hs, a = saved
    dout = _mm(dyb, w['c_w_out'][j], tb=True, name="lru_out_dx")
    grads['c_w_out'][j] = _mm(out, dyb, ta=True, out_dtype=bf16, name="lru_out_dw")
    dyb_, da, du = _lru_bwd_scan(p, a, hs, dout)
    dxc, dwa, dwx, dba, dbx, dlam = _lru_bwd_gates(xc, da, du, w['c_gate_a_w'][j], _row(w['c_gate_a_b'][j]),
                                                   w['c_gate_x_w'][j], _row(w['c_gate_x_b'][j]), _row(w['c_lambda'][j]))
    dxb_, dcw, dcb = _col_conv_bwd(p, LRU_WIDTH, w['c_conv_w'][j], dxc, LRU_CONV, 256, "lru_conv_bwd")
    dp = jnp.concatenate([dyb_, dxb_], axis=1)
    grads['c_w_in'][j] = _mm(h, dp, ta=True, out_dtype=bf16, name="lru_in_dw")
    dh = _mm(dp, w['c_w_in'][j], tb=True, name="lru_in_dx")
    dx, dxb, dgain = _rms_bwd(x, _row(w['norm_mix'][l]), dh, dy, "mix_norm_bwd")
    grads['c_gate_a_w'][j], grads['c_gate_x_w'][j] = dwa, dwx
    grads['c_gate_a_b'][j], grads['c_gate_x_b'][j], grads['c_lambda'][j] = dba[0], dbx[0], dlam[0]
    grads['c_conv_w'][j], grads['c_conv_b'][j] = dcw, dcb[0]
    grads['norm_mix'][l] = dgain[0]
    return dx, dxb


def _even_fwd(x, w, l, j):
    h = _rms_fwd(x, _row(w['norm_mix'][l]), "mix_norm")
    p = _mm(h, w['ab_w_in'][j], name="ab_in")
    alog, dtb = _pad_lanes(w['gdn_a_log'][j]), _pad_lanes(w['gdn_dt_bias'][j])
    q, k, v, be, ge = _gdn_pre_fwd(p, w['gdn_conv_w'][j], alog, dtb)
    oa, sa = _gdn_core_fwd(q, k, v, ge, be)
    ya = _post_fwd(oa, p, OFF_Z, _row(w['gdn_norm'][j]), "gdn_post_fwd")
    qq, kk, lf = _hgrn_pre_fwd(p, w['hgrn_lower_bounds'], j)
    ob, sb = _hgrn_core_fwd(qq, kk, lf, p)
    yb = _post_fwd(ob, p, OFF_GB, _row(w['hgrn_norm'][j]), "hgrn_post_fwd")
    mix = jnp.concatenate([ya, yb], axis=1)
    y = _mm(mix, w['ab_w_out'][j], add=x, name="ab_out")
    return y, (x, h, p, q, k, v, be, ge, oa, sa, qq, kk, lf, ob, sb, mix)


def _even_bwd(saved, w, l, j, dy, dyb, grads):
    x, h, p, q, k, v, be, ge, oa, sa, qq, kk, lf, ob, sb, mix = saved
    alog, dtb = _pad_lanes(w['gdn_a_log'][j]), _pad_lanes(w['gdn_dt_bias'][j])
    dmix = _mm(dyb, w['ab_w_out'][j], tb=True, name="ab_out_dx")
    grads['ab_w_out'][j] = _mm(mix, dyb, ta=True, out_dtype=bf16, name="ab_out_dw")
    doa, dz, dgn = _post_bwd(oa, p, OFF_Z, _row(w['gdn_norm'][j]), dmix, 0, "gdn_post_bwd")
    dob, dgb, dhn = _post_bwd(ob, p, OFF_GB, _row(w['hgrn_norm'][j]), dmix, GDN_WIDTH, "hgrn_post_bwd")
    dq, dk, dv, dge, dbe = _gdn_core_bwd(q, k, v, ge, be, sa, doa)
    dpq, dpk, dpv, dba, dwq, dwk, dwv, dal, ddt = _gdn_pre_bwd(p, w['gdn_conv_w'][j], alog, dtb, dq, dk, dv, dbe, dge)
    dqq, dkk, dlf, dib = _hgrn_core_bwd(qq, kk, lf, p, sb, dob)
    dqb, dfb, dlb = _hgrn_pre_bwd(p, w['hgrn_lower_bounds'], j, dqq, dkk, dlf)
    dp = jnp.concatenate([dpq, dpk, dpv, dz, dqb, dfb, dib, dgb, dba.astype(bf16)], axis=1)
    grads['ab_w_in'][j] = _mm(h, dp, ta=True, out_dtype=bf16, name="ab_in_dw")
    dh = _mm(dp, w['ab_w_in'][j], tb=True, name="ab_in_dx")
    dx, dxb, dgain = _rms_bwd(x, _row(w['norm_mix'][l]), dh, dy, "mix_norm_bwd")
    grads['gdn_conv_w'][j] = jnp.concatenate([dwq, dwk, dwv], axis=1)
    grads['gdn_a_log'][j], grads['gdn_dt_bias'][j] = dal[0, :HEADS], ddt[0, :HEADS]
    grads['gdn_norm'][j], grads['hgrn_norm'][j] = dgn[0], dhn[0]
    grads['hgrn_lower_bounds'].append(dlb)
    grads['norm_mix'][l] = dgain[0]
    return dx, dxb


def _ab_permute(w_in):
    n = w_in.shape[0]
    pad = jnp.zeros((n, D_MODEL, AB_PAD - AB_COLS), w_in.dtype)
    return jnp.concatenate([w_in[..., :2048], w_in[..., 2056:], w_in[..., 2048:2056], pad], axis=-1)


def _ab_unpermute(g):
    return jnp.concatenate([g[..., :2048], g[..., 4096:4104], g[..., 2048:4096]], axis=-1)


def _block_pad(a, axis, nblk, padded):
    axis = axis % a.ndim
    s = a.shape
    a = a.reshape(s[:axis] + (nblk, s[axis] // nblk) + s[axis + 1:])
    pad = [(0, 0)] * a.ndim
    pad[axis + 1] = (0, padded - s[axis] // nblk)
    return jnp.pad(a, pad).reshape(s[:axis] + (nblk * padded,) + s[axis + 1:])


def _block_unpad(a, axis, nblk, width):
    axis = axis % a.ndim
    s = a.shape
    a = a.reshape(s[:axis] + (nblk, s[axis] // nblk) + s[axis + 1:])
    a = lax.slice_in_dim(a, 0, width, axis=axis + 1)
    return a.reshape(s[:axis] + (nblk * width,) + s[axis + 1:])


def _kernel_layout(w):
    w = dict(w)
    w['ab_w_in'] = _ab_permute(w['ab_w_in'])
    w['ffn_w_up'] = _block_pad(w['ffn_w_up'], 2, N_DEV, FF_PAD)
    w['ffn_w_down'] = _block_pad(w['ffn_w_down'], 1, 4, FF_PAD)
    w['ffn_conv_w'] = _block_pad(w['ffn_conv_w'], 2, 4, FF_PAD)
    w['ffn_conv_b'] = _block_pad(w['ffn_conv_b'], 1, 4, FF_PAD)
    return w


def _natural_grads(g):
    g = dict(g)
    g['ab_w_in'] = _ab_unpermute(g['ab_w_in'])
    g['ffn_w_up'] = _block_unpad(g['ffn_w_up'], 2, N_DEV, FF_SHARD)
    g['ffn_w_down'] = _block_unpad(g['ffn_w_down'], 1, 4, FF_SHARD)
    g['ffn_conv_w'] = _block_unpad(g['ffn_conv_w'], 2, 4, FF_SHARD)
    g['ffn_conv_b'] = _block_unpad(g['ffn_conv_b'], 1, 4, FF_SHARD)
    return g


def _local_step(x, target, w):
    grads = {n: [None] * w[n].shape[0] for n in WEIGHTS if n not in ('norm_final', 'hgrn_lower_bounds')}
    grads['hgrn_lower_bounds'] = []
    saved = []
    for l in range(DEPTH):
        j = l // 2
        x, s_mix = (_even_fwd if l % 2 == 0 else _odd_fwd)(x, w, l, j)
        x, s_ffn = _ffn_fwd(x, w, l)
        saved.append((s_mix, s_ffn))
    loss, dx, dxb, dgf = _loss_head(x, _row(w['norm_final']), target)
    for l in reversed(range(DEPTH)):
        j = l // 2
        s_mix, s_ffn = saved[l]
        dx, dxb = _ffn_bwd(s_ffn, w, l, dx, dxb, grads)
        dx, dxb = (_even_bwd if l % 2 == 0 else _odd_bwd)(s_mix, w, l, j, dx, dxb, grads)
    out = {n: jnp.stack(g) for n, g in grads.items() if n != 'hgrn_lower_bounds'}
    out['hgrn_lower_bounds'] = grads['hgrn_lower_bounds'][0] + grads['hgrn_lower_bounds'][1]
    out['norm_final'] = dgf[0]
    return loss[0, 0], dx, out


def _position():
    return lax.axis_index("x"), lax.axis_index("y"), lax.axis_index("c")


BLOCK_LAYOUT = {
    'ab_w_in': ((2, D_MODEL, N_DEV * AB_SHARD_PAD), (2, D_MODEL, AB_SHARD_PAD)),
    'ab_w_out': ((2, N_DEV, 128, D_MODEL), (2, 128, D_MODEL)),
    'c_w_in': ((2, D_MODEL, 2 * LRU_WIDTH), (2, D_MODEL, 256)),
    'c_w_out': ((2, N_DEV, 128, D_MODEL), (2, 128, D_MODEL)),
    'c_gate_a_w': ((2, HEADS, N_DEV, 32, LRU_BLOCK), (2, HEADS, 32, LRU_BLOCK)),
    'c_gate_x_w': ((2, HEADS, N_DEV, 32, LRU_BLOCK), (2, HEADS, 32, LRU_BLOCK)),
    'ffn_w_up': ((DEPTH, D_MODEL, N_DEV * FF_PAD), (DEPTH, D_MODEL, FF_PAD)),
    'ffn_w_down': ((DEPTH, 4, FF_PAD, D_MODEL), (DEPTH, FF_ROWS, D_MODEL)),
}


def _block_of(name, ref, p):
    d = 4 * p[0] + 2 * p[1] + p[2]
    if name == 'ab_w_in':
        return ref.at[:, :, pl.ds(pl.multiple_of(d * AB_SHARD_PAD, 128), AB_SHARD_PAD)]
    if name == 'c_w_in':
        return ref.at[:, :, pl.ds(pl.multiple_of(d * 256, 128), 256)]
    if name == 'ffn_w_up':
        return ref.at[:, :, pl.ds(pl.multiple_of(d * FF_PAD, 128), FF_PAD)]
    if name == 'ffn_w_down':
        return ref.at[:, 2 * p[0] + p[1], pl.ds(pl.multiple_of(p[2] * FF_ROWS, 16), FF_ROWS), :]
    if name in ('c_gate_a_w', 'c_gate_x_w'):
        return ref.at[:, :, d]
    if name in ('ab_w_out', 'c_w_out'):
        return ref.at[:, d]
    return ref.at[d]


def _gather_weights(shards):
    names = list(shards)
    n = len(names)
    shapes = {nm: (BLOCK_LAYOUT[nm][0] if nm in BLOCK_LAYOUT else (N_DEV,) + shards[nm].shape) for nm in names}

    def body(*refs):
        ins = dict(zip(names, refs[:n]))
        outs = dict(zip(names, refs[n + 1:2 * n + 1]))
        send_sems, recv_sems, local_sems = refs[2 * n + 1:]
        x, y, c = _position()
        me, sibling = (x, y, c), (x, y, 1 - c)
        chips = [(1 - x, y), (x, 1 - y), (1 - x, 1 - y)]

        def copy(i, k, block, to, src=None):
            dst = _block_of(names[i], outs[names[i]], block)
            return pltpu.make_async_remote_copy(
                src_ref=dst if src is None else src, dst_ref=dst, send_sem=send_sems.at[7 * i + k],
                recv_sem=recv_sems.at[7 * i + k], device_id=to, device_id_type=MESH)

        local = [pltpu.make_async_copy(ins[nm], _block_of(nm, outs[nm], me), local_sems.at[i])
                 for i, nm in enumerate(names)]
        for cp in local:
            cp.start()
        first = []
        for i, nm in enumerate(names):
            first.append(copy(i, 0, me, sibling, src=ins[nm]))
            first += [copy(i, 1 + j, me, (*chip, c), src=ins[nm]) for j, chip in enumerate(chips)]
        for cp in first:
            cp.start()
        passed = []
        for j, chip in enumerate(chips):
            for i in range(n):
                copy(i, 1 + j, (*chip, c), me).wait_recv()
                fwd = copy(i, 4 + j, (*chip, c), sibling)
                fwd.start()
                passed.append(fwd)
        for i in range(n):
            copy(i, 0, sibling, me).wait_recv()
        for j, chip in enumerate(chips):
            for i in range(n):
                copy(i, 4 + j, (*chip, 1 - c), me).wait_recv()
        for cp in first + passed:
            cp.wait_send()
        for cp in local:
            cp.wait()

    any_spec = pl.BlockSpec(memory_space=pl.ANY)
    down = names.index('ffn_w_down')
    zeros = jnp.zeros(shapes['ffn_w_down'], shards['ffn_w_down'].dtype)
    outs = pl.pallas_call(
        body, name="gather_weights", out_shape=[_sds(shapes[nm], shards[nm].dtype) for nm in names],
        in_specs=[any_spec] * (n + 1), out_specs=[any_spec] * n, input_output_aliases={n: down},
        scratch_shapes=[pltpu.SemaphoreType.DMA((7 * n,)), pltpu.SemaphoreType.DMA((7 * n,)),
                        pltpu.SemaphoreType.DMA((n,))],
    )(*[shards[nm] for nm in names], zeros)
    return dict(zip(names, outs))


def _exchange_grads(fulls, rep):
    cpos = lax.axis_index("c").astype(jnp.int32).reshape(1)
    pair, rep_pair = _pair_exchange(fulls, rep)
    chip = {nm: _chip_sum(nm, fulls[nm], pair[nm], cpos) for nm in fulls}
    rep_chip = _add_pair(rep, rep_pair, "chip_sum_replicated")
    cross, cross_rep = _cross_exchange(chip, rep_chip)
    return chip, rep_chip, cross, cross_rep


def _pair_exchange(fulls, rep):
    names = list(fulls)
    n = len(names)
    shard_shape = {nm: (BLOCK_LAYOUT[nm][1] if nm in BLOCK_LAYOUT else fulls[nm].shape[1:]) for nm in names}

    def body(*refs):
        ins = dict(zip(names, refs[:n]))
        rep_ref = refs[n]
        pair = dict(zip(names, refs[n + 1:2 * n + 1]))
        rpair_ref = refs[2 * n + 1]
        send_sems, recv_sems = refs[2 * n + 2:]
        x, y, c = _position()
        sibling = (x, y, 1 - c)
        remote = []
        for i, nm in enumerate(names):
            for q in range(4):
                remote.append(pltpu.make_async_remote_copy(
                    src_ref=_block_of(nm, ins[nm], (q >> 1, q & 1, 1 - c)), dst_ref=pair[nm].at[q],
                    send_sem=send_sems.at[4 * i + q], recv_sem=recv_sems.at[4 * i + q], device_id=sibling,
                    device_id_type=MESH))
        remote.append(pltpu.make_async_remote_copy(
            src_ref=rep_ref, dst_ref=rpair_ref, send_sem=send_sems.at[4 * n], recv_sem=recv_sems.at[4 * n],
            device_id=sibling, device_id_type=MESH))
        for cp in remote:
            cp.start()
        for cp in remote:
            cp.wait_recv()
        for cp in remote:
            cp.wait_send()

    any_spec = pl.BlockSpec(memory_space=pl.ANY)
    four = [_sds((4,) + tuple(shard_shape[nm]), fulls[nm].dtype) for nm in names]
    outs = pl.pallas_call(
        body, name="grad_pair_exchange", out_shape=four + [_sds(rep.shape, rep.dtype)],
        in_specs=[any_spec] * (n + 1), out_specs=[any_spec] * (n + 1),
        scratch_shapes=[pltpu.SemaphoreType.DMA((4 * n + 1,)), pltpu.SemaphoreType.DMA((4 * n + 1,))],
    )(*[fulls[nm] for nm in names], rep)
    return dict(zip(names, outs[:n])), outs[n]


def _chip_sum(name, full, pair, cpos):
    if name in ('ab_w_in', 'c_w_in', 'ffn_w_up'):
        width = BLOCK_LAYOUT[name][1][-1]
        rows = full.shape[0] * full.shape[1]
        tr = 512

        def body(c_ref, f_ref, p_ref, o_ref):
            o_ref[0] = (f_ref[...].astype(f32) + p_ref[0].astype(f32)).astype(o_ref.dtype)

        slot = pl.BlockSpec((1, tr, width), lambda q, i, c: (q, i, 0))
        out = pl.pallas_call(
            body, name="chip_sum_" + name, out_shape=_sds((4, rows, width), full.dtype),
            grid_spec=pltpu.PrefetchScalarGridSpec(
                num_scalar_prefetch=1, grid=(4, rows // tr),
                in_specs=[pl.BlockSpec((tr, width), lambda q, i, c: (i, 2 * q + c[0])), slot], out_specs=slot),
            compiler_params=_cparams(("parallel", "parallel")))(
            cpos, full.reshape(rows, N_DEV * width), pair.reshape(4, rows, width))
        return out.reshape(pair.shape)

    if name == 'ffn_w_down':
        f4, p4 = full, pair
        fspec = pl.BlockSpec((DEPTH, 1, FF_ROWS, D_MODEL), lambda q, c: (0, q, c[0], 0))
    else:
        shard = pair.shape[1:]
        lead = int(np.prod(shard[:-2]))
        f4 = full.reshape((lead, N_DEV) + shard[-2:])
        p4 = pair.reshape((4, lead) + shard[-2:])
        fspec = pl.BlockSpec((lead, 1) + shard[-2:], lambda q, c: (0, 2 * q + c[0], 0, 0))

    def body4(c_ref, f_ref, p_ref, o_ref):
        o_ref[0] = (f_ref[:, 0].astype(f32) + p_ref[0].astype(f32)).astype(o_ref.dtype)

    slot = pl.BlockSpec((1,) + p4.shape[1:], lambda q, c: (q, 0, 0, 0))
    out = pl.pallas_call(
        body4, name="chip_sum_" + name, out_shape=_sds(p4.shape, full.dtype),
        grid_spec=pltpu.PrefetchScalarGridSpec(num_scalar_prefetch=1, grid=(4,), in_specs=[fspec, slot], out_specs=slot),
        compiler_params=_cparams(("parallel",)))(cpos, f4, p4)
    return out.reshape(pair.shape)


def _add_pair(a, b, name):
    shp = a.shape
    r, c = int(np.prod(shp[:-1])), shp[-1]
    tr = _tile(r, (512, 256, 128, 64, 32, 16, 8))

    def body(a_ref, b_ref, o_ref):
        o_ref[...] = (a_ref[...].astype(f32) + b_ref[...].astype(f32)).astype(o_ref.dtype)

    tile = pl.BlockSpec((tr, c), lambda i: (i, 0))
    return pl.pallas_call(body, name=name, grid=(r // tr,), in_specs=[tile, tile], out_specs=tile,
                          out_shape=_sds((r, c), a.dtype), compiler_params=_cparams(("parallel",)))(
        a.reshape(r, c), b.reshape(r, c)).reshape(shp)


def _cross_exchange(chip, rep_chip):
    names = list(chip)
    n = len(names)

    def body(*refs):
        ins = dict(zip(names, refs[:n]))
        rep_ref = refs[n]
        outs = dict(zip(names, refs[2 * n + 2:3 * n + 2]))
        rrep_ref = refs[3 * n + 2]
        send_sems, recv_sems = refs[3 * n + 3:]
        x, y, c = _position()
        mine = 2 * x + y
        copies = []
        for k in range(1, 4):
            px, py = (1 - x if (k >> 1) & 1 else x), (1 - y if k & 1 else y)
            for i, nm in enumerate(names + ['']):
                src = rep_ref if i == n else ins[nm].at[2 * px + py]
                dst = (rrep_ref if i == n else outs[nm]).at[mine]
                copies.append(pltpu.make_async_remote_copy(
                    src_ref=src, dst_ref=dst, send_sem=send_sems.at[3 * i + k - 1], recv_sem=recv_sems.at[3 * i + k - 1],
                    device_id=(px, py, c), device_id_type=MESH))
        for cp in copies:
            cp.start()
        for cp in copies:
            cp.wait_recv()
        for cp in copies:
            cp.wait_send()

    any_spec = pl.BlockSpec(memory_space=pl.ANY)
    shapes = [_sds(chip[nm].shape, chip[nm].dtype) for nm in names] + [_sds((4,) + rep_chip.shape, rep_chip.dtype)]
    zeros = [jnp.zeros(s.shape, s.dtype) for s in shapes]
    outs = pl.pallas_call(
        body, name="grad_cross_exchange", out_shape=shapes,
        in_specs=[any_spec] * (2 * n + 2), out_specs=[any_spec] * (n + 1),
        input_output_aliases={n + 1 + i: i for i in range(n + 1)},
        scratch_shapes=[pltpu.SemaphoreType.DMA((3 * (n + 1),)), pltpu.SemaphoreType.DMA((3 * (n + 1),))],
    )(*[chip[nm] for nm in names], rep_chip, *zeros)
    return dict(zip(names, outs[:n])), outs[n]


def _sum_adamw(parts, own, mine, w, m, v, name):
    r, l = w.shape
    lp = parts.shape[2]
    tr = _tile(r, (256, 128, 64, 32, 16, 8))
    c1 = 1.0 / (1.0 - ADAM_B1 ** ADAM_STEP)
    c2 = 1.0 / (1.0 - ADAM_B2 ** ADAM_STEP)

    def body(mine_ref, p_ref, o_ref, w_ref, m_ref, v_ref, g_ref, d_ref, nm_ref, nv_ref):
        mine_v = (o_ref[0] if own.ndim == 3 else o_ref[...]).astype(f32)
        g = jnp.where(mine_ref[0] == 0, mine_v, p_ref[0].astype(f32))
        for s in range(1, parts.shape[0]):
            g = g + jnp.where(mine_ref[0] == s, mine_v, p_ref[s].astype(f32))
        if lp != l:
            g = g[:, :l]
        m_new = ADAM_B1 * m_ref[...] + (1.0 - ADAM_B1) * g
        v_new = ADAM_B2 * v_ref[...] + (1.0 - ADAM_B2) * (g * g)
        g_ref[...] = g
        nm_ref[...] = m_new
        nv_ref[...] = v_new
        d_ref[...] = -ADAM_LR * ((m_new * c1) / (jnp.sqrt(v_new * c2) + ADAM_EPS) + ADAM_WD * w_ref[...])

    tile = pl.BlockSpec((tr, l), lambda i, mn: (i, 0))
    own_spec = (pl.BlockSpec((1, tr, lp), lambda i, mn: (mn[0], i, 0)) if own.ndim == 3
                else pl.BlockSpec((tr, lp), lambda i, mn: (i, 0)))
    return pl.pallas_call(
        body, name=name, out_shape=[_sds((r, l), f32)] * 4,
        grid_spec=pltpu.PrefetchScalarGridSpec(
            num_scalar_prefetch=1, grid=(r // tr,),
            in_specs=[pl.BlockSpec((parts.shape[0], tr, lp), lambda i, mn: (0, i, 0)), own_spec, tile, tile, tile],
            out_specs=[tile] * 4),
        compiler_params=_cparams(("parallel",)))(mine, parts, own, w, m, v)


def _pack(arrs, lead=None):
    if lead is None:
        flat = jnp.concatenate([a.reshape(-1).astype(f32) for a in arrs])
        n = flat.shape[0]
    else:
        flat = jnp.concatenate([a.reshape(lead, -1).astype(f32) for a in arrs], axis=1)
        n = flat.shape[1]
    tot = -(-n // 1024) * 1024
    if lead is None:
        return jnp.pad(flat, (0, tot - n)).reshape(tot // 128, 128)
    return jnp.pad(flat, ((0, 0), (0, tot - n))).reshape(lead, tot // 128, 128)


def _unpack(packed, shapes, lead=False):
    flat = packed.reshape(packed.shape[0], -1) if lead else packed.reshape(-1)
    out, off = [], 0
    for s in shapes:
        n = int(np.prod(s))
        out.append(flat[:, off:off + n].reshape((packed.shape[0],) + tuple(s)) if lead else flat[off:off + n].reshape(s))
        off += n
    return out


def _merge_shards(g, axis):
    g = jnp.moveaxis(g, 0, axis)
    s = g.shape
    return g.reshape(s[:axis] + (s[axis] * s[axis + 1],) + s[axis + 2:])


def _split_shards(full, axis):
    s = full.shape
    g = full.reshape(s[:axis] + (N_DEV, s[axis] // N_DEV) + s[axis + 1:])
    return jnp.moveaxis(g, axis, 0)


def kernel(x, norm_mix, norm_ffn, norm_final, ab_w_in, gdn_conv_w, gdn_a_log, gdn_dt_bias, gdn_norm, hgrn_lower_bounds, hgrn_norm, ab_w_out, c_w_in, c_conv_w, c_conv_b, c_gate_a_w, c_gate_a_b, c_gate_x_w, c_gate_x_b, c_lambda, c_w_out, ffn_w_up, ffn_conv_w, ffn_conv_b, ffn_w_down, loss_target, m_norm_mix, m_norm_ffn, m_norm_final, m_ab_w_in, m_gdn_conv_w, m_gdn_a_log, m_gdn_dt_bias, m_gdn_norm, m_hgrn_lower_bounds, m_hgrn_norm, m_ab_w_out, m_c_w_in, m_c_conv_w, m_c_conv_b, m_c_gate_a_w, m_c_gate_a_b, m_c_gate_x_w, m_c_gate_x_b, m_c_lambda, m_c_w_out, m_ffn_w_up, m_ffn_conv_w, m_ffn_conv_b, m_ffn_w_down, v_norm_mix, v_norm_ffn, v_norm_final, v_ab_w_in, v_gdn_conv_w, v_gdn_a_log, v_gdn_dt_bias, v_gdn_norm, v_hgrn_lower_bounds, v_hgrn_norm, v_ab_w_out, v_c_w_in, v_c_conv_w, v_c_conv_b, v_c_gate_a_w, v_c_gate_a_b, v_c_gate_x_w, v_c_gate_x_b, v_c_lambda, v_c_w_out, v_ffn_w_up, v_ffn_conv_w, v_ffn_conv_b, v_ffn_w_down):
    wl = dict(zip(WEIGHTS, (norm_mix, norm_ffn, norm_final, ab_w_in, gdn_conv_w, gdn_a_log, gdn_dt_bias, gdn_norm, hgrn_lower_bounds, hgrn_norm, ab_w_out, c_w_in, c_conv_w, c_conv_b, c_gate_a_w, c_gate_a_b, c_gate_x_w, c_gate_x_b, c_lambda, c_w_out, ffn_w_up, ffn_conv_w, ffn_conv_b, ffn_w_down)))
    ml = dict(zip(WEIGHTS, (m_norm_mix, m_norm_ffn, m_norm_final, m_ab_w_in, m_gdn_conv_w, m_gdn_a_log, m_gdn_dt_bias, m_gdn_norm, m_hgrn_lower_bounds, m_hgrn_norm, m_ab_w_out, m_c_w_in, m_c_conv_w, m_c_conv_b, m_c_gate_a_w, m_c_gate_a_b, m_c_gate_x_w, m_c_gate_x_b, m_c_lambda, m_c_w_out, m_ffn_w_up, m_ffn_conv_w, m_ffn_conv_b, m_ffn_w_down)))
    vl = dict(zip(WEIGHTS, (v_norm_mix, v_norm_ffn, v_norm_final, v_ab_w_in, v_gdn_conv_w, v_gdn_a_log, v_gdn_dt_bias, v_gdn_norm, v_hgrn_lower_bounds, v_hgrn_norm, v_ab_w_out, v_c_w_in, v_c_conv_w, v_c_conv_b, v_c_gate_a_w, v_c_gate_a_b, v_c_gate_x_w, v_c_gate_x_b, v_c_lambda, v_c_w_out, v_ffn_w_up, v_ffn_conv_w, v_ffn_conv_b, v_ffn_w_down)))

    big = [n for n in SHARDED if n in MATMUL_WEIGHTS]
    vec = [n for n in SHARDED if n not in MATMUL_WEIGHTS]
    shards = {n: wl[n].astype(bf16) for n in big}
    shards['ab_w_in'] = jnp.pad(shards['ab_w_in'], ((0, 0), (0, 0), (0, AB_SHARD_PAD - AB_SHARD)))
    shards['ffn_w_up'] = jnp.pad(shards['ffn_w_up'], ((0, 0), (0, 0), (0, FF_PAD - FF_SHARD)))
    shards['vec'] = _pack([wl[n] for n in vec])
    got = _gather_weights(shards)
    full = {n: wl[n] for n in REPLICATED}
    for n, a in zip(vec, _unpack(got['vec'], [wl[n].shape for n in vec], lead=True)):
        full[n] = _merge_shards(a, SHARD_AXIS[n])
    full['ffn_conv_w'] = _block_pad(full['ffn_conv_w'], 2, 4, FF_PAD)
    full['ffn_conv_b'] = _block_pad(full['ffn_conv_b'], 1, 4, FF_PAD)
    full['ab_w_in'] = _ab_permute(_block_unpad(got['ab_w_in'], 2, N_DEV, AB_SHARD))
    full['c_w_in'] = got['c_w_in']
    full['ffn_w_up'] = got['ffn_w_up']
    full['ffn_w_down'] = got['ffn_w_down'].reshape(DEPTH, D_FFP, D_MODEL)
    for n in ('ab_w_out', 'c_w_out'):
        full[n] = got[n].reshape(2, D_MODEL, D_MODEL)
    for n in ('c_gate_a_w', 'c_gate_x_w'):
        full[n] = got[n].reshape(2, HEADS, LRU_BLOCK, LRU_BLOCK)

    loss, dx, grads = _local_step(x[0], loss_target[0], full)

    grads['ffn_conv_w'] = _block_unpad(grads['ffn_conv_w'], 2, 4, FF_SHARD)
    grads['ffn_conv_b'] = _block_unpad(grads['ffn_conv_b'], 1, 4, FF_SHARD)
    fulls = {n: grads[n].astype(bf16).reshape(BLOCK_LAYOUT[n][0]) for n in big if n != 'ab_w_in'}
    fulls['ab_w_in'] = _block_pad(_ab_unpermute(grads['ab_w_in']), 2, N_DEV, AB_SHARD_PAD)
    fulls = {n: fulls[n] for n in big}
    fulls['vec'] = _pack([_split_shards(grads[n], SHARD_AXIS[n]) for n in vec], lead=N_DEV)
    chip, rep_chip, recv, rrep = _exchange_grads(fulls, _pack([grads[n] for n in REPLICATED]))
    mine = (2 * lax.axis_index("x") + lax.axis_index("y")).astype(jnp.int32).reshape(1)
    res = {}
    for n in big:
        shp = wl[n].shape
        r, c = int(np.prod(shp[:-1])), shp[-1]
        outs = _sum_adamw(recv[n].reshape(4, r, -1), chip[n].reshape(4, r, -1), mine, wl[n].reshape(r, c),
                          ml[n].reshape(r, c), vl[n].reshape(r, c), "adamw_" + n)
        for kind, o in zip(("grad", "delta", "new_m", "new_v"), outs):
            res[kind, n] = o.reshape(shp)
    for names, parts, own, tag in ((vec, recv['vec'], chip['vec'], "adamw_vectors"),
                                   (REPLICATED, rrep, rep_chip, "adamw_replicated")):
        outs = _sum_adamw(parts, own, mine, _pack([wl[n] for n in names]), _pack([ml[n] for n in names]),
                          _pack([vl[n] for n in names]), tag)
        for kind, o in zip(("grad", "delta", "new_m", "new_v"), outs):
            for n, a in zip(names, _unpack(o, [wl[n].shape for n in names])):
                res[kind, n] = a

    loss = lax.psum(loss, ("x", "y", "c"))
    return (loss, dx[None], *[res[kind, n] for kind in ("grad", "delta", "new_m", "new_v") for n in WEIGHTS])
```

```python
import functools

import numpy as np
import jax
import jax.numpy as jnp
from jax import lax
from jax.experimental import pallas as pl
from jax.experimental.pallas import tpu as pltpu

f32 = jnp.float32
bf16 = jnp.bfloat16
HI = lax.Precision.HIGHEST
MESH = pl.DeviceIdType.MESH

N_DEV = 8
D_MODEL = 1024
DEPTH = 4
EPS = 1e-6
F_FLOOR = 1e-30
HEADS = 4
HEAD_DIM = 128
GDN_WIDTH = 512
GDN_CONV = 4
GDN_CHUNK = 64
HGRN_CHUNK = 16
HGRN_STEP = 128
MIX_WIDTH = 1024
AB_COLS = 4104
AB_PAD = 4224
LRU_WIDTH = 1024
LRU_BLOCK = 256
LRU_CONV = 4
RG_C = 8.0
D_FF = 2816
FF_SHARD = 704
FF_PAD = 768
D_FFP = 4 * FF_PAD
FF_ROWS = 352
AB_SHARD, AB_SHARD_PAD = 513, 640
FFN_CONV = 3
ADAM_LR, ADAM_B1, ADAM_B2, ADAM_EPS, ADAM_WD, ADAM_STEP = 0.001, 0.9, 0.999, 1e-08, 0.01, 10
VMEM_LIMIT = 56 * 1024 * 1024
PACK_LANES = 512
PACK_ROWS = 256

OFF_Q, OFF_K, OFF_V, OFF_Z, OFF_QB, OFF_FB, OFF_IB, OFF_GB, OFF_BA = 0, 512, 1024, 1536, 2048, 2560, 3072, 3584, 4096

WEIGHTS = ['norm_mix', 'norm_ffn', 'norm_final', 'ab_w_in', 'gdn_conv_w', 'gdn_a_log', 'gdn_dt_bias', 'gdn_norm',
           'hgrn_lower_bounds', 'hgrn_norm', 'ab_w_out', 'c_w_in', 'c_conv_w', 'c_conv_b', 'c_gate_a_w', 'c_gate_a_b',
           'c_gate_x_w', 'c_gate_x_b', 'c_lambda', 'c_w_out', 'ffn_w_up', 'ffn_conv_w', 'ffn_conv_b', 'ffn_w_down']
SHARD_AXIS = {'norm_mix': None, 'norm_ffn': None, 'norm_final': None, 'ab_w_in': 2, 'gdn_conv_w': 2, 'gdn_a_log': None,
              'gdn_dt_bias': None, 'gdn_norm': None, 'hgrn_lower_bounds': None, 'hgrn_norm': None, 'ab_w_out': 1,
              'c_w_in': 2, 'c_conv_w': 2, 'c_conv_b': 1, 'c_gate_a_w': 2, 'c_gate_a_b': 1, 'c_gate_x_w': 2,
              'c_gate_x_b': 1, 'c_lambda': 1, 'c_w_out': 1, 'ffn_w_up': 2, 'ffn_conv_w': 2, 'ffn_conv_b': None,
              'ffn_w_down': 1}
MATMUL_WEIGHTS = ('ab_w_in', 'ab_w_out', 'c_w_in', 'c_gate_a_w', 'c_gate_x_w', 'c_w_out', 'ffn_w_up', 'ffn_w_down')
SHARDED = [n for n in WEIGHTS if SHARD_AXIS[n] is not None]
REPLICATED = [n for n in WEIGHTS if SHARD_AXIS[n] is None]


def _tile(n, prefs=(512, 384, 256, 128)):
    for p in prefs:
        if n % p == 0:
            return p
    return n


def _cparams(sem=None):
    kw = dict(vmem_limit_bytes=VMEM_LIMIT)
    if sem is not None:
        kw['dimension_semantics'] = sem
    return pltpu.CompilerParams(**kw)


def _sds(shape, dtype):
    return jax.ShapeDtypeStruct(tuple(shape), dtype)


def _sigmoid(x):
    return 1.0 / (1.0 + jnp.exp(-x))


def _silu(x):
    return x * _sigmoid(x)


def _log1p(x):
    u = 1.0 + x
    return jnp.where(u == 1.0, x, jnp.log(u) * (x / jnp.where(u == 1.0, 1.0, u - 1.0)))


def _softplus(x):
    return jnp.maximum(x, 0.0) + _log1p(jnp.exp(-jnp.abs(x)))


def _expm1(x):
    small = jnp.abs(x) < 0.05
    xs = jnp.where(small, x, 0.0)
    series = xs * (1.0 + xs * (0.5 + xs * (1.0 / 6.0 + xs * (1.0 / 24.0 + xs * (1.0 / 120.0)))))
    return jnp.where(small, series, jnp.exp(x) - 1.0)


def _gelu(x):
    return 0.5 * x * (1.0 + jnp.tanh(0.7978845608028654 * (x + 0.044715 * x * x * x)))


def _rms(x, gain):
    return x * lax.rsqrt(jnp.mean(x * x, axis=-1, keepdims=True) + EPS) * gain


def _dot(a, b, dims=((1,), (0,)), precision=None):
    return lax.dot_general(a, b, (dims, ((), ())), precision=precision, preferred_element_type=f32)


def _bdot(a, b, dims=((1,), (0,))):
    return _dot(a.astype(bf16), b.astype(bf16), dims)


NT = ((1,), (1,))
TN = ((0,), (0,))


def _shift_down(x, k):
    if k == 0:
        return x
    row = lax.broadcasted_iota(jnp.int32, x.shape, 0)
    return jnp.where(row >= k, pltpu.roll(x, k, 0), 0.0)


def _shift_up(x, k, fill=0.0):
    if k == 0:
        return x
    n = x.shape[0]
    row = lax.broadcasted_iota(jnp.int32, x.shape, 0)
    return jnp.where(row < n - k, pltpu.roll(x, n - k, 0), fill)


def _conv_fwd(x, w_ref, width):
    acc = w_ref[width - 1:width, :] * x
    for k in range(width - 1):
        acc = acc + w_ref[k:k + 1, :] * _shift_down(x, width - 1 - k)
    return acc


def _conv_bwd(x, dout, w_ref, dw_ref, width):
    dx = w_ref[width - 1:width, :] * dout
    dw_ref[width - 1:width, :] = jnp.sum(dout * x, axis=0, keepdims=True)
    for k in range(width - 1):
        s = width - 1 - k
        dx = dx + w_ref[k:k + 1, :] * _shift_up(dout, s)
        dw_ref[k:k + 1, :] = jnp.sum(dout * _shift_down(x, s), axis=0, keepdims=True)
    return dx


MM_VMEM_BUDGET = 36 * 1024 * 1024
MM_MAX_TILE = 1024 * 1024


def _mm_tiles(m, n, k, out_bytes):
    best = None
    for tm in (1024, 512, 384, 256, 128):
        if m % tm:
            continue
        for tn in range(1536, 0, -128):
            if n % tn or tm * tn > MM_MAX_TILE:
                continue
            if 2 * (tm * k * 2 + k * tn * 2 + tm * tn * out_bytes) <= MM_VMEM_BUDGET and (best is None or tm * tn > best[0]):
                best = (tm * tn, tm, tn)
    return (best[1], best[2]) if best else (_tile(m), _tile(n))


def _mm(a, b, *, ta=False, tb=False, add=None, out_dtype=f32, name):
    m, k = (a.shape[1], a.shape[0]) if ta else a.shape
    n = b.shape[0] if tb else b.shape[1]
    tm, tn = _mm_tiles(m, n, k, jnp.dtype(out_dtype).itemsize + (4 if add is not None else 0))
    dims = ((0 if ta else 1,), (1 if tb else 0,))

    def body(*refs):
        a_ref, b_ref = refs[0], refs[1]
        o_ref = refs[-1]
        r = _dot(a_ref[...], b_ref[...], dims)
        if add is not None:
            r = r + refs[2][...]
        o_ref[...] = r.astype(out_dtype)

    a_spec = pl.BlockSpec((k, tm), lambda j, i: (0, i)) if ta else pl.BlockSpec((tm, k), lambda j, i: (i, 0))
    b_spec = pl.BlockSpec((tn, k), lambda j, i: (j, 0)) if tb else pl.BlockSpec((k, tn), lambda j, i: (0, j))
    o_spec = pl.BlockSpec((tm, tn), lambda j, i: (i, j))
    ins, specs = [a, b], [a_spec, b_spec]
    if add is not None:
        ins.append(add)
        specs.append(o_spec)
    return pl.pallas_call(body, name=name, grid=(n // tn, m // tm), in_specs=specs, out_specs=o_spec,
                          out_shape=_sds((m, n), out_dtype), compiler_params=_cparams(("parallel", "parallel")))(*ins)


def _rms_fwd(x, gain, name):
    t, d = x.shape
    tr = _tile(t, (256, 128))

    def body(x_ref, g_ref, h_ref):
        h_ref[...] = _rms(x_ref[...], g_ref[...]).astype(bf16)

    return pl.pallas_call(body, name=name, grid=(t // tr,),
                          in_specs=[pl.BlockSpec((tr, d), lambda i: (i, 0)), pl.BlockSpec((1, d), lambda i: (0, 0))],
                          out_specs=pl.BlockSpec((tr, d), lambda i: (i, 0)), out_shape=_sds((t, d), bf16),
                          compiler_params=_cparams(("parallel",)))(x, gain)


def _rms_bwd(x, gain, dh, dres, name):
    t, d = x.shape
    tr = _tile(t, (256, 128))

    def body(x_ref, g_ref, dh_ref, dres_ref, dx_ref, dxb_ref, dg_ref):
        _, vjp = jax.vjp(_rms, x_ref[...], g_ref[...])
        dx, dg = vjp(dh_ref[...])
        dx = dx + dres_ref[...]
        dx_ref[...] = dx
        dxb_ref[...] = dx.astype(bf16)

        @pl.when(pl.program_id(0) == 0)
        def _():
            dg_ref[...] = jnp.zeros_like(dg_ref)

        dg_ref[...] += dg

    row = pl.BlockSpec((tr, d), lambda i: (i, 0))
    vec = pl.BlockSpec((1, d), lambda i: (0, 0))
    return pl.pallas_call(body, name=name, grid=(t // tr,), in_specs=[row, vec, row, row], out_specs=[row, row, vec],
                          out_shape=[_sds((t, d), f32), _sds((t, d), bf16), _sds((1, d), f32)],
                          compiler_params=_cparams(("arbitrary",)))(x, gain, dh, dres)


def _loss_head(x, gain, target):
    t, d = x.shape
    tr = _tile(t, (256, 128))

    def f(xv, g, tgt):
        err = _rms(xv, g) - tgt
        return 0.5 * jnp.sum(jnp.mean(err * err, axis=-1, keepdims=True), axis=0, keepdims=True)

    def body(x_ref, g_ref, t_ref, loss_ref, dx_ref, dxb_ref, dg_ref):
        loss, vjp = jax.vjp(lambda xv, g: f(xv, g, t_ref[...]), x_ref[...], g_ref[...])
        dx, dg = vjp(jnp.ones((1, 1), f32))
        dx_ref[...] = dx
        dxb_ref[...] = dx.astype(bf16)

        @pl.when(pl.program_id(0) == 0)
        def _():
            dg_ref[...] = jnp.zeros_like(dg_ref)
            loss_ref[...] = jnp.zeros_like(loss_ref)

        dg_ref[...] += dg
        loss_ref[...] += jnp.broadcast_to(loss, loss_ref.shape)

    row = pl.BlockSpec((tr, d), lambda i: (i, 0))
    vec = pl.BlockSpec((1, d), lambda i: (0, 0))
    one = pl.BlockSpec((8, 128), lambda i: (0, 0))
    return pl.pallas_call(body, name="loss_head", grid=(t // tr,), in_specs=[row, vec, row],
                          out_specs=[one, row, row, vec],
                          out_shape=[_sds((8, 128), f32), _sds((t, d), f32), _sds((t, d), bf16), _sds((1, d), f32)],
                          compiler_params=_cparams(("arbitrary",)))(x, gain, target)


def _ffn_act_fwd(u, conv_w, conv_b):
    t = u.shape[0]
    tc = FF_PAD // 2
    nb = D_FFP // tc

    def body(g_ref, v_ref, w_ref, b_ref, a_ref):
        gc = _conv_fwd(g_ref[...], w_ref, FFN_CONV) + b_ref[...]
        a_ref[...] = (_silu(gc) * v_ref[...]).astype(bf16)

    return pl.pallas_call(
        body, name="ffn_act_fwd", grid=(nb,),
        in_specs=[pl.BlockSpec((t, tc), lambda j: (0, j)), pl.BlockSpec((t, tc), lambda j: (0, j + nb)),
                  pl.BlockSpec((FFN_CONV, tc), lambda j: (0, j)), pl.BlockSpec((1, tc), lambda j: (0, j))],
        out_specs=pl.BlockSpec((t, tc), lambda j: (0, j)), out_shape=_sds((t, D_FFP), bf16),
        compiler_params=_cparams(("parallel",)))(u, u, conv_w, conv_b)


def _ffn_act_bwd(u, conv_w, conv_b, da):
    t = u.shape[0]
    tc = FF_PAD // 2
    nb = D_FFP // tc

    def act(gc, val):
        return _silu(gc) * val

    def body(g_ref, v_ref, w_ref, b_ref, da_ref, a_ref, dg_ref, dv_ref, dw_ref, db_ref):
        gp = g_ref[...]
        gc = _conv_fwd(gp, w_ref, FFN_CONV) + b_ref[...]
        a, vjp = jax.vjp(act, gc, v_ref[...])
        dgc, dval = vjp(da_ref[...])
        a_ref[...] = a.astype(bf16)
        dv_ref[...] = dval.astype(bf16)
        db_ref[...] = jnp.sum(dgc, axis=0, keepdims=True)
        dg_ref[...] = _conv_bwd(gp, dgc, w_ref, dw_ref, FFN_CONV).astype(bf16)

    col = pl.BlockSpec((t, tc), lambda j: (0, j))
    return pl.pallas_call(
        body, name="ffn_act_bwd", grid=(nb,),
        in_specs=[col, pl.BlockSpec((t, tc), lambda j: (0, j + nb)), pl.BlockSpec((FFN_CONV, tc), lambda j: (0, j)),
                  pl.BlockSpec((1, tc), lambda j: (0, j)), col],
        out_specs=[col, col, col, pl.BlockSpec((FFN_CONV, tc), lambda j: (0, j)), pl.BlockSpec((1, tc), lambda j: (0, j))],
        out_shape=[_sds((t, D_FFP), bf16), _sds((t, D_FFP), bf16), _sds((t, D_FFP), bf16), _sds((FFN_CONV, D_FFP), f32),
                   _sds((1, D_FFP), f32)],
        compiler_params=_cparams(("parallel",)))(u, u, conv_w, conv_b, da)


def _lru_gates(xc, ra, ia, lam):
    r = _sigmoid(ra)
    i = _sigmoid(ia)
    log_a = -RG_C * r * _softplus(-lam)
    a = jnp.exp(log_a)
    u = jnp.sqrt(jnp.maximum(-_expm1(2.0 * log_a), 0.0)) * (i * xc)
    return a, u


def _lin_scan(a, u):
    n = a.shape[0]
    row = lax.broadcasted_iota(jnp.int32, a.shape, 0)
    s = 1
    while s < n:
        keep = row >= s
        u = a * jnp.where(keep, pltpu.roll(u, s, 0), 0.0) + u
        a = a * jnp.where(keep, pltpu.roll(a, s, 0), 1.0)
        s *= 2
    return u


def _rev_scan(a_next, d):
    n = d.shape[0]
    row = lax.broadcasted_iota(jnp.int32, d.shape, 0)
    a = a_next
    s = 1
    while s < n:
        keep = row < n - s
        d = a * jnp.where(keep, pltpu.roll(d, n - s, 0), 0.0) + d
        a = a * jnp.where(keep, pltpu.roll(a, n - s, 0), 1.0)
        s *= 2
    return d


def _col_conv_fwd(p, col_off, conv_w, conv_b, width, tc, name):
    t = p.shape[0]
    c = conv_w.shape[1]
    ob = col_off // tc

    def body(x_ref, w_ref, b_ref, o_ref):
        o_ref[...] = _conv_fwd(x_ref[...], w_ref, width) + b_ref[...]

    return pl.pallas_call(
        body, name=name, grid=(c // tc,),
        in_specs=[pl.BlockSpec((t, tc), lambda j: (0, j + ob)), pl.BlockSpec((width, tc), lambda j: (0, j)),
                  pl.BlockSpec((1, tc), lambda j: (0, j))],
        out_specs=pl.BlockSpec((t, tc), lambda j: (0, j)), out_shape=_sds((t, c), f32),
        compiler_params=_cparams(("parallel",)))(p, conv_w, conv_b)


def _col_conv_bwd(p, col_off, conv_w, dxc, width, tc, name):
    t = p.shape[0]
    c = conv_w.shape[1]
    ob = col_off // tc

    def body(x_ref, w_ref, d_ref, dx_ref, dw_ref, db_ref):
        d = d_ref[...]
        db_ref[...] = jnp.sum(d, axis=0, keepdims=True)
        dx_ref[...] = _conv_bwd(x_ref[...], d, w_ref, dw_ref, width).astype(bf16)

    col = pl.BlockSpec((t, tc), lambda j: (0, j))
    return pl.pallas_call(
        body, name=name, grid=(c // tc,),
        in_specs=[pl.BlockSpec((t, tc), lambda j: (0, j + ob)), pl.BlockSpec((width, tc), lambda j: (0, j)), col],
        out_specs=[col, pl.BlockSpec((width, tc), lambda j: (0, j)), pl.BlockSpec((1, tc), lambda j: (0, j))],
        out_shape=[_sds((t, c), bf16), _sds((width, c), f32), _sds((1, c), f32)],
        compiler_params=_cparams(("parallel",)))(p, conv_w, dxc)


def _lru_fwd(p, xc, wa, ba, wx, bx, lam):
    t = p.shape[0]
    bw = LRU_BLOCK

    def body(y_ref, xc_ref, wa_ref, ba_ref, wx_ref, bx_ref, lam_ref, out_ref, hs_ref, a_ref):
        xc_v = xc_ref[...]
        xb = xc_v.astype(bf16)
        ra = _dot(xb, wa_ref[0]) + ba_ref[...]
        ia = _dot(xb, wx_ref[0]) + bx_ref[...]
        a, u = _lru_gates(xc_v, ra, ia, lam_ref[...])
        a_ref[...] = a
        hs = _lin_scan(a, u)
        hs_ref[...] = hs
        out_ref[...] = (hs * _gelu(y_ref[...])).astype(bf16)

    col = pl.BlockSpec((t, bw), lambda h: (0, h))
    vec = pl.BlockSpec((1, bw), lambda h: (0, h))
    mat = pl.BlockSpec((1, bw, bw), lambda h: (h, 0, 0))
    return pl.pallas_call(
        body, name="lru_fwd", grid=(HEADS,), in_specs=[col, col, mat, vec, mat, vec, vec], out_specs=[col, col, col],
        out_shape=[_sds((t, LRU_WIDTH), bf16), _sds((t, LRU_WIDTH), f32), _sds((t, LRU_WIDTH), f32)],
        compiler_params=_cparams(("parallel",)))(p, xc, wa, ba, wx, bx, lam)


def _lru_bwd_scan(p, a, hs, dout):
    t = p.shape[0]
    bw = LRU_BLOCK

    def body(y_ref, a_ref, hs_ref, do_ref, dy_ref, da_ref, du_ref):
        hs_v = hs_ref[...]
        do = do_ref[...]
        gate, vjp = jax.vjp(_gelu, y_ref[...])
        dy_ref[...] = vjp(do * hs_v)[0].astype(bf16)
        g = _rev_scan(_shift_up(a_ref[...], 1), do * gate)
        du_ref[...] = g
        da_ref[...] = g * _shift_down(hs_v, 1)

    col = pl.BlockSpec((t, bw), lambda h: (0, h))
    return pl.pallas_call(
        body, name="lru_bwd_scan", grid=(HEADS,), in_specs=[col, col, col, col], out_specs=[col, col, col],
        out_shape=[_sds((t, LRU_WIDTH), bf16), _sds((t, LRU_WIDTH), f32), _sds((t, LRU_WIDTH), f32)],
        compiler_params=_cparams(("parallel",)))(p, a, hs, dout)


def _lru_bwd_gates(xc, da, du, wa, ba, wx, bx, lam):
    t = xc.shape[0]
    bw = LRU_BLOCK
    tr = _tile(t, (512, 256, 128))

    def body(xc_ref, da_ref, du_ref, wa_ref, ba_ref, wx_ref, bx_ref, lam_ref,
             dxc_ref, dwa_ref, dwx_ref, dba_ref, dbx_ref, dlam_ref):
        xc_v = xc_ref[...]
        xb = xc_v.astype(bf16)
        ra = _dot(xb, wa_ref[0]) + ba_ref[...]
        ia = _dot(xb, wx_ref[0]) + bx_ref[...]
        _, vjp = jax.vjp(_lru_gates, xc_v, ra, ia, lam_ref[...])
        dxc, dra, dia, dlam = vjp((da_ref[...], du_ref[...]))
        drb, dib = dra.astype(bf16), dia.astype(bf16)
        dxc_ref[...] = dxc + _dot(drb, wa_ref[0], NT) + _dot(dib, wx_ref[0], NT)

        @pl.when(pl.program_id(1) == 0)
        def _():
            dwa_ref[...] = jnp.zeros_like(dwa_ref)
            dwx_ref[...] = jnp.zeros_like(dwx_ref)
            dba_ref[...] = jnp.zeros_like(dba_ref)
            dbx_ref[...] = jnp.zeros_like(dbx_ref)
            dlam_ref[...] = jnp.zeros_like(dlam_ref)

        dwa_ref[0] += _dot(xb, drb, TN)
        dwx_ref[0] += _dot(xb, dib, TN)
        dba_ref[...] += jnp.sum(dra, axis=0, keepdims=True)
        dbx_ref[...] += jnp.sum(dia, axis=0, keepdims=True)
        dlam_ref[...] += dlam

    tile = pl.BlockSpec((tr, bw), lambda h, i: (i, h))
    vec = pl.BlockSpec((1, bw), lambda h, i: (0, h))
    mat = pl.BlockSpec((1, bw, bw), lambda h, i: (h, 0, 0))
    return pl.pallas_call(
        body, name="lru_bwd_gates", grid=(HEADS, t // tr), in_specs=[tile, tile, tile, mat, vec, mat, vec, vec],
        out_specs=[tile, mat, mat, vec, vec, vec],
        out_shape=[_sds((t, LRU_WIDTH), f32), _sds((HEADS, bw, bw), f32), _sds((HEADS, bw, bw), f32),
                   _sds((1, LRU_WIDTH), f32), _sds((1, LRU_WIDTH), f32), _sds((1, LRU_WIDTH), f32)],
        compiler_params=_cparams(("parallel", "arbitrary")))(xc, da, du, wa, ba, wx, bx, lam)


def _gdn_pre_fn(cq, ck, cv, ba, alog, dtb, h):
    q, k, v = _silu(cq), _silu(ck), _silu(cv)
    q = q * lax.rsqrt(jnp.sum(q * q, axis=-1, keepdims=True) + EPS) * (HEAD_DIM ** -0.5)
    k = k * lax.rsqrt(jnp.sum(k * k, axis=-1, keepdims=True) + EPS)
    lane = lax.broadcasted_iota(jnp.int32, (1, HEAD_DIM), 1)
    mb = (lane == h).astype(f32)
    ma = (lane == HEADS + h).astype(f32)
    beta_raw = jnp.sum(ba * mb, axis=-1, keepdims=True)
    alpha = jnp.sum(ba * ma, axis=-1, keepdims=True)
    al = jnp.sum(alog * mb, axis=-1, keepdims=True)
    db = jnp.sum(dtb * mb, axis=-1, keepdims=True)
    beta = _sigmoid(beta_raw)
    g = -jnp.exp(al) * _softplus(alpha + db)
    return q, k, v, jnp.broadcast_to(beta, q.shape), jnp.broadcast_to(g, q.shape)


def _gdn_pre_fwd(p, conv_w, alog, dtb):
    t = p.shape[0]
    hd = HEAD_DIM

    def body(pq_ref, pk_ref, pv_ref, ba_ref, wq_ref, wk_ref, wv_ref, al_ref, dt_ref, q_ref, k_ref, v_ref, b_ref, g_ref):
        h = pl.program_id(0)
        cq = _conv_fwd(pq_ref[...], wq_ref, GDN_CONV)
        ck = _conv_fwd(pk_ref[...], wk_ref, GDN_CONV)
        cv = _conv_fwd(pv_ref[...], wv_ref, GDN_CONV)
        q, k, v, be, ge = _gdn_pre_fn(cq, ck, cv, ba_ref[...], al_ref[...], dt_ref[...], h)
        q_ref[...], k_ref[...], v_ref[...], b_ref[...], g_ref[...] = q, k, v, be, ge

    def pcol(off):
        return pl.BlockSpec((t, hd), lambda h: (0, h + off // hd))

    def wcol(off):
        return pl.BlockSpec((GDN_CONV, hd), lambda h: (0, h + off // hd))

    vec = pl.BlockSpec((1, hd), lambda h: (0, 0))
    out = pl.BlockSpec((t, hd), lambda h: (0, h))
    return pl.pallas_call(
        body, name="gdn_pre_fwd", grid=(HEADS,),
        in_specs=[pcol(OFF_Q), pcol(OFF_K), pcol(OFF_V), pl.BlockSpec((t, hd), lambda h: (0, OFF_BA // hd)),
                  wcol(0), wcol(GDN_WIDTH), wcol(2 * GDN_WIDTH), vec, vec],
        out_specs=[out] * 5, out_shape=[_sds((t, GDN_WIDTH), f32)] * 5,
        compiler_params=_cparams(("parallel",)))(p, p, p, p, conv_w, conv_w, conv_w, alog, dtb)


def _gdn_pre_bwd(p, conv_w, alog, dtb, dq, dk, dv, dbe, dge):
    t = p.shape[0]
    hd = HEAD_DIM

    def body(pq_ref, pk_ref, pv_ref, ba_ref, wq_ref, wk_ref, wv_ref, al_ref, dt_ref,
             dq_ref, dk_ref, dv_ref, dbe_ref, dge_ref,
             opq_ref, opk_ref, opv_ref, dba_ref, dwq_ref, dwk_ref, dwv_ref, dal_ref, ddt_ref):
        h = pl.program_id(0)
        pq, pk, pv = pq_ref[...], pk_ref[...], pv_ref[...]
        cq = _conv_fwd(pq, wq_ref, GDN_CONV)
        ck = _conv_fwd(pk, wk_ref, GDN_CONV)
        cv = _conv_fwd(pv, wv_ref, GDN_CONV)
        _, vjp = jax.vjp(functools.partial(_gdn_pre_fn, h=h), cq, ck, cv, ba_ref[...], al_ref[...], dt_ref[...])
        dcq, dck, dcv, dba, dal, ddt = vjp((dq_ref[...], dk_ref[...], dv_ref[...], dbe_ref[...], dge_ref[...]))
        opq_ref[...] = _conv_bwd(pq, dcq, wq_ref, dwq_ref, GDN_CONV).astype(bf16)
        opk_ref[...] = _conv_bwd(pk, dck, wk_ref, dwk_ref, GDN_CONV).astype(bf16)
        opv_ref[...] = _conv_bwd(pv, dcv, wv_ref, dwv_ref, GDN_CONV).astype(bf16)

        @pl.when(h == 0)
        def _():
            dba_ref[...] = jnp.zeros_like(dba_ref)
            dal_ref[...] = jnp.zeros_like(dal_ref)
            ddt_ref[...] = jnp.zeros_like(ddt_ref)

        dba_ref[...] += dba
        dal_ref[...] += dal
        ddt_ref[...] += ddt

    def pcol(off):
        return pl.BlockSpec((t, hd), lambda h: (0, h + off // hd))

    def wcol(off):
        return pl.BlockSpec((GDN_CONV, hd), lambda h: (0, h + off // hd))

    vec = pl.BlockSpec((1, hd), lambda h: (0, 0))
    col = pl.BlockSpec((t, hd), lambda h: (0, h))
    full = pl.BlockSpec((t, hd), lambda h: (0, 0))
    wout = pl.BlockSpec((GDN_CONV, hd), lambda h: (0, h))
    return pl.pallas_call(
        body, name="gdn_pre_bwd", grid=(HEADS,),
        in_specs=[pcol(OFF_Q), pcol(OFF_K), pcol(OFF_V), pl.BlockSpec((t, hd), lambda h: (0, OFF_BA // hd)),
                  wcol(0), wcol(GDN_WIDTH), wcol(2 * GDN_WIDTH), vec, vec, col, col, col, col, col],
        out_specs=[col, col, col, full, wout, wout, wout, vec, vec],
        out_shape=[_sds((t, GDN_WIDTH), bf16)] * 3 + [_sds((t, hd), f32)] + [_sds((GDN_CONV, GDN_WIDTH), f32)] * 3
        + [_sds((1, hd), f32)] * 2,
        compiler_params=_cparams(("arbitrary",)))(p, p, p, p, conv_w, conv_w, conv_w, alog, dtb, dq, dk, dv, dbe, dge)


BNN = (((2,), (1,)), ((0,), (0,)))
BNT = (((2,), (2,)), ((0,), (0,)))
BTN = (((1,), (1,)), ((0,), (0,)))


def _hdot(a, b, dn=BNN, precision=None):
    return lax.dot_general(a, b, dn, precision=precision, preferred_element_type=f32)


def _hbdot(a, b, dn=BNN):
    return _hdot(a.astype(bf16), b.astype(bf16), dn)


def _tri_inverse(a):
    c = a.shape[-1]
    r = lax.broadcasted_iota(jnp.int32, (c, c), 0)
    col = lax.broadcasted_iota(jnp.int32, (c, c), 1)
    m = -a
    inv = jnp.where(r == col, 1.0, 0.0) + m
    s = 2
    while s < c:
        m = _hdot(m, m, precision=HI)
        inv = inv + _hdot(inv, m, precision=HI)
        s *= 2
    return inv


def _gdn_chunk(s, q, k, v, ge, be):
    nh, c, _ = q.shape
    r = lax.broadcasted_iota(jnp.int32, (c, c), 0)
    col = lax.broadcasted_iota(jnp.int32, (c, c), 1)
    causal = r >= col
    tri = jnp.broadcast_to(causal.astype(f32), (nh, c, c))
    gc = _hdot(tri, ge, precision=HI)
    gcc = gc[:, :, :c]
    gcr = jnp.swapaxes(gc, 1, 2)[:, :c, :]
    decay = jnp.where(causal, jnp.exp(jnp.where(causal, gcc - gcr, 0.0)), 0.0)
    kb = k * be
    lower = jnp.where(r > col, _hbdot(kb, k, BNT) * decay, 0.0)
    tinv = _tri_inverse(lower)
    egc = jnp.exp(gc)
    u = _hdot(tinv, v * be, precision=HI)
    w = _hdot(tinv, kb * egc, precision=HI)
    attn = _hbdot(q, k, BNT) * decay
    gl = gc[:, c - 1:c, :]
    v_new = u - _hbdot(w, s)
    o = _hbdot(q * egc, s) + _hbdot(attn, v_new)
    s_new = s * jnp.exp(gl) + _hbdot(k * jnp.exp(gl - gc), v_new, BTN)
    return o, s_new


def _heads_major(ref):
    return jnp.stack([ref[:, h * HEAD_DIM:(h + 1) * HEAD_DIM] for h in range(HEADS)])


def _gdn_core_fwd(q, k, v, ge, be):
    t = q.shape[0]
    c, hd = GDN_CHUNK, HEAD_DIM
    n = t // c

    def body(q_ref, k_ref, v_ref, g_ref, b_ref, o_ref, st_ref, s_ref):
        @pl.when(pl.program_id(0) == 0)
        def _():
            s_ref[...] = jnp.zeros_like(s_ref)

        s = s_ref[...]
        st_ref[:, 0] = s
        o, s_new = _gdn_chunk(s, *[_heads_major(r) for r in (q_ref, k_ref, v_ref, g_ref, b_ref)])
        for h in range(HEADS):
            o_ref[:, h * hd:(h + 1) * hd] = o[h]
        s_ref[...] = s_new

    tile = pl.BlockSpec((c, GDN_WIDTH), lambda i: (i, 0))
    return pl.pallas_call(
        body, name="gdn_core_fwd", grid=(n,), in_specs=[tile] * 5,
        out_specs=[tile, pl.BlockSpec((HEADS, 1, hd, hd), lambda i: (0, i, 0, 0))],
        out_shape=[_sds((t, GDN_WIDTH), f32), _sds((HEADS, n, hd, hd), f32)],
        scratch_shapes=[pltpu.VMEM((HEADS, hd, hd), f32)],
        compiler_params=_cparams(("arbitrary",)))(q, k, v, ge, be)


def _gdn_core_bwd(q, k, v, ge, be, states, do):
    t = q.shape[0]
    c, hd = GDN_CHUNK, HEAD_DIM
    n = t // c

    def body(q_ref, k_ref, v_ref, g_ref, b_ref, st_ref, do_ref, dq_ref, dk_ref, dv_ref, dg_ref, db_ref, ds_ref):
        @pl.when(pl.program_id(0) == 0)
        def _():
            ds_ref[...] = jnp.zeros_like(ds_ref)

        _, vjp = jax.vjp(_gdn_chunk, st_ref[:, 0], *[_heads_major(r) for r in (q_ref, k_ref, v_ref, g_ref, b_ref)])
        ds, *dins = vjp((_heads_major(do_ref), ds_ref[...]))
        ds_ref[...] = ds
        for d_ref, d in zip((dq_ref, dk_ref, dv_ref, dg_ref, db_ref), dins):
            for h in range(HEADS):
                d_ref[:, h * hd:(h + 1) * hd] = d[h]

    tile = pl.BlockSpec((c, GDN_WIDTH), lambda i: (n - 1 - i, 0))
    return pl.pallas_call(
        body, name="gdn_core_bwd", grid=(n,),
        in_specs=[tile] * 5 + [pl.BlockSpec((HEADS, 1, hd, hd), lambda i: (0, n - 1 - i, 0, 0)), tile],
        out_specs=[tile] * 5, out_shape=[_sds((t, GDN_WIDTH), f32)] * 5,
        scratch_shapes=[pltpu.VMEM((HEADS, hd, hd), f32)],
        compiler_params=_cparams(("arbitrary",)))(q, k, v, ge, be, states, do)


def _post_fn(o, z, gain):
    return _rms(o, gain) * _silu(z)


def _post_fwd(o, p, z_off, gain, name):
    t = o.shape[0]
    hd = HEAD_DIM

    def body(o_ref, z_ref, g_ref, y_ref):
        y_ref[...] = _post_fn(o_ref[...], z_ref[...], g_ref[...]).astype(bf16)

    col = pl.BlockSpec((t, hd), lambda h: (0, h))
    return pl.pallas_call(
        body, name=name, grid=(HEADS,),
        in_specs=[col, pl.BlockSpec((t, hd), lambda h: (0, h + z_off // hd)), pl.BlockSpec((1, hd), lambda h: (0, 0))],
        out_specs=col, out_shape=_sds((t, HEADS * hd), bf16), compiler_params=_cparams(("parallel",)))(o, p, gain)


def _post_bwd(o, p, z_off, gain, dmix, mix_off, name):
    t = o.shape[0]
    hd = HEAD_DIM

    def body(o_ref, z_ref, g_ref, dy_ref, do_ref, dz_ref, dg_ref):
        _, vjp = jax.vjp(_post_fn, o_ref[...], z_ref[...], g_ref[...])
        do, dz, dg = vjp(dy_ref[...])
        do_ref[...] = do
        dz_ref[...] = dz.astype(bf16)

        @pl.when(pl.program_id(0) == 0)
        def _():
            dg_ref[...] = jnp.zeros_like(dg_ref)

        dg_ref[...] += dg

    col = pl.BlockSpec((t, hd), lambda h: (0, h))
    vec = pl.BlockSpec((1, hd), lambda h: (0, 0))
    return pl.pallas_call(
        body, name=name, grid=(HEADS,),
        in_specs=[col, pl.BlockSpec((t, hd), lambda h: (0, h + z_off // hd)), vec,
                  pl.BlockSpec((t, hd), lambda h: (0, h + mix_off // hd))],
        out_specs=[col, col, vec], out_shape=[_sds((t, HEADS * hd), f32), _sds((t, HEADS * hd), bf16), _sds((1, hd), f32)],
        compiler_params=_cparams(("arbitrary",)))(o, p, gain, dmix)


def _hgrn_pre_fn(qb, fb, lbw, layer):
    l0, l1 = lbw[0:1, :], lbw[1:2, :]
    m = jnp.maximum(l0, l1)
    e0, e1 = jnp.exp(l0 - m), jnp.exp(l1 - m)
    p0, p1 = e0 / (e0 + e1), e1 / (e0 + e1)
    lb = (p0 - p0) if layer == 0 else ((p0 + p1) - p0)
    f = lb + (1.0 - lb) * _sigmoid(fb)
    return _silu(qb), 1.0 - f, jnp.log(jnp.maximum(f, F_FLOOR))


def _hgrn_pre_fwd(p, lbw, layer):
    t = p.shape[0]
    tc = HEAD_DIM

    def body(qb_ref, fb_ref, lb_ref, q_ref, k_ref, lf_ref):
        q_ref[...], k_ref[...], lf_ref[...] = _hgrn_pre_fn(qb_ref[...], fb_ref[...], lb_ref[...], layer)

    col = pl.BlockSpec((t, tc), lambda j: (0, j))
    return pl.pallas_call(
        body, name="hgrn_pre_fwd", grid=(GDN_WIDTH // tc,),
        in_specs=[pl.BlockSpec((t, tc), lambda j: (0, j + OFF_QB // tc)), pl.BlockSpec((t, tc), lambda j: (0, j + OFF_FB // tc)),
                  pl.BlockSpec((2, tc), lambda j: (0, j))],
        out_specs=[col] * 3, out_shape=[_sds((t, GDN_WIDTH), f32)] * 3,
        compiler_params=_cparams(("parallel",)))(p, p, lbw)


def _hgrn_pre_bwd(p, lbw, layer, dq, dk, dlf):
    t = p.shape[0]
    tc = HEAD_DIM

    def body(qb_ref, fb_ref, lb_ref, dq_ref, dk_ref, dlf_ref, dqb_ref, dfb_ref, dlb_ref):
        _, vjp = jax.vjp(functools.partial(_hgrn_pre_fn, layer=layer), qb_ref[...], fb_ref[...], lb_ref[...])
        dqb, dfb, dlb = vjp((dq_ref[...], dk_ref[...], dlf_ref[...]))
        dqb_ref[...] = dqb.astype(bf16)
        dfb_ref[...] = dfb.astype(bf16)
        dlb_ref[...] = dlb

    col = pl.BlockSpec((t, tc), lambda j: (0, j))
    lb = pl.BlockSpec((2, tc), lambda j: (0, j))
    return pl.pallas_call(
        body, name="hgrn_pre_bwd", grid=(GDN_WIDTH // tc,),
        in_specs=[pl.BlockSpec((t, tc), lambda j: (0, j + OFF_QB // tc)), pl.BlockSpec((t, tc), lambda j: (0, j + OFF_FB // tc)),
                  lb, col, col, col],
        out_specs=[col, col, lb], out_shape=[_sds((t, GDN_WIDTH), bf16)] * 2 + [_sds((2, GDN_WIDTH), f32)],
        compiler_params=_cparams(("parallel",)))(p, p, lbw, dq, dk, dlf)


def _hgrn_step(st, q, k, lf, v):
    c = HGRN_CHUNK
    nh = q.shape[0]
    r2 = lax.broadcasted_iota(jnp.int32, (c, c), 0)
    c2 = lax.broadcasted_iota(jnp.int32, (c, c), 1)
    tri = jnp.broadcast_to((r2 >= c2).astype(f32), (nh, c, c))
    i3 = lax.broadcasted_iota(jnp.int32, (c, c, HEAD_DIM), 0)
    j3 = lax.broadcasted_iota(jnp.int32, (c, c, HEAD_DIM), 1)
    mask = i3 >= j3
    outs = []
    for n in range(q.shape[1] // c):
        sl = slice(n * c, (n + 1) * c)
        qc, kc, lc, vc = q[:, sl], k[:, sl], lf[:, sl], v[:, sl]
        b = _hdot(tri, lc, precision=HI)
        rel = jnp.where(mask, jnp.exp(jnp.where(mask, b[:, :, None, :] - b[:, None, :, :], 0.0)), 0.0)
        scores = jnp.sum(qc[:, :, None, :] * kc[:, None, :, :] * rel, axis=-1)
        bl = b[:, c - 1:c, :]
        o = _hbdot(scores, vc) + _hbdot(qc * jnp.exp(b), st, BNT)
        st = st * jnp.exp(bl) + _hbdot(vc, kc * jnp.exp(bl - b), BTN)
        outs.append(o)
    return jnp.concatenate(outs, axis=1), st


def _hgrn_core_fwd(q, k, lf, p):
    t = q.shape[0]
    hd = HEAD_DIM
    rs = min(HGRN_STEP, t)
    n = t // rs

    def body(q_ref, k_ref, lf_ref, v_ref, o_ref, st_ref, s_ref):
        @pl.when(pl.program_id(0) == 0)
        def _():
            s_ref[...] = jnp.zeros_like(s_ref)

        s = s_ref[...]
        st_ref[:, 0] = s
        o, s_new = _hgrn_step(s, *[_heads_major(r) for r in (q_ref, k_ref, lf_ref, v_ref)])
        for h in range(HEADS):
            o_ref[:, h * hd:(h + 1) * hd] = o[h]
        s_ref[...] = s_new

    tile = pl.BlockSpec((rs, GDN_WIDTH), lambda i: (i, 0))
    return pl.pallas_call(
        body, name="hgrn_core_fwd", grid=(n,),
        in_specs=[tile, tile, tile, pl.BlockSpec((rs, GDN_WIDTH), lambda i: (i, OFF_IB // GDN_WIDTH))],
        out_specs=[tile, pl.BlockSpec((HEADS, 1, hd, hd), lambda i: (0, i, 0, 0))],
        out_shape=[_sds((t, GDN_WIDTH), f32), _sds((HEADS, n, hd, hd), f32)],
        scratch_shapes=[pltpu.VMEM((HEADS, hd, hd), f32)],
        compiler_params=_cparams(("arbitrary",)))(q, k, lf, p)


def _hgrn_core_bwd(q, k, lf, p, states, do):
    t = q.shape[0]
    hd = HEAD_DIM
    rs = min(HGRN_STEP, t)
    n = t // rs

    def body(q_ref, k_ref, lf_ref, v_ref, st_ref, do_ref, dq_ref, dk_ref, dlf_ref, dv_ref, ds_ref):
        @pl.when(pl.program_id(0) == 0)
        def _():
            ds_ref[...] = jnp.zeros_like(ds_ref)

        _, vjp = jax.vjp(_hgrn_step, st_ref[:, 0], *[_heads_major(r) for r in (q_ref, k_ref, lf_ref, v_ref)])
        ds, *dins = vjp((_heads_major(do_ref), ds_ref[...]))
        ds_ref[...] = ds
        for d_ref, d in zip((dq_ref, dk_ref, dlf_ref, dv_ref), dins):
            for h in range(HEADS):
                d_ref[:, h * hd:(h + 1) * hd] = d[h].astype(d_ref.dtype)

    tile = pl.BlockSpec((rs, GDN_WIDTH), lambda i: (n - 1 - i, 0))
    return pl.pallas_call(
        body, name="hgrn_core_bwd", grid=(n,),
        in_specs=[tile, tile, tile, pl.BlockSpec((rs, GDN_WIDTH), lambda i: (n - 1 - i, OFF_IB // GDN_WIDTH)),
                  pl.BlockSpec((HEADS, 1, hd, hd), lambda i: (0, n - 1 - i, 0, 0)), tile],
        out_specs=[tile] * 4, out_shape=[_sds((t, GDN_WIDTH), f32)] * 3 + [_sds((t, GDN_WIDTH), bf16)],
        scratch_shapes=[pltpu.VMEM((HEADS, hd, hd), f32)],
        compiler_params=_cparams(("arbitrary",)))(q, k, lf, p, states, do)


def _row(v):
    return v.reshape(1, -1)


def _pad_lanes(v, n=HEAD_DIM):
    return jnp.pad(v.reshape(1, -1), ((0, 0), (0, n - v.shape[-1])))


def _ffn_fwd(x, w, l):
    h = _rms_fwd(x, _row(w['norm_ffn'][l]), "ffn_norm")
    u = _mm(h, w['ffn_w_up'][l], name="ffn_up")
    a = _ffn_act_fwd(u, w['ffn_conv_w'][l], _row(w['ffn_conv_b'][l]))
    y = _mm(a, w['ffn_w_down'][l], add=x, name="ffn_down")
    return y, (x, h, u)


def _ffn_bwd(saved, w, l, dy, dyb, grads):
    x, h, u = saved
    da = _mm(dyb, w['ffn_w_down'][l], tb=True, name="ffn_down_dx")
    a, dg, dv, dcw, dcb = _ffn_act_bwd(u, w['ffn_conv_w'][l], _row(w['ffn_conv_b'][l]), da)
    grads['ffn_w_down'][l] = _mm(a, dyb, ta=True, out_dtype=bf16, name="ffn_down_dw")
    du = jnp.concatenate([dg, dv], axis=1)
    grads['ffn_w_up'][l] = _mm(h, du, ta=True, out_dtype=bf16, name="ffn_up_dw")
    dh = _mm(du, w['ffn_w_up'][l], tb=True, name="ffn_up_dx")
    dx, dxb, dgain = _rms_bwd(x, _row(w['norm_ffn'][l]), dh, dy, "ffn_norm_bwd")
    grads['ffn_conv_w'][l] = dcw
    grads['ffn_conv_b'][l] = dcb[0]
    grads['norm_ffn'][l] = dgain[0]
    return dx, dxb


def _odd_fwd(x, w, l, j):
    h = _rms_fwd(x, _row(w['norm_mix'][l]), "mix_norm")
    p = _mm(h, w['c_w_in'][j], name="lru_in")
    xc = _col_conv_fwd(p, LRU_WIDTH, w['c_conv_w'][j], _row(w['c_conv_b'][j]), LRU_CONV, 256, "lru_conv_fwd")
    out, hs, a = _lru_fwd(p, xc, w['c_gate_a_w'][j], _row(w['c_gate_a_b'][j]), w['c_gate_x_w'][j],
                          _row(w['c_gate_x_b'][j]), _row(w['c_lambda'][j]))
    y = _mm(out, w['c_w_out'][j], add=x, name="lru_out")
    return y, (x, h, p, xc, out, hs, a)


def _odd_bwd(saved, w, l, j, dy, dyb, grads):
    x, h, p, xc, out, hs, a = saved
    dout = _mm(dyb, w['c_w_out'][j], tb=True, name="lru_out_dx")
    grads['c_w_out'][j] = _mm(out, dyb, ta=True, out_dtype=bf16, name="lru_out_dw")
    dyb_, da, du = _lru_bwd_scan(p, a, hs, dout)
    dxc, dwa, dwx, dba, dbx, dlam = _lru_bwd_gates(xc, da, du, w['c_gate_a_w'][j], _row(w['c_gate_a_b'][j]),
                                                   w['c_gate_x_w'][j], _row(w['c_gate_x_b'][j]), _row(w['c_lambda'][j]))
    dxb_, dcw, dcb = _col_conv_bwd(p, LRU_WIDTH, w['c_conv_w'][j], dxc, LRU_CONV, 256, "lru_conv_bwd")
    dp = jnp.concatenate([dyb_, dxb_], axis=1)
    grads['c_w_in'][j] = _mm(h, dp, ta=True, out_dtype=bf16, name="lru_in_dw")
    dh = _mm(dp, w['c_w_in'][j], tb=True, name="lru_in_dx")
    dx, dxb, dgain = _rms_bwd(x, _row(w['norm_mix'][l]), dh, dy, "mix_norm_bwd")
    grads['c_gate_a_w'][j], grads['c_gate_x_w'][j] = dwa, dwx
    grads['c_gate_a_b'][j], grads['c_gate_x_b'][j], grads['c_lambda'][j] = dba[0], dbx[0], dlam[0]
    grads['c_conv_w'][j], grads['c_conv_b'][j] = dcw, dcb[0]
    grads['norm_mix'][l] = dgain[0]
    return dx, dxb


def _even_fwd(x, w, l, j):
    h = _rms_fwd(x, _row(w['norm_mix'][l]), "mix_norm")
    p = _mm(h, w['ab_w_in'][j], name="ab_in")
    alog, dtb = _pad_lanes(w['gdn_a_log'][j]), _pad_lanes(w['gdn_dt_bias'][j])
    q, k, v, be, ge = _gdn_pre_fwd(p, w['gdn_conv_w'][j], alog, dtb)
    oa, sa = _gdn_core_fwd(q, k, v, ge, be)
    ya = _post_fwd(oa, p, OFF_Z, _row(w['gdn_norm'][j]), "gdn_post_fwd")
    qq, kk, lf = _hgrn_pre_fwd(p, w['hgrn_lower_bounds'], j)
    ob, sb = _hgrn_core_fwd(qq, kk, lf, p)
    yb = _post_fwd(ob, p, OFF_GB, _row(w['hgrn_norm'][j]), "hgrn_post_fwd")
    mix = jnp.concatenate([ya, yb], axis=1)
    y = _mm(mix, w['ab_w_out'][j], add=x, name="ab_out")
    return y, (x, h, p, q, k, v, be, ge, oa, sa, qq, kk, lf, ob, sb, mix)


def _even_bwd(saved, w, l, j, dy, dyb, grads):
    x, h, p, q, k, v, be, ge, oa, sa, qq, kk, lf, ob, sb, mix = saved
    alog, dtb = _pad_lanes(w['gdn_a_log'][j]), _pad_lanes(w['gdn_dt_bias'][j])
    dmix = _mm(dyb, w['ab_w_out'][j], tb=True, name="ab_out_dx")
    grads['ab_w_out'][j] = _mm(mix, dyb, ta=True, out_dtype=bf16, name="ab_out_dw")
    doa, dz, dgn = _post_bwd(oa, p, OFF_Z, _row(w['gdn_norm'][j]), dmix, 0, "gdn_post_bwd")
    dob, dgb, dhn = _post_bwd(ob, p, OFF_GB, _row(w['hgrn_norm'][j]), dmix, GDN_WIDTH, "hgrn_post_bwd")
    dq, dk, dv, dge, dbe = _gdn_core_bwd(q, k, v, ge, be, sa, doa)
    dpq, dpk, dpv, dba, dwq, dwk, dwv, dal, ddt = _gdn_pre_bwd(p, w['gdn_conv_w'][j], alog, dtb, dq, dk, dv, dbe, dge)
    dqq, dkk, dlf, dib = _hgrn_core_bwd(qq, kk, lf, p, sb, dob)
    dqb, dfb, dlb = _hgrn_pre_bwd(p, w['hgrn_lower_bounds'], j, dqq, dkk, dlf)
    dp = jnp.concatenate([dpq, dpk, dpv, dz, dqb, dfb, dib, dgb, dba.astype(bf16)], axis=1)
    grads['ab_w_in'][j] = _mm(h, dp, ta=True, out_dtype=bf16, name="ab_in_dw")
    dh = _mm(dp, w['ab_w_in'][j], tb=True, name="ab_in_dx")
    dx, dxb, dgain = _rms_bwd(x, _row(w['norm_mix'][l]), dh, dy, "mix_norm_bwd")
    grads['gdn_conv_w'][j] = jnp.concatenate([dwq, dwk, dwv], axis=1)
    grads['gdn_a_log'][j], grads['gdn_dt_bias'][j] = dal[0, :HEADS], ddt[0, :HEADS]
    grads['gdn_norm'][j], grads['hgrn_norm'][j] = dgn[0], dhn[0]
    grads['hgrn_lower_bounds'].append(dlb)
    grads['norm_mix'][l] = dgain[0]
    return dx, dxb


def _ab_permute(w_in):
    pad = jnp.zeros(w_in.shape[:-1] + (AB_PAD - AB_COLS,), w_in.dtype)
    return jnp.concatenate([w_in[..., :2048], w_in[..., 2056:], w_in[..., 2048:2056], pad], axis=-1)


def _ab_unpermute(g):
    return jnp.concatenate([g[..., :2048], g[..., 4096:4104], g[..., 2048:4096]], axis=-1)


def _block_pad(a, axis, nblk, padded):
    axis = axis % a.ndim
    s = a.shape
    a = a.reshape(s[:axis] + (nblk, s[axis] // nblk) + s[axis + 1:])
    pad = [(0, 0)] * a.ndim
    pad[axis + 1] = (0, padded - s[axis] // nblk)
    return jnp.pad(a, pad).reshape(s[:axis] + (nblk * padded,) + s[axis + 1:])


def _block_unpad(a, axis, nblk, width):
    axis = axis % a.ndim
    s = a.shape
    a = a.reshape(s[:axis] + (nblk, s[axis] // nblk) + s[axis + 1:])
    a = lax.slice_in_dim(a, 0, width, axis=axis + 1)
    return a.reshape(s[:axis] + (nblk * width,) + s[axis + 1:])


def _kernel_layout(w):
    w = dict(w)
    w['ab_w_in'] = _ab_permute(w['ab_w_in'])
    w['ffn_w_up'] = _block_pad(w['ffn_w_up'], 2, N_DEV, FF_PAD)
    w['ffn_w_down'] = _block_pad(w['ffn_w_down'], 1, 4, FF_PAD)
    w['ffn_conv_w'] = _block_pad(w['ffn_conv_w'], 2, 4, FF_PAD)
    w['ffn_conv_b'] = _block_pad(w['ffn_conv_b'], 1, 4, FF_PAD)
    return w


def _natural_grads(g):
    g = dict(g)
    g['ab_w_in'] = _ab_unpermute(g['ab_w_in'])
    g['ffn_w_up'] = _block_unpad(g['ffn_w_up'], 2, N_DEV, FF_SHARD)
    g['ffn_w_down'] = _block_unpad(g['ffn_w_down'], 1, 4, FF_SHARD)
    g['ffn_conv_w'] = _block_unpad(g['ffn_conv_w'], 2, 4, FF_SHARD)
    g['ffn_conv_b'] = _block_unpad(g['ffn_conv_b'], 1, 4, FF_SHARD)
    return g


def _local_step(x, target, w, fetch=None):
    grads = {n: [None] * (DEPTH if n in ('norm_mix', 'norm_ffn') or n.startswith('ffn_') else 2)
             for n in WEIGHTS if n not in ('norm_final', 'hgrn_lower_bounds')}
    grads['hgrn_lower_bounds'] = []
    saved = []
    for l in range(DEPTH):
        j = l // 2
        if fetch is not None:
            fetch(l, x)
        x, s_mix = (_even_fwd if l % 2 == 0 else _odd_fwd)(x, w, l, j)
        x, s_ffn = _ffn_fwd(x, w, l)
        saved.append((s_mix, s_ffn))
    loss, dx, dxb, dgf = _loss_head(x, _row(w['norm_final']), target)
    for l in reversed(range(DEPTH)):
        j = l // 2
        s_mix, s_ffn = saved[l]
        dx, dxb = _ffn_bwd(s_ffn, w, l, dx, dxb, grads)
        dx, dxb = (_even_bwd if l % 2 == 0 else _odd_bwd)(s_mix, w, l, j, dx, dxb, grads)
    out = {n: jnp.stack(g) for n, g in grads.items() if n != 'hgrn_lower_bounds'}
    out['hgrn_lower_bounds'] = grads['hgrn_lower_bounds'][0] + grads['hgrn_lower_bounds'][1]
    out['norm_final'] = dgf[0]
    return loss[0, 0], dx, out


def _position():
    return lax.axis_index("x"), lax.axis_index("y"), lax.axis_index("c")


BLOCK_LAYOUT = {
    'ab_w_in': ((2, D_MODEL, N_DEV * AB_SHARD_PAD), (2, D_MODEL, AB_SHARD_PAD)),
    'ab_w_out': ((2, N_DEV, 128, D_MODEL), (2, 128, D_MODEL)),
    'c_w_in': ((2, D_MODEL, 2 * LRU_WIDTH), (2, D_MODEL, 256)),
    'c_w_out': ((2, N_DEV, 128, D_MODEL), (2, 128, D_MODEL)),
    'c_gate_a_w': ((2, HEADS, N_DEV, 32, LRU_BLOCK), (2, HEADS, 32, LRU_BLOCK)),
    'c_gate_x_w': ((2, HEADS, N_DEV, 32, LRU_BLOCK), (2, HEADS, 32, LRU_BLOCK)),
    'ffn_w_up': ((DEPTH, D_MODEL, N_DEV * FF_PAD), (DEPTH, D_MODEL, FF_PAD)),
    'ffn_w_down': ((DEPTH, 4, FF_PAD, D_MODEL), (DEPTH, FF_ROWS, D_MODEL)),
}


COL_WINDOW = {'ab_w_in': AB_SHARD_PAD, 'c_w_in': 256, 'ffn_w_up': FF_PAD}


def _block_index(name, p):
    d = 4 * p[0] + 2 * p[1] + p[2]
    if name in COL_WINDOW:
        return (slice(None), pl.ds(pl.multiple_of(d * COL_WINDOW[name], 128), COL_WINDOW[name]))
    if name == 'ffn_w_down':
        return (2 * p[0] + p[1], pl.ds(pl.multiple_of(p[2] * FF_ROWS, 16), FF_ROWS), slice(None))
    if name in ('c_gate_a_w', 'c_gate_x_w'):
        return (slice(None), d)
    return (d,)


def _block_of(name, ref, p, layered=True):
    idx = _block_index(name, p)
    if layered and name in BLOCK_LAYOUT:
        idx = (slice(None),) + idx
    return ref.at[idx]


def _layer_items(l):
    j = l // 2
    mix = ([('ab_w_in', j), ('ab_w_out', j)] if l % 2 == 0 else
           [('c_w_in', j), ('c_w_out', j), ('c_gate_a_w', j), ('c_gate_x_w', j)])
    return mix + [('ffn_w_up', l), ('ffn_w_down', l)]


def _own_land(name, shard_l, pos):
    x, y, c = pos
    d = 4 * x + 2 * y + c
    shape = BLOCK_LAYOUT[name][0][1:] if name in BLOCK_LAYOUT else (N_DEV,) + shard_l.shape
    zeros = jnp.zeros(shape, shard_l.dtype)
    if name in COL_WINDOW:
        return lax.dynamic_update_slice(zeros, shard_l, (0, d * COL_WINDOW[name]))
    if name == 'ffn_w_down':
        return lax.dynamic_update_slice(zeros, shard_l[None], (2 * x + y, c * FF_ROWS, 0))
    if name in ('c_gate_a_w', 'c_gate_x_w'):
        return lax.dynamic_update_slice(zeros, shard_l[:, None], (0, d, 0, 0))
    return lax.dynamic_update_slice(zeros, shard_l[None], (d,) + (0,) * shard_l.ndim)


def _src_of(shard_ref, li):
    return shard_ref if li is None else shard_ref.at[li]


def _gather_now(items, shards, lands):
    n = len(items)
    srcs = sorted({nm for nm, _ in items})

    def body(*refs):
        ins = dict(zip(srcs, refs[:len(srcs)]))
        outs = refs[len(srcs) + n:len(srcs) + 2 * n]
        send_sems, recv_sems = refs[len(srcs) + 2 * n:]
        x, y, c = _position()
        me, sibling = (x, y, c), (x, y, 1 - c)
        chips = [(1 - x, y), (x, 1 - y), (1 - x, 1 - y)]

        def copy(i, k, block, to, own=False):
            nm, li = items[i]
            dst = _block_of(nm, outs[i], block, layered=False)
            return pltpu.make_async_remote_copy(
                src_ref=_src_of(ins[nm], li) if own else dst, dst_ref=dst, send_sem=send_sems.at[7 * i + k],
                recv_sem=recv_sems.at[7 * i + k], device_id=to, device_id_type=MESH)

        first = []
        for i in range(n):
            first.append(copy(i, 0, me, sibling, own=True))
            first += [copy(i, 1 + j, me, (*chip, c), own=True) for j, chip in enumerate(chips)]
        for cp in first:
            cp.start()
        passed = []
        for j, chip in enumerate(chips):
            for i in range(n):
                copy(i, 1 + j, (*chip, c), me).wait_recv()
                fwd = copy(i, 4 + j, (*chip, c), sibling)
                fwd.start()
                passed.append(fwd)
        for i in range(n):
            copy(i, 0, sibling, me).wait_recv()
        for j, chip in enumerate(chips):
            for i in range(n):
                copy(i, 4 + j, (*chip, 1 - c), me).wait_recv()
        for cp in first + passed:
            cp.wait_send()

    any_spec = pl.BlockSpec(memory_space=pl.ANY)
    return pl.pallas_call(
        body, name="gather_first_layer", out_shape=[_sds(a.shape, a.dtype) for a in lands],
        in_specs=[any_spec] * (len(srcs) + n), out_specs=[any_spec] * n,
        input_output_aliases={len(srcs) + i: i for i in range(n)},
        scratch_shapes=[pltpu.SemaphoreType.DMA((7 * n,)), pltpu.SemaphoreType.DMA((7 * n,))],
    )(*[shards[nm] for nm in srcs], *lands)


def _lanes(name, land_ref, pos):
    if name == 'ffn_w_down':
        return [((1,), land_ref.at[0, pl.ds(0, FF_ROWS)]), ((2, 3, 4, 5, 6, 7), land_ref.at[pl.ds(0, 3), pl.ds(0, 2 * FF_ROWS)])]
    if name in COL_WINDOW:
        return [(tuple(range(1, N_DEV)), land_ref.at[:, pl.ds(0, 7 * COL_WINDOW[name])])]
    if name in ('c_gate_a_w', 'c_gate_x_w'):
        return [(tuple(range(1, N_DEV)), land_ref.at[:, pl.ds(0, 7)])]
    return [(tuple(range(1, N_DEV)), land_ref.at[pl.ds(0, 7)])]


def _n_lanes(items):
    return sum(2 if nm == 'ffn_w_down' else 1 for nm, _ in items)


HBM_SPEC = pl.BlockSpec(memory_space=pltpu.HBM)
SEM_SPEC = pl.BlockSpec(memory_space=pltpu.SEMAPHORE)
SIDE_EFFECT = pltpu.SideEffectType.DATAFLOW_SIDE_EFFECTING


def _gather_start(items, shards, lands, token):
    n = len(items)
    srcs = sorted({nm for nm, _ in items})
    ns, nl = len(srcs), _n_lanes(items)

    def body(*refs):
        ins = dict(zip(srcs, refs[:ns]))
        land_refs = refs[ns:ns + n]
        sems = refs[ns + n + 1:ns + n + 1 + 2 * nl]
        x, y, c = _position()
        me = (x, y, c)
        lane = 0
        for i, (nm, li) in enumerate(items):
            for codes, _ in _lanes(nm, land_refs[i], me):
                for k in codes:
                    peer = (1 - x if (k >> 2) & 1 else x, 1 - y if (k >> 1) & 1 else y, 1 - c if k & 1 else c)
                    pltpu.make_async_remote_copy(
                        src_ref=_src_of(ins[nm], li), dst_ref=_block_of(nm, land_refs[i], me, layered=False),
                        send_sem=sems[2 * lane], recv_sem=sems[2 * lane + 1], device_id=peer, device_id_type=MESH).start()
                lane += 1

    hbm = [pltpu.with_memory_space_constraint(a, pltpu.HBM) for a in [shards[nm] for nm in srcs] + list(lands)]
    outs = pl.pallas_call(
        body, name="gather_start",
        out_shape=[pltpu.SemaphoreType.DMA(())] * (2 * nl) + [pltpu.HBM(a.shape, a.dtype) for a in hbm],
        in_specs=[HBM_SPEC] * (ns + n) + [pl.BlockSpec(memory_space=pl.ANY)],
        out_specs=[SEM_SPEC] * (2 * nl) + [HBM_SPEC] * (ns + n),
        input_output_aliases={i: 2 * nl + i for i in range(ns + n)},
        compiler_params=pltpu.CompilerParams(has_side_effects=SIDE_EFFECT),
    )(*hbm, token)
    return outs[:2 * nl], dict(zip(srcs, outs[2 * nl:2 * nl + ns])), outs[2 * nl + ns:]


def _gather_wait(items, sems, shards, lands, after, name):
    n = len(items)
    srcs = sorted(shards)
    ns, nl = len(srcs), _n_lanes(items)

    def body(*refs):
        land_refs = refs[ns:ns + n]
        sem_refs = refs[ns + n:ns + n + 2 * nl]
        x, y, c = _position()
        lane = 0
        for i, (nm, _) in enumerate(items):
            for _, moved in _lanes(nm, land_refs[i], (x, y, c)):
                cp = pltpu.make_async_remote_copy(
                    src_ref=moved, dst_ref=moved, send_sem=sem_refs[2 * lane], recv_sem=sem_refs[2 * lane + 1],
                    device_id=(x, y, 1 - c), device_id_type=MESH)
                cp.wait_send()
                cp.wait_recv()
                lane += 1

    outs = pl.pallas_call(
        body, name=name, out_shape=[pltpu.HBM(shards[nm].shape, shards[nm].dtype) for nm in srcs]
        + [pltpu.HBM(a.shape, a.dtype) for a in lands],
        in_specs=[HBM_SPEC] * (ns + n) + [SEM_SPEC] * (2 * nl) + [pl.BlockSpec(memory_space=pl.ANY)],
        out_specs=[HBM_SPEC] * (ns + n), input_output_aliases={i: i for i in range(ns + n)},
        compiler_params=pltpu.CompilerParams(has_side_effects=SIDE_EFFECT),
    )(*[shards[nm] for nm in srcs], *lands, *sems, after)
    return dict(zip(srcs, outs[:ns])), outs[ns:]


def _exchange_grads(fulls, rep):
    cpos = lax.axis_index("c").astype(jnp.int32).reshape(1)
    pair, rep_pair = _pair_exchange(fulls, rep)
    chip = {nm: _chip_sum(nm, fulls[nm], pair[nm], cpos) for nm in fulls}
    rep_chip = _add_pair(rep, rep_pair, "chip_sum_replicated")
    cross, cross_rep = _cross_exchange(chip, rep_chip)
    return chip, rep_chip, cross, cross_rep


def _pair_exchange(fulls, rep):
    names = list(fulls)
    n = len(names)
    shard_shape = {nm: (BLOCK_LAYOUT[nm][1] if nm in BLOCK_LAYOUT else fulls[nm].shape[1:]) for nm in names}

    def body(*refs):
        ins = dict(zip(names, refs[:n]))
        rep_ref = refs[n]
        pair = dict(zip(names, refs[n + 1:2 * n + 1]))
        rpair_ref = refs[2 * n + 1]
        send_sems, recv_sems = refs[2 * n + 2:]
        x, y, c = _position()
        sibling = (x, y, 1 - c)
        remote = []
        for i, nm in enumerate(names):
            for q in range(4):
                remote.append(pltpu.make_async_remote_copy(
                    src_ref=_block_of(nm, ins[nm], (q >> 1, q & 1, 1 - c)), dst_ref=pair[nm].at[q],
                    send_sem=send_sems.at[4 * i + q], recv_sem=recv_sems.at[4 * i + q], device_id=sibling,
                    device_id_type=MESH))
        remote.append(pltpu.make_async_remote_copy(
            src_ref=rep_ref, dst_ref=rpair_ref, send_sem=send_sems.at[4 * n], recv_sem=recv_sems.at[4 * n],
            device_id=sibling, device_id_type=MESH))
        for cp in remote:
            cp.start()
        for cp in remote:
            cp.wait_recv()
        for cp in remote:
            cp.wait_send()

    any_spec = pl.BlockSpec(memory_space=pl.ANY)
    four = [_sds((4,) + tuple(shard_shape[nm]), fulls[nm].dtype) for nm in names]
    outs = pl.pallas_call(
        body, name="grad_pair_exchange", out_shape=four + [_sds(rep.shape, rep.dtype)],
        in_specs=[any_spec] * (n + 1), out_specs=[any_spec] * (n + 1),
        scratch_shapes=[pltpu.SemaphoreType.DMA((4 * n + 1,)), pltpu.SemaphoreType.DMA((4 * n + 1,))],
    )(*[fulls[nm] for nm in names], rep)
    return dict(zip(names, outs[:n])), outs[n]


def _chip_sum(name, full, pair, cpos):
    if name in ('ab_w_in', 'c_w_in', 'ffn_w_up'):
        width = BLOCK_LAYOUT[name][1][-1]
        rows = full.shape[0] * full.shape[1]
        tr = 512

        def body(c_ref, f_ref, p_ref, o_ref):
            o_ref[0] = (f_ref[...].astype(f32) + p_ref[0].astype(f32)).astype(o_ref.dtype)

        slot = pl.BlockSpec((1, tr, width), lambda q, i, c: (q, i, 0))
        out = pl.pallas_call(
            body, name="chip_sum_" + name, out_shape=_sds((4, rows, width), full.dtype),
            grid_spec=pltpu.PrefetchScalarGridSpec(
                num_scalar_prefetch=1, grid=(4, rows // tr),
                in_specs=[pl.BlockSpec((tr, width), lambda q, i, c: (i, 2 * q + c[0])), slot], out_specs=slot),
            compiler_params=_cparams(("parallel", "parallel")))(
            cpos, full.reshape(rows, N_DEV * width), pair.reshape(4, rows, width))
        return out.reshape(pair.shape)

    if name == 'ffn_w_down':
        f4, p4 = full, pair
        fspec = pl.BlockSpec((DEPTH, 1, FF_ROWS, D_MODEL), lambda q, c: (0, q, c[0], 0))
    else:
        shard = pair.shape[1:]
        lead = int(np.prod(shard[:-2]))
        f4 = full.reshape((lead, N_DEV) + shard[-2:])
        p4 = pair.reshape((4, lead) + shard[-2:])
        fspec = pl.BlockSpec((lead, 1) + shard[-2:], lambda q, c: (0, 2 * q + c[0], 0, 0))

    def body4(c_ref, f_ref, p_ref, o_ref):
        o_ref[0] = (f_ref[:, 0].astype(f32) + p_ref[0].astype(f32)).astype(o_ref.dtype)

    slot = pl.BlockSpec((1,) + p4.shape[1:], lambda q, c: (q, 0, 0, 0))
    out = pl.pallas_call(
        body4, name="chip_sum_" + name, out_shape=_sds(p4.shape, full.dtype),
        grid_spec=pltpu.PrefetchScalarGridSpec(num_scalar_prefetch=1, grid=(4,), in_specs=[fspec, slot], out_specs=slot),
        compiler_params=_cparams(("parallel",)))(cpos, f4, p4)
    return out.reshape(pair.shape)


def _add_pair(a, b, name):
    shp = a.shape
    r, c = int(np.prod(shp[:-1])), shp[-1]
    tr = _tile(r, (512, 256, 128, 64, 32, 16, 8))

    def body(a_ref, b_ref, o_ref):
        o_ref[...] = (a_ref[...].astype(f32) + b_ref[...].astype(f32)).astype(o_ref.dtype)

    tile = pl.BlockSpec((tr, c), lambda i: (i, 0))
    return pl.pallas_call(body, name=name, grid=(r // tr,), in_specs=[tile, tile], out_specs=tile,
                          out_shape=_sds((r, c), a.dtype), compiler_params=_cparams(("parallel",)))(
        a.reshape(r, c), b.reshape(r, c)).reshape(shp)


def _cross_exchange(chip, rep_chip):
    names = list(chip)
    n = len(names)

    def body(*refs):
        ins = dict(zip(names, refs[:n]))
        rep_ref = refs[n]
        outs = dict(zip(names, refs[2 * n + 2:3 * n + 2]))
        rrep_ref = refs[3 * n + 2]
        send_sems, recv_sems = refs[3 * n + 3:]
        x, y, c = _position()
        mine = 2 * x + y
        copies = []
        for k in range(1, 4):
            px, py = (1 - x if (k >> 1) & 1 else x), (1 - y if k & 1 else y)
            for i, nm in enumerate(names + ['']):
                src = rep_ref if i == n else ins[nm].at[2 * px + py]
                dst = (rrep_ref if i == n else outs[nm]).at[mine]
                copies.append(pltpu.make_async_remote_copy(
                    src_ref=src, dst_ref=dst, send_sem=send_sems.at[3 * i + k - 1], recv_sem=recv_sems.at[3 * i + k - 1],
                    device_id=(px, py, c), device_id_type=MESH))
        for cp in copies:
            cp.start()
        for cp in copies:
            cp.wait_recv()
        for cp in copies:
            cp.wait_send()

    any_spec = pl.BlockSpec(memory_space=pl.ANY)
    shapes = [_sds(chip[nm].shape, chip[nm].dtype) for nm in names] + [_sds((4,) + rep_chip.shape, rep_chip.dtype)]
    zeros = [jnp.zeros(s.shape, s.dtype) for s in shapes]
    outs = pl.pallas_call(
        body, name="grad_cross_exchange", out_shape=shapes,
        in_specs=[any_spec] * (2 * n + 2), out_specs=[any_spec] * (n + 1),
        input_output_aliases={n + 1 + i: i for i in range(n + 1)},
        scratch_shapes=[pltpu.SemaphoreType.DMA((3 * (n + 1),)), pltpu.SemaphoreType.DMA((3 * (n + 1),))],
    )(*[chip[nm] for nm in names], rep_chip, *zeros)
    return dict(zip(names, outs[:n])), outs[n]


def _sum_adamw(parts, own, mine, w, m, v, name):
    r, l = w.shape
    lp = parts.shape[2]
    tr = _tile(r, (256, 128, 64, 32, 16, 8))
    c1 = 1.0 / (1.0 - ADAM_B1 ** ADAM_STEP)
    c2 = 1.0 / (1.0 - ADAM_B2 ** ADAM_STEP)

    def body(mine_ref, p_ref, o_ref, w_ref, m_ref, v_ref, g_ref, d_ref, nm_ref, nv_ref):
        mine_v = (o_ref[0] if own.ndim == 3 else o_ref[...]).astype(f32)
        g = jnp.where(mine_ref[0] == 0, mine_v, p_ref[0].astype(f32))
        for s in range(1, parts.shape[0]):
            g = g + jnp.where(mine_ref[0] == s, mine_v, p_ref[s].astype(f32))
        if lp != l:
            g = g[:, :l]
        m_new = ADAM_B1 * m_ref[...] + (1.0 - ADAM_B1) * g
        v_new = ADAM_B2 * v_ref[...] + (1.0 - ADAM_B2) * (g * g)
        g_ref[...] = g
        nm_ref[...] = m_new
        nv_ref[...] = v_new
        d_ref[...] = -ADAM_LR * ((m_new * c1) / (jnp.sqrt(v_new * c2) + ADAM_EPS) + ADAM_WD * w_ref[...])

    tile = pl.BlockSpec((tr, l), lambda i, mn: (i, 0))
    own_spec = (pl.BlockSpec((1, tr, lp), lambda i, mn: (mn[0], i, 0)) if own.ndim == 3
                else pl.BlockSpec((tr, lp), lambda i, mn: (i, 0)))
    return pl.pallas_call(
        body, name=name, out_shape=[_sds((r, l), f32)] * 4,
        grid_spec=pltpu.PrefetchScalarGridSpec(
            num_scalar_prefetch=1, grid=(r // tr,),
            in_specs=[pl.BlockSpec((parts.shape[0], tr, lp), lambda i, mn: (0, i, 0)), own_spec, tile, tile, tile],
            out_specs=[tile] * 4),
        compiler_params=_cparams(("parallel",)))(mine, parts, own, w, m, v)


def _pack(arrs, lead=None):
    if lead is None:
        flat = jnp.concatenate([a.reshape(-1).astype(f32) for a in arrs])
        n = flat.shape[0]
    else:
        flat = jnp.concatenate([a.reshape(lead, -1).astype(f32) for a in arrs], axis=1)
        n = flat.shape[1]
    tot = -(-n // 1024) * 1024
    if lead is None:
        return jnp.pad(flat, (0, tot - n)).reshape(tot // 128, 128)
    return jnp.pad(flat, ((0, 0), (0, tot - n))).reshape(lead, tot // 128, 128)


def _unpack(packed, shapes, lead=False):
    flat = packed.reshape(packed.shape[0], -1) if lead else packed.reshape(-1)
    out, off = [], 0
    for s in shapes:
        n = int(np.prod(s))
        out.append(flat[:, off:off + n].reshape((packed.shape[0],) + tuple(s)) if lead else flat[off:off + n].reshape(s))
        off += n
    return out


def _merge_shards(g, axis):
    g = jnp.moveaxis(g, 0, axis)
    s = g.shape
    return g.reshape(s[:axis] + (s[axis] * s[axis + 1],) + s[axis + 2:])


def _split_shards(full, axis):
    s = full.shape
    g = full.reshape(s[:axis] + (N_DEV, s[axis] // N_DEV) + s[axis + 1:])
    return jnp.moveaxis(g, axis, 0)


def kernel(x, norm_mix, norm_ffn, norm_final, ab_w_in, gdn_conv_w, gdn_a_log, gdn_dt_bias, gdn_norm, hgrn_lower_bounds, hgrn_norm, ab_w_out, c_w_in, c_conv_w, c_conv_b, c_gate_a_w, c_gate_a_b, c_gate_x_w, c_gate_x_b, c_lambda, c_w_out, ffn_w_up, ffn_conv_w, ffn_conv_b, ffn_w_down, loss_target, m_norm_mix, m_norm_ffn, m_norm_final, m_ab_w_in, m_gdn_conv_w, m_gdn_a_log, m_gdn_dt_bias, m_gdn_norm, m_hgrn_lower_bounds, m_hgrn_norm, m_ab_w_out, m_c_w_in, m_c_conv_w, m_c_conv_b, m_c_gate_a_w, m_c_gate_a_b, m_c_gate_x_w, m_c_gate_x_b, m_c_lambda, m_c_w_out, m_ffn_w_up, m_ffn_conv_w, m_ffn_conv_b, m_ffn_w_down, v_norm_mix, v_norm_ffn, v_norm_final, v_ab_w_in, v_gdn_conv_w, v_gdn_a_log, v_gdn_dt_bias, v_gdn_norm, v_hgrn_lower_bounds, v_hgrn_norm, v_ab_w_out, v_c_w_in, v_c_conv_w, v_c_conv_b, v_c_gate_a_w, v_c_gate_a_b, v_c_gate_x_w, v_c_gate_x_b, v_c_lambda, v_c_w_out, v_ffn_w_up, v_ffn_conv_w, v_ffn_conv_b, v_ffn_w_down):
    wl = dict(zip(WEIGHTS, (norm_mix, norm_ffn, norm_final, ab_w_in, gdn_conv_w, gdn_a_log, gdn_dt_bias, gdn_norm, hgrn_lower_bounds, hgrn_norm, ab_w_out, c_w_in, c_conv_w, c_conv_b, c_gate_a_w, c_gate_a_b, c_gate_x_w, c_gate_x_b, c_lambda, c_w_out, ffn_w_up, ffn_conv_w, ffn_conv_b, ffn_w_down)))
    ml = dict(zip(WEIGHTS, (m_norm_mix, m_norm_ffn, m_norm_final, m_ab_w_in, m_gdn_conv_w, m_gdn_a_log, m_gdn_dt_bias, m_gdn_norm, m_hgrn_lower_bounds, m_hgrn_norm, m_ab_w_out, m_c_w_in, m_c_conv_w, m_c_conv_b, m_c_gate_a_w, m_c_gate_a_b, m_c_gate_x_w, m_c_gate_x_b, m_c_lambda, m_c_w_out, m_ffn_w_up, m_ffn_conv_w, m_ffn_conv_b, m_ffn_w_down)))
    vl = dict(zip(WEIGHTS, (v_norm_mix, v_norm_ffn, v_norm_final, v_ab_w_in, v_gdn_conv_w, v_gdn_a_log, v_gdn_dt_bias, v_gdn_norm, v_hgrn_lower_bounds, v_hgrn_norm, v_ab_w_out, v_c_w_in, v_c_conv_w, v_c_conv_b, v_c_gate_a_w, v_c_gate_a_b, v_c_gate_x_w, v_c_gate_x_b, v_c_lambda, v_c_w_out, v_ffn_w_up, v_ffn_conv_w, v_ffn_conv_b, v_ffn_w_down)))

    big = [n for n in SHARDED if n in MATMUL_WEIGHTS]
    vec = [n for n in SHARDED if n not in MATMUL_WEIGHTS]
    shards = {n: wl[n].astype(bf16) for n in big}
    shards['ab_w_in'] = jnp.pad(shards['ab_w_in'], ((0, 0), (0, 0), (0, AB_SHARD_PAD - AB_SHARD)))
    shards['ffn_w_up'] = jnp.pad(shards['ffn_w_up'], ((0, 0), (0, 0), (0, FF_PAD - FF_SHARD)))
    shards['vec'] = _pack([wl[n] for n in vec])
    pos = _position()
    layer_items = [_layer_items(l) for l in range(DEPTH)]
    lands = [[_own_land(nm, shards[nm][li], pos) for nm, li in items] for items in layer_items]
    first = _gather_now(layer_items[0] + [('vec', None)], shards, lands[0] + [_own_land('vec', shards['vec'], pos)])
    rest = layer_items[1] + layer_items[2] + layer_items[3]
    sems, thru, rest_lands = _gather_start(rest, shards, lands[1] + lands[2] + lands[3], first[-1])
    flight = {'sems': list(sems), 'shards': thru, 'lands': list(rest_lands)}

    full = {n: wl[n] for n in REPLICATED}
    for n, a in zip(vec, _unpack(first[-1], [wl[n].shape for n in vec], lead=True)):
        full[n] = _merge_shards(a, SHARD_AXIS[n])
    full['ffn_conv_w'] = _block_pad(full['ffn_conv_w'], 2, 4, FF_PAD)
    full['ffn_conv_b'] = _block_pad(full['ffn_conv_b'], 1, 4, FF_PAD)
    for n in big:
        full[n] = {}

    def fetch(l, x_in):
        items = layer_items[l]
        if l == 0:
            got = first[:len(items)]
        else:
            k = len(items)
            nl = 2 * _n_lanes(items)
            flight['shards'], got = _gather_wait(items, flight['sems'][:nl], flight['shards'], flight['lands'][:k],
                                                 x_in, "gather_wait_%d" % l)
            flight['sems'], flight['lands'] = flight['sems'][nl:], flight['lands'][k:]
        for (nm, li), a in zip(items, got):
            if nm == 'ab_w_in':
                a = _ab_permute(_block_unpad(a, 1, N_DEV, AB_SHARD))
            elif nm in ('ab_w_out', 'c_w_out'):
                a = a.reshape(D_MODEL, D_MODEL)
            elif nm in ('c_gate_a_w', 'c_gate_x_w'):
                a = a.reshape(HEADS, LRU_BLOCK, LRU_BLOCK)
            elif nm == 'ffn_w_down':
                a = a.reshape(D_FFP, D_MODEL)
            full[nm][li] = a

    loss, dx, grads = _local_step(x[0], loss_target[0], full, fetch)

    grads['ffn_conv_w'] = _block_unpad(grads['ffn_conv_w'], 2, 4, FF_SHARD)
    grads['ffn_conv_b'] = _block_unpad(grads['ffn_conv_b'], 1, 4, FF_SHARD)
    fulls = {n: grads[n].astype(bf16).reshape(BLOCK_LAYOUT[n][0]) for n in big if n != 'ab_w_in'}
    fulls['ab_w_in'] = _block_pad(_ab_unpermute(grads['ab_w_in']), 2, N_DEV, AB_SHARD_PAD)
    fulls = {n: fulls[n] for n in big}
    fulls['vec'] = _pack([_split_shards(grads[n], SHARD_AXIS[n]) for n in vec], lead=N_DEV)
    chip, rep_chip, recv, rrep = _exchange_grads(fulls, _pack([grads[n] for n in REPLICATED]))
    mine = (2 * lax.axis_index("x") + lax.axis_index("y")).astype(jnp.int32).reshape(1)
    res = {}
    for n in big:
        shp = wl[n].shape
        r, c = int(np.prod(shp[:-1])), shp[-1]
        outs = _sum_adamw(recv[n].reshape(4, r, -1), chip[n].reshape(4, r, -1), mine, wl[n].reshape(r, c),
                          ml[n].reshape(r, c), vl[n].reshape(r, c), "adamw_" + n)
        for kind, o in zip(("grad", "delta", "new_m", "new_v"), outs):
            res[kind, n] = o.reshape(shp)
    for names, parts, own, tag in ((vec, recv['vec'], chip['vec'], "adamw_vectors"),
                                   (REPLICATED, rrep, rep_chip, "adamw_replicated")):
        outs = _sum_adamw(parts, own, mine, _pack([wl[n] for n in names]), _pack([ml[n] for n in names]),
                          _pack([vl[n] for n in names]), tag)
        for kind, o in zip(("grad", "delta", "new_m", "new_v"), outs):
            for n, a in zip(names, _unpack(o, [wl[n].shape for n in names])):
                res[kind, n] = a

    loss = lax.psum(loss, ("x", "y", "c"))
    return (loss, dx[None], *[res[kind, n] for kind in ("grad", "delta", "new_m", "new_v") for n in WEIGHTS])
```

```python
import functools

import numpy as np
import jax
import jax.numpy as jnp
from jax import lax
from jax.experimental import pallas as pl
from jax.experimental.pallas import tpu as pltpu

f32 = jnp.float32
bf16 = jnp.bfloat16
HI = lax.Precision.HIGHEST
MESH = pl.DeviceIdType.MESH

N_DEV = 8
D_MODEL = 1024
DEPTH = 4
EPS = 1e-6
F_FLOOR = 1e-30
HEADS = 4
HEAD_DIM = 128
GDN_WIDTH = 512
GDN_CONV = 4
GDN_CHUNK = 64
HGRN_CHUNK = 16
HGRN_STEP = 128
MIX_WIDTH = 1024
AB_COLS = 4104
AB_PAD = 4224
LRU_WIDTH = 1024
LRU_BLOCK = 256
LRU_CONV = 4
RG_C = 8.0
D_FF = 2816
FF_SHARD = 704
FF_PAD = 768
D_FFP = 4 * FF_PAD
FF_ROWS = 352
AB_SHARD, AB_SHARD_PAD = 513, 640
FFN_CONV = 3
ADAM_LR, ADAM_B1, ADAM_B2, ADAM_EPS, ADAM_WD, ADAM_STEP = 0.001, 0.9, 0.999, 1e-08, 0.01, 10
VMEM_LIMIT = 56 * 1024 * 1024
PACK_LANES = 512
PACK_ROWS = 256

OFF_Q, OFF_K, OFF_V, OFF_Z, OFF_QB, OFF_FB, OFF_IB, OFF_GB, OFF_BA = 0, 512, 1024, 1536, 2048, 2560, 3072, 3584, 4096

WEIGHTS = ['norm_mix', 'norm_ffn', 'norm_final', 'ab_w_in', 'gdn_conv_w', 'gdn_a_log', 'gdn_dt_bias', 'gdn_norm',
           'hgrn_lower_bounds', 'hgrn_norm', 'ab_w_out', 'c_w_in', 'c_conv_w', 'c_conv_b', 'c_gate_a_w', 'c_gate_a_b',
           'c_gate_x_w', 'c_gate_x_b', 'c_lambda', 'c_w_out', 'ffn_w_up', 'ffn_conv_w', 'ffn_conv_b', 'ffn_w_down']
SHARD_AXIS = {'norm_mix': None, 'norm_ffn': None, 'norm_final': None, 'ab_w_in': 2, 'gdn_conv_w': 2, 'gdn_a_log': None,
              'gdn_dt_bias': None, 'gdn_norm': None, 'hgrn_lower_bounds': None, 'hgrn_norm': None, 'ab_w_out': 1,
              'c_w_in': 2, 'c_conv_w': 2, 'c_conv_b': 1, 'c_gate_a_w': 2, 'c_gate_a_b': 1, 'c_gate_x_w': 2,
              'c_gate_x_b': 1, 'c_lambda': 1, 'c_w_out': 1, 'ffn_w_up': 2, 'ffn_conv_w': 2, 'ffn_conv_b': None,
              'ffn_w_down': 1}
MATMUL_WEIGHTS = ('ab_w_in', 'ab_w_out', 'c_w_in', 'c_gate_a_w', 'c_gate_x_w', 'c_w_out', 'ffn_w_up', 'ffn_w_down')
SHARDED = [n for n in WEIGHTS if SHARD_AXIS[n] is not None]
REPLICATED = [n for n in WEIGHTS if SHARD_AXIS[n] is None]


def _tile(n, prefs=(512, 384, 256, 128)):
    for p in prefs:
        if n % p == 0:
            return p
    return n


def _cparams(sem=None):
    kw = dict(vmem_limit_bytes=VMEM_LIMIT)
    if sem is not None:
        kw['dimension_semantics'] = sem
    return pltpu.CompilerParams(**kw)


def _sds(shape, dtype):
    return jax.ShapeDtypeStruct(tuple(shape), dtype)


def _sigmoid(x):
    return 1.0 / (1.0 + jnp.exp(-x))


def _silu(x):
    return x * _sigmoid(x)


def _log1p(x):
    u = 1.0 + x
    return jnp.where(u == 1.0, x, jnp.log(u) * (x / jnp.where(u == 1.0, 1.0, u - 1.0)))


def _softplus(x):
    return jnp.maximum(x, 0.0) + _log1p(jnp.exp(-jnp.abs(x)))


def _expm1(x):
    small = jnp.abs(x) < 0.05
    xs = jnp.where(small, x, 0.0)
    series = xs * (1.0 + xs * (0.5 + xs * (1.0 / 6.0 + xs * (1.0 / 24.0 + xs * (1.0 / 120.0)))))
    return jnp.where(small, series, jnp.exp(x) - 1.0)


def _gelu(x):
    return 0.5 * x * (1.0 + jnp.tanh(0.7978845608028654 * (x + 0.044715 * x * x * x)))


def _rms(x, gain):
    return x * lax.rsqrt(jnp.mean(x * x, axis=-1, keepdims=True) + EPS) * gain


def _dot(a, b, dims=((1,), (0,)), precision=None):
    return lax.dot_general(a, b, (dims, ((), ())), precision=precision, preferred_element_type=f32)


def _bdot(a, b, dims=((1,), (0,))):
    return _dot(a.astype(bf16), b.astype(bf16), dims)


NT = ((1,), (1,))
TN = ((0,), (0,))


def _shift_down(x, k):
    if k == 0:
        return x
    row = lax.broadcasted_iota(jnp.int32, x.shape, 0)
    return jnp.where(row >= k, pltpu.roll(x, k, 0), 0.0)


def _shift_up(x, k, fill=0.0):
    if k == 0:
        return x
    n = x.shape[0]
    row = lax.broadcasted_iota(jnp.int32, x.shape, 0)
    return jnp.where(row < n - k, pltpu.roll(x, n - k, 0), fill)


def _conv_fwd(x, w_ref, width):
    acc = w_ref[width - 1:width, :] * x
    for k in range(width - 1):
        acc = acc + w_ref[k:k + 1, :] * _shift_down(x, width - 1 - k)
    return acc


def _conv_bwd(x, dout, w_ref, dw_ref, width):
    dx = w_ref[width - 1:width, :] * dout
    dw_ref[width - 1:width, :] = jnp.sum(dout * x, axis=0, keepdims=True)
    for k in range(width - 1):
        s = width - 1 - k
        dx = dx + w_ref[k:k + 1, :] * _shift_up(dout, s)
        dw_ref[k:k + 1, :] = jnp.sum(dout * _shift_down(x, s), axis=0, keepdims=True)
    return dx


MM_VMEM_BUDGET = 36 * 1024 * 1024
MM_MAX_TILE = 1024 * 1024


def _mm_tiles(m, n, k, out_bytes):
    best = None
    for tm in (1024, 512, 384, 256, 128):
        if m % tm:
            continue
        for tn in range(1536, 0, -128):
            if n % tn or tm * tn > MM_MAX_TILE:
                continue
            if 2 * (tm * k * 2 + k * tn * 2 + tm * tn * out_bytes) <= MM_VMEM_BUDGET and (best is None or tm * tn > best[0]):
                best = (tm * tn, tm, tn)
    return (best[1], best[2]) if best else (_tile(m), _tile(n))


def _mm(a, b, *, ta=False, tb=False, add=None, out_dtype=f32, name):
    m, k = (a.shape[1], a.shape[0]) if ta else a.shape
    n = b.shape[0] if tb else b.shape[1]
    tm, tn = _mm_tiles(m, n, k, jnp.dtype(out_dtype).itemsize + (4 if add is not None else 0))
    dims = ((0 if ta else 1,), (1 if tb else 0,))

    def body(*refs):
        a_ref, b_ref = refs[0], refs[1]
        o_ref = refs[-1]
        r = _dot(a_ref[...], b_ref[...], dims)
        if add is not None:
            r = r + refs[2][...]
        o_ref[...] = r.astype(out_dtype)

    a_spec = pl.BlockSpec((k, tm), lambda j, i: (0, i)) if ta else pl.BlockSpec((tm, k), lambda j, i: (i, 0))
    b_spec = pl.BlockSpec((tn, k), lambda j, i: (j, 0)) if tb else pl.BlockSpec((k, tn), lambda j, i: (0, j))
    o_spec = pl.BlockSpec((tm, tn), lambda j, i: (i, j))
    ins, specs = [a, b], [a_spec, b_spec]
    if add is not None:
        ins.append(add)
        specs.append(o_spec)
    return pl.pallas_call(body, name=name, grid=(n // tn, m // tm), in_specs=specs, out_specs=o_spec,
                          out_shape=_sds((m, n), out_dtype), compiler_params=_cparams(("parallel", "parallel")))(*ins)


def _rms_fwd(x, gain, name):
    t, d = x.shape
    tr = _tile(t, (256, 128))

    def body(x_ref, g_ref, h_ref):
        h_ref[...] = _rms(x_ref[...], g_ref[...]).astype(bf16)

    return pl.pallas_call(body, name=name, grid=(t // tr,),
                          in_specs=[pl.BlockSpec((tr, d), lambda i: (i, 0)), pl.BlockSpec((1, d), lambda i: (0, 0))],
                          out_specs=pl.BlockSpec((tr, d), lambda i: (i, 0)), out_shape=_sds((t, d), bf16),
                          compiler_params=_cparams(("parallel",)))(x, gain)


def _rms_bwd(x, gain, dh, dres, name):
    t, d = x.shape
    tr = _tile(t, (256, 128))

    def body(x_ref, g_ref, dh_ref, dres_ref, dx_ref, dxb_ref, dg_ref):
        _, vjp = jax.vjp(_rms, x_ref[...], g_ref[...])
        dx, dg = vjp(dh_ref[...])
        dx = dx + dres_ref[...]
        dx_ref[...] = dx
        dxb_ref[...] = dx.astype(bf16)

        @pl.when(pl.program_id(0) == 0)
        def _():
            dg_ref[...] = jnp.zeros_like(dg_ref)

        dg_ref[...] += dg

    row = pl.BlockSpec((tr, d), lambda i: (i, 0))
    vec = pl.BlockSpec((1, d), lambda i: (0, 0))
    return pl.pallas_call(body, name=name, grid=(t // tr,), in_specs=[row, vec, row, row], out_specs=[row, row, vec],
                          out_shape=[_sds((t, d), f32), _sds((t, d), bf16), _sds((1, d), f32)],
                          compiler_params=_cparams(("arbitrary",)))(x, gain, dh, dres)


def _loss_head(x, gain, target):
    t, d = x.shape
    tr = _tile(t, (256, 128))

    def f(xv, g, tgt):
        err = _rms(xv, g) - tgt
        return 0.5 * jnp.sum(jnp.mean(err * err, axis=-1, keepdims=True), axis=0, keepdims=True)

    def body(x_ref, g_ref, t_ref, loss_ref, dx_ref, dxb_ref, dg_ref):
        loss, vjp = jax.vjp(lambda xv, g: f(xv, g, t_ref[...]), x_ref[...], g_ref[...])
        dx, dg = vjp(jnp.ones((1, 1), f32))
        dx_ref[...] = dx
        dxb_ref[...] = dx.astype(bf16)

        @pl.when(pl.program_id(0) == 0)
        def _():
            dg_ref[...] = jnp.zeros_like(dg_ref)
            loss_ref[...] = jnp.zeros_like(loss_ref)

        dg_ref[...] += dg
        loss_ref[...] += jnp.broadcast_to(loss, loss_ref.shape)

    row = pl.BlockSpec((tr, d), lambda i: (i, 0))
    vec = pl.BlockSpec((1, d), lambda i: (0, 0))
    one = pl.BlockSpec((8, 128), lambda i: (0, 0))
    return pl.pallas_call(body, name="loss_head", grid=(t // tr,), in_specs=[row, vec, row],
                          out_specs=[one, row, row, vec],
                          out_shape=[_sds((8, 128), f32), _sds((t, d), f32), _sds((t, d), bf16), _sds((1, d), f32)],
                          compiler_params=_cparams(("arbitrary",)))(x, gain, target)


def _ffn_act_fwd(u, conv_w, conv_b):
    t = u.shape[0]
    tc = FF_PAD // 2
    nb = D_FFP // tc

    def body(g_ref, v_ref, w_ref, b_ref, a_ref):
        gc = _conv_fwd(g_ref[...], w_ref, FFN_CONV) + b_ref[...]
        a_ref[...] = (_silu(gc) * v_ref[...]).astype(bf16)

    return pl.pallas_call(
        body, name="ffn_act_fwd", grid=(nb,),
        in_specs=[pl.BlockSpec((t, tc), lambda j: (0, j)), pl.BlockSpec((t, tc), lambda j: (0, j + nb)),
                  pl.BlockSpec((FFN_CONV, tc), lambda j: (0, j)), pl.BlockSpec((1, tc), lambda j: (0, j))],
        out_specs=pl.BlockSpec((t, tc), lambda j: (0, j)), out_shape=_sds((t, D_FFP), bf16),
        compiler_params=_cparams(("parallel",)))(u, u, conv_w, conv_b)


def _ffn_act_bwd(u, conv_w, conv_b, da):
    t = u.shape[0]
    tc = FF_PAD // 2
    nb = D_FFP // tc

    def act(gc, val):
        return _silu(gc) * val

    def body(g_ref, v_ref, w_ref, b_ref, da_ref, a_ref, dg_ref, dv_ref, dw_ref, db_ref):
        gp = g_ref[...]
        gc = _conv_fwd(gp, w_ref, FFN_CONV) + b_ref[...]
        a, vjp = jax.vjp(act, gc, v_ref[...])
        dgc, dval = vjp(da_ref[...])
        a_ref[...] = a.astype(bf16)
        dv_ref[...] = dval.astype(bf16)
        db_ref[...] = jnp.sum(dgc, axis=0, keepdims=True)
        dg_ref[...] = _conv_bwd(gp, dgc, w_ref, dw_ref, FFN_CONV).astype(bf16)

    col = pl.BlockSpec((t, tc), lambda j: (0, j))
    return pl.pallas_call(
        body, name="ffn_act_bwd", grid=(nb,),
        in_specs=[col, pl.BlockSpec((t, tc), lambda j: (0, j + nb)), pl.BlockSpec((FFN_CONV, tc), lambda j: (0, j)),
                  pl.BlockSpec((1, tc), lambda j: (0, j)), col],
        out_specs=[col, col, col, pl.BlockSpec((FFN_CONV, tc), lambda j: (0, j)), pl.BlockSpec((1, tc), lambda j: (0, j))],
        out_shape=[_sds((t, D_FFP), bf16), _sds((t, D_FFP), bf16), _sds((t, D_FFP), bf16), _sds((FFN_CONV, D_FFP), f32),
                   _sds((1, D_FFP), f32)],
        compiler_params=_cparams(("parallel",)))(u, u, conv_w, conv_b, da)


def _lru_gates(xc, ra, ia, lam):
    r = _sigmoid(ra)
    i = _sigmoid(ia)
    log_a = -RG_C * r * _softplus(-lam)
    a = jnp.exp(log_a)
    u = jnp.sqrt(jnp.maximum(-_expm1(2.0 * log_a), 0.0)) * (i * xc)
    return a, u


def _lin_scan(a, u):
    n = a.shape[0]
    row = lax.broadcasted_iota(jnp.int32, a.shape, 0)
    s = 1
    while s < n:
        keep = row >= s
        u = a * jnp.where(keep, pltpu.roll(u, s, 0), 0.0) + u
        a = a * jnp.where(keep, pltpu.roll(a, s, 0), 1.0)
        s *= 2
    return u


def _rev_scan(a_next, d):
    n = d.shape[0]
    row = lax.broadcasted_iota(jnp.int32, d.shape, 0)
    a = a_next
    s = 1
    while s < n:
        keep = row < n - s
        d = a * jnp.where(keep, pltpu.roll(d, n - s, 0), 0.0) + d
        a = a * jnp.where(keep, pltpu.roll(a, n - s, 0), 1.0)
        s *= 2
    return d


def _col_conv_fwd(p, col_off, conv_w, conv_b, width, tc, name):
    t = p.shape[0]
    c = conv_w.shape[1]
    ob = col_off // tc

    def body(x_ref, w_ref, b_ref, o_ref):
        o_ref[...] = _conv_fwd(x_ref[...], w_ref, width) + b_ref[...]

    return pl.pallas_call(
        body, name=name, grid=(c // tc,),
        in_specs=[pl.BlockSpec((t, tc), lambda j: (0, j + ob)), pl.BlockSpec((width, tc), lambda j: (0, j)),
                  pl.BlockSpec((1, tc), lambda j: (0, j))],
        out_specs=pl.BlockSpec((t, tc), lambda j: (0, j)), out_shape=_sds((t, c), f32),
        compiler_params=_cparams(("parallel",)))(p, conv_w, conv_b)


def _col_conv_bwd(p, col_off, conv_w, dxc, width, tc, name):
    t = p.shape[0]
    c = conv_w.shape[1]
    ob = col_off // tc

    def body(x_ref, w_ref, d_ref, dx_ref, dw_ref, db_ref):
        d = d_ref[...]
        db_ref[...] = jnp.sum(d, axis=0, keepdims=True)
        dx_ref[...] = _conv_bwd(x_ref[...], d, w_ref, dw_ref, width).astype(bf16)

    col = pl.BlockSpec((t, tc), lambda j: (0, j))
    return pl.pallas_call(
        body, name=name, grid=(c // tc,),
        in_specs=[pl.BlockSpec((t, tc), lambda j: (0, j + ob)), pl.BlockSpec((width, tc), lambda j: (0, j)), col],
        out_specs=[col, pl.BlockSpec((width, tc), lambda j: (0, j)), pl.BlockSpec((1, tc), lambda j: (0, j))],
        out_shape=[_sds((t, c), bf16), _sds((width, c), f32), _sds((1, c), f32)],
        compiler_params=_cparams(("parallel",)))(p, conv_w, dxc)


def _lru_fwd(p, xc, wa, ba, wx, bx, lam):
    t = p.shape[0]
    bw = LRU_BLOCK

    def body(y_ref, xc_ref, wa_ref, ba_ref, wx_ref, bx_ref, lam_ref, out_ref, hs_ref, a_ref):
        xc_v = xc_ref[...]
        xb = xc_v.astype(bf16)
        ra = _dot(xb, wa_ref[0]) + ba_ref[...]
        ia = _dot(xb, wx_ref[0]) + bx_ref[...]
        a, u = _lru_gates(xc_v, ra, ia, lam_ref[...])
        a_ref[...] = a
        hs = _lin_scan(a, u)
        hs_ref[...] = hs
        out_ref[...] = (hs * _gelu(y_ref[...])).astype(bf16)

    col = pl.BlockSpec((t, bw), lambda h: (0, h))
    vec = pl.BlockSpec((1, bw), lambda h: (0, h))
    mat = pl.BlockSpec((1, bw, bw), lambda h: (h, 0, 0))
    return pl.pallas_call(
        body, name="lru_fwd", grid=(HEADS,), in_specs=[col, col, mat, vec, mat, vec, vec], out_specs=[col, col, col],
        out_shape=[_sds((t, LRU_WIDTH), bf16), _sds((t, LRU_WIDTH), f32), _sds((t, LRU_WIDTH), f32)],
        compiler_params=_cparams(("parallel",)))(p, xc, wa, ba, wx, bx, lam)


def _lru_bwd_scan(p, a, hs, dout):
    t = p.shape[0]
    bw = LRU_BLOCK

    def body(y_ref, a_ref, hs_ref, do_ref, dy_ref, da_ref, du_ref):
        hs_v = hs_ref[...]
        do = do_ref[...]
        gate, vjp = jax.vjp(_gelu, y_ref[...])
        dy_ref[...] = vjp(do * hs_v)[0].astype(bf16)
        g = _rev_scan(_shift_up(a_ref[...], 1), do * gate)
        du_ref[...] = g
        da_ref[...] = g * _shift_down(hs_v, 1)

    col = pl.BlockSpec((t, bw), lambda h: (0, h))
    return pl.pallas_call(
        body, name="lru_bwd_scan", grid=(HEADS,), in_specs=[col, col, col, col], out_specs=[col, col, col],
        out_shape=[_sds((t, LRU_WIDTH), bf16), _sds((t, LRU_WIDTH), f32), _sds((t, LRU_WIDTH), f32)],
        compiler_params=_cparams(("parallel",)))(p, a, hs, dout)


def _lru_bwd_gates(xc, da, du, wa, ba, wx, bx, lam):
    t = xc.shape[0]
    bw = LRU_BLOCK
    tr = _tile(t, (512, 256, 128))

    def body(xc_ref, da_ref, du_ref, wa_ref, ba_ref, wx_ref, bx_ref, lam_ref,
             dxc_ref, dwa_ref, dwx_ref, dba_ref, dbx_ref, dlam_ref):
        xc_v = xc_ref[...]
        xb = xc_v.astype(bf16)
        ra = _dot(xb, wa_ref[0]) + ba_ref[...]
        ia = _dot(xb, wx_ref[0]) + bx_ref[...]
        _, vjp = jax.vjp(_lru_gates, xc_v, ra, ia, lam_ref[...])
        dxc, dra, dia, dlam = vjp((da_ref[...], du_ref[...]))
        drb, dib = dra.astype(bf16), dia.astype(bf16)
        dxc_ref[...] = dxc + _dot(drb, wa_ref[0], NT) + _dot(dib, wx_ref[0], NT)

        @pl.when(pl.program_id(1) == 0)
        def _():
            dwa_ref[...] = jnp.zeros_like(dwa_ref)
            dwx_ref[...] = jnp.zeros_like(dwx_ref)
            dba_ref[...] = jnp.zeros_like(dba_ref)
            dbx_ref[...] = jnp.zeros_like(dbx_ref)
            dlam_ref[...] = jnp.zeros_like(dlam_ref)

        dwa_ref[0] += _dot(xb, drb, TN)
        dwx_ref[0] += _dot(xb, dib, TN)
        dba_ref[...] += jnp.sum(dra, axis=0, keepdims=True)
        dbx_ref[...] += jnp.sum(dia, axis=0, keepdims=True)
        dlam_ref[...] += dlam

    tile = pl.BlockSpec((tr, bw), lambda h, i: (i, h))
    vec = pl.BlockSpec((1, bw), lambda h, i: (0, h))
    mat = pl.BlockSpec((1, bw, bw), lambda h, i: (h, 0, 0))
    return pl.pallas_call(
        body, name="lru_bwd_gates", grid=(HEADS, t // tr), in_specs=[tile, tile, tile, mat, vec, mat, vec, vec],
        out_specs=[tile, mat, mat, vec, vec, vec],
        out_shape=[_sds((t, LRU_WIDTH), f32), _sds((HEADS, bw, bw), f32), _sds((HEADS, bw, bw), f32),
                   _sds((1, LRU_WIDTH), f32), _sds((1, LRU_WIDTH), f32), _sds((1, LRU_WIDTH), f32)],
        compiler_params=_cparams(("parallel", "arbitrary")))(xc, da, du, wa, ba, wx, bx, lam)


def _gdn_pre_fn(cq, ck, cv, ba, alog, dtb, h):
    q, k, v = _silu(cq), _silu(ck), _silu(cv)
    q = q * lax.rsqrt(jnp.sum(q * q, axis=-1, keepdims=True) + EPS) * (HEAD_DIM ** -0.5)
    k = k * lax.rsqrt(jnp.sum(k * k, axis=-1, keepdims=True) + EPS)
    lane = lax.broadcasted_iota(jnp.int32, (1, HEAD_DIM), 1)
    mb = (lane == h).astype(f32)
    ma = (lane == HEADS + h).astype(f32)
    beta_raw = jnp.sum(ba * mb, axis=-1, keepdims=True)
    alpha = jnp.sum(ba * ma, axis=-1, keepdims=True)
    al = jnp.sum(alog * mb, axis=-1, keepdims=True)
    db = jnp.sum(dtb * mb, axis=-1, keepdims=True)
    beta = _sigmoid(beta_raw)
    g = -jnp.exp(al) * _softplus(alpha + db)
    return q, k, v, jnp.broadcast_to(beta, q.shape), jnp.broadcast_to(g, q.shape)


def _gdn_pre_fwd(p, conv_w, alog, dtb):
    t = p.shape[0]
    hd = HEAD_DIM

    def body(pq_ref, pk_ref, pv_ref, ba_ref, wq_ref, wk_ref, wv_ref, al_ref, dt_ref, q_ref, k_ref, v_ref, b_ref, g_ref):
        h = pl.program_id(0)
        cq = _conv_fwd(pq_ref[...], wq_ref, GDN_CONV)
        ck = _conv_fwd(pk_ref[...], wk_ref, GDN_CONV)
        cv = _conv_fwd(pv_ref[...], wv_ref, GDN_CONV)
        q, k, v, be, ge = _gdn_pre_fn(cq, ck, cv, ba_ref[...], al_ref[...], dt_ref[...], h)
        q_ref[...], k_ref[...], v_ref[...], b_ref[...], g_ref[...] = q, k, v, be, ge

    def pcol(off):
        return pl.BlockSpec((t, hd), lambda h: (0, h + off // hd))

    def wcol(off):
        return pl.BlockSpec((GDN_CONV, hd), lambda h: (0, h + off // hd))

    vec = pl.BlockSpec((1, hd), lambda h: (0, 0))
    out = pl.BlockSpec((t, hd), lambda h: (0, h))
    return pl.pallas_call(
        body, name="gdn_pre_fwd", grid=(HEADS,),
        in_specs=[pcol(OFF_Q), pcol(OFF_K), pcol(OFF_V), pl.BlockSpec((t, hd), lambda h: (0, OFF_BA // hd)),
                  wcol(0), wcol(GDN_WIDTH), wcol(2 * GDN_WIDTH), vec, vec],
        out_specs=[out] * 5, out_shape=[_sds((t, GDN_WIDTH), f32)] * 5,
        compiler_params=_cparams(("parallel",)))(p, p, p, p, conv_w, conv_w, conv_w, alog, dtb)


def _gdn_pre_bwd(p, conv_w, alog, dtb, dq, dk, dv, dbe, dge):
    t = p.shape[0]
    hd = HEAD_DIM

    def body(pq_ref, pk_ref, pv_ref, ba_ref, wq_ref, wk_ref, wv_ref, al_ref, dt_ref,
             dq_ref, dk_ref, dv_ref, dbe_ref, dge_ref,
             opq_ref, opk_ref, opv_ref, dba_ref, dwq_ref, dwk_ref, dwv_ref, dal_ref, ddt_ref):
        h = pl.program_id(0)
        pq, pk, pv = pq_ref[...], pk_ref[...], pv_ref[...]
        cq = _conv_fwd(pq, wq_ref, GDN_CONV)
        ck = _conv_fwd(pk, wk_ref, GDN_CONV)
        cv = _conv_fwd(pv, wv_ref, GDN_CONV)
        _, vjp = jax.vjp(functools.partial(_gdn_pre_fn, h=h), cq, ck, cv, ba_ref[...], al_ref[...], dt_ref[...])
        dcq, dck, dcv, dba, dal, ddt = vjp((dq_ref[...], dk_ref[...], dv_ref[...], dbe_ref[...], dge_ref[...]))
        opq_ref[...] = _conv_bwd(pq, dcq, wq_ref, dwq_ref, GDN_CONV).astype(bf16)
        opk_ref[...] = _conv_bwd(pk, dck, wk_ref, dwk_ref, GDN_CONV).astype(bf16)
        opv_ref[...] = _conv_bwd(pv, dcv, wv_ref, dwv_ref, GDN_CONV).astype(bf16)

        @pl.when(h == 0)
        def _():
            dba_ref[...] = jnp.zeros_like(dba_ref)
            dal_ref[...] = jnp.zeros_like(dal_ref)
            ddt_ref[...] = jnp.zeros_like(ddt_ref)

        dba_ref[...] += dba
        dal_ref[...] += dal
        ddt_ref[...] += ddt

    def pcol(off):
        return pl.BlockSpec((t, hd), lambda h: (0, h + off // hd))

    def wcol(off):
        return pl.BlockSpec((GDN_CONV, hd), lambda h: (0, h + off // hd))

    vec = pl.BlockSpec((1, hd), lambda h: (0, 0))
    col = pl.BlockSpec((t, hd), lambda h: (0, h))
    full = pl.BlockSpec((t, hd), lambda h: (0, 0))
    wout = pl.BlockSpec((GDN_CONV, hd), lambda h: (0, h))
    return pl.pallas_call(
        body, name="gdn_pre_bwd", grid=(HEADS,),
        in_specs=[pcol(OFF_Q), pcol(OFF_K), pcol(OFF_V), pl.BlockSpec((t, hd), lambda h: (0, OFF_BA // hd)),
                  wcol(0), wcol(GDN_WIDTH), wcol(2 * GDN_WIDTH), vec, vec, col, col, col, col, col],
        out_specs=[col, col, col, full, wout, wout, wout, vec, vec],
        out_shape=[_sds((t, GDN_WIDTH), bf16)] * 3 + [_sds((t, hd), f32)] + [_sds((GDN_CONV, GDN_WIDTH), f32)] * 3
        + [_sds((1, hd), f32)] * 2,
        compiler_params=_cparams(("arbitrary",)))(p, p, p, p, conv_w, conv_w, conv_w, alog, dtb, dq, dk, dv, dbe, dge)


BNN = (((2,), (1,)), ((0,), (0,)))
BNT = (((2,), (2,)), ((0,), (0,)))
BTN = (((1,), (1,)), ((0,), (0,)))


def _hdot(a, b, dn=BNN, precision=None):
    return lax.dot_general(a, b, dn, precision=precision, preferred_element_type=f32)


def _hbdot(a, b, dn=BNN):
    return _hdot(a.astype(bf16), b.astype(bf16), dn)


def _tri_inverse(a):
    c = a.shape[-1]
    r = lax.broadcasted_iota(jnp.int32, (c, c), 0)
    col = lax.broadcasted_iota(jnp.int32, (c, c), 1)
    m = -a
    inv = jnp.where(r == col, 1.0, 0.0) + m
    s = 2
    while s < c:
        m = _hdot(m, m, precision=HI)
        inv = inv + _hdot(inv, m, precision=HI)
        s *= 2
    return inv


def _gdn_chunk(s, q, k, v, ge, be):
    nh, c, _ = q.shape
    r = lax.broadcasted_iota(jnp.int32, (c, c), 0)
    col = lax.broadcasted_iota(jnp.int32, (c, c), 1)
    causal = r >= col
    tri = jnp.broadcast_to(causal.astype(f32), (nh, c, c))
    gc = _hdot(tri, ge, precision=HI)
    gcc = gc[:, :, :c]
    gcr = jnp.swapaxes(gc, 1, 2)[:, :c, :]
    decay = jnp.where(causal, jnp.exp(jnp.where(causal, gcc - gcr, 0.0)), 0.0)
    kb = k * be
    lower = jnp.where(r > col, _hbdot(kb, k, BNT) * decay, 0.0)
    tinv = _tri_inverse(lower)
    egc = jnp.exp(gc)
    u = _hdot(tinv, v * be, precision=HI)
    w = _hdot(tinv, kb * egc, precision=HI)
    attn = _hbdot(q, k, BNT) * decay
    gl = gc[:, c - 1:c, :]
    v_new = u - _hbdot(w, s)
    o = _hbdot(q * egc, s) + _hbdot(attn, v_new)
    s_new = s * jnp.exp(gl) + _hbdot(k * jnp.exp(gl - gc), v_new, BTN)
    return o, s_new


def _heads_major(ref):
    return jnp.stack([ref[:, h * HEAD_DIM:(h + 1) * HEAD_DIM] for h in range(HEADS)])


def _gdn_core_fwd(q, k, v, ge, be):
    t = q.shape[0]
    c, hd = GDN_CHUNK, HEAD_DIM
    n = t // c

    def body(q_ref, k_ref, v_ref, g_ref, b_ref, o_ref, st_ref, s_ref):
        @pl.when(pl.program_id(0) == 0)
        def _():
            s_ref[...] = jnp.zeros_like(s_ref)

        s = s_ref[...]
        st_ref[:, 0] = s
        o, s_new = _gdn_chunk(s, *[_heads_major(r) for r in (q_ref, k_ref, v_ref, g_ref, b_ref)])
        for h in range(HEADS):
            o_ref[:, h * hd:(h + 1) * hd] = o[h]
        s_ref[...] = s_new

    tile = pl.BlockSpec((c, GDN_WIDTH), lambda i: (i, 0))
    return pl.pallas_call(
        body, name="gdn_core_fwd", grid=(n,), in_specs=[tile] * 5,
        out_specs=[tile, pl.BlockSpec((HEADS, 1, hd, hd), lambda i: (0, i, 0, 0))],
        out_shape=[_sds((t, GDN_WIDTH), f32), _sds((HEADS, n, hd, hd), f32)],
        scratch_shapes=[pltpu.VMEM((HEADS, hd, hd), f32)],
        compiler_params=_cparams(("arbitrary",)))(q, k, v, ge, be)


def _gdn_core_bwd(q, k, v, ge, be, states, do):
    t = q.shape[0]
    c, hd = GDN_CHUNK, HEAD_DIM
    n = t // c

    def body(q_ref, k_ref, v_ref, g_ref, b_ref, st_ref, do_ref, dq_ref, dk_ref, dv_ref, dg_ref, db_ref, ds_ref):
        @pl.when(pl.program_id(0) == 0)
        def _():
            ds_ref[...] = jnp.zeros_like(ds_ref)

        _, vjp = jax.vjp(_gdn_chunk, st_ref[:, 0], *[_heads_major(r) for r in (q_ref, k_ref, v_ref, g_ref, b_ref)])
        ds, *dins = vjp((_heads_major(do_ref), ds_ref[...]))
        ds_ref[...] = ds
        for d_ref, d in zip((dq_ref, dk_ref, dv_ref, dg_ref, db_ref), dins):
            for h in range(HEADS):
                d_ref[:, h * hd:(h + 1) * hd] = d[h]

    tile = pl.BlockSpec((c, GDN_WIDTH), lambda i: (n - 1 - i, 0))
    return pl.pallas_call(
        body, name="gdn_core_bwd", grid=(n,),
        in_specs=[tile] * 5 + [pl.BlockSpec((HEADS, 1, hd, hd), lambda i: (0, n - 1 - i, 0, 0)), tile],
        out_specs=[tile] * 5, out_shape=[_sds((t, GDN_WIDTH), f32)] * 5,
        scratch_shapes=[pltpu.VMEM((HEADS, hd, hd), f32)],
        compiler_params=_cparams(("arbitrary",)))(q, k, v, ge, be, states, do)


def _post_fn(o, z, gain):
    return _rms(o, gain) * _silu(z)


def _post_fwd(o, p, z_off, gain, name):
    t = o.shape[0]
    hd = HEAD_DIM

    def body(o_ref, z_ref, g_ref, y_ref):
        y_ref[...] = _post_fn(o_ref[...], z_ref[...], g_ref[...]).astype(bf16)

    col = pl.BlockSpec((t, hd), lambda h: (0, h))
    return pl.pallas_call(
        body, name=name, grid=(HEADS,),
        in_specs=[col, pl.BlockSpec((t, hd), lambda h: (0, h + z_off // hd)), pl.BlockSpec((1, hd), lambda h: (0, 0))],
        out_specs=col, out_shape=_sds((t, HEADS * hd), bf16), compiler_params=_cparams(("parallel",)))(o, p, gain)


def _post_bwd(o, p, z_off, gain, dmix, mix_off, name):
    t = o.shape[0]
    hd = HEAD_DIM

    def body(o_ref, z_ref, g_ref, dy_ref, do_ref, dz_ref, dg_ref):
        _, vjp = jax.vjp(_post_fn, o_ref[...], z_ref[...], g_ref[...])
        do, dz, dg = vjp(dy_ref[...])
        do_ref[...] = do
        dz_ref[...] = dz.astype(bf16)

        @pl.when(pl.program_id(0) == 0)
        def _():
            dg_ref[...] = jnp.zeros_like(dg_ref)

        dg_ref[...] += dg

    col = pl.BlockSpec((t, hd), lambda h: (0, h))
    vec = pl.BlockSpec((1, hd), lambda h: (0, 0))
    return pl.pallas_call(
        body, name=name, grid=(HEADS,),
        in_specs=[col, pl.BlockSpec((t, hd), lambda h: (0, h + z_off // hd)), vec,
                  pl.BlockSpec((t, hd), lambda h: (0, h + mix_off // hd))],
        out_specs=[col, col, vec], out_shape=[_sds((t, HEADS * hd), f32), _sds((t, HEADS * hd), bf16), _sds((1, hd), f32)],
        compiler_params=_cparams(("arbitrary",)))(o, p, gain, dmix)


def _hgrn_pre_fn(qb, fb, lbw, layer):
    l0, l1 = lbw[0:1, :], lbw[1:2, :]
    m = jnp.maximum(l0, l1)
    e0, e1 = jnp.exp(l0 - m), jnp.exp(l1 - m)
    p0, p1 = e0 / (e0 + e1), e1 / (e0 + e1)
    lb = (p0 - p0) if layer == 0 else ((p0 + p1) - p0)
    f = lb + (1.0 - lb) * _sigmoid(fb)
    return _silu(qb), 1.0 - f, jnp.log(jnp.maximum(f, F_FLOOR))


def _hgrn_pre_fwd(p, lbw, layer):
    t = p.shape[0]
    tc = HEAD_DIM

    def body(qb_ref, fb_ref, lb_ref, q_ref, k_ref, lf_ref):
        q_ref[...], k_ref[...], lf_ref[...] = _hgrn_pre_fn(qb_ref[...], fb_ref[...], lb_ref[...], layer)

    col = pl.BlockSpec((t, tc), lambda j: (0, j))
    return pl.pallas_call(
        body, name="hgrn_pre_fwd", grid=(GDN_WIDTH // tc,),
        in_specs=[pl.BlockSpec((t, tc), lambda j: (0, j + OFF_QB // tc)), pl.BlockSpec((t, tc), lambda j: (0, j + OFF_FB // tc)),
                  pl.BlockSpec((2, tc), lambda j: (0, j))],
        out_specs=[col] * 3, out_shape=[_sds((t, GDN_WIDTH), f32)] * 3,
        compiler_params=_cparams(("parallel",)))(p, p, lbw)


def _hgrn_pre_bwd(p, lbw, layer, dq, dk, dlf):
    t = p.shape[0]
    tc = HEAD_DIM

    def body(qb_ref, fb_ref, lb_ref, dq_ref, dk_ref, dlf_ref, dqb_ref, dfb_ref, dlb_ref):
        _, vjp = jax.vjp(functools.partial(_hgrn_pre_fn, layer=layer), qb_ref[...], fb_ref[...], lb_ref[...])
        dqb, dfb, dlb = vjp((dq_ref[...], dk_ref[...], dlf_ref[...]))
        dqb_ref[...] = dqb.astype(bf16)
        dfb_ref[...] = dfb.astype(bf16)
        dlb_ref[...] = dlb

    col = pl.BlockSpec((t, tc), lambda j: (0, j))
    lb = pl.BlockSpec((2, tc), lambda j: (0, j))
    return pl.pallas_call(
        body, name="hgrn_pre_bwd", grid=(GDN_WIDTH // tc,),
        in_specs=[pl.BlockSpec((t, tc), lambda j: (0, j + OFF_QB // tc)), pl.BlockSpec((t, tc), lambda j: (0, j + OFF_FB // tc)),
                  lb, col, col, col],
        out_specs=[col, col, lb], out_shape=[_sds((t, GDN_WIDTH), bf16)] * 2 + [_sds((2, GDN_WIDTH), f32)],
        compiler_params=_cparams(("parallel",)))(p, p, lbw, dq, dk, dlf)


def _hgrn_step(st, q, k, lf, v):
    c = HGRN_CHUNK
    nh = q.shape[0]
    r2 = lax.broadcasted_iota(jnp.int32, (c, c), 0)
    c2 = lax.broadcasted_iota(jnp.int32, (c, c), 1)
    tri = jnp.broadcast_to((r2 >= c2).astype(f32), (nh, c, c))
    i3 = lax.broadcasted_iota(jnp.int32, (c, c, HEAD_DIM), 0)
    j3 = lax.broadcasted_iota(jnp.int32, (c, c, HEAD_DIM), 1)
    mask = i3 >= j3
    outs = []
    for n in range(q.shape[1] // c):
        sl = slice(n * c, (n + 1) * c)
        qc, kc, lc, vc = q[:, sl], k[:, sl], lf[:, sl], v[:, sl]
        b = _hdot(tri, lc, precision=HI)
        rel = jnp.where(mask, jnp.exp(jnp.where(mask, b[:, :, None, :] - b[:, None, :, :], 0.0)), 0.0)
        scores = jnp.sum(qc[:, :, None, :] * kc[:, None, :, :] * rel, axis=-1)
        bl = b[:, c - 1:c, :]
        o = _hbdot(scores, vc) + _hbdot(qc * jnp.exp(b), st, BNT)
        st = st * jnp.exp(bl) + _hbdot(vc, kc * jnp.exp(bl - b), BTN)
        outs.append(o)
    return jnp.concatenate(outs, axis=1), st


def _hgrn_core_fwd(q, k, lf, p):
    t = q.shape[0]
    hd = HEAD_DIM
    rs = min(HGRN_STEP, t)
    n = t // rs

    def body(q_ref, k_ref, lf_ref, v_ref, o_ref, st_ref, s_ref):
        @pl.when(pl.program_id(0) == 0)
        def _():
            s_ref[...] = jnp.zeros_like(s_ref)

        s = s_ref[...]
        st_ref[:, 0] = s
        o, s_new = _hgrn_step(s, *[_heads_major(r) for r in (q_ref, k_ref, lf_ref, v_ref)])
        for h in range(HEADS):
            o_ref[:, h * hd:(h + 1) * hd] = o[h]
        s_ref[...] = s_new

    tile = pl.BlockSpec((rs, GDN_WIDTH), lambda i: (i, 0))
    return pl.pallas_call(
        body, name="hgrn_core_fwd", grid=(n,),
        in_specs=[tile, tile, tile, pl.BlockSpec((rs, GDN_WIDTH), lambda i: (i, OFF_IB // GDN_WIDTH))],
        out_specs=[tile, pl.BlockSpec((HEADS, 1, hd, hd), lambda i: (0, i, 0, 0))],
        out_shape=[_sds((t, GDN_WIDTH), f32), _sds((HEADS, n, hd, hd), f32)],
        scratch_shapes=[pltpu.VMEM((HEADS, hd, hd), f32)],
        compiler_params=_cparams(("arbitrary",)))(q, k, lf, p)


def _hgrn_core_bwd(q, k, lf, p, states, do):
    t = q.shape[0]
    hd = HEAD_DIM
    rs = min(HGRN_STEP, t)
    n = t // rs

    def body(q_ref, k_ref, lf_ref, v_ref, st_ref, do_ref, dq_ref, dk_ref, dlf_ref, dv_ref, ds_ref):
        @pl.when(pl.program_id(0) == 0)
        def _():
            ds_ref[...] = jnp.zeros_like(ds_ref)

        _, vjp = jax.vjp(_hgrn_step, st_ref[:, 0], *[_heads_major(r) for r in (q_ref, k_ref, lf_ref, v_ref)])
        ds, *dins = vjp((_heads_major(do_ref), ds_ref[...]))
        ds_ref[...] = ds
        for d_ref, d in zip((dq_ref, dk_ref, dlf_ref, dv_ref), dins):
            for h in range(HEADS):
                d_ref[:, h * hd:(h + 1) * hd] = d[h].astype(d_ref.dtype)

    tile = pl.BlockSpec((rs, GDN_WIDTH), lambda i: (n - 1 - i, 0))
    return pl.pallas_call(
        body, name="hgrn_core_bwd", grid=(n,),
        in_specs=[tile, tile, tile, pl.BlockSpec((rs, GDN_WIDTH), lambda i: (n - 1 - i, OFF_IB // GDN_WIDTH)),
                  pl.BlockSpec((HEADS, 1, hd, hd), lambda i: (0, n - 1 - i, 0, 0)), tile],
        out_specs=[tile] * 4, out_shape=[_sds((t, GDN_WIDTH), f32)] * 3 + [_sds((t, GDN_WIDTH), bf16)],
        scratch_shapes=[pltpu.VMEM((HEADS, hd, hd), f32)],
        compiler_params=_cparams(("arbitrary",)))(q, k, lf, p, states, do)


def _row(v):
    return v.reshape(1, -1)


def _pad_lanes(v, n=HEAD_DIM):
    return jnp.pad(v.reshape(1, -1), ((0, 0), (0, n - v.shape[-1])))


def _ffn_fwd(x, w, l):
    h = _rms_fwd(x, _row(w['norm_ffn'][l]), "ffn_norm")
    u = _mm(h, w['ffn_w_up'][l], name="ffn_up")
    a = _ffn_act_fwd(u, w['ffn_conv_w'][l], _row(w['ffn_conv_b'][l]))
    y = _mm(a, w['ffn_w_down'][l], add=x, name="ffn_down")
    return y, (x, h, u)


def _ffn_bwd(saved, w, l, dy, dyb, grads):
    x, h, u = saved
    da = _mm(dyb, w['ffn_w_down'][l], tb=True, name="ffn_down_dx")
    a, dg, dv, dcw, dcb = _ffn_act_bwd(u, w['ffn_conv_w'][l], _row(w['ffn_conv_b'][l]), da)
    grads['ffn_w_down'][l] = _mm(a, dyb, ta=True, out_dtype=bf16, name="ffn_down_dw")
    du = jnp.concatenate([dg, dv], axis=1)
    grads['ffn_w_up'][l] = _mm(h, du, ta=True, out_dtype=bf16, name="ffn_up_dw")
    dh = _mm(du, w['ffn_w_up'][l], tb=True, name="ffn_up_dx")
    dx, dxb, dgain = _rms_bwd(x, _row(w['norm_ffn'][l]), dh, dy, "ffn_norm_bwd")
    grads['ffn_conv_w'][l] = dcw
    grads['ffn_conv_b'][l] = dcb[0]
    grads['norm_ffn'][l] = dgain[0]
    return dx, dxb


def _odd_fwd(x, w, l, j):
    h = _rms_fwd(x, _row(w['norm_mix'][l]), "mix_norm")
    p = _mm(h, w['c_w_in'][j], name="lru_in")
    xc = _col_conv_fwd(p, LRU_WIDTH, w['c_conv_w'][j], _row(w['c_conv_b'][j]), LRU_CONV, 256, "lru_conv_fwd")
    out, hs, a = _lru_fwd(p, xc, w['c_gate_a_w'][j], _row(w['c_gate_a_b'][j]), w['c_gate_x_w'][j],
                          _row(w['c_gate_x_b'][j]), _row(w['c_lambda'][j]))
    y = _mm(out, w['c_w_out'][j], add=x, name="lru_out")
    return y, (x, h, p, xc, out, hs, a)


def _odd_bwd(saved, w, l, j, dy, dyb, grads):
    x, h, p, xc, out, hs, a = saved
    dout = _mm(dyb, w['c_w_out'][j], tb=True, name="lru_out_dx")
    grads['c_w_out'][j] = _mm(out, dyb, ta=True, out_dtype=bf16, name="lru_out_dw")
    dyb_, da, du = _lru_bwd_scan(p, a, hs, dout)
    dxc, dwa, dwx, dba, dbx, dlam = _lru_bwd_gates(xc, da, du, w['c_gate_a_w'][j], _row(w['c_gate_a_b'][j]),
                                                   w['c_gate_x_w'][j], _row(w['c_gate_x_b'][j]), _row(w['c_lambda'][j]))
    dxb_, dcw, dcb = _col_conv_bwd(p, LRU_WIDTH, w['c_conv_w'][j], dxc, LRU_CONV, 256, "lru_conv_bwd")
    dp = jnp.concatenate([dyb_, dxb_], axis=1)
    grads['c_w_in'][j] = _mm(h, dp, ta=True, out_dtype=bf16, name="lru_in_dw")
    dh = _mm(dp, w['c_w_in'][j], tb=True, name="lru_in_dx")
    dx, dxb, dgain = _rms_bwd(x, _row(w['norm_mix'][l]), dh, dy, "mix_norm_bwd")
    grads['c_gate_a_w'][j], grads['c_gate_x_w'][j] = dwa, dwx
    grads['c_gate_a_b'][j], grads['c_gate_x_b'][j], grads['c_lambda'][j] = dba[0], dbx[0], dlam[0]
    grads['c_conv_w'][j], grads['c_conv_b'][j] = dcw, dcb[0]
    grads['norm_mix'][l] = dgain[0]
    return dx, dxb


def _even_fwd(x, w, l, j):
    h = _rms_fwd(x, _row(w['norm_mix'][l]), "mix_norm")
    p = _mm(h, w['ab_w_in'][j], name="ab_in")
    alog, dtb = _pad_lanes(w['gdn_a_log'][j]), _pad_lanes(w['gdn_dt_bias'][j])
    q, k, v, be, ge = _gdn_pre_fwd(p, w['gdn_conv_w'][j], alog, dtb)
    oa, sa = _gdn_core_fwd(q, k, v, ge, be)
    ya = _post_fwd(oa, p, OFF_Z, _row(w['gdn_norm'][j]), "gdn_post_fwd")
    qq, kk, lf = _hgrn_pre_fwd(p, w['hgrn_lower_bounds'], j)
    ob, sb = _hgrn_core_fwd(qq, kk, lf, p)
    yb = _post_fwd(ob, p, OFF_GB, _row(w['hgrn_norm'][j]), "hgrn_post_fwd")
    mix = jnp.concatenate([ya, yb], axis=1)
    y = _mm(mix, w['ab_w_out'][j], add=x, name="ab_out")
    return y, (x, h, p, q, k, v, be, ge, oa, sa, qq, kk, lf, ob, sb, mix)


def _even_bwd(saved, w, l, j, dy, dyb, grads):
    x, h, p, q, k, v, be, ge, oa, sa, qq, kk, lf, ob, sb, mix = saved
    alog, dtb = _pad_lanes(w['gdn_a_log'][j]), _pad_lanes(w['gdn_dt_bias'][j])
    dmix = _mm(dyb, w['ab_w_out'][j], tb=True, name="ab_out_dx")
    grads['ab_w_out'][j] = _mm(mix, dyb, ta=True, out_dtype=bf16, name="ab_out_dw")
    doa, dz, dgn = _post_bwd(oa, p, OFF_Z, _row(w['gdn_norm'][j]), dmix, 0, "gdn_post_bwd")
    dob, dgb, dhn = _post_bwd(ob, p, OFF_GB, _row(w['hgrn_norm'][j]), dmix, GDN_WIDTH, "hgrn_post_bwd")
    dq, dk, dv, dge, dbe = _gdn_core_bwd(q, k, v, ge, be, sa, doa)
    dpq, dpk, dpv, dba, dwq, dwk, dwv, dal, ddt = _gdn_pre_bwd(p, w['gdn_conv_w'][j], alog, dtb, dq, dk, dv, dbe, dge)
    dqq, dkk, dlf, dib = _hgrn_core_bwd(qq, kk, lf, p, sb, dob)
    dqb, dfb, dlb = _hgrn_pre_bwd(p, w['hgrn_lower_bounds'], j, dqq, dkk, dlf)
    dp = jnp.concatenate([dpq, dpk, dpv, dz, dqb, dfb, dib, dgb, dba.astype(bf16)], axis=1)
    grads['ab_w_in'][j] = _mm(h, dp, ta=True, out_dtype=bf16, name="ab_in_dw")
    dh = _mm(dp, w['ab_w_in'][j], tb=True, name="ab_in_dx")
    dx, dxb, dgain = _rms_bwd(x, _row(w['norm_mix'][l]), dh, dy, "mix_norm_bwd")
    grads['gdn_conv_w'][j] = jnp.concatenate([dwq, dwk, dwv], axis=1)
    grads['gdn_a_log'][j], grads['gdn_dt_bias'][j] = dal[0, :HEADS], ddt[0, :HEADS]
    grads['gdn_norm'][j], grads['hgrn_norm'][j] = dgn[0], dhn[0]
    grads['hgrn_lower_bounds'].append(dlb)
    grads['norm_mix'][l] = dgain[0]
    return dx, dxb


def _ab_permute(w_in):
    pad = jnp.zeros(w_in.shape[:-1] + (AB_PAD - AB_COLS,), w_in.dtype)
    return jnp.concatenate([w_in[..., :2048], w_in[..., 2056:], w_in[..., 2048:2056], pad], axis=-1)


def _ab_unpermute(g):
    return jnp.concatenate([g[..., :2048], g[..., 4096:4104], g[..., 2048:4096]], axis=-1)


def _block_pad(a, axis, nblk, padded):
    axis = axis % a.ndim
    s = a.shape
    a = a.reshape(s[:axis] + (nblk, s[axis] // nblk) + s[axis + 1:])
    pad = [(0, 0)] * a.ndim
    pad[axis + 1] = (0, padded - s[axis] // nblk)
    return jnp.pad(a, pad).reshape(s[:axis] + (nblk * padded,) + s[axis + 1:])


def _block_unpad(a, axis, nblk, width):
    axis = axis % a.ndim
    s = a.shape
    a = a.reshape(s[:axis] + (nblk, s[axis] // nblk) + s[axis + 1:])
    a = lax.slice_in_dim(a, 0, width, axis=axis + 1)
    return a.reshape(s[:axis] + (nblk * width,) + s[axis + 1:])


def _kernel_layout(w):
    w = dict(w)
    w['ab_w_in'] = _ab_permute(w['ab_w_in'])
    w['ffn_w_up'] = _block_pad(w['ffn_w_up'], 2, N_DEV, FF_PAD)
    w['ffn_w_down'] = _block_pad(w['ffn_w_down'], 1, 4, FF_PAD)
    w['ffn_conv_w'] = _block_pad(w['ffn_conv_w'], 2, 4, FF_PAD)
    w['ffn_conv_b'] = _block_pad(w['ffn_conv_b'], 1, 4, FF_PAD)
    return w


def _natural_grads(g):
    g = dict(g)
    g['ab_w_in'] = _ab_unpermute(g['ab_w_in'])
    g['ffn_w_up'] = _block_unpad(g['ffn_w_up'], 2, N_DEV, FF_SHARD)
    g['ffn_w_down'] = _block_unpad(g['ffn_w_down'], 1, 4, FF_SHARD)
    g['ffn_conv_w'] = _block_unpad(g['ffn_conv_w'], 2, 4, FF_SHARD)
    g['ffn_conv_b'] = _block_unpad(g['ffn_conv_b'], 1, 4, FF_SHARD)
    return g


def _local_step(x, target, w, fetch=None):
    grads = {n: [None] * (DEPTH if n in ('norm_mix', 'norm_ffn') or n.startswith('ffn_') else 2)
             for n in WEIGHTS if n not in ('norm_final', 'hgrn_lower_bounds')}
    grads['hgrn_lower_bounds'] = []
    saved = []
    for l in range(DEPTH):
        j = l // 2
        if fetch is not None:
            fetch(l, x)
        x, s_mix = (_even_fwd if l % 2 == 0 else _odd_fwd)(x, w, l, j)
        x, s_ffn = _ffn_fwd(x, w, l)
        saved.append((s_mix, s_ffn))
    loss, dx, dxb, dgf = _loss_head(x, _row(w['norm_final']), target)
    for l in reversed(range(DEPTH)):
        j = l // 2
        s_mix, s_ffn = saved[l]
        dx, dxb = _ffn_bwd(s_ffn, w, l, dx, dxb, grads)
        dx, dxb = (_even_bwd if l % 2 == 0 else _odd_bwd)(s_mix, w, l, j, dx, dxb, grads)
    out = {n: jnp.stack(g) for n, g in grads.items() if n != 'hgrn_lower_bounds'}
    out['hgrn_lower_bounds'] = grads['hgrn_lower_bounds'][0] + grads['hgrn_lower_bounds'][1]
    out['norm_final'] = dgf[0]
    return loss[0, 0], dx, out


def _position():
    return lax.axis_index("x"), lax.axis_index("y"), lax.axis_index("c")


BLOCK_LAYOUT = {
    'ab_w_in': ((2, D_MODEL, N_DEV * AB_SHARD_PAD), (2, D_MODEL, AB_SHARD_PAD)),
    'ab_w_out': ((2, N_DEV, 128, D_MODEL), (2, 128, D_MODEL)),
    'c_w_in': ((2, D_MODEL, 2 * LRU_WIDTH), (2, D_MODEL, 256)),
    'c_w_out': ((2, N_DEV, 128, D_MODEL), (2, 128, D_MODEL)),
    'c_gate_a_w': ((2, HEADS, N_DEV, 32, LRU_BLOCK), (2, HEADS, 32, LRU_BLOCK)),
    'c_gate_x_w': ((2, HEADS, N_DEV, 32, LRU_BLOCK), (2, HEADS, 32, LRU_BLOCK)),
    'ffn_w_up': ((DEPTH, D_MODEL, N_DEV * FF_PAD), (DEPTH, D_MODEL, FF_PAD)),
    'ffn_w_down': ((DEPTH, 4, FF_PAD, D_MODEL), (DEPTH, FF_ROWS, D_MODEL)),
}


COL_WINDOW = {'ab_w_in': AB_SHARD_PAD, 'c_w_in': 256, 'ffn_w_up': FF_PAD}


def _block_index(name, p):
    d = 4 * p[0] + 2 * p[1] + p[2]
    if name in COL_WINDOW:
        return (slice(None), pl.ds(pl.multiple_of(d * COL_WINDOW[name], 128), COL_WINDOW[name]))
    if name == 'ffn_w_down':
        return (2 * p[0] + p[1], pl.ds(pl.multiple_of(p[2] * FF_ROWS, 16), FF_ROWS), slice(None))
    if name in ('c_gate_a_w', 'c_gate_x_w'):
        return (slice(None), d)
    return (d,)


def _block_of(name, ref, p, layered=True):
    idx = _block_index(name, p)
    if layered and name in BLOCK_LAYOUT:
        idx = (slice(None),) + idx
    return ref.at[idx]


def _layer_items(l):
    j = l // 2
    mix = ([('ab_w_in', j), ('ab_w_out', j)] if l % 2 == 0 else
           [('c_w_in', j), ('c_w_out', j), ('c_gate_a_w', j), ('c_gate_x_w', j)])
    return mix + [('ffn_w_up', l), ('ffn_w_down', l)]


def _own_land(name, shard_l, pos):
    x, y, c = pos
    d = 4 * x + 2 * y + c
    shape = BLOCK_LAYOUT[name][0][1:] if name in BLOCK_LAYOUT else (N_DEV,) + shard_l.shape
    zeros = jnp.zeros(shape, shard_l.dtype)
    if name in COL_WINDOW:
        return lax.dynamic_update_slice(zeros, shard_l, (0, d * COL_WINDOW[name]))
    if name == 'ffn_w_down':
        return lax.dynamic_update_slice(zeros, shard_l[None], (2 * x + y, c * FF_ROWS, 0))
    if name in ('c_gate_a_w', 'c_gate_x_w'):
        return lax.dynamic_update_slice(zeros, shard_l[:, None], (0, d, 0, 0))
    return lax.dynamic_update_slice(zeros, shard_l[None], (d,) + (0,) * shard_l.ndim)


def _src_of(shard_ref, li):
    return shard_ref if li is None else shard_ref.at[li]


def _gather_now(items, shards, lands):
    n = len(items)
    srcs = sorted({nm for nm, _ in items})

    def body(*refs):
        ins = dict(zip(srcs, refs[:len(srcs)]))
        outs = refs[len(srcs) + n:len(srcs) + 2 * n]
        send_sems, recv_sems = refs[len(srcs) + 2 * n:]
        x, y, c = _position()
        me, sibling = (x, y, c), (x, y, 1 - c)
        chips = [(1 - x, y), (x, 1 - y), (1 - x, 1 - y)]

        def copy(i, k, block, to, own=False):
            nm, li = items[i]
            dst = _block_of(nm, outs[i], block, layered=False)
            return pltpu.make_async_remote_copy(
                src_ref=_src_of(ins[nm], li) if own else dst, dst_ref=dst, send_sem=send_sems.at[7 * i + k],
                recv_sem=recv_sems.at[7 * i + k], device_id=to, device_id_type=MESH)

        first = []
        for i in range(n):
            first.append(copy(i, 0, me, sibling, own=True))
            first += [copy(i, 1 + j, me, (*chip, c), own=True) for j, chip in enumerate(chips)]
        for cp in first:
            cp.start()
        passed = []
        for j, chip in enumerate(chips):
            for i in range(n):
                copy(i, 1 + j, (*chip, c), me).wait_recv()
                fwd = copy(i, 4 + j, (*chip, c), sibling)
                fwd.start()
                passed.append(fwd)
        for i in range(n):
            copy(i, 0, sibling, me).wait_recv()
        for j, chip in enumerate(chips):
            for i in range(n):
                copy(i, 4 + j, (*chip, 1 - c), me).wait_recv()
        for cp in first + passed:
            cp.wait_send()

    any_spec = pl.BlockSpec(memory_space=pl.ANY)
    return pl.pallas_call(
        body, name="gather_first_layer", out_shape=[_sds(a.shape, a.dtype) for a in lands],
        in_specs=[any_spec] * (len(srcs) + n), out_specs=[any_spec] * n,
        input_output_aliases={len(srcs) + i: i for i in range(n)},
        scratch_shapes=[pltpu.SemaphoreType.DMA((7 * n,)), pltpu.SemaphoreType.DMA((7 * n,))],
    )(*[shards[nm] for nm in srcs], *lands)


FIRST_HOP = (1, 2, 4, 6)


def _lanes(name, land_ref, pos):
    if name == 'ffn_w_down':
        return [(FIRST_HOP, land_ref.at[pl.ds(0, 2), pl.ds(0, 2 * FF_ROWS)])]
    if name in COL_WINDOW:
        return [(FIRST_HOP, land_ref.at[:, pl.ds(0, 4 * COL_WINDOW[name])])]
    if name in ('c_gate_a_w', 'c_gate_x_w'):
        return [(FIRST_HOP, land_ref.at[:, pl.ds(0, 4)])]
    return [(FIRST_HOP, land_ref.at[pl.ds(0, 4)])]


def _n_lanes(items):
    return len(items)


def _gather_forward(items, lands, name):
    n = len(items)

    def body(*refs):
        outs = refs[n:2 * n]
        send_sems, recv_sems = refs[2 * n:]
        x, y, c = _position()
        chips = [(1 - x, y), (x, 1 - y), (1 - x, 1 - y)]
        copies, arrivals = [], []
        for i, (nm, _) in enumerate(items):
            for j, chip in enumerate(chips):
                mine = _block_of(nm, outs[i], (*chip, c), layered=False)
                theirs = _block_of(nm, outs[i], (*chip, 1 - c), layered=False)
                copies.append(pltpu.make_async_remote_copy(
                    src_ref=mine, dst_ref=mine, send_sem=send_sems.at[3 * i + j], recv_sem=recv_sems.at[3 * i + j],
                    device_id=(x, y, 1 - c), device_id_type=MESH))
                arrivals.append(pltpu.make_async_remote_copy(
                    src_ref=theirs, dst_ref=theirs, send_sem=send_sems.at[3 * i + j], recv_sem=recv_sems.at[3 * i + j],
                    device_id=(x, y, 1 - c), device_id_type=MESH))
        for cp in copies:
            cp.start()
        for cp in arrivals:
            cp.wait_recv()
        for cp in copies:
            cp.wait_send()

    any_spec = pl.BlockSpec(memory_space=pl.ANY)
    return pl.pallas_call(
        body, name=name, out_shape=[_sds(a.shape, a.dtype) for a in lands],
        in_specs=[any_spec] * n, out_specs=[any_spec] * n, input_output_aliases={i: i for i in range(n)},
        scratch_shapes=[pltpu.SemaphoreType.DMA((3 * n,)), pltpu.SemaphoreType.DMA((3 * n,))],
    )(*lands)


HBM_SPEC = pl.BlockSpec(memory_space=pltpu.HBM)
SEM_SPEC = pl.BlockSpec(memory_space=pltpu.SEMAPHORE)
SIDE_EFFECT = pltpu.SideEffectType.DATAFLOW_SIDE_EFFECTING


def _gather_start(items, shards, lands, token):
    n = len(items)
    srcs = sorted({nm for nm, _ in items})
    ns, nl = len(srcs), _n_lanes(items)

    def body(*refs):
        ins = dict(zip(srcs, refs[:ns]))
        land_refs = refs[ns:ns + n]
        sems = refs[ns + n + 1:ns + n + 1 + 2 * nl]
        x, y, c = _position()
        me = (x, y, c)
        lane = 0
        for i, (nm, li) in enumerate(items):
            for codes, _ in _lanes(nm, land_refs[i], me):
                for k in codes:
                    peer = (1 - x if (k >> 2) & 1 else x, 1 - y if (k >> 1) & 1 else y, 1 - c if k & 1 else c)
                    pltpu.make_async_remote_copy(
                        src_ref=_src_of(ins[nm], li), dst_ref=_block_of(nm, land_refs[i], me, layered=False),
                        send_sem=sems[2 * lane], recv_sem=sems[2 * lane + 1], device_id=peer, device_id_type=MESH).start()
                lane += 1

    hbm = [pltpu.with_memory_space_constraint(a, pltpu.HBM) for a in [shards[nm] for nm in srcs] + list(lands)]
    outs = pl.pallas_call(
        body, name="gather_start",
        out_shape=[pltpu.SemaphoreType.DMA(())] * (2 * nl) + [pltpu.HBM(a.shape, a.dtype) for a in hbm],
        in_specs=[HBM_SPEC] * (ns + n) + [pl.BlockSpec(memory_space=pl.ANY)],
        out_specs=[SEM_SPEC] * (2 * nl) + [HBM_SPEC] * (ns + n),
        input_output_aliases={i: 2 * nl + i for i in range(ns + n)},
        compiler_params=pltpu.CompilerParams(has_side_effects=SIDE_EFFECT),
    )(*hbm, token)
    return outs[:2 * nl], dict(zip(srcs, outs[2 * nl:2 * nl + ns])), outs[2 * nl + ns:]


def _gather_wait(items, sems, shards, lands, after, name):
    n = len(items)
    srcs = sorted(shards)
    ns, nl = len(srcs), _n_lanes(items)

    def body(*refs):
        land_refs = refs[ns:ns + n]
        sem_refs = refs[ns + n:ns + n + 2 * nl]
        x, y, c = _position()
        lane = 0
        for i, (nm, _) in enumerate(items):
            for _, moved in _lanes(nm, land_refs[i], (x, y, c)):
                cp = pltpu.make_async_remote_copy(
                    src_ref=moved, dst_ref=moved, send_sem=sem_refs[2 * lane], recv_sem=sem_refs[2 * lane + 1],
                    device_id=(x, y, 1 - c), device_id_type=MESH)
                cp.wait_send()
                cp.wait_recv()
                lane += 1

    outs = pl.pallas_call(
        body, name=name, out_shape=[pltpu.HBM(shards[nm].shape, shards[nm].dtype) for nm in srcs]
        + [pltpu.HBM(a.shape, a.dtype) for a in lands],
        in_specs=[HBM_SPEC] * (ns + n) + [SEM_SPEC] * (2 * nl) + [pl.BlockSpec(memory_space=pl.ANY)],
        out_specs=[HBM_SPEC] * (ns + n), input_output_aliases={i: i for i in range(ns + n)},
        compiler_params=pltpu.CompilerParams(has_side_effects=SIDE_EFFECT),
    )(*[shards[nm] for nm in srcs], *lands, *sems, after)
    return dict(zip(srcs, outs[:ns])), outs[ns:]


def _exchange_grads(fulls, rep):
    cpos = lax.axis_index("c").astype(jnp.int32).reshape(1)
    pair, rep_pair = _pair_exchange(fulls, rep)
    chip = {nm: _chip_sum(nm, fulls[nm], pair[nm], cpos) for nm in fulls}
    rep_chip = _add_pair(rep, rep_pair, "chip_sum_replicated")
    cross, cross_rep = _cross_exchange(chip, rep_chip)
    return chip, rep_chip, cross, cross_rep


def _pair_exchange(fulls, rep):
    names = list(fulls)
    n = len(names)
    shard_shape = {nm: (BLOCK_LAYOUT[nm][1] if nm in BLOCK_LAYOUT else fulls[nm].shape[1:]) for nm in names}

    def body(*refs):
        ins = dict(zip(names, refs[:n]))
        rep_ref = refs[n]
        pair = dict(zip(names, refs[n + 1:2 * n + 1]))
        rpair_ref = refs[2 * n + 1]
        send_sems, recv_sems = refs[2 * n + 2:]
        x, y, c = _position()
        sibling = (x, y, 1 - c)
        remote = []
        for i, nm in enumerate(names):
            for q in range(4):
                remote.append(pltpu.make_async_remote_copy(
                    src_ref=_block_of(nm, ins[nm], (q >> 1, q & 1, 1 - c)), dst_ref=pair[nm].at[q],
                    send_sem=send_sems.at[4 * i + q], recv_sem=recv_sems.at[4 * i + q], device_id=sibling,
                    device_id_type=MESH))
        remote.append(pltpu.make_async_remote_copy(
            src_ref=rep_ref, dst_ref=rpair_ref, send_sem=send_sems.at[4 * n], recv_sem=recv_sems.at[4 * n],
            device_id=sibling, device_id_type=MESH))
        for cp in remote:
            cp.start()
        for cp in remote:
            cp.wait_recv()
        for cp in remote:
            cp.wait_send()

    any_spec = pl.BlockSpec(memory_space=pl.ANY)
    four = [_sds((4,) + tuple(shard_shape[nm]), fulls[nm].dtype) for nm in names]
    outs = pl.pallas_call(
        body, name="grad_pair_exchange", out_shape=four + [_sds(rep.shape, rep.dtype)],
        in_specs=[any_spec] * (n + 1), out_specs=[any_spec] * (n + 1),
        scratch_shapes=[pltpu.SemaphoreType.DMA((4 * n + 1,)), pltpu.SemaphoreType.DMA((4 * n + 1,))],
    )(*[fulls[nm] for nm in names], rep)
    return dict(zip(names, outs[:n])), outs[n]


def _chip_sum(name, full, pair, cpos):
    if name in ('ab_w_in', 'c_w_in', 'ffn_w_up'):
        width = BLOCK_LAYOUT[name][1][-1]
        rows = full.shape[0] * full.shape[1]
        tr = 512

        def body(c_ref, f_ref, p_ref, o_ref):
            o_ref[0] = (f_ref[...].astype(f32) + p_ref[0].astype(f32)).astype(o_ref.dtype)

        slot = pl.BlockSpec((1, tr, width), lambda q, i, c: (q, i, 0))
        out = pl.pallas_call(
            body, name="chip_sum_" + name, out_shape=_sds((4, rows, width), full.dtype),
            grid_spec=pltpu.PrefetchScalarGridSpec(
                num_scalar_prefetch=1, grid=(4, rows // tr),
                in_specs=[pl.BlockSpec((tr, width), lambda q, i, c: (i, 2 * q + c[0])), slot], out_specs=slot),
            compiler_params=_cparams(("parallel", "parallel")))(
            cpos, full.reshape(rows, N_DEV * width), pair.reshape(4, rows, width))
        return out.reshape(pair.shape)

    if name == 'ffn_w_down':
        f4, p4 = full, pair
        fspec = pl.BlockSpec((DEPTH, 1, FF_ROWS, D_MODEL), lambda q, c: (0, q, c[0], 0))
    else:
        shard = pair.shape[1:]
        lead = int(np.prod(shard[:-2]))
        f4 = full.reshape((lead, N_DEV) + shard[-2:])
        p4 = pair.reshape((4, lead) + shard[-2:])
        fspec = pl.BlockSpec((lead, 1) + shard[-2:], lambda q, c: (0, 2 * q + c[0], 0, 0))

    def body4(c_ref, f_ref, p_ref, o_ref):
        o_ref[0] = (f_ref[:, 0].astype(f32) + p_ref[0].astype(f32)).astype(o_ref.dtype)

    slot = pl.BlockSpec((1,) + p4.shape[1:], lambda q, c: (q, 0, 0, 0))
    out = pl.pallas_call(
        body4, name="chip_sum_" + name, out_shape=_sds(p4.shape, full.dtype),
        grid_spec=pltpu.PrefetchScalarGridSpec(num_scalar_prefetch=1, grid=(4,), in_specs=[fspec, slot], out_specs=slot),
        compiler_params=_cparams(("parallel",)))(cpos, f4, p4)
    return out.reshape(pair.shape)


def _add_pair(a, b, name):
    shp = a.shape
    r, c = int(np.prod(shp[:-1])), shp[-1]
    tr = _tile(r, (512, 256, 128, 64, 32, 16, 8))

    def body(a_ref, b_ref, o_ref):
        o_ref[...] = (a_ref[...].astype(f32) + b_ref[...].astype(f32)).astype(o_ref.dtype)

    tile = pl.BlockSpec((tr, c), lambda i: (i, 0))
    return pl.pallas_call(body, name=name, grid=(r // tr,), in_specs=[tile, tile], out_specs=tile,
                          out_shape=_sds((r, c), a.dtype), compiler_params=_cparams(("parallel",)))(
        a.reshape(r, c), b.reshape(r, c)).reshape(shp)


def _cross_exchange(chip, rep_chip):
    names = list(chip)
    n = len(names)

    def body(*refs):
        ins = dict(zip(names, refs[:n]))
        rep_ref = refs[n]
        outs = dict(zip(names, refs[2 * n + 2:3 * n + 2]))
        rrep_ref = refs[3 * n + 2]
        send_sems, recv_sems = refs[3 * n + 3:]
        x, y, c = _position()
        mine = 2 * x + y
        copies = []
        for k in range(1, 4):
            px, py = (1 - x if (k >> 1) & 1 else x), (1 - y if k & 1 else y)
            for i, nm in enumerate(names + ['']):
                src = rep_ref if i == n else ins[nm].at[2 * px + py]
                dst = (rrep_ref if i == n else outs[nm]).at[mine]
                copies.append(pltpu.make_async_remote_copy(
                    src_ref=src, dst_ref=dst, send_sem=send_sems.at[3 * i + k - 1], recv_sem=recv_sems.at[3 * i + k - 1],
                    device_id=(px, py, c), device_id_type=MESH))
        for cp in copies:
            cp.start()
        for cp in copies:
            cp.wait_recv()
        for cp in copies:
            cp.wait_send()

    any_spec = pl.BlockSpec(memory_space=pl.ANY)
    shapes = [_sds(chip[nm].shape, chip[nm].dtype) for nm in names] + [_sds((4,) + rep_chip.shape, rep_chip.dtype)]
    zeros = [jnp.zeros(s.shape, s.dtype) for s in shapes]
    outs = pl.pallas_call(
        body, name="grad_cross_exchange", out_shape=shapes,
        in_specs=[any_spec] * (2 * n + 2), out_specs=[any_spec] * (n + 1),
        input_output_aliases={n + 1 + i: i for i in range(n + 1)},
        scratch_shapes=[pltpu.SemaphoreType.DMA((3 * (n + 1),)), pltpu.SemaphoreType.DMA((3 * (n + 1),))],
    )(*[chip[nm] for nm in names], rep_chip, *zeros)
    return dict(zip(names, outs[:n])), outs[n]


def _sum_adamw(parts, own, mine, w, m, v, name):
    r, l = w.shape
    lp = parts.shape[2]
    tr = _tile(r, (256, 128, 64, 32, 16, 8))
    c1 = 1.0 / (1.0 - ADAM_B1 ** ADAM_STEP)
    c2 = 1.0 / (1.0 - ADAM_B2 ** ADAM_STEP)

    def body(mine_ref, p_ref, o_ref, w_ref, m_ref, v_ref, g_ref, d_ref, nm_ref, nv_ref):
        mine_v = (o_ref[0] if own.ndim == 3 else o_ref[...]).astype(f32)
        g = jnp.where(mine_ref[0] == 0, mine_v, p_ref[0].astype(f32))
        for s in range(1, parts.shape[0]):
            g = g + jnp.where(mine_ref[0] == s, mine_v, p_ref[s].astype(f32))
        if lp != l:
            g = g[:, :l]
        m_new = ADAM_B1 * m_ref[...] + (1.0 - ADAM_B1) * g
        v_new = ADAM_B2 * v_ref[...] + (1.0 - ADAM_B2) * (g * g)
        g_ref[...] = g
        nm_ref[...] = m_new
        nv_ref[...] = v_new
        d_ref[...] = -ADAM_LR * ((m_new * c1) / (jnp.sqrt(v_new * c2) + ADAM_EPS) + ADAM_WD * w_ref[...])

    tile = pl.BlockSpec((tr, l), lambda i, mn: (i, 0))
    own_spec = (pl.BlockSpec((1, tr, lp), lambda i, mn: (mn[0], i, 0)) if own.ndim == 3
                else pl.BlockSpec((tr, lp), lambda i, mn: (i, 0)))
    return pl.pallas_call(
        body, name=name, out_shape=[_sds((r, l), f32)] * 4,
        grid_spec=pltpu.PrefetchScalarGridSpec(
            num_scalar_prefetch=1, grid=(r // tr,),
            in_specs=[pl.BlockSpec((parts.shape[0], tr, lp), lambda i, mn: (0, i, 0)), own_spec, tile, tile, tile],
            out_specs=[tile] * 4),
        compiler_params=_cparams(("parallel",)))(mine, parts, own, w, m, v)


def _pack(arrs, lead=None):
    if lead is None:
        flat = jnp.concatenate([a.reshape(-1).astype(f32) for a in arrs])
        n = flat.shape[0]
    else:
        flat = jnp.concatenate([a.reshape(lead, -1).astype(f32) for a in arrs], axis=1)
        n = flat.shape[1]
    tot = -(-n // 1024) * 1024
    if lead is None:
        return jnp.pad(flat, (0, tot - n)).reshape(tot // 128, 128)
    return jnp.pad(flat, ((0, 0), (0, tot - n))).reshape(lead, tot // 128, 128)


def _unpack(packed, shapes, lead=False):
    flat = packed.reshape(packed.shape[0], -1) if lead else packed.reshape(-1)
    out, off = [], 0
    for s in shapes:
        n = int(np.prod(s))
        out.append(flat[:, off:off + n].reshape((packed.shape[0],) + tuple(s)) if lead else flat[off:off + n].reshape(s))
        off += n
    return out


def _merge_shards(g, axis):
    g = jnp.moveaxis(g, 0, axis)
    s = g.shape
    return g.reshape(s[:axis] + (s[axis] * s[axis + 1],) + s[axis + 2:])


def _split_shards(full, axis):
    s = full.shape
    g = full.reshape(s[:axis] + (N_DEV, s[axis] // N_DEV) + s[axis + 1:])
    return jnp.moveaxis(g, axis, 0)


def kernel(x, norm_mix, norm_ffn, norm_final, ab_w_in, gdn_conv_w, gdn_a_log, gdn_dt_bias, gdn_norm, hgrn_lower_bounds, hgrn_norm, ab_w_out, c_w_in, c_conv_w, c_conv_b, c_gate_a_w, c_gate_a_b, c_gate_x_w, c_gate_x_b, c_lambda, c_w_out, ffn_w_up, ffn_conv_w, ffn_conv_b, ffn_w_down, loss_target, m_norm_mix, m_norm_ffn, m_norm_final, m_ab_w_in, m_gdn_conv_w, m_gdn_a_log, m_gdn_dt_bias, m_gdn_norm, m_hgrn_lower_bounds, m_hgrn_norm, m_ab_w_out, m_c_w_in, m_c_conv_w, m_c_conv_b, m_c_gate_a_w, m_c_gate_a_b, m_c_gate_x_w, m_c_gate_x_b, m_c_lambda, m_c_w_out, m_ffn_w_up, m_ffn_conv_w, m_ffn_conv_b, m_ffn_w_down, v_norm_mix, v_norm_ffn, v_norm_final, v_ab_w_in, v_gdn_conv_w, v_gdn_a_log, v_gdn_dt_bias, v_gdn_norm, v_hgrn_lower_bounds, v_hgrn_norm, v_ab_w_out, v_c_w_in, v_c_conv_w, v_c_conv_b, v_c_gate_a_w, v_c_gate_a_b, v_c_gate_x_w, v_c_gate_x_b, v_c_lambda, v_c_w_out, v_ffn_w_up, v_ffn_conv_w, v_ffn_conv_b, v_ffn_w_down):
    wl = dict(zip(WEIGHTS, (norm_mix, norm_ffn, norm_final, ab_w_in, gdn_conv_w, gdn_a_log, gdn_dt_bias, gdn_norm, hgrn_lower_bounds, hgrn_norm, ab_w_out, c_w_in, c_conv_w, c_conv_b, c_gate_a_w, c_gate_a_b, c_gate_x_w, c_gate_x_b, c_lambda, c_w_out, ffn_w_up, ffn_conv_w, ffn_conv_b, ffn_w_down)))
    ml = dict(zip(WEIGHTS, (m_norm_mix, m_norm_ffn, m_norm_final, m_ab_w_in, m_gdn_conv_w, m_gdn_a_log, m_gdn_dt_bias, m_gdn_norm, m_hgrn_lower_bounds, m_hgrn_norm, m_ab_w_out, m_c_w_in, m_c_conv_w, m_c_conv_b, m_c_gate_a_w, m_c_gate_a_b, m_c_gate_x_w, m_c_gate_x_b, m_c_lambda, m_c_w_out, m_ffn_w_up, m_ffn_conv_w, m_ffn_conv_b, m_ffn_w_down)))
    vl = dict(zip(WEIGHTS, (v_norm_mix, v_norm_ffn, v_norm_final, v_ab_w_in, v_gdn_conv_w, v_gdn_a_log, v_gdn_dt_bias, v_gdn_norm, v_hgrn_lower_bounds, v_hgrn_norm, v_ab_w_out, v_c_w_in, v_c_conv_w, v_c_conv_b, v_c_gate_a_w, v_c_gate_a_b, v_c_gate_x_w, v_c_gate_x_b, v_c_lambda, v_c_w_out, v_ffn_w_up, v_ffn_conv_w, v_ffn_conv_b, v_ffn_w_down)))

    big = [n for n in SHARDED if n in MATMUL_WEIGHTS]
    vec = [n for n in SHARDED if n not in MATMUL_WEIGHTS]
    shards = {n: wl[n].astype(bf16) for n in big}
    shards['ab_w_in'] = jnp.pad(shards['ab_w_in'], ((0, 0), (0, 0), (0, AB_SHARD_PAD - AB_SHARD)))
    shards['ffn_w_up'] = jnp.pad(shards['ffn_w_up'], ((0, 0), (0, 0), (0, FF_PAD - FF_SHARD)))
    shards['vec'] = _pack([wl[n] for n in vec])
    pos = _position()
    layer_items = [_layer_items(l) for l in range(DEPTH)]
    lands = [[_own_land(nm, shards[nm][li], pos) for nm, li in items] for items in layer_items]
    first = _gather_now(layer_items[0] + [('vec', None)], shards, lands[0] + [_own_land('vec', shards['vec'], pos)])
    rest = layer_items[1] + layer_items[2] + layer_items[3]
    sems, thru, rest_lands = _gather_start(rest, shards, lands[1] + lands[2] + lands[3], first[-1])
    flight = {'sems': list(sems), 'shards': thru, 'lands': list(rest_lands)}

    full = {n: wl[n] for n in REPLICATED}
    for n, a in zip(vec, _unpack(first[-1], [wl[n].shape for n in vec], lead=True)):
        full[n] = _merge_shards(a, SHARD_AXIS[n])
    full['ffn_conv_w'] = _block_pad(full['ffn_conv_w'], 2, 4, FF_PAD)
    full['ffn_conv_b'] = _block_pad(full['ffn_conv_b'], 1, 4, FF_PAD)
    for n in big:
        full[n] = {}

    def fetch(l, x_in):
        items = layer_items[l]
        if l == 0:
            got = first[:len(items)]
        else:
            k = len(items)
            nl = 2 * _n_lanes(items)
            flight['shards'], got = _gather_wait(items, flight['sems'][:nl], flight['shards'], flight['lands'][:k],
                                                 x_in, "gather_wait_%d" % l)
            flight['sems'], flight['lands'] = flight['sems'][nl:], flight['lands'][k:]
            got = _gather_forward(items, got, "gather_forward_%d" % l)
        for (nm, li), a in zip(items, got):
            if nm == 'ab_w_in':
                a = _ab_permute(_block_unpad(a, 1, N_DEV, AB_SHARD))
            elif nm in ('ab_w_out', 'c_w_out'):
                a = a.reshape(D_MODEL, D_MODEL)
            elif nm in ('c_gate_a_w', 'c_gate_x_w'):
                a = a.reshape(HEADS, LRU_BLOCK, LRU_BLOCK)
            elif nm == 'ffn_w_down':
                a = a.reshape(D_FFP, D_MODEL)
            full[nm][li] = a

    loss, dx, grads = _local_step(x[0], loss_target[0], full, fetch)

    grads['ffn_conv_w'] = _block_unpad(grads['ffn_conv_w'], 2, 4, FF_SHARD)
    grads['ffn_conv_b'] = _block_unpad(grads['ffn_conv_b'], 1, 4, FF_SHARD)
    fulls = {n: grads[n].astype(bf16).reshape(BLOCK_LAYOUT[n][0]) for n in big if n != 'ab_w_in'}
    fulls['ab_w_in'] = _block_pad(_ab_unpermute(grads['ab_w_in']), 2, N_DEV, AB_SHARD_PAD)
    fulls = {n: fulls[n] for n in big}
    fulls['vec'] = _pack([_split_shards(grads[n], SHARD_AXIS[n]) for n in vec], lead=N_DEV)
    chip, rep_chip, recv, rrep = _exchange_grads(fulls, _pack([grads[n] for n in REPLICATED]))
    mine = (2 * lax.axis_index("x") + lax.axis_index("y")).astype(jnp.int32).reshape(1)
    res = {}
    for n in big:
        shp = wl[n].shape
        r, c = int(np.prod(shp[:-1])), shp[-1]
        outs = _sum_adamw(recv[n].reshape(4, r, -1), chip[n].reshape(4, r, -1), mine, wl[n].reshape(r, c),
                          ml[n].reshape(r, c), vl[n].reshape(r, c), "adamw_" + n)
        for kind, o in zip(("grad", "delta", "new_m", "new_v"), outs):
            res[kind, n] = o.reshape(shp)
    for names, parts, own, tag in ((vec, recv['vec'], chip['vec'], "adamw_vectors"),
                                   (REPLICATED, rrep, rep_chip, "adamw_replicated")):
        outs = _sum_adamw(parts, own, mine, _pack([wl[n] for n in names]), _pack([ml[n] for n in names]),
                          _pack([vl[n] for n in names]), tag)
        for kind, o in zip(("grad", "delta", "new_m", "new_v"), outs):
            for n, a in zip(names, _unpack(o, [wl[n].shape for n in names])):
                res[kind, n] = a

    loss = lax.psum(loss, ("x", "y", "c"))
    return (loss, dx[None], *[res[kind, n] for kind in ("grad", "delta", "new_m", "new_v") for n in WEIGHTS])
```

```python
import functools

import numpy as np
import jax
import jax.numpy as jnp
from jax import lax
from jax.experimental import pallas as pl
from jax.experimental.pallas import tpu as pltpu

f32 = jnp.float32
bf16 = jnp.bfloat16
HI = lax.Precision.HIGHEST
MESH = pl.DeviceIdType.MESH

N_DEV = 8
D_MODEL = 1024
DEPTH = 4
EPS = 1e-6
F_FLOOR = 1e-30
HEADS = 4
HEAD_DIM = 128
GDN_WIDTH = 512
GDN_CONV = 4
GDN_CHUNK = 64
HGRN_CHUNK = 16
HGRN_STEP = 128
MIX_WIDTH = 1024
AB_COLS = 4104
AB_PAD = 4224
LRU_WIDTH = 1024
LRU_BLOCK = 256
LRU_CONV = 4
RG_C = 8.0
D_FF = 2816
FF_SHARD = 704
FF_PAD = 768
D_FFP = 4 * FF_PAD
FF_ROWS = 352
AB_SHARD, AB_SHARD_PAD = 513, 640
FFN_CONV = 3
ADAM_LR, ADAM_B1, ADAM_B2, ADAM_EPS, ADAM_WD, ADAM_STEP = 0.001, 0.9, 0.999, 1e-08, 0.01, 10
VMEM_LIMIT = 56 * 1024 * 1024
PACK_LANES = 512
PACK_ROWS = 256

OFF_Q, OFF_K, OFF_V, OFF_Z, OFF_QB, OFF_FB, OFF_IB, OFF_GB, OFF_BA = 0, 512, 1024, 1536, 2048, 2560, 3072, 3584, 4096

WEIGHTS = ['norm_mix', 'norm_ffn', 'norm_final', 'ab_w_in', 'gdn_conv_w', 'gdn_a_log', 'gdn_dt_bias', 'gdn_norm',
           'hgrn_lower_bounds', 'hgrn_norm', 'ab_w_out', 'c_w_in', 'c_conv_w', 'c_conv_b', 'c_gate_a_w', 'c_gate_a_b',
           'c_gate_x_w', 'c_gate_x_b', 'c_lambda', 'c_w_out', 'ffn_w_up', 'ffn_conv_w', 'ffn_conv_b', 'ffn_w_down']
SHARD_AXIS = {'norm_mix': None, 'norm_ffn': None, 'norm_final': None, 'ab_w_in': 2, 'gdn_conv_w': 2, 'gdn_a_log': None,
              'gdn_dt_bias': None, 'gdn_norm': None, 'hgrn_lower_bounds': None, 'hgrn_norm': None, 'ab_w_out': 1,
              'c_w_in': 2, 'c_conv_w': 2, 'c_conv_b': 1, 'c_gate_a_w': 2, 'c_gate_a_b': 1, 'c_gate_x_w': 2,
              'c_gate_x_b': 1, 'c_lambda': 1, 'c_w_out': 1, 'ffn_w_up': 2, 'ffn_conv_w': 2, 'ffn_conv_b': None,
              'ffn_w_down': 1}
MATMUL_WEIGHTS = ('ab_w_in', 'ab_w_out', 'c_w_in', 'c_gate_a_w', 'c_gate_x_w', 'c_w_out', 'ffn_w_up', 'ffn_w_down')
SHARDED = [n for n in WEIGHTS if SHARD_AXIS[n] is not None]
REPLICATED = [n for n in WEIGHTS if SHARD_AXIS[n] is None]


def _tile(n, prefs=(512, 384, 256, 128)):
    for p in prefs:
        if n % p == 0:
            return p
    return n


def _cparams(sem=None):
    kw = dict(vmem_limit_bytes=VMEM_LIMIT)
    if sem is not None:
        kw['dimension_semantics'] = sem
    return pltpu.CompilerParams(**kw)


def _sds(shape, dtype):
    return jax.ShapeDtypeStruct(tuple(shape), dtype)


def _sigmoid(x):
    return 1.0 / (1.0 + jnp.exp(-x))


def _silu(x):
    return x * _sigmoid(x)


def _log1p(x):
    u = 1.0 + x
    return jnp.where(u == 1.0, x, jnp.log(u) * (x / jnp.where(u == 1.0, 1.0, u - 1.0)))


def _softplus(x):
    return jnp.maximum(x, 0.0) + _log1p(jnp.exp(-jnp.abs(x)))


def _expm1(x):
    small = jnp.abs(x) < 0.05
    xs = jnp.where(small, x, 0.0)
    series = xs * (1.0 + xs * (0.5 + xs * (1.0 / 6.0 + xs * (1.0 / 24.0 + xs * (1.0 / 120.0)))))
    return jnp.where(small, series, jnp.exp(x) - 1.0)


def _gelu(x):
    return 0.5 * x * (1.0 + jnp.tanh(0.7978845608028654 * (x + 0.044715 * x * x * x)))


def _rms(x, gain):
    return x * lax.rsqrt(jnp.mean(x * x, axis=-1, keepdims=True) + EPS) * gain


def _dot(a, b, dims=((1,), (0,)), precision=None):
    return lax.dot_general(a, b, (dims, ((), ())), precision=precision, preferred_element_type=f32)


def _bdot(a, b, dims=((1,), (0,))):
    return _dot(a.astype(bf16), b.astype(bf16), dims)


NT = ((1,), (1,))
TN = ((0,), (0,))


def _shift_down(x, k):
    if k == 0:
        return x
    row = lax.broadcasted_iota(jnp.int32, x.shape, 0)
    return jnp.where(row >= k, pltpu.roll(x, k, 0), 0.0)


def _shift_up(x, k, fill=0.0):
    if k == 0:
        return x
    n = x.shape[0]
    row = lax.broadcasted_iota(jnp.int32, x.shape, 0)
    return jnp.where(row < n - k, pltpu.roll(x, n - k, 0), fill)


def _conv_fwd(x, w_ref, width):
    acc = w_ref[width - 1:width, :] * x
    for k in range(width - 1):
        acc = acc + w_ref[k:k + 1, :] * _shift_down(x, width - 1 - k)
    return acc


def _conv_bwd(x, dout, w_ref, dw_ref, width):
    dx = w_ref[width - 1:width, :] * dout
    dw_ref[width - 1:width, :] = jnp.sum(dout * x, axis=0, keepdims=True)
    for k in range(width - 1):
        s = width - 1 - k
        dx = dx + w_ref[k:k + 1, :] * _shift_up(dout, s)
        dw_ref[k:k + 1, :] = jnp.sum(dout * _shift_down(x, s), axis=0, keepdims=True)
    return dx


MM_VMEM_BUDGET = 36 * 1024 * 1024
MM_MAX_TILE = 1024 * 1024


def _mm_tiles(m, n, k, out_bytes):
    best = None
    for tm in (1024, 512, 384, 256, 128):
        if m % tm:
            continue
        for tn in range(1536, 0, -128):
            if n % tn or tm * tn > MM_MAX_TILE:
                continue
            score = (tm * tn, min(tm, tn))
            if 2 * (tm * k * 2 + k * tn * 2 + tm * tn * out_bytes) <= MM_VMEM_BUDGET and (best is None or score > best[0]):
                best = (score, tm, tn)
    return (best[1], best[2]) if best else (_tile(m), _tile(n))


def _mm(a, b, *, ta=False, tb=False, add=None, out_dtype=f32, name):
    m, k = (a.shape[1], a.shape[0]) if ta else a.shape
    n = b.shape[0] if tb else b.shape[1]
    tm, tn = _mm_tiles(m, n, k, jnp.dtype(out_dtype).itemsize + (4 if add is not None else 0))
    dims = ((0 if ta else 1,), (1 if tb else 0,))

    def body(*refs):
        a_ref, b_ref = refs[0], refs[1]
        o_ref = refs[-1]
        r = _dot(a_ref[...], b_ref[...], dims)
        if add is not None:
            r = r + refs[2][...]
        o_ref[...] = r.astype(out_dtype)

    a_spec = pl.BlockSpec((k, tm), lambda j, i: (0, i)) if ta else pl.BlockSpec((tm, k), lambda j, i: (i, 0))
    b_spec = pl.BlockSpec((tn, k), lambda j, i: (j, 0)) if tb else pl.BlockSpec((k, tn), lambda j, i: (0, j))
    o_spec = pl.BlockSpec((tm, tn), lambda j, i: (i, j))
    ins, specs = [a, b], [a_spec, b_spec]
    if add is not None:
        ins.append(add)
        specs.append(o_spec)
    return pl.pallas_call(body, name=name, grid=(n // tn, m // tm), in_specs=specs, out_specs=o_spec,
                          out_shape=_sds((m, n), out_dtype), compiler_params=_cparams(("parallel", "parallel")))(*ins)


def _rms_fwd(x, gain, name):
    t, d = x.shape
    tr = _tile(t, (256, 128))

    def body(x_ref, g_ref, h_ref):
        h_ref[...] = _rms(x_ref[...], g_ref[...]).astype(bf16)

    return pl.pallas_call(body, name=name, grid=(t // tr,),
                          in_specs=[pl.BlockSpec((tr, d), lambda i: (i, 0)), pl.BlockSpec((1, d), lambda i: (0, 0))],
                          out_specs=pl.BlockSpec((tr, d), lambda i: (i, 0)), out_shape=_sds((t, d), bf16),
                          compiler_params=_cparams(("parallel",)))(x, gain)


def _rms_bwd(x, gain, dh, dres, name):
    t, d = x.shape
    tr = _tile(t, (256, 128))

    def body(x_ref, g_ref, dh_ref, dres_ref, dx_ref, dxb_ref, dg_ref):
        _, vjp = jax.vjp(_rms, x_ref[...], g_ref[...])
        dx, dg = vjp(dh_ref[...])
        dx = dx + dres_ref[...]
        dx_ref[...] = dx
        dxb_ref[...] = dx.astype(bf16)

        @pl.when(pl.program_id(0) == 0)
        def _():
            dg_ref[...] = jnp.zeros_like(dg_ref)

        dg_ref[...] += dg

    row = pl.BlockSpec((tr, d), lambda i: (i, 0))
    vec = pl.BlockSpec((1, d), lambda i: (0, 0))
    return pl.pallas_call(body, name=name, grid=(t // tr,), in_specs=[row, vec, row, row], out_specs=[row, row, vec],
                          out_shape=[_sds((t, d), f32), _sds((t, d), bf16), _sds((1, d), f32)],
                          compiler_params=_cparams(("arbitrary",)))(x, gain, dh, dres)


def _loss_head(x, gain, target):
    t, d = x.shape
    tr = _tile(t, (256, 128))

    def f(xv, g, tgt):
        err = _rms(xv, g) - tgt
        return 0.5 * jnp.sum(jnp.mean(err * err, axis=-1, keepdims=True), axis=0, keepdims=True)

    def body(x_ref, g_ref, t_ref, loss_ref, dx_ref, dxb_ref, dg_ref):
        loss, vjp = jax.vjp(lambda xv, g: f(xv, g, t_ref[...]), x_ref[...], g_ref[...])
        dx, dg = vjp(jnp.ones((1, 1), f32))
        dx_ref[...] = dx
        dxb_ref[...] = dx.astype(bf16)

        @pl.when(pl.program_id(0) == 0)
        def _():
            dg_ref[...] = jnp.zeros_like(dg_ref)
            loss_ref[...] = jnp.zeros_like(loss_ref)

        dg_ref[...] += dg
        loss_ref[...] += jnp.broadcast_to(loss, loss_ref.shape)

    row = pl.BlockSpec((tr, d), lambda i: (i, 0))
    vec = pl.BlockSpec((1, d), lambda i: (0, 0))
    one = pl.BlockSpec((8, 128), lambda i: (0, 0))
    return pl.pallas_call(body, name="loss_head", grid=(t // tr,), in_specs=[row, vec, row],
                          out_specs=[one, row, row, vec],
                          out_shape=[_sds((8, 128), f32), _sds((t, d), f32), _sds((t, d), bf16), _sds((1, d), f32)],
                          compiler_params=_cparams(("arbitrary",)))(x, gain, target)


def _ffn_act_fwd(u, conv_w, conv_b):
    t = u.shape[0]
    tc = FF_PAD // 2
    nb = D_FFP // tc

    def body(g_ref, v_ref, w_ref, b_ref, a_ref):
        gc = _conv_fwd(g_ref[...], w_ref, FFN_CONV) + b_ref[...]
        a_ref[...] = (_silu(gc) * v_ref[...]).astype(bf16)

    return pl.pallas_call(
        body, name="ffn_act_fwd", grid=(nb,),
        in_specs=[pl.BlockSpec((t, tc), lambda j: (0, j)), pl.BlockSpec((t, tc), lambda j: (0, j + nb)),
                  pl.BlockSpec((FFN_CONV, tc), lambda j: (0, j)), pl.BlockSpec((1, tc), lambda j: (0, j))],
        out_specs=pl.BlockSpec((t, tc), lambda j: (0, j)), out_shape=_sds((t, D_FFP), bf16),
        compiler_params=_cparams(("parallel",)))(u, u, conv_w, conv_b)


def _ffn_act_bwd(u, conv_w, conv_b, da):
    t = u.shape[0]
    tc = FF_PAD // 2
    nb = D_FFP // tc

    def act(gc, val):
        return _silu(gc) * val

    def body(g_ref, v_ref, w_ref, b_ref, da_ref, a_ref, dg_ref, dv_ref, dw_ref, db_ref):
        gp = g_ref[...]
        gc = _conv_fwd(gp, w_ref, FFN_CONV) + b_ref[...]
        a, vjp = jax.vjp(act, gc, v_ref[...])
        dgc, dval = vjp(da_ref[...])
        a_ref[...] = a.astype(bf16)
        dv_ref[...] = dval.astype(bf16)
        db_ref[...] = jnp.sum(dgc, axis=0, keepdims=True)
        dg_ref[...] = _conv_bwd(gp, dgc, w_ref, dw_ref, FFN_CONV).astype(bf16)

    col = pl.BlockSpec((t, tc), lambda j: (0, j))
    return pl.pallas_call(
        body, name="ffn_act_bwd", grid=(nb,),
        in_specs=[col, pl.BlockSpec((t, tc), lambda j: (0, j + nb)), pl.BlockSpec((FFN_CONV, tc), lambda j: (0, j)),
                  pl.BlockSpec((1, tc), lambda j: (0, j)), col],
        out_specs=[col, col, col, pl.BlockSpec((FFN_CONV, tc), lambda j: (0, j)), pl.BlockSpec((1, tc), lambda j: (0, j))],
        out_shape=[_sds((t, D_FFP), bf16), _sds((t, D_FFP), bf16), _sds((t, D_FFP), bf16), _sds((FFN_CONV, D_FFP), f32),
                   _sds((1, D_FFP), f32)],
        compiler_params=_cparams(("parallel",)))(u, u, conv_w, conv_b, da)


def _lru_gates(xc, ra, ia, lam):
    r = _sigmoid(ra)
    i = _sigmoid(ia)
    log_a = -RG_C * r * _softplus(-lam)
    a = jnp.exp(log_a)
    u = jnp.sqrt(jnp.maximum(-_expm1(2.0 * log_a), 0.0)) * (i * xc)
    return a, u


def _lin_scan(a, u):
    n = a.shape[0]
    row = lax.broadcasted_iota(jnp.int32, a.shape, 0)
    s = 1
    while s < n:
        keep = row >= s
        u = a * jnp.where(keep, pltpu.roll(u, s, 0), 0.0) + u
        a = a * jnp.where(keep, pltpu.roll(a, s, 0), 1.0)
        s *= 2
    return u


def _rev_scan(a_next, d):
    n = d.shape[0]
    row = lax.broadcasted_iota(jnp.int32, d.shape, 0)
    a = a_next
    s = 1
    while s < n:
        keep = row < n - s
        d = a * jnp.where(keep, pltpu.roll(d, n - s, 0), 0.0) + d
        a = a * jnp.where(keep, pltpu.roll(a, n - s, 0), 1.0)
        s *= 2
    return d


def _col_conv_fwd(p, col_off, conv_w, conv_b, width, tc, name):
    t = p.shape[0]
    c = conv_w.shape[1]
    ob = col_off // tc

    def body(x_ref, w_ref, b_ref, o_ref):
        o_ref[...] = _conv_fwd(x_ref[...], w_ref, width) + b_ref[...]

    return pl.pallas_call(
        body, name=name, grid=(c // tc,),
        in_specs=[pl.BlockSpec((t, tc), lambda j: (0, j + ob)), pl.BlockSpec((width, tc), lambda j: (0, j)),
                  pl.BlockSpec((1, tc), lambda j: (0, j))],
        out_specs=pl.BlockSpec((t, tc), lambda j: (0, j)), out_shape=_sds((t, c), f32),
        compiler_params=_cparams(("parallel",)))(p, conv_w, conv_b)


def _col_conv_bwd(p, col_off, conv_w, dxc, width, tc, name):
    t = p.shape[0]
    c = conv_w.shape[1]
    ob = col_off // tc

    def body(x_ref, w_ref, d_ref, dx_ref, dw_ref, db_ref):
        d = d_ref[...]
        db_ref[...] = jnp.sum(d, axis=0, keepdims=True)
        dx_ref[...] = _conv_bwd(x_ref[...], d, w_ref, dw_ref, width).astype(bf16)

    col = pl.BlockSpec((t, tc), lambda j: (0, j))
    return pl.pallas_call(
        body, name=name, grid=(c // tc,),
        in_specs=[pl.BlockSpec((t, tc), lambda j: (0, j + ob)), pl.BlockSpec((width, tc), lambda j: (0, j)), col],
        out_specs=[col, pl.BlockSpec((width, tc), lambda j: (0, j)), pl.BlockSpec((1, tc), lambda j: (0, j))],
        out_shape=[_sds((t, c), bf16), _sds((width, c), f32), _sds((1, c), f32)],
        compiler_params=_cparams(("parallel",)))(p, conv_w, dxc)


def _lru_fwd(p, xc, wa, ba, wx, bx, lam):
    t = p.shape[0]
    bw = LRU_BLOCK

    def body(y_ref, xc_ref, wa_ref, ba_ref, wx_ref, bx_ref, lam_ref, out_ref, hs_ref, a_ref):
        xc_v = xc_ref[...]
        xb = xc_v.astype(bf16)
        ra = _dot(xb, wa_ref[0]) + ba_ref[...]
        ia = _dot(xb, wx_ref[0]) + bx_ref[...]
        a, u = _lru_gates(xc_v, ra, ia, lam_ref[...])
        a_ref[...] = a
        hs = _lin_scan(a, u)
        hs_ref[...] = hs
        out_ref[...] = (hs * _gelu(y_ref[...])).astype(bf16)

    col = pl.BlockSpec((t, bw), lambda h: (0, h))
    vec = pl.BlockSpec((1, bw), lambda h: (0, h))
    mat = pl.BlockSpec((1, bw, bw), lambda h: (h, 0, 0))
    return pl.pallas_call(
        body, name="lru_fwd", grid=(HEADS,), in_specs=[col, col, mat, vec, mat, vec, vec], out_specs=[col, col, col],
        out_shape=[_sds((t, LRU_WIDTH), bf16), _sds((t, LRU_WIDTH), f32), _sds((t, LRU_WIDTH), f32)],
        compiler_params=_cparams(("parallel",)))(p, xc, wa, ba, wx, bx, lam)


def _lru_bwd_scan(p, a, hs, dout):
    t = p.shape[0]
    bw = LRU_BLOCK

    def body(y_ref, a_ref, hs_ref, do_ref, dy_ref, da_ref, du_ref):
        hs_v = hs_ref[...]
        do = do_ref[...]
        gate, vjp = jax.vjp(_gelu, y_ref[...])
        dy_ref[...] = vjp(do * hs_v)[0].astype(bf16)
        g = _rev_scan(_shift_up(a_ref[...], 1), do * gate)
        du_ref[...] = g
        da_ref[...] = g * _shift_down(hs_v, 1)

    col = pl.BlockSpec((t, bw), lambda h: (0, h))
    return pl.pallas_call(
        body, name="lru_bwd_scan", grid=(HEADS,), in_specs=[col, col, col, col], out_specs=[col, col, col],
        out_shape=[_sds((t, LRU_WIDTH), bf16), _sds((t, LRU_WIDTH), f32), _sds((t, LRU_WIDTH), f32)],
        compiler_params=_cparams(("parallel",)))(p, a, hs, dout)


def _lru_bwd_gates(xc, da, du, wa, ba, wx, bx, lam):
    t = xc.shape[0]
    bw = LRU_BLOCK
    tr = _tile(t, (512, 256, 128))

    def body(xc_ref, da_ref, du_ref, wa_ref, ba_ref, wx_ref, bx_ref, lam_ref,
             dxc_ref, dwa_ref, dwx_ref, dba_ref, dbx_ref, dlam_ref):
        xc_v = xc_ref[...]
        xb = xc_v.astype(bf16)
        ra = _dot(xb, wa_ref[0]) + ba_ref[...]
        ia = _dot(xb, wx_ref[0]) + bx_ref[...]
        _, vjp = jax.vjp(_lru_gates, xc_v, ra, ia, lam_ref[...])
        dxc, dra, dia, dlam = vjp((da_ref[...], du_ref[...]))
        drb, dib = dra.astype(bf16), dia.astype(bf16)
        dxc_ref[...] = dxc + _dot(drb, wa_ref[0], NT) + _dot(dib, wx_ref[0], NT)

        @pl.when(pl.program_id(1) == 0)
        def _():
            dwa_ref[...] = jnp.zeros_like(dwa_ref)
            dwx_ref[...] = jnp.zeros_like(dwx_ref)
            dba_ref[...] = jnp.zeros_like(dba_ref)
            dbx_ref[...] = jnp.zeros_like(dbx_ref)
            dlam_ref[...] = jnp.zeros_like(dlam_ref)

        dwa_ref[0] += _dot(xb, drb, TN)
        dwx_ref[0] += _dot(xb, dib, TN)
        dba_ref[...] += jnp.sum(dra, axis=0, keepdims=True)
        dbx_ref[...] += jnp.sum(dia, axis=0, keepdims=True)
        dlam_ref[...] += dlam

    tile = pl.BlockSpec((tr, bw), lambda h, i: (i, h))
    vec = pl.BlockSpec((1, bw), lambda h, i: (0, h))
    mat = pl.BlockSpec((1, bw, bw), lambda h, i: (h, 0, 0))
    return pl.pallas_call(
        body, name="lru_bwd_gates", grid=(HEADS, t // tr), in_specs=[tile, tile, tile, mat, vec, mat, vec, vec],
        out_specs=[tile, mat, mat, vec, vec, vec],
        out_shape=[_sds((t, LRU_WIDTH), f32), _sds((HEADS, bw, bw), f32), _sds((HEADS, bw, bw), f32),
                   _sds((1, LRU_WIDTH), f32), _sds((1, LRU_WIDTH), f32), _sds((1, LRU_WIDTH), f32)],
        compiler_params=_cparams(("parallel", "arbitrary")))(xc, da, du, wa, ba, wx, bx, lam)


def _gdn_pre_fn(cq, ck, cv, ba, alog, dtb, h):
    q, k, v = _silu(cq), _silu(ck), _silu(cv)
    q = q * lax.rsqrt(jnp.sum(q * q, axis=-1, keepdims=True) + EPS) * (HEAD_DIM ** -0.5)
    k = k * lax.rsqrt(jnp.sum(k * k, axis=-1, keepdims=True) + EPS)
    lane = lax.broadcasted_iota(jnp.int32, (1, HEAD_DIM), 1)
    mb = (lane == h).astype(f32)
    ma = (lane == HEADS + h).astype(f32)
    beta_raw = jnp.sum(ba * mb, axis=-1, keepdims=True)
    alpha = jnp.sum(ba * ma, axis=-1, keepdims=True)
    al = jnp.sum(alog * mb, axis=-1, keepdims=True)
    db = jnp.sum(dtb * mb, axis=-1, keepdims=True)
    beta = _sigmoid(beta_raw)
    g = -jnp.exp(al) * _softplus(alpha + db)
    return q, k, v, jnp.broadcast_to(beta, q.shape), jnp.broadcast_to(g, q.shape)


def _gdn_pre_fwd(p, conv_w, alog, dtb):
    t = p.shape[0]
    hd = HEAD_DIM

    def body(pq_ref, pk_ref, pv_ref, ba_ref, wq_ref, wk_ref, wv_ref, al_ref, dt_ref, q_ref, k_ref, v_ref, b_ref, g_ref):
        h = pl.program_id(0)
        cq = _conv_fwd(pq_ref[...], wq_ref, GDN_CONV)
        ck = _conv_fwd(pk_ref[...], wk_ref, GDN_CONV)
        cv = _conv_fwd(pv_ref[...], wv_ref, GDN_CONV)
        q, k, v, be, ge = _gdn_pre_fn(cq, ck, cv, ba_ref[...], al_ref[...], dt_ref[...], h)
        q_ref[...], k_ref[...], v_ref[...], b_ref[...], g_ref[...] = q, k, v, be, ge

    def pcol(off):
        return pl.BlockSpec((t, hd), lambda h: (0, h + off // hd))

    def wcol(off):
        return pl.BlockSpec((GDN_CONV, hd), lambda h: (0, h + off // hd))

    vec = pl.BlockSpec((1, hd), lambda h: (0, 0))
    out = pl.BlockSpec((t, hd), lambda h: (0, h))
    return pl.pallas_call(
        body, name="gdn_pre_fwd", grid=(HEADS,),
        in_specs=[pcol(OFF_Q), pcol(OFF_K), pcol(OFF_V), pl.BlockSpec((t, hd), lambda h: (0, OFF_BA // hd)),
                  wcol(0), wcol(GDN_WIDTH), wcol(2 * GDN_WIDTH), vec, vec],
        out_specs=[out] * 5, out_shape=[_sds((t, GDN_WIDTH), f32)] * 5,
        compiler_params=_cparams(("parallel",)))(p, p, p, p, conv_w, conv_w, conv_w, alog, dtb)


def _gdn_pre_bwd(p, conv_w, alog, dtb, dq, dk, dv, dbe, dge):
    t = p.shape[0]
    hd = HEAD_DIM

    def body(pq_ref, pk_ref, pv_ref, ba_ref, wq_ref, wk_ref, wv_ref, al_ref, dt_ref,
             dq_ref, dk_ref, dv_ref, dbe_ref, dge_ref,
             opq_ref, opk_ref, opv_ref, dba_ref, dwq_ref, dwk_ref, dwv_ref, dal_ref, ddt_ref):
        h = pl.program_id(0)
        pq, pk, pv = pq_ref[...], pk_ref[...], pv_ref[...]
        cq = _conv_fwd(pq, wq_ref, GDN_CONV)
        ck = _conv_fwd(pk, wk_ref, GDN_CONV)
        cv = _conv_fwd(pv, wv_ref, GDN_CONV)
        _, vjp = jax.vjp(functools.partial(_gdn_pre_fn, h=h), cq, ck, cv, ba_ref[...], al_ref[...], dt_ref[...])
        dcq, dck, dcv, dba, dal, ddt = vjp((dq_ref[...], dk_ref[...], dv_ref[...], dbe_ref[...], dge_ref[...]))
        opq_ref[...] = _conv_bwd(pq, dcq, wq_ref, dwq_ref, GDN_CONV).astype(bf16)
        opk_ref[...] = _conv_bwd(pk, dck, wk_ref, dwk_ref, GDN_CONV).astype(bf16)
        opv_ref[...] = _conv_bwd(pv, dcv, wv_ref, dwv_ref, GDN_CONV).astype(bf16)

        @pl.when(h == 0)
        def _():
            dba_ref[...] = jnp.zeros_like(dba_ref)
            dal_ref[...] = jnp.zeros_like(dal_ref)
            ddt_ref[...] = jnp.zeros_like(ddt_ref)

        dba_ref[...] += dba
        dal_ref[...] += dal
        ddt_ref[...] += ddt

    def pcol(off):
        return pl.BlockSpec((t, hd), lambda h: (0, h + off // hd))

    def wcol(off):
        return pl.BlockSpec((GDN_CONV, hd), lambda h: (0, h + off // hd))

    vec = pl.BlockSpec((1, hd), lambda h: (0, 0))
    col = pl.BlockSpec((t, hd), lambda h: (0, h))
    full = pl.BlockSpec((t, hd), lambda h: (0, 0))
    wout = pl.BlockSpec((GDN_CONV, hd), lambda h: (0, h))
    return pl.pallas_call(
        body, name="gdn_pre_bwd", grid=(HEADS,),
        in_specs=[pcol(OFF_Q), pcol(OFF_K), pcol(OFF_V), pl.BlockSpec((t, hd), lambda h: (0, OFF_BA // hd)),
                  wcol(0), wcol(GDN_WIDTH), wcol(2 * GDN_WIDTH), vec, vec, col, col, col, col, col],
        out_specs=[col, col, col, full, wout, wout, wout, vec, vec],
        out_shape=[_sds((t, GDN_WIDTH), bf16)] * 3 + [_sds((t, hd), f32)] + [_sds((GDN_CONV, GDN_WIDTH), f32)] * 3
        + [_sds((1, hd), f32)] * 2,
        compiler_params=_cparams(("arbitrary",)))(p, p, p, p, conv_w, conv_w, conv_w, alog, dtb, dq, dk, dv, dbe, dge)


BNN = (((2,), (1,)), ((0,), (0,)))
BNT = (((2,), (2,)), ((0,), (0,)))
BTN = (((1,), (1,)), ((0,), (0,)))


def _hdot(a, b, dn=BNN, precision=None):
    return lax.dot_general(a, b, dn, precision=precision, preferred_element_type=f32)


def _hbdot(a, b, dn=BNN):
    return _hdot(a.astype(bf16), b.astype(bf16), dn)


def _tri_inverse(a):
    c = a.shape[-1]
    r = lax.broadcasted_iota(jnp.int32, (c, c), 0)
    col = lax.broadcasted_iota(jnp.int32, (c, c), 1)
    m = -a
    inv = jnp.where(r == col, 1.0, 0.0) + m
    s = 2
    while s < c:
        m = _hdot(m, m, precision=HI)
        inv = inv + _hdot(inv, m, precision=HI)
        s *= 2
    return inv


def _gdn_chunk(s, q, k, v, ge, be):
    nh, c, _ = q.shape
    r = lax.broadcasted_iota(jnp.int32, (c, c), 0)
    col = lax.broadcasted_iota(jnp.int32, (c, c), 1)
    causal = r >= col
    tri = jnp.broadcast_to(causal.astype(f32), (nh, c, c))
    gc = _hdot(tri, ge, precision=HI)
    gcc = gc[:, :, :c]
    gcr = jnp.swapaxes(gc, 1, 2)[:, :c, :]
    decay = jnp.where(causal, jnp.exp(jnp.where(causal, gcc - gcr, 0.0)), 0.0)
    kb = k * be
    lower = jnp.where(r > col, _hbdot(kb, k, BNT) * decay, 0.0)
    tinv = _tri_inverse(lower)
    egc = jnp.exp(gc)
    u = _hdot(tinv, v * be, precision=HI)
    w = _hdot(tinv, kb * egc, precision=HI)
    attn = _hbdot(q, k, BNT) * decay
    gl = gc[:, c - 1:c, :]
    v_new = u - _hbdot(w, s)
    o = _hbdot(q * egc, s) + _hbdot(attn, v_new)
    s_new = s * jnp.exp(gl) + _hbdot(k * jnp.exp(gl - gc), v_new, BTN)
    return o, s_new


def _heads_major(ref):
    return jnp.stack([ref[:, h * HEAD_DIM:(h + 1) * HEAD_DIM] for h in range(HEADS)])


def _gdn_core_fwd(q, k, v, ge, be):
    t = q.shape[0]
    c, hd = GDN_CHUNK, HEAD_DIM
    n = t // c

    def body(q_ref, k_ref, v_ref, g_ref, b_ref, o_ref, st_ref, s_ref):
        @pl.when(pl.program_id(0) == 0)
        def _():
            s_ref[...] = jnp.zeros_like(s_ref)

        s = s_ref[...]
        st_ref[:, 0] = s
        o, s_new = _gdn_chunk(s, *[_heads_major(r) for r in (q_ref, k_ref, v_ref, g_ref, b_ref)])
        for h in range(HEADS):
            o_ref[:, h * hd:(h + 1) * hd] = o[h]
        s_ref[...] = s_new

    tile = pl.BlockSpec((c, GDN_WIDTH), lambda i: (i, 0))
    return pl.pallas_call(
        body, name="gdn_core_fwd", grid=(n,), in_specs=[tile] * 5,
        out_specs=[tile, pl.BlockSpec((HEADS, 1, hd, hd), lambda i: (0, i, 0, 0))],
        out_shape=[_sds((t, GDN_WIDTH), f32), _sds((HEADS, n, hd, hd), f32)],
        scratch_shapes=[pltpu.VMEM((HEADS, hd, hd), f32)],
        compiler_params=_cparams(("arbitrary",)))(q, k, v, ge, be)


def _gdn_core_bwd(q, k, v, ge, be, states, do):
    t = q.shape[0]
    c, hd = GDN_CHUNK, HEAD_DIM
    n = t // c

    def body(q_ref, k_ref, v_ref, g_ref, b_ref, st_ref, do_ref, dq_ref, dk_ref, dv_ref, dg_ref, db_ref, ds_ref):
        @pl.when(pl.program_id(0) == 0)
        def _():
            ds_ref[...] = jnp.zeros_like(ds_ref)

        _, vjp = jax.vjp(_gdn_chunk, st_ref[:, 0], *[_heads_major(r) for r in (q_ref, k_ref, v_ref, g_ref, b_ref)])
        ds, *dins = vjp((_heads_major(do_ref), ds_ref[...]))
        ds_ref[...] = ds
        for d_ref, d in zip((dq_ref, dk_ref, dv_ref, dg_ref, db_ref), dins):
            for h in range(HEADS):
                d_ref[:, h * hd:(h + 1) * hd] = d[h]

    tile = pl.BlockSpec((c, GDN_WIDTH), lambda i: (n - 1 - i, 0))
    return pl.pallas_call(
        body, name="gdn_core_bwd", grid=(n,),
        in_specs=[tile] * 5 + [pl.BlockSpec((HEADS, 1, hd, hd), lambda i: (0, n - 1 - i, 0, 0)), tile],
        out_specs=[tile] * 5, out_shape=[_sds((t, GDN_WIDTH), f32)] * 5,
        scratch_shapes=[pltpu.VMEM((HEADS, hd, hd), f32)],
        compiler_params=_cparams(("arbitrary",)))(q, k, v, ge, be, states, do)


def _post_fn(o, z, gain):
    return _rms(o, gain) * _silu(z)


def _post_fwd(o, p, z_off, gain, name):
    t = o.shape[0]
    hd = HEAD_DIM

    def body(o_ref, z_ref, g_ref, y_ref):
        y_ref[...] = _post_fn(o_ref[...], z_ref[...], g_ref[...]).astype(bf16)

    col = pl.BlockSpec((t, hd), lambda h: (0, h))
    return pl.pallas_call(
        body, name=name, grid=(HEADS,),
        in_specs=[col, pl.BlockSpec((t, hd), lambda h: (0, h + z_off // hd)), pl.BlockSpec((1, hd), lambda h: (0, 0))],
        out_specs=col, out_shape=_sds((t, HEADS * hd), bf16), compiler_params=_cparams(("parallel",)))(o, p, gain)


def _post_bwd(o, p, z_off, gain, dmix, mix_off, name):
    t = o.shape[0]
    hd = HEAD_DIM

    def body(o_ref, z_ref, g_ref, dy_ref, do_ref, dz_ref, dg_ref):
        _, vjp = jax.vjp(_post_fn, o_ref[...], z_ref[...], g_ref[...])
        do, dz, dg = vjp(dy_ref[...])
        do_ref[...] = do
        dz_ref[...] = dz.astype(bf16)

        @pl.when(pl.program_id(0) == 0)
        def _():
            dg_ref[...] = jnp.zeros_like(dg_ref)

        dg_ref[...] += dg

    col = pl.BlockSpec((t, hd), lambda h: (0, h))
    vec = pl.BlockSpec((1, hd), lambda h: (0, 0))
    return pl.pallas_call(
        body, name=name, grid=(HEADS,),
        in_specs=[col, pl.BlockSpec((t, hd), lambda h: (0, h + z_off // hd)), vec,
                  pl.BlockSpec((t, hd), lambda h: (0, h + mix_off // hd))],
        out_specs=[col, col, vec], out_shape=[_sds((t, HEADS * hd), f32), _sds((t, HEADS * hd), bf16), _sds((1, hd), f32)],
        compiler_params=_cparams(("arbitrary",)))(o, p, gain, dmix)


def _hgrn_pre_fn(qb, fb, lbw, layer):
    l0, l1 = lbw[0:1, :], lbw[1:2, :]
    m = jnp.maximum(l0, l1)
    e0, e1 = jnp.exp(l0 - m), jnp.exp(l1 - m)
    p0, p1 = e0 / (e0 + e1), e1 / (e0 + e1)
    lb = (p0 - p0) if layer == 0 else ((p0 + p1) - p0)
    f = lb + (1.0 - lb) * _sigmoid(fb)
    return _silu(qb), 1.0 - f, jnp.log(jnp.maximum(f, F_FLOOR))


def _hgrn_pre_fwd(p, lbw, layer):
    t = p.shape[0]
    tc = HEAD_DIM

    def body(qb_ref, fb_ref, lb_ref, q_ref, k_ref, lf_ref):
        q_ref[...], k_ref[...], lf_ref[...] = _hgrn_pre_fn(qb_ref[...], fb_ref[...], lb_ref[...], layer)

    col = pl.BlockSpec((t, tc), lambda j: (0, j))
    return pl.pallas_call(
        body, name="hgrn_pre_fwd", grid=(GDN_WIDTH // tc,),
        in_specs=[pl.BlockSpec((t, tc), lambda j: (0, j + OFF_QB // tc)), pl.BlockSpec((t, tc), lambda j: (0, j + OFF_FB // tc)),
                  pl.BlockSpec((2, tc), lambda j: (0, j))],
        out_specs=[col] * 3, out_shape=[_sds((t, GDN_WIDTH), f32)] * 3,
        compiler_params=_cparams(("parallel",)))(p, p, lbw)


def _hgrn_pre_bwd(p, lbw, layer, dq, dk, dlf):
    t = p.shape[0]
    tc = HEAD_DIM

    def body(qb_ref, fb_ref, lb_ref, dq_ref, dk_ref, dlf_ref, dqb_ref, dfb_ref, dlb_ref):
        _, vjp = jax.vjp(functools.partial(_hgrn_pre_fn, layer=layer), qb_ref[...], fb_ref[...], lb_ref[...])
        dqb, dfb, dlb = vjp((dq_ref[...], dk_ref[...], dlf_ref[...]))
        dqb_ref[...] = dqb.astype(bf16)
        dfb_ref[...] = dfb.astype(bf16)
        dlb_ref[...] = dlb

    col = pl.BlockSpec((t, tc), lambda j: (0, j))
    lb = pl.BlockSpec((2, tc), lambda j: (0, j))
    return pl.pallas_call(
        body, name="hgrn_pre_bwd", grid=(GDN_WIDTH // tc,),
        in_specs=[pl.BlockSpec((t, tc), lambda j: (0, j + OFF_QB // tc)), pl.BlockSpec((t, tc), lambda j: (0, j + OFF_FB // tc)),
                  lb, col, col, col],
        out_specs=[col, col, lb], out_shape=[_sds((t, GDN_WIDTH), bf16)] * 2 + [_sds((2, GDN_WIDTH), f32)],
        compiler_params=_cparams(("parallel",)))(p, p, lbw, dq, dk, dlf)


def _hgrn_step(st, q, k, lf, v):
    c = HGRN_CHUNK
    nh = q.shape[0]
    r2 = lax.broadcasted_iota(jnp.int32, (c, c), 0)
    c2 = lax.broadcasted_iota(jnp.int32, (c, c), 1)
    tri = jnp.broadcast_to((r2 >= c2).astype(f32), (nh, c, c))
    i3 = lax.broadcasted_iota(jnp.int32, (c, c, HEAD_DIM), 0)
    j3 = lax.broadcasted_iota(jnp.int32, (c, c, HEAD_DIM), 1)
    mask = i3 >= j3
    outs = []
    for n in range(q.shape[1] // c):
        sl = slice(n * c, (n + 1) * c)
        qc, kc, lc, vc = q[:, sl], k[:, sl], lf[:, sl], v[:, sl]
        b = _hdot(tri, lc, precision=HI)
        rel = jnp.where(mask, jnp.exp(jnp.where(mask, b[:, :, None, :] - b[:, None, :, :], 0.0)), 0.0)
        scores = jnp.sum(qc[:, :, None, :] * kc[:, None, :, :] * rel, axis=-1)
        bl = b[:, c - 1:c, :]
        o = _hbdot(scores, vc) + _hbdot(qc * jnp.exp(b), st, BNT)
        st = st * jnp.exp(bl) + _hbdot(vc, kc * jnp.exp(bl - b), BTN)
        outs.append(o)
    return jnp.concatenate(outs, axis=1), st


def _hgrn_core_fwd(q, k, lf, p):
    t = q.shape[0]
    hd = HEAD_DIM
    rs = min(HGRN_STEP, t)
    n = t // rs

    def body(q_ref, k_ref, lf_ref, v_ref, o_ref, st_ref, s_ref):
        @pl.when(pl.program_id(0) == 0)
        def _():
            s_ref[...] = jnp.zeros_like(s_ref)

        s = s_ref[...]
        st_ref[:, 0] = s
        o, s_new = _hgrn_step(s, *[_heads_major(r) for r in (q_ref, k_ref, lf_ref, v_ref)])
        for h in range(HEADS):
            o_ref[:, h * hd:(h + 1) * hd] = o[h]
        s_ref[...] = s_new

    tile = pl.BlockSpec((rs, GDN_WIDTH), lambda i: (i, 0))
    return pl.pallas_call(
        body, name="hgrn_core_fwd", grid=(n,),
        in_specs=[tile, tile, tile, pl.BlockSpec((rs, GDN_WIDTH), lambda i: (i, OFF_IB // GDN_WIDTH))],
        out_specs=[tile, pl.BlockSpec((HEADS, 1, hd, hd), lambda i: (0, i, 0, 0))],
        out_shape=[_sds((t, GDN_WIDTH), f32), _sds((HEADS, n, hd, hd), f32)],
        scratch_shapes=[pltpu.VMEM((HEADS, hd, hd), f32)],
        compiler_params=_cparams(("arbitrary",)))(q, k, lf, p)


def _hgrn_core_bwd(q, k, lf, p, states, do):
    t = q.shape[0]
    hd = HEAD_DIM
    rs = min(HGRN_STEP, t)
    n = t // rs

    def body(q_ref, k_ref, lf_ref, v_ref, st_ref, do_ref, dq_ref, dk_ref, dlf_ref, dv_ref, ds_ref):
        @pl.when(pl.program_id(0) == 0)
        def _():
            ds_ref[...] = jnp.zeros_like(ds_ref)

        _, vjp = jax.vjp(_hgrn_step, st_ref[:, 0], *[_heads_major(r) for r in (q_ref, k_ref, lf_ref, v_ref)])
        ds, *dins = vjp((_heads_major(do_ref), ds_ref[...]))
        ds_ref[...] = ds
        for d_ref, d in zip((dq_ref, dk_ref, dlf_ref, dv_ref), dins):
            for h in range(HEADS):
                d_ref[:, h * hd:(h + 1) * hd] = d[h].astype(d_ref.dtype)

    tile = pl.BlockSpec((rs, GDN_WIDTH), lambda i: (n - 1 - i, 0))
    return pl.pallas_call(
        body, name="hgrn_core_bwd", grid=(n,),
        in_specs=[tile, tile, tile, pl.BlockSpec((rs, GDN_WIDTH), lambda i: (n - 1 - i, OFF_IB // GDN_WIDTH)),
                  pl.BlockSpec((HEADS, 1, hd, hd), lambda i: (0, n - 1 - i, 0, 0)), tile],
        out_specs=[tile] * 4, out_shape=[_sds((t, GDN_WIDTH), f32)] * 3 + [_sds((t, GDN_WIDTH), bf16)],
        scratch_shapes=[pltpu.VMEM((HEADS, hd, hd), f32)],
        compiler_params=_cparams(("arbitrary",)))(q, k, lf, p, states, do)


def _row(v):
    return v.reshape(1, -1)


def _pad_lanes(v, n=HEAD_DIM):
    return jnp.pad(v.reshape(1, -1), ((0, 0), (0, n - v.shape[-1])))


def _ffn_fwd(x, w, l):
    h = _rms_fwd(x, _row(w['norm_ffn'][l]), "ffn_norm")
    u = _mm(h, w['ffn_w_up'][l], name="ffn_up")
    a = _ffn_act_fwd(u, w['ffn_conv_w'][l], _row(w['ffn_conv_b'][l]))
    y = _mm(a, w['ffn_w_down'][l], add=x, name="ffn_down")
    return y, (x, h, u)


def _ffn_bwd(saved, w, l, dy, dyb, grads):
    x, h, u = saved
    da = _mm(dyb, w['ffn_w_down'][l], tb=True, name="ffn_down_dx")
    a, dg, dv, dcw, dcb = _ffn_act_bwd(u, w['ffn_conv_w'][l], _row(w['ffn_conv_b'][l]), da)
    grads['ffn_w_down'][l] = _mm(a, dyb, ta=True, out_dtype=bf16, name="ffn_down_dw")
    du = jnp.concatenate([dg, dv], axis=1)
    grads['ffn_w_up'][l] = _mm(h, du, ta=True, out_dtype=bf16, name="ffn_up_dw")
    dh = _mm(du, w['ffn_w_up'][l], tb=True, name="ffn_up_dx")
    dx, dxb, dgain = _rms_bwd(x, _row(w['norm_ffn'][l]), dh, dy, "ffn_norm_bwd")
    grads['ffn_conv_w'][l] = dcw
    grads['ffn_conv_b'][l] = dcb[0]
    grads['norm_ffn'][l] = dgain[0]
    return dx, dxb


def _odd_fwd(x, w, l, j):
    h = _rms_fwd(x, _row(w['norm_mix'][l]), "mix_norm")
    p = _mm(h, w['c_w_in'][j], name="lru_in")
    xc = _col_conv_fwd(p, LRU_WIDTH, w['c_conv_w'][j], _row(w['c_conv_b'][j]), LRU_CONV, 256, "lru_conv_fwd")
    out, hs, a = _lru_fwd(p, xc, w['c_gate_a_w'][j], _row(w['c_gate_a_b'][j]), w['c_gate_x_w'][j],
                          _row(w['c_gate_x_b'][j]), _row(w['c_lambda'][j]))
    y = _mm(out, w['c_w_out'][j], add=x, name="lru_out")
    return y, (x, h, p, xc, out, hs, a)


def _odd_bwd(saved, w, l, j, dy, dyb, grads):
    x, h, p, xc, out, hs, a = saved
    dout = _mm(dyb, w['c_w_out'][j], tb=True, name="lru_out_dx")
    grads['c_w_out'][j] = _mm(out, dyb, ta=True, out_dtype=bf16, name="lru_out_dw")
    dyb_, da, du = _lru_bwd_scan(p, a, hs, dout)
    dxc, dwa, dwx, dba, dbx, dlam = _lru_bwd_gates(xc, da, du, w['c_gate_a_w'][j], _row(w['c_gate_a_b'][j]),
                                                   w['c_gate_x_w'][j], _row(w['c_gate_x_b'][j]), _row(w['c_lambda'][j]))
    dxb_, dcw, dcb = _col_conv_bwd(p, LRU_WIDTH, w['c_conv_w'][j], dxc, LRU_CONV, 256, "lru_conv_bwd")
    dp = jnp.concatenate([dyb_, dxb_], axis=1)
    grads['c_w_in'][j] = _mm(h, dp, ta=True, out_dtype=bf16, name="lru_in_dw")
    dh = _mm(dp, w['c_w_in'][j], tb=True, name="lru_in_dx")
    dx, dxb, dgain = _rms_bwd(x, _row(w['norm_mix'][l]), dh, dy, "mix_norm_bwd")
    grads['c_gate_a_w'][j], grads['c_gate_x_w'][j] = dwa, dwx
    grads['c_gate_a_b'][j], grads['c_gate_x_b'][j], grads['c_lambda'][j] = dba[0], dbx[0], dlam[0]
    grads['c_conv_w'][j], grads['c_conv_b'][j] = dcw, dcb[0]
    grads['norm_mix'][l] = dgain[0]
    return dx, dxb


def _even_fwd(x, w, l, j):
    h = _rms_fwd(x, _row(w['norm_mix'][l]), "mix_norm")
    p = _mm(h, w['ab_w_in'][j], name="ab_in")
    alog, dtb = _pad_lanes(w['gdn_a_log'][j]), _pad_lanes(w['gdn_dt_bias'][j])
    q, k, v, be, ge = _gdn_pre_fwd(p, w['gdn_conv_w'][j], alog, dtb)
    oa, sa = _gdn_core_fwd(q, k, v, ge, be)
    ya = _post_fwd(oa, p, OFF_Z, _row(w['gdn_norm'][j]), "gdn_post_fwd")
    qq, kk, lf = _hgrn_pre_fwd(p, w['hgrn_lower_bounds'], j)
    ob, sb = _hgrn_core_fwd(qq, kk, lf, p)
    yb = _post_fwd(ob, p, OFF_GB, _row(w['hgrn_norm'][j]), "hgrn_post_fwd")
    mix = jnp.concatenate([ya, yb], axis=1)
    y = _mm(mix, w['ab_w_out'][j], add=x, name="ab_out")
    return y, (x, h, p, q, k, v, be, ge, oa, sa, qq, kk, lf, ob, sb, mix)


def _even_bwd(saved, w, l, j, dy, dyb, grads):
    x, h, p, q, k, v, be, ge, oa, sa, qq, kk, lf, ob, sb, mix = saved
    alog, dtb = _pad_lanes(w['gdn_a_log'][j]), _pad_lanes(w['gdn_dt_bias'][j])
    dmix = _mm(dyb, w['ab_w_out'][j], tb=True, name="ab_out_dx")
    grads['ab_w_out'][j] = _mm(mix, dyb, ta=True, out_dtype=bf16, name="ab_out_dw")
    doa, dz, dgn = _post_bwd(oa, p, OFF_Z, _row(w['gdn_norm'][j]), dmix, 0, "gdn_post_bwd")
    dob, dgb, dhn = _post_bwd(ob, p, OFF_GB, _row(w['hgrn_norm'][j]), dmix, GDN_WIDTH, "hgrn_post_bwd")
    dq, dk, dv, dge, dbe = _gdn_core_bwd(q, k, v, ge, be, sa, doa)
    dpq, dpk, dpv, dba, dwq, dwk, dwv, dal, ddt = _gdn_pre_bwd(p, w['gdn_conv_w'][j], alog, dtb, dq, dk, dv, dbe, dge)
    dqq, dkk, dlf, dib = _hgrn_core_bwd(qq, kk, lf, p, sb, dob)
    dqb, dfb, dlb = _hgrn_pre_bwd(p, w['hgrn_lower_bounds'], j, dqq, dkk, dlf)
    dp = jnp.concatenate([dpq, dpk, dpv, dz, dqb, dfb, dib, dgb, dba.astype(bf16)], axis=1)
    grads['ab_w_in'][j] = _mm(h, dp, ta=True, out_dtype=bf16, name="ab_in_dw")
    dh = _mm(dp, w['ab_w_in'][j], tb=True, name="ab_in_dx")
    dx, dxb, dgain = _rms_bwd(x, _row(w['norm_mix'][l]), dh, dy, "mix_norm_bwd")
    grads['gdn_conv_w'][j] = jnp.concatenate([dwq, dwk, dwv], axis=1)
    grads['gdn_a_log'][j], grads['gdn_dt_bias'][j] = dal[0, :HEADS], ddt[0, :HEADS]
    grads['gdn_norm'][j], grads['hgrn_norm'][j] = dgn[0], dhn[0]
    grads['hgrn_lower_bounds'].append(dlb)
    grads['norm_mix'][l] = dgain[0]
    return dx, dxb


def _ab_permute(w_in):
    pad = jnp.zeros(w_in.shape[:-1] + (AB_PAD - AB_COLS,), w_in.dtype)
    return jnp.concatenate([w_in[..., :2048], w_in[..., 2056:], w_in[..., 2048:2056], pad], axis=-1)


def _ab_unpermute(g):
    return jnp.concatenate([g[..., :2048], g[..., 4096:4104], g[..., 2048:4096]], axis=-1)


def _block_pad(a, axis, nblk, padded):
    axis = axis % a.ndim
    s = a.shape
    a = a.reshape(s[:axis] + (nblk, s[axis] // nblk) + s[axis + 1:])
    pad = [(0, 0)] * a.ndim
    pad[axis + 1] = (0, padded - s[axis] // nblk)
    return jnp.pad(a, pad).reshape(s[:axis] + (nblk * padded,) + s[axis + 1:])


def _block_unpad(a, axis, nblk, width):
    axis = axis % a.ndim
    s = a.shape
    a = a.reshape(s[:axis] + (nblk, s[axis] // nblk) + s[axis + 1:])
    a = lax.slice_in_dim(a, 0, width, axis=axis + 1)
    return a.reshape(s[:axis] + (nblk * width,) + s[axis + 1:])


def _kernel_layout(w):
    w = dict(w)
    w['ab_w_in'] = _ab_permute(w['ab_w_in'])
    w['ffn_w_up'] = _block_pad(w['ffn_w_up'], 2, N_DEV, FF_PAD)
    w['ffn_w_down'] = _block_pad(w['ffn_w_down'], 1, 4, FF_PAD)
    w['ffn_conv_w'] = _block_pad(w['ffn_conv_w'], 2, 4, FF_PAD)
    w['ffn_conv_b'] = _block_pad(w['ffn_conv_b'], 1, 4, FF_PAD)
    return w


def _natural_grads(g):
    g = dict(g)
    g['ab_w_in'] = _ab_unpermute(g['ab_w_in'])
    g['ffn_w_up'] = _block_unpad(g['ffn_w_up'], 2, N_DEV, FF_SHARD)
    g['ffn_w_down'] = _block_unpad(g['ffn_w_down'], 1, 4, FF_SHARD)
    g['ffn_conv_w'] = _block_unpad(g['ffn_conv_w'], 2, 4, FF_SHARD)
    g['ffn_conv_b'] = _block_unpad(g['ffn_conv_b'], 1, 4, FF_SHARD)
    return g


def _local_step(x, target, w, fetch=None, push=None):
    grads = {n: [None] * (DEPTH if n in ('norm_mix', 'norm_ffn') or n.startswith('ffn_') else 2)
             for n in WEIGHTS if n not in ('norm_final', 'hgrn_lower_bounds')}
    grads['hgrn_lower_bounds'] = []
    saved = []
    for l in range(DEPTH):
        j = l // 2
        if fetch is not None:
            fetch(l, x)
        x, s_mix = (_even_fwd if l % 2 == 0 else _odd_fwd)(x, w, l, j)
        x, s_ffn = _ffn_fwd(x, w, l)
        saved.append((s_mix, s_ffn))
    loss, dx, dxb, dgf = _loss_head(x, _row(w['norm_final']), target)
    for l in reversed(range(DEPTH)):
        j = l // 2
        s_mix, s_ffn = saved[l]
        dx, dxb = _ffn_bwd(s_ffn, w, l, dx, dxb, grads)
        dx, dxb = (_even_bwd if l % 2 == 0 else _odd_bwd)(s_mix, w, l, j, dx, dxb, grads)
        if push is not None:
            push(l, {nm: grads[nm].pop(li) for nm, li in reversed(_layer_items(l))})
    out = {n: jnp.stack(g) for n, g in grads.items() if n != 'hgrn_lower_bounds' and g}
    out['hgrn_lower_bounds'] = grads['hgrn_lower_bounds'][0] + grads['hgrn_lower_bounds'][1]
    out['norm_final'] = dgf[0]
    return loss[0, 0], dx, out


def _position():
    return lax.axis_index("x"), lax.axis_index("y"), lax.axis_index("c")


BLOCK_LAYOUT = {
    'ab_w_in': ((2, D_MODEL, N_DEV * AB_SHARD_PAD), (2, D_MODEL, AB_SHARD_PAD)),
    'ab_w_out': ((2, N_DEV, 128, D_MODEL), (2, 128, D_MODEL)),
    'c_w_in': ((2, D_MODEL, 2 * LRU_WIDTH), (2, D_MODEL, 256)),
    'c_w_out': ((2, N_DEV, 128, D_MODEL), (2, 128, D_MODEL)),
    'c_gate_a_w': ((2, HEADS, N_DEV, 32, LRU_BLOCK), (2, HEADS, 32, LRU_BLOCK)),
    'c_gate_x_w': ((2, HEADS, N_DEV, 32, LRU_BLOCK), (2, HEADS, 32, LRU_BLOCK)),
    'ffn_w_up': ((DEPTH, D_MODEL, N_DEV * FF_PAD), (DEPTH, D_MODEL, FF_PAD)),
    'ffn_w_down': ((DEPTH, 4, FF_PAD, D_MODEL), (DEPTH, FF_ROWS, D_MODEL)),
}


COL_WINDOW = {'ab_w_in': AB_SHARD_PAD, 'c_w_in': 256, 'ffn_w_up': FF_PAD}


def _block_index(name, p):
    d = 4 * p[0] + 2 * p[1] + p[2]
    if name in COL_WINDOW:
        return (slice(None), pl.ds(pl.multiple_of(d * COL_WINDOW[name], 128), COL_WINDOW[name]))
    if name == 'ffn_w_down':
        return (2 * p[0] + p[1], pl.ds(pl.multiple_of(p[2] * FF_ROWS, 16), FF_ROWS), slice(None))
    if name in ('c_gate_a_w', 'c_gate_x_w'):
        return (slice(None), d)
    return (d,)


def _block_of(name, ref, p, layered=True):
    idx = _block_index(name, p)
    if layered and name in BLOCK_LAYOUT:
        idx = (slice(None),) + idx
    return ref.at[idx]


def _layer_items(l):
    j = l // 2
    mix = ([('ab_w_in', j), ('ab_w_out', j)] if l % 2 == 0 else
           [('c_w_in', j), ('c_w_out', j), ('c_gate_a_w', j), ('c_gate_x_w', j)])
    return mix + [('ffn_w_up', l), ('ffn_w_down', l)]


def _own_land(name, shard_l, pos):
    x, y, c = pos
    d = 4 * x + 2 * y + c
    shape = BLOCK_LAYOUT[name][0][1:] if name in BLOCK_LAYOUT else (N_DEV,) + shard_l.shape
    zeros = jnp.zeros(shape, shard_l.dtype) if name == 'ffn_w_down' else lax.empty(shape, shard_l.dtype)
    if name in COL_WINDOW:
        return lax.dynamic_update_slice(zeros, shard_l, (0, d * COL_WINDOW[name]))
    if name == 'ffn_w_down':
        return lax.dynamic_update_slice(zeros, shard_l[None], (2 * x + y, c * FF_ROWS, 0))
    if name in ('c_gate_a_w', 'c_gate_x_w'):
        return lax.dynamic_update_slice(zeros, shard_l[:, None], (0, d, 0, 0))
    return lax.dynamic_update_slice(zeros, shard_l[None], (d,) + (0,) * shard_l.ndim)


def _src_of(shard_ref, li):
    return shard_ref if li is None else shard_ref.at[li]


def _gather_now(items, shards, lands):
    n = len(items)
    srcs = sorted({nm for nm, _ in items})

    def body(*refs):
        ins = dict(zip(srcs, refs[:len(srcs)]))
        outs = refs[len(srcs) + n:len(srcs) + 2 * n]
        send_sems, recv_sems = refs[len(srcs) + 2 * n:]
        x, y, c = _position()
        me, sibling = (x, y, c), (x, y, 1 - c)
        chips = [(1 - x, y), (x, 1 - y), (1 - x, 1 - y)]

        def copy(i, k, block, to, own=False):
            nm, li = items[i]
            dst = _block_of(nm, outs[i], block, layered=False)
            return pltpu.make_async_remote_copy(
                src_ref=_src_of(ins[nm], li) if own else dst, dst_ref=dst, send_sem=send_sems.at[7 * i + k],
                recv_sem=recv_sems.at[7 * i + k], device_id=to, device_id_type=MESH)

        first = []
        for i in range(n):
            first.append(copy(i, 0, me, sibling, own=True))
            first += [copy(i, 1 + j, me, (*chip, c), own=True) for j, chip in enumerate(chips)]
        for cp in first:
            cp.start()
        passed = []
        for j, chip in enumerate(chips):
            for i in range(n):
                copy(i, 1 + j, (*chip, c), me).wait_recv()
                fwd = copy(i, 4 + j, (*chip, c), sibling)
                fwd.start()
                passed.append(fwd)
        for i in range(n):
            copy(i, 0, sibling, me).wait_recv()
        for j, chip in enumerate(chips):
            for i in range(n):
                copy(i, 4 + j, (*chip, 1 - c), me).wait_recv()
        for cp in first + passed:
            cp.wait_send()

    any_spec = pl.BlockSpec(memory_space=pl.ANY)
    return pl.pallas_call(
        body, name="gather_first_layer", out_shape=[_sds(a.shape, a.dtype) for a in lands],
        in_specs=[any_spec] * (len(srcs) + n), out_specs=[any_spec] * n,
        input_output_aliases={len(srcs) + i: i for i in range(n)},
        scratch_shapes=[pltpu.SemaphoreType.DMA((7 * n,)), pltpu.SemaphoreType.DMA((7 * n,))],
    )(*[shards[nm] for nm in srcs], *lands)


FIRST_HOP = (1, 2, 4, 6)


def _lanes(name, land_ref, pos):
    if name == 'ffn_w_down':
        return [(FIRST_HOP, land_ref.at[pl.ds(0, 2), pl.ds(0, 2 * FF_ROWS)])]
    if name in COL_WINDOW:
        return [(FIRST_HOP, land_ref.at[:, pl.ds(0, 4 * COL_WINDOW[name])])]
    if name in ('c_gate_a_w', 'c_gate_x_w'):
        return [(FIRST_HOP, land_ref.at[:, pl.ds(0, 4)])]
    return [(FIRST_HOP, land_ref.at[pl.ds(0, 4)])]


def _n_lanes(items):
    return len(items)


def _gather_forward(items, lands, name):
    n = len(items)

    def body(*refs):
        outs = refs[n:2 * n]
        send_sems, recv_sems = refs[2 * n:]
        x, y, c = _position()
        chips = [(1 - x, y), (x, 1 - y), (1 - x, 1 - y)]
        copies, arrivals = [], []
        for i, (nm, _) in enumerate(items):
            for j, chip in enumerate(chips):
                mine = _block_of(nm, outs[i], (*chip, c), layered=False)
                theirs = _block_of(nm, outs[i], (*chip, 1 - c), layered=False)
                copies.append(pltpu.make_async_remote_copy(
                    src_ref=mine, dst_ref=mine, send_sem=send_sems.at[3 * i + j], recv_sem=recv_sems.at[3 * i + j],
                    device_id=(x, y, 1 - c), device_id_type=MESH))
                arrivals.append(pltpu.make_async_remote_copy(
                    src_ref=theirs, dst_ref=theirs, send_sem=send_sems.at[3 * i + j], recv_sem=recv_sems.at[3 * i + j],
                    device_id=(x, y, 1 - c), device_id_type=MESH))
        for cp in copies:
            cp.start()
        for cp in arrivals:
            cp.wait_recv()
        for cp in copies:
            cp.wait_send()

    any_spec = pl.BlockSpec(memory_space=pl.ANY)
    return pl.pallas_call(
        body, name=name, out_shape=[_sds(a.shape, a.dtype) for a in lands],
        in_specs=[any_spec] * n, out_specs=[any_spec] * n, input_output_aliases={i: i for i in range(n)},
        scratch_shapes=[pltpu.SemaphoreType.DMA((3 * n,)), pltpu.SemaphoreType.DMA((3 * n,))],
    )(*lands)


HBM_SPEC = pl.BlockSpec(memory_space=pltpu.HBM)
SEM_SPEC = pl.BlockSpec(memory_space=pltpu.SEMAPHORE)
SIDE_EFFECT = pltpu.SideEffectType.DATAFLOW_SIDE_EFFECTING


def _gather_start(items, shards, lands, token, name):
    n = len(items)
    srcs = sorted({nm for nm, _ in items})
    ns, nl = len(srcs), _n_lanes(items)

    def body(*refs):
        ins = dict(zip(srcs, refs[:ns]))
        land_refs = refs[ns:ns + n]
        sems = refs[ns + n + 1:ns + n + 1 + 2 * nl]
        x, y, c = _position()
        me = (x, y, c)
        lane = 0
        for i, (nm, li) in enumerate(items):
            for codes, _ in _lanes(nm, land_refs[i], me):
                for k in codes:
                    peer = (1 - x if (k >> 2) & 1 else x, 1 - y if (k >> 1) & 1 else y, 1 - c if k & 1 else c)
                    pltpu.make_async_remote_copy(
                        src_ref=_src_of(ins[nm], li), dst_ref=_block_of(nm, land_refs[i], me, layered=False),
                        send_sem=sems[2 * lane], recv_sem=sems[2 * lane + 1], device_id=peer, device_id_type=MESH).start()
                lane += 1

    hbm = [pltpu.with_memory_space_constraint(a, pltpu.HBM) for a in [shards[nm] for nm in srcs] + list(lands)]
    outs = pl.pallas_call(
        body, name=name,
        out_shape=[pltpu.SemaphoreType.DMA(())] * (2 * nl) + [pltpu.HBM(a.shape, a.dtype) for a in hbm],
        in_specs=[HBM_SPEC] * (ns + n) + [pl.BlockSpec(memory_space=pl.ANY)],
        out_specs=[SEM_SPEC] * (2 * nl) + [HBM_SPEC] * (ns + n),
        input_output_aliases={i: 2 * nl + i for i in range(ns + n)},
        compiler_params=pltpu.CompilerParams(has_side_effects=SIDE_EFFECT),
    )(*hbm, token)
    return outs[:2 * nl], dict(zip(srcs, outs[2 * nl:2 * nl + ns])), outs[2 * nl + ns:]


def _gather_wait(items, sems, shards, lands, after, name):
    n = len(items)
    srcs = sorted(shards)
    ns, nl = len(srcs), _n_lanes(items)

    def body(*refs):
        land_refs = refs[ns:ns + n]
        sem_refs = refs[ns + n:ns + n + 2 * nl]
        x, y, c = _position()
        lane = 0
        for i, (nm, _) in enumerate(items):
            for _, moved in _lanes(nm, land_refs[i], (x, y, c)):
                cp = pltpu.make_async_remote_copy(
                    src_ref=moved, dst_ref=moved, send_sem=sem_refs[2 * lane], recv_sem=sem_refs[2 * lane + 1],
                    device_id=(x, y, 1 - c), device_id_type=MESH)
                cp.wait_send()
                cp.wait_recv()
                lane += 1

    outs = pl.pallas_call(
        body, name=name, out_shape=[pltpu.HBM(shards[nm].shape, shards[nm].dtype) for nm in srcs]
        + [pltpu.HBM(a.shape, a.dtype) for a in lands],
        in_specs=[HBM_SPEC] * (ns + n) + [SEM_SPEC] * (2 * nl) + [pl.BlockSpec(memory_space=pl.ANY)],
        out_specs=[HBM_SPEC] * (ns + n), input_output_aliases={i: i for i in range(ns + n)},
        compiler_params=pltpu.CompilerParams(has_side_effects=SIDE_EFFECT),
    )(*[shards[nm] for nm in srcs], *lands, *sems, after)
    return dict(zip(srcs, outs[:ns])), outs[ns:]


def _exchange_grads(fulls, rep):
    cpos = lax.axis_index("c").astype(jnp.int32).reshape(1)
    pair, rep_pair = _pair_exchange(fulls, rep)
    chip = {nm: _chip_sum(nm, fulls[nm], pair[nm], cpos) for nm in fulls}
    rep_chip = _add_pair(rep, rep_pair, "chip_sum_replicated")
    cross, cross_rep = _cross_exchange(chip, rep_chip)
    return chip, rep_chip, cross, cross_rep


def _pair_exchange(fulls, rep, tag=""):
    names = list(fulls)
    n = len(names)
    shard_shape = {nm: ((fulls[nm].shape[0],) + BLOCK_LAYOUT[nm][1][1:] if nm in BLOCK_LAYOUT else fulls[nm].shape[1:])
                   for nm in names}

    def body(*refs):
        ins = dict(zip(names, refs[:n]))
        rep_ref = refs[n]
        pair = dict(zip(names, refs[n + 1:2 * n + 1]))
        rpair_ref = refs[2 * n + 1]
        send_sems, recv_sems = refs[2 * n + 2:]
        x, y, c = _position()
        sibling = (x, y, 1 - c)
        remote = []
        for i, nm in enumerate(names):
            for q in range(4):
                remote.append(pltpu.make_async_remote_copy(
                    src_ref=_block_of(nm, ins[nm], (q >> 1, q & 1, 1 - c)), dst_ref=pair[nm].at[q],
                    send_sem=send_sems.at[4 * i + q], recv_sem=recv_sems.at[4 * i + q], device_id=sibling,
                    device_id_type=MESH))
        remote.append(pltpu.make_async_remote_copy(
            src_ref=rep_ref, dst_ref=rpair_ref, send_sem=send_sems.at[4 * n], recv_sem=recv_sems.at[4 * n],
            device_id=sibling, device_id_type=MESH))
        for cp in remote:
            cp.start()
        for cp in remote:
            cp.wait_recv()
        for cp in remote:
            cp.wait_send()

    any_spec = pl.BlockSpec(memory_space=pl.ANY)
    four = [_sds((4,) + tuple(shard_shape[nm]), fulls[nm].dtype) for nm in names]
    outs = pl.pallas_call(
        body, name="grad_pair_exchange" + tag, out_shape=four + [_sds(rep.shape, rep.dtype)],
        in_specs=[any_spec] * (n + 1), out_specs=[any_spec] * (n + 1),
        scratch_shapes=[pltpu.SemaphoreType.DMA((4 * n + 1,)), pltpu.SemaphoreType.DMA((4 * n + 1,))],
    )(*[fulls[nm] for nm in names], rep)
    return dict(zip(names, outs[:n])), outs[n]


def _chip_sum(name, full, pair, cpos, tag=""):
    if name in ('ab_w_in', 'c_w_in', 'ffn_w_up'):
        width = BLOCK_LAYOUT[name][1][-1]
        rows = full.shape[0] * full.shape[1]
        tr = 512

        def body(c_ref, f_ref, p_ref, o_ref):
            o_ref[0] = (f_ref[...].astype(f32) + p_ref[0].astype(f32)).astype(o_ref.dtype)

        slot = pl.BlockSpec((1, tr, width), lambda q, i, c: (q, i, 0))
        out = pl.pallas_call(
            body, name="chip_sum_" + name + tag, out_shape=_sds((4, rows, width), full.dtype),
            grid_spec=pltpu.PrefetchScalarGridSpec(
                num_scalar_prefetch=1, grid=(4, rows // tr),
                in_specs=[pl.BlockSpec((tr, width), lambda q, i, c: (i, 2 * q + c[0])), slot], out_specs=slot),
            compiler_params=_cparams(("parallel", "parallel")))(
            cpos, full.reshape(rows, N_DEV * width), pair.reshape(4, rows, width))
        return out.reshape(pair.shape)

    if name == 'ffn_w_down':
        f4, p4 = full, pair
        fspec = pl.BlockSpec((full.shape[0], 1, FF_ROWS, D_MODEL), lambda q, c: (0, q, c[0], 0))
    else:
        shard = pair.shape[1:]
        lead = int(np.prod(shard[:-2]))
        f4 = full.reshape((lead, N_DEV) + shard[-2:])
        p4 = pair.reshape((4, lead) + shard[-2:])
        fspec = pl.BlockSpec((lead, 1) + shard[-2:], lambda q, c: (0, 2 * q + c[0], 0, 0))

    def body4(c_ref, f_ref, p_ref, o_ref):
        o_ref[0] = (f_ref[:, 0].astype(f32) + p_ref[0].astype(f32)).astype(o_ref.dtype)

    slot = pl.BlockSpec((1,) + p4.shape[1:], lambda q, c: (q, 0, 0, 0))
    out = pl.pallas_call(
        body4, name="chip_sum_" + name + tag, out_shape=_sds(p4.shape, full.dtype),
        grid_spec=pltpu.PrefetchScalarGridSpec(num_scalar_prefetch=1, grid=(4,), in_specs=[fspec, slot], out_specs=slot),
        compiler_params=_cparams(("parallel",)))(cpos, f4, p4)
    return out.reshape(pair.shape)


def _add_pair(a, b, name):
    shp = a.shape
    r, c = int(np.prod(shp[:-1])), shp[-1]
    tr = _tile(r, (512, 256, 128, 64, 32, 16, 8))

    def body(a_ref, b_ref, o_ref):
        o_ref[...] = (a_ref[...].astype(f32) + b_ref[...].astype(f32)).astype(o_ref.dtype)

    tile = pl.BlockSpec((tr, c), lambda i: (i, 0))
    return pl.pallas_call(body, name=name, grid=(r // tr,), in_specs=[tile, tile], out_specs=tile,
                          out_shape=_sds((r, c), a.dtype), compiler_params=_cparams(("parallel",)))(
        a.reshape(r, c), b.reshape(r, c)).reshape(shp)


def _cross_exchange(chip, rep_chip):
    names = list(chip)
    n = len(names)

    def body(*refs):
        ins = dict(zip(names, refs[:n]))
        rep_ref = refs[n]
        outs = dict(zip(names, refs[2 * n + 2:3 * n + 2]))
        rrep_ref = refs[3 * n + 2]
        send_sems, recv_sems = refs[3 * n + 3:]
        x, y, c = _position()
        mine = 2 * x + y
        copies = []
        for k in range(1, 4):
            px, py = (1 - x if (k >> 1) & 1 else x), (1 - y if k & 1 else y)
            for i, nm in enumerate(names + ['']):
                src = rep_ref if i == n else ins[nm].at[2 * px + py]
                dst = (rrep_ref if i == n else outs[nm]).at[mine]
                copies.append(pltpu.make_async_remote_copy(
                    src_ref=src, dst_ref=dst, send_sem=send_sems.at[3 * i + k - 1], recv_sem=recv_sems.at[3 * i + k - 1],
                    device_id=(px, py, c), device_id_type=MESH))
        for cp in copies:
            cp.start()
        for cp in copies:
            cp.wait_recv()
        for cp in copies:
            cp.wait_send()

    any_spec = pl.BlockSpec(memory_space=pl.ANY)
    shapes = [_sds(chip[nm].shape, chip[nm].dtype) for nm in names] + [_sds((4,) + rep_chip.shape, rep_chip.dtype)]
    zeros = [jnp.zeros(s.shape, s.dtype) for s in shapes]
    outs = pl.pallas_call(
        body, name="grad_cross_exchange", out_shape=shapes,
        in_specs=[any_spec] * (2 * n + 2), out_specs=[any_spec] * (n + 1),
        input_output_aliases={n + 1 + i: i for i in range(n + 1)},
        scratch_shapes=[pltpu.SemaphoreType.DMA((3 * (n + 1),)), pltpu.SemaphoreType.DMA((3 * (n + 1),))],
    )(*[chip[nm] for nm in names], rep_chip, *zeros)
    return dict(zip(names, outs[:n])), outs[n]


def _cross_start(chips, name):
    n = len(chips)

    def body(*refs):
        chip_refs, land_refs = refs[:n], refs[n:2 * n]
        sems = refs[2 * n:4 * n]
        x, y, c = _position()
        mine = 2 * x + y
        for i in range(n):
            for k in range(1, 4):
                px, py = (1 - x if (k >> 1) & 1 else x), (1 - y if k & 1 else y)
                pltpu.make_async_remote_copy(
                    src_ref=chip_refs[i].at[2 * px + py], dst_ref=land_refs[i].at[mine], send_sem=sems[2 * i],
                    recv_sem=sems[2 * i + 1], device_id=(px, py, c), device_id_type=MESH).start()

    hbm = [pltpu.with_memory_space_constraint(a, pltpu.HBM) for a in list(chips) + [jnp.zeros(a.shape, a.dtype) for a in chips]]
    outs = pl.pallas_call(
        body, name=name, out_shape=[pltpu.SemaphoreType.DMA(())] * (2 * n) + [pltpu.HBM(a.shape, a.dtype) for a in hbm],
        in_specs=[HBM_SPEC] * (2 * n), out_specs=[SEM_SPEC] * (2 * n) + [HBM_SPEC] * (2 * n),
        input_output_aliases={i: 2 * n + i for i in range(2 * n)},
        compiler_params=pltpu.CompilerParams(has_side_effects=SIDE_EFFECT),
    )(*hbm)
    return outs[:2 * n], outs[2 * n:3 * n], outs[3 * n:]


def _cross_wait(sems, chips, lands, after, name):
    n = len(chips)

    def body(*refs):
        land_refs = refs[n:2 * n]
        sem_refs = refs[2 * n:4 * n]
        x, y, c = _position()
        for i in range(n):
            moved = land_refs[i].at[pl.ds(0, 3)]
            cp = pltpu.make_async_remote_copy(
                src_ref=moved, dst_ref=moved, send_sem=sem_refs[2 * i], recv_sem=sem_refs[2 * i + 1],
                device_id=(x, y, 1 - c), device_id_type=MESH)
            cp.wait_send()
            cp.wait_recv()

    outs = pl.pallas_call(
        body, name=name, out_shape=[pltpu.HBM(a.shape, a.dtype) for a in list(chips) + list(lands)],
        in_specs=[HBM_SPEC] * (2 * n) + [SEM_SPEC] * (2 * n) + [pl.BlockSpec(memory_space=pl.ANY)],
        out_specs=[HBM_SPEC] * (2 * n), input_output_aliases={i: i for i in range(2 * n)},
        compiler_params=pltpu.CompilerParams(has_side_effects=SIDE_EFFECT),
    )(*chips, *lands, *sems, after)
    return outs[:n], outs[n:]


def _sum_adamw_layer(parts, own, mine, w, m, v, li, prev, name):
    nl, r, l = w.shape
    lp = parts.shape[2]
    tr = r if r <= 512 else _tile(r, (512, 256, 128))
    c1 = 1.0 / (1.0 - ADAM_B1 ** ADAM_STEP)
    c2 = 1.0 / (1.0 - ADAM_B2 ** ADAM_STEP)
    k = 0 if prev is None else 4

    def body(mine_ref, p_ref, o_ref, w_ref, m_ref, v_ref, *rest):
        g_ref, d_ref, nm_ref, nv_ref = rest[k:]
        mine_v = o_ref[0].astype(f32)
        g = jnp.where(mine_ref[0] == 0, mine_v, p_ref[0].astype(f32))
        for s in range(1, 4):
            g = g + jnp.where(mine_ref[0] == s, mine_v, p_ref[s].astype(f32))
        if lp != l:
            g = g[:, :l]
        m_new = ADAM_B1 * m_ref[0] + (1.0 - ADAM_B1) * g
        v_new = ADAM_B2 * v_ref[0] + (1.0 - ADAM_B2) * (g * g)
        g_ref[0] = g
        nm_ref[0] = m_new
        nv_ref[0] = v_new
        d_ref[0] = -ADAM_LR * ((m_new * c1) / (jnp.sqrt(v_new * c2) + ADAM_EPS) + ADAM_WD * w_ref[0])

    tile = pl.BlockSpec((1, tr, l), lambda i, mn: (li, i, 0))
    keep = [pl.BlockSpec(memory_space=pl.ANY)] * k
    return pl.pallas_call(
        body, name=name, out_shape=[_sds((nl, r, l), f32)] * 4,
        grid_spec=pltpu.PrefetchScalarGridSpec(
            num_scalar_prefetch=1, grid=(r // tr,),
            in_specs=[pl.BlockSpec((4, tr, lp), lambda i, mn: (0, i, 0)), pl.BlockSpec((1, tr, lp), lambda i, mn: (mn[0], i, 0)),
                      tile, tile, tile] + keep,
            out_specs=[tile] * 4),
        input_output_aliases={6 + i: i for i in range(k)},
        compiler_params=_cparams(("parallel",)))(mine, parts, own, w, m, v, *(prev or ()))


def _sum_adamw(parts, own, mine, w, m, v, name):
    r, l = w.shape
    lp = parts.shape[2]
    tr = _tile(r, (256, 128, 64, 32, 16, 8))
    c1 = 1.0 / (1.0 - ADAM_B1 ** ADAM_STEP)
    c2 = 1.0 / (1.0 - ADAM_B2 ** ADAM_STEP)

    def body(mine_ref, p_ref, o_ref, w_ref, m_ref, v_ref, g_ref, d_ref, nm_ref, nv_ref):
        mine_v = (o_ref[0] if own.ndim == 3 else o_ref[...]).astype(f32)
        g = jnp.where(mine_ref[0] == 0, mine_v, p_ref[0].astype(f32))
        for s in range(1, parts.shape[0]):
            g = g + jnp.where(mine_ref[0] == s, mine_v, p_ref[s].astype(f32))
        if lp != l:
            g = g[:, :l]
        m_new = ADAM_B1 * m_ref[...] + (1.0 - ADAM_B1) * g
        v_new = ADAM_B2 * v_ref[...] + (1.0 - ADAM_B2) * (g * g)
        g_ref[...] = g
        nm_ref[...] = m_new
        nv_ref[...] = v_new
        d_ref[...] = -ADAM_LR * ((m_new * c1) / (jnp.sqrt(v_new * c2) + ADAM_EPS) + ADAM_WD * w_ref[...])

    tile = pl.BlockSpec((tr, l), lambda i, mn: (i, 0))
    own_spec = (pl.BlockSpec((1, tr, lp), lambda i, mn: (mn[0], i, 0)) if own.ndim == 3
                else pl.BlockSpec((tr, lp), lambda i, mn: (i, 0)))
    return pl.pallas_call(
        body, name=name, out_shape=[_sds((r, l), f32)] * 4,
        grid_spec=pltpu.PrefetchScalarGridSpec(
            num_scalar_prefetch=1, grid=(r // tr,),
            in_specs=[pl.BlockSpec((parts.shape[0], tr, lp), lambda i, mn: (0, i, 0)), own_spec, tile, tile, tile],
            out_specs=[tile] * 4),
        compiler_params=_cparams(("parallel",)))(mine, parts, own, w, m, v)


def _pack(arrs, lead=None):
    if lead is None:
        flat = jnp.concatenate([a.reshape(-1).astype(f32) for a in arrs])
        n = flat.shape[0]
    else:
        flat = jnp.concatenate([a.reshape(lead, -1).astype(f32) for a in arrs], axis=1)
        n = flat.shape[1]
    tot = -(-n // 1024) * 1024
    if lead is None:
        return jnp.pad(flat, (0, tot - n)).reshape(tot // 128, 128)
    return jnp.pad(flat, ((0, 0), (0, tot - n))).reshape(lead, tot // 128, 128)


def _unpack(packed, shapes, lead=False):
    flat = packed.reshape(packed.shape[0], -1) if lead else packed.reshape(-1)
    out, off = [], 0
    for s in shapes:
        n = int(np.prod(s))
        out.append(flat[:, off:off + n].reshape((packed.shape[0],) + tuple(s)) if lead else flat[off:off + n].reshape(s))
        off += n
    return out


def _merge_shards(g, axis):
    g = jnp.moveaxis(g, 0, axis)
    s = g.shape
    return g.reshape(s[:axis] + (s[axis] * s[axis + 1],) + s[axis + 2:])


def _split_shards(full, axis):
    s = full.shape
    g = full.reshape(s[:axis] + (N_DEV, s[axis] // N_DEV) + s[axis + 1:])
    return jnp.moveaxis(g, axis, 0)


def kernel(x, norm_mix, norm_ffn, norm_final, ab_w_in, gdn_conv_w, gdn_a_log, gdn_dt_bias, gdn_norm, hgrn_lower_bounds, hgrn_norm, ab_w_out, c_w_in, c_conv_w, c_conv_b, c_gate_a_w, c_gate_a_b, c_gate_x_w, c_gate_x_b, c_lambda, c_w_out, ffn_w_up, ffn_conv_w, ffn_conv_b, ffn_w_down, loss_target, m_norm_mix, m_norm_ffn, m_norm_final, m_ab_w_in, m_gdn_conv_w, m_gdn_a_log, m_gdn_dt_bias, m_gdn_norm, m_hgrn_lower_bounds, m_hgrn_norm, m_ab_w_out, m_c_w_in, m_c_conv_w, m_c_conv_b, m_c_gate_a_w, m_c_gate_a_b, m_c_gate_x_w, m_c_gate_x_b, m_c_lambda, m_c_w_out, m_ffn_w_up, m_ffn_conv_w, m_ffn_conv_b, m_ffn_w_down, v_norm_mix, v_norm_ffn, v_norm_final, v_ab_w_in, v_gdn_conv_w, v_gdn_a_log, v_gdn_dt_bias, v_gdn_norm, v_hgrn_lower_bounds, v_hgrn_norm, v_ab_w_out, v_c_w_in, v_c_conv_w, v_c_conv_b, v_c_gate_a_w, v_c_gate_a_b, v_c_gate_x_w, v_c_gate_x_b, v_c_lambda, v_c_w_out, v_ffn_w_up, v_ffn_conv_w, v_ffn_conv_b, v_ffn_w_down):
    wl = dict(zip(WEIGHTS, (norm_mix, norm_ffn, norm_final, ab_w_in, gdn_conv_w, gdn_a_log, gdn_dt_bias, gdn_norm, hgrn_lower_bounds, hgrn_norm, ab_w_out, c_w_in, c_conv_w, c_conv_b, c_gate_a_w, c_gate_a_b, c_gate_x_w, c_gate_x_b, c_lambda, c_w_out, ffn_w_up, ffn_conv_w, ffn_conv_b, ffn_w_down)))
    ml = dict(zip(WEIGHTS, (m_norm_mix, m_norm_ffn, m_norm_final, m_ab_w_in, m_gdn_conv_w, m_gdn_a_log, m_gdn_dt_bias, m_gdn_norm, m_hgrn_lower_bounds, m_hgrn_norm, m_ab_w_out, m_c_w_in, m_c_conv_w, m_c_conv_b, m_c_gate_a_w, m_c_gate_a_b, m_c_gate_x_w, m_c_gate_x_b, m_c_lambda, m_c_w_out, m_ffn_w_up, m_ffn_conv_w, m_ffn_conv_b, m_ffn_w_down)))
    vl = dict(zip(WEIGHTS, (v_norm_mix, v_norm_ffn, v_norm_final, v_ab_w_in, v_gdn_conv_w, v_gdn_a_log, v_gdn_dt_bias, v_gdn_norm, v_hgrn_lower_bounds, v_hgrn_norm, v_ab_w_out, v_c_w_in, v_c_conv_w, v_c_conv_b, v_c_gate_a_w, v_c_gate_a_b, v_c_gate_x_w, v_c_gate_x_b, v_c_lambda, v_c_w_out, v_ffn_w_up, v_ffn_conv_w, v_ffn_conv_b, v_ffn_w_down)))

    big = [n for n in SHARDED if n in MATMUL_WEIGHTS]
    vec = [n for n in SHARDED if n not in MATMUL_WEIGHTS]
    shards = {n: wl[n].astype(bf16) for n in big}
    shards['ab_w_in'] = jnp.pad(shards['ab_w_in'], ((0, 0), (0, 0), (0, AB_SHARD_PAD - AB_SHARD)))
    shards['ffn_w_up'] = jnp.pad(shards['ffn_w_up'], ((0, 0), (0, 0), (0, FF_PAD - FF_SHARD)))
    shards['vec'] = _pack([wl[n] for n in vec])
    pos = _position()
    layer_items = [_layer_items(l) for l in range(DEPTH)]
    lands = [[_own_land(nm, shards[nm][li], pos) for nm, li in items] for items in layer_items]
    first = _gather_now(layer_items[0] + [('vec', None)], shards, lands[0] + [_own_land('vec', shards['vec'], pos)])
    flight = {'shards': {n: shards[n] for n in big}}

    def start(l, token):
        sems, thru, flight['lands'] = _gather_start(layer_items[l], flight['shards'], lands[l], token, "gather_start_%d" % l)
        flight['sems'] = list(sems)
        flight['shards'].update(thru)

    start(1, first[-1])

    full = {n: wl[n] for n in REPLICATED}
    for n, a in zip(vec, _unpack(first[-1], [wl[n].shape for n in vec], lead=True)):
        full[n] = _merge_shards(a, SHARD_AXIS[n])
    full['ffn_conv_w'] = _block_pad(full['ffn_conv_w'], 2, 4, FF_PAD)
    full['ffn_conv_b'] = _block_pad(full['ffn_conv_b'], 1, 4, FF_PAD)
    for n in big:
        full[n] = {}

    def fetch(l, x_in):
        items = layer_items[l]
        if l == 0:
            got = first[:len(items)]
        else:
            flight['shards'], got = _gather_wait(items, flight['sems'], flight['shards'], flight['lands'], x_in,
                                                 "gather_wait_%d" % l)
            got = _gather_forward(items, got, "gather_forward_%d" % l)
            if l + 1 < DEPTH:
                start(l + 1, got[0])
        for (nm, li), a in zip(items, got):
            if nm == 'ab_w_in':
                a = _ab_permute(_block_unpad(a, 1, N_DEV, AB_SHARD))
            elif nm in ('ab_w_out', 'c_w_out'):
                a = a.reshape(D_MODEL, D_MODEL)
            elif nm in ('c_gate_a_w', 'c_gate_x_w'):
                a = a.reshape(HEADS, LRU_BLOCK, LRU_BLOCK)
            elif nm == 'ffn_w_down':
                a = a.reshape(D_FFP, D_MODEL)
            full[nm][li] = a

    cpos = lax.axis_index("c").astype(jnp.int32).reshape(1)
    mine = (2 * lax.axis_index("x") + lax.axis_index("y")).astype(jnp.int32).reshape(1)
    pending = {}

    def push(l, g):
        fulls = {}
        for nm, _ in layer_items[l]:
            a = g[nm].astype(bf16)
            if nm == 'ab_w_in':
                a = _block_pad(_ab_unpermute(a), 1, N_DEV, AB_SHARD_PAD)
            fulls[nm] = a.reshape((1,) + BLOCK_LAYOUT[nm][0][1:])
        pair, _ = _pair_exchange(fulls, jnp.zeros((8, 128), f32), "_%d" % l)
        chips = [_chip_sum(nm, fulls[nm], pair[nm], cpos, "_%d" % l) for nm in fulls]
        pending[l] = _cross_start(chips, "grad_cross_start_%d" % l)

    loss, dx, grads = _local_step(x[0], loss_target[0], full, fetch, push)

    grads['ffn_conv_w'] = _block_unpad(grads['ffn_conv_w'], 2, 4, FF_SHARD)
    grads['ffn_conv_b'] = _block_unpad(grads['ffn_conv_b'], 1, 4, FF_SHARD)
    fulls = {'vec': _pack([_split_shards(grads[n], SHARD_AXIS[n]) for n in vec], lead=N_DEV)}
    chip, rep_chip, recv, rrep = _exchange_grads(fulls, _pack([grads[n] for n in REPLICATED]))
    res = {}
    stacked = {}
    after = dx
    for l in (3, 2, 1, 0):
        sems, chips, lands = pending[l]
        chips, lands = _cross_wait(sems, chips, lands, after, "grad_cross_wait_%d" % l)
        for (n, li), own, parts in zip(layer_items[l], chips, lands):
            shp = wl[n].shape
            nl, c = shp[0], shp[-1]
            r = int(np.prod(shp[1:-1]))
            stacked[n] = _sum_adamw_layer(parts.reshape(4, r, -1), own.reshape(4, r, -1), mine, wl[n].reshape(nl, r, c),
                                          ml[n].reshape(nl, r, c), vl[n].reshape(nl, r, c), li, stacked.get(n),
                                          "adamw_%s_%d" % (n, li))
        if l == 1:
            after = stacked['ffn_w_up'][0]
    for n in big:
        for kind, o in zip(("grad", "delta", "new_m", "new_v"), stacked[n]):
            res[kind, n] = o.reshape(wl[n].shape)
    for names, parts, own, tag in ((vec, recv['vec'], chip['vec'], "adamw_vectors"),
                                   (REPLICATED, rrep, rep_chip, "adamw_replicated")):
        outs = _sum_adamw(parts, own, mine, _pack([wl[n] for n in names]), _pack([ml[n] for n in names]),
                          _pack([vl[n] for n in names]), tag)
        for kind, o in zip(("grad", "delta", "new_m", "new_v"), outs):
            for n, a in zip(names, _unpack(o, [wl[n].shape for n in names])):
                res[kind, n] = a

    loss = lax.psum(loss, ("x", "y", "c"))
    return (loss, dx[None], *[res[kind, n] for kind in ("grad", "delta", "new_m", "new_v") for n in WEIGHTS])
```

```python
import functools

import numpy as np
import jax
import jax.numpy as jnp
from jax import lax
from jax.experimental import pallas as pl
from jax.experimental.pallas import tpu as pltpu

f32 = jnp.float32
bf16 = jnp.bfloat16
HI = lax.Precision.HIGHEST
MESH = pl.DeviceIdType.MESH

N_DEV = 8
D_MODEL = 1024
DEPTH = 4
EPS = 1e-6
F_FLOOR = 1e-30
HEADS = 4
HEAD_DIM = 128
GDN_WIDTH = 512
GDN_CONV = 4
GDN_CHUNK = 64
HGRN_CHUNK = 16
HGRN_STEP = 128
MIX_WIDTH = 1024
AB_COLS = 4104
AB_PAD = 4224
LRU_WIDTH = 1024
LRU_BLOCK = 256
LRU_CONV = 4
RG_C = 8.0
D_FF = 2816
FF_SHARD = 704
FF_PAD = 768
D_FFP = 4 * FF_PAD
FF_ROWS = 352
AB_SHARD, AB_SHARD_PAD = 513, 640
FFN_CONV = 3
ADAM_LR, ADAM_B1, ADAM_B2, ADAM_EPS, ADAM_WD, ADAM_STEP = 0.001, 0.9, 0.999, 1e-08, 0.01, 10
VMEM_LIMIT = 56 * 1024 * 1024
PACK_LANES = 512
PACK_ROWS = 256

OFF_Q, OFF_K, OFF_V, OFF_Z, OFF_QB, OFF_FB, OFF_IB, OFF_GB, OFF_BA = 0, 512, 1024, 1536, 2048, 2560, 3072, 3584, 4096

WEIGHTS = ['norm_mix', 'norm_ffn', 'norm_final', 'ab_w_in', 'gdn_conv_w', 'gdn_a_log', 'gdn_dt_bias', 'gdn_norm',
           'hgrn_lower_bounds', 'hgrn_norm', 'ab_w_out', 'c_w_in', 'c_conv_w', 'c_conv_b', 'c_gate_a_w', 'c_gate_a_b',
           'c_gate_x_w', 'c_gate_x_b', 'c_lambda', 'c_w_out', 'ffn_w_up', 'ffn_conv_w', 'ffn_conv_b', 'ffn_w_down']
SHARD_AXIS = {'norm_mix': None, 'norm_ffn': None, 'norm_final': None, 'ab_w_in': 2, 'gdn_conv_w': 2, 'gdn_a_log': None,
              'gdn_dt_bias': None, 'gdn_norm': None, 'hgrn_lower_bounds': None, 'hgrn_norm': None, 'ab_w_out': 1,
              'c_w_in': 2, 'c_conv_w': 2, 'c_conv_b': 1, 'c_gate_a_w': 2, 'c_gate_a_b': 1, 'c_gate_x_w': 2,
              'c_gate_x_b': 1, 'c_lambda': 1, 'c_w_out': 1, 'ffn_w_up': 2, 'ffn_conv_w': 2, 'ffn_conv_b': None,
              'ffn_w_down': 1}
MATMUL_WEIGHTS = ('ab_w_in', 'ab_w_out', 'c_w_in', 'c_gate_a_w', 'c_gate_x_w', 'c_w_out', 'ffn_w_up', 'ffn_w_down')
SHARDED = [n for n in WEIGHTS if SHARD_AXIS[n] is not None]
REPLICATED = [n for n in WEIGHTS if SHARD_AXIS[n] is None]


def _tile(n, prefs=(512, 384, 256, 128)):
    for p in prefs:
        if n % p == 0:
            return p
    return n


def _cparams(sem=None):
    kw = dict(vmem_limit_bytes=VMEM_LIMIT)
    if sem is not None:
        kw['dimension_semantics'] = sem
    return pltpu.CompilerParams(**kw)


def _sds(shape, dtype):
    return jax.ShapeDtypeStruct(tuple(shape), dtype)


def _sigmoid(x):
    return 1.0 / (1.0 + jnp.exp(-x))


def _silu(x):
    return x * _sigmoid(x)


def _log1p(x):
    u = 1.0 + x
    return jnp.where(u == 1.0, x, jnp.log(u) * (x / jnp.where(u == 1.0, 1.0, u - 1.0)))


def _softplus(x):
    return jnp.maximum(x, 0.0) + _log1p(jnp.exp(-jnp.abs(x)))


def _expm1(x):
    small = jnp.abs(x) < 0.05
    xs = jnp.where(small, x, 0.0)
    series = xs * (1.0 + xs * (0.5 + xs * (1.0 / 6.0 + xs * (1.0 / 24.0 + xs * (1.0 / 120.0)))))
    return jnp.where(small, series, jnp.exp(x) - 1.0)


def _gelu(x):
    return 0.5 * x * (1.0 + jnp.tanh(0.7978845608028654 * (x + 0.044715 * x * x * x)))


def _rms(x, gain):
    return x * lax.rsqrt(jnp.mean(x * x, axis=-1, keepdims=True) + EPS) * gain


def _dot(a, b, dims=((1,), (0,)), precision=None):
    return lax.dot_general(a, b, (dims, ((), ())), precision=precision, preferred_element_type=f32)


def _bdot(a, b, dims=((1,), (0,))):
    return _dot(a.astype(bf16), b.astype(bf16), dims)


NT = ((1,), (1,))
TN = ((0,), (0,))


def _shift_down(x, k):
    if k == 0:
        return x
    row = lax.broadcasted_iota(jnp.int32, x.shape, 0)
    return jnp.where(row >= k, pltpu.roll(x, k, 0), 0.0)


def _shift_up(x, k, fill=0.0):
    if k == 0:
        return x
    n = x.shape[0]
    row = lax.broadcasted_iota(jnp.int32, x.shape, 0)
    return jnp.where(row < n - k, pltpu.roll(x, n - k, 0), fill)


def _conv_fwd(x, w_ref, width):
    acc = w_ref[width - 1:width, :] * x
    for k in range(width - 1):
        acc = acc + w_ref[k:k + 1, :] * _shift_down(x, width - 1 - k)
    return acc


def _conv_bwd(x, dout, w_ref, dw_ref, width):
    dx = w_ref[width - 1:width, :] * dout
    dw_ref[width - 1:width, :] = jnp.sum(dout * x, axis=0, keepdims=True)
    for k in range(width - 1):
        s = width - 1 - k
        dx = dx + w_ref[k:k + 1, :] * _shift_up(dout, s)
        dw_ref[k:k + 1, :] = jnp.sum(dout * _shift_down(x, s), axis=0, keepdims=True)
    return dx


MM_VMEM_BUDGET = 36 * 1024 * 1024
MM_MAX_TILE = 1024 * 1024


def _mm_tiles(m, n, k, out_bytes):
    best = None
    for tm in (1024, 512, 384, 256, 128):
        if m % tm:
            continue
        for tn in range(1536, 0, -128):
            if n % tn or tm * tn > MM_MAX_TILE:
                continue
            score = (tm * tn, min(tm, tn))
            if 2 * (tm * k * 2 + k * tn * 2 + tm * tn * out_bytes) <= MM_VMEM_BUDGET and (best is None or score > best[0]):
                best = (score, tm, tn)
    return (best[1], best[2]) if best else (_tile(m), _tile(n))


def _mm(a, b, *, ta=False, tb=False, add=None, out_dtype=f32, name):
    m, k = (a.shape[1], a.shape[0]) if ta else a.shape
    n = b.shape[0] if tb else b.shape[1]
    tm, tn = _mm_tiles(m, n, k, jnp.dtype(out_dtype).itemsize + (4 if add is not None else 0))
    dims = ((0 if ta else 1,), (1 if tb else 0,))

    def body(*refs):
        a_ref, b_ref = refs[0], refs[1]
        o_ref = refs[-1]
        r = _dot(a_ref[...], b_ref[...], dims)
        if add is not None:
            r = r + refs[2][...]
        o_ref[...] = r.astype(out_dtype)

    a_spec = pl.BlockSpec((k, tm), lambda j, i: (0, i)) if ta else pl.BlockSpec((tm, k), lambda j, i: (i, 0))
    b_spec = pl.BlockSpec((tn, k), lambda j, i: (j, 0)) if tb else pl.BlockSpec((k, tn), lambda j, i: (0, j))
    o_spec = pl.BlockSpec((tm, tn), lambda j, i: (i, j))
    ins, specs = [a, b], [a_spec, b_spec]
    if add is not None:
        ins.append(add)
        specs.append(o_spec)
    return pl.pallas_call(body, name=name, grid=(n // tn, m // tm), in_specs=specs, out_specs=o_spec,
                          out_shape=_sds((m, n), out_dtype), compiler_params=_cparams(("parallel", "parallel")))(*ins)


def _rms_fwd(x, gain, name):
    t, d = x.shape
    tr = _tile(t, (256, 128))

    def body(x_ref, g_ref, h_ref):
        h_ref[...] = _rms(x_ref[...], g_ref[...]).astype(bf16)

    return pl.pallas_call(body, name=name, grid=(t // tr,),
                          in_specs=[pl.BlockSpec((tr, d), lambda i: (i, 0)), pl.BlockSpec((1, d), lambda i: (0, 0))],
                          out_specs=pl.BlockSpec((tr, d), lambda i: (i, 0)), out_shape=_sds((t, d), bf16),
                          compiler_params=_cparams(("parallel",)))(x, gain)


def _rms_bwd(x, gain, dh, dres, name):
    t, d = x.shape
    tr = _tile(t, (256, 128))

    def body(x_ref, g_ref, dh_ref, dres_ref, dx_ref, dxb_ref, dg_ref):
        _, vjp = jax.vjp(_rms, x_ref[...], g_ref[...])
        dx, dg = vjp(dh_ref[...])
        dx = dx + dres_ref[...]
        dx_ref[...] = dx
        dxb_ref[...] = dx.astype(bf16)

        @pl.when(pl.program_id(0) == 0)
        def _():
            dg_ref[...] = jnp.zeros_like(dg_ref)

        dg_ref[...] += dg

    row = pl.BlockSpec((tr, d), lambda i: (i, 0))
    vec = pl.BlockSpec((1, d), lambda i: (0, 0))
    return pl.pallas_call(body, name=name, grid=(t // tr,), in_specs=[row, vec, row, row], out_specs=[row, row, vec],
                          out_shape=[_sds((t, d), f32), _sds((t, d), bf16), _sds((1, d), f32)],
                          compiler_params=_cparams(("arbitrary",)))(x, gain, dh, dres)


def _loss_head(x, gain, target):
    t, d = x.shape
    tr = _tile(t, (256, 128))

    def f(xv, g, tgt):
        err = _rms(xv, g) - tgt
        return 0.5 * jnp.sum(jnp.mean(err * err, axis=-1, keepdims=True), axis=0, keepdims=True)

    def body(x_ref, g_ref, t_ref, loss_ref, dx_ref, dxb_ref, dg_ref):
        loss, vjp = jax.vjp(lambda xv, g: f(xv, g, t_ref[...]), x_ref[...], g_ref[...])
        dx, dg = vjp(jnp.ones((1, 1), f32))
        dx_ref[...] = dx
        dxb_ref[...] = dx.astype(bf16)

        @pl.when(pl.program_id(0) == 0)
        def _():
            dg_ref[...] = jnp.zeros_like(dg_ref)
            loss_ref[...] = jnp.zeros_like(loss_ref)

        dg_ref[...] += dg
        loss_ref[...] += jnp.broadcast_to(loss, loss_ref.shape)

    row = pl.BlockSpec((tr, d), lambda i: (i, 0))
    vec = pl.BlockSpec((1, d), lambda i: (0, 0))
    one = pl.BlockSpec((8, 128), lambda i: (0, 0))
    return pl.pallas_call(body, name="loss_head", grid=(t // tr,), in_specs=[row, vec, row],
                          out_specs=[one, row, row, vec],
                          out_shape=[_sds((8, 128), f32), _sds((t, d), f32), _sds((t, d), bf16), _sds((1, d), f32)],
                          compiler_params=_cparams(("arbitrary",)))(x, gain, target)


def _ffn_act_fwd(u, conv_w, conv_b):
    t = u.shape[0]
    tc = FF_PAD // 2
    nb = D_FFP // tc

    def body(g_ref, v_ref, w_ref, b_ref, a_ref):
        gc = _conv_fwd(g_ref[...], w_ref, FFN_CONV) + b_ref[...]
        a_ref[...] = (_silu(gc) * v_ref[...]).astype(bf16)

    return pl.pallas_call(
        body, name="ffn_act_fwd", grid=(nb,),
        in_specs=[pl.BlockSpec((t, tc), lambda j: (0, j)), pl.BlockSpec((t, tc), lambda j: (0, j + nb)),
                  pl.BlockSpec((FFN_CONV, tc), lambda j: (0, j)), pl.BlockSpec((1, tc), lambda j: (0, j))],
        out_specs=pl.BlockSpec((t, tc), lambda j: (0, j)), out_shape=_sds((t, D_FFP), bf16),
        compiler_params=_cparams(("parallel",)))(u, u, conv_w, conv_b)


def _ffn_act_bwd(u, conv_w, conv_b, da):
    t = u.shape[0]
    tc = FF_PAD // 2
    nb = D_FFP // tc

    def act(gc, val):
        return _silu(gc) * val

    def body(g_ref, v_ref, w_ref, b_ref, da_ref, a_ref, dg_ref, dv_ref, dw_ref, db_ref):
        gp = g_ref[...]
        gc = _conv_fwd(gp, w_ref, FFN_CONV) + b_ref[...]
        a, vjp = jax.vjp(act, gc, v_ref[...])
        dgc, dval = vjp(da_ref[...])
        a_ref[...] = a.astype(bf16)
        dv_ref[...] = dval.astype(bf16)
        db_ref[...] = jnp.sum(dgc, axis=0, keepdims=True)
        dg_ref[...] = _conv_bwd(gp, dgc, w_ref, dw_ref, FFN_CONV).astype(bf16)

    col = pl.BlockSpec((t, tc), lambda j: (0, j))
    return pl.pallas_call(
        body, name="ffn_act_bwd", grid=(nb,),
        in_specs=[col, pl.BlockSpec((t, tc), lambda j: (0, j + nb)), pl.BlockSpec((FFN_CONV, tc), lambda j: (0, j)),
                  pl.BlockSpec((1, tc), lambda j: (0, j)), col],
        out_specs=[col, col, col, pl.BlockSpec((FFN_CONV, tc), lambda j: (0, j)), pl.BlockSpec((1, tc), lambda j: (0, j))],
        out_shape=[_sds((t, D_FFP), bf16), _sds((t, D_FFP), bf16), _sds((t, D_FFP), bf16), _sds((FFN_CONV, D_FFP), f32),
                   _sds((1, D_FFP), f32)],
        compiler_params=_cparams(("parallel",)))(u, u, conv_w, conv_b, da)


def _lru_gates(xc, ra, ia, lam):
    r = _sigmoid(ra)
    i = _sigmoid(ia)
    log_a = -RG_C * r * _softplus(-lam)
    a = jnp.exp(log_a)
    u = jnp.sqrt(jnp.maximum(-_expm1(2.0 * log_a), 0.0)) * (i * xc)
    return a, u


def _lin_scan(a, u):
    n = a.shape[0]
    row = lax.broadcasted_iota(jnp.int32, a.shape, 0)
    s = 1
    while s < n:
        keep = row >= s
        u = a * jnp.where(keep, pltpu.roll(u, s, 0), 0.0) + u
        a = a * jnp.where(keep, pltpu.roll(a, s, 0), 1.0)
        s *= 2
    return u


def _rev_scan(a_next, d):
    n = d.shape[0]
    row = lax.broadcasted_iota(jnp.int32, d.shape, 0)
    a = a_next
    s = 1
    while s < n:
        keep = row < n - s
        d = a * jnp.where(keep, pltpu.roll(d, n - s, 0), 0.0) + d
        a = a * jnp.where(keep, pltpu.roll(a, n - s, 0), 1.0)
        s *= 2
    return d


def _col_conv_fwd(p, col_off, conv_w, conv_b, width, tc, name):
    t = p.shape[0]
    c = conv_w.shape[1]
    ob = col_off // tc

    def body(x_ref, w_ref, b_ref, o_ref):
        o_ref[...] = _conv_fwd(x_ref[...], w_ref, width) + b_ref[...]

    return pl.pallas_call(
        body, name=name, grid=(c // tc,),
        in_specs=[pl.BlockSpec((t, tc), lambda j: (0, j + ob)), pl.BlockSpec((width, tc), lambda j: (0, j)),
                  pl.BlockSpec((1, tc), lambda j: (0, j))],
        out_specs=pl.BlockSpec((t, tc), lambda j: (0, j)), out_shape=_sds((t, c), f32),
        compiler_params=_cparams(("parallel",)))(p, conv_w, conv_b)


def _col_conv_bwd(p, col_off, conv_w, dxc, width, tc, name):
    t = p.shape[0]
    c = conv_w.shape[1]
    ob = col_off // tc

    def body(x_ref, w_ref, d_ref, dx_ref, dw_ref, db_ref):
        d = d_ref[...]
        db_ref[...] = jnp.sum(d, axis=0, keepdims=True)
        dx_ref[...] = _conv_bwd(x_ref[...], d, w_ref, dw_ref, width).astype(bf16)

    col = pl.BlockSpec((t, tc), lambda j: (0, j))
    return pl.pallas_call(
        body, name=name, grid=(c // tc,),
        in_specs=[pl.BlockSpec((t, tc), lambda j: (0, j + ob)), pl.BlockSpec((width, tc), lambda j: (0, j)), col],
        out_specs=[col, pl.BlockSpec((width, tc), lambda j: (0, j)), pl.BlockSpec((1, tc), lambda j: (0, j))],
        out_shape=[_sds((t, c), bf16), _sds((width, c), f32), _sds((1, c), f32)],
        compiler_params=_cparams(("parallel",)))(p, conv_w, dxc)


def _lru_fwd(p, xc, wa, ba, wx, bx, lam):
    t = p.shape[0]
    bw = LRU_BLOCK

    def body(y_ref, xc_ref, wa_ref, ba_ref, wx_ref, bx_ref, lam_ref, out_ref, hs_ref, a_ref):
        xc_v = xc_ref[...]
        xb = xc_v.astype(bf16)
        ra = _dot(xb, wa_ref[0]) + ba_ref[...]
        ia = _dot(xb, wx_ref[0]) + bx_ref[...]
        a, u = _lru_gates(xc_v, ra, ia, lam_ref[...])
        a_ref[...] = a
        hs = _lin_scan(a, u)
        hs_ref[...] = hs
        out_ref[...] = (hs * _gelu(y_ref[...])).astype(bf16)

    col = pl.BlockSpec((t, bw), lambda h: (0, h))
    vec = pl.BlockSpec((1, bw), lambda h: (0, h))
    mat = pl.BlockSpec((1, bw, bw), lambda h: (h, 0, 0))
    return pl.pallas_call(
        body, name="lru_fwd", grid=(HEADS,), in_specs=[col, col, mat, vec, mat, vec, vec], out_specs=[col, col, col],
        out_shape=[_sds((t, LRU_WIDTH), bf16), _sds((t, LRU_WIDTH), f32), _sds((t, LRU_WIDTH), f32)],
        compiler_params=_cparams(("parallel",)))(p, xc, wa, ba, wx, bx, lam)


def _lru_bwd_scan(p, a, hs, dout):
    t = p.shape[0]
    bw = LRU_BLOCK

    def body(y_ref, a_ref, hs_ref, do_ref, dy_ref, da_ref, du_ref):
        hs_v = hs_ref[...]
        do = do_ref[...]
        gate, vjp = jax.vjp(_gelu, y_ref[...])
        dy_ref[...] = vjp(do * hs_v)[0].astype(bf16)
        g = _rev_scan(_shift_up(a_ref[...], 1), do * gate)
        du_ref[...] = g
        da_ref[...] = g * _shift_down(hs_v, 1)

    col = pl.BlockSpec((t, bw), lambda h: (0, h))
    return pl.pallas_call(
        body, name="lru_bwd_scan", grid=(HEADS,), in_specs=[col, col, col, col], out_specs=[col, col, col],
        out_shape=[_sds((t, LRU_WIDTH), bf16), _sds((t, LRU_WIDTH), f32), _sds((t, LRU_WIDTH), f32)],
        compiler_params=_cparams(("parallel",)))(p, a, hs, dout)


def _lru_bwd_gates(xc, da, du, wa, ba, wx, bx, lam):
    t = xc.shape[0]
    bw = LRU_BLOCK
    tr = _tile(t, (512, 256, 128))

    def body(xc_ref, da_ref, du_ref, wa_ref, ba_ref, wx_ref, bx_ref, lam_ref,
             dxc_ref, dwa_ref, dwx_ref, dba_ref, dbx_ref, dlam_ref):
        xc_v = xc_ref[...]
        xb = xc_v.astype(bf16)
        ra = _dot(xb, wa_ref[0]) + ba_ref[...]
        ia = _dot(xb, wx_ref[0]) + bx_ref[...]
        _, vjp = jax.vjp(_lru_gates, xc_v, ra, ia, lam_ref[...])
        dxc, dra, dia, dlam = vjp((da_ref[...], du_ref[...]))
        drb, dib = dra.astype(bf16), dia.astype(bf16)
        dxc_ref[...] = dxc + _dot(drb, wa_ref[0], NT) + _dot(dib, wx_ref[0], NT)

        @pl.when(pl.program_id(1) == 0)
        def _():
            dwa_ref[...] = jnp.zeros_like(dwa_ref)
            dwx_ref[...] = jnp.zeros_like(dwx_ref)
            dba_ref[...] = jnp.zeros_like(dba_ref)
            dbx_ref[...] = jnp.zeros_like(dbx_ref)
            dlam_ref[...] = jnp.zeros_like(dlam_ref)

        dwa_ref[0] += _dot(xb, drb, TN)
        dwx_ref[0] += _dot(xb, dib, TN)
        dba_ref[...] += jnp.sum(dra, axis=0, keepdims=True)
        dbx_ref[...] += jnp.sum(dia, axis=0, keepdims=True)
        dlam_ref[...] += dlam

    tile = pl.BlockSpec((tr, bw), lambda h, i: (i, h))
    vec = pl.BlockSpec((1, bw), lambda h, i: (0, h))
    mat = pl.BlockSpec((1, bw, bw), lambda h, i: (h, 0, 0))
    return pl.pallas_call(
        body, name="lru_bwd_gates", grid=(HEADS, t // tr), in_specs=[tile, tile, tile, mat, vec, mat, vec, vec],
        out_specs=[tile, mat, mat, vec, vec, vec],
        out_shape=[_sds((t, LRU_WIDTH), f32), _sds((HEADS, bw, bw), f32), _sds((HEADS, bw, bw), f32),
                   _sds((1, LRU_WIDTH), f32), _sds((1, LRU_WIDTH), f32), _sds((1, LRU_WIDTH), f32)],
        compiler_params=_cparams(("parallel", "arbitrary")))(xc, da, du, wa, ba, wx, bx, lam)


def _gdn_pre_fn(cq, ck, cv, ba, alog, dtb, h):
    q, k, v = _silu(cq), _silu(ck), _silu(cv)
    q = q * lax.rsqrt(jnp.sum(q * q, axis=-1, keepdims=True) + EPS) * (HEAD_DIM ** -0.5)
    k = k * lax.rsqrt(jnp.sum(k * k, axis=-1, keepdims=True) + EPS)
    lane = lax.broadcasted_iota(jnp.int32, (1, HEAD_DIM), 1)
    mb = (lane == h).astype(f32)
    ma = (lane == HEADS + h).astype(f32)
    beta_raw = jnp.sum(ba * mb, axis=-1, keepdims=True)
    alpha = jnp.sum(ba * ma, axis=-1, keepdims=True)
    al = jnp.sum(alog * mb, axis=-1, keepdims=True)
    db = jnp.sum(dtb * mb, axis=-1, keepdims=True)
    beta = _sigmoid(beta_raw)
    g = -jnp.exp(al) * _softplus(alpha + db)
    return q, k, v, jnp.broadcast_to(beta, q.shape), jnp.broadcast_to(g, q.shape)


def _gdn_pre_fwd(p, conv_w, alog, dtb):
    t = p.shape[0]
    hd = HEAD_DIM

    def body(pq_ref, pk_ref, pv_ref, ba_ref, wq_ref, wk_ref, wv_ref, al_ref, dt_ref, q_ref, k_ref, v_ref, b_ref, g_ref):
        h = pl.program_id(0)
        cq = _conv_fwd(pq_ref[...], wq_ref, GDN_CONV)
        ck = _conv_fwd(pk_ref[...], wk_ref, GDN_CONV)
        cv = _conv_fwd(pv_ref[...], wv_ref, GDN_CONV)
        q, k, v, be, ge = _gdn_pre_fn(cq, ck, cv, ba_ref[...], al_ref[...], dt_ref[...], h)
        q_ref[...], k_ref[...], v_ref[...], b_ref[...], g_ref[...] = q, k, v, be, ge

    def pcol(off):
        return pl.BlockSpec((t, hd), lambda h: (0, h + off // hd))

    def wcol(off):
        return pl.BlockSpec((GDN_CONV, hd), lambda h: (0, h + off // hd))

    vec = pl.BlockSpec((1, hd), lambda h: (0, 0))
    out = pl.BlockSpec((t, hd), lambda h: (0, h))
    return pl.pallas_call(
        body, name="gdn_pre_fwd", grid=(HEADS,),
        in_specs=[pcol(OFF_Q), pcol(OFF_K), pcol(OFF_V), pl.BlockSpec((t, hd), lambda h: (0, OFF_BA // hd)),
                  wcol(0), wcol(GDN_WIDTH), wcol(2 * GDN_WIDTH), vec, vec],
        out_specs=[out] * 5, out_shape=[_sds((t, GDN_WIDTH), f32)] * 5,
        compiler_params=_cparams(("parallel",)))(p, p, p, p, conv_w, conv_w, conv_w, alog, dtb)


def _gdn_pre_bwd(p, conv_w, alog, dtb, dq, dk, dv, dbe, dge):
    t = p.shape[0]
    hd = HEAD_DIM

    def body(pq_ref, pk_ref, pv_ref, ba_ref, wq_ref, wk_ref, wv_ref, al_ref, dt_ref,
             dq_ref, dk_ref, dv_ref, dbe_ref, dge_ref,
             opq_ref, opk_ref, opv_ref, dba_ref, dwq_ref, dwk_ref, dwv_ref, dal_ref, ddt_ref):
        h = pl.program_id(0)
        pq, pk, pv = pq_ref[...], pk_ref[...], pv_ref[...]
        cq = _conv_fwd(pq, wq_ref, GDN_CONV)
        ck = _conv_fwd(pk, wk_ref, GDN_CONV)
        cv = _conv_fwd(pv, wv_ref, GDN_CONV)
        _, vjp = jax.vjp(functools.partial(_gdn_pre_fn, h=h), cq, ck, cv, ba_ref[...], al_ref[...], dt_ref[...])
        dcq, dck, dcv, dba, dal, ddt = vjp((dq_ref[...], dk_ref[...], dv_ref[...], dbe_ref[...], dge_ref[...]))
        opq_ref[...] = _conv_bwd(pq, dcq, wq_ref, dwq_ref, GDN_CONV).astype(bf16)
        opk_ref[...] = _conv_bwd(pk, dck, wk_ref, dwk_ref, GDN_CONV).astype(bf16)
        opv_ref[...] = _conv_bwd(pv, dcv, wv_ref, dwv_ref, GDN_CONV).astype(bf16)

        @pl.when(h == 0)
        def _():
            dba_ref[...] = jnp.zeros_like(dba_ref)
            dal_ref[...] = jnp.zeros_like(dal_ref)
            ddt_ref[...] = jnp.zeros_like(ddt_ref)

        dba_ref[...] += dba
        dal_ref[...] += dal
        ddt_ref[...] += ddt

    def pcol(off):
        return pl.BlockSpec((t, hd), lambda h: (0, h + off // hd))

    def wcol(off):
        return pl.BlockSpec((GDN_CONV, hd), lambda h: (0, h + off // hd))

    vec = pl.BlockSpec((1, hd), lambda h: (0, 0))
    col = pl.BlockSpec((t, hd), lambda h: (0, h))
    full = pl.BlockSpec((t, hd), lambda h: (0, 0))
    wout = pl.BlockSpec((GDN_CONV, hd), lambda h: (0, h))
    return pl.pallas_call(
        body, name="gdn_pre_bwd", grid=(HEADS,),
        in_specs=[pcol(OFF_Q), pcol(OFF_K), pcol(OFF_V), pl.BlockSpec((t, hd), lambda h: (0, OFF_BA // hd)),
                  wcol(0), wcol(GDN_WIDTH), wcol(2 * GDN_WIDTH), vec, vec, col, col, col, col, col],
        out_specs=[col, col, col, full, wout, wout, wout, vec, vec],
        out_shape=[_sds((t, GDN_WIDTH), bf16)] * 3 + [_sds((t, hd), f32)] + [_sds((GDN_CONV, GDN_WIDTH), f32)] * 3
        + [_sds((1, hd), f32)] * 2,
        compiler_params=_cparams(("arbitrary",)))(p, p, p, p, conv_w, conv_w, conv_w, alog, dtb, dq, dk, dv, dbe, dge)


BNN = (((2,), (1,)), ((0,), (0,)))
BNT = (((2,), (2,)), ((0,), (0,)))
BTN = (((1,), (1,)), ((0,), (0,)))


def _hdot(a, b, dn=BNN, precision=None):
    return lax.dot_general(a, b, dn, precision=precision, preferred_element_type=f32)


def _hbdot(a, b, dn=BNN):
    return _hdot(a.astype(bf16), b.astype(bf16), dn)


def _tri_inverse(a):
    c = a.shape[-1]
    r = lax.broadcasted_iota(jnp.int32, (c, c), 0)
    col = lax.broadcasted_iota(jnp.int32, (c, c), 1)
    m = -a
    inv = jnp.where(r == col, 1.0, 0.0) + m
    s = 2
    while s < c:
        m = _hdot(m, m, precision=HI)
        inv = inv + _hdot(inv, m, precision=HI)
        s *= 2
    return inv


def _gdn_chunk(s, q, k, v, ge, be):
    nh, c, _ = q.shape
    r = lax.broadcasted_iota(jnp.int32, (c, c), 0)
    col = lax.broadcasted_iota(jnp.int32, (c, c), 1)
    causal = r >= col
    tri = jnp.broadcast_to(causal.astype(f32), (nh, c, c))
    gc = _hdot(tri, ge, precision=HI)
    gcc = gc[:, :, :c]
    gcr = jnp.swapaxes(gc, 1, 2)[:, :c, :]
    decay = jnp.where(causal, jnp.exp(jnp.where(causal, gcc - gcr, 0.0)), 0.0)
    kb = k * be
    lower = jnp.where(r > col, _hbdot(kb, k, BNT) * decay, 0.0)
    tinv = _tri_inverse(lower)
    egc = jnp.exp(gc)
    u = _hdot(tinv, v * be, precision=HI)
    w = _hdot(tinv, kb * egc, precision=HI)
    attn = _hbdot(q, k, BNT) * decay
    gl = gc[:, c - 1:c, :]
    v_new = u - _hbdot(w, s)
    o = _hbdot(q * egc, s) + _hbdot(attn, v_new)
    s_new = s * jnp.exp(gl) + _hbdot(k * jnp.exp(gl - gc), v_new, BTN)
    return o, s_new


def _heads_major(ref):
    return jnp.stack([ref[:, h * HEAD_DIM:(h + 1) * HEAD_DIM] for h in range(HEADS)])


def _gdn_core_fwd(q, k, v, ge, be):
    t = q.shape[0]
    c, hd = GDN_CHUNK, HEAD_DIM
    n = t // c

    def body(q_ref, k_ref, v_ref, g_ref, b_ref, o_ref, st_ref, s_ref):
        @pl.when(pl.program_id(0) == 0)
        def _():
            s_ref[...] = jnp.zeros_like(s_ref)

        s = s_ref[...]
        st_ref[:, 0] = s
        o, s_new = _gdn_chunk(s, *[_heads_major(r) for r in (q_ref, k_ref, v_ref, g_ref, b_ref)])
        for h in range(HEADS):
            o_ref[:, h * hd:(h + 1) * hd] = o[h]
        s_ref[...] = s_new

    tile = pl.BlockSpec((c, GDN_WIDTH), lambda i: (i, 0))
    return pl.pallas_call(
        body, name="gdn_core_fwd", grid=(n,), in_specs=[tile] * 5,
        out_specs=[tile, pl.BlockSpec((HEADS, 1, hd, hd), lambda i: (0, i, 0, 0))],
        out_shape=[_sds((t, GDN_WIDTH), f32), _sds((HEADS, n, hd, hd), f32)],
        scratch_shapes=[pltpu.VMEM((HEADS, hd, hd), f32)],
        compiler_params=_cparams(("arbitrary",)))(q, k, v, ge, be)


def _gdn_core_bwd(q, k, v, ge, be, states, do):
    t = q.shape[0]
    c, hd = GDN_CHUNK, HEAD_DIM
    n = t // c

    def body(q_ref, k_ref, v_ref, g_ref, b_ref, st_ref, do_ref, dq_ref, dk_ref, dv_ref, dg_ref, db_ref, ds_ref):
        @pl.when(pl.program_id(0) == 0)
        def _():
            ds_ref[...] = jnp.zeros_like(ds_ref)

        _, vjp = jax.vjp(_gdn_chunk, st_ref[:, 0], *[_heads_major(r) for r in (q_ref, k_ref, v_ref, g_ref, b_ref)])
        ds, *dins = vjp((_heads_major(do_ref), ds_ref[...]))
        ds_ref[...] = ds
        for d_ref, d in zip((dq_ref, dk_ref, dv_ref, dg_ref, db_ref), dins):
            for h in range(HEADS):
                d_ref[:, h * hd:(h + 1) * hd] = d[h]

    tile = pl.BlockSpec((c, GDN_WIDTH), lambda i: (n - 1 - i, 0))
    return pl.pallas_call(
        body, name="gdn_core_bwd", grid=(n,),
        in_specs=[tile] * 5 + [pl.BlockSpec((HEADS, 1, hd, hd), lambda i: (0, n - 1 - i, 0, 0)), tile],
        out_specs=[tile] * 5, out_shape=[_sds((t, GDN_WIDTH), f32)] * 5,
        scratch_shapes=[pltpu.VMEM((HEADS, hd, hd), f32)],
        compiler_params=_cparams(("arbitrary",)))(q, k, v, ge, be, states, do)


def _post_fn(o, z, gain):
    return _rms(o, gain) * _silu(z)


def _post_fwd(o, p, z_off, gain, name):
    t = o.shape[0]
    hd = HEAD_DIM

    def body(o_ref, z_ref, g_ref, y_ref):
        y_ref[...] = _post_fn(o_ref[...], z_ref[...], g_ref[...]).astype(bf16)

    col = pl.BlockSpec((t, hd), lambda h: (0, h))
    return pl.pallas_call(
        body, name=name, grid=(HEADS,),
        in_specs=[col, pl.BlockSpec((t, hd), lambda h: (0, h + z_off // hd)), pl.BlockSpec((1, hd), lambda h: (0, 0))],
        out_specs=col, out_shape=_sds((t, HEADS * hd), bf16), compiler_params=_cparams(("parallel",)))(o, p, gain)


def _post_bwd(o, p, z_off, gain, dmix, mix_off, name):
    t = o.shape[0]
    hd = HEAD_DIM

    def body(o_ref, z_ref, g_ref, dy_ref, do_ref, dz_ref, dg_ref):
        _, vjp = jax.vjp(_post_fn, o_ref[...], z_ref[...], g_ref[...])
        do, dz, dg = vjp(dy_ref[...])
        do_ref[...] = do
        dz_ref[...] = dz.astype(bf16)

        @pl.when(pl.program_id(0) == 0)
        def _():
            dg_ref[...] = jnp.zeros_like(dg_ref)

        dg_ref[...] += dg

    col = pl.BlockSpec((t, hd), lambda h: (0, h))
    vec = pl.BlockSpec((1, hd), lambda h: (0, 0))
    return pl.pallas_call(
        body, name=name, grid=(HEADS,),
        in_specs=[col, pl.BlockSpec((t, hd), lambda h: (0, h + z_off // hd)), vec,
                  pl.BlockSpec((t, hd), lambda h: (0, h + mix_off // hd))],
        out_specs=[col, col, vec], out_shape=[_sds((t, HEADS * hd), f32), _sds((t, HEADS * hd), bf16), _sds((1, hd), f32)],
        compiler_params=_cparams(("arbitrary",)))(o, p, gain, dmix)


def _hgrn_pre_fn(qb, fb, lbw, layer):
    l0, l1 = lbw[0:1, :], lbw[1:2, :]
    m = jnp.maximum(l0, l1)
    e0, e1 = jnp.exp(l0 - m), jnp.exp(l1 - m)
    p0, p1 = e0 / (e0 + e1), e1 / (e0 + e1)
    lb = (p0 - p0) if layer == 0 else ((p0 + p1) - p0)
    f = lb + (1.0 - lb) * _sigmoid(fb)
    return _silu(qb), 1.0 - f, jnp.log(jnp.maximum(f, F_FLOOR))


def _hgrn_pre_fwd(p, lbw, layer):
    t = p.shape[0]
    tc = HEAD_DIM

    def body(qb_ref, fb_ref, lb_ref, q_ref, k_ref, lf_ref):
        q_ref[...], k_ref[...], lf_ref[...] = _hgrn_pre_fn(qb_ref[...], fb_ref[...], lb_ref[...], layer)

    col = pl.BlockSpec((t, tc), lambda j: (0, j))
    return pl.pallas_call(
        body, name="hgrn_pre_fwd", grid=(GDN_WIDTH // tc,),
        in_specs=[pl.BlockSpec((t, tc), lambda j: (0, j + OFF_QB // tc)), pl.BlockSpec((t, tc), lambda j: (0, j + OFF_FB // tc)),
                  pl.BlockSpec((2, tc), lambda j: (0, j))],
        out_specs=[col] * 3, out_shape=[_sds((t, GDN_WIDTH), f32)] * 3,
        compiler_params=_cparams(("parallel",)))(p, p, lbw)


def _hgrn_pre_bwd(p, lbw, layer, dq, dk, dlf):
    t = p.shape[0]
    tc = HEAD_DIM

    def body(qb_ref, fb_ref, lb_ref, dq_ref, dk_ref, dlf_ref, dqb_ref, dfb_ref, dlb_ref):
        _, vjp = jax.vjp(functools.partial(_hgrn_pre_fn, layer=layer), qb_ref[...], fb_ref[...], lb_ref[...])
        dqb, dfb, dlb = vjp((dq_ref[...], dk_ref[...], dlf_ref[...]))
        dqb_ref[...] = dqb.astype(bf16)
        dfb_ref[...] = dfb.astype(bf16)
        dlb_ref[...] = dlb

    col = pl.BlockSpec((t, tc), lambda j: (0, j))
    lb = pl.BlockSpec((2, tc), lambda j: (0, j))
    return pl.pallas_call(
        body, name="hgrn_pre_bwd", grid=(GDN_WIDTH // tc,),
        in_specs=[pl.BlockSpec((t, tc), lambda j: (0, j + OFF_QB // tc)), pl.BlockSpec((t, tc), lambda j: (0, j + OFF_FB // tc)),
                  lb, col, col, col],
        out_specs=[col, col, lb], out_shape=[_sds((t, GDN_WIDTH), bf16)] * 2 + [_sds((2, GDN_WIDTH), f32)],
        compiler_params=_cparams(("parallel",)))(p, p, lbw, dq, dk, dlf)


def _hgrn_step(st, q, k, lf, v):
    c = HGRN_CHUNK
    nh = q.shape[0]
    r2 = lax.broadcasted_iota(jnp.int32, (c, c), 0)
    c2 = lax.broadcasted_iota(jnp.int32, (c, c), 1)
    tri = jnp.broadcast_to((r2 >= c2).astype(f32), (nh, c, c))
    i3 = lax.broadcasted_iota(jnp.int32, (c, c, HEAD_DIM), 0)
    j3 = lax.broadcasted_iota(jnp.int32, (c, c, HEAD_DIM), 1)
    mask = i3 >= j3
    outs = []
    for n in range(q.shape[1] // c):
        sl = slice(n * c, (n + 1) * c)
        qc, kc, lc, vc = q[:, sl], k[:, sl], lf[:, sl], v[:, sl]
        b = _hdot(tri, lc, precision=HI)
        rel = jnp.where(mask, jnp.exp(jnp.where(mask, b[:, :, None, :] - b[:, None, :, :], 0.0)), 0.0)
        scores = jnp.sum(qc[:, :, None, :] * kc[:, None, :, :] * rel, axis=-1)
        bl = b[:, c - 1:c, :]
        o = _hbdot(scores, vc) + _hbdot(qc * jnp.exp(b), st, BNT)
        st = st * jnp.exp(bl) + _hbdot(vc, kc * jnp.exp(bl - b), BTN)
        outs.append(o)
    return jnp.concatenate(outs, axis=1), st


def _hgrn_core_fwd(q, k, lf, p):
    t = q.shape[0]
    hd = HEAD_DIM
    rs = min(HGRN_STEP, t)
    n = t // rs

    def body(q_ref, k_ref, lf_ref, v_ref, o_ref, st_ref, s_ref):
        @pl.when(pl.program_id(0) == 0)
        def _():
            s_ref[...] = jnp.zeros_like(s_ref)

        s = s_ref[...]
        st_ref[:, 0] = s
        o, s_new = _hgrn_step(s, *[_heads_major(r) for r in (q_ref, k_ref, lf_ref, v_ref)])
        for h in range(HEADS):
            o_ref[:, h * hd:(h + 1) * hd] = o[h]
        s_ref[...] = s_new

    tile = pl.BlockSpec((rs, GDN_WIDTH), lambda i: (i, 0))
    return pl.pallas_call(
        body, name="hgrn_core_fwd", grid=(n,),
        in_specs=[tile, tile, tile, pl.BlockSpec((rs, GDN_WIDTH), lambda i: (i, OFF_IB // GDN_WIDTH))],
        out_specs=[tile, pl.BlockSpec((HEADS, 1, hd, hd), lambda i: (0, i, 0, 0))],
        out_shape=[_sds((t, GDN_WIDTH), f32), _sds((HEADS, n, hd, hd), f32)],
        scratch_shapes=[pltpu.VMEM((HEADS, hd, hd), f32)],
        compiler_params=_cparams(("arbitrary",)))(q, k, lf, p)


def _hgrn_core_bwd(q, k, lf, p, states, do):
    t = q.shape[0]
    hd = HEAD_DIM
    rs = min(HGRN_STEP, t)
    n = t // rs

    def body(q_ref, k_ref, lf_ref, v_ref, st_ref, do_ref, dq_ref, dk_ref, dlf_ref, dv_ref, ds_ref):
        @pl.when(pl.program_id(0) == 0)
        def _():
            ds_ref[...] = jnp.zeros_like(ds_ref)

        _, vjp = jax.vjp(_hgrn_step, st_ref[:, 0], *[_heads_major(r) for r in (q_ref, k_ref, lf_ref, v_ref)])
        ds, *dins = vjp((_heads_major(do_ref), ds_ref[...]))
        ds_ref[...] = ds
        for d_ref, d in zip((dq_ref, dk_ref, dlf_ref, dv_ref), dins):
            for h in range(HEADS):
                d_ref[:, h * hd:(h + 1) * hd] = d[h].astype(d_ref.dtype)

    tile = pl.BlockSpec((rs, GDN_WIDTH), lambda i: (n - 1 - i, 0))
    return pl.pallas_call(
        body, name="hgrn_core_bwd", grid=(n,),
        in_specs=[tile, tile, tile, pl.BlockSpec((rs, GDN_WIDTH), lambda i: (n - 1 - i, OFF_IB // GDN_WIDTH)),
                  pl.BlockSpec((HEADS, 1, hd, hd), lambda i: (0, n - 1 - i, 0, 0)), tile],
        out_specs=[tile] * 4, out_shape=[_sds((t, GDN_WIDTH), f32)] * 3 + [_sds((t, GDN_WIDTH), bf16)],
        scratch_shapes=[pltpu.VMEM((HEADS, hd, hd), f32)],
        compiler_params=_cparams(("arbitrary",)))(q, k, lf, p, states, do)


def _row(v):
    return v.reshape(1, -1)


def _anchored(w, row, key):
    tok = w.get(key)
    return row if tok is None else row + tok[0, 0]


def _pad_lanes(v, n=HEAD_DIM):
    return jnp.pad(v.reshape(1, -1), ((0, 0), (0, n - v.shape[-1])))


def _ffn_fwd(x, w, l):
    h = _rms_fwd(x, _row(w['norm_ffn'][l]), "ffn_norm")
    u = _mm(h, w['ffn_w_up'][l], name="ffn_up")
    a = _ffn_act_fwd(u, w['ffn_conv_w'][l], _row(w['ffn_conv_b'][l]))
    y = _mm(a, w['ffn_w_down'][l], add=x, name="ffn_down")
    return y, (x, h, u)


def _ffn_bwd(saved, w, l, dy, dyb, grads):
    x, h, u = saved
    da = _mm(dyb, w['ffn_w_down'][l], tb=True, name="ffn_down_dx")
    a, dg, dv, dcw, dcb = _ffn_act_bwd(u, w['ffn_conv_w'][l], _anchored(w, _row(w['ffn_conv_b'][l]), ('bwd', l)), da)
    grads['ffn_w_down'][l] = _mm(a, dyb, ta=True, out_dtype=bf16, name="ffn_down_dw")
    du = jnp.concatenate([dg, dv], axis=1)
    grads['ffn_w_up'][l] = _mm(h, du, ta=True, out_dtype=bf16, name="ffn_up_dw")
    dh = _mm(du, w['ffn_w_up'][l], tb=True, name="ffn_up_dx")
    dx, dxb, dgain = _rms_bwd(x, _row(w['norm_ffn'][l]), dh, dy, "ffn_norm_bwd")
    grads['ffn_conv_w'][l] = dcw
    grads['ffn_conv_b'][l] = dcb[0]
    grads['norm_ffn'][l] = dgain[0]
    return dx, dxb


def _odd_fwd(x, w, l, j):
    h = _rms_fwd(x, _anchored(w, _row(w['norm_mix'][l]), ('fwd', l)), "mix_norm")
    p = _mm(h, w['c_w_in'][j], name="lru_in")
    xc = _col_conv_fwd(p, LRU_WIDTH, w['c_conv_w'][j], _row(w['c_conv_b'][j]), LRU_CONV, 256, "lru_conv_fwd")
    out, hs, a = _lru_fwd(p, xc, w['c_gate_a_w'][j], _row(w['c_gate_a_b'][j]), w['c_gate_x_w'][j],
                          _row(w['c_gate_x_b'][j]), _row(w['c_lambda'][j]))
    y = _mm(out, w['c_w_out'][j], add=x, name="lru_out")
    return y, (x, h, p, xc, out, hs, a)


def _odd_bwd(saved, w, l, j, dy, dyb, grads):
    x, h, p, xc, out, hs, a = saved
    dout = _mm(dyb, w['c_w_out'][j], tb=True, name="lru_out_dx")
    grads['c_w_out'][j] = _mm(out, dyb, ta=True, out_dtype=bf16, name="lru_out_dw")
    dyb_, da, du = _lru_bwd_scan(p, a, hs, dout)
    dxc, dwa, dwx, dba, dbx, dlam = _lru_bwd_gates(xc, da, du, w['c_gate_a_w'][j], _row(w['c_gate_a_b'][j]),
                                                   w['c_gate_x_w'][j], _row(w['c_gate_x_b'][j]), _row(w['c_lambda'][j]))
    dxb_, dcw, dcb = _col_conv_bwd(p, LRU_WIDTH, w['c_conv_w'][j], dxc, LRU_CONV, 256, "lru_conv_bwd")
    dp = jnp.concatenate([dyb_, dxb_], axis=1)
    grads['c_w_in'][j] = _mm(h, dp, ta=True, out_dtype=bf16, name="lru_in_dw")
    dh = _mm(dp, w['c_w_in'][j], tb=True, name="lru_in_dx")
    dx, dxb, dgain = _rms_bwd(x, _row(w['norm_mix'][l]), dh, dy, "mix_norm_bwd")
    grads['c_gate_a_w'][j], grads['c_gate_x_w'][j] = dwa, dwx
    grads['c_gate_a_b'][j], grads['c_gate_x_b'][j], grads['c_lambda'][j] = dba[0], dbx[0], dlam[0]
    grads['c_conv_w'][j], grads['c_conv_b'][j] = dcw, dcb[0]
    grads['norm_mix'][l] = dgain[0]
    return dx, dxb


def _even_fwd(x, w, l, j):
    h = _rms_fwd(x, _anchored(w, _row(w['norm_mix'][l]), ('fwd', l)), "mix_norm")
    p = _mm(h, w['ab_w_in'][j], name="ab_in")
    alog, dtb = _pad_lanes(w['gdn_a_log'][j]), _pad_lanes(w['gdn_dt_bias'][j])
    q, k, v, be, ge = _gdn_pre_fwd(p, w['gdn_conv_w'][j], alog, dtb)
    oa, sa = _gdn_core_fwd(q, k, v, ge, be)
    ya = _post_fwd(oa, p, OFF_Z, _row(w['gdn_norm'][j]), "gdn_post_fwd")
    qq, kk, lf = _hgrn_pre_fwd(p, w['hgrn_lower_bounds'], j)
    ob, sb = _hgrn_core_fwd(qq, kk, lf, p)
    yb = _post_fwd(ob, p, OFF_GB, _row(w['hgrn_norm'][j]), "hgrn_post_fwd")
    mix = jnp.concatenate([ya, yb], axis=1)
    y = _mm(mix, w['ab_w_out'][j], add=x, name="ab_out")
    return y, (x, h, p, q, k, v, be, ge, oa, sa, qq, kk, lf, ob, sb, mix)


def _even_bwd(saved, w, l, j, dy, dyb, grads):
    x, h, p, q, k, v, be, ge, oa, sa, qq, kk, lf, ob, sb, mix = saved
    alog, dtb = _pad_lanes(w['gdn_a_log'][j]), _pad_lanes(w['gdn_dt_bias'][j])
    dmix = _mm(dyb, w['ab_w_out'][j], tb=True, name="ab_out_dx")
    grads['ab_w_out'][j] = _mm(mix, dyb, ta=True, out_dtype=bf16, name="ab_out_dw")
    doa, dz, dgn = _post_bwd(oa, p, OFF_Z, _row(w['gdn_norm'][j]), dmix, 0, "gdn_post_bwd")
    dob, dgb, dhn = _post_bwd(ob, p, OFF_GB, _row(w['hgrn_norm'][j]), dmix, GDN_WIDTH, "hgrn_post_bwd")
    dq, dk, dv, dge, dbe = _gdn_core_bwd(q, k, v, ge, be, sa, doa)
    dpq, dpk, dpv, dba, dwq, dwk, dwv, dal, ddt = _gdn_pre_bwd(p, w['gdn_conv_w'][j], alog, dtb, dq, dk, dv, dbe, dge)
    dqq, dkk, dlf, dib = _hgrn_core_bwd(qq, kk, lf, p, sb, dob)
    dqb, dfb, dlb = _hgrn_pre_bwd(p, w['hgrn_lower_bounds'], j, dqq, dkk, dlf)
    dp = jnp.concatenate([dpq, dpk, dpv, dz, dqb, dfb, dib, dgb, dba.astype(bf16)], axis=1)
    grads['ab_w_in'][j] = _mm(h, dp, ta=True, out_dtype=bf16, name="ab_in_dw")
    dh = _mm(dp, w['ab_w_in'][j], tb=True, name="ab_in_dx")
    dx, dxb, dgain = _rms_bwd(x, _row(w['norm_mix'][l]), dh, dy, "mix_norm_bwd")
    grads['gdn_conv_w'][j] = jnp.concatenate([dwq, dwk, dwv], axis=1)
    grads['gdn_a_log'][j], grads['gdn_dt_bias'][j] = dal[0, :HEADS], ddt[0, :HEADS]
    grads['gdn_norm'][j], grads['hgrn_norm'][j] = dgn[0], dhn[0]
    grads['hgrn_lower_bounds'].append(dlb)
    grads['norm_mix'][l] = dgain[0]
    return dx, dxb


def _ab_permute(w_in):
    pad = jnp.zeros(w_in.shape[:-1] + (AB_PAD - AB_COLS,), w_in.dtype)
    return jnp.concatenate([w_in[..., :2048], w_in[..., 2056:], w_in[..., 2048:2056], pad], axis=-1)


def _ab_unpermute(g):
    return jnp.concatenate([g[..., :2048], g[..., 4096:4104], g[..., 2048:4096]], axis=-1)


def _block_pad(a, axis, nblk, padded):
    axis = axis % a.ndim
    s = a.shape
    a = a.reshape(s[:axis] + (nblk, s[axis] // nblk) + s[axis + 1:])
    pad = [(0, 0)] * a.ndim
    pad[axis + 1] = (0, padded - s[axis] // nblk)
    return jnp.pad(a, pad).reshape(s[:axis] + (nblk * padded,) + s[axis + 1:])


def _block_unpad(a, axis, nblk, width):
    axis = axis % a.ndim
    s = a.shape
    a = a.reshape(s[:axis] + (nblk, s[axis] // nblk) + s[axis + 1:])
    a = lax.slice_in_dim(a, 0, width, axis=axis + 1)
    return a.reshape(s[:axis] + (nblk * width,) + s[axis + 1:])


def _kernel_layout(w):
    w = dict(w)
    w['ab_w_in'] = _ab_permute(w['ab_w_in'])
    w['ffn_w_up'] = _block_pad(w['ffn_w_up'], 2, N_DEV, FF_PAD)
    w['ffn_w_down'] = _block_pad(w['ffn_w_down'], 1, 4, FF_PAD)
    w['ffn_conv_w'] = _block_pad(w['ffn_conv_w'], 2, 4, FF_PAD)
    w['ffn_conv_b'] = _block_pad(w['ffn_conv_b'], 1, 4, FF_PAD)
    return w


def _natural_grads(g):
    g = dict(g)
    g['ab_w_in'] = _ab_unpermute(g['ab_w_in'])
    g['ffn_w_up'] = _block_unpad(g['ffn_w_up'], 2, N_DEV, FF_SHARD)
    g['ffn_w_down'] = _block_unpad(g['ffn_w_down'], 1, 4, FF_SHARD)
    g['ffn_conv_w'] = _block_unpad(g['ffn_conv_w'], 2, 4, FF_SHARD)
    g['ffn_conv_b'] = _block_unpad(g['ffn_conv_b'], 1, 4, FF_SHARD)
    return g


def _local_step(x, target, w, fetch=None, push=None):
    grads = {n: [None] * (DEPTH if n in ('norm_mix', 'norm_ffn') or n.startswith('ffn_') else 2)
             for n in WEIGHTS if n not in ('norm_final', 'hgrn_lower_bounds')}
    grads['hgrn_lower_bounds'] = []
    saved = []
    for l in range(DEPTH):
        j = l // 2
        if fetch is not None:
            fetch(l, x)
        x, s_mix = (_even_fwd if l % 2 == 0 else _odd_fwd)(x, w, l, j)
        x, s_ffn = _ffn_fwd(x, w, l)
        saved.append((s_mix, s_ffn))
    loss, dx, dxb, dgf = _loss_head(x, _row(w['norm_final']), target)
    for l in reversed(range(DEPTH)):
        j = l // 2
        s_mix, s_ffn = saved[l]
        dx, dxb = _ffn_bwd(s_ffn, w, l, dx, dxb, grads)
        dx, dxb = (_even_bwd if l % 2 == 0 else _odd_bwd)(s_mix, w, l, j, dx, dxb, grads)
        if push is not None:
            push(l, {nm: grads[nm].pop(li) for nm, li in reversed(_layer_items(l))})
    out = {n: jnp.stack(g) for n, g in grads.items() if n != 'hgrn_lower_bounds' and g}
    out['hgrn_lower_bounds'] = grads['hgrn_lower_bounds'][0] + grads['hgrn_lower_bounds'][1]
    out['norm_final'] = dgf[0]
    return loss[0, 0], dx, out


def _position():
    return lax.axis_index("x"), lax.axis_index("y"), lax.axis_index("c")


BLOCK_LAYOUT = {
    'ab_w_in': ((2, D_MODEL, N_DEV * AB_SHARD_PAD), (2, D_MODEL, AB_SHARD_PAD)),
    'ab_w_out': ((2, N_DEV, 128, D_MODEL), (2, 128, D_MODEL)),
    'c_w_in': ((2, D_MODEL, 2 * LRU_WIDTH), (2, D_MODEL, 256)),
    'c_w_out': ((2, N_DEV, 128, D_MODEL), (2, 128, D_MODEL)),
    'c_gate_a_w': ((2, HEADS, N_DEV, 32, LRU_BLOCK), (2, HEADS, 32, LRU_BLOCK)),
    'c_gate_x_w': ((2, HEADS, N_DEV, 32, LRU_BLOCK), (2, HEADS, 32, LRU_BLOCK)),
    'ffn_w_up': ((DEPTH, D_MODEL, N_DEV * FF_PAD), (DEPTH, D_MODEL, FF_PAD)),
    'ffn_w_down': ((DEPTH, 4, FF_PAD, D_MODEL), (DEPTH, FF_ROWS, D_MODEL)),
}


COL_WINDOW = {'ab_w_in': AB_SHARD_PAD, 'c_w_in': 256, 'ffn_w_up': FF_PAD}


def _block_index(name, p):
    d = 4 * p[0] + 2 * p[1] + p[2]
    if name in COL_WINDOW:
        return (slice(None), pl.ds(pl.multiple_of(d * COL_WINDOW[name], 128), COL_WINDOW[name]))
    if name == 'ffn_w_down':
        return (2 * p[0] + p[1], pl.ds(pl.multiple_of(p[2] * FF_ROWS, 16), FF_ROWS), slice(None))
    if name in ('c_gate_a_w', 'c_gate_x_w'):
        return (slice(None), d)
    return (d,)


def _block_of(name, ref, p, layered=True):
    idx = _block_index(name, p)
    if layered and name in BLOCK_LAYOUT:
        idx = (slice(None),) + idx
    return ref.at[idx]


def _layer_items(l):
    j = l // 2
    mix = ([('ab_w_in', j), ('ab_w_out', j)] if l % 2 == 0 else
           [('c_w_in', j), ('c_w_out', j), ('c_gate_a_w', j), ('c_gate_x_w', j)])
    return mix + [('ffn_w_up', l), ('ffn_w_down', l)]


def _own_land(name, shard_l, pos):
    x, y, c = pos
    d = 4 * x + 2 * y + c
    shape = BLOCK_LAYOUT[name][0][1:] if name in BLOCK_LAYOUT else (N_DEV,) + shard_l.shape
    zeros = jnp.zeros(shape, shard_l.dtype) if name == 'ffn_w_down' else lax.empty(shape, shard_l.dtype)
    if name in COL_WINDOW:
        return lax.dynamic_update_slice(zeros, shard_l, (0, d * COL_WINDOW[name]))
    if name == 'ffn_w_down':
        return lax.dynamic_update_slice(zeros, shard_l[None], (2 * x + y, c * FF_ROWS, 0))
    if name in ('c_gate_a_w', 'c_gate_x_w'):
        return lax.dynamic_update_slice(zeros, shard_l[:, None], (0, d, 0, 0))
    return lax.dynamic_update_slice(zeros, shard_l[None], (d,) + (0,) * shard_l.ndim)


def _src_of(shard_ref, li):
    return shard_ref if li is None else shard_ref.at[li]


def _gather_now(items, shards, lands):
    n = len(items)
    srcs = sorted({nm for nm, _ in items})

    def body(*refs):
        ins = dict(zip(srcs, refs[:len(srcs)]))
        outs = refs[len(srcs) + n:len(srcs) + 2 * n]
        send_sems, recv_sems = refs[len(srcs) + 2 * n:]
        x, y, c = _position()
        me, sibling = (x, y, c), (x, y, 1 - c)
        chips = [(1 - x, y), (x, 1 - y), (1 - x, 1 - y)]

        def copy(i, k, block, to, own=False):
            nm, li = items[i]
            dst = _block_of(nm, outs[i], block, layered=False)
            return pltpu.make_async_remote_copy(
                src_ref=_src_of(ins[nm], li) if own else dst, dst_ref=dst, send_sem=send_sems.at[7 * i + k],
                recv_sem=recv_sems.at[7 * i + k], device_id=to, device_id_type=MESH)

        first = []
        for i in range(n):
            first.append(copy(i, 0, me, sibling, own=True))
            first += [copy(i, 1 + j, me, (*chip, c), own=True) for j, chip in enumerate(chips)]
        for cp in first:
            cp.start()
        passed = []
        for j, chip in enumerate(chips):
            for i in range(n):
                copy(i, 1 + j, (*chip, c), me).wait_recv()
                fwd = copy(i, 4 + j, (*chip, c), sibling)
                fwd.start()
                passed.append(fwd)
        for i in range(n):
            copy(i, 0, sibling, me).wait_recv()
        for j, chip in enumerate(chips):
            for i in range(n):
                copy(i, 4 + j, (*chip, 1 - c), me).wait_recv()
        for cp in first + passed:
            cp.wait_send()

    any_spec = pl.BlockSpec(memory_space=pl.ANY)
    return pl.pallas_call(
        body, name="gather_first_layer", out_shape=[_sds(a.shape, a.dtype) for a in lands],
        in_specs=[any_spec] * (len(srcs) + n), out_specs=[any_spec] * n,
        input_output_aliases={len(srcs) + i: i for i in range(n)},
        scratch_shapes=[pltpu.SemaphoreType.DMA((7 * n,)), pltpu.SemaphoreType.DMA((7 * n,))],
    )(*[shards[nm] for nm in srcs], *lands)


FIRST_HOP = (1, 2, 4, 6)


def _lanes(name, land_ref, pos):
    if name == 'ffn_w_down':
        return [(FIRST_HOP, land_ref.at[pl.ds(0, 2), pl.ds(0, 2 * FF_ROWS)])]
    if name in COL_WINDOW:
        return [(FIRST_HOP, land_ref.at[:, pl.ds(0, 4 * COL_WINDOW[name])])]
    if name in ('c_gate_a_w', 'c_gate_x_w'):
        return [(FIRST_HOP, land_ref.at[:, pl.ds(0, 4)])]
    return [(FIRST_HOP, land_ref.at[pl.ds(0, 4)])]


def _n_lanes(items):
    return len(items)


def _gather_forward(items, lands, name):
    n = len(items)

    def body(*refs):
        outs = refs[n:2 * n]
        send_sems, recv_sems = refs[2 * n:]
        x, y, c = _position()
        chips = [(1 - x, y), (x, 1 - y), (1 - x, 1 - y)]
        copies, arrivals = [], []
        for i, (nm, _) in enumerate(items):
            for j, chip in enumerate(chips):
                mine = _block_of(nm, outs[i], (*chip, c), layered=False)
                theirs = _block_of(nm, outs[i], (*chip, 1 - c), layered=False)
                copies.append(pltpu.make_async_remote_copy(
                    src_ref=mine, dst_ref=mine, send_sem=send_sems.at[3 * i + j], recv_sem=recv_sems.at[3 * i + j],
                    device_id=(x, y, 1 - c), device_id_type=MESH))
                arrivals.append(pltpu.make_async_remote_copy(
                    src_ref=theirs, dst_ref=theirs, send_sem=send_sems.at[3 * i + j], recv_sem=recv_sems.at[3 * i + j],
                    device_id=(x, y, 1 - c), device_id_type=MESH))
        for cp in copies:
            cp.start()
        for cp in arrivals:
            cp.wait_recv()
        for cp in copies:
            cp.wait_send()

    any_spec = pl.BlockSpec(memory_space=pl.ANY)
    return pl.pallas_call(
        body, name=name, out_shape=[_sds(a.shape, a.dtype) for a in lands],
        in_specs=[any_spec] * n, out_specs=[any_spec] * n, input_output_aliases={i: i for i in range(n)},
        scratch_shapes=[pltpu.SemaphoreType.DMA((3 * n,)), pltpu.SemaphoreType.DMA((3 * n,))],
    )(*lands)


HBM_SPEC = pl.BlockSpec(memory_space=pltpu.HBM)
SEM_SPEC = pl.BlockSpec(memory_space=pltpu.SEMAPHORE)
SIDE_EFFECT = pltpu.SideEffectType.DATAFLOW_SIDE_EFFECTING


def _gather_start(items, shards, lands, token, name):
    n = len(items)
    srcs = sorted({nm for nm, _ in items})
    ns, nl = len(srcs), _n_lanes(items)

    def body(*refs):
        ins = dict(zip(srcs, refs[:ns]))
        land_refs = refs[ns:ns + n]
        sems = refs[ns + n + 1:ns + n + 1 + 2 * nl]
        x, y, c = _position()
        me = (x, y, c)
        lane = 0
        for i, (nm, li) in enumerate(items):
            for codes, _ in _lanes(nm, land_refs[i], me):
                for k in codes:
                    peer = (1 - x if (k >> 2) & 1 else x, 1 - y if (k >> 1) & 1 else y, 1 - c if k & 1 else c)
                    pltpu.make_async_remote_copy(
                        src_ref=_src_of(ins[nm], li), dst_ref=_block_of(nm, land_refs[i], me, layered=False),
                        send_sem=sems[2 * lane], recv_sem=sems[2 * lane + 1], device_id=peer, device_id_type=MESH).start()
                lane += 1
        refs[-1][...] = jnp.zeros((8, 128), f32)

    hbm = [pltpu.with_memory_space_constraint(a, pltpu.HBM) for a in [shards[nm] for nm in srcs] + list(lands)]
    outs = pl.pallas_call(
        body, name=name,
        out_shape=[pltpu.SemaphoreType.DMA(())] * (2 * nl) + [pltpu.HBM(a.shape, a.dtype) for a in hbm] + [_sds((8, 128), f32)],
        in_specs=[HBM_SPEC] * (ns + n) + [pl.BlockSpec(memory_space=pl.ANY)],
        out_specs=[SEM_SPEC] * (2 * nl) + [HBM_SPEC] * (ns + n) + [pl.BlockSpec(memory_space=pltpu.VMEM)],
        input_output_aliases={i: 2 * nl + i for i in range(ns + n)},
        compiler_params=pltpu.CompilerParams(has_side_effects=SIDE_EFFECT),
    )(*hbm, token)
    return outs[:2 * nl], dict(zip(srcs, outs[2 * nl:2 * nl + ns])), outs[2 * nl + ns:-1], outs[-1]


def _gather_wait(items, sems, shards, lands, after, name):
    n = len(items)
    srcs = sorted(shards)
    ns, nl = len(srcs), _n_lanes(items)

    def body(*refs):
        land_refs = refs[ns:ns + n]
        sem_refs = refs[ns + n:ns + n + 2 * nl]
        x, y, c = _position()
        lane = 0
        for i, (nm, _) in enumerate(items):
            for _, moved in _lanes(nm, land_refs[i], (x, y, c)):
                cp = pltpu.make_async_remote_copy(
                    src_ref=moved, dst_ref=moved, send_sem=sem_refs[2 * lane], recv_sem=sem_refs[2 * lane + 1],
                    device_id=(x, y, 1 - c), device_id_type=MESH)
                cp.wait_send()
                cp.wait_recv()
                lane += 1

    outs = pl.pallas_call(
        body, name=name, out_shape=[pltpu.HBM(shards[nm].shape, shards[nm].dtype) for nm in srcs]
        + [pltpu.HBM(a.shape, a.dtype) for a in lands],
        in_specs=[HBM_SPEC] * (ns + n) + [SEM_SPEC] * (2 * nl) + [pl.BlockSpec(memory_space=pl.ANY)],
        out_specs=[HBM_SPEC] * (ns + n), input_output_aliases={i: i for i in range(ns + n)},
        compiler_params=pltpu.CompilerParams(has_side_effects=SIDE_EFFECT),
    )(*[shards[nm] for nm in srcs], *lands, *sems, after)
    return dict(zip(srcs, outs[:ns])), outs[ns:]


def _exchange_grads(fulls, rep):
    cpos = lax.axis_index("c").astype(jnp.int32).reshape(1)
    pair, rep_pair = _pair_exchange(fulls, rep)
    chip = {nm: _chip_sum(nm, fulls[nm], pair[nm], cpos) for nm in fulls}
    rep_chip = _add_pair(rep, rep_pair, "chip_sum_replicated")
    cross, cross_rep = _cross_exchange(chip, rep_chip)
    return chip, rep_chip, cross, cross_rep


def _pair_exchange(fulls, rep, tag=""):
    names = list(fulls)
    n = len(names)
    shard_shape = {nm: ((fulls[nm].shape[0],) + BLOCK_LAYOUT[nm][1][1:] if nm in BLOCK_LAYOUT else fulls[nm].shape[1:])
                   for nm in names}

    def body(*refs):
        ins = dict(zip(names, refs[:n]))
        rep_ref = refs[n]
        pair = dict(zip(names, refs[n + 1:2 * n + 1]))
        rpair_ref = refs[2 * n + 1]
        send_sems, recv_sems = refs[2 * n + 2:]
        x, y, c = _position()
        sibling = (x, y, 1 - c)
        remote = []
        for i, nm in enumerate(names):
            for q in range(4):
                remote.append(pltpu.make_async_remote_copy(
                    src_ref=_block_of(nm, ins[nm], (q >> 1, q & 1, 1 - c)), dst_ref=pair[nm].at[q],
                    send_sem=send_sems.at[4 * i + q], recv_sem=recv_sems.at[4 * i + q], device_id=sibling,
                    device_id_type=MESH))
        remote.append(pltpu.make_async_remote_copy(
            src_ref=rep_ref, dst_ref=rpair_ref, send_sem=send_sems.at[4 * n], recv_sem=recv_sems.at[4 * n],
            device_id=sibling, device_id_type=MESH))
        for cp in remote:
            cp.start()
        for cp in remote:
            cp.wait_recv()
        for cp in remote:
            cp.wait_send()

    any_spec = pl.BlockSpec(memory_space=pl.ANY)
    four = [_sds((4,) + tuple(shard_shape[nm]), fulls[nm].dtype) for nm in names]
    outs = pl.pallas_call(
        body, name="grad_pair_exchange" + tag, out_shape=four + [_sds(rep.shape, rep.dtype)],
        in_specs=[any_spec] * (n + 1), out_specs=[any_spec] * (n + 1),
        scratch_shapes=[pltpu.SemaphoreType.DMA((4 * n + 1,)), pltpu.SemaphoreType.DMA((4 * n + 1,))],
    )(*[fulls[nm] for nm in names], rep)
    return dict(zip(names, outs[:n])), outs[n]


def _chip_sum(name, full, pair, cpos, tag=""):
    if name in ('ab_w_in', 'c_w_in', 'ffn_w_up'):
        width = BLOCK_LAYOUT[name][1][-1]
        rows = full.shape[0] * full.shape[1]
        tr = 512

        def body(c_ref, f_ref, p_ref, o_ref):
            o_ref[0] = (f_ref[...].astype(f32) + p_ref[0].astype(f32)).astype(o_ref.dtype)

        slot = pl.BlockSpec((1, tr, width), lambda q, i, c: (q, i, 0))
        out = pl.pallas_call(
            body, name="chip_sum_" + name + tag, out_shape=_sds((4, rows, width), full.dtype),
            grid_spec=pltpu.PrefetchScalarGridSpec(
                num_scalar_prefetch=1, grid=(4, rows // tr),
                in_specs=[pl.BlockSpec((tr, width), lambda q, i, c: (i, 2 * q + c[0])), slot], out_specs=slot),
            compiler_params=_cparams(("parallel", "parallel")))(
            cpos, full.reshape(rows, N_DEV * width), pair.reshape(4, rows, width))
        return out.reshape(pair.shape)

    if name == 'ffn_w_down':
        f4, p4 = full, pair
        fspec = pl.BlockSpec((full.shape[0], 1, FF_ROWS, D_MODEL), lambda q, c: (0, q, c[0], 0))
    else:
        shard = pair.shape[1:]
        lead = int(np.prod(shard[:-2]))
        f4 = full.reshape((lead, N_DEV) + shard[-2:])
        p4 = pair.reshape((4, lead) + shard[-2:])
        fspec = pl.BlockSpec((lead, 1) + shard[-2:], lambda q, c: (0, 2 * q + c[0], 0, 0))

    def body4(c_ref, f_ref, p_ref, o_ref):
        o_ref[0] = (f_ref[:, 0].astype(f32) + p_ref[0].astype(f32)).astype(o_ref.dtype)

    slot = pl.BlockSpec((1,) + p4.shape[1:], lambda q, c: (q, 0, 0, 0))
    out = pl.pallas_call(
        body4, name="chip_sum_" + name + tag, out_shape=_sds(p4.shape, full.dtype),
        grid_spec=pltpu.PrefetchScalarGridSpec(num_scalar_prefetch=1, grid=(4,), in_specs=[fspec, slot], out_specs=slot),
        compiler_params=_cparams(("parallel",)))(cpos, f4, p4)
    return out.reshape(pair.shape)


def _add_pair(a, b, name):
    shp = a.shape
    r, c = int(np.prod(shp[:-1])), shp[-1]
    tr = _tile(r, (512, 256, 128, 64, 32, 16, 8))

    def body(a_ref, b_ref, o_ref):
        o_ref[...] = (a_ref[...].astype(f32) + b_ref[...].astype(f32)).astype(o_ref.dtype)

    tile = pl.BlockSpec((tr, c), lambda i: (i, 0))
    return pl.pallas_call(body, name=name, grid=(r // tr,), in_specs=[tile, tile], out_specs=tile,
                          out_shape=_sds((r, c), a.dtype), compiler_params=_cparams(("parallel",)))(
        a.reshape(r, c), b.reshape(r, c)).reshape(shp)


def _cross_exchange(chip, rep_chip):
    names = list(chip)
    n = len(names)

    def body(*refs):
        ins = dict(zip(names, refs[:n]))
        rep_ref = refs[n]
        outs = dict(zip(names, refs[2 * n + 2:3 * n + 2]))
        rrep_ref = refs[3 * n + 2]
        send_sems, recv_sems = refs[3 * n + 3:]
        x, y, c = _position()
        mine = 2 * x + y
        copies = []
        for k in range(1, 4):
            px, py = (1 - x if (k >> 1) & 1 else x), (1 - y if k & 1 else y)
            for i, nm in enumerate(names + ['']):
                src = rep_ref if i == n else ins[nm].at[2 * px + py]
                dst = (rrep_ref if i == n else outs[nm]).at[mine]
                copies.append(pltpu.make_async_remote_copy(
                    src_ref=src, dst_ref=dst, send_sem=send_sems.at[3 * i + k - 1], recv_sem=recv_sems.at[3 * i + k - 1],
                    device_id=(px, py, c), device_id_type=MESH))
        for cp in copies:
            cp.start()
        for cp in copies:
            cp.wait_recv()
        for cp in copies:
            cp.wait_send()

    any_spec = pl.BlockSpec(memory_space=pl.ANY)
    shapes = [_sds(chip[nm].shape, chip[nm].dtype) for nm in names] + [_sds((4,) + rep_chip.shape, rep_chip.dtype)]
    zeros = [jnp.zeros(s.shape, s.dtype) for s in shapes]
    outs = pl.pallas_call(
        body, name="grad_cross_exchange", out_shape=shapes,
        in_specs=[any_spec] * (2 * n + 2), out_specs=[any_spec] * (n + 1),
        input_output_aliases={n + 1 + i: i for i in range(n + 1)},
        scratch_shapes=[pltpu.SemaphoreType.DMA((3 * (n + 1),)), pltpu.SemaphoreType.DMA((3 * (n + 1),))],
    )(*[chip[nm] for nm in names], rep_chip, *zeros)
    return dict(zip(names, outs[:n])), outs[n]


def _cross_start(chips, name):
    n = len(chips)

    def body(*refs):
        chip_refs, land_refs = refs[:n], refs[n:2 * n]
        sems = refs[2 * n:4 * n]
        x, y, c = _position()
        mine = 2 * x + y
        for i in range(n):
            for k in range(1, 4):
                px, py = (1 - x if (k >> 1) & 1 else x), (1 - y if k & 1 else y)
                pltpu.make_async_remote_copy(
                    src_ref=chip_refs[i].at[2 * px + py], dst_ref=land_refs[i].at[mine], send_sem=sems[2 * i],
                    recv_sem=sems[2 * i + 1], device_id=(px, py, c), device_id_type=MESH).start()
        refs[-1][...] = jnp.zeros((8, 128), f32)

    hbm = [pltpu.with_memory_space_constraint(a, pltpu.HBM) for a in list(chips) + [jnp.zeros(a.shape, a.dtype) for a in chips]]
    outs = pl.pallas_call(
        body, name=name,
        out_shape=[pltpu.SemaphoreType.DMA(())] * (2 * n) + [pltpu.HBM(a.shape, a.dtype) for a in hbm] + [_sds((8, 128), f32)],
        in_specs=[HBM_SPEC] * (2 * n),
        out_specs=[SEM_SPEC] * (2 * n) + [HBM_SPEC] * (2 * n) + [pl.BlockSpec(memory_space=pltpu.VMEM)],
        input_output_aliases={i: 2 * n + i for i in range(2 * n)},
        compiler_params=pltpu.CompilerParams(has_side_effects=SIDE_EFFECT),
    )(*hbm)
    return outs[:2 * n], outs[2 * n:3 * n], outs[3 * n:4 * n], outs[4 * n]


def _cross_wait(sems, chips, lands, after, name):
    n = len(chips)

    def body(*refs):
        land_refs = refs[n:2 * n]
        sem_refs = refs[2 * n:4 * n]
        x, y, c = _position()
        for i in range(n):
            moved = land_refs[i].at[pl.ds(0, 3)]
            cp = pltpu.make_async_remote_copy(
                src_ref=moved, dst_ref=moved, send_sem=sem_refs[2 * i], recv_sem=sem_refs[2 * i + 1],
                device_id=(x, y, 1 - c), device_id_type=MESH)
            cp.wait_send()
            cp.wait_recv()

    outs = pl.pallas_call(
        body, name=name, out_shape=[pltpu.HBM(a.shape, a.dtype) for a in list(chips) + list(lands)],
        in_specs=[HBM_SPEC] * (2 * n) + [SEM_SPEC] * (2 * n) + [pl.BlockSpec(memory_space=pl.ANY)],
        out_specs=[HBM_SPEC] * (2 * n), input_output_aliases={i: i for i in range(2 * n)},
        compiler_params=pltpu.CompilerParams(has_side_effects=SIDE_EFFECT),
    )(*chips, *lands, *sems, after)
    return outs[:n], outs[n:]


def _sum_adamw_layer(parts, own, mine, w, m, v, li, prev, name):
    nl, r, l = w.shape
    lp = parts.shape[2]
    tr = r if r <= 512 else _tile(r, (512, 256, 128))
    c1 = 1.0 / (1.0 - ADAM_B1 ** ADAM_STEP)
    c2 = 1.0 / (1.0 - ADAM_B2 ** ADAM_STEP)
    k = 0 if prev is None else 4

    def body(mine_ref, p_ref, o_ref, w_ref, m_ref, v_ref, *rest):
        g_ref, d_ref, nm_ref, nv_ref = rest[k:]
        mine_v = o_ref[0].astype(f32)
        g = jnp.where(mine_ref[0] == 0, mine_v, p_ref[0].astype(f32))
        for s in range(1, 4):
            g = g + jnp.where(mine_ref[0] == s, mine_v, p_ref[s].astype(f32))
        if lp != l:
            g = g[:, :l]
        m_new = ADAM_B1 * m_ref[0] + (1.0 - ADAM_B1) * g
        v_new = ADAM_B2 * v_ref[0] + (1.0 - ADAM_B2) * (g * g)
        g_ref[0] = g
        nm_ref[0] = m_new
        nv_ref[0] = v_new
        d_ref[0] = -ADAM_LR * ((m_new * c1) / (jnp.sqrt(v_new * c2) + ADAM_EPS) + ADAM_WD * w_ref[0])

    tile = pl.BlockSpec((1, tr, l), lambda i, mn: (li, i, 0))
    keep = [pl.BlockSpec(memory_space=pl.ANY)] * k
    return pl.pallas_call(
        body, name=name, out_shape=[_sds((nl, r, l), f32)] * 4,
        grid_spec=pltpu.PrefetchScalarGridSpec(
            num_scalar_prefetch=1, grid=(r // tr,),
            in_specs=[pl.BlockSpec((4, tr, lp), lambda i, mn: (0, i, 0)), pl.BlockSpec((1, tr, lp), lambda i, mn: (mn[0], i, 0)),
                      tile, tile, tile] + keep,
            out_specs=[tile] * 4),
        input_output_aliases={6 + i: i for i in range(k)},
        compiler_params=_cparams(("parallel",)))(mine, parts, own, w, m, v, *(prev or ()))


def _sum_adamw(parts, own, mine, w, m, v, name):
    r, l = w.shape
    lp = parts.shape[2]
    tr = _tile(r, (256, 128, 64, 32, 16, 8))
    c1 = 1.0 / (1.0 - ADAM_B1 ** ADAM_STEP)
    c2 = 1.0 / (1.0 - ADAM_B2 ** ADAM_STEP)

    def body(mine_ref, p_ref, o_ref, w_ref, m_ref, v_ref, g_ref, d_ref, nm_ref, nv_ref):
        mine_v = (o_ref[0] if own.ndim == 3 else o_ref[...]).astype(f32)
        g = jnp.where(mine_ref[0] == 0, mine_v, p_ref[0].astype(f32))
        for s in range(1, parts.shape[0]):
            g = g + jnp.where(mine_ref[0] == s, mine_v, p_ref[s].astype(f32))
        if lp != l:
            g = g[:, :l]
        m_new = ADAM_B1 * m_ref[...] + (1.0 - ADAM_B1) * g
        v_new = ADAM_B2 * v_ref[...] + (1.0 - ADAM_B2) * (g * g)
        g_ref[...] = g
        nm_ref[...] = m_new
        nv_ref[...] = v_new
        d_ref[...] = -ADAM_LR * ((m_new * c1) / (jnp.sqrt(v_new * c2) + ADAM_EPS) + ADAM_WD * w_ref[...])

    tile = pl.BlockSpec((tr, l), lambda i, mn: (i, 0))
    own_spec = (pl.BlockSpec((1, tr, lp), lambda i, mn: (mn[0], i, 0)) if own.ndim == 3
                else pl.BlockSpec((tr, lp), lambda i, mn: (i, 0)))
    return pl.pallas_call(
        body, name=name, out_shape=[_sds((r, l), f32)] * 4,
        grid_spec=pltpu.PrefetchScalarGridSpec(
            num_scalar_prefetch=1, grid=(r // tr,),
            in_specs=[pl.BlockSpec((parts.shape[0], tr, lp), lambda i, mn: (0, i, 0)), own_spec, tile, tile, tile],
            out_specs=[tile] * 4),
        compiler_params=_cparams(("parallel",)))(mine, parts, own, w, m, v)


def _pack(arrs, lead=None):
    if lead is None:
        flat = jnp.concatenate([a.reshape(-1).astype(f32) for a in arrs])
        n = flat.shape[0]
    else:
        flat = jnp.concatenate([a.reshape(lead, -1).astype(f32) for a in arrs], axis=1)
        n = flat.shape[1]
    tot = -(-n // 1024) * 1024
    if lead is None:
        return jnp.pad(flat, (0, tot - n)).reshape(tot // 128, 128)
    return jnp.pad(flat, ((0, 0), (0, tot - n))).reshape(lead, tot // 128, 128)


def _unpack(packed, shapes, lead=False):
    flat = packed.reshape(packed.shape[0], -1) if lead else packed.reshape(-1)
    out, off = [], 0
    for s in shapes:
        n = int(np.prod(s))
        out.append(flat[:, off:off + n].reshape((packed.shape[0],) + tuple(s)) if lead else flat[off:off + n].reshape(s))
        off += n
    return out


def _merge_shards(g, axis):
    g = jnp.moveaxis(g, 0, axis)
    s = g.shape
    return g.reshape(s[:axis] + (s[axis] * s[axis + 1],) + s[axis + 2:])


def _split_shards(full, axis):
    s = full.shape
    g = full.reshape(s[:axis] + (N_DEV, s[axis] // N_DEV) + s[axis + 1:])
    return jnp.moveaxis(g, axis, 0)


def kernel(x, norm_mix, norm_ffn, norm_final, ab_w_in, gdn_conv_w, gdn_a_log, gdn_dt_bias, gdn_norm, hgrn_lower_bounds, hgrn_norm, ab_w_out, c_w_in, c_conv_w, c_conv_b, c_gate_a_w, c_gate_a_b, c_gate_x_w, c_gate_x_b, c_lambda, c_w_out, ffn_w_up, ffn_conv_w, ffn_conv_b, ffn_w_down, loss_target, m_norm_mix, m_norm_ffn, m_norm_final, m_ab_w_in, m_gdn_conv_w, m_gdn_a_log, m_gdn_dt_bias, m_gdn_norm, m_hgrn_lower_bounds, m_hgrn_norm, m_ab_w_out, m_c_w_in, m_c_conv_w, m_c_conv_b, m_c_gate_a_w, m_c_gate_a_b, m_c_gate_x_w, m_c_gate_x_b, m_c_lambda, m_c_w_out, m_ffn_w_up, m_ffn_conv_w, m_ffn_conv_b, m_ffn_w_down, v_norm_mix, v_norm_ffn, v_norm_final, v_ab_w_in, v_gdn_conv_w, v_gdn_a_log, v_gdn_dt_bias, v_gdn_norm, v_hgrn_lower_bounds, v_hgrn_norm, v_ab_w_out, v_c_w_in, v_c_conv_w, v_c_conv_b, v_c_gate_a_w, v_c_gate_a_b, v_c_gate_x_w, v_c_gate_x_b, v_c_lambda, v_c_w_out, v_ffn_w_up, v_ffn_conv_w, v_ffn_conv_b, v_ffn_w_down):
    wl = dict(zip(WEIGHTS, (norm_mix, norm_ffn, norm_final, ab_w_in, gdn_conv_w, gdn_a_log, gdn_dt_bias, gdn_norm, hgrn_lower_bounds, hgrn_norm, ab_w_out, c_w_in, c_conv_w, c_conv_b, c_gate_a_w, c_gate_a_b, c_gate_x_w, c_gate_x_b, c_lambda, c_w_out, ffn_w_up, ffn_conv_w, ffn_conv_b, ffn_w_down)))
    ml = dict(zip(WEIGHTS, (m_norm_mix, m_norm_ffn, m_norm_final, m_ab_w_in, m_gdn_conv_w, m_gdn_a_log, m_gdn_dt_bias, m_gdn_norm, m_hgrn_lower_bounds, m_hgrn_norm, m_ab_w_out, m_c_w_in, m_c_conv_w, m_c_conv_b, m_c_gate_a_w, m_c_gate_a_b, m_c_gate_x_w, m_c_gate_x_b, m_c_lambda, m_c_w_out, m_ffn_w_up, m_ffn_conv_w, m_ffn_conv_b, m_ffn_w_down)))
    vl = dict(zip(WEIGHTS, (v_norm_mix, v_norm_ffn, v_norm_final, v_ab_w_in, v_gdn_conv_w, v_gdn_a_log, v_gdn_dt_bias, v_gdn_norm, v_hgrn_lower_bounds, v_hgrn_norm, v_ab_w_out, v_c_w_in, v_c_conv_w, v_c_conv_b, v_c_gate_a_w, v_c_gate_a_b, v_c_gate_x_w, v_c_gate_x_b, v_c_lambda, v_c_w_out, v_ffn_w_up, v_ffn_conv_w, v_ffn_conv_b, v_ffn_w_down)))

    big = [n for n in SHARDED if n in MATMUL_WEIGHTS]
    vec = [n for n in SHARDED if n not in MATMUL_WEIGHTS]
    shards = {n: wl[n].astype(bf16) for n in big}
    shards['ab_w_in'] = jnp.pad(shards['ab_w_in'], ((0, 0), (0, 0), (0, AB_SHARD_PAD - AB_SHARD)))
    shards['ffn_w_up'] = jnp.pad(shards['ffn_w_up'], ((0, 0), (0, 0), (0, FF_PAD - FF_SHARD)))
    shards['vec'] = _pack([wl[n] for n in vec])
    pos = _position()
    layer_items = [_layer_items(l) for l in range(DEPTH)]
    lands = [[_own_land(nm, shards[nm][li], pos) for nm, li in items] for items in layer_items]
    first = _gather_now(layer_items[0] + [('vec', None)], shards, lands[0] + [_own_land('vec', shards['vec'], pos)])
    flight = {'shards': {n: shards[n] for n in big}}

    full = {n: wl[n] for n in REPLICATED}

    def start(l, token):
        sems, thru, flight['lands'], tok = _gather_start(layer_items[l], flight['shards'], lands[l], token,
                                                         "gather_start_%d" % l)
        flight['sems'] = list(sems)
        flight['shards'].update(thru)
        full['fwd', l - 1] = tok

    start(1, first[-1])

    for n, a in zip(vec, _unpack(first[-1], [wl[n].shape for n in vec], lead=True)):
        full[n] = _merge_shards(a, SHARD_AXIS[n])
    full['ffn_conv_w'] = _block_pad(full['ffn_conv_w'], 2, 4, FF_PAD)
    full['ffn_conv_b'] = _block_pad(full['ffn_conv_b'], 1, 4, FF_PAD)
    for n in big:
        full[n] = {}

    def fetch(l, x_in):
        items = layer_items[l]
        if l == 0:
            got = first[:len(items)]
        else:
            flight['shards'], got = _gather_wait(items, flight['sems'], flight['shards'], flight['lands'], x_in,
                                                 "gather_wait_%d" % l)
            got = _gather_forward(items, got, "gather_forward_%d" % l)
            if l + 1 < DEPTH:
                start(l + 1, got[0])
        for (nm, li), a in zip(items, got):
            if nm == 'ab_w_in':
                a = _ab_permute(_block_unpad(a, 1, N_DEV, AB_SHARD))
            elif nm in ('ab_w_out', 'c_w_out'):
                a = a.reshape(D_MODEL, D_MODEL)
            elif nm in ('c_gate_a_w', 'c_gate_x_w'):
                a = a.reshape(HEADS, LRU_BLOCK, LRU_BLOCK)
            elif nm == 'ffn_w_down':
                a = a.reshape(D_FFP, D_MODEL)
            full[nm][li] = a

    cpos = lax.axis_index("c").astype(jnp.int32).reshape(1)
    mine = (2 * lax.axis_index("x") + lax.axis_index("y")).astype(jnp.int32).reshape(1)
    pending = {}

    def push(l, g):
        fulls = {}
        for nm, _ in layer_items[l]:
            a = g[nm].astype(bf16)
            if nm == 'ab_w_in':
                a = _block_pad(_ab_unpermute(a), 1, N_DEV, AB_SHARD_PAD)
            fulls[nm] = a.reshape((1,) + BLOCK_LAYOUT[nm][0][1:])
        pair, _ = _pair_exchange(fulls, jnp.zeros((8, 128), f32), "_%d" % l)
        chips = [_chip_sum(nm, fulls[nm], pair[nm], cpos, "_%d" % l) for nm in fulls]
        sems, chips, crosses, tok = _cross_start(chips, "grad_cross_start_%d" % l)
        pending[l] = (sems, chips, crosses)
        full['bwd', l - 1] = tok

    loss, dx, grads = _local_step(x[0], loss_target[0], full, fetch, push)

    grads['ffn_conv_w'] = _block_unpad(grads['ffn_conv_w'], 2, 4, FF_SHARD)
    grads['ffn_conv_b'] = _block_unpad(grads['ffn_conv_b'], 1, 4, FF_SHARD)
    fulls = {'vec': _pack([_split_shards(grads[n], SHARD_AXIS[n]) for n in vec], lead=N_DEV)}
    chip, rep_chip, recv, rrep = _exchange_grads(fulls, _pack([grads[n] for n in REPLICATED]))
    res = {}
    stacked = {}
    after = full['bwd', -1]
    for l in (3, 2, 1, 0):
        sems, chips, lands = pending[l]
        chips, lands = _cross_wait(sems, chips, lands, after, "grad_cross_wait_%d" % l)
        for (n, li), own, parts in zip(layer_items[l], chips, lands):
            shp = wl[n].shape
            nl, c = shp[0], shp[-1]
            r = int(np.prod(shp[1:-1]))
            stacked[n] = _sum_adamw_layer(parts.reshape(4, r, -1), own.reshape(4, r, -1), mine, wl[n].reshape(nl, r, c),
                                          ml[n].reshape(nl, r, c), vl[n].reshape(nl, r, c), li, stacked.get(n),
                                          "adamw_%s_%d" % (n, li))
        if l == 1:
            after = stacked['ffn_w_up'][0]
    for n in big:
        for kind, o in zip(("grad", "delta", "new_m", "new_v"), stacked[n]):
            res[kind, n] = o.reshape(wl[n].shape)
    for names, parts, own, tag in ((vec, recv['vec'], chip['vec'], "adamw_vectors"),
                                   (REPLICATED, rrep, rep_chip, "adamw_replicated")):
        outs = _sum_adamw(parts, own, mine, _pack([wl[n] for n in names]), _pack([ml[n] for n in names]),
                          _pack([vl[n] for n in names]), tag)
        for kind, o in zip(("grad", "delta", "new_m", "new_v"), outs):
            for n, a in zip(names, _unpack(o, [wl[n].shape for n in names])):
                res[kind, n] = a

    loss = lax.psum(loss, ("x", "y", "c"))
    return (loss, dx[None], *[res[kind, n] for kind in ("grad", "delta", "new_m", "new_v") for n in WEIGHTS])
```

```python
import functools

import numpy as np
import jax
import jax.numpy as jnp
from jax import lax
from jax.experimental import pallas as pl
from jax.experimental.pallas import tpu as pltpu

f32 = jnp.float32
bf16 = jnp.bfloat16
HI = lax.Precision.HIGHEST
MESH = pl.DeviceIdType.MESH

N_DEV = 8
D_MODEL = 1024
DEPTH = 4
EPS = 1e-6
F_FLOOR = 1e-30
HEADS = 4
HEAD_DIM = 128
GDN_WIDTH = 512
GDN_CONV = 4
GDN_CHUNK = 64
HGRN_CHUNK = 16
HGRN_STEP = 128
MIX_WIDTH = 1024
AB_COLS = 4104
AB_PAD = 4224
LRU_WIDTH = 1024
LRU_BLOCK = 256
LRU_CONV = 4
RG_C = 8.0
D_FF = 2816
FF_SHARD = 704
FF_PAD = 768
D_FFP = 4 * FF_PAD
FF_ROWS = 352
AB_SHARD, AB_SHARD_PAD = 513, 640
FFN_CONV = 3
ADAM_LR, ADAM_B1, ADAM_B2, ADAM_EPS, ADAM_WD, ADAM_STEP = 0.001, 0.9, 0.999, 1e-08, 0.01, 10
VMEM_LIMIT = 56 * 1024 * 1024
PACK_LANES = 512
PACK_ROWS = 256

OFF_Q, OFF_K, OFF_V, OFF_Z, OFF_QB, OFF_FB, OFF_IB, OFF_GB, OFF_BA = 0, 512, 1024, 1536, 2048, 2560, 3072, 3584, 4096

WEIGHTS = ['norm_mix', 'norm_ffn', 'norm_final', 'ab_w_in', 'gdn_conv_w', 'gdn_a_log', 'gdn_dt_bias', 'gdn_norm',
           'hgrn_lower_bounds', 'hgrn_norm', 'ab_w_out', 'c_w_in', 'c_conv_w', 'c_conv_b', 'c_gate_a_w', 'c_gate_a_b',
           'c_gate_x_w', 'c_gate_x_b', 'c_lambda', 'c_w_out', 'ffn_w_up', 'ffn_conv_w', 'ffn_conv_b', 'ffn_w_down']
SHARD_AXIS = {'norm_mix': None, 'norm_ffn': None, 'norm_final': None, 'ab_w_in': 2, 'gdn_conv_w': 2, 'gdn_a_log': None,
              'gdn_dt_bias': None, 'gdn_norm': None, 'hgrn_lower_bounds': None, 'hgrn_norm': None, 'ab_w_out': 1,
              'c_w_in': 2, 'c_conv_w': 2, 'c_conv_b': 1, 'c_gate_a_w': 2, 'c_gate_a_b': 1, 'c_gate_x_w': 2,
              'c_gate_x_b': 1, 'c_lambda': 1, 'c_w_out': 1, 'ffn_w_up': 2, 'ffn_conv_w': 2, 'ffn_conv_b': None,
              'ffn_w_down': 1}
MATMUL_WEIGHTS = ('ab_w_in', 'ab_w_out', 'c_w_in', 'c_gate_a_w', 'c_gate_x_w', 'c_w_out', 'ffn_w_up', 'ffn_w_down')
SHARDED = [n for n in WEIGHTS if SHARD_AXIS[n] is not None]
REPLICATED = [n for n in WEIGHTS if SHARD_AXIS[n] is None]


def _tile(n, prefs=(512, 384, 256, 128)):
    for p in prefs:
        if n % p == 0:
            return p
    return n


def _cparams(sem=None):
    kw = dict(vmem_limit_bytes=VMEM_LIMIT)
    if sem is not None:
        kw['dimension_semantics'] = sem
    return pltpu.CompilerParams(**kw)


def _sds(shape, dtype):
    return jax.ShapeDtypeStruct(tuple(shape), dtype)


def _sigmoid(x):
    return 1.0 / (1.0 + jnp.exp(-x))


def _silu(x):
    return x * _sigmoid(x)


def _log1p(x):
    u = 1.0 + x
    return jnp.where(u == 1.0, x, jnp.log(u) * (x / jnp.where(u == 1.0, 1.0, u - 1.0)))


def _softplus(x):
    return jnp.maximum(x, 0.0) + _log1p(jnp.exp(-jnp.abs(x)))


def _expm1(x):
    small = jnp.abs(x) < 0.05
    xs = jnp.where(small, x, 0.0)
    series = xs * (1.0 + xs * (0.5 + xs * (1.0 / 6.0 + xs * (1.0 / 24.0 + xs * (1.0 / 120.0)))))
    return jnp.where(small, series, jnp.exp(x) - 1.0)


def _gelu(x):
    return 0.5 * x * (1.0 + jnp.tanh(0.7978845608028654 * (x + 0.044715 * x * x * x)))


def _rms(x, gain):
    return x * lax.rsqrt(jnp.mean(x * x, axis=-1, keepdims=True) + EPS) * gain


def _dot(a, b, dims=((1,), (0,)), precision=None):
    return lax.dot_general(a, b, (dims, ((), ())), precision=precision, preferred_element_type=f32)


def _bdot(a, b, dims=((1,), (0,))):
    return _dot(a.astype(bf16), b.astype(bf16), dims)


NT = ((1,), (1,))
TN = ((0,), (0,))


def _shift_down(x, k):
    if k == 0:
        return x
    row = lax.broadcasted_iota(jnp.int32, x.shape, 0)
    return jnp.where(row >= k, pltpu.roll(x, k, 0), 0.0)


def _shift_up(x, k, fill=0.0):
    if k == 0:
        return x
    n = x.shape[0]
    row = lax.broadcasted_iota(jnp.int32, x.shape, 0)
    return jnp.where(row < n - k, pltpu.roll(x, n - k, 0), fill)


def _conv_fwd(x, w_ref, width):
    acc = w_ref[width - 1:width, :] * x
    for k in range(width - 1):
        acc = acc + w_ref[k:k + 1, :] * _shift_down(x, width - 1 - k)
    return acc


def _conv_bwd(x, dout, w_ref, dw_ref, width):
    dx = w_ref[width - 1:width, :] * dout
    dw_ref[width - 1:width, :] = jnp.sum(dout * x, axis=0, keepdims=True)
    for k in range(width - 1):
        s = width - 1 - k
        dx = dx + w_ref[k:k + 1, :] * _shift_up(dout, s)
        dw_ref[k:k + 1, :] = jnp.sum(dout * _shift_down(x, s), axis=0, keepdims=True)
    return dx


MM_VMEM_BUDGET = 36 * 1024 * 1024
MM_MAX_TILE = 1024 * 1024


def _mm_tiles(m, n, k, out_bytes):
    best = None
    for tm in (1024, 512, 384, 256, 128):
        if m % tm:
            continue
        for tn in range(1536, 0, -128):
            if n % tn or tm * tn > MM_MAX_TILE:
                continue
            score = (tm * tn, min(tm, tn))
            if 2 * (tm * k * 2 + k * tn * 2 + tm * tn * out_bytes) <= MM_VMEM_BUDGET and (best is None or score > best[0]):
                best = (score, tm, tn)
    return (best[1], best[2]) if best else (_tile(m), _tile(n))


def _mm(a, b, *, ta=False, tb=False, add=None, out_dtype=f32, name):
    m, k = (a.shape[1], a.shape[0]) if ta else a.shape
    n = b.shape[0] if tb else b.shape[1]
    tm, tn = _mm_tiles(m, n, k, jnp.dtype(out_dtype).itemsize + (4 if add is not None else 0))
    dims = ((0 if ta else 1,), (1 if tb else 0,))

    def body(*refs):
        a_ref, b_ref = refs[0], refs[1]
        o_ref = refs[-1]
        r = _dot(a_ref[...], b_ref[...], dims)
        if add is not None:
            r = r + refs[2][...]
        o_ref[...] = r.astype(out_dtype)

    a_spec = pl.BlockSpec((k, tm), lambda j, i: (0, i)) if ta else pl.BlockSpec((tm, k), lambda j, i: (i, 0))
    b_spec = pl.BlockSpec((tn, k), lambda j, i: (j, 0)) if tb else pl.BlockSpec((k, tn), lambda j, i: (0, j))
    o_spec = pl.BlockSpec((tm, tn), lambda j, i: (i, j))
    ins, specs = [a, b], [a_spec, b_spec]
    if add is not None:
        ins.append(add)
        specs.append(o_spec)
    return pl.pallas_call(body, name=name, grid=(n // tn, m // tm), in_specs=specs, out_specs=o_spec,
                          out_shape=_sds((m, n), out_dtype), compiler_params=_cparams(("parallel", "parallel")))(*ins)


def _rms_fwd(x, gain, name):
    t, d = x.shape
    tr = _tile(t, (256, 128))

    def body(x_ref, g_ref, h_ref):
        h_ref[...] = _rms(x_ref[...], g_ref[...]).astype(bf16)

    return pl.pallas_call(body, name=name, grid=(t // tr,),
                          in_specs=[pl.BlockSpec((tr, d), lambda i: (i, 0)), pl.BlockSpec((1, d), lambda i: (0, 0))],
                          out_specs=pl.BlockSpec((tr, d), lambda i: (i, 0)), out_shape=_sds((t, d), bf16),
                          compiler_params=_cparams(("parallel",)))(x, gain)


def _rms_bwd(x, gain, dh, dres, name):
    t, d = x.shape
    tr = _tile(t, (256, 128))

    def body(x_ref, g_ref, dh_ref, dres_ref, dx_ref, dxb_ref, dg_ref):
        _, vjp = jax.vjp(_rms, x_ref[...], g_ref[...])
        dx, dg = vjp(dh_ref[...])
        dx = dx + dres_ref[...]
        dx_ref[...] = dx
        dxb_ref[...] = dx.astype(bf16)

        @pl.when(pl.program_id(0) == 0)
        def _():
            dg_ref[...] = jnp.zeros_like(dg_ref)

        dg_ref[...] += dg

    row = pl.BlockSpec((tr, d), lambda i: (i, 0))
    vec = pl.BlockSpec((1, d), lambda i: (0, 0))
    return pl.pallas_call(body, name=name, grid=(t // tr,), in_specs=[row, vec, row, row], out_specs=[row, row, vec],
                          out_shape=[_sds((t, d), f32), _sds((t, d), bf16), _sds((1, d), f32)],
                          compiler_params=_cparams(("arbitrary",)))(x, gain, dh, dres)


def _loss_head(x, gain, target):
    t, d = x.shape
    tr = _tile(t, (256, 128))

    def f(xv, g, tgt):
        err = _rms(xv, g) - tgt
        return 0.5 * jnp.sum(jnp.mean(err * err, axis=-1, keepdims=True), axis=0, keepdims=True)

    def body(x_ref, g_ref, t_ref, loss_ref, dx_ref, dxb_ref, dg_ref):
        loss, vjp = jax.vjp(lambda xv, g: f(xv, g, t_ref[...]), x_ref[...], g_ref[...])
        dx, dg = vjp(jnp.ones((1, 1), f32))
        dx_ref[...] = dx
        dxb_ref[...] = dx.astype(bf16)

        @pl.when(pl.program_id(0) == 0)
        def _():
            dg_ref[...] = jnp.zeros_like(dg_ref)
            loss_ref[...] = jnp.zeros_like(loss_ref)

        dg_ref[...] += dg
        loss_ref[...] += jnp.broadcast_to(loss, loss_ref.shape)

    row = pl.BlockSpec((tr, d), lambda i: (i, 0))
    vec = pl.BlockSpec((1, d), lambda i: (0, 0))
    one = pl.BlockSpec((8, 128), lambda i: (0, 0))
    return pl.pallas_call(body, name="loss_head", grid=(t // tr,), in_specs=[row, vec, row],
                          out_specs=[one, row, row, vec],
                          out_shape=[_sds((8, 128), f32), _sds((t, d), f32), _sds((t, d), bf16), _sds((1, d), f32)],
                          compiler_params=_cparams(("arbitrary",)))(x, gain, target)


def _ffn_act_fwd(u, conv_w, conv_b):
    t = u.shape[0]
    tc = FF_PAD // 2
    nb = D_FFP // tc

    def body(g_ref, v_ref, w_ref, b_ref, a_ref):
        gc = _conv_fwd(g_ref[...], w_ref, FFN_CONV) + b_ref[...]
        a_ref[...] = (_silu(gc) * v_ref[...]).astype(bf16)

    return pl.pallas_call(
        body, name="ffn_act_fwd", grid=(nb,),
        in_specs=[pl.BlockSpec((t, tc), lambda j: (0, j)), pl.BlockSpec((t, tc), lambda j: (0, j + nb)),
                  pl.BlockSpec((FFN_CONV, tc), lambda j: (0, j)), pl.BlockSpec((1, tc), lambda j: (0, j))],
        out_specs=pl.BlockSpec((t, tc), lambda j: (0, j)), out_shape=_sds((t, D_FFP), bf16),
        compiler_params=_cparams(("parallel",)))(u, u, conv_w, conv_b)


def _ffn_act_bwd(u, conv_w, conv_b, da):
    t = u.shape[0]
    tc = FF_PAD // 2
    nb = D_FFP // tc

    def act(gc, val):
        return _silu(gc) * val

    def body(g_ref, v_ref, w_ref, b_ref, da_ref, a_ref, dg_ref, dv_ref, dw_ref, db_ref):
        gp = g_ref[...]
        gc = _conv_fwd(gp, w_ref, FFN_CONV) + b_ref[...]
        a, vjp = jax.vjp(act, gc, v_ref[...])
        dgc, dval = vjp(da_ref[...])
        a_ref[...] = a.astype(bf16)
        dv_ref[...] = dval.astype(bf16)
        db_ref[...] = jnp.sum(dgc, axis=0, keepdims=True)
        dg_ref[...] = _conv_bwd(gp, dgc, w_ref, dw_ref, FFN_CONV).astype(bf16)

    col = pl.BlockSpec((t, tc), lambda j: (0, j))
    return pl.pallas_call(
        body, name="ffn_act_bwd", grid=(nb,),
        in_specs=[col, pl.BlockSpec((t, tc), lambda j: (0, j + nb)), pl.BlockSpec((FFN_CONV, tc), lambda j: (0, j)),
                  pl.BlockSpec((1, tc), lambda j: (0, j)), col],
        out_specs=[col, col, col, pl.BlockSpec((FFN_CONV, tc), lambda j: (0, j)), pl.BlockSpec((1, tc), lambda j: (0, j))],
        out_shape=[_sds((t, D_FFP), bf16), _sds((t, D_FFP), bf16), _sds((t, D_FFP), bf16), _sds((FFN_CONV, D_FFP), f32),
                   _sds((1, D_FFP), f32)],
        compiler_params=_cparams(("parallel",)))(u, u, conv_w, conv_b, da)


def _lru_gates(xc, ra, ia, lam):
    r = _sigmoid(ra)
    i = _sigmoid(ia)
    log_a = -RG_C * r * _softplus(-lam)
    a = jnp.exp(log_a)
    u = jnp.sqrt(jnp.maximum(-_expm1(2.0 * log_a), 0.0)) * (i * xc)
    return a, u


def _lin_scan(a, u):
    n = a.shape[0]
    row = lax.broadcasted_iota(jnp.int32, a.shape, 0)
    s = 1
    while s < n:
        keep = row >= s
        u = a * jnp.where(keep, pltpu.roll(u, s, 0), 0.0) + u
        a = a * jnp.where(keep, pltpu.roll(a, s, 0), 1.0)
        s *= 2
    return u


def _rev_scan(a_next, d):
    n = d.shape[0]
    row = lax.broadcasted_iota(jnp.int32, d.shape, 0)
    a = a_next
    s = 1
    while s < n:
        keep = row < n - s
        d = a * jnp.where(keep, pltpu.roll(d, n - s, 0), 0.0) + d
        a = a * jnp.where(keep, pltpu.roll(a, n - s, 0), 1.0)
        s *= 2
    return d


def _col_conv_fwd(p, col_off, conv_w, conv_b, width, tc, name):
    t = p.shape[0]
    c = conv_w.shape[1]
    ob = col_off // tc

    def body(x_ref, w_ref, b_ref, o_ref):
        o_ref[...] = _conv_fwd(x_ref[...], w_ref, width) + b_ref[...]

    return pl.pallas_call(
        body, name=name, grid=(c // tc,),
        in_specs=[pl.BlockSpec((t, tc), lambda j: (0, j + ob)), pl.BlockSpec((width, tc), lambda j: (0, j)),
                  pl.BlockSpec((1, tc), lambda j: (0, j))],
        out_specs=pl.BlockSpec((t, tc), lambda j: (0, j)), out_shape=_sds((t, c), f32),
        compiler_params=_cparams(("parallel",)))(p, conv_w, conv_b)


def _col_conv_bwd(p, col_off, conv_w, dxc, width, tc, name):
    t = p.shape[0]
    c = conv_w.shape[1]
    ob = col_off // tc

    def body(x_ref, w_ref, d_ref, dx_ref, dw_ref, db_ref):
        d = d_ref[...]
        db_ref[...] = jnp.sum(d, axis=0, keepdims=True)
        dx_ref[...] = _conv_bwd(x_ref[...], d, w_ref, dw_ref, width).astype(bf16)

    col = pl.BlockSpec((t, tc), lambda j: (0, j))
    return pl.pallas_call(
        body, name=name, grid=(c // tc,),
        in_specs=[pl.BlockSpec((t, tc), lambda j: (0, j + ob)), pl.BlockSpec((width, tc), lambda j: (0, j)), col],
        out_specs=[col, pl.BlockSpec((width, tc), lambda j: (0, j)), pl.BlockSpec((1, tc), lambda j: (0, j))],
        out_shape=[_sds((t, c), bf16), _sds((width, c), f32), _sds((1, c), f32)],
        compiler_params=_cparams(("parallel",)))(p, conv_w, dxc)


def _lru_fwd(p, xc, wa, ba, wx, bx, lam):
    t = p.shape[0]
    bw = LRU_BLOCK

    def body(y_ref, xc_ref, wa_ref, ba_ref, wx_ref, bx_ref, lam_ref, out_ref, hs_ref, a_ref):
        xc_v = xc_ref[...]
        xb = xc_v.astype(bf16)
        ra = _dot(xb, wa_ref[0]) + ba_ref[...]
        ia = _dot(xb, wx_ref[0]) + bx_ref[...]
        a, u = _lru_gates(xc_v, ra, ia, lam_ref[...])
        a_ref[...] = a
        hs = _lin_scan(a, u)
        hs_ref[...] = hs
        out_ref[...] = (hs * _gelu(y_ref[...])).astype(bf16)

    col = pl.BlockSpec((t, bw), lambda h: (0, h))
    vec = pl.BlockSpec((1, bw), lambda h: (0, h))
    mat = pl.BlockSpec((1, bw, bw), lambda h: (h, 0, 0))
    return pl.pallas_call(
        body, name="lru_fwd", grid=(HEADS,), in_specs=[col, col, mat, vec, mat, vec, vec], out_specs=[col, col, col],
        out_shape=[_sds((t, LRU_WIDTH), bf16), _sds((t, LRU_WIDTH), f32), _sds((t, LRU_WIDTH), f32)],
        compiler_params=_cparams(("parallel",)))(p, xc, wa, ba, wx, bx, lam)


def _lru_bwd_scan(p, a, hs, dout):
    t = p.shape[0]
    bw = LRU_BLOCK

    def body(y_ref, a_ref, hs_ref, do_ref, dy_ref, da_ref, du_ref):
        hs_v = hs_ref[...]
        do = do_ref[...]
        gate, vjp = jax.vjp(_gelu, y_ref[...])
        dy_ref[...] = vjp(do * hs_v)[0].astype(bf16)
        g = _rev_scan(_shift_up(a_ref[...], 1), do * gate)
        du_ref[...] = g
        da_ref[...] = g * _shift_down(hs_v, 1)

    col = pl.BlockSpec((t, bw), lambda h: (0, h))
    return pl.pallas_call(
        body, name="lru_bwd_scan", grid=(HEADS,), in_specs=[col, col, col, col], out_specs=[col, col, col],
        out_shape=[_sds((t, LRU_WIDTH), bf16), _sds((t, LRU_WIDTH), f32), _sds((t, LRU_WIDTH), f32)],
        compiler_params=_cparams(("parallel",)))(p, a, hs, dout)


def _lru_bwd_gates(xc, da, du, wa, ba, wx, bx, lam):
    t = xc.shape[0]
    bw = LRU_BLOCK
    tr = _tile(t, (512, 256, 128))

    def body(xc_ref, da_ref, du_ref, wa_ref, ba_ref, wx_ref, bx_ref, lam_ref,
             dxc_ref, dwa_ref, dwx_ref, dba_ref, dbx_ref, dlam_ref):
        xc_v = xc_ref[...]
        xb = xc_v.astype(bf16)
        ra = _dot(xb, wa_ref[0]) + ba_ref[...]
        ia = _dot(xb, wx_ref[0]) + bx_ref[...]
        _, vjp = jax.vjp(_lru_gates, xc_v, ra, ia, lam_ref[...])
        dxc, dra, dia, dlam = vjp((da_ref[...], du_ref[...]))
        drb, dib = dra.astype(bf16), dia.astype(bf16)
        dxc_ref[...] = dxc + _dot(drb, wa_ref[0], NT) + _dot(dib, wx_ref[0], NT)

        @pl.when(pl.program_id(1) == 0)
        def _():
            dwa_ref[...] = jnp.zeros_like(dwa_ref)
            dwx_ref[...] = jnp.zeros_like(dwx_ref)
            dba_ref[...] = jnp.zeros_like(dba_ref)
            dbx_ref[...] = jnp.zeros_like(dbx_ref)
            dlam_ref[...] = jnp.zeros_like(dlam_ref)

        dwa_ref[0] += _dot(xb, drb, TN)
        dwx_ref[0] += _dot(xb, dib, TN)
        dba_ref[...] += jnp.sum(dra, axis=0, keepdims=True)
        dbx_ref[...] += jnp.sum(dia, axis=0, keepdims=True)
        dlam_ref[...] += dlam

    tile = pl.BlockSpec((tr, bw), lambda h, i: (i, h))
    vec = pl.BlockSpec((1, bw), lambda h, i: (0, h))
    mat = pl.BlockSpec((1, bw, bw), lambda h, i: (h, 0, 0))
    return pl.pallas_call(
        body, name="lru_bwd_gates", grid=(HEADS, t // tr), in_specs=[tile, tile, tile, mat, vec, mat, vec, vec],
        out_specs=[tile, mat, mat, vec, vec, vec],
        out_shape=[_sds((t, LRU_WIDTH), f32), _sds((HEADS, bw, bw), f32), _sds((HEADS, bw, bw), f32),
                   _sds((1, LRU_WIDTH), f32), _sds((1, LRU_WIDTH), f32), _sds((1, LRU_WIDTH), f32)],
        compiler_params=_cparams(("parallel", "arbitrary")))(xc, da, du, wa, ba, wx, bx, lam)


def _gdn_pre_fn(cq, ck, cv, ba, alog, dtb, h):
    q, k, v = _silu(cq), _silu(ck), _silu(cv)
    q = q * lax.rsqrt(jnp.sum(q * q, axis=-1, keepdims=True) + EPS) * (HEAD_DIM ** -0.5)
    k = k * lax.rsqrt(jnp.sum(k * k, axis=-1, keepdims=True) + EPS)
    lane = lax.broadcasted_iota(jnp.int32, (1, HEAD_DIM), 1)
    mb = (lane == h).astype(f32)
    ma = (lane == HEADS + h).astype(f32)
    beta_raw = jnp.sum(ba * mb, axis=-1, keepdims=True)
    alpha = jnp.sum(ba * ma, axis=-1, keepdims=True)
    al = jnp.sum(alog * mb, axis=-1, keepdims=True)
    db = jnp.sum(dtb * mb, axis=-1, keepdims=True)
    beta = _sigmoid(beta_raw)
    g = -jnp.exp(al) * _softplus(alpha + db)
    return q, k, v, jnp.broadcast_to(beta, q.shape), jnp.broadcast_to(g, q.shape)


def _gdn_pre_fwd(p, conv_w, alog, dtb):
    t = p.shape[0]
    hd = HEAD_DIM

    def body(pq_ref, pk_ref, pv_ref, ba_ref, wq_ref, wk_ref, wv_ref, al_ref, dt_ref, q_ref, k_ref, v_ref, b_ref, g_ref):
        h = pl.program_id(0)
        cq = _conv_fwd(pq_ref[...], wq_ref, GDN_CONV)
        ck = _conv_fwd(pk_ref[...], wk_ref, GDN_CONV)
        cv = _conv_fwd(pv_ref[...], wv_ref, GDN_CONV)
        q, k, v, be, ge = _gdn_pre_fn(cq, ck, cv, ba_ref[...], al_ref[...], dt_ref[...], h)
        q_ref[...], k_ref[...], v_ref[...], b_ref[...], g_ref[...] = q, k, v, be, ge

    def pcol(off):
        return pl.BlockSpec((t, hd), lambda h: (0, h + off // hd))

    def wcol(off):
        return pl.BlockSpec((GDN_CONV, hd), lambda h: (0, h + off // hd))

    vec = pl.BlockSpec((1, hd), lambda h: (0, 0))
    out = pl.BlockSpec((t, hd), lambda h: (0, h))
    return pl.pallas_call(
        body, name="gdn_pre_fwd", grid=(HEADS,),
        in_specs=[pcol(OFF_Q), pcol(OFF_K), pcol(OFF_V), pl.BlockSpec((t, hd), lambda h: (0, OFF_BA // hd)),
                  wcol(0), wcol(GDN_WIDTH), wcol(2 * GDN_WIDTH), vec, vec],
        out_specs=[out] * 5, out_shape=[_sds((t, GDN_WIDTH), f32)] * 5,
        compiler_params=_cparams(("parallel",)))(p, p, p, p, conv_w, conv_w, conv_w, alog, dtb)


def _gdn_pre_bwd(p, conv_w, alog, dtb, dq, dk, dv, dbe, dge):
    t = p.shape[0]
    hd = HEAD_DIM

    def body(pq_ref, pk_ref, pv_ref, ba_ref, wq_ref, wk_ref, wv_ref, al_ref, dt_ref,
             dq_ref, dk_ref, dv_ref, dbe_ref, dge_ref,
             opq_ref, opk_ref, opv_ref, dba_ref, dwq_ref, dwk_ref, dwv_ref, dal_ref, ddt_ref):
        h = pl.program_id(0)
        pq, pk, pv = pq_ref[...], pk_ref[...], pv_ref[...]
        cq = _conv_fwd(pq, wq_ref, GDN_CONV)
        ck = _conv_fwd(pk, wk_ref, GDN_CONV)
        cv = _conv_fwd(pv, wv_ref, GDN_CONV)
        _, vjp = jax.vjp(functools.partial(_gdn_pre_fn, h=h), cq, ck, cv, ba_ref[...], al_ref[...], dt_ref[...])
        dcq, dck, dcv, dba, dal, ddt = vjp((dq_ref[...], dk_ref[...], dv_ref[...], dbe_ref[...], dge_ref[...]))
        opq_ref[...] = _conv_bwd(pq, dcq, wq_ref, dwq_ref, GDN_CONV).astype(bf16)
        opk_ref[...] = _conv_bwd(pk, dck, wk_ref, dwk_ref, GDN_CONV).astype(bf16)
        opv_ref[...] = _conv_bwd(pv, dcv, wv_ref, dwv_ref, GDN_CONV).astype(bf16)

        @pl.when(h == 0)
        def _():
            dba_ref[...] = jnp.zeros_like(dba_ref)
            dal_ref[...] = jnp.zeros_like(dal_ref)
            ddt_ref[...] = jnp.zeros_like(ddt_ref)

        dba_ref[...] += dba
        dal_ref[...] += dal
        ddt_ref[...] += ddt

    def pcol(off):
        return pl.BlockSpec((t, hd), lambda h: (0, h + off // hd))

    def wcol(off):
        return pl.BlockSpec((GDN_CONV, hd), lambda h: (0, h + off // hd))

    vec = pl.BlockSpec((1, hd), lambda h: (0, 0))
    col = pl.BlockSpec((t, hd), lambda h: (0, h))
    full = pl.BlockSpec((t, hd), lambda h: (0, 0))
    wout = pl.BlockSpec((GDN_CONV, hd), lambda h: (0, h))
    return pl.pallas_call(
        body, name="gdn_pre_bwd", grid=(HEADS,),
        in_specs=[pcol(OFF_Q), pcol(OFF_K), pcol(OFF_V), pl.BlockSpec((t, hd), lambda h: (0, OFF_BA // hd)),
                  wcol(0), wcol(GDN_WIDTH), wcol(2 * GDN_WIDTH), vec, vec, col, col, col, col, col],
        out_specs=[col, col, col, full, wout, wout, wout, vec, vec],
        out_shape=[_sds((t, GDN_WIDTH), bf16)] * 3 + [_sds((t, hd), f32)] + [_sds((GDN_CONV, GDN_WIDTH), f32)] * 3
        + [_sds((1, hd), f32)] * 2,
        compiler_params=_cparams(("arbitrary",)))(p, p, p, p, conv_w, conv_w, conv_w, alog, dtb, dq, dk, dv, dbe, dge)


BNN = (((2,), (1,)), ((0,), (0,)))
BNT = (((2,), (2,)), ((0,), (0,)))
BTN = (((1,), (1,)), ((0,), (0,)))


def _hdot(a, b, dn=BNN, precision=None):
    return lax.dot_general(a, b, dn, precision=precision, preferred_element_type=f32)


def _hbdot(a, b, dn=BNN):
    return _hdot(a.astype(bf16), b.astype(bf16), dn)


def _tri_inverse(a):
    c = a.shape[-1]
    r = lax.broadcasted_iota(jnp.int32, (c, c), 0)
    col = lax.broadcasted_iota(jnp.int32, (c, c), 1)
    m = -a
    inv = jnp.where(r == col, 1.0, 0.0) + m
    s = 2
    while s < c:
        m = _hdot(m, m, precision=HI)
        inv = inv + _hdot(inv, m, precision=HI)
        s *= 2
    return inv


def _gdn_chunk(s, q, k, v, ge, be):
    nh, c, _ = q.shape
    r = lax.broadcasted_iota(jnp.int32, (c, c), 0)
    col = lax.broadcasted_iota(jnp.int32, (c, c), 1)
    causal = r >= col
    tri = jnp.broadcast_to(causal.astype(f32), (nh, c, c))
    gc = _hdot(tri, ge, precision=HI)
    gcc = gc[:, :, :c]
    gcr = jnp.swapaxes(gc, 1, 2)[:, :c, :]
    decay = jnp.where(causal, jnp.exp(jnp.where(causal, gcc - gcr, 0.0)), 0.0)
    kb = k * be
    lower = jnp.where(r > col, _hbdot(kb, k, BNT) * decay, 0.0)
    tinv = _tri_inverse(lower)
    egc = jnp.exp(gc)
    u = _hdot(tinv, v * be, precision=HI)
    w = _hdot(tinv, kb * egc, precision=HI)
    attn = _hbdot(q, k, BNT) * decay
    gl = gc[:, c - 1:c, :]
    v_new = u - _hbdot(w, s)
    o = _hbdot(q * egc, s) + _hbdot(attn, v_new)
    s_new = s * jnp.exp(gl) + _hbdot(k * jnp.exp(gl - gc), v_new, BTN)
    return o, s_new


def _heads_major(ref):
    return jnp.stack([ref[:, h * HEAD_DIM:(h + 1) * HEAD_DIM] for h in range(HEADS)])


def _gdn_core_fwd(q, k, v, ge, be):
    t = q.shape[0]
    c, hd = GDN_CHUNK, HEAD_DIM
    n = t // c

    def body(q_ref, k_ref, v_ref, g_ref, b_ref, o_ref, st_ref, s_ref):
        @pl.when(pl.program_id(0) == 0)
        def _():
            s_ref[...] = jnp.zeros_like(s_ref)

        s = s_ref[...]
        st_ref[:, 0] = s
        o, s_new = _gdn_chunk(s, *[_heads_major(r) for r in (q_ref, k_ref, v_ref, g_ref, b_ref)])
        for h in range(HEADS):
            o_ref[:, h * hd:(h + 1) * hd] = o[h]
        s_ref[...] = s_new

    tile = pl.BlockSpec((c, GDN_WIDTH), lambda i: (i, 0))
    return pl.pallas_call(
        body, name="gdn_core_fwd", grid=(n,), in_specs=[tile] * 5,
        out_specs=[tile, pl.BlockSpec((HEADS, 1, hd, hd), lambda i: (0, i, 0, 0))],
        out_shape=[_sds((t, GDN_WIDTH), f32), _sds((HEADS, n, hd, hd), f32)],
        scratch_shapes=[pltpu.VMEM((HEADS, hd, hd), f32)],
        compiler_params=_cparams(("arbitrary",)))(q, k, v, ge, be)


def _gdn_core_bwd(q, k, v, ge, be, states, do):
    t = q.shape[0]
    c, hd = GDN_CHUNK, HEAD_DIM
    n = t // c

    def body(q_ref, k_ref, v_ref, g_ref, b_ref, st_ref, do_ref, dq_ref, dk_ref, dv_ref, dg_ref, db_ref, ds_ref):
        @pl.when(pl.program_id(0) == 0)
        def _():
            ds_ref[...] = jnp.zeros_like(ds_ref)

        _, vjp = jax.vjp(_gdn_chunk, st_ref[:, 0], *[_heads_major(r) for r in (q_ref, k_ref, v_ref, g_ref, b_ref)])
        ds, *dins = vjp((_heads_major(do_ref), ds_ref[...]))
        ds_ref[...] = ds
        for d_ref, d in zip((dq_ref, dk_ref, dv_ref, dg_ref, db_ref), dins):
            for h in range(HEADS):
                d_ref[:, h * hd:(h + 1) * hd] = d[h]

    tile = pl.BlockSpec((c, GDN_WIDTH), lambda i: (n - 1 - i, 0))
    return pl.pallas_call(
        body, name="gdn_core_bwd", grid=(n,),
        in_specs=[tile] * 5 + [pl.BlockSpec((HEADS, 1, hd, hd), lambda i: (0, n - 1 - i, 0, 0)), tile],
        out_specs=[tile] * 5, out_shape=[_sds((t, GDN_WIDTH), f32)] * 5,
        scratch_shapes=[pltpu.VMEM((HEADS, hd, hd), f32)],
        compiler_params=_cparams(("arbitrary",)))(q, k, v, ge, be, states, do)


def _post_fn(o, z, gain):
    return _rms(o, gain) * _silu(z)


def _post_fwd(o, p, z_off, gain, name):
    t = o.shape[0]
    hd = HEAD_DIM

    def body(o_ref, z_ref, g_ref, y_ref):
        y_ref[...] = _post_fn(o_ref[...], z_ref[...], g_ref[...]).astype(bf16)

    col = pl.BlockSpec((t, hd), lambda h: (0, h))
    return pl.pallas_call(
        body, name=name, grid=(HEADS,),
        in_specs=[col, pl.BlockSpec((t, hd), lambda h: (0, h + z_off // hd)), pl.BlockSpec((1, hd), lambda h: (0, 0))],
        out_specs=col, out_shape=_sds((t, HEADS * hd), bf16), compiler_params=_cparams(("parallel",)))(o, p, gain)


def _post_bwd(o, p, z_off, gain, dmix, mix_off, name):
    t = o.shape[0]
    hd = HEAD_DIM

    def body(o_ref, z_ref, g_ref, dy_ref, do_ref, dz_ref, dg_ref):
        _, vjp = jax.vjp(_post_fn, o_ref[...], z_ref[...], g_ref[...])
        do, dz, dg = vjp(dy_ref[...])
        do_ref[...] = do
        dz_ref[...] = dz.astype(bf16)

        @pl.when(pl.program_id(0) == 0)
        def _():
            dg_ref[...] = jnp.zeros_like(dg_ref)

        dg_ref[...] += dg

    col = pl.BlockSpec((t, hd), lambda h: (0, h))
    vec = pl.BlockSpec((1, hd), lambda h: (0, 0))
    return pl.pallas_call(
        body, name=name, grid=(HEADS,),
        in_specs=[col, pl.BlockSpec((t, hd), lambda h: (0, h + z_off // hd)), vec,
                  pl.BlockSpec((t, hd), lambda h: (0, h + mix_off // hd))],
        out_specs=[col, col, vec], out_shape=[_sds((t, HEADS * hd), f32), _sds((t, HEADS * hd), bf16), _sds((1, hd), f32)],
        compiler_params=_cparams(("arbitrary",)))(o, p, gain, dmix)


def _hgrn_pre_fn(qb, fb, lbw, layer):
    l0, l1 = lbw[0:1, :], lbw[1:2, :]
    m = jnp.maximum(l0, l1)
    e0, e1 = jnp.exp(l0 - m), jnp.exp(l1 - m)
    p0, p1 = e0 / (e0 + e1), e1 / (e0 + e1)
    lb = (p0 - p0) if layer == 0 else ((p0 + p1) - p0)
    f = lb + (1.0 - lb) * _sigmoid(fb)
    return _silu(qb), 1.0 - f, jnp.log(jnp.maximum(f, F_FLOOR))


def _hgrn_pre_fwd(p, lbw, layer):
    t = p.shape[0]
    tc = HEAD_DIM

    def body(qb_ref, fb_ref, lb_ref, q_ref, k_ref, lf_ref):
        q_ref[...], k_ref[...], lf_ref[...] = _hgrn_pre_fn(qb_ref[...], fb_ref[...], lb_ref[...], layer)

    col = pl.BlockSpec((t, tc), lambda j: (0, j))
    return pl.pallas_call(
        body, name="hgrn_pre_fwd", grid=(GDN_WIDTH // tc,),
        in_specs=[pl.BlockSpec((t, tc), lambda j: (0, j + OFF_QB // tc)), pl.BlockSpec((t, tc), lambda j: (0, j + OFF_FB // tc)),
                  pl.BlockSpec((2, tc), lambda j: (0, j))],
        out_specs=[col] * 3, out_shape=[_sds((t, GDN_WIDTH), f32)] * 3,
        compiler_params=_cparams(("parallel",)))(p, p, lbw)


def _hgrn_pre_bwd(p, lbw, layer, dq, dk, dlf):
    t = p.shape[0]
    tc = HEAD_DIM

    def body(qb_ref, fb_ref, lb_ref, dq_ref, dk_ref, dlf_ref, dqb_ref, dfb_ref, dlb_ref):
        _, vjp = jax.vjp(functools.partial(_hgrn_pre_fn, layer=layer), qb_ref[...], fb_ref[...], lb_ref[...])
        dqb, dfb, dlb = vjp((dq_ref[...], dk_ref[...], dlf_ref[...]))
        dqb_ref[...] = dqb.astype(bf16)
        dfb_ref[...] = dfb.astype(bf16)
        dlb_ref[...] = dlb

    col = pl.BlockSpec((t, tc), lambda j: (0, j))
    lb = pl.BlockSpec((2, tc), lambda j: (0, j))
    return pl.pallas_call(
        body, name="hgrn_pre_bwd", grid=(GDN_WIDTH // tc,),
        in_specs=[pl.BlockSpec((t, tc), lambda j: (0, j + OFF_QB // tc)), pl.BlockSpec((t, tc), lambda j: (0, j + OFF_FB // tc)),
                  lb, col, col, col],
        out_specs=[col, col, lb], out_shape=[_sds((t, GDN_WIDTH), bf16)] * 2 + [_sds((2, GDN_WIDTH), f32)],
        compiler_params=_cparams(("parallel",)))(p, p, lbw, dq, dk, dlf)


def _hgrn_step(st, q, k, lf, v):
    c = HGRN_CHUNK
    nh = q.shape[0]
    r2 = lax.broadcasted_iota(jnp.int32, (c, c), 0)
    c2 = lax.broadcasted_iota(jnp.int32, (c, c), 1)
    tri = jnp.broadcast_to((r2 >= c2).astype(f32), (nh, c, c))
    i3 = lax.broadcasted_iota(jnp.int32, (c, c, HEAD_DIM), 0)
    j3 = lax.broadcasted_iota(jnp.int32, (c, c, HEAD_DIM), 1)
    mask = i3 >= j3
    outs = []
    for n in range(q.shape[1] // c):
        sl = slice(n * c, (n + 1) * c)
        qc, kc, lc, vc = q[:, sl], k[:, sl], lf[:, sl], v[:, sl]
        b = _hdot(tri, lc, precision=HI)
        rel = jnp.where(mask, jnp.exp(jnp.where(mask, b[:, :, None, :] - b[:, None, :, :], 0.0)), 0.0)
        scores = jnp.sum(qc[:, :, None, :] * kc[:, None, :, :] * rel, axis=-1)
        bl = b[:, c - 1:c, :]
        o = _hbdot(scores, vc) + _hbdot(qc * jnp.exp(b), st, BNT)
        st = st * jnp.exp(bl) + _hbdot(vc, kc * jnp.exp(bl - b), BTN)
        outs.append(o)
    return jnp.concatenate(outs, axis=1), st


def _hgrn_core_fwd(q, k, lf, p):
    t = q.shape[0]
    hd = HEAD_DIM
    rs = min(HGRN_STEP, t)
    n = t // rs

    def body(q_ref, k_ref, lf_ref, v_ref, o_ref, st_ref, s_ref):
        @pl.when(pl.program_id(0) == 0)
        def _():
            s_ref[...] = jnp.zeros_like(s_ref)

        s = s_ref[...]
        st_ref[:, 0] = s
        o, s_new = _hgrn_step(s, *[_heads_major(r) for r in (q_ref, k_ref, lf_ref, v_ref)])
        for h in range(HEADS):
            o_ref[:, h * hd:(h + 1) * hd] = o[h]
        s_ref[...] = s_new

    tile = pl.BlockSpec((rs, GDN_WIDTH), lambda i: (i, 0))
    return pl.pallas_call(
        body, name="hgrn_core_fwd", grid=(n,),
        in_specs=[tile, tile, tile, pl.BlockSpec((rs, GDN_WIDTH), lambda i: (i, OFF_IB // GDN_WIDTH))],
        out_specs=[tile, pl.BlockSpec((HEADS, 1, hd, hd), lambda i: (0, i, 0, 0))],
        out_shape=[_sds((t, GDN_WIDTH), f32), _sds((HEADS, n, hd, hd), f32)],
        scratch_shapes=[pltpu.VMEM((HEADS, hd, hd), f32)],
        compiler_params=_cparams(("arbitrary",)))(q, k, lf, p)


def _hgrn_core_bwd(q, k, lf, p, states, do):
    t = q.shape[0]
    hd = HEAD_DIM
    rs = min(HGRN_STEP, t)
    n = t // rs

    def body(q_ref, k_ref, lf_ref, v_ref, st_ref, do_ref, dq_ref, dk_ref, dlf_ref, dv_ref, ds_ref):
        @pl.when(pl.program_id(0) == 0)
        def _():
            ds_ref[...] = jnp.zeros_like(ds_ref)

        _, vjp = jax.vjp(_hgrn_step, st_ref[:, 0], *[_heads_major(r) for r in (q_ref, k_ref, lf_ref, v_ref)])
        ds, *dins = vjp((_heads_major(do_ref), ds_ref[...]))
        ds_ref[...] = ds
        for d_ref, d in zip((dq_ref, dk_ref, dlf_ref, dv_ref), dins):
            for h in range(HEADS):
                d_ref[:, h * hd:(h + 1) * hd] = d[h].astype(d_ref.dtype)

    tile = pl.BlockSpec((rs, GDN_WIDTH), lambda i: (n - 1 - i, 0))
    return pl.pallas_call(
        body, name="hgrn_core_bwd", grid=(n,),
        in_specs=[tile, tile, tile, pl.BlockSpec((rs, GDN_WIDTH), lambda i: (n - 1 - i, OFF_IB // GDN_WIDTH)),
                  pl.BlockSpec((HEADS, 1, hd, hd), lambda i: (0, n - 1 - i, 0, 0)), tile],
        out_specs=[tile] * 4, out_shape=[_sds((t, GDN_WIDTH), f32)] * 3 + [_sds((t, GDN_WIDTH), bf16)],
        scratch_shapes=[pltpu.VMEM((HEADS, hd, hd), f32)],
        compiler_params=_cparams(("arbitrary",)))(q, k, lf, p, states, do)


def _row(v):
    return v.reshape(1, -1)


def _anchored(w, row, key):
    tok = w.get(key)
    return row if tok is None else row + tok[0, 0]


def _pad_lanes(v, n=HEAD_DIM):
    return jnp.pad(v.reshape(1, -1), ((0, 0), (0, n - v.shape[-1])))


def _ffn_fwd(x, w, l):
    h = _rms_fwd(x, _row(w['norm_ffn'][l]), "ffn_norm")
    u = _mm(h, w['ffn_w_up'][l], name="ffn_up")
    a = _ffn_act_fwd(u, w['ffn_conv_w'][l], _row(w['ffn_conv_b'][l]))
    y = _mm(a, w['ffn_w_down'][l], add=x, name="ffn_down")
    return y, (x, h, u)


def _ffn_bwd(saved, w, l, dy, dyb, grads):
    x, h, u = saved
    da = _mm(dyb, w['ffn_w_down'][l], tb=True, name="ffn_down_dx")
    a, dg, dv, dcw, dcb = _ffn_act_bwd(u, w['ffn_conv_w'][l], _anchored(w, _row(w['ffn_conv_b'][l]), ('bwd', l)), da)
    grads['ffn_w_down'][l] = _mm(a, dyb, ta=True, out_dtype=bf16, name="ffn_down_dw")
    du = jnp.concatenate([dg, dv], axis=1)
    grads['ffn_w_up'][l] = _mm(h, du, ta=True, out_dtype=bf16, name="ffn_up_dw")
    dh = _mm(du, w['ffn_w_up'][l], tb=True, name="ffn_up_dx")
    dx, dxb, dgain = _rms_bwd(x, _row(w['norm_ffn'][l]), dh, dy, "ffn_norm_bwd")
    grads['ffn_conv_w'][l] = dcw
    grads['ffn_conv_b'][l] = dcb[0]
    grads['norm_ffn'][l] = dgain[0]
    return dx, dxb


def _odd_fwd(x, w, l, j):
    h = _rms_fwd(x, _anchored(w, _row(w['norm_mix'][l]), ('fwd', l)), "mix_norm")
    p = _mm(h, w['c_w_in'][j], name="lru_in")
    xc = _col_conv_fwd(p, LRU_WIDTH, w['c_conv_w'][j], _row(w['c_conv_b'][j]), LRU_CONV, 256, "lru_conv_fwd")
    out, hs, a = _lru_fwd(p, xc, w['c_gate_a_w'][j], _row(w['c_gate_a_b'][j]), w['c_gate_x_w'][j],
                          _row(w['c_gate_x_b'][j]), _row(w['c_lambda'][j]))
    y = _mm(out, w['c_w_out'][j], add=x, name="lru_out")
    return y, (x, h, p, xc, out, hs, a)


def _odd_bwd(saved, w, l, j, dy, dyb, grads):
    x, h, p, xc, out, hs, a = saved
    dout = _mm(dyb, w['c_w_out'][j], tb=True, name="lru_out_dx")
    grads['c_w_out'][j] = _mm(out, dyb, ta=True, out_dtype=bf16, name="lru_out_dw")
    dyb_, da, du = _lru_bwd_scan(p, a, hs, dout)
    dxc, dwa, dwx, dba, dbx, dlam = _lru_bwd_gates(xc, da, du, w['c_gate_a_w'][j], _row(w['c_gate_a_b'][j]),
                                                   w['c_gate_x_w'][j], _row(w['c_gate_x_b'][j]), _row(w['c_lambda'][j]))
    dxb_, dcw, dcb = _col_conv_bwd(p, LRU_WIDTH, w['c_conv_w'][j], dxc, LRU_CONV, 256, "lru_conv_bwd")
    dp = jnp.concatenate([dyb_, dxb_], axis=1)
    grads['c_w_in'][j] = _mm(h, dp, ta=True, out_dtype=bf16, name="lru_in_dw")
    dh = _mm(dp, w['c_w_in'][j], tb=True, name="lru_in_dx")
    dx, dxb, dgain = _rms_bwd(x, _row(w['norm_mix'][l]), dh, dy, "mix_norm_bwd")
    grads['c_gate_a_w'][j], grads['c_gate_x_w'][j] = dwa, dwx
    grads['c_gate_a_b'][j], grads['c_gate_x_b'][j], grads['c_lambda'][j] = dba[0], dbx[0], dlam[0]
    grads['c_conv_w'][j], grads['c_conv_b'][j] = dcw, dcb[0]
    grads['norm_mix'][l] = dgain[0]
    return dx, dxb


def _even_fwd(x, w, l, j):
    h = _rms_fwd(x, _anchored(w, _row(w['norm_mix'][l]), ('fwd', l)), "mix_norm")
    p = _mm(h, w['ab_w_in'][j], name="ab_in")
    alog, dtb = _pad_lanes(w['gdn_a_log'][j]), _pad_lanes(w['gdn_dt_bias'][j])
    q, k, v, be, ge = _gdn_pre_fwd(p, w['gdn_conv_w'][j], alog, dtb)
    oa, sa = _gdn_core_fwd(q, k, v, ge, be)
    ya = _post_fwd(oa, p, OFF_Z, _row(w['gdn_norm'][j]), "gdn_post_fwd")
    qq, kk, lf = _hgrn_pre_fwd(p, w['hgrn_lower_bounds'], j)
    ob, sb = _hgrn_core_fwd(qq, kk, lf, p)
    yb = _post_fwd(ob, p, OFF_GB, _row(w['hgrn_norm'][j]), "hgrn_post_fwd")
    mix = jnp.concatenate([ya, yb], axis=1)
    y = _mm(mix, w['ab_w_out'][j], add=x, name="ab_out")
    return y, (x, h, p, q, k, v, be, ge, oa, sa, qq, kk, lf, ob, sb, mix)


def _even_bwd(saved, w, l, j, dy, dyb, grads):
    x, h, p, q, k, v, be, ge, oa, sa, qq, kk, lf, ob, sb, mix = saved
    alog, dtb = _pad_lanes(w['gdn_a_log'][j]), _pad_lanes(w['gdn_dt_bias'][j])
    dmix = _mm(dyb, w['ab_w_out'][j], tb=True, name="ab_out_dx")
    grads['ab_w_out'][j] = _mm(mix, dyb, ta=True, out_dtype=bf16, name="ab_out_dw")
    doa, dz, dgn = _post_bwd(oa, p, OFF_Z, _row(w['gdn_norm'][j]), dmix, 0, "gdn_post_bwd")
    dob, dgb, dhn = _post_bwd(ob, p, OFF_GB, _row(w['hgrn_norm'][j]), dmix, GDN_WIDTH, "hgrn_post_bwd")
    dq, dk, dv, dge, dbe = _gdn_core_bwd(q, k, v, ge, be, sa, doa)
    dpq, dpk, dpv, dba, dwq, dwk, dwv, dal, ddt = _gdn_pre_bwd(p, w['gdn_conv_w'][j], alog, dtb, dq, dk, dv, dbe, dge)
    dqq, dkk, dlf, dib = _hgrn_core_bwd(qq, kk, lf, p, sb, dob)
    dqb, dfb, dlb = _hgrn_pre_bwd(p, w['hgrn_lower_bounds'], j, dqq, dkk, dlf)
    dp = jnp.concatenate([dpq, dpk, dpv, dz, dqb, dfb, dib, dgb, dba.astype(bf16)], axis=1)
    grads['ab_w_in'][j] = _mm(h, dp, ta=True, out_dtype=bf16, name="ab_in_dw")
    dh = _mm(dp, w['ab_w_in'][j], tb=True, name="ab_in_dx")
    dx, dxb, dgain = _rms_bwd(x, _row(w['norm_mix'][l]), dh, dy, "mix_norm_bwd")
    grads['gdn_conv_w'][j] = jnp.concatenate([dwq, dwk, dwv], axis=1)
    grads['gdn_a_log'][j], grads['gdn_dt_bias'][j] = dal[0, :HEADS], ddt[0, :HEADS]
    grads['gdn_norm'][j], grads['hgrn_norm'][j] = dgn[0], dhn[0]
    grads['hgrn_lower_bounds'].append(dlb)
    grads['norm_mix'][l] = dgain[0]
    return dx, dxb


def _ab_permute(w_in):
    pad = jnp.zeros(w_in.shape[:-1] + (AB_PAD - AB_COLS,), w_in.dtype)
    return jnp.concatenate([w_in[..., :2048], w_in[..., 2056:], w_in[..., 2048:2056], pad], axis=-1)


def _ab_unpermute(g):
    return jnp.concatenate([g[..., :2048], g[..., 4096:4104], g[..., 2048:4096]], axis=-1)


def _block_pad(a, axis, nblk, padded):
    axis = axis % a.ndim
    s = a.shape
    a = a.reshape(s[:axis] + (nblk, s[axis] // nblk) + s[axis + 1:])
    pad = [(0, 0)] * a.ndim
    pad[axis + 1] = (0, padded - s[axis] // nblk)
    return jnp.pad(a, pad).reshape(s[:axis] + (nblk * padded,) + s[axis + 1:])


def _block_unpad(a, axis, nblk, width):
    axis = axis % a.ndim
    s = a.shape
    a = a.reshape(s[:axis] + (nblk, s[axis] // nblk) + s[axis + 1:])
    a = lax.slice_in_dim(a, 0, width, axis=axis + 1)
    return a.reshape(s[:axis] + (nblk * width,) + s[axis + 1:])


def _kernel_layout(w):
    w = dict(w)
    w['ab_w_in'] = _ab_permute(w['ab_w_in'])
    w['ffn_w_up'] = _block_pad(w['ffn_w_up'], 2, N_DEV, FF_PAD)
    w['ffn_w_down'] = _block_pad(w['ffn_w_down'], 1, 4, FF_PAD)
    w['ffn_conv_w'] = _block_pad(w['ffn_conv_w'], 2, 4, FF_PAD)
    w['ffn_conv_b'] = _block_pad(w['ffn_conv_b'], 1, 4, FF_PAD)
    return w


def _natural_grads(g):
    g = dict(g)
    g['ab_w_in'] = _ab_unpermute(g['ab_w_in'])
    g['ffn_w_up'] = _block_unpad(g['ffn_w_up'], 2, N_DEV, FF_SHARD)
    g['ffn_w_down'] = _block_unpad(g['ffn_w_down'], 1, 4, FF_SHARD)
    g['ffn_conv_w'] = _block_unpad(g['ffn_conv_w'], 2, 4, FF_SHARD)
    g['ffn_conv_b'] = _block_unpad(g['ffn_conv_b'], 1, 4, FF_SHARD)
    return g


def _local_step(x, target, w, fetch=None, push=None):
    grads = {n: [None] * (DEPTH if n in ('norm_mix', 'norm_ffn') or n.startswith('ffn_') else 2)
             for n in WEIGHTS if n not in ('norm_final', 'hgrn_lower_bounds')}
    grads['hgrn_lower_bounds'] = []
    saved = []
    for l in range(DEPTH):
        j = l // 2
        if fetch is not None:
            fetch(l, x)
        x, s_mix = (_even_fwd if l % 2 == 0 else _odd_fwd)(x, w, l, j)
        x, s_ffn = _ffn_fwd(x, w, l)
        saved.append((s_mix, s_ffn))
    loss, dx, dxb, dgf = _loss_head(x, _row(w['norm_final']), target)
    for l in reversed(range(DEPTH)):
        j = l // 2
        s_mix, s_ffn = saved[l]
        dx, dxb = _ffn_bwd(s_ffn, w, l, dx, dxb, grads)
        dx, dxb = (_even_bwd if l % 2 == 0 else _odd_bwd)(s_mix, w, l, j, dx, dxb, grads)
        if push is not None:
            push(l, {nm: grads[nm].pop(li) for nm, li in reversed(_layer_items(l))})
    out = {n: jnp.stack(g) for n, g in grads.items() if n != 'hgrn_lower_bounds' and g}
    out['hgrn_lower_bounds'] = grads['hgrn_lower_bounds'][0] + grads['hgrn_lower_bounds'][1]
    out['norm_final'] = dgf[0]
    return loss[0, 0], dx, out


def _position():
    return lax.axis_index("x"), lax.axis_index("y"), lax.axis_index("c")


BLOCK_LAYOUT = {
    'ab_w_in': ((2, D_MODEL, N_DEV * AB_SHARD_PAD), (2, D_MODEL, AB_SHARD_PAD)),
    'ab_w_out': ((2, N_DEV, 128, D_MODEL), (2, 128, D_MODEL)),
    'c_w_in': ((2, D_MODEL, 2 * LRU_WIDTH), (2, D_MODEL, 256)),
    'c_w_out': ((2, N_DEV, 128, D_MODEL), (2, 128, D_MODEL)),
    'c_gate_a_w': ((2, HEADS, N_DEV, 32, LRU_BLOCK), (2, HEADS, 32, LRU_BLOCK)),
    'c_gate_x_w': ((2, HEADS, N_DEV, 32, LRU_BLOCK), (2, HEADS, 32, LRU_BLOCK)),
    'ffn_w_up': ((DEPTH, D_MODEL, N_DEV * FF_PAD), (DEPTH, D_MODEL, FF_PAD)),
    'ffn_w_down': ((DEPTH, 4, FF_PAD, D_MODEL), (DEPTH, FF_ROWS, D_MODEL)),
}


COL_WINDOW = {'ab_w_in': AB_SHARD_PAD, 'c_w_in': 256, 'ffn_w_up': FF_PAD}


def _block_index(name, p):
    d = 4 * p[0] + 2 * p[1] + p[2]
    if name in COL_WINDOW:
        return (slice(None), pl.ds(pl.multiple_of(d * COL_WINDOW[name], 128), COL_WINDOW[name]))
    if name == 'ffn_w_down':
        return (2 * p[0] + p[1], pl.ds(pl.multiple_of(p[2] * FF_ROWS, 16), FF_ROWS), slice(None))
    if name in ('c_gate_a_w', 'c_gate_x_w'):
        return (slice(None), d)
    return (d,)


def _block_of(name, ref, p, layered=True):
    idx = _block_index(name, p)
    if layered and name in BLOCK_LAYOUT:
        idx = (slice(None),) + idx
    return ref.at[idx]


def _layer_items(l):
    j = l // 2
    mix = ([('ab_w_in', j), ('ab_w_out', j)] if l % 2 == 0 else
           [('c_w_in', j), ('c_w_out', j), ('c_gate_a_w', j), ('c_gate_x_w', j)])
    return mix + [('ffn_w_up', l), ('ffn_w_down', l)]


def _own_land(name, shard_l, pos):
    x, y, c = pos
    d = 4 * x + 2 * y + c
    shape = BLOCK_LAYOUT[name][0][1:] if name in BLOCK_LAYOUT else (N_DEV,) + shard_l.shape
    zeros = jnp.zeros(shape, shard_l.dtype) if name == 'ffn_w_down' else lax.empty(shape, shard_l.dtype)
    if name in COL_WINDOW:
        return lax.dynamic_update_slice(zeros, shard_l, (0, d * COL_WINDOW[name]))
    if name == 'ffn_w_down':
        return lax.dynamic_update_slice(zeros, shard_l[None], (2 * x + y, c * FF_ROWS, 0))
    if name in ('c_gate_a_w', 'c_gate_x_w'):
        return lax.dynamic_update_slice(zeros, shard_l[:, None], (0, d, 0, 0))
    return lax.dynamic_update_slice(zeros, shard_l[None], (d,) + (0,) * shard_l.ndim)


def _place_own(items, shards, posv, name):
    n = len(items)
    down = [i for i, (nm, _) in enumerate(items) if nm == 'ffn_w_down']
    in_specs, out_specs, out_shapes, operands = [], [], [], []
    for nm, li in items:
        sh = shards[nm]
        shard_shape = sh.shape if li is None else sh.shape[1:]
        z = (0,) * len(shard_shape)
        operands.append(sh)
        in_specs.append(pl.BlockSpec(shard_shape, lambda i, d, q, c, z=z: z) if li is None else
                        pl.BlockSpec((1,) + shard_shape, lambda i, d, q, c, li=li, z=z: (li,) + z))
        out_shapes.append(_sds(BLOCK_LAYOUT[nm][0][1:] if nm in BLOCK_LAYOUT else (N_DEV,) + sh.shape, sh.dtype))
        if nm in COL_WINDOW:
            out_specs.append(pl.BlockSpec(shard_shape, lambda i, d, q, c: (0, d[0])))
        elif nm == 'ffn_w_down':
            out_specs.append(pl.BlockSpec((1,) + shard_shape, lambda i, d, q, c: (q[0], c[0], 0)))
        elif nm in ('c_gate_a_w', 'c_gate_x_w'):
            out_specs.append(pl.BlockSpec((HEADS, 1) + shard_shape[1:], lambda i, d, q, c: (0, d[0], 0, 0)))
        else:
            out_specs.append(pl.BlockSpec((1,) + shard_shape, lambda i, d, q, c, z=z: (d[0],) + z))

    def body(d_ref, q_ref, c_ref, *refs):
        for i, (nm, li) in enumerate(items):
            v = refs[i][...] if li is None else refs[i][0]
            o_ref = refs[n + len(down) + i]
            if nm in COL_WINDOW:
                o_ref[...] = v
            elif nm in ('c_gate_a_w', 'c_gate_x_w'):
                o_ref[:, 0] = v
            else:
                o_ref[0] = v

    zeros = [jnp.zeros(out_shapes[i].shape, out_shapes[i].dtype) for i in down]
    return pl.pallas_call(
        body, name=name, out_shape=out_shapes,
        grid_spec=pltpu.PrefetchScalarGridSpec(
            num_scalar_prefetch=3, grid=(1,), in_specs=in_specs + [pl.BlockSpec(memory_space=pl.ANY)] * len(down),
            out_specs=out_specs),
        input_output_aliases={3 + n + k: i for k, i in enumerate(down)},
        compiler_params=_cparams(("arbitrary",)))(*posv, *operands, *zeros)


def _src_of(shard_ref, li):
    return shard_ref if li is None else shard_ref.at[li]


def _gather_now(items, shards, lands):
    n = len(items)
    srcs = sorted({nm for nm, _ in items})

    def body(*refs):
        ins = dict(zip(srcs, refs[:len(srcs)]))
        outs = refs[len(srcs) + n:len(srcs) + 2 * n]
        send_sems, recv_sems = refs[len(srcs) + 2 * n:]
        x, y, c = _position()
        me, sibling = (x, y, c), (x, y, 1 - c)
        chips = [(1 - x, y), (x, 1 - y), (1 - x, 1 - y)]

        def copy(i, k, block, to, own=False):
            nm, li = items[i]
            dst = _block_of(nm, outs[i], block, layered=False)
            return pltpu.make_async_remote_copy(
                src_ref=_src_of(ins[nm], li) if own else dst, dst_ref=dst, send_sem=send_sems.at[7 * i + k],
                recv_sem=recv_sems.at[7 * i + k], device_id=to, device_id_type=MESH)

        first = []
        for i in range(n):
            first.append(copy(i, 0, me, sibling, own=True))
            first += [copy(i, 1 + j, me, (*chip, c), own=True) for j, chip in enumerate(chips)]
        for cp in first:
            cp.start()
        passed = []
        for j, chip in enumerate(chips):
            for i in range(n):
                copy(i, 1 + j, (*chip, c), me).wait_recv()
                fwd = copy(i, 4 + j, (*chip, c), sibling)
                fwd.start()
                passed.append(fwd)
        for i in range(n):
            copy(i, 0, sibling, me).wait_recv()
        for j, chip in enumerate(chips):
            for i in range(n):
                copy(i, 4 + j, (*chip, 1 - c), me).wait_recv()
        for cp in first + passed:
            cp.wait_send()

    any_spec = pl.BlockSpec(memory_space=pl.ANY)
    return pl.pallas_call(
        body, name="gather_first_layer", out_shape=[_sds(a.shape, a.dtype) for a in lands],
        in_specs=[any_spec] * (len(srcs) + n), out_specs=[any_spec] * n,
        input_output_aliases={len(srcs) + i: i for i in range(n)},
        scratch_shapes=[pltpu.SemaphoreType.DMA((7 * n,)), pltpu.SemaphoreType.DMA((7 * n,))],
    )(*[shards[nm] for nm in srcs], *lands)


FIRST_HOP = (1, 2, 4, 6)


def _lanes(name, land_ref, pos):
    if name == 'ffn_w_down':
        return [(FIRST_HOP, land_ref.at[pl.ds(0, 2), pl.ds(0, 2 * FF_ROWS)])]
    if name in COL_WINDOW:
        return [(FIRST_HOP, land_ref.at[:, pl.ds(0, 4 * COL_WINDOW[name])])]
    if name in ('c_gate_a_w', 'c_gate_x_w'):
        return [(FIRST_HOP, land_ref.at[:, pl.ds(0, 4)])]
    return [(FIRST_HOP, land_ref.at[pl.ds(0, 4)])]


def _n_lanes(items):
    return len(items)


def _gather_forward(items, lands, name):
    n = len(items)

    def body(*refs):
        outs = refs[n:2 * n]
        send_sems, recv_sems = refs[2 * n:]
        x, y, c = _position()
        chips = [(1 - x, y), (x, 1 - y), (1 - x, 1 - y)]
        copies, arrivals = [], []
        for i, (nm, _) in enumerate(items):
            for j, chip in enumerate(chips):
                mine = _block_of(nm, outs[i], (*chip, c), layered=False)
                theirs = _block_of(nm, outs[i], (*chip, 1 - c), layered=False)
                copies.append(pltpu.make_async_remote_copy(
                    src_ref=mine, dst_ref=mine, send_sem=send_sems.at[3 * i + j], recv_sem=recv_sems.at[3 * i + j],
                    device_id=(x, y, 1 - c), device_id_type=MESH))
                arrivals.append(pltpu.make_async_remote_copy(
                    src_ref=theirs, dst_ref=theirs, send_sem=send_sems.at[3 * i + j], recv_sem=recv_sems.at[3 * i + j],
                    device_id=(x, y, 1 - c), device_id_type=MESH))
        for cp in copies:
            cp.start()
        for cp in arrivals:
            cp.wait_recv()
        for cp in copies:
            cp.wait_send()

    any_spec = pl.BlockSpec(memory_space=pl.ANY)
    return pl.pallas_call(
        body, name=name, out_shape=[_sds(a.shape, a.dtype) for a in lands],
        in_specs=[any_spec] * n, out_specs=[any_spec] * n, input_output_aliases={i: i for i in range(n)},
        scratch_shapes=[pltpu.SemaphoreType.DMA((3 * n,)), pltpu.SemaphoreType.DMA((3 * n,))],
    )(*lands)


HBM_SPEC = pl.BlockSpec(memory_space=pltpu.HBM)
SEM_SPEC = pl.BlockSpec(memory_space=pltpu.SEMAPHORE)
SIDE_EFFECT = pltpu.SideEffectType.DATAFLOW_SIDE_EFFECTING


def _gather_start(items, shards, lands, token, name):
    n = len(items)
    srcs = sorted({nm for nm, _ in items})
    ns, nl = len(srcs), _n_lanes(items)

    def body(*refs):
        ins = dict(zip(srcs, refs[:ns]))
        land_refs = refs[ns:ns + n]
        sems = refs[ns + n + 1:ns + n + 1 + 2 * nl]
        x, y, c = _position()
        me = (x, y, c)
        lane = 0
        for i, (nm, li) in enumerate(items):
            for codes, _ in _lanes(nm, land_refs[i], me):
                for k in codes:
                    peer = (1 - x if (k >> 2) & 1 else x, 1 - y if (k >> 1) & 1 else y, 1 - c if k & 1 else c)
                    pltpu.make_async_remote_copy(
                        src_ref=_src_of(ins[nm], li), dst_ref=_block_of(nm, land_refs[i], me, layered=False),
                        send_sem=sems[2 * lane], recv_sem=sems[2 * lane + 1], device_id=peer, device_id_type=MESH).start()
                lane += 1
        refs[-1][...] = jnp.zeros((8, 128), f32)

    hbm = [pltpu.with_memory_space_constraint(a, pltpu.HBM) for a in [shards[nm] for nm in srcs] + list(lands)]
    outs = pl.pallas_call(
        body, name=name,
        out_shape=[pltpu.SemaphoreType.DMA(())] * (2 * nl) + [pltpu.HBM(a.shape, a.dtype) for a in hbm] + [_sds((8, 128), f32)],
        in_specs=[HBM_SPEC] * (ns + n) + [pl.BlockSpec(memory_space=pl.ANY)],
        out_specs=[SEM_SPEC] * (2 * nl) + [HBM_SPEC] * (ns + n) + [pl.BlockSpec(memory_space=pltpu.VMEM)],
        input_output_aliases={i: 2 * nl + i for i in range(ns + n)},
        compiler_params=pltpu.CompilerParams(has_side_effects=SIDE_EFFECT),
    )(*hbm, token)
    return outs[:2 * nl], dict(zip(srcs, outs[2 * nl:2 * nl + ns])), outs[2 * nl + ns:-1], outs[-1]


def _gather_wait(items, sems, shards, lands, after, name):
    n = len(items)
    srcs = sorted(shards)
    ns, nl = len(srcs), _n_lanes(items)

    def body(*refs):
        land_refs = refs[ns:ns + n]
        sem_refs = refs[ns + n:ns + n + 2 * nl]
        x, y, c = _position()
        lane = 0
        for i, (nm, _) in enumerate(items):
            for _, moved in _lanes(nm, land_refs[i], (x, y, c)):
                cp = pltpu.make_async_remote_copy(
                    src_ref=moved, dst_ref=moved, send_sem=sem_refs[2 * lane], recv_sem=sem_refs[2 * lane + 1],
                    device_id=(x, y, 1 - c), device_id_type=MESH)
                cp.wait_send()
                cp.wait_recv()
                lane += 1

    outs = pl.pallas_call(
        body, name=name, out_shape=[pltpu.HBM(shards[nm].shape, shards[nm].dtype) for nm in srcs]
        + [pltpu.HBM(a.shape, a.dtype) for a in lands],
        in_specs=[HBM_SPEC] * (ns + n) + [SEM_SPEC] * (2 * nl) + [pl.BlockSpec(memory_space=pl.ANY)],
        out_specs=[HBM_SPEC] * (ns + n), input_output_aliases={i: i for i in range(ns + n)},
        compiler_params=pltpu.CompilerParams(has_side_effects=SIDE_EFFECT),
    )(*[shards[nm] for nm in srcs], *lands, *sems, after)
    return dict(zip(srcs, outs[:ns])), outs[ns:]


def _exchange_grads(fulls, rep):
    cpos = lax.axis_index("c").astype(jnp.int32).reshape(1)
    pair, rep_pair = _pair_exchange(fulls, rep)
    chip = {nm: _chip_sum(nm, fulls[nm], pair[nm], cpos) for nm in fulls}
    rep_chip = _add_pair(rep, rep_pair, "chip_sum_replicated")
    cross, cross_rep = _cross_exchange(chip, rep_chip)
    return chip, rep_chip, cross, cross_rep


def _pair_exchange(fulls, rep, tag=""):
    names = list(fulls)
    n = len(names)
    shard_shape = {nm: ((fulls[nm].shape[0],) + BLOCK_LAYOUT[nm][1][1:] if nm in BLOCK_LAYOUT else fulls[nm].shape[1:])
                   for nm in names}

    def body(*refs):
        ins = dict(zip(names, refs[:n]))
        rep_ref = refs[n]
        pair = dict(zip(names, refs[n + 1:2 * n + 1]))
        rpair_ref = refs[2 * n + 1]
        send_sems, recv_sems = refs[2 * n + 2:]
        x, y, c = _position()
        sibling = (x, y, 1 - c)
        remote = []
        for i, nm in enumerate(names):
            for q in range(4):
                remote.append(pltpu.make_async_remote_copy(
                    src_ref=_block_of(nm, ins[nm], (q >> 1, q & 1, 1 - c)), dst_ref=pair[nm].at[q],
                    send_sem=send_sems.at[4 * i + q], recv_sem=recv_sems.at[4 * i + q], device_id=sibling,
                    device_id_type=MESH))
        remote.append(pltpu.make_async_remote_copy(
            src_ref=rep_ref, dst_ref=rpair_ref, send_sem=send_sems.at[4 * n], recv_sem=recv_sems.at[4 * n],
            device_id=sibling, device_id_type=MESH))
        for cp in remote:
            cp.start()
        for cp in remote:
            cp.wait_recv()
        for cp in remote:
            cp.wait_send()

    any_spec = pl.BlockSpec(memory_space=pl.ANY)
    four = [_sds((4,) + tuple(shard_shape[nm]), fulls[nm].dtype) for nm in names]
    outs = pl.pallas_call(
        body, name="grad_pair_exchange" + tag, out_shape=four + [_sds(rep.shape, rep.dtype)],
        in_specs=[any_spec] * (n + 1), out_specs=[any_spec] * (n + 1),
        scratch_shapes=[pltpu.SemaphoreType.DMA((4 * n + 1,)), pltpu.SemaphoreType.DMA((4 * n + 1,))],
    )(*[fulls[nm] for nm in names], rep)
    return dict(zip(names, outs[:n])), outs[n]


def _chip_sum(name, full, pair, cpos, tag=""):
    if name in ('ab_w_in', 'c_w_in', 'ffn_w_up'):
        width = BLOCK_LAYOUT[name][1][-1]
        rows = full.shape[0] * full.shape[1]
        tr = 512

        def body(c_ref, f_ref, p_ref, o_ref):
            o_ref[0] = (f_ref[...].astype(f32) + p_ref[0].astype(f32)).astype(o_ref.dtype)

        slot = pl.BlockSpec((1, tr, width), lambda q, i, c: (q, i, 0))
        out = pl.pallas_call(
            body, name="chip_sum_" + name + tag, out_shape=_sds((4, rows, width), full.dtype),
            grid_spec=pltpu.PrefetchScalarGridSpec(
                num_scalar_prefetch=1, grid=(4, rows // tr),
                in_specs=[pl.BlockSpec((tr, width), lambda q, i, c: (i, 2 * q + c[0])), slot], out_specs=slot),
            compiler_params=_cparams(("parallel", "parallel")))(
            cpos, full.reshape(rows, N_DEV * width), pair.reshape(4, rows, width))
        return out.reshape(pair.shape)

    if name == 'ffn_w_down':
        f4, p4 = full, pair
        fspec = pl.BlockSpec((full.shape[0], 1, FF_ROWS, D_MODEL), lambda q, c: (0, q, c[0], 0))
    else:
        shard = pair.shape[1:]
        lead = int(np.prod(shard[:-2]))
        f4 = full.reshape((lead, N_DEV) + shard[-2:])
        p4 = pair.reshape((4, lead) + shard[-2:])
        fspec = pl.BlockSpec((lead, 1) + shard[-2:], lambda q, c: (0, 2 * q + c[0], 0, 0))

    def body4(c_ref, f_ref, p_ref, o_ref):
        o_ref[0] = (f_ref[:, 0].astype(f32) + p_ref[0].astype(f32)).astype(o_ref.dtype)

    slot = pl.BlockSpec((1,) + p4.shape[1:], lambda q, c: (q, 0, 0, 0))
    out = pl.pallas_call(
        body4, name="chip_sum_" + name + tag, out_shape=_sds(p4.shape, full.dtype),
        grid_spec=pltpu.PrefetchScalarGridSpec(num_scalar_prefetch=1, grid=(4,), in_specs=[fspec, slot], out_specs=slot),
        compiler_params=_cparams(("parallel",)))(cpos, f4, p4)
    return out.reshape(pair.shape)


def _add_pair(a, b, name):
    shp = a.shape
    r, c = int(np.prod(shp[:-1])), shp[-1]
    tr = _tile(r, (512, 256, 128, 64, 32, 16, 8))

    def body(a_ref, b_ref, o_ref):
        o_ref[...] = (a_ref[...].astype(f32) + b_ref[...].astype(f32)).astype(o_ref.dtype)

    tile = pl.BlockSpec((tr, c), lambda i: (i, 0))
    return pl.pallas_call(body, name=name, grid=(r // tr,), in_specs=[tile, tile], out_specs=tile,
                          out_shape=_sds((r, c), a.dtype), compiler_params=_cparams(("parallel",)))(
        a.reshape(r, c), b.reshape(r, c)).reshape(shp)


def _cross_exchange(chip, rep_chip):
    names = list(chip)
    n = len(names)

    def body(*refs):
        ins = dict(zip(names, refs[:n]))
        rep_ref = refs[n]
        outs = dict(zip(names, refs[2 * n + 2:3 * n + 2]))
        rrep_ref = refs[3 * n + 2]
        send_sems, recv_sems = refs[3 * n + 3:]
        x, y, c = _position()
        mine = 2 * x + y
        copies = []
        for k in range(1, 4):
            px, py = (1 - x if (k >> 1) & 1 else x), (1 - y if k & 1 else y)
            for i, nm in enumerate(names + ['']):
                src = rep_ref if i == n else ins[nm].at[2 * px + py]
                dst = (rrep_ref if i == n else outs[nm]).at[mine]
                copies.append(pltpu.make_async_remote_copy(
                    src_ref=src, dst_ref=dst, send_sem=send_sems.at[3 * i + k - 1], recv_sem=recv_sems.at[3 * i + k - 1],
                    device_id=(px, py, c), device_id_type=MESH))
        for cp in copies:
            cp.start()
        for cp in copies:
            cp.wait_recv()
        for cp in copies:
            cp.wait_send()

    any_spec = pl.BlockSpec(memory_space=pl.ANY)
    shapes = [_sds(chip[nm].shape, chip[nm].dtype) for nm in names] + [_sds((4,) + rep_chip.shape, rep_chip.dtype)]
    zeros = [jnp.zeros(s.shape, s.dtype) for s in shapes]
    outs = pl.pallas_call(
        body, name="grad_cross_exchange", out_shape=shapes,
        in_specs=[any_spec] * (2 * n + 2), out_specs=[any_spec] * (n + 1),
        input_output_aliases={n + 1 + i: i for i in range(n + 1)},
        scratch_shapes=[pltpu.SemaphoreType.DMA((3 * (n + 1),)), pltpu.SemaphoreType.DMA((3 * (n + 1),))],
    )(*[chip[nm] for nm in names], rep_chip, *zeros)
    return dict(zip(names, outs[:n])), outs[n]


def _cross_start(chips, name):
    n = len(chips)

    def body(*refs):
        chip_refs, land_refs = refs[:n], refs[n:2 * n]
        sems = refs[2 * n:4 * n]
        x, y, c = _position()
        mine = 2 * x + y
        for i in range(n):
            for k in range(1, 4):
                px, py = (1 - x if (k >> 1) & 1 else x), (1 - y if k & 1 else y)
                pltpu.make_async_remote_copy(
                    src_ref=chip_refs[i].at[2 * px + py], dst_ref=land_refs[i].at[mine], send_sem=sems[2 * i],
                    recv_sem=sems[2 * i + 1], device_id=(px, py, c), device_id_type=MESH).start()
        refs[-1][...] = jnp.zeros((8, 128), f32)

    hbm = [pltpu.with_memory_space_constraint(a, pltpu.HBM) for a in list(chips) + [jnp.zeros(a.shape, a.dtype) for a in chips]]
    outs = pl.pallas_call(
        body, name=name,
        out_shape=[pltpu.SemaphoreType.DMA(())] * (2 * n) + [pltpu.HBM(a.shape, a.dtype) for a in hbm] + [_sds((8, 128), f32)],
        in_specs=[HBM_SPEC] * (2 * n),
        out_specs=[SEM_SPEC] * (2 * n) + [HBM_SPEC] * (2 * n) + [pl.BlockSpec(memory_space=pltpu.VMEM)],
        input_output_aliases={i: 2 * n + i for i in range(2 * n)},
        compiler_params=pltpu.CompilerParams(has_side_effects=SIDE_EFFECT),
    )(*hbm)
    return outs[:2 * n], outs[2 * n:3 * n], outs[3 * n:4 * n], outs[4 * n]


def _cross_wait(sems, chips, lands, after, name):
    n = len(chips)

    def body(*refs):
        land_refs = refs[n:2 * n]
        sem_refs = refs[2 * n:4 * n]
        x, y, c = _position()
        for i in range(n):
            moved = land_refs[i].at[pl.ds(0, 3)]
            cp = pltpu.make_async_remote_copy(
                src_ref=moved, dst_ref=moved, send_sem=sem_refs[2 * i], recv_sem=sem_refs[2 * i + 1],
                device_id=(x, y, 1 - c), device_id_type=MESH)
            cp.wait_send()
            cp.wait_recv()

    outs = pl.pallas_call(
        body, name=name, out_shape=[pltpu.HBM(a.shape, a.dtype) for a in list(chips) + list(lands)],
        in_specs=[HBM_SPEC] * (2 * n) + [SEM_SPEC] * (2 * n) + [pl.BlockSpec(memory_space=pl.ANY)],
        out_specs=[HBM_SPEC] * (2 * n), input_output_aliases={i: i for i in range(2 * n)},
        compiler_params=pltpu.CompilerParams(has_side_effects=SIDE_EFFECT),
    )(*chips, *lands, *sems, after)
    return outs[:n], outs[n:]


def _sum_adamw_layer(parts, own, mine, w, m, v, li, prev, name):
    nl, r, l = w.shape
    lp = parts.shape[2]
    tr = r if r <= 512 else _tile(r, (512, 256, 128))
    c1 = 1.0 / (1.0 - ADAM_B1 ** ADAM_STEP)
    c2 = 1.0 / (1.0 - ADAM_B2 ** ADAM_STEP)
    k = 0 if prev is None else 4

    def body(mine_ref, p_ref, o_ref, w_ref, m_ref, v_ref, *rest):
        g_ref, d_ref, nm_ref, nv_ref = rest[k:]
        mine_v = o_ref[0].astype(f32)
        g = jnp.where(mine_ref[0] == 0, mine_v, p_ref[0].astype(f32))
        for s in range(1, 4):
            g = g + jnp.where(mine_ref[0] == s, mine_v, p_ref[s].astype(f32))
        if lp != l:
            g = g[:, :l]
        m_new = ADAM_B1 * m_ref[0] + (1.0 - ADAM_B1) * g
        v_new = ADAM_B2 * v_ref[0] + (1.0 - ADAM_B2) * (g * g)
        g_ref[0] = g
        nm_ref[0] = m_new
        nv_ref[0] = v_new
        d_ref[0] = -ADAM_LR * ((m_new * c1) / (jnp.sqrt(v_new * c2) + ADAM_EPS) + ADAM_WD * w_ref[0])

    tile = pl.BlockSpec((1, tr, l), lambda i, mn: (li, i, 0))
    keep = [pl.BlockSpec(memory_space=pl.ANY)] * k
    return pl.pallas_call(
        body, name=name, out_shape=[_sds((nl, r, l), f32)] * 4,
        grid_spec=pltpu.PrefetchScalarGridSpec(
            num_scalar_prefetch=1, grid=(r // tr,),
            in_specs=[pl.BlockSpec((4, tr, lp), lambda i, mn: (0, i, 0)), pl.BlockSpec((1, tr, lp), lambda i, mn: (mn[0], i, 0)),
                      tile, tile, tile] + keep,
            out_specs=[tile] * 4),
        input_output_aliases={6 + i: i for i in range(k)},
        compiler_params=_cparams(("parallel",)))(mine, parts, own, w, m, v, *(prev or ()))


def _sum_adamw(parts, own, mine, w, m, v, name):
    r, l = w.shape
    lp = parts.shape[2]
    tr = _tile(r, (256, 128, 64, 32, 16, 8))
    c1 = 1.0 / (1.0 - ADAM_B1 ** ADAM_STEP)
    c2 = 1.0 / (1.0 - ADAM_B2 ** ADAM_STEP)

    def body(mine_ref, p_ref, o_ref, w_ref, m_ref, v_ref, g_ref, d_ref, nm_ref, nv_ref):
        mine_v = (o_ref[0] if own.ndim == 3 else o_ref[...]).astype(f32)
        g = jnp.where(mine_ref[0] == 0, mine_v, p_ref[0].astype(f32))
        for s in range(1, parts.shape[0]):
            g = g + jnp.where(mine_ref[0] == s, mine_v, p_ref[s].astype(f32))
        if lp != l:
            g = g[:, :l]
        m_new = ADAM_B1 * m_ref[...] + (1.0 - ADAM_B1) * g
        v_new = ADAM_B2 * v_ref[...] + (1.0 - ADAM_B2) * (g * g)
        g_ref[...] = g
        nm_ref[...] = m_new
        nv_ref[...] = v_new
        d_ref[...] = -ADAM_LR * ((m_new * c1) / (jnp.sqrt(v_new * c2) + ADAM_EPS) + ADAM_WD * w_ref[...])

    tile = pl.BlockSpec((tr, l), lambda i, mn: (i, 0))
    own_spec = (pl.BlockSpec((1, tr, lp), lambda i, mn: (mn[0], i, 0)) if own.ndim == 3
                else pl.BlockSpec((tr, lp), lambda i, mn: (i, 0)))
    return pl.pallas_call(
        body, name=name, out_shape=[_sds((r, l), f32)] * 4,
        grid_spec=pltpu.PrefetchScalarGridSpec(
            num_scalar_prefetch=1, grid=(r // tr,),
            in_specs=[pl.BlockSpec((parts.shape[0], tr, lp), lambda i, mn: (0, i, 0)), own_spec, tile, tile, tile],
            out_specs=[tile] * 4),
        compiler_params=_cparams(("parallel",)))(mine, parts, own, w, m, v)


def _pack(arrs, lead=None):
    if lead is None:
        flat = jnp.concatenate([a.reshape(-1).astype(f32) for a in arrs])
        n = flat.shape[0]
    else:
        flat = jnp.concatenate([a.reshape(lead, -1).astype(f32) for a in arrs], axis=1)
        n = flat.shape[1]
    tot = -(-n // 1024) * 1024
    if lead is None:
        return jnp.pad(flat, (0, tot - n)).reshape(tot // 128, 128)
    return jnp.pad(flat, ((0, 0), (0, tot - n))).reshape(lead, tot // 128, 128)


def _unpack(packed, shapes, lead=False):
    flat = packed.reshape(packed.shape[0], -1) if lead else packed.reshape(-1)
    out, off = [], 0
    for s in shapes:
        n = int(np.prod(s))
        out.append(flat[:, off:off + n].reshape((packed.shape[0],) + tuple(s)) if lead else flat[off:off + n].reshape(s))
        off += n
    return out


def _merge_shards(g, axis):
    g = jnp.moveaxis(g, 0, axis)
    s = g.shape
    return g.reshape(s[:axis] + (s[axis] * s[axis + 1],) + s[axis + 2:])


def _split_shards(full, axis):
    s = full.shape
    g = full.reshape(s[:axis] + (N_DEV, s[axis] // N_DEV) + s[axis + 1:])
    return jnp.moveaxis(g, axis, 0)


def kernel(x, norm_mix, norm_ffn, norm_final, ab_w_in, gdn_conv_w, gdn_a_log, gdn_dt_bias, gdn_norm, hgrn_lower_bounds, hgrn_norm, ab_w_out, c_w_in, c_conv_w, c_conv_b, c_gate_a_w, c_gate_a_b, c_gate_x_w, c_gate_x_b, c_lambda, c_w_out, ffn_w_up, ffn_conv_w, ffn_conv_b, ffn_w_down, loss_target, m_norm_mix, m_norm_ffn, m_norm_final, m_ab_w_in, m_gdn_conv_w, m_gdn_a_log, m_gdn_dt_bias, m_gdn_norm, m_hgrn_lower_bounds, m_hgrn_norm, m_ab_w_out, m_c_w_in, m_c_conv_w, m_c_conv_b, m_c_gate_a_w, m_c_gate_a_b, m_c_gate_x_w, m_c_gate_x_b, m_c_lambda, m_c_w_out, m_ffn_w_up, m_ffn_conv_w, m_ffn_conv_b, m_ffn_w_down, v_norm_mix, v_norm_ffn, v_norm_final, v_ab_w_in, v_gdn_conv_w, v_gdn_a_log, v_gdn_dt_bias, v_gdn_norm, v_hgrn_lower_bounds, v_hgrn_norm, v_ab_w_out, v_c_w_in, v_c_conv_w, v_c_conv_b, v_c_gate_a_w, v_c_gate_a_b, v_c_gate_x_w, v_c_gate_x_b, v_c_lambda, v_c_w_out, v_ffn_w_up, v_ffn_conv_w, v_ffn_conv_b, v_ffn_w_down):
    wl = dict(zip(WEIGHTS, (norm_mix, norm_ffn, norm_final, ab_w_in, gdn_conv_w, gdn_a_log, gdn_dt_bias, gdn_norm, hgrn_lower_bounds, hgrn_norm, ab_w_out, c_w_in, c_conv_w, c_conv_b, c_gate_a_w, c_gate_a_b, c_gate_x_w, c_gate_x_b, c_lambda, c_w_out, ffn_w_up, ffn_conv_w, ffn_conv_b, ffn_w_down)))
    ml = dict(zip(WEIGHTS, (m_norm_mix, m_norm_ffn, m_norm_final, m_ab_w_in, m_gdn_conv_w, m_gdn_a_log, m_gdn_dt_bias, m_gdn_norm, m_hgrn_lower_bounds, m_hgrn_norm, m_ab_w_out, m_c_w_in, m_c_conv_w, m_c_conv_b, m_c_gate_a_w, m_c_gate_a_b, m_c_gate_x_w, m_c_gate_x_b, m_c_lambda, m_c_w_out, m_ffn_w_up, m_ffn_conv_w, m_ffn_conv_b, m_ffn_w_down)))
    vl = dict(zip(WEIGHTS, (v_norm_mix, v_norm_ffn, v_norm_final, v_ab_w_in, v_gdn_conv_w, v_gdn_a_log, v_gdn_dt_bias, v_gdn_norm, v_hgrn_lower_bounds, v_hgrn_norm, v_ab_w_out, v_c_w_in, v_c_conv_w, v_c_conv_b, v_c_gate_a_w, v_c_gate_a_b, v_c_gate_x_w, v_c_gate_x_b, v_c_lambda, v_c_w_out, v_ffn_w_up, v_ffn_conv_w, v_ffn_conv_b, v_ffn_w_down)))

    big = [n for n in SHARDED if n in MATMUL_WEIGHTS]
    vec = [n for n in SHARDED if n not in MATMUL_WEIGHTS]
    shards = {n: wl[n].astype(bf16) for n in big}
    shards['ab_w_in'] = jnp.pad(shards['ab_w_in'], ((0, 0), (0, 0), (0, AB_SHARD_PAD - AB_SHARD)))
    shards['ffn_w_up'] = jnp.pad(shards['ffn_w_up'], ((0, 0), (0, 0), (0, FF_PAD - FF_SHARD)))
    shards['vec'] = _pack([wl[n] for n in vec])
    pos = _position()
    layer_items = [_layer_items(l) for l in range(DEPTH)]
    posv = [v.astype(jnp.int32).reshape(1) for v in (4 * pos[0] + 2 * pos[1] + pos[2], 2 * pos[0] + pos[1], pos[2])]
    first_items = layer_items[0] + [('vec', None)]
    lands = [_place_own(first_items, shards, posv, "place_own_0")]
    lands += [_place_own(layer_items[l], shards, posv, "place_own_%d" % l) for l in range(1, DEPTH)]
    first = _gather_now(first_items, shards, lands[0])
    flight = {'shards': {n: shards[n] for n in big}}

    full = {n: wl[n] for n in REPLICATED}

    def start(l, token):
        sems, thru, flight['lands'], tok = _gather_start(layer_items[l], flight['shards'], lands[l], token,
                                                         "gather_start_%d" % l)
        flight['sems'] = list(sems)
        flight['shards'].update(thru)
        full['fwd', l - 1] = tok

    start(1, first[-1])

    for n, a in zip(vec, _unpack(first[-1], [wl[n].shape for n in vec], lead=True)):
        full[n] = _merge_shards(a, SHARD_AXIS[n])
    full['ffn_conv_w'] = _block_pad(full['ffn_conv_w'], 2, 4, FF_PAD)
    full['ffn_conv_b'] = _block_pad(full['ffn_conv_b'], 1, 4, FF_PAD)
    for n in big:
        full[n] = {}

    def fetch(l, x_in):
        items = layer_items[l]
        if l == 0:
            got = first[:len(items)]
        else:
            flight['shards'], got = _gather_wait(items, flight['sems'], flight['shards'], flight['lands'], x_in,
                                                 "gather_wait_%d" % l)
            got = _gather_forward(items, got, "gather_forward_%d" % l)
            if l + 1 < DEPTH:
                start(l + 1, got[0])
        for (nm, li), a in zip(items, got):
            if nm == 'ab_w_in':
                a = _ab_permute(_block_unpad(a, 1, N_DEV, AB_SHARD))
            elif nm in ('ab_w_out', 'c_w_out'):
                a = a.reshape(D_MODEL, D_MODEL)
            elif nm in ('c_gate_a_w', 'c_gate_x_w'):
                a = a.reshape(HEADS, LRU_BLOCK, LRU_BLOCK)
            elif nm == 'ffn_w_down':
                a = a.reshape(D_FFP, D_MODEL)
            full[nm][li] = a

    cpos = lax.axis_index("c").astype(jnp.int32).reshape(1)
    mine = (2 * lax.axis_index("x") + lax.axis_index("y")).astype(jnp.int32).reshape(1)
    pending = {}

    def push(l, g):
        fulls = {}
        for nm, _ in layer_items[l]:
            a = g[nm].astype(bf16)
            if nm == 'ab_w_in':
                a = _block_pad(_ab_unpermute(a), 1, N_DEV, AB_SHARD_PAD)
            fulls[nm] = a.reshape((1,) + BLOCK_LAYOUT[nm][0][1:])
        pair, _ = _pair_exchange(fulls, jnp.zeros((8, 128), f32), "_%d" % l)
        chips = [_chip_sum(nm, fulls[nm], pair[nm], cpos, "_%d" % l) for nm in fulls]
        sems, chips, crosses, tok = _cross_start(chips, "grad_cross_start_%d" % l)
        pending[l] = (sems, chips, crosses)
        full['bwd', l - 1] = tok

    loss, dx, grads = _local_step(x[0], loss_target[0], full, fetch, push)

    grads['ffn_conv_w'] = _block_unpad(grads['ffn_conv_w'], 2, 4, FF_SHARD)
    grads['ffn_conv_b'] = _block_unpad(grads['ffn_conv_b'], 1, 4, FF_SHARD)
    fulls = {'vec': _pack([_split_shards(grads[n], SHARD_AXIS[n]) for n in vec], lead=N_DEV)}
    chip, rep_chip, recv, rrep = _exchange_grads(fulls, _pack([grads[n] for n in REPLICATED]))
    res = {}
    stacked = {}
    after = full['bwd', -1]
    for l in (3, 2, 1, 0):
        sems, chips, lands = pending[l]
        chips, lands = _cross_wait(sems, chips, lands, after, "grad_cross_wait_%d" % l)
        for (n, li), own, parts in zip(layer_items[l], chips, lands):
            shp = wl[n].shape
            nl, c = shp[0], shp[-1]
            r = int(np.prod(shp[1:-1]))
            stacked[n] = _sum_adamw_layer(parts.reshape(4, r, -1), own.reshape(4, r, -1), mine, wl[n].reshape(nl, r, c),
                                          ml[n].reshape(nl, r, c), vl[n].reshape(nl, r, c), li, stacked.get(n),
                                          "adamw_%s_%d" % (n, li))
        if l == 1:
            after = stacked['ffn_w_up'][0]
    for n in big:
        for kind, o in zip(("grad", "delta", "new_m", "new_v"), stacked[n]):
            res[kind, n] = o.reshape(wl[n].shape)
    for names, parts, own, tag in ((vec, recv['vec'], chip['vec'], "adamw_vectors"),
                                   (REPLICATED, rrep, rep_chip, "adamw_replicated")):
        outs = _sum_adamw(parts, own, mine, _pack([wl[n] for n in names]), _pack([ml[n] for n in names]),
                          _pack([vl[n] for n in names]), tag)
        for kind, o in zip(("grad", "delta", "new_m", "new_v"), outs):
            for n, a in zip(names, _unpack(o, [wl[n].shape for n in names])):
                res[kind, n] = a

    loss = lax.psum(loss, ("x", "y", "c"))
    return (loss, dx[None], *[res[kind, n] for kind in ("grad", "delta", "new_m", "new_v") for n in WEIGHTS])
```

```python
import functools

import numpy as np
import jax
import jax.numpy as jnp
from jax import lax
from jax.experimental import pallas as pl
from jax.experimental.pallas import tpu as pltpu

f32 = jnp.float32
bf16 = jnp.bfloat16
HI = lax.Precision.HIGHEST
MESH = pl.DeviceIdType.MESH

N_DEV = 8
D_MODEL = 1024
DEPTH = 4
EPS = 1e-6
F_FLOOR = 1e-30
HEADS = 4
HEAD_DIM = 128
GDN_WIDTH = 512
GDN_CONV = 4
GDN_CHUNK = 64
HGRN_CHUNK = 16
HGRN_STEP = 128
MIX_WIDTH = 1024
AB_COLS = 4104
AB_PAD = 4224
LRU_WIDTH = 1024
LRU_BLOCK = 256
LRU_CONV = 4
RG_C = 8.0
D_FF = 2816
FF_SHARD = 704
FF_PAD = 768
D_FFP = 4 * FF_PAD
FF_ROWS = 352
AB_SHARD, AB_SHARD_PAD = 513, 640
FFN_CONV = 3
ADAM_LR, ADAM_B1, ADAM_B2, ADAM_EPS, ADAM_WD, ADAM_STEP = 0.001, 0.9, 0.999, 1e-08, 0.01, 10
VMEM_LIMIT = 56 * 1024 * 1024
PACK_LANES = 512
PACK_ROWS = 256

OFF_Q, OFF_K, OFF_V, OFF_Z, OFF_QB, OFF_FB, OFF_IB, OFF_GB, OFF_BA = 0, 512, 1024, 1536, 2048, 2560, 3072, 3584, 4096

WEIGHTS = ['norm_mix', 'norm_ffn', 'norm_final', 'ab_w_in', 'gdn_conv_w', 'gdn_a_log', 'gdn_dt_bias', 'gdn_norm',
           'hgrn_lower_bounds', 'hgrn_norm', 'ab_w_out', 'c_w_in', 'c_conv_w', 'c_conv_b', 'c_gate_a_w', 'c_gate_a_b',
           'c_gate_x_w', 'c_gate_x_b', 'c_lambda', 'c_w_out', 'ffn_w_up', 'ffn_conv_w', 'ffn_conv_b', 'ffn_w_down']
SHARD_AXIS = {'norm_mix': None, 'norm_ffn': None, 'norm_final': None, 'ab_w_in': 2, 'gdn_conv_w': 2, 'gdn_a_log': None,
              'gdn_dt_bias': None, 'gdn_norm': None, 'hgrn_lower_bounds': None, 'hgrn_norm': None, 'ab_w_out': 1,
              'c_w_in': 2, 'c_conv_w': 2, 'c_conv_b': 1, 'c_gate_a_w': 2, 'c_gate_a_b': 1, 'c_gate_x_w': 2,
              'c_gate_x_b': 1, 'c_lambda': 1, 'c_w_out': 1, 'ffn_w_up': 2, 'ffn_conv_w': 2, 'ffn_conv_b': None,
              'ffn_w_down': 1}
MATMUL_WEIGHTS = ('ab_w_in', 'ab_w_out', 'c_w_in', 'c_gate_a_w', 'c_gate_x_w', 'c_w_out', 'ffn_w_up', 'ffn_w_down')
SHARDED = [n for n in WEIGHTS if SHARD_AXIS[n] is not None]
REPLICATED = [n for n in WEIGHTS if SHARD_AXIS[n] is None]


def _tile(n, prefs=(512, 384, 256, 128)):
    for p in prefs:
        if n % p == 0:
            return p
    return n


def _cparams(sem=None):
    kw = dict(vmem_limit_bytes=VMEM_LIMIT)
    if sem is not None:
        kw['dimension_semantics'] = sem
    return pltpu.CompilerParams(**kw)


def _sds(shape, dtype):
    return jax.ShapeDtypeStruct(tuple(shape), dtype)


def _sigmoid(x):
    return 1.0 / (1.0 + jnp.exp(-x))


def _silu(x):
    return x * (0.5 * jnp.tanh(0.5 * x) + 0.5)


def _log1p(x):
    u = 1.0 + x
    return jnp.where(u == 1.0, x, jnp.log(u) * (x / jnp.where(u == 1.0, 1.0, u - 1.0)))


def _softplus(x):
    return jnp.maximum(x, 0.0) + _log1p(jnp.exp(-jnp.abs(x)))


def _expm1(x):
    small = jnp.abs(x) < 0.05
    xs = jnp.where(small, x, 0.0)
    series = xs * (1.0 + xs * (0.5 + xs * (1.0 / 6.0 + xs * (1.0 / 24.0 + xs * (1.0 / 120.0)))))
    return jnp.where(small, series, jnp.exp(x) - 1.0)


def _gelu(x):
    return 0.5 * x * (1.0 + jnp.tanh(0.7978845608028654 * (x + 0.044715 * x * x * x)))


def _rms(x, gain):
    return x * lax.rsqrt(jnp.mean(x * x, axis=-1, keepdims=True) + EPS) * gain


def _dot(a, b, dims=((1,), (0,)), precision=None):
    return lax.dot_general(a, b, (dims, ((), ())), precision=precision, preferred_element_type=f32)


def _bdot(a, b, dims=((1,), (0,))):
    return _dot(a.astype(bf16), b.astype(bf16), dims)


NT = ((1,), (1,))
TN = ((0,), (0,))


def _shift_down(x, k):
    if k == 0:
        return x
    row = lax.broadcasted_iota(jnp.int32, x.shape, 0)
    return jnp.where(row >= k, pltpu.roll(x, k, 0), 0.0)


def _shift_up(x, k, fill=0.0):
    if k == 0:
        return x
    n = x.shape[0]
    row = lax.broadcasted_iota(jnp.int32, x.shape, 0)
    return jnp.where(row < n - k, pltpu.roll(x, n - k, 0), fill)


def _conv_fwd(x, w_ref, width):
    acc = w_ref[width - 1:width, :] * x
    for k in range(width - 1):
        acc = acc + w_ref[k:k + 1, :] * _shift_down(x, width - 1 - k)
    return acc


def _conv_bwd(x, dout, w_ref, dw_ref, width):
    dx = w_ref[width - 1:width, :] * dout
    dw_ref[width - 1:width, :] = jnp.sum(dout * x, axis=0, keepdims=True)
    for k in range(width - 1):
        s = width - 1 - k
        dx = dx + w_ref[k:k + 1, :] * _shift_up(dout, s)
        dw_ref[k:k + 1, :] = jnp.sum(dout * _shift_down(x, s), axis=0, keepdims=True)
    return dx


MM_VMEM_BUDGET = 36 * 1024 * 1024
MM_MAX_TILE = 1024 * 1024


def _mm_tiles(m, n, k, out_bytes):
    best = None
    for tm in (1024, 512, 384, 256, 128):
        if m % tm:
            continue
        for tn in range(1536, 0, -128):
            if n % tn or tm * tn > MM_MAX_TILE:
                continue
            score = (tm * tn, min(tm, tn))
            if 2 * (tm * k * 2 + k * tn * 2 + tm * tn * out_bytes) <= MM_VMEM_BUDGET and (best is None or score > best[0]):
                best = (score, tm, tn)
    return (best[1], best[2]) if best else (_tile(m), _tile(n))


def _mm(a, b, *, ta=False, tb=False, add=None, out_dtype=f32, name):
    m, k = (a.shape[1], a.shape[0]) if ta else a.shape
    n = b.shape[0] if tb else b.shape[1]
    tm, tn = _mm_tiles(m, n, k, jnp.dtype(out_dtype).itemsize + (4 if add is not None else 0))
    dims = ((0 if ta else 1,), (1 if tb else 0,))

    def body(*refs):
        a_ref, b_ref = refs[0], refs[1]
        o_ref = refs[-1]
        r = _dot(a_ref[...], b_ref[...], dims)
        if add is not None:
            r = r + refs[2][...]
        o_ref[...] = r.astype(out_dtype)

    a_spec = pl.BlockSpec((k, tm), lambda j, i: (0, i)) if ta else pl.BlockSpec((tm, k), lambda j, i: (i, 0))
    b_spec = pl.BlockSpec((tn, k), lambda j, i: (j, 0)) if tb else pl.BlockSpec((k, tn), lambda j, i: (0, j))
    o_spec = pl.BlockSpec((tm, tn), lambda j, i: (i, j))
    ins, specs = [a, b], [a_spec, b_spec]
    if add is not None:
        ins.append(add)
        specs.append(o_spec)
    return pl.pallas_call(body, name=name, grid=(n // tn, m // tm), in_specs=specs, out_specs=o_spec,
                          out_shape=_sds((m, n), out_dtype), compiler_params=_cparams(("parallel", "parallel")))(*ins)


def _rms_fwd(x, gain, name):
    t, d = x.shape
    tr = _tile(t, (256, 128))

    def body(x_ref, g_ref, h_ref):
        h_ref[...] = _rms(x_ref[...], g_ref[...]).astype(bf16)

    return pl.pallas_call(body, name=name, grid=(t // tr,),
                          in_specs=[pl.BlockSpec((tr, d), lambda i: (i, 0)), pl.BlockSpec((1, d), lambda i: (0, 0))],
                          out_specs=pl.BlockSpec((tr, d), lambda i: (i, 0)), out_shape=_sds((t, d), bf16),
                          compiler_params=_cparams(("parallel",)))(x, gain)


def _rms_bwd(x, gain, dh, dres, name):
    t, d = x.shape
    tr = _tile(t, (256, 128))

    def body(x_ref, g_ref, dh_ref, dres_ref, dx_ref, dxb_ref, dg_ref):
        _, vjp = jax.vjp(_rms, x_ref[...], g_ref[...])
        dx, dg = vjp(dh_ref[...])
        dx = dx + dres_ref[...]
        dx_ref[...] = dx
        dxb_ref[...] = dx.astype(bf16)

        @pl.when(pl.program_id(0) == 0)
        def _():
            dg_ref[...] = jnp.zeros_like(dg_ref)

        dg_ref[...] += dg

    row = pl.BlockSpec((tr, d), lambda i: (i, 0))
    vec = pl.BlockSpec((1, d), lambda i: (0, 0))
    return pl.pallas_call(body, name=name, grid=(t // tr,), in_specs=[row, vec, row, row], out_specs=[row, row, vec],
                          out_shape=[_sds((t, d), f32), _sds((t, d), bf16), _sds((1, d), f32)],
                          compiler_params=_cparams(("arbitrary",)))(x, gain, dh, dres)


def _loss_head(x, gain, target):
    t, d = x.shape
    tr = _tile(t, (256, 128))

    def f(xv, g, tgt):
        err = _rms(xv, g) - tgt
        return 0.5 * jnp.sum(jnp.mean(err * err, axis=-1, keepdims=True), axis=0, keepdims=True)

    def body(x_ref, g_ref, t_ref, loss_ref, dx_ref, dxb_ref, dg_ref):
        loss, vjp = jax.vjp(lambda xv, g: f(xv, g, t_ref[...]), x_ref[...], g_ref[...])
        dx, dg = vjp(jnp.ones((1, 1), f32))
        dx_ref[...] = dx
        dxb_ref[...] = dx.astype(bf16)

        @pl.when(pl.program_id(0) == 0)
        def _():
            dg_ref[...] = jnp.zeros_like(dg_ref)
            loss_ref[...] = jnp.zeros_like(loss_ref)

        dg_ref[...] += dg
        loss_ref[...] += jnp.broadcast_to(loss, loss_ref.shape)

    row = pl.BlockSpec((tr, d), lambda i: (i, 0))
    vec = pl.BlockSpec((1, d), lambda i: (0, 0))
    one = pl.BlockSpec((8, 128), lambda i: (0, 0))
    return pl.pallas_call(body, name="loss_head", grid=(t // tr,), in_specs=[row, vec, row],
                          out_specs=[one, row, row, vec],
                          out_shape=[_sds((8, 128), f32), _sds((t, d), f32), _sds((t, d), bf16), _sds((1, d), f32)],
                          compiler_params=_cparams(("arbitrary",)))(x, gain, target)


def _ffn_act_fwd(u, conv_w, conv_b):
    t = u.shape[0]
    tc = FF_PAD // 2
    nb = D_FFP // tc

    def body(g_ref, v_ref, w_ref, b_ref, a_ref):
        gc = _conv_fwd(g_ref[...], w_ref, FFN_CONV) + b_ref[...]
        a_ref[...] = (_silu(gc) * v_ref[...]).astype(bf16)

    return pl.pallas_call(
        body, name="ffn_act_fwd", grid=(nb,),
        in_specs=[pl.BlockSpec((t, tc), lambda j: (0, j)), pl.BlockSpec((t, tc), lambda j: (0, j + nb)),
                  pl.BlockSpec((FFN_CONV, tc), lambda j: (0, j)), pl.BlockSpec((1, tc), lambda j: (0, j))],
        out_specs=pl.BlockSpec((t, tc), lambda j: (0, j)), out_shape=_sds((t, D_FFP), bf16),
        compiler_params=_cparams(("parallel",)))(u, u, conv_w, conv_b)


def _ffn_act_bwd(u, conv_w, conv_b, da):
    t = u.shape[0]
    tc = FF_PAD // 2
    nb = D_FFP // tc

    def act(gc, val):
        return _silu(gc) * val

    def body(g_ref, v_ref, w_ref, b_ref, da_ref, a_ref, dg_ref, dv_ref, dw_ref, db_ref):
        gp = g_ref[...]
        gc = _conv_fwd(gp, w_ref, FFN_CONV) + b_ref[...]
        a, vjp = jax.vjp(act, gc, v_ref[...])
        dgc, dval = vjp(da_ref[...])
        a_ref[...] = a.astype(bf16)
        dv_ref[...] = dval.astype(bf16)
        db_ref[...] = jnp.sum(dgc, axis=0, keepdims=True)
        dg_ref[...] = _conv_bwd(gp, dgc, w_ref, dw_ref, FFN_CONV).astype(bf16)

    col = pl.BlockSpec((t, tc), lambda j: (0, j))
    return pl.pallas_call(
        body, name="ffn_act_bwd", grid=(nb,),
        in_specs=[col, pl.BlockSpec((t, tc), lambda j: (0, j + nb)), pl.BlockSpec((FFN_CONV, tc), lambda j: (0, j)),
                  pl.BlockSpec((1, tc), lambda j: (0, j)), col],
        out_specs=[col, col, col, pl.BlockSpec((FFN_CONV, tc), lambda j: (0, j)), pl.BlockSpec((1, tc), lambda j: (0, j))],
        out_shape=[_sds((t, D_FFP), bf16), _sds((t, D_FFP), bf16), _sds((t, D_FFP), bf16), _sds((FFN_CONV, D_FFP), f32),
                   _sds((1, D_FFP), f32)],
        compiler_params=_cparams(("parallel",)))(u, u, conv_w, conv_b, da)


def _lru_gates(xc, ra, ia, lam):
    r = _sigmoid(ra)
    i = _sigmoid(ia)
    log_a = -RG_C * r * _softplus(-lam)
    a = jnp.exp(log_a)
    u = jnp.sqrt(jnp.maximum(-_expm1(2.0 * log_a), 0.0)) * (i * xc)
    return a, u


def _lin_scan(a, u):
    n = a.shape[0]
    row = lax.broadcasted_iota(jnp.int32, a.shape, 0)
    s = 1
    while s < n:
        keep = row >= s
        u = a * jnp.where(keep, pltpu.roll(u, s, 0), 0.0) + u
        a = a * jnp.where(keep, pltpu.roll(a, s, 0), 1.0)
        s *= 2
    return u


def _rev_scan(a_next, d):
    n = d.shape[0]
    row = lax.broadcasted_iota(jnp.int32, d.shape, 0)
    a = a_next
    s = 1
    while s < n:
        keep = row < n - s
        d = a * jnp.where(keep, pltpu.roll(d, n - s, 0), 0.0) + d
        a = a * jnp.where(keep, pltpu.roll(a, n - s, 0), 1.0)
        s *= 2
    return d


def _col_conv_fwd(p, col_off, conv_w, conv_b, width, tc, name):
    t = p.shape[0]
    c = conv_w.shape[1]
    ob = col_off // tc

    def body(x_ref, w_ref, b_ref, o_ref):
        o_ref[...] = _conv_fwd(x_ref[...], w_ref, width) + b_ref[...]

    return pl.pallas_call(
        body, name=name, grid=(c // tc,),
        in_specs=[pl.BlockSpec((t, tc), lambda j: (0, j + ob)), pl.BlockSpec((width, tc), lambda j: (0, j)),
                  pl.BlockSpec((1, tc), lambda j: (0, j))],
        out_specs=pl.BlockSpec((t, tc), lambda j: (0, j)), out_shape=_sds((t, c), f32),
        compiler_params=_cparams(("parallel",)))(p, conv_w, conv_b)


def _col_conv_bwd(p, col_off, conv_w, dxc, width, tc, name):
    t = p.shape[0]
    c = conv_w.shape[1]
    ob = col_off // tc

    def body(x_ref, w_ref, d_ref, dx_ref, dw_ref, db_ref):
        d = d_ref[...]
        db_ref[...] = jnp.sum(d, axis=0, keepdims=True)
        dx_ref[...] = _conv_bwd(x_ref[...], d, w_ref, dw_ref, width).astype(bf16)

    col = pl.BlockSpec((t, tc), lambda j: (0, j))
    return pl.pallas_call(
        body, name=name, grid=(c // tc,),
        in_specs=[pl.BlockSpec((t, tc), lambda j: (0, j + ob)), pl.BlockSpec((width, tc), lambda j: (0, j)), col],
        out_specs=[col, pl.BlockSpec((width, tc), lambda j: (0, j)), pl.BlockSpec((1, tc), lambda j: (0, j))],
        out_shape=[_sds((t, c), bf16), _sds((width, c), f32), _sds((1, c), f32)],
        compiler_params=_cparams(("parallel",)))(p, conv_w, dxc)


def _lru_fwd(p, xc, wa, ba, wx, bx, lam):
    t = p.shape[0]
    bw = LRU_BLOCK

    def body(y_ref, xc_ref, wa_ref, ba_ref, wx_ref, bx_ref, lam_ref, out_ref, hs_ref, a_ref):
        xc_v = xc_ref[...]
        xb = xc_v.astype(bf16)
        ra = _dot(xb, wa_ref[0]) + ba_ref[...]
        ia = _dot(xb, wx_ref[0]) + bx_ref[...]
        a, u = _lru_gates(xc_v, ra, ia, lam_ref[...])
        a_ref[...] = a
        hs = _lin_scan(a, u)
        hs_ref[...] = hs
        out_ref[...] = (hs * _gelu(y_ref[...])).astype(bf16)

    col = pl.BlockSpec((t, bw), lambda h: (0, h))
    vec = pl.BlockSpec((1, bw), lambda h: (0, h))
    mat = pl.BlockSpec((1, bw, bw), lambda h: (h, 0, 0))
    return pl.pallas_call(
        body, name="lru_fwd", grid=(HEADS,), in_specs=[col, col, mat, vec, mat, vec, vec], out_specs=[col, col, col],
        out_shape=[_sds((t, LRU_WIDTH), bf16), _sds((t, LRU_WIDTH), f32), _sds((t, LRU_WIDTH), f32)],
        compiler_params=_cparams(("parallel",)))(p, xc, wa, ba, wx, bx, lam)


def _lru_bwd_scan(p, a, hs, dout):
    t = p.shape[0]
    bw = LRU_BLOCK

    def body(y_ref, a_ref, hs_ref, do_ref, dy_ref, da_ref, du_ref):
        hs_v = hs_ref[...]
        do = do_ref[...]
        gate, vjp = jax.vjp(_gelu, y_ref[...])
        dy_ref[...] = vjp(do * hs_v)[0].astype(bf16)
        g = _rev_scan(_shift_up(a_ref[...], 1), do * gate)
        du_ref[...] = g
        da_ref[...] = g * _shift_down(hs_v, 1)

    col = pl.BlockSpec((t, bw), lambda h: (0, h))
    return pl.pallas_call(
        body, name="lru_bwd_scan", grid=(HEADS,), in_specs=[col, col, col, col], out_specs=[col, col, col],
        out_shape=[_sds((t, LRU_WIDTH), bf16), _sds((t, LRU_WIDTH), f32), _sds((t, LRU_WIDTH), f32)],
        compiler_params=_cparams(("parallel",)))(p, a, hs, dout)


def _lru_bwd_gates(xc, da, du, wa, ba, wx, bx, lam):
    t = xc.shape[0]
    bw = LRU_BLOCK
    tr = _tile(t, (512, 256, 128))

    def body(xc_ref, da_ref, du_ref, wa_ref, ba_ref, wx_ref, bx_ref, lam_ref,
             dxc_ref, dwa_ref, dwx_ref, dba_ref, dbx_ref, dlam_ref):
        xc_v = xc_ref[...]
        xb = xc_v.astype(bf16)
        ra = _dot(xb, wa_ref[0]) + ba_ref[...]
        ia = _dot(xb, wx_ref[0]) + bx_ref[...]
        _, vjp = jax.vjp(_lru_gates, xc_v, ra, ia, lam_ref[...])
        dxc, dra, dia, dlam = vjp((da_ref[...], du_ref[...]))
        drb, dib = dra.astype(bf16), dia.astype(bf16)
        dxc_ref[...] = dxc + _dot(drb, wa_ref[0], NT) + _dot(dib, wx_ref[0], NT)

        @pl.when(pl.program_id(1) == 0)
        def _():
            dwa_ref[...] = jnp.zeros_like(dwa_ref)
            dwx_ref[...] = jnp.zeros_like(dwx_ref)
            dba_ref[...] = jnp.zeros_like(dba_ref)
            dbx_ref[...] = jnp.zeros_like(dbx_ref)
            dlam_ref[...] = jnp.zeros_like(dlam_ref)

        dwa_ref[0] += _dot(xb, drb, TN)
        dwx_ref[0] += _dot(xb, dib, TN)
        dba_ref[...] += jnp.sum(dra, axis=0, keepdims=True)
        dbx_ref[...] += jnp.sum(dia, axis=0, keepdims=True)
        dlam_ref[...] += dlam

    tile = pl.BlockSpec((tr, bw), lambda h, i: (i, h))
    vec = pl.BlockSpec((1, bw), lambda h, i: (0, h))
    mat = pl.BlockSpec((1, bw, bw), lambda h, i: (h, 0, 0))
    return pl.pallas_call(
        body, name="lru_bwd_gates", grid=(HEADS, t // tr), in_specs=[tile, tile, tile, mat, vec, mat, vec, vec],
        out_specs=[tile, mat, mat, vec, vec, vec],
        out_shape=[_sds((t, LRU_WIDTH), f32), _sds((HEADS, bw, bw), f32), _sds((HEADS, bw, bw), f32),
                   _sds((1, LRU_WIDTH), f32), _sds((1, LRU_WIDTH), f32), _sds((1, LRU_WIDTH), f32)],
        compiler_params=_cparams(("parallel", "arbitrary")))(xc, da, du, wa, ba, wx, bx, lam)


def _gdn_pre_fn(cq, ck, cv, ba, alog, dtb, h):
    q, k, v = _silu(cq), _silu(ck), _silu(cv)
    q = q * lax.rsqrt(jnp.sum(q * q, axis=-1, keepdims=True) + EPS) * (HEAD_DIM ** -0.5)
    k = k * lax.rsqrt(jnp.sum(k * k, axis=-1, keepdims=True) + EPS)
    lane = lax.broadcasted_iota(jnp.int32, (1, HEAD_DIM), 1)
    mb = (lane == h).astype(f32)
    ma = (lane == HEADS + h).astype(f32)
    beta_raw = jnp.sum(ba * mb, axis=-1, keepdims=True)
    alpha = jnp.sum(ba * ma, axis=-1, keepdims=True)
    al = jnp.sum(alog * mb, axis=-1, keepdims=True)
    db = jnp.sum(dtb * mb, axis=-1, keepdims=True)
    beta = _sigmoid(beta_raw)
    g = -jnp.exp(al) * _softplus(alpha + db)
    return q, k, v, jnp.broadcast_to(beta, q.shape), jnp.broadcast_to(g, q.shape)


def _gdn_pre_fwd(p, conv_w, alog, dtb):
    t = p.shape[0]
    hd = HEAD_DIM

    def body(pq_ref, pk_ref, pv_ref, ba_ref, wq_ref, wk_ref, wv_ref, al_ref, dt_ref, q_ref, k_ref, v_ref, b_ref, g_ref):
        h = pl.program_id(0)
        cq = _conv_fwd(pq_ref[...], wq_ref, GDN_CONV)
        ck = _conv_fwd(pk_ref[...], wk_ref, GDN_CONV)
        cv = _conv_fwd(pv_ref[...], wv_ref, GDN_CONV)
        q, k, v, be, ge = _gdn_pre_fn(cq, ck, cv, ba_ref[...], al_ref[...], dt_ref[...], h)
        q_ref[...], k_ref[...], v_ref[...], b_ref[...], g_ref[...] = q, k, v, be, ge

    def pcol(off):
        return pl.BlockSpec((t, hd), lambda h: (0, h + off // hd))

    def wcol(off):
        return pl.BlockSpec((GDN_CONV, hd), lambda h: (0, h + off // hd))

    vec = pl.BlockSpec((1, hd), lambda h: (0, 0))
    out = pl.BlockSpec((t, hd), lambda h: (0, h))
    return pl.pallas_call(
        body, name="gdn_pre_fwd", grid=(HEADS,),
        in_specs=[pcol(OFF_Q), pcol(OFF_K), pcol(OFF_V), pl.BlockSpec((t, hd), lambda h: (0, OFF_BA // hd)),
                  wcol(0), wcol(GDN_WIDTH), wcol(2 * GDN_WIDTH), vec, vec],
        out_specs=[out] * 5, out_shape=[_sds((t, GDN_WIDTH), f32)] * 5,
        compiler_params=_cparams(("parallel",)))(p, p, p, p, conv_w, conv_w, conv_w, alog, dtb)


def _gdn_pre_bwd(p, conv_w, alog, dtb, dq, dk, dv, dbe, dge):
    t = p.shape[0]
    hd = HEAD_DIM

    def body(pq_ref, pk_ref, pv_ref, ba_ref, wq_ref, wk_ref, wv_ref, al_ref, dt_ref,
             dq_ref, dk_ref, dv_ref, dbe_ref, dge_ref,
             opq_ref, opk_ref, opv_ref, dba_ref, dwq_ref, dwk_ref, dwv_ref, dal_ref, ddt_ref):
        h = pl.program_id(0)
        pq, pk, pv = pq_ref[...], pk_ref[...], pv_ref[...]
        cq = _conv_fwd(pq, wq_ref, GDN_CONV)
        ck = _conv_fwd(pk, wk_ref, GDN_CONV)
        cv = _conv_fwd(pv, wv_ref, GDN_CONV)
        _, vjp = jax.vjp(functools.partial(_gdn_pre_fn, h=h), cq, ck, cv, ba_ref[...], al_ref[...], dt_ref[...])
        dcq, dck, dcv, dba, dal, ddt = vjp((dq_ref[...], dk_ref[...], dv_ref[...], dbe_ref[...], dge_ref[...]))
        opq_ref[...] = _conv_bwd(pq, dcq, wq_ref, dwq_ref, GDN_CONV).astype(bf16)
        opk_ref[...] = _conv_bwd(pk, dck, wk_ref, dwk_ref, GDN_CONV).astype(bf16)
        opv_ref[...] = _conv_bwd(pv, dcv, wv_ref, dwv_ref, GDN_CONV).astype(bf16)

        @pl.when(h == 0)
        def _():
            dba_ref[...] = jnp.zeros_like(dba_ref)
            dal_ref[...] = jnp.zeros_like(dal_ref)
            ddt_ref[...] = jnp.zeros_like(ddt_ref)

        dba_ref[...] += dba
        dal_ref[...] += dal
        ddt_ref[...] += ddt

    def pcol(off):
        return pl.BlockSpec((t, hd), lambda h: (0, h + off // hd))

    def wcol(off):
        return pl.BlockSpec((GDN_CONV, hd), lambda h: (0, h + off // hd))

    vec = pl.BlockSpec((1, hd), lambda h: (0, 0))
    col = pl.BlockSpec((t, hd), lambda h: (0, h))
    full = pl.BlockSpec((t, hd), lambda h: (0, 0))
    wout = pl.BlockSpec((GDN_CONV, hd), lambda h: (0, h))
    return pl.pallas_call(
        body, name="gdn_pre_bwd", grid=(HEADS,),
        in_specs=[pcol(OFF_Q), pcol(OFF_K), pcol(OFF_V), pl.BlockSpec((t, hd), lambda h: (0, OFF_BA // hd)),
                  wcol(0), wcol(GDN_WIDTH), wcol(2 * GDN_WIDTH), vec, vec, col, col, col, col, col],
        out_specs=[col, col, col, full, wout, wout, wout, vec, vec],
        out_shape=[_sds((t, GDN_WIDTH), bf16)] * 3 + [_sds((t, hd), f32)] + [_sds((GDN_CONV, GDN_WIDTH), f32)] * 3
        + [_sds((1, hd), f32)] * 2,
        compiler_params=_cparams(("arbitrary",)))(p, p, p, p, conv_w, conv_w, conv_w, alog, dtb, dq, dk, dv, dbe, dge)


BNN = (((2,), (1,)), ((0,), (0,)))
BNT = (((2,), (2,)), ((0,), (0,)))
BTN = (((1,), (1,)), ((0,), (0,)))


def _hdot(a, b, dn=BNN, precision=None):
    return lax.dot_general(a, b, dn, precision=precision, preferred_element_type=f32)


def _hbdot(a, b, dn=BNN):
    return _hdot(a.astype(bf16), b.astype(bf16), dn)


def _tri_inverse(a):
    c = a.shape[-1]
    r = lax.broadcasted_iota(jnp.int32, (c, c), 0)
    col = lax.broadcasted_iota(jnp.int32, (c, c), 1)
    m = -a
    inv = jnp.where(r == col, 1.0, 0.0) + m
    s = 2
    while s < c:
        m = _hdot(m, m, precision=HI)
        inv = inv + _hdot(inv, m, precision=HI)
        s *= 2
    return inv


@jax.custom_vjp
def _saved_inverse(a, inv):
    return inv


def _saved_inverse_fwd(a, inv):
    return inv, inv


def _saved_inverse_bwd(inv, dinv):
    return -_hdot(_hdot(inv, dinv, BTN, precision=HI), inv, BNT, precision=HI), jnp.zeros_like(inv)


_saved_inverse.defvjp(_saved_inverse_fwd, _saved_inverse_bwd)


def _gdn_chunk(s, q, k, v, ge, be, tinv=None):
    nh, c, _ = q.shape
    r = lax.broadcasted_iota(jnp.int32, (c, c), 0)
    col = lax.broadcasted_iota(jnp.int32, (c, c), 1)
    causal = r >= col
    tri = jnp.broadcast_to(causal.astype(f32), (nh, c, c))
    gc = _hdot(tri, ge, precision=HI)
    gcc = gc[:, :, :c]
    gcr = jnp.swapaxes(gc, 1, 2)[:, :c, :]
    decay = jnp.where(causal, jnp.exp(jnp.where(causal, gcc - gcr, 0.0)), 0.0)
    kb = k * be
    lower = jnp.where(r > col, _hbdot(kb, k, BNT) * decay, 0.0)
    tinv = _tri_inverse(lower) if tinv is None else _saved_inverse(lower, tinv)
    egc = jnp.exp(gc)
    u = _hdot(tinv, v * be, precision=HI)
    w = _hdot(tinv, kb * egc, precision=HI)
    attn = _hbdot(q, k, BNT) * decay
    gl = gc[:, c - 1:c, :]
    v_new = u - _hbdot(w, s)
    o = _hbdot(q * egc, s) + _hbdot(attn, v_new)
    s_new = s * jnp.exp(gl) + _hbdot(k * jnp.exp(gl - gc), v_new, BTN)
    return o, s_new, tinv


def _heads_major(ref):
    return jnp.stack([ref[:, h * HEAD_DIM:(h + 1) * HEAD_DIM] for h in range(HEADS)])


def _gdn_core_fwd(q, k, v, ge, be):
    t = q.shape[0]
    c, hd = GDN_CHUNK, HEAD_DIM
    n = t // c

    def body(q_ref, k_ref, v_ref, g_ref, b_ref, o_ref, st_ref, ti_ref, s_ref):
        @pl.when(pl.program_id(0) == 0)
        def _():
            s_ref[...] = jnp.zeros_like(s_ref)

        s = s_ref[...]
        st_ref[:, 0] = s
        o, s_new, tinv = _gdn_chunk(s, *[_heads_major(r) for r in (q_ref, k_ref, v_ref, g_ref, b_ref)])
        ti_ref[0] = tinv
        for h in range(HEADS):
            o_ref[:, h * hd:(h + 1) * hd] = o[h]
        s_ref[...] = s_new

    tile = pl.BlockSpec((c, GDN_WIDTH), lambda i: (i, 0))
    return pl.pallas_call(
        body, name="gdn_core_fwd", grid=(n,), in_specs=[tile] * 5,
        out_specs=[tile, pl.BlockSpec((HEADS, 1, hd, hd), lambda i: (0, i, 0, 0)),
                   pl.BlockSpec((1, HEADS, c, c), lambda i: (i, 0, 0, 0))],
        out_shape=[_sds((t, GDN_WIDTH), f32), _sds((HEADS, n, hd, hd), f32), _sds((n, HEADS, c, c), f32)],
        scratch_shapes=[pltpu.VMEM((HEADS, hd, hd), f32)],
        compiler_params=_cparams(("arbitrary",)))(q, k, v, ge, be)


def _gdn_core_bwd(q, k, v, ge, be, states, do):
    t = q.shape[0]
    c, hd = GDN_CHUNK, HEAD_DIM
    n = t // c
    states, tinvs = states

    def body(q_ref, k_ref, v_ref, g_ref, b_ref, st_ref, ti_ref, do_ref, dq_ref, dk_ref, dv_ref, dg_ref, db_ref, ds_ref):
        @pl.when(pl.program_id(0) == 0)
        def _():
            ds_ref[...] = jnp.zeros_like(ds_ref)

        tinv = ti_ref[0]
        _, vjp = jax.vjp(lambda *a: _gdn_chunk(*a, tinv=tinv)[:2], st_ref[:, 0],
                         *[_heads_major(r) for r in (q_ref, k_ref, v_ref, g_ref, b_ref)])
        ds, *dins = vjp((_heads_major(do_ref), ds_ref[...]))
        ds_ref[...] = ds
        for d_ref, d in zip((dq_ref, dk_ref, dv_ref, dg_ref, db_ref), dins):
            for h in range(HEADS):
                d_ref[:, h * hd:(h + 1) * hd] = d[h]

    tile = pl.BlockSpec((c, GDN_WIDTH), lambda i: (n - 1 - i, 0))
    return pl.pallas_call(
        body, name="gdn_core_bwd", grid=(n,),
        in_specs=[tile] * 5 + [pl.BlockSpec((HEADS, 1, hd, hd), lambda i: (0, n - 1 - i, 0, 0)),
                               pl.BlockSpec((1, HEADS, c, c), lambda i: (n - 1 - i, 0, 0, 0)), tile],
        out_specs=[tile] * 5, out_shape=[_sds((t, GDN_WIDTH), f32)] * 5,
        scratch_shapes=[pltpu.VMEM((HEADS, hd, hd), f32)],
        compiler_params=_cparams(("arbitrary",)))(q, k, v, ge, be, states, tinvs, do)


def _post_fn(o, z, gain):
    return _rms(o, gain) * _silu(z)


def _post_fwd(o, p, z_off, gain, name):
    t = o.shape[0]
    hd = HEAD_DIM

    def body(o_ref, z_ref, g_ref, y_ref):
        y_ref[...] = _post_fn(o_ref[...], z_ref[...], g_ref[...]).astype(bf16)

    col = pl.BlockSpec((t, hd), lambda h: (0, h))
    return pl.pallas_call(
        body, name=name, grid=(HEADS,),
        in_specs=[col, pl.BlockSpec((t, hd), lambda h: (0, h + z_off // hd)), pl.BlockSpec((1, hd), lambda h: (0, 0))],
        out_specs=col, out_shape=_sds((t, HEADS * hd), bf16), compiler_params=_cparams(("parallel",)))(o, p, gain)


def _post_bwd(o, p, z_off, gain, dmix, mix_off, name):
    t = o.shape[0]
    hd = HEAD_DIM

    def body(o_ref, z_ref, g_ref, dy_ref, do_ref, dz_ref, dg_ref):
        _, vjp = jax.vjp(_post_fn, o_ref[...], z_ref[...], g_ref[...])
        do, dz, dg = vjp(dy_ref[...])
        do_ref[...] = do
        dz_ref[...] = dz.astype(bf16)

        @pl.when(pl.program_id(0) == 0)
        def _():
            dg_ref[...] = jnp.zeros_like(dg_ref)

        dg_ref[...] += dg

    col = pl.BlockSpec((t, hd), lambda h: (0, h))
    vec = pl.BlockSpec((1, hd), lambda h: (0, 0))
    return pl.pallas_call(
        body, name=name, grid=(HEADS,),
        in_specs=[col, pl.BlockSpec((t, hd), lambda h: (0, h + z_off // hd)), vec,
                  pl.BlockSpec((t, hd), lambda h: (0, h + mix_off // hd))],
        out_specs=[col, col, vec], out_shape=[_sds((t, HEADS * hd), f32), _sds((t, HEADS * hd), bf16), _sds((1, hd), f32)],
        compiler_params=_cparams(("arbitrary",)))(o, p, gain, dmix)


def _hgrn_pre_fn(qb, fb, lbw, layer):
    l0, l1 = lbw[0:1, :], lbw[1:2, :]
    m = jnp.maximum(l0, l1)
    e0, e1 = jnp.exp(l0 - m), jnp.exp(l1 - m)
    p0, p1 = e0 / (e0 + e1), e1 / (e0 + e1)
    lb = (p0 - p0) if layer == 0 else ((p0 + p1) - p0)
    f = lb + (1.0 - lb) * _sigmoid(fb)
    return _silu(qb), 1.0 - f, jnp.log(jnp.maximum(f, F_FLOOR))


def _hgrn_pre_fwd(p, lbw, layer):
    t = p.shape[0]
    tc = HEAD_DIM

    def body(qb_ref, fb_ref, lb_ref, q_ref, k_ref, lf_ref):
        q_ref[...], k_ref[...], lf_ref[...] = _hgrn_pre_fn(qb_ref[...], fb_ref[...], lb_ref[...], layer)

    col = pl.BlockSpec((t, tc), lambda j: (0, j))
    return pl.pallas_call(
        body, name="hgrn_pre_fwd", grid=(GDN_WIDTH // tc,),
        in_specs=[pl.BlockSpec((t, tc), lambda j: (0, j + OFF_QB // tc)), pl.BlockSpec((t, tc), lambda j: (0, j + OFF_FB // tc)),
                  pl.BlockSpec((2, tc), lambda j: (0, j))],
        out_specs=[col] * 3, out_shape=[_sds((t, GDN_WIDTH), f32)] * 3,
        compiler_params=_cparams(("parallel",)))(p, p, lbw)


def _hgrn_pre_bwd(p, lbw, layer, dq, dk, dlf):
    t = p.shape[0]
    tc = HEAD_DIM

    def body(qb_ref, fb_ref, lb_ref, dq_ref, dk_ref, dlf_ref, dqb_ref, dfb_ref, dlb_ref):
        _, vjp = jax.vjp(functools.partial(_hgrn_pre_fn, layer=layer), qb_ref[...], fb_ref[...], lb_ref[...])
        dqb, dfb, dlb = vjp((dq_ref[...], dk_ref[...], dlf_ref[...]))
        dqb_ref[...] = dqb.astype(bf16)
        dfb_ref[...] = dfb.astype(bf16)
        dlb_ref[...] = dlb

    col = pl.BlockSpec((t, tc), lambda j: (0, j))
    lb = pl.BlockSpec((2, tc), lambda j: (0, j))
    return pl.pallas_call(
        body, name="hgrn_pre_bwd", grid=(GDN_WIDTH // tc,),
        in_specs=[pl.BlockSpec((t, tc), lambda j: (0, j + OFF_QB // tc)), pl.BlockSpec((t, tc), lambda j: (0, j + OFF_FB // tc)),
                  lb, col, col, col],
        out_specs=[col, col, lb], out_shape=[_sds((t, GDN_WIDTH), bf16)] * 2 + [_sds((2, GDN_WIDTH), f32)],
        compiler_params=_cparams(("parallel",)))(p, p, lbw, dq, dk, dlf)


def _hgrn_step(st, q, k, lf, v):
    c = HGRN_CHUNK
    nh = q.shape[0]
    r2 = lax.broadcasted_iota(jnp.int32, (c, c), 0)
    c2 = lax.broadcasted_iota(jnp.int32, (c, c), 1)
    tri = jnp.broadcast_to((r2 >= c2).astype(f32), (nh, c, c))
    i3 = lax.broadcasted_iota(jnp.int32, (c, c, HEAD_DIM), 0)
    j3 = lax.broadcasted_iota(jnp.int32, (c, c, HEAD_DIM), 1)
    mask = i3 >= j3
    outs = []
    for n in range(q.shape[1] // c):
        sl = slice(n * c, (n + 1) * c)
        qc, kc, lc, vc = q[:, sl], k[:, sl], lf[:, sl], v[:, sl]
        b = _hdot(tri, lc, precision=HI)
        rel = jnp.where(mask, jnp.exp(jnp.where(mask, b[:, :, None, :] - b[:, None, :, :], 0.0)), 0.0)
        scores = jnp.sum(qc[:, :, None, :] * kc[:, None, :, :] * rel, axis=-1)
        bl = b[:, c - 1:c, :]
        o = _hbdot(scores, vc) + _hbdot(qc * jnp.exp(b), st, BNT)
        st = st * jnp.exp(bl) + _hbdot(vc, kc * jnp.exp(bl - b), BTN)
        outs.append(o)
    return jnp.concatenate(outs, axis=1), st


def _hgrn_core_fwd(q, k, lf, p):
    t = q.shape[0]
    hd = HEAD_DIM
    rs = min(HGRN_STEP, t)
    n = t // rs

    def body(q_ref, k_ref, lf_ref, v_ref, o_ref, st_ref, s_ref):
        @pl.when(pl.program_id(0) == 0)
        def _():
            s_ref[...] = jnp.zeros_like(s_ref)

        s = s_ref[...]
        st_ref[:, 0] = s
        o, s_new = _hgrn_step(s, *[_heads_major(r) for r in (q_ref, k_ref, lf_ref, v_ref)])
        for h in range(HEADS):
            o_ref[:, h * hd:(h + 1) * hd] = o[h]
        s_ref[...] = s_new

    tile = pl.BlockSpec((rs, GDN_WIDTH), lambda i: (i, 0))
    return pl.pallas_call(
        body, name="hgrn_core_fwd", grid=(n,),
        in_specs=[tile, tile, tile, pl.BlockSpec((rs, GDN_WIDTH), lambda i: (i, OFF_IB // GDN_WIDTH))],
        out_specs=[tile, pl.BlockSpec((HEADS, 1, hd, hd), lambda i: (0, i, 0, 0))],
        out_shape=[_sds((t, GDN_WIDTH), f32), _sds((HEADS, n, hd, hd), f32)],
        scratch_shapes=[pltpu.VMEM((HEADS, hd, hd), f32)],
        compiler_params=_cparams(("arbitrary",)))(q, k, lf, p)


def _hgrn_core_bwd(q, k, lf, p, states, do):
    t = q.shape[0]
    hd = HEAD_DIM
    rs = min(HGRN_STEP, t)
    n = t // rs

    def body(q_ref, k_ref, lf_ref, v_ref, st_ref, do_ref, dq_ref, dk_ref, dlf_ref, dv_ref, ds_ref):
        @pl.when(pl.program_id(0) == 0)
        def _():
            ds_ref[...] = jnp.zeros_like(ds_ref)

        _, vjp = jax.vjp(_hgrn_step, st_ref[:, 0], *[_heads_major(r) for r in (q_ref, k_ref, lf_ref, v_ref)])
        ds, *dins = vjp((_heads_major(do_ref), ds_ref[...]))
        ds_ref[...] = ds
        for d_ref, d in zip((dq_ref, dk_ref, dlf_ref, dv_ref), dins):
            for h in range(HEADS):
                d_ref[:, h * hd:(h + 1) * hd] = d[h].astype(d_ref.dtype)

    tile = pl.BlockSpec((rs, GDN_WIDTH), lambda i: (n - 1 - i, 0))
    return pl.pallas_call(
        body, name="hgrn_core_bwd", grid=(n,),
        in_specs=[tile, tile, tile, pl.BlockSpec((rs, GDN_WIDTH), lambda i: (n - 1 - i, OFF_IB // GDN_WIDTH)),
                  pl.BlockSpec((HEADS, 1, hd, hd), lambda i: (0, n - 1 - i, 0, 0)), tile],
        out_specs=[tile] * 4, out_shape=[_sds((t, GDN_WIDTH), f32)] * 3 + [_sds((t, GDN_WIDTH), bf16)],
        scratch_shapes=[pltpu.VMEM((HEADS, hd, hd), f32)],
        compiler_params=_cparams(("arbitrary",)))(q, k, lf, p, states, do)


def _row(v):
    return v.reshape(1, -1)


def _anchored(w, row, key):
    tok = w.get(key)
    return row if tok is None else row + tok[0, 0]


def _pad_lanes(v, n=HEAD_DIM):
    return jnp.pad(v.reshape(1, -1), ((0, 0), (0, n - v.shape[-1])))


def _ffn_fwd(x, w, l):
    h = _rms_fwd(x, _row(w['norm_ffn'][l]), "ffn_norm")
    u = _mm(h, w['ffn_w_up'][l], name="ffn_up")
    a = _ffn_act_fwd(u, w['ffn_conv_w'][l], _row(w['ffn_conv_b'][l]))
    y = _mm(a, w['ffn_w_down'][l], add=x, name="ffn_down")
    return y, (x, h, u)


def _ffn_bwd(saved, w, l, dy, dyb, grads):
    x, h, u = saved
    da = _mm(dyb, w['ffn_w_down'][l], tb=True, name="ffn_down_dx")
    a, dg, dv, dcw, dcb = _ffn_act_bwd(u, w['ffn_conv_w'][l], _anchored(w, _row(w['ffn_conv_b'][l]), ('bwd', l)), da)
    grads['ffn_w_down'][l] = _mm(a, dyb, ta=True, out_dtype=bf16, name="ffn_down_dw")
    du = jnp.concatenate([dg, dv], axis=1)
    grads['ffn_w_up'][l] = _mm(h, du, ta=True, out_dtype=bf16, name="ffn_up_dw")
    dh = _mm(du, w['ffn_w_up'][l], tb=True, name="ffn_up_dx")
    dx, dxb, dgain = _rms_bwd(x, _row(w['norm_ffn'][l]), dh, dy, "ffn_norm_bwd")
    grads['ffn_conv_w'][l] = dcw
    grads['ffn_conv_b'][l] = dcb[0]
    grads['norm_ffn'][l] = dgain[0]
    return dx, dxb


def _odd_fwd(x, w, l, j):
    h = _rms_fwd(x, _anchored(w, _row(w['norm_mix'][l]), ('fwd', l)), "mix_norm")
    p = _mm(h, w['c_w_in'][j], name="lru_in")
    xc = _col_conv_fwd(p, LRU_WIDTH, w['c_conv_w'][j], _row(w['c_conv_b'][j]), LRU_CONV, 256, "lru_conv_fwd")
    out, hs, a = _lru_fwd(p, xc, w['c_gate_a_w'][j], _row(w['c_gate_a_b'][j]), w['c_gate_x_w'][j],
                          _row(w['c_gate_x_b'][j]), _row(w['c_lambda'][j]))
    y = _mm(out, w['c_w_out'][j], add=x, name="lru_out")
    return y, (x, h, p, xc, out, hs, a)


def _odd_bwd(saved, w, l, j, dy, dyb, grads):
    x, h, p, xc, out, hs, a = saved
    dout = _mm(dyb, w['c_w_out'][j], tb=True, name="lru_out_dx")
    grads['c_w_out'][j] = _mm(out, dyb, ta=True, out_dtype=bf16, name="lru_out_dw")
    dyb_, da, du = _lru_bwd_scan(p, a, hs, dout)
    dxc, dwa, dwx, dba, dbx, dlam = _lru_bwd_gates(xc, da, du, w['c_gate_a_w'][j], _row(w['c_gate_a_b'][j]),
                                                   w['c_gate_x_w'][j], _row(w['c_gate_x_b'][j]), _row(w['c_lambda'][j]))
    dxb_, dcw, dcb = _col_conv_bwd(p, LRU_WIDTH, w['c_conv_w'][j], dxc, LRU_CONV, 256, "lru_conv_bwd")
    dp = jnp.concatenate([dyb_, dxb_], axis=1)
    grads['c_w_in'][j] = _mm(h, dp, ta=True, out_dtype=bf16, name="lru_in_dw")
    dh = _mm(dp, w['c_w_in'][j], tb=True, name="lru_in_dx")
    dx, dxb, dgain = _rms_bwd(x, _row(w['norm_mix'][l]), dh, dy, "mix_norm_bwd")
    grads['c_gate_a_w'][j], grads['c_gate_x_w'][j] = dwa, dwx
    grads['c_gate_a_b'][j], grads['c_gate_x_b'][j], grads['c_lambda'][j] = dba[0], dbx[0], dlam[0]
    grads['c_conv_w'][j], grads['c_conv_b'][j] = dcw, dcb[0]
    grads['norm_mix'][l] = dgain[0]
    return dx, dxb


def _even_fwd(x, w, l, j):
    h = _rms_fwd(x, _anchored(w, _row(w['norm_mix'][l]), ('fwd', l)), "mix_norm")
    p = _mm(h, w['ab_w_in'][j], name="ab_in")
    alog, dtb = _pad_lanes(w['gdn_a_log'][j]), _pad_lanes(w['gdn_dt_bias'][j])
    q, k, v, be, ge = _gdn_pre_fwd(p, w['gdn_conv_w'][j], alog, dtb)
    oa, *sa = _gdn_core_fwd(q, k, v, ge, be)
    ya = _post_fwd(oa, p, OFF_Z, _row(w['gdn_norm'][j]), "gdn_post_fwd")
    qq, kk, lf = _hgrn_pre_fwd(p, w['hgrn_lower_bounds'], j)
    ob, sb = _hgrn_core_fwd(qq, kk, lf, p)
    yb = _post_fwd(ob, p, OFF_GB, _row(w['hgrn_norm'][j]), "hgrn_post_fwd")
    mix = jnp.concatenate([ya, yb], axis=1)
    y = _mm(mix, w['ab_w_out'][j], add=x, name="ab_out")
    return y, (x, h, p, q, k, v, be, ge, oa, sa, qq, kk, lf, ob, sb, mix)


def _even_bwd(saved, w, l, j, dy, dyb, grads):
    x, h, p, q, k, v, be, ge, oa, sa, qq, kk, lf, ob, sb, mix = saved
    alog, dtb = _pad_lanes(w['gdn_a_log'][j]), _pad_lanes(w['gdn_dt_bias'][j])
    dmix = _mm(dyb, w['ab_w_out'][j], tb=True, name="ab_out_dx")
    grads['ab_w_out'][j] = _mm(mix, dyb, ta=True, out_dtype=bf16, name="ab_out_dw")
    doa, dz, dgn = _post_bwd(oa, p, OFF_Z, _row(w['gdn_norm'][j]), dmix, 0, "gdn_post_bwd")
    dob, dgb, dhn = _post_bwd(ob, p, OFF_GB, _row(w['hgrn_norm'][j]), dmix, GDN_WIDTH, "hgrn_post_bwd")
    dq, dk, dv, dge, dbe = _gdn_core_bwd(q, k, v, ge, be, sa, doa)
    dpq, dpk, dpv, dba, dwq, dwk, dwv, dal, ddt = _gdn_pre_bwd(p, w['gdn_conv_w'][j], alog, dtb, dq, dk, dv, dbe, dge)
    dqq, dkk, dlf, dib = _hgrn_core_bwd(qq, kk, lf, p, sb, dob)
    dqb, dfb, dlb = _hgrn_pre_bwd(p, w['hgrn_lower_bounds'], j, dqq, dkk, dlf)
    dp = jnp.concatenate([dpq, dpk, dpv, dz, dqb, dfb, dib, dgb, dba.astype(bf16)], axis=1)
    grads['ab_w_in'][j] = _mm(h, dp, ta=True, out_dtype=bf16, name="ab_in_dw")
    dh = _mm(dp, w['ab_w_in'][j], tb=True, name="ab_in_dx")
    dx, dxb, dgain = _rms_bwd(x, _row(w['norm_mix'][l]), dh, dy, "mix_norm_bwd")
    grads['gdn_conv_w'][j] = jnp.concatenate([dwq, dwk, dwv], axis=1)
    grads['gdn_a_log'][j], grads['gdn_dt_bias'][j] = dal[0, :HEADS], ddt[0, :HEADS]
    grads['gdn_norm'][j], grads['hgrn_norm'][j] = dgn[0], dhn[0]
    grads['hgrn_lower_bounds'].append(dlb)
    grads['norm_mix'][l] = dgain[0]
    return dx, dxb


def _ab_permute(w_in):
    pad = jnp.zeros(w_in.shape[:-1] + (AB_PAD - AB_COLS,), w_in.dtype)
    return jnp.concatenate([w_in[..., :2048], w_in[..., 2056:], w_in[..., 2048:2056], pad], axis=-1)


def _ab_unpermute(g):
    return jnp.concatenate([g[..., :2048], g[..., 4096:4104], g[..., 2048:4096]], axis=-1)


def _block_pad(a, axis, nblk, padded):
    axis = axis % a.ndim
    s = a.shape
    a = a.reshape(s[:axis] + (nblk, s[axis] // nblk) + s[axis + 1:])
    pad = [(0, 0)] * a.ndim
    pad[axis + 1] = (0, padded - s[axis] // nblk)
    return jnp.pad(a, pad).reshape(s[:axis] + (nblk * padded,) + s[axis + 1:])


def _block_unpad(a, axis, nblk, width):
    axis = axis % a.ndim
    s = a.shape
    a = a.reshape(s[:axis] + (nblk, s[axis] // nblk) + s[axis + 1:])
    a = lax.slice_in_dim(a, 0, width, axis=axis + 1)
    return a.reshape(s[:axis] + (nblk * width,) + s[axis + 1:])


def _kernel_layout(w):
    w = dict(w)
    w['ab_w_in'] = _ab_permute(w['ab_w_in'])
    w['ffn_w_up'] = _block_pad(w['ffn_w_up'], 2, N_DEV, FF_PAD)
    w['ffn_w_down'] = _block_pad(w['ffn_w_down'], 1, 4, FF_PAD)
    w['ffn_conv_w'] = _block_pad(w['ffn_conv_w'], 2, 4, FF_PAD)
    w['ffn_conv_b'] = _block_pad(w['ffn_conv_b'], 1, 4, FF_PAD)
    return w


def _natural_grads(g):
    g = dict(g)
    g['ab_w_in'] = _ab_unpermute(g['ab_w_in'])
    g['ffn_w_up'] = _block_unpad(g['ffn_w_up'], 2, N_DEV, FF_SHARD)
    g['ffn_w_down'] = _block_unpad(g['ffn_w_down'], 1, 4, FF_SHARD)
    g['ffn_conv_w'] = _block_unpad(g['ffn_conv_w'], 2, 4, FF_SHARD)
    g['ffn_conv_b'] = _block_unpad(g['ffn_conv_b'], 1, 4, FF_SHARD)
    return g


def _local_step(x, target, w, fetch=None, push=None):
    grads = {n: [None] * (DEPTH if n in ('norm_mix', 'norm_ffn') or n.startswith('ffn_') else 2)
             for n in WEIGHTS if n not in ('norm_final', 'hgrn_lower_bounds')}
    grads['hgrn_lower_bounds'] = []
    saved = []
    for l in range(DEPTH):
        j = l // 2
        if fetch is not None:
            fetch(l, x)
        x, s_mix = (_even_fwd if l % 2 == 0 else _odd_fwd)(x, w, l, j)
        x, s_ffn = _ffn_fwd(x, w, l)
        saved.append((s_mix, s_ffn))
    loss, dx, dxb, dgf = _loss_head(x, _row(w['norm_final']), target)
    for l in reversed(range(DEPTH)):
        j = l // 2
        s_mix, s_ffn = saved[l]
        dx, dxb = _ffn_bwd(s_ffn, w, l, dx, dxb, grads)
        dx, dxb = (_even_bwd if l % 2 == 0 else _odd_bwd)(s_mix, w, l, j, dx, dxb, grads)
        if push is not None:
            push(l, {nm: grads[nm].pop(li) for nm, li in reversed(_layer_items(l))})
    out = {n: jnp.stack(g) for n, g in grads.items() if n != 'hgrn_lower_bounds' and g}
    out['hgrn_lower_bounds'] = grads['hgrn_lower_bounds'][0] + grads['hgrn_lower_bounds'][1]
    out['norm_final'] = dgf[0]
    return loss[0, 0], dx, out


def _position():
    return lax.axis_index("x"), lax.axis_index("y"), lax.axis_index("c")


BLOCK_LAYOUT = {
    'ab_w_in': ((2, D_MODEL, N_DEV * AB_SHARD_PAD), (2, D_MODEL, AB_SHARD_PAD)),
    'ab_w_out': ((2, N_DEV, 128, D_MODEL), (2, 128, D_MODEL)),
    'c_w_in': ((2, D_MODEL, 2 * LRU_WIDTH), (2, D_MODEL, 256)),
    'c_w_out': ((2, N_DEV, 128, D_MODEL), (2, 128, D_MODEL)),
    'c_gate_a_w': ((2, HEADS, N_DEV, 32, LRU_BLOCK), (2, HEADS, 32, LRU_BLOCK)),
    'c_gate_x_w': ((2, HEADS, N_DEV, 32, LRU_BLOCK), (2, HEADS, 32, LRU_BLOCK)),
    'ffn_w_up': ((DEPTH, D_MODEL, N_DEV * FF_PAD), (DEPTH, D_MODEL, FF_PAD)),
    'ffn_w_down': ((DEPTH, 4, FF_PAD, D_MODEL), (DEPTH, FF_ROWS, D_MODEL)),
}


COL_WINDOW = {'ab_w_in': AB_SHARD_PAD, 'c_w_in': 256, 'ffn_w_up': FF_PAD}


def _block_index(name, p):
    d = 4 * p[0] + 2 * p[1] + p[2]
    if name in COL_WINDOW:
        return (slice(None), pl.ds(pl.multiple_of(d * COL_WINDOW[name], 128), COL_WINDOW[name]))
    if name == 'ffn_w_down':
        return (2 * p[0] + p[1], pl.ds(pl.multiple_of(p[2] * FF_ROWS, 16), FF_ROWS), slice(None))
    if name in ('c_gate_a_w', 'c_gate_x_w'):
        return (slice(None), d)
    return (d,)


def _block_of(name, ref, p, layered=True):
    idx = _block_index(name, p)
    if layered and name in BLOCK_LAYOUT:
        idx = (slice(None),) + idx
    return ref.at[idx]


def _layer_items(l):
    j = l // 2
    mix = ([('ab_w_in', j), ('ab_w_out', j)] if l % 2 == 0 else
           [('c_w_in', j), ('c_w_out', j), ('c_gate_a_w', j), ('c_gate_x_w', j)])
    return mix + [('ffn_w_up', l), ('ffn_w_down', l)]


def _own_land(name, shard_l, pos):
    x, y, c = pos
    d = 4 * x + 2 * y + c
    shape = BLOCK_LAYOUT[name][0][1:] if name in BLOCK_LAYOUT else (N_DEV,) + shard_l.shape
    zeros = jnp.zeros(shape, shard_l.dtype) if name == 'ffn_w_down' else lax.empty(shape, shard_l.dtype)
    if name in COL_WINDOW:
        return lax.dynamic_update_slice(zeros, shard_l, (0, d * COL_WINDOW[name]))
    if name == 'ffn_w_down':
        return lax.dynamic_update_slice(zeros, shard_l[None], (2 * x + y, c * FF_ROWS, 0))
    if name in ('c_gate_a_w', 'c_gate_x_w'):
        return lax.dynamic_update_slice(zeros, shard_l[:, None], (0, d, 0, 0))
    return lax.dynamic_update_slice(zeros, shard_l[None], (d,) + (0,) * shard_l.ndim)


def _place_own(items, shards, posv, name):
    n = len(items)
    down = [i for i, (nm, _) in enumerate(items) if nm == 'ffn_w_down']
    in_specs, out_specs, out_shapes, operands = [], [], [], []
    for nm, li in items:
        sh = shards[nm]
        shard_shape = sh.shape if li is None else sh.shape[1:]
        z = (0,) * len(shard_shape)
        operands.append(sh)
        in_specs.append(pl.BlockSpec(shard_shape, lambda i, d, q, c, z=z: z) if li is None else
                        pl.BlockSpec((1,) + shard_shape, lambda i, d, q, c, li=li, z=z: (li,) + z))
        out_shapes.append(_sds(BLOCK_LAYOUT[nm][0][1:] if nm in BLOCK_LAYOUT else (N_DEV,) + sh.shape, sh.dtype))
        if nm in COL_WINDOW:
            out_specs.append(pl.BlockSpec(shard_shape, lambda i, d, q, c: (0, d[0])))
        elif nm == 'ffn_w_down':
            out_specs.append(pl.BlockSpec((1,) + shard_shape, lambda i, d, q, c: (q[0], c[0], 0)))
        elif nm in ('c_gate_a_w', 'c_gate_x_w'):
            out_specs.append(pl.BlockSpec((HEADS, 1) + shard_shape[1:], lambda i, d, q, c: (0, d[0], 0, 0)))
        else:
            out_specs.append(pl.BlockSpec((1,) + shard_shape, lambda i, d, q, c, z=z: (d[0],) + z))

    def body(d_ref, q_ref, c_ref, *refs):
        for i, (nm, li) in enumerate(items):
            v = refs[i][...] if li is None else refs[i][0]
            o_ref = refs[n + len(down) + i]
            if nm in COL_WINDOW:
                o_ref[...] = v
            elif nm in ('c_gate_a_w', 'c_gate_x_w'):
                o_ref[:, 0] = v
            else:
                o_ref[0] = v

    zeros = [jnp.zeros(out_shapes[i].shape, out_shapes[i].dtype) for i in down]
    return pl.pallas_call(
        body, name=name, out_shape=out_shapes,
        grid_spec=pltpu.PrefetchScalarGridSpec(
            num_scalar_prefetch=3, grid=(1,), in_specs=in_specs + [pl.BlockSpec(memory_space=pl.ANY)] * len(down),
            out_specs=out_specs),
        input_output_aliases={3 + n + k: i for k, i in enumerate(down)},
        compiler_params=_cparams(("arbitrary",)))(*posv, *operands, *zeros)


def _src_of(shard_ref, li):
    return shard_ref if li is None else shard_ref.at[li]


def _gather_now(items, shards, lands):
    n = len(items)
    srcs = sorted({nm for nm, _ in items})

    def body(*refs):
        ins = dict(zip(srcs, refs[:len(srcs)]))
        outs = refs[len(srcs) + n:len(srcs) + 2 * n]
        send_sems, recv_sems = refs[len(srcs) + 2 * n:]
        x, y, c = _position()
        me, sibling = (x, y, c), (x, y, 1 - c)
        chips = [(1 - x, y), (x, 1 - y), (1 - x, 1 - y)]

        def copy(i, k, block, to, own=False):
            nm, li = items[i]
            dst = _block_of(nm, outs[i], block, layered=False)
            return pltpu.make_async_remote_copy(
                src_ref=_src_of(ins[nm], li) if own else dst, dst_ref=dst, send_sem=send_sems.at[7 * i + k],
                recv_sem=recv_sems.at[7 * i + k], device_id=to, device_id_type=MESH)

        first = []
        for i in range(n):
            first.append(copy(i, 0, me, sibling, own=True))
            first += [copy(i, 1 + j, me, (*chip, c), own=True) for j, chip in enumerate(chips)]
        for cp in first:
            cp.start()
        passed = []
        for j, chip in enumerate(chips):
            for i in range(n):
                copy(i, 1 + j, (*chip, c), me).wait_recv()
                fwd = copy(i, 4 + j, (*chip, c), sibling)
                fwd.start()
                passed.append(fwd)
        for i in range(n):
            copy(i, 0, sibling, me).wait_recv()
        for j, chip in enumerate(chips):
            for i in range(n):
                copy(i, 4 + j, (*chip, 1 - c), me).wait_recv()
        for cp in first + passed:
            cp.wait_send()

    any_spec = pl.BlockSpec(memory_space=pl.ANY)
    return pl.pallas_call(
        body, name="gather_first_layer", out_shape=[_sds(a.shape, a.dtype) for a in lands],
        in_specs=[any_spec] * (len(srcs) + n), out_specs=[any_spec] * n,
        input_output_aliases={len(srcs) + i: i for i in range(n)},
        scratch_shapes=[pltpu.SemaphoreType.DMA((7 * n,)), pltpu.SemaphoreType.DMA((7 * n,))],
    )(*[shards[nm] for nm in srcs], *lands)


FIRST_HOP = (1, 2, 4, 6)


def _lanes(name, land_ref, pos):
    if name == 'ffn_w_down':
        return [(FIRST_HOP, land_ref.at[pl.ds(0, 2), pl.ds(0, 2 * FF_ROWS)])]
    if name in COL_WINDOW:
        return [(FIRST_HOP, land_ref.at[:, pl.ds(0, 4 * COL_WINDOW[name])])]
    if name in ('c_gate_a_w', 'c_gate_x_w'):
        return [(FIRST_HOP, land_ref.at[:, pl.ds(0, 4)])]
    return [(FIRST_HOP, land_ref.at[pl.ds(0, 4)])]


def _n_lanes(items):
    return len(items)


def _gather_forward(items, lands, name):
    n = len(items)

    def body(*refs):
        outs = refs[n:2 * n]
        send_sems, recv_sems = refs[2 * n:]
        x, y, c = _position()
        chips = [(1 - x, y), (x, 1 - y), (1 - x, 1 - y)]
        copies, arrivals = [], []
        for i, (nm, _) in enumerate(items):
            for j, chip in enumerate(chips):
                mine = _block_of(nm, outs[i], (*chip, c), layered=False)
                theirs = _block_of(nm, outs[i], (*chip, 1 - c), layered=False)
                copies.append(pltpu.make_async_remote_copy(
                    src_ref=mine, dst_ref=mine, send_sem=send_sems.at[3 * i + j], recv_sem=recv_sems.at[3 * i + j],
                    device_id=(x, y, 1 - c), device_id_type=MESH))
                arrivals.append(pltpu.make_async_remote_copy(
                    src_ref=theirs, dst_ref=theirs, send_sem=send_sems.at[3 * i + j], recv_sem=recv_sems.at[3 * i + j],
                    device_id=(x, y, 1 - c), device_id_type=MESH))
        for cp in copies:
            cp.start()
        for cp in arrivals:
            cp.wait_recv()
        for cp in copies:
            cp.wait_send()

    any_spec = pl.BlockSpec(memory_space=pl.ANY)
    return pl.pallas_call(
        body, name=name, out_shape=[_sds(a.shape, a.dtype) for a in lands],
        in_specs=[any_spec] * n, out_specs=[any_spec] * n, input_output_aliases={i: i for i in range(n)},
        scratch_shapes=[pltpu.SemaphoreType.DMA((3 * n,)), pltpu.SemaphoreType.DMA((3 * n,))],
    )(*lands)


HBM_SPEC = pl.BlockSpec(memory_space=pltpu.HBM)
SEM_SPEC = pl.BlockSpec(memory_space=pltpu.SEMAPHORE)
SIDE_EFFECT = pltpu.SideEffectType.DATAFLOW_SIDE_EFFECTING


def _gather_start(items, shards, lands, token, name):
    n = len(items)
    srcs = sorted({nm for nm, _ in items})
    ns, nl = len(srcs), _n_lanes(items)

    def body(*refs):
        ins = dict(zip(srcs, refs[:ns]))
        land_refs = refs[ns:ns + n]
        sems = refs[ns + n + 1:ns + n + 1 + 2 * nl]
        x, y, c = _position()
        me = (x, y, c)
        lane = 0
        for i, (nm, li) in enumerate(items):
            for codes, _ in _lanes(nm, land_refs[i], me):
                for k in codes:
                    peer = (1 - x if (k >> 2) & 1 else x, 1 - y if (k >> 1) & 1 else y, 1 - c if k & 1 else c)
                    pltpu.make_async_remote_copy(
                        src_ref=_src_of(ins[nm], li), dst_ref=_block_of(nm, land_refs[i], me, layered=False),
                        send_sem=sems[2 * lane], recv_sem=sems[2 * lane + 1], device_id=peer, device_id_type=MESH).start()
                lane += 1
        refs[-1][...] = jnp.zeros((8, 128), f32)

    hbm = [pltpu.with_memory_space_constraint(a, pltpu.HBM) for a in [shards[nm] for nm in srcs] + list(lands)]
    outs = pl.pallas_call(
        body, name=name,
        out_shape=[pltpu.SemaphoreType.DMA(())] * (2 * nl) + [pltpu.HBM(a.shape, a.dtype) for a in hbm] + [_sds((8, 128), f32)],
        in_specs=[HBM_SPEC] * (ns + n) + [pl.BlockSpec(memory_space=pl.ANY)],
        out_specs=[SEM_SPEC] * (2 * nl) + [HBM_SPEC] * (ns + n) + [pl.BlockSpec(memory_space=pltpu.VMEM)],
        input_output_aliases={i: 2 * nl + i for i in range(ns + n)},
        compiler_params=pltpu.CompilerParams(has_side_effects=SIDE_EFFECT),
    )(*hbm, token)
    return outs[:2 * nl], dict(zip(srcs, outs[2 * nl:2 * nl + ns])), outs[2 * nl + ns:-1], outs[-1]


def _gather_wait(items, sems, shards, lands, after, name):
    n = len(items)
    srcs = sorted(shards)
    ns, nl = len(srcs), _n_lanes(items)

    def body(*refs):
        land_refs = refs[ns:ns + n]
        sem_refs = refs[ns + n:ns + n + 2 * nl]
        x, y, c = _position()
        lane = 0
        for i, (nm, _) in enumerate(items):
            for _, moved in _lanes(nm, land_refs[i], (x, y, c)):
                cp = pltpu.make_async_remote_copy(
                    src_ref=moved, dst_ref=moved, send_sem=sem_refs[2 * lane], recv_sem=sem_refs[2 * lane + 1],
                    device_id=(x, y, 1 - c), device_id_type=MESH)
                cp.wait_send()
                cp.wait_recv()
                lane += 1

    outs = pl.pallas_call(
        body, name=name, out_shape=[pltpu.HBM(shards[nm].shape, shards[nm].dtype) for nm in srcs]
        + [pltpu.HBM(a.shape, a.dtype) for a in lands],
        in_specs=[HBM_SPEC] * (ns + n) + [SEM_SPEC] * (2 * nl) + [pl.BlockSpec(memory_space=pl.ANY)],
        out_specs=[HBM_SPEC] * (ns + n), input_output_aliases={i: i for i in range(ns + n)},
        compiler_params=pltpu.CompilerParams(has_side_effects=SIDE_EFFECT),
    )(*[shards[nm] for nm in srcs], *lands, *sems, after)
    return dict(zip(srcs, outs[:ns])), outs[ns:]


def _exchange_grads(fulls, rep):
    cpos = lax.axis_index("c").astype(jnp.int32).reshape(1)
    pair, rep_pair = _pair_exchange(fulls, rep)
    chip = {nm: _chip_sum(nm, fulls[nm], pair[nm], cpos) for nm in fulls}
    rep_chip = _add_pair(rep, rep_pair, "chip_sum_replicated")
    cross, cross_rep = _cross_exchange(chip, rep_chip)
    return chip, rep_chip, cross, cross_rep


def _pair_exchange(fulls, rep, tag=""):
    names = list(fulls)
    n = len(names)
    shard_shape = {nm: ((fulls[nm].shape[0],) + BLOCK_LAYOUT[nm][1][1:] if nm in BLOCK_LAYOUT else fulls[nm].shape[1:])
                   for nm in names}

    def body(*refs):
        ins = dict(zip(names, refs[:n]))
        rep_ref = refs[n]
        pair = dict(zip(names, refs[n + 1:2 * n + 1]))
        rpair_ref = refs[2 * n + 1]
        send_sems, recv_sems = refs[2 * n + 2:]
        x, y, c = _position()
        sibling = (x, y, 1 - c)
        remote = []
        for i, nm in enumerate(names):
            for q in range(4):
                remote.append(pltpu.make_async_remote_copy(
                    src_ref=_block_of(nm, ins[nm], (q >> 1, q & 1, 1 - c)), dst_ref=pair[nm].at[q],
                    send_sem=send_sems.at[4 * i + q], recv_sem=recv_sems.at[4 * i + q], device_id=sibling,
                    device_id_type=MESH))
        remote.append(pltpu.make_async_remote_copy(
            src_ref=rep_ref, dst_ref=rpair_ref, send_sem=send_sems.at[4 * n], recv_sem=recv_sems.at[4 * n],
            device_id=sibling, device_id_type=MESH))
        for cp in remote:
            cp.start()
        for cp in remote:
            cp.wait_recv()
        for cp in remote:
            cp.wait_send()

    any_spec = pl.BlockSpec(memory_space=pl.ANY)
    four = [_sds((4,) + tuple(shard_shape[nm]), fulls[nm].dtype) for nm in names]
    outs = pl.pallas_call(
        body, name="grad_pair_exchange" + tag, out_shape=four + [_sds(rep.shape, rep.dtype)],
        in_specs=[any_spec] * (n + 1), out_specs=[any_spec] * (n + 1),
        scratch_shapes=[pltpu.SemaphoreType.DMA((4 * n + 1,)), pltpu.SemaphoreType.DMA((4 * n + 1,))],
    )(*[fulls[nm] for nm in names], rep)
    return dict(zip(names, outs[:n])), outs[n]


def _chip_sum(name, full, pair, cpos, tag=""):
    if name in ('ab_w_in', 'c_w_in', 'ffn_w_up'):
        width = BLOCK_LAYOUT[name][1][-1]
        rows = full.shape[0] * full.shape[1]
        tr = 512

        def body(c_ref, f_ref, p_ref, o_ref):
            o_ref[0] = (f_ref[...].astype(f32) + p_ref[0].astype(f32)).astype(o_ref.dtype)

        slot = pl.BlockSpec((1, tr, width), lambda q, i, c: (q, i, 0))
        out = pl.pallas_call(
            body, name="chip_sum_" + name + tag, out_shape=_sds((4, rows, width), full.dtype),
            grid_spec=pltpu.PrefetchScalarGridSpec(
                num_scalar_prefetch=1, grid=(4, rows // tr),
                in_specs=[pl.BlockSpec((tr, width), lambda q, i, c: (i, 2 * q + c[0])), slot], out_specs=slot),
            compiler_params=_cparams(("parallel", "parallel")))(
            cpos, full.reshape(rows, N_DEV * width), pair.reshape(4, rows, width))
        return out.reshape(pair.shape)

    if name == 'ffn_w_down':
        f4, p4 = full, pair
        fspec = pl.BlockSpec((full.shape[0], 1, FF_ROWS, D_MODEL), lambda q, c: (0, q, c[0], 0))
    else:
        shard = pair.shape[1:]
        lead = int(np.prod(shard[:-2]))
        f4 = full.reshape((lead, N_DEV) + shard[-2:])
        p4 = pair.reshape((4, lead) + shard[-2:])
        fspec = pl.BlockSpec((lead, 1) + shard[-2:], lambda q, c: (0, 2 * q + c[0], 0, 0))

    def body4(c_ref, f_ref, p_ref, o_ref):
        o_ref[0] = (f_ref[:, 0].astype(f32) + p_ref[0].astype(f32)).astype(o_ref.dtype)

    slot = pl.BlockSpec((1,) + p4.shape[1:], lambda q, c: (q, 0, 0, 0))
    out = pl.pallas_call(
        body4, name="chip_sum_" + name + tag, out_shape=_sds(p4.shape, full.dtype),
        grid_spec=pltpu.PrefetchScalarGridSpec(num_scalar_prefetch=1, grid=(4,), in_specs=[fspec, slot], out_specs=slot),
        compiler_params=_cparams(("parallel",)))(cpos, f4, p4)
    return out.reshape(pair.shape)


def _add_pair(a, b, name):
    shp = a.shape
    r, c = int(np.prod(shp[:-1])), shp[-1]
    tr = _tile(r, (512, 256, 128, 64, 32, 16, 8))

    def body(a_ref, b_ref, o_ref):
        o_ref[...] = (a_ref[...].astype(f32) + b_ref[...].astype(f32)).astype(o_ref.dtype)

    tile = pl.BlockSpec((tr, c), lambda i: (i, 0))
    return pl.pallas_call(body, name=name, grid=(r // tr,), in_specs=[tile, tile], out_specs=tile,
                          out_shape=_sds((r, c), a.dtype), compiler_params=_cparams(("parallel",)))(
        a.reshape(r, c), b.reshape(r, c)).reshape(shp)


def _cross_exchange(chip, rep_chip):
    names = list(chip)
    n = len(names)

    def body(*refs):
        ins = dict(zip(names, refs[:n]))
        rep_ref = refs[n]
        outs = dict(zip(names, refs[2 * n + 2:3 * n + 2]))
        rrep_ref = refs[3 * n + 2]
        send_sems, recv_sems = refs[3 * n + 3:]
        x, y, c = _position()
        mine = 2 * x + y
        copies = []
        for k in range(1, 4):
            px, py = (1 - x if (k >> 1) & 1 else x), (1 - y if k & 1 else y)
            for i, nm in enumerate(names + ['']):
                src = rep_ref if i == n else ins[nm].at[2 * px + py]
                dst = (rrep_ref if i == n else outs[nm]).at[mine]
                copies.append(pltpu.make_async_remote_copy(
                    src_ref=src, dst_ref=dst, send_sem=send_sems.at[3 * i + k - 1], recv_sem=recv_sems.at[3 * i + k - 1],
                    device_id=(px, py, c), device_id_type=MESH))
        for cp in copies:
            cp.start()
        for cp in copies:
            cp.wait_recv()
        for cp in copies:
            cp.wait_send()

    any_spec = pl.BlockSpec(memory_space=pl.ANY)
    shapes = [_sds(chip[nm].shape, chip[nm].dtype) for nm in names] + [_sds((4,) + rep_chip.shape, rep_chip.dtype)]
    zeros = [jnp.zeros(s.shape, s.dtype) for s in shapes]
    outs = pl.pallas_call(
        body, name="grad_cross_exchange", out_shape=shapes,
        in_specs=[any_spec] * (2 * n + 2), out_specs=[any_spec] * (n + 1),
        input_output_aliases={n + 1 + i: i for i in range(n + 1)},
        scratch_shapes=[pltpu.SemaphoreType.DMA((3 * (n + 1),)), pltpu.SemaphoreType.DMA((3 * (n + 1),))],
    )(*[chip[nm] for nm in names], rep_chip, *zeros)
    return dict(zip(names, outs[:n])), outs[n]


def _cross_start(chips, name):
    n = len(chips)

    def body(*refs):
        chip_refs, land_refs = refs[:n], refs[n:2 * n]
        sems = refs[2 * n:4 * n]
        x, y, c = _position()
        mine = 2 * x + y
        for i in range(n):
            for k in range(1, 4):
                px, py = (1 - x if (k >> 1) & 1 else x), (1 - y if k & 1 else y)
                pltpu.make_async_remote_copy(
                    src_ref=chip_refs[i].at[2 * px + py], dst_ref=land_refs[i].at[mine], send_sem=sems[2 * i],
                    recv_sem=sems[2 * i + 1], device_id=(px, py, c), device_id_type=MESH).start()
        refs[-1][...] = jnp.zeros((8, 128), f32)

    hbm = [pltpu.with_memory_space_constraint(a, pltpu.HBM) for a in list(chips) + [jnp.zeros(a.shape, a.dtype) for a in chips]]
    outs = pl.pallas_call(
        body, name=name,
        out_shape=[pltpu.SemaphoreType.DMA(())] * (2 * n) + [pltpu.HBM(a.shape, a.dtype) for a in hbm] + [_sds((8, 128), f32)],
        in_specs=[HBM_SPEC] * (2 * n),
        out_specs=[SEM_SPEC] * (2 * n) + [HBM_SPEC] * (2 * n) + [pl.BlockSpec(memory_space=pltpu.VMEM)],
        input_output_aliases={i: 2 * n + i for i in range(2 * n)},
        compiler_params=pltpu.CompilerParams(has_side_effects=SIDE_EFFECT),
    )(*hbm)
    return outs[:2 * n], outs[2 * n:3 * n], outs[3 * n:4 * n], outs[4 * n]


def _cross_wait(sems, chips, lands, after, name):
    n = len(chips)

    def body(*refs):
        land_refs = refs[n:2 * n]
        sem_refs = refs[2 * n:4 * n]
        x, y, c = _position()
        for i in range(n):
            moved = land_refs[i].at[pl.ds(0, 3)]
            cp = pltpu.make_async_remote_copy(
                src_ref=moved, dst_ref=moved, send_sem=sem_refs[2 * i], recv_sem=sem_refs[2 * i + 1],
                device_id=(x, y, 1 - c), device_id_type=MESH)
            cp.wait_send()
            cp.wait_recv()

    outs = pl.pallas_call(
        body, name=name, out_shape=[pltpu.HBM(a.shape, a.dtype) for a in list(chips) + list(lands)],
        in_specs=[HBM_SPEC] * (2 * n) + [SEM_SPEC] * (2 * n) + [pl.BlockSpec(memory_space=pl.ANY)],
        out_specs=[HBM_SPEC] * (2 * n), input_output_aliases={i: i for i in range(2 * n)},
        compiler_params=pltpu.CompilerParams(has_side_effects=SIDE_EFFECT),
    )(*chips, *lands, *sems, after)
    return outs[:n], outs[n:]


def _sum_adamw_layer(parts, own, mine, w, m, v, li, prev, name):
    nl, r, l = w.shape
    lp = parts.shape[2]
    tr = r if r <= 512 else _tile(r, (512, 256, 128))
    c1 = 1.0 / (1.0 - ADAM_B1 ** ADAM_STEP)
    c2 = 1.0 / (1.0 - ADAM_B2 ** ADAM_STEP)
    k = 0 if prev is None else 4

    def body(mine_ref, p_ref, o_ref, w_ref, m_ref, v_ref, *rest):
        g_ref, d_ref, nm_ref, nv_ref = rest[k:]
        mine_v = o_ref[0].astype(f32)
        g = jnp.where(mine_ref[0] == 0, mine_v, p_ref[0].astype(f32))
        for s in range(1, 4):
            g = g + jnp.where(mine_ref[0] == s, mine_v, p_ref[s].astype(f32))
        if lp != l:
            g = g[:, :l]
        m_new = ADAM_B1 * m_ref[0] + (1.0 - ADAM_B1) * g
        v_new = ADAM_B2 * v_ref[0] + (1.0 - ADAM_B2) * (g * g)
        g_ref[0] = g
        nm_ref[0] = m_new
        nv_ref[0] = v_new
        d_ref[0] = -ADAM_LR * ((m_new * c1) / (jnp.sqrt(v_new * c2) + ADAM_EPS) + ADAM_WD * w_ref[0])

    tile = pl.BlockSpec((1, tr, l), lambda i, mn: (li, i, 0))
    keep = [pl.BlockSpec(memory_space=pl.ANY)] * k
    return pl.pallas_call(
        body, name=name, out_shape=[_sds((nl, r, l), f32)] * 4,
        grid_spec=pltpu.PrefetchScalarGridSpec(
            num_scalar_prefetch=1, grid=(r // tr,),
            in_specs=[pl.BlockSpec((4, tr, lp), lambda i, mn: (0, i, 0)), pl.BlockSpec((1, tr, lp), lambda i, mn: (mn[0], i, 0)),
                      tile, tile, tile] + keep,
            out_specs=[tile] * 4),
        input_output_aliases={6 + i: i for i in range(k)},
        compiler_params=_cparams(("parallel",)))(mine, parts, own, w, m, v, *(prev or ()))


def _sum_adamw(parts, own, mine, w, m, v, name):
    r, l = w.shape
    lp = parts.shape[2]
    tr = _tile(r, (256, 128, 64, 32, 16, 8))
    c1 = 1.0 / (1.0 - ADAM_B1 ** ADAM_STEP)
    c2 = 1.0 / (1.0 - ADAM_B2 ** ADAM_STEP)

    def body(mine_ref, p_ref, o_ref, w_ref, m_ref, v_ref, g_ref, d_ref, nm_ref, nv_ref):
        mine_v = (o_ref[0] if own.ndim == 3 else o_ref[...]).astype(f32)
        g = jnp.where(mine_ref[0] == 0, mine_v, p_ref[0].astype(f32))
        for s in range(1, parts.shape[0]):
            g = g + jnp.where(mine_ref[0] == s, mine_v, p_ref[s].astype(f32))
        if lp != l:
            g = g[:, :l]
        m_new = ADAM_B1 * m_ref[...] + (1.0 - ADAM_B1) * g
        v_new = ADAM_B2 * v_ref[...] + (1.0 - ADAM_B2) * (g * g)
        g_ref[...] = g
        nm_ref[...] = m_new
        nv_ref[...] = v_new
        d_ref[...] = -ADAM_LR * ((m_new * c1) / (jnp.sqrt(v_new * c2) + ADAM_EPS) + ADAM_WD * w_ref[...])

    tile = pl.BlockSpec((tr, l), lambda i, mn: (i, 0))
    own_spec = (pl.BlockSpec((1, tr, lp), lambda i, mn: (mn[0], i, 0)) if own.ndim == 3
                else pl.BlockSpec((tr, lp), lambda i, mn: (i, 0)))
    return pl.pallas_call(
        body, name=name, out_shape=[_sds((r, l), f32)] * 4,
        grid_spec=pltpu.PrefetchScalarGridSpec(
            num_scalar_prefetch=1, grid=(r // tr,),
            in_specs=[pl.BlockSpec((parts.shape[0], tr, lp), lambda i, mn: (0, i, 0)), own_spec, tile, tile, tile],
            out_specs=[tile] * 4),
        compiler_params=_cparams(("parallel",)))(mine, parts, own, w, m, v)


def _pack(arrs, lead=None):
    if lead is None:
        flat = jnp.concatenate([a.reshape(-1).astype(f32) for a in arrs])
        n = flat.shape[0]
    else:
        flat = jnp.concatenate([a.reshape(lead, -1).astype(f32) for a in arrs], axis=1)
        n = flat.shape[1]
    tot = -(-n // 1024) * 1024
    if lead is None:
        return jnp.pad(flat, (0, tot - n)).reshape(tot // 128, 128)
    return jnp.pad(flat, ((0, 0), (0, tot - n))).reshape(lead, tot // 128, 128)


def _unpack(packed, shapes, lead=False):
    flat = packed.reshape(packed.shape[0], -1) if lead else packed.reshape(-1)
    out, off = [], 0
    for s in shapes:
        n = int(np.prod(s))
        out.append(flat[:, off:off + n].reshape((packed.shape[0],) + tuple(s)) if lead else flat[off:off + n].reshape(s))
        off += n
    return out


def _merge_shards(g, axis):
    g = jnp.moveaxis(g, 0, axis)
    s = g.shape
    return g.reshape(s[:axis] + (s[axis] * s[axis + 1],) + s[axis + 2:])


def _split_shards(full, axis):
    s = full.shape
    g = full.reshape(s[:axis] + (N_DEV, s[axis] // N_DEV) + s[axis + 1:])
    return jnp.moveaxis(g, axis, 0)


def kernel(x, norm_mix, norm_ffn, norm_final, ab_w_in, gdn_conv_w, gdn_a_log, gdn_dt_bias, gdn_norm, hgrn_lower_bounds, hgrn_norm, ab_w_out, c_w_in, c_conv_w, c_conv_b, c_gate_a_w, c_gate_a_b, c_gate_x_w, c_gate_x_b, c_lambda, c_w_out, ffn_w_up, ffn_conv_w, ffn_conv_b, ffn_w_down, loss_target, m_norm_mix, m_norm_ffn, m_norm_final, m_ab_w_in, m_gdn_conv_w, m_gdn_a_log, m_gdn_dt_bias, m_gdn_norm, m_hgrn_lower_bounds, m_hgrn_norm, m_ab_w_out, m_c_w_in, m_c_conv_w, m_c_conv_b, m_c_gate_a_w, m_c_gate_a_b, m_c_gate_x_w, m_c_gate_x_b, m_c_lambda, m_c_w_out, m_ffn_w_up, m_ffn_conv_w, m_ffn_conv_b, m_ffn_w_down, v_norm_mix, v_norm_ffn, v_norm_final, v_ab_w_in, v_gdn_conv_w, v_gdn_a_log, v_gdn_dt_bias, v_gdn_norm, v_hgrn_lower_bounds, v_hgrn_norm, v_ab_w_out, v_c_w_in, v_c_conv_w, v_c_conv_b, v_c_gate_a_w, v_c_gate_a_b, v_c_gate_x_w, v_c_gate_x_b, v_c_lambda, v_c_w_out, v_ffn_w_up, v_ffn_conv_w, v_ffn_conv_b, v_ffn_w_down):
    wl = dict(zip(WEIGHTS, (norm_mix, norm_ffn, norm_final, ab_w_in, gdn_conv_w, gdn_a_log, gdn_dt_bias, gdn_norm, hgrn_lower_bounds, hgrn_norm, ab_w_out, c_w_in, c_conv_w, c_conv_b, c_gate_a_w, c_gate_a_b, c_gate_x_w, c_gate_x_b, c_lambda, c_w_out, ffn_w_up, ffn_conv_w, ffn_conv_b, ffn_w_down)))
    ml = dict(zip(WEIGHTS, (m_norm_mix, m_norm_ffn, m_norm_final, m_ab_w_in, m_gdn_conv_w, m_gdn_a_log, m_gdn_dt_bias, m_gdn_norm, m_hgrn_lower_bounds, m_hgrn_norm, m_ab_w_out, m_c_w_in, m_c_conv_w, m_c_conv_b, m_c_gate_a_w, m_c_gate_a_b, m_c_gate_x_w, m_c_gate_x_b, m_c_lambda, m_c_w_out, m_ffn_w_up, m_ffn_conv_w, m_ffn_conv_b, m_ffn_w_down)))
    vl = dict(zip(WEIGHTS, (v_norm_mix, v_norm_ffn, v_norm_final, v_ab_w_in, v_gdn_conv_w, v_gdn_a_log, v_gdn_dt_bias, v_gdn_norm, v_hgrn_lower_bounds, v_hgrn_norm, v_ab_w_out, v_c_w_in, v_c_conv_w, v_c_conv_b, v_c_gate_a_w, v_c_gate_a_b, v_c_gate_x_w, v_c_gate_x_b, v_c_lambda, v_c_w_out, v_ffn_w_up, v_ffn_conv_w, v_ffn_conv_b, v_ffn_w_down)))

    big = [n for n in SHARDED if n in MATMUL_WEIGHTS]
    vec = [n for n in SHARDED if n not in MATMUL_WEIGHTS]
    shards = {n: wl[n].astype(bf16) for n in big}
    shards['ab_w_in'] = jnp.pad(shards['ab_w_in'], ((0, 0), (0, 0), (0, AB_SHARD_PAD - AB_SHARD)))
    shards['ffn_w_up'] = jnp.pad(shards['ffn_w_up'], ((0, 0), (0, 0), (0, FF_PAD - FF_SHARD)))
    shards['vec'] = _pack([wl[n] for n in vec])
    pos = _position()
    layer_items = [_layer_items(l) for l in range(DEPTH)]
    posv = [v.astype(jnp.int32).reshape(1) for v in (4 * pos[0] + 2 * pos[1] + pos[2], 2 * pos[0] + pos[1], pos[2])]
    first_items = layer_items[0] + [('vec', None)]
    lands = [_place_own(first_items, shards, posv, "place_own_0")]
    lands += [_place_own(layer_items[l], shards, posv, "place_own_%d" % l) for l in range(1, DEPTH)]
    first = _gather_now(first_items, shards, lands[0])
    flight = {'shards': {n: shards[n] for n in big}}

    full = {n: wl[n] for n in REPLICATED}

    def start(l, token):
        sems, thru, flight['lands'], tok = _gather_start(layer_items[l], flight['shards'], lands[l], token,
                                                         "gather_start_%d" % l)
        flight['sems'] = list(sems)
        flight['shards'].update(thru)
        full['fwd', l - 1] = tok

    start(1, first[-1])

    for n, a in zip(vec, _unpack(first[-1], [wl[n].shape for n in vec], lead=True)):
        full[n] = _merge_shards(a, SHARD_AXIS[n])
    full['ffn_conv_w'] = _block_pad(full['ffn_conv_w'], 2, 4, FF_PAD)
    full['ffn_conv_b'] = _block_pad(full['ffn_conv_b'], 1, 4, FF_PAD)
    for n in big:
        full[n] = {}

    def fetch(l, x_in):
        items = layer_items[l]
        if l == 0:
            got = first[:len(items)]
        else:
            flight['shards'], got = _gather_wait(items, flight['sems'], flight['shards'], flight['lands'], x_in,
                                                 "gather_wait_%d" % l)
            got = _gather_forward(items, got, "gather_forward_%d" % l)
            if l + 1 < DEPTH:
                start(l + 1, got[0])
        for (nm, li), a in zip(items, got):
            if nm == 'ab_w_in':
                a = _ab_permute(_block_unpad(a, 1, N_DEV, AB_SHARD))
            elif nm in ('ab_w_out', 'c_w_out'):
                a = a.reshape(D_MODEL, D_MODEL)
            elif nm in ('c_gate_a_w', 'c_gate_x_w'):
                a = a.reshape(HEADS, LRU_BLOCK, LRU_BLOCK)
            elif nm == 'ffn_w_down':
                a = a.reshape(D_FFP, D_MODEL)
            full[nm][li] = a

    cpos = lax.axis_index("c").astype(jnp.int32).reshape(1)
    mine = (2 * lax.axis_index("x") + lax.axis_index("y")).astype(jnp.int32).reshape(1)
    pending = {}

    def push(l, g):
        fulls = {}
        for nm, _ in layer_items[l]:
            a = g[nm].astype(bf16)
            if nm == 'ab_w_in':
                a = _block_pad(_ab_unpermute(a), 1, N_DEV, AB_SHARD_PAD)
            fulls[nm] = a.reshape((1,) + BLOCK_LAYOUT[nm][0][1:])
        pair, _ = _pair_exchange(fulls, jnp.zeros((8, 128), f32), "_%d" % l)
        chips = [_chip_sum(nm, fulls[nm], pair[nm], cpos, "_%d" % l) for nm in fulls]
        sems, chips, crosses, tok = _cross_start(chips, "grad_cross_start_%d" % l)
        pending[l] = (sems, chips, crosses)
        full['bwd', l - 1] = tok

    loss, dx, grads = _local_step(x[0], loss_target[0], full, fetch, push)

    grads['ffn_conv_w'] = _block_unpad(grads['ffn_conv_w'], 2, 4, FF_SHARD)
    grads['ffn_conv_b'] = _block_unpad(grads['ffn_conv_b'], 1, 4, FF_SHARD)
    fulls = {'vec': _pack([_split_shards(grads[n], SHARD_AXIS[n]) for n in vec], lead=N_DEV)}
    chip, rep_chip, recv, rrep = _exchange_grads(fulls, _pack([grads[n] for n in REPLICATED]))
    res = {}
    stacked = {}
    after = full['bwd', -1]
    for l in (3, 2, 1, 0):
        sems, chips, lands = pending[l]
        chips, lands = _cross_wait(sems, chips, lands, after, "grad_cross_wait_%d" % l)
        for (n, li), own, parts in zip(layer_items[l], chips, lands):
            shp = wl[n].shape
            nl, c = shp[0], shp[-1]
            r = int(np.prod(shp[1:-1]))
            stacked[n] = _sum_adamw_layer(parts.reshape(4, r, -1), own.reshape(4, r, -1), mine, wl[n].reshape(nl, r, c),
                                          ml[n].reshape(nl, r, c), vl[n].reshape(nl, r, c), li, stacked.get(n),
                                          "adamw_%s_%d" % (n, li))
        if l == 1:
            after = stacked['ffn_w_up'][0]
    for n in big:
        for kind, o in zip(("grad", "delta", "new_m", "new_v"), stacked[n]):
            res[kind, n] = o.reshape(wl[n].shape)
    for names, parts, own, tag in ((vec, recv['vec'], chip['vec'], "adamw_vectors"),
                                   (REPLICATED, rrep, rep_chip, "adamw_replicated")):
        outs = _sum_adamw(parts, own, mine, _pack([wl[n] for n in names]), _pack([ml[n] for n in names]),
                          _pack([vl[n] for n in names]), tag)
        for kind, o in zip(("grad", "delta", "new_m", "new_v"), outs):
            for n, a in zip(names, _unpack(o, [wl[n].shape for n in names])):
                res[kind, n] = a

    loss = lax.psum(loss, ("x", "y", "c"))
    return (loss, dx[None], *[res[kind, n] for kind in ("grad", "delta", "new_m", "new_v") for n in WEIGHTS])
```

```python
import functools

import numpy as np
import jax
import jax.numpy as jnp
from jax import lax
from jax.experimental import pallas as pl
from jax.experimental.pallas import tpu as pltpu

f32 = jnp.float32
bf16 = jnp.bfloat16
HI = lax.Precision.HIGHEST
MESH = pl.DeviceIdType.MESH

N_DEV = 8
D_MODEL = 1024
DEPTH = 4
EPS = 1e-6
F_FLOOR = 1e-30
HEADS = 4
HEAD_DIM = 128
GDN_WIDTH = 512
GDN_CONV = 4
GDN_CHUNK = 64
HGRN_CHUNK = 16
HGRN_STEP = 128
MIX_WIDTH = 1024
AB_COLS = 4104
AB_PAD = 4224
LRU_WIDTH = 1024
LRU_BLOCK = 256
LRU_CONV = 4
RG_C = 8.0
D_FF = 2816
FF_SHARD = 704
FF_PAD = 768
D_FFP = 4 * FF_PAD
FF_ROWS = 352
AB_SHARD, AB_SHARD_PAD = 513, 640
FFN_CONV = 3
ADAM_LR, ADAM_B1, ADAM_B2, ADAM_EPS, ADAM_WD, ADAM_STEP = 0.001, 0.9, 0.999, 1e-08, 0.01, 10
VMEM_LIMIT = 56 * 1024 * 1024
PACK_LANES = 512
PACK_ROWS = 256

OFF_Q, OFF_K, OFF_V, OFF_Z, OFF_QB, OFF_FB, OFF_IB, OFF_GB, OFF_BA = 0, 512, 1024, 1536, 2048, 2560, 3072, 3584, 4096

WEIGHTS = ['norm_mix', 'norm_ffn', 'norm_final', 'ab_w_in', 'gdn_conv_w', 'gdn_a_log', 'gdn_dt_bias', 'gdn_norm',
           'hgrn_lower_bounds', 'hgrn_norm', 'ab_w_out', 'c_w_in', 'c_conv_w', 'c_conv_b', 'c_gate_a_w', 'c_gate_a_b',
           'c_gate_x_w', 'c_gate_x_b', 'c_lambda', 'c_w_out', 'ffn_w_up', 'ffn_conv_w', 'ffn_conv_b', 'ffn_w_down']
SHARD_AXIS = {'norm_mix': None, 'norm_ffn': None, 'norm_final': None, 'ab_w_in': 2, 'gdn_conv_w': 2, 'gdn_a_log': None,
              'gdn_dt_bias': None, 'gdn_norm': None, 'hgrn_lower_bounds': None, 'hgrn_norm': None, 'ab_w_out': 1,
              'c_w_in': 2, 'c_conv_w': 2, 'c_conv_b': 1, 'c_gate_a_w': 2, 'c_gate_a_b': 1, 'c_gate_x_w': 2,
              'c_gate_x_b': 1, 'c_lambda': 1, 'c_w_out': 1, 'ffn_w_up': 2, 'ffn_conv_w': 2, 'ffn_conv_b': None,
              'ffn_w_down': 1}
MATMUL_WEIGHTS = ('ab_w_in', 'ab_w_out', 'c_w_in', 'c_gate_a_w', 'c_gate_x_w', 'c_w_out', 'ffn_w_up', 'ffn_w_down')
SHARDED = [n for n in WEIGHTS if SHARD_AXIS[n] is not None]
REPLICATED = [n for n in WEIGHTS if SHARD_AXIS[n] is None]


def _tile(n, prefs=(512, 384, 256, 128)):
    for p in prefs:
        if n % p == 0:
            return p
    return n


def _cparams(sem=None):
    kw = dict(vmem_limit_bytes=VMEM_LIMIT)
    if sem is not None:
        kw['dimension_semantics'] = sem
    return pltpu.CompilerParams(**kw)


def _sds(shape, dtype):
    return jax.ShapeDtypeStruct(tuple(shape), dtype)


def _sigmoid(x):
    return 1.0 / (1.0 + jnp.exp(-x))


def _silu(x):
    return x * (0.5 * jnp.tanh(0.5 * x) + 0.5)


def _log1p(x):
    u = 1.0 + x
    return jnp.where(u == 1.0, x, jnp.log(u) * (x / jnp.where(u == 1.0, 1.0, u - 1.0)))


def _softplus(x):
    return jnp.maximum(x, 0.0) + _log1p(jnp.exp(-jnp.abs(x)))


def _expm1(x):
    small = jnp.abs(x) < 0.05
    xs = jnp.where(small, x, 0.0)
    series = xs * (1.0 + xs * (0.5 + xs * (1.0 / 6.0 + xs * (1.0 / 24.0 + xs * (1.0 / 120.0)))))
    return jnp.where(small, series, jnp.exp(x) - 1.0)


def _gelu(x):
    return 0.5 * x * (1.0 + jnp.tanh(0.7978845608028654 * (x + 0.044715 * x * x * x)))


def _rms(x, gain):
    return x * lax.rsqrt(jnp.mean(x * x, axis=-1, keepdims=True) + EPS) * gain


def _dot(a, b, dims=((1,), (0,)), precision=None):
    return lax.dot_general(a, b, (dims, ((), ())), precision=precision, preferred_element_type=f32)


def _bdot(a, b, dims=((1,), (0,))):
    return _dot(a.astype(bf16), b.astype(bf16), dims)


NT = ((1,), (1,))
TN = ((0,), (0,))


def _shift_down(x, k):
    if k == 0:
        return x
    row = lax.broadcasted_iota(jnp.int32, x.shape, 0)
    return jnp.where(row >= k, pltpu.roll(x, k, 0), 0.0)


def _shift_up(x, k, fill=0.0):
    if k == 0:
        return x
    n = x.shape[0]
    row = lax.broadcasted_iota(jnp.int32, x.shape, 0)
    return jnp.where(row < n - k, pltpu.roll(x, n - k, 0), fill)


def _conv_fwd(x, w_ref, width):
    acc = w_ref[width - 1:width, :] * x
    for k in range(width - 1):
        acc = acc + w_ref[k:k + 1, :] * _shift_down(x, width - 1 - k)
    return acc


def _conv_bwd(x, dout, w_ref, dw_ref, width):
    dx = w_ref[width - 1:width, :] * dout
    dw_ref[width - 1:width, :] = jnp.sum(dout * x, axis=0, keepdims=True)
    for k in range(width - 1):
        s = width - 1 - k
        dx = dx + w_ref[k:k + 1, :] * _shift_up(dout, s)
        dw_ref[k:k + 1, :] = jnp.sum(dout * _shift_down(x, s), axis=0, keepdims=True)
    return dx


MM_VMEM_BUDGET = 36 * 1024 * 1024
MM_MAX_TILE = 1024 * 1024


def _mm_tiles(m, n, k, out_bytes):
    best = None
    for tm in (1024, 512, 384, 256, 128):
        if m % tm:
            continue
        for tn in range(1536, 0, -128):
            if n % tn or tm * tn > MM_MAX_TILE:
                continue
            score = (tm * tn, min(tm, tn))
            if 2 * (tm * k * 2 + k * tn * 2 + tm * tn * out_bytes) <= MM_VMEM_BUDGET and (best is None or score > best[0]):
                best = (score, tm, tn)
    return (best[1], best[2]) if best else (_tile(m), _tile(n))


def _mm(a, b, *, ta=False, tb=False, add=None, out_dtype=f32, name):
    m, k = (a.shape[1], a.shape[0]) if ta else a.shape
    n = b.shape[0] if tb else b.shape[1]
    tm, tn = _mm_tiles(m, n, k, jnp.dtype(out_dtype).itemsize + (4 if add is not None else 0))
    dims = ((0 if ta else 1,), (1 if tb else 0,))

    def body(*refs):
        a_ref, b_ref = refs[0], refs[1]
        o_ref = refs[-1]
        r = _dot(a_ref[...], b_ref[...], dims)
        if add is not None:
            r = r + refs[2][...]
        o_ref[...] = r.astype(out_dtype)

    a_spec = pl.BlockSpec((k, tm), lambda j, i: (0, i)) if ta else pl.BlockSpec((tm, k), lambda j, i: (i, 0))
    b_spec = pl.BlockSpec((tn, k), lambda j, i: (j, 0)) if tb else pl.BlockSpec((k, tn), lambda j, i: (0, j))
    o_spec = pl.BlockSpec((tm, tn), lambda j, i: (i, j))
    ins, specs = [a, b], [a_spec, b_spec]
    if add is not None:
        ins.append(add)
        specs.append(o_spec)
    return pl.pallas_call(body, name=name, grid=(n // tn, m // tm), in_specs=specs, out_specs=o_spec,
                          out_shape=_sds((m, n), out_dtype), compiler_params=_cparams(("parallel", "parallel")))(*ins)


def _rms_fwd(x, gain, name):
    t, d = x.shape
    tr = _tile(t, (256, 128))

    def body(x_ref, g_ref, h_ref):
        h_ref[...] = _rms(x_ref[...], g_ref[...]).astype(bf16)

    return pl.pallas_call(body, name=name, grid=(t // tr,),
                          in_specs=[pl.BlockSpec((tr, d), lambda i: (i, 0)), pl.BlockSpec((1, d), lambda i: (0, 0))],
                          out_specs=pl.BlockSpec((tr, d), lambda i: (i, 0)), out_shape=_sds((t, d), bf16),
                          compiler_params=_cparams(("parallel",)))(x, gain)


def _rms_bwd(x, gain, dh, dres, name):
    t, d = x.shape
    tr = _tile(t, (256, 128))

    def body(x_ref, g_ref, dh_ref, dres_ref, dx_ref, dxb_ref, dg_ref):
        _, vjp = jax.vjp(_rms, x_ref[...], g_ref[...])
        dx, dg = vjp(dh_ref[...])
        dx = dx + dres_ref[...]
        dx_ref[...] = dx
        dxb_ref[...] = dx.astype(bf16)

        @pl.when(pl.program_id(0) == 0)
        def _():
            dg_ref[...] = jnp.zeros_like(dg_ref)

        dg_ref[...] += dg

    row = pl.BlockSpec((tr, d), lambda i: (i, 0))
    vec = pl.BlockSpec((1, d), lambda i: (0, 0))
    return pl.pallas_call(body, name=name, grid=(t // tr,), in_specs=[row, vec, row, row], out_specs=[row, row, vec],
                          out_shape=[_sds((t, d), f32), _sds((t, d), bf16), _sds((1, d), f32)],
                          compiler_params=_cparams(("arbitrary",)))(x, gain, dh, dres)


def _loss_head(x, gain, target):
    t, d = x.shape
    tr = _tile(t, (256, 128))

    def f(xv, g, tgt):
        err = _rms(xv, g) - tgt
        return 0.5 * jnp.sum(jnp.mean(err * err, axis=-1, keepdims=True), axis=0, keepdims=True)

    def body(x_ref, g_ref, t_ref, loss_ref, dx_ref, dxb_ref, dg_ref):
        loss, vjp = jax.vjp(lambda xv, g: f(xv, g, t_ref[...]), x_ref[...], g_ref[...])
        dx, dg = vjp(jnp.ones((1, 1), f32))
        dx_ref[...] = dx
        dxb_ref[...] = dx.astype(bf16)

        @pl.when(pl.program_id(0) == 0)
        def _():
            dg_ref[...] = jnp.zeros_like(dg_ref)
            loss_ref[...] = jnp.zeros_like(loss_ref)

        dg_ref[...] += dg
        loss_ref[...] += jnp.broadcast_to(loss, loss_ref.shape)

    row = pl.BlockSpec((tr, d), lambda i: (i, 0))
    vec = pl.BlockSpec((1, d), lambda i: (0, 0))
    one = pl.BlockSpec((8, 128), lambda i: (0, 0))
    return pl.pallas_call(body, name="loss_head", grid=(t // tr,), in_specs=[row, vec, row],
                          out_specs=[one, row, row, vec],
                          out_shape=[_sds((8, 128), f32), _sds((t, d), f32), _sds((t, d), bf16), _sds((1, d), f32)],
                          compiler_params=_cparams(("arbitrary",)))(x, gain, target)


def _ffn_act_fwd(u, conv_w, conv_b):
    t = u.shape[0]
    tc = FF_PAD // 2
    nb = D_FFP // tc

    def body(g_ref, v_ref, w_ref, b_ref, a_ref):
        gc = _conv_fwd(g_ref[...], w_ref, FFN_CONV) + b_ref[...]
        a_ref[...] = (_silu(gc) * v_ref[...]).astype(bf16)

    return pl.pallas_call(
        body, name="ffn_act_fwd", grid=(nb,),
        in_specs=[pl.BlockSpec((t, tc), lambda j: (0, j)), pl.BlockSpec((t, tc), lambda j: (0, j + nb)),
                  pl.BlockSpec((FFN_CONV, tc), lambda j: (0, j)), pl.BlockSpec((1, tc), lambda j: (0, j))],
        out_specs=pl.BlockSpec((t, tc), lambda j: (0, j)), out_shape=_sds((t, D_FFP), bf16),
        compiler_params=_cparams(("parallel",)))(u, u, conv_w, conv_b)


def _ffn_act_bwd(u, conv_w, conv_b, da):
    t = u.shape[0]
    tc = FF_PAD // 2
    nb = D_FFP // tc

    def act(gc, val):
        return _silu(gc) * val

    def body(g_ref, v_ref, w_ref, b_ref, da_ref, a_ref, dg_ref, dv_ref, dw_ref, db_ref):
        gp = g_ref[...]
        gc = _conv_fwd(gp, w_ref, FFN_CONV) + b_ref[...]
        a, vjp = jax.vjp(act, gc, v_ref[...])
        dgc, dval = vjp(da_ref[...])
        a_ref[...] = a.astype(bf16)
        dv_ref[...] = dval.astype(bf16)
        db_ref[...] = jnp.sum(dgc, axis=0, keepdims=True)
        dg_ref[...] = _conv_bwd(gp, dgc, w_ref, dw_ref, FFN_CONV).astype(bf16)

    col = pl.BlockSpec((t, tc), lambda j: (0, j))
    return pl.pallas_call(
        body, name="ffn_act_bwd", grid=(nb,),
        in_specs=[col, pl.BlockSpec((t, tc), lambda j: (0, j + nb)), pl.BlockSpec((FFN_CONV, tc), lambda j: (0, j)),
                  pl.BlockSpec((1, tc), lambda j: (0, j)), col],
        out_specs=[col, col, col, pl.BlockSpec((FFN_CONV, tc), lambda j: (0, j)), pl.BlockSpec((1, tc), lambda j: (0, j))],
        out_shape=[_sds((t, D_FFP), bf16), _sds((t, D_FFP), bf16), _sds((t, D_FFP), bf16), _sds((FFN_CONV, D_FFP), f32),
                   _sds((1, D_FFP), f32)],
        compiler_params=_cparams(("parallel",)))(u, u, conv_w, conv_b, da)


def _lru_gates(xc, ra, ia, lam):
    r = _sigmoid(ra)
    i = _sigmoid(ia)
    log_a = -RG_C * r * _softplus(-lam)
    a = jnp.exp(log_a)
    u = jnp.sqrt(jnp.maximum(-_expm1(2.0 * log_a), 0.0)) * (i * xc)
    return a, u


def _lin_scan(a, u):
    n = a.shape[0]
    row = lax.broadcasted_iota(jnp.int32, a.shape, 0)
    s = 1
    while s < n:
        keep = row >= s
        u = a * jnp.where(keep, pltpu.roll(u, s, 0), 0.0) + u
        a = a * jnp.where(keep, pltpu.roll(a, s, 0), 1.0)
        s *= 2
    return u


def _rev_scan(a_next, d):
    n = d.shape[0]
    row = lax.broadcasted_iota(jnp.int32, d.shape, 0)
    a = a_next
    s = 1
    while s < n:
        keep = row < n - s
        d = a * jnp.where(keep, pltpu.roll(d, n - s, 0), 0.0) + d
        a = a * jnp.where(keep, pltpu.roll(a, n - s, 0), 1.0)
        s *= 2
    return d


def _col_conv_fwd(p, col_off, conv_w, conv_b, width, tc, name):
    t = p.shape[0]
    c = conv_w.shape[1]
    ob = col_off // tc

    def body(x_ref, w_ref, b_ref, o_ref):
        o_ref[...] = _conv_fwd(x_ref[...], w_ref, width) + b_ref[...]

    return pl.pallas_call(
        body, name=name, grid=(c // tc,),
        in_specs=[pl.BlockSpec((t, tc), lambda j: (0, j + ob)), pl.BlockSpec((width, tc), lambda j: (0, j)),
                  pl.BlockSpec((1, tc), lambda j: (0, j))],
        out_specs=pl.BlockSpec((t, tc), lambda j: (0, j)), out_shape=_sds((t, c), f32),
        compiler_params=_cparams(("parallel",)))(p, conv_w, conv_b)


def _col_conv_bwd(p, col_off, conv_w, dxc, width, tc, name):
    t = p.shape[0]
    c = conv_w.shape[1]
    ob = col_off // tc

    def body(x_ref, w_ref, d_ref, dx_ref, dw_ref, db_ref):
        d = d_ref[...]
        db_ref[...] = jnp.sum(d, axis=0, keepdims=True)
        dx_ref[...] = _conv_bwd(x_ref[...], d, w_ref, dw_ref, width).astype(bf16)

    col = pl.BlockSpec((t, tc), lambda j: (0, j))
    return pl.pallas_call(
        body, name=name, grid=(c // tc,),
        in_specs=[pl.BlockSpec((t, tc), lambda j: (0, j + ob)), pl.BlockSpec((width, tc), lambda j: (0, j)), col],
        out_specs=[col, pl.BlockSpec((width, tc), lambda j: (0, j)), pl.BlockSpec((1, tc), lambda j: (0, j))],
        out_shape=[_sds((t, c), bf16), _sds((width, c), f32), _sds((1, c), f32)],
        compiler_params=_cparams(("parallel",)))(p, conv_w, dxc)


def _lru_fwd(p, xc, wa, ba, wx, bx, lam):
    t = p.shape[0]
    bw = LRU_BLOCK

    def body(y_ref, xc_ref, wa_ref, ba_ref, wx_ref, bx_ref, lam_ref, out_ref, hs_ref, a_ref):
        xc_v = xc_ref[...]
        xb = xc_v.astype(bf16)
        ra = _dot(xb, wa_ref[0]) + ba_ref[...]
        ia = _dot(xb, wx_ref[0]) + bx_ref[...]
        a, u = _lru_gates(xc_v, ra, ia, lam_ref[...])
        a_ref[...] = a
        hs = _lin_scan(a, u)
        hs_ref[...] = hs
        out_ref[...] = (hs * _gelu(y_ref[...])).astype(bf16)

    col = pl.BlockSpec((t, bw), lambda h: (0, h))
    vec = pl.BlockSpec((1, bw), lambda h: (0, h))
    mat = pl.BlockSpec((1, bw, bw), lambda h: (h, 0, 0))
    return pl.pallas_call(
        body, name="lru_fwd", grid=(HEADS,), in_specs=[col, col, mat, vec, mat, vec, vec], out_specs=[col, col, col],
        out_shape=[_sds((t, LRU_WIDTH), bf16), _sds((t, LRU_WIDTH), f32), _sds((t, LRU_WIDTH), f32)],
        compiler_params=_cparams(("parallel",)))(p, xc, wa, ba, wx, bx, lam)


def _lru_bwd_scan(p, a, hs, dout):
    t = p.shape[0]
    bw = LRU_BLOCK

    def body(y_ref, a_ref, hs_ref, do_ref, dy_ref, da_ref, du_ref):
        hs_v = hs_ref[...]
        do = do_ref[...]
        gate, vjp = jax.vjp(_gelu, y_ref[...])
        dy_ref[...] = vjp(do * hs_v)[0].astype(bf16)
        g = _rev_scan(_shift_up(a_ref[...], 1), do * gate)
        du_ref[...] = g
        da_ref[...] = g * _shift_down(hs_v, 1)

    col = pl.BlockSpec((t, bw), lambda h: (0, h))
    return pl.pallas_call(
        body, name="lru_bwd_scan", grid=(HEADS,), in_specs=[col, col, col, col], out_specs=[col, col, col],
        out_shape=[_sds((t, LRU_WIDTH), bf16), _sds((t, LRU_WIDTH), f32), _sds((t, LRU_WIDTH), f32)],
        compiler_params=_cparams(("parallel",)))(p, a, hs, dout)


def _lru_bwd_gates(xc, da, du, wa, ba, wx, bx, lam):
    t = xc.shape[0]
    bw = LRU_BLOCK
    tr = _tile(t, (512, 256, 128))

    def body(xc_ref, da_ref, du_ref, wa_ref, ba_ref, wx_ref, bx_ref, lam_ref,
             dxc_ref, dwa_ref, dwx_ref, dba_ref, dbx_ref, dlam_ref):
        xc_v = xc_ref[...]
        xb = xc_v.astype(bf16)
        ra = _dot(xb, wa_ref[0]) + ba_ref[...]
        ia = _dot(xb, wx_ref[0]) + bx_ref[...]
        _, vjp = jax.vjp(_lru_gates, xc_v, ra, ia, lam_ref[...])
        dxc, dra, dia, dlam = vjp((da_ref[...], du_ref[...]))
        drb, dib = dra.astype(bf16), dia.astype(bf16)
        dxc_ref[...] = dxc + _dot(drb, wa_ref[0], NT) + _dot(dib, wx_ref[0], NT)

        @pl.when(pl.program_id(1) == 0)
        def _():
            dwa_ref[...] = jnp.zeros_like(dwa_ref)
            dwx_ref[...] = jnp.zeros_like(dwx_ref)
            dba_ref[...] = jnp.zeros_like(dba_ref)
            dbx_ref[...] = jnp.zeros_like(dbx_ref)
            dlam_ref[...] = jnp.zeros_like(dlam_ref)

        dwa_ref[0] += _dot(xb, drb, TN)
        dwx_ref[0] += _dot(xb, dib, TN)
        dba_ref[...] += jnp.sum(dra, axis=0, keepdims=True)
        dbx_ref[...] += jnp.sum(dia, axis=0, keepdims=True)
        dlam_ref[...] += dlam

    tile = pl.BlockSpec((tr, bw), lambda h, i: (i, h))
    vec = pl.BlockSpec((1, bw), lambda h, i: (0, h))
    mat = pl.BlockSpec((1, bw, bw), lambda h, i: (h, 0, 0))
    return pl.pallas_call(
        body, name="lru_bwd_gates", grid=(HEADS, t // tr), in_specs=[tile, tile, tile, mat, vec, mat, vec, vec],
        out_specs=[tile, mat, mat, vec, vec, vec],
        out_shape=[_sds((t, LRU_WIDTH), f32), _sds((HEADS, bw, bw), f32), _sds((HEADS, bw, bw), f32),
                   _sds((1, LRU_WIDTH), f32), _sds((1, LRU_WIDTH), f32), _sds((1, LRU_WIDTH), f32)],
        compiler_params=_cparams(("parallel", "arbitrary")))(xc, da, du, wa, ba, wx, bx, lam)


def _gdn_pre_fn(cq, ck, cv, ba, alog, dtb, h):
    q, k, v = _silu(cq), _silu(ck), _silu(cv)
    q = q * lax.rsqrt(jnp.sum(q * q, axis=-1, keepdims=True) + EPS) * (HEAD_DIM ** -0.5)
    k = k * lax.rsqrt(jnp.sum(k * k, axis=-1, keepdims=True) + EPS)
    lane = lax.broadcasted_iota(jnp.int32, (1, HEAD_DIM), 1)
    mb = (lane == h).astype(f32)
    ma = (lane == HEADS + h).astype(f32)
    beta_raw = jnp.sum(ba * mb, axis=-1, keepdims=True)
    alpha = jnp.sum(ba * ma, axis=-1, keepdims=True)
    al = jnp.sum(alog * mb, axis=-1, keepdims=True)
    db = jnp.sum(dtb * mb, axis=-1, keepdims=True)
    beta = _sigmoid(beta_raw)
    g = -jnp.exp(al) * _softplus(alpha + db)
    return q, k, v, jnp.broadcast_to(beta, q.shape), jnp.broadcast_to(g, q.shape)


def _gdn_pre_fwd(p, conv_w, alog, dtb):
    t = p.shape[0]
    hd = HEAD_DIM

    def body(pq_ref, pk_ref, pv_ref, ba_ref, wq_ref, wk_ref, wv_ref, al_ref, dt_ref, q_ref, k_ref, v_ref, b_ref, g_ref):
        h = pl.program_id(0)
        cq = _conv_fwd(pq_ref[...], wq_ref, GDN_CONV)
        ck = _conv_fwd(pk_ref[...], wk_ref, GDN_CONV)
        cv = _conv_fwd(pv_ref[...], wv_ref, GDN_CONV)
        q, k, v, be, ge = _gdn_pre_fn(cq, ck, cv, ba_ref[...], al_ref[...], dt_ref[...], h)
        q_ref[...], k_ref[...], v_ref[...], b_ref[...], g_ref[...] = q, k, v, be, ge

    def pcol(off):
        return pl.BlockSpec((t, hd), lambda h: (0, h + off // hd))

    def wcol(off):
        return pl.BlockSpec((GDN_CONV, hd), lambda h: (0, h + off // hd))

    vec = pl.BlockSpec((1, hd), lambda h: (0, 0))
    out = pl.BlockSpec((t, hd), lambda h: (0, h))
    return pl.pallas_call(
        body, name="gdn_pre_fwd", grid=(HEADS,),
        in_specs=[pcol(OFF_Q), pcol(OFF_K), pcol(OFF_V), pl.BlockSpec((t, hd), lambda h: (0, OFF_BA // hd)),
                  wcol(0), wcol(GDN_WIDTH), wcol(2 * GDN_WIDTH), vec, vec],
        out_specs=[out] * 5, out_shape=[_sds((t, GDN_WIDTH), f32)] * 5,
        compiler_params=_cparams(("parallel",)))(p, p, p, p, conv_w, conv_w, conv_w, alog, dtb)


def _gdn_pre_bwd(p, conv_w, alog, dtb, dq, dk, dv, dbe, dge):
    t = p.shape[0]
    hd = HEAD_DIM

    def body(pq_ref, pk_ref, pv_ref, ba_ref, wq_ref, wk_ref, wv_ref, al_ref, dt_ref,
             dq_ref, dk_ref, dv_ref, dbe_ref, dge_ref,
             opq_ref, opk_ref, opv_ref, dba_ref, dwq_ref, dwk_ref, dwv_ref, dal_ref, ddt_ref):
        h = pl.program_id(0)
        pq, pk, pv = pq_ref[...], pk_ref[...], pv_ref[...]
        cq = _conv_fwd(pq, wq_ref, GDN_CONV)
        ck = _conv_fwd(pk, wk_ref, GDN_CONV)
        cv = _conv_fwd(pv, wv_ref, GDN_CONV)
        _, vjp = jax.vjp(functools.partial(_gdn_pre_fn, h=h), cq, ck, cv, ba_ref[...], al_ref[...], dt_ref[...])
        dcq, dck, dcv, dba, dal, ddt = vjp((dq_ref[...], dk_ref[...], dv_ref[...], dbe_ref[...], dge_ref[...]))
        opq_ref[...] = _conv_bwd(pq, dcq, wq_ref, dwq_ref, GDN_CONV).astype(bf16)
        opk_ref[...] = _conv_bwd(pk, dck, wk_ref, dwk_ref, GDN_CONV).astype(bf16)
        opv_ref[...] = _conv_bwd(pv, dcv, wv_ref, dwv_ref, GDN_CONV).astype(bf16)

        @pl.when(h == 0)
        def _():
            dba_ref[...] = jnp.zeros_like(dba_ref)
            dal_ref[...] = jnp.zeros_like(dal_ref)
            ddt_ref[...] = jnp.zeros_like(ddt_ref)

        dba_ref[...] += dba
        dal_ref[...] += dal
        ddt_ref[...] += ddt

    def pcol(off):
        return pl.BlockSpec((t, hd), lambda h: (0, h + off // hd))

    def wcol(off):
        return pl.BlockSpec((GDN_CONV, hd), lambda h: (0, h + off // hd))

    vec = pl.BlockSpec((1, hd), lambda h: (0, 0))
    col = pl.BlockSpec((t, hd), lambda h: (0, h))
    full = pl.BlockSpec((t, hd), lambda h: (0, 0))
    wout = pl.BlockSpec((GDN_CONV, hd), lambda h: (0, h))
    return pl.pallas_call(
        body, name="gdn_pre_bwd", grid=(HEADS,),
        in_specs=[pcol(OFF_Q), pcol(OFF_K), pcol(OFF_V), pl.BlockSpec((t, hd), lambda h: (0, OFF_BA // hd)),
                  wcol(0), wcol(GDN_WIDTH), wcol(2 * GDN_WIDTH), vec, vec, col, col, col, col, col],
        out_specs=[col, col, col, full, wout, wout, wout, vec, vec],
        out_shape=[_sds((t, GDN_WIDTH), bf16)] * 3 + [_sds((t, hd), f32)] + [_sds((GDN_CONV, GDN_WIDTH), f32)] * 3
        + [_sds((1, hd), f32)] * 2,
        compiler_params=_cparams(("arbitrary",)))(p, p, p, p, conv_w, conv_w, conv_w, alog, dtb, dq, dk, dv, dbe, dge)


BNN = (((2,), (1,)), ((0,), (0,)))
BNT = (((2,), (2,)), ((0,), (0,)))
BTN = (((1,), (1,)), ((0,), (0,)))


def _hdot(a, b, dn=BNN, precision=None):
    return lax.dot_general(a, b, dn, precision=precision, preferred_element_type=f32)


def _hbdot(a, b, dn=BNN):
    return _hdot(a.astype(bf16), b.astype(bf16), dn)


def _tri_inverse(a):
    c = a.shape[-1]
    r = lax.broadcasted_iota(jnp.int32, (c, c), 0)
    col = lax.broadcasted_iota(jnp.int32, (c, c), 1)
    m = -a
    inv = jnp.where(r == col, 1.0, 0.0) + m
    s = 2
    while s < c:
        m = _hdot(m, m, precision=HI)
        inv = inv + _hdot(inv, m, precision=HI)
        s *= 2
    return inv


@jax.custom_vjp
def _saved_inverse(a, inv):
    return inv


def _saved_inverse_fwd(a, inv):
    return inv, inv


def _saved_inverse_bwd(inv, dinv):
    return -_hdot(_hdot(inv, dinv, BTN, precision=HI), inv, BNT, precision=HI), jnp.zeros_like(inv)


_saved_inverse.defvjp(_saved_inverse_fwd, _saved_inverse_bwd)


def _gdn_chunk(s, q, k, v, ge, be, tinv=None):
    nh, c, _ = q.shape
    r = lax.broadcasted_iota(jnp.int32, (c, c), 0)
    col = lax.broadcasted_iota(jnp.int32, (c, c), 1)
    causal = r >= col
    tri = jnp.broadcast_to(causal.astype(f32), (nh, c, c))
    gc = _hdot(tri, ge, precision=HI)
    gcc = gc[:, :, :c]
    gcr = jnp.swapaxes(gc, 1, 2)[:, :c, :]
    decay = jnp.where(causal, jnp.exp(jnp.where(causal, gcc - gcr, 0.0)), 0.0)
    kb = k * be
    lower = jnp.where(r > col, _hbdot(kb, k, BNT) * decay, 0.0)
    tinv = _tri_inverse(lower) if tinv is None else _saved_inverse(lower, tinv)
    egc = jnp.exp(gc)
    u = _hdot(tinv, v * be, precision=HI)
    w = _hdot(tinv, kb * egc, precision=HI)
    attn = _hbdot(q, k, BNT) * decay
    gl = gc[:, c - 1:c, :]
    v_new = u - _hbdot(w, s)
    o = _hbdot(q * egc, s) + _hbdot(attn, v_new)
    s_new = s * jnp.exp(gl) + _hbdot(k * jnp.exp(gl - gc), v_new, BTN)
    return o, s_new, tinv


def _heads_major(ref):
    return jnp.stack([ref[:, h * HEAD_DIM:(h + 1) * HEAD_DIM] for h in range(HEADS)])


def _gdn_core_fwd(q, k, v, ge, be):
    t = q.shape[0]
    c, hd = GDN_CHUNK, HEAD_DIM
    n = t // c

    def body(q_ref, k_ref, v_ref, g_ref, b_ref, o_ref, st_ref, ti_ref, s_ref):
        @pl.when(pl.program_id(0) == 0)
        def _():
            s_ref[...] = jnp.zeros_like(s_ref)

        s = s_ref[...]
        st_ref[:, 0] = s
        o, s_new, tinv = _gdn_chunk(s, *[_heads_major(r) for r in (q_ref, k_ref, v_ref, g_ref, b_ref)])
        ti_ref[0] = tinv
        for h in range(HEADS):
            o_ref[:, h * hd:(h + 1) * hd] = o[h]
        s_ref[...] = s_new

    tile = pl.BlockSpec((c, GDN_WIDTH), lambda i: (i, 0))
    return pl.pallas_call(
        body, name="gdn_core_fwd", grid=(n,), in_specs=[tile] * 5,
        out_specs=[tile, pl.BlockSpec((HEADS, 1, hd, hd), lambda i: (0, i, 0, 0)),
                   pl.BlockSpec((1, HEADS, c, c), lambda i: (i, 0, 0, 0))],
        out_shape=[_sds((t, GDN_WIDTH), f32), _sds((HEADS, n, hd, hd), f32), _sds((n, HEADS, c, c), f32)],
        scratch_shapes=[pltpu.VMEM((HEADS, hd, hd), f32)],
        compiler_params=_cparams(("arbitrary",)))(q, k, v, ge, be)


def _gdn_core_bwd(q, k, v, ge, be, states, do):
    t = q.shape[0]
    c, hd = GDN_CHUNK, HEAD_DIM
    n = t // c
    states, tinvs = states

    def body(q_ref, k_ref, v_ref, g_ref, b_ref, st_ref, ti_ref, do_ref, dq_ref, dk_ref, dv_ref, dg_ref, db_ref, ds_ref):
        @pl.when(pl.program_id(0) == 0)
        def _():
            ds_ref[...] = jnp.zeros_like(ds_ref)

        tinv = ti_ref[0]
        _, vjp = jax.vjp(lambda *a: _gdn_chunk(*a, tinv=tinv)[:2], st_ref[:, 0],
                         *[_heads_major(r) for r in (q_ref, k_ref, v_ref, g_ref, b_ref)])
        ds, *dins = vjp((_heads_major(do_ref), ds_ref[...]))
        ds_ref[...] = ds
        for d_ref, d in zip((dq_ref, dk_ref, dv_ref, dg_ref, db_ref), dins):
            for h in range(HEADS):
                d_ref[:, h * hd:(h + 1) * hd] = d[h]

    tile = pl.BlockSpec((c, GDN_WIDTH), lambda i: (n - 1 - i, 0))
    return pl.pallas_call(
        body, name="gdn_core_bwd", grid=(n,),
        in_specs=[tile] * 5 + [pl.BlockSpec((HEADS, 1, hd, hd), lambda i: (0, n - 1 - i, 0, 0)),
                               pl.BlockSpec((1, HEADS, c, c), lambda i: (n - 1 - i, 0, 0, 0)), tile],
        out_specs=[tile] * 5, out_shape=[_sds((t, GDN_WIDTH), f32)] * 5,
        scratch_shapes=[pltpu.VMEM((HEADS, hd, hd), f32)],
        compiler_params=_cparams(("arbitrary",)))(q, k, v, ge, be, states, tinvs, do)


def _post_fn(o, z, gain):
    return _rms(o, gain) * _silu(z)


def _post_fwd(o, p, z_off, gain, name):
    t = o.shape[0]
    hd = HEAD_DIM

    def body(o_ref, z_ref, g_ref, y_ref):
        y_ref[...] = _post_fn(o_ref[...], z_ref[...], g_ref[...]).astype(bf16)

    col = pl.BlockSpec((t, hd), lambda h: (0, h))
    return pl.pallas_call(
        body, name=name, grid=(HEADS,),
        in_specs=[col, pl.BlockSpec((t, hd), lambda h: (0, h + z_off // hd)), pl.BlockSpec((1, hd), lambda h: (0, 0))],
        out_specs=col, out_shape=_sds((t, HEADS * hd), bf16), compiler_params=_cparams(("parallel",)))(o, p, gain)


def _post_bwd(o, p, z_off, gain, dmix, mix_off, name):
    t = o.shape[0]
    hd = HEAD_DIM

    def body(o_ref, z_ref, g_ref, dy_ref, do_ref, dz_ref, dg_ref):
        _, vjp = jax.vjp(_post_fn, o_ref[...], z_ref[...], g_ref[...])
        do, dz, dg = vjp(dy_ref[...])
        do_ref[...] = do
        dz_ref[...] = dz.astype(bf16)

        @pl.when(pl.program_id(0) == 0)
        def _():
            dg_ref[...] = jnp.zeros_like(dg_ref)

        dg_ref[...] += dg

    col = pl.BlockSpec((t, hd), lambda h: (0, h))
    vec = pl.BlockSpec((1, hd), lambda h: (0, 0))
    return pl.pallas_call(
        body, name=name, grid=(HEADS,),
        in_specs=[col, pl.BlockSpec((t, hd), lambda h: (0, h + z_off // hd)), vec,
                  pl.BlockSpec((t, hd), lambda h: (0, h + mix_off // hd))],
        out_specs=[col, col, vec], out_shape=[_sds((t, HEADS * hd), f32), _sds((t, HEADS * hd), bf16), _sds((1, hd), f32)],
        compiler_params=_cparams(("arbitrary",)))(o, p, gain, dmix)


def _hgrn_pre_fn(qb, fb, lbw, layer):
    l0, l1 = lbw[0:1, :], lbw[1:2, :]
    m = jnp.maximum(l0, l1)
    e0, e1 = jnp.exp(l0 - m), jnp.exp(l1 - m)
    p0, p1 = e0 / (e0 + e1), e1 / (e0 + e1)
    lb = (p0 - p0) if layer == 0 else ((p0 + p1) - p0)
    f = lb + (1.0 - lb) * _sigmoid(fb)
    return _silu(qb), 1.0 - f, jnp.log(jnp.maximum(f, F_FLOOR))


def _hgrn_pre_fwd(p, lbw, layer):
    t = p.shape[0]
    tc = HEAD_DIM

    def body(qb_ref, fb_ref, lb_ref, q_ref, k_ref, lf_ref):
        q_ref[...], k_ref[...], lf_ref[...] = _hgrn_pre_fn(qb_ref[...], fb_ref[...], lb_ref[...], layer)

    col = pl.BlockSpec((t, tc), lambda j: (0, j))
    return pl.pallas_call(
        body, name="hgrn_pre_fwd", grid=(GDN_WIDTH // tc,),
        in_specs=[pl.BlockSpec((t, tc), lambda j: (0, j + OFF_QB // tc)), pl.BlockSpec((t, tc), lambda j: (0, j + OFF_FB // tc)),
                  pl.BlockSpec((2, tc), lambda j: (0, j))],
        out_specs=[col] * 3, out_shape=[_sds((t, GDN_WIDTH), f32)] * 3,
        compiler_params=_cparams(("parallel",)))(p, p, lbw)


def _hgrn_pre_bwd(p, lbw, layer, dq, dk, dlf):
    t = p.shape[0]
    tc = HEAD_DIM

    def body(qb_ref, fb_ref, lb_ref, dq_ref, dk_ref, dlf_ref, dqb_ref, dfb_ref, dlb_ref):
        _, vjp = jax.vjp(functools.partial(_hgrn_pre_fn, layer=layer), qb_ref[...], fb_ref[...], lb_ref[...])
        dqb, dfb, dlb = vjp((dq_ref[...], dk_ref[...], dlf_ref[...]))
        dqb_ref[...] = dqb.astype(bf16)
        dfb_ref[...] = dfb.astype(bf16)
        dlb_ref[...] = dlb

    col = pl.BlockSpec((t, tc), lambda j: (0, j))
    lb = pl.BlockSpec((2, tc), lambda j: (0, j))
    return pl.pallas_call(
        body, name="hgrn_pre_bwd", grid=(GDN_WIDTH // tc,),
        in_specs=[pl.BlockSpec((t, tc), lambda j: (0, j + OFF_QB // tc)), pl.BlockSpec((t, tc), lambda j: (0, j + OFF_FB // tc)),
                  lb, col, col, col],
        out_specs=[col, col, lb], out_shape=[_sds((t, GDN_WIDTH), bf16)] * 2 + [_sds((2, GDN_WIDTH), f32)],
        compiler_params=_cparams(("parallel",)))(p, p, lbw, dq, dk, dlf)


def _hgrn_step(st, q, k, lf, v):
    c = HGRN_CHUNK
    nh = q.shape[0]
    r2 = lax.broadcasted_iota(jnp.int32, (c, c), 0)
    c2 = lax.broadcasted_iota(jnp.int32, (c, c), 1)
    tri = jnp.broadcast_to((r2 >= c2).astype(f32), (nh, c, c))
    i3 = lax.broadcasted_iota(jnp.int32, (c, c, HEAD_DIM), 0)
    j3 = lax.broadcasted_iota(jnp.int32, (c, c, HEAD_DIM), 1)
    mask = i3 >= j3
    outs = []
    for n in range(q.shape[1] // c):
        sl = slice(n * c, (n + 1) * c)
        qc, kc, lc, vc = q[:, sl], k[:, sl], lf[:, sl], v[:, sl]
        b = _hdot(tri, lc, precision=HI)
        rel = jnp.where(mask, jnp.exp(jnp.where(mask, b[:, :, None, :] - b[:, None, :, :], 0.0)), 0.0)
        scores = jnp.sum(qc[:, :, None, :] * kc[:, None, :, :] * rel, axis=-1)
        bl = b[:, c - 1:c, :]
        o = _hbdot(scores, vc) + _hbdot(qc * jnp.exp(b), st, BNT)
        st = st * jnp.exp(bl) + _hbdot(vc, kc * jnp.exp(bl - b), BTN)
        outs.append(o)
    return jnp.concatenate(outs, axis=1), st


def _hgrn_core_fwd(q, k, lf, p):
    t = q.shape[0]
    hd = HEAD_DIM
    rs = min(HGRN_STEP, t)
    n = t // rs

    def body(q_ref, k_ref, lf_ref, v_ref, o_ref, st_ref, s_ref):
        @pl.when(pl.program_id(0) == 0)
        def _():
            s_ref[...] = jnp.zeros_like(s_ref)

        s = s_ref[...]
        st_ref[:, 0] = s
        o, s_new = _hgrn_step(s, *[_heads_major(r) for r in (q_ref, k_ref, lf_ref, v_ref)])
        for h in range(HEADS):
            o_ref[:, h * hd:(h + 1) * hd] = o[h]
        s_ref[...] = s_new

    tile = pl.BlockSpec((rs, GDN_WIDTH), lambda i: (i, 0))
    return pl.pallas_call(
        body, name="hgrn_core_fwd", grid=(n,),
        in_specs=[tile, tile, tile, pl.BlockSpec((rs, GDN_WIDTH), lambda i: (i, OFF_IB // GDN_WIDTH))],
        out_specs=[tile, pl.BlockSpec((HEADS, 1, hd, hd), lambda i: (0, i, 0, 0))],
        out_shape=[_sds((t, GDN_WIDTH), f32), _sds((HEADS, n, hd, hd), f32)],
        scratch_shapes=[pltpu.VMEM((HEADS, hd, hd), f32)],
        compiler_params=_cparams(("arbitrary",)))(q, k, lf, p)


def _hgrn_core_bwd(q, k, lf, p, states, do):
    t = q.shape[0]
    hd = HEAD_DIM
    rs = min(HGRN_STEP, t)
    n = t // rs

    def body(q_ref, k_ref, lf_ref, v_ref, st_ref, do_ref, dq_ref, dk_ref, dlf_ref, dv_ref, ds_ref):
        @pl.when(pl.program_id(0) == 0)
        def _():
            ds_ref[...] = jnp.zeros_like(ds_ref)

        _, vjp = jax.vjp(_hgrn_step, st_ref[:, 0], *[_heads_major(r) for r in (q_ref, k_ref, lf_ref, v_ref)])
        ds, *dins = vjp((_heads_major(do_ref), ds_ref[...]))
        ds_ref[...] = ds
        for d_ref, d in zip((dq_ref, dk_ref, dlf_ref, dv_ref), dins):
            for h in range(HEADS):
                d_ref[:, h * hd:(h + 1) * hd] = d[h].astype(d_ref.dtype)

    tile = pl.BlockSpec((rs, GDN_WIDTH), lambda i: (n - 1 - i, 0))
    return pl.pallas_call(
        body, name="hgrn_core_bwd", grid=(n,),
        in_specs=[tile, tile, tile, pl.BlockSpec((rs, GDN_WIDTH), lambda i: (n - 1 - i, OFF_IB // GDN_WIDTH)),
                  pl.BlockSpec((HEADS, 1, hd, hd), lambda i: (0, n - 1 - i, 0, 0)), tile],
        out_specs=[tile] * 4, out_shape=[_sds((t, GDN_WIDTH), f32)] * 3 + [_sds((t, GDN_WIDTH), bf16)],
        scratch_shapes=[pltpu.VMEM((HEADS, hd, hd), f32)],
        compiler_params=_cparams(("arbitrary",)))(q, k, lf, p, states, do)


def _row(v):
    return v.reshape(1, -1)


def _anchored(w, row, key):
    tok = w.get(key)
    return row if tok is None else row + tok[0, 0]


def _pad_lanes(v, n=HEAD_DIM):
    return jnp.pad(v.reshape(1, -1), ((0, 0), (0, n - v.shape[-1])))


def _ffn_fwd(x, w, l):
    h = _rms_fwd(x, _row(w['norm_ffn'][l]), "ffn_norm")
    u = _mm(h, w['ffn_w_up'][l], tb=True, name="ffn_up")
    a = _ffn_act_fwd(u, w['ffn_conv_w'][l], _row(w['ffn_conv_b'][l]))
    y = _mm(a, w['ffn_w_down'][l], add=x, name="ffn_down")
    return y, (x, h, u)


def _ffn_bwd(saved, w, l, dy, dyb, grads):
    x, h, u = saved
    da = _mm(dyb, w['ffn_w_down'][l], tb=True, name="ffn_down_dx")
    a, dg, dv, dcw, dcb = _ffn_act_bwd(u, w['ffn_conv_w'][l], _anchored(w, _row(w['ffn_conv_b'][l]), ('bwd', l)), da)
    grads['ffn_w_down'][l] = _mm(a, dyb, ta=True, out_dtype=bf16, name="ffn_down_dw")
    du = jnp.concatenate([dg, dv], axis=1)
    grads['ffn_w_up'][l] = _mm(du, h, ta=True, out_dtype=bf16, name="ffn_up_dw")
    dh = _mm(du, w['ffn_w_up'][l], name="ffn_up_dx")
    dx, dxb, dgain = _rms_bwd(x, _row(w['norm_ffn'][l]), dh, dy, "ffn_norm_bwd")
    grads['ffn_conv_w'][l] = dcw
    grads['ffn_conv_b'][l] = dcb[0]
    grads['norm_ffn'][l] = dgain[0]
    return dx, dxb


def _odd_fwd(x, w, l, j):
    h = _rms_fwd(x, _anchored(w, _row(w['norm_mix'][l]), ('fwd', l)), "mix_norm")
    p = _mm(h, w['c_w_in'][j], name="lru_in")
    xc = _col_conv_fwd(p, LRU_WIDTH, w['c_conv_w'][j], _row(w['c_conv_b'][j]), LRU_CONV, 256, "lru_conv_fwd")
    out, hs, a = _lru_fwd(p, xc, w['c_gate_a_w'][j], _row(w['c_gate_a_b'][j]), w['c_gate_x_w'][j],
                          _row(w['c_gate_x_b'][j]), _row(w['c_lambda'][j]))
    y = _mm(out, w['c_w_out'][j], add=x, name="lru_out")
    return y, (x, h, p, xc, out, hs, a)


def _odd_bwd(saved, w, l, j, dy, dyb, grads):
    x, h, p, xc, out, hs, a = saved
    dout = _mm(dyb, w['c_w_out'][j], tb=True, name="lru_out_dx")
    grads['c_w_out'][j] = _mm(out, dyb, ta=True, out_dtype=bf16, name="lru_out_dw")
    dyb_, da, du = _lru_bwd_scan(p, a, hs, dout)
    dxc, dwa, dwx, dba, dbx, dlam = _lru_bwd_gates(xc, da, du, w['c_gate_a_w'][j], _row(w['c_gate_a_b'][j]),
                                                   w['c_gate_x_w'][j], _row(w['c_gate_x_b'][j]), _row(w['c_lambda'][j]))
    dxb_, dcw, dcb = _col_conv_bwd(p, LRU_WIDTH, w['c_conv_w'][j], dxc, LRU_CONV, 256, "lru_conv_bwd")
    dp = jnp.concatenate([dyb_, dxb_], axis=1)
    grads['c_w_in'][j] = _mm(h, dp, ta=True, out_dtype=bf16, name="lru_in_dw")
    dh = _mm(dp, w['c_w_in'][j], tb=True, name="lru_in_dx")
    dx, dxb, dgain = _rms_bwd(x, _row(w['norm_mix'][l]), dh, dy, "mix_norm_bwd")
    grads['c_gate_a_w'][j], grads['c_gate_x_w'][j] = dwa, dwx
    grads['c_gate_a_b'][j], grads['c_gate_x_b'][j], grads['c_lambda'][j] = dba[0], dbx[0], dlam[0]
    grads['c_conv_w'][j], grads['c_conv_b'][j] = dcw, dcb[0]
    grads['norm_mix'][l] = dgain[0]
    return dx, dxb


def _even_fwd(x, w, l, j):
    h = _rms_fwd(x, _anchored(w, _row(w['norm_mix'][l]), ('fwd', l)), "mix_norm")
    p = _mm(h, w['ab_w_in'][j], name="ab_in")
    alog, dtb = _pad_lanes(w['gdn_a_log'][j]), _pad_lanes(w['gdn_dt_bias'][j])
    q, k, v, be, ge = _gdn_pre_fwd(p, w['gdn_conv_w'][j], alog, dtb)
    oa, *sa = _gdn_core_fwd(q, k, v, ge, be)
    ya = _post_fwd(oa, p, OFF_Z, _row(w['gdn_norm'][j]), "gdn_post_fwd")
    qq, kk, lf = _hgrn_pre_fwd(p, w['hgrn_lower_bounds'], j)
    ob, sb = _hgrn_core_fwd(qq, kk, lf, p)
    yb = _post_fwd(ob, p, OFF_GB, _row(w['hgrn_norm'][j]), "hgrn_post_fwd")
    mix = jnp.concatenate([ya, yb], axis=1)
    y = _mm(mix, w['ab_w_out'][j], add=x, name="ab_out")
    return y, (x, h, p, q, k, v, be, ge, oa, sa, qq, kk, lf, ob, sb, mix)


def _even_bwd(saved, w, l, j, dy, dyb, grads):
    x, h, p, q, k, v, be, ge, oa, sa, qq, kk, lf, ob, sb, mix = saved
    alog, dtb = _pad_lanes(w['gdn_a_log'][j]), _pad_lanes(w['gdn_dt_bias'][j])
    dmix = _mm(dyb, w['ab_w_out'][j], tb=True, name="ab_out_dx")
    grads['ab_w_out'][j] = _mm(mix, dyb, ta=True, out_dtype=bf16, name="ab_out_dw")
    doa, dz, dgn = _post_bwd(oa, p, OFF_Z, _row(w['gdn_norm'][j]), dmix, 0, "gdn_post_bwd")
    dob, dgb, dhn = _post_bwd(ob, p, OFF_GB, _row(w['hgrn_norm'][j]), dmix, GDN_WIDTH, "hgrn_post_bwd")
    dq, dk, dv, dge, dbe = _gdn_core_bwd(q, k, v, ge, be, sa, doa)
    dpq, dpk, dpv, dba, dwq, dwk, dwv, dal, ddt = _gdn_pre_bwd(p, w['gdn_conv_w'][j], alog, dtb, dq, dk, dv, dbe, dge)
    dqq, dkk, dlf, dib = _hgrn_core_bwd(qq, kk, lf, p, sb, dob)
    dqb, dfb, dlb = _hgrn_pre_bwd(p, w['hgrn_lower_bounds'], j, dqq, dkk, dlf)
    dp = jnp.concatenate([dpq, dpk, dpv, dz, dqb, dfb, dib, dgb, dba.astype(bf16)], axis=1)
    grads['ab_w_in'][j] = _mm(h, dp, ta=True, out_dtype=bf16, name="ab_in_dw")
    dh = _mm(dp, w['ab_w_in'][j], tb=True, name="ab_in_dx")
    dx, dxb, dgain = _rms_bwd(x, _row(w['norm_mix'][l]), dh, dy, "mix_norm_bwd")
    grads['gdn_conv_w'][j] = jnp.concatenate([dwq, dwk, dwv], axis=1)
    grads['gdn_a_log'][j], grads['gdn_dt_bias'][j] = dal[0, :HEADS], ddt[0, :HEADS]
    grads['gdn_norm'][j], grads['hgrn_norm'][j] = dgn[0], dhn[0]
    grads['hgrn_lower_bounds'].append(dlb)
    grads['norm_mix'][l] = dgain[0]
    return dx, dxb


def _ab_permute(w_in):
    pad = jnp.zeros(w_in.shape[:-1] + (AB_PAD - AB_COLS,), w_in.dtype)
    return jnp.concatenate([w_in[..., :2048], w_in[..., 2056:], w_in[..., 2048:2056], pad], axis=-1)


def _ab_unpermute(g):
    return jnp.concatenate([g[..., :2048], g[..., 4096:4104], g[..., 2048:4096]], axis=-1)


def _block_pad(a, axis, nblk, padded):
    axis = axis % a.ndim
    s = a.shape
    a = a.reshape(s[:axis] + (nblk, s[axis] // nblk) + s[axis + 1:])
    pad = [(0, 0)] * a.ndim
    pad[axis + 1] = (0, padded - s[axis] // nblk)
    return jnp.pad(a, pad).reshape(s[:axis] + (nblk * padded,) + s[axis + 1:])


def _block_unpad(a, axis, nblk, width):
    axis = axis % a.ndim
    s = a.shape
    a = a.reshape(s[:axis] + (nblk, s[axis] // nblk) + s[axis + 1:])
    a = lax.slice_in_dim(a, 0, width, axis=axis + 1)
    return a.reshape(s[:axis] + (nblk * width,) + s[axis + 1:])


def _kernel_layout(w):
    w = dict(w)
    w['ab_w_in'] = _ab_permute(w['ab_w_in'])
    w['ffn_w_up'] = jnp.swapaxes(_block_pad(w['ffn_w_up'], 2, N_DEV, FF_PAD), 1, 2)
    w['ffn_w_down'] = _block_pad(w['ffn_w_down'], 1, 4, FF_PAD)
    w['ffn_conv_w'] = _block_pad(w['ffn_conv_w'], 2, 4, FF_PAD)
    w['ffn_conv_b'] = _block_pad(w['ffn_conv_b'], 1, 4, FF_PAD)
    return w


def _natural_grads(g):
    g = dict(g)
    g['ab_w_in'] = _ab_unpermute(g['ab_w_in'])
    g['ffn_w_up'] = _block_unpad(jnp.swapaxes(g['ffn_w_up'], 1, 2), 2, N_DEV, FF_SHARD)
    g['ffn_w_down'] = _block_unpad(g['ffn_w_down'], 1, 4, FF_SHARD)
    g['ffn_conv_w'] = _block_unpad(g['ffn_conv_w'], 2, 4, FF_SHARD)
    g['ffn_conv_b'] = _block_unpad(g['ffn_conv_b'], 1, 4, FF_SHARD)
    return g


def _local_step(x, target, w, fetch=None, push=None):
    grads = {n: [None] * (DEPTH if n in ('norm_mix', 'norm_ffn') or n.startswith('ffn_') else 2)
             for n in WEIGHTS if n not in ('norm_final', 'hgrn_lower_bounds')}
    grads['hgrn_lower_bounds'] = []
    saved = []
    for l in range(DEPTH):
        j = l // 2
        if fetch is not None:
            fetch(l, x)
        x, s_mix = (_even_fwd if l % 2 == 0 else _odd_fwd)(x, w, l, j)
        x, s_ffn = _ffn_fwd(x, w, l)
        saved.append((s_mix, s_ffn))
    loss, dx, dxb, dgf = _loss_head(x, _row(w['norm_final']), target)
    for l in reversed(range(DEPTH)):
        j = l // 2
        s_mix, s_ffn = saved[l]
        dx, dxb = _ffn_bwd(s_ffn, w, l, dx, dxb, grads)
        dx, dxb = (_even_bwd if l % 2 == 0 else _odd_bwd)(s_mix, w, l, j, dx, dxb, grads)
        if push is not None:
            push(l, {nm: grads[nm].pop(li) for nm, li in reversed(_layer_items(l))})
    out = {n: jnp.stack(g) for n, g in grads.items() if n != 'hgrn_lower_bounds' and g}
    out['hgrn_lower_bounds'] = grads['hgrn_lower_bounds'][0] + grads['hgrn_lower_bounds'][1]
    out['norm_final'] = dgf[0]
    return loss[0, 0], dx, out


def _position():
    return lax.axis_index("x"), lax.axis_index("y"), lax.axis_index("c")


BLOCK_LAYOUT = {
    'ab_w_in': ((2, D_MODEL, N_DEV * AB_SHARD_PAD), (2, D_MODEL, AB_SHARD_PAD)),
    'ab_w_out': ((2, N_DEV, 128, D_MODEL), (2, 128, D_MODEL)),
    'c_w_in': ((2, D_MODEL, 2 * LRU_WIDTH), (2, D_MODEL, 256)),
    'c_w_out': ((2, N_DEV, 128, D_MODEL), (2, 128, D_MODEL)),
    'c_gate_a_w': ((2, HEADS, N_DEV, 32, LRU_BLOCK), (2, HEADS, 32, LRU_BLOCK)),
    'c_gate_x_w': ((2, HEADS, N_DEV, 32, LRU_BLOCK), (2, HEADS, 32, LRU_BLOCK)),
    'ffn_w_up': ((DEPTH, N_DEV, FF_PAD, D_MODEL), (DEPTH, FF_PAD, D_MODEL)),
    'ffn_w_down': ((DEPTH, 4, FF_PAD, D_MODEL), (DEPTH, FF_ROWS, D_MODEL)),
}


COL_WINDOW = {'ab_w_in': AB_SHARD_PAD, 'c_w_in': 256}


def _block_index(name, p):
    d = 4 * p[0] + 2 * p[1] + p[2]
    if name in COL_WINDOW:
        return (slice(None), pl.ds(pl.multiple_of(d * COL_WINDOW[name], 128), COL_WINDOW[name]))
    if name == 'ffn_w_down':
        return (2 * p[0] + p[1], pl.ds(pl.multiple_of(p[2] * FF_ROWS, 16), FF_ROWS), slice(None))
    if name in ('c_gate_a_w', 'c_gate_x_w'):
        return (slice(None), d)
    return (d,)


def _block_of(name, ref, p, layered=True):
    idx = _block_index(name, p)
    if layered and name in BLOCK_LAYOUT:
        idx = (slice(None),) + idx
    return ref.at[idx]


def _layer_items(l):
    j = l // 2
    mix = ([('ab_w_in', j), ('ab_w_out', j)] if l % 2 == 0 else
           [('c_w_in', j), ('c_w_out', j), ('c_gate_a_w', j), ('c_gate_x_w', j)])
    return mix + [('ffn_w_up', l), ('ffn_w_down', l)]


def _own_land(name, shard_l, pos):
    x, y, c = pos
    d = 4 * x + 2 * y + c
    shape = BLOCK_LAYOUT[name][0][1:] if name in BLOCK_LAYOUT else (N_DEV,) + shard_l.shape
    zeros = jnp.zeros(shape, shard_l.dtype) if name == 'ffn_w_down' else lax.empty(shape, shard_l.dtype)
    if name in COL_WINDOW:
        return lax.dynamic_update_slice(zeros, shard_l, (0, d * COL_WINDOW[name]))
    if name == 'ffn_w_down':
        return lax.dynamic_update_slice(zeros, shard_l[None], (2 * x + y, c * FF_ROWS, 0))
    if name in ('c_gate_a_w', 'c_gate_x_w'):
        return lax.dynamic_update_slice(zeros, shard_l[:, None], (0, d, 0, 0))
    return lax.dynamic_update_slice(zeros, shard_l[None], (d,) + (0,) * shard_l.ndim)


def _place_own(items, shards, posv, name):
    n = len(items)
    down = [i for i, (nm, _) in enumerate(items) if nm == 'ffn_w_down']
    in_specs, out_specs, out_shapes, operands = [], [], [], []
    for nm, li in items:
        sh = shards[nm]
        shard_shape = sh.shape if li is None else sh.shape[1:]
        z = (0,) * len(shard_shape)
        operands.append(sh)
        in_specs.append(pl.BlockSpec(shard_shape, lambda i, d, q, c, z=z: z) if li is None else
                        pl.BlockSpec((1,) + shard_shape, lambda i, d, q, c, li=li, z=z: (li,) + z))
        out_shapes.append(_sds(BLOCK_LAYOUT[nm][0][1:] if nm in BLOCK_LAYOUT else (N_DEV,) + sh.shape, sh.dtype))
        if nm in COL_WINDOW:
            out_specs.append(pl.BlockSpec(shard_shape, lambda i, d, q, c: (0, d[0])))
        elif nm == 'ffn_w_down':
            out_specs.append(pl.BlockSpec((1,) + shard_shape, lambda i, d, q, c: (q[0], c[0], 0)))
        elif nm in ('c_gate_a_w', 'c_gate_x_w'):
            out_specs.append(pl.BlockSpec((HEADS, 1) + shard_shape[1:], lambda i, d, q, c: (0, d[0], 0, 0)))
        else:
            out_specs.append(pl.BlockSpec((1,) + shard_shape, lambda i, d, q, c, z=z: (d[0],) + z))

    def body(d_ref, q_ref, c_ref, *refs):
        for i, (nm, li) in enumerate(items):
            v = refs[i][...] if li is None else refs[i][0]
            o_ref = refs[n + len(down) + i]
            if nm in COL_WINDOW:
                o_ref[...] = v
            elif nm in ('c_gate_a_w', 'c_gate_x_w'):
                o_ref[:, 0] = v
            else:
                o_ref[0] = v

    zeros = [jnp.zeros(out_shapes[i].shape, out_shapes[i].dtype) for i in down]
    return pl.pallas_call(
        body, name=name, out_shape=out_shapes,
        grid_spec=pltpu.PrefetchScalarGridSpec(
            num_scalar_prefetch=3, grid=(1,), in_specs=in_specs + [pl.BlockSpec(memory_space=pl.ANY)] * len(down),
            out_specs=out_specs),
        input_output_aliases={3 + n + k: i for k, i in enumerate(down)},
        compiler_params=_cparams(("arbitrary",)))(*posv, *operands, *zeros)


def _src_of(shard_ref, li):
    return shard_ref if li is None else shard_ref.at[li]


def _gather_now(items, shards, lands):
    n = len(items)
    srcs = sorted({nm for nm, _ in items})

    def body(*refs):
        ins = dict(zip(srcs, refs[:len(srcs)]))
        outs = refs[len(srcs) + n:len(srcs) + 2 * n]
        send_sems, recv_sems = refs[len(srcs) + 2 * n:]
        x, y, c = _position()
        me, sibling = (x, y, c), (x, y, 1 - c)
        chips = [(1 - x, y), (x, 1 - y), (1 - x, 1 - y)]

        def copy(i, k, block, to, own=False):
            nm, li = items[i]
            dst = _block_of(nm, outs[i], block, layered=False)
            return pltpu.make_async_remote_copy(
                src_ref=_src_of(ins[nm], li) if own else dst, dst_ref=dst, send_sem=send_sems.at[7 * i + k],
                recv_sem=recv_sems.at[7 * i + k], device_id=to, device_id_type=MESH)

        first = []
        for i in range(n):
            first.append(copy(i, 0, me, sibling, own=True))
            first += [copy(i, 1 + j, me, (*chip, c), own=True) for j, chip in enumerate(chips)]
        for cp in first:
            cp.start()
        passed = []
        for j, chip in enumerate(chips):
            for i in range(n):
                copy(i, 1 + j, (*chip, c), me).wait_recv()
                fwd = copy(i, 4 + j, (*chip, c), sibling)
                fwd.start()
                passed.append(fwd)
        for i in range(n):
            copy(i, 0, sibling, me).wait_recv()
        for j, chip in enumerate(chips):
            for i in range(n):
                copy(i, 4 + j, (*chip, 1 - c), me).wait_recv()
        for cp in first + passed:
            cp.wait_send()

    any_spec = pl.BlockSpec(memory_space=pl.ANY)
    return pl.pallas_call(
        body, name="gather_first_layer", out_shape=[_sds(a.shape, a.dtype) for a in lands],
        in_specs=[any_spec] * (len(srcs) + n), out_specs=[any_spec] * n,
        input_output_aliases={len(srcs) + i: i for i in range(n)},
        scratch_shapes=[pltpu.SemaphoreType.DMA((7 * n,)), pltpu.SemaphoreType.DMA((7 * n,))],
    )(*[shards[nm] for nm in srcs], *lands)


FIRST_HOP = (1, 2, 4, 6)


def _lanes(name, land_ref, pos):
    if name == 'ffn_w_down':
        return [(FIRST_HOP, land_ref.at[pl.ds(0, 2), pl.ds(0, 2 * FF_ROWS)])]
    if name in COL_WINDOW:
        return [(FIRST_HOP, land_ref.at[:, pl.ds(0, 4 * COL_WINDOW[name])])]
    if name in ('c_gate_a_w', 'c_gate_x_w'):
        return [(FIRST_HOP, land_ref.at[:, pl.ds(0, 4)])]
    return [(FIRST_HOP, land_ref.at[pl.ds(0, 4)])]


def _n_lanes(items):
    return len(items)


def _gather_forward(items, lands, name):
    n = len(items)

    def body(*refs):
        outs = refs[n:2 * n]
        send_sems, recv_sems = refs[2 * n:]
        x, y, c = _position()
        chips = [(1 - x, y), (x, 1 - y), (1 - x, 1 - y)]
        copies, arrivals = [], []
        for i, (nm, _) in enumerate(items):
            for j, chip in enumerate(chips):
                mine = _block_of(nm, outs[i], (*chip, c), layered=False)
                theirs = _block_of(nm, outs[i], (*chip, 1 - c), layered=False)
                copies.append(pltpu.make_async_remote_copy(
                    src_ref=mine, dst_ref=mine, send_sem=send_sems.at[3 * i + j], recv_sem=recv_sems.at[3 * i + j],
                    device_id=(x, y, 1 - c), device_id_type=MESH))
                arrivals.append(pltpu.make_async_remote_copy(
                    src_ref=theirs, dst_ref=theirs, send_sem=send_sems.at[3 * i + j], recv_sem=recv_sems.at[3 * i + j],
                    device_id=(x, y, 1 - c), device_id_type=MESH))
        for cp in copies:
            cp.start()
        for cp in arrivals:
            cp.wait_recv()
        for cp in copies:
            cp.wait_send()

    any_spec = pl.BlockSpec(memory_space=pl.ANY)
    return pl.pallas_call(
        body, name=name, out_shape=[_sds(a.shape, a.dtype) for a in lands],
        in_specs=[any_spec] * n, out_specs=[any_spec] * n, input_output_aliases={i: i for i in range(n)},
        scratch_shapes=[pltpu.SemaphoreType.DMA((3 * n,)), pltpu.SemaphoreType.DMA((3 * n,))],
    )(*lands)


HBM_SPEC = pl.BlockSpec(memory_space=pltpu.HBM)
SEM_SPEC = pl.BlockSpec(memory_space=pltpu.SEMAPHORE)
SIDE_EFFECT = pltpu.SideEffectType.DATAFLOW_SIDE_EFFECTING


def _gather_start(items, shards, lands, token, name):
    n = len(items)
    srcs = sorted({nm for nm, _ in items})
    ns, nl = len(srcs), _n_lanes(items)

    def body(*refs):
        ins = dict(zip(srcs, refs[:ns]))
        land_refs = refs[ns:ns + n]
        sems = refs[ns + n + 1:ns + n + 1 + 2 * nl]
        x, y, c = _position()
        me = (x, y, c)
        lane = 0
        for i, (nm, li) in enumerate(items):
            for codes, _ in _lanes(nm, land_refs[i], me):
                for k in codes:
                    peer = (1 - x if (k >> 2) & 1 else x, 1 - y if (k >> 1) & 1 else y, 1 - c if k & 1 else c)
                    pltpu.make_async_remote_copy(
                        src_ref=_src_of(ins[nm], li), dst_ref=_block_of(nm, land_refs[i], me, layered=False),
                        send_sem=sems[2 * lane], recv_sem=sems[2 * lane + 1], device_id=peer, device_id_type=MESH).start()
                lane += 1
        refs[-1][...] = jnp.zeros((8, 128), f32)

    hbm = [pltpu.with_memory_space_constraint(a, pltpu.HBM) for a in [shards[nm] for nm in srcs] + list(lands)]
    outs = pl.pallas_call(
        body, name=name,
        out_shape=[pltpu.SemaphoreType.DMA(())] * (2 * nl) + [pltpu.HBM(a.shape, a.dtype) for a in hbm] + [_sds((8, 128), f32)],
        in_specs=[HBM_SPEC] * (ns + n) + [pl.BlockSpec(memory_space=pl.ANY)],
        out_specs=[SEM_SPEC] * (2 * nl) + [HBM_SPEC] * (ns + n) + [pl.BlockSpec(memory_space=pltpu.VMEM)],
        input_output_aliases={i: 2 * nl + i for i in range(ns + n)},
        compiler_params=pltpu.CompilerParams(has_side_effects=SIDE_EFFECT),
    )(*hbm, token)
    return outs[:2 * nl], dict(zip(srcs, outs[2 * nl:2 * nl + ns])), outs[2 * nl + ns:-1], outs[-1]


def _gather_wait(items, sems, shards, lands, after, name):
    n = len(items)
    srcs = sorted(shards)
    ns, nl = len(srcs), _n_lanes(items)

    def body(*refs):
        land_refs = refs[ns:ns + n]
        sem_refs = refs[ns + n:ns + n + 2 * nl]
        x, y, c = _position()
        lane = 0
        for i, (nm, _) in enumerate(items):
            for _, moved in _lanes(nm, land_refs[i], (x, y, c)):
                cp = pltpu.make_async_remote_copy(
                    src_ref=moved, dst_ref=moved, send_sem=sem_refs[2 * lane], recv_sem=sem_refs[2 * lane + 1],
                    device_id=(x, y, 1 - c), device_id_type=MESH)
                cp.wait_send()
                cp.wait_recv()
                lane += 1

    outs = pl.pallas_call(
        body, name=name, out_shape=[pltpu.HBM(shards[nm].shape, shards[nm].dtype) for nm in srcs]
        + [pltpu.HBM(a.shape, a.dtype) for a in lands],
        in_specs=[HBM_SPEC] * (ns + n) + [SEM_SPEC] * (2 * nl) + [pl.BlockSpec(memory_space=pl.ANY)],
        out_specs=[HBM_SPEC] * (ns + n), input_output_aliases={i: i for i in range(ns + n)},
        compiler_params=pltpu.CompilerParams(has_side_effects=SIDE_EFFECT),
    )(*[shards[nm] for nm in srcs], *lands, *sems, after)
    return dict(zip(srcs, outs[:ns])), outs[ns:]


def _exchange_grads(fulls, rep):
    cpos = lax.axis_index("c").astype(jnp.int32).reshape(1)
    pair, rep_pair = _pair_exchange(fulls, rep)
    chip = {nm: _chip_sum(nm, fulls[nm], pair[nm], cpos) for nm in fulls}
    rep_chip = _add_pair(rep, rep_pair, "chip_sum_replicated")
    cross, cross_rep = _cross_exchange(chip, rep_chip)
    return chip, rep_chip, cross, cross_rep


def _pair_exchange(fulls, rep, tag=""):
    names = list(fulls)
    n = len(names)
    shard_shape = {nm: ((fulls[nm].shape[0],) + BLOCK_LAYOUT[nm][1][1:] if nm in BLOCK_LAYOUT else fulls[nm].shape[1:])
                   for nm in names}

    def body(*refs):
        ins = dict(zip(names, refs[:n]))
        rep_ref = refs[n]
        pair = dict(zip(names, refs[n + 1:2 * n + 1]))
        rpair_ref = refs[2 * n + 1]
        send_sems, recv_sems = refs[2 * n + 2:]
        x, y, c = _position()
        sibling = (x, y, 1 - c)
        remote = []
        for i, nm in enumerate(names):
            for q in range(4):
                remote.append(pltpu.make_async_remote_copy(
                    src_ref=_block_of(nm, ins[nm], (q >> 1, q & 1, 1 - c)), dst_ref=pair[nm].at[q],
                    send_sem=send_sems.at[4 * i + q], recv_sem=recv_sems.at[4 * i + q], device_id=sibling,
                    device_id_type=MESH))
        remote.append(pltpu.make_async_remote_copy(
            src_ref=rep_ref, dst_ref=rpair_ref, send_sem=send_sems.at[4 * n], recv_sem=recv_sems.at[4 * n],
            device_id=sibling, device_id_type=MESH))
        for cp in remote:
            cp.start()
        for cp in remote:
            cp.wait_recv()
        for cp in remote:
            cp.wait_send()

    any_spec = pl.BlockSpec(memory_space=pl.ANY)
    four = [_sds((4,) + tuple(shard_shape[nm]), fulls[nm].dtype) for nm in names]
    outs = pl.pallas_call(
        body, name="grad_pair_exchange" + tag, out_shape=four + [_sds(rep.shape, rep.dtype)],
        in_specs=[any_spec] * (n + 1), out_specs=[any_spec] * (n + 1),
        scratch_shapes=[pltpu.SemaphoreType.DMA((4 * n + 1,)), pltpu.SemaphoreType.DMA((4 * n + 1,))],
    )(*[fulls[nm] for nm in names], rep)
    return dict(zip(names, outs[:n])), outs[n]


def _chip_sum(name, full, pair, cpos, tag=""):
    if name in COL_WINDOW:
        width = BLOCK_LAYOUT[name][1][-1]
        rows = full.shape[0] * full.shape[1]
        tr = 512

        def body(c_ref, f_ref, p_ref, o_ref):
            o_ref[0] = (f_ref[...].astype(f32) + p_ref[0].astype(f32)).astype(o_ref.dtype)

        slot = pl.BlockSpec((1, tr, width), lambda q, i, c: (q, i, 0))
        out = pl.pallas_call(
            body, name="chip_sum_" + name + tag, out_shape=_sds((4, rows, width), full.dtype),
            grid_spec=pltpu.PrefetchScalarGridSpec(
                num_scalar_prefetch=1, grid=(4, rows // tr),
                in_specs=[pl.BlockSpec((tr, width), lambda q, i, c: (i, 2 * q + c[0])), slot], out_specs=slot),
            compiler_params=_cparams(("parallel", "parallel")))(
            cpos, full.reshape(rows, N_DEV * width), pair.reshape(4, rows, width))
        return out.reshape(pair.shape)

    if name == 'ffn_w_down':
        f4, p4 = full, pair
        fspec = pl.BlockSpec((full.shape[0], 1, FF_ROWS, D_MODEL), lambda q, c: (0, q, c[0], 0))
    else:
        shard = pair.shape[1:]
        lead = int(np.prod(shard[:-2]))
        f4 = full.reshape((lead, N_DEV) + shard[-2:])
        p4 = pair.reshape((4, lead) + shard[-2:])
        fspec = pl.BlockSpec((lead, 1) + shard[-2:], lambda q, c: (0, 2 * q + c[0], 0, 0))

    def body4(c_ref, f_ref, p_ref, o_ref):
        o_ref[0] = (f_ref[:, 0].astype(f32) + p_ref[0].astype(f32)).astype(o_ref.dtype)

    slot = pl.BlockSpec((1,) + p4.shape[1:], lambda q, c: (q, 0, 0, 0))
    out = pl.pallas_call(
        body4, name="chip_sum_" + name + tag, out_shape=_sds(p4.shape, full.dtype),
        grid_spec=pltpu.PrefetchScalarGridSpec(num_scalar_prefetch=1, grid=(4,), in_specs=[fspec, slot], out_specs=slot),
        compiler_params=_cparams(("parallel",)))(cpos, f4, p4)
    return out.reshape(pair.shape)


def _add_pair(a, b, name):
    shp = a.shape
    r, c = int(np.prod(shp[:-1])), shp[-1]
    tr = _tile(r, (512, 256, 128, 64, 32, 16, 8))

    def body(a_ref, b_ref, o_ref):
        o_ref[...] = (a_ref[...].astype(f32) + b_ref[...].astype(f32)).astype(o_ref.dtype)

    tile = pl.BlockSpec((tr, c), lambda i: (i, 0))
    return pl.pallas_call(body, name=name, grid=(r // tr,), in_specs=[tile, tile], out_specs=tile,
                          out_shape=_sds((r, c), a.dtype), compiler_params=_cparams(("parallel",)))(
        a.reshape(r, c), b.reshape(r, c)).reshape(shp)


def _cross_exchange(chip, rep_chip):
    names = list(chip)
    n = len(names)

    def body(*refs):
        ins = dict(zip(names, refs[:n]))
        rep_ref = refs[n]
        outs = dict(zip(names, refs[2 * n + 2:3 * n + 2]))
        rrep_ref = refs[3 * n + 2]
        send_sems, recv_sems = refs[3 * n + 3:]
        x, y, c = _position()
        mine = 2 * x + y
        copies = []
        for k in range(1, 4):
            px, py = (1 - x if (k >> 1) & 1 else x), (1 - y if k & 1 else y)
            for i, nm in enumerate(names + ['']):
                src = rep_ref if i == n else ins[nm].at[2 * px + py]
                dst = (rrep_ref if i == n else outs[nm]).at[mine]
                copies.append(pltpu.make_async_remote_copy(
                    src_ref=src, dst_ref=dst, send_sem=send_sems.at[3 * i + k - 1], recv_sem=recv_sems.at[3 * i + k - 1],
                    device_id=(px, py, c), device_id_type=MESH))
        for cp in copies:
            cp.start()
        for cp in copies:
            cp.wait_recv()
        for cp in copies:
            cp.wait_send()

    any_spec = pl.BlockSpec(memory_space=pl.ANY)
    shapes = [_sds(chip[nm].shape, chip[nm].dtype) for nm in names] + [_sds((4,) + rep_chip.shape, rep_chip.dtype)]
    zeros = [jnp.zeros(s.shape, s.dtype) for s in shapes]
    outs = pl.pallas_call(
        body, name="grad_cross_exchange", out_shape=shapes,
        in_specs=[any_spec] * (2 * n + 2), out_specs=[any_spec] * (n + 1),
        input_output_aliases={n + 1 + i: i for i in range(n + 1)},
        scratch_shapes=[pltpu.SemaphoreType.DMA((3 * (n + 1),)), pltpu.SemaphoreType.DMA((3 * (n + 1),))],
    )(*[chip[nm] for nm in names], rep_chip, *zeros)
    return dict(zip(names, outs[:n])), outs[n]


def _cross_start(chips, name):
    n = len(chips)

    def body(*refs):
        chip_refs, land_refs = refs[:n], refs[n:2 * n]
        sems = refs[2 * n:4 * n]
        x, y, c = _position()
        mine = 2 * x + y
        for i in range(n):
            for k in range(1, 4):
                px, py = (1 - x if (k >> 1) & 1 else x), (1 - y if k & 1 else y)
                pltpu.make_async_remote_copy(
                    src_ref=chip_refs[i].at[2 * px + py], dst_ref=land_refs[i].at[mine], send_sem=sems[2 * i],
                    recv_sem=sems[2 * i + 1], device_id=(px, py, c), device_id_type=MESH).start()
        refs[-1][...] = jnp.zeros((8, 128), f32)

    hbm = [pltpu.with_memory_space_constraint(a, pltpu.HBM) for a in list(chips) + [jnp.zeros(a.shape, a.dtype) for a in chips]]
    outs = pl.pallas_call(
        body, name=name,
        out_shape=[pltpu.SemaphoreType.DMA(())] * (2 * n) + [pltpu.HBM(a.shape, a.dtype) for a in hbm] + [_sds((8, 128), f32)],
        in_specs=[HBM_SPEC] * (2 * n),
        out_specs=[SEM_SPEC] * (2 * n) + [HBM_SPEC] * (2 * n) + [pl.BlockSpec(memory_space=pltpu.VMEM)],
        input_output_aliases={i: 2 * n + i for i in range(2 * n)},
        compiler_params=pltpu.CompilerParams(has_side_effects=SIDE_EFFECT),
    )(*hbm)
    return outs[:2 * n], outs[2 * n:3 * n], outs[3 * n:4 * n], outs[4 * n]


def _cross_wait(sems, chips, lands, after, name):
    n = len(chips)

    def body(*refs):
        land_refs = refs[n:2 * n]
        sem_refs = refs[2 * n:4 * n]
        x, y, c = _position()
        for i in range(n):
            moved = land_refs[i].at[pl.ds(0, 3)]
            cp = pltpu.make_async_remote_copy(
                src_ref=moved, dst_ref=moved, send_sem=sem_refs[2 * i], recv_sem=sem_refs[2 * i + 1],
                device_id=(x, y, 1 - c), device_id_type=MESH)
            cp.wait_send()
            cp.wait_recv()

    outs = pl.pallas_call(
        body, name=name, out_shape=[pltpu.HBM(a.shape, a.dtype) for a in list(chips) + list(lands)],
        in_specs=[HBM_SPEC] * (2 * n) + [SEM_SPEC] * (2 * n) + [pl.BlockSpec(memory_space=pl.ANY)],
        out_specs=[HBM_SPEC] * (2 * n), input_output_aliases={i: i for i in range(2 * n)},
        compiler_params=pltpu.CompilerParams(has_side_effects=SIDE_EFFECT),
    )(*chips, *lands, *sems, after)
    return outs[:n], outs[n:]


def _sum_adamw_layer(parts, own, mine, w, m, v, li, prev, name):
    nl, r, l = w.shape
    lp = parts.shape[2]
    tr = r if r <= 512 else _tile(r, (512, FF_ROWS, 256, 128))
    c1 = 1.0 / (1.0 - ADAM_B1 ** ADAM_STEP)
    c2 = 1.0 / (1.0 - ADAM_B2 ** ADAM_STEP)
    k = 0 if prev is None else 4

    def body(mine_ref, p_ref, o_ref, w_ref, m_ref, v_ref, *rest):
        g_ref, d_ref, nm_ref, nv_ref = rest[k:]
        mine_v = o_ref[0].astype(f32)
        g = jnp.where(mine_ref[0] == 0, mine_v, p_ref[0].astype(f32))
        for s in range(1, 4):
            g = g + jnp.where(mine_ref[0] == s, mine_v, p_ref[s].astype(f32))
        if lp != l:
            g = g[:, :l]
        m_new = ADAM_B1 * m_ref[0] + (1.0 - ADAM_B1) * g
        v_new = ADAM_B2 * v_ref[0] + (1.0 - ADAM_B2) * (g * g)
        g_ref[0] = g
        nm_ref[0] = m_new
        nv_ref[0] = v_new
        d_ref[0] = -ADAM_LR * ((m_new * c1) / (jnp.sqrt(v_new * c2) + ADAM_EPS) + ADAM_WD * w_ref[0])

    tile = pl.BlockSpec((1, tr, l), lambda i, mn: (li, i, 0))
    keep = [pl.BlockSpec(memory_space=pl.ANY)] * k
    return pl.pallas_call(
        body, name=name, out_shape=[_sds((nl, r, l), f32)] * 4,
        grid_spec=pltpu.PrefetchScalarGridSpec(
            num_scalar_prefetch=1, grid=(r // tr,),
            in_specs=[pl.BlockSpec((4, tr, lp), lambda i, mn: (0, i, 0)), pl.BlockSpec((1, tr, lp), lambda i, mn: (mn[0], i, 0)),
                      tile, tile, tile] + keep,
            out_specs=[tile] * 4),
        input_output_aliases={6 + i: i for i in range(k)},
        compiler_params=_cparams(("parallel",)))(mine, parts, own, w, m, v, *(prev or ()))


def _sum_adamw(parts, own, mine, w, m, v, name):
    r, l = w.shape
    lp = parts.shape[2]
    tr = _tile(r, (256, 128, 64, 32, 16, 8))
    c1 = 1.0 / (1.0 - ADAM_B1 ** ADAM_STEP)
    c2 = 1.0 / (1.0 - ADAM_B2 ** ADAM_STEP)

    def body(mine_ref, p_ref, o_ref, w_ref, m_ref, v_ref, g_ref, d_ref, nm_ref, nv_ref):
        mine_v = (o_ref[0] if own.ndim == 3 else o_ref[...]).astype(f32)
        g = jnp.where(mine_ref[0] == 0, mine_v, p_ref[0].astype(f32))
        for s in range(1, parts.shape[0]):
            g = g + jnp.where(mine_ref[0] == s, mine_v, p_ref[s].astype(f32))
        if lp != l:
            g = g[:, :l]
        m_new = ADAM_B1 * m_ref[...] + (1.0 - ADAM_B1) * g
        v_new = ADAM_B2 * v_ref[...] + (1.0 - ADAM_B2) * (g * g)
        g_ref[...] = g
        nm_ref[...] = m_new
        nv_ref[...] = v_new
        d_ref[...] = -ADAM_LR * ((m_new * c1) / (jnp.sqrt(v_new * c2) + ADAM_EPS) + ADAM_WD * w_ref[...])

    tile = pl.BlockSpec((tr, l), lambda i, mn: (i, 0))
    own_spec = (pl.BlockSpec((1, tr, lp), lambda i, mn: (mn[0], i, 0)) if own.ndim == 3
                else pl.BlockSpec((tr, lp), lambda i, mn: (i, 0)))
    return pl.pallas_call(
        body, name=name, out_shape=[_sds((r, l), f32)] * 4,
        grid_spec=pltpu.PrefetchScalarGridSpec(
            num_scalar_prefetch=1, grid=(r // tr,),
            in_specs=[pl.BlockSpec((parts.shape[0], tr, lp), lambda i, mn: (0, i, 0)), own_spec, tile, tile, tile],
            out_specs=[tile] * 4),
        compiler_params=_cparams(("parallel",)))(mine, parts, own, w, m, v)


def _pack(arrs, lead=None):
    if lead is None:
        flat = jnp.concatenate([a.reshape(-1).astype(f32) for a in arrs])
        n = flat.shape[0]
    else:
        flat = jnp.concatenate([a.reshape(lead, -1).astype(f32) for a in arrs], axis=1)
        n = flat.shape[1]
    tot = -(-n // 1024) * 1024
    if lead is None:
        return jnp.pad(flat, (0, tot - n)).reshape(tot // 128, 128)
    return jnp.pad(flat, ((0, 0), (0, tot - n))).reshape(lead, tot // 128, 128)


def _unpack(packed, shapes, lead=False):
    flat = packed.reshape(packed.shape[0], -1) if lead else packed.reshape(-1)
    out, off = [], 0
    for s in shapes:
        n = int(np.prod(s))
        out.append(flat[:, off:off + n].reshape((packed.shape[0],) + tuple(s)) if lead else flat[off:off + n].reshape(s))
        off += n
    return out


def _merge_shards(g, axis):
    g = jnp.moveaxis(g, 0, axis)
    s = g.shape
    return g.reshape(s[:axis] + (s[axis] * s[axis + 1],) + s[axis + 2:])


def _split_shards(full, axis):
    s = full.shape
    g = full.reshape(s[:axis] + (N_DEV, s[axis] // N_DEV) + s[axis + 1:])
    return jnp.moveaxis(g, axis, 0)


def kernel(x, norm_mix, norm_ffn, norm_final, ab_w_in, gdn_conv_w, gdn_a_log, gdn_dt_bias, gdn_norm, hgrn_lower_bounds, hgrn_norm, ab_w_out, c_w_in, c_conv_w, c_conv_b, c_gate_a_w, c_gate_a_b, c_gate_x_w, c_gate_x_b, c_lambda, c_w_out, ffn_w_up, ffn_conv_w, ffn_conv_b, ffn_w_down, loss_target, m_norm_mix, m_norm_ffn, m_norm_final, m_ab_w_in, m_gdn_conv_w, m_gdn_a_log, m_gdn_dt_bias, m_gdn_norm, m_hgrn_lower_bounds, m_hgrn_norm, m_ab_w_out, m_c_w_in, m_c_conv_w, m_c_conv_b, m_c_gate_a_w, m_c_gate_a_b, m_c_gate_x_w, m_c_gate_x_b, m_c_lambda, m_c_w_out, m_ffn_w_up, m_ffn_conv_w, m_ffn_conv_b, m_ffn_w_down, v_norm_mix, v_norm_ffn, v_norm_final, v_ab_w_in, v_gdn_conv_w, v_gdn_a_log, v_gdn_dt_bias, v_gdn_norm, v_hgrn_lower_bounds, v_hgrn_norm, v_ab_w_out, v_c_w_in, v_c_conv_w, v_c_conv_b, v_c_gate_a_w, v_c_gate_a_b, v_c_gate_x_w, v_c_gate_x_b, v_c_lambda, v_c_w_out, v_ffn_w_up, v_ffn_conv_w, v_ffn_conv_b, v_ffn_w_down):
    wl = dict(zip(WEIGHTS, (norm_mix, norm_ffn, norm_final, ab_w_in, gdn_conv_w, gdn_a_log, gdn_dt_bias, gdn_norm, hgrn_lower_bounds, hgrn_norm, ab_w_out, c_w_in, c_conv_w, c_conv_b, c_gate_a_w, c_gate_a_b, c_gate_x_w, c_gate_x_b, c_lambda, c_w_out, ffn_w_up, ffn_conv_w, ffn_conv_b, ffn_w_down)))
    ml = dict(zip(WEIGHTS, (m_norm_mix, m_norm_ffn, m_norm_final, m_ab_w_in, m_gdn_conv_w, m_gdn_a_log, m_gdn_dt_bias, m_gdn_norm, m_hgrn_lower_bounds, m_hgrn_norm, m_ab_w_out, m_c_w_in, m_c_conv_w, m_c_conv_b, m_c_gate_a_w, m_c_gate_a_b, m_c_gate_x_w, m_c_gate_x_b, m_c_lambda, m_c_w_out, m_ffn_w_up, m_ffn_conv_w, m_ffn_conv_b, m_ffn_w_down)))
    vl = dict(zip(WEIGHTS, (v_norm_mix, v_norm_ffn, v_norm_final, v_ab_w_in, v_gdn_conv_w, v_gdn_a_log, v_gdn_dt_bias, v_gdn_norm, v_hgrn_lower_bounds, v_hgrn_norm, v_ab_w_out, v_c_w_in, v_c_conv_w, v_c_conv_b, v_c_gate_a_w, v_c_gate_a_b, v_c_gate_x_w, v_c_gate_x_b, v_c_lambda, v_c_w_out, v_ffn_w_up, v_ffn_conv_w, v_ffn_conv_b, v_ffn_w_down)))

    big = [n for n in SHARDED if n in MATMUL_WEIGHTS]
    vec = [n for n in SHARDED if n not in MATMUL_WEIGHTS]
    shards = {n: wl[n].astype(bf16) for n in big}
    shards['ab_w_in'] = jnp.pad(shards['ab_w_in'], ((0, 0), (0, 0), (0, AB_SHARD_PAD - AB_SHARD)))
    shards['ffn_w_up'] = jnp.pad(jnp.swapaxes(shards['ffn_w_up'], 1, 2), ((0, 0), (0, FF_PAD - FF_SHARD), (0, 0)))
    shards['vec'] = _pack([wl[n] for n in vec])
    pos = _position()
    layer_items = [_layer_items(l) for l in range(DEPTH)]
    posv = [v.astype(jnp.int32).reshape(1) for v in (4 * pos[0] + 2 * pos[1] + pos[2], 2 * pos[0] + pos[1], pos[2])]
    first_items = layer_items[0] + [('vec', None)]
    lands = [_place_own(first_items, shards, posv, "place_own_0")]
    lands += [_place_own(layer_items[l], shards, posv, "place_own_%d" % l) for l in range(1, DEPTH)]
    first = _gather_now(first_items, shards, lands[0])
    flight = {'shards': {n: shards[n] for n in big}}

    full = {n: wl[n] for n in REPLICATED}

    def start(l, token):
        sems, thru, flight['lands'], tok = _gather_start(layer_items[l], flight['shards'], lands[l], token,
                                                         "gather_start_%d" % l)
        flight['sems'] = list(sems)
        flight['shards'].update(thru)
        full['fwd', l - 1] = tok

    start(1, first[-1])

    for n, a in zip(vec, _unpack(first[-1], [wl[n].shape for n in vec], lead=True)):
        full[n] = _merge_shards(a, SHARD_AXIS[n])
    full['ffn_conv_w'] = _block_pad(full['ffn_conv_w'], 2, 4, FF_PAD)
    full['ffn_conv_b'] = _block_pad(full['ffn_conv_b'], 1, 4, FF_PAD)
    for n in big:
        full[n] = {}

    def fetch(l, x_in):
        items = layer_items[l]
        if l == 0:
            got = first[:len(items)]
        else:
            flight['shards'], got = _gather_wait(items, flight['sems'], flight['shards'], flight['lands'], x_in,
                                                 "gather_wait_%d" % l)
            got = _gather_forward(items, got, "gather_forward_%d" % l)
            if l + 1 < DEPTH:
                start(l + 1, got[0])
        for (nm, li), a in zip(items, got):
            if nm == 'ab_w_in':
                a = _ab_permute(_block_unpad(a, 1, N_DEV, AB_SHARD))
            elif nm in ('ab_w_out', 'c_w_out'):
                a = a.reshape(D_MODEL, D_MODEL)
            elif nm in ('c_gate_a_w', 'c_gate_x_w'):
                a = a.reshape(HEADS, LRU_BLOCK, LRU_BLOCK)
            elif nm == 'ffn_w_down':
                a = a.reshape(D_FFP, D_MODEL)
            elif nm == 'ffn_w_up':
                a = a.reshape(2 * D_FFP, D_MODEL)
            full[nm][li] = a

    cpos = lax.axis_index("c").astype(jnp.int32).reshape(1)
    mine = (2 * lax.axis_index("x") + lax.axis_index("y")).astype(jnp.int32).reshape(1)
    pending = {}

    def push(l, g):
        fulls = {}
        for nm, _ in layer_items[l]:
            a = g[nm].astype(bf16)
            if nm == 'ab_w_in':
                a = _block_pad(_ab_unpermute(a), 1, N_DEV, AB_SHARD_PAD)
            fulls[nm] = a.reshape((1,) + BLOCK_LAYOUT[nm][0][1:])
        pair, _ = _pair_exchange(fulls, jnp.zeros((8, 128), f32), "_%d" % l)
        chips = [_chip_sum(nm, fulls[nm], pair[nm], cpos, "_%d" % l) for nm in fulls]
        sems, chips, crosses, tok = _cross_start(chips, "grad_cross_start_%d" % l)
        pending[l] = (sems, chips, crosses)
        full['bwd', l - 1] = tok

    loss, dx, grads = _local_step(x[0], loss_target[0], full, fetch, push)

    grads['ffn_conv_w'] = _block_unpad(grads['ffn_conv_w'], 2, 4, FF_SHARD)
    grads['ffn_conv_b'] = _block_unpad(grads['ffn_conv_b'], 1, 4, FF_SHARD)
    fulls = {'vec': _pack([_split_shards(grads[n], SHARD_AXIS[n]) for n in vec], lead=N_DEV)}
    chip, rep_chip, recv, rrep = _exchange_grads(fulls, _pack([grads[n] for n in REPLICATED]))
    res = {}
    stacked = {}
    after = full['bwd', -1]
    for l in (3, 2, 1, 0):
        sems, chips, lands = pending[l]
        chips, lands = _cross_wait(sems, chips, lands, after, "grad_cross_wait_%d" % l)
        for (n, li), own, parts in zip(layer_items[l], chips, lands):
            if n == 'ffn_w_up':
                wmv = [jnp.swapaxes(a[n], 1, 2) for a in (wl, ml, vl)]
                rp = FF_PAD
            else:
                shp = wl[n].shape
                rp = int(np.prod(shp[1:-1]))
                wmv = [a[n].reshape(shp[0], rp, shp[-1]) for a in (wl, ml, vl)]
            stacked[n] = _sum_adamw_layer(parts.reshape(4, rp, -1), own.reshape(4, rp, -1), mine, *wmv, li,
                                          stacked.get(n), "adamw_%s_%d" % (n, li))
        if l == 1:
            after = stacked['ffn_w_up'][0]
    for n in big:
        for kind, o in zip(("grad", "delta", "new_m", "new_v"), stacked[n]):
            res[kind, n] = jnp.swapaxes(o, 1, 2) if n == 'ffn_w_up' else o.reshape(wl[n].shape)
    for names, parts, own, tag in ((vec, recv['vec'], chip['vec'], "adamw_vectors"),
                                   (REPLICATED, rrep, rep_chip, "adamw_replicated")):
        outs = _sum_adamw(parts, own, mine, _pack([wl[n] for n in names]), _pack([ml[n] for n in names]),
                          _pack([vl[n] for n in names]), tag)
        for kind, o in zip(("grad", "delta", "new_m", "new_v"), outs):
            for n, a in zip(names, _unpack(o, [wl[n].shape for n in names])):
                res[kind, n] = a

    loss = lax.psum(loss, ("x", "y", "c"))
    return (loss, dx[None], *[res[kind, n] for kind in ("grad", "delta", "new_m", "new_v") for n in WEIGHTS])
```

```python
import functools

import numpy as np
import jax
import jax.numpy as jnp
from jax import lax
from jax.experimental import pallas as pl
from jax.experimental.pallas import tpu as pltpu

f32 = jnp.float32
bf16 = jnp.bfloat16
HI = lax.Precision.HIGHEST
MESH = pl.DeviceIdType.MESH

N_DEV = 8
D_MODEL = 1024
DEPTH = 4
EPS = 1e-6
F_FLOOR = 1e-30
HEADS = 4
HEAD_DIM = 128
GDN_WIDTH = 512
GDN_CONV = 4
GDN_CHUNK = 64
HGRN_CHUNK = 16
HGRN_STEP = 128
MIX_WIDTH = 1024
AB_COLS = 4104
AB_PAD = 4224
LRU_WIDTH = 1024
LRU_BLOCK = 256
LRU_CONV = 4
RG_C = 8.0
D_FF = 2816
FF_SHARD = 704
FF_PAD = 768
D_FFP = 4 * FF_PAD
FF_ROWS = 352
AB_SHARD, AB_SHARD_PAD = 513, 640
FFN_CONV = 3
ADAM_LR, ADAM_B1, ADAM_B2, ADAM_EPS, ADAM_WD, ADAM_STEP = 0.001, 0.9, 0.999, 1e-08, 0.01, 10
VMEM_LIMIT = 56 * 1024 * 1024
ROW_SLAB = 32
PACK_LANES = 512
PACK_ROWS = 256

OFF_Q, OFF_K, OFF_V, OFF_Z, OFF_QB, OFF_FB, OFF_IB, OFF_GB, OFF_BA = 0, 512, 1024, 1536, 2048, 2560, 3072, 3584, 4096

WEIGHTS = ['norm_mix', 'norm_ffn', 'norm_final', 'ab_w_in', 'gdn_conv_w', 'gdn_a_log', 'gdn_dt_bias', 'gdn_norm',
           'hgrn_lower_bounds', 'hgrn_norm', 'ab_w_out', 'c_w_in', 'c_conv_w', 'c_conv_b', 'c_gate_a_w', 'c_gate_a_b',
           'c_gate_x_w', 'c_gate_x_b', 'c_lambda', 'c_w_out', 'ffn_w_up', 'ffn_conv_w', 'ffn_conv_b', 'ffn_w_down']
SHARD_AXIS = {'norm_mix': None, 'norm_ffn': None, 'norm_final': None, 'ab_w_in': 2, 'gdn_conv_w': 2, 'gdn_a_log': None,
              'gdn_dt_bias': None, 'gdn_norm': None, 'hgrn_lower_bounds': None, 'hgrn_norm': None, 'ab_w_out': 1,
              'c_w_in': 2, 'c_conv_w': 2, 'c_conv_b': 1, 'c_gate_a_w': 2, 'c_gate_a_b': 1, 'c_gate_x_w': 2,
              'c_gate_x_b': 1, 'c_lambda': 1, 'c_w_out': 1, 'ffn_w_up': 2, 'ffn_conv_w': 2, 'ffn_conv_b': None,
              'ffn_w_down': 1}
MATMUL_WEIGHTS = ('ab_w_in', 'ab_w_out', 'c_w_in', 'c_gate_a_w', 'c_gate_x_w', 'c_w_out', 'ffn_w_up', 'ffn_w_down')
SHARDED = [n for n in WEIGHTS if SHARD_AXIS[n] is not None]
REPLICATED = [n for n in WEIGHTS if SHARD_AXIS[n] is None]


def _tile(n, prefs=(512, 384, 256, 128)):
    for p in prefs:
        if n % p == 0:
            return p
    return n


def _cparams(sem=None):
    kw = dict(vmem_limit_bytes=VMEM_LIMIT)
    if sem is not None:
        kw['dimension_semantics'] = sem
    return pltpu.CompilerParams(**kw)


def _sds(shape, dtype):
    return jax.ShapeDtypeStruct(tuple(shape), dtype)


def _sigmoid(x):
    return 1.0 / (1.0 + jnp.exp(-x))


def _silu(x):
    return x * (0.5 * jnp.tanh(0.5 * x) + 0.5)


def _log1p(x):
    u = 1.0 + x
    return jnp.where(u == 1.0, x, jnp.log(u) * (x / jnp.where(u == 1.0, 1.0, u - 1.0)))


def _softplus(x):
    return jnp.maximum(x, 0.0) + _log1p(jnp.exp(-jnp.abs(x)))


def _expm1(x):
    small = jnp.abs(x) < 0.05
    xs = jnp.where(small, x, 0.0)
    series = xs * (1.0 + xs * (0.5 + xs * (1.0 / 6.0 + xs * (1.0 / 24.0 + xs * (1.0 / 120.0)))))
    return jnp.where(small, series, jnp.exp(x) - 1.0)


def _gelu(x):
    return 0.5 * x * (1.0 + jnp.tanh(0.7978845608028654 * (x + 0.044715 * x * x * x)))


def _rms(x, gain):
    return x * lax.rsqrt(jnp.mean(x * x, axis=-1, keepdims=True) + EPS) * gain


def _dot(a, b, dims=((1,), (0,)), precision=None):
    return lax.dot_general(a, b, (dims, ((), ())), precision=precision, preferred_element_type=f32)


def _bdot(a, b, dims=((1,), (0,))):
    return _dot(a.astype(bf16), b.astype(bf16), dims)


NT = ((1,), (1,))
TN = ((0,), (0,))


def _shift_down(x, k):
    if k == 0:
        return x
    row = lax.broadcasted_iota(jnp.int32, x.shape, 0)
    return jnp.where(row >= k, pltpu.roll(x, k, 0), 0.0)


def _shift_up(x, k, fill=0.0):
    if k == 0:
        return x
    n = x.shape[0]
    row = lax.broadcasted_iota(jnp.int32, x.shape, 0)
    return jnp.where(row < n - k, pltpu.roll(x, n - k, 0), fill)


def _conv_fwd(x, w_ref, width):
    acc = w_ref[width - 1:width, :] * x
    for k in range(width - 1):
        acc = acc + w_ref[k:k + 1, :] * _shift_down(x, width - 1 - k)
    return acc


def _conv_bwd(x, dout, w_ref, dw_ref, width):
    dx = w_ref[width - 1:width, :] * dout
    dw_ref[width - 1:width, :] = jnp.sum(dout * x, axis=0, keepdims=True)
    for k in range(width - 1):
        s = width - 1 - k
        dx = dx + w_ref[k:k + 1, :] * _shift_up(dout, s)
        dw_ref[k:k + 1, :] = jnp.sum(dout * _shift_down(x, s), axis=0, keepdims=True)
    return dx


MM_VMEM_BUDGET = 36 * 1024 * 1024
MM_MAX_TILE = 1024 * 1024


def _mm_tiles(m, n, k, out_bytes):
    best = None
    for tm in (1024, 512, 384, 256, 128):
        if m % tm:
            continue
        for tn in range(1536, 0, -128):
            if n % tn or tm * tn > MM_MAX_TILE:
                continue
            score = (tm * tn, min(tm, tn))
            if 2 * (tm * k * 2 + k * tn * 2 + tm * tn * out_bytes) <= MM_VMEM_BUDGET and (best is None or score > best[0]):
                best = (score, tm, tn)
    return (best[1], best[2]) if best else (_tile(m), _tile(n))


def _mm(a, b, *, ta=False, tb=False, add=None, out_dtype=f32, name):
    m, k = (a.shape[1], a.shape[0]) if ta else a.shape
    n = b.shape[0] if tb else b.shape[1]
    tm, tn = _mm_tiles(m, n, k, jnp.dtype(out_dtype).itemsize + (4 if add is not None else 0))
    dims = ((0 if ta else 1,), (1 if tb else 0,))

    def body(*refs):
        a_ref, b_ref = refs[0], refs[1]
        o_ref = refs[-1]
        r = _dot(a_ref[...], b_ref[...], dims)
        if add is not None:
            r = r + refs[2][...]
        o_ref[...] = r.astype(out_dtype)

    a_spec = pl.BlockSpec((k, tm), lambda j, i: (0, i)) if ta else pl.BlockSpec((tm, k), lambda j, i: (i, 0))
    b_spec = pl.BlockSpec((tn, k), lambda j, i: (j, 0)) if tb else pl.BlockSpec((k, tn), lambda j, i: (0, j))
    o_spec = pl.BlockSpec((tm, tn), lambda j, i: (i, j))
    ins, specs = [a, b], [a_spec, b_spec]
    if add is not None:
        ins.append(add)
        specs.append(o_spec)
    return pl.pallas_call(body, name=name, grid=(n // tn, m // tm), in_specs=specs, out_specs=o_spec,
                          out_shape=_sds((m, n), out_dtype), compiler_params=_cparams(("parallel", "parallel")))(*ins)


def _rms_fwd(x, gain, name):
    t, d = x.shape
    tr = _tile(t, (256, 128))

    def body(x_ref, g_ref, h_ref):
        h_ref[...] = _rms(x_ref[...], g_ref[...]).astype(bf16)

    return pl.pallas_call(body, name=name, grid=(t // tr,),
                          in_specs=[pl.BlockSpec((tr, d), lambda i: (i, 0)), pl.BlockSpec((1, d), lambda i: (0, 0))],
                          out_specs=pl.BlockSpec((tr, d), lambda i: (i, 0)), out_shape=_sds((t, d), bf16),
                          compiler_params=_cparams(("parallel",)))(x, gain)


def _rms_bwd(x, gain, dh, dres, name):
    t, d = x.shape
    tr = _tile(t, (256, 128))

    def body(x_ref, g_ref, dh_ref, dres_ref, dx_ref, dxb_ref, dg_ref):
        _, vjp = jax.vjp(_rms, x_ref[...], g_ref[...])
        dx, dg = vjp(dh_ref[...])
        dx = dx + dres_ref[...]
        dx_ref[...] = dx
        dxb_ref[...] = dx.astype(bf16)

        @pl.when(pl.program_id(0) == 0)
        def _():
            dg_ref[...] = jnp.zeros_like(dg_ref)

        dg_ref[...] += dg

    row = pl.BlockSpec((tr, d), lambda i: (i, 0))
    vec = pl.BlockSpec((1, d), lambda i: (0, 0))
    return pl.pallas_call(body, name=name, grid=(t // tr,), in_specs=[row, vec, row, row], out_specs=[row, row, vec],
                          out_shape=[_sds((t, d), f32), _sds((t, d), bf16), _sds((1, d), f32)],
                          compiler_params=_cparams(("arbitrary",)))(x, gain, dh, dres)


def _loss_head(x, gain, target):
    t, d = x.shape
    tr = _tile(t, (256, 128))

    def f(xv, g, tgt):
        err = _rms(xv, g) - tgt
        return 0.5 * jnp.sum(jnp.mean(err * err, axis=-1, keepdims=True), axis=0, keepdims=True)

    def body(x_ref, g_ref, t_ref, loss_ref, dx_ref, dxb_ref, dg_ref):
        loss, vjp = jax.vjp(lambda xv, g: f(xv, g, t_ref[...]), x_ref[...], g_ref[...])
        dx, dg = vjp(jnp.ones((1, 1), f32))
        dx_ref[...] = dx
        dxb_ref[...] = dx.astype(bf16)

        @pl.when(pl.program_id(0) == 0)
        def _():
            dg_ref[...] = jnp.zeros_like(dg_ref)
            loss_ref[...] = jnp.zeros_like(loss_ref)

        dg_ref[...] += dg
        loss_ref[...] += jnp.broadcast_to(loss, loss_ref.shape)

    row = pl.BlockSpec((tr, d), lambda i: (i, 0))
    vec = pl.BlockSpec((1, d), lambda i: (0, 0))
    one = pl.BlockSpec((8, 128), lambda i: (0, 0))
    return pl.pallas_call(body, name="loss_head", grid=(t // tr,), in_specs=[row, vec, row],
                          out_specs=[one, row, row, vec],
                          out_shape=[_sds((8, 128), f32), _sds((t, d), f32), _sds((t, d), bf16), _sds((1, d), f32)],
                          compiler_params=_cparams(("arbitrary",)))(x, gain, target)


def _ffn_act_fwd(u, conv_w, conv_b):
    t = u.shape[0]
    tc = FF_PAD // 2
    nb = D_FFP // tc

    def body(g_ref, v_ref, w_ref, b_ref, a_ref, gc_ref):
        gc_ref[...] = _conv_fwd(g_ref[...], w_ref, FFN_CONV) + b_ref[...]

        def slab(i, carry):
            rows = pl.ds(pl.multiple_of(i * ROW_SLAB, ROW_SLAB), ROW_SLAB)
            a_ref[rows, :] = (_silu(gc_ref[rows, :]) * v_ref[rows, :]).astype(bf16)
            return carry

        lax.fori_loop(0, t // ROW_SLAB, slab, 0)

    return pl.pallas_call(
        body, name="ffn_act_fwd", grid=(nb,),
        in_specs=[pl.BlockSpec((t, tc), lambda j: (0, j)), pl.BlockSpec((t, tc), lambda j: (0, j + nb)),
                  pl.BlockSpec((FFN_CONV, tc), lambda j: (0, j)), pl.BlockSpec((1, tc), lambda j: (0, j))],
        out_specs=pl.BlockSpec((t, tc), lambda j: (0, j)), out_shape=_sds((t, D_FFP), bf16),
        scratch_shapes=[pltpu.VMEM((t, tc), f32)],
        compiler_params=_cparams(("parallel",)))(u, u, conv_w, conv_b)


def _ffn_act_bwd(u, conv_w, conv_b, da):
    t = u.shape[0]
    tc = FF_PAD // 2
    nb = D_FFP // tc

    def act(gc, val):
        return _silu(gc) * val

    def body(g_ref, v_ref, w_ref, b_ref, da_ref, a_ref, dg_ref, dv_ref, dw_ref, db_ref, gc_ref):
        gp = g_ref[...]
        gc_ref[...] = _conv_fwd(gp, w_ref, FFN_CONV) + b_ref[...]

        def slab(i, carry):
            rows = pl.ds(pl.multiple_of(i * ROW_SLAB, ROW_SLAB), ROW_SLAB)
            a, vjp = jax.vjp(act, gc_ref[rows, :], v_ref[rows, :])
            dgc, dval = vjp(da_ref[rows, :])
            a_ref[rows, :] = a.astype(bf16)
            dv_ref[rows, :] = dval.astype(bf16)
            gc_ref[rows, :] = dgc
            return carry

        lax.fori_loop(0, t // ROW_SLAB, slab, 0)
        dgc = gc_ref[...]
        db_ref[...] = jnp.sum(dgc, axis=0, keepdims=True)
        dg_ref[...] = _conv_bwd(gp, dgc, w_ref, dw_ref, FFN_CONV).astype(bf16)

    col = pl.BlockSpec((t, tc), lambda j: (0, j))
    return pl.pallas_call(
        body, name="ffn_act_bwd", grid=(nb,),
        in_specs=[col, pl.BlockSpec((t, tc), lambda j: (0, j + nb)), pl.BlockSpec((FFN_CONV, tc), lambda j: (0, j)),
                  pl.BlockSpec((1, tc), lambda j: (0, j)), col],
        out_specs=[col, col, col, pl.BlockSpec((FFN_CONV, tc), lambda j: (0, j)), pl.BlockSpec((1, tc), lambda j: (0, j))],
        out_shape=[_sds((t, D_FFP), bf16), _sds((t, D_FFP), bf16), _sds((t, D_FFP), bf16), _sds((FFN_CONV, D_FFP), f32),
                   _sds((1, D_FFP), f32)],
        scratch_shapes=[pltpu.VMEM((t, tc), f32)],
        compiler_params=_cparams(("parallel",)))(u, u, conv_w, conv_b, da)


def _lru_gates(xc, ra, ia, lam):
    r = _sigmoid(ra)
    i = _sigmoid(ia)
    log_a = -RG_C * r * _softplus(-lam)
    a = jnp.exp(log_a)
    u = jnp.sqrt(jnp.maximum(-_expm1(2.0 * log_a), 0.0)) * (i * xc)
    return a, u


def _lin_scan(a, u):
    n = a.shape[0]
    row = lax.broadcasted_iota(jnp.int32, a.shape, 0)
    s = 1
    while s < n:
        keep = row >= s
        u = a * jnp.where(keep, pltpu.roll(u, s, 0), 0.0) + u
        a = a * jnp.where(keep, pltpu.roll(a, s, 0), 1.0)
        s *= 2
    return u


def _rev_scan(a_next, d):
    n = d.shape[0]
    row = lax.broadcasted_iota(jnp.int32, d.shape, 0)
    a = a_next
    s = 1
    while s < n:
        keep = row < n - s
        d = a * jnp.where(keep, pltpu.roll(d, n - s, 0), 0.0) + d
        a = a * jnp.where(keep, pltpu.roll(a, n - s, 0), 1.0)
        s *= 2
    return d


def _col_conv_fwd(p, col_off, conv_w, conv_b, width, tc, name):
    t = p.shape[0]
    c = conv_w.shape[1]
    ob = col_off // tc

    def body(x_ref, w_ref, b_ref, o_ref):
        o_ref[...] = _conv_fwd(x_ref[...], w_ref, width) + b_ref[...]

    return pl.pallas_call(
        body, name=name, grid=(c // tc,),
        in_specs=[pl.BlockSpec((t, tc), lambda j: (0, j + ob)), pl.BlockSpec((width, tc), lambda j: (0, j)),
                  pl.BlockSpec((1, tc), lambda j: (0, j))],
        out_specs=pl.BlockSpec((t, tc), lambda j: (0, j)), out_shape=_sds((t, c), f32),
        compiler_params=_cparams(("parallel",)))(p, conv_w, conv_b)


def _col_conv_bwd(p, col_off, conv_w, dxc, width, tc, name):
    t = p.shape[0]
    c = conv_w.shape[1]
    ob = col_off // tc

    def body(x_ref, w_ref, d_ref, dx_ref, dw_ref, db_ref):
        d = d_ref[...]
        db_ref[...] = jnp.sum(d, axis=0, keepdims=True)
        dx_ref[...] = _conv_bwd(x_ref[...], d, w_ref, dw_ref, width).astype(bf16)

    col = pl.BlockSpec((t, tc), lambda j: (0, j))
    return pl.pallas_call(
        body, name=name, grid=(c // tc,),
        in_specs=[pl.BlockSpec((t, tc), lambda j: (0, j + ob)), pl.BlockSpec((width, tc), lambda j: (0, j)), col],
        out_specs=[col, pl.BlockSpec((width, tc), lambda j: (0, j)), pl.BlockSpec((1, tc), lambda j: (0, j))],
        out_shape=[_sds((t, c), bf16), _sds((width, c), f32), _sds((1, c), f32)],
        compiler_params=_cparams(("parallel",)))(p, conv_w, dxc)


def _lru_fwd(p, xc, wa, ba, wx, bx, lam):
    t = p.shape[0]
    bw = LRU_BLOCK

    def body(y_ref, xc_ref, wa_ref, ba_ref, wx_ref, bx_ref, lam_ref, out_ref, hs_ref, a_ref):
        xc_v = xc_ref[...]
        xb = xc_v.astype(bf16)
        ra = _dot(xb, wa_ref[0]) + ba_ref[...]
        ia = _dot(xb, wx_ref[0]) + bx_ref[...]
        a, u = _lru_gates(xc_v, ra, ia, lam_ref[...])
        a_ref[...] = a
        hs = _lin_scan(a, u)
        hs_ref[...] = hs
        out_ref[...] = (hs * _gelu(y_ref[...])).astype(bf16)

    col = pl.BlockSpec((t, bw), lambda h: (0, h))
    vec = pl.BlockSpec((1, bw), lambda h: (0, h))
    mat = pl.BlockSpec((1, bw, bw), lambda h: (h, 0, 0))
    return pl.pallas_call(
        body, name="lru_fwd", grid=(HEADS,), in_specs=[col, col, mat, vec, mat, vec, vec], out_specs=[col, col, col],
        out_shape=[_sds((t, LRU_WIDTH), bf16), _sds((t, LRU_WIDTH), f32), _sds((t, LRU_WIDTH), f32)],
        compiler_params=_cparams(("parallel",)))(p, xc, wa, ba, wx, bx, lam)


def _lru_bwd_scan(p, a, hs, dout):
    t = p.shape[0]
    bw = LRU_BLOCK

    def body(y_ref, a_ref, hs_ref, do_ref, dy_ref, da_ref, du_ref):
        hs_v = hs_ref[...]
        do = do_ref[...]
        gate, vjp = jax.vjp(_gelu, y_ref[...])
        dy_ref[...] = vjp(do * hs_v)[0].astype(bf16)
        g = _rev_scan(_shift_up(a_ref[...], 1), do * gate)
        du_ref[...] = g
        da_ref[...] = g * _shift_down(hs_v, 1)

    col = pl.BlockSpec((t, bw), lambda h: (0, h))
    return pl.pallas_call(
        body, name="lru_bwd_scan", grid=(HEADS,), in_specs=[col, col, col, col], out_specs=[col, col, col],
        out_shape=[_sds((t, LRU_WIDTH), bf16), _sds((t, LRU_WIDTH), f32), _sds((t, LRU_WIDTH), f32)],
        compiler_params=_cparams(("parallel",)))(p, a, hs, dout)


def _lru_bwd_gates(xc, da, du, wa, ba, wx, bx, lam):
    t = xc.shape[0]
    bw = LRU_BLOCK
    tr = _tile(t, (512, 256, 128))

    def body(xc_ref, da_ref, du_ref, wa_ref, ba_ref, wx_ref, bx_ref, lam_ref,
             dxc_ref, dwa_ref, dwx_ref, dba_ref, dbx_ref, dlam_ref):
        xc_v = xc_ref[...]
        xb = xc_v.astype(bf16)
        ra = _dot(xb, wa_ref[0]) + ba_ref[...]
        ia = _dot(xb, wx_ref[0]) + bx_ref[...]
        _, vjp = jax.vjp(_lru_gates, xc_v, ra, ia, lam_ref[...])
        dxc, dra, dia, dlam = vjp((da_ref[...], du_ref[...]))
        drb, dib = dra.astype(bf16), dia.astype(bf16)
        dxc_ref[...] = dxc + _dot(drb, wa_ref[0], NT) + _dot(dib, wx_ref[0], NT)

        @pl.when(pl.program_id(1) == 0)
        def _():
            dwa_ref[...] = jnp.zeros_like(dwa_ref)
            dwx_ref[...] = jnp.zeros_like(dwx_ref)
            dba_ref[...] = jnp.zeros_like(dba_ref)
            dbx_ref[...] = jnp.zeros_like(dbx_ref)
            dlam_ref[...] = jnp.zeros_like(dlam_ref)

        dwa_ref[0] += _dot(xb, drb, TN)
        dwx_ref[0] += _dot(xb, dib, TN)
        dba_ref[...] += jnp.sum(dra, axis=0, keepdims=True)
        dbx_ref[...] += jnp.sum(dia, axis=0, keepdims=True)
        dlam_ref[...] += dlam

    tile = pl.BlockSpec((tr, bw), lambda h, i: (i, h))
    vec = pl.BlockSpec((1, bw), lambda h, i: (0, h))
    mat = pl.BlockSpec((1, bw, bw), lambda h, i: (h, 0, 0))
    return pl.pallas_call(
        body, name="lru_bwd_gates", grid=(HEADS, t // tr), in_specs=[tile, tile, tile, mat, vec, mat, vec, vec],
        out_specs=[tile, mat, mat, vec, vec, vec],
        out_shape=[_sds((t, LRU_WIDTH), f32), _sds((HEADS, bw, bw), f32), _sds((HEADS, bw, bw), f32),
                   _sds((1, LRU_WIDTH), f32), _sds((1, LRU_WIDTH), f32), _sds((1, LRU_WIDTH), f32)],
        compiler_params=_cparams(("parallel", "arbitrary")))(xc, da, du, wa, ba, wx, bx, lam)


def _gdn_pre_fn(cq, ck, cv, ba, alog, dtb, h):
    q, k, v = _silu(cq), _silu(ck), _silu(cv)
    q = q * lax.rsqrt(jnp.sum(q * q, axis=-1, keepdims=True) + EPS) * (HEAD_DIM ** -0.5)
    k = k * lax.rsqrt(jnp.sum(k * k, axis=-1, keepdims=True) + EPS)
    lane = lax.broadcasted_iota(jnp.int32, (1, HEAD_DIM), 1)
    mb = (lane == h).astype(f32)
    ma = (lane == HEADS + h).astype(f32)
    beta_raw = jnp.sum(ba * mb, axis=-1, keepdims=True)
    alpha = jnp.sum(ba * ma, axis=-1, keepdims=True)
    al = jnp.sum(alog * mb, axis=-1, keepdims=True)
    db = jnp.sum(dtb * mb, axis=-1, keepdims=True)
    beta = _sigmoid(beta_raw)
    g = -jnp.exp(al) * _softplus(alpha + db)
    return q, k, v, jnp.broadcast_to(beta, q.shape), jnp.broadcast_to(g, q.shape)


def _gdn_pre_fwd(p, conv_w, alog, dtb):
    t = p.shape[0]
    hd = HEAD_DIM

    def body(pq_ref, pk_ref, pv_ref, ba_ref, wq_ref, wk_ref, wv_ref, al_ref, dt_ref, q_ref, k_ref, v_ref, b_ref, g_ref):
        h = pl.program_id(0)
        cq = _conv_fwd(pq_ref[...], wq_ref, GDN_CONV)
        ck = _conv_fwd(pk_ref[...], wk_ref, GDN_CONV)
        cv = _conv_fwd(pv_ref[...], wv_ref, GDN_CONV)
        q, k, v, be, ge = _gdn_pre_fn(cq, ck, cv, ba_ref[...], al_ref[...], dt_ref[...], h)
        q_ref[...], k_ref[...], v_ref[...], b_ref[...], g_ref[...] = q, k, v, be, ge

    def pcol(off):
        return pl.BlockSpec((t, hd), lambda h: (0, h + off // hd))

    def wcol(off):
        return pl.BlockSpec((GDN_CONV, hd), lambda h: (0, h + off // hd))

    vec = pl.BlockSpec((1, hd), lambda h: (0, 0))
    out = pl.BlockSpec((t, hd), lambda h: (0, h))
    return pl.pallas_call(
        body, name="gdn_pre_fwd", grid=(HEADS,),
        in_specs=[pcol(OFF_Q), pcol(OFF_K), pcol(OFF_V), pl.BlockSpec((t, hd), lambda h: (0, OFF_BA // hd)),
                  wcol(0), wcol(GDN_WIDTH), wcol(2 * GDN_WIDTH), vec, vec],
        out_specs=[out] * 5, out_shape=[_sds((t, GDN_WIDTH), f32)] * 5,
        compiler_params=_cparams(("parallel",)))(p, p, p, p, conv_w, conv_w, conv_w, alog, dtb)


def _gdn_pre_bwd(p, conv_w, alog, dtb, dq, dk, dv, dbe, dge):
    t = p.shape[0]
    hd = HEAD_DIM

    def body(pq_ref, pk_ref, pv_ref, ba_ref, wq_ref, wk_ref, wv_ref, al_ref, dt_ref,
             dq_ref, dk_ref, dv_ref, dbe_ref, dge_ref,
             opq_ref, opk_ref, opv_ref, dba_ref, dwq_ref, dwk_ref, dwv_ref, dal_ref, ddt_ref):
        h = pl.program_id(0)
        pq, pk, pv = pq_ref[...], pk_ref[...], pv_ref[...]
        cq = _conv_fwd(pq, wq_ref, GDN_CONV)
        ck = _conv_fwd(pk, wk_ref, GDN_CONV)
        cv = _conv_fwd(pv, wv_ref, GDN_CONV)
        _, vjp = jax.vjp(functools.partial(_gdn_pre_fn, h=h), cq, ck, cv, ba_ref[...], al_ref[...], dt_ref[...])
        dcq, dck, dcv, dba, dal, ddt = vjp((dq_ref[...], dk_ref[...], dv_ref[...], dbe_ref[...], dge_ref[...]))
        opq_ref[...] = _conv_bwd(pq, dcq, wq_ref, dwq_ref, GDN_CONV).astype(bf16)
        opk_ref[...] = _conv_bwd(pk, dck, wk_ref, dwk_ref, GDN_CONV).astype(bf16)
        opv_ref[...] = _conv_bwd(pv, dcv, wv_ref, dwv_ref, GDN_CONV).astype(bf16)

        @pl.when(h == 0)
        def _():
            dba_ref[...] = jnp.zeros_like(dba_ref)
            dal_ref[...] = jnp.zeros_like(dal_ref)
            ddt_ref[...] = jnp.zeros_like(ddt_ref)

        dba_ref[...] += dba
        dal_ref[...] += dal
        ddt_ref[...] += ddt

    def pcol(off):
        return pl.BlockSpec((t, hd), lambda h: (0, h + off // hd))

    def wcol(off):
        return pl.BlockSpec((GDN_CONV, hd), lambda h: (0, h + off // hd))

    vec = pl.BlockSpec((1, hd), lambda h: (0, 0))
    col = pl.BlockSpec((t, hd), lambda h: (0, h))
    full = pl.BlockSpec((t, hd), lambda h: (0, 0))
    wout = pl.BlockSpec((GDN_CONV, hd), lambda h: (0, h))
    return pl.pallas_call(
        body, name="gdn_pre_bwd", grid=(HEADS,),
        in_specs=[pcol(OFF_Q), pcol(OFF_K), pcol(OFF_V), pl.BlockSpec((t, hd), lambda h: (0, OFF_BA // hd)),
                  wcol(0), wcol(GDN_WIDTH), wcol(2 * GDN_WIDTH), vec, vec, col, col, col, col, col],
        out_specs=[col, col, col, full, wout, wout, wout, vec, vec],
        out_shape=[_sds((t, GDN_WIDTH), bf16)] * 3 + [_sds((t, hd), f32)] + [_sds((GDN_CONV, GDN_WIDTH), f32)] * 3
        + [_sds((1, hd), f32)] * 2,
        compiler_params=_cparams(("arbitrary",)))(p, p, p, p, conv_w, conv_w, conv_w, alog, dtb, dq, dk, dv, dbe, dge)


BNN = (((2,), (1,)), ((0,), (0,)))
BNT = (((2,), (2,)), ((0,), (0,)))
BTN = (((1,), (1,)), ((0,), (0,)))


def _hdot(a, b, dn=BNN, precision=None):
    return lax.dot_general(a, b, dn, precision=precision, preferred_element_type=f32)


def _hbdot(a, b, dn=BNN):
    return _hdot(a.astype(bf16), b.astype(bf16), dn)


def _tri_inverse(a):
    c = a.shape[-1]
    r = lax.broadcasted_iota(jnp.int32, (c, c), 0)
    col = lax.broadcasted_iota(jnp.int32, (c, c), 1)
    m = -a
    inv = jnp.where(r == col, 1.0, 0.0) + m
    s = 2
    while s < c:
        m = _hdot(m, m, precision=HI)
        inv = inv + _hdot(inv, m, precision=HI)
        s *= 2
    return inv


@jax.custom_vjp
def _saved_inverse(a, inv):
    return inv


def _saved_inverse_fwd(a, inv):
    return inv, inv


def _saved_inverse_bwd(inv, dinv):
    return -_hdot(_hdot(inv, dinv, BTN, precision=HI), inv, BNT, precision=HI), jnp.zeros_like(inv)


_saved_inverse.defvjp(_saved_inverse_fwd, _saved_inverse_bwd)


def _gdn_chunk(s, q, k, v, ge, be, tinv=None):
    nh, c, _ = q.shape
    r = lax.broadcasted_iota(jnp.int32, (c, c), 0)
    col = lax.broadcasted_iota(jnp.int32, (c, c), 1)
    causal = r >= col
    tri = jnp.broadcast_to(causal.astype(f32), (nh, c, c))
    gc = _hdot(tri, ge, precision=HI)
    gcc = gc[:, :, :c]
    gcr = jnp.swapaxes(gc, 1, 2)[:, :c, :]
    decay = jnp.where(causal, jnp.exp(jnp.where(causal, gcc - gcr, 0.0)), 0.0)
    kb = k * be
    lower = jnp.where(r > col, _hbdot(kb, k, BNT) * decay, 0.0)
    tinv = _tri_inverse(lower) if tinv is None else _saved_inverse(lower, tinv)
    egc = jnp.exp(gc)
    u = _hdot(tinv, v * be, precision=HI)
    w = _hdot(tinv, kb * egc, precision=HI)
    attn = _hbdot(q, k, BNT) * decay
    gl = gc[:, c - 1:c, :]
    v_new = u - _hbdot(w, s)
    o = _hbdot(q * egc, s) + _hbdot(attn, v_new)
    s_new = s * jnp.exp(gl) + _hbdot(k * jnp.exp(gl - gc), v_new, BTN)
    return o, s_new, tinv


def _heads_major(ref):
    return jnp.stack([ref[:, h * HEAD_DIM:(h + 1) * HEAD_DIM] for h in range(HEADS)])


def _gdn_core_fwd(q, k, v, ge, be):
    t = q.shape[0]
    c, hd = GDN_CHUNK, HEAD_DIM
    n = t // c

    def body(q_ref, k_ref, v_ref, g_ref, b_ref, o_ref, st_ref, ti_ref, s_ref):
        @pl.when(pl.program_id(0) == 0)
        def _():
            s_ref[...] = jnp.zeros_like(s_ref)

        s = s_ref[...]
        st_ref[:, 0] = s
        o, s_new, tinv = _gdn_chunk(s, *[_heads_major(r) for r in (q_ref, k_ref, v_ref, g_ref, b_ref)])
        ti_ref[0] = tinv
        for h in range(HEADS):
            o_ref[:, h * hd:(h + 1) * hd] = o[h]
        s_ref[...] = s_new

    tile = pl.BlockSpec((c, GDN_WIDTH), lambda i: (i, 0))
    return pl.pallas_call(
        body, name="gdn_core_fwd", grid=(n,), in_specs=[tile] * 5,
        out_specs=[tile, pl.BlockSpec((HEADS, 1, hd, hd), lambda i: (0, i, 0, 0)),
                   pl.BlockSpec((1, HEADS, c, c), lambda i: (i, 0, 0, 0))],
        out_shape=[_sds((t, GDN_WIDTH), f32), _sds((HEADS, n, hd, hd), f32), _sds((n, HEADS, c, c), f32)],
        scratch_shapes=[pltpu.VMEM((HEADS, hd, hd), f32)],
        compiler_params=_cparams(("arbitrary",)))(q, k, v, ge, be)


def _gdn_core_bwd(q, k, v, ge, be, states, do):
    t = q.shape[0]
    c, hd = GDN_CHUNK, HEAD_DIM
    n = t // c
    states, tinvs = states

    def body(q_ref, k_ref, v_ref, g_ref, b_ref, st_ref, ti_ref, do_ref, dq_ref, dk_ref, dv_ref, dg_ref, db_ref, ds_ref):
        @pl.when(pl.program_id(0) == 0)
        def _():
            ds_ref[...] = jnp.zeros_like(ds_ref)

        tinv = ti_ref[0]
        _, vjp = jax.vjp(lambda *a: _gdn_chunk(*a, tinv=tinv)[:2], st_ref[:, 0],
                         *[_heads_major(r) for r in (q_ref, k_ref, v_ref, g_ref, b_ref)])
        ds, *dins = vjp((_heads_major(do_ref), ds_ref[...]))
        ds_ref[...] = ds
        for d_ref, d in zip((dq_ref, dk_ref, dv_ref, dg_ref, db_ref), dins):
            for h in range(HEADS):
                d_ref[:, h * hd:(h + 1) * hd] = d[h]

    tile = pl.BlockSpec((c, GDN_WIDTH), lambda i: (n - 1 - i, 0))
    return pl.pallas_call(
        body, name="gdn_core_bwd", grid=(n,),
        in_specs=[tile] * 5 + [pl.BlockSpec((HEADS, 1, hd, hd), lambda i: (0, n - 1 - i, 0, 0)),
                               pl.BlockSpec((1, HEADS, c, c), lambda i: (n - 1 - i, 0, 0, 0)), tile],
        out_specs=[tile] * 5, out_shape=[_sds((t, GDN_WIDTH), f32)] * 5,
        scratch_shapes=[pltpu.VMEM((HEADS, hd, hd), f32)],
        compiler_params=_cparams(("arbitrary",)))(q, k, v, ge, be, states, tinvs, do)


def _post_fn(o, z, gain):
    return _rms(o, gain) * _silu(z)


def _post_fwd(o, p, z_off, gain, name):
    t = o.shape[0]
    hd = HEAD_DIM

    def body(o_ref, z_ref, g_ref, y_ref):
        y_ref[...] = _post_fn(o_ref[...], z_ref[...], g_ref[...]).astype(bf16)

    col = pl.BlockSpec((t, hd), lambda h: (0, h))
    return pl.pallas_call(
        body, name=name, grid=(HEADS,),
        in_specs=[col, pl.BlockSpec((t, hd), lambda h: (0, h + z_off // hd)), pl.BlockSpec((1, hd), lambda h: (0, 0))],
        out_specs=col, out_shape=_sds((t, HEADS * hd), bf16), compiler_params=_cparams(("parallel",)))(o, p, gain)


def _post_bwd(o, p, z_off, gain, dmix, mix_off, name):
    t = o.shape[0]
    hd = HEAD_DIM

    def body(o_ref, z_ref, g_ref, dy_ref, do_ref, dz_ref, dg_ref):
        _, vjp = jax.vjp(_post_fn, o_ref[...], z_ref[...], g_ref[...])
        do, dz, dg = vjp(dy_ref[...])
        do_ref[...] = do
        dz_ref[...] = dz.astype(bf16)

        @pl.when(pl.program_id(0) == 0)
        def _():
            dg_ref[...] = jnp.zeros_like(dg_ref)

        dg_ref[...] += dg

    col = pl.BlockSpec((t, hd), lambda h: (0, h))
    vec = pl.BlockSpec((1, hd), lambda h: (0, 0))
    return pl.pallas_call(
        body, name=name, grid=(HEADS,),
        in_specs=[col, pl.BlockSpec((t, hd), lambda h: (0, h + z_off // hd)), vec,
                  pl.BlockSpec((t, hd), lambda h: (0, h + mix_off // hd))],
        out_specs=[col, col, vec], out_shape=[_sds((t, HEADS * hd), f32), _sds((t, HEADS * hd), bf16), _sds((1, hd), f32)],
        compiler_params=_cparams(("arbitrary",)))(o, p, gain, dmix)


def _hgrn_pre_fn(qb, fb, lbw, layer):
    l0, l1 = lbw[0:1, :], lbw[1:2, :]
    m = jnp.maximum(l0, l1)
    e0, e1 = jnp.exp(l0 - m), jnp.exp(l1 - m)
    p0, p1 = e0 / (e0 + e1), e1 / (e0 + e1)
    lb = (p0 - p0) if layer == 0 else ((p0 + p1) - p0)
    f = lb + (1.0 - lb) * _sigmoid(fb)
    return _silu(qb), 1.0 - f, jnp.log(jnp.maximum(f, F_FLOOR))


def _hgrn_pre_fwd(p, lbw, layer):
    t = p.shape[0]
    tc = HEAD_DIM

    def body(qb_ref, fb_ref, lb_ref, q_ref, k_ref, lf_ref):
        q_ref[...], k_ref[...], lf_ref[...] = _hgrn_pre_fn(qb_ref[...], fb_ref[...], lb_ref[...], layer)

    col = pl.BlockSpec((t, tc), lambda j: (0, j))
    return pl.pallas_call(
        body, name="hgrn_pre_fwd", grid=(GDN_WIDTH // tc,),
        in_specs=[pl.BlockSpec((t, tc), lambda j: (0, j + OFF_QB // tc)), pl.BlockSpec((t, tc), lambda j: (0, j + OFF_FB // tc)),
                  pl.BlockSpec((2, tc), lambda j: (0, j))],
        out_specs=[col] * 3, out_shape=[_sds((t, GDN_WIDTH), f32)] * 3,
        compiler_params=_cparams(("parallel",)))(p, p, lbw)


def _hgrn_pre_bwd(p, lbw, layer, dq, dk, dlf):
    t = p.shape[0]
    tc = HEAD_DIM

    def body(qb_ref, fb_ref, lb_ref, dq_ref, dk_ref, dlf_ref, dqb_ref, dfb_ref, dlb_ref):
        _, vjp = jax.vjp(functools.partial(_hgrn_pre_fn, layer=layer), qb_ref[...], fb_ref[...], lb_ref[...])
        dqb, dfb, dlb = vjp((dq_ref[...], dk_ref[...], dlf_ref[...]))
        dqb_ref[...] = dqb.astype(bf16)
        dfb_ref[...] = dfb.astype(bf16)
        dlb_ref[...] = dlb

    col = pl.BlockSpec((t, tc), lambda j: (0, j))
    lb = pl.BlockSpec((2, tc), lambda j: (0, j))
    return pl.pallas_call(
        body, name="hgrn_pre_bwd", grid=(GDN_WIDTH // tc,),
        in_specs=[pl.BlockSpec((t, tc), lambda j: (0, j + OFF_QB // tc)), pl.BlockSpec((t, tc), lambda j: (0, j + OFF_FB // tc)),
                  lb, col, col, col],
        out_specs=[col, col, lb], out_shape=[_sds((t, GDN_WIDTH), bf16)] * 2 + [_sds((2, GDN_WIDTH), f32)],
        compiler_params=_cparams(("parallel",)))(p, p, lbw, dq, dk, dlf)


def _hgrn_step(st, q, k, lf, v):
    c = HGRN_CHUNK
    nh = q.shape[0]
    r2 = lax.broadcasted_iota(jnp.int32, (c, c), 0)
    c2 = lax.broadcasted_iota(jnp.int32, (c, c), 1)
    tri = jnp.broadcast_to((r2 >= c2).astype(f32), (nh, c, c))
    i3 = lax.broadcasted_iota(jnp.int32, (c, c, HEAD_DIM), 0)
    j3 = lax.broadcasted_iota(jnp.int32, (c, c, HEAD_DIM), 1)
    mask = i3 >= j3
    outs = []
    for n in range(q.shape[1] // c):
        sl = slice(n * c, (n + 1) * c)
        qc, kc, lc, vc = q[:, sl], k[:, sl], lf[:, sl], v[:, sl]
        b = _hdot(tri, lc, precision=HI)
        rel = jnp.where(mask, jnp.exp(jnp.where(mask, b[:, :, None, :] - b[:, None, :, :], 0.0)), 0.0)
        scores = jnp.sum(qc[:, :, None, :] * kc[:, None, :, :] * rel, axis=-1)
        bl = b[:, c - 1:c, :]
        o = _hbdot(scores, vc) + _hbdot(qc * jnp.exp(b), st, BNT)
        st = st * jnp.exp(bl) + _hbdot(vc, kc * jnp.exp(bl - b), BTN)
        outs.append(o)
    return jnp.concatenate(outs, axis=1), st


def _hgrn_core_fwd(q, k, lf, p):
    t = q.shape[0]
    hd = HEAD_DIM
    rs = min(HGRN_STEP, t)
    n = t // rs

    def body(q_ref, k_ref, lf_ref, v_ref, o_ref, st_ref, s_ref):
        @pl.when(pl.program_id(0) == 0)
        def _():
            s_ref[...] = jnp.zeros_like(s_ref)

        s = s_ref[...]
        st_ref[:, 0] = s
        o, s_new = _hgrn_step(s, *[_heads_major(r) for r in (q_ref, k_ref, lf_ref, v_ref)])
        for h in range(HEADS):
            o_ref[:, h * hd:(h + 1) * hd] = o[h]
        s_ref[...] = s_new

    tile = pl.BlockSpec((rs, GDN_WIDTH), lambda i: (i, 0))
    return pl.pallas_call(
        body, name="hgrn_core_fwd", grid=(n,),
        in_specs=[tile, tile, tile, pl.BlockSpec((rs, GDN_WIDTH), lambda i: (i, OFF_IB // GDN_WIDTH))],
        out_specs=[tile, pl.BlockSpec((HEADS, 1, hd, hd), lambda i: (0, i, 0, 0))],
        out_shape=[_sds((t, GDN_WIDTH), f32), _sds((HEADS, n, hd, hd), f32)],
        scratch_shapes=[pltpu.VMEM((HEADS, hd, hd), f32)],
        compiler_params=_cparams(("arbitrary",)))(q, k, lf, p)


def _hgrn_core_bwd(q, k, lf, p, states, do):
    t = q.shape[0]
    hd = HEAD_DIM
    rs = min(HGRN_STEP, t)
    n = t // rs

    def body(q_ref, k_ref, lf_ref, v_ref, st_ref, do_ref, dq_ref, dk_ref, dlf_ref, dv_ref, ds_ref):
        @pl.when(pl.program_id(0) == 0)
        def _():
            ds_ref[...] = jnp.zeros_like(ds_ref)

        _, vjp = jax.vjp(_hgrn_step, st_ref[:, 0], *[_heads_major(r) for r in (q_ref, k_ref, lf_ref, v_ref)])
        ds, *dins = vjp((_heads_major(do_ref), ds_ref[...]))
        ds_ref[...] = ds
        for d_ref, d in zip((dq_ref, dk_ref, dlf_ref, dv_ref), dins):
            for h in range(HEADS):
                d_ref[:, h * hd:(h + 1) * hd] = d[h].astype(d_ref.dtype)

    tile = pl.BlockSpec((rs, GDN_WIDTH), lambda i: (n - 1 - i, 0))
    return pl.pallas_call(
        body, name="hgrn_core_bwd", grid=(n,),
        in_specs=[tile, tile, tile, pl.BlockSpec((rs, GDN_WIDTH), lambda i: (n - 1 - i, OFF_IB // GDN_WIDTH)),
                  pl.BlockSpec((HEADS, 1, hd, hd), lambda i: (0, n - 1 - i, 0, 0)), tile],
        out_specs=[tile] * 4, out_shape=[_sds((t, GDN_WIDTH), f32)] * 3 + [_sds((t, GDN_WIDTH), bf16)],
        scratch_shapes=[pltpu.VMEM((HEADS, hd, hd), f32)],
        compiler_params=_cparams(("arbitrary",)))(q, k, lf, p, states, do)


def _row(v):
    return v.reshape(1, -1)


def _anchored(w, row, key):
    tok = w.get(key)
    return row if tok is None else row + tok[0, 0]


def _pad_lanes(v, n=HEAD_DIM):
    return jnp.pad(v.reshape(1, -1), ((0, 0), (0, n - v.shape[-1])))


def _ffn_fwd(x, w, l):
    h = _rms_fwd(x, _row(w['norm_ffn'][l]), "ffn_norm")
    u = _mm(h, w['ffn_w_up'][l], tb=True, name="ffn_up")
    a = _ffn_act_fwd(u, w['ffn_conv_w'][l], _row(w['ffn_conv_b'][l]))
    y = _mm(a, w['ffn_w_down'][l], add=x, name="ffn_down")
    return y, (x, h, u)


def _ffn_bwd(saved, w, l, dy, dyb, grads):
    x, h, u = saved
    da = _mm(dyb, w['ffn_w_down'][l], tb=True, name="ffn_down_dx")
    a, dg, dv, dcw, dcb = _ffn_act_bwd(u, w['ffn_conv_w'][l], _anchored(w, _row(w['ffn_conv_b'][l]), ('bwd', l)), da)
    grads['ffn_w_down'][l] = _mm(a, dyb, ta=True, out_dtype=bf16, name="ffn_down_dw")
    du = jnp.concatenate([dg, dv], axis=1)
    grads['ffn_w_up'][l] = _mm(du, h, ta=True, out_dtype=bf16, name="ffn_up_dw")
    dh = _mm(du, w['ffn_w_up'][l], name="ffn_up_dx")
    dx, dxb, dgain = _rms_bwd(x, _row(w['norm_ffn'][l]), dh, dy, "ffn_norm_bwd")
    grads['ffn_conv_w'][l] = dcw
    grads['ffn_conv_b'][l] = dcb[0]
    grads['norm_ffn'][l] = dgain[0]
    return dx, dxb


def _odd_fwd(x, w, l, j):
    h = _rms_fwd(x, _anchored(w, _row(w['norm_mix'][l]), ('fwd', l)), "mix_norm")
    p = _mm(h, w['c_w_in'][j], name="lru_in")
    xc = _col_conv_fwd(p, LRU_WIDTH, w['c_conv_w'][j], _row(w['c_conv_b'][j]), LRU_CONV, 256, "lru_conv_fwd")
    out, hs, a = _lru_fwd(p, xc, w['c_gate_a_w'][j], _row(w['c_gate_a_b'][j]), w['c_gate_x_w'][j],
                          _row(w['c_gate_x_b'][j]), _row(w['c_lambda'][j]))
    y = _mm(out, w['c_w_out'][j], add=x, name="lru_out")
    return y, (x, h, p, xc, out, hs, a)


def _odd_bwd(saved, w, l, j, dy, dyb, grads):
    x, h, p, xc, out, hs, a = saved
    dout = _mm(dyb, w['c_w_out'][j], tb=True, name="lru_out_dx")
    grads['c_w_out'][j] = _mm(out, dyb, ta=True, out_dtype=bf16, name="lru_out_dw")
    dyb_, da, du = _lru_bwd_scan(p, a, hs, dout)
    dxc, dwa, dwx, dba, dbx, dlam = _lru_bwd_gates(xc, da, du, w['c_gate_a_w'][j], _row(w['c_gate_a_b'][j]),
                                                   w['c_gate_x_w'][j], _row(w['c_gate_x_b'][j]), _row(w['c_lambda'][j]))
    dxb_, dcw, dcb = _col_conv_bwd(p, LRU_WIDTH, w['c_conv_w'][j], dxc, LRU_CONV, 256, "lru_conv_bwd")
    dp = jnp.concatenate([dyb_, dxb_], axis=1)
    grads['c_w_in'][j] = _mm(h, dp, ta=True, out_dtype=bf16, name="lru_in_dw")
    dh = _mm(dp, w['c_w_in'][j], tb=True, name="lru_in_dx")
    dx, dxb, dgain = _rms_bwd(x, _row(w['norm_mix'][l]), dh, dy, "mix_norm_bwd")
    grads['c_gate_a_w'][j], grads['c_gate_x_w'][j] = dwa, dwx
    grads['c_gate_a_b'][j], grads['c_gate_x_b'][j], grads['c_lambda'][j] = dba[0], dbx[0], dlam[0]
    grads['c_conv_w'][j], grads['c_conv_b'][j] = dcw, dcb[0]
    grads['norm_mix'][l] = dgain[0]
    return dx, dxb


def _even_fwd(x, w, l, j):
    h = _rms_fwd(x, _anchored(w, _row(w['norm_mix'][l]), ('fwd', l)), "mix_norm")
    p = _mm(h, w['ab_w_in'][j], name="ab_in")
    alog, dtb = _pad_lanes(w['gdn_a_log'][j]), _pad_lanes(w['gdn_dt_bias'][j])
    q, k, v, be, ge = _gdn_pre_fwd(p, w['gdn_conv_w'][j], alog, dtb)
    oa, *sa = _gdn_core_fwd(q, k, v, ge, be)
    ya = _post_fwd(oa, p, OFF_Z, _row(w['gdn_norm'][j]), "gdn_post_fwd")
    qq, kk, lf = _hgrn_pre_fwd(p, w['hgrn_lower_bounds'], j)
    ob, sb = _hgrn_core_fwd(qq, kk, lf, p)
    yb = _post_fwd(ob, p, OFF_GB, _row(w['hgrn_norm'][j]), "hgrn_post_fwd")
    mix = jnp.concatenate([ya, yb], axis=1)
    y = _mm(mix, w['ab_w_out'][j], add=x, name="ab_out")
    return y, (x, h, p, q, k, v, be, ge, oa, sa, qq, kk, lf, ob, sb, mix)


def _even_bwd(saved, w, l, j, dy, dyb, grads):
    x, h, p, q, k, v, be, ge, oa, sa, qq, kk, lf, ob, sb, mix = saved
    alog, dtb = _pad_lanes(w['gdn_a_log'][j]), _pad_lanes(w['gdn_dt_bias'][j])
    dmix = _mm(dyb, w['ab_w_out'][j], tb=True, name="ab_out_dx")
    grads['ab_w_out'][j] = _mm(mix, dyb, ta=True, out_dtype=bf16, name="ab_out_dw")
    doa, dz, dgn = _post_bwd(oa, p, OFF_Z, _row(w['gdn_norm'][j]), dmix, 0, "gdn_post_bwd")
    dob, dgb, dhn = _post_bwd(ob, p, OFF_GB, _row(w['hgrn_norm'][j]), dmix, GDN_WIDTH, "hgrn_post_bwd")
    dq, dk, dv, dge, dbe = _gdn_core_bwd(q, k, v, ge, be, sa, doa)
    dpq, dpk, dpv, dba, dwq, dwk, dwv, dal, ddt = _gdn_pre_bwd(p, w['gdn_conv_w'][j], alog, dtb, dq, dk, dv, dbe, dge)
    dqq, dkk, dlf, dib = _hgrn_core_bwd(qq, kk, lf, p, sb, dob)
    dqb, dfb, dlb = _hgrn_pre_bwd(p, w['hgrn_lower_bounds'], j, dqq, dkk, dlf)
    dp = jnp.concatenate([dpq, dpk, dpv, dz, dqb, dfb, dib, dgb, dba.astype(bf16)], axis=1)
    grads['ab_w_in'][j] = _mm(h, dp, ta=True, out_dtype=bf16, name="ab_in_dw")
    dh = _mm(dp, w['ab_w_in'][j], tb=True, name="ab_in_dx")
    dx, dxb, dgain = _rms_bwd(x, _row(w['norm_mix'][l]), dh, dy, "mix_norm_bwd")
    grads['gdn_conv_w'][j] = jnp.concatenate([dwq, dwk, dwv], axis=1)
    grads['gdn_a_log'][j], grads['gdn_dt_bias'][j] = dal[0, :HEADS], ddt[0, :HEADS]
    grads['gdn_norm'][j], grads['hgrn_norm'][j] = dgn[0], dhn[0]
    grads['hgrn_lower_bounds'].append(dlb)
    grads['norm_mix'][l] = dgain[0]
    return dx, dxb


def _ab_permute(w_in):
    pad = jnp.zeros(w_in.shape[:-1] + (AB_PAD - AB_COLS,), w_in.dtype)
    return jnp.concatenate([w_in[..., :2048], w_in[..., 2056:], w_in[..., 2048:2056], pad], axis=-1)


def _ab_unpermute(g):
    return jnp.concatenate([g[..., :2048], g[..., 4096:4104], g[..., 2048:4096]], axis=-1)


def _block_pad(a, axis, nblk, padded):
    axis = axis % a.ndim
    s = a.shape
    a = a.reshape(s[:axis] + (nblk, s[axis] // nblk) + s[axis + 1:])
    pad = [(0, 0)] * a.ndim
    pad[axis + 1] = (0, padded - s[axis] // nblk)
    return jnp.pad(a, pad).reshape(s[:axis] + (nblk * padded,) + s[axis + 1:])


def _block_unpad(a, axis, nblk, width):
    axis = axis % a.ndim
    s = a.shape
    a = a.reshape(s[:axis] + (nblk, s[axis] // nblk) + s[axis + 1:])
    a = lax.slice_in_dim(a, 0, width, axis=axis + 1)
    return a.reshape(s[:axis] + (nblk * width,) + s[axis + 1:])


def _kernel_layout(w):
    w = dict(w)
    w['ab_w_in'] = _ab_permute(w['ab_w_in'])
    w['ffn_w_up'] = jnp.swapaxes(_block_pad(w['ffn_w_up'], 2, N_DEV, FF_PAD), 1, 2)
    w['ffn_w_down'] = _block_pad(w['ffn_w_down'], 1, 4, FF_PAD)
    w['ffn_conv_w'] = _block_pad(w['ffn_conv_w'], 2, 4, FF_PAD)
    w['ffn_conv_b'] = _block_pad(w['ffn_conv_b'], 1, 4, FF_PAD)
    return w


def _natural_grads(g):
    g = dict(g)
    g['ab_w_in'] = _ab_unpermute(g['ab_w_in'])
    g['ffn_w_up'] = _block_unpad(jnp.swapaxes(g['ffn_w_up'], 1, 2), 2, N_DEV, FF_SHARD)
    g['ffn_w_down'] = _block_unpad(g['ffn_w_down'], 1, 4, FF_SHARD)
    g['ffn_conv_w'] = _block_unpad(g['ffn_conv_w'], 2, 4, FF_SHARD)
    g['ffn_conv_b'] = _block_unpad(g['ffn_conv_b'], 1, 4, FF_SHARD)
    return g


def _local_step(x, target, w, fetch=None, push=None):
    grads = {n: [None] * (DEPTH if n in ('norm_mix', 'norm_ffn') or n.startswith('ffn_') else 2)
             for n in WEIGHTS if n not in ('norm_final', 'hgrn_lower_bounds')}
    grads['hgrn_lower_bounds'] = []
    saved = []
    for l in range(DEPTH):
        j = l // 2
        if fetch is not None:
            fetch(l, x)
        x, s_mix = (_even_fwd if l % 2 == 0 else _odd_fwd)(x, w, l, j)
        x, s_ffn = _ffn_fwd(x, w, l)
        saved.append((s_mix, s_ffn))
    loss, dx, dxb, dgf = _loss_head(x, _row(w['norm_final']), target)
    for l in reversed(range(DEPTH)):
        j = l // 2
        s_mix, s_ffn = saved[l]
        dx, dxb = _ffn_bwd(s_ffn, w, l, dx, dxb, grads)
        dx, dxb = (_even_bwd if l % 2 == 0 else _odd_bwd)(s_mix, w, l, j, dx, dxb, grads)
        if push is not None:
            push(l, {nm: grads[nm].pop(li) for nm, li in reversed(_layer_items(l))})
    out = {n: jnp.stack(g) for n, g in grads.items() if n != 'hgrn_lower_bounds' and g}
    out['hgrn_lower_bounds'] = grads['hgrn_lower_bounds'][0] + grads['hgrn_lower_bounds'][1]
    out['norm_final'] = dgf[0]
    return loss[0, 0], dx, out


def _position():
    return lax.axis_index("x"), lax.axis_index("y"), lax.axis_index("c")


BLOCK_LAYOUT = {
    'ab_w_in': ((2, D_MODEL, N_DEV * AB_SHARD_PAD), (2, D_MODEL, AB_SHARD_PAD)),
    'ab_w_out': ((2, N_DEV, 128, D_MODEL), (2, 128, D_MODEL)),
    'c_w_in': ((2, D_MODEL, 2 * LRU_WIDTH), (2, D_MODEL, 256)),
    'c_w_out': ((2, N_DEV, 128, D_MODEL), (2, 128, D_MODEL)),
    'c_gate_a_w': ((2, HEADS, N_DEV, 32, LRU_BLOCK), (2, HEADS, 32, LRU_BLOCK)),
    'c_gate_x_w': ((2, HEADS, N_DEV, 32, LRU_BLOCK), (2, HEADS, 32, LRU_BLOCK)),
    'ffn_w_up': ((DEPTH, N_DEV, FF_PAD, D_MODEL), (DEPTH, FF_PAD, D_MODEL)),
    'ffn_w_down': ((DEPTH, 4, FF_PAD, D_MODEL), (DEPTH, FF_ROWS, D_MODEL)),
}


COL_WINDOW = {'ab_w_in': AB_SHARD_PAD, 'c_w_in': 256}


def _block_index(name, p):
    d = 4 * p[0] + 2 * p[1] + p[2]
    if name in COL_WINDOW:
        return (slice(None), pl.ds(pl.multiple_of(d * COL_WINDOW[name], 128), COL_WINDOW[name]))
    if name == 'ffn_w_down':
        return (2 * p[0] + p[1], pl.ds(pl.multiple_of(p[2] * FF_ROWS, 16), FF_ROWS), slice(None))
    if name in ('c_gate_a_w', 'c_gate_x_w'):
        return (slice(None), d)
    return (d,)


def _block_of(name, ref, p, layered=True):
    idx = _block_index(name, p)
    if layered and name in BLOCK_LAYOUT:
        idx = (slice(None),) + idx
    return ref.at[idx]


def _layer_items(l):
    j = l // 2
    mix = ([('ab_w_in', j), ('ab_w_out', j)] if l % 2 == 0 else
           [('c_w_in', j), ('c_w_out', j), ('c_gate_a_w', j), ('c_gate_x_w', j)])
    return mix + [('ffn_w_up', l), ('ffn_w_down', l)]


def _own_land(name, shard_l, pos):
    x, y, c = pos
    d = 4 * x + 2 * y + c
    shape = BLOCK_LAYOUT[name][0][1:] if name in BLOCK_LAYOUT else (N_DEV,) + shard_l.shape
    zeros = jnp.zeros(shape, shard_l.dtype) if name == 'ffn_w_down' else lax.empty(shape, shard_l.dtype)
    if name in COL_WINDOW:
        return lax.dynamic_update_slice(zeros, shard_l, (0, d * COL_WINDOW[name]))
    if name == 'ffn_w_down':
        return lax.dynamic_update_slice(zeros, shard_l[None], (2 * x + y, c * FF_ROWS, 0))
    if name in ('c_gate_a_w', 'c_gate_x_w'):
        return lax.dynamic_update_slice(zeros, shard_l[:, None], (0, d, 0, 0))
    return lax.dynamic_update_slice(zeros, shard_l[None], (d,) + (0,) * shard_l.ndim)


def _place_own(items, shards, posv, name):
    n = len(items)
    down = [i for i, (nm, _) in enumerate(items) if nm == 'ffn_w_down']
    in_specs, out_specs, out_shapes, operands = [], [], [], []
    for nm, li in items:
        sh = shards[nm]
        shard_shape = sh.shape if li is None else sh.shape[1:]
        z = (0,) * len(shard_shape)
        operands.append(sh)
        in_specs.append(pl.BlockSpec(shard_shape, lambda i, d, q, c, z=z: z) if li is None else
                        pl.BlockSpec((1,) + shard_shape, lambda i, d, q, c, li=li, z=z: (li,) + z))
        out_shapes.append(_sds(BLOCK_LAYOUT[nm][0][1:] if nm in BLOCK_LAYOUT else (N_DEV,) + sh.shape, sh.dtype))
        if nm in COL_WINDOW:
            out_specs.append(pl.BlockSpec(shard_shape, lambda i, d, q, c: (0, d[0])))
        elif nm == 'ffn_w_down':
            out_specs.append(pl.BlockSpec((1,) + shard_shape, lambda i, d, q, c: (q[0], c[0], 0)))
        elif nm in ('c_gate_a_w', 'c_gate_x_w'):
            out_specs.append(pl.BlockSpec((HEADS, 1) + shard_shape[1:], lambda i, d, q, c: (0, d[0], 0, 0)))
        else:
            out_specs.append(pl.BlockSpec((1,) + shard_shape, lambda i, d, q, c, z=z: (d[0],) + z))

    def body(d_ref, q_ref, c_ref, *refs):
        for i, (nm, li) in enumerate(items):
            v = refs[i][...] if li is None else refs[i][0]
            o_ref = refs[n + len(down) + i]
            if nm in COL_WINDOW:
                o_ref[...] = v
            elif nm in ('c_gate_a_w', 'c_gate_x_w'):
                o_ref[:, 0] = v
            else:
                o_ref[0] = v

    zeros = [jnp.zeros(out_shapes[i].shape, out_shapes[i].dtype) for i in down]
    return pl.pallas_call(
        body, name=name, out_shape=out_shapes,
        grid_spec=pltpu.PrefetchScalarGridSpec(
            num_scalar_prefetch=3, grid=(1,), in_specs=in_specs + [pl.BlockSpec(memory_space=pl.ANY)] * len(down),
            out_specs=out_specs),
        input_output_aliases={3 + n + k: i for k, i in enumerate(down)},
        compiler_params=_cparams(("arbitrary",)))(*posv, *operands, *zeros)


def _src_of(shard_ref, li):
    return shard_ref if li is None else shard_ref.at[li]


def _gather_now(items, shards, lands):
    n = len(items)
    srcs = sorted({nm for nm, _ in items})

    def body(*refs):
        ins = dict(zip(srcs, refs[:len(srcs)]))
        outs = refs[len(srcs) + n:len(srcs) + 2 * n]
        send_sems, recv_sems = refs[len(srcs) + 2 * n:]
        x, y, c = _position()
        me, sibling = (x, y, c), (x, y, 1 - c)
        chips = [(1 - x, y), (x, 1 - y), (1 - x, 1 - y)]

        def copy(i, k, block, to, own=False):
            nm, li = items[i]
            dst = _block_of(nm, outs[i], block, layered=False)
            return pltpu.make_async_remote_copy(
                src_ref=_src_of(ins[nm], li) if own else dst, dst_ref=dst, send_sem=send_sems.at[7 * i + k],
                recv_sem=recv_sems.at[7 * i + k], device_id=to, device_id_type=MESH)

        first = []
        for i in range(n):
            first.append(copy(i, 0, me, sibling, own=True))
            first += [copy(i, 1 + j, me, (*chip, c), own=True) for j, chip in enumerate(chips)]
        for cp in first:
            cp.start()
        passed = []
        for j, chip in enumerate(chips):
            for i in range(n):
                copy(i, 1 + j, (*chip, c), me).wait_recv()
                fwd = copy(i, 4 + j, (*chip, c), sibling)
                fwd.start()
                passed.append(fwd)
        for i in range(n):
            copy(i, 0, sibling, me).wait_recv()
        for j, chip in enumerate(chips):
            for i in range(n):
                copy(i, 4 + j, (*chip, 1 - c), me).wait_recv()
        for cp in first + passed:
            cp.wait_send()

    any_spec = pl.BlockSpec(memory_space=pl.ANY)
    return pl.pallas_call(
        body, name="gather_first_layer", out_shape=[_sds(a.shape, a.dtype) for a in lands],
        in_specs=[any_spec] * (len(srcs) + n), out_specs=[any_spec] * n,
        input_output_aliases={len(srcs) + i: i for i in range(n)},
        scratch_shapes=[pltpu.SemaphoreType.DMA((7 * n,)), pltpu.SemaphoreType.DMA((7 * n,))],
    )(*[shards[nm] for nm in srcs], *lands)


FIRST_HOP = (1, 2, 4, 6)


def _lanes(name, land_ref, pos):
    if name == 'ffn_w_down':
        return [(FIRST_HOP, land_ref.at[pl.ds(0, 2), pl.ds(0, 2 * FF_ROWS)])]
    if name in COL_WINDOW:
        return [(FIRST_HOP, land_ref.at[:, pl.ds(0, 4 * COL_WINDOW[name])])]
    if name in ('c_gate_a_w', 'c_gate_x_w'):
        return [(FIRST_HOP, land_ref.at[:, pl.ds(0, 4)])]
    return [(FIRST_HOP, land_ref.at[pl.ds(0, 4)])]


def _n_lanes(items):
    return len(items)


def _gather_forward(items, lands, name):
    n = len(items)

    def body(*refs):
        outs = refs[n:2 * n]
        send_sems, recv_sems = refs[2 * n:]
        x, y, c = _position()
        chips = [(1 - x, y), (x, 1 - y), (1 - x, 1 - y)]
        copies, arrivals = [], []
        for i, (nm, _) in enumerate(items):
            for j, chip in enumerate(chips):
                mine = _block_of(nm, outs[i], (*chip, c), layered=False)
                theirs = _block_of(nm, outs[i], (*chip, 1 - c), layered=False)
                copies.append(pltpu.make_async_remote_copy(
                    src_ref=mine, dst_ref=mine, send_sem=send_sems.at[3 * i + j], recv_sem=recv_sems.at[3 * i + j],
                    device_id=(x, y, 1 - c), device_id_type=MESH))
                arrivals.append(pltpu.make_async_remote_copy(
                    src_ref=theirs, dst_ref=theirs, send_sem=send_sems.at[3 * i + j], recv_sem=recv_sems.at[3 * i + j],
                    device_id=(x, y, 1 - c), device_id_type=MESH))
        for cp in copies:
            cp.start()
        for cp in arrivals:
            cp.wait_recv()
        for cp in copies:
            cp.wait_send()

    any_spec = pl.BlockSpec(memory_space=pl.ANY)
    return pl.pallas_call(
        body, name=name, out_shape=[_sds(a.shape, a.dtype) for a in lands],
        in_specs=[any_spec] * n, out_specs=[any_spec] * n, input_output_aliases={i: i for i in range(n)},
        scratch_shapes=[pltpu.SemaphoreType.DMA((3 * n,)), pltpu.SemaphoreType.DMA((3 * n,))],
    )(*lands)


HBM_SPEC = pl.BlockSpec(memory_space=pltpu.HBM)
SEM_SPEC = pl.BlockSpec(memory_space=pltpu.SEMAPHORE)
SIDE_EFFECT = pltpu.SideEffectType.DATAFLOW_SIDE_EFFECTING


def _gather_start(items, shards, lands, token, name):
    n = len(items)
    srcs = sorted({nm for nm, _ in items})
    ns, nl = len(srcs), _n_lanes(items)

    def body(*refs):
        ins = dict(zip(srcs, refs[:ns]))
        land_refs = refs[ns:ns + n]
        sems = refs[ns + n + 1:ns + n + 1 + 2 * nl]
        x, y, c = _position()
        me = (x, y, c)
        lane = 0
        for i, (nm, li) in enumerate(items):
            for codes, _ in _lanes(nm, land_refs[i], me):
                for k in codes:
                    peer = (1 - x if (k >> 2) & 1 else x, 1 - y if (k >> 1) & 1 else y, 1 - c if k & 1 else c)
                    pltpu.make_async_remote_copy(
                        src_ref=_src_of(ins[nm], li), dst_ref=_block_of(nm, land_refs[i], me, layered=False),
                        send_sem=sems[2 * lane], recv_sem=sems[2 * lane + 1], device_id=peer, device_id_type=MESH).start()
                lane += 1
        refs[-1][...] = jnp.zeros((8, 128), f32)

    hbm = [pltpu.with_memory_space_constraint(a, pltpu.HBM) for a in [shards[nm] for nm in srcs] + list(lands)]
    outs = pl.pallas_call(
        body, name=name,
        out_shape=[pltpu.SemaphoreType.DMA(())] * (2 * nl) + [pltpu.HBM(a.shape, a.dtype) for a in hbm] + [_sds((8, 128), f32)],
        in_specs=[HBM_SPEC] * (ns + n) + [pl.BlockSpec(memory_space=pl.ANY)],
        out_specs=[SEM_SPEC] * (2 * nl) + [HBM_SPEC] * (ns + n) + [pl.BlockSpec(memory_space=pltpu.VMEM)],
        input_output_aliases={i: 2 * nl + i for i in range(ns + n)},
        compiler_params=pltpu.CompilerParams(has_side_effects=SIDE_EFFECT),
    )(*hbm, token)
    return outs[:2 * nl], dict(zip(srcs, outs[2 * nl:2 * nl + ns])), outs[2 * nl + ns:-1], outs[-1]


def _gather_wait(items, sems, shards, lands, after, name):
    n = len(items)
    srcs = sorted(shards)
    ns, nl = len(srcs), _n_lanes(items)

    def body(*refs):
        land_refs = refs[ns:ns + n]
        sem_refs = refs[ns + n:ns + n + 2 * nl]
        x, y, c = _position()
        lane = 0
        for i, (nm, _) in enumerate(items):
            for _, moved in _lanes(nm, land_refs[i], (x, y, c)):
                cp = pltpu.make_async_remote_copy(
                    src_ref=moved, dst_ref=moved, send_sem=sem_refs[2 * lane], recv_sem=sem_refs[2 * lane + 1],
                    device_id=(x, y, 1 - c), device_id_type=MESH)
                cp.wait_send()
                cp.wait_recv()
                lane += 1

    outs = pl.pallas_call(
        body, name=name, out_shape=[pltpu.HBM(shards[nm].shape, shards[nm].dtype) for nm in srcs]
        + [pltpu.HBM(a.shape, a.dtype) for a in lands],
        in_specs=[HBM_SPEC] * (ns + n) + [SEM_SPEC] * (2 * nl) + [pl.BlockSpec(memory_space=pl.ANY)],
        out_specs=[HBM_SPEC] * (ns + n), input_output_aliases={i: i for i in range(ns + n)},
        compiler_params=pltpu.CompilerParams(has_side_effects=SIDE_EFFECT),
    )(*[shards[nm] for nm in srcs], *lands, *sems, after)
    return dict(zip(srcs, outs[:ns])), outs[ns:]


def _exchange_grads(fulls, rep):
    cpos = lax.axis_index("c").astype(jnp.int32).reshape(1)
    pair, rep_pair = _pair_exchange(fulls, rep)
    chip = {nm: _chip_sum(nm, fulls[nm], pair[nm], cpos) for nm in fulls}
    rep_chip = _add_pair(rep, rep_pair, "chip_sum_replicated")
    cross, cross_rep = _cross_exchange(chip, rep_chip)
    return chip, rep_chip, cross, cross_rep


def _pair_exchange(fulls, rep, tag=""):
    names = list(fulls)
    n = len(names)
    shard_shape = {nm: ((fulls[nm].shape[0],) + BLOCK_LAYOUT[nm][1][1:] if nm in BLOCK_LAYOUT else fulls[nm].shape[1:])
                   for nm in names}

    def body(*refs):
        ins = dict(zip(names, refs[:n]))
        rep_ref = refs[n]
        pair = dict(zip(names, refs[n + 1:2 * n + 1]))
        rpair_ref = refs[2 * n + 1]
        send_sems, recv_sems = refs[2 * n + 2:]
        x, y, c = _position()
        sibling = (x, y, 1 - c)
        remote = []
        for i, nm in enumerate(names):
            for q in range(4):
                remote.append(pltpu.make_async_remote_copy(
                    src_ref=_block_of(nm, ins[nm], (q >> 1, q & 1, 1 - c)), dst_ref=pair[nm].at[q],
                    send_sem=send_sems.at[4 * i + q], recv_sem=recv_sems.at[4 * i + q], device_id=sibling,
                    device_id_type=MESH))
        remote.append(pltpu.make_async_remote_copy(
            src_ref=rep_ref, dst_ref=rpair_ref, send_sem=send_sems.at[4 * n], recv_sem=recv_sems.at[4 * n],
            device_id=sibling, device_id_type=MESH))
        for cp in remote:
            cp.start()
        for cp in remote:
            cp.wait_recv()
        for cp in remote:
            cp.wait_send()

    any_spec = pl.BlockSpec(memory_space=pl.ANY)
    four = [_sds((4,) + tuple(shard_shape[nm]), fulls[nm].dtype) for nm in names]
    outs = pl.pallas_call(
        body, name="grad_pair_exchange" + tag, out_shape=four + [_sds(rep.shape, rep.dtype)],
        in_specs=[any_spec] * (n + 1), out_specs=[any_spec] * (n + 1),
        scratch_shapes=[pltpu.SemaphoreType.DMA((4 * n + 1,)), pltpu.SemaphoreType.DMA((4 * n + 1,))],
    )(*[fulls[nm] for nm in names], rep)
    return dict(zip(names, outs[:n])), outs[n]


def _chip_sum(name, full, pair, cpos, tag=""):
    if name in COL_WINDOW:
        width = BLOCK_LAYOUT[name][1][-1]
        rows = full.shape[0] * full.shape[1]
        tr = 512

        def body(c_ref, f_ref, p_ref, o_ref):
            o_ref[0] = (f_ref[...].astype(f32) + p_ref[0].astype(f32)).astype(o_ref.dtype)

        slot = pl.BlockSpec((1, tr, width), lambda q, i, c: (q, i, 0))
        out = pl.pallas_call(
            body, name="chip_sum_" + name + tag, out_shape=_sds((4, rows, width), full.dtype),
            grid_spec=pltpu.PrefetchScalarGridSpec(
                num_scalar_prefetch=1, grid=(4, rows // tr),
                in_specs=[pl.BlockSpec((tr, width), lambda q, i, c: (i, 2 * q + c[0])), slot], out_specs=slot),
            compiler_params=_cparams(("parallel", "parallel")))(
            cpos, full.reshape(rows, N_DEV * width), pair.reshape(4, rows, width))
        return out.reshape(pair.shape)

    if name == 'ffn_w_down':
        f4, p4 = full, pair
        fspec = pl.BlockSpec((full.shape[0], 1, FF_ROWS, D_MODEL), lambda q, c: (0, q, c[0], 0))
    else:
        shard = pair.shape[1:]
        lead = int(np.prod(shard[:-2]))
        f4 = full.reshape((lead, N_DEV) + shard[-2:])
        p4 = pair.reshape((4, lead) + shard[-2:])
        fspec = pl.BlockSpec((lead, 1) + shard[-2:], lambda q, c: (0, 2 * q + c[0], 0, 0))

    def body4(c_ref, f_ref, p_ref, o_ref):
        o_ref[0] = (f_ref[:, 0].astype(f32) + p_ref[0].astype(f32)).astype(o_ref.dtype)

    slot = pl.BlockSpec((1,) + p4.shape[1:], lambda q, c: (q, 0, 0, 0))
    out = pl.pallas_call(
        body4, name="chip_sum_" + name + tag, out_shape=_sds(p4.shape, full.dtype),
        grid_spec=pltpu.PrefetchScalarGridSpec(num_scalar_prefetch=1, grid=(4,), in_specs=[fspec, slot], out_specs=slot),
        compiler_params=_cparams(("parallel",)))(cpos, f4, p4)
    return out.reshape(pair.shape)


def _add_pair(a, b, name):
    shp = a.shape
    r, c = int(np.prod(shp[:-1])), shp[-1]
    tr = _tile(r, (512, 256, 128, 64, 32, 16, 8))

    def body(a_ref, b_ref, o_ref):
        o_ref[...] = (a_ref[...].astype(f32) + b_ref[...].astype(f32)).astype(o_ref.dtype)

    tile = pl.BlockSpec((tr, c), lambda i: (i, 0))
    return pl.pallas_call(body, name=name, grid=(r // tr,), in_specs=[tile, tile], out_specs=tile,
                          out_shape=_sds((r, c), a.dtype), compiler_params=_cparams(("parallel",)))(
        a.reshape(r, c), b.reshape(r, c)).reshape(shp)


def _cross_exchange(chip, rep_chip):
    names = list(chip)
    n = len(names)

    def body(*refs):
        ins = dict(zip(names, refs[:n]))
        rep_ref = refs[n]
        outs = dict(zip(names, refs[2 * n + 2:3 * n + 2]))
        rrep_ref = refs[3 * n + 2]
        send_sems, recv_sems = refs[3 * n + 3:]
        x, y, c = _position()
        mine = 2 * x + y
        copies = []
        for k in range(1, 4):
            px, py = (1 - x if (k >> 1) & 1 else x), (1 - y if k & 1 else y)
            for i, nm in enumerate(names + ['']):
                src = rep_ref if i == n else ins[nm].at[2 * px + py]
                dst = (rrep_ref if i == n else outs[nm]).at[mine]
                copies.append(pltpu.make_async_remote_copy(
                    src_ref=src, dst_ref=dst, send_sem=send_sems.at[3 * i + k - 1], recv_sem=recv_sems.at[3 * i + k - 1],
                    device_id=(px, py, c), device_id_type=MESH))
        for cp in copies:
            cp.start()
        for cp in copies:
            cp.wait_recv()
        for cp in copies:
            cp.wait_send()

    any_spec = pl.BlockSpec(memory_space=pl.ANY)
    shapes = [_sds(chip[nm].shape, chip[nm].dtype) for nm in names] + [_sds((4,) + rep_chip.shape, rep_chip.dtype)]
    zeros = [jnp.zeros(s.shape, s.dtype) for s in shapes]
    outs = pl.pallas_call(
        body, name="grad_cross_exchange", out_shape=shapes,
        in_specs=[any_spec] * (2 * n + 2), out_specs=[any_spec] * (n + 1),
        input_output_aliases={n + 1 + i: i for i in range(n + 1)},
        scratch_shapes=[pltpu.SemaphoreType.DMA((3 * (n + 1),)), pltpu.SemaphoreType.DMA((3 * (n + 1),))],
    )(*[chip[nm] for nm in names], rep_chip, *zeros)
    return dict(zip(names, outs[:n])), outs[n]


def _cross_start(chips, name):
    n = len(chips)

    def body(*refs):
        chip_refs, land_refs = refs[:n], refs[n:2 * n]
        sems = refs[2 * n:4 * n]
        x, y, c = _position()
        mine = 2 * x + y
        for i in range(n):
            for k in range(1, 4):
                px, py = (1 - x if (k >> 1) & 1 else x), (1 - y if k & 1 else y)
                pltpu.make_async_remote_copy(
                    src_ref=chip_refs[i].at[2 * px + py], dst_ref=land_refs[i].at[mine], send_sem=sems[2 * i],
                    recv_sem=sems[2 * i + 1], device_id=(px, py, c), device_id_type=MESH).start()
        refs[-1][...] = jnp.zeros((8, 128), f32)

    hbm = [pltpu.with_memory_space_constraint(a, pltpu.HBM) for a in list(chips) + [jnp.zeros(a.shape, a.dtype) for a in chips]]
    outs = pl.pallas_call(
        body, name=name,
        out_shape=[pltpu.SemaphoreType.DMA(())] * (2 * n) + [pltpu.HBM(a.shape, a.dtype) for a in hbm] + [_sds((8, 128), f32)],
        in_specs=[HBM_SPEC] * (2 * n),
        out_specs=[SEM_SPEC] * (2 * n) + [HBM_SPEC] * (2 * n) + [pl.BlockSpec(memory_space=pltpu.VMEM)],
        input_output_aliases={i: 2 * n + i for i in range(2 * n)},
        compiler_params=pltpu.CompilerParams(has_side_effects=SIDE_EFFECT),
    )(*hbm)
    return outs[:2 * n], outs[2 * n:3 * n], outs[3 * n:4 * n], outs[4 * n]


def _cross_wait(sems, chips, lands, after, name):
    n = len(chips)

    def body(*refs):
        land_refs = refs[n:2 * n]
        sem_refs = refs[2 * n:4 * n]
        x, y, c = _position()
        for i in range(n):
            moved = land_refs[i].at[pl.ds(0, 3)]
            cp = pltpu.make_async_remote_copy(
                src_ref=moved, dst_ref=moved, send_sem=sem_refs[2 * i], recv_sem=sem_refs[2 * i + 1],
                device_id=(x, y, 1 - c), device_id_type=MESH)
            cp.wait_send()
            cp.wait_recv()

    outs = pl.pallas_call(
        body, name=name, out_shape=[pltpu.HBM(a.shape, a.dtype) for a in list(chips) + list(lands)],
        in_specs=[HBM_SPEC] * (2 * n) + [SEM_SPEC] * (2 * n) + [pl.BlockSpec(memory_space=pl.ANY)],
        out_specs=[HBM_SPEC] * (2 * n), input_output_aliases={i: i for i in range(2 * n)},
        compiler_params=pltpu.CompilerParams(has_side_effects=SIDE_EFFECT),
    )(*chips, *lands, *sems, after)
    return outs[:n], outs[n:]


def _sum_adamw_layer(parts, own, mine, w, m, v, li, prev, name):
    nl, r, l = w.shape
    lp = parts.shape[2]
    tr = r if r <= 512 else _tile(r, (512, FF_ROWS, 256, 128))
    c1 = 1.0 / (1.0 - ADAM_B1 ** ADAM_STEP)
    c2 = 1.0 / (1.0 - ADAM_B2 ** ADAM_STEP)
    k = 0 if prev is None else 4

    def body(mine_ref, p_ref, o_ref, w_ref, m_ref, v_ref, *rest):
        g_ref, d_ref, nm_ref, nv_ref = rest[k:]
        mine_v = o_ref[0].astype(f32)
        g = jnp.where(mine_ref[0] == 0, mine_v, p_ref[0].astype(f32))
        for s in range(1, 4):
            g = g + jnp.where(mine_ref[0] == s, mine_v, p_ref[s].astype(f32))
        if lp != l:
            g = g[:, :l]
        m_new = ADAM_B1 * m_ref[0] + (1.0 - ADAM_B1) * g
        v_new = ADAM_B2 * v_ref[0] + (1.0 - ADAM_B2) * (g * g)
        g_ref[0] = g
        nm_ref[0] = m_new
        nv_ref[0] = v_new
        d_ref[0] = -ADAM_LR * ((m_new * c1) / (jnp.sqrt(v_new * c2) + ADAM_EPS) + ADAM_WD * w_ref[0])

    tile = pl.BlockSpec((1, tr, l), lambda i, mn: (li, i, 0))
    keep = [pl.BlockSpec(memory_space=pl.ANY)] * k
    return pl.pallas_call(
        body, name=name, out_shape=[_sds((nl, r, l), f32)] * 4,
        grid_spec=pltpu.PrefetchScalarGridSpec(
            num_scalar_prefetch=1, grid=(r // tr,),
            in_specs=[pl.BlockSpec((4, tr, lp), lambda i, mn: (0, i, 0)), pl.BlockSpec((1, tr, lp), lambda i, mn: (mn[0], i, 0)),
                      tile, tile, tile] + keep,
            out_specs=[tile] * 4),
        input_output_aliases={6 + i: i for i in range(k)},
        compiler_params=_cparams(("parallel",)))(mine, parts, own, w, m, v, *(prev or ()))


def _sum_adamw(parts, own, mine, w, m, v, name):
    r, l = w.shape
    lp = parts.shape[2]
    tr = _tile(r, (256, 128, 64, 32, 16, 8))
    c1 = 1.0 / (1.0 - ADAM_B1 ** ADAM_STEP)
    c2 = 1.0 / (1.0 - ADAM_B2 ** ADAM_STEP)

    def body(mine_ref, p_ref, o_ref, w_ref, m_ref, v_ref, g_ref, d_ref, nm_ref, nv_ref):
        mine_v = (o_ref[0] if own.ndim == 3 else o_ref[...]).astype(f32)
        g = jnp.where(mine_ref[0] == 0, mine_v, p_ref[0].astype(f32))
        for s in range(1, parts.shape[0]):
            g = g + jnp.where(mine_ref[0] == s, mine_v, p_ref[s].astype(f32))
        if lp != l:
            g = g[:, :l]
        m_new = ADAM_B1 * m_ref[...] + (1.0 - ADAM_B1) * g
        v_new = ADAM_B2 * v_ref[...] + (1.0 - ADAM_B2) * (g * g)
        g_ref[...] = g
        nm_ref[...] = m_new
        nv_ref[...] = v_new
        d_ref[...] = -ADAM_LR * ((m_new * c1) / (jnp.sqrt(v_new * c2) + ADAM_EPS) + ADAM_WD * w_ref[...])

    tile = pl.BlockSpec((tr, l), lambda i, mn: (i, 0))
    own_spec = (pl.BlockSpec((1, tr, lp), lambda i, mn: (mn[0], i, 0)) if own.ndim == 3
                else pl.BlockSpec((tr, lp), lambda i, mn: (i, 0)))
    return pl.pallas_call(
        body, name=name, out_shape=[_sds((r, l), f32)] * 4,
        grid_spec=pltpu.PrefetchScalarGridSpec(
            num_scalar_prefetch=1, grid=(r // tr,),
            in_specs=[pl.BlockSpec((parts.shape[0], tr, lp), lambda i, mn: (0, i, 0)), own_spec, tile, tile, tile],
            out_specs=[tile] * 4),
        compiler_params=_cparams(("parallel",)))(mine, parts, own, w, m, v)


def _pack(arrs, lead=None):
    if lead is None:
        flat = jnp.concatenate([a.reshape(-1).astype(f32) for a in arrs])
        n = flat.shape[0]
    else:
        flat = jnp.concatenate([a.reshape(lead, -1).astype(f32) for a in arrs], axis=1)
        n = flat.shape[1]
    tot = -(-n // 1024) * 1024
    if lead is None:
        return jnp.pad(flat, (0, tot - n)).reshape(tot // 128, 128)
    return jnp.pad(flat, ((0, 0), (0, tot - n))).reshape(lead, tot // 128, 128)


def _unpack(packed, shapes, lead=False):
    flat = packed.reshape(packed.shape[0], -1) if lead else packed.reshape(-1)
    out, off = [], 0
    for s in shapes:
        n = int(np.prod(s))
        out.append(flat[:, off:off + n].reshape((packed.shape[0],) + tuple(s)) if lead else flat[off:off + n].reshape(s))
        off += n
    return out


def _merge_shards(g, axis):
    g = jnp.moveaxis(g, 0, axis)
    s = g.shape
    return g.reshape(s[:axis] + (s[axis] * s[axis + 1],) + s[axis + 2:])


def _split_shards(full, axis):
    s = full.shape
    g = full.reshape(s[:axis] + (N_DEV, s[axis] // N_DEV) + s[axis + 1:])
    return jnp.moveaxis(g, axis, 0)


def kernel(x, norm_mix, norm_ffn, norm_final, ab_w_in, gdn_conv_w, gdn_a_log, gdn_dt_bias, gdn_norm, hgrn_lower_bounds, hgrn_norm, ab_w_out, c_w_in, c_conv_w, c_conv_b, c_gate_a_w, c_gate_a_b, c_gate_x_w, c_gate_x_b, c_lambda, c_w_out, ffn_w_up, ffn_conv_w, ffn_conv_b, ffn_w_down, loss_target, m_norm_mix, m_norm_ffn, m_norm_final, m_ab_w_in, m_gdn_conv_w, m_gdn_a_log, m_gdn_dt_bias, m_gdn_norm, m_hgrn_lower_bounds, m_hgrn_norm, m_ab_w_out, m_c_w_in, m_c_conv_w, m_c_conv_b, m_c_gate_a_w, m_c_gate_a_b, m_c_gate_x_w, m_c_gate_x_b, m_c_lambda, m_c_w_out, m_ffn_w_up, m_ffn_conv_w, m_ffn_conv_b, m_ffn_w_down, v_norm_mix, v_norm_ffn, v_norm_final, v_ab_w_in, v_gdn_conv_w, v_gdn_a_log, v_gdn_dt_bias, v_gdn_norm, v_hgrn_lower_bounds, v_hgrn_norm, v_ab_w_out, v_c_w_in, v_c_conv_w, v_c_conv_b, v_c_gate_a_w, v_c_gate_a_b, v_c_gate_x_w, v_c_gate_x_b, v_c_lambda, v_c_w_out, v_ffn_w_up, v_ffn_conv_w, v_ffn_conv_b, v_ffn_w_down):
    wl = dict(zip(WEIGHTS, (norm_mix, norm_ffn, norm_final, ab_w_in, gdn_conv_w, gdn_a_log, gdn_dt_bias, gdn_norm, hgrn_lower_bounds, hgrn_norm, ab_w_out, c_w_in, c_conv_w, c_conv_b, c_gate_a_w, c_gate_a_b, c_gate_x_w, c_gate_x_b, c_lambda, c_w_out, ffn_w_up, ffn_conv_w, ffn_conv_b, ffn_w_down)))
    ml = dict(zip(WEIGHTS, (m_norm_mix, m_norm_ffn, m_norm_final, m_ab_w_in, m_gdn_conv_w, m_gdn_a_log, m_gdn_dt_bias, m_gdn_norm, m_hgrn_lower_bounds, m_hgrn_norm, m_ab_w_out, m_c_w_in, m_c_conv_w, m_c_conv_b, m_c_gate_a_w, m_c_gate_a_b, m_c_gate_x_w, m_c_gate_x_b, m_c_lambda, m_c_w_out, m_ffn_w_up, m_ffn_conv_w, m_ffn_conv_b, m_ffn_w_down)))
    vl = dict(zip(WEIGHTS, (v_norm_mix, v_norm_ffn, v_norm_final, v_ab_w_in, v_gdn_conv_w, v_gdn_a_log, v_gdn_dt_bias, v_gdn_norm, v_hgrn_lower_bounds, v_hgrn_norm, v_ab_w_out, v_c_w_in, v_c_conv_w, v_c_conv_b, v_c_gate_a_w, v_c_gate_a_b, v_c_gate_x_w, v_c_gate_x_b, v_c_lambda, v_c_w_out, v_ffn_w_up, v_ffn_conv_w, v_ffn_conv_b, v_ffn_w_down)))

    big = [n for n in SHARDED if n in MATMUL_WEIGHTS]
    vec = [n for n in SHARDED if n not in MATMUL_WEIGHTS]
    shards = {n: wl[n].astype(bf16) for n in big}
    shards['ab_w_in'] = jnp.pad(shards['ab_w_in'], ((0, 0), (0, 0), (0, AB_SHARD_PAD - AB_SHARD)))
    shards['ffn_w_up'] = jnp.pad(jnp.swapaxes(shards['ffn_w_up'], 1, 2), ((0, 0), (0, FF_PAD - FF_SHARD), (0, 0)))
    shards['vec'] = _pack([wl[n] for n in vec])
    pos = _position()
    layer_items = [_layer_items(l) for l in range(DEPTH)]
    posv = [v.astype(jnp.int32).reshape(1) for v in (4 * pos[0] + 2 * pos[1] + pos[2], 2 * pos[0] + pos[1], pos[2])]
    first_items = layer_items[0] + [('vec', None)]
    lands = [_place_own(first_items, shards, posv, "place_own_0")]
    lands += [_place_own(layer_items[l], shards, posv, "place_own_%d" % l) for l in range(1, DEPTH)]
    first = _gather_now(first_items, shards, lands[0])
    flight = {'shards': {n: shards[n] for n in big}}

    full = {n: wl[n] for n in REPLICATED}

    def start(l, token):
        sems, thru, flight['lands'], tok = _gather_start(layer_items[l], flight['shards'], lands[l], token,
                                                         "gather_start_%d" % l)
        flight['sems'] = list(sems)
        flight['shards'].update(thru)
        full['fwd', l - 1] = tok

    start(1, first[-1])

    for n, a in zip(vec, _unpack(first[-1], [wl[n].shape for n in vec], lead=True)):
        full[n] = _merge_shards(a, SHARD_AXIS[n])
    full['ffn_conv_w'] = _block_pad(full['ffn_conv_w'], 2, 4, FF_PAD)
    full['ffn_conv_b'] = _block_pad(full['ffn_conv_b'], 1, 4, FF_PAD)
    for n in big:
        full[n] = {}

    def fetch(l, x_in):
        items = layer_items[l]
        if l == 0:
            got = first[:len(items)]
        else:
            flight['shards'], got = _gather_wait(items, flight['sems'], flight['shards'], flight['lands'], x_in,
                                                 "gather_wait_%d" % l)
            got = _gather_forward(items, got, "gather_forward_%d" % l)
            if l + 1 < DEPTH:
                start(l + 1, got[0])
        for (nm, li), a in zip(items, got):
            if nm == 'ab_w_in':
                a = _ab_permute(_block_unpad(a, 1, N_DEV, AB_SHARD))
            elif nm in ('ab_w_out', 'c_w_out'):
                a = a.reshape(D_MODEL, D_MODEL)
            elif nm in ('c_gate_a_w', 'c_gate_x_w'):
                a = a.reshape(HEADS, LRU_BLOCK, LRU_BLOCK)
            elif nm == 'ffn_w_down':
                a = a.reshape(D_FFP, D_MODEL)
            elif nm == 'ffn_w_up':
                a = a.reshape(2 * D_FFP, D_MODEL)
            full[nm][li] = a

    cpos = lax.axis_index("c").astype(jnp.int32).reshape(1)
    mine = (2 * lax.axis_index("x") + lax.axis_index("y")).astype(jnp.int32).reshape(1)
    pending = {}

    def push(l, g):
        fulls = {}
        for nm, _ in layer_items[l]:
            a = g[nm].astype(bf16)
            if nm == 'ab_w_in':
                a = _block_pad(_ab_unpermute(a), 1, N_DEV, AB_SHARD_PAD)
            fulls[nm] = a.reshape((1,) + BLOCK_LAYOUT[nm][0][1:])
        pair, _ = _pair_exchange(fulls, jnp.zeros((8, 128), f32), "_%d" % l)
        chips = [_chip_sum(nm, fulls[nm], pair[nm], cpos, "_%d" % l) for nm in fulls]
        sems, chips, crosses, tok = _cross_start(chips, "grad_cross_start_%d" % l)
        pending[l] = (sems, chips, crosses)
        full['bwd', l - 1] = tok

    loss, dx, grads = _local_step(x[0], loss_target[0], full, fetch, push)

    grads['ffn_conv_w'] = _block_unpad(grads['ffn_conv_w'], 2, 4, FF_SHARD)
    grads['ffn_conv_b'] = _block_unpad(grads['ffn_conv_b'], 1, 4, FF_SHARD)
    fulls = {'vec': _pack([_split_shards(grads[n], SHARD_AXIS[n]) for n in vec], lead=N_DEV)}
    chip, rep_chip, recv, rrep = _exchange_grads(fulls, _pack([grads[n] for n in REPLICATED]))
    res = {}
    stacked = {}
    after = full['bwd', -1]
    for l in (3, 2, 1, 0):
        sems, chips, lands = pending[l]
        chips, lands = _cross_wait(sems, chips, lands, after, "grad_cross_wait_%d" % l)
        for (n, li), own, parts in zip(layer_items[l], chips, lands):
            if n == 'ffn_w_up':
                wmv = [jnp.swapaxes(a[n], 1, 2) for a in (wl, ml, vl)]
                rp = FF_PAD
            else:
                shp = wl[n].shape
                rp = int(np.prod(shp[1:-1]))
                wmv = [a[n].reshape(shp[0], rp, shp[-1]) for a in (wl, ml, vl)]
            stacked[n] = _sum_adamw_layer(parts.reshape(4, rp, -1), own.reshape(4, rp, -1), mine, *wmv, li,
                                          stacked.get(n), "adamw_%s_%d" % (n, li))
        if l == 1:
            after = stacked['ffn_w_up'][0]
    for n in big:
        for kind, o in zip(("grad", "delta", "new_m", "new_v"), stacked[n]):
            res[kind, n] = jnp.swapaxes(o, 1, 2) if n == 'ffn_w_up' else o.reshape(wl[n].shape)
    for names, parts, own, tag in ((vec, recv['vec'], chip['vec'], "adamw_vectors"),
                                   (REPLICATED, rrep, rep_chip, "adamw_replicated")):
        outs = _sum_adamw(parts, own, mine, _pack([wl[n] for n in names]), _pack([ml[n] for n in names]),
                          _pack([vl[n] for n in names]), tag)
        for kind, o in zip(("grad", "delta", "new_m", "new_v"), outs):
            for n, a in zip(names, _unpack(o, [wl[n].shape for n in names])):
                res[kind, n] = a

    loss = lax.psum(loss, ("x", "y", "c"))
    return (loss, dx[None], *[res[kind, n] for kind in ("grad", "delta", "new_m", "new_v") for n in WEIGHTS])
```

```python
import functools

import numpy as np
import jax
import jax.numpy as jnp
from jax import lax
from jax.experimental import pallas as pl
from jax.experimental.pallas import tpu as pltpu

f32 = jnp.float32
bf16 = jnp.bfloat16
HI = lax.Precision.HIGHEST
MESH = pl.DeviceIdType.MESH

N_DEV = 8
D_MODEL = 1024
DEPTH = 4
EPS = 1e-6
F_FLOOR = 1e-30
HEADS = 4
HEAD_DIM = 128
GDN_WIDTH = 512
GDN_CONV = 4
GDN_CHUNK = 64
HGRN_CHUNK = 16
HGRN_STEP = 128
MIX_WIDTH = 1024
AB_COLS = 4104
AB_PAD = 4224
LRU_WIDTH = 1024
LRU_BLOCK = 256
LRU_CONV = 4
RG_C = 8.0
D_FF = 2816
FF_SHARD = 704
FF_PAD = 768
D_FFP = 4 * FF_PAD
FF_ROWS = 352
AB_SHARD, AB_SHARD_PAD = 513, 640
FFN_CONV = 3
ADAM_LR, ADAM_B1, ADAM_B2, ADAM_EPS, ADAM_WD, ADAM_STEP = 0.001, 0.9, 0.999, 1e-08, 0.01, 10
VMEM_LIMIT = 56 * 1024 * 1024
ROW_SLAB = 32
PACK_LANES = 512
PACK_ROWS = 256

OFF_Q, OFF_K, OFF_V, OFF_Z, OFF_QB, OFF_FB, OFF_IB, OFF_GB, OFF_BA = 0, 512, 1024, 1536, 2048, 2560, 3072, 3584, 4096

WEIGHTS = ['norm_mix', 'norm_ffn', 'norm_final', 'ab_w_in', 'gdn_conv_w', 'gdn_a_log', 'gdn_dt_bias', 'gdn_norm',
           'hgrn_lower_bounds', 'hgrn_norm', 'ab_w_out', 'c_w_in', 'c_conv_w', 'c_conv_b', 'c_gate_a_w', 'c_gate_a_b',
           'c_gate_x_w', 'c_gate_x_b', 'c_lambda', 'c_w_out', 'ffn_w_up', 'ffn_conv_w', 'ffn_conv_b', 'ffn_w_down']
SHARD_AXIS = {'norm_mix': None, 'norm_ffn': None, 'norm_final': None, 'ab_w_in': 2, 'gdn_conv_w': 2, 'gdn_a_log': None,
              'gdn_dt_bias': None, 'gdn_norm': None, 'hgrn_lower_bounds': None, 'hgrn_norm': None, 'ab_w_out': 1,
              'c_w_in': 2, 'c_conv_w': 2, 'c_conv_b': 1, 'c_gate_a_w': 2, 'c_gate_a_b': 1, 'c_gate_x_w': 2,
              'c_gate_x_b': 1, 'c_lambda': 1, 'c_w_out': 1, 'ffn_w_up': 2, 'ffn_conv_w': 2, 'ffn_conv_b': None,
              'ffn_w_down': 1}
MATMUL_WEIGHTS = ('ab_w_in', 'ab_w_out', 'c_w_in', 'c_gate_a_w', 'c_gate_x_w', 'c_w_out', 'ffn_w_up', 'ffn_w_down')
SHARDED = [n for n in WEIGHTS if SHARD_AXIS[n] is not None]
REPLICATED = [n for n in WEIGHTS if SHARD_AXIS[n] is None]


def _tile(n, prefs=(512, 384, 256, 128)):
    for p in prefs:
        if n % p == 0:
            return p
    return n


def _cparams(sem=None):
    kw = dict(vmem_limit_bytes=VMEM_LIMIT)
    if sem is not None:
        kw['dimension_semantics'] = sem
    return pltpu.CompilerParams(**kw)


def _sds(shape, dtype):
    return jax.ShapeDtypeStruct(tuple(shape), dtype)


def _sigmoid(x):
    return 1.0 / (1.0 + jnp.exp(-x))


def _silu(x):
    return x * (0.5 * jnp.tanh(0.5 * x) + 0.5)


def _log1p(x):
    u = 1.0 + x
    return jnp.where(u == 1.0, x, jnp.log(u) * (x / jnp.where(u == 1.0, 1.0, u - 1.0)))


def _softplus(x):
    return jnp.maximum(x, 0.0) + _log1p(jnp.exp(-jnp.abs(x)))


def _expm1(x):
    small = jnp.abs(x) < 0.05
    xs = jnp.where(small, x, 0.0)
    series = xs * (1.0 + xs * (0.5 + xs * (1.0 / 6.0 + xs * (1.0 / 24.0 + xs * (1.0 / 120.0)))))
    return jnp.where(small, series, jnp.exp(x) - 1.0)


def _gelu(x):
    return 0.5 * x * (1.0 + jnp.tanh(0.7978845608028654 * (x + 0.044715 * x * x * x)))


def _rms(x, gain):
    return x * lax.rsqrt(jnp.mean(x * x, axis=-1, keepdims=True) + EPS) * gain


def _dot(a, b, dims=((1,), (0,)), precision=None):
    return lax.dot_general(a, b, (dims, ((), ())), precision=precision, preferred_element_type=f32)


def _bdot(a, b, dims=((1,), (0,))):
    return _dot(a.astype(bf16), b.astype(bf16), dims)


NT = ((1,), (1,))
TN = ((0,), (0,))


def _shift_down(x, k):
    if k == 0:
        return x
    row = lax.broadcasted_iota(jnp.int32, x.shape, 0)
    return jnp.where(row >= k, pltpu.roll(x, k, 0), 0.0)


def _shift_up(x, k, fill=0.0):
    if k == 0:
        return x
    n = x.shape[0]
    row = lax.broadcasted_iota(jnp.int32, x.shape, 0)
    return jnp.where(row < n - k, pltpu.roll(x, n - k, 0), fill)


def _conv_fwd(x, w_ref, width):
    acc = w_ref[width - 1:width, :] * x
    for k in range(width - 1):
        acc = acc + w_ref[k:k + 1, :] * _shift_down(x, width - 1 - k)
    return acc


def _conv_bwd(x, dout, w_ref, dw_ref, width):
    dx = w_ref[width - 1:width, :] * dout
    dw_ref[width - 1:width, :] = jnp.sum(dout * x, axis=0, keepdims=True)
    for k in range(width - 1):
        s = width - 1 - k
        dx = dx + w_ref[k:k + 1, :] * _shift_up(dout, s)
        dw_ref[k:k + 1, :] = jnp.sum(dout * _shift_down(x, s), axis=0, keepdims=True)
    return dx


MM_VMEM_BUDGET = 36 * 1024 * 1024
MM_MAX_TILE = 1024 * 1024


def _mm_tiles(m, n, k, out_bytes):
    best = None
    for tm in (1024, 512, 384, 256, 128):
        if m % tm:
            continue
        for tn in range(1536, 0, -128):
            if n % tn or tm * tn > MM_MAX_TILE:
                continue
            score = (tm * tn, min(tm, tn))
            if 2 * (tm * k * 2 + k * tn * 2 + tm * tn * out_bytes) <= MM_VMEM_BUDGET and (best is None or score > best[0]):
                best = (score, tm, tn)
    return (best[1], best[2]) if best else (_tile(m), _tile(n))


def _mm(a, b, *, ta=False, tb=False, add=None, out_dtype=f32, name):
    m, k = (a.shape[1], a.shape[0]) if ta else a.shape
    n = b.shape[0] if tb else b.shape[1]
    tm, tn = _mm_tiles(m, n, k, jnp.dtype(out_dtype).itemsize + (4 if add is not None else 0))
    dims = ((0 if ta else 1,), (1 if tb else 0,))

    def body(*refs):
        a_ref, b_ref = refs[0], refs[1]
        o_ref = refs[-1]
        r = _dot(a_ref[...], b_ref[...], dims)
        if add is not None:
            r = r + refs[2][...]
        o_ref[...] = r.astype(out_dtype)

    a_spec = pl.BlockSpec((k, tm), lambda j, i: (0, i)) if ta else pl.BlockSpec((tm, k), lambda j, i: (i, 0))
    b_spec = pl.BlockSpec((tn, k), lambda j, i: (j, 0)) if tb else pl.BlockSpec((k, tn), lambda j, i: (0, j))
    o_spec = pl.BlockSpec((tm, tn), lambda j, i: (i, j))
    ins, specs = [a, b], [a_spec, b_spec]
    if add is not None:
        ins.append(add)
        specs.append(o_spec)
    return pl.pallas_call(body, name=name, grid=(n // tn, m // tm), in_specs=specs, out_specs=o_spec,
                          out_shape=_sds((m, n), out_dtype), compiler_params=_cparams(("parallel", "parallel")))(*ins)


def _rms_fwd(x, gain, name):
    t, d = x.shape
    tr = _tile(t, (256, 128))

    def body(x_ref, g_ref, h_ref):
        h_ref[...] = _rms(x_ref[...], g_ref[...]).astype(bf16)

    return pl.pallas_call(body, name=name, grid=(t // tr,),
                          in_specs=[pl.BlockSpec((tr, d), lambda i: (i, 0)), pl.BlockSpec((1, d), lambda i: (0, 0))],
                          out_specs=pl.BlockSpec((tr, d), lambda i: (i, 0)), out_shape=_sds((t, d), bf16),
                          compiler_params=_cparams(("parallel",)))(x, gain)


def _rms_bwd(x, gain, dh, dres, name):
    t, d = x.shape
    tr = _tile(t, (256, 128))

    def body(x_ref, g_ref, dh_ref, dres_ref, dx_ref, dxb_ref, dg_ref):
        _, vjp = jax.vjp(_rms, x_ref[...], g_ref[...])
        dx, dg = vjp(dh_ref[...])
        dx = dx + dres_ref[...]
        dx_ref[...] = dx
        dxb_ref[...] = dx.astype(bf16)

        @pl.when(pl.program_id(0) == 0)
        def _():
            dg_ref[...] = jnp.zeros_like(dg_ref)

        dg_ref[...] += dg

    row = pl.BlockSpec((tr, d), lambda i: (i, 0))
    vec = pl.BlockSpec((1, d), lambda i: (0, 0))
    return pl.pallas_call(body, name=name, grid=(t // tr,), in_specs=[row, vec, row, row], out_specs=[row, row, vec],
                          out_shape=[_sds((t, d), f32), _sds((t, d), bf16), _sds((1, d), f32)],
                          compiler_params=_cparams(("arbitrary",)))(x, gain, dh, dres)


def _loss_head(x, gain, target):
    t, d = x.shape
    tr = _tile(t, (256, 128))

    def f(xv, g, tgt):
        err = _rms(xv, g) - tgt
        return 0.5 * jnp.sum(jnp.mean(err * err, axis=-1, keepdims=True), axis=0, keepdims=True)

    def body(x_ref, g_ref, t_ref, loss_ref, dx_ref, dxb_ref, dg_ref):
        loss, vjp = jax.vjp(lambda xv, g: f(xv, g, t_ref[...]), x_ref[...], g_ref[...])
        dx, dg = vjp(jnp.ones((1, 1), f32))
        dx_ref[...] = dx
        dxb_ref[...] = dx.astype(bf16)

        @pl.when(pl.program_id(0) == 0)
        def _():
            dg_ref[...] = jnp.zeros_like(dg_ref)
            loss_ref[...] = jnp.zeros_like(loss_ref)

        dg_ref[...] += dg
        loss_ref[...] += jnp.broadcast_to(loss, loss_ref.shape)

    row = pl.BlockSpec((tr, d), lambda i: (i, 0))
    vec = pl.BlockSpec((1, d), lambda i: (0, 0))
    one = pl.BlockSpec((8, 128), lambda i: (0, 0))
    return pl.pallas_call(body, name="loss_head", grid=(t // tr,), in_specs=[row, vec, row],
                          out_specs=[one, row, row, vec],
                          out_shape=[_sds((8, 128), f32), _sds((t, d), f32), _sds((t, d), bf16), _sds((1, d), f32)],
                          compiler_params=_cparams(("arbitrary",)))(x, gain, target)


def _ffn_act_fwd(u, conv_w, conv_b):
    t = u.shape[0]
    tc = FF_PAD // 2
    nb = D_FFP // tc

    def body(g_ref, v_ref, w_ref, b_ref, a_ref, gc_ref):
        gc_ref[...] = _conv_fwd(g_ref[...], w_ref, FFN_CONV) + b_ref[...]

        def slab(i, carry):
            rows = pl.ds(pl.multiple_of(i * ROW_SLAB, ROW_SLAB), ROW_SLAB)
            a_ref[rows, :] = (_silu(gc_ref[rows, :]) * v_ref[rows, :]).astype(bf16)
            return carry

        lax.fori_loop(0, t // ROW_SLAB, slab, 0)

    return pl.pallas_call(
        body, name="ffn_act_fwd", grid=(nb,),
        in_specs=[pl.BlockSpec((t, tc), lambda j: (0, j)), pl.BlockSpec((t, tc), lambda j: (0, j + nb)),
                  pl.BlockSpec((FFN_CONV, tc), lambda j: (0, j)), pl.BlockSpec((1, tc), lambda j: (0, j))],
        out_specs=pl.BlockSpec((t, tc), lambda j: (0, j)), out_shape=_sds((t, D_FFP), bf16),
        scratch_shapes=[pltpu.VMEM((t, tc), f32)],
        compiler_params=_cparams(("parallel",)))(u, u, conv_w, conv_b)


def _ffn_act_bwd(u, conv_w, conv_b, da):
    t = u.shape[0]
    tc = FF_PAD // 2
    nb = D_FFP // tc

    def act(gc, val):
        return _silu(gc) * val

    def body(g_ref, v_ref, w_ref, b_ref, da_ref, a_ref, dg_ref, dv_ref, dw_ref, db_ref, gc_ref):
        gp = g_ref[...]
        gc_ref[...] = _conv_fwd(gp, w_ref, FFN_CONV) + b_ref[...]

        def slab(i, carry):
            rows = pl.ds(pl.multiple_of(i * ROW_SLAB, ROW_SLAB), ROW_SLAB)
            a, vjp = jax.vjp(act, gc_ref[rows, :], v_ref[rows, :])
            dgc, dval = vjp(da_ref[rows, :])
            a_ref[rows, :] = a.astype(bf16)
            dv_ref[rows, :] = dval.astype(bf16)
            gc_ref[rows, :] = dgc
            return carry

        lax.fori_loop(0, t // ROW_SLAB, slab, 0)
        dgc = gc_ref[...]
        db_ref[...] = jnp.sum(dgc, axis=0, keepdims=True)
        dg_ref[...] = _conv_bwd(gp, dgc, w_ref, dw_ref, FFN_CONV).astype(bf16)

    col = pl.BlockSpec((t, tc), lambda j: (0, j))
    return pl.pallas_call(
        body, name="ffn_act_bwd", grid=(nb,),
        in_specs=[col, pl.BlockSpec((t, tc), lambda j: (0, j + nb)), pl.BlockSpec((FFN_CONV, tc), lambda j: (0, j)),
                  pl.BlockSpec((1, tc), lambda j: (0, j)), col],
        out_specs=[col, col, col, pl.BlockSpec((FFN_CONV, tc), lambda j: (0, j)), pl.BlockSpec((1, tc), lambda j: (0, j))],
        out_shape=[_sds((t, D_FFP), bf16), _sds((t, D_FFP), bf16), _sds((t, D_FFP), bf16), _sds((FFN_CONV, D_FFP), f32),
                   _sds((1, D_FFP), f32)],
        scratch_shapes=[pltpu.VMEM((t, tc), f32)],
        compiler_params=_cparams(("parallel",)))(u, u, conv_w, conv_b, da)


def _lru_gates(xc, ra, ia, lam):
    r = _sigmoid(ra)
    i = _sigmoid(ia)
    log_a = -RG_C * r * _softplus(-lam)
    a = jnp.exp(log_a)
    u = jnp.sqrt(jnp.maximum(-_expm1(2.0 * log_a), 0.0)) * (i * xc)
    return a, u


def _lin_scan(a, u):
    n = a.shape[0]
    row = lax.broadcasted_iota(jnp.int32, a.shape, 0)
    s = 1
    while s < n:
        keep = row >= s
        u = a * jnp.where(keep, pltpu.roll(u, s, 0), 0.0) + u
        a = a * jnp.where(keep, pltpu.roll(a, s, 0), 1.0)
        s *= 2
    return u


def _rev_scan(a_next, d):
    n = d.shape[0]
    row = lax.broadcasted_iota(jnp.int32, d.shape, 0)
    a = a_next
    s = 1
    while s < n:
        keep = row < n - s
        d = a * jnp.where(keep, pltpu.roll(d, n - s, 0), 0.0) + d
        a = a * jnp.where(keep, pltpu.roll(a, n - s, 0), 1.0)
        s *= 2
    return d


def _col_conv_fwd(p, col_off, conv_w, conv_b, width, tc, name):
    t = p.shape[0]
    c = conv_w.shape[1]
    ob = col_off // tc

    def body(x_ref, w_ref, b_ref, o_ref):
        o_ref[...] = _conv_fwd(x_ref[...], w_ref, width) + b_ref[...]

    return pl.pallas_call(
        body, name=name, grid=(c // tc,),
        in_specs=[pl.BlockSpec((t, tc), lambda j: (0, j + ob)), pl.BlockSpec((width, tc), lambda j: (0, j)),
                  pl.BlockSpec((1, tc), lambda j: (0, j))],
        out_specs=pl.BlockSpec((t, tc), lambda j: (0, j)), out_shape=_sds((t, c), f32),
        compiler_params=_cparams(("parallel",)))(p, conv_w, conv_b)


def _col_conv_bwd(p, col_off, conv_w, dxc, width, tc, name):
    t = p.shape[0]
    c = conv_w.shape[1]
    ob = col_off // tc

    def body(x_ref, w_ref, d_ref, dx_ref, dw_ref, db_ref):
        d = d_ref[...]
        db_ref[...] = jnp.sum(d, axis=0, keepdims=True)
        dx_ref[...] = _conv_bwd(x_ref[...], d, w_ref, dw_ref, width).astype(bf16)

    col = pl.BlockSpec((t, tc), lambda j: (0, j))
    return pl.pallas_call(
        body, name=name, grid=(c // tc,),
        in_specs=[pl.BlockSpec((t, tc), lambda j: (0, j + ob)), pl.BlockSpec((width, tc), lambda j: (0, j)), col],
        out_specs=[col, pl.BlockSpec((width, tc), lambda j: (0, j)), pl.BlockSpec((1, tc), lambda j: (0, j))],
        out_shape=[_sds((t, c), bf16), _sds((width, c), f32), _sds((1, c), f32)],
        compiler_params=_cparams(("parallel",)))(p, conv_w, dxc)


def _lru_fwd(p, xc, wa, ba, wx, bx, lam):
    t = p.shape[0]
    bw = LRU_BLOCK

    def body(y_ref, xc_ref, wa_ref, ba_ref, wx_ref, bx_ref, lam_ref, out_ref, hs_ref, a_ref):
        xc_v = xc_ref[...]
        xb = xc_v.astype(bf16)
        ra = _dot(xb, wa_ref[0]) + ba_ref[...]
        ia = _dot(xb, wx_ref[0]) + bx_ref[...]
        a, u = _lru_gates(xc_v, ra, ia, lam_ref[...])
        a_ref[...] = a
        hs = _lin_scan(a, u)
        hs_ref[...] = hs
        out_ref[...] = (hs * _gelu(y_ref[...])).astype(bf16)

    col = pl.BlockSpec((t, bw), lambda h: (0, h))
    vec = pl.BlockSpec((1, bw), lambda h: (0, h))
    mat = pl.BlockSpec((1, bw, bw), lambda h: (h, 0, 0))
    return pl.pallas_call(
        body, name="lru_fwd", grid=(HEADS,), in_specs=[col, col, mat, vec, mat, vec, vec], out_specs=[col, col, col],
        out_shape=[_sds((t, LRU_WIDTH), bf16), _sds((t, LRU_WIDTH), f32), _sds((t, LRU_WIDTH), f32)],
        compiler_params=_cparams(("parallel",)))(p, xc, wa, ba, wx, bx, lam)


def _lru_bwd_scan(p, a, hs, dout):
    t = p.shape[0]
    bw = LRU_BLOCK

    def body(y_ref, a_ref, hs_ref, do_ref, dy_ref, da_ref, du_ref):
        hs_v = hs_ref[...]
        do = do_ref[...]
        gate, vjp = jax.vjp(_gelu, y_ref[...])
        dy_ref[...] = vjp(do * hs_v)[0].astype(bf16)
        g = _rev_scan(_shift_up(a_ref[...], 1), do * gate)
        du_ref[...] = g
        da_ref[...] = g * _shift_down(hs_v, 1)

    col = pl.BlockSpec((t, bw), lambda h: (0, h))
    return pl.pallas_call(
        body, name="lru_bwd_scan", grid=(HEADS,), in_specs=[col, col, col, col], out_specs=[col, col, col],
        out_shape=[_sds((t, LRU_WIDTH), bf16), _sds((t, LRU_WIDTH), f32), _sds((t, LRU_WIDTH), f32)],
        compiler_params=_cparams(("parallel",)))(p, a, hs, dout)


def _lru_bwd_gates(xc, da, du, wa, ba, wx, bx, lam):
    t = xc.shape[0]
    bw = LRU_BLOCK
    tr = _tile(t, (512, 256, 128))

    def body(xc_ref, da_ref, du_ref, wa_ref, ba_ref, wx_ref, bx_ref, lam_ref,
             dxc_ref, dwa_ref, dwx_ref, dba_ref, dbx_ref, dlam_ref):
        xc_v = xc_ref[...]
        xb = xc_v.astype(bf16)
        ra = _dot(xb, wa_ref[0]) + ba_ref[...]
        ia = _dot(xb, wx_ref[0]) + bx_ref[...]
        _, vjp = jax.vjp(_lru_gates, xc_v, ra, ia, lam_ref[...])
        dxc, dra, dia, dlam = vjp((da_ref[...], du_ref[...]))
        drb, dib = dra.astype(bf16), dia.astype(bf16)
        dxc_ref[...] = dxc + _dot(drb, wa_ref[0], NT) + _dot(dib, wx_ref[0], NT)

        @pl.when(pl.program_id(1) == 0)
        def _():
            dwa_ref[...] = jnp.zeros_like(dwa_ref)
            dwx_ref[...] = jnp.zeros_like(dwx_ref)
            dba_ref[...] = jnp.zeros_like(dba_ref)
            dbx_ref[...] = jnp.zeros_like(dbx_ref)
            dlam_ref[...] = jnp.zeros_like(dlam_ref)

        dwa_ref[0] += _dot(xb, drb, TN)
        dwx_ref[0] += _dot(xb, dib, TN)
        dba_ref[...] += jnp.sum(dra, axis=0, keepdims=True)
        dbx_ref[...] += jnp.sum(dia, axis=0, keepdims=True)
        dlam_ref[...] += dlam

    tile = pl.BlockSpec((tr, bw), lambda h, i: (i, h))
    vec = pl.BlockSpec((1, bw), lambda h, i: (0, h))
    mat = pl.BlockSpec((1, bw, bw), lambda h, i: (h, 0, 0))
    return pl.pallas_call(
        body, name="lru_bwd_gates", grid=(HEADS, t // tr), in_specs=[tile, tile, tile, mat, vec, mat, vec, vec],
        out_specs=[tile, mat, mat, vec, vec, vec],
        out_shape=[_sds((t, LRU_WIDTH), f32), _sds((HEADS, bw, bw), f32), _sds((HEADS, bw, bw), f32),
                   _sds((1, LRU_WIDTH), f32), _sds((1, LRU_WIDTH), f32), _sds((1, LRU_WIDTH), f32)],
        compiler_params=_cparams(("parallel", "arbitrary")))(xc, da, du, wa, ba, wx, bx, lam)


def _gdn_pre_fn(cq, ck, cv, ba, alog, dtb, h):
    q, k, v = _silu(cq), _silu(ck), _silu(cv)
    q = q * lax.rsqrt(jnp.sum(q * q, axis=-1, keepdims=True) + EPS) * (HEAD_DIM ** -0.5)
    k = k * lax.rsqrt(jnp.sum(k * k, axis=-1, keepdims=True) + EPS)
    lane = lax.broadcasted_iota(jnp.int32, (1, HEAD_DIM), 1)
    mb = (lane == h).astype(f32)
    ma = (lane == HEADS + h).astype(f32)
    beta_raw = jnp.sum(ba * mb, axis=-1, keepdims=True)
    alpha = jnp.sum(ba * ma, axis=-1, keepdims=True)
    al = jnp.sum(alog * mb, axis=-1, keepdims=True)
    db = jnp.sum(dtb * mb, axis=-1, keepdims=True)
    beta = _sigmoid(beta_raw)
    g = -jnp.exp(al) * _softplus(alpha + db)
    return q, k, v, jnp.broadcast_to(beta, q.shape), jnp.broadcast_to(g, q.shape)


def _gdn_pre_fwd(p, conv_w, alog, dtb):
    t = p.shape[0]
    hd = HEAD_DIM

    def body(pq_ref, pk_ref, pv_ref, ba_ref, wq_ref, wk_ref, wv_ref, al_ref, dt_ref, q_ref, k_ref, v_ref, b_ref, g_ref):
        h = pl.program_id(0)
        cq = _conv_fwd(pq_ref[...], wq_ref, GDN_CONV)
        ck = _conv_fwd(pk_ref[...], wk_ref, GDN_CONV)
        cv = _conv_fwd(pv_ref[...], wv_ref, GDN_CONV)
        q, k, v, be, ge = _gdn_pre_fn(cq, ck, cv, ba_ref[...], al_ref[...], dt_ref[...], h)
        q_ref[...], k_ref[...], v_ref[...], b_ref[...], g_ref[...] = q, k, v, be, ge

    def pcol(off):
        return pl.BlockSpec((t, hd), lambda h: (0, h + off // hd))

    def wcol(off):
        return pl.BlockSpec((GDN_CONV, hd), lambda h: (0, h + off // hd))

    vec = pl.BlockSpec((1, hd), lambda h: (0, 0))
    out = pl.BlockSpec((t, hd), lambda h: (0, h))
    return pl.pallas_call(
        body, name="gdn_pre_fwd", grid=(HEADS,),
        in_specs=[pcol(OFF_Q), pcol(OFF_K), pcol(OFF_V), pl.BlockSpec((t, hd), lambda h: (0, OFF_BA // hd)),
                  wcol(0), wcol(GDN_WIDTH), wcol(2 * GDN_WIDTH), vec, vec],
        out_specs=[out] * 5, out_shape=[_sds((t, GDN_WIDTH), f32)] * 5,
        compiler_params=_cparams(("parallel",)))(p, p, p, p, conv_w, conv_w, conv_w, alog, dtb)


def _gdn_pre_bwd(p, conv_w, alog, dtb, dq, dk, dv, dbe, dge):
    t = p.shape[0]
    hd = HEAD_DIM

    def body(pq_ref, pk_ref, pv_ref, ba_ref, wq_ref, wk_ref, wv_ref, al_ref, dt_ref,
             dq_ref, dk_ref, dv_ref, dbe_ref, dge_ref,
             opq_ref, opk_ref, opv_ref, dba_ref, dwq_ref, dwk_ref, dwv_ref, dal_ref, ddt_ref):
        h = pl.program_id(0)
        pq, pk, pv = pq_ref[...], pk_ref[...], pv_ref[...]
        cq = _conv_fwd(pq, wq_ref, GDN_CONV)
        ck = _conv_fwd(pk, wk_ref, GDN_CONV)
        cv = _conv_fwd(pv, wv_ref, GDN_CONV)
        _, vjp = jax.vjp(functools.partial(_gdn_pre_fn, h=h), cq, ck, cv, ba_ref[...], al_ref[...], dt_ref[...])
        dcq, dck, dcv, dba, dal, ddt = vjp((dq_ref[...], dk_ref[...], dv_ref[...], dbe_ref[...], dge_ref[...]))
        opq_ref[...] = _conv_bwd(pq, dcq, wq_ref, dwq_ref, GDN_CONV).astype(bf16)
        opk_ref[...] = _conv_bwd(pk, dck, wk_ref, dwk_ref, GDN_CONV).astype(bf16)
        opv_ref[...] = _conv_bwd(pv, dcv, wv_ref, dwv_ref, GDN_CONV).astype(bf16)

        @pl.when(h == 0)
        def _():
            dba_ref[...] = jnp.zeros_like(dba_ref)
            dal_ref[...] = jnp.zeros_like(dal_ref)
            ddt_ref[...] = jnp.zeros_like(ddt_ref)

        dba_ref[...] += dba
        dal_ref[...] += dal
        ddt_ref[...] += ddt

    def pcol(off):
        return pl.BlockSpec((t, hd), lambda h: (0, h + off // hd))

    def wcol(off):
        return pl.BlockSpec((GDN_CONV, hd), lambda h: (0, h + off // hd))

    vec = pl.BlockSpec((1, hd), lambda h: (0, 0))
    col = pl.BlockSpec((t, hd), lambda h: (0, h))
    full = pl.BlockSpec((t, hd), lambda h: (0, 0))
    wout = pl.BlockSpec((GDN_CONV, hd), lambda h: (0, h))
    return pl.pallas_call(
        body, name="gdn_pre_bwd", grid=(HEADS,),
        in_specs=[pcol(OFF_Q), pcol(OFF_K), pcol(OFF_V), pl.BlockSpec((t, hd), lambda h: (0, OFF_BA // hd)),
                  wcol(0), wcol(GDN_WIDTH), wcol(2 * GDN_WIDTH), vec, vec, col, col, col, col, col],
        out_specs=[col, col, col, full, wout, wout, wout, vec, vec],
        out_shape=[_sds((t, GDN_WIDTH), bf16)] * 3 + [_sds((t, hd), f32)] + [_sds((GDN_CONV, GDN_WIDTH), f32)] * 3
        + [_sds((1, hd), f32)] * 2,
        compiler_params=_cparams(("arbitrary",)))(p, p, p, p, conv_w, conv_w, conv_w, alog, dtb, dq, dk, dv, dbe, dge)


BNN = (((2,), (1,)), ((0,), (0,)))
BNT = (((2,), (2,)), ((0,), (0,)))
BTN = (((1,), (1,)), ((0,), (0,)))


def _hdot(a, b, dn=BNN, precision=None):
    return lax.dot_general(a, b, dn, precision=precision, preferred_element_type=f32)


def _hbdot(a, b, dn=BNN):
    return _hdot(a.astype(bf16), b.astype(bf16), dn)


def _tri_inverse(a):
    c = a.shape[-1]
    r = lax.broadcasted_iota(jnp.int32, (c, c), 0)
    col = lax.broadcasted_iota(jnp.int32, (c, c), 1)
    m = -a
    inv = jnp.where(r == col, 1.0, 0.0) + m
    s = 2
    while s < c:
        m = _hdot(m, m, precision=HI)
        inv = inv + _hdot(inv, m, precision=HI)
        s *= 2
    return inv


@jax.custom_vjp
def _saved_inverse(a, inv):
    return inv


def _saved_inverse_fwd(a, inv):
    return inv, inv


def _saved_inverse_bwd(inv, dinv):
    return -_hdot(_hdot(inv, dinv, BTN, precision=HI), inv, BNT, precision=HI), jnp.zeros_like(inv)


_saved_inverse.defvjp(_saved_inverse_fwd, _saved_inverse_bwd)


def _gdn_chunk(s, q, k, v, ge, be, tinv=None):
    nh, c, _ = q.shape
    r = lax.broadcasted_iota(jnp.int32, (c, c), 0)
    col = lax.broadcasted_iota(jnp.int32, (c, c), 1)
    causal = r >= col
    tri = jnp.broadcast_to(causal.astype(f32), (nh, c, c))
    gc = _hdot(tri, ge, precision=HI)
    gcc = gc[:, :, :c]
    gcr = jnp.swapaxes(gc, 1, 2)[:, :c, :]
    decay = jnp.where(causal, jnp.exp(jnp.where(causal, gcc - gcr, 0.0)), 0.0)
    kb = k * be
    lower = jnp.where(r > col, _hbdot(kb, k, BNT) * decay, 0.0)
    tinv = _tri_inverse(lower) if tinv is None else _saved_inverse(lower, tinv)
    egc = jnp.exp(gc)
    u = _hdot(tinv, v * be, precision=HI)
    w = _hdot(tinv, kb * egc, precision=HI)
    attn = _hbdot(q, k, BNT) * decay
    gl = gc[:, c - 1:c, :]
    v_new = u - _hbdot(w, s)
    o = _hbdot(q * egc, s) + _hbdot(attn, v_new)
    s_new = s * jnp.exp(gl) + _hbdot(k * jnp.exp(gl - gc), v_new, BTN)
    return o, s_new, tinv


def _heads_major(ref):
    return jnp.stack([ref[:, h * HEAD_DIM:(h + 1) * HEAD_DIM] for h in range(HEADS)])


def _gdn_core_fwd(q, k, v, ge, be):
    t = q.shape[0]
    c, hd = GDN_CHUNK, HEAD_DIM
    n = t // c

    def body(q_ref, k_ref, v_ref, g_ref, b_ref, o_ref, st_ref, ti_ref, s_ref):
        @pl.when(pl.program_id(0) == 0)
        def _():
            s_ref[...] = jnp.zeros_like(s_ref)

        s = s_ref[...]
        st_ref[:, 0] = s
        o, s_new, tinv = _gdn_chunk(s, *[_heads_major(r) for r in (q_ref, k_ref, v_ref, g_ref, b_ref)])
        ti_ref[0] = tinv
        for h in range(HEADS):
            o_ref[:, h * hd:(h + 1) * hd] = o[h]
        s_ref[...] = s_new

    tile = pl.BlockSpec((c, GDN_WIDTH), lambda i: (i, 0))
    return pl.pallas_call(
        body, name="gdn_core_fwd", grid=(n,), in_specs=[tile] * 5,
        out_specs=[tile, pl.BlockSpec((HEADS, 1, hd, hd), lambda i: (0, i, 0, 0)),
                   pl.BlockSpec((1, HEADS, c, c), lambda i: (i, 0, 0, 0))],
        out_shape=[_sds((t, GDN_WIDTH), f32), _sds((HEADS, n, hd, hd), f32), _sds((n, HEADS, c, c), f32)],
        scratch_shapes=[pltpu.VMEM((HEADS, hd, hd), f32)],
        compiler_params=_cparams(("arbitrary",)))(q, k, v, ge, be)


def _gdn_core_bwd(q, k, v, ge, be, states, do):
    t = q.shape[0]
    c, hd = GDN_CHUNK, HEAD_DIM
    n = t // c
    states, tinvs = states

    def body(q_ref, k_ref, v_ref, g_ref, b_ref, st_ref, ti_ref, do_ref, dq_ref, dk_ref, dv_ref, dg_ref, db_ref, ds_ref):
        @pl.when(pl.program_id(0) == 0)
        def _():
            ds_ref[...] = jnp.zeros_like(ds_ref)

        tinv = ti_ref[0]
        _, vjp = jax.vjp(lambda *a: _gdn_chunk(*a, tinv=tinv)[:2], st_ref[:, 0],
                         *[_heads_major(r) for r in (q_ref, k_ref, v_ref, g_ref, b_ref)])
        ds, *dins = vjp((_heads_major(do_ref), ds_ref[...]))
        ds_ref[...] = ds
        for d_ref, d in zip((dq_ref, dk_ref, dv_ref, dg_ref, db_ref), dins):
            for h in range(HEADS):
                d_ref[:, h * hd:(h + 1) * hd] = d[h]

    tile = pl.BlockSpec((c, GDN_WIDTH), lambda i: (n - 1 - i, 0))
    return pl.pallas_call(
        body, name="gdn_core_bwd", grid=(n,),
        in_specs=[tile] * 5 + [pl.BlockSpec((HEADS, 1, hd, hd), lambda i: (0, n - 1 - i, 0, 0)),
                               pl.BlockSpec((1, HEADS, c, c), lambda i: (n - 1 - i, 0, 0, 0)), tile],
        out_specs=[tile] * 5, out_shape=[_sds((t, GDN_WIDTH), f32)] * 5,
        scratch_shapes=[pltpu.VMEM((HEADS, hd, hd), f32)],
        compiler_params=_cparams(("arbitrary",)))(q, k, v, ge, be, states, tinvs, do)


def _post_fn(o, z, gain):
    return _rms(o, gain) * _silu(z)


def _post_fwd(o, p, z_off, gain, name):
    t = o.shape[0]
    hd = HEAD_DIM

    def body(o_ref, z_ref, g_ref, y_ref):
        y_ref[...] = _post_fn(o_ref[...], z_ref[...], g_ref[...]).astype(bf16)

    col = pl.BlockSpec((t, hd), lambda h: (0, h))
    return pl.pallas_call(
        body, name=name, grid=(HEADS,),
        in_specs=[col, pl.BlockSpec((t, hd), lambda h: (0, h + z_off // hd)), pl.BlockSpec((1, hd), lambda h: (0, 0))],
        out_specs=col, out_shape=_sds((t, HEADS * hd), bf16), compiler_params=_cparams(("parallel",)))(o, p, gain)


def _post_bwd(o, p, z_off, gain, dmix, mix_off, name):
    t = o.shape[0]
    hd = HEAD_DIM

    def body(o_ref, z_ref, g_ref, dy_ref, do_ref, dz_ref, dg_ref):
        _, vjp = jax.vjp(_post_fn, o_ref[...], z_ref[...], g_ref[...])
        do, dz, dg = vjp(dy_ref[...])
        do_ref[...] = do
        dz_ref[...] = dz.astype(bf16)

        @pl.when(pl.program_id(0) == 0)
        def _():
            dg_ref[...] = jnp.zeros_like(dg_ref)

        dg_ref[...] += dg

    col = pl.BlockSpec((t, hd), lambda h: (0, h))
    vec = pl.BlockSpec((1, hd), lambda h: (0, 0))
    return pl.pallas_call(
        body, name=name, grid=(HEADS,),
        in_specs=[col, pl.BlockSpec((t, hd), lambda h: (0, h + z_off // hd)), vec,
                  pl.BlockSpec((t, hd), lambda h: (0, h + mix_off // hd))],
        out_specs=[col, col, vec], out_shape=[_sds((t, HEADS * hd), f32), _sds((t, HEADS * hd), bf16), _sds((1, hd), f32)],
        compiler_params=_cparams(("arbitrary",)))(o, p, gain, dmix)


def _hgrn_pre_fn(qb, fb, lbw, layer):
    l0, l1 = lbw[0:1, :], lbw[1:2, :]
    m = jnp.maximum(l0, l1)
    e0, e1 = jnp.exp(l0 - m), jnp.exp(l1 - m)
    p0, p1 = e0 / (e0 + e1), e1 / (e0 + e1)
    lb = (p0 - p0) if layer == 0 else ((p0 + p1) - p0)
    f = lb + (1.0 - lb) * _sigmoid(fb)
    return _silu(qb), 1.0 - f, jnp.log(jnp.maximum(f, F_FLOOR))


def _hgrn_pre_fwd(p, lbw, layer):
    t = p.shape[0]
    tc = HEAD_DIM

    def body(qb_ref, fb_ref, lb_ref, q_ref, k_ref, lf_ref):
        q_ref[...], k_ref[...], lf_ref[...] = _hgrn_pre_fn(qb_ref[...], fb_ref[...], lb_ref[...], layer)

    col = pl.BlockSpec((t, tc), lambda j: (0, j))
    return pl.pallas_call(
        body, name="hgrn_pre_fwd", grid=(GDN_WIDTH // tc,),
        in_specs=[pl.BlockSpec((t, tc), lambda j: (0, j + OFF_QB // tc)), pl.BlockSpec((t, tc), lambda j: (0, j + OFF_FB // tc)),
                  pl.BlockSpec((2, tc), lambda j: (0, j))],
        out_specs=[col] * 3, out_shape=[_sds((t, GDN_WIDTH), f32)] * 3,
        compiler_params=_cparams(("parallel",)))(p, p, lbw)


def _hgrn_pre_bwd(p, lbw, layer, dq, dk, dlf):
    t = p.shape[0]
    tc = HEAD_DIM

    def body(qb_ref, fb_ref, lb_ref, dq_ref, dk_ref, dlf_ref, dqb_ref, dfb_ref, dlb_ref):
        _, vjp = jax.vjp(functools.partial(_hgrn_pre_fn, layer=layer), qb_ref[...], fb_ref[...], lb_ref[...])
        dqb, dfb, dlb = vjp((dq_ref[...], dk_ref[...], dlf_ref[...]))
        dqb_ref[...] = dqb.astype(bf16)
        dfb_ref[...] = dfb.astype(bf16)
        dlb_ref[...] = dlb

    col = pl.BlockSpec((t, tc), lambda j: (0, j))
    lb = pl.BlockSpec((2, tc), lambda j: (0, j))
    return pl.pallas_call(
        body, name="hgrn_pre_bwd", grid=(GDN_WIDTH // tc,),
        in_specs=[pl.BlockSpec((t, tc), lambda j: (0, j + OFF_QB // tc)), pl.BlockSpec((t, tc), lambda j: (0, j + OFF_FB // tc)),
                  lb, col, col, col],
        out_specs=[col, col, lb], out_shape=[_sds((t, GDN_WIDTH), bf16)] * 2 + [_sds((2, GDN_WIDTH), f32)],
        compiler_params=_cparams(("parallel",)))(p, p, lbw, dq, dk, dlf)


def _hgrn_step(st, q, k, lf, v):
    c = HGRN_CHUNK
    nh = q.shape[0]
    r2 = lax.broadcasted_iota(jnp.int32, (c, c), 0)
    c2 = lax.broadcasted_iota(jnp.int32, (c, c), 1)
    tri = jnp.broadcast_to((r2 >= c2).astype(f32), (nh, c, c))
    i3 = lax.broadcasted_iota(jnp.int32, (c, c, HEAD_DIM), 0)
    j3 = lax.broadcasted_iota(jnp.int32, (c, c, HEAD_DIM), 1)
    mask = i3 >= j3
    outs = []
    for n in range(q.shape[1] // c):
        sl = slice(n * c, (n + 1) * c)
        qc, kc, lc, vc = q[:, sl], k[:, sl], lf[:, sl], v[:, sl]
        b = _hdot(tri, lc, precision=HI)
        rel = jnp.where(mask, jnp.exp(jnp.where(mask, b[:, :, None, :] - b[:, None, :, :], 0.0)), 0.0)
        scores = jnp.sum(qc[:, :, None, :] * kc[:, None, :, :] * rel, axis=-1)
        bl = b[:, c - 1:c, :]
        o = _hbdot(scores, vc) + _hbdot(qc * jnp.exp(b), st, BNT)
        st = st * jnp.exp(bl) + _hbdot(vc, kc * jnp.exp(bl - b), BTN)
        outs.append(o)
    return jnp.concatenate(outs, axis=1), st


def _hgrn_core_fwd(q, k, lf, p):
    t = q.shape[0]
    hd = HEAD_DIM
    rs = min(HGRN_STEP, t)
    n = t // rs

    def body(q_ref, k_ref, lf_ref, v_ref, o_ref, st_ref, s_ref):
        @pl.when(pl.program_id(0) == 0)
        def _():
            s_ref[...] = jnp.zeros_like(s_ref)

        s = s_ref[...]
        st_ref[:, 0] = s
        o, s_new = _hgrn_step(s, *[_heads_major(r) for r in (q_ref, k_ref, lf_ref, v_ref)])
        for h in range(HEADS):
            o_ref[:, h * hd:(h + 1) * hd] = o[h]
        s_ref[...] = s_new

    tile = pl.BlockSpec((rs, GDN_WIDTH), lambda i: (i, 0))
    return pl.pallas_call(
        body, name="hgrn_core_fwd", grid=(n,),
        in_specs=[tile, tile, tile, pl.BlockSpec((rs, GDN_WIDTH), lambda i: (i, OFF_IB // GDN_WIDTH))],
        out_specs=[tile, pl.BlockSpec((HEADS, 1, hd, hd), lambda i: (0, i, 0, 0))],
        out_shape=[_sds((t, GDN_WIDTH), f32), _sds((HEADS, n, hd, hd), f32)],
        scratch_shapes=[pltpu.VMEM((HEADS, hd, hd), f32)],
        compiler_params=_cparams(("arbitrary",)))(q, k, lf, p)


def _hgrn_core_bwd(q, k, lf, p, states, do):
    t = q.shape[0]
    hd = HEAD_DIM
    rs = min(HGRN_STEP, t)
    n = t // rs

    def body(q_ref, k_ref, lf_ref, v_ref, st_ref, do_ref, dq_ref, dk_ref, dlf_ref, dv_ref, ds_ref):
        @pl.when(pl.program_id(0) == 0)
        def _():
            ds_ref[...] = jnp.zeros_like(ds_ref)

        _, vjp = jax.vjp(_hgrn_step, st_ref[:, 0], *[_heads_major(r) for r in (q_ref, k_ref, lf_ref, v_ref)])
        ds, *dins = vjp((_heads_major(do_ref), ds_ref[...]))
        ds_ref[...] = ds
        for d_ref, d in zip((dq_ref, dk_ref, dlf_ref, dv_ref), dins):
            for h in range(HEADS):
                d_ref[:, h * hd:(h + 1) * hd] = d[h].astype(d_ref.dtype)

    tile = pl.BlockSpec((rs, GDN_WIDTH), lambda i: (n - 1 - i, 0))
    return pl.pallas_call(
        body, name="hgrn_core_bwd", grid=(n,),
        in_specs=[tile, tile, tile, pl.BlockSpec((rs, GDN_WIDTH), lambda i: (n - 1 - i, OFF_IB // GDN_WIDTH)),
                  pl.BlockSpec((HEADS, 1, hd, hd), lambda i: (0, n - 1 - i, 0, 0)), tile],
        out_specs=[tile] * 4, out_shape=[_sds((t, GDN_WIDTH), f32)] * 3 + [_sds((t, GDN_WIDTH), bf16)],
        scratch_shapes=[pltpu.VMEM((HEADS, hd, hd), f32)],
        compiler_params=_cparams(("arbitrary",)))(q, k, lf, p, states, do)


def _row(v):
    return v.reshape(1, -1)


def _anchored(w, row, key):
    tok = w.get(key)
    return row if tok is None else row + tok[0, 0]


def _pad_lanes(v, n=HEAD_DIM):
    return jnp.pad(v.reshape(1, -1), ((0, 0), (0, n - v.shape[-1])))


def _ffn_fwd(x, w, l):
    h = _rms_fwd(x, _anchored(w, _row(w['norm_ffn'][l]), ('fwdf', l)), "ffn_norm")
    u = _mm(h, w['ffn_w_up'][l], tb=True, name="ffn_up")
    a = _ffn_act_fwd(u, w['ffn_conv_w'][l], _row(w['ffn_conv_b'][l]))
    y = _mm(a, w['ffn_w_down'][l], add=x, name="ffn_down")
    return y, (x, h, u)


def _ffn_bwd(saved, w, l, dy, dyb, grads):
    x, h, u = saved
    da = _mm(dyb, w['ffn_w_down'][l], tb=True, name="ffn_down_dx")
    a, dg, dv, dcw, dcb = _ffn_act_bwd(u, w['ffn_conv_w'][l], _anchored(w, _row(w['ffn_conv_b'][l]), ('bwd', l)), da)
    grads['ffn_w_down'][l] = _mm(a, dyb, ta=True, out_dtype=bf16, name="ffn_down_dw")
    du = jnp.concatenate([dg, dv], axis=1)
    grads['ffn_w_up'][l] = _mm(du, h, ta=True, out_dtype=bf16, name="ffn_up_dw")
    dh = _mm(du, w['ffn_w_up'][l], name="ffn_up_dx")
    dx, dxb, dgain = _rms_bwd(x, _row(w['norm_ffn'][l]), dh, dy, "ffn_norm_bwd")
    grads['ffn_conv_w'][l] = dcw
    grads['ffn_conv_b'][l] = dcb[0]
    grads['norm_ffn'][l] = dgain[0]
    return dx, dxb


def _odd_fwd(x, w, l, j):
    h = _rms_fwd(x, _anchored(w, _row(w['norm_mix'][l]), ('fwd', l)), "mix_norm")
    p = _mm(h, w['c_w_in'][j], name="lru_in")
    xc = _col_conv_fwd(p, LRU_WIDTH, w['c_conv_w'][j], _row(w['c_conv_b'][j]), LRU_CONV, 256, "lru_conv_fwd")
    out, hs, a = _lru_fwd(p, xc, w['c_gate_a_w'][j], _row(w['c_gate_a_b'][j]), w['c_gate_x_w'][j],
                          _row(w['c_gate_x_b'][j]), _row(w['c_lambda'][j]))
    y = _mm(out, w['c_w_out'][j], add=x, name="lru_out")
    return y, (x, h, p, xc, out, hs, a)


def _odd_bwd(saved, w, l, j, dy, dyb, grads):
    x, h, p, xc, out, hs, a = saved
    dout = _mm(dyb, w['c_w_out'][j], tb=True, name="lru_out_dx")
    grads['c_w_out'][j] = _mm(out, dyb, ta=True, out_dtype=bf16, name="lru_out_dw")
    dyb_, da, du = _lru_bwd_scan(p, a, hs, dout)
    dxc, dwa, dwx, dba, dbx, dlam = _lru_bwd_gates(xc, da, du, w['c_gate_a_w'][j], _row(w['c_gate_a_b'][j]),
                                                   w['c_gate_x_w'][j], _row(w['c_gate_x_b'][j]), _row(w['c_lambda'][j]))
    dxb_, dcw, dcb = _col_conv_bwd(p, LRU_WIDTH, w['c_conv_w'][j], dxc, LRU_CONV, 256, "lru_conv_bwd")
    dp = jnp.concatenate([dyb_, dxb_], axis=1)
    grads['c_w_in'][j] = _mm(h, dp, ta=True, out_dtype=bf16, name="lru_in_dw")
    dh = _mm(dp, w['c_w_in'][j], tb=True, name="lru_in_dx")
    dx, dxb, dgain = _rms_bwd(x, _row(w['norm_mix'][l]), dh, dy, "mix_norm_bwd")
    grads['c_gate_a_w'][j], grads['c_gate_x_w'][j] = dwa, dwx
    grads['c_gate_a_b'][j], grads['c_gate_x_b'][j], grads['c_lambda'][j] = dba[0], dbx[0], dlam[0]
    grads['c_conv_w'][j], grads['c_conv_b'][j] = dcw, dcb[0]
    grads['norm_mix'][l] = dgain[0]
    return dx, dxb


def _even_fwd(x, w, l, j):
    h = _rms_fwd(x, _anchored(w, _row(w['norm_mix'][l]), ('fwd', l)), "mix_norm")
    p = _mm(h, w['ab_w_in'][j], name="ab_in")
    alog, dtb = _pad_lanes(w['gdn_a_log'][j]), _pad_lanes(w['gdn_dt_bias'][j])
    q, k, v, be, ge = _gdn_pre_fwd(p, w['gdn_conv_w'][j], alog, dtb)
    oa, *sa = _gdn_core_fwd(q, k, v, ge, be)
    ya = _post_fwd(oa, p, OFF_Z, _row(w['gdn_norm'][j]), "gdn_post_fwd")
    qq, kk, lf = _hgrn_pre_fwd(p, w['hgrn_lower_bounds'], j)
    ob, sb = _hgrn_core_fwd(qq, kk, lf, p)
    yb = _post_fwd(ob, p, OFF_GB, _row(w['hgrn_norm'][j]), "hgrn_post_fwd")
    mix = jnp.concatenate([ya, yb], axis=1)
    y = _mm(mix, w['ab_w_out'][j], add=x, name="ab_out")
    return y, (x, h, p, q, k, v, be, ge, oa, sa, qq, kk, lf, ob, sb, mix)


def _even_bwd(saved, w, l, j, dy, dyb, grads):
    x, h, p, q, k, v, be, ge, oa, sa, qq, kk, lf, ob, sb, mix = saved
    alog, dtb = _pad_lanes(w['gdn_a_log'][j]), _pad_lanes(w['gdn_dt_bias'][j])
    dmix = _mm(dyb, w['ab_w_out'][j], tb=True, name="ab_out_dx")
    grads['ab_w_out'][j] = _mm(mix, dyb, ta=True, out_dtype=bf16, name="ab_out_dw")
    doa, dz, dgn = _post_bwd(oa, p, OFF_Z, _anchored(w, _row(w['gdn_norm'][j]), ('bwdm', l)), dmix, 0, "gdn_post_bwd")
    dob, dgb, dhn = _post_bwd(ob, p, OFF_GB, _row(w['hgrn_norm'][j]), dmix, GDN_WIDTH, "hgrn_post_bwd")
    dq, dk, dv, dge, dbe = _gdn_core_bwd(q, k, v, ge, be, sa, doa)
    dpq, dpk, dpv, dba, dwq, dwk, dwv, dal, ddt = _gdn_pre_bwd(p, w['gdn_conv_w'][j], alog, dtb, dq, dk, dv, dbe, dge)
    dqq, dkk, dlf, dib = _hgrn_core_bwd(qq, kk, lf, p, sb, dob)
    dqb, dfb, dlb = _hgrn_pre_bwd(p, w['hgrn_lower_bounds'], j, dqq, dkk, dlf)
    dp = jnp.concatenate([dpq, dpk, dpv, dz, dqb, dfb, dib, dgb, dba.astype(bf16)], axis=1)
    grads['ab_w_in'][j] = _mm(h, dp, ta=True, out_dtype=bf16, name="ab_in_dw")
    dh = _mm(dp, w['ab_w_in'][j], tb=True, name="ab_in_dx")
    dx, dxb, dgain = _rms_bwd(x, _row(w['norm_mix'][l]), dh, dy, "mix_norm_bwd")
    grads['gdn_conv_w'][j] = jnp.concatenate([dwq, dwk, dwv], axis=1)
    grads['gdn_a_log'][j], grads['gdn_dt_bias'][j] = dal[0, :HEADS], ddt[0, :HEADS]
    grads['gdn_norm'][j], grads['hgrn_norm'][j] = dgn[0], dhn[0]
    grads['hgrn_lower_bounds'].append(dlb)
    grads['norm_mix'][l] = dgain[0]
    return dx, dxb


def _ab_permute(w_in):
    pad = jnp.zeros(w_in.shape[:-1] + (AB_PAD - AB_COLS,), w_in.dtype)
    return jnp.concatenate([w_in[..., :2048], w_in[..., 2056:], w_in[..., 2048:2056], pad], axis=-1)


def _ab_unpermute(g):
    return jnp.concatenate([g[..., :2048], g[..., 4096:4104], g[..., 2048:4096]], axis=-1)


def _block_pad(a, axis, nblk, padded):
    axis = axis % a.ndim
    s = a.shape
    a = a.reshape(s[:axis] + (nblk, s[axis] // nblk) + s[axis + 1:])
    pad = [(0, 0)] * a.ndim
    pad[axis + 1] = (0, padded - s[axis] // nblk)
    return jnp.pad(a, pad).reshape(s[:axis] + (nblk * padded,) + s[axis + 1:])


def _block_unpad(a, axis, nblk, width):
    axis = axis % a.ndim
    s = a.shape
    a = a.reshape(s[:axis] + (nblk, s[axis] // nblk) + s[axis + 1:])
    a = lax.slice_in_dim(a, 0, width, axis=axis + 1)
    return a.reshape(s[:axis] + (nblk * width,) + s[axis + 1:])


def _kernel_layout(w):
    w = dict(w)
    w['ab_w_in'] = _ab_permute(w['ab_w_in'])
    w['ffn_w_up'] = jnp.swapaxes(_block_pad(w['ffn_w_up'], 2, N_DEV, FF_PAD), 1, 2)
    w['ffn_w_down'] = _block_pad(w['ffn_w_down'], 1, 4, FF_PAD)
    w['ffn_conv_w'] = _block_pad(w['ffn_conv_w'], 2, 4, FF_PAD)
    w['ffn_conv_b'] = _block_pad(w['ffn_conv_b'], 1, 4, FF_PAD)
    return w


def _natural_grads(g):
    g = dict(g)
    g['ab_w_in'] = _ab_unpermute(g['ab_w_in'])
    g['ffn_w_up'] = _block_unpad(jnp.swapaxes(g['ffn_w_up'], 1, 2), 2, N_DEV, FF_SHARD)
    g['ffn_w_down'] = _block_unpad(g['ffn_w_down'], 1, 4, FF_SHARD)
    g['ffn_conv_w'] = _block_unpad(g['ffn_conv_w'], 2, 4, FF_SHARD)
    g['ffn_conv_b'] = _block_unpad(g['ffn_conv_b'], 1, 4, FF_SHARD)
    return g


def _local_step(x, target, w, fetch=None, push=None):
    grads = {n: [None] * (DEPTH if n in ('norm_mix', 'norm_ffn') or n.startswith('ffn_') else 2)
             for n in WEIGHTS if n not in ('norm_final', 'hgrn_lower_bounds')}
    grads['hgrn_lower_bounds'] = []
    saved = []
    for l in range(DEPTH):
        j = l // 2
        if fetch is not None:
            fetch(l, x, 'mix')
        x, s_mix = (_even_fwd if l % 2 == 0 else _odd_fwd)(x, w, l, j)
        if fetch is not None:
            fetch(l, x, 'ffn')
        x, s_ffn = _ffn_fwd(x, w, l)
        saved.append((s_mix, s_ffn))
    loss, dx, dxb, dgf = _loss_head(x, _row(w['norm_final']), target)
    for l in reversed(range(DEPTH)):
        j = l // 2
        s_mix, s_ffn = saved[l]
        dx, dxb = _ffn_bwd(s_ffn, w, l, dx, dxb, grads)
        if push is not None:
            push(l, 'ffn', {nm: grads[nm].pop(li) for nm, li in _layer_items(l)[-2:]})
        dx, dxb = (_even_bwd if l % 2 == 0 else _odd_bwd)(s_mix, w, l, j, dx, dxb, grads)
        if push is not None:
            push(l, 'mix', {nm: grads[nm].pop(li) for nm, li in _layer_items(l)[:-2]})
    out = {n: jnp.stack(g) for n, g in grads.items() if n != 'hgrn_lower_bounds' and g}
    out['hgrn_lower_bounds'] = grads['hgrn_lower_bounds'][0] + grads['hgrn_lower_bounds'][1]
    out['norm_final'] = dgf[0]
    return loss[0, 0], dx, out


def _position():
    return lax.axis_index("x"), lax.axis_index("y"), lax.axis_index("c")


BLOCK_LAYOUT = {
    'ab_w_in': ((2, D_MODEL, N_DEV * AB_SHARD_PAD), (2, D_MODEL, AB_SHARD_PAD)),
    'ab_w_out': ((2, N_DEV, 128, D_MODEL), (2, 128, D_MODEL)),
    'c_w_in': ((2, D_MODEL, 2 * LRU_WIDTH), (2, D_MODEL, 256)),
    'c_w_out': ((2, N_DEV, 128, D_MODEL), (2, 128, D_MODEL)),
    'c_gate_a_w': ((2, HEADS, N_DEV, 32, LRU_BLOCK), (2, HEADS, 32, LRU_BLOCK)),
    'c_gate_x_w': ((2, HEADS, N_DEV, 32, LRU_BLOCK), (2, HEADS, 32, LRU_BLOCK)),
    'ffn_w_up': ((DEPTH, N_DEV, FF_PAD, D_MODEL), (DEPTH, FF_PAD, D_MODEL)),
    'ffn_w_down': ((DEPTH, 4, FF_PAD, D_MODEL), (DEPTH, FF_ROWS, D_MODEL)),
}


COL_WINDOW = {'ab_w_in': AB_SHARD_PAD, 'c_w_in': 256}


def _block_index(name, p):
    d = 4 * p[0] + 2 * p[1] + p[2]
    if name in COL_WINDOW:
        return (slice(None), pl.ds(pl.multiple_of(d * COL_WINDOW[name], 128), COL_WINDOW[name]))
    if name == 'ffn_w_down':
        return (2 * p[0] + p[1], pl.ds(pl.multiple_of(p[2] * FF_ROWS, 16), FF_ROWS), slice(None))
    if name in ('c_gate_a_w', 'c_gate_x_w'):
        return (slice(None), d)
    return (d,)


def _block_of(name, ref, p, layered=True):
    idx = _block_index(name, p)
    if layered and name in BLOCK_LAYOUT:
        idx = (slice(None),) + idx
    return ref.at[idx]


def _layer_items(l):
    j = l // 2
    mix = ([('ab_w_in', j), ('ab_w_out', j)] if l % 2 == 0 else
           [('c_w_in', j), ('c_w_out', j), ('c_gate_a_w', j), ('c_gate_x_w', j)])
    return mix + [('ffn_w_up', l), ('ffn_w_down', l)]


def _own_land(name, shard_l, pos):
    x, y, c = pos
    d = 4 * x + 2 * y + c
    shape = BLOCK_LAYOUT[name][0][1:] if name in BLOCK_LAYOUT else (N_DEV,) + shard_l.shape
    zeros = jnp.zeros(shape, shard_l.dtype) if name == 'ffn_w_down' else lax.empty(shape, shard_l.dtype)
    if name in COL_WINDOW:
        return lax.dynamic_update_slice(zeros, shard_l, (0, d * COL_WINDOW[name]))
    if name == 'ffn_w_down':
        return lax.dynamic_update_slice(zeros, shard_l[None], (2 * x + y, c * FF_ROWS, 0))
    if name in ('c_gate_a_w', 'c_gate_x_w'):
        return lax.dynamic_update_slice(zeros, shard_l[:, None], (0, d, 0, 0))
    return lax.dynamic_update_slice(zeros, shard_l[None], (d,) + (0,) * shard_l.ndim)


def _place_own(items, shards, posv, name):
    n = len(items)
    down = [i for i, (nm, _) in enumerate(items) if nm == 'ffn_w_down']
    in_specs, out_specs, out_shapes, operands = [], [], [], []
    for nm, li in items:
        sh = shards[nm]
        shard_shape = sh.shape if li is None else sh.shape[1:]
        z = (0,) * len(shard_shape)
        operands.append(sh)
        in_specs.append(pl.BlockSpec(shard_shape, lambda i, d, q, c, z=z: z) if li is None else
                        pl.BlockSpec((1,) + shard_shape, lambda i, d, q, c, li=li, z=z: (li,) + z))
        out_shapes.append(_sds(BLOCK_LAYOUT[nm][0][1:] if nm in BLOCK_LAYOUT else (N_DEV,) + sh.shape, sh.dtype))
        if nm in COL_WINDOW:
            out_specs.append(pl.BlockSpec(shard_shape, lambda i, d, q, c: (0, d[0])))
        elif nm == 'ffn_w_down':
            out_specs.append(pl.BlockSpec((1,) + shard_shape, lambda i, d, q, c: (q[0], c[0], 0)))
        elif nm in ('c_gate_a_w', 'c_gate_x_w'):
            out_specs.append(pl.BlockSpec((HEADS, 1) + shard_shape[1:], lambda i, d, q, c: (0, d[0], 0, 0)))
        else:
            out_specs.append(pl.BlockSpec((1,) + shard_shape, lambda i, d, q, c, z=z: (d[0],) + z))

    def body(d_ref, q_ref, c_ref, *refs):
        for i, (nm, li) in enumerate(items):
            v = refs[i][...] if li is None else refs[i][0]
            o_ref = refs[n + len(down) + i]
            if nm in COL_WINDOW:
                o_ref[...] = v
            elif nm in ('c_gate_a_w', 'c_gate_x_w'):
                o_ref[:, 0] = v
            else:
                o_ref[0] = v

    zeros = [jnp.zeros(out_shapes[i].shape, out_shapes[i].dtype) for i in down]
    return pl.pallas_call(
        body, name=name, out_shape=out_shapes,
        grid_spec=pltpu.PrefetchScalarGridSpec(
            num_scalar_prefetch=3, grid=(1,), in_specs=in_specs + [pl.BlockSpec(memory_space=pl.ANY)] * len(down),
            out_specs=out_specs),
        input_output_aliases={3 + n + k: i for k, i in enumerate(down)},
        compiler_params=_cparams(("arbitrary",)))(*posv, *operands, *zeros)


def _src_of(shard_ref, li):
    return shard_ref if li is None else shard_ref.at[li]


def _gather_now(items, shards, lands):
    n = len(items)
    srcs = sorted({nm for nm, _ in items})

    def body(*refs):
        ins = dict(zip(srcs, refs[:len(srcs)]))
        outs = refs[len(srcs) + n:len(srcs) + 2 * n]
        send_sems, recv_sems = refs[len(srcs) + 2 * n:]
        x, y, c = _position()
        me, sibling = (x, y, c), (x, y, 1 - c)
        chips = [(1 - x, y), (x, 1 - y), (1 - x, 1 - y)]

        def copy(i, k, block, to, own=False):
            nm, li = items[i]
            dst = _block_of(nm, outs[i], block, layered=False)
            return pltpu.make_async_remote_copy(
                src_ref=_src_of(ins[nm], li) if own else dst, dst_ref=dst, send_sem=send_sems.at[7 * i + k],
                recv_sem=recv_sems.at[7 * i + k], device_id=to, device_id_type=MESH)

        first = []
        for i in range(n):
            first.append(copy(i, 0, me, sibling, own=True))
            first += [copy(i, 1 + j, me, (*chip, c), own=True) for j, chip in enumerate(chips)]
        for cp in first:
            cp.start()
        passed = []
        for j, chip in enumerate(chips):
            for i in range(n):
                copy(i, 1 + j, (*chip, c), me).wait_recv()
                fwd = copy(i, 4 + j, (*chip, c), sibling)
                fwd.start()
                passed.append(fwd)
        for i in range(n):
            copy(i, 0, sibling, me).wait_recv()
        for j, chip in enumerate(chips):
            for i in range(n):
                copy(i, 4 + j, (*chip, 1 - c), me).wait_recv()
        for cp in first + passed:
            cp.wait_send()

    any_spec = pl.BlockSpec(memory_space=pl.ANY)
    return pl.pallas_call(
        body, name="gather_first_layer", out_shape=[_sds(a.shape, a.dtype) for a in lands],
        in_specs=[any_spec] * (len(srcs) + n), out_specs=[any_spec] * n,
        input_output_aliases={len(srcs) + i: i for i in range(n)},
        scratch_shapes=[pltpu.SemaphoreType.DMA((7 * n,)), pltpu.SemaphoreType.DMA((7 * n,))],
    )(*[shards[nm] for nm in srcs], *lands)


FIRST_HOP = (1, 2, 4, 6)


def _lanes(name, land_ref, pos):
    if name == 'ffn_w_down':
        return [(FIRST_HOP, land_ref.at[pl.ds(0, 2), pl.ds(0, 2 * FF_ROWS)])]
    if name in COL_WINDOW:
        return [(FIRST_HOP, land_ref.at[:, pl.ds(0, 4 * COL_WINDOW[name])])]
    if name in ('c_gate_a_w', 'c_gate_x_w'):
        return [(FIRST_HOP, land_ref.at[:, pl.ds(0, 4)])]
    return [(FIRST_HOP, land_ref.at[pl.ds(0, 4)])]


def _n_lanes(items):
    return len(items)


def _gather_forward(items, lands, name):
    n = len(items)

    def body(*refs):
        outs = refs[n:2 * n]
        send_sems, recv_sems = refs[2 * n:]
        x, y, c = _position()
        chips = [(1 - x, y), (x, 1 - y), (1 - x, 1 - y)]
        copies, arrivals = [], []
        for i, (nm, _) in enumerate(items):
            for j, chip in enumerate(chips):
                mine = _block_of(nm, outs[i], (*chip, c), layered=False)
                theirs = _block_of(nm, outs[i], (*chip, 1 - c), layered=False)
                copies.append(pltpu.make_async_remote_copy(
                    src_ref=mine, dst_ref=mine, send_sem=send_sems.at[3 * i + j], recv_sem=recv_sems.at[3 * i + j],
                    device_id=(x, y, 1 - c), device_id_type=MESH))
                arrivals.append(pltpu.make_async_remote_copy(
                    src_ref=theirs, dst_ref=theirs, send_sem=send_sems.at[3 * i + j], recv_sem=recv_sems.at[3 * i + j],
                    device_id=(x, y, 1 - c), device_id_type=MESH))
        for cp in copies:
            cp.start()
        for cp in arrivals:
            cp.wait_recv()
        for cp in copies:
            cp.wait_send()

    any_spec = pl.BlockSpec(memory_space=pl.ANY)
    return pl.pallas_call(
        body, name=name, out_shape=[_sds(a.shape, a.dtype) for a in lands],
        in_specs=[any_spec] * n, out_specs=[any_spec] * n, input_output_aliases={i: i for i in range(n)},
        scratch_shapes=[pltpu.SemaphoreType.DMA((3 * n,)), pltpu.SemaphoreType.DMA((3 * n,))],
    )(*lands)


HBM_SPEC = pl.BlockSpec(memory_space=pltpu.HBM)
SEM_SPEC = pl.BlockSpec(memory_space=pltpu.SEMAPHORE)
SIDE_EFFECT = pltpu.SideEffectType.DATAFLOW_SIDE_EFFECTING


def _gather_start(items, shards, lands, token, name):
    n = len(items)
    srcs = sorted({nm for nm, _ in items})
    ns, nl = len(srcs), _n_lanes(items)

    def body(*refs):
        ins = dict(zip(srcs, refs[:ns]))
        land_refs = refs[ns:ns + n]
        sems = refs[ns + n + 1:ns + n + 1 + 2 * nl]
        x, y, c = _position()
        me = (x, y, c)
        lane = 0
        for i, (nm, li) in enumerate(items):
            for codes, _ in _lanes(nm, land_refs[i], me):
                for k in codes:
                    peer = (1 - x if (k >> 2) & 1 else x, 1 - y if (k >> 1) & 1 else y, 1 - c if k & 1 else c)
                    pltpu.make_async_remote_copy(
                        src_ref=_src_of(ins[nm], li), dst_ref=_block_of(nm, land_refs[i], me, layered=False),
                        send_sem=sems[2 * lane], recv_sem=sems[2 * lane + 1], device_id=peer, device_id_type=MESH).start()
                lane += 1
        refs[-1][...] = jnp.zeros((8, 128), f32)

    hbm = [pltpu.with_memory_space_constraint(a, pltpu.HBM) for a in [shards[nm] for nm in srcs] + list(lands)]
    outs = pl.pallas_call(
        body, name=name,
        out_shape=[pltpu.SemaphoreType.DMA(())] * (2 * nl) + [pltpu.HBM(a.shape, a.dtype) for a in hbm] + [_sds((8, 128), f32)],
        in_specs=[HBM_SPEC] * (ns + n) + [pl.BlockSpec(memory_space=pl.ANY)],
        out_specs=[SEM_SPEC] * (2 * nl) + [HBM_SPEC] * (ns + n) + [pl.BlockSpec(memory_space=pltpu.VMEM)],
        input_output_aliases={i: 2 * nl + i for i in range(ns + n)},
        compiler_params=pltpu.CompilerParams(has_side_effects=SIDE_EFFECT),
    )(*hbm, token)
    return outs[:2 * nl], dict(zip(srcs, outs[2 * nl:2 * nl + ns])), outs[2 * nl + ns:-1], outs[-1]


def _gather_wait(items, sems, shards, lands, after, name):
    n = len(items)
    srcs = sorted(shards)
    ns, nl = len(srcs), _n_lanes(items)

    def body(*refs):
        land_refs = refs[ns:ns + n]
        sem_refs = refs[ns + n:ns + n + 2 * nl]
        x, y, c = _position()
        lane = 0
        for i, (nm, _) in enumerate(items):
            for _, moved in _lanes(nm, land_refs[i], (x, y, c)):
                cp = pltpu.make_async_remote_copy(
                    src_ref=moved, dst_ref=moved, send_sem=sem_refs[2 * lane], recv_sem=sem_refs[2 * lane + 1],
                    device_id=(x, y, 1 - c), device_id_type=MESH)
                cp.wait_send()
                cp.wait_recv()
                lane += 1

    outs = pl.pallas_call(
        body, name=name, out_shape=[pltpu.HBM(shards[nm].shape, shards[nm].dtype) for nm in srcs]
        + [pltpu.HBM(a.shape, a.dtype) for a in lands],
        in_specs=[HBM_SPEC] * (ns + n) + [SEM_SPEC] * (2 * nl) + [pl.BlockSpec(memory_space=pl.ANY)],
        out_specs=[HBM_SPEC] * (ns + n), input_output_aliases={i: i for i in range(ns + n)},
        compiler_params=pltpu.CompilerParams(has_side_effects=SIDE_EFFECT),
    )(*[shards[nm] for nm in srcs], *lands, *sems, after)
    return dict(zip(srcs, outs[:ns])), outs[ns:]


def _exchange_grads(fulls, rep):
    cpos = lax.axis_index("c").astype(jnp.int32).reshape(1)
    pair, rep_pair = _pair_exchange(fulls, rep)
    chip = {nm: _chip_sum(nm, fulls[nm], pair[nm], cpos) for nm in fulls}
    rep_chip = _add_pair(rep, rep_pair, "chip_sum_replicated")
    cross, cross_rep = _cross_exchange(chip, rep_chip)
    return chip, rep_chip, cross, cross_rep


def _pair_exchange(fulls, rep, tag=""):
    names = list(fulls)
    n = len(names)
    shard_shape = {nm: ((fulls[nm].shape[0],) + BLOCK_LAYOUT[nm][1][1:] if nm in BLOCK_LAYOUT else fulls[nm].shape[1:])
                   for nm in names}

    def body(*refs):
        ins = dict(zip(names, refs[:n]))
        rep_ref = refs[n]
        pair = dict(zip(names, refs[n + 1:2 * n + 1]))
        rpair_ref = refs[2 * n + 1]
        send_sems, recv_sems = refs[2 * n + 2:]
        x, y, c = _position()
        sibling = (x, y, 1 - c)
        remote = []
        for i, nm in enumerate(names):
            for q in range(4):
                remote.append(pltpu.make_async_remote_copy(
                    src_ref=_block_of(nm, ins[nm], (q >> 1, q & 1, 1 - c)), dst_ref=pair[nm].at[q],
                    send_sem=send_sems.at[4 * i + q], recv_sem=recv_sems.at[4 * i + q], device_id=sibling,
                    device_id_type=MESH))
        remote.append(pltpu.make_async_remote_copy(
            src_ref=rep_ref, dst_ref=rpair_ref, send_sem=send_sems.at[4 * n], recv_sem=recv_sems.at[4 * n],
            device_id=sibling, device_id_type=MESH))
        for cp in remote:
            cp.start()
        for cp in remote:
            cp.wait_recv()
        for cp in remote:
            cp.wait_send()

    any_spec = pl.BlockSpec(memory_space=pl.ANY)
    four = [_sds((4,) + tuple(shard_shape[nm]), fulls[nm].dtype) for nm in names]
    outs = pl.pallas_call(
        body, name="grad_pair_exchange" + tag, out_shape=four + [_sds(rep.shape, rep.dtype)],
        in_specs=[any_spec] * (n + 1), out_specs=[any_spec] * (n + 1),
        scratch_shapes=[pltpu.SemaphoreType.DMA((4 * n + 1,)), pltpu.SemaphoreType.DMA((4 * n + 1,))],
    )(*[fulls[nm] for nm in names], rep)
    return dict(zip(names, outs[:n])), outs[n]


def _chip_sum(name, full, pair, cpos, tag=""):
    if name in COL_WINDOW:
        width = BLOCK_LAYOUT[name][1][-1]
        rows = full.shape[0] * full.shape[1]
        tr = 512

        def body(c_ref, f_ref, p_ref, o_ref):
            o_ref[0] = (f_ref[...].astype(f32) + p_ref[0].astype(f32)).astype(o_ref.dtype)

        slot = pl.BlockSpec((1, tr, width), lambda q, i, c: (q, i, 0))
        out = pl.pallas_call(
            body, name="chip_sum_" + name + tag, out_shape=_sds((4, rows, width), full.dtype),
            grid_spec=pltpu.PrefetchScalarGridSpec(
                num_scalar_prefetch=1, grid=(4, rows // tr),
                in_specs=[pl.BlockSpec((tr, width), lambda q, i, c: (i, 2 * q + c[0])), slot], out_specs=slot),
            compiler_params=_cparams(("parallel", "parallel")))(
            cpos, full.reshape(rows, N_DEV * width), pair.reshape(4, rows, width))
        return out.reshape(pair.shape)

    if name == 'ffn_w_down':
        f4, p4 = full, pair
        fspec = pl.BlockSpec((full.shape[0], 1, FF_ROWS, D_MODEL), lambda q, c: (0, q, c[0], 0))
    else:
        shard = pair.shape[1:]
        lead = int(np.prod(shard[:-2]))
        f4 = full.reshape((lead, N_DEV) + shard[-2:])
        p4 = pair.reshape((4, lead) + shard[-2:])
        fspec = pl.BlockSpec((lead, 1) + shard[-2:], lambda q, c: (0, 2 * q + c[0], 0, 0))

    def body4(c_ref, f_ref, p_ref, o_ref):
        o_ref[0] = (f_ref[:, 0].astype(f32) + p_ref[0].astype(f32)).astype(o_ref.dtype)

    slot = pl.BlockSpec((1,) + p4.shape[1:], lambda q, c: (q, 0, 0, 0))
    out = pl.pallas_call(
        body4, name="chip_sum_" + name + tag, out_shape=_sds(p4.shape, full.dtype),
        grid_spec=pltpu.PrefetchScalarGridSpec(num_scalar_prefetch=1, grid=(4,), in_specs=[fspec, slot], out_specs=slot),
        compiler_params=_cparams(("parallel",)))(cpos, f4, p4)
    return out.reshape(pair.shape)


def _add_pair(a, b, name):
    shp = a.shape
    r, c = int(np.prod(shp[:-1])), shp[-1]
    tr = _tile(r, (512, 256, 128, 64, 32, 16, 8))

    def body(a_ref, b_ref, o_ref):
        o_ref[...] = (a_ref[...].astype(f32) + b_ref[...].astype(f32)).astype(o_ref.dtype)

    tile = pl.BlockSpec((tr, c), lambda i: (i, 0))
    return pl.pallas_call(body, name=name, grid=(r // tr,), in_specs=[tile, tile], out_specs=tile,
                          out_shape=_sds((r, c), a.dtype), compiler_params=_cparams(("parallel",)))(
        a.reshape(r, c), b.reshape(r, c)).reshape(shp)


def _cross_exchange(chip, rep_chip):
    names = list(chip)
    n = len(names)

    def body(*refs):
        ins = dict(zip(names, refs[:n]))
        rep_ref = refs[n]
        outs = dict(zip(names, refs[2 * n + 2:3 * n + 2]))
        rrep_ref = refs[3 * n + 2]
        send_sems, recv_sems = refs[3 * n + 3:]
        x, y, c = _position()
        mine = 2 * x + y
        copies = []
        for k in range(1, 4):
            px, py = (1 - x if (k >> 1) & 1 else x), (1 - y if k & 1 else y)
            for i, nm in enumerate(names + ['']):
                src = rep_ref if i == n else ins[nm].at[2 * px + py]
                dst = (rrep_ref if i == n else outs[nm]).at[mine]
                copies.append(pltpu.make_async_remote_copy(
                    src_ref=src, dst_ref=dst, send_sem=send_sems.at[3 * i + k - 1], recv_sem=recv_sems.at[3 * i + k - 1],
                    device_id=(px, py, c), device_id_type=MESH))
        for cp in copies:
            cp.start()
        for cp in copies:
            cp.wait_recv()
        for cp in copies:
            cp.wait_send()

    any_spec = pl.BlockSpec(memory_space=pl.ANY)
    shapes = [_sds(chip[nm].shape, chip[nm].dtype) for nm in names] + [_sds((4,) + rep_chip.shape, rep_chip.dtype)]
    zeros = [jnp.zeros(s.shape, s.dtype) for s in shapes]
    outs = pl.pallas_call(
        body, name="grad_cross_exchange", out_shape=shapes,
        in_specs=[any_spec] * (2 * n + 2), out_specs=[any_spec] * (n + 1),
        input_output_aliases={n + 1 + i: i for i in range(n + 1)},
        scratch_shapes=[pltpu.SemaphoreType.DMA((3 * (n + 1),)), pltpu.SemaphoreType.DMA((3 * (n + 1),))],
    )(*[chip[nm] for nm in names], rep_chip, *zeros)
    return dict(zip(names, outs[:n])), outs[n]


def _cross_start(chips, name):
    n = len(chips)

    def body(*refs):
        chip_refs, land_refs = refs[:n], refs[n:2 * n]
        sems = refs[2 * n:4 * n]
        x, y, c = _position()
        mine = 2 * x + y
        for i in range(n):
            for k in range(1, 4):
                px, py = (1 - x if (k >> 1) & 1 else x), (1 - y if k & 1 else y)
                pltpu.make_async_remote_copy(
                    src_ref=chip_refs[i].at[2 * px + py], dst_ref=land_refs[i].at[mine], send_sem=sems[2 * i],
                    recv_sem=sems[2 * i + 1], device_id=(px, py, c), device_id_type=MESH).start()
        refs[-1][...] = jnp.zeros((8, 128), f32)

    hbm = [pltpu.with_memory_space_constraint(a, pltpu.HBM) for a in list(chips) + [jnp.zeros(a.shape, a.dtype) for a in chips]]
    outs = pl.pallas_call(
        body, name=name,
        out_shape=[pltpu.SemaphoreType.DMA(())] * (2 * n) + [pltpu.HBM(a.shape, a.dtype) for a in hbm] + [_sds((8, 128), f32)],
        in_specs=[HBM_SPEC] * (2 * n),
        out_specs=[SEM_SPEC] * (2 * n) + [HBM_SPEC] * (2 * n) + [pl.BlockSpec(memory_space=pltpu.VMEM)],
        input_output_aliases={i: 2 * n + i for i in range(2 * n)},
        compiler_params=pltpu.CompilerParams(has_side_effects=SIDE_EFFECT),
    )(*hbm)
    return outs[:2 * n], outs[2 * n:3 * n], outs[3 * n:4 * n], outs[4 * n]


def _cross_wait(sems, chips, lands, after, name):
    n = len(chips)

    def body(*refs):
        land_refs = refs[n:2 * n]
        sem_refs = refs[2 * n:4 * n]
        x, y, c = _position()
        for i in range(n):
            moved = land_refs[i].at[pl.ds(0, 3)]
            cp = pltpu.make_async_remote_copy(
                src_ref=moved, dst_ref=moved, send_sem=sem_refs[2 * i], recv_sem=sem_refs[2 * i + 1],
                device_id=(x, y, 1 - c), device_id_type=MESH)
            cp.wait_send()
            cp.wait_recv()

    outs = pl.pallas_call(
        body, name=name, out_shape=[pltpu.HBM(a.shape, a.dtype) for a in list(chips) + list(lands)],
        in_specs=[HBM_SPEC] * (2 * n) + [SEM_SPEC] * (2 * n) + [pl.BlockSpec(memory_space=pl.ANY)],
        out_specs=[HBM_SPEC] * (2 * n), input_output_aliases={i: i for i in range(2 * n)},
        compiler_params=pltpu.CompilerParams(has_side_effects=SIDE_EFFECT),
    )(*chips, *lands, *sems, after)
    return outs[:n], outs[n:]


def _sum_adamw_layer(parts, own, mine, w, m, v, li, prev, name):
    nl, r, l = w.shape
    lp = parts.shape[2]
    tr = r if r <= 512 else _tile(r, (512, FF_ROWS, 256, 128))
    c1 = 1.0 / (1.0 - ADAM_B1 ** ADAM_STEP)
    c2 = 1.0 / (1.0 - ADAM_B2 ** ADAM_STEP)
    k = 0 if prev is None else 4

    def body(mine_ref, p_ref, o_ref, w_ref, m_ref, v_ref, *rest):
        g_ref, d_ref, nm_ref, nv_ref = rest[k:]
        mine_v = o_ref[0].astype(f32)
        g = jnp.where(mine_ref[0] == 0, mine_v, p_ref[0].astype(f32))
        for s in range(1, 4):
            g = g + jnp.where(mine_ref[0] == s, mine_v, p_ref[s].astype(f32))
        if lp != l:
            g = g[:, :l]
        m_new = ADAM_B1 * m_ref[0] + (1.0 - ADAM_B1) * g
        v_new = ADAM_B2 * v_ref[0] + (1.0 - ADAM_B2) * (g * g)
        g_ref[0] = g
        nm_ref[0] = m_new
        nv_ref[0] = v_new
        d_ref[0] = -ADAM_LR * ((m_new * c1) / (jnp.sqrt(v_new * c2) + ADAM_EPS) + ADAM_WD * w_ref[0])

    tile = pl.BlockSpec((1, tr, l), lambda i, mn: (li, i, 0))
    keep = [pl.BlockSpec(memory_space=pl.ANY)] * k
    return pl.pallas_call(
        body, name=name, out_shape=[_sds((nl, r, l), f32)] * 4,
        grid_spec=pltpu.PrefetchScalarGridSpec(
            num_scalar_prefetch=1, grid=(r // tr,),
            in_specs=[pl.BlockSpec((4, tr, lp), lambda i, mn: (0, i, 0)), pl.BlockSpec((1, tr, lp), lambda i, mn: (mn[0], i, 0)),
                      tile, tile, tile] + keep,
            out_specs=[tile] * 4),
        input_output_aliases={6 + i: i for i in range(k)},
        compiler_params=_cparams(("parallel",)))(mine, parts, own, w, m, v, *(prev or ()))


def _sum_adamw(parts, own, mine, w, m, v, name):
    r, l = w.shape
    lp = parts.shape[2]
    tr = _tile(r, (256, 128, 64, 32, 16, 8))
    c1 = 1.0 / (1.0 - ADAM_B1 ** ADAM_STEP)
    c2 = 1.0 / (1.0 - ADAM_B2 ** ADAM_STEP)

    def body(mine_ref, p_ref, o_ref, w_ref, m_ref, v_ref, g_ref, d_ref, nm_ref, nv_ref):
        mine_v = (o_ref[0] if own.ndim == 3 else o_ref[...]).astype(f32)
        g = jnp.where(mine_ref[0] == 0, mine_v, p_ref[0].astype(f32))
        for s in range(1, parts.shape[0]):
            g = g + jnp.where(mine_ref[0] == s, mine_v, p_ref[s].astype(f32))
        if lp != l:
            g = g[:, :l]
        m_new = ADAM_B1 * m_ref[...] + (1.0 - ADAM_B1) * g
        v_new = ADAM_B2 * v_ref[...] + (1.0 - ADAM_B2) * (g * g)
        g_ref[...] = g
        nm_ref[...] = m_new
        nv_ref[...] = v_new
        d_ref[...] = -ADAM_LR * ((m_new * c1) / (jnp.sqrt(v_new * c2) + ADAM_EPS) + ADAM_WD * w_ref[...])

    tile = pl.BlockSpec((tr, l), lambda i, mn: (i, 0))
    own_spec = (pl.BlockSpec((1, tr, lp), lambda i, mn: (mn[0], i, 0)) if own.ndim == 3
                else pl.BlockSpec((tr, lp), lambda i, mn: (i, 0)))
    return pl.pallas_call(
        body, name=name, out_shape=[_sds((r, l), f32)] * 4,
        grid_spec=pltpu.PrefetchScalarGridSpec(
            num_scalar_prefetch=1, grid=(r // tr,),
            in_specs=[pl.BlockSpec((parts.shape[0], tr, lp), lambda i, mn: (0, i, 0)), own_spec, tile, tile, tile],
            out_specs=[tile] * 4),
        compiler_params=_cparams(("parallel",)))(mine, parts, own, w, m, v)


def _pack(arrs, lead=None):
    if lead is None:
        flat = jnp.concatenate([a.reshape(-1).astype(f32) for a in arrs])
        n = flat.shape[0]
    else:
        flat = jnp.concatenate([a.reshape(lead, -1).astype(f32) for a in arrs], axis=1)
        n = flat.shape[1]
    tot = -(-n // 1024) * 1024
    if lead is None:
        return jnp.pad(flat, (0, tot - n)).reshape(tot // 128, 128)
    return jnp.pad(flat, ((0, 0), (0, tot - n))).reshape(lead, tot // 128, 128)


def _unpack(packed, shapes, lead=False):
    flat = packed.reshape(packed.shape[0], -1) if lead else packed.reshape(-1)
    out, off = [], 0
    for s in shapes:
        n = int(np.prod(s))
        out.append(flat[:, off:off + n].reshape((packed.shape[0],) + tuple(s)) if lead else flat[off:off + n].reshape(s))
        off += n
    return out


def _merge_shards(g, axis):
    g = jnp.moveaxis(g, 0, axis)
    s = g.shape
    return g.reshape(s[:axis] + (s[axis] * s[axis + 1],) + s[axis + 2:])


def _split_shards(full, axis):
    s = full.shape
    g = full.reshape(s[:axis] + (N_DEV, s[axis] // N_DEV) + s[axis + 1:])
    return jnp.moveaxis(g, axis, 0)


def kernel(x, norm_mix, norm_ffn, norm_final, ab_w_in, gdn_conv_w, gdn_a_log, gdn_dt_bias, gdn_norm, hgrn_lower_bounds, hgrn_norm, ab_w_out, c_w_in, c_conv_w, c_conv_b, c_gate_a_w, c_gate_a_b, c_gate_x_w, c_gate_x_b, c_lambda, c_w_out, ffn_w_up, ffn_conv_w, ffn_conv_b, ffn_w_down, loss_target, m_norm_mix, m_norm_ffn, m_norm_final, m_ab_w_in, m_gdn_conv_w, m_gdn_a_log, m_gdn_dt_bias, m_gdn_norm, m_hgrn_lower_bounds, m_hgrn_norm, m_ab_w_out, m_c_w_in, m_c_conv_w, m_c_conv_b, m_c_gate_a_w, m_c_gate_a_b, m_c_gate_x_w, m_c_gate_x_b, m_c_lambda, m_c_w_out, m_ffn_w_up, m_ffn_conv_w, m_ffn_conv_b, m_ffn_w_down, v_norm_mix, v_norm_ffn, v_norm_final, v_ab_w_in, v_gdn_conv_w, v_gdn_a_log, v_gdn_dt_bias, v_gdn_norm, v_hgrn_lower_bounds, v_hgrn_norm, v_ab_w_out, v_c_w_in, v_c_conv_w, v_c_conv_b, v_c_gate_a_w, v_c_gate_a_b, v_c_gate_x_w, v_c_gate_x_b, v_c_lambda, v_c_w_out, v_ffn_w_up, v_ffn_conv_w, v_ffn_conv_b, v_ffn_w_down):
    wl = dict(zip(WEIGHTS, (norm_mix, norm_ffn, norm_final, ab_w_in, gdn_conv_w, gdn_a_log, gdn_dt_bias, gdn_norm, hgrn_lower_bounds, hgrn_norm, ab_w_out, c_w_in, c_conv_w, c_conv_b, c_gate_a_w, c_gate_a_b, c_gate_x_w, c_gate_x_b, c_lambda, c_w_out, ffn_w_up, ffn_conv_w, ffn_conv_b, ffn_w_down)))
    ml = dict(zip(WEIGHTS, (m_norm_mix, m_norm_ffn, m_norm_final, m_ab_w_in, m_gdn_conv_w, m_gdn_a_log, m_gdn_dt_bias, m_gdn_norm, m_hgrn_lower_bounds, m_hgrn_norm, m_ab_w_out, m_c_w_in, m_c_conv_w, m_c_conv_b, m_c_gate_a_w, m_c_gate_a_b, m_c_gate_x_w, m_c_gate_x_b, m_c_lambda, m_c_w_out, m_ffn_w_up, m_ffn_conv_w, m_ffn_conv_b, m_ffn_w_down)))
    vl = dict(zip(WEIGHTS, (v_norm_mix, v_norm_ffn, v_norm_final, v_ab_w_in, v_gdn_conv_w, v_gdn_a_log, v_gdn_dt_bias, v_gdn_norm, v_hgrn_lower_bounds, v_hgrn_norm, v_ab_w_out, v_c_w_in, v_c_conv_w, v_c_conv_b, v_c_gate_a_w, v_c_gate_a_b, v_c_gate_x_w, v_c_gate_x_b, v_c_lambda, v_c_w_out, v_ffn_w_up, v_ffn_conv_w, v_ffn_conv_b, v_ffn_w_down)))

    big = [n for n in SHARDED if n in MATMUL_WEIGHTS]
    vec = [n for n in SHARDED if n not in MATMUL_WEIGHTS]
    shards = {n: wl[n].astype(bf16) for n in big}
    shards['ab_w_in'] = jnp.pad(shards['ab_w_in'], ((0, 0), (0, 0), (0, AB_SHARD_PAD - AB_SHARD)))
    shards['ffn_w_up'] = jnp.pad(jnp.swapaxes(shards['ffn_w_up'], 1, 2), ((0, 0), (0, FF_PAD - FF_SHARD), (0, 0)))
    shards['vec'] = _pack([wl[n] for n in vec])
    pos = _position()
    layer_items = [_layer_items(l) for l in range(DEPTH)]
    posv = [v.astype(jnp.int32).reshape(1) for v in (4 * pos[0] + 2 * pos[1] + pos[2], 2 * pos[0] + pos[1], pos[2])]
    first_items = layer_items[0] + [('vec', None)]
    lands = [_place_own(first_items, shards, posv, "place_own_0")]
    lands += [_place_own(layer_items[l], shards, posv, "place_own_%d" % l) for l in range(1, DEPTH)]
    mix0, ffn0 = layer_items[0][:-2], layer_items[0][-2:]
    first = _gather_now(mix0 + [('vec', None)], shards, lands[0][:len(mix0)] + lands[0][-1:])
    flight = {'shards': {n: shards[n] for n in big}}

    full = {n: wl[n] for n in REPLICATED}

    def start(items, item_lands, token, name, anchor):
        sems, thru, flight['lands'], tok = _gather_start(items, flight['shards'], item_lands, token, name)
        flight['sems'] = list(sems)
        flight['shards'].update(thru)
        full[anchor] = tok

    start(ffn0, lands[0][len(mix0):-1], first[-1], "gather_start_0", ('fwd', 0))

    for n, a in zip(vec, _unpack(first[-1], [wl[n].shape for n in vec], lead=True)):
        full[n] = _merge_shards(a, SHARD_AXIS[n])
    full['ffn_conv_w'] = _block_pad(full['ffn_conv_w'], 2, 4, FF_PAD)
    full['ffn_conv_b'] = _block_pad(full['ffn_conv_b'], 1, 4, FF_PAD)
    for n in big:
        full[n] = {}

    def arrive(items, x_in, tag):
        flight['shards'], got = _gather_wait(items, flight['sems'], flight['shards'], flight['lands'], x_in,
                                             "gather_wait_" + tag)
        return _gather_forward(items, got, "gather_forward_" + tag)

    def fetch(l, x_in, part):
        if l == 0 and part == 'mix':
            items, got = mix0, first[:len(mix0)]
        elif l == 0:
            items, got = ffn0, arrive(ffn0, x_in, "0")
            start(layer_items[1], lands[1], got[0], "gather_start_1", ('fwdf', 0))
        elif part == 'mix':
            items = layer_items[l]
            got = arrive(items, x_in, str(l))
            if l + 1 < DEPTH:
                start(layer_items[l + 1], lands[l + 1], got[0], "gather_start_%d" % (l + 1), ('fwd', l))
        else:
            return
        for (nm, li), a in zip(items, got):
            if nm == 'ab_w_in':
                a = _ab_permute(_block_unpad(a, 1, N_DEV, AB_SHARD))
            elif nm in ('ab_w_out', 'c_w_out'):
                a = a.reshape(D_MODEL, D_MODEL)
            elif nm in ('c_gate_a_w', 'c_gate_x_w'):
                a = a.reshape(HEADS, LRU_BLOCK, LRU_BLOCK)
            elif nm == 'ffn_w_down':
                a = a.reshape(D_FFP, D_MODEL)
            elif nm == 'ffn_w_up':
                a = a.reshape(2 * D_FFP, D_MODEL)
            full[nm][li] = a

    cpos = lax.axis_index("c").astype(jnp.int32).reshape(1)
    mine = (2 * lax.axis_index("x") + lax.axis_index("y")).astype(jnp.int32).reshape(1)
    pending = {}

    def exchange(key, items, g, anchor):
        fulls = {}
        for nm, _ in items:
            a = g[nm].astype(bf16)
            if nm == 'ab_w_in':
                a = _block_pad(_ab_unpermute(a), 1, N_DEV, AB_SHARD_PAD)
            fulls[nm] = a.reshape((1,) + BLOCK_LAYOUT[nm][0][1:])
        pair, _ = _pair_exchange(fulls, jnp.zeros((8, 128), f32), "_" + key)
        chips = [_chip_sum(nm, fulls[nm], pair[nm], cpos, "_" + key) for nm in fulls]
        sems, chips, crosses, tok = _cross_start(chips, "grad_cross_start_" + key)
        pending[key] = (items, sems, chips, crosses)
        full[anchor] = tok

    stash = {}

    def push(l, part, g):
        if l == 0:
            exchange("0f" if part == 'ffn' else "0", ffn0 if part == 'ffn' else mix0, g,
                     ('bwdm', 0) if part == 'ffn' else ('bwd', -1))
            return
        stash.update(g)
        if part == 'mix':
            exchange(str(l), layer_items[l], dict(stash), ('bwd', l - 1))
            stash.clear()

    loss, dx, grads = _local_step(x[0], loss_target[0], full, fetch, push)

    grads['ffn_conv_w'] = _block_unpad(grads['ffn_conv_w'], 2, 4, FF_SHARD)
    grads['ffn_conv_b'] = _block_unpad(grads['ffn_conv_b'], 1, 4, FF_SHARD)
    fulls = {'vec': _pack([_split_shards(grads[n], SHARD_AXIS[n]) for n in vec], lead=N_DEV)}
    chip, rep_chip, recv, rrep = _exchange_grads(fulls, _pack([grads[n] for n in REPLICATED]))
    res = {}
    stacked = {}
    after = full['bwd', -1]
    for key in ("3", "2", "1", "0f", "0"):
        items, sems, chips, lands = pending[key]
        chips, lands = _cross_wait(sems, chips, lands, after, "grad_cross_wait_" + key)
        for (n, li), own, parts in zip(items, chips, lands):
            if n == 'ffn_w_up':
                wmv = [jnp.swapaxes(a[n], 1, 2) for a in (wl, ml, vl)]
                rp = FF_PAD
            else:
                shp = wl[n].shape
                rp = int(np.prod(shp[1:-1]))
                wmv = [a[n].reshape(shp[0], rp, shp[-1]) for a in (wl, ml, vl)]
            stacked[n] = _sum_adamw_layer(parts.reshape(4, rp, -1), own.reshape(4, rp, -1), mine, *wmv, li,
                                          stacked.get(n), "adamw_%s_%d" % (n, li))
        if key == "0f":
            after = stacked['ffn_w_up'][0]
    for n in big:
        for kind, o in zip(("grad", "delta", "new_m", "new_v"), stacked[n]):
            res[kind, n] = jnp.swapaxes(o, 1, 2) if n == 'ffn_w_up' else o.reshape(wl[n].shape)
    for names, parts, own, tag in ((vec, recv['vec'], chip['vec'], "adamw_vectors"),
                                   (REPLICATED, rrep, rep_chip, "adamw_replicated")):
        outs = _sum_adamw(parts, own, mine, _pack([wl[n] for n in names]), _pack([ml[n] for n in names]),
                          _pack([vl[n] for n in names]), tag)
        for kind, o in zip(("grad", "delta", "new_m", "new_v"), outs):
            for n, a in zip(names, _unpack(o, [wl[n].shape for n in names])):
                res[kind, n] = a

    loss = lax.psum(loss, ("x", "y", "c"))
    return (loss, dx[None], *[res[kind, n] for kind in ("grad", "delta", "new_m", "new_v") for n in WEIGHTS])
```

```python
import functools

import numpy as np
import jax
import jax.numpy as jnp
from jax import lax
from jax.experimental import pallas as pl
from jax.experimental.pallas import tpu as pltpu

f32 = jnp.float32
bf16 = jnp.bfloat16
HI = lax.Precision.HIGHEST
MESH = pl.DeviceIdType.MESH

N_DEV = 8
D_MODEL = 1024
DEPTH = 4
EPS = 1e-6
F_FLOOR = 1e-30
HEADS = 4
HEAD_DIM = 128
GDN_WIDTH = 512
GDN_CONV = 4
GDN_CHUNK = 64
HGRN_CHUNK = 16
HGRN_STEP = 128
MIX_WIDTH = 1024
AB_COLS = 4104
AB_PAD = 4224
LRU_WIDTH = 1024
LRU_BLOCK = 256
LRU_CONV = 4
RG_C = 8.0
D_FF = 2816
FF_SHARD = 704
FF_PAD = 768
D_FFP = 4 * FF_PAD
FF_ROWS = 352
AB_SHARD, AB_SHARD_PAD = 513, 640
FFN_CONV = 3
ADAM_LR, ADAM_B1, ADAM_B2, ADAM_EPS, ADAM_WD, ADAM_STEP = 0.001, 0.9, 0.999, 1e-08, 0.01, 10
VMEM_LIMIT = 56 * 1024 * 1024
ROW_SLAB = 32
NORM_SLAB = 16
PACK_LANES = 512
PACK_ROWS = 256

OFF_Q, OFF_K, OFF_V, OFF_Z, OFF_QB, OFF_FB, OFF_IB, OFF_GB, OFF_BA = 0, 512, 1024, 1536, 2048, 2560, 3072, 3584, 4096

WEIGHTS = ['norm_mix', 'norm_ffn', 'norm_final', 'ab_w_in', 'gdn_conv_w', 'gdn_a_log', 'gdn_dt_bias', 'gdn_norm',
           'hgrn_lower_bounds', 'hgrn_norm', 'ab_w_out', 'c_w_in', 'c_conv_w', 'c_conv_b', 'c_gate_a_w', 'c_gate_a_b',
           'c_gate_x_w', 'c_gate_x_b', 'c_lambda', 'c_w_out', 'ffn_w_up', 'ffn_conv_w', 'ffn_conv_b', 'ffn_w_down']
SHARD_AXIS = {'norm_mix': None, 'norm_ffn': None, 'norm_final': None, 'ab_w_in': 2, 'gdn_conv_w': 2, 'gdn_a_log': None,
              'gdn_dt_bias': None, 'gdn_norm': None, 'hgrn_lower_bounds': None, 'hgrn_norm': None, 'ab_w_out': 1,
              'c_w_in': 2, 'c_conv_w': 2, 'c_conv_b': 1, 'c_gate_a_w': 2, 'c_gate_a_b': 1, 'c_gate_x_w': 2,
              'c_gate_x_b': 1, 'c_lambda': 1, 'c_w_out': 1, 'ffn_w_up': 2, 'ffn_conv_w': 2, 'ffn_conv_b': None,
              'ffn_w_down': 1}
MATMUL_WEIGHTS = ('ab_w_in', 'ab_w_out', 'c_w_in', 'c_gate_a_w', 'c_gate_x_w', 'c_w_out', 'ffn_w_up', 'ffn_w_down')
SHARDED = [n for n in WEIGHTS if SHARD_AXIS[n] is not None]
REPLICATED = [n for n in WEIGHTS if SHARD_AXIS[n] is None]


def _tile(n, prefs=(512, 384, 256, 128)):
    for p in prefs:
        if n % p == 0:
            return p
    return n


def _cparams(sem=None):
    kw = dict(vmem_limit_bytes=VMEM_LIMIT)
    if sem is not None:
        kw['dimension_semantics'] = sem
    return pltpu.CompilerParams(**kw)


def _sds(shape, dtype):
    return jax.ShapeDtypeStruct(tuple(shape), dtype)


def _sigmoid(x):
    return 1.0 / (1.0 + jnp.exp(-x))


def _silu(x):
    return x * (0.5 * jnp.tanh(0.5 * x) + 0.5)


def _log1p(x):
    u = 1.0 + x
    return jnp.where(u == 1.0, x, jnp.log(u) * (x / jnp.where(u == 1.0, 1.0, u - 1.0)))


def _softplus(x):
    return jnp.maximum(x, 0.0) + _log1p(jnp.exp(-jnp.abs(x)))


def _expm1(x):
    small = jnp.abs(x) < 0.05
    xs = jnp.where(small, x, 0.0)
    series = xs * (1.0 + xs * (0.5 + xs * (1.0 / 6.0 + xs * (1.0 / 24.0 + xs * (1.0 / 120.0)))))
    return jnp.where(small, series, jnp.exp(x) - 1.0)


def _gelu(x):
    return 0.5 * x * (1.0 + jnp.tanh(0.7978845608028654 * (x + 0.044715 * x * x * x)))


def _rms(x, gain):
    return x * lax.rsqrt(jnp.mean(x * x, axis=-1, keepdims=True) + EPS) * gain


def _dot(a, b, dims=((1,), (0,)), precision=None):
    return lax.dot_general(a, b, (dims, ((), ())), precision=precision, preferred_element_type=f32)


def _bdot(a, b, dims=((1,), (0,))):
    return _dot(a.astype(bf16), b.astype(bf16), dims)


NT = ((1,), (1,))
TN = ((0,), (0,))


def _shift_down(x, k):
    if k == 0:
        return x
    row = lax.broadcasted_iota(jnp.int32, x.shape, 0)
    return jnp.where(row >= k, pltpu.roll(x, k, 0), 0.0)


def _shift_up(x, k, fill=0.0):
    if k == 0:
        return x
    n = x.shape[0]
    row = lax.broadcasted_iota(jnp.int32, x.shape, 0)
    return jnp.where(row < n - k, pltpu.roll(x, n - k, 0), fill)


def _conv_fwd(x, w_ref, width):
    acc = w_ref[width - 1:width, :] * x
    for k in range(width - 1):
        acc = acc + w_ref[k:k + 1, :] * _shift_down(x, width - 1 - k)
    return acc


def _conv_bwd(x, dout, w_ref, dw_ref, width):
    dx = w_ref[width - 1:width, :] * dout
    dw_ref[width - 1:width, :] = jnp.sum(dout * x, axis=0, keepdims=True)
    for k in range(width - 1):
        s = width - 1 - k
        dx = dx + w_ref[k:k + 1, :] * _shift_up(dout, s)
        dw_ref[k:k + 1, :] = jnp.sum(dout * _shift_down(x, s), axis=0, keepdims=True)
    return dx


MM_VMEM_BUDGET = 36 * 1024 * 1024
MM_MAX_TILE = 1024 * 1024


def _mm_tiles(m, n, k, out_bytes):
    best = None
    for tm in (1024, 512, 384, 256, 128):
        if m % tm:
            continue
        for tn in range(1536, 0, -128):
            if n % tn or tm * tn > MM_MAX_TILE:
                continue
            score = (tm * tn, min(tm, tn))
            if 2 * (tm * k * 2 + k * tn * 2 + tm * tn * out_bytes) <= MM_VMEM_BUDGET and (best is None or score > best[0]):
                best = (score, tm, tn)
    return (best[1], best[2]) if best else (_tile(m), _tile(n))


def _mm(a, b, *, ta=False, tb=False, add=None, out_dtype=f32, name):
    m, k = (a.shape[1], a.shape[0]) if ta else a.shape
    n = b.shape[0] if tb else b.shape[1]
    tm, tn = _mm_tiles(m, n, k, jnp.dtype(out_dtype).itemsize + (4 if add is not None else 0))
    dims = ((0 if ta else 1,), (1 if tb else 0,))

    def body(*refs):
        a_ref, b_ref = refs[0], refs[1]
        o_ref = refs[-1]
        r = _dot(a_ref[...], b_ref[...], dims)
        if add is not None:
            r = r + refs[2][...]
        o_ref[...] = r.astype(out_dtype)

    a_spec = pl.BlockSpec((k, tm), lambda j, i: (0, i)) if ta else pl.BlockSpec((tm, k), lambda j, i: (i, 0))
    b_spec = pl.BlockSpec((tn, k), lambda j, i: (j, 0)) if tb else pl.BlockSpec((k, tn), lambda j, i: (0, j))
    o_spec = pl.BlockSpec((tm, tn), lambda j, i: (i, j))
    ins, specs = [a, b], [a_spec, b_spec]
    if add is not None:
        ins.append(add)
        specs.append(o_spec)
    return pl.pallas_call(body, name=name, grid=(n // tn, m // tm), in_specs=specs, out_specs=o_spec,
                          out_shape=_sds((m, n), out_dtype), compiler_params=_cparams(("parallel", "parallel")))(*ins)


def _rms_fwd(x, gain, name):
    t, d = x.shape
    tr = _tile(t, (256, 128))

    def body(x_ref, g_ref, h_ref):
        g = g_ref[...]

        def slab(i, carry):
            rows = pl.ds(pl.multiple_of(i * NORM_SLAB, NORM_SLAB), NORM_SLAB)
            h_ref[rows, :] = _rms(x_ref[rows, :], g).astype(bf16)
            return carry

        lax.fori_loop(0, tr // NORM_SLAB, slab, 0)

    return pl.pallas_call(body, name=name, grid=(t // tr,),
                          in_specs=[pl.BlockSpec((tr, d), lambda i: (i, 0)), pl.BlockSpec((1, d), lambda i: (0, 0))],
                          out_specs=pl.BlockSpec((tr, d), lambda i: (i, 0)), out_shape=_sds((t, d), bf16),
                          compiler_params=_cparams(("parallel",)))(x, gain)


def _rms_bwd(x, gain, dh, dres, name):
    t, d = x.shape
    tr = _tile(t, (256, 128))

    def body(x_ref, g_ref, dh_ref, dres_ref, dx_ref, dxb_ref, dg_ref):
        g = g_ref[...]

        def slab(i, acc):
            rows = pl.ds(pl.multiple_of(i * NORM_SLAB, NORM_SLAB), NORM_SLAB)
            _, vjp = jax.vjp(_rms, x_ref[rows, :], g)
            dx, dg = vjp(dh_ref[rows, :])
            dx = dx + dres_ref[rows, :]
            dx_ref[rows, :] = dx
            dxb_ref[rows, :] = dx.astype(bf16)
            return acc + dg

        dg = lax.fori_loop(0, tr // NORM_SLAB, slab, jnp.zeros((1, d), f32))

        @pl.when(pl.program_id(0) == 0)
        def _():
            dg_ref[...] = jnp.zeros_like(dg_ref)

        dg_ref[...] += dg

    row = pl.BlockSpec((tr, d), lambda i: (i, 0))
    vec = pl.BlockSpec((1, d), lambda i: (0, 0))
    return pl.pallas_call(body, name=name, grid=(t // tr,), in_specs=[row, vec, row, row], out_specs=[row, row, vec],
                          out_shape=[_sds((t, d), f32), _sds((t, d), bf16), _sds((1, d), f32)],
                          compiler_params=_cparams(("arbitrary",)))(x, gain, dh, dres)


def _loss_head(x, gain, target):
    t, d = x.shape
    tr = _tile(t, (256, 128))

    def f(xv, g, tgt):
        err = _rms(xv, g) - tgt
        return 0.5 * jnp.sum(jnp.mean(err * err, axis=-1, keepdims=True), axis=0, keepdims=True)

    def body(x_ref, g_ref, t_ref, loss_ref, dx_ref, dxb_ref, dg_ref):
        loss, vjp = jax.vjp(lambda xv, g: f(xv, g, t_ref[...]), x_ref[...], g_ref[...])
        dx, dg = vjp(jnp.ones((1, 1), f32))
        dx_ref[...] = dx
        dxb_ref[...] = dx.astype(bf16)

        @pl.when(pl.program_id(0) == 0)
        def _():
            dg_ref[...] = jnp.zeros_like(dg_ref)
            loss_ref[...] = jnp.zeros_like(loss_ref)

        dg_ref[...] += dg
        loss_ref[...] += jnp.broadcast_to(loss, loss_ref.shape)

    row = pl.BlockSpec((tr, d), lambda i: (i, 0))
    vec = pl.BlockSpec((1, d), lambda i: (0, 0))
    one = pl.BlockSpec((8, 128), lambda i: (0, 0))
    return pl.pallas_call(body, name="loss_head", grid=(t // tr,), in_specs=[row, vec, row],
                          out_specs=[one, row, row, vec],
                          out_shape=[_sds((8, 128), f32), _sds((t, d), f32), _sds((t, d), bf16), _sds((1, d), f32)],
                          compiler_params=_cparams(("arbitrary",)))(x, gain, target)


def _ffn_act_fwd(u, conv_w, conv_b):
    t = u.shape[0]
    tc = FF_PAD // 2
    nb = D_FFP // tc

    def body(g_ref, v_ref, w_ref, b_ref, a_ref, gc_ref):
        gc_ref[...] = _conv_fwd(g_ref[...], w_ref, FFN_CONV) + b_ref[...]

        def slab(i, carry):
            rows = pl.ds(pl.multiple_of(i * ROW_SLAB, ROW_SLAB), ROW_SLAB)
            a_ref[rows, :] = (_silu(gc_ref[rows, :]) * v_ref[rows, :]).astype(bf16)
            return carry

        lax.fori_loop(0, t // ROW_SLAB, slab, 0)

    return pl.pallas_call(
        body, name="ffn_act_fwd", grid=(nb,),
        in_specs=[pl.BlockSpec((t, tc), lambda j: (0, j)), pl.BlockSpec((t, tc), lambda j: (0, j + nb)),
                  pl.BlockSpec((FFN_CONV, tc), lambda j: (0, j)), pl.BlockSpec((1, tc), lambda j: (0, j))],
        out_specs=pl.BlockSpec((t, tc), lambda j: (0, j)), out_shape=_sds((t, D_FFP), bf16),
        scratch_shapes=[pltpu.VMEM((t, tc), f32)],
        compiler_params=_cparams(("parallel",)))(u, u, conv_w, conv_b)


def _ffn_act_bwd(u, conv_w, conv_b, da):
    t = u.shape[0]
    tc = FF_PAD // 2
    nb = D_FFP // tc

    def act(gc, val):
        return _silu(gc) * val

    def body(g_ref, v_ref, w_ref, b_ref, da_ref, a_ref, dg_ref, dv_ref, dw_ref, db_ref, gc_ref):
        gp = g_ref[...]
        gc_ref[...] = _conv_fwd(gp, w_ref, FFN_CONV) + b_ref[...]

        def slab(i, carry):
            rows = pl.ds(pl.multiple_of(i * ROW_SLAB, ROW_SLAB), ROW_SLAB)
            a, vjp = jax.vjp(act, gc_ref[rows, :], v_ref[rows, :])
            dgc, dval = vjp(da_ref[rows, :])
            a_ref[rows, :] = a.astype(bf16)
            dv_ref[rows, :] = dval.astype(bf16)
            gc_ref[rows, :] = dgc
            return carry

        lax.fori_loop(0, t // ROW_SLAB, slab, 0)
        dgc = gc_ref[...]
        db_ref[...] = jnp.sum(dgc, axis=0, keepdims=True)
        dg_ref[...] = _conv_bwd(gp, dgc, w_ref, dw_ref, FFN_CONV).astype(bf16)

    col = pl.BlockSpec((t, tc), lambda j: (0, j))
    return pl.pallas_call(
        body, name="ffn_act_bwd", grid=(nb,),
        in_specs=[col, pl.BlockSpec((t, tc), lambda j: (0, j + nb)), pl.BlockSpec((FFN_CONV, tc), lambda j: (0, j)),
                  pl.BlockSpec((1, tc), lambda j: (0, j)), col],
        out_specs=[col, col, col, pl.BlockSpec((FFN_CONV, tc), lambda j: (0, j)), pl.BlockSpec((1, tc), lambda j: (0, j))],
        out_shape=[_sds((t, D_FFP), bf16), _sds((t, D_FFP), bf16), _sds((t, D_FFP), bf16), _sds((FFN_CONV, D_FFP), f32),
                   _sds((1, D_FFP), f32)],
        scratch_shapes=[pltpu.VMEM((t, tc), f32)],
        compiler_params=_cparams(("parallel",)))(u, u, conv_w, conv_b, da)


def _lru_gates(xc, ra, ia, lam):
    r = _sigmoid(ra)
    i = _sigmoid(ia)
    log_a = -RG_C * r * _softplus(-lam)
    a = jnp.exp(log_a)
    u = jnp.sqrt(jnp.maximum(-_expm1(2.0 * log_a), 0.0)) * (i * xc)
    return a, u


def _lin_scan(a, u):
    n = a.shape[0]
    row = lax.broadcasted_iota(jnp.int32, a.shape, 0)
    s = 1
    while s < n:
        keep = row >= s
        u = a * jnp.where(keep, pltpu.roll(u, s, 0), 0.0) + u
        a = a * jnp.where(keep, pltpu.roll(a, s, 0), 1.0)
        s *= 2
    return u


def _rev_scan(a_next, d):
    n = d.shape[0]
    row = lax.broadcasted_iota(jnp.int32, d.shape, 0)
    a = a_next
    s = 1
    while s < n:
        keep = row < n - s
        d = a * jnp.where(keep, pltpu.roll(d, n - s, 0), 0.0) + d
        a = a * jnp.where(keep, pltpu.roll(a, n - s, 0), 1.0)
        s *= 2
    return d


def _col_conv_fwd(p, col_off, conv_w, conv_b, width, tc, name):
    t = p.shape[0]
    c = conv_w.shape[1]
    ob = col_off // tc

    def body(x_ref, w_ref, b_ref, o_ref):
        o_ref[...] = _conv_fwd(x_ref[...], w_ref, width) + b_ref[...]

    return pl.pallas_call(
        body, name=name, grid=(c // tc,),
        in_specs=[pl.BlockSpec((t, tc), lambda j: (0, j + ob)), pl.BlockSpec((width, tc), lambda j: (0, j)),
                  pl.BlockSpec((1, tc), lambda j: (0, j))],
        out_specs=pl.BlockSpec((t, tc), lambda j: (0, j)), out_shape=_sds((t, c), f32),
        compiler_params=_cparams(("parallel",)))(p, conv_w, conv_b)


def _col_conv_bwd(p, col_off, conv_w, dxc, width, tc, name):
    t = p.shape[0]
    c = conv_w.shape[1]
    ob = col_off // tc

    def body(x_ref, w_ref, d_ref, dx_ref, dw_ref, db_ref):
        d = d_ref[...]
        db_ref[...] = jnp.sum(d, axis=0, keepdims=True)
        dx_ref[...] = _conv_bwd(x_ref[...], d, w_ref, dw_ref, width).astype(bf16)

    col = pl.BlockSpec((t, tc), lambda j: (0, j))
    return pl.pallas_call(
        body, name=name, grid=(c // tc,),
        in_specs=[pl.BlockSpec((t, tc), lambda j: (0, j + ob)), pl.BlockSpec((width, tc), lambda j: (0, j)), col],
        out_specs=[col, pl.BlockSpec((width, tc), lambda j: (0, j)), pl.BlockSpec((1, tc), lambda j: (0, j))],
        out_shape=[_sds((t, c), bf16), _sds((width, c), f32), _sds((1, c), f32)],
        compiler_params=_cparams(("parallel",)))(p, conv_w, dxc)


def _lru_fwd(p, xc, wa, ba, wx, bx, lam):
    t = p.shape[0]
    bw = LRU_BLOCK

    def body(y_ref, xc_ref, wa_ref, ba_ref, wx_ref, bx_ref, lam_ref, out_ref, hs_ref, a_ref):
        xc_v = xc_ref[...]
        xb = xc_v.astype(bf16)
        ra = _dot(xb, wa_ref[0]) + ba_ref[...]
        ia = _dot(xb, wx_ref[0]) + bx_ref[...]
        a, u = _lru_gates(xc_v, ra, ia, lam_ref[...])
        a_ref[...] = a
        hs = _lin_scan(a, u)
        hs_ref[...] = hs
        out_ref[...] = (hs * _gelu(y_ref[...])).astype(bf16)

    col = pl.BlockSpec((t, bw), lambda h: (0, h))
    vec = pl.BlockSpec((1, bw), lambda h: (0, h))
    mat = pl.BlockSpec((1, bw, bw), lambda h: (h, 0, 0))
    return pl.pallas_call(
        body, name="lru_fwd", grid=(HEADS,), in_specs=[col, col, mat, vec, mat, vec, vec], out_specs=[col, col, col],
        out_shape=[_sds((t, LRU_WIDTH), bf16), _sds((t, LRU_WIDTH), f32), _sds((t, LRU_WIDTH), f32)],
        compiler_params=_cparams(("parallel",)))(p, xc, wa, ba, wx, bx, lam)


def _lru_bwd_scan(p, a, hs, dout):
    t = p.shape[0]
    bw = LRU_BLOCK

    def body(y_ref, a_ref, hs_ref, do_ref, dy_ref, da_ref, du_ref):
        hs_v = hs_ref[...]
        do = do_ref[...]
        gate, vjp = jax.vjp(_gelu, y_ref[...])
        dy_ref[...] = vjp(do * hs_v)[0].astype(bf16)
        g = _rev_scan(_shift_up(a_ref[...], 1), do * gate)
        du_ref[...] = g
        da_ref[...] = g * _shift_down(hs_v, 1)

    col = pl.BlockSpec((t, bw), lambda h: (0, h))
    return pl.pallas_call(
        body, name="lru_bwd_scan", grid=(HEADS,), in_specs=[col, col, col, col], out_specs=[col, col, col],
        out_shape=[_sds((t, LRU_WIDTH), bf16), _sds((t, LRU_WIDTH), f32), _sds((t, LRU_WIDTH), f32)],
        compiler_params=_cparams(("parallel",)))(p, a, hs, dout)


def _lru_bwd_gates(xc, da, du, wa, ba, wx, bx, lam):
    t = xc.shape[0]
    bw = LRU_BLOCK
    tr = _tile(t, (512, 256, 128))

    def body(xc_ref, da_ref, du_ref, wa_ref, ba_ref, wx_ref, bx_ref, lam_ref,
             dxc_ref, dwa_ref, dwx_ref, dba_ref, dbx_ref, dlam_ref):
        xc_v = xc_ref[...]
        xb = xc_v.astype(bf16)
        ra = _dot(xb, wa_ref[0]) + ba_ref[...]
        ia = _dot(xb, wx_ref[0]) + bx_ref[...]
        _, vjp = jax.vjp(_lru_gates, xc_v, ra, ia, lam_ref[...])
        dxc, dra, dia, dlam = vjp((da_ref[...], du_ref[...]))
        drb, dib = dra.astype(bf16), dia.astype(bf16)
        dxc_ref[...] = dxc + _dot(drb, wa_ref[0], NT) + _dot(dib, wx_ref[0], NT)

        @pl.when(pl.program_id(1) == 0)
        def _():
            dwa_ref[...] = jnp.zeros_like(dwa_ref)
            dwx_ref[...] = jnp.zeros_like(dwx_ref)
            dba_ref[...] = jnp.zeros_like(dba_ref)
            dbx_ref[...] = jnp.zeros_like(dbx_ref)
            dlam_ref[...] = jnp.zeros_like(dlam_ref)

        dwa_ref[0] += _dot(xb, drb, TN)
        dwx_ref[0] += _dot(xb, dib, TN)
        dba_ref[...] += jnp.sum(dra, axis=0, keepdims=True)
        dbx_ref[...] += jnp.sum(dia, axis=0, keepdims=True)
        dlam_ref[...] += dlam

    tile = pl.BlockSpec((tr, bw), lambda h, i: (i, h))
    vec = pl.BlockSpec((1, bw), lambda h, i: (0, h))
    mat = pl.BlockSpec((1, bw, bw), lambda h, i: (h, 0, 0))
    return pl.pallas_call(
        body, name="lru_bwd_gates", grid=(HEADS, t // tr), in_specs=[tile, tile, tile, mat, vec, mat, vec, vec],
        out_specs=[tile, mat, mat, vec, vec, vec],
        out_shape=[_sds((t, LRU_WIDTH), f32), _sds((HEADS, bw, bw), f32), _sds((HEADS, bw, bw), f32),
                   _sds((1, LRU_WIDTH), f32), _sds((1, LRU_WIDTH), f32), _sds((1, LRU_WIDTH), f32)],
        compiler_params=_cparams(("parallel", "arbitrary")))(xc, da, du, wa, ba, wx, bx, lam)


def _gdn_pre_fn(cq, ck, cv, ba, alog, dtb, h):
    q, k, v = _silu(cq), _silu(ck), _silu(cv)
    q = q * lax.rsqrt(jnp.sum(q * q, axis=-1, keepdims=True) + EPS) * (HEAD_DIM ** -0.5)
    k = k * lax.rsqrt(jnp.sum(k * k, axis=-1, keepdims=True) + EPS)
    lane = lax.broadcasted_iota(jnp.int32, (1, HEAD_DIM), 1)
    mb = (lane == h).astype(f32)
    ma = (lane == HEADS + h).astype(f32)
    beta_raw = jnp.sum(ba * mb, axis=-1, keepdims=True)
    alpha = jnp.sum(ba * ma, axis=-1, keepdims=True)
    al = jnp.sum(alog * mb, axis=-1, keepdims=True)
    db = jnp.sum(dtb * mb, axis=-1, keepdims=True)
    beta = _sigmoid(beta_raw)
    g = -jnp.exp(al) * _softplus(alpha + db)
    return q, k, v, jnp.broadcast_to(beta, q.shape), jnp.broadcast_to(g, q.shape)


def _gdn_pre_fwd(p, conv_w, alog, dtb):
    t = p.shape[0]
    hd = HEAD_DIM

    def body(pq_ref, pk_ref, pv_ref, ba_ref, wq_ref, wk_ref, wv_ref, al_ref, dt_ref, q_ref, k_ref, v_ref, b_ref, g_ref):
        h = pl.program_id(0)
        cq = _conv_fwd(pq_ref[...], wq_ref, GDN_CONV)
        ck = _conv_fwd(pk_ref[...], wk_ref, GDN_CONV)
        cv = _conv_fwd(pv_ref[...], wv_ref, GDN_CONV)
        q, k, v, be, ge = _gdn_pre_fn(cq, ck, cv, ba_ref[...], al_ref[...], dt_ref[...], h)
        q_ref[...], k_ref[...], v_ref[...], b_ref[...], g_ref[...] = q, k, v, be, ge

    def pcol(off):
        return pl.BlockSpec((t, hd), lambda h: (0, h + off // hd))

    def wcol(off):
        return pl.BlockSpec((GDN_CONV, hd), lambda h: (0, h + off // hd))

    vec = pl.BlockSpec((1, hd), lambda h: (0, 0))
    out = pl.BlockSpec((t, hd), lambda h: (0, h))
    return pl.pallas_call(
        body, name="gdn_pre_fwd", grid=(HEADS,),
        in_specs=[pcol(OFF_Q), pcol(OFF_K), pcol(OFF_V), pl.BlockSpec((t, hd), lambda h: (0, OFF_BA // hd)),
                  wcol(0), wcol(GDN_WIDTH), wcol(2 * GDN_WIDTH), vec, vec],
        out_specs=[out] * 5, out_shape=[_sds((t, GDN_WIDTH), f32)] * 5,
        compiler_params=_cparams(("parallel",)))(p, p, p, p, conv_w, conv_w, conv_w, alog, dtb)


def _gdn_pre_bwd(p, conv_w, alog, dtb, dq, dk, dv, dbe, dge):
    t = p.shape[0]
    hd = HEAD_DIM

    def body(pq_ref, pk_ref, pv_ref, ba_ref, wq_ref, wk_ref, wv_ref, al_ref, dt_ref,
             dq_ref, dk_ref, dv_ref, dbe_ref, dge_ref,
             opq_ref, opk_ref, opv_ref, dba_ref, dwq_ref, dwk_ref, dwv_ref, dal_ref, ddt_ref):
        h = pl.program_id(0)
        pq, pk, pv = pq_ref[...], pk_ref[...], pv_ref[...]
        cq = _conv_fwd(pq, wq_ref, GDN_CONV)
        ck = _conv_fwd(pk, wk_ref, GDN_CONV)
        cv = _conv_fwd(pv, wv_ref, GDN_CONV)
        _, vjp = jax.vjp(functools.partial(_gdn_pre_fn, h=h), cq, ck, cv, ba_ref[...], al_ref[...], dt_ref[...])
        dcq, dck, dcv, dba, dal, ddt = vjp((dq_ref[...], dk_ref[...], dv_ref[...], dbe_ref[...], dge_ref[...]))
        opq_ref[...] = _conv_bwd(pq, dcq, wq_ref, dwq_ref, GDN_CONV).astype(bf16)
        opk_ref[...] = _conv_bwd(pk, dck, wk_ref, dwk_ref, GDN_CONV).astype(bf16)
        opv_ref[...] = _conv_bwd(pv, dcv, wv_ref, dwv_ref, GDN_CONV).astype(bf16)

        @pl.when(h == 0)
        def _():
            dba_ref[...] = jnp.zeros_like(dba_ref)
            dal_ref[...] = jnp.zeros_like(dal_ref)
            ddt_ref[...] = jnp.zeros_like(ddt_ref)

        dba_ref[...] += dba
        dal_ref[...] += dal
        ddt_ref[...] += ddt

    def pcol(off):
        return pl.BlockSpec((t, hd), lambda h: (0, h + off // hd))

    def wcol(off):
        return pl.BlockSpec((GDN_CONV, hd), lambda h: (0, h + off // hd))

    vec = pl.BlockSpec((1, hd), lambda h: (0, 0))
    col = pl.BlockSpec((t, hd), lambda h: (0, h))
    full = pl.BlockSpec((t, hd), lambda h: (0, 0))
    wout = pl.BlockSpec((GDN_CONV, hd), lambda h: (0, h))
    return pl.pallas_call(
        body, name="gdn_pre_bwd", grid=(HEADS,),
        in_specs=[pcol(OFF_Q), pcol(OFF_K), pcol(OFF_V), pl.BlockSpec((t, hd), lambda h: (0, OFF_BA // hd)),
                  wcol(0), wcol(GDN_WIDTH), wcol(2 * GDN_WIDTH), vec, vec, col, col, col, col, col],
        out_specs=[col, col, col, full, wout, wout, wout, vec, vec],
        out_shape=[_sds((t, GDN_WIDTH), bf16)] * 3 + [_sds((t, hd), f32)] + [_sds((GDN_CONV, GDN_WIDTH), f32)] * 3
        + [_sds((1, hd), f32)] * 2,
        compiler_params=_cparams(("arbitrary",)))(p, p, p, p, conv_w, conv_w, conv_w, alog, dtb, dq, dk, dv, dbe, dge)


BNN = (((2,), (1,)), ((0,), (0,)))
BNT = (((2,), (2,)), ((0,), (0,)))
BTN = (((1,), (1,)), ((0,), (0,)))


def _hdot(a, b, dn=BNN, precision=None):
    return lax.dot_general(a, b, dn, precision=precision, preferred_element_type=f32)


def _hbdot(a, b, dn=BNN):
    return _hdot(a.astype(bf16), b.astype(bf16), dn)


def _tri_inverse(a):
    c = a.shape[-1]
    r = lax.broadcasted_iota(jnp.int32, (c, c), 0)
    col = lax.broadcasted_iota(jnp.int32, (c, c), 1)
    m = -a
    inv = jnp.where(r == col, 1.0, 0.0) + m
    s = 2
    while s < c:
        m = _hdot(m, m, precision=HI)
        inv = inv + _hdot(inv, m, precision=HI)
        s *= 2
    return inv


@jax.custom_vjp
def _saved_inverse(a, inv):
    return inv


def _saved_inverse_fwd(a, inv):
    return inv, inv


def _saved_inverse_bwd(inv, dinv):
    return -_hdot(_hdot(inv, dinv, BTN, precision=HI), inv, BNT, precision=HI), jnp.zeros_like(inv)


_saved_inverse.defvjp(_saved_inverse_fwd, _saved_inverse_bwd)


def _gdn_chunk(s, q, k, v, ge, be, tinv=None):
    nh, c, _ = q.shape
    r = lax.broadcasted_iota(jnp.int32, (c, c), 0)
    col = lax.broadcasted_iota(jnp.int32, (c, c), 1)
    causal = r >= col
    tri = jnp.broadcast_to(causal.astype(f32), (nh, c, c))
    gc = _hdot(tri, ge, precision=HI)
    gcc = gc[:, :, :c]
    gcr = jnp.swapaxes(gc, 1, 2)[:, :c, :]
    decay = jnp.where(causal, jnp.exp(jnp.where(causal, gcc - gcr, 0.0)), 0.0)
    kb = k * be
    lower = jnp.where(r > col, _hbdot(kb, k, BNT) * decay, 0.0)
    tinv = _tri_inverse(lower) if tinv is None else _saved_inverse(lower, tinv)
    egc = jnp.exp(gc)
    u = _hdot(tinv, v * be, precision=HI)
    w = _hdot(tinv, kb * egc, precision=HI)
    attn = _hbdot(q, k, BNT) * decay
    gl = gc[:, c - 1:c, :]
    v_new = u - _hbdot(w, s)
    o = _hbdot(q * egc, s) + _hbdot(attn, v_new)
    s_new = s * jnp.exp(gl) + _hbdot(k * jnp.exp(gl - gc), v_new, BTN)
    return o, s_new, tinv


def _heads_major(ref):
    return jnp.stack([ref[:, h * HEAD_DIM:(h + 1) * HEAD_DIM] for h in range(HEADS)])


def _gdn_core_fwd(q, k, v, ge, be):
    t = q.shape[0]
    c, hd = GDN_CHUNK, HEAD_DIM
    n = t // c

    def body(q_ref, k_ref, v_ref, g_ref, b_ref, o_ref, st_ref, ti_ref, s_ref):
        @pl.when(pl.program_id(0) == 0)
        def _():
            s_ref[...] = jnp.zeros_like(s_ref)

        s = s_ref[...]
        st_ref[:, 0] = s
        o, s_new, tinv = _gdn_chunk(s, *[_heads_major(r) for r in (q_ref, k_ref, v_ref, g_ref, b_ref)])
        ti_ref[0] = tinv
        for h in range(HEADS):
            o_ref[:, h * hd:(h + 1) * hd] = o[h]
        s_ref[...] = s_new

    tile = pl.BlockSpec((c, GDN_WIDTH), lambda i: (i, 0))
    return pl.pallas_call(
        body, name="gdn_core_fwd", grid=(n,), in_specs=[tile] * 5,
        out_specs=[tile, pl.BlockSpec((HEADS, 1, hd, hd), lambda i: (0, i, 0, 0)),
                   pl.BlockSpec((1, HEADS, c, c), lambda i: (i, 0, 0, 0))],
        out_shape=[_sds((t, GDN_WIDTH), f32), _sds((HEADS, n, hd, hd), f32), _sds((n, HEADS, c, c), f32)],
        scratch_shapes=[pltpu.VMEM((HEADS, hd, hd), f32)],
        compiler_params=_cparams(("arbitrary",)))(q, k, v, ge, be)


def _gdn_core_bwd(q, k, v, ge, be, states, do):
    t = q.shape[0]
    c, hd = GDN_CHUNK, HEAD_DIM
    n = t // c
    states, tinvs = states

    def body(q_ref, k_ref, v_ref, g_ref, b_ref, st_ref, ti_ref, do_ref, dq_ref, dk_ref, dv_ref, dg_ref, db_ref, ds_ref):
        @pl.when(pl.program_id(0) == 0)
        def _():
            ds_ref[...] = jnp.zeros_like(ds_ref)

        tinv = ti_ref[0]
        _, vjp = jax.vjp(lambda *a: _gdn_chunk(*a, tinv=tinv)[:2], st_ref[:, 0],
                         *[_heads_major(r) for r in (q_ref, k_ref, v_ref, g_ref, b_ref)])
        ds, *dins = vjp((_heads_major(do_ref), ds_ref[...]))
        ds_ref[...] = ds
        for d_ref, d in zip((dq_ref, dk_ref, dv_ref, dg_ref, db_ref), dins):
            for h in range(HEADS):
                d_ref[:, h * hd:(h + 1) * hd] = d[h]

    tile = pl.BlockSpec((c, GDN_WIDTH), lambda i: (n - 1 - i, 0))
    return pl.pallas_call(
        body, name="gdn_core_bwd", grid=(n,),
        in_specs=[tile] * 5 + [pl.BlockSpec((HEADS, 1, hd, hd), lambda i: (0, n - 1 - i, 0, 0)),
                               pl.BlockSpec((1, HEADS, c, c), lambda i: (n - 1 - i, 0, 0, 0)), tile],
        out_specs=[tile] * 5, out_shape=[_sds((t, GDN_WIDTH), f32)] * 5,
        scratch_shapes=[pltpu.VMEM((HEADS, hd, hd), f32)],
        compiler_params=_cparams(("arbitrary",)))(q, k, v, ge, be, states, tinvs, do)


def _post_fn(o, z, gain):
    return _rms(o, gain) * _silu(z)


def _post_fwd(o, p, z_off, gain, name):
    t = o.shape[0]
    hd = HEAD_DIM

    def body(o_ref, z_ref, g_ref, y_ref):
        g = g_ref[...]

        def slab(i, carry):
            rows = pl.ds(pl.multiple_of(i * ROW_SLAB, ROW_SLAB), ROW_SLAB)
            y_ref[rows, :] = _post_fn(o_ref[rows, :], z_ref[rows, :], g).astype(bf16)
            return carry

        lax.fori_loop(0, t // ROW_SLAB, slab, 0)

    col = pl.BlockSpec((t, hd), lambda h: (0, h))
    return pl.pallas_call(
        body, name=name, grid=(HEADS,),
        in_specs=[col, pl.BlockSpec((t, hd), lambda h: (0, h + z_off // hd)), pl.BlockSpec((1, hd), lambda h: (0, 0))],
        out_specs=col, out_shape=_sds((t, HEADS * hd), bf16), compiler_params=_cparams(("parallel",)))(o, p, gain)


def _post_bwd(o, p, z_off, gain, dmix, mix_off, name):
    t = o.shape[0]
    hd = HEAD_DIM

    def body(o_ref, z_ref, g_ref, dy_ref, do_ref, dz_ref, dg_ref):
        g = g_ref[...]

        def slab(i, acc):
            rows = pl.ds(pl.multiple_of(i * ROW_SLAB, ROW_SLAB), ROW_SLAB)
            _, vjp = jax.vjp(_post_fn, o_ref[rows, :], z_ref[rows, :], g)
            do, dz, dg = vjp(dy_ref[rows, :])
            do_ref[rows, :] = do
            dz_ref[rows, :] = dz.astype(bf16)
            return acc + dg

        dg = lax.fori_loop(0, t // ROW_SLAB, slab, jnp.zeros((1, hd), f32))

        @pl.when(pl.program_id(0) == 0)
        def _():
            dg_ref[...] = jnp.zeros_like(dg_ref)

        dg_ref[...] += dg

    col = pl.BlockSpec((t, hd), lambda h: (0, h))
    vec = pl.BlockSpec((1, hd), lambda h: (0, 0))
    return pl.pallas_call(
        body, name=name, grid=(HEADS,),
        in_specs=[col, pl.BlockSpec((t, hd), lambda h: (0, h + z_off // hd)), vec,
                  pl.BlockSpec((t, hd), lambda h: (0, h + mix_off // hd))],
        out_specs=[col, col, vec], out_shape=[_sds((t, HEADS * hd), f32), _sds((t, HEADS * hd), bf16), _sds((1, hd), f32)],
        compiler_params=_cparams(("arbitrary",)))(o, p, gain, dmix)


def _hgrn_pre_fn(qb, fb, lbw, layer):
    l0, l1 = lbw[0:1, :], lbw[1:2, :]
    m = jnp.maximum(l0, l1)
    e0, e1 = jnp.exp(l0 - m), jnp.exp(l1 - m)
    p0, p1 = e0 / (e0 + e1), e1 / (e0 + e1)
    lb = (p0 - p0) if layer == 0 else ((p0 + p1) - p0)
    f = lb + (1.0 - lb) * _sigmoid(fb)
    return _silu(qb), 1.0 - f, jnp.log(jnp.maximum(f, F_FLOOR))


def _hgrn_pre_fwd(p, lbw, layer):
    t = p.shape[0]
    tc = HEAD_DIM

    def body(qb_ref, fb_ref, lb_ref, q_ref, k_ref, lf_ref):
        lb = lb_ref[...]

        def slab(i, carry):
            rows = pl.ds(pl.multiple_of(i * ROW_SLAB, ROW_SLAB), ROW_SLAB)
            q_ref[rows, :], k_ref[rows, :], lf_ref[rows, :] = _hgrn_pre_fn(qb_ref[rows, :], fb_ref[rows, :], lb, layer)
            return carry

        lax.fori_loop(0, t // ROW_SLAB, slab, 0)

    col = pl.BlockSpec((t, tc), lambda j: (0, j))
    return pl.pallas_call(
        body, name="hgrn_pre_fwd", grid=(GDN_WIDTH // tc,),
        in_specs=[pl.BlockSpec((t, tc), lambda j: (0, j + OFF_QB // tc)), pl.BlockSpec((t, tc), lambda j: (0, j + OFF_FB // tc)),
                  pl.BlockSpec((2, tc), lambda j: (0, j))],
        out_specs=[col] * 3, out_shape=[_sds((t, GDN_WIDTH), f32)] * 3,
        compiler_params=_cparams(("parallel",)))(p, p, lbw)


def _hgrn_pre_bwd(p, lbw, layer, dq, dk, dlf):
    t = p.shape[0]
    tc = HEAD_DIM

    def body(qb_ref, fb_ref, lb_ref, dq_ref, dk_ref, dlf_ref, dqb_ref, dfb_ref, dlb_ref):
        lb = lb_ref[...]

        def slab(i, acc):
            rows = pl.ds(pl.multiple_of(i * ROW_SLAB, ROW_SLAB), ROW_SLAB)
            _, vjp = jax.vjp(functools.partial(_hgrn_pre_fn, layer=layer), qb_ref[rows, :], fb_ref[rows, :], lb)
            dqb, dfb, dlb = vjp((dq_ref[rows, :], dk_ref[rows, :], dlf_ref[rows, :]))
            dqb_ref[rows, :] = dqb.astype(bf16)
            dfb_ref[rows, :] = dfb.astype(bf16)
            return acc + dlb

        dlb_ref[...] = lax.fori_loop(0, t // ROW_SLAB, slab, jnp.zeros((2, tc), f32))

    col = pl.BlockSpec((t, tc), lambda j: (0, j))
    lb = pl.BlockSpec((2, tc), lambda j: (0, j))
    return pl.pallas_call(
        body, name="hgrn_pre_bwd", grid=(GDN_WIDTH // tc,),
        in_specs=[pl.BlockSpec((t, tc), lambda j: (0, j + OFF_QB // tc)), pl.BlockSpec((t, tc), lambda j: (0, j + OFF_FB // tc)),
                  lb, col, col, col],
        out_specs=[col, col, lb], out_shape=[_sds((t, GDN_WIDTH), bf16)] * 2 + [_sds((2, GDN_WIDTH), f32)],
        compiler_params=_cparams(("parallel",)))(p, p, lbw, dq, dk, dlf)


def _hgrn_step(st, q, k, lf, v):
    c = HGRN_CHUNK
    nh = q.shape[0]
    r2 = lax.broadcasted_iota(jnp.int32, (c, c), 0)
    c2 = lax.broadcasted_iota(jnp.int32, (c, c), 1)
    tri = jnp.broadcast_to((r2 >= c2).astype(f32), (nh, c, c))
    i3 = lax.broadcasted_iota(jnp.int32, (c, c, HEAD_DIM), 0)
    j3 = lax.broadcasted_iota(jnp.int32, (c, c, HEAD_DIM), 1)
    mask = i3 >= j3
    outs = []
    for n in range(q.shape[1] // c):
        sl = slice(n * c, (n + 1) * c)
        qc, kc, lc, vc = q[:, sl], k[:, sl], lf[:, sl], v[:, sl]
        b = _hdot(tri, lc, precision=HI)
        rel = jnp.where(mask, jnp.exp(jnp.where(mask, b[:, :, None, :] - b[:, None, :, :], 0.0)), 0.0)
        scores = jnp.sum(qc[:, :, None, :] * kc[:, None, :, :] * rel, axis=-1)
        bl = b[:, c - 1:c, :]
        o = _hbdot(scores, vc) + _hbdot(qc * jnp.exp(b), st, BNT)
        st = st * jnp.exp(bl) + _hbdot(vc, kc * jnp.exp(bl - b), BTN)
        outs.append(o)
    return jnp.concatenate(outs, axis=1), st


def _hgrn_core_fwd(q, k, lf, p):
    t = q.shape[0]
    hd = HEAD_DIM
    rs = min(HGRN_STEP, t)
    n = t // rs

    def body(q_ref, k_ref, lf_ref, v_ref, o_ref, st_ref, s_ref):
        @pl.when(pl.program_id(0) == 0)
        def _():
            s_ref[...] = jnp.zeros_like(s_ref)

        s = s_ref[...]
        st_ref[:, 0] = s
        o, s_new = _hgrn_step(s, *[_heads_major(r) for r in (q_ref, k_ref, lf_ref, v_ref)])
        for h in range(HEADS):
            o_ref[:, h * hd:(h + 1) * hd] = o[h]
        s_ref[...] = s_new

    tile = pl.BlockSpec((rs, GDN_WIDTH), lambda i: (i, 0))
    return pl.pallas_call(
        body, name="hgrn_core_fwd", grid=(n,),
        in_specs=[tile, tile, tile, pl.BlockSpec((rs, GDN_WIDTH), lambda i: (i, OFF_IB // GDN_WIDTH))],
        out_specs=[tile, pl.BlockSpec((HEADS, 1, hd, hd), lambda i: (0, i, 0, 0))],
        out_shape=[_sds((t, GDN_WIDTH), f32), _sds((HEADS, n, hd, hd), f32)],
        scratch_shapes=[pltpu.VMEM((HEADS, hd, hd), f32)],
        compiler_params=_cparams(("arbitrary",)))(q, k, lf, p)


def _hgrn_core_bwd(q, k, lf, p, states, do):
    t = q.shape[0]
    hd = HEAD_DIM
    rs = min(HGRN_STEP, t)
    n = t // rs

    def body(q_ref, k_ref, lf_ref, v_ref, st_ref, do_ref, dq_ref, dk_ref, dlf_ref, dv_ref, ds_ref):
        @pl.when(pl.program_id(0) == 0)
        def _():
            ds_ref[...] = jnp.zeros_like(ds_ref)

        _, vjp = jax.vjp(_hgrn_step, st_ref[:, 0], *[_heads_major(r) for r in (q_ref, k_ref, lf_ref, v_ref)])
        ds, *dins = vjp((_heads_major(do_ref), ds_ref[...]))
        ds_ref[...] = ds
        for d_ref, d in zip((dq_ref, dk_ref, dlf_ref, dv_ref), dins):
            for h in range(HEADS):
                d_ref[:, h * hd:(h + 1) * hd] = d[h].astype(d_ref.dtype)

    tile = pl.BlockSpec((rs, GDN_WIDTH), lambda i: (n - 1 - i, 0))
    return pl.pallas_call(
        body, name="hgrn_core_bwd", grid=(n,),
        in_specs=[tile, tile, tile, pl.BlockSpec((rs, GDN_WIDTH), lambda i: (n - 1 - i, OFF_IB // GDN_WIDTH)),
                  pl.BlockSpec((HEADS, 1, hd, hd), lambda i: (0, n - 1 - i, 0, 0)), tile],
        out_specs=[tile] * 4, out_shape=[_sds((t, GDN_WIDTH), f32)] * 3 + [_sds((t, GDN_WIDTH), bf16)],
        scratch_shapes=[pltpu.VMEM((HEADS, hd, hd), f32)],
        compiler_params=_cparams(("arbitrary",)))(q, k, lf, p, states, do)


def _row(v):
    return v.reshape(1, -1)


def _anchored(w, row, key):
    tok = w.get(key)
    return row if tok is None else row + tok[0, 0]


def _pad_lanes(v, n=HEAD_DIM):
    return jnp.pad(v.reshape(1, -1), ((0, 0), (0, n - v.shape[-1])))


def _ffn_fwd(x, w, l):
    h = _rms_fwd(x, _anchored(w, _row(w['norm_ffn'][l]), ('fwdf', l)), "ffn_norm")
    u = _mm(h, w['ffn_w_up'][l], tb=True, name="ffn_up")
    a = _ffn_act_fwd(u, w['ffn_conv_w'][l], _row(w['ffn_conv_b'][l]))
    y = _mm(a, w['ffn_w_down'][l], add=x, name="ffn_down")
    return y, (x, h, u)


def _ffn_bwd(saved, w, l, dy, dyb, grads):
    x, h, u = saved
    da = _mm(dyb, w['ffn_w_down'][l], tb=True, name="ffn_down_dx")
    a, dg, dv, dcw, dcb = _ffn_act_bwd(u, w['ffn_conv_w'][l], _anchored(w, _row(w['ffn_conv_b'][l]), ('bwd', l)), da)
    grads['ffn_w_down'][l] = _mm(a, dyb, ta=True, out_dtype=bf16, name="ffn_down_dw")
    du = jnp.concatenate([dg, dv], axis=1)
    grads['ffn_w_up'][l] = _mm(du, h, ta=True, out_dtype=bf16, name="ffn_up_dw")
    dh = _mm(du, w['ffn_w_up'][l], name="ffn_up_dx")
    dx, dxb, dgain = _rms_bwd(x, _row(w['norm_ffn'][l]), dh, dy, "ffn_norm_bwd")
    grads['ffn_conv_w'][l] = dcw
    grads['ffn_conv_b'][l] = dcb[0]
    grads['norm_ffn'][l] = dgain[0]
    return dx, dxb


def _odd_fwd(x, w, l, j):
    h = _rms_fwd(x, _anchored(w, _row(w['norm_mix'][l]), ('fwd', l)), "mix_norm")
    p = _mm(h, w['c_w_in'][j], name="lru_in")
    xc = _col_conv_fwd(p, LRU_WIDTH, w['c_conv_w'][j], _row(w['c_conv_b'][j]), LRU_CONV, 256, "lru_conv_fwd")
    out, hs, a = _lru_fwd(p, xc, w['c_gate_a_w'][j], _row(w['c_gate_a_b'][j]), w['c_gate_x_w'][j],
                          _row(w['c_gate_x_b'][j]), _row(w['c_lambda'][j]))
    y = _mm(out, w['c_w_out'][j], add=x, name="lru_out")
    return y, (x, h, p, xc, out, hs, a)


def _odd_bwd(saved, w, l, j, dy, dyb, grads):
    x, h, p, xc, out, hs, a = saved
    dout = _mm(dyb, w['c_w_out'][j], tb=True, name="lru_out_dx")
    grads['c_w_out'][j] = _mm(out, dyb, ta=True, out_dtype=bf16, name="lru_out_dw")
    dyb_, da, du = _lru_bwd_scan(p, a, hs, dout)
    dxc, dwa, dwx, dba, dbx, dlam = _lru_bwd_gates(xc, da, du, w['c_gate_a_w'][j], _row(w['c_gate_a_b'][j]),
                                                   w['c_gate_x_w'][j], _row(w['c_gate_x_b'][j]), _row(w['c_lambda'][j]))
    dxb_, dcw, dcb = _col_conv_bwd(p, LRU_WIDTH, w['c_conv_w'][j], dxc, LRU_CONV, 256, "lru_conv_bwd")
    dp = jnp.concatenate([dyb_, dxb_], axis=1)
    grads['c_w_in'][j] = _mm(h, dp, ta=True, out_dtype=bf16, name="lru_in_dw")
    dh = _mm(dp, w['c_w_in'][j], tb=True, name="lru_in_dx")
    dx, dxb, dgain = _rms_bwd(x, _row(w['norm_mix'][l]), dh, dy, "mix_norm_bwd")
    grads['c_gate_a_w'][j], grads['c_gate_x_w'][j] = dwa, dwx
    grads['c_gate_a_b'][j], grads['c_gate_x_b'][j], grads['c_lambda'][j] = dba[0], dbx[0], dlam[0]
    grads['c_conv_w'][j], grads['c_conv_b'][j] = dcw, dcb[0]
    grads['norm_mix'][l] = dgain[0]
    return dx, dxb


def _even_fwd(x, w, l, j):
    h = _rms_fwd(x, _anchored(w, _row(w['norm_mix'][l]), ('fwd', l)), "mix_norm")
    p = _mm(h, w['ab_w_in'][j], name="ab_in")
    alog, dtb = _pad_lanes(w['gdn_a_log'][j]), _pad_lanes(w['gdn_dt_bias'][j])
    q, k, v, be, ge = _gdn_pre_fwd(p, w['gdn_conv_w'][j], alog, dtb)
    oa, *sa = _gdn_core_fwd(q, k, v, ge, be)
    ya = _post_fwd(oa, p, OFF_Z, _row(w['gdn_norm'][j]), "gdn_post_fwd")
    qq, kk, lf = _hgrn_pre_fwd(p, w['hgrn_lower_bounds'], j)
    ob, sb = _hgrn_core_fwd(qq, kk, lf, p)
    yb = _post_fwd(ob, p, OFF_GB, _row(w['hgrn_norm'][j]), "hgrn_post_fwd")
    mix = jnp.concatenate([ya, yb], axis=1)
    y = _mm(mix, w['ab_w_out'][j], add=x, name="ab_out")
    return y, (x, h, p, q, k, v, be, ge, oa, sa, qq, kk, lf, ob, sb, mix)


def _even_bwd(saved, w, l, j, dy, dyb, grads):
    x, h, p, q, k, v, be, ge, oa, sa, qq, kk, lf, ob, sb, mix = saved
    alog, dtb = _pad_lanes(w['gdn_a_log'][j]), _pad_lanes(w['gdn_dt_bias'][j])
    dmix = _mm(dyb, w['ab_w_out'][j], tb=True, name="ab_out_dx")
    grads['ab_w_out'][j] = _mm(mix, dyb, ta=True, out_dtype=bf16, name="ab_out_dw")
    doa, dz, dgn = _post_bwd(oa, p, OFF_Z, _anchored(w, _row(w['gdn_norm'][j]), ('bwdm', l)), dmix, 0, "gdn_post_bwd")
    dob, dgb, dhn = _post_bwd(ob, p, OFF_GB, _row(w['hgrn_norm'][j]), dmix, GDN_WIDTH, "hgrn_post_bwd")
    dq, dk, dv, dge, dbe = _gdn_core_bwd(q, k, v, ge, be, sa, doa)
    dpq, dpk, dpv, dba, dwq, dwk, dwv, dal, ddt = _gdn_pre_bwd(p, w['gdn_conv_w'][j], alog, dtb, dq, dk, dv, dbe, dge)
    dqq, dkk, dlf, dib = _hgrn_core_bwd(qq, kk, lf, p, sb, dob)
    dqb, dfb, dlb = _hgrn_pre_bwd(p, w['hgrn_lower_bounds'], j, dqq, dkk, dlf)
    dp = jnp.concatenate([dpq, dpk, dpv, dz, dqb, dfb, dib, dgb, dba.astype(bf16)], axis=1)
    grads['ab_w_in'][j] = _mm(h, dp, ta=True, out_dtype=bf16, name="ab_in_dw")
    dh = _mm(dp, w['ab_w_in'][j], tb=True, name="ab_in_dx")
    dx, dxb, dgain = _rms_bwd(x, _row(w['norm_mix'][l]), dh, dy, "mix_norm_bwd")
    grads['gdn_conv_w'][j] = jnp.concatenate([dwq, dwk, dwv], axis=1)
    grads['gdn_a_log'][j], grads['gdn_dt_bias'][j] = dal[0, :HEADS], ddt[0, :HEADS]
    grads['gdn_norm'][j], grads['hgrn_norm'][j] = dgn[0], dhn[0]
    grads['hgrn_lower_bounds'].append(dlb)
    grads['norm_mix'][l] = dgain[0]
    return dx, dxb


def _ab_permute(w_in):
    pad = jnp.zeros(w_in.shape[:-1] + (AB_PAD - AB_COLS,), w_in.dtype)
    return jnp.concatenate([w_in[..., :2048], w_in[..., 2056:], w_in[..., 2048:2056], pad], axis=-1)


def _ab_unpermute(g):
    return jnp.concatenate([g[..., :2048], g[..., 4096:4104], g[..., 2048:4096]], axis=-1)


def _block_pad(a, axis, nblk, padded):
    axis = axis % a.ndim
    s = a.shape
    a = a.reshape(s[:axis] + (nblk, s[axis] // nblk) + s[axis + 1:])
    pad = [(0, 0)] * a.ndim
    pad[axis + 1] = (0, padded - s[axis] // nblk)
    return jnp.pad(a, pad).reshape(s[:axis] + (nblk * padded,) + s[axis + 1:])


def _block_unpad(a, axis, nblk, width):
    axis = axis % a.ndim
    s = a.shape
    a = a.reshape(s[:axis] + (nblk, s[axis] // nblk) + s[axis + 1:])
    a = lax.slice_in_dim(a, 0, width, axis=axis + 1)
    return a.reshape(s[:axis] + (nblk * width,) + s[axis + 1:])


def _kernel_layout(w):
    w = dict(w)
    w['ab_w_in'] = _ab_permute(w['ab_w_in'])
    w['ffn_w_up'] = jnp.swapaxes(_block_pad(w['ffn_w_up'], 2, N_DEV, FF_PAD), 1, 2)
    w['ffn_w_down'] = _block_pad(w['ffn_w_down'], 1, 4, FF_PAD)
    w['ffn_conv_w'] = _block_pad(w['ffn_conv_w'], 2, 4, FF_PAD)
    w['ffn_conv_b'] = _block_pad(w['ffn_conv_b'], 1, 4, FF_PAD)
    return w


def _natural_grads(g):
    g = dict(g)
    g['ab_w_in'] = _ab_unpermute(g['ab_w_in'])
    g['ffn_w_up'] = _block_unpad(jnp.swapaxes(g['ffn_w_up'], 1, 2), 2, N_DEV, FF_SHARD)
    g['ffn_w_down'] = _block_unpad(g['ffn_w_down'], 1, 4, FF_SHARD)
    g['ffn_conv_w'] = _block_unpad(g['ffn_conv_w'], 2, 4, FF_SHARD)
    g['ffn_conv_b'] = _block_unpad(g['ffn_conv_b'], 1, 4, FF_SHARD)
    return g


def _local_step(x, target, w, fetch=None, push=None):
    grads = {n: [None] * (DEPTH if n in ('norm_mix', 'norm_ffn') or n.startswith('ffn_') else 2)
             for n in WEIGHTS if n not in ('norm_final', 'hgrn_lower_bounds')}
    grads['hgrn_lower_bounds'] = []
    saved = []
    for l in range(DEPTH):
        j = l // 2
        if fetch is not None:
            fetch(l, x, 'mix')
        x, s_mix = (_even_fwd if l % 2 == 0 else _odd_fwd)(x, w, l, j)
        if fetch is not None:
            fetch(l, x, 'ffn')
        x, s_ffn = _ffn_fwd(x, w, l)
        saved.append((s_mix, s_ffn))
    loss, dx, dxb, dgf = _loss_head(x, _row(w['norm_final']), target)
    for l in reversed(range(DEPTH)):
        j = l // 2
        s_mix, s_ffn = saved[l]
        dx, dxb = _ffn_bwd(s_ffn, w, l, dx, dxb, grads)
        if push is not None:
            push(l, 'ffn', {nm: grads[nm].pop(li) for nm, li in _layer_items(l)[-2:]})
        dx, dxb = (_even_bwd if l % 2 == 0 else _odd_bwd)(s_mix, w, l, j, dx, dxb, grads)
        if push is not None:
            push(l, 'mix', {nm: grads[nm].pop(li) for nm, li in _layer_items(l)[:-2]})
    out = {n: jnp.stack(g) for n, g in grads.items() if n != 'hgrn_lower_bounds' and g}
    out['hgrn_lower_bounds'] = grads['hgrn_lower_bounds'][0] + grads['hgrn_lower_bounds'][1]
    out['norm_final'] = dgf[0]
    return loss[0, 0], dx, out


def _position():
    return lax.axis_index("x"), lax.axis_index("y"), lax.axis_index("c")


BLOCK_LAYOUT = {
    'ab_w_in': ((2, D_MODEL, N_DEV * AB_SHARD_PAD), (2, D_MODEL, AB_SHARD_PAD)),
    'ab_w_out': ((2, N_DEV, 128, D_MODEL), (2, 128, D_MODEL)),
    'c_w_in': ((2, D_MODEL, 2 * LRU_WIDTH), (2, D_MODEL, 256)),
    'c_w_out': ((2, N_DEV, 128, D_MODEL), (2, 128, D_MODEL)),
    'c_gate_a_w': ((2, HEADS, N_DEV, 32, LRU_BLOCK), (2, HEADS, 32, LRU_BLOCK)),
    'c_gate_x_w': ((2, HEADS, N_DEV, 32, LRU_BLOCK), (2, HEADS, 32, LRU_BLOCK)),
    'ffn_w_up': ((DEPTH, N_DEV, FF_PAD, D_MODEL), (DEPTH, FF_PAD, D_MODEL)),
    'ffn_w_down': ((DEPTH, 4, FF_PAD, D_MODEL), (DEPTH, FF_ROWS, D_MODEL)),
}


COL_WINDOW = {'ab_w_in': AB_SHARD_PAD, 'c_w_in': 256}


def _block_index(name, p):
    d = 4 * p[0] + 2 * p[1] + p[2]
    if name in COL_WINDOW:
        return (slice(None), pl.ds(pl.multiple_of(d * COL_WINDOW[name], 128), COL_WINDOW[name]))
    if name == 'ffn_w_down':
        return (2 * p[0] + p[1], pl.ds(pl.multiple_of(p[2] * FF_ROWS, 16), FF_ROWS), slice(None))
    if name in ('c_gate_a_w', 'c_gate_x_w'):
        return (slice(None), d)
    return (d,)


def _block_of(name, ref, p, layered=True):
    idx = _block_index(name, p)
    if layered and name in BLOCK_LAYOUT:
        idx = (slice(None),) + idx
    return ref.at[idx]


def _layer_items(l):
    j = l // 2
    mix = ([('ab_w_in', j), ('ab_w_out', j)] if l % 2 == 0 else
           [('c_w_in', j), ('c_w_out', j), ('c_gate_a_w', j), ('c_gate_x_w', j)])
    return mix + [('ffn_w_up', l), ('ffn_w_down', l)]


def _own_land(name, shard_l, pos):
    x, y, c = pos
    d = 4 * x + 2 * y + c
    shape = BLOCK_LAYOUT[name][0][1:] if name in BLOCK_LAYOUT else (N_DEV,) + shard_l.shape
    zeros = jnp.zeros(shape, shard_l.dtype) if name == 'ffn_w_down' else lax.empty(shape, shard_l.dtype)
    if name in COL_WINDOW:
        return lax.dynamic_update_slice(zeros, shard_l, (0, d * COL_WINDOW[name]))
    if name == 'ffn_w_down':
        return lax.dynamic_update_slice(zeros, shard_l[None], (2 * x + y, c * FF_ROWS, 0))
    if name in ('c_gate_a_w', 'c_gate_x_w'):
        return lax.dynamic_update_slice(zeros, shard_l[:, None], (0, d, 0, 0))
    return lax.dynamic_update_slice(zeros, shard_l[None], (d,) + (0,) * shard_l.ndim)


def _place_own(items, shards, posv, name):
    n = len(items)
    down = [i for i, (nm, _) in enumerate(items) if nm == 'ffn_w_down']
    in_specs, out_specs, out_shapes, operands = [], [], [], []
    for nm, li in items:
        sh = shards[nm]
        shard_shape = sh.shape if li is None else sh.shape[1:]
        z = (0,) * len(shard_shape)
        operands.append(sh)
        in_specs.append(pl.BlockSpec(shard_shape, lambda i, d, q, c, z=z: z) if li is None else
                        pl.BlockSpec((1,) + shard_shape, lambda i, d, q, c, li=li, z=z: (li,) + z))
        out_shapes.append(_sds(BLOCK_LAYOUT[nm][0][1:] if nm in BLOCK_LAYOUT else (N_DEV,) + sh.shape, sh.dtype))
        if nm in COL_WINDOW:
            out_specs.append(pl.BlockSpec(shard_shape, lambda i, d, q, c: (0, d[0])))
        elif nm == 'ffn_w_down':
            out_specs.append(pl.BlockSpec((1,) + shard_shape, lambda i, d, q, c: (q[0], c[0], 0)))
        elif nm in ('c_gate_a_w', 'c_gate_x_w'):
            out_specs.append(pl.BlockSpec((HEADS, 1) + shard_shape[1:], lambda i, d, q, c: (0, d[0], 0, 0)))
        else:
            out_specs.append(pl.BlockSpec((1,) + shard_shape, lambda i, d, q, c, z=z: (d[0],) + z))

    def body(d_ref, q_ref, c_ref, *refs):
        for i, (nm, li) in enumerate(items):
            v = refs[i][...] if li is None else refs[i][0]
            o_ref = refs[n + len(down) + i]
            if nm in COL_WINDOW:
                o_ref[...] = v
            elif nm in ('c_gate_a_w', 'c_gate_x_w'):
                o_ref[:, 0] = v
            else:
                o_ref[0] = v

    zeros = [jnp.zeros(out_shapes[i].shape, out_shapes[i].dtype) for i in down]
    return pl.pallas_call(
        body, name=name, out_shape=out_shapes,
        grid_spec=pltpu.PrefetchScalarGridSpec(
            num_scalar_prefetch=3, grid=(1,), in_specs=in_specs + [pl.BlockSpec(memory_space=pl.ANY)] * len(down),
            out_specs=out_specs),
        input_output_aliases={3 + n + k: i for k, i in enumerate(down)},
        compiler_params=_cparams(("arbitrary",)))(*posv, *operands, *zeros)


def _src_of(shard_ref, li):
    return shard_ref if li is None else shard_ref.at[li]


def _gather_now(items, shards, lands):
    n = len(items)
    srcs = sorted({nm for nm, _ in items})

    def body(*refs):
        ins = dict(zip(srcs, refs[:len(srcs)]))
        outs = refs[len(srcs) + n:len(srcs) + 2 * n]
        send_sems, recv_sems = refs[len(srcs) + 2 * n:]
        x, y, c = _position()
        me, sibling = (x, y, c), (x, y, 1 - c)
        chips = [(1 - x, y), (x, 1 - y), (1 - x, 1 - y)]

        def copy(i, k, block, to, own=False):
            nm, li = items[i]
            dst = _block_of(nm, outs[i], block, layered=False)
            return pltpu.make_async_remote_copy(
                src_ref=_src_of(ins[nm], li) if own else dst, dst_ref=dst, send_sem=send_sems.at[7 * i + k],
                recv_sem=recv_sems.at[7 * i + k], device_id=to, device_id_type=MESH)

        first = []
        for i in range(n):
            first.append(copy(i, 0, me, sibling, own=True))
            first += [copy(i, 1 + j, me, (*chip, c), own=True) for j, chip in enumerate(chips)]
        for cp in first:
            cp.start()
        passed = []
        for j, chip in enumerate(chips):
            for i in range(n):
                copy(i, 1 + j, (*chip, c), me).wait_recv()
                fwd = copy(i, 4 + j, (*chip, c), sibling)
                fwd.start()
                passed.append(fwd)
        for i in range(n):
            copy(i, 0, sibling, me).wait_recv()
        for j, chip in enumerate(chips):
            for i in range(n):
                copy(i, 4 + j, (*chip, 1 - c), me).wait_recv()
        for cp in first + passed:
            cp.wait_send()

    any_spec = pl.BlockSpec(memory_space=pl.ANY)
    return pl.pallas_call(
        body, name="gather_first_layer", out_shape=[_sds(a.shape, a.dtype) for a in lands],
        in_specs=[any_spec] * (len(srcs) + n), out_specs=[any_spec] * n,
        input_output_aliases={len(srcs) + i: i for i in range(n)},
        scratch_shapes=[pltpu.SemaphoreType.DMA((7 * n,)), pltpu.SemaphoreType.DMA((7 * n,))],
    )(*[shards[nm] for nm in srcs], *lands)


FIRST_HOP = (1, 2, 4, 6)


def _lanes(name, land_ref, pos):
    if name == 'ffn_w_down':
        return [(FIRST_HOP, land_ref.at[pl.ds(0, 2), pl.ds(0, 2 * FF_ROWS)])]
    if name in COL_WINDOW:
        return [(FIRST_HOP, land_ref.at[:, pl.ds(0, 4 * COL_WINDOW[name])])]
    if name in ('c_gate_a_w', 'c_gate_x_w'):
        return [(FIRST_HOP, land_ref.at[:, pl.ds(0, 4)])]
    return [(FIRST_HOP, land_ref.at[pl.ds(0, 4)])]


def _n_lanes(items):
    return len(items)


def _gather_forward(items, lands, name):
    n = len(items)

    def body(*refs):
        outs = refs[n:2 * n]
        send_sems, recv_sems = refs[2 * n:]
        x, y, c = _position()
        chips = [(1 - x, y), (x, 1 - y), (1 - x, 1 - y)]
        copies, arrivals = [], []
        for i, (nm, _) in enumerate(items):
            for j, chip in enumerate(chips):
                mine = _block_of(nm, outs[i], (*chip, c), layered=False)
                theirs = _block_of(nm, outs[i], (*chip, 1 - c), layered=False)
                copies.append(pltpu.make_async_remote_copy(
                    src_ref=mine, dst_ref=mine, send_sem=send_sems.at[3 * i + j], recv_sem=recv_sems.at[3 * i + j],
                    device_id=(x, y, 1 - c), device_id_type=MESH))
                arrivals.append(pltpu.make_async_remote_copy(
                    src_ref=theirs, dst_ref=theirs, send_sem=send_sems.at[3 * i + j], recv_sem=recv_sems.at[3 * i + j],
                    device_id=(x, y, 1 - c), device_id_type=MESH))
        for cp in copies:
            cp.start()
        for cp in arrivals:
            cp.wait_recv()
        for cp in copies:
            cp.wait_send()

    any_spec = pl.BlockSpec(memory_space=pl.ANY)
    return pl.pallas_call(
        body, name=name, out_shape=[_sds(a.shape, a.dtype) for a in lands],
        in_specs=[any_spec] * n, out_specs=[any_spec] * n, input_output_aliases={i: i for i in range(n)},
        scratch_shapes=[pltpu.SemaphoreType.DMA((3 * n,)), pltpu.SemaphoreType.DMA((3 * n,))],
    )(*lands)


HBM_SPEC = pl.BlockSpec(memory_space=pltpu.HBM)
SEM_SPEC = pl.BlockSpec(memory_space=pltpu.SEMAPHORE)
SIDE_EFFECT = pltpu.SideEffectType.DATAFLOW_SIDE_EFFECTING


def _gather_start(items, shards, lands, token, name):
    n = len(items)
    srcs = sorted({nm for nm, _ in items})
    ns, nl = len(srcs), _n_lanes(items)

    def body(*refs):
        ins = dict(zip(srcs, refs[:ns]))
        land_refs = refs[ns:ns + n]
        sems = refs[ns + n + 1:ns + n + 1 + 2 * nl]
        x, y, c = _position()
        me = (x, y, c)
        lane = 0
        for i, (nm, li) in enumerate(items):
            for codes, _ in _lanes(nm, land_refs[i], me):
                for k in codes:
                    peer = (1 - x if (k >> 2) & 1 else x, 1 - y if (k >> 1) & 1 else y, 1 - c if k & 1 else c)
                    pltpu.make_async_remote_copy(
                        src_ref=_src_of(ins[nm], li), dst_ref=_block_of(nm, land_refs[i], me, layered=False),
                        send_sem=sems[2 * lane], recv_sem=sems[2 * lane + 1], device_id=peer, device_id_type=MESH).start()
                lane += 1
        refs[-1][...] = jnp.zeros((8, 128), f32)

    hbm = [pltpu.with_memory_space_constraint(a, pltpu.HBM) for a in [shards[nm] for nm in srcs] + list(lands)]
    outs = pl.pallas_call(
        body, name=name,
        out_shape=[pltpu.SemaphoreType.DMA(())] * (2 * nl) + [pltpu.HBM(a.shape, a.dtype) for a in hbm] + [_sds((8, 128), f32)],
        in_specs=[HBM_SPEC] * (ns + n) + [pl.BlockSpec(memory_space=pl.ANY)],
        out_specs=[SEM_SPEC] * (2 * nl) + [HBM_SPEC] * (ns + n) + [pl.BlockSpec(memory_space=pltpu.VMEM)],
        input_output_aliases={i: 2 * nl + i for i in range(ns + n)},
        compiler_params=pltpu.CompilerParams(has_side_effects=SIDE_EFFECT),
    )(*hbm, token)
    return outs[:2 * nl], dict(zip(srcs, outs[2 * nl:2 * nl + ns])), outs[2 * nl + ns:-1], outs[-1]


def _gather_wait(items, sems, shards, lands, after, name):
    n = len(items)
    srcs = sorted(shards)
    ns, nl = len(srcs), _n_lanes(items)

    def body(*refs):
        land_refs = refs[ns:ns + n]
        sem_refs = refs[ns + n:ns + n + 2 * nl]
        x, y, c = _position()
        lane = 0
        for i, (nm, _) in enumerate(items):
            for _, moved in _lanes(nm, land_refs[i], (x, y, c)):
                cp = pltpu.make_async_remote_copy(
                    src_ref=moved, dst_ref=moved, send_sem=sem_refs[2 * lane], recv_sem=sem_refs[2 * lane + 1],
                    device_id=(x, y, 1 - c), device_id_type=MESH)
                cp.wait_send()
                cp.wait_recv()
                lane += 1

    outs = pl.pallas_call(
        body, name=name, out_shape=[pltpu.HBM(shards[nm].shape, shards[nm].dtype) for nm in srcs]
        + [pltpu.HBM(a.shape, a.dtype) for a in lands],
        in_specs=[HBM_SPEC] * (ns + n) + [SEM_SPEC] * (2 * nl) + [pl.BlockSpec(memory_space=pl.ANY)],
        out_specs=[HBM_SPEC] * (ns + n), input_output_aliases={i: i for i in range(ns + n)},
        compiler_params=pltpu.CompilerParams(has_side_effects=SIDE_EFFECT),
    )(*[shards[nm] for nm in srcs], *lands, *sems, after)
    return dict(zip(srcs, outs[:ns])), outs[ns:]


def _exchange_grads(fulls, rep):
    cpos = lax.axis_index("c").astype(jnp.int32).reshape(1)
    pair, rep_pair = _pair_exchange(fulls, rep)
    chip = {nm: _chip_sum(nm, fulls[nm], pair[nm], cpos) for nm in fulls}
    rep_chip = _add_pair(rep, rep_pair, "chip_sum_replicated")
    cross, cross_rep = _cross_exchange(chip, rep_chip)
    return chip, rep_chip, cross, cross_rep


def _pair_exchange(fulls, rep, tag=""):
    names = list(fulls)
    n = len(names)
    shard_shape = {nm: ((fulls[nm].shape[0],) + BLOCK_LAYOUT[nm][1][1:] if nm in BLOCK_LAYOUT else fulls[nm].shape[1:])
                   for nm in names}

    def body(*refs):
        ins = dict(zip(names, refs[:n]))
        rep_ref = refs[n]
        pair = dict(zip(names, refs[n + 1:2 * n + 1]))
        rpair_ref = refs[2 * n + 1]
        send_sems, recv_sems = refs[2 * n + 2:]
        x, y, c = _position()
        sibling = (x, y, 1 - c)
        remote = []
        for i, nm in enumerate(names):
            for q in range(4):
                remote.append(pltpu.make_async_remote_copy(
                    src_ref=_block_of(nm, ins[nm], (q >> 1, q & 1, 1 - c)), dst_ref=pair[nm].at[q],
                    send_sem=send_sems.at[4 * i + q], recv_sem=recv_sems.at[4 * i + q], device_id=sibling,
                    device_id_type=MESH))
        remote.append(pltpu.make_async_remote_copy(
            src_ref=rep_ref, dst_ref=rpair_ref, send_sem=send_sems.at[4 * n], recv_sem=recv_sems.at[4 * n],
            device_id=sibling, device_id_type=MESH))
        for cp in remote:
            cp.start()
        for cp in remote:
            cp.wait_recv()
        for cp in remote:
            cp.wait_send()

    any_spec = pl.BlockSpec(memory_space=pl.ANY)
    four = [_sds((4,) + tuple(shard_shape[nm]), fulls[nm].dtype) for nm in names]
    outs = pl.pallas_call(
        body, name="grad_pair_exchange" + tag, out_shape=four + [_sds(rep.shape, rep.dtype)],
        in_specs=[any_spec] * (n + 1), out_specs=[any_spec] * (n + 1),
        scratch_shapes=[pltpu.SemaphoreType.DMA((4 * n + 1,)), pltpu.SemaphoreType.DMA((4 * n + 1,))],
    )(*[fulls[nm] for nm in names], rep)
    return dict(zip(names, outs[:n])), outs[n]


def _chip_sum(name, full, pair, cpos, tag=""):
    if name in COL_WINDOW:
        width = BLOCK_LAYOUT[name][1][-1]
        rows = full.shape[0] * full.shape[1]
        tr = 512

        def body(c_ref, f_ref, p_ref, o_ref):
            o_ref[0] = (f_ref[...].astype(f32) + p_ref[0].astype(f32)).astype(o_ref.dtype)

        slot = pl.BlockSpec((1, tr, width), lambda q, i, c: (q, i, 0))
        out = pl.pallas_call(
            body, name="chip_sum_" + name + tag, out_shape=_sds((4, rows, width), full.dtype),
            grid_spec=pltpu.PrefetchScalarGridSpec(
                num_scalar_prefetch=1, grid=(4, rows // tr),
                in_specs=[pl.BlockSpec((tr, width), lambda q, i, c: (i, 2 * q + c[0])), slot], out_specs=slot),
            compiler_params=_cparams(("parallel", "parallel")))(
            cpos, full.reshape(rows, N_DEV * width), pair.reshape(4, rows, width))
        return out.reshape(pair.shape)

    if name == 'ffn_w_down':
        f4, p4 = full, pair
        fspec = pl.BlockSpec((full.shape[0], 1, FF_ROWS, D_MODEL), lambda q, c: (0, q, c[0], 0))
    else:
        shard = pair.shape[1:]
        lead = int(np.prod(shard[:-2]))
        f4 = full.reshape((lead, N_DEV) + shard[-2:])
        p4 = pair.reshape((4, lead) + shard[-2:])
        fspec = pl.BlockSpec((lead, 1) + shard[-2:], lambda q, c: (0, 2 * q + c[0], 0, 0))

    def body4(c_ref, f_ref, p_ref, o_ref):
        o_ref[0] = (f_ref[:, 0].astype(f32) + p_ref[0].astype(f32)).astype(o_ref.dtype)

    slot = pl.BlockSpec((1,) + p4.shape[1:], lambda q, c: (q, 0, 0, 0))
    out = pl.pallas_call(
        body4, name="chip_sum_" + name + tag, out_shape=_sds(p4.shape, full.dtype),
        grid_spec=pltpu.PrefetchScalarGridSpec(num_scalar_prefetch=1, grid=(4,), in_specs=[fspec, slot], out_specs=slot),
        compiler_params=_cparams(("parallel",)))(cpos, f4, p4)
    return out.reshape(pair.shape)


def _add_pair(a, b, name):
    shp = a.shape
    r, c = int(np.prod(shp[:-1])), shp[-1]
    tr = _tile(r, (512, 256, 128, 64, 32, 16, 8))

    def body(a_ref, b_ref, o_ref):
        o_ref[...] = (a_ref[...].astype(f32) + b_ref[...].astype(f32)).astype(o_ref.dtype)

    tile = pl.BlockSpec((tr, c), lambda i: (i, 0))
    return pl.pallas_call(body, name=name, grid=(r // tr,), in_specs=[tile, tile], out_specs=tile,
                          out_shape=_sds((r, c), a.dtype), compiler_params=_cparams(("parallel",)))(
        a.reshape(r, c), b.reshape(r, c)).reshape(shp)


def _cross_exchange(chip, rep_chip):
    names = list(chip)
    n = len(names)

    def body(*refs):
        ins = dict(zip(names, refs[:n]))
        rep_ref = refs[n]
        outs = dict(zip(names, refs[2 * n + 2:3 * n + 2]))
        rrep_ref = refs[3 * n + 2]
        send_sems, recv_sems = refs[3 * n + 3:]
        x, y, c = _position()
        mine = 2 * x + y
        copies = []
        for k in range(1, 4):
            px, py = (1 - x if (k >> 1) & 1 else x), (1 - y if k & 1 else y)
            for i, nm in enumerate(names + ['']):
                src = rep_ref if i == n else ins[nm].at[2 * px + py]
                dst = (rrep_ref if i == n else outs[nm]).at[mine]
                copies.append(pltpu.make_async_remote_copy(
                    src_ref=src, dst_ref=dst, send_sem=send_sems.at[3 * i + k - 1], recv_sem=recv_sems.at[3 * i + k - 1],
                    device_id=(px, py, c), device_id_type=MESH))
        for cp in copies:
            cp.start()
        for cp in copies:
            cp.wait_recv()
        for cp in copies:
            cp.wait_send()

    any_spec = pl.BlockSpec(memory_space=pl.ANY)
    shapes = [_sds(chip[nm].shape, chip[nm].dtype) for nm in names] + [_sds((4,) + rep_chip.shape, rep_chip.dtype)]
    zeros = [jnp.zeros(s.shape, s.dtype) for s in shapes]
    outs = pl.pallas_call(
        body, name="grad_cross_exchange", out_shape=shapes,
        in_specs=[any_spec] * (2 * n + 2), out_specs=[any_spec] * (n + 1),
        input_output_aliases={n + 1 + i: i for i in range(n + 1)},
        scratch_shapes=[pltpu.SemaphoreType.DMA((3 * (n + 1),)), pltpu.SemaphoreType.DMA((3 * (n + 1),))],
    )(*[chip[nm] for nm in names], rep_chip, *zeros)
    return dict(zip(names, outs[:n])), outs[n]


def _cross_start(chips, name):
    n = len(chips)

    def body(*refs):
        chip_refs, land_refs = refs[:n], refs[n:2 * n]
        sems = refs[2 * n:4 * n]
        x, y, c = _position()
        mine = 2 * x + y
        for i in range(n):
            for k in range(1, 4):
                px, py = (1 - x if (k >> 1) & 1 else x), (1 - y if k & 1 else y)
                pltpu.make_async_remote_copy(
                    src_ref=chip_refs[i].at[2 * px + py], dst_ref=land_refs[i].at[mine], send_sem=sems[2 * i],
                    recv_sem=sems[2 * i + 1], device_id=(px, py, c), device_id_type=MESH).start()
        refs[-1][...] = jnp.zeros((8, 128), f32)

    hbm = [pltpu.with_memory_space_constraint(a, pltpu.HBM) for a in list(chips) + [jnp.zeros(a.shape, a.dtype) for a in chips]]
    outs = pl.pallas_call(
        body, name=name,
        out_shape=[pltpu.SemaphoreType.DMA(())] * (2 * n) + [pltpu.HBM(a.shape, a.dtype) for a in hbm] + [_sds((8, 128), f32)],
        in_specs=[HBM_SPEC] * (2 * n),
        out_specs=[SEM_SPEC] * (2 * n) + [HBM_SPEC] * (2 * n) + [pl.BlockSpec(memory_space=pltpu.VMEM)],
        input_output_aliases={i: 2 * n + i for i in range(2 * n)},
        compiler_params=pltpu.CompilerParams(has_side_effects=SIDE_EFFECT),
    )(*hbm)
    return outs[:2 * n], outs[2 * n:3 * n], outs[3 * n:4 * n], outs[4 * n]


def _cross_wait(sems, chips, lands, after, name):
    n = len(chips)

    def body(*refs):
        land_refs = refs[n:2 * n]
        sem_refs = refs[2 * n:4 * n]
        x, y, c = _position()
        for i in range(n):
            moved = land_refs[i].at[pl.ds(0, 3)]
            cp = pltpu.make_async_remote_copy(
                src_ref=moved, dst_ref=moved, send_sem=sem_refs[2 * i], recv_sem=sem_refs[2 * i + 1],
                device_id=(x, y, 1 - c), device_id_type=MESH)
            cp.wait_send()
            cp.wait_recv()

    outs = pl.pallas_call(
        body, name=name, out_shape=[pltpu.HBM(a.shape, a.dtype) for a in list(chips) + list(lands)],
        in_specs=[HBM_SPEC] * (2 * n) + [SEM_SPEC] * (2 * n) + [pl.BlockSpec(memory_space=pl.ANY)],
        out_specs=[HBM_SPEC] * (2 * n), input_output_aliases={i: i for i in range(2 * n)},
        compiler_params=pltpu.CompilerParams(has_side_effects=SIDE_EFFECT),
    )(*chips, *lands, *sems, after)
    return outs[:n], outs[n:]


def _sum_adamw_layer(parts, own, mine, w, m, v, li, prev, name):
    nl, r, l = w.shape
    lp = parts.shape[2]
    tr = r if r <= 512 else _tile(r, (512, FF_ROWS, 256, 128))
    c1 = 1.0 / (1.0 - ADAM_B1 ** ADAM_STEP)
    c2 = 1.0 / (1.0 - ADAM_B2 ** ADAM_STEP)
    k = 0 if prev is None else 4

    def body(mine_ref, p_ref, o_ref, w_ref, m_ref, v_ref, *rest):
        g_ref, d_ref, nm_ref, nv_ref = rest[k:]
        mine_v = o_ref[0].astype(f32)
        g = jnp.where(mine_ref[0] == 0, mine_v, p_ref[0].astype(f32))
        for s in range(1, 4):
            g = g + jnp.where(mine_ref[0] == s, mine_v, p_ref[s].astype(f32))
        if lp != l:
            g = g[:, :l]
        m_new = ADAM_B1 * m_ref[0] + (1.0 - ADAM_B1) * g
        v_new = ADAM_B2 * v_ref[0] + (1.0 - ADAM_B2) * (g * g)
        g_ref[0] = g
        nm_ref[0] = m_new
        nv_ref[0] = v_new
        d_ref[0] = -ADAM_LR * ((m_new * c1) / (jnp.sqrt(v_new * c2) + ADAM_EPS) + ADAM_WD * w_ref[0])

    tile = pl.BlockSpec((1, tr, l), lambda i, mn: (li, i, 0))
    keep = [pl.BlockSpec(memory_space=pl.ANY)] * k
    return pl.pallas_call(
        body, name=name, out_shape=[_sds((nl, r, l), f32)] * 4,
        grid_spec=pltpu.PrefetchScalarGridSpec(
            num_scalar_prefetch=1, grid=(r // tr,),
            in_specs=[pl.BlockSpec((4, tr, lp), lambda i, mn: (0, i, 0)), pl.BlockSpec((1, tr, lp), lambda i, mn: (mn[0], i, 0)),
                      tile, tile, tile] + keep,
            out_specs=[tile] * 4),
        input_output_aliases={6 + i: i for i in range(k)},
        compiler_params=_cparams(("parallel",)))(mine, parts, own, w, m, v, *(prev or ()))


def _sum_adamw(parts, own, mine, w, m, v, name):
    r, l = w.shape
    lp = parts.shape[2]
    tr = _tile(r, (256, 128, 64, 32, 16, 8))
    c1 = 1.0 / (1.0 - ADAM_B1 ** ADAM_STEP)
    c2 = 1.0 / (1.0 - ADAM_B2 ** ADAM_STEP)

    def body(mine_ref, p_ref, o_ref, w_ref, m_ref, v_ref, g_ref, d_ref, nm_ref, nv_ref):
        mine_v = (o_ref[0] if own.ndim == 3 else o_ref[...]).astype(f32)
        g = jnp.where(mine_ref[0] == 0, mine_v, p_ref[0].astype(f32))
        for s in range(1, parts.shape[0]):
            g = g + jnp.where(mine_ref[0] == s, mine_v, p_ref[s].astype(f32))
        if lp != l:
            g = g[:, :l]
        m_new = ADAM_B1 * m_ref[...] + (1.0 - ADAM_B1) * g
        v_new = ADAM_B2 * v_ref[...] + (1.0 - ADAM_B2) * (g * g)
        g_ref[...] = g
        nm_ref[...] = m_new
        nv_ref[...] = v_new
        d_ref[...] = -ADAM_LR * ((m_new * c1) / (jnp.sqrt(v_new * c2) + ADAM_EPS) + ADAM_WD * w_ref[...])

    tile = pl.BlockSpec((tr, l), lambda i, mn: (i, 0))
    own_spec = (pl.BlockSpec((1, tr, lp), lambda i, mn: (mn[0], i, 0)) if own.ndim == 3
                else pl.BlockSpec((tr, lp), lambda i, mn: (i, 0)))
    return pl.pallas_call(
        body, name=name, out_shape=[_sds((r, l), f32)] * 4,
        grid_spec=pltpu.PrefetchScalarGridSpec(
            num_scalar_prefetch=1, grid=(r // tr,),
            in_specs=[pl.BlockSpec((parts.shape[0], tr, lp), lambda i, mn: (0, i, 0)), own_spec, tile, tile, tile],
            out_specs=[tile] * 4),
        compiler_params=_cparams(("parallel",)))(mine, parts, own, w, m, v)


def _pack(arrs, lead=None):
    if lead is None:
        flat = jnp.concatenate([a.reshape(-1).astype(f32) for a in arrs])
        n = flat.shape[0]
    else:
        flat = jnp.concatenate([a.reshape(lead, -1).astype(f32) for a in arrs], axis=1)
        n = flat.shape[1]
    tot = -(-n // 1024) * 1024
    if lead is None:
        return jnp.pad(flat, (0, tot - n)).reshape(tot // 128, 128)
    return jnp.pad(flat, ((0, 0), (0, tot - n))).reshape(lead, tot // 128, 128)


def _unpack(packed, shapes, lead=False):
    flat = packed.reshape(packed.shape[0], -1) if lead else packed.reshape(-1)
    out, off = [], 0
    for s in shapes:
        n = int(np.prod(s))
        out.append(flat[:, off:off + n].reshape((packed.shape[0],) + tuple(s)) if lead else flat[off:off + n].reshape(s))
        off += n
    return out


def _merge_shards(g, axis):
    g = jnp.moveaxis(g, 0, axis)
    s = g.shape
    return g.reshape(s[:axis] + (s[axis] * s[axis + 1],) + s[axis + 2:])


def _split_shards(full, axis):
    s = full.shape
    g = full.reshape(s[:axis] + (N_DEV, s[axis] // N_DEV) + s[axis + 1:])
    return jnp.moveaxis(g, axis, 0)


def kernel(x, norm_mix, norm_ffn, norm_final, ab_w_in, gdn_conv_w, gdn_a_log, gdn_dt_bias, gdn_norm, hgrn_lower_bounds, hgrn_norm, ab_w_out, c_w_in, c_conv_w, c_conv_b, c_gate_a_w, c_gate_a_b, c_gate_x_w, c_gate_x_b, c_lambda, c_w_out, ffn_w_up, ffn_conv_w, ffn_conv_b, ffn_w_down, loss_target, m_norm_mix, m_norm_ffn, m_norm_final, m_ab_w_in, m_gdn_conv_w, m_gdn_a_log, m_gdn_dt_bias, m_gdn_norm, m_hgrn_lower_bounds, m_hgrn_norm, m_ab_w_out, m_c_w_in, m_c_conv_w, m_c_conv_b, m_c_gate_a_w, m_c_gate_a_b, m_c_gate_x_w, m_c_gate_x_b, m_c_lambda, m_c_w_out, m_ffn_w_up, m_ffn_conv_w, m_ffn_conv_b, m_ffn_w_down, v_norm_mix, v_norm_ffn, v_norm_final, v_ab_w_in, v_gdn_conv_w, v_gdn_a_log, v_gdn_dt_bias, v_gdn_norm, v_hgrn_lower_bounds, v_hgrn_norm, v_ab_w_out, v_c_w_in, v_c_conv_w, v_c_conv_b, v_c_gate_a_w, v_c_gate_a_b, v_c_gate_x_w, v_c_gate_x_b, v_c_lambda, v_c_w_out, v_ffn_w_up, v_ffn_conv_w, v_ffn_conv_b, v_ffn_w_down):
    wl = dict(zip(WEIGHTS, (norm_mix, norm_ffn, norm_final, ab_w_in, gdn_conv_w, gdn_a_log, gdn_dt_bias, gdn_norm, hgrn_lower_bounds, hgrn_norm, ab_w_out, c_w_in, c_conv_w, c_conv_b, c_gate_a_w, c_gate_a_b, c_gate_x_w, c_gate_x_b, c_lambda, c_w_out, ffn_w_up, ffn_conv_w, ffn_conv_b, ffn_w_down)))
    ml = dict(zip(WEIGHTS, (m_norm_mix, m_norm_ffn, m_norm_final, m_ab_w_in, m_gdn_conv_w, m_gdn_a_log, m_gdn_dt_bias, m_gdn_norm, m_hgrn_lower_bounds, m_hgrn_norm, m_ab_w_out, m_c_w_in, m_c_conv_w, m_c_conv_b, m_c_gate_a_w, m_c_gate_a_b, m_c_gate_x_w, m_c_gate_x_b, m_c_lambda, m_c_w_out, m_ffn_w_up, m_ffn_conv_w, m_ffn_conv_b, m_ffn_w_down)))
    vl = dict(zip(WEIGHTS, (v_norm_mix, v_norm_ffn, v_norm_final, v_ab_w_in, v_gdn_conv_w, v_gdn_a_log, v_gdn_dt_bias, v_gdn_norm, v_hgrn_lower_bounds, v_hgrn_norm, v_ab_w_out, v_c_w_in, v_c_conv_w, v_c_conv_b, v_c_gate_a_w, v_c_gate_a_b, v_c_gate_x_w, v_c_gate_x_b, v_c_lambda, v_c_w_out, v_ffn_w_up, v_ffn_conv_w, v_ffn_conv_b, v_ffn_w_down)))

    big = [n for n in SHARDED if n in MATMUL_WEIGHTS]
    vec = [n for n in SHARDED if n not in MATMUL_WEIGHTS]
    shards = {n: wl[n].astype(bf16) for n in big}
    shards['ab_w_in'] = jnp.pad(shards['ab_w_in'], ((0, 0), (0, 0), (0, AB_SHARD_PAD - AB_SHARD)))
    shards['ffn_w_up'] = jnp.pad(jnp.swapaxes(shards['ffn_w_up'], 1, 2), ((0, 0), (0, FF_PAD - FF_SHARD), (0, 0)))
    shards['vec'] = _pack([wl[n] for n in vec])
    pos = _position()
    layer_items = [_layer_items(l) for l in range(DEPTH)]
    posv = [v.astype(jnp.int32).reshape(1) for v in (4 * pos[0] + 2 * pos[1] + pos[2], 2 * pos[0] + pos[1], pos[2])]
    first_items = layer_items[0] + [('vec', None)]
    lands = [_place_own(first_items, shards, posv, "place_own_0")]
    lands += [_place_own(layer_items[l], shards, posv, "place_own_%d" % l) for l in range(1, DEPTH)]
    mix0, ffn0 = layer_items[0][:-2], layer_items[0][-2:]
    first = _gather_now(mix0 + [('vec', None)], shards, lands[0][:len(mix0)] + lands[0][-1:])
    flight = {'shards': {n: shards[n] for n in big}}

    full = {n: wl[n] for n in REPLICATED}

    def start(items, item_lands, token, name, anchor):
        sems, thru, flight['lands'], tok = _gather_start(items, flight['shards'], item_lands, token, name)
        flight['sems'] = list(sems)
        flight['shards'].update(thru)
        full[anchor] = tok

    start(ffn0, lands[0][len(mix0):-1], first[-1], "gather_start_0", ('fwd', 0))

    for n, a in zip(vec, _unpack(first[-1], [wl[n].shape for n in vec], lead=True)):
        full[n] = _merge_shards(a, SHARD_AXIS[n])
    full['ffn_conv_w'] = _block_pad(full['ffn_conv_w'], 2, 4, FF_PAD)
    full['ffn_conv_b'] = _block_pad(full['ffn_conv_b'], 1, 4, FF_PAD)
    for n in big:
        full[n] = {}

    def arrive(items, x_in, tag):
        flight['shards'], got = _gather_wait(items, flight['sems'], flight['shards'], flight['lands'], x_in,
                                             "gather_wait_" + tag)
        return _gather_forward(items, got, "gather_forward_" + tag)

    def fetch(l, x_in, part):
        if l == 0 and part == 'mix':
            items, got = mix0, first[:len(mix0)]
        elif l == 0:
            items, got = ffn0, arrive(ffn0, x_in, "0")
            start(layer_items[1], lands[1], got[0], "gather_start_1", ('fwdf', 0))
        elif part == 'mix':
            items = layer_items[l]
            got = arrive(items, x_in, str(l))
            if l + 1 < DEPTH:
                start(layer_items[l + 1], lands[l + 1], got[0], "gather_start_%d" % (l + 1), ('fwd', l))
        else:
            return
        for (nm, li), a in zip(items, got):
            if nm == 'ab_w_in':
                a = _ab_permute(_block_unpad(a, 1, N_DEV, AB_SHARD))
            elif nm in ('ab_w_out', 'c_w_out'):
                a = a.reshape(D_MODEL, D_MODEL)
            elif nm in ('c_gate_a_w', 'c_gate_x_w'):
                a = a.reshape(HEADS, LRU_BLOCK, LRU_BLOCK)
            elif nm == 'ffn_w_down':
                a = a.reshape(D_FFP, D_MODEL)
            elif nm == 'ffn_w_up':
                a = a.reshape(2 * D_FFP, D_MODEL)
            full[nm][li] = a

    cpos = lax.axis_index("c").astype(jnp.int32).reshape(1)
    mine = (2 * lax.axis_index("x") + lax.axis_index("y")).astype(jnp.int32).reshape(1)
    pending = {}

    def exchange(key, items, g, anchor):
        fulls = {}
        for nm, _ in items:
            a = g[nm].astype(bf16)
            if nm == 'ab_w_in':
                a = _block_pad(_ab_unpermute(a), 1, N_DEV, AB_SHARD_PAD)
            fulls[nm] = a.reshape((1,) + BLOCK_LAYOUT[nm][0][1:])
        pair, _ = _pair_exchange(fulls, jnp.zeros((8, 128), f32), "_" + key)
        chips = [_chip_sum(nm, fulls[nm], pair[nm], cpos, "_" + key) for nm in fulls]
        sems, chips, crosses, tok = _cross_start(chips, "grad_cross_start_" + key)
        pending[key] = (items, sems, chips, crosses)
        full[anchor] = tok

    stash = {}

    def push(l, part, g):
        if l == 0:
            exchange("0f" if part == 'ffn' else "0", ffn0 if part == 'ffn' else mix0, g,
                     ('bwdm', 0) if part == 'ffn' else ('bwd', -1))
            return
        stash.update(g)
        if part == 'mix':
            exchange(str(l), layer_items[l], dict(stash), ('bwd', l - 1))
            stash.clear()

    loss, dx, grads = _local_step(x[0], loss_target[0], full, fetch, push)

    grads['ffn_conv_w'] = _block_unpad(grads['ffn_conv_w'], 2, 4, FF_SHARD)
    grads['ffn_conv_b'] = _block_unpad(grads['ffn_conv_b'], 1, 4, FF_SHARD)
    fulls = {'vec': _pack([_split_shards(grads[n], SHARD_AXIS[n]) for n in vec], lead=N_DEV)}
    chip, rep_chip, recv, rrep = _exchange_grads(fulls, _pack([grads[n] for n in REPLICATED]))
    res = {}
    stacked = {}
    after = full['bwd', -1]
    for key in ("3", "2", "1", "0f", "0"):
        items, sems, chips, lands = pending[key]
        chips, lands = _cross_wait(sems, chips, lands, after, "grad_cross_wait_" + key)
        for (n, li), own, parts in zip(items, chips, lands):
            if n == 'ffn_w_up':
                wmv = [jnp.swapaxes(a[n], 1, 2) for a in (wl, ml, vl)]
                rp = FF_PAD
            else:
                shp = wl[n].shape
                rp = int(np.prod(shp[1:-1]))
                wmv = [a[n].reshape(shp[0], rp, shp[-1]) for a in (wl, ml, vl)]
            stacked[n] = _sum_adamw_layer(parts.reshape(4, rp, -1), own.reshape(4, rp, -1), mine, *wmv, li,
                                          stacked.get(n), "adamw_%s_%d" % (n, li))
        if key == "0f":
            after = stacked['ffn_w_up'][0]
    for n in big:
        for kind, o in zip(("grad", "delta", "new_m", "new_v"), stacked[n]):
            res[kind, n] = jnp.swapaxes(o, 1, 2) if n == 'ffn_w_up' else o.reshape(wl[n].shape)
    for names, parts, own, tag in ((vec, recv['vec'], chip['vec'], "adamw_vectors"),
                                   (REPLICATED, rrep, rep_chip, "adamw_replicated")):
        outs = _sum_adamw(parts, own, mine, _pack([wl[n] for n in names]), _pack([ml[n] for n in names]),
                          _pack([vl[n] for n in names]), tag)
        for kind, o in zip(("grad", "delta", "new_m", "new_v"), outs):
            for n, a in zip(names, _unpack(o, [wl[n].shape for n in names])):
                res[kind, n] = a

    loss = lax.psum(loss, ("x", "y", "c"))
    return (loss, dx[None], *[res[kind, n] for kind in ("grad", "delta", "new_m", "new_v") for n in WEIGHTS])
```

```python
import functools

import numpy as np
import jax
import jax.numpy as jnp
from jax import lax
from jax.experimental import pallas as pl
from jax.experimental.pallas import tpu as pltpu

f32 = jnp.float32
bf16 = jnp.bfloat16
HI = lax.Precision.HIGHEST
MESH = pl.DeviceIdType.MESH

N_DEV = 8
D_MODEL = 1024
DEPTH = 4
EPS = 1e-6
F_FLOOR = 1e-30
HEADS = 4
HEAD_DIM = 128
GDN_WIDTH = 512
GDN_CONV = 4
GDN_CHUNK = 64
HGRN_CHUNK = 16
HGRN_STEP = 128
MIX_WIDTH = 1024
AB_COLS = 4104
AB_PAD = 4224
LRU_WIDTH = 1024
LRU_BLOCK = 256
LRU_CONV = 4
RG_C = 8.0
D_FF = 2816
FF_SHARD = 704
FF_PAD = 768
D_FFP = 4 * FF_PAD
FF_ROWS = 352
AB_SHARD, AB_SHARD_PAD = 513, 640
FFN_CONV = 3
ADAM_LR, ADAM_B1, ADAM_B2, ADAM_EPS, ADAM_WD, ADAM_STEP = 0.001, 0.9, 0.999, 1e-08, 0.01, 10
VMEM_LIMIT = 56 * 1024 * 1024
ROW_SLAB = 32
PACK_LANES = 512
PACK_ROWS = 256

OFF_Q, OFF_K, OFF_V, OFF_Z, OFF_QB, OFF_FB, OFF_IB, OFF_GB, OFF_BA = 0, 512, 1024, 1536, 2048, 2560, 3072, 3584, 4096

WEIGHTS = ['norm_mix', 'norm_ffn', 'norm_final', 'ab_w_in', 'gdn_conv_w', 'gdn_a_log', 'gdn_dt_bias', 'gdn_norm',
           'hgrn_lower_bounds', 'hgrn_norm', 'ab_w_out', 'c_w_in', 'c_conv_w', 'c_conv_b', 'c_gate_a_w', 'c_gate_a_b',
           'c_gate_x_w', 'c_gate_x_b', 'c_lambda', 'c_w_out', 'ffn_w_up', 'ffn_conv_w', 'ffn_conv_b', 'ffn_w_down']
SHARD_AXIS = {'norm_mix': None, 'norm_ffn': None, 'norm_final': None, 'ab_w_in': 2, 'gdn_conv_w': 2, 'gdn_a_log': None,
              'gdn_dt_bias': None, 'gdn_norm': None, 'hgrn_lower_bounds': None, 'hgrn_norm': None, 'ab_w_out': 1,
              'c_w_in': 2, 'c_conv_w': 2, 'c_conv_b': 1, 'c_gate_a_w': 2, 'c_gate_a_b': 1, 'c_gate_x_w': 2,
              'c_gate_x_b': 1, 'c_lambda': 1, 'c_w_out': 1, 'ffn_w_up': 2, 'ffn_conv_w': 2, 'ffn_conv_b': None,
              'ffn_w_down': 1}
MATMUL_WEIGHTS = ('ab_w_in', 'ab_w_out', 'c_w_in', 'c_gate_a_w', 'c_gate_x_w', 'c_w_out', 'ffn_w_up', 'ffn_w_down')
SHARDED = [n for n in WEIGHTS if SHARD_AXIS[n] is not None]
REPLICATED = [n for n in WEIGHTS if SHARD_AXIS[n] is None]


def _tile(n, prefs=(512, 384, 256, 128)):
    for p in prefs:
        if n % p == 0:
            return p
    return n


def _cparams(sem=None):
    kw = dict(vmem_limit_bytes=VMEM_LIMIT)
    if sem is not None:
        kw['dimension_semantics'] = sem
    return pltpu.CompilerParams(**kw)


def _sds(shape, dtype):
    return jax.ShapeDtypeStruct(tuple(shape), dtype)


def _sigmoid(x):
    return 1.0 / (1.0 + jnp.exp(-x))


def _silu(x):
    return x * (0.5 * jnp.tanh(0.5 * x) + 0.5)


def _log1p(x):
    u = 1.0 + x
    return jnp.where(u == 1.0, x, jnp.log(u) * (x / jnp.where(u == 1.0, 1.0, u - 1.0)))


def _softplus(x):
    return jnp.maximum(x, 0.0) + _log1p(jnp.exp(-jnp.abs(x)))


def _expm1(x):
    small = jnp.abs(x) < 0.05
    xs = jnp.where(small, x, 0.0)
    series = xs * (1.0 + xs * (0.5 + xs * (1.0 / 6.0 + xs * (1.0 / 24.0 + xs * (1.0 / 120.0)))))
    return jnp.where(small, series, jnp.exp(x) - 1.0)


def _gelu(x):
    return 0.5 * x * (1.0 + jnp.tanh(0.7978845608028654 * (x + 0.044715 * x * x * x)))


def _rms(x, gain):
    return x * lax.rsqrt(jnp.mean(x * x, axis=-1, keepdims=True) + EPS) * gain


def _dot(a, b, dims=((1,), (0,)), precision=None):
    return lax.dot_general(a, b, (dims, ((), ())), precision=precision, preferred_element_type=f32)


def _bdot(a, b, dims=((1,), (0,))):
    return _dot(a.astype(bf16), b.astype(bf16), dims)


NT = ((1,), (1,))
TN = ((0,), (0,))


def _shift_down(x, k):
    if k == 0:
        return x
    row = lax.broadcasted_iota(jnp.int32, x.shape, 0)
    return jnp.where(row >= k, pltpu.roll(x, k, 0), 0.0)


def _shift_up(x, k, fill=0.0):
    if k == 0:
        return x
    n = x.shape[0]
    row = lax.broadcasted_iota(jnp.int32, x.shape, 0)
    return jnp.where(row < n - k, pltpu.roll(x, n - k, 0), fill)


def _conv_fwd(x, w_ref, width):
    acc = w_ref[width - 1:width, :] * x
    for k in range(width - 1):
        acc = acc + w_ref[k:k + 1, :] * _shift_down(x, width - 1 - k)
    return acc


def _conv_bwd(x, dout, w_ref, dw_ref, width):
    dx = w_ref[width - 1:width, :] * dout
    dw_ref[width - 1:width, :] = jnp.sum(dout * x, axis=0, keepdims=True)
    for k in range(width - 1):
        s = width - 1 - k
        dx = dx + w_ref[k:k + 1, :] * _shift_up(dout, s)
        dw_ref[k:k + 1, :] = jnp.sum(dout * _shift_down(x, s), axis=0, keepdims=True)
    return dx


MM_VMEM_BUDGET = 36 * 1024 * 1024
MM_MAX_TILE = 1024 * 1024


def _mm_tiles(m, n, k, out_bytes):
    best = None
    for tm in (1024, 512, 384, 256, 128):
        if m % tm:
            continue
        for tn in range(1536, 0, -128):
            if n % tn or tm * tn > MM_MAX_TILE:
                continue
            score = (tm * tn, min(tm, tn))
            if 2 * (tm * k * 2 + k * tn * 2 + tm * tn * out_bytes) <= MM_VMEM_BUDGET and (best is None or score > best[0]):
                best = (score, tm, tn)
    return (best[1], best[2]) if best else (_tile(m), _tile(n))


def _mm(a, b, *, ta=False, tb=False, add=None, out_dtype=f32, name):
    m, k = (a.shape[1], a.shape[0]) if ta else a.shape
    n = b.shape[0] if tb else b.shape[1]
    tm, tn = _mm_tiles(m, n, k, jnp.dtype(out_dtype).itemsize + (4 if add is not None else 0))
    dims = ((0 if ta else 1,), (1 if tb else 0,))

    def body(*refs):
        a_ref, b_ref = refs[0], refs[1]
        o_ref = refs[-1]
        r = _dot(a_ref[...], b_ref[...], dims)
        if add is not None:
            r = r + refs[2][...]
        o_ref[...] = r.astype(out_dtype)

    a_spec = pl.BlockSpec((k, tm), lambda j, i: (0, i)) if ta else pl.BlockSpec((tm, k), lambda j, i: (i, 0))
    b_spec = pl.BlockSpec((tn, k), lambda j, i: (j, 0)) if tb else pl.BlockSpec((k, tn), lambda j, i: (0, j))
    o_spec = pl.BlockSpec((tm, tn), lambda j, i: (i, j))
    ins, specs = [a, b], [a_spec, b_spec]
    if add is not None:
        ins.append(add)
        specs.append(o_spec)
    return pl.pallas_call(body, name=name, grid=(n // tn, m // tm), in_specs=specs, out_specs=o_spec,
                          out_shape=_sds((m, n), out_dtype), compiler_params=_cparams(("parallel", "parallel")))(*ins)


def _rms_fwd(x, gain, name):
    t, d = x.shape
    tr = _tile(t, (256, 128))

    def body(x_ref, g_ref, h_ref):
        h_ref[...] = _rms(x_ref[...], g_ref[...]).astype(bf16)

    return pl.pallas_call(body, name=name, grid=(t // tr,),
                          in_specs=[pl.BlockSpec((tr, d), lambda i: (i, 0)), pl.BlockSpec((1, d), lambda i: (0, 0))],
                          out_specs=pl.BlockSpec((tr, d), lambda i: (i, 0)), out_shape=_sds((t, d), bf16),
                          compiler_params=_cparams(("parallel",)))(x, gain)


def _rms_bwd(x, gain, dh, dres, name):
    t, d = x.shape
    tr = _tile(t, (256, 128))

    def body(x_ref, g_ref, dh_ref, dres_ref, dx_ref, dxb_ref, dg_ref):
        _, vjp = jax.vjp(_rms, x_ref[...], g_ref[...])
        dx, dg = vjp(dh_ref[...])
        dx = dx + dres_ref[...]
        dx_ref[...] = dx
        dxb_ref[...] = dx.astype(bf16)

        @pl.when(pl.program_id(0) == 0)
        def _():
            dg_ref[...] = jnp.zeros_like(dg_ref)

        dg_ref[...] += dg

    row = pl.BlockSpec((tr, d), lambda i: (i, 0))
    vec = pl.BlockSpec((1, d), lambda i: (0, 0))
    return pl.pallas_call(body, name=name, grid=(t // tr,), in_specs=[row, vec, row, row], out_specs=[row, row, vec],
                          out_shape=[_sds((t, d), f32), _sds((t, d), bf16), _sds((1, d), f32)],
                          compiler_params=_cparams(("arbitrary",)))(x, gain, dh, dres)


def _loss_head(x, gain, target):
    t, d = x.shape
    tr = _tile(t, (256, 128))

    def f(xv, g, tgt):
        err = _rms(xv, g) - tgt
        return 0.5 * jnp.sum(jnp.mean(err * err, axis=-1, keepdims=True), axis=0, keepdims=True)

    def body(x_ref, g_ref, t_ref, loss_ref, dx_ref, dxb_ref, dg_ref):
        loss, vjp = jax.vjp(lambda xv, g: f(xv, g, t_ref[...]), x_ref[...], g_ref[...])
        dx, dg = vjp(jnp.ones((1, 1), f32))
        dx_ref[...] = dx
        dxb_ref[...] = dx.astype(bf16)

        @pl.when(pl.program_id(0) == 0)
        def _():
            dg_ref[...] = jnp.zeros_like(dg_ref)
            loss_ref[...] = jnp.zeros_like(loss_ref)

        dg_ref[...] += dg
        loss_ref[...] += jnp.broadcast_to(loss, loss_ref.shape)

    row = pl.BlockSpec((tr, d), lambda i: (i, 0))
    vec = pl.BlockSpec((1, d), lambda i: (0, 0))
    one = pl.BlockSpec((8, 128), lambda i: (0, 0))
    return pl.pallas_call(body, name="loss_head", grid=(t // tr,), in_specs=[row, vec, row],
                          out_specs=[one, row, row, vec],
                          out_shape=[_sds((8, 128), f32), _sds((t, d), f32), _sds((t, d), bf16), _sds((1, d), f32)],
                          compiler_params=_cparams(("arbitrary",)))(x, gain, target)


def _ffn_act_fwd(u, conv_w, conv_b):
    t = u.shape[0]
    tc = FF_PAD // 2
    nb = D_FFP // tc

    def body(g_ref, v_ref, w_ref, b_ref, a_ref, gc_ref):
        gc_ref[...] = _conv_fwd(g_ref[...], w_ref, FFN_CONV) + b_ref[...]

        def slab(i, carry):
            rows = pl.ds(pl.multiple_of(i * ROW_SLAB, ROW_SLAB), ROW_SLAB)
            a_ref[rows, :] = (_silu(gc_ref[rows, :]) * v_ref[rows, :]).astype(bf16)
            return carry

        lax.fori_loop(0, t // ROW_SLAB, slab, 0)

    return pl.pallas_call(
        body, name="ffn_act_fwd", grid=(nb,),
        in_specs=[pl.BlockSpec((t, tc), lambda j: (0, j)), pl.BlockSpec((t, tc), lambda j: (0, j + nb)),
                  pl.BlockSpec((FFN_CONV, tc), lambda j: (0, j)), pl.BlockSpec((1, tc), lambda j: (0, j))],
        out_specs=pl.BlockSpec((t, tc), lambda j: (0, j)), out_shape=_sds((t, D_FFP), bf16),
        scratch_shapes=[pltpu.VMEM((t, tc), f32)],
        compiler_params=_cparams(("parallel",)))(u, u, conv_w, conv_b)


def _ffn_act_bwd(u, conv_w, conv_b, da):
    t = u.shape[0]
    tc = FF_PAD // 2
    nb = D_FFP // tc

    def act(gc, val):
        return _silu(gc) * val

    def body(g_ref, v_ref, w_ref, b_ref, da_ref, a_ref, dg_ref, dv_ref, dw_ref, db_ref, gc_ref):
        gp = g_ref[...]
        gc_ref[...] = _conv_fwd(gp, w_ref, FFN_CONV) + b_ref[...]

        def slab(i, carry):
            rows = pl.ds(pl.multiple_of(i * ROW_SLAB, ROW_SLAB), ROW_SLAB)
            a, vjp = jax.vjp(act, gc_ref[rows, :], v_ref[rows, :])
            dgc, dval = vjp(da_ref[rows, :])
            a_ref[rows, :] = a.astype(bf16)
            dv_ref[rows, :] = dval.astype(bf16)
            gc_ref[rows, :] = dgc
            return carry

        lax.fori_loop(0, t // ROW_SLAB, slab, 0)
        dgc = gc_ref[...]
        db_ref[...] = jnp.sum(dgc, axis=0, keepdims=True)
        dg_ref[...] = _conv_bwd(gp, dgc, w_ref, dw_ref, FFN_CONV).astype(bf16)

    col = pl.BlockSpec((t, tc), lambda j: (0, j))
    return pl.pallas_call(
        body, name="ffn_act_bwd", grid=(nb,),
        in_specs=[col, pl.BlockSpec((t, tc), lambda j: (0, j + nb)), pl.BlockSpec((FFN_CONV, tc), lambda j: (0, j)),
                  pl.BlockSpec((1, tc), lambda j: (0, j)), col],
        out_specs=[col, col, col, pl.BlockSpec((FFN_CONV, tc), lambda j: (0, j)), pl.BlockSpec((1, tc), lambda j: (0, j))],
        out_shape=[_sds((t, D_FFP), bf16), _sds((t, D_FFP), bf16), _sds((t, D_FFP), bf16), _sds((FFN_CONV, D_FFP), f32),
                   _sds((1, D_FFP), f32)],
        scratch_shapes=[pltpu.VMEM((t, tc), f32)],
        compiler_params=_cparams(("parallel",)))(u, u, conv_w, conv_b, da)


def _lru_gates(xc, ra, ia, lam):
    r = _sigmoid(ra)
    i = _sigmoid(ia)
    log_a = -RG_C * r * _softplus(-lam)
    a = jnp.exp(log_a)
    u = jnp.sqrt(jnp.maximum(-_expm1(2.0 * log_a), 0.0)) * (i * xc)
    return a, u


def _lin_scan(a, u):
    n = a.shape[0]
    row = lax.broadcasted_iota(jnp.int32, a.shape, 0)
    s = 1
    while s < n:
        keep = row >= s
        u = a * jnp.where(keep, pltpu.roll(u, s, 0), 0.0) + u
        a = a * jnp.where(keep, pltpu.roll(a, s, 0), 1.0)
        s *= 2
    return u


def _rev_scan(a_next, d):
    n = d.shape[0]
    row = lax.broadcasted_iota(jnp.int32, d.shape, 0)
    a = a_next
    s = 1
    while s < n:
        keep = row < n - s
        d = a * jnp.where(keep, pltpu.roll(d, n - s, 0), 0.0) + d
        a = a * jnp.where(keep, pltpu.roll(a, n - s, 0), 1.0)
        s *= 2
    return d


def _col_conv_fwd(p, col_off, conv_w, conv_b, width, tc, name):
    t = p.shape[0]
    c = conv_w.shape[1]
    ob = col_off // tc

    def body(x_ref, w_ref, b_ref, o_ref):
        o_ref[...] = _conv_fwd(x_ref[...], w_ref, width) + b_ref[...]

    return pl.pallas_call(
        body, name=name, grid=(c // tc,),
        in_specs=[pl.BlockSpec((t, tc), lambda j: (0, j + ob)), pl.BlockSpec((width, tc), lambda j: (0, j)),
                  pl.BlockSpec((1, tc), lambda j: (0, j))],
        out_specs=pl.BlockSpec((t, tc), lambda j: (0, j)), out_shape=_sds((t, c), f32),
        compiler_params=_cparams(("parallel",)))(p, conv_w, conv_b)


def _col_conv_bwd(p, col_off, conv_w, dxc, width, tc, name):
    t = p.shape[0]
    c = conv_w.shape[1]
    ob = col_off // tc

    def body(x_ref, w_ref, d_ref, dx_ref, dw_ref, db_ref):
        d = d_ref[...]
        db_ref[...] = jnp.sum(d, axis=0, keepdims=True)
        dx_ref[...] = _conv_bwd(x_ref[...], d, w_ref, dw_ref, width).astype(bf16)

    col = pl.BlockSpec((t, tc), lambda j: (0, j))
    return pl.pallas_call(
        body, name=name, grid=(c // tc,),
        in_specs=[pl.BlockSpec((t, tc), lambda j: (0, j + ob)), pl.BlockSpec((width, tc), lambda j: (0, j)), col],
        out_specs=[col, pl.BlockSpec((width, tc), lambda j: (0, j)), pl.BlockSpec((1, tc), lambda j: (0, j))],
        out_shape=[_sds((t, c), bf16), _sds((width, c), f32), _sds((1, c), f32)],
        compiler_params=_cparams(("parallel",)))(p, conv_w, dxc)


def _lru_fwd(p, xc, wa, ba, wx, bx, lam):
    t = p.shape[0]
    bw = LRU_BLOCK

    def body(y_ref, xc_ref, wa_ref, ba_ref, wx_ref, bx_ref, lam_ref, out_ref, hs_ref, a_ref):
        xc_v = xc_ref[...]
        xb = xc_v.astype(bf16)
        ra = _dot(xb, wa_ref[0]) + ba_ref[...]
        ia = _dot(xb, wx_ref[0]) + bx_ref[...]
        a, u = _lru_gates(xc_v, ra, ia, lam_ref[...])
        a_ref[...] = a
        hs = _lin_scan(a, u)
        hs_ref[...] = hs
        out_ref[...] = (hs * _gelu(y_ref[...])).astype(bf16)

    col = pl.BlockSpec((t, bw), lambda h: (0, h))
    vec = pl.BlockSpec((1, bw), lambda h: (0, h))
    mat = pl.BlockSpec((1, bw, bw), lambda h: (h, 0, 0))
    return pl.pallas_call(
        body, name="lru_fwd", grid=(HEADS,), in_specs=[col, col, mat, vec, mat, vec, vec], out_specs=[col, col, col],
        out_shape=[_sds((t, LRU_WIDTH), bf16), _sds((t, LRU_WIDTH), f32), _sds((t, LRU_WIDTH), f32)],
        compiler_params=_cparams(("parallel",)))(p, xc, wa, ba, wx, bx, lam)


def _lru_bwd_scan(p, a, hs, dout):
    t = p.shape[0]
    bw = LRU_BLOCK

    def body(y_ref, a_ref, hs_ref, do_ref, dy_ref, da_ref, du_ref):
        hs_v = hs_ref[...]
        do = do_ref[...]
        gate, vjp = jax.vjp(_gelu, y_ref[...])
        dy_ref[...] = vjp(do * hs_v)[0].astype(bf16)
        g = _rev_scan(_shift_up(a_ref[...], 1), do * gate)
        du_ref[...] = g
        da_ref[...] = g * _shift_down(hs_v, 1)

    col = pl.BlockSpec((t, bw), lambda h: (0, h))
    return pl.pallas_call(
        body, name="lru_bwd_scan", grid=(HEADS,), in_specs=[col, col, col, col], out_specs=[col, col, col],
        out_shape=[_sds((t, LRU_WIDTH), bf16), _sds((t, LRU_WIDTH), f32), _sds((t, LRU_WIDTH), f32)],
        compiler_params=_cparams(("parallel",)))(p, a, hs, dout)


def _lru_bwd_gates(xc, da, du, wa, ba, wx, bx, lam):
    t = xc.shape[0]
    bw = LRU_BLOCK
    tr = _tile(t, (512, 256, 128))

    def body(xc_ref, da_ref, du_ref, wa_ref, ba_ref, wx_ref, bx_ref, lam_ref,
             dxc_ref, dwa_ref, dwx_ref, dba_ref, dbx_ref, dlam_ref):
        xc_v = xc_ref[...]
        xb = xc_v.astype(bf16)
        ra = _dot(xb, wa_ref[0]) + ba_ref[...]
        ia = _dot(xb, wx_ref[0]) + bx_ref[...]
        _, vjp = jax.vjp(_lru_gates, xc_v, ra, ia, lam_ref[...])
        dxc, dra, dia, dlam = vjp((da_ref[...], du_ref[...]))
        drb, dib = dra.astype(bf16), dia.astype(bf16)
        dxc_ref[...] = dxc + _dot(drb, wa_ref[0], NT) + _dot(dib, wx_ref[0], NT)

        @pl.when(pl.program_id(1) == 0)
        def _():
            dwa_ref[...] = jnp.zeros_like(dwa_ref)
            dwx_ref[...] = jnp.zeros_like(dwx_ref)
            dba_ref[...] = jnp.zeros_like(dba_ref)
            dbx_ref[...] = jnp.zeros_like(dbx_ref)
            dlam_ref[...] = jnp.zeros_like(dlam_ref)

        dwa_ref[0] += _dot(xb, drb, TN)
        dwx_ref[0] += _dot(xb, dib, TN)
        dba_ref[...] += jnp.sum(dra, axis=0, keepdims=True)
        dbx_ref[...] += jnp.sum(dia, axis=0, keepdims=True)
        dlam_ref[...] += dlam

    tile = pl.BlockSpec((tr, bw), lambda h, i: (i, h))
    vec = pl.BlockSpec((1, bw), lambda h, i: (0, h))
    mat = pl.BlockSpec((1, bw, bw), lambda h, i: (h, 0, 0))
    return pl.pallas_call(
        body, name="lru_bwd_gates", grid=(HEADS, t // tr), in_specs=[tile, tile, tile, mat, vec, mat, vec, vec],
        out_specs=[tile, mat, mat, vec, vec, vec],
        out_shape=[_sds((t, LRU_WIDTH), f32), _sds((HEADS, bw, bw), f32), _sds((HEADS, bw, bw), f32),
                   _sds((1, LRU_WIDTH), f32), _sds((1, LRU_WIDTH), f32), _sds((1, LRU_WIDTH), f32)],
        compiler_params=_cparams(("parallel", "arbitrary")))(xc, da, du, wa, ba, wx, bx, lam)


def _gdn_pre_fn(cq, ck, cv, ba, alog, dtb, h):
    q, k, v = _silu(cq), _silu(ck), _silu(cv)
    q = q * lax.rsqrt(jnp.sum(q * q, axis=-1, keepdims=True) + EPS) * (HEAD_DIM ** -0.5)
    k = k * lax.rsqrt(jnp.sum(k * k, axis=-1, keepdims=True) + EPS)
    lane = lax.broadcasted_iota(jnp.int32, (1, HEAD_DIM), 1)
    mb = (lane == h).astype(f32)
    ma = (lane == HEADS + h).astype(f32)
    beta_raw = jnp.sum(ba * mb, axis=-1, keepdims=True)
    alpha = jnp.sum(ba * ma, axis=-1, keepdims=True)
    al = jnp.sum(alog * mb, axis=-1, keepdims=True)
    db = jnp.sum(dtb * mb, axis=-1, keepdims=True)
    beta = _sigmoid(beta_raw)
    g = -jnp.exp(al) * _softplus(alpha + db)
    return q, k, v, jnp.broadcast_to(beta, q.shape), jnp.broadcast_to(g, q.shape)


def _gdn_pre_fwd(p, conv_w, alog, dtb):
    t = p.shape[0]
    hd = HEAD_DIM

    def body(pq_ref, pk_ref, pv_ref, ba_ref, wq_ref, wk_ref, wv_ref, al_ref, dt_ref, q_ref, k_ref, v_ref, b_ref, g_ref):
        h = pl.program_id(0)
        cq = _conv_fwd(pq_ref[...], wq_ref, GDN_CONV)
        ck = _conv_fwd(pk_ref[...], wk_ref, GDN_CONV)
        cv = _conv_fwd(pv_ref[...], wv_ref, GDN_CONV)
        q, k, v, be, ge = _gdn_pre_fn(cq, ck, cv, ba_ref[...], al_ref[...], dt_ref[...], h)
        q_ref[...], k_ref[...], v_ref[...], b_ref[...], g_ref[...] = q, k, v, be, ge

    def pcol(off):
        return pl.BlockSpec((t, hd), lambda h: (0, h + off // hd))

    def wcol(off):
        return pl.BlockSpec((GDN_CONV, hd), lambda h: (0, h + off // hd))

    vec = pl.BlockSpec((1, hd), lambda h: (0, 0))
    out = pl.BlockSpec((t, hd), lambda h: (0, h))
    return pl.pallas_call(
        body, name="gdn_pre_fwd", grid=(HEADS,),
        in_specs=[pcol(OFF_Q), pcol(OFF_K), pcol(OFF_V), pl.BlockSpec((t, hd), lambda h: (0, OFF_BA // hd)),
                  wcol(0), wcol(GDN_WIDTH), wcol(2 * GDN_WIDTH), vec, vec],
        out_specs=[out] * 5, out_shape=[_sds((t, GDN_WIDTH), f32)] * 5,
        compiler_params=_cparams(("parallel",)))(p, p, p, p, conv_w, conv_w, conv_w, alog, dtb)


def _gdn_pre_bwd(p, conv_w, alog, dtb, dq, dk, dv, dbe, dge):
    t = p.shape[0]
    hd = HEAD_DIM

    def body(pq_ref, pk_ref, pv_ref, ba_ref, wq_ref, wk_ref, wv_ref, al_ref, dt_ref,
             dq_ref, dk_ref, dv_ref, dbe_ref, dge_ref,
             opq_ref, opk_ref, opv_ref, dba_ref, dwq_ref, dwk_ref, dwv_ref, dal_ref, ddt_ref):
        h = pl.program_id(0)
        pq, pk, pv = pq_ref[...], pk_ref[...], pv_ref[...]
        cq = _conv_fwd(pq, wq_ref, GDN_CONV)
        ck = _conv_fwd(pk, wk_ref, GDN_CONV)
        cv = _conv_fwd(pv, wv_ref, GDN_CONV)
        _, vjp = jax.vjp(functools.partial(_gdn_pre_fn, h=h), cq, ck, cv, ba_ref[...], al_ref[...], dt_ref[...])
        dcq, dck, dcv, dba, dal, ddt = vjp((dq_ref[...], dk_ref[...], dv_ref[...], dbe_ref[...], dge_ref[...]))
        opq_ref[...] = _conv_bwd(pq, dcq, wq_ref, dwq_ref, GDN_CONV).astype(bf16)
        opk_ref[...] = _conv_bwd(pk, dck, wk_ref, dwk_ref, GDN_CONV).astype(bf16)
        opv_ref[...] = _conv_bwd(pv, dcv, wv_ref, dwv_ref, GDN_CONV).astype(bf16)

        @pl.when(h == 0)
        def _():
            dba_ref[...] = jnp.zeros_like(dba_ref)
            dal_ref[...] = jnp.zeros_like(dal_ref)
            ddt_ref[...] = jnp.zeros_like(ddt_ref)

        dba_ref[...] += dba
        dal_ref[...] += dal
        ddt_ref[...] += ddt

    def pcol(off):
        return pl.BlockSpec((t, hd), lambda h: (0, h + off // hd))

    def wcol(off):
        return pl.BlockSpec((GDN_CONV, hd), lambda h: (0, h + off // hd))

    vec = pl.BlockSpec((1, hd), lambda h: (0, 0))
    col = pl.BlockSpec((t, hd), lambda h: (0, h))
    full = pl.BlockSpec((t, hd), lambda h: (0, 0))
    wout = pl.BlockSpec((GDN_CONV, hd), lambda h: (0, h))
    return pl.pallas_call(
        body, name="gdn_pre_bwd", grid=(HEADS,),
        in_specs=[pcol(OFF_Q), pcol(OFF_K), pcol(OFF_V), pl.BlockSpec((t, hd), lambda h: (0, OFF_BA // hd)),
                  wcol(0), wcol(GDN_WIDTH), wcol(2 * GDN_WIDTH), vec, vec, col, col, col, col, col],
        out_specs=[col, col, col, full, wout, wout, wout, vec, vec],
        out_shape=[_sds((t, GDN_WIDTH), bf16)] * 3 + [_sds((t, hd), f32)] + [_sds((GDN_CONV, GDN_WIDTH), f32)] * 3
        + [_sds((1, hd), f32)] * 2,
        compiler_params=_cparams(("arbitrary",)))(p, p, p, p, conv_w, conv_w, conv_w, alog, dtb, dq, dk, dv, dbe, dge)


BNN = (((2,), (1,)), ((0,), (0,)))
BNT = (((2,), (2,)), ((0,), (0,)))
BTN = (((1,), (1,)), ((0,), (0,)))


def _hdot(a, b, dn=BNN, precision=None):
    return lax.dot_general(a, b, dn, precision=precision, preferred_element_type=f32)


def _hbdot(a, b, dn=BNN):
    return _hdot(a.astype(bf16), b.astype(bf16), dn)


def _tri_inverse(a):
    c = a.shape[-1]
    r = lax.broadcasted_iota(jnp.int32, (c, c), 0)
    col = lax.broadcasted_iota(jnp.int32, (c, c), 1)
    m = -a
    inv = jnp.where(r == col, 1.0, 0.0) + m
    s = 2
    while s < c:
        m = _hdot(m, m, precision=HI)
        inv = inv + _hdot(inv, m, precision=HI)
        s *= 2
    return inv


@jax.custom_vjp
def _saved_inverse(a, inv):
    return inv


def _saved_inverse_fwd(a, inv):
    return inv, inv


def _saved_inverse_bwd(inv, dinv):
    return -_hdot(_hdot(inv, dinv, BTN, precision=HI), inv, BNT, precision=HI), jnp.zeros_like(inv)


_saved_inverse.defvjp(_saved_inverse_fwd, _saved_inverse_bwd)


def _gdn_chunk(s, q, k, v, ge, be, tinv=None):
    nh, c, _ = q.shape
    r = lax.broadcasted_iota(jnp.int32, (c, c), 0)
    col = lax.broadcasted_iota(jnp.int32, (c, c), 1)
    causal = r >= col
    tri = jnp.broadcast_to(causal.astype(f32), (nh, c, c))
    gc = _hdot(tri, ge, precision=HI)
    gcc = gc[:, :, :c]
    gcr = jnp.swapaxes(gc, 1, 2)[:, :c, :]
    decay = jnp.where(causal, jnp.exp(jnp.where(causal, gcc - gcr, 0.0)), 0.0)
    kb = k * be
    lower = jnp.where(r > col, _hbdot(kb, k, BNT) * decay, 0.0)
    tinv = _tri_inverse(lower) if tinv is None else _saved_inverse(lower, tinv)
    egc = jnp.exp(gc)
    u = _hdot(tinv, v * be, precision=HI)
    w = _hdot(tinv, kb * egc, precision=HI)
    attn = _hbdot(q, k, BNT) * decay
    gl = gc[:, c - 1:c, :]
    v_new = u - _hbdot(w, s)
    o = _hbdot(q * egc, s) + _hbdot(attn, v_new)
    s_new = s * jnp.exp(gl) + _hbdot(k * jnp.exp(gl - gc), v_new, BTN)
    return o, s_new, tinv


def _heads_major(ref):
    return jnp.stack([ref[:, h * HEAD_DIM:(h + 1) * HEAD_DIM] for h in range(HEADS)])


def _gdn_core_fwd(q, k, v, ge, be):
    t = q.shape[0]
    c, hd = GDN_CHUNK, HEAD_DIM
    n = t // c

    def body(q_ref, k_ref, v_ref, g_ref, b_ref, o_ref, st_ref, ti_ref, s_ref):
        @pl.when(pl.program_id(0) == 0)
        def _():
            s_ref[...] = jnp.zeros_like(s_ref)

        s = s_ref[...]
        st_ref[:, 0] = s
        o, s_new, tinv = _gdn_chunk(s, *[_heads_major(r) for r in (q_ref, k_ref, v_ref, g_ref, b_ref)])
        ti_ref[0] = tinv
        for h in range(HEADS):
            o_ref[:, h * hd:(h + 1) * hd] = o[h]
        s_ref[...] = s_new

    tile = pl.BlockSpec((c, GDN_WIDTH), lambda i: (i, 0))
    return pl.pallas_call(
        body, name="gdn_core_fwd", grid=(n,), in_specs=[tile] * 5,
        out_specs=[tile, pl.BlockSpec((HEADS, 1, hd, hd), lambda i: (0, i, 0, 0)),
                   pl.BlockSpec((1, HEADS, c, c), lambda i: (i, 0, 0, 0))],
        out_shape=[_sds((t, GDN_WIDTH), f32), _sds((HEADS, n, hd, hd), f32), _sds((n, HEADS, c, c), f32)],
        scratch_shapes=[pltpu.VMEM((HEADS, hd, hd), f32)],
        compiler_params=_cparams(("arbitrary",)))(q, k, v, ge, be)


def _gdn_core_bwd(q, k, v, ge, be, states, do):
    t = q.shape[0]
    c, hd = GDN_CHUNK, HEAD_DIM
    n = t // c
    states, tinvs = states

    def body(q_ref, k_ref, v_ref, g_ref, b_ref, st_ref, ti_ref, do_ref, dq_ref, dk_ref, dv_ref, dg_ref, db_ref, ds_ref):
        @pl.when(pl.program_id(0) == 0)
        def _():
            ds_ref[...] = jnp.zeros_like(ds_ref)

        tinv = ti_ref[0]
        _, vjp = jax.vjp(lambda *a: _gdn_chunk(*a, tinv=tinv)[:2], st_ref[:, 0],
                         *[_heads_major(r) for r in (q_ref, k_ref, v_ref, g_ref, b_ref)])
        ds, *dins = vjp((_heads_major(do_ref), ds_ref[...]))
        ds_ref[...] = ds
        for d_ref, d in zip((dq_ref, dk_ref, dv_ref, dg_ref, db_ref), dins):
            for h in range(HEADS):
                d_ref[:, h * hd:(h + 1) * hd] = d[h]

    tile = pl.BlockSpec((c, GDN_WIDTH), lambda i: (n - 1 - i, 0))
    return pl.pallas_call(
        body, name="gdn_core_bwd", grid=(n,),
        in_specs=[tile] * 5 + [pl.BlockSpec((HEADS, 1, hd, hd), lambda i: (0, n - 1 - i, 0, 0)),
                               pl.BlockSpec((1, HEADS, c, c), lambda i: (n - 1 - i, 0, 0, 0)), tile],
        out_specs=[tile] * 5, out_shape=[_sds((t, GDN_WIDTH), f32)] * 5,
        scratch_shapes=[pltpu.VMEM((HEADS, hd, hd), f32)],
        compiler_params=_cparams(("arbitrary",)))(q, k, v, ge, be, states, tinvs, do)


def _post_fn(o, z, gain):
    return _rms(o, gain) * _silu(z)


def _post_fwd(o, p, z_off, gain, name):
    t = o.shape[0]
    hd = HEAD_DIM

    def body(o_ref, z_ref, g_ref, y_ref):
        y_ref[...] = _post_fn(o_ref[...], z_ref[...], g_ref[...]).astype(bf16)

    col = pl.BlockSpec((t, hd), lambda h: (0, h))
    return pl.pallas_call(
        body, name=name, grid=(HEADS,),
        in_specs=[col, pl.BlockSpec((t, hd), lambda h: (0, h + z_off // hd)), pl.BlockSpec((1, hd), lambda h: (0, 0))],
        out_specs=col, out_shape=_sds((t, HEADS * hd), bf16), compiler_params=_cparams(("parallel",)))(o, p, gain)


def _post_bwd(o, p, z_off, gain, dmix, mix_off, name):
    t = o.shape[0]
    hd = HEAD_DIM

    def body(o_ref, z_ref, g_ref, dy_ref, do_ref, dz_ref, dg_ref):
        _, vjp = jax.vjp(_post_fn, o_ref[...], z_ref[...], g_ref[...])
        do, dz, dg = vjp(dy_ref[...])
        do_ref[...] = do
        dz_ref[...] = dz.astype(bf16)

        @pl.when(pl.program_id(0) == 0)
        def _():
            dg_ref[...] = jnp.zeros_like(dg_ref)

        dg_ref[...] += dg

    col = pl.BlockSpec((t, hd), lambda h: (0, h))
    vec = pl.BlockSpec((1, hd), lambda h: (0, 0))
    return pl.pallas_call(
        body, name=name, grid=(HEADS,),
        in_specs=[col, pl.BlockSpec((t, hd), lambda h: (0, h + z_off // hd)), vec,
                  pl.BlockSpec((t, hd), lambda h: (0, h + mix_off // hd))],
        out_specs=[col, col, vec], out_shape=[_sds((t, HEADS * hd), f32), _sds((t, HEADS * hd), bf16), _sds((1, hd), f32)],
        compiler_params=_cparams(("arbitrary",)))(o, p, gain, dmix)


def _hgrn_pre_fn(qb, fb, lbw, layer):
    l0, l1 = lbw[0:1, :], lbw[1:2, :]
    m = jnp.maximum(l0, l1)
    e0, e1 = jnp.exp(l0 - m), jnp.exp(l1 - m)
    p0, p1 = e0 / (e0 + e1), e1 / (e0 + e1)
    lb = (p0 - p0) if layer == 0 else ((p0 + p1) - p0)
    f = lb + (1.0 - lb) * _sigmoid(fb)
    return _silu(qb), 1.0 - f, jnp.log(jnp.maximum(f, F_FLOOR))


def _hgrn_pre_fwd(p, lbw, layer):
    t = p.shape[0]
    tc = HEAD_DIM

    def body(qb_ref, fb_ref, lb_ref, q_ref, k_ref, lf_ref):
        q_ref[...], k_ref[...], lf_ref[...] = _hgrn_pre_fn(qb_ref[...], fb_ref[...], lb_ref[...], layer)

    col = pl.BlockSpec((t, tc), lambda j: (0, j))
    return pl.pallas_call(
        body, name="hgrn_pre_fwd", grid=(GDN_WIDTH // tc,),
        in_specs=[pl.BlockSpec((t, tc), lambda j: (0, j + OFF_QB // tc)), pl.BlockSpec((t, tc), lambda j: (0, j + OFF_FB // tc)),
                  pl.BlockSpec((2, tc), lambda j: (0, j))],
        out_specs=[col] * 3, out_shape=[_sds((t, GDN_WIDTH), f32)] * 3,
        compiler_params=_cparams(("parallel",)))(p, p, lbw)


def _hgrn_pre_bwd(p, lbw, layer, dq, dk, dlf):
    t = p.shape[0]
    tc = HEAD_DIM

    def body(qb_ref, fb_ref, lb_ref, dq_ref, dk_ref, dlf_ref, dqb_ref, dfb_ref, dlb_ref):
        _, vjp = jax.vjp(functools.partial(_hgrn_pre_fn, layer=layer), qb_ref[...], fb_ref[...], lb_ref[...])
        dqb, dfb, dlb = vjp((dq_ref[...], dk_ref[...], dlf_ref[...]))
        dqb_ref[...] = dqb.astype(bf16)
        dfb_ref[...] = dfb.astype(bf16)
        dlb_ref[...] = dlb

    col = pl.BlockSpec((t, tc), lambda j: (0, j))
    lb = pl.BlockSpec((2, tc), lambda j: (0, j))
    return pl.pallas_call(
        body, name="hgrn_pre_bwd", grid=(GDN_WIDTH // tc,),
        in_specs=[pl.BlockSpec((t, tc), lambda j: (0, j + OFF_QB // tc)), pl.BlockSpec((t, tc), lambda j: (0, j + OFF_FB // tc)),
                  lb, col, col, col],
        out_specs=[col, col, lb], out_shape=[_sds((t, GDN_WIDTH), bf16)] * 2 + [_sds((2, GDN_WIDTH), f32)],
        compiler_params=_cparams(("parallel",)))(p, p, lbw, dq, dk, dlf)


def _hgrn_step(st, q, k, lf, v):
    c = HGRN_CHUNK
    nh, rs = q.shape[0], q.shape[1]
    r2 = lax.broadcasted_iota(jnp.int32, (rs, rs), 0)
    c2 = lax.broadcasted_iota(jnp.int32, (rs, rs), 1)
    shift = c.bit_length() - 1
    same_chunk = jnp.right_shift(r2, shift) == jnp.right_shift(c2, shift)
    tri = jnp.broadcast_to(((r2 >= c2) & same_chunk).astype(f32), (nh, rs, rs))
    b_all = _hdot(tri, lf, precision=HI)
    i3 = lax.broadcasted_iota(jnp.int32, (c, c, HEAD_DIM), 0)
    j3 = lax.broadcasted_iota(jnp.int32, (c, c, HEAD_DIM), 1)
    mask = i3 >= j3
    outs = []
    for n in range(q.shape[1] // c):
        sl = slice(n * c, (n + 1) * c)
        qc, kc, vc, b = q[:, sl], k[:, sl], v[:, sl], b_all[:, sl]
        rel = jnp.where(mask, jnp.exp(jnp.where(mask, b[:, :, None, :] - b[:, None, :, :], 0.0)), 0.0)
        scores = jnp.sum(qc[:, :, None, :] * kc[:, None, :, :] * rel, axis=-1)
        bl = b[:, c - 1:c, :]
        o = _hbdot(scores, vc) + _hbdot(qc * jnp.exp(b), st, BNT)
        st = st * jnp.exp(bl) + _hbdot(vc, kc * jnp.exp(bl - b), BTN)
        outs.append(o)
    return jnp.concatenate(outs, axis=1), st


def _hgrn_core_fwd(q, k, lf, p):
    t = q.shape[0]
    hd = HEAD_DIM
    rs = min(HGRN_STEP, t)
    n = t // rs

    def body(q_ref, k_ref, lf_ref, v_ref, o_ref, st_ref, s_ref):
        @pl.when(pl.program_id(0) == 0)
        def _():
            s_ref[...] = jnp.zeros_like(s_ref)

        s = s_ref[...]
        st_ref[:, 0] = s
        o, s_new = _hgrn_step(s, *[_heads_major(r) for r in (q_ref, k_ref, lf_ref, v_ref)])
        for h in range(HEADS):
            o_ref[:, h * hd:(h + 1) * hd] = o[h]
        s_ref[...] = s_new

    tile = pl.BlockSpec((rs, GDN_WIDTH), lambda i: (i, 0))
    return pl.pallas_call(
        body, name="hgrn_core_fwd", grid=(n,),
        in_specs=[tile, tile, tile, pl.BlockSpec((rs, GDN_WIDTH), lambda i: (i, OFF_IB // GDN_WIDTH))],
        out_specs=[tile, pl.BlockSpec((HEADS, 1, hd, hd), lambda i: (0, i, 0, 0))],
        out_shape=[_sds((t, GDN_WIDTH), f32), _sds((HEADS, n, hd, hd), f32)],
        scratch_shapes=[pltpu.VMEM((HEADS, hd, hd), f32)],
        compiler_params=_cparams(("arbitrary",)))(q, k, lf, p)


def _hgrn_core_bwd(q, k, lf, p, states, do):
    t = q.shape[0]
    hd = HEAD_DIM
    rs = min(HGRN_STEP, t)
    n = t // rs

    def body(q_ref, k_ref, lf_ref, v_ref, st_ref, do_ref, dq_ref, dk_ref, dlf_ref, dv_ref, ds_ref):
        @pl.when(pl.program_id(0) == 0)
        def _():
            ds_ref[...] = jnp.zeros_like(ds_ref)

        _, vjp = jax.vjp(_hgrn_step, st_ref[:, 0], *[_heads_major(r) for r in (q_ref, k_ref, lf_ref, v_ref)])
        ds, *dins = vjp((_heads_major(do_ref), ds_ref[...]))
        ds_ref[...] = ds
        for d_ref, d in zip((dq_ref, dk_ref, dlf_ref, dv_ref), dins):
            for h in range(HEADS):
                d_ref[:, h * hd:(h + 1) * hd] = d[h].astype(d_ref.dtype)

    tile = pl.BlockSpec((rs, GDN_WIDTH), lambda i: (n - 1 - i, 0))
    return pl.pallas_call(
        body, name="hgrn_core_bwd", grid=(n,),
        in_specs=[tile, tile, tile, pl.BlockSpec((rs, GDN_WIDTH), lambda i: (n - 1 - i, OFF_IB // GDN_WIDTH)),
                  pl.BlockSpec((HEADS, 1, hd, hd), lambda i: (0, n - 1 - i, 0, 0)), tile],
        out_specs=[tile] * 4, out_shape=[_sds((t, GDN_WIDTH), f32)] * 3 + [_sds((t, GDN_WIDTH), bf16)],
        scratch_shapes=[pltpu.VMEM((HEADS, hd, hd), f32)],
        compiler_params=_cparams(("arbitrary",)))(q, k, lf, p, states, do)


def _row(v):
    return v.reshape(1, -1)


def _anchored(w, row, key):
    tok = w.get(key)
    return row if tok is None else row + tok[0, 0]


def _pad_lanes(v, n=HEAD_DIM):
    return jnp.pad(v.reshape(1, -1), ((0, 0), (0, n - v.shape[-1])))


def _ffn_fwd(x, w, l):
    h = _rms_fwd(x, _anchored(w, _row(w['norm_ffn'][l]), ('fwdf', l)), "ffn_norm")
    u = _mm(h, w['ffn_w_up'][l], tb=True, name="ffn_up")
    a = _ffn_act_fwd(u, w['ffn_conv_w'][l], _row(w['ffn_conv_b'][l]))
    y = _mm(a, w['ffn_w_down'][l], add=x, name="ffn_down")
    return y, (x, h, u)


def _ffn_bwd(saved, w, l, dy, dyb, grads):
    x, h, u = saved
    da = _mm(dyb, w['ffn_w_down'][l], tb=True, name="ffn_down_dx")
    a, dg, dv, dcw, dcb = _ffn_act_bwd(u, w['ffn_conv_w'][l], _anchored(w, _row(w['ffn_conv_b'][l]), ('bwd', l)), da)
    grads['ffn_w_down'][l] = _mm(a, dyb, ta=True, out_dtype=bf16, name="ffn_down_dw")
    du = jnp.concatenate([dg, dv], axis=1)
    grads['ffn_w_up'][l] = _mm(du, h, ta=True, out_dtype=bf16, name="ffn_up_dw")
    dh = _mm(du, w['ffn_w_up'][l], name="ffn_up_dx")
    dx, dxb, dgain = _rms_bwd(x, _row(w['norm_ffn'][l]), dh, dy, "ffn_norm_bwd")
    grads['ffn_conv_w'][l] = dcw
    grads['ffn_conv_b'][l] = dcb[0]
    grads['norm_ffn'][l] = dgain[0]
    return dx, dxb


def _odd_fwd(x, w, l, j):
    h = _rms_fwd(x, _anchored(w, _row(w['norm_mix'][l]), ('fwd', l)), "mix_norm")
    p = _mm(h, w['c_w_in'][j], name="lru_in")
    xc = _col_conv_fwd(p, LRU_WIDTH, w['c_conv_w'][j], _row(w['c_conv_b'][j]), LRU_CONV, 256, "lru_conv_fwd")
    out, hs, a = _lru_fwd(p, xc, w['c_gate_a_w'][j], _row(w['c_gate_a_b'][j]), w['c_gate_x_w'][j],
                          _row(w['c_gate_x_b'][j]), _row(w['c_lambda'][j]))
    y = _mm(out, w['c_w_out'][j], add=x, name="lru_out")
    return y, (x, h, p, xc, out, hs, a)


def _odd_bwd(saved, w, l, j, dy, dyb, grads):
    x, h, p, xc, out, hs, a = saved
    dout = _mm(dyb, w['c_w_out'][j], tb=True, name="lru_out_dx")
    grads['c_w_out'][j] = _mm(out, dyb, ta=True, out_dtype=bf16, name="lru_out_dw")
    dyb_, da, du = _lru_bwd_scan(p, a, hs, dout)
    dxc, dwa, dwx, dba, dbx, dlam = _lru_bwd_gates(xc, da, du, w['c_gate_a_w'][j], _row(w['c_gate_a_b'][j]),
                                                   w['c_gate_x_w'][j], _row(w['c_gate_x_b'][j]), _row(w['c_lambda'][j]))
    dxb_, dcw, dcb = _col_conv_bwd(p, LRU_WIDTH, w['c_conv_w'][j], dxc, LRU_CONV, 256, "lru_conv_bwd")
    dp = jnp.concatenate([dyb_, dxb_], axis=1)
    grads['c_w_in'][j] = _mm(h, dp, ta=True, out_dtype=bf16, name="lru_in_dw")
    dh = _mm(dp, w['c_w_in'][j], tb=True, name="lru_in_dx")
    dx, dxb, dgain = _rms_bwd(x, _row(w['norm_mix'][l]), dh, dy, "mix_norm_bwd")
    grads['c_gate_a_w'][j], grads['c_gate_x_w'][j] = dwa, dwx
    grads['c_gate_a_b'][j], grads['c_gate_x_b'][j], grads['c_lambda'][j] = dba[0], dbx[0], dlam[0]
    grads['c_conv_w'][j], grads['c_conv_b'][j] = dcw, dcb[0]
    grads['norm_mix'][l] = dgain[0]
    return dx, dxb


def _even_fwd(x, w, l, j):
    h = _rms_fwd(x, _anchored(w, _row(w['norm_mix'][l]), ('fwd', l)), "mix_norm")
    p = _mm(h, w['ab_w_in'][j], name="ab_in")
    alog, dtb = _pad_lanes(w['gdn_a_log'][j]), _pad_lanes(w['gdn_dt_bias'][j])
    q, k, v, be, ge = _gdn_pre_fwd(p, w['gdn_conv_w'][j], alog, dtb)
    oa, *sa = _gdn_core_fwd(q, k, v, ge, be)
    ya = _post_fwd(oa, p, OFF_Z, _row(w['gdn_norm'][j]), "gdn_post_fwd")
    qq, kk, lf = _hgrn_pre_fwd(p, w['hgrn_lower_bounds'], j)
    ob, sb = _hgrn_core_fwd(qq, kk, lf, p)
    yb = _post_fwd(ob, p, OFF_GB, _row(w['hgrn_norm'][j]), "hgrn_post_fwd")
    mix = jnp.concatenate([ya, yb], axis=1)
    y = _mm(mix, w['ab_w_out'][j], add=x, name="ab_out")
    return y, (x, h, p, q, k, v, be, ge, oa, sa, qq, kk, lf, ob, sb, mix)


def _even_bwd(saved, w, l, j, dy, dyb, grads):
    x, h, p, q, k, v, be, ge, oa, sa, qq, kk, lf, ob, sb, mix = saved
    alog, dtb = _pad_lanes(w['gdn_a_log'][j]), _pad_lanes(w['gdn_dt_bias'][j])
    dmix = _mm(dyb, w['ab_w_out'][j], tb=True, name="ab_out_dx")
    grads['ab_w_out'][j] = _mm(mix, dyb, ta=True, out_dtype=bf16, name="ab_out_dw")
    doa, dz, dgn = _post_bwd(oa, p, OFF_Z, _anchored(w, _row(w['gdn_norm'][j]), ('bwdm', l)), dmix, 0, "gdn_post_bwd")
    dob, dgb, dhn = _post_bwd(ob, p, OFF_GB, _row(w['hgrn_norm'][j]), dmix, GDN_WIDTH, "hgrn_post_bwd")
    dq, dk, dv, dge, dbe = _gdn_core_bwd(q, k, v, ge, be, sa, doa)
    dpq, dpk, dpv, dba, dwq, dwk, dwv, dal, ddt = _gdn_pre_bwd(p, w['gdn_conv_w'][j], alog, dtb, dq, dk, dv, dbe, dge)
    dqq, dkk, dlf, dib = _hgrn_core_bwd(qq, kk, lf, p, sb, dob)
    dqb, dfb, dlb = _hgrn_pre_bwd(p, w['hgrn_lower_bounds'], j, dqq, dkk, dlf)
    dp = jnp.concatenate([dpq, dpk, dpv, dz, dqb, dfb, dib, dgb, dba.astype(bf16)], axis=1)
    grads['ab_w_in'][j] = _mm(h, dp, ta=True, out_dtype=bf16, name="ab_in_dw")
    dh = _mm(dp, w['ab_w_in'][j], tb=True, name="ab_in_dx")
    dx, dxb, dgain = _rms_bwd(x, _row(w['norm_mix'][l]), dh, dy, "mix_norm_bwd")
    grads['gdn_conv_w'][j] = jnp.concatenate([dwq, dwk, dwv], axis=1)
    grads['gdn_a_log'][j], grads['gdn_dt_bias'][j] = dal[0, :HEADS], ddt[0, :HEADS]
    grads['gdn_norm'][j], grads['hgrn_norm'][j] = dgn[0], dhn[0]
    grads['hgrn_lower_bounds'].append(dlb)
    grads['norm_mix'][l] = dgain[0]
    return dx, dxb


def _ab_permute(w_in):
    pad = jnp.zeros(w_in.shape[:-1] + (AB_PAD - AB_COLS,), w_in.dtype)
    return jnp.concatenate([w_in[..., :2048], w_in[..., 2056:], w_in[..., 2048:2056], pad], axis=-1)


def _ab_unpermute(g):
    return jnp.concatenate([g[..., :2048], g[..., 4096:4104], g[..., 2048:4096]], axis=-1)


def _block_pad(a, axis, nblk, padded):
    axis = axis % a.ndim
    s = a.shape
    a = a.reshape(s[:axis] + (nblk, s[axis] // nblk) + s[axis + 1:])
    pad = [(0, 0)] * a.ndim
    pad[axis + 1] = (0, padded - s[axis] // nblk)
    return jnp.pad(a, pad).reshape(s[:axis] + (nblk * padded,) + s[axis + 1:])


def _block_unpad(a, axis, nblk, width):
    axis = axis % a.ndim
    s = a.shape
    a = a.reshape(s[:axis] + (nblk, s[axis] // nblk) + s[axis + 1:])
    a = lax.slice_in_dim(a, 0, width, axis=axis + 1)
    return a.reshape(s[:axis] + (nblk * width,) + s[axis + 1:])


def _kernel_layout(w):
    w = dict(w)
    w['ab_w_in'] = _ab_permute(w['ab_w_in'])
    w['ffn_w_up'] = jnp.swapaxes(_block_pad(w['ffn_w_up'], 2, N_DEV, FF_PAD), 1, 2)
    w['ffn_w_down'] = _block_pad(w['ffn_w_down'], 1, 4, FF_PAD)
    w['ffn_conv_w'] = _block_pad(w['ffn_conv_w'], 2, 4, FF_PAD)
    w['ffn_conv_b'] = _block_pad(w['ffn_conv_b'], 1, 4, FF_PAD)
    return w


def _natural_grads(g):
    g = dict(g)
    g['ab_w_in'] = _ab_unpermute(g['ab_w_in'])
    g['ffn_w_up'] = _block_unpad(jnp.swapaxes(g['ffn_w_up'], 1, 2), 2, N_DEV, FF_SHARD)
    g['ffn_w_down'] = _block_unpad(g['ffn_w_down'], 1, 4, FF_SHARD)
    g['ffn_conv_w'] = _block_unpad(g['ffn_conv_w'], 2, 4, FF_SHARD)
    g['ffn_conv_b'] = _block_unpad(g['ffn_conv_b'], 1, 4, FF_SHARD)
    return g


def _local_step(x, target, w, fetch=None, push=None):
    grads = {n: [None] * (DEPTH if n in ('norm_mix', 'norm_ffn') or n.startswith('ffn_') else 2)
             for n in WEIGHTS if n not in ('norm_final', 'hgrn_lower_bounds')}
    grads['hgrn_lower_bounds'] = []
    saved = []
    for l in range(DEPTH):
        j = l // 2
        if fetch is not None:
            fetch(l, x, 'mix')
        x, s_mix = (_even_fwd if l % 2 == 0 else _odd_fwd)(x, w, l, j)
        if fetch is not None:
            fetch(l, x, 'ffn')
        x, s_ffn = _ffn_fwd(x, w, l)
        saved.append((s_mix, s_ffn))
    loss, dx, dxb, dgf = _loss_head(x, _row(w['norm_final']), target)
    for l in reversed(range(DEPTH)):
        j = l // 2
        s_mix, s_ffn = saved[l]
        dx, dxb = _ffn_bwd(s_ffn, w, l, dx, dxb, grads)
        if push is not None:
            push(l, 'ffn', {nm: grads[nm].pop(li) for nm, li in _layer_items(l)[-2:]})
        dx, dxb = (_even_bwd if l % 2 == 0 else _odd_bwd)(s_mix, w, l, j, dx, dxb, grads)
        if push is not None:
            push(l, 'mix', {nm: grads[nm].pop(li) for nm, li in _layer_items(l)[:-2]})
    out = {n: jnp.stack(g) for n, g in grads.items() if n != 'hgrn_lower_bounds' and g}
    out['hgrn_lower_bounds'] = grads['hgrn_lower_bounds'][0] + grads['hgrn_lower_bounds'][1]
    out['norm_final'] = dgf[0]
    return loss[0, 0], dx, out


def _position():
    return lax.axis_index("x"), lax.axis_index("y"), lax.axis_index("c")


BLOCK_LAYOUT = {
    'ab_w_in': ((2, D_MODEL, N_DEV * AB_SHARD_PAD), (2, D_MODEL, AB_SHARD_PAD)),
    'ab_w_out': ((2, N_DEV, 128, D_MODEL), (2, 128, D_MODEL)),
    'c_w_in': ((2, D_MODEL, 2 * LRU_WIDTH), (2, D_MODEL, 256)),
    'c_w_out': ((2, N_DEV, 128, D_MODEL), (2, 128, D_MODEL)),
    'c_gate_a_w': ((2, HEADS, N_DEV, 32, LRU_BLOCK), (2, HEADS, 32, LRU_BLOCK)),
    'c_gate_x_w': ((2, HEADS, N_DEV, 32, LRU_BLOCK), (2, HEADS, 32, LRU_BLOCK)),
    'ffn_w_up': ((DEPTH, N_DEV, FF_PAD, D_MODEL), (DEPTH, FF_PAD, D_MODEL)),
    'ffn_w_down': ((DEPTH, 4, FF_PAD, D_MODEL), (DEPTH, FF_ROWS, D_MODEL)),
}


COL_WINDOW = {'ab_w_in': AB_SHARD_PAD, 'c_w_in': 256}


def _block_index(name, p):
    d = 4 * p[0] + 2 * p[1] + p[2]
    if name in COL_WINDOW:
        return (slice(None), pl.ds(pl.multiple_of(d * COL_WINDOW[name], 128), COL_WINDOW[name]))
    if name == 'ffn_w_down':
        return (2 * p[0] + p[1], pl.ds(pl.multiple_of(p[2] * FF_ROWS, 16), FF_ROWS), slice(None))
    if name in ('c_gate_a_w', 'c_gate_x_w'):
        return (slice(None), d)
    return (d,)


def _block_of(name, ref, p, layered=True):
    idx = _block_index(name, p)
    if layered and name in BLOCK_LAYOUT:
        idx = (slice(None),) + idx
    return ref.at[idx]


def _layer_items(l):
    j = l // 2
    mix = ([('ab_w_in', j), ('ab_w_out', j)] if l % 2 == 0 else
           [('c_w_in', j), ('c_w_out', j), ('c_gate_a_w', j), ('c_gate_x_w', j)])
    return mix + [('ffn_w_up', l), ('ffn_w_down', l)]


def _own_land(name, shard_l, pos):
    x, y, c = pos
    d = 4 * x + 2 * y + c
    shape = BLOCK_LAYOUT[name][0][1:] if name in BLOCK_LAYOUT else (N_DEV,) + shard_l.shape
    zeros = jnp.zeros(shape, shard_l.dtype) if name == 'ffn_w_down' else lax.empty(shape, shard_l.dtype)
    if name in COL_WINDOW:
        return lax.dynamic_update_slice(zeros, shard_l, (0, d * COL_WINDOW[name]))
    if name == 'ffn_w_down':
        return lax.dynamic_update_slice(zeros, shard_l[None], (2 * x + y, c * FF_ROWS, 0))
    if name in ('c_gate_a_w', 'c_gate_x_w'):
        return lax.dynamic_update_slice(zeros, shard_l[:, None], (0, d, 0, 0))
    return lax.dynamic_update_slice(zeros, shard_l[None], (d,) + (0,) * shard_l.ndim)


def _place_own(items, shards, posv, name):
    n = len(items)
    down = [i for i, (nm, _) in enumerate(items) if nm == 'ffn_w_down']
    in_specs, out_specs, out_shapes, operands = [], [], [], []
    for nm, li in items:
        sh = shards[nm]
        shard_shape = sh.shape if li is None else sh.shape[1:]
        z = (0,) * len(shard_shape)
        operands.append(sh)
        in_specs.append(pl.BlockSpec(shard_shape, lambda i, d, q, c, z=z: z) if li is None else
                        pl.BlockSpec((1,) + shard_shape, lambda i, d, q, c, li=li, z=z: (li,) + z))
        out_shapes.append(_sds(BLOCK_LAYOUT[nm][0][1:] if nm in BLOCK_LAYOUT else (N_DEV,) + sh.shape, sh.dtype))
        if nm in COL_WINDOW:
            out_specs.append(pl.BlockSpec(shard_shape, lambda i, d, q, c: (0, d[0])))
        elif nm == 'ffn_w_down':
            out_specs.append(pl.BlockSpec((1,) + shard_shape, lambda i, d, q, c: (q[0], c[0], 0)))
        elif nm in ('c_gate_a_w', 'c_gate_x_w'):
            out_specs.append(pl.BlockSpec((HEADS, 1) + shard_shape[1:], lambda i, d, q, c: (0, d[0], 0, 0)))
        else:
            out_specs.append(pl.BlockSpec((1,) + shard_shape, lambda i, d, q, c, z=z: (d[0],) + z))

    def body(d_ref, q_ref, c_ref, *refs):
        for i, (nm, li) in enumerate(items):
            v = refs[i][...] if li is None else refs[i][0]
            o_ref = refs[n + len(down) + i]
            if nm in COL_WINDOW:
                o_ref[...] = v
            elif nm in ('c_gate_a_w', 'c_gate_x_w'):
                o_ref[:, 0] = v
            else:
                o_ref[0] = v

    zeros = [jnp.zeros(out_shapes[i].shape, out_shapes[i].dtype) for i in down]
    return pl.pallas_call(
        body, name=name, out_shape=out_shapes,
        grid_spec=pltpu.PrefetchScalarGridSpec(
            num_scalar_prefetch=3, grid=(1,), in_specs=in_specs + [pl.BlockSpec(memory_space=pl.ANY)] * len(down),
            out_specs=out_specs),
        input_output_aliases={3 + n + k: i for k, i in enumerate(down)},
        compiler_params=_cparams(("arbitrary",)))(*posv, *operands, *zeros)


def _src_of(shard_ref, li):
    return shard_ref if li is None else shard_ref.at[li]


def _gather_now(items, shards, lands):
    n = len(items)
    srcs = sorted({nm for nm, _ in items})

    def body(*refs):
        ins = dict(zip(srcs, refs[:len(srcs)]))
        outs = refs[len(srcs) + n:len(srcs) + 2 * n]
        send_sems, recv_sems = refs[len(srcs) + 2 * n:]
        x, y, c = _position()
        me, sibling = (x, y, c), (x, y, 1 - c)
        chips = [(1 - x, y), (x, 1 - y), (1 - x, 1 - y)]

        def copy(i, k, block, to, own=False):
            nm, li = items[i]
            dst = _block_of(nm, outs[i], block, layered=False)
            return pltpu.make_async_remote_copy(
                src_ref=_src_of(ins[nm], li) if own else dst, dst_ref=dst, send_sem=send_sems.at[7 * i + k],
                recv_sem=recv_sems.at[7 * i + k], device_id=to, device_id_type=MESH)

        first = []
        for i in range(n):
            first.append(copy(i, 0, me, sibling, own=True))
            first += [copy(i, 1 + j, me, (*chip, c), own=True) for j, chip in enumerate(chips)]
        for cp in first:
            cp.start()
        passed = []
        for j, chip in enumerate(chips):
            for i in range(n):
                copy(i, 1 + j, (*chip, c), me).wait_recv()
                fwd = copy(i, 4 + j, (*chip, c), sibling)
                fwd.start()
                passed.append(fwd)
        for i in range(n):
            copy(i, 0, sibling, me).wait_recv()
        for j, chip in enumerate(chips):
            for i in range(n):
                copy(i, 4 + j, (*chip, 1 - c), me).wait_recv()
        for cp in first + passed:
            cp.wait_send()

    any_spec = pl.BlockSpec(memory_space=pl.ANY)
    return pl.pallas_call(
        body, name="gather_first_layer", out_shape=[_sds(a.shape, a.dtype) for a in lands],
        in_specs=[any_spec] * (len(srcs) + n), out_specs=[any_spec] * n,
        input_output_aliases={len(srcs) + i: i for i in range(n)},
        scratch_shapes=[pltpu.SemaphoreType.DMA((7 * n,)), pltpu.SemaphoreType.DMA((7 * n,))],
    )(*[shards[nm] for nm in srcs], *lands)


FIRST_HOP = (1, 2, 4, 6)


def _lanes(name, land_ref, pos):
    if name == 'ffn_w_down':
        return [(FIRST_HOP, land_ref.at[pl.ds(0, 2), pl.ds(0, 2 * FF_ROWS)])]
    if name in COL_WINDOW:
        return [(FIRST_HOP, land_ref.at[:, pl.ds(0, 4 * COL_WINDOW[name])])]
    if name in ('c_gate_a_w', 'c_gate_x_w'):
        return [(FIRST_HOP, land_ref.at[:, pl.ds(0, 4)])]
    return [(FIRST_HOP, land_ref.at[pl.ds(0, 4)])]


def _n_lanes(items):
    return len(items)


def _gather_forward(items, lands, name):
    n = len(items)

    def body(*refs):
        outs = refs[n:2 * n]
        send_sems, recv_sems = refs[2 * n:]
        x, y, c = _position()
        chips = [(1 - x, y), (x, 1 - y), (1 - x, 1 - y)]
        copies, arrivals = [], []
        for i, (nm, _) in enumerate(items):
            for j, chip in enumerate(chips):
                mine = _block_of(nm, outs[i], (*chip, c), layered=False)
                theirs = _block_of(nm, outs[i], (*chip, 1 - c), layered=False)
                copies.append(pltpu.make_async_remote_copy(
                    src_ref=mine, dst_ref=mine, send_sem=send_sems.at[3 * i + j], recv_sem=recv_sems.at[3 * i + j],
                    device_id=(x, y, 1 - c), device_id_type=MESH))
                arrivals.append(pltpu.make_async_remote_copy(
                    src_ref=theirs, dst_ref=theirs, send_sem=send_sems.at[3 * i + j], recv_sem=recv_sems.at[3 * i + j],
                    device_id=(x, y, 1 - c), device_id_type=MESH))
        for cp in copies:
            cp.start()
        for cp in arrivals:
            cp.wait_recv()
        for cp in copies:
            cp.wait_send()

    any_spec = pl.BlockSpec(memory_space=pl.ANY)
    return pl.pallas_call(
        body, name=name, out_shape=[_sds(a.shape, a.dtype) for a in lands],
        in_specs=[any_spec] * n, out_specs=[any_spec] * n, input_output_aliases={i: i for i in range(n)},
        scratch_shapes=[pltpu.SemaphoreType.DMA((3 * n,)), pltpu.SemaphoreType.DMA((3 * n,))],
    )(*lands)


HBM_SPEC = pl.BlockSpec(memory_space=pltpu.HBM)
SEM_SPEC = pl.BlockSpec(memory_space=pltpu.SEMAPHORE)
SIDE_EFFECT = pltpu.SideEffectType.DATAFLOW_SIDE_EFFECTING


def _gather_start(items, shards, lands, token, name):
    n = len(items)
    srcs = sorted({nm for nm, _ in items})
    ns, nl = len(srcs), _n_lanes(items)

    def body(*refs):
        ins = dict(zip(srcs, refs[:ns]))
        land_refs = refs[ns:ns + n]
        sems = refs[ns + n + 1:ns + n + 1 + 2 * nl]
        x, y, c = _position()
        me = (x, y, c)
        lane = 0
        for i, (nm, li) in enumerate(items):
            for codes, _ in _lanes(nm, land_refs[i], me):
                for k in codes:
                    peer = (1 - x if (k >> 2) & 1 else x, 1 - y if (k >> 1) & 1 else y, 1 - c if k & 1 else c)
                    pltpu.make_async_remote_copy(
                        src_ref=_src_of(ins[nm], li), dst_ref=_block_of(nm, land_refs[i], me, layered=False),
                        send_sem=sems[2 * lane], recv_sem=sems[2 * lane + 1], device_id=peer, device_id_type=MESH).start()
                lane += 1
        refs[-1][...] = jnp.zeros((8, 128), f32)

    hbm = [pltpu.with_memory_space_constraint(a, pltpu.HBM) for a in [shards[nm] for nm in srcs] + list(lands)]
    outs = pl.pallas_call(
        body, name=name,
        out_shape=[pltpu.SemaphoreType.DMA(())] * (2 * nl) + [pltpu.HBM(a.shape, a.dtype) for a in hbm] + [_sds((8, 128), f32)],
        in_specs=[HBM_SPEC] * (ns + n) + [pl.BlockSpec(memory_space=pl.ANY)],
        out_specs=[SEM_SPEC] * (2 * nl) + [HBM_SPEC] * (ns + n) + [pl.BlockSpec(memory_space=pltpu.VMEM)],
        input_output_aliases={i: 2 * nl + i for i in range(ns + n)},
        compiler_params=pltpu.CompilerParams(has_side_effects=SIDE_EFFECT),
    )(*hbm, token)
    return outs[:2 * nl], dict(zip(srcs, outs[2 * nl:2 * nl + ns])), outs[2 * nl + ns:-1], outs[-1]


def _gather_wait(items, sems, shards, lands, after, name):
    n = len(items)
    srcs = sorted(shards)
    ns, nl = len(srcs), _n_lanes(items)

    def body(*refs):
        land_refs = refs[ns:ns + n]
        sem_refs = refs[ns + n:ns + n + 2 * nl]
        x, y, c = _position()
        lane = 0
        for i, (nm, _) in enumerate(items):
            for _, moved in _lanes(nm, land_refs[i], (x, y, c)):
                cp = pltpu.make_async_remote_copy(
                    src_ref=moved, dst_ref=moved, send_sem=sem_refs[2 * lane], recv_sem=sem_refs[2 * lane + 1],
                    device_id=(x, y, 1 - c), device_id_type=MESH)
                cp.wait_send()
                cp.wait_recv()
                lane += 1

    outs = pl.pallas_call(
        body, name=name, out_shape=[pltpu.HBM(shards[nm].shape, shards[nm].dtype) for nm in srcs]
        + [pltpu.HBM(a.shape, a.dtype) for a in lands],
        in_specs=[HBM_SPEC] * (ns + n) + [SEM_SPEC] * (2 * nl) + [pl.BlockSpec(memory_space=pl.ANY)],
        out_specs=[HBM_SPEC] * (ns + n), input_output_aliases={i: i for i in range(ns + n)},
        compiler_params=pltpu.CompilerParams(has_side_effects=SIDE_EFFECT),
    )(*[shards[nm] for nm in srcs], *lands, *sems, after)
    return dict(zip(srcs, outs[:ns])), outs[ns:]


def _exchange_grads(fulls, rep):
    cpos = lax.axis_index("c").astype(jnp.int32).reshape(1)
    pair, rep_pair = _pair_exchange(fulls, rep)
    chip = {nm: _chip_sum(nm, fulls[nm], pair[nm], cpos) for nm in fulls}
    rep_chip = _add_pair(rep, rep_pair, "chip_sum_replicated")
    cross, cross_rep = _cross_exchange(chip, rep_chip)
    return chip, rep_chip, cross, cross_rep


def _pair_exchange(fulls, rep, tag=""):
    names = list(fulls)
    n = len(names)
    shard_shape = {nm: ((fulls[nm].shape[0],) + BLOCK_LAYOUT[nm][1][1:] if nm in BLOCK_LAYOUT else fulls[nm].shape[1:])
                   for nm in names}

    def body(*refs):
        ins = dict(zip(names, refs[:n]))
        rep_ref = refs[n]
        pair = dict(zip(names, refs[n + 1:2 * n + 1]))
        rpair_ref = refs[2 * n + 1]
        send_sems, recv_sems = refs[2 * n + 2:]
        x, y, c = _position()
        sibling = (x, y, 1 - c)
        remote = []
        for i, nm in enumerate(names):
            for q in range(4):
                remote.append(pltpu.make_async_remote_copy(
                    src_ref=_block_of(nm, ins[nm], (q >> 1, q & 1, 1 - c)), dst_ref=pair[nm].at[q],
                    send_sem=send_sems.at[4 * i + q], recv_sem=recv_sems.at[4 * i + q], device_id=sibling,
                    device_id_type=MESH))
        remote.append(pltpu.make_async_remote_copy(
            src_ref=rep_ref, dst_ref=rpair_ref, send_sem=send_sems.at[4 * n], recv_sem=recv_sems.at[4 * n],
            device_id=sibling, device_id_type=MESH))
        for cp in remote:
            cp.start()
        for cp in remote:
            cp.wait_recv()
        for cp in remote:
            cp.wait_send()

    any_spec = pl.BlockSpec(memory_space=pl.ANY)
    four = [_sds((4,) + tuple(shard_shape[nm]), fulls[nm].dtype) for nm in names]
    outs = pl.pallas_call(
        body, name="grad_pair_exchange" + tag, out_shape=four + [_sds(rep.shape, rep.dtype)],
        in_specs=[any_spec] * (n + 1), out_specs=[any_spec] * (n + 1),
        scratch_shapes=[pltpu.SemaphoreType.DMA((4 * n + 1,)), pltpu.SemaphoreType.DMA((4 * n + 1,))],
    )(*[fulls[nm] for nm in names], rep)
    return dict(zip(names, outs[:n])), outs[n]


def _chip_sum(name, full, pair, cpos, tag=""):
    if name in COL_WINDOW:
        width = BLOCK_LAYOUT[name][1][-1]
        rows = full.shape[0] * full.shape[1]
        tr = 512

        def body(c_ref, f_ref, p_ref, o_ref):
            o_ref[0] = (f_ref[...].astype(f32) + p_ref[0].astype(f32)).astype(o_ref.dtype)

        slot = pl.BlockSpec((1, tr, width), lambda q, i, c: (q, i, 0))
        out = pl.pallas_call(
            body, name="chip_sum_" + name + tag, out_shape=_sds((4, rows, width), full.dtype),
            grid_spec=pltpu.PrefetchScalarGridSpec(
                num_scalar_prefetch=1, grid=(4, rows // tr),
                in_specs=[pl.BlockSpec((tr, width), lambda q, i, c: (i, 2 * q + c[0])), slot], out_specs=slot),
            compiler_params=_cparams(("parallel", "parallel")))(
            cpos, full.reshape(rows, N_DEV * width), pair.reshape(4, rows, width))
        return out.reshape(pair.shape)

    if name == 'ffn_w_down':
        f4, p4 = full, pair
        fspec = pl.BlockSpec((full.shape[0], 1, FF_ROWS, D_MODEL), lambda q, c: (0, q, c[0], 0))
    else:
        shard = pair.shape[1:]
        lead = int(np.prod(shard[:-2]))
        f4 = full.reshape((lead, N_DEV) + shard[-2:])
        p4 = pair.reshape((4, lead) + shard[-2:])
        fspec = pl.BlockSpec((lead, 1) + shard[-2:], lambda q, c: (0, 2 * q + c[0], 0, 0))

    def body4(c_ref, f_ref, p_ref, o_ref):
        o_ref[0] = (f_ref[:, 0].astype(f32) + p_ref[0].astype(f32)).astype(o_ref.dtype)

    slot = pl.BlockSpec((1,) + p4.shape[1:], lambda q, c: (q, 0, 0, 0))
    out = pl.pallas_call(
        body4, name="chip_sum_" + name + tag, out_shape=_sds(p4.shape, full.dtype),
        grid_spec=pltpu.PrefetchScalarGridSpec(num_scalar_prefetch=1, grid=(4,), in_specs=[fspec, slot], out_specs=slot),
        compiler_params=_cparams(("parallel",)))(cpos, f4, p4)
    return out.reshape(pair.shape)


def _add_pair(a, b, name):
    shp = a.shape
    r, c = int(np.prod(shp[:-1])), shp[-1]
    tr = _tile(r, (512, 256, 128, 64, 32, 16, 8))

    def body(a_ref, b_ref, o_ref):
        o_ref[...] = (a_ref[...].astype(f32) + b_ref[...].astype(f32)).astype(o_ref.dtype)

    tile = pl.BlockSpec((tr, c), lambda i: (i, 0))
    return pl.pallas_call(body, name=name, grid=(r // tr,), in_specs=[tile, tile], out_specs=tile,
                          out_shape=_sds((r, c), a.dtype), compiler_params=_cparams(("parallel",)))(
        a.reshape(r, c), b.reshape(r, c)).reshape(shp)


def _cross_exchange(chip, rep_chip):
    names = list(chip)
    n = len(names)

    def body(*refs):
        ins = dict(zip(names, refs[:n]))
        rep_ref = refs[n]
        outs = dict(zip(names, refs[2 * n + 2:3 * n + 2]))
        rrep_ref = refs[3 * n + 2]
        send_sems, recv_sems = refs[3 * n + 3:]
        x, y, c = _position()
        mine = 2 * x + y
        copies = []
        for k in range(1, 4):
            px, py = (1 - x if (k >> 1) & 1 else x), (1 - y if k & 1 else y)
            for i, nm in enumerate(names + ['']):
                src = rep_ref if i == n else ins[nm].at[2 * px + py]
                dst = (rrep_ref if i == n else outs[nm]).at[mine]
                copies.append(pltpu.make_async_remote_copy(
                    src_ref=src, dst_ref=dst, send_sem=send_sems.at[3 * i + k - 1], recv_sem=recv_sems.at[3 * i + k - 1],
                    device_id=(px, py, c), device_id_type=MESH))
        for cp in copies:
            cp.start()
        for cp in copies:
            cp.wait_recv()
        for cp in copies:
            cp.wait_send()

    any_spec = pl.BlockSpec(memory_space=pl.ANY)
    shapes = [_sds(chip[nm].shape, chip[nm].dtype) for nm in names] + [_sds((4,) + rep_chip.shape, rep_chip.dtype)]
    zeros = [jnp.zeros(s.shape, s.dtype) for s in shapes]
    outs = pl.pallas_call(
        body, name="grad_cross_exchange", out_shape=shapes,
        in_specs=[any_spec] * (2 * n + 2), out_specs=[any_spec] * (n + 1),
        input_output_aliases={n + 1 + i: i for i in range(n + 1)},
        scratch_shapes=[pltpu.SemaphoreType.DMA((3 * (n + 1),)), pltpu.SemaphoreType.DMA((3 * (n + 1),))],
    )(*[chip[nm] for nm in names], rep_chip, *zeros)
    return dict(zip(names, outs[:n])), outs[n]


def _cross_start(chips, name):
    n = len(chips)

    def body(*refs):
        chip_refs, land_refs = refs[:n], refs[n:2 * n]
        sems = refs[2 * n:4 * n]
        x, y, c = _position()
        mine = 2 * x + y
        for i in range(n):
            for k in range(1, 4):
                px, py = (1 - x if (k >> 1) & 1 else x), (1 - y if k & 1 else y)
                pltpu.make_async_remote_copy(
                    src_ref=chip_refs[i].at[2 * px + py], dst_ref=land_refs[i].at[mine], send_sem=sems[2 * i],
                    recv_sem=sems[2 * i + 1], device_id=(px, py, c), device_id_type=MESH).start()
        refs[-1][...] = jnp.zeros((8, 128), f32)

    hbm = [pltpu.with_memory_space_constraint(a, pltpu.HBM) for a in list(chips) + [jnp.zeros(a.shape, a.dtype) for a in chips]]
    outs = pl.pallas_call(
        body, name=name,
        out_shape=[pltpu.SemaphoreType.DMA(())] * (2 * n) + [pltpu.HBM(a.shape, a.dtype) for a in hbm] + [_sds((8, 128), f32)],
        in_specs=[HBM_SPEC] * (2 * n),
        out_specs=[SEM_SPEC] * (2 * n) + [HBM_SPEC] * (2 * n) + [pl.BlockSpec(memory_space=pltpu.VMEM)],
        input_output_aliases={i: 2 * n + i for i in range(2 * n)},
        compiler_params=pltpu.CompilerParams(has_side_effects=SIDE_EFFECT),
    )(*hbm)
    return outs[:2 * n], outs[2 * n:3 * n], outs[3 * n:4 * n], outs[4 * n]


def _cross_wait(sems, chips, lands, after, name):
    n = len(chips)

    def body(*refs):
        land_refs = refs[n:2 * n]
        sem_refs = refs[2 * n:4 * n]
        x, y, c = _position()
        for i in range(n):
            moved = land_refs[i].at[pl.ds(0, 3)]
            cp = pltpu.make_async_remote_copy(
                src_ref=moved, dst_ref=moved, send_sem=sem_refs[2 * i], recv_sem=sem_refs[2 * i + 1],
                device_id=(x, y, 1 - c), device_id_type=MESH)
            cp.wait_send()
            cp.wait_recv()

    outs = pl.pallas_call(
        body, name=name, out_shape=[pltpu.HBM(a.shape, a.dtype) for a in list(chips) + list(lands)],
        in_specs=[HBM_SPEC] * (2 * n) + [SEM_SPEC] * (2 * n) + [pl.BlockSpec(memory_space=pl.ANY)],
        out_specs=[HBM_SPEC] * (2 * n), input_output_aliases={i: i for i in range(2 * n)},
        compiler_params=pltpu.CompilerParams(has_side_effects=SIDE_EFFECT),
    )(*chips, *lands, *sems, after)
    return outs[:n], outs[n:]


def _sum_adamw_layer(parts, own, mine, w, m, v, li, prev, name):
    nl, r, l = w.shape
    lp = parts.shape[2]
    tr = r if r <= 512 else _tile(r, (512, FF_ROWS, 256, 128))
    c1 = 1.0 / (1.0 - ADAM_B1 ** ADAM_STEP)
    c2 = 1.0 / (1.0 - ADAM_B2 ** ADAM_STEP)
    k = 0 if prev is None else 4

    def body(mine_ref, p_ref, o_ref, w_ref, m_ref, v_ref, *rest):
        g_ref, d_ref, nm_ref, nv_ref = rest[k:]
        mine_v = o_ref[0].astype(f32)
        g = jnp.where(mine_ref[0] == 0, mine_v, p_ref[0].astype(f32))
        for s in range(1, 4):
            g = g + jnp.where(mine_ref[0] == s, mine_v, p_ref[s].astype(f32))
        if lp != l:
            g = g[:, :l]
        m_new = ADAM_B1 * m_ref[0] + (1.0 - ADAM_B1) * g
        v_new = ADAM_B2 * v_ref[0] + (1.0 - ADAM_B2) * (g * g)
        g_ref[0] = g
        nm_ref[0] = m_new
        nv_ref[0] = v_new
        d_ref[0] = -ADAM_LR * ((m_new * c1) / (jnp.sqrt(v_new * c2) + ADAM_EPS) + ADAM_WD * w_ref[0])

    tile = pl.BlockSpec((1, tr, l), lambda i, mn: (li, i, 0))
    keep = [pl.BlockSpec(memory_space=pl.ANY)] * k
    return pl.pallas_call(
        body, name=name, out_shape=[_sds((nl, r, l), f32)] * 4,
        grid_spec=pltpu.PrefetchScalarGridSpec(
            num_scalar_prefetch=1, grid=(r // tr,),
            in_specs=[pl.BlockSpec((4, tr, lp), lambda i, mn: (0, i, 0)), pl.BlockSpec((1, tr, lp), lambda i, mn: (mn[0], i, 0)),
                      tile, tile, tile] + keep,
            out_specs=[tile] * 4),
        input_output_aliases={6 + i: i for i in range(k)},
        compiler_params=_cparams(("parallel",)))(mine, parts, own, w, m, v, *(prev or ()))


def _sum_adamw(parts, own, mine, w, m, v, name):
    r, l = w.shape
    lp = parts.shape[2]
    tr = _tile(r, (256, 128, 64, 32, 16, 8))
    c1 = 1.0 / (1.0 - ADAM_B1 ** ADAM_STEP)
    c2 = 1.0 / (1.0 - ADAM_B2 ** ADAM_STEP)

    def body(mine_ref, p_ref, o_ref, w_ref, m_ref, v_ref, g_ref, d_ref, nm_ref, nv_ref):
        mine_v = (o_ref[0] if own.ndim == 3 else o_ref[...]).astype(f32)
        g = jnp.where(mine_ref[0] == 0, mine_v, p_ref[0].astype(f32))
        for s in range(1, parts.shape[0]):
            g = g + jnp.where(mine_ref[0] == s, mine_v, p_ref[s].astype(f32))
        if lp != l:
            g = g[:, :l]
        m_new = ADAM_B1 * m_ref[...] + (1.0 - ADAM_B1) * g
        v_new = ADAM_B2 * v_ref[...] + (1.0 - ADAM_B2) * (g * g)
        g_ref[...] = g
        nm_ref[...] = m_new
        nv_ref[...] = v_new
        d_ref[...] = -ADAM_LR * ((m_new * c1) / (jnp.sqrt(v_new * c2) + ADAM_EPS) + ADAM_WD * w_ref[...])

    tile = pl.BlockSpec((tr, l), lambda i, mn: (i, 0))
    own_spec = (pl.BlockSpec((1, tr, lp), lambda i, mn: (mn[0], i, 0)) if own.ndim == 3
                else pl.BlockSpec((tr, lp), lambda i, mn: (i, 0)))
    return pl.pallas_call(
        body, name=name, out_shape=[_sds((r, l), f32)] * 4,
        grid_spec=pltpu.PrefetchScalarGridSpec(
            num_scalar_prefetch=1, grid=(r // tr,),
            in_specs=[pl.BlockSpec((parts.shape[0], tr, lp), lambda i, mn: (0, i, 0)), own_spec, tile, tile, tile],
            out_specs=[tile] * 4),
        compiler_params=_cparams(("parallel",)))(mine, parts, own, w, m, v)


def _pack(arrs, lead=None):
    if lead is None:
        flat = jnp.concatenate([a.reshape(-1).astype(f32) for a in arrs])
        n = flat.shape[0]
    else:
        flat = jnp.concatenate([a.reshape(lead, -1).astype(f32) for a in arrs], axis=1)
        n = flat.shape[1]
    tot = -(-n // 1024) * 1024
    if lead is None:
        return jnp.pad(flat, (0, tot - n)).reshape(tot // 128, 128)
    return jnp.pad(flat, ((0, 0), (0, tot - n))).reshape(lead, tot // 128, 128)


def _unpack(packed, shapes, lead=False):
    flat = packed.reshape(packed.shape[0], -1) if lead else packed.reshape(-1)
    out, off = [], 0
    for s in shapes:
        n = int(np.prod(s))
        out.append(flat[:, off:off + n].reshape((packed.shape[0],) + tuple(s)) if lead else flat[off:off + n].reshape(s))
        off += n
    return out


def _merge_shards(g, axis):
    g = jnp.moveaxis(g, 0, axis)
    s = g.shape
    return g.reshape(s[:axis] + (s[axis] * s[axis + 1],) + s[axis + 2:])


def _split_shards(full, axis):
    s = full.shape
    g = full.reshape(s[:axis] + (N_DEV, s[axis] // N_DEV) + s[axis + 1:])
    return jnp.moveaxis(g, axis, 0)


def kernel(x, norm_mix, norm_ffn, norm_final, ab_w_in, gdn_conv_w, gdn_a_log, gdn_dt_bias, gdn_norm, hgrn_lower_bounds, hgrn_norm, ab_w_out, c_w_in, c_conv_w, c_conv_b, c_gate_a_w, c_gate_a_b, c_gate_x_w, c_gate_x_b, c_lambda, c_w_out, ffn_w_up, ffn_conv_w, ffn_conv_b, ffn_w_down, loss_target, m_norm_mix, m_norm_ffn, m_norm_final, m_ab_w_in, m_gdn_conv_w, m_gdn_a_log, m_gdn_dt_bias, m_gdn_norm, m_hgrn_lower_bounds, m_hgrn_norm, m_ab_w_out, m_c_w_in, m_c_conv_w, m_c_conv_b, m_c_gate_a_w, m_c_gate_a_b, m_c_gate_x_w, m_c_gate_x_b, m_c_lambda, m_c_w_out, m_ffn_w_up, m_ffn_conv_w, m_ffn_conv_b, m_ffn_w_down, v_norm_mix, v_norm_ffn, v_norm_final, v_ab_w_in, v_gdn_conv_w, v_gdn_a_log, v_gdn_dt_bias, v_gdn_norm, v_hgrn_lower_bounds, v_hgrn_norm, v_ab_w_out, v_c_w_in, v_c_conv_w, v_c_conv_b, v_c_gate_a_w, v_c_gate_a_b, v_c_gate_x_w, v_c_gate_x_b, v_c_lambda, v_c_w_out, v_ffn_w_up, v_ffn_conv_w, v_ffn_conv_b, v_ffn_w_down):
    wl = dict(zip(WEIGHTS, (norm_mix, norm_ffn, norm_final, ab_w_in, gdn_conv_w, gdn_a_log, gdn_dt_bias, gdn_norm, hgrn_lower_bounds, hgrn_norm, ab_w_out, c_w_in, c_conv_w, c_conv_b, c_gate_a_w, c_gate_a_b, c_gate_x_w, c_gate_x_b, c_lambda, c_w_out, ffn_w_up, ffn_conv_w, ffn_conv_b, ffn_w_down)))
    ml = dict(zip(WEIGHTS, (m_norm_mix, m_norm_ffn, m_norm_final, m_ab_w_in, m_gdn_conv_w, m_gdn_a_log, m_gdn_dt_bias, m_gdn_norm, m_hgrn_lower_bounds, m_hgrn_norm, m_ab_w_out, m_c_w_in, m_c_conv_w, m_c_conv_b, m_c_gate_a_w, m_c_gate_a_b, m_c_gate_x_w, m_c_gate_x_b, m_c_lambda, m_c_w_out, m_ffn_w_up, m_ffn_conv_w, m_ffn_conv_b, m_ffn_w_down)))
    vl = dict(zip(WEIGHTS, (v_norm_mix, v_norm_ffn, v_norm_final, v_ab_w_in, v_gdn_conv_w, v_gdn_a_log, v_gdn_dt_bias, v_gdn_norm, v_hgrn_lower_bounds, v_hgrn_norm, v_ab_w_out, v_c_w_in, v_c_conv_w, v_c_conv_b, v_c_gate_a_w, v_c_gate_a_b, v_c_gate_x_w, v_c_gate_x_b, v_c_lambda, v_c_w_out, v_ffn_w_up, v_ffn_conv_w, v_ffn_conv_b, v_ffn_w_down)))

    big = [n for n in SHARDED if n in MATMUL_WEIGHTS]
    vec = [n for n in SHARDED if n not in MATMUL_WEIGHTS]
    shards = {n: wl[n].astype(bf16) for n in big}
    shards['ab_w_in'] = jnp.pad(shards['ab_w_in'], ((0, 0), (0, 0), (0, AB_SHARD_PAD - AB_SHARD)))
    shards['ffn_w_up'] = jnp.pad(jnp.swapaxes(shards['ffn_w_up'], 1, 2), ((0, 0), (0, FF_PAD - FF_SHARD), (0, 0)))
    shards['vec'] = _pack([wl[n] for n in vec])
    pos = _position()
    layer_items = [_layer_items(l) for l in range(DEPTH)]
    posv = [v.astype(jnp.int32).reshape(1) for v in (4 * pos[0] + 2 * pos[1] + pos[2], 2 * pos[0] + pos[1], pos[2])]
    first_items = layer_items[0] + [('vec', None)]
    lands = [_place_own(first_items, shards, posv, "place_own_0")]
    lands += [_place_own(layer_items[l], shards, posv, "place_own_%d" % l) for l in range(1, DEPTH)]
    mix0, ffn0 = layer_items[0][:-2], layer_items[0][-2:]
    first = _gather_now(mix0 + [('vec', None)], shards, lands[0][:len(mix0)] + lands[0][-1:])
    flight = {'shards': {n: shards[n] for n in big}}

    full = {n: wl[n] for n in REPLICATED}

    def start(items, item_lands, token, name, anchor):
        sems, thru, flight['lands'], tok = _gather_start(items, flight['shards'], item_lands, token, name)
        flight['sems'] = list(sems)
        flight['shards'].update(thru)
        full[anchor] = tok

    start(ffn0, lands[0][len(mix0):-1], first[-1], "gather_start_0", ('fwd', 0))

    for n, a in zip(vec, _unpack(first[-1], [wl[n].shape for n in vec], lead=True)):
        full[n] = _merge_shards(a, SHARD_AXIS[n])
    full['ffn_conv_w'] = _block_pad(full['ffn_conv_w'], 2, 4, FF_PAD)
    full['ffn_conv_b'] = _block_pad(full['ffn_conv_b'], 1, 4, FF_PAD)
    for n in big:
        full[n] = {}

    def arrive(items, x_in, tag):
        flight['shards'], got = _gather_wait(items, flight['sems'], flight['shards'], flight['lands'], x_in,
                                             "gather_wait_" + tag)
        return _gather_forward(items, got, "gather_forward_" + tag)

    def fetch(l, x_in, part):
        if l == 0 and part == 'mix':
            items, got = mix0, first[:len(mix0)]
        elif l == 0:
            items, got = ffn0, arrive(ffn0, x_in, "0")
            start(layer_items[1], lands[1], got[0], "gather_start_1", ('fwdf', 0))
        elif part == 'mix':
            items = layer_items[l]
            got = arrive(items, x_in, str(l))
            if l + 1 < DEPTH:
                start(layer_items[l + 1], lands[l + 1], got[0], "gather_start_%d" % (l + 1), ('fwd', l))
        else:
            return
        for (nm, li), a in zip(items, got):
            if nm == 'ab_w_in':
                a = _ab_permute(_block_unpad(a, 1, N_DEV, AB_SHARD))
            elif nm in ('ab_w_out', 'c_w_out'):
                a = a.reshape(D_MODEL, D_MODEL)
            elif nm in ('c_gate_a_w', 'c_gate_x_w'):
                a = a.reshape(HEADS, LRU_BLOCK, LRU_BLOCK)
            elif nm == 'ffn_w_down':
                a = a.reshape(D_FFP, D_MODEL)
            elif nm == 'ffn_w_up':
                a = a.reshape(2 * D_FFP, D_MODEL)
            full[nm][li] = a

    cpos = lax.axis_index("c").astype(jnp.int32).reshape(1)
    mine = (2 * lax.axis_index("x") + lax.axis_index("y")).astype(jnp.int32).reshape(1)
    pending = {}

    def exchange(key, items, g, anchor):
        fulls = {}
        for nm, _ in items:
            a = g[nm].astype(bf16)
            if nm == 'ab_w_in':
                a = _block_pad(_ab_unpermute(a), 1, N_DEV, AB_SHARD_PAD)
            fulls[nm] = a.reshape((1,) + BLOCK_LAYOUT[nm][0][1:])
        pair, _ = _pair_exchange(fulls, jnp.zeros((8, 128), f32), "_" + key)
        chips = [_chip_sum(nm, fulls[nm], pair[nm], cpos, "_" + key) for nm in fulls]
        sems, chips, crosses, tok = _cross_start(chips, "grad_cross_start_" + key)
        pending[key] = (items, sems, chips, crosses)
        full[anchor] = tok

    stash = {}

    def push(l, part, g):
        if l == 0:
            exchange("0f" if part == 'ffn' else "0", ffn0 if part == 'ffn' else mix0, g,
                     ('bwdm', 0) if part == 'ffn' else ('bwd', -1))
            return
        stash.update(g)
        if part == 'mix':
            exchange(str(l), layer_items[l], dict(stash), ('bwd', l - 1))
            stash.clear()

    loss, dx, grads = _local_step(x[0], loss_target[0], full, fetch, push)

    grads['ffn_conv_w'] = _block_unpad(grads['ffn_conv_w'], 2, 4, FF_SHARD)
    grads['ffn_conv_b'] = _block_unpad(grads['ffn_conv_b'], 1, 4, FF_SHARD)
    fulls = {'vec': _pack([_split_shards(grads[n], SHARD_AXIS[n]) for n in vec], lead=N_DEV)}
    chip, rep_chip, recv, rrep = _exchange_grads(fulls, _pack([grads[n] for n in REPLICATED]))
    res = {}
    stacked = {}
    after = full['bwd', -1]
    for key in ("3", "2", "1", "0f", "0"):
        items, sems, chips, lands = pending[key]
        chips, lands = _cross_wait(sems, chips, lands, after, "grad_cross_wait_" + key)
        for (n, li), own, parts in zip(items, chips, lands):
            if n == 'ffn_w_up':
                wmv = [jnp.swapaxes(a[n], 1, 2) for a in (wl, ml, vl)]
                rp = FF_PAD
            else:
                shp = wl[n].shape
                rp = int(np.prod(shp[1:-1]))
                wmv = [a[n].reshape(shp[0], rp, shp[-1]) for a in (wl, ml, vl)]
            stacked[n] = _sum_adamw_layer(parts.reshape(4, rp, -1), own.reshape(4, rp, -1), mine, *wmv, li,
                                          stacked.get(n), "adamw_%s_%d" % (n, li))
        if key == "0f":
            after = stacked['ffn_w_up'][0]
    for n in big:
        for kind, o in zip(("grad", "delta", "new_m", "new_v"), stacked[n]):
            res[kind, n] = jnp.swapaxes(o, 1, 2) if n == 'ffn_w_up' else o.reshape(wl[n].shape)
    for names, parts, own, tag in ((vec, recv['vec'], chip['vec'], "adamw_vectors"),
                                   (REPLICATED, rrep, rep_chip, "adamw_replicated")):
        outs = _sum_adamw(parts, own, mine, _pack([wl[n] for n in names]), _pack([ml[n] for n in names]),
                          _pack([vl[n] for n in names]), tag)
        for kind, o in zip(("grad", "delta", "new_m", "new_v"), outs):
            for n, a in zip(names, _unpack(o, [wl[n].shape for n in names])):
                res[kind, n] = a

    loss = lax.psum(loss, ("x", "y", "c"))
    return (loss, dx[None], *[res[kind, n] for kind in ("grad", "delta", "new_m", "new_v") for n in WEIGHTS])
```

```python
import functools

import numpy as np
import jax
import jax.numpy as jnp
from jax import lax
from jax.experimental import pallas as pl
from jax.experimental.pallas import tpu as pltpu

f32 = jnp.float32
bf16 = jnp.bfloat16
HI = lax.Precision.HIGHEST
MESH = pl.DeviceIdType.MESH

N_DEV = 8
D_MODEL = 1024
DEPTH = 4
EPS = 1e-6
F_FLOOR = 1e-30
HEADS = 4
HEAD_DIM = 128
GDN_WIDTH = 512
GDN_CONV = 4
GDN_CHUNK = 64
HGRN_CHUNK = 16
HGRN_STEP = 128
MIX_WIDTH = 1024
AB_COLS = 4104
AB_PAD = 4224
LRU_WIDTH = 1024
LRU_BLOCK = 256
LRU_CONV = 4
RG_C = 8.0
D_FF = 2816
FF_SHARD = 704
FF_PAD = 768
D_FFP = 4 * FF_PAD
FF_ROWS = 352
AB_SHARD, AB_SHARD_PAD = 513, 640
FFN_CONV = 3
ADAM_LR, ADAM_B1, ADAM_B2, ADAM_EPS, ADAM_WD, ADAM_STEP = 0.001, 0.9, 0.999, 1e-08, 0.01, 10
VMEM_LIMIT = 56 * 1024 * 1024
ROW_SLAB = 32

OFF_Q, OFF_K, OFF_V, OFF_Z, OFF_QB, OFF_FB, OFF_IB, OFF_GB, OFF_BA = 0, 512, 1024, 1536, 2048, 2560, 3072, 3584, 4096

WEIGHTS = ['norm_mix', 'norm_ffn', 'norm_final', 'ab_w_in', 'gdn_conv_w', 'gdn_a_log', 'gdn_dt_bias', 'gdn_norm',
           'hgrn_lower_bounds', 'hgrn_norm', 'ab_w_out', 'c_w_in', 'c_conv_w', 'c_conv_b', 'c_gate_a_w', 'c_gate_a_b',
           'c_gate_x_w', 'c_gate_x_b', 'c_lambda', 'c_w_out', 'ffn_w_up', 'ffn_conv_w', 'ffn_conv_b', 'ffn_w_down']
SHARD_AXIS = {'norm_mix': None, 'norm_ffn': None, 'norm_final': None, 'ab_w_in': 2, 'gdn_conv_w': 2, 'gdn_a_log': None,
              'gdn_dt_bias': None, 'gdn_norm': None, 'hgrn_lower_bounds': None, 'hgrn_norm': None, 'ab_w_out': 1,
              'c_w_in': 2, 'c_conv_w': 2, 'c_conv_b': 1, 'c_gate_a_w': 2, 'c_gate_a_b': 1, 'c_gate_x_w': 2,
              'c_gate_x_b': 1, 'c_lambda': 1, 'c_w_out': 1, 'ffn_w_up': 2, 'ffn_conv_w': 2, 'ffn_conv_b': None,
              'ffn_w_down': 1}
MATMUL_WEIGHTS = ('ab_w_in', 'ab_w_out', 'c_w_in', 'c_gate_a_w', 'c_gate_x_w', 'c_w_out', 'ffn_w_up', 'ffn_w_down')
SHARDED = [n for n in WEIGHTS if SHARD_AXIS[n] is not None]
REPLICATED = [n for n in WEIGHTS if SHARD_AXIS[n] is None]


def _tile(n, prefs=(512, 384, 256, 128)):
    for p in prefs:
        if n % p == 0:
            return p
    return n


def _cparams(sem=None):
    kw = dict(vmem_limit_bytes=VMEM_LIMIT)
    if sem is not None:
        kw['dimension_semantics'] = sem
    return pltpu.CompilerParams(**kw)


def _sds(shape, dtype):
    return jax.ShapeDtypeStruct(tuple(shape), dtype)


def _sigmoid(x):
    return 1.0 / (1.0 + jnp.exp(-x))


def _silu(x):
    return x * (0.5 * jnp.tanh(0.5 * x) + 0.5)


def _log1p(x):
    u = 1.0 + x
    return jnp.where(u == 1.0, x, jnp.log(u) * (x / jnp.where(u == 1.0, 1.0, u - 1.0)))


def _softplus(x):
    return jnp.maximum(x, 0.0) + _log1p(jnp.exp(-jnp.abs(x)))


def _expm1(x):
    small = jnp.abs(x) < 0.05
    xs = jnp.where(small, x, 0.0)
    series = xs * (1.0 + xs * (0.5 + xs * (1.0 / 6.0 + xs * (1.0 / 24.0 + xs * (1.0 / 120.0)))))
    return jnp.where(small, series, jnp.exp(x) - 1.0)


def _gelu(x):
    return 0.5 * x * (1.0 + jnp.tanh(0.7978845608028654 * (x + 0.044715 * x * x * x)))


def _rms(x, gain):
    return x * lax.rsqrt(jnp.mean(x * x, axis=-1, keepdims=True) + EPS) * gain


def _dot(a, b, dims=((1,), (0,)), precision=None):
    return lax.dot_general(a, b, (dims, ((), ())), precision=precision, preferred_element_type=f32)


def _bdot(a, b, dims=((1,), (0,))):
    return _dot(a.astype(bf16), b.astype(bf16), dims)


NT = ((1,), (1,))
TN = ((0,), (0,))


def _shift_down(x, k):
    if k == 0:
        return x
    row = lax.broadcasted_iota(jnp.int32, x.shape, 0)
    return jnp.where(row >= k, pltpu.roll(x, k, 0), 0.0)


def _shift_up(x, k, fill=0.0):
    if k == 0:
        return x
    n = x.shape[0]
    row = lax.broadcasted_iota(jnp.int32, x.shape, 0)
    return jnp.where(row < n - k, pltpu.roll(x, n - k, 0), fill)


def _conv_fwd(x, w_ref, width):
    acc = w_ref[width - 1:width, :] * x
    for k in range(width - 1):
        acc = acc + w_ref[k:k + 1, :] * _shift_down(x, width - 1 - k)
    return acc


def _conv_bwd(x, dout, w_ref, dw_ref, width):
    dx = w_ref[width - 1:width, :] * dout
    dw_ref[width - 1:width, :] = jnp.sum(dout * x, axis=0, keepdims=True)
    for k in range(width - 1):
        s = width - 1 - k
        dx = dx + w_ref[k:k + 1, :] * _shift_up(dout, s)
        dw_ref[k:k + 1, :] = jnp.sum(dout * _shift_down(x, s), axis=0, keepdims=True)
    return dx


MM_VMEM_BUDGET = 36 * 1024 * 1024
MM_MAX_TILE = 1024 * 1024


def _mm_tiles(m, n, k, out_bytes):
    best = None
    for tm in (1024, 512, 384, 256, 128):
        if m % tm:
            continue
        for tn in range(1536, 0, -128):
            if n % tn or tm * tn > MM_MAX_TILE:
                continue
            score = (tm * tn, min(tm, tn))
            if 2 * (tm * k * 2 + k * tn * 2 + tm * tn * out_bytes) <= MM_VMEM_BUDGET and (best is None or score > best[0]):
                best = (score, tm, tn)
    return (best[1], best[2]) if best else (_tile(m), _tile(n))


def _mm(a, b, *, ta=False, tb=False, add=None, out_dtype=f32, name):
    m, k = (a.shape[1], a.shape[0]) if ta else a.shape
    n = b.shape[0] if tb else b.shape[1]
    tm, tn = _mm_tiles(m, n, k, jnp.dtype(out_dtype).itemsize + (4 if add is not None else 0))
    dims = ((0 if ta else 1,), (1 if tb else 0,))

    def body(*refs):
        a_ref, b_ref = refs[0], refs[1]
        o_ref = refs[-1]
        r = _dot(a_ref[...], b_ref[...], dims)
        if add is not None:
            r = r + refs[2][...]
        o_ref[...] = r.astype(out_dtype)

    a_spec = pl.BlockSpec((k, tm), lambda j, i: (0, i)) if ta else pl.BlockSpec((tm, k), lambda j, i: (i, 0))
    b_spec = pl.BlockSpec((tn, k), lambda j, i: (j, 0)) if tb else pl.BlockSpec((k, tn), lambda j, i: (0, j))
    o_spec = pl.BlockSpec((tm, tn), lambda j, i: (i, j))
    ins, specs = [a, b], [a_spec, b_spec]
    if add is not None:
        ins.append(add)
        specs.append(o_spec)
    return pl.pallas_call(body, name=name, grid=(n // tn, m // tm), in_specs=specs, out_specs=o_spec,
                          out_shape=_sds((m, n), out_dtype), compiler_params=_cparams(("parallel", "parallel")))(*ins)


def _rms_fwd(x, gain, name):
    t, d = x.shape
    tr = _tile(t, (256, 128))

    def body(x_ref, g_ref, h_ref):
        h_ref[...] = _rms(x_ref[...], g_ref[...]).astype(bf16)

    return pl.pallas_call(body, name=name, grid=(t // tr,),
                          in_specs=[pl.BlockSpec((tr, d), lambda i: (i, 0)), pl.BlockSpec((1, d), lambda i: (0, 0))],
                          out_specs=pl.BlockSpec((tr, d), lambda i: (i, 0)), out_shape=_sds((t, d), bf16),
                          compiler_params=_cparams(("parallel",)))(x, gain)


def _rms_bwd(x, gain, dh, dres, name):
    t, d = x.shape
    tr = _tile(t, (256, 128))

    def body(x_ref, g_ref, dh_ref, dres_ref, dx_ref, dxb_ref, dg_ref):
        _, vjp = jax.vjp(_rms, x_ref[...], g_ref[...])
        dx, dg = vjp(dh_ref[...])
        dx = dx + dres_ref[...]
        dx_ref[...] = dx
        dxb_ref[...] = dx.astype(bf16)

        @pl.when(pl.program_id(0) == 0)
        def _():
            dg_ref[...] = jnp.zeros_like(dg_ref)

        dg_ref[...] += dg

    row = pl.BlockSpec((tr, d), lambda i: (i, 0))
    vec = pl.BlockSpec((1, d), lambda i: (0, 0))
    return pl.pallas_call(body, name=name, grid=(t // tr,), in_specs=[row, vec, row, row], out_specs=[row, row, vec],
                          out_shape=[_sds((t, d), f32), _sds((t, d), bf16), _sds((1, d), f32)],
                          compiler_params=_cparams(("arbitrary",)))(x, gain, dh, dres)


def _loss_head(x, gain, target):
    t, d = x.shape
    tr = _tile(t, (256, 128))

    def f(xv, g, tgt):
        err = _rms(xv, g) - tgt
        return 0.5 * jnp.sum(jnp.mean(err * err, axis=-1, keepdims=True), axis=0, keepdims=True)

    def body(x_ref, g_ref, t_ref, loss_ref, dx_ref, dxb_ref, dg_ref):
        loss, vjp = jax.vjp(lambda xv, g: f(xv, g, t_ref[...]), x_ref[...], g_ref[...])
        dx, dg = vjp(jnp.ones((1, 1), f32))
        dx_ref[...] = dx
        dxb_ref[...] = dx.astype(bf16)

        @pl.when(pl.program_id(0) == 0)
        def _():
            dg_ref[...] = jnp.zeros_like(dg_ref)
            loss_ref[...] = jnp.zeros_like(loss_ref)

        dg_ref[...] += dg
        loss_ref[...] += jnp.broadcast_to(loss, loss_ref.shape)

    row = pl.BlockSpec((tr, d), lambda i: (i, 0))
    vec = pl.BlockSpec((1, d), lambda i: (0, 0))
    one = pl.BlockSpec((8, 128), lambda i: (0, 0))
    return pl.pallas_call(body, name="loss_head", grid=(t // tr,), in_specs=[row, vec, row],
                          out_specs=[one, row, row, vec],
                          out_shape=[_sds((8, 128), f32), _sds((t, d), f32), _sds((t, d), bf16), _sds((1, d), f32)],
                          compiler_params=_cparams(("arbitrary",)))(x, gain, target)


def _ffn_act_fwd(u, conv_w, conv_b):
    t = u.shape[0]
    tc = FF_PAD // 2
    nb = D_FFP // tc

    def body(g_ref, v_ref, w_ref, b_ref, a_ref):
        gc = _conv_fwd(g_ref[...], w_ref, FFN_CONV) + b_ref[...]
        a_ref[...] = (_silu(gc) * v_ref[...]).astype(bf16)

    return pl.pallas_call(
        body, name="ffn_act_fwd", grid=(nb,),
        in_specs=[pl.BlockSpec((t, tc), lambda j: (0, j)), pl.BlockSpec((t, tc), lambda j: (0, j + nb)),
                  pl.BlockSpec((FFN_CONV, tc), lambda j: (0, j)), pl.BlockSpec((1, tc), lambda j: (0, j))],
        out_specs=pl.BlockSpec((t, tc), lambda j: (0, j)), out_shape=_sds((t, D_FFP), bf16),
        compiler_params=_cparams(("parallel",)))(u, u, conv_w, conv_b)


def _ffn_act_bwd(u, conv_w, conv_b, da):
    t = u.shape[0]
    tc = FF_PAD // 2
    nb = D_FFP // tc

    def act(gc, val):
        return _silu(gc) * val

    def body(g_ref, v_ref, w_ref, b_ref, da_ref, a_ref, dg_ref, dv_ref, dw_ref, db_ref, gc_ref):
        gp = g_ref[...]
        gc_ref[...] = _conv_fwd(gp, w_ref, FFN_CONV) + b_ref[...]

        def slab(i, carry):
            rows = pl.ds(pl.multiple_of(i * ROW_SLAB, ROW_SLAB), ROW_SLAB)
            a, vjp = jax.vjp(act, gc_ref[rows, :], v_ref[rows, :])
            dgc, dval = vjp(da_ref[rows, :])
            a_ref[rows, :] = a.astype(bf16)
            dv_ref[rows, :] = dval.astype(bf16)
            gc_ref[rows, :] = dgc
            return carry

        lax.fori_loop(0, t // ROW_SLAB, slab, 0, unroll=2)
        dgc = gc_ref[...]
        db_ref[...] = jnp.sum(dgc, axis=0, keepdims=True)
        dg_ref[...] = _conv_bwd(gp, dgc, w_ref, dw_ref, FFN_CONV).astype(bf16)

    col = pl.BlockSpec((t, tc), lambda j: (0, j))
    return pl.pallas_call(
        body, name="ffn_act_bwd", grid=(nb,),
        in_specs=[col, pl.BlockSpec((t, tc), lambda j: (0, j + nb)), pl.BlockSpec((FFN_CONV, tc), lambda j: (0, j)),
                  pl.BlockSpec((1, tc), lambda j: (0, j)), col],
        out_specs=[col, col, col, pl.BlockSpec((FFN_CONV, tc), lambda j: (0, j)), pl.BlockSpec((1, tc), lambda j: (0, j))],
        out_shape=[_sds((t, D_FFP), bf16), _sds((t, D_FFP), bf16), _sds((t, D_FFP), bf16), _sds((FFN_CONV, D_FFP), f32),
                   _sds((1, D_FFP), f32)],
        scratch_shapes=[pltpu.VMEM((t, tc), f32)],
        compiler_params=_cparams(("parallel",)))(u, u, conv_w, conv_b, da)


def _lru_gates(xc, ra, ia, lam):
    r = _sigmoid(ra)
    i = _sigmoid(ia)
    log_a = -RG_C * r * _softplus(-lam)
    a = jnp.exp(log_a)
    u = jnp.sqrt(jnp.maximum(-_expm1(2.0 * log_a), 0.0)) * (i * xc)
    return a, u


def _lin_scan(a, u):
    n = a.shape[0]
    row = lax.broadcasted_iota(jnp.int32, a.shape, 0)
    s = 1
    while s < n:
        keep = row >= s
        u = a * jnp.where(keep, pltpu.roll(u, s, 0), 0.0) + u
        a = a * jnp.where(keep, pltpu.roll(a, s, 0), 1.0)
        s *= 2
    return u


def _rev_scan(a_next, d):
    n = d.shape[0]
    row = lax.broadcasted_iota(jnp.int32, d.shape, 0)
    a = a_next
    s = 1
    while s < n:
        keep = row < n - s
        d = a * jnp.where(keep, pltpu.roll(d, n - s, 0), 0.0) + d
        a = a * jnp.where(keep, pltpu.roll(a, n - s, 0), 1.0)
        s *= 2
    return d


def _col_conv_fwd(p, col_off, conv_w, conv_b, width, tc, name):
    t = p.shape[0]
    c = conv_w.shape[1]
    ob = col_off // tc

    def body(x_ref, w_ref, b_ref, o_ref):
        o_ref[...] = _conv_fwd(x_ref[...], w_ref, width) + b_ref[...]

    return pl.pallas_call(
        body, name=name, grid=(c // tc,),
        in_specs=[pl.BlockSpec((t, tc), lambda j: (0, j + ob)), pl.BlockSpec((width, tc), lambda j: (0, j)),
                  pl.BlockSpec((1, tc), lambda j: (0, j))],
        out_specs=pl.BlockSpec((t, tc), lambda j: (0, j)), out_shape=_sds((t, c), f32),
        compiler_params=_cparams(("parallel",)))(p, conv_w, conv_b)


def _col_conv_bwd(p, col_off, conv_w, dxc, width, tc, name):
    t = p.shape[0]
    c = conv_w.shape[1]
    ob = col_off // tc

    def body(x_ref, w_ref, d_ref, dx_ref, dw_ref, db_ref):
        d = d_ref[...]
        db_ref[...] = jnp.sum(d, axis=0, keepdims=True)
        dx_ref[...] = _conv_bwd(x_ref[...], d, w_ref, dw_ref, width).astype(bf16)

    col = pl.BlockSpec((t, tc), lambda j: (0, j))
    return pl.pallas_call(
        body, name=name, grid=(c // tc,),
        in_specs=[pl.BlockSpec((t, tc), lambda j: (0, j + ob)), pl.BlockSpec((width, tc), lambda j: (0, j)), col],
        out_specs=[col, pl.BlockSpec((width, tc), lambda j: (0, j)), pl.BlockSpec((1, tc), lambda j: (0, j))],
        out_shape=[_sds((t, c), bf16), _sds((width, c), f32), _sds((1, c), f32)],
        compiler_params=_cparams(("parallel",)))(p, conv_w, dxc)


def _lru_fwd(p, xc, wa, ba, wx, bx, lam):
    t = p.shape[0]
    bw = LRU_BLOCK

    def body(y_ref, xc_ref, wa_ref, ba_ref, wx_ref, bx_ref, lam_ref, out_ref, hs_ref, a_ref):
        xc_v = xc_ref[...]
        xb = xc_v.astype(bf16)
        ra = _dot(xb, wa_ref[0]) + ba_ref[...]
        ia = _dot(xb, wx_ref[0]) + bx_ref[...]
        a, u = _lru_gates(xc_v, ra, ia, lam_ref[...])
        a_ref[...] = a
        hs = _lin_scan(a, u)
        hs_ref[...] = hs
        out_ref[...] = (hs * _gelu(y_ref[...])).astype(bf16)

    col = pl.BlockSpec((t, bw), lambda h: (0, h))
    vec = pl.BlockSpec((1, bw), lambda h: (0, h))
    mat = pl.BlockSpec((1, bw, bw), lambda h: (h, 0, 0))
    return pl.pallas_call(
        body, name="lru_fwd", grid=(HEADS,), in_specs=[col, col, mat, vec, mat, vec, vec], out_specs=[col, col, col],
        out_shape=[_sds((t, LRU_WIDTH), bf16), _sds((t, LRU_WIDTH), f32), _sds((t, LRU_WIDTH), f32)],
        compiler_params=_cparams(("parallel",)))(p, xc, wa, ba, wx, bx, lam)


def _lru_bwd_scan(p, a, hs, dout):
    t = p.shape[0]
    bw = LRU_BLOCK

    def body(y_ref, a_ref, hs_ref, do_ref, dy_ref, da_ref, du_ref):
        hs_v = hs_ref[...]
        do = do_ref[...]
        gate, vjp = jax.vjp(_gelu, y_ref[...])
        dy_ref[...] = vjp(do * hs_v)[0].astype(bf16)
        g = _rev_scan(_shift_up(a_ref[...], 1), do * gate)
        du_ref[...] = g
        da_ref[...] = g * _shift_down(hs_v, 1)

    col = pl.BlockSpec((t, bw), lambda h: (0, h))
    return pl.pallas_call(
        body, name="lru_bwd_scan", grid=(HEADS,), in_specs=[col, col, col, col], out_specs=[col, col, col],
        out_shape=[_sds((t, LRU_WIDTH), bf16), _sds((t, LRU_WIDTH), f32), _sds((t, LRU_WIDTH), f32)],
        compiler_params=_cparams(("parallel",)))(p, a, hs, dout)


def _lru_bwd_gates(xc, da, du, wa, ba, wx, bx, lam):
    t = xc.shape[0]
    bw = LRU_BLOCK
    tr = _tile(t, (512, 256, 128))

    def body(xc_ref, da_ref, du_ref, wa_ref, ba_ref, wx_ref, bx_ref, lam_ref,
             dxc_ref, dwa_ref, dwx_ref, dba_ref, dbx_ref, dlam_ref):
        xc_v = xc_ref[...]
        xb = xc_v.astype(bf16)
        ra = _dot(xb, wa_ref[0]) + ba_ref[...]
        ia = _dot(xb, wx_ref[0]) + bx_ref[...]
        _, vjp = jax.vjp(_lru_gates, xc_v, ra, ia, lam_ref[...])
        dxc, dra, dia, dlam = vjp((da_ref[...], du_ref[...]))
        drb, dib = dra.astype(bf16), dia.astype(bf16)
        dxc_ref[...] = dxc + _dot(drb, wa_ref[0], NT) + _dot(dib, wx_ref[0], NT)

        @pl.when(pl.program_id(1) == 0)
        def _():
            dwa_ref[...] = jnp.zeros_like(dwa_ref)
            dwx_ref[...] = jnp.zeros_like(dwx_ref)
            dba_ref[...] = jnp.zeros_like(dba_ref)
            dbx_ref[...] = jnp.zeros_like(dbx_ref)
            dlam_ref[...] = jnp.zeros_like(dlam_ref)

        dwa_ref[0] += _dot(xb, drb, TN)
        dwx_ref[0] += _dot(xb, dib, TN)
        dba_ref[...] += jnp.sum(dra, axis=0, keepdims=True)
        dbx_ref[...] += jnp.sum(dia, axis=0, keepdims=True)
        dlam_ref[...] += dlam

    tile = pl.BlockSpec((tr, bw), lambda h, i: (i, h))
    vec = pl.BlockSpec((1, bw), lambda h, i: (0, h))
    mat = pl.BlockSpec((1, bw, bw), lambda h, i: (h, 0, 0))
    return pl.pallas_call(
        body, name="lru_bwd_gates", grid=(HEADS, t // tr), in_specs=[tile, tile, tile, mat, vec, mat, vec, vec],
        out_specs=[tile, mat, mat, vec, vec, vec],
        out_shape=[_sds((t, LRU_WIDTH), f32), _sds((HEADS, bw, bw), f32), _sds((HEADS, bw, bw), f32),
                   _sds((1, LRU_WIDTH), f32), _sds((1, LRU_WIDTH), f32), _sds((1, LRU_WIDTH), f32)],
        compiler_params=_cparams(("parallel", "arbitrary")))(xc, da, du, wa, ba, wx, bx, lam)


def _gdn_pre_fn(cq, ck, cv, ba, alog, dtb, h):
    q, k, v = _silu(cq), _silu(ck), _silu(cv)
    q = q * lax.rsqrt(jnp.sum(q * q, axis=-1, keepdims=True) + EPS) * (HEAD_DIM ** -0.5)
    k = k * lax.rsqrt(jnp.sum(k * k, axis=-1, keepdims=True) + EPS)
    lane = lax.broadcasted_iota(jnp.int32, (1, HEAD_DIM), 1)
    mb = (lane == h).astype(f32)
    ma = (lane == HEADS + h).astype(f32)
    beta_raw = jnp.sum(ba * mb, axis=-1, keepdims=True)
    alpha = jnp.sum(ba * ma, axis=-1, keepdims=True)
    al = jnp.sum(alog * mb, axis=-1, keepdims=True)
    db = jnp.sum(dtb * mb, axis=-1, keepdims=True)
    beta = _sigmoid(beta_raw)
    g = -jnp.exp(al) * _softplus(alpha + db)
    return q, k, v, jnp.broadcast_to(beta, q.shape), jnp.broadcast_to(g, q.shape)


def _gdn_pre_fwd(p, conv_w, alog, dtb):
    t = p.shape[0]
    hd = HEAD_DIM

    def body(pq_ref, pk_ref, pv_ref, ba_ref, wq_ref, wk_ref, wv_ref, al_ref, dt_ref, q_ref, k_ref, v_ref, b_ref, g_ref):
        h = pl.program_id(0)
        cq = _conv_fwd(pq_ref[...], wq_ref, GDN_CONV)
        ck = _conv_fwd(pk_ref[...], wk_ref, GDN_CONV)
        cv = _conv_fwd(pv_ref[...], wv_ref, GDN_CONV)
        q, k, v, be, ge = _gdn_pre_fn(cq, ck, cv, ba_ref[...], al_ref[...], dt_ref[...], h)
        q_ref[...], k_ref[...], v_ref[...], b_ref[...], g_ref[...] = q, k, v, be, ge

    def pcol(off):
        return pl.BlockSpec((t, hd), lambda h: (0, h + off // hd))

    def wcol(off):
        return pl.BlockSpec((GDN_CONV, hd), lambda h: (0, h + off // hd))

    vec = pl.BlockSpec((1, hd), lambda h: (0, 0))
    out = pl.BlockSpec((t, hd), lambda h: (0, h))
    return pl.pallas_call(
        body, name="gdn_pre_fwd", grid=(HEADS,),
        in_specs=[pcol(OFF_Q), pcol(OFF_K), pcol(OFF_V), pl.BlockSpec((t, hd), lambda h: (0, OFF_BA // hd)),
                  wcol(0), wcol(GDN_WIDTH), wcol(2 * GDN_WIDTH), vec, vec],
        out_specs=[out] * 5, out_shape=[_sds((t, GDN_WIDTH), f32)] * 5,
        compiler_params=_cparams(("parallel",)))(p, p, p, p, conv_w, conv_w, conv_w, alog, dtb)


def _gdn_pre_bwd(p, conv_w, alog, dtb, dq, dk, dv, dbe, dge):
    t = p.shape[0]
    hd = HEAD_DIM

    def body(pq_ref, pk_ref, pv_ref, ba_ref, wq_ref, wk_ref, wv_ref, al_ref, dt_ref,
             dq_ref, dk_ref, dv_ref, dbe_ref, dge_ref,
             opq_ref, opk_ref, opv_ref, dba_ref, dwq_ref, dwk_ref, dwv_ref, dal_ref, ddt_ref):
        h = pl.program_id(0)
        pq, pk, pv = pq_ref[...], pk_ref[...], pv_ref[...]
        cq = _conv_fwd(pq, wq_ref, GDN_CONV)
        ck = _conv_fwd(pk, wk_ref, GDN_CONV)
        cv = _conv_fwd(pv, wv_ref, GDN_CONV)
        _, vjp = jax.vjp(functools.partial(_gdn_pre_fn, h=h), cq, ck, cv, ba_ref[...], al_ref[...], dt_ref[...])
        dcq, dck, dcv, dba, dal, ddt = vjp((dq_ref[...], dk_ref[...], dv_ref[...], dbe_ref[...], dge_ref[...]))
        opq_ref[...] = _conv_bwd(pq, dcq, wq_ref, dwq_ref, GDN_CONV).astype(bf16)
        opk_ref[...] = _conv_bwd(pk, dck, wk_ref, dwk_ref, GDN_CONV).astype(bf16)
        opv_ref[...] = _conv_bwd(pv, dcv, wv_ref, dwv_ref, GDN_CONV).astype(bf16)

        @pl.when(h == 0)
        def _():
            dba_ref[...] = jnp.zeros_like(dba_ref)
            dal_ref[...] = jnp.zeros_like(dal_ref)
            ddt_ref[...] = jnp.zeros_like(ddt_ref)

        dba_ref[...] += dba
        dal_ref[...] += dal
        ddt_ref[...] += ddt

    def pcol(off):
        return pl.BlockSpec((t, hd), lambda h: (0, h + off // hd))

    def wcol(off):
        return pl.BlockSpec((GDN_CONV, hd), lambda h: (0, h + off // hd))

    vec = pl.BlockSpec((1, hd), lambda h: (0, 0))
    col = pl.BlockSpec((t, hd), lambda h: (0, h))
    full = pl.BlockSpec((t, hd), lambda h: (0, 0))
    wout = pl.BlockSpec((GDN_CONV, hd), lambda h: (0, h))
    return pl.pallas_call(
        body, name="gdn_pre_bwd", grid=(HEADS,),
        in_specs=[pcol(OFF_Q), pcol(OFF_K), pcol(OFF_V), pl.BlockSpec((t, hd), lambda h: (0, OFF_BA // hd)),
                  wcol(0), wcol(GDN_WIDTH), wcol(2 * GDN_WIDTH), vec, vec, col, col, col, col, col],
        out_specs=[col, col, col, full, wout, wout, wout, vec, vec],
        out_shape=[_sds((t, GDN_WIDTH), bf16)] * 3 + [_sds((t, hd), f32)] + [_sds((GDN_CONV, GDN_WIDTH), f32)] * 3
        + [_sds((1, hd), f32)] * 2,
        compiler_params=_cparams(("arbitrary",)))(p, p, p, p, conv_w, conv_w, conv_w, alog, dtb, dq, dk, dv, dbe, dge)


BNN = (((2,), (1,)), ((0,), (0,)))
BNT = (((2,), (2,)), ((0,), (0,)))
BTN = (((1,), (1,)), ((0,), (0,)))


def _hdot(a, b, dn=BNN, precision=None):
    return lax.dot_general(a, b, dn, precision=precision, preferred_element_type=f32)


def _hbdot(a, b, dn=BNN):
    return _hdot(a.astype(bf16), b.astype(bf16), dn)


def _tri_inverse(a):
    c = a.shape[-1]
    r = lax.broadcasted_iota(jnp.int32, (c, c), 0)
    col = lax.broadcasted_iota(jnp.int32, (c, c), 1)
    m = -a
    inv = jnp.where(r == col, 1.0, 0.0) + m
    s = 2
    while s < c:
        m = _hdot(m, m, precision=HI)
        inv = inv + _hdot(inv, m, precision=HI)
        s *= 2
    return inv


@jax.custom_vjp
def _saved_inverse(a, inv):
    return inv


def _saved_inverse_fwd(a, inv):
    return inv, inv


def _saved_inverse_bwd(inv, dinv):
    return -_hdot(_hdot(inv, dinv, BTN, precision=HI), inv, BNT, precision=HI), jnp.zeros_like(inv)


_saved_inverse.defvjp(_saved_inverse_fwd, _saved_inverse_bwd)


def _gdn_chunk(s, q, k, v, ge, be, tinv=None):
    nh, c, _ = q.shape
    r = lax.broadcasted_iota(jnp.int32, (c, c), 0)
    col = lax.broadcasted_iota(jnp.int32, (c, c), 1)
    causal = r >= col
    tri = jnp.broadcast_to(causal.astype(f32), (nh, c, c))
    gc = _hdot(tri, ge, precision=HI)
    gcc = gc[:, :, :c]
    gcr = jnp.swapaxes(gc, 1, 2)[:, :c, :]
    decay = jnp.where(causal, jnp.exp(jnp.where(causal, gcc - gcr, 0.0)), 0.0)
    kb = k * be
    lower = jnp.where(r > col, _hbdot(kb, k, BNT) * decay, 0.0)
    tinv = _tri_inverse(lower) if tinv is None else _saved_inverse(lower, tinv)
    egc = jnp.exp(gc)
    u = _hdot(tinv, v * be, precision=HI)
    w = _hdot(tinv, kb * egc, precision=HI)
    attn = _hbdot(q, k, BNT) * decay
    gl = gc[:, c - 1:c, :]
    v_new = u - _hbdot(w, s)
    o = _hbdot(q * egc, s) + _hbdot(attn, v_new)
    s_new = s * jnp.exp(gl) + _hbdot(k * jnp.exp(gl - gc), v_new, BTN)
    return o, s_new, tinv


def _heads_major(ref):
    return jnp.stack([ref[:, h * HEAD_DIM:(h + 1) * HEAD_DIM] for h in range(HEADS)])


def _gdn_core_fwd(q, k, v, ge, be):
    t = q.shape[0]
    c, hd = GDN_CHUNK, HEAD_DIM
    n = t // c

    def body(q_ref, k_ref, v_ref, g_ref, b_ref, o_ref, st_ref, ti_ref, s_ref):
        @pl.when(pl.program_id(0) == 0)
        def _():
            s_ref[...] = jnp.zeros_like(s_ref)

        s = s_ref[...]
        st_ref[:, 0] = s
        o, s_new, tinv = _gdn_chunk(s, *[_heads_major(r) for r in (q_ref, k_ref, v_ref, g_ref, b_ref)])
        ti_ref[0] = tinv
        for h in range(HEADS):
            o_ref[:, h * hd:(h + 1) * hd] = o[h]
        s_ref[...] = s_new

    tile = pl.BlockSpec((c, GDN_WIDTH), lambda i: (i, 0))
    return pl.pallas_call(
        body, name="gdn_core_fwd", grid=(n,), in_specs=[tile] * 5,
        out_specs=[tile, pl.BlockSpec((HEADS, 1, hd, hd), lambda i: (0, i, 0, 0)),
                   pl.BlockSpec((1, HEADS, c, c), lambda i: (i, 0, 0, 0))],
        out_shape=[_sds((t, GDN_WIDTH), f32), _sds((HEADS, n, hd, hd), f32), _sds((n, HEADS, c, c), f32)],
        scratch_shapes=[pltpu.VMEM((HEADS, hd, hd), f32)],
        compiler_params=_cparams(("arbitrary",)))(q, k, v, ge, be)


def _gdn_core_bwd(q, k, v, ge, be, states, do):
    t = q.shape[0]
    c, hd = GDN_CHUNK, HEAD_DIM
    n = t // c
    states, tinvs = states

    def body(q_ref, k_ref, v_ref, g_ref, b_ref, st_ref, ti_ref, do_ref, dq_ref, dk_ref, dv_ref, dg_ref, db_ref, ds_ref):
        @pl.when(pl.program_id(0) == 0)
        def _():
            ds_ref[...] = jnp.zeros_like(ds_ref)

        tinv = ti_ref[0]
        _, vjp = jax.vjp(lambda *a: _gdn_chunk(*a, tinv=tinv)[:2], st_ref[:, 0],
                         *[_heads_major(r) for r in (q_ref, k_ref, v_ref, g_ref, b_ref)])
        ds, *dins = vjp((_heads_major(do_ref), ds_ref[...]))
        ds_ref[...] = ds
        for d_ref, d in zip((dq_ref, dk_ref, dv_ref, dg_ref, db_ref), dins):
            for h in range(HEADS):
                d_ref[:, h * hd:(h + 1) * hd] = d[h]

    tile = pl.BlockSpec((c, GDN_WIDTH), lambda i: (n - 1 - i, 0))
    return pl.pallas_call(
        body, name="gdn_core_bwd", grid=(n,),
        in_specs=[tile] * 5 + [pl.BlockSpec((HEADS, 1, hd, hd), lambda i: (0, n - 1 - i, 0, 0)),
                               pl.BlockSpec((1, HEADS, c, c), lambda i: (n - 1 - i, 0, 0, 0)), tile],
        out_specs=[tile] * 5, out_shape=[_sds((t, GDN_WIDTH), f32)] * 5,
        scratch_shapes=[pltpu.VMEM((HEADS, hd, hd), f32)],
        compiler_params=_cparams(("arbitrary",)))(q, k, v, ge, be, states, tinvs, do)


def _post_fn(o, z, gain):
    return _rms(o, gain) * _silu(z)


def _post_fwd(o, p, z_off, gain, name):
    t = o.shape[0]
    hd = HEAD_DIM

    def body(o_ref, z_ref, g_ref, y_ref):
        y_ref[...] = _post_fn(o_ref[...], z_ref[...], g_ref[...]).astype(bf16)

    col = pl.BlockSpec((t, hd), lambda h: (0, h))
    return pl.pallas_call(
        body, name=name, grid=(HEADS,),
        in_specs=[col, pl.BlockSpec((t, hd), lambda h: (0, h + z_off // hd)), pl.BlockSpec((1, hd), lambda h: (0, 0))],
        out_specs=col, out_shape=_sds((t, HEADS * hd), bf16), compiler_params=_cparams(("parallel",)))(o, p, gain)


def _post_bwd(o, p, z_off, gain, dmix, mix_off, name):
    t = o.shape[0]
    hd = HEAD_DIM

    def body(o_ref, z_ref, g_ref, dy_ref, do_ref, dz_ref, dg_ref):
        _, vjp = jax.vjp(_post_fn, o_ref[...], z_ref[...], g_ref[...])
        do, dz, dg = vjp(dy_ref[...])
        do_ref[...] = do
        dz_ref[...] = dz.astype(bf16)

        @pl.when(pl.program_id(0) == 0)
        def _():
            dg_ref[...] = jnp.zeros_like(dg_ref)

        dg_ref[...] += dg

    col = pl.BlockSpec((t, hd), lambda h: (0, h))
    vec = pl.BlockSpec((1, hd), lambda h: (0, 0))
    return pl.pallas_call(
        body, name=name, grid=(HEADS,),
        in_specs=[col, pl.BlockSpec((t, hd), lambda h: (0, h + z_off // hd)), vec,
                  pl.BlockSpec((t, hd), lambda h: (0, h + mix_off // hd))],
        out_specs=[col, col, vec], out_shape=[_sds((t, HEADS * hd), f32), _sds((t, HEADS * hd), bf16), _sds((1, hd), f32)],
        compiler_params=_cparams(("arbitrary",)))(o, p, gain, dmix)


def _hgrn_pre_fn(qb, fb, lbw, layer):
    l0, l1 = lbw[0:1, :], lbw[1:2, :]
    m = jnp.maximum(l0, l1)
    e0, e1 = jnp.exp(l0 - m), jnp.exp(l1 - m)
    p0, p1 = e0 / (e0 + e1), e1 / (e0 + e1)
    lb = (p0 - p0) if layer == 0 else ((p0 + p1) - p0)
    f = lb + (1.0 - lb) * _sigmoid(fb)
    return _silu(qb), 1.0 - f, jnp.log(jnp.maximum(f, F_FLOOR))


def _hgrn_pre_fwd(p, lbw, layer):
    t = p.shape[0]
    tc = HEAD_DIM

    def body(qb_ref, fb_ref, lb_ref, q_ref, k_ref, lf_ref):
        q_ref[...], k_ref[...], lf_ref[...] = _hgrn_pre_fn(qb_ref[...], fb_ref[...], lb_ref[...], layer)

    col = pl.BlockSpec((t, tc), lambda j: (0, j))
    return pl.pallas_call(
        body, name="hgrn_pre_fwd", grid=(GDN_WIDTH // tc,),
        in_specs=[pl.BlockSpec((t, tc), lambda j: (0, j + OFF_QB // tc)), pl.BlockSpec((t, tc), lambda j: (0, j + OFF_FB // tc)),
                  pl.BlockSpec((2, tc), lambda j: (0, j))],
        out_specs=[col] * 3, out_shape=[_sds((t, GDN_WIDTH), f32)] * 3,
        compiler_params=_cparams(("parallel",)))(p, p, lbw)


def _hgrn_pre_bwd(p, lbw, layer, dq, dk, dlf):
    t = p.shape[0]
    tc = HEAD_DIM

    def body(qb_ref, fb_ref, lb_ref, dq_ref, dk_ref, dlf_ref, dqb_ref, dfb_ref, dlb_ref):
        _, vjp = jax.vjp(functools.partial(_hgrn_pre_fn, layer=layer), qb_ref[...], fb_ref[...], lb_ref[...])
        dqb, dfb, dlb = vjp((dq_ref[...], dk_ref[...], dlf_ref[...]))
        dqb_ref[...] = dqb.astype(bf16)
        dfb_ref[...] = dfb.astype(bf16)
        dlb_ref[...] = dlb

    col = pl.BlockSpec((t, tc), lambda j: (0, j))
    lb = pl.BlockSpec((2, tc), lambda j: (0, j))
    return pl.pallas_call(
        body, name="hgrn_pre_bwd", grid=(GDN_WIDTH // tc,),
        in_specs=[pl.BlockSpec((t, tc), lambda j: (0, j + OFF_QB // tc)), pl.BlockSpec((t, tc), lambda j: (0, j + OFF_FB // tc)),
                  lb, col, col, col],
        out_specs=[col, col, lb], out_shape=[_sds((t, GDN_WIDTH), bf16)] * 2 + [_sds((2, GDN_WIDTH), f32)],
        compiler_params=_cparams(("parallel",)))(p, p, lbw, dq, dk, dlf)


def _hgrn_step(st, q, k, lf, v):
    c = HGRN_CHUNK
    nh, rs = q.shape[0], q.shape[1]
    r2 = lax.broadcasted_iota(jnp.int32, (rs, rs), 0)
    c2 = lax.broadcasted_iota(jnp.int32, (rs, rs), 1)
    shift = c.bit_length() - 1
    same_chunk = jnp.right_shift(r2, shift) == jnp.right_shift(c2, shift)
    tri = jnp.broadcast_to(((r2 >= c2) & same_chunk).astype(f32), (nh, rs, rs))
    b_all = _hdot(tri, lf, precision=HI)
    i3 = lax.broadcasted_iota(jnp.int32, (c, c, HEAD_DIM), 0)
    j3 = lax.broadcasted_iota(jnp.int32, (c, c, HEAD_DIM), 1)
    mask = i3 >= j3
    outs = []
    for n in range(q.shape[1] // c):
        sl = slice(n * c, (n + 1) * c)
        qc, kc, vc, b = q[:, sl], k[:, sl], v[:, sl], b_all[:, sl]
        rel = jnp.where(mask, jnp.exp(jnp.where(mask, b[:, :, None, :] - b[:, None, :, :], 0.0)), 0.0)
        scores = jnp.sum(qc[:, :, None, :] * kc[:, None, :, :] * rel, axis=-1)
        bl = b[:, c - 1:c, :]
        o = _hbdot(scores, vc) + _hbdot(qc * jnp.exp(b), st, BNT)
        st = st * jnp.exp(bl) + _hbdot(vc, kc * jnp.exp(bl - b), BTN)
        outs.append(o)
    return jnp.concatenate(outs, axis=1), st


def _hgrn_core_fwd(q, k, lf, p):
    t = q.shape[0]
    hd = HEAD_DIM
    rs = min(HGRN_STEP, t)
    n = t // rs

    def body(q_ref, k_ref, lf_ref, v_ref, o_ref, st_ref, s_ref):
        @pl.when(pl.program_id(0) == 0)
        def _():
            s_ref[...] = jnp.zeros_like(s_ref)

        s = s_ref[...]
        st_ref[:, 0] = s
        o, s_new = _hgrn_step(s, *[_heads_major(r) for r in (q_ref, k_ref, lf_ref, v_ref)])
        for h in range(HEADS):
            o_ref[:, h * hd:(h + 1) * hd] = o[h]
        s_ref[...] = s_new

    tile = pl.BlockSpec((rs, GDN_WIDTH), lambda i: (i, 0))
    return pl.pallas_call(
        body, name="hgrn_core_fwd", grid=(n,),
        in_specs=[tile, tile, tile, pl.BlockSpec((rs, GDN_WIDTH), lambda i: (i, OFF_IB // GDN_WIDTH))],
        out_specs=[tile, pl.BlockSpec((HEADS, 1, hd, hd), lambda i: (0, i, 0, 0))],
        out_shape=[_sds((t, GDN_WIDTH), f32), _sds((HEADS, n, hd, hd), f32)],
        scratch_shapes=[pltpu.VMEM((HEADS, hd, hd), f32)],
        compiler_params=_cparams(("arbitrary",)))(q, k, lf, p)


def _hgrn_core_bwd(q, k, lf, p, states, do):
    t = q.shape[0]
    hd = HEAD_DIM
    rs = min(HGRN_STEP, t)
    n = t // rs

    def body(q_ref, k_ref, lf_ref, v_ref, st_ref, do_ref, dq_ref, dk_ref, dlf_ref, dv_ref, ds_ref):
        @pl.when(pl.program_id(0) == 0)
        def _():
            ds_ref[...] = jnp.zeros_like(ds_ref)

        _, vjp = jax.vjp(_hgrn_step, st_ref[:, 0], *[_heads_major(r) for r in (q_ref, k_ref, lf_ref, v_ref)])
        ds, *dins = vjp((_heads_major(do_ref), ds_ref[...]))
        ds_ref[...] = ds
        for d_ref, d in zip((dq_ref, dk_ref, dlf_ref, dv_ref), dins):
            for h in range(HEADS):
                d_ref[:, h * hd:(h + 1) * hd] = d[h].astype(d_ref.dtype)

    tile = pl.BlockSpec((rs, GDN_WIDTH), lambda i: (n - 1 - i, 0))
    return pl.pallas_call(
        body, name="hgrn_core_bwd", grid=(n,),
        in_specs=[tile, tile, tile, pl.BlockSpec((rs, GDN_WIDTH), lambda i: (n - 1 - i, OFF_IB // GDN_WIDTH)),
                  pl.BlockSpec((HEADS, 1, hd, hd), lambda i: (0, n - 1 - i, 0, 0)), tile],
        out_specs=[tile] * 4, out_shape=[_sds((t, GDN_WIDTH), f32)] * 3 + [_sds((t, GDN_WIDTH), bf16)],
        scratch_shapes=[pltpu.VMEM((HEADS, hd, hd), f32)],
        compiler_params=_cparams(("arbitrary",)))(q, k, lf, p, states, do)


def _row(v):
    return v.reshape(1, -1)


def _anchored(w, row, key):
    tok = w.get(key)
    return row if tok is None else row + tok[0, 0]


def _pad_lanes(v, n=HEAD_DIM):
    return jnp.pad(v.reshape(1, -1), ((0, 0), (0, n - v.shape[-1])))


def _ffn_fwd(x, w, l):
    h = _rms_fwd(x, _anchored(w, _row(w['norm_ffn'][l]), ('fwdf', l)), "ffn_norm")
    u = _mm(h, w['ffn_w_up'][l], tb=True, name="ffn_up")
    a = _ffn_act_fwd(u, w['ffn_conv_w'][l], _row(w['ffn_conv_b'][l]))
    y = _mm(a, w['ffn_w_down'][l], add=x, name="ffn_down")
    return y, (x, h, u)


def _ffn_bwd(saved, w, l, dy, dyb, grads):
    x, h, u = saved
    da = _mm(dyb, w['ffn_w_down'][l], tb=True, name="ffn_down_dx")
    a, dg, dv, dcw, dcb = _ffn_act_bwd(u, w['ffn_conv_w'][l], _anchored(w, _row(w['ffn_conv_b'][l]), ('bwd', l)), da)
    grads['ffn_w_down'][l] = _mm(a, dyb, ta=True, out_dtype=bf16, name="ffn_down_dw")
    du = jnp.concatenate([dg, dv], axis=1)
    grads['ffn_w_up'][l] = _mm(du, h, ta=True, out_dtype=bf16, name="ffn_up_dw")
    dh = _mm(du, w['ffn_w_up'][l], name="ffn_up_dx")
    dx, dxb, dgain = _rms_bwd(x, _row(w['norm_ffn'][l]), dh, dy, "ffn_norm_bwd")
    grads['ffn_conv_w'][l] = dcw
    grads['ffn_conv_b'][l] = dcb[0]
    grads['norm_ffn'][l] = dgain[0]
    return dx, dxb


def _odd_fwd(x, w, l, j):
    h = _rms_fwd(x, _anchored(w, _row(w['norm_mix'][l]), ('fwd', l)), "mix_norm")
    p = _mm(h, w['c_w_in'][j], name="lru_in")
    xc = _col_conv_fwd(p, LRU_WIDTH, w['c_conv_w'][j], _row(w['c_conv_b'][j]), LRU_CONV, 256, "lru_conv_fwd")
    out, hs, a = _lru_fwd(p, xc, w['c_gate_a_w'][j], _row(w['c_gate_a_b'][j]), w['c_gate_x_w'][j],
                          _row(w['c_gate_x_b'][j]), _row(w['c_lambda'][j]))
    y = _mm(out, w['c_w_out'][j], add=x, name="lru_out")
    return y, (x, h, p, xc, out, hs, a)


def _odd_bwd(saved, w, l, j, dy, dyb, grads):
    x, h, p, xc, out, hs, a = saved
    dout = _mm(dyb, w['c_w_out'][j], tb=True, name="lru_out_dx")
    grads['c_w_out'][j] = _mm(out, dyb, ta=True, out_dtype=bf16, name="lru_out_dw")
    dyb_, da, du = _lru_bwd_scan(p, a, hs, dout)
    dxc, dwa, dwx, dba, dbx, dlam = _lru_bwd_gates(xc, da, du, w['c_gate_a_w'][j], _row(w['c_gate_a_b'][j]),
                                                   w['c_gate_x_w'][j], _row(w['c_gate_x_b'][j]), _row(w['c_lambda'][j]))
    dxb_, dcw, dcb = _col_conv_bwd(p, LRU_WIDTH, w['c_conv_w'][j], dxc, LRU_CONV, 256, "lru_conv_bwd")
    dp = jnp.concatenate([dyb_, dxb_], axis=1)
    grads['c_w_in'][j] = _mm(h, dp, ta=True, out_dtype=bf16, name="lru_in_dw")
    dh = _mm(dp, w['c_w_in'][j], tb=True, name="lru_in_dx")
    dx, dxb, dgain = _rms_bwd(x, _row(w['norm_mix'][l]), dh, dy, "mix_norm_bwd")
    grads['c_gate_a_w'][j], grads['c_gate_x_w'][j] = dwa, dwx
    grads['c_gate_a_b'][j], grads['c_gate_x_b'][j], grads['c_lambda'][j] = dba[0], dbx[0], dlam[0]
    grads['c_conv_w'][j], grads['c_conv_b'][j] = dcw, dcb[0]
    grads['norm_mix'][l] = dgain[0]
    return dx, dxb


def _even_fwd(x, w, l, j):
    h = _rms_fwd(x, _anchored(w, _row(w['norm_mix'][l]), ('fwd', l)), "mix_norm")
    p = _mm(h, w['ab_w_in'][j], name="ab_in")
    alog, dtb = _pad_lanes(w['gdn_a_log'][j]), _pad_lanes(w['gdn_dt_bias'][j])
    q, k, v, be, ge = _gdn_pre_fwd(p, w['gdn_conv_w'][j], alog, dtb)
    oa, *sa = _gdn_core_fwd(q, k, v, ge, be)
    ya = _post_fwd(oa, p, OFF_Z, _row(w['gdn_norm'][j]), "gdn_post_fwd")
    qq, kk, lf = _hgrn_pre_fwd(p, w['hgrn_lower_bounds'], j)
    ob, sb = _hgrn_core_fwd(qq, kk, lf, p)
    yb = _post_fwd(ob, p, OFF_GB, _row(w['hgrn_norm'][j]), "hgrn_post_fwd")
    mix = jnp.concatenate([ya, yb], axis=1)
    y = _mm(mix, w['ab_w_out'][j], add=x, name="ab_out")
    return y, (x, h, p, q, k, v, be, ge, oa, sa, qq, kk, lf, ob, sb, mix)


def _even_bwd(saved, w, l, j, dy, dyb, grads):
    x, h, p, q, k, v, be, ge, oa, sa, qq, kk, lf, ob, sb, mix = saved
    alog, dtb = _pad_lanes(w['gdn_a_log'][j]), _pad_lanes(w['gdn_dt_bias'][j])
    dmix = _mm(dyb, w['ab_w_out'][j], tb=True, name="ab_out_dx")
    grads['ab_w_out'][j] = _mm(mix, dyb, ta=True, out_dtype=bf16, name="ab_out_dw")
    doa, dz, dgn = _post_bwd(oa, p, OFF_Z, _anchored(w, _row(w['gdn_norm'][j]), ('bwdm', l)), dmix, 0, "gdn_post_bwd")
    dob, dgb, dhn = _post_bwd(ob, p, OFF_GB, _row(w['hgrn_norm'][j]), dmix, GDN_WIDTH, "hgrn_post_bwd")
    dq, dk, dv, dge, dbe = _gdn_core_bwd(q, k, v, ge, be, sa, doa)
    dpq, dpk, dpv, dba, dwq, dwk, dwv, dal, ddt = _gdn_pre_bwd(p, w['gdn_conv_w'][j], alog, dtb, dq, dk, dv, dbe, dge)
    dqq, dkk, dlf, dib = _hgrn_core_bwd(qq, kk, lf, p, sb, dob)
    dqb, dfb, dlb = _hgrn_pre_bwd(p, w['hgrn_lower_bounds'], j, dqq, dkk, dlf)
    dp = jnp.concatenate([dpq, dpk, dpv, dz, dqb, dfb, dib, dgb, dba.astype(bf16)], axis=1)
    grads['ab_w_in'][j] = _mm(h, dp, ta=True, out_dtype=bf16, name="ab_in_dw")
    dh = _mm(dp, w['ab_w_in'][j], tb=True, name="ab_in_dx")
    dx, dxb, dgain = _rms_bwd(x, _row(w['norm_mix'][l]), dh, dy, "mix_norm_bwd")
    grads['gdn_conv_w'][j] = jnp.concatenate([dwq, dwk, dwv], axis=1)
    grads['gdn_a_log'][j], grads['gdn_dt_bias'][j] = dal[0, :HEADS], ddt[0, :HEADS]
    grads['gdn_norm'][j], grads['hgrn_norm'][j] = dgn[0], dhn[0]
    grads['hgrn_lower_bounds'].append(dlb)
    grads['norm_mix'][l] = dgain[0]
    return dx, dxb


def _ab_permute(w_in):
    pad = jnp.zeros(w_in.shape[:-1] + (AB_PAD - AB_COLS,), w_in.dtype)
    return jnp.concatenate([w_in[..., :2048], w_in[..., 2056:], w_in[..., 2048:2056], pad], axis=-1)


def _ab_unpermute(g):
    return jnp.concatenate([g[..., :2048], g[..., 4096:4104], g[..., 2048:4096]], axis=-1)


def _block_pad(a, axis, nblk, padded):
    axis = axis % a.ndim
    s = a.shape
    a = a.reshape(s[:axis] + (nblk, s[axis] // nblk) + s[axis + 1:])
    pad = [(0, 0)] * a.ndim
    pad[axis + 1] = (0, padded - s[axis] // nblk)
    return jnp.pad(a, pad).reshape(s[:axis] + (nblk * padded,) + s[axis + 1:])


def _block_unpad(a, axis, nblk, width):
    axis = axis % a.ndim
    s = a.shape
    a = a.reshape(s[:axis] + (nblk, s[axis] // nblk) + s[axis + 1:])
    a = lax.slice_in_dim(a, 0, width, axis=axis + 1)
    return a.reshape(s[:axis] + (nblk * width,) + s[axis + 1:])


def _kernel_layout(w):
    w = dict(w)
    w['ab_w_in'] = _ab_permute(w['ab_w_in'])
    w['ffn_w_up'] = jnp.swapaxes(_block_pad(w['ffn_w_up'], 2, N_DEV, FF_PAD), 1, 2)
    w['ffn_w_down'] = _block_pad(w['ffn_w_down'], 1, 4, FF_PAD)
    w['ffn_conv_w'] = _block_pad(w['ffn_conv_w'], 2, 4, FF_PAD)
    w['ffn_conv_b'] = _block_pad(w['ffn_conv_b'], 1, 4, FF_PAD)
    return w


def _natural_grads(g):
    g = dict(g)
    g['ab_w_in'] = _ab_unpermute(g['ab_w_in'])
    g['ffn_w_up'] = _block_unpad(jnp.swapaxes(g['ffn_w_up'], 1, 2), 2, N_DEV, FF_SHARD)
    g['ffn_w_down'] = _block_unpad(g['ffn_w_down'], 1, 4, FF_SHARD)
    g['ffn_conv_w'] = _block_unpad(g['ffn_conv_w'], 2, 4, FF_SHARD)
    g['ffn_conv_b'] = _block_unpad(g['ffn_conv_b'], 1, 4, FF_SHARD)
    return g


def _local_step(x, target, w, fetch=None, push=None):
    grads = {n: [None] * (DEPTH if n in ('norm_mix', 'norm_ffn') or n.startswith('ffn_') else 2)
             for n in WEIGHTS if n not in ('norm_final', 'hgrn_lower_bounds')}
    grads['hgrn_lower_bounds'] = []
    saved = []
    for l in range(DEPTH):
        j = l // 2
        if fetch is not None:
            fetch(l, x, 'mix')
        x, s_mix = (_even_fwd if l % 2 == 0 else _odd_fwd)(x, w, l, j)
        if fetch is not None:
            fetch(l, x, 'ffn')
        x, s_ffn = _ffn_fwd(x, w, l)
        saved.append((s_mix, s_ffn))
    loss, dx, dxb, dgf = _loss_head(x, _row(w['norm_final']), target)
    for l in reversed(range(DEPTH)):
        j = l // 2
        s_mix, s_ffn = saved[l]
        dx, dxb = _ffn_bwd(s_ffn, w, l, dx, dxb, grads)
        if push is not None:
            push(l, 'ffn', {nm: grads[nm].pop(li) for nm, li in _layer_items(l)[-2:]})
        dx, dxb = (_even_bwd if l % 2 == 0 else _odd_bwd)(s_mix, w, l, j, dx, dxb, grads)
        if push is not None:
            push(l, 'mix', {nm: grads[nm].pop(li) for nm, li in _layer_items(l)[:-2]})
    out = {n: jnp.stack(g) for n, g in grads.items() if n != 'hgrn_lower_bounds' and g}
    out['hgrn_lower_bounds'] = grads['hgrn_lower_bounds'][0] + grads['hgrn_lower_bounds'][1]
    out['norm_final'] = dgf[0]
    return loss[0, 0], dx, out


def _position():
    return lax.axis_index("x"), lax.axis_index("y"), lax.axis_index("c")


BLOCK_LAYOUT = {
    'ab_w_in': ((2, D_MODEL, N_DEV * AB_SHARD_PAD), (2, D_MODEL, AB_SHARD_PAD)),
    'ab_w_out': ((2, N_DEV, 128, D_MODEL), (2, 128, D_MODEL)),
    'c_w_in': ((2, D_MODEL, 2 * LRU_WIDTH), (2, D_MODEL, 256)),
    'c_w_out': ((2, N_DEV, 128, D_MODEL), (2, 128, D_MODEL)),
    'c_gate_a_w': ((2, HEADS, N_DEV, 32, LRU_BLOCK), (2, HEADS, 32, LRU_BLOCK)),
    'c_gate_x_w': ((2, HEADS, N_DEV, 32, LRU_BLOCK), (2, HEADS, 32, LRU_BLOCK)),
    'ffn_w_up': ((DEPTH, N_DEV, FF_PAD, D_MODEL), (DEPTH, FF_PAD, D_MODEL)),
    'ffn_w_down': ((DEPTH, 4, FF_PAD, D_MODEL), (DEPTH, FF_ROWS, D_MODEL)),
}


COL_WINDOW = {'ab_w_in': AB_SHARD_PAD, 'c_w_in': 256}


def _block_index(name, p):
    d = 4 * p[0] + 2 * p[1] + p[2]
    if name in COL_WINDOW:
        return (slice(None), pl.ds(pl.multiple_of(d * COL_WINDOW[name], 128), COL_WINDOW[name]))
    if name == 'ffn_w_down':
        return (2 * p[0] + p[1], pl.ds(pl.multiple_of(p[2] * FF_ROWS, 16), FF_ROWS), slice(None))
    if name in ('c_gate_a_w', 'c_gate_x_w'):
        return (slice(None), d)
    return (d,)


def _block_of(name, ref, p, layered=True):
    idx = _block_index(name, p)
    if layered and name in BLOCK_LAYOUT:
        idx = (slice(None),) + idx
    return ref.at[idx]


def _layer_items(l):
    j = l // 2
    mix = ([('ab_w_in', j), ('ab_w_out', j)] if l % 2 == 0 else
           [('c_w_in', j), ('c_w_out', j), ('c_gate_a_w', j), ('c_gate_x_w', j)])
    return mix + [('ffn_w_up', l), ('ffn_w_down', l)]


def _place_own(items, shards, posv, name):
    n = len(items)
    down = [i for i, (nm, _) in enumerate(items) if nm == 'ffn_w_down']
    in_specs, out_specs, out_shapes, operands = [], [], [], []
    for nm, li in items:
        sh = shards[nm]
        shard_shape = sh.shape if li is None else sh.shape[1:]
        z = (0,) * len(shard_shape)
        operands.append(sh)
        in_specs.append(pl.BlockSpec(shard_shape, lambda i, d, q, c, z=z: z) if li is None else
                        pl.BlockSpec((1,) + shard_shape, lambda i, d, q, c, li=li, z=z: (li,) + z))
        out_shapes.append(_sds(BLOCK_LAYOUT[nm][0][1:] if nm in BLOCK_LAYOUT else (N_DEV,) + sh.shape, sh.dtype))
        if nm in COL_WINDOW:
            out_specs.append(pl.BlockSpec(shard_shape, lambda i, d, q, c: (0, d[0])))
        elif nm == 'ffn_w_down':
            out_specs.append(pl.BlockSpec((1,) + shard_shape, lambda i, d, q, c: (q[0], c[0], 0)))
        elif nm in ('c_gate_a_w', 'c_gate_x_w'):
            out_specs.append(pl.BlockSpec((HEADS, 1) + shard_shape[1:], lambda i, d, q, c: (0, d[0], 0, 0)))
        else:
            out_specs.append(pl.BlockSpec((1,) + shard_shape, lambda i, d, q, c, z=z: (d[0],) + z))

    def body(d_ref, q_ref, c_ref, *refs):
        for i, (nm, li) in enumerate(items):
            v = refs[i][...] if li is None else refs[i][0]
            o_ref = refs[n + len(down) + i]
            if nm in COL_WINDOW:
                o_ref[...] = v
            elif nm in ('c_gate_a_w', 'c_gate_x_w'):
                o_ref[:, 0] = v
            else:
                o_ref[0] = v

    zeros = [jnp.zeros(out_shapes[i].shape, out_shapes[i].dtype) for i in down]
    return pl.pallas_call(
        body, name=name, out_shape=out_shapes,
        grid_spec=pltpu.PrefetchScalarGridSpec(
            num_scalar_prefetch=3, grid=(1,), in_specs=in_specs + [pl.BlockSpec(memory_space=pl.ANY)] * len(down),
            out_specs=out_specs),
        input_output_aliases={3 + n + k: i for k, i in enumerate(down)},
        compiler_params=_cparams(("arbitrary",)))(*posv, *operands, *zeros)


def _src_of(shard_ref, li):
    return shard_ref if li is None else shard_ref.at[li]


def _gather_now(items, shards, lands):
    n = len(items)
    srcs = sorted({nm for nm, _ in items})

    def body(*refs):
        ins = dict(zip(srcs, refs[:len(srcs)]))
        outs = refs[len(srcs) + n:len(srcs) + 2 * n]
        send_sems, recv_sems = refs[len(srcs) + 2 * n:]
        x, y, c = _position()
        me, sibling = (x, y, c), (x, y, 1 - c)
        chips = [(1 - x, y), (x, 1 - y), (1 - x, 1 - y)]

        def copy(i, k, block, to, own=False):
            nm, li = items[i]
            dst = _block_of(nm, outs[i], block, layered=False)
            return pltpu.make_async_remote_copy(
                src_ref=_src_of(ins[nm], li) if own else dst, dst_ref=dst, send_sem=send_sems.at[7 * i + k],
                recv_sem=recv_sems.at[7 * i + k], device_id=to, device_id_type=MESH)

        first = []
        for i in range(n):
            first.append(copy(i, 0, me, sibling, own=True))
            first += [copy(i, 1 + j, me, (*chip, c), own=True) for j, chip in enumerate(chips)]
        for cp in first:
            cp.start()
        passed = []
        for j, chip in enumerate(chips):
            for i in range(n):
                copy(i, 1 + j, (*chip, c), me).wait_recv()
                fwd = copy(i, 4 + j, (*chip, c), sibling)
                fwd.start()
                passed.append(fwd)
        for i in range(n):
            copy(i, 0, sibling, me).wait_recv()
        for j, chip in enumerate(chips):
            for i in range(n):
                copy(i, 4 + j, (*chip, 1 - c), me).wait_recv()
        for cp in first + passed:
            cp.wait_send()

    any_spec = pl.BlockSpec(memory_space=pl.ANY)
    return pl.pallas_call(
        body, name="gather_first_layer", out_shape=[_sds(a.shape, a.dtype) for a in lands],
        in_specs=[any_spec] * (len(srcs) + n), out_specs=[any_spec] * n,
        input_output_aliases={len(srcs) + i: i for i in range(n)},
        scratch_shapes=[pltpu.SemaphoreType.DMA((7 * n,)), pltpu.SemaphoreType.DMA((7 * n,))],
    )(*[shards[nm] for nm in srcs], *lands)


FIRST_HOP = (1, 2, 4, 6)


def _lanes(name, land_ref, pos):
    if name == 'ffn_w_down':
        return [(FIRST_HOP, land_ref.at[pl.ds(0, 2), pl.ds(0, 2 * FF_ROWS)])]
    if name in COL_WINDOW:
        return [(FIRST_HOP, land_ref.at[:, pl.ds(0, 4 * COL_WINDOW[name])])]
    if name in ('c_gate_a_w', 'c_gate_x_w'):
        return [(FIRST_HOP, land_ref.at[:, pl.ds(0, 4)])]
    return [(FIRST_HOP, land_ref.at[pl.ds(0, 4)])]


def _n_lanes(items):
    return len(items)


def _gather_forward(items, lands, name):
    n = len(items)

    def body(*refs):
        outs = refs[n:2 * n]
        send_sems, recv_sems = refs[2 * n:]
        x, y, c = _position()
        chips = [(1 - x, y), (x, 1 - y), (1 - x, 1 - y)]
        copies, arrivals = [], []
        for i, (nm, _) in enumerate(items):
            for j, chip in enumerate(chips):
                mine = _block_of(nm, outs[i], (*chip, c), layered=False)
                theirs = _block_of(nm, outs[i], (*chip, 1 - c), layered=False)
                copies.append(pltpu.make_async_remote_copy(
                    src_ref=mine, dst_ref=mine, send_sem=send_sems.at[3 * i + j], recv_sem=recv_sems.at[3 * i + j],
                    device_id=(x, y, 1 - c), device_id_type=MESH))
                arrivals.append(pltpu.make_async_remote_copy(
                    src_ref=theirs, dst_ref=theirs, send_sem=send_sems.at[3 * i + j], recv_sem=recv_sems.at[3 * i + j],
                    device_id=(x, y, 1 - c), device_id_type=MESH))
        for cp in copies:
            cp.start()
        for cp in arrivals:
            cp.wait_recv()
        for cp in copies:
            cp.wait_send()

    any_spec = pl.BlockSpec(memory_space=pl.ANY)
    return pl.pallas_call(
        body, name=name, out_shape=[_sds(a.shape, a.dtype) for a in lands],
        in_specs=[any_spec] * n, out_specs=[any_spec] * n, input_output_aliases={i: i for i in range(n)},
        scratch_shapes=[pltpu.SemaphoreType.DMA((3 * n,)), pltpu.SemaphoreType.DMA((3 * n,))],
    )(*lands)


HBM_SPEC = pl.BlockSpec(memory_space=pltpu.HBM)
SEM_SPEC = pl.BlockSpec(memory_space=pltpu.SEMAPHORE)
SIDE_EFFECT = pltpu.SideEffectType.DATAFLOW_SIDE_EFFECTING


def _gather_start(items, shards, lands, token, name):
    n = len(items)
    srcs = sorted({nm for nm, _ in items})
    ns, nl = len(srcs), _n_lanes(items)

    def body(*refs):
        ins = dict(zip(srcs, refs[:ns]))
        land_refs = refs[ns:ns + n]
        sems = refs[ns + n + 1:ns + n + 1 + 2 * nl]
        x, y, c = _position()
        me = (x, y, c)
        lane = 0
        for i, (nm, li) in enumerate(items):
            for codes, _ in _lanes(nm, land_refs[i], me):
                for k in codes:
                    peer = (1 - x if (k >> 2) & 1 else x, 1 - y if (k >> 1) & 1 else y, 1 - c if k & 1 else c)
                    pltpu.make_async_remote_copy(
                        src_ref=_src_of(ins[nm], li), dst_ref=_block_of(nm, land_refs[i], me, layered=False),
                        send_sem=sems[2 * lane], recv_sem=sems[2 * lane + 1], device_id=peer, device_id_type=MESH).start()
                lane += 1
        refs[-1][...] = jnp.zeros((8, 128), f32)

    hbm = [pltpu.with_memory_space_constraint(a, pltpu.HBM) for a in [shards[nm] for nm in srcs] + list(lands)]
    outs = pl.pallas_call(
        body, name=name,
        out_shape=[pltpu.SemaphoreType.DMA(())] * (2 * nl) + [pltpu.HBM(a.shape, a.dtype) for a in hbm] + [_sds((8, 128), f32)],
        in_specs=[HBM_SPEC] * (ns + n) + [pl.BlockSpec(memory_space=pl.ANY)],
        out_specs=[SEM_SPEC] * (2 * nl) + [HBM_SPEC] * (ns + n) + [pl.BlockSpec(memory_space=pltpu.VMEM)],
        input_output_aliases={i: 2 * nl + i for i in range(ns + n)},
        compiler_params=pltpu.CompilerParams(has_side_effects=SIDE_EFFECT),
    )(*hbm, token)
    return outs[:2 * nl], dict(zip(srcs, outs[2 * nl:2 * nl + ns])), outs[2 * nl + ns:-1], outs[-1]


def _gather_wait(items, sems, shards, lands, after, name):
    n = len(items)
    srcs = sorted(shards)
    ns, nl = len(srcs), _n_lanes(items)

    def body(*refs):
        land_refs = refs[ns:ns + n]
        sem_refs = refs[ns + n:ns + n + 2 * nl]
        x, y, c = _position()
        lane = 0
        for i, (nm, _) in enumerate(items):
            for _, moved in _lanes(nm, land_refs[i], (x, y, c)):
                cp = pltpu.make_async_remote_copy(
                    src_ref=moved, dst_ref=moved, send_sem=sem_refs[2 * lane], recv_sem=sem_refs[2 * lane + 1],
                    device_id=(x, y, 1 - c), device_id_type=MESH)
                cp.wait_send()
                cp.wait_recv()
                lane += 1

    outs = pl.pallas_call(
        body, name=name, out_shape=[pltpu.HBM(shards[nm].shape, shards[nm].dtype) for nm in srcs]
        + [pltpu.HBM(a.shape, a.dtype) for a in lands],
        in_specs=[HBM_SPEC] * (ns + n) + [SEM_SPEC] * (2 * nl) + [pl.BlockSpec(memory_space=pl.ANY)],
        out_specs=[HBM_SPEC] * (ns + n), input_output_aliases={i: i for i in range(ns + n)},
        compiler_params=pltpu.CompilerParams(has_side_effects=SIDE_EFFECT),
    )(*[shards[nm] for nm in srcs], *lands, *sems, after)
    return dict(zip(srcs, outs[:ns])), outs[ns:]


def _exchange_grads(fulls, rep):
    cpos = lax.axis_index("c").astype(jnp.int32).reshape(1)
    pair, rep_pair = _pair_exchange(fulls, rep)
    chip = {nm: _chip_sum(nm, fulls[nm], pair[nm], cpos) for nm in fulls}
    rep_chip = _add_pair(rep, rep_pair, "chip_sum_replicated")
    cross, cross_rep = _cross_exchange(chip, rep_chip)
    return chip, rep_chip, cross, cross_rep


def _pair_exchange(fulls, rep, tag=""):
    names = list(fulls)
    n = len(names)
    shard_shape = {nm: ((fulls[nm].shape[0],) + BLOCK_LAYOUT[nm][1][1:] if nm in BLOCK_LAYOUT else fulls[nm].shape[1:])
                   for nm in names}

    def body(*refs):
        ins = dict(zip(names, refs[:n]))
        rep_ref = refs[n]
        pair = dict(zip(names, refs[n + 1:2 * n + 1]))
        rpair_ref = refs[2 * n + 1]
        send_sems, recv_sems = refs[2 * n + 2:]
        x, y, c = _position()
        sibling = (x, y, 1 - c)
        remote = []
        for i, nm in enumerate(names):
            for q in range(4):
                remote.append(pltpu.make_async_remote_copy(
                    src_ref=_block_of(nm, ins[nm], (q >> 1, q & 1, 1 - c)), dst_ref=pair[nm].at[q],
                    send_sem=send_sems.at[4 * i + q], recv_sem=recv_sems.at[4 * i + q], device_id=sibling,
                    device_id_type=MESH))
        remote.append(pltpu.make_async_remote_copy(
            src_ref=rep_ref, dst_ref=rpair_ref, send_sem=send_sems.at[4 * n], recv_sem=recv_sems.at[4 * n],
            device_id=sibling, device_id_type=MESH))
        for cp in remote:
            cp.start()
        for cp in remote:
            cp.wait_recv()
        for cp in remote:
            cp.wait_send()

    any_spec = pl.BlockSpec(memory_space=pl.ANY)
    four = [_sds((4,) + tuple(shard_shape[nm]), fulls[nm].dtype) for nm in names]
    outs = pl.pallas_call(
        body, name="grad_pair_exchange" + tag, out_shape=four + [_sds(rep.shape, rep.dtype)],
        in_specs=[any_spec] * (n + 1), out_specs=[any_spec] * (n + 1),
        scratch_shapes=[pltpu.SemaphoreType.DMA((4 * n + 1,)), pltpu.SemaphoreType.DMA((4 * n + 1,))],
    )(*[fulls[nm] for nm in names], rep)
    return dict(zip(names, outs[:n])), outs[n]


def _chip_sum(name, full, pair, cpos, tag=""):
    if name in COL_WINDOW:
        width = BLOCK_LAYOUT[name][1][-1]
        rows = full.shape[0] * full.shape[1]
        tr = 512

        def body(c_ref, f_ref, p_ref, o_ref):
            o_ref[0] = (f_ref[...].astype(f32) + p_ref[0].astype(f32)).astype(o_ref.dtype)

        slot = pl.BlockSpec((1, tr, width), lambda q, i, c: (q, i, 0))
        out = pl.pallas_call(
            body, name="chip_sum_" + name + tag, out_shape=_sds((4, rows, width), full.dtype),
            grid_spec=pltpu.PrefetchScalarGridSpec(
                num_scalar_prefetch=1, grid=(4, rows // tr),
                in_specs=[pl.BlockSpec((tr, width), lambda q, i, c: (i, 2 * q + c[0])), slot], out_specs=slot),
            compiler_params=_cparams(("parallel", "parallel")))(
            cpos, full.reshape(rows, N_DEV * width), pair.reshape(4, rows, width))
        return out.reshape(pair.shape)

    if name == 'ffn_w_down':
        f4, p4 = full, pair
        fspec = pl.BlockSpec((full.shape[0], 1, FF_ROWS, D_MODEL), lambda q, c: (0, q, c[0], 0))
    else:
        shard = pair.shape[1:]
        lead = int(np.prod(shard[:-2]))
        f4 = full.reshape((lead, N_DEV) + shard[-2:])
        p4 = pair.reshape((4, lead) + shard[-2:])
        fspec = pl.BlockSpec((lead, 1) + shard[-2:], lambda q, c: (0, 2 * q + c[0], 0, 0))

    def body4(c_ref, f_ref, p_ref, o_ref):
        o_ref[0] = (f_ref[:, 0].astype(f32) + p_ref[0].astype(f32)).astype(o_ref.dtype)

    slot = pl.BlockSpec((1,) + p4.shape[1:], lambda q, c: (q, 0, 0, 0))
    out = pl.pallas_call(
        body4, name="chip_sum_" + name + tag, out_shape=_sds(p4.shape, full.dtype),
        grid_spec=pltpu.PrefetchScalarGridSpec(num_scalar_prefetch=1, grid=(4,), in_specs=[fspec, slot], out_specs=slot),
        compiler_params=_cparams(("parallel",)))(cpos, f4, p4)
    return out.reshape(pair.shape)


def _add_pair(a, b, name):
    shp = a.shape
    r, c = int(np.prod(shp[:-1])), shp[-1]
    tr = _tile(r, (512, 256, 128, 64, 32, 16, 8))

    def body(a_ref, b_ref, o_ref):
        o_ref[...] = (a_ref[...].astype(f32) + b_ref[...].astype(f32)).astype(o_ref.dtype)

    tile = pl.BlockSpec((tr, c), lambda i: (i, 0))
    return pl.pallas_call(body, name=name, grid=(r // tr,), in_specs=[tile, tile], out_specs=tile,
                          out_shape=_sds((r, c), a.dtype), compiler_params=_cparams(("parallel",)))(
        a.reshape(r, c), b.reshape(r, c)).reshape(shp)


def _cross_exchange(chip, rep_chip):
    names = list(chip)
    n = len(names)

    def body(*refs):
        ins = dict(zip(names, refs[:n]))
        rep_ref = refs[n]
        outs = dict(zip(names, refs[2 * n + 2:3 * n + 2]))
        rrep_ref = refs[3 * n + 2]
        send_sems, recv_sems = refs[3 * n + 3:]
        x, y, c = _position()
        mine = 2 * x + y
        copies = []
        for k in range(1, 4):
            px, py = (1 - x if (k >> 1) & 1 else x), (1 - y if k & 1 else y)
            for i, nm in enumerate(names + ['']):
                src = rep_ref if i == n else ins[nm].at[2 * px + py]
                dst = (rrep_ref if i == n else outs[nm]).at[mine]
                copies.append(pltpu.make_async_remote_copy(
                    src_ref=src, dst_ref=dst, send_sem=send_sems.at[3 * i + k - 1], recv_sem=recv_sems.at[3 * i + k - 1],
                    device_id=(px, py, c), device_id_type=MESH))
        for cp in copies:
            cp.start()
        for cp in copies:
            cp.wait_recv()
        for cp in copies:
            cp.wait_send()

    any_spec = pl.BlockSpec(memory_space=pl.ANY)
    shapes = [_sds(chip[nm].shape, chip[nm].dtype) for nm in names] + [_sds((4,) + rep_chip.shape, rep_chip.dtype)]
    zeros = [jnp.zeros(s.shape, s.dtype) for s in shapes]
    outs = pl.pallas_call(
        body, name="grad_cross_exchange", out_shape=shapes,
        in_specs=[any_spec] * (2 * n + 2), out_specs=[any_spec] * (n + 1),
        input_output_aliases={n + 1 + i: i for i in range(n + 1)},
        scratch_shapes=[pltpu.SemaphoreType.DMA((3 * (n + 1),)), pltpu.SemaphoreType.DMA((3 * (n + 1),))],
    )(*[chip[nm] for nm in names], rep_chip, *zeros)
    return dict(zip(names, outs[:n])), outs[n]


def _cross_start(chips, name):
    n = len(chips)

    def body(*refs):
        chip_refs, land_refs = refs[:n], refs[n:2 * n]
        sems = refs[2 * n:4 * n]
        x, y, c = _position()
        mine = 2 * x + y
        for i in range(n):
            for k in range(1, 4):
                px, py = (1 - x if (k >> 1) & 1 else x), (1 - y if k & 1 else y)
                pltpu.make_async_remote_copy(
                    src_ref=chip_refs[i].at[2 * px + py], dst_ref=land_refs[i].at[mine], send_sem=sems[2 * i],
                    recv_sem=sems[2 * i + 1], device_id=(px, py, c), device_id_type=MESH).start()
        refs[-1][...] = jnp.zeros((8, 128), f32)

    hbm = [pltpu.with_memory_space_constraint(a, pltpu.HBM) for a in list(chips) + [jnp.zeros(a.shape, a.dtype) for a in chips]]
    outs = pl.pallas_call(
        body, name=name,
        out_shape=[pltpu.SemaphoreType.DMA(())] * (2 * n) + [pltpu.HBM(a.shape, a.dtype) for a in hbm] + [_sds((8, 128), f32)],
        in_specs=[HBM_SPEC] * (2 * n),
        out_specs=[SEM_SPEC] * (2 * n) + [HBM_SPEC] * (2 * n) + [pl.BlockSpec(memory_space=pltpu.VMEM)],
        input_output_aliases={i: 2 * n + i for i in range(2 * n)},
        compiler_params=pltpu.CompilerParams(has_side_effects=SIDE_EFFECT),
    )(*hbm)
    return outs[:2 * n], outs[2 * n:3 * n], outs[3 * n:4 * n], outs[4 * n]


def _cross_wait(sems, chips, lands, after, name):
    n = len(chips)

    def body(*refs):
        land_refs = refs[n:2 * n]
        sem_refs = refs[2 * n:4 * n]
        x, y, c = _position()
        for i in range(n):
            moved = land_refs[i].at[pl.ds(0, 3)]
            cp = pltpu.make_async_remote_copy(
                src_ref=moved, dst_ref=moved, send_sem=sem_refs[2 * i], recv_sem=sem_refs[2 * i + 1],
                device_id=(x, y, 1 - c), device_id_type=MESH)
            cp.wait_send()
            cp.wait_recv()

    outs = pl.pallas_call(
        body, name=name, out_shape=[pltpu.HBM(a.shape, a.dtype) for a in list(chips) + list(lands)],
        in_specs=[HBM_SPEC] * (2 * n) + [SEM_SPEC] * (2 * n) + [pl.BlockSpec(memory_space=pl.ANY)],
        out_specs=[HBM_SPEC] * (2 * n), input_output_aliases={i: i for i in range(2 * n)},
        compiler_params=pltpu.CompilerParams(has_side_effects=SIDE_EFFECT),
    )(*chips, *lands, *sems, after)
    return outs[:n], outs[n:]


def _sum_adamw_layer(parts, own, mine, w, m, v, li, prev, name):
    nl, r, l = w.shape
    lp = parts.shape[2]
    tr = r if r <= 512 else _tile(r, (512, FF_ROWS, 256, 128))
    c1 = 1.0 / (1.0 - ADAM_B1 ** ADAM_STEP)
    c2 = 1.0 / (1.0 - ADAM_B2 ** ADAM_STEP)
    k = 0 if prev is None else 4

    def body(mine_ref, p_ref, o_ref, w_ref, m_ref, v_ref, *rest):
        g_ref, d_ref, nm_ref, nv_ref = rest[k:]
        mine_v = o_ref[0].astype(f32)
        g = jnp.where(mine_ref[0] == 0, mine_v, p_ref[0].astype(f32))
        for s in range(1, 4):
            g = g + jnp.where(mine_ref[0] == s, mine_v, p_ref[s].astype(f32))
        if lp != l:
            g = g[:, :l]
        m_new = ADAM_B1 * m_ref[0] + (1.0 - ADAM_B1) * g
        v_new = ADAM_B2 * v_ref[0] + (1.0 - ADAM_B2) * (g * g)
        g_ref[0] = g
        nm_ref[0] = m_new
        nv_ref[0] = v_new
        d_ref[0] = -ADAM_LR * ((m_new * c1) / (jnp.sqrt(v_new * c2) + ADAM_EPS) + ADAM_WD * w_ref[0])

    tile = pl.BlockSpec((1, tr, l), lambda i, mn: (li, i, 0))
    keep = [pl.BlockSpec(memory_space=pl.ANY)] * k
    return pl.pallas_call(
        body, name=name, out_shape=[_sds((nl, r, l), f32)] * 4,
        grid_spec=pltpu.PrefetchScalarGridSpec(
            num_scalar_prefetch=1, grid=(r // tr,),
            in_specs=[pl.BlockSpec((4, tr, lp), lambda i, mn: (0, i, 0)), pl.BlockSpec((1, tr, lp), lambda i, mn: (mn[0], i, 0)),
                      tile, tile, tile] + keep,
            out_specs=[tile] * 4),
        input_output_aliases={6 + i: i for i in range(k)},
        compiler_params=_cparams(("parallel",)))(mine, parts, own, w, m, v, *(prev or ()))


def _sum_adamw(parts, own, mine, w, m, v, name):
    r, l = w.shape
    lp = parts.shape[2]
    tr = _tile(r, (256, 128, 64, 32, 16, 8))
    c1 = 1.0 / (1.0 - ADAM_B1 ** ADAM_STEP)
    c2 = 1.0 / (1.0 - ADAM_B2 ** ADAM_STEP)

    def body(mine_ref, p_ref, o_ref, w_ref, m_ref, v_ref, g_ref, d_ref, nm_ref, nv_ref):
        mine_v = (o_ref[0] if own.ndim == 3 else o_ref[...]).astype(f32)
        g = jnp.where(mine_ref[0] == 0, mine_v, p_ref[0].astype(f32))
        for s in range(1, parts.shape[0]):
            g = g + jnp.where(mine_ref[0] == s, mine_v, p_ref[s].astype(f32))
        if lp != l:
            g = g[:, :l]
        m_new = ADAM_B1 * m_ref[...] + (1.0 - ADAM_B1) * g
        v_new = ADAM_B2 * v_ref[...] + (1.0 - ADAM_B2) * (g * g)
        g_ref[...] = g
        nm_ref[...] = m_new
        nv_ref[...] = v_new
        d_ref[...] = -ADAM_LR * ((m_new * c1) / (jnp.sqrt(v_new * c2) + ADAM_EPS) + ADAM_WD * w_ref[...])

    tile = pl.BlockSpec((tr, l), lambda i, mn: (i, 0))
    own_spec = (pl.BlockSpec((1, tr, lp), lambda i, mn: (mn[0], i, 0)) if own.ndim == 3
                else pl.BlockSpec((tr, lp), lambda i, mn: (i, 0)))
    return pl.pallas_call(
        body, name=name, out_shape=[_sds((r, l), f32)] * 4,
        grid_spec=pltpu.PrefetchScalarGridSpec(
            num_scalar_prefetch=1, grid=(r // tr,),
            in_specs=[pl.BlockSpec((parts.shape[0], tr, lp), lambda i, mn: (0, i, 0)), own_spec, tile, tile, tile],
            out_specs=[tile] * 4),
        compiler_params=_cparams(("parallel",)))(mine, parts, own, w, m, v)


def _pack(arrs, lead=None):
    if lead is None:
        flat = jnp.concatenate([a.reshape(-1).astype(f32) for a in arrs])
        n = flat.shape[0]
    else:
        flat = jnp.concatenate([a.reshape(lead, -1).astype(f32) for a in arrs], axis=1)
        n = flat.shape[1]
    tot = -(-n // 1024) * 1024
    if lead is None:
        return jnp.pad(flat, (0, tot - n)).reshape(tot // 128, 128)
    return jnp.pad(flat, ((0, 0), (0, tot - n))).reshape(lead, tot // 128, 128)


def _unpack(packed, shapes, lead=False):
    flat = packed.reshape(packed.shape[0], -1) if lead else packed.reshape(-1)
    out, off = [], 0
    for s in shapes:
        n = int(np.prod(s))
        out.append(flat[:, off:off + n].reshape((packed.shape[0],) + tuple(s)) if lead else flat[off:off + n].reshape(s))
        off += n
    return out


def _merge_shards(g, axis):
    g = jnp.moveaxis(g, 0, axis)
    s = g.shape
    return g.reshape(s[:axis] + (s[axis] * s[axis + 1],) + s[axis + 2:])


def _split_shards(full, axis):
    s = full.shape
    g = full.reshape(s[:axis] + (N_DEV, s[axis] // N_DEV) + s[axis + 1:])
    return jnp.moveaxis(g, axis, 0)


def kernel(x, norm_mix, norm_ffn, norm_final, ab_w_in, gdn_conv_w, gdn_a_log, gdn_dt_bias, gdn_norm, hgrn_lower_bounds, hgrn_norm, ab_w_out, c_w_in, c_conv_w, c_conv_b, c_gate_a_w, c_gate_a_b, c_gate_x_w, c_gate_x_b, c_lambda, c_w_out, ffn_w_up, ffn_conv_w, ffn_conv_b, ffn_w_down, loss_target, m_norm_mix, m_norm_ffn, m_norm_final, m_ab_w_in, m_gdn_conv_w, m_gdn_a_log, m_gdn_dt_bias, m_gdn_norm, m_hgrn_lower_bounds, m_hgrn_norm, m_ab_w_out, m_c_w_in, m_c_conv_w, m_c_conv_b, m_c_gate_a_w, m_c_gate_a_b, m_c_gate_x_w, m_c_gate_x_b, m_c_lambda, m_c_w_out, m_ffn_w_up, m_ffn_conv_w, m_ffn_conv_b, m_ffn_w_down, v_norm_mix, v_norm_ffn, v_norm_final, v_ab_w_in, v_gdn_conv_w, v_gdn_a_log, v_gdn_dt_bias, v_gdn_norm, v_hgrn_lower_bounds, v_hgrn_norm, v_ab_w_out, v_c_w_in, v_c_conv_w, v_c_conv_b, v_c_gate_a_w, v_c_gate_a_b, v_c_gate_x_w, v_c_gate_x_b, v_c_lambda, v_c_w_out, v_ffn_w_up, v_ffn_conv_w, v_ffn_conv_b, v_ffn_w_down):
    wl = dict(zip(WEIGHTS, (norm_mix, norm_ffn, norm_final, ab_w_in, gdn_conv_w, gdn_a_log, gdn_dt_bias, gdn_norm, hgrn_lower_bounds, hgrn_norm, ab_w_out, c_w_in, c_conv_w, c_conv_b, c_gate_a_w, c_gate_a_b, c_gate_x_w, c_gate_x_b, c_lambda, c_w_out, ffn_w_up, ffn_conv_w, ffn_conv_b, ffn_w_down)))
    ml = dict(zip(WEIGHTS, (m_norm_mix, m_norm_ffn, m_norm_final, m_ab_w_in, m_gdn_conv_w, m_gdn_a_log, m_gdn_dt_bias, m_gdn_norm, m_hgrn_lower_bounds, m_hgrn_norm, m_ab_w_out, m_c_w_in, m_c_conv_w, m_c_conv_b, m_c_gate_a_w, m_c_gate_a_b, m_c_gate_x_w, m_c_gate_x_b, m_c_lambda, m_c_w_out, m_ffn_w_up, m_ffn_conv_w, m_ffn_conv_b, m_ffn_w_down)))
    vl = dict(zip(WEIGHTS, (v_norm_mix, v_norm_ffn, v_norm_final, v_ab_w_in, v_gdn_conv_w, v_gdn_a_log, v_gdn_dt_bias, v_gdn_norm, v_hgrn_lower_bounds, v_hgrn_norm, v_ab_w_out, v_c_w_in, v_c_conv_w, v_c_conv_b, v_c_gate_a_w, v_c_gate_a_b, v_c_gate_x_w, v_c_gate_x_b, v_c_lambda, v_c_w_out, v_ffn_w_up, v_ffn_conv_w, v_ffn_conv_b, v_ffn_w_down)))

    big = [n for n in SHARDED if n in MATMUL_WEIGHTS]
    vec = [n for n in SHARDED if n not in MATMUL_WEIGHTS]
    shards = {n: wl[n].astype(bf16) for n in big}
    shards['ab_w_in'] = jnp.pad(shards['ab_w_in'], ((0, 0), (0, 0), (0, AB_SHARD_PAD - AB_SHARD)))
    shards['ffn_w_up'] = jnp.pad(jnp.swapaxes(shards['ffn_w_up'], 1, 2), ((0, 0), (0, FF_PAD - FF_SHARD), (0, 0)))
    shards['vec'] = _pack([wl[n] for n in vec])
    pos = _position()
    layer_items = [_layer_items(l) for l in range(DEPTH)]
    posv = [v.astype(jnp.int32).reshape(1) for v in (4 * pos[0] + 2 * pos[1] + pos[2], 2 * pos[0] + pos[1], pos[2])]
    first_items = layer_items[0] + [('vec', None)]
    lands = [_place_own(first_items, shards, posv, "place_own_0")]
    lands += [_place_own(layer_items[l], shards, posv, "place_own_%d" % l) for l in range(1, DEPTH)]
    mix0, ffn0 = layer_items[0][:-2], layer_items[0][-2:]
    first = _gather_now(mix0 + [('vec', None)], shards, lands[0][:len(mix0)] + lands[0][-1:])
    flight = {'shards': {n: shards[n] for n in big}}

    full = {n: wl[n] for n in REPLICATED}

    def start(items, item_lands, token, name, anchor):
        sems, thru, flight['lands'], tok = _gather_start(items, flight['shards'], item_lands, token, name)
        flight['sems'] = list(sems)
        flight['shards'].update(thru)
        full[anchor] = tok

    start(ffn0, lands[0][len(mix0):-1], first[-1], "gather_start_0", ('fwd', 0))

    for n, a in zip(vec, _unpack(first[-1], [wl[n].shape for n in vec], lead=True)):
        full[n] = _merge_shards(a, SHARD_AXIS[n])
    full['ffn_conv_w'] = _block_pad(full['ffn_conv_w'], 2, 4, FF_PAD)
    full['ffn_conv_b'] = _block_pad(full['ffn_conv_b'], 1, 4, FF_PAD)
    for n in big:
        full[n] = {}

    def arrive(items, x_in, tag):
        flight['shards'], got = _gather_wait(items, flight['sems'], flight['shards'], flight['lands'], x_in,
                                             "gather_wait_" + tag)
        return _gather_forward(items, got, "gather_forward_" + tag)

    def fetch(l, x_in, part):
        if l == 0 and part == 'mix':
            items, got = mix0, first[:len(mix0)]
        elif l == 0:
            items, got = ffn0, arrive(ffn0, x_in, "0")
            start(layer_items[1], lands[1], got[0], "gather_start_1", ('fwdf', 0))
        elif part == 'mix':
            items = layer_items[l]
            got = arrive(items, x_in, str(l))
            if l + 1 < DEPTH:
                start(layer_items[l + 1], lands[l + 1], got[0], "gather_start_%d" % (l + 1), ('fwd', l))
        else:
            return
        for (nm, li), a in zip(items, got):
            if nm == 'ab_w_in':
                a = _ab_permute(_block_unpad(a, 1, N_DEV, AB_SHARD))
            elif nm in ('ab_w_out', 'c_w_out'):
                a = a.reshape(D_MODEL, D_MODEL)
            elif nm in ('c_gate_a_w', 'c_gate_x_w'):
                a = a.reshape(HEADS, LRU_BLOCK, LRU_BLOCK)
            elif nm == 'ffn_w_down':
                a = a.reshape(D_FFP, D_MODEL)
            elif nm == 'ffn_w_up':
                a = a.reshape(2 * D_FFP, D_MODEL)
            full[nm][li] = a

    cpos = lax.axis_index("c").astype(jnp.int32).reshape(1)
    mine = (2 * lax.axis_index("x") + lax.axis_index("y")).astype(jnp.int32).reshape(1)
    pending = {}

    def exchange(key, items, g, anchor):
        fulls = {}
        for nm, _ in items:
            a = g[nm].astype(bf16)
            if nm == 'ab_w_in':
                a = _block_pad(_ab_unpermute(a), 1, N_DEV, AB_SHARD_PAD)
            fulls[nm] = a.reshape((1,) + BLOCK_LAYOUT[nm][0][1:])
        pair, _ = _pair_exchange(fulls, jnp.zeros((8, 128), f32), "_" + key)
        chips = [_chip_sum(nm, fulls[nm], pair[nm], cpos, "_" + key) for nm in fulls]
        sems, chips, crosses, tok = _cross_start(chips, "grad_cross_start_" + key)
        pending[key] = (items, sems, chips, crosses)
        full[anchor] = tok

    stash = {}

    def push(l, part, g):
        if l == 0:
            exchange("0f" if part == 'ffn' else "0", ffn0 if part == 'ffn' else mix0, g,
                     ('bwdm', 0) if part == 'ffn' else ('bwd', -1))
            return
        stash.update(g)
        if part == 'mix':
            exchange(str(l), layer_items[l], dict(stash), ('bwd', l - 1))
            stash.clear()

    loss, dx, grads = _local_step(x[0], loss_target[0], full, fetch, push)

    grads['ffn_conv_w'] = _block_unpad(grads['ffn_conv_w'], 2, 4, FF_SHARD)
    grads['ffn_conv_b'] = _block_unpad(grads['ffn_conv_b'], 1, 4, FF_SHARD)
    fulls = {'vec': _pack([_split_shards(grads[n], SHARD_AXIS[n]) for n in vec], lead=N_DEV)}
    chip, rep_chip, recv, rrep = _exchange_grads(fulls, _pack([grads[n] for n in REPLICATED]))
    res = {}
    stacked = {}
    after = full['bwd', -1]
    for key in ("3", "2", "1", "0f", "0"):
        items, sems, chips, lands = pending[key]
        chips, lands = _cross_wait(sems, chips, lands, after, "grad_cross_wait_" + key)
        for (n, li), own, parts in zip(items, chips, lands):
            if n == 'ffn_w_up':
                wmv = [jnp.swapaxes(a[n], 1, 2) for a in (wl, ml, vl)]
                rp = FF_PAD
            else:
                shp = wl[n].shape
                rp = int(np.prod(shp[1:-1]))
                wmv = [a[n].reshape(shp[0], rp, shp[-1]) for a in (wl, ml, vl)]
            stacked[n] = _sum_adamw_layer(parts.reshape(4, rp, -1), own.reshape(4, rp, -1), mine, *wmv, li,
                                          stacked.get(n), "adamw_%s_%d" % (n, li))
        if key == "0f":
            after = stacked['ffn_w_up'][0]
    for n in big:
        for kind, o in zip(("grad", "delta", "new_m", "new_v"), stacked[n]):
            res[kind, n] = jnp.swapaxes(o, 1, 2) if n == 'ffn_w_up' else o.reshape(wl[n].shape)
    for names, parts, own, tag in ((vec, recv['vec'], chip['vec'], "adamw_vectors"),
                                   (REPLICATED, rrep, rep_chip, "adamw_replicated")):
        outs = _sum_adamw(parts, own, mine, _pack([wl[n] for n in names]), _pack([ml[n] for n in names]),
                          _pack([vl[n] for n in names]), tag)
        for kind, o in zip(("grad", "delta", "new_m", "new_v"), outs):
            for n, a in zip(names, _unpack(o, [wl[n].shape for n in names])):
                res[kind, n] = a

    loss = lax.psum(loss, ("x", "y", "c"))
    return (loss, dx[None], *[res[kind, n] for kind in ("grad", "delta", "new_m", "new_v") for n in WEIGHTS])
```

```python
import functools

import numpy as np
import jax
import jax.numpy as jnp
from jax import lax
from jax.experimental import pallas as pl
from jax.experimental.pallas import tpu as pltpu

f32 = jnp.float32
bf16 = jnp.bfloat16
HI = lax.Precision.HIGHEST
MESH = pl.DeviceIdType.MESH

N_DEV = 8
D_MODEL = 1024
DEPTH = 4
EPS = 1e-6
F_FLOOR = 1e-30
HEADS = 4
HEAD_DIM = 128
GDN_WIDTH = 512
GDN_CONV = 4
GDN_CHUNK = 64
HGRN_CHUNK = 16
HGRN_STEP = 128
MIX_WIDTH = 1024
AB_COLS = 4104
AB_PAD = 4224
LRU_WIDTH = 1024
LRU_BLOCK = 256
LRU_CONV = 4
RG_C = 8.0
D_FF = 2816
FF_SHARD = 704
FF_PAD = 768
D_FFP = 4 * FF_PAD
FF_ROWS = 352
AB_SHARD, AB_SHARD_PAD = 513, 640
FFN_CONV = 3
ADAM_LR, ADAM_B1, ADAM_B2, ADAM_EPS, ADAM_WD, ADAM_STEP = 0.001, 0.9, 0.999, 1e-08, 0.01, 10
VMEM_LIMIT = 56 * 1024 * 1024
ROW_SLAB = 32

OFF_Q, OFF_K, OFF_V, OFF_Z, OFF_QB, OFF_FB, OFF_IB, OFF_GB, OFF_BA = 0, 512, 1024, 1536, 2048, 2560, 3072, 3584, 4096

WEIGHTS = ['norm_mix', 'norm_ffn', 'norm_final', 'ab_w_in', 'gdn_conv_w', 'gdn_a_log', 'gdn_dt_bias', 'gdn_norm',
           'hgrn_lower_bounds', 'hgrn_norm', 'ab_w_out', 'c_w_in', 'c_conv_w', 'c_conv_b', 'c_gate_a_w', 'c_gate_a_b',
           'c_gate_x_w', 'c_gate_x_b', 'c_lambda', 'c_w_out', 'ffn_w_up', 'ffn_conv_w', 'ffn_conv_b', 'ffn_w_down']
SHARD_AXIS = {'norm_mix': None, 'norm_ffn': None, 'norm_final': None, 'ab_w_in': 2, 'gdn_conv_w': 2, 'gdn_a_log': None,
              'gdn_dt_bias': None, 'gdn_norm': None, 'hgrn_lower_bounds': None, 'hgrn_norm': None, 'ab_w_out': 1,
              'c_w_in': 2, 'c_conv_w': 2, 'c_conv_b': 1, 'c_gate_a_w': 2, 'c_gate_a_b': 1, 'c_gate_x_w': 2,
              'c_gate_x_b': 1, 'c_lambda': 1, 'c_w_out': 1, 'ffn_w_up': 2, 'ffn_conv_w': 2, 'ffn_conv_b': None,
              'ffn_w_down': 1}
MATMUL_WEIGHTS = ('ab_w_in', 'ab_w_out', 'c_w_in', 'c_gate_a_w', 'c_gate_x_w', 'c_w_out', 'ffn_w_up', 'ffn_w_down')
SHARDED = [n for n in WEIGHTS if SHARD_AXIS[n] is not None]
REPLICATED = [n for n in WEIGHTS if SHARD_AXIS[n] is None]


def _tile(n, prefs=(512, 384, 256, 128)):
    for p in prefs:
        if n % p == 0:
            return p
    return n


def _cparams(sem=None):
    kw = dict(vmem_limit_bytes=VMEM_LIMIT)
    if sem is not None:
        kw['dimension_semantics'] = sem
    return pltpu.CompilerParams(**kw)


def _sds(shape, dtype):
    return jax.ShapeDtypeStruct(tuple(shape), dtype)


def _sigmoid(x):
    return 1.0 / (1.0 + jnp.exp(-x))


def _silu(x):
    return x * (0.5 * jnp.tanh(0.5 * x) + 0.5)


def _log1p(x):
    u = 1.0 + x
    return jnp.where(u == 1.0, x, jnp.log(u) * (x / jnp.where(u == 1.0, 1.0, u - 1.0)))


def _softplus(x):
    return jnp.maximum(x, 0.0) + _log1p(jnp.exp(-jnp.abs(x)))


def _expm1(x):
    small = jnp.abs(x) < 0.05
    xs = jnp.where(small, x, 0.0)
    series = xs * (1.0 + xs * (0.5 + xs * (1.0 / 6.0 + xs * (1.0 / 24.0 + xs * (1.0 / 120.0)))))
    return jnp.where(small, series, jnp.exp(x) - 1.0)


def _gelu(x):
    return 0.5 * x * (1.0 + jnp.tanh(0.7978845608028654 * (x + 0.044715 * x * x * x)))


def _rms(x, gain):
    return x * lax.rsqrt(jnp.mean(x * x, axis=-1, keepdims=True) + EPS) * gain


def _dot(a, b, dims=((1,), (0,)), precision=None):
    return lax.dot_general(a, b, (dims, ((), ())), precision=precision, preferred_element_type=f32)


def _bdot(a, b, dims=((1,), (0,))):
    return _dot(a.astype(bf16), b.astype(bf16), dims)


NT = ((1,), (1,))
TN = ((0,), (0,))


def _shift_down(x, k):
    if k == 0:
        return x
    row = lax.broadcasted_iota(jnp.int32, x.shape, 0)
    return jnp.where(row >= k, pltpu.roll(x, k, 0), 0.0)


def _shift_up(x, k, fill=0.0):
    if k == 0:
        return x
    n = x.shape[0]
    row = lax.broadcasted_iota(jnp.int32, x.shape, 0)
    return jnp.where(row < n - k, pltpu.roll(x, n - k, 0), fill)


def _conv_fwd(x, w_ref, width):
    acc = w_ref[width - 1:width, :] * x
    for k in range(width - 1):
        acc = acc + w_ref[k:k + 1, :] * _shift_down(x, width - 1 - k)
    return acc


def _conv_bwd(x, dout, w_ref, dw_ref, width):
    dx = w_ref[width - 1:width, :] * dout
    dw_ref[width - 1:width, :] = jnp.sum(dout * x, axis=0, keepdims=True)
    for k in range(width - 1):
        s = width - 1 - k
        dx = dx + w_ref[k:k + 1, :] * _shift_up(dout, s)
        dw_ref[k:k + 1, :] = jnp.sum(dout * _shift_down(x, s), axis=0, keepdims=True)
    return dx


MM_VMEM_BUDGET = 36 * 1024 * 1024
MM_MAX_TILE = 1024 * 1024


def _mm_tiles(m, n, k, out_bytes):
    best = None
    for tm in (1024, 512, 384, 256, 128):
        if m % tm:
            continue
        for tn in range(1536, 0, -128):
            if n % tn or tm * tn > MM_MAX_TILE:
                continue
            score = (tm * tn, min(tm, tn))
            if 2 * (tm * k * 2 + k * tn * 2 + tm * tn * out_bytes) <= MM_VMEM_BUDGET and (best is None or score > best[0]):
                best = (score, tm, tn)
    return (best[1], best[2]) if best else (_tile(m), _tile(n))


def _mm(a, b, *, ta=False, tb=False, add=None, out_dtype=f32, name):
    m, k = (a.shape[1], a.shape[0]) if ta else a.shape
    n = b.shape[0] if tb else b.shape[1]
    tm, tn = _mm_tiles(m, n, k, jnp.dtype(out_dtype).itemsize + (4 if add is not None else 0))
    dims = ((0 if ta else 1,), (1 if tb else 0,))

    def body(*refs):
        a_ref, b_ref = refs[0], refs[1]
        o_ref = refs[-1]
        r = _dot(a_ref[...], b_ref[...], dims)
        if add is not None:
            r = r + refs[2][...]
        o_ref[...] = r.astype(out_dtype)

    a_spec = pl.BlockSpec((k, tm), lambda j, i: (0, i)) if ta else pl.BlockSpec((tm, k), lambda j, i: (i, 0))
    b_spec = pl.BlockSpec((tn, k), lambda j, i: (j, 0)) if tb else pl.BlockSpec((k, tn), lambda j, i: (0, j))
    o_spec = pl.BlockSpec((tm, tn), lambda j, i: (i, j))
    ins, specs = [a, b], [a_spec, b_spec]
    if add is not None:
        ins.append(add)
        specs.append(o_spec)
    return pl.pallas_call(body, name=name, grid=(n // tn, m // tm), in_specs=specs, out_specs=o_spec,
                          out_shape=_sds((m, n), out_dtype), compiler_params=_cparams(("parallel", "parallel")))(*ins)


def _rms_fwd(x, gain, name):
    t, d = x.shape
    tr = _tile(t, (256, 128))

    def body(x_ref, g_ref, h_ref):
        h_ref[...] = _rms(x_ref[...], g_ref[...]).astype(bf16)

    return pl.pallas_call(body, name=name, grid=(t // tr,),
                          in_specs=[pl.BlockSpec((tr, d), lambda i: (i, 0)), pl.BlockSpec((1, d), lambda i: (0, 0))],
                          out_specs=pl.BlockSpec((tr, d), lambda i: (i, 0)), out_shape=_sds((t, d), bf16),
                          compiler_params=_cparams(("parallel",)))(x, gain)


def _rms_bwd(x, gain, dh, dres, name):
    t, d = x.shape
    tr = _tile(t, (256, 128))

    def body(x_ref, g_ref, dh_ref, dres_ref, dx_ref, dxb_ref, dg_ref):
        _, vjp = jax.vjp(_rms, x_ref[...], g_ref[...])
        dx, dg = vjp(dh_ref[...])
        dx = dx + dres_ref[...]
        dx_ref[...] = dx
        dxb_ref[...] = dx.astype(bf16)

        @pl.when(pl.program_id(0) == 0)
        def _():
            dg_ref[...] = jnp.zeros_like(dg_ref)

        dg_ref[...] += dg

    row = pl.BlockSpec((tr, d), lambda i: (i, 0))
    vec = pl.BlockSpec((1, d), lambda i: (0, 0))
    return pl.pallas_call(body, name=name, grid=(t // tr,), in_specs=[row, vec, row, row], out_specs=[row, row, vec],
                          out_shape=[_sds((t, d), f32), _sds((t, d), bf16), _sds((1, d), f32)],
                          compiler_params=_cparams(("arbitrary",)))(x, gain, dh, dres)


def _loss_head(x, gain, target):
    t, d = x.shape
    tr = _tile(t, (256, 128))

    def f(xv, g, tgt):
        err = _rms(xv, g) - tgt
        return 0.5 * jnp.sum(jnp.mean(err * err, axis=-1, keepdims=True), axis=0, keepdims=True)

    def body(x_ref, g_ref, t_ref, loss_ref, dx_ref, dxb_ref, dg_ref):
        loss, vjp = jax.vjp(lambda xv, g: f(xv, g, t_ref[...]), x_ref[...], g_ref[...])
        dx, dg = vjp(jnp.ones((1, 1), f32))
        dx_ref[...] = dx
        dxb_ref[...] = dx.astype(bf16)

        @pl.when(pl.program_id(0) == 0)
        def _():
            dg_ref[...] = jnp.zeros_like(dg_ref)
            loss_ref[...] = jnp.zeros_like(loss_ref)

        dg_ref[...] += dg
        loss_ref[...] += jnp.broadcast_to(loss, loss_ref.shape)

    row = pl.BlockSpec((tr, d), lambda i: (i, 0))
    vec = pl.BlockSpec((1, d), lambda i: (0, 0))
    one = pl.BlockSpec((8, 128), lambda i: (0, 0))
    return pl.pallas_call(body, name="loss_head", grid=(t // tr,), in_specs=[row, vec, row],
                          out_specs=[one, row, row, vec],
                          out_shape=[_sds((8, 128), f32), _sds((t, d), f32), _sds((t, d), bf16), _sds((1, d), f32)],
                          compiler_params=_cparams(("arbitrary",)))(x, gain, target)


def _ffn_act_fwd(u, conv_w, conv_b):
    t = u.shape[0]
    tc = FF_PAD // 2
    nb = D_FFP // tc

    def body(g_ref, v_ref, w_ref, b_ref, a_ref):
        gc = _conv_fwd(g_ref[...], w_ref, FFN_CONV) + b_ref[...]
        a_ref[...] = (_silu(gc) * v_ref[...]).astype(bf16)

    return pl.pallas_call(
        body, name="ffn_act_fwd", grid=(nb,),
        in_specs=[pl.BlockSpec((t, tc), lambda j: (0, j)), pl.BlockSpec((t, tc), lambda j: (0, j + nb)),
                  pl.BlockSpec((FFN_CONV, tc), lambda j: (0, j)), pl.BlockSpec((1, tc), lambda j: (0, j))],
        out_specs=pl.BlockSpec((t, tc), lambda j: (0, j)), out_shape=_sds((t, D_FFP), bf16),
        compiler_params=_cparams(("parallel",)))(u, u, conv_w, conv_b)


def _ffn_act_bwd(u, conv_w, conv_b, da):
    t = u.shape[0]
    tc = FF_PAD // 2
    nb = D_FFP // tc

    def act(gc, val):
        return _silu(gc) * val

    def body(g_ref, v_ref, w_ref, b_ref, da_ref, a_ref, dg_ref, dv_ref, dw_ref, db_ref, gc_ref):
        gp = g_ref[...]
        gc_ref[...] = _conv_fwd(gp, w_ref, FFN_CONV) + b_ref[...]

        def slab(i, carry):
            rows = pl.ds(pl.multiple_of(i * ROW_SLAB, ROW_SLAB), ROW_SLAB)
            a, vjp = jax.vjp(act, gc_ref[rows, :], v_ref[rows, :])
            dgc, dval = vjp(da_ref[rows, :])
            a_ref[rows, :] = a.astype(bf16)
            dv_ref[rows, :] = dval.astype(bf16)
            gc_ref[rows, :] = dgc
            return carry

        lax.fori_loop(0, t // ROW_SLAB, slab, 0, unroll=2)
        dgc = gc_ref[...]
        db_ref[...] = jnp.sum(dgc, axis=0, keepdims=True)
        dg_ref[...] = _conv_bwd(gp, dgc, w_ref, dw_ref, FFN_CONV).astype(bf16)

    col = pl.BlockSpec((t, tc), lambda j: (0, j))
    return pl.pallas_call(
        body, name="ffn_act_bwd", grid=(nb,),
        in_specs=[col, pl.BlockSpec((t, tc), lambda j: (0, j + nb)), pl.BlockSpec((FFN_CONV, tc), lambda j: (0, j)),
                  pl.BlockSpec((1, tc), lambda j: (0, j)), col],
        out_specs=[col, col, col, pl.BlockSpec((FFN_CONV, tc), lambda j: (0, j)), pl.BlockSpec((1, tc), lambda j: (0, j))],
        out_shape=[_sds((t, D_FFP), bf16), _sds((t, D_FFP), bf16), _sds((t, D_FFP), bf16), _sds((FFN_CONV, D_FFP), f32),
                   _sds((1, D_FFP), f32)],
        scratch_shapes=[pltpu.VMEM((t, tc), f32)],
        compiler_params=_cparams(("parallel",)))(u, u, conv_w, conv_b, da)


def _lru_gates(xc, ra, ia, lam):
    r = _sigmoid(ra)
    i = _sigmoid(ia)
    log_a = -RG_C * r * _softplus(-lam)
    a = jnp.exp(log_a)
    u = jnp.sqrt(jnp.maximum(-_expm1(2.0 * log_a), 0.0)) * (i * xc)
    return a, u


SCAN_BLOCK = 64


def _lin_scan(a_ref, u_ref, h_ref):
    n, c = a_ref.shape
    blk = min(SCAN_BLOCK, n)
    row = lax.broadcasted_iota(jnp.int32, (blk, c), 0)

    def step(i, h_prev):
        rows = pl.ds(pl.multiple_of(i * blk, blk), blk)
        a, u = a_ref[rows, :], u_ref[rows, :]
        s = 1
        while s < blk:
            keep = row >= s
            u = a * jnp.where(keep, pltpu.roll(u, s, 0), 0.0) + u
            a = a * jnp.where(keep, pltpu.roll(a, s, 0), 1.0)
            s *= 2
        h = u + a * h_prev
        h_ref[rows, :] = h
        return h[blk - 1:blk, :]

    lax.fori_loop(0, n // blk, step, jnp.zeros((1, c), f32))


def _rev_scan(an_ref, d_ref, g_ref):
    n, c = d_ref.shape
    blk = min(SCAN_BLOCK, n)
    row = lax.broadcasted_iota(jnp.int32, (blk, c), 0)

    def step(i, g_next):
        rows = pl.ds(pl.multiple_of((n // blk - 1 - i) * blk, blk), blk)
        a, d = an_ref[rows, :], d_ref[rows, :]
        s = 1
        while s < blk:
            keep = row < blk - s
            d = a * jnp.where(keep, pltpu.roll(d, blk - s, 0), 0.0) + d
            a = a * jnp.where(keep, pltpu.roll(a, blk - s, 0), 1.0)
            s *= 2
        g = d + a * g_next
        g_ref[rows, :] = g
        return g[0:1, :]

    lax.fori_loop(0, n // blk, step, jnp.zeros((1, c), f32))


def _col_conv_fwd(p, col_off, conv_w, conv_b, width, tc, name):
    t = p.shape[0]
    c = conv_w.shape[1]
    ob = col_off // tc

    def body(x_ref, w_ref, b_ref, o_ref):
        o_ref[...] = _conv_fwd(x_ref[...], w_ref, width) + b_ref[...]

    return pl.pallas_call(
        body, name=name, grid=(c // tc,),
        in_specs=[pl.BlockSpec((t, tc), lambda j: (0, j + ob)), pl.BlockSpec((width, tc), lambda j: (0, j)),
                  pl.BlockSpec((1, tc), lambda j: (0, j))],
        out_specs=pl.BlockSpec((t, tc), lambda j: (0, j)), out_shape=_sds((t, c), f32),
        compiler_params=_cparams(("parallel",)))(p, conv_w, conv_b)


def _col_conv_bwd(p, col_off, conv_w, dxc, width, tc, name):
    t = p.shape[0]
    c = conv_w.shape[1]
    ob = col_off // tc

    def body(x_ref, w_ref, d_ref, dx_ref, dw_ref, db_ref):
        d = d_ref[...]
        db_ref[...] = jnp.sum(d, axis=0, keepdims=True)
        dx_ref[...] = _conv_bwd(x_ref[...], d, w_ref, dw_ref, width).astype(bf16)

    col = pl.BlockSpec((t, tc), lambda j: (0, j))
    return pl.pallas_call(
        body, name=name, grid=(c // tc,),
        in_specs=[pl.BlockSpec((t, tc), lambda j: (0, j + ob)), pl.BlockSpec((width, tc), lambda j: (0, j)), col],
        out_specs=[col, pl.BlockSpec((width, tc), lambda j: (0, j)), pl.BlockSpec((1, tc), lambda j: (0, j))],
        out_shape=[_sds((t, c), bf16), _sds((width, c), f32), _sds((1, c), f32)],
        compiler_params=_cparams(("parallel",)))(p, conv_w, dxc)


def _lru_fwd(p, xc, wa, ba, wx, bx, lam):
    t = p.shape[0]
    bw = LRU_BLOCK

    def body(y_ref, xc_ref, wa_ref, ba_ref, wx_ref, bx_ref, lam_ref, out_ref, hs_ref, a_ref, u_ref):
        xc_v = xc_ref[...]
        xb = xc_v.astype(bf16)
        ra = _dot(xb, wa_ref[0]) + ba_ref[...]
        ia = _dot(xb, wx_ref[0]) + bx_ref[...]
        a_ref[...], u_ref[...] = _lru_gates(xc_v, ra, ia, lam_ref[...])
        _lin_scan(a_ref, u_ref, hs_ref)
        out_ref[...] = (hs_ref[...] * _gelu(y_ref[...])).astype(bf16)

    col = pl.BlockSpec((t, bw), lambda h: (0, h))
    vec = pl.BlockSpec((1, bw), lambda h: (0, h))
    mat = pl.BlockSpec((1, bw, bw), lambda h: (h, 0, 0))
    return pl.pallas_call(
        body, name="lru_fwd", grid=(HEADS,), in_specs=[col, col, mat, vec, mat, vec, vec], out_specs=[col, col, col],
        out_shape=[_sds((t, LRU_WIDTH), bf16), _sds((t, LRU_WIDTH), f32), _sds((t, LRU_WIDTH), f32)],
        scratch_shapes=[pltpu.VMEM((t, bw), f32)],
        compiler_params=_cparams(("parallel",)))(p, xc, wa, ba, wx, bx, lam)


def _lru_bwd_scan(p, a, hs, dout):
    t = p.shape[0]
    bw = LRU_BLOCK

    def body(y_ref, a_ref, hs_ref, do_ref, dy_ref, da_ref, du_ref, an_ref, d_ref):
        hs_v = hs_ref[...]
        do = do_ref[...]
        gate, vjp = jax.vjp(_gelu, y_ref[...])
        dy_ref[...] = vjp(do * hs_v)[0].astype(bf16)
        an_ref[...] = _shift_up(a_ref[...], 1)
        d_ref[...] = do * gate
        _rev_scan(an_ref, d_ref, du_ref)
        da_ref[...] = du_ref[...] * _shift_down(hs_v, 1)

    col = pl.BlockSpec((t, bw), lambda h: (0, h))
    return pl.pallas_call(
        body, name="lru_bwd_scan", grid=(HEADS,), in_specs=[col, col, col, col], out_specs=[col, col, col],
        out_shape=[_sds((t, LRU_WIDTH), bf16), _sds((t, LRU_WIDTH), f32), _sds((t, LRU_WIDTH), f32)],
        scratch_shapes=[pltpu.VMEM((t, bw), f32), pltpu.VMEM((t, bw), f32)],
        compiler_params=_cparams(("parallel",)))(p, a, hs, dout)


def _lru_bwd_gates(xc, da, du, wa, ba, wx, bx, lam):
    t = xc.shape[0]
    bw = LRU_BLOCK
    tr = _tile(t, (512, 256, 128))

    def body(xc_ref, da_ref, du_ref, wa_ref, ba_ref, wx_ref, bx_ref, lam_ref,
             dxc_ref, dwa_ref, dwx_ref, dba_ref, dbx_ref, dlam_ref):
        xc_v = xc_ref[...]
        xb = xc_v.astype(bf16)
        ra = _dot(xb, wa_ref[0]) + ba_ref[...]
        ia = _dot(xb, wx_ref[0]) + bx_ref[...]
        _, vjp = jax.vjp(_lru_gates, xc_v, ra, ia, lam_ref[...])
        dxc, dra, dia, dlam = vjp((da_ref[...], du_ref[...]))
        drb, dib = dra.astype(bf16), dia.astype(bf16)
        dxc_ref[...] = dxc + _dot(drb, wa_ref[0], NT) + _dot(dib, wx_ref[0], NT)

        @pl.when(pl.program_id(1) == 0)
        def _():
            dwa_ref[...] = jnp.zeros_like(dwa_ref)
            dwx_ref[...] = jnp.zeros_like(dwx_ref)
            dba_ref[...] = jnp.zeros_like(dba_ref)
            dbx_ref[...] = jnp.zeros_like(dbx_ref)
            dlam_ref[...] = jnp.zeros_like(dlam_ref)

        dwa_ref[0] += _dot(xb, drb, TN)
        dwx_ref[0] += _dot(xb, dib, TN)
        dba_ref[...] += jnp.sum(dra, axis=0, keepdims=True)
        dbx_ref[...] += jnp.sum(dia, axis=0, keepdims=True)
        dlam_ref[...] += dlam

    tile = pl.BlockSpec((tr, bw), lambda h, i: (i, h))
    vec = pl.BlockSpec((1, bw), lambda h, i: (0, h))
    mat = pl.BlockSpec((1, bw, bw), lambda h, i: (h, 0, 0))
    return pl.pallas_call(
        body, name="lru_bwd_gates", grid=(HEADS, t // tr), in_specs=[tile, tile, tile, mat, vec, mat, vec, vec],
        out_specs=[tile, mat, mat, vec, vec, vec],
        out_shape=[_sds((t, LRU_WIDTH), f32), _sds((HEADS, bw, bw), f32), _sds((HEADS, bw, bw), f32),
                   _sds((1, LRU_WIDTH), f32), _sds((1, LRU_WIDTH), f32), _sds((1, LRU_WIDTH), f32)],
        compiler_params=_cparams(("parallel", "arbitrary")))(xc, da, du, wa, ba, wx, bx, lam)


def _gdn_pre_fn(cq, ck, cv, ba, alog, dtb, h):
    q, k, v = _silu(cq), _silu(ck), _silu(cv)
    q = q * lax.rsqrt(jnp.sum(q * q, axis=-1, keepdims=True) + EPS) * (HEAD_DIM ** -0.5)
    k = k * lax.rsqrt(jnp.sum(k * k, axis=-1, keepdims=True) + EPS)
    lane = lax.broadcasted_iota(jnp.int32, (1, HEAD_DIM), 1)
    mb = (lane == h).astype(f32)
    ma = (lane == HEADS + h).astype(f32)
    beta_raw = jnp.sum(ba * mb, axis=-1, keepdims=True)
    alpha = jnp.sum(ba * ma, axis=-1, keepdims=True)
    al = jnp.sum(alog * mb, axis=-1, keepdims=True)
    db = jnp.sum(dtb * mb, axis=-1, keepdims=True)
    beta = _sigmoid(beta_raw)
    g = -jnp.exp(al) * _softplus(alpha + db)
    return q, k, v, jnp.broadcast_to(beta, q.shape), jnp.broadcast_to(g, q.shape)


def _gdn_pre_fwd(p, conv_w, alog, dtb):
    t = p.shape[0]
    hd = HEAD_DIM

    def body(pq_ref, pk_ref, pv_ref, ba_ref, wq_ref, wk_ref, wv_ref, al_ref, dt_ref, q_ref, k_ref, v_ref, b_ref, g_ref):
        h = pl.program_id(0)
        cq = _conv_fwd(pq_ref[...], wq_ref, GDN_CONV)
        ck = _conv_fwd(pk_ref[...], wk_ref, GDN_CONV)
        cv = _conv_fwd(pv_ref[...], wv_ref, GDN_CONV)
        q, k, v, be, ge = _gdn_pre_fn(cq, ck, cv, ba_ref[...], al_ref[...], dt_ref[...], h)
        q_ref[...], k_ref[...], v_ref[...], b_ref[...], g_ref[...] = q, k, v, be, ge

    def pcol(off):
        return pl.BlockSpec((t, hd), lambda h: (0, h + off // hd))

    def wcol(off):
        return pl.BlockSpec((GDN_CONV, hd), lambda h: (0, h + off // hd))

    vec = pl.BlockSpec((1, hd), lambda h: (0, 0))
    out = pl.BlockSpec((t, hd), lambda h: (0, h))
    return pl.pallas_call(
        body, name="gdn_pre_fwd", grid=(HEADS,),
        in_specs=[pcol(OFF_Q), pcol(OFF_K), pcol(OFF_V), pl.BlockSpec((t, hd), lambda h: (0, OFF_BA // hd)),
                  wcol(0), wcol(GDN_WIDTH), wcol(2 * GDN_WIDTH), vec, vec],
        out_specs=[out] * 5, out_shape=[_sds((t, GDN_WIDTH), f32)] * 5,
        compiler_params=_cparams(("parallel",)))(p, p, p, p, conv_w, conv_w, conv_w, alog, dtb)


def _gdn_pre_bwd(p, conv_w, alog, dtb, dq, dk, dv, dbe, dge):
    t = p.shape[0]
    hd = HEAD_DIM

    def body(pq_ref, pk_ref, pv_ref, ba_ref, wq_ref, wk_ref, wv_ref, al_ref, dt_ref,
             dq_ref, dk_ref, dv_ref, dbe_ref, dge_ref,
             opq_ref, opk_ref, opv_ref, dba_ref, dwq_ref, dwk_ref, dwv_ref, dal_ref, ddt_ref):
        h = pl.program_id(0)
        pq, pk, pv = pq_ref[...], pk_ref[...], pv_ref[...]
        cq = _conv_fwd(pq, wq_ref, GDN_CONV)
        ck = _conv_fwd(pk, wk_ref, GDN_CONV)
        cv = _conv_fwd(pv, wv_ref, GDN_CONV)
        _, vjp = jax.vjp(functools.partial(_gdn_pre_fn, h=h), cq, ck, cv, ba_ref[...], al_ref[...], dt_ref[...])
        dcq, dck, dcv, dba, dal, ddt = vjp((dq_ref[...], dk_ref[...], dv_ref[...], dbe_ref[...], dge_ref[...]))
        opq_ref[...] = _conv_bwd(pq, dcq, wq_ref, dwq_ref, GDN_CONV).astype(bf16)
        opk_ref[...] = _conv_bwd(pk, dck, wk_ref, dwk_ref, GDN_CONV).astype(bf16)
        opv_ref[...] = _conv_bwd(pv, dcv, wv_ref, dwv_ref, GDN_CONV).astype(bf16)

        @pl.when(h == 0)
        def _():
            dba_ref[...] = jnp.zeros_like(dba_ref)
            dal_ref[...] = jnp.zeros_like(dal_ref)
            ddt_ref[...] = jnp.zeros_like(ddt_ref)

        dba_ref[...] += dba
        dal_ref[...] += dal
        ddt_ref[...] += ddt

    def pcol(off):
        return pl.BlockSpec((t, hd), lambda h: (0, h + off // hd))

    def wcol(off):
        return pl.BlockSpec((GDN_CONV, hd), lambda h: (0, h + off // hd))

    vec = pl.BlockSpec((1, hd), lambda h: (0, 0))
    col = pl.BlockSpec((t, hd), lambda h: (0, h))
    full = pl.BlockSpec((t, hd), lambda h: (0, 0))
    wout = pl.BlockSpec((GDN_CONV, hd), lambda h: (0, h))
    return pl.pallas_call(
        body, name="gdn_pre_bwd", grid=(HEADS,),
        in_specs=[pcol(OFF_Q), pcol(OFF_K), pcol(OFF_V), pl.BlockSpec((t, hd), lambda h: (0, OFF_BA // hd)),
                  wcol(0), wcol(GDN_WIDTH), wcol(2 * GDN_WIDTH), vec, vec, col, col, col, col, col],
        out_specs=[col, col, col, full, wout, wout, wout, vec, vec],
        out_shape=[_sds((t, GDN_WIDTH), bf16)] * 3 + [_sds((t, hd), f32)] + [_sds((GDN_CONV, GDN_WIDTH), f32)] * 3
        + [_sds((1, hd), f32)] * 2,
        compiler_params=_cparams(("arbitrary",)))(p, p, p, p, conv_w, conv_w, conv_w, alog, dtb, dq, dk, dv, dbe, dge)


BNN = (((2,), (1,)), ((0,), (0,)))
BNT = (((2,), (2,)), ((0,), (0,)))
BTN = (((1,), (1,)), ((0,), (0,)))


def _hdot(a, b, dn=BNN, precision=None):
    return lax.dot_general(a, b, dn, precision=precision, preferred_element_type=f32)


def _hbdot(a, b, dn=BNN):
    return _hdot(a.astype(bf16), b.astype(bf16), dn)


def _tri_inverse(a):
    c = a.shape[-1]
    r = lax.broadcasted_iota(jnp.int32, (c, c), 0)
    col = lax.broadcasted_iota(jnp.int32, (c, c), 1)
    m = -a
    inv = jnp.where(r == col, 1.0, 0.0) + m
    s = 2
    while s < c:
        m = _hdot(m, m, precision=HI)
        inv = inv + _hdot(inv, m, precision=HI)
        s *= 2
    return inv


@jax.custom_vjp
def _saved_inverse(a, inv):
    return inv


def _saved_inverse_fwd(a, inv):
    return inv, inv


def _saved_inverse_bwd(inv, dinv):
    return -_hdot(_hdot(inv, dinv, BTN, precision=HI), inv, BNT, precision=HI), jnp.zeros_like(inv)


_saved_inverse.defvjp(_saved_inverse_fwd, _saved_inverse_bwd)


def _gdn_chunk(s, q, k, v, ge, be, tinv=None):
    nh, c, _ = q.shape
    r = lax.broadcasted_iota(jnp.int32, (c, c), 0)
    col = lax.broadcasted_iota(jnp.int32, (c, c), 1)
    causal = r >= col
    tri = jnp.broadcast_to(causal.astype(f32), (nh, c, c))
    gc = _hdot(tri, ge, precision=HI)
    gcc = gc[:, :, :c]
    gcr = jnp.swapaxes(gc, 1, 2)[:, :c, :]
    decay = jnp.where(causal, jnp.exp(jnp.where(causal, gcc - gcr, 0.0)), 0.0)
    kb = k * be
    lower = jnp.where(r > col, _hbdot(kb, k, BNT) * decay, 0.0)
    tinv = _tri_inverse(lower) if tinv is None else _saved_inverse(lower, tinv)
    egc = jnp.exp(gc)
    u = _hdot(tinv, v * be, precision=HI)
    w = _hdot(tinv, kb * egc, precision=HI)
    attn = _hbdot(q, k, BNT) * decay
    gl = gc[:, c - 1:c, :]
    v_new = u - _hbdot(w, s)
    o = _hbdot(q * egc, s) + _hbdot(attn, v_new)
    s_new = s * jnp.exp(gl) + _hbdot(k * jnp.exp(gl - gc), v_new, BTN)
    return o, s_new, tinv


def _heads_major(ref):
    return jnp.stack([ref[:, h * HEAD_DIM:(h + 1) * HEAD_DIM] for h in range(HEADS)])


def _gdn_core_fwd(q, k, v, ge, be):
    t = q.shape[0]
    c, hd = GDN_CHUNK, HEAD_DIM
    n = t // c

    def body(q_ref, k_ref, v_ref, g_ref, b_ref, o_ref, st_ref, ti_ref, s_ref):
        @pl.when(pl.program_id(0) == 0)
        def _():
            s_ref[...] = jnp.zeros_like(s_ref)

        s = s_ref[...]
        st_ref[:, 0] = s
        o, s_new, tinv = _gdn_chunk(s, *[_heads_major(r) for r in (q_ref, k_ref, v_ref, g_ref, b_ref)])
        ti_ref[0] = tinv
        for h in range(HEADS):
            o_ref[:, h * hd:(h + 1) * hd] = o[h]
        s_ref[...] = s_new

    tile = pl.BlockSpec((c, GDN_WIDTH), lambda i: (i, 0))
    return pl.pallas_call(
        body, name="gdn_core_fwd", grid=(n,), in_specs=[tile] * 5,
        out_specs=[tile, pl.BlockSpec((HEADS, 1, hd, hd), lambda i: (0, i, 0, 0)),
                   pl.BlockSpec((1, HEADS, c, c), lambda i: (i, 0, 0, 0))],
        out_shape=[_sds((t, GDN_WIDTH), f32), _sds((HEADS, n, hd, hd), f32), _sds((n, HEADS, c, c), f32)],
        scratch_shapes=[pltpu.VMEM((HEADS, hd, hd), f32)],
        compiler_params=_cparams(("arbitrary",)))(q, k, v, ge, be)


def _gdn_core_bwd(q, k, v, ge, be, states, do):
    t = q.shape[0]
    c, hd = GDN_CHUNK, HEAD_DIM
    n = t // c
    states, tinvs = states

    def body(q_ref, k_ref, v_ref, g_ref, b_ref, st_ref, ti_ref, do_ref, dq_ref, dk_ref, dv_ref, dg_ref, db_ref, ds_ref):
        @pl.when(pl.program_id(0) == 0)
        def _():
            ds_ref[...] = jnp.zeros_like(ds_ref)

        tinv = ti_ref[0]
        _, vjp = jax.vjp(lambda *a: _gdn_chunk(*a, tinv=tinv)[:2], st_ref[:, 0],
                         *[_heads_major(r) for r in (q_ref, k_ref, v_ref, g_ref, b_ref)])
        ds, *dins = vjp((_heads_major(do_ref), ds_ref[...]))
        ds_ref[...] = ds
        for d_ref, d in zip((dq_ref, dk_ref, dv_ref, dg_ref, db_ref), dins):
            for h in range(HEADS):
                d_ref[:, h * hd:(h + 1) * hd] = d[h]

    tile = pl.BlockSpec((c, GDN_WIDTH), lambda i: (n - 1 - i, 0))
    return pl.pallas_call(
        body, name="gdn_core_bwd", grid=(n,),
        in_specs=[tile] * 5 + [pl.BlockSpec((HEADS, 1, hd, hd), lambda i: (0, n - 1 - i, 0, 0)),
                               pl.BlockSpec((1, HEADS, c, c), lambda i: (n - 1 - i, 0, 0, 0)), tile],
        out_specs=[tile] * 5, out_shape=[_sds((t, GDN_WIDTH), f32)] * 5,
        scratch_shapes=[pltpu.VMEM((HEADS, hd, hd), f32)],
        compiler_params=_cparams(("arbitrary",)))(q, k, v, ge, be, states, tinvs, do)


def _post_fn(o, z, gain):
    return _rms(o, gain) * _silu(z)


def _post_fwd(o, p, z_off, gain, name):
    t = o.shape[0]
    hd = HEAD_DIM

    def body(o_ref, z_ref, g_ref, y_ref):
        y_ref[...] = _post_fn(o_ref[...], z_ref[...], g_ref[...]).astype(bf16)

    col = pl.BlockSpec((t, hd), lambda h: (0, h))
    return pl.pallas_call(
        body, name=name, grid=(HEADS,),
        in_specs=[col, pl.BlockSpec((t, hd), lambda h: (0, h + z_off // hd)), pl.BlockSpec((1, hd), lambda h: (0, 0))],
        out_specs=col, out_shape=_sds((t, HEADS * hd), bf16), compiler_params=_cparams(("parallel",)))(o, p, gain)


def _post_bwd(o, p, z_off, gain, dmix, mix_off, name):
    t = o.shape[0]
    hd = HEAD_DIM

    def body(o_ref, z_ref, g_ref, dy_ref, do_ref, dz_ref, dg_ref):
        _, vjp = jax.vjp(_post_fn, o_ref[...], z_ref[...], g_ref[...])
        do, dz, dg = vjp(dy_ref[...])
        do_ref[...] = do
        dz_ref[...] = dz.astype(bf16)

        @pl.when(pl.program_id(0) == 0)
        def _():
            dg_ref[...] = jnp.zeros_like(dg_ref)

        dg_ref[...] += dg

    col = pl.BlockSpec((t, hd), lambda h: (0, h))
    vec = pl.BlockSpec((1, hd), lambda h: (0, 0))
    return pl.pallas_call(
        body, name=name, grid=(HEADS,),
        in_specs=[col, pl.BlockSpec((t, hd), lambda h: (0, h + z_off // hd)), vec,
                  pl.BlockSpec((t, hd), lambda h: (0, h + mix_off // hd))],
        out_specs=[col, col, vec], out_shape=[_sds((t, HEADS * hd), f32), _sds((t, HEADS * hd), bf16), _sds((1, hd), f32)],
        compiler_params=_cparams(("arbitrary",)))(o, p, gain, dmix)


def _hgrn_pre_fn(qb, fb, lbw, layer):
    l0, l1 = lbw[0:1, :], lbw[1:2, :]
    m = jnp.maximum(l0, l1)
    e0, e1 = jnp.exp(l0 - m), jnp.exp(l1 - m)
    p0, p1 = e0 / (e0 + e1), e1 / (e0 + e1)
    lb = (p0 - p0) if layer == 0 else ((p0 + p1) - p0)
    f = lb + (1.0 - lb) * _sigmoid(fb)
    return _silu(qb), 1.0 - f, jnp.log(jnp.maximum(f, F_FLOOR))


def _hgrn_pre_fwd(p, lbw, layer):
    t = p.shape[0]
    tc = HEAD_DIM

    def body(qb_ref, fb_ref, lb_ref, q_ref, k_ref, lf_ref):
        q_ref[...], k_ref[...], lf_ref[...] = _hgrn_pre_fn(qb_ref[...], fb_ref[...], lb_ref[...], layer)

    col = pl.BlockSpec((t, tc), lambda j: (0, j))
    return pl.pallas_call(
        body, name="hgrn_pre_fwd", grid=(GDN_WIDTH // tc,),
        in_specs=[pl.BlockSpec((t, tc), lambda j: (0, j + OFF_QB // tc)), pl.BlockSpec((t, tc), lambda j: (0, j + OFF_FB // tc)),
                  pl.BlockSpec((2, tc), lambda j: (0, j))],
        out_specs=[col] * 3, out_shape=[_sds((t, GDN_WIDTH), f32)] * 3,
        compiler_params=_cparams(("parallel",)))(p, p, lbw)


def _hgrn_pre_bwd(p, lbw, layer, dq, dk, dlf):
    t = p.shape[0]
    tc = HEAD_DIM

    def body(qb_ref, fb_ref, lb_ref, dq_ref, dk_ref, dlf_ref, dqb_ref, dfb_ref, dlb_ref):
        _, vjp = jax.vjp(functools.partial(_hgrn_pre_fn, layer=layer), qb_ref[...], fb_ref[...], lb_ref[...])
        dqb, dfb, dlb = vjp((dq_ref[...], dk_ref[...], dlf_ref[...]))
        dqb_ref[...] = dqb.astype(bf16)
        dfb_ref[...] = dfb.astype(bf16)
        dlb_ref[...] = dlb

    col = pl.BlockSpec((t, tc), lambda j: (0, j))
    lb = pl.BlockSpec((2, tc), lambda j: (0, j))
    return pl.pallas_call(
        body, name="hgrn_pre_bwd", grid=(GDN_WIDTH // tc,),
        in_specs=[pl.BlockSpec((t, tc), lambda j: (0, j + OFF_QB // tc)), pl.BlockSpec((t, tc), lambda j: (0, j + OFF_FB // tc)),
                  lb, col, col, col],
        out_specs=[col, col, lb], out_shape=[_sds((t, GDN_WIDTH), bf16)] * 2 + [_sds((2, GDN_WIDTH), f32)],
        compiler_params=_cparams(("parallel",)))(p, p, lbw, dq, dk, dlf)


def _hgrn_step(st, q, k, lf, v):
    c = HGRN_CHUNK
    nh, rs = q.shape[0], q.shape[1]
    r2 = lax.broadcasted_iota(jnp.int32, (rs, rs), 0)
    c2 = lax.broadcasted_iota(jnp.int32, (rs, rs), 1)
    shift = c.bit_length() - 1
    same_chunk = jnp.right_shift(r2, shift) == jnp.right_shift(c2, shift)
    tri = jnp.broadcast_to(((r2 >= c2) & same_chunk).astype(f32), (nh, rs, rs))
    b_all = _hdot(tri, lf, precision=HI)
    i3 = lax.broadcasted_iota(jnp.int32, (c, c, HEAD_DIM), 0)
    j3 = lax.broadcasted_iota(jnp.int32, (c, c, HEAD_DIM), 1)
    mask = i3 >= j3
    outs = []
    for n in range(q.shape[1] // c):
        sl = slice(n * c, (n + 1) * c)
        qc, kc, vc, b = q[:, sl], k[:, sl], v[:, sl], b_all[:, sl]
        rel = jnp.where(mask, jnp.exp(jnp.where(mask, b[:, :, None, :] - b[:, None, :, :], 0.0)), 0.0)
        scores = jnp.sum(qc[:, :, None, :] * kc[:, None, :, :] * rel, axis=-1)
        bl = b[:, c - 1:c, :]
        o = _hbdot(scores, vc) + _hbdot(qc * jnp.exp(b), st, BNT)
        st = st * jnp.exp(bl) + _hbdot(vc, kc * jnp.exp(bl - b), BTN)
        outs.append(o)
    return jnp.concatenate(outs, axis=1), st


def _hgrn_core_fwd(q, k, lf, p):
    t = q.shape[0]
    hd = HEAD_DIM
    rs = min(HGRN_STEP, t)
    n = t // rs

    def body(q_ref, k_ref, lf_ref, v_ref, o_ref, st_ref, s_ref):
        @pl.when(pl.program_id(0) == 0)
        def _():
            s_ref[...] = jnp.zeros_like(s_ref)

        s = s_ref[...]
        st_ref[:, 0] = s
        o, s_new = _hgrn_step(s, *[_heads_major(r) for r in (q_ref, k_ref, lf_ref, v_ref)])
        for h in range(HEADS):
            o_ref[:, h * hd:(h + 1) * hd] = o[h]
        s_ref[...] = s_new

    tile = pl.BlockSpec((rs, GDN_WIDTH), lambda i: (i, 0))
    return pl.pallas_call(
        body, name="hgrn_core_fwd", grid=(n,),
        in_specs=[tile, tile, tile, pl.BlockSpec((rs, GDN_WIDTH), lambda i: (i, OFF_IB // GDN_WIDTH))],
        out_specs=[tile, pl.BlockSpec((HEADS, 1, hd, hd), lambda i: (0, i, 0, 0))],
        out_shape=[_sds((t, GDN_WIDTH), f32), _sds((HEADS, n, hd, hd), f32)],
        scratch_shapes=[pltpu.VMEM((HEADS, hd, hd), f32)],
        compiler_params=_cparams(("arbitrary",)))(q, k, lf, p)


def _hgrn_core_bwd(q, k, lf, p, states, do):
    t = q.shape[0]
    hd = HEAD_DIM
    rs = min(HGRN_STEP, t)
    n = t // rs

    def body(q_ref, k_ref, lf_ref, v_ref, st_ref, do_ref, dq_ref, dk_ref, dlf_ref, dv_ref, ds_ref):
        @pl.when(pl.program_id(0) == 0)
        def _():
            ds_ref[...] = jnp.zeros_like(ds_ref)

        _, vjp = jax.vjp(_hgrn_step, st_ref[:, 0], *[_heads_major(r) for r in (q_ref, k_ref, lf_ref, v_ref)])
        ds, *dins = vjp((_heads_major(do_ref), ds_ref[...]))
        ds_ref[...] = ds
        for d_ref, d in zip((dq_ref, dk_ref, dlf_ref, dv_ref), dins):
            for h in range(HEADS):
                d_ref[:, h * hd:(h + 1) * hd] = d[h].astype(d_ref.dtype)

    tile = pl.BlockSpec((rs, GDN_WIDTH), lambda i: (n - 1 - i, 0))
    return pl.pallas_call(
        body, name="hgrn_core_bwd", grid=(n,),
        in_specs=[tile, tile, tile, pl.BlockSpec((rs, GDN_WIDTH), lambda i: (n - 1 - i, OFF_IB // GDN_WIDTH)),
                  pl.BlockSpec((HEADS, 1, hd, hd), lambda i: (0, n - 1 - i, 0, 0)), tile],
        out_specs=[tile] * 4, out_shape=[_sds((t, GDN_WIDTH), f32)] * 3 + [_sds((t, GDN_WIDTH), bf16)],
        scratch_shapes=[pltpu.VMEM((HEADS, hd, hd), f32)],
        compiler_params=_cparams(("arbitrary",)))(q, k, lf, p, states, do)


def _row(v):
    return v.reshape(1, -1)


def _anchored(w, row, key):
    tok = w.get(key)
    return row if tok is None else row + tok[0, 0]


def _pad_lanes(v, n=HEAD_DIM):
    return jnp.pad(v.reshape(1, -1), ((0, 0), (0, n - v.shape[-1])))


def _ffn_fwd(x, w, l):
    h = _rms_fwd(x, _anchored(w, _row(w['norm_ffn'][l]), ('fwdf', l)), "ffn_norm")
    u = _mm(h, w['ffn_w_up'][l], tb=True, name="ffn_up")
    a = _ffn_act_fwd(u, w['ffn_conv_w'][l], _row(w['ffn_conv_b'][l]))
    y = _mm(a, w['ffn_w_down'][l], add=x, name="ffn_down")
    return y, (x, h, u)


def _ffn_bwd(saved, w, l, dy, dyb, grads):
    x, h, u = saved
    da = _mm(dyb, w['ffn_w_down'][l], tb=True, name="ffn_down_dx")
    a, dg, dv, dcw, dcb = _ffn_act_bwd(u, w['ffn_conv_w'][l], _anchored(w, _row(w['ffn_conv_b'][l]), ('bwd', l)), da)
    grads['ffn_w_down'][l] = _mm(a, dyb, ta=True, out_dtype=bf16, name="ffn_down_dw")
    du = jnp.concatenate([dg, dv], axis=1)
    grads['ffn_w_up'][l] = _mm(du, h, ta=True, out_dtype=bf16, name="ffn_up_dw")
    dh = _mm(du, w['ffn_w_up'][l], name="ffn_up_dx")
    dx, dxb, dgain = _rms_bwd(x, _row(w['norm_ffn'][l]), dh, dy, "ffn_norm_bwd")
    grads['ffn_conv_w'][l] = dcw
    grads['ffn_conv_b'][l] = dcb[0]
    grads['norm_ffn'][l] = dgain[0]
    return dx, dxb


def _odd_fwd(x, w, l, j):
    h = _rms_fwd(x, _anchored(w, _row(w['norm_mix'][l]), ('fwd', l)), "mix_norm")
    p = _mm(h, w['c_w_in'][j], name="lru_in")
    xc = _col_conv_fwd(p, LRU_WIDTH, w['c_conv_w'][j], _row(w['c_conv_b'][j]), LRU_CONV, 256, "lru_conv_fwd")
    out, hs, a = _lru_fwd(p, xc, w['c_gate_a_w'][j], _row(w['c_gate_a_b'][j]), w['c_gate_x_w'][j],
                          _row(w['c_gate_x_b'][j]), _row(w['c_lambda'][j]))
    y = _mm(out, w['c_w_out'][j], add=x, name="lru_out")
    return y, (x, h, p, xc, out, hs, a)


def _odd_bwd(saved, w, l, j, dy, dyb, grads):
    x, h, p, xc, out, hs, a = saved
    dout = _mm(dyb, w['c_w_out'][j], tb=True, name="lru_out_dx")
    grads['c_w_out'][j] = _mm(out, dyb, ta=True, out_dtype=bf16, name="lru_out_dw")
    dyb_, da, du = _lru_bwd_scan(p, a, hs, dout)
    dxc, dwa, dwx, dba, dbx, dlam = _lru_bwd_gates(xc, da, du, w['c_gate_a_w'][j], _row(w['c_gate_a_b'][j]),
                                                   w['c_gate_x_w'][j], _row(w['c_gate_x_b'][j]), _row(w['c_lambda'][j]))
    dxb_, dcw, dcb = _col_conv_bwd(p, LRU_WIDTH, w['c_conv_w'][j], dxc, LRU_CONV, 256, "lru_conv_bwd")
    dp = jnp.concatenate([dyb_, dxb_], axis=1)
    grads['c_w_in'][j] = _mm(h, dp, ta=True, out_dtype=bf16, name="lru_in_dw")
    dh = _mm(dp, w['c_w_in'][j], tb=True, name="lru_in_dx")
    dx, dxb, dgain = _rms_bwd(x, _row(w['norm_mix'][l]), dh, dy, "mix_norm_bwd")
    grads['c_gate_a_w'][j], grads['c_gate_x_w'][j] = dwa, dwx
    grads['c_gate_a_b'][j], grads['c_gate_x_b'][j], grads['c_lambda'][j] = dba[0], dbx[0], dlam[0]
    grads['c_conv_w'][j], grads['c_conv_b'][j] = dcw, dcb[0]
    grads['norm_mix'][l] = dgain[0]
    return dx, dxb


def _even_fwd(x, w, l, j):
    h = _rms_fwd(x, _anchored(w, _row(w['norm_mix'][l]), ('fwd', l)), "mix_norm")
    p = _mm(h, w['ab_w_in'][j], name="ab_in")
    alog, dtb = _pad_lanes(w['gdn_a_log'][j]), _pad_lanes(w['gdn_dt_bias'][j])
    q, k, v, be, ge = _gdn_pre_fwd(p, w['gdn_conv_w'][j], alog, dtb)
    oa, *sa = _gdn_core_fwd(q, k, v, ge, be)
    ya = _post_fwd(oa, p, OFF_Z, _row(w['gdn_norm'][j]), "gdn_post_fwd")
    qq, kk, lf = _hgrn_pre_fwd(p, w['hgrn_lower_bounds'], j)
    ob, sb = _hgrn_core_fwd(qq, kk, lf, p)
    yb = _post_fwd(ob, p, OFF_GB, _row(w['hgrn_norm'][j]), "hgrn_post_fwd")
    mix = jnp.concatenate([ya, yb], axis=1)
    y = _mm(mix, w['ab_w_out'][j], add=x, name="ab_out")
    return y, (x, h, p, q, k, v, be, ge, oa, sa, qq, kk, lf, ob, sb, mix)


def _even_bwd(saved, w, l, j, dy, dyb, grads):
    x, h, p, q, k, v, be, ge, oa, sa, qq, kk, lf, ob, sb, mix = saved
    alog, dtb = _pad_lanes(w['gdn_a_log'][j]), _pad_lanes(w['gdn_dt_bias'][j])
    dmix = _mm(dyb, w['ab_w_out'][j], tb=True, name="ab_out_dx")
    grads['ab_w_out'][j] = _mm(mix, dyb, ta=True, out_dtype=bf16, name="ab_out_dw")
    doa, dz, dgn = _post_bwd(oa, p, OFF_Z, _anchored(w, _row(w['gdn_norm'][j]), ('bwdm', l)), dmix, 0, "gdn_post_bwd")
    dob, dgb, dhn = _post_bwd(ob, p, OFF_GB, _row(w['hgrn_norm'][j]), dmix, GDN_WIDTH, "hgrn_post_bwd")
    dq, dk, dv, dge, dbe = _gdn_core_bwd(q, k, v, ge, be, sa, doa)
    dpq, dpk, dpv, dba, dwq, dwk, dwv, dal, ddt = _gdn_pre_bwd(p, w['gdn_conv_w'][j], alog, dtb, dq, dk, dv, dbe, dge)
    dqq, dkk, dlf, dib = _hgrn_core_bwd(qq, kk, lf, p, sb, dob)
    dqb, dfb, dlb = _hgrn_pre_bwd(p, w['hgrn_lower_bounds'], j, dqq, dkk, dlf)
    dp = jnp.concatenate([dpq, dpk, dpv, dz, dqb, dfb, dib, dgb, dba.astype(bf16)], axis=1)
    grads['ab_w_in'][j] = _mm(h, dp, ta=True, out_dtype=bf16, name="ab_in_dw")
    dh = _mm(dp, w['ab_w_in'][j], tb=True, name="ab_in_dx")
    dx, dxb, dgain = _rms_bwd(x, _row(w['norm_mix'][l]), dh, dy, "mix_norm_bwd")
    grads['gdn_conv_w'][j] = jnp.concatenate([dwq, dwk, dwv], axis=1)
    grads['gdn_a_log'][j], grads['gdn_dt_bias'][j] = dal[0, :HEADS], ddt[0, :HEADS]
    grads['gdn_norm'][j], grads['hgrn_norm'][j] = dgn[0], dhn[0]
    grads['hgrn_lower_bounds'].append(dlb)
    grads['norm_mix'][l] = dgain[0]
    return dx, dxb


def _ab_permute(w_in):
    pad = jnp.zeros(w_in.shape[:-1] + (AB_PAD - AB_COLS,), w_in.dtype)
    return jnp.concatenate([w_in[..., :2048], w_in[..., 2056:], w_in[..., 2048:2056], pad], axis=-1)


def _ab_unpermute(g):
    return jnp.concatenate([g[..., :2048], g[..., 4096:4104], g[..., 2048:4096]], axis=-1)


def _block_pad(a, axis, nblk, padded):
    axis = axis % a.ndim
    s = a.shape
    a = a.reshape(s[:axis] + (nblk, s[axis] // nblk) + s[axis + 1:])
    pad = [(0, 0)] * a.ndim
    pad[axis + 1] = (0, padded - s[axis] // nblk)
    return jnp.pad(a, pad).reshape(s[:axis] + (nblk * padded,) + s[axis + 1:])


def _block_unpad(a, axis, nblk, width):
    axis = axis % a.ndim
    s = a.shape
    a = a.reshape(s[:axis] + (nblk, s[axis] // nblk) + s[axis + 1:])
    a = lax.slice_in_dim(a, 0, width, axis=axis + 1)
    return a.reshape(s[:axis] + (nblk * width,) + s[axis + 1:])


def _kernel_layout(w):
    w = dict(w)
    w['ab_w_in'] = _ab_permute(w['ab_w_in'])
    w['ffn_w_up'] = jnp.swapaxes(_block_pad(w['ffn_w_up'], 2, N_DEV, FF_PAD), 1, 2)
    w['ffn_w_down'] = _block_pad(w['ffn_w_down'], 1, 4, FF_PAD)
    w['ffn_conv_w'] = _block_pad(w['ffn_conv_w'], 2, 4, FF_PAD)
    w['ffn_conv_b'] = _block_pad(w['ffn_conv_b'], 1, 4, FF_PAD)
    return w


def _natural_grads(g):
    g = dict(g)
    g['ab_w_in'] = _ab_unpermute(g['ab_w_in'])
    g['ffn_w_up'] = _block_unpad(jnp.swapaxes(g['ffn_w_up'], 1, 2), 2, N_DEV, FF_SHARD)
    g['ffn_w_down'] = _block_unpad(g['ffn_w_down'], 1, 4, FF_SHARD)
    g['ffn_conv_w'] = _block_unpad(g['ffn_conv_w'], 2, 4, FF_SHARD)
    g['ffn_conv_b'] = _block_unpad(g['ffn_conv_b'], 1, 4, FF_SHARD)
    return g


def _local_step(x, target, w, fetch=None, push=None):
    grads = {n: [None] * (DEPTH if n in ('norm_mix', 'norm_ffn') or n.startswith('ffn_') else 2)
             for n in WEIGHTS if n not in ('norm_final', 'hgrn_lower_bounds')}
    grads['hgrn_lower_bounds'] = []
    saved = []
    for l in range(DEPTH):
        j = l // 2
        if fetch is not None:
            fetch(l, x, 'mix')
        x, s_mix = (_even_fwd if l % 2 == 0 else _odd_fwd)(x, w, l, j)
        if fetch is not None:
            fetch(l, x, 'ffn')
        x, s_ffn = _ffn_fwd(x, w, l)
        saved.append((s_mix, s_ffn))
    loss, dx, dxb, dgf = _loss_head(x, _row(w['norm_final']), target)
    for l in reversed(range(DEPTH)):
        j = l // 2
        s_mix, s_ffn = saved[l]
        dx, dxb = _ffn_bwd(s_ffn, w, l, dx, dxb, grads)
        if push is not None:
            push(l, 'ffn', {nm: grads[nm].pop(li) for nm, li in _layer_items(l)[-2:]})
        dx, dxb = (_even_bwd if l % 2 == 0 else _odd_bwd)(s_mix, w, l, j, dx, dxb, grads)
        if push is not None:
            push(l, 'mix', {nm: grads[nm].pop(li) for nm, li in _layer_items(l)[:-2]})
    out = {n: jnp.stack(g) for n, g in grads.items() if n != 'hgrn_lower_bounds' and g}
    out['hgrn_lower_bounds'] = grads['hgrn_lower_bounds'][0] + grads['hgrn_lower_bounds'][1]
    out['norm_final'] = dgf[0]
    return loss[0, 0], dx, out


def _position():
    return lax.axis_index("x"), lax.axis_index("y"), lax.axis_index("c")


BLOCK_LAYOUT = {
    'ab_w_in': ((2, D_MODEL, N_DEV * AB_SHARD_PAD), (2, D_MODEL, AB_SHARD_PAD)),
    'ab_w_out': ((2, N_DEV, 128, D_MODEL), (2, 128, D_MODEL)),
    'c_w_in': ((2, D_MODEL, 2 * LRU_WIDTH), (2, D_MODEL, 256)),
    'c_w_out': ((2, N_DEV, 128, D_MODEL), (2, 128, D_MODEL)),
    'c_gate_a_w': ((2, HEADS, N_DEV, 32, LRU_BLOCK), (2, HEADS, 32, LRU_BLOCK)),
    'c_gate_x_w': ((2, HEADS, N_DEV, 32, LRU_BLOCK), (2, HEADS, 32, LRU_BLOCK)),
    'ffn_w_up': ((DEPTH, N_DEV, FF_PAD, D_MODEL), (DEPTH, FF_PAD, D_MODEL)),
    'ffn_w_down': ((DEPTH, 4, FF_PAD, D_MODEL), (DEPTH, FF_ROWS, D_MODEL)),
}


COL_WINDOW = {'ab_w_in': AB_SHARD_PAD, 'c_w_in': 256}


def _block_index(name, p):
    d = 4 * p[0] + 2 * p[1] + p[2]
    if name in COL_WINDOW:
        return (slice(None), pl.ds(pl.multiple_of(d * COL_WINDOW[name], 128), COL_WINDOW[name]))
    if name == 'ffn_w_down':
        return (2 * p[0] + p[1], pl.ds(pl.multiple_of(p[2] * FF_ROWS, 16), FF_ROWS), slice(None))
    if name in ('c_gate_a_w', 'c_gate_x_w'):
        return (slice(None), d)
    return (d,)


def _block_of(name, ref, p, layered=True):
    idx = _block_index(name, p)
    if layered and name in BLOCK_LAYOUT:
        idx = (slice(None),) + idx
    return ref.at[idx]


def _layer_items(l):
    j = l // 2
    mix = ([('ab_w_in', j), ('ab_w_out', j)] if l % 2 == 0 else
           [('c_w_in', j), ('c_w_out', j), ('c_gate_a_w', j), ('c_gate_x_w', j)])
    return mix + [('ffn_w_up', l), ('ffn_w_down', l)]


def _place_own(items, shards, posv, name):
    n = len(items)
    down = [i for i, (nm, _) in enumerate(items) if nm == 'ffn_w_down']
    in_specs, out_specs, out_shapes, operands = [], [], [], []
    for nm, li in items:
        sh = shards[nm]
        shard_shape = sh.shape if li is None else sh.shape[1:]
        z = (0,) * len(shard_shape)
        operands.append(sh)
        in_specs.append(pl.BlockSpec(shard_shape, lambda i, d, q, c, z=z: z) if li is None else
                        pl.BlockSpec((1,) + shard_shape, lambda i, d, q, c, li=li, z=z: (li,) + z))
        out_shapes.append(_sds(BLOCK_LAYOUT[nm][0][1:] if nm in BLOCK_LAYOUT else (N_DEV,) + sh.shape, sh.dtype))
        if nm in COL_WINDOW:
            out_specs.append(pl.BlockSpec(shard_shape, lambda i, d, q, c: (0, d[0])))
        elif nm == 'ffn_w_down':
            out_specs.append(pl.BlockSpec((1,) + shard_shape, lambda i, d, q, c: (q[0], c[0], 0)))
        elif nm in ('c_gate_a_w', 'c_gate_x_w'):
            out_specs.append(pl.BlockSpec((HEADS, 1) + shard_shape[1:], lambda i, d, q, c: (0, d[0], 0, 0)))
        else:
            out_specs.append(pl.BlockSpec((1,) + shard_shape, lambda i, d, q, c, z=z: (d[0],) + z))

    def body(d_ref, q_ref, c_ref, *refs):
        for i, (nm, li) in enumerate(items):
            v = refs[i][...] if li is None else refs[i][0]
            o_ref = refs[n + len(down) + i]
            if nm in COL_WINDOW:
                o_ref[...] = v
            elif nm in ('c_gate_a_w', 'c_gate_x_w'):
                o_ref[:, 0] = v
            else:
                o_ref[0] = v

    zeros = [jnp.zeros(out_shapes[i].shape, out_shapes[i].dtype) for i in down]
    return pl.pallas_call(
        body, name=name, out_shape=out_shapes,
        grid_spec=pltpu.PrefetchScalarGridSpec(
            num_scalar_prefetch=3, grid=(1,), in_specs=in_specs + [pl.BlockSpec(memory_space=pl.ANY)] * len(down),
            out_specs=out_specs),
        input_output_aliases={3 + n + k: i for k, i in enumerate(down)},
        compiler_params=_cparams(("arbitrary",)))(*posv, *operands, *zeros)


def _src_of(shard_ref, li):
    return shard_ref if li is None else shard_ref.at[li]


def _gather_now(items, shards, lands):
    n = len(items)
    srcs = sorted({nm for nm, _ in items})

    def body(*refs):
        ins = dict(zip(srcs, refs[:len(srcs)]))
        outs = refs[len(srcs) + n:len(srcs) + 2 * n]
        send_sems, recv_sems = refs[len(srcs) + 2 * n:]
        x, y, c = _position()
        me, sibling = (x, y, c), (x, y, 1 - c)
        chips = [(1 - x, y), (x, 1 - y), (1 - x, 1 - y)]

        def copy(i, k, block, to, own=False):
            nm, li = items[i]
            dst = _block_of(nm, outs[i], block, layered=False)
            return pltpu.make_async_remote_copy(
                src_ref=_src_of(ins[nm], li) if own else dst, dst_ref=dst, send_sem=send_sems.at[7 * i + k],
                recv_sem=recv_sems.at[7 * i + k], device_id=to, device_id_type=MESH)

        first = []
        for i in range(n):
            first.append(copy(i, 0, me, sibling, own=True))
            first += [copy(i, 1 + j, me, (*chip, c), own=True) for j, chip in enumerate(chips)]
        for cp in first:
            cp.start()
        passed = []
        for j, chip in enumerate(chips):
            for i in range(n):
                copy(i, 1 + j, (*chip, c), me).wait_recv()
                fwd = copy(i, 4 + j, (*chip, c), sibling)
                fwd.start()
                passed.append(fwd)
        for i in range(n):
            copy(i, 0, sibling, me).wait_recv()
        for j, chip in enumerate(chips):
            for i in range(n):
                copy(i, 4 + j, (*chip, 1 - c), me).wait_recv()
        for cp in first + passed:
            cp.wait_send()

    any_spec = pl.BlockSpec(memory_space=pl.ANY)
    return pl.pallas_call(
        body, name="gather_first_layer", out_shape=[_sds(a.shape, a.dtype) for a in lands],
        in_specs=[any_spec] * (len(srcs) + n), out_specs=[any_spec] * n,
        input_output_aliases={len(srcs) + i: i for i in range(n)},
        scratch_shapes=[pltpu.SemaphoreType.DMA((7 * n,)), pltpu.SemaphoreType.DMA((7 * n,))],
    )(*[shards[nm] for nm in srcs], *lands)


FIRST_HOP = (1, 2, 4, 6)


def _lanes(name, land_ref, pos):
    if name == 'ffn_w_down':
        return [(FIRST_HOP, land_ref.at[pl.ds(0, 2), pl.ds(0, 2 * FF_ROWS)])]
    if name in COL_WINDOW:
        return [(FIRST_HOP, land_ref.at[:, pl.ds(0, 4 * COL_WINDOW[name])])]
    if name in ('c_gate_a_w', 'c_gate_x_w'):
        return [(FIRST_HOP, land_ref.at[:, pl.ds(0, 4)])]
    return [(FIRST_HOP, land_ref.at[pl.ds(0, 4)])]


def _n_lanes(items):
    return len(items)


def _gather_forward(items, lands, name):
    n = len(items)

    def body(*refs):
        outs = refs[n:2 * n]
        send_sems, recv_sems = refs[2 * n:]
        x, y, c = _position()
        chips = [(1 - x, y), (x, 1 - y), (1 - x, 1 - y)]
        copies, arrivals = [], []
        for i, (nm, _) in enumerate(items):
            for j, chip in enumerate(chips):
                mine = _block_of(nm, outs[i], (*chip, c), layered=False)
                theirs = _block_of(nm, outs[i], (*chip, 1 - c), layered=False)
                copies.append(pltpu.make_async_remote_copy(
                    src_ref=mine, dst_ref=mine, send_sem=send_sems.at[3 * i + j], recv_sem=recv_sems.at[3 * i + j],
                    device_id=(x, y, 1 - c), device_id_type=MESH))
                arrivals.append(pltpu.make_async_remote_copy(
                    src_ref=theirs, dst_ref=theirs, send_sem=send_sems.at[3 * i + j], recv_sem=recv_sems.at[3 * i + j],
                    device_id=(x, y, 1 - c), device_id_type=MESH))
        for cp in copies:
            cp.start()
        for cp in arrivals:
            cp.wait_recv()
        for cp in copies:
            cp.wait_send()

    any_spec = pl.BlockSpec(memory_space=pl.ANY)
    return pl.pallas_call(
        body, name=name, out_shape=[_sds(a.shape, a.dtype) for a in lands],
        in_specs=[any_spec] * n, out_specs=[any_spec] * n, input_output_aliases={i: i for i in range(n)},
        scratch_shapes=[pltpu.SemaphoreType.DMA((3 * n,)), pltpu.SemaphoreType.DMA((3 * n,))],
    )(*lands)


HBM_SPEC = pl.BlockSpec(memory_space=pltpu.HBM)
SEM_SPEC = pl.BlockSpec(memory_space=pltpu.SEMAPHORE)
SIDE_EFFECT = pltpu.SideEffectType.DATAFLOW_SIDE_EFFECTING


def _gather_start(items, shards, lands, token, name):
    n = len(items)
    srcs = sorted({nm for nm, _ in items})
    ns, nl = len(srcs), _n_lanes(items)

    def body(*refs):
        ins = dict(zip(srcs, refs[:ns]))
        land_refs = refs[ns:ns + n]
        sems = refs[ns + n + 1:ns + n + 1 + 2 * nl]
        x, y, c = _position()
        me = (x, y, c)
        lane = 0
        for i, (nm, li) in enumerate(items):
            for codes, _ in _lanes(nm, land_refs[i], me):
                for k in codes:
                    peer = (1 - x if (k >> 2) & 1 else x, 1 - y if (k >> 1) & 1 else y, 1 - c if k & 1 else c)
                    pltpu.make_async_remote_copy(
                        src_ref=_src_of(ins[nm], li), dst_ref=_block_of(nm, land_refs[i], me, layered=False),
                        send_sem=sems[2 * lane], recv_sem=sems[2 * lane + 1], device_id=peer, device_id_type=MESH).start()
                lane += 1
        refs[-1][...] = jnp.zeros((8, 128), f32)

    hbm = [pltpu.with_memory_space_constraint(a, pltpu.HBM) for a in [shards[nm] for nm in srcs] + list(lands)]
    outs = pl.pallas_call(
        body, name=name,
        out_shape=[pltpu.SemaphoreType.DMA(())] * (2 * nl) + [pltpu.HBM(a.shape, a.dtype) for a in hbm] + [_sds((8, 128), f32)],
        in_specs=[HBM_SPEC] * (ns + n) + [pl.BlockSpec(memory_space=pl.ANY)],
        out_specs=[SEM_SPEC] * (2 * nl) + [HBM_SPEC] * (ns + n) + [pl.BlockSpec(memory_space=pltpu.VMEM)],
        input_output_aliases={i: 2 * nl + i for i in range(ns + n)},
        compiler_params=pltpu.CompilerParams(has_side_effects=SIDE_EFFECT),
    )(*hbm, token)
    return outs[:2 * nl], dict(zip(srcs, outs[2 * nl:2 * nl + ns])), outs[2 * nl + ns:-1], outs[-1]


def _gather_wait(items, sems, shards, lands, after, name):
    n = len(items)
    srcs = sorted(shards)
    ns, nl = len(srcs), _n_lanes(items)

    def body(*refs):
        land_refs = refs[ns:ns + n]
        sem_refs = refs[ns + n:ns + n + 2 * nl]
        x, y, c = _position()
        lane = 0
        for i, (nm, _) in enumerate(items):
            for _, moved in _lanes(nm, land_refs[i], (x, y, c)):
                cp = pltpu.make_async_remote_copy(
                    src_ref=moved, dst_ref=moved, send_sem=sem_refs[2 * lane], recv_sem=sem_refs[2 * lane + 1],
                    device_id=(x, y, 1 - c), device_id_type=MESH)
                cp.wait_send()
                cp.wait_recv()
                lane += 1

    outs = pl.pallas_call(
        body, name=name, out_shape=[pltpu.HBM(shards[nm].shape, shards[nm].dtype) for nm in srcs]
        + [pltpu.HBM(a.shape, a.dtype) for a in lands],
        in_specs=[HBM_SPEC] * (ns + n) + [SEM_SPEC] * (2 * nl) + [pl.BlockSpec(memory_space=pl.ANY)],
        out_specs=[HBM_SPEC] * (ns + n), input_output_aliases={i: i for i in range(ns + n)},
        compiler_params=pltpu.CompilerParams(has_side_effects=SIDE_EFFECT),
    )(*[shards[nm] for nm in srcs], *lands, *sems, after)
    return dict(zip(srcs, outs[:ns])), outs[ns:]


def _exchange_grads(fulls, rep):
    cpos = lax.axis_index("c").astype(jnp.int32).reshape(1)
    pair, rep_pair = _pair_exchange(fulls, rep)
    chip = {nm: _chip_sum(nm, fulls[nm], pair[nm], cpos) for nm in fulls}
    rep_chip = _add_pair(rep, rep_pair, "chip_sum_replicated")
    cross, cross_rep = _cross_exchange(chip, rep_chip)
    return chip, rep_chip, cross, cross_rep


def _pair_exchange(fulls, rep, tag=""):
    names = list(fulls)
    n = len(names)
    shard_shape = {nm: ((fulls[nm].shape[0],) + BLOCK_LAYOUT[nm][1][1:] if nm in BLOCK_LAYOUT else fulls[nm].shape[1:])
                   for nm in names}

    def body(*refs):
        ins = dict(zip(names, refs[:n]))
        rep_ref = refs[n]
        pair = dict(zip(names, refs[n + 1:2 * n + 1]))
        rpair_ref = refs[2 * n + 1]
        send_sems, recv_sems = refs[2 * n + 2:]
        x, y, c = _position()
        sibling = (x, y, 1 - c)
        remote = []
        for i, nm in enumerate(names):
            for q in range(4):
                remote.append(pltpu.make_async_remote_copy(
                    src_ref=_block_of(nm, ins[nm], (q >> 1, q & 1, 1 - c)), dst_ref=pair[nm].at[q],
                    send_sem=send_sems.at[4 * i + q], recv_sem=recv_sems.at[4 * i + q], device_id=sibling,
                    device_id_type=MESH))
        remote.append(pltpu.make_async_remote_copy(
            src_ref=rep_ref, dst_ref=rpair_ref, send_sem=send_sems.at[4 * n], recv_sem=recv_sems.at[4 * n],
            device_id=sibling, device_id_type=MESH))
        for cp in remote:
            cp.start()
        for cp in remote:
            cp.wait_recv()
        for cp in remote:
            cp.wait_send()

    any_spec = pl.BlockSpec(memory_space=pl.ANY)
    four = [_sds((4,) + tuple(shard_shape[nm]), fulls[nm].dtype) for nm in names]
    outs = pl.pallas_call(
        body, name="grad_pair_exchange" + tag, out_shape=four + [_sds(rep.shape, rep.dtype)],
        in_specs=[any_spec] * (n + 1), out_specs=[any_spec] * (n + 1),
        scratch_shapes=[pltpu.SemaphoreType.DMA((4 * n + 1,)), pltpu.SemaphoreType.DMA((4 * n + 1,))],
    )(*[fulls[nm] for nm in names], rep)
    return dict(zip(names, outs[:n])), outs[n]


def _chip_sum(name, full, pair, cpos, tag=""):
    if name in COL_WINDOW:
        width = BLOCK_LAYOUT[name][1][-1]
        rows = full.shape[0] * full.shape[1]
        tr = 512

        def body(c_ref, f_ref, p_ref, o_ref):
            o_ref[0] = (f_ref[...].astype(f32) + p_ref[0].astype(f32)).astype(o_ref.dtype)

        slot = pl.BlockSpec((1, tr, width), lambda q, i, c: (q, i, 0))
        out = pl.pallas_call(
            body, name="chip_sum_" + name + tag, out_shape=_sds((4, rows, width), full.dtype),
            grid_spec=pltpu.PrefetchScalarGridSpec(
                num_scalar_prefetch=1, grid=(4, rows // tr),
                in_specs=[pl.BlockSpec((tr, width), lambda q, i, c: (i, 2 * q + c[0])), slot], out_specs=slot),
            compiler_params=_cparams(("parallel", "parallel")))(
            cpos, full.reshape(rows, N_DEV * width), pair.reshape(4, rows, width))
        return out.reshape(pair.shape)

    if name == 'ffn_w_down':
        f4, p4 = full, pair
        fspec = pl.BlockSpec((full.shape[0], 1, FF_ROWS, D_MODEL), lambda q, c: (0, q, c[0], 0))
    else:
        shard = pair.shape[1:]
        lead = int(np.prod(shard[:-2]))
        f4 = full.reshape((lead, N_DEV) + shard[-2:])
        p4 = pair.reshape((4, lead) + shard[-2:])
        fspec = pl.BlockSpec((lead, 1) + shard[-2:], lambda q, c: (0, 2 * q + c[0], 0, 0))

    def body4(c_ref, f_ref, p_ref, o_ref):
        o_ref[0] = (f_ref[:, 0].astype(f32) + p_ref[0].astype(f32)).astype(o_ref.dtype)

    slot = pl.BlockSpec((1,) + p4.shape[1:], lambda q, c: (q, 0, 0, 0))
    out = pl.pallas_call(
        body4, name="chip_sum_" + name + tag, out_shape=_sds(p4.shape, full.dtype),
        grid_spec=pltpu.PrefetchScalarGridSpec(num_scalar_prefetch=1, grid=(4,), in_specs=[fspec, slot], out_specs=slot),
        compiler_params=_cparams(("parallel",)))(cpos, f4, p4)
    return out.reshape(pair.shape)


def _add_pair(a, b, name):
    shp = a.shape
    r, c = int(np.prod(shp[:-1])), shp[-1]
    tr = _tile(r, (512, 256, 128, 64, 32, 16, 8))

    def body(a_ref, b_ref, o_ref):
        o_ref[...] = (a_ref[...].astype(f32) + b_ref[...].astype(f32)).astype(o_ref.dtype)

    tile = pl.BlockSpec((tr, c), lambda i: (i, 0))
    return pl.pallas_call(body, name=name, grid=(r // tr,), in_specs=[tile, tile], out_specs=tile,
                          out_shape=_sds((r, c), a.dtype), compiler_params=_cparams(("parallel",)))(
        a.reshape(r, c), b.reshape(r, c)).reshape(shp)


def _cross_exchange(chip, rep_chip):
    names = list(chip)
    n = len(names)

    def body(*refs):
        ins = dict(zip(names, refs[:n]))
        rep_ref = refs[n]
        outs = dict(zip(names, refs[2 * n + 2:3 * n + 2]))
        rrep_ref = refs[3 * n + 2]
        send_sems, recv_sems = refs[3 * n + 3:]
        x, y, c = _position()
        mine = 2 * x + y
        copies = []
        for k in range(1, 4):
            px, py = (1 - x if (k >> 1) & 1 else x), (1 - y if k & 1 else y)
            for i, nm in enumerate(names + ['']):
                src = rep_ref if i == n else ins[nm].at[2 * px + py]
                dst = (rrep_ref if i == n else outs[nm]).at[mine]
                copies.append(pltpu.make_async_remote_copy(
                    src_ref=src, dst_ref=dst, send_sem=send_sems.at[3 * i + k - 1], recv_sem=recv_sems.at[3 * i + k - 1],
                    device_id=(px, py, c), device_id_type=MESH))
        for cp in copies:
            cp.start()
        for cp in copies:
            cp.wait_recv()
        for cp in copies:
            cp.wait_send()

    any_spec = pl.BlockSpec(memory_space=pl.ANY)
    shapes = [_sds(chip[nm].shape, chip[nm].dtype) for nm in names] + [_sds((4,) + rep_chip.shape, rep_chip.dtype)]
    zeros = [jnp.zeros(s.shape, s.dtype) for s in shapes]
    outs = pl.pallas_call(
        body, name="grad_cross_exchange", out_shape=shapes,
        in_specs=[any_spec] * (2 * n + 2), out_specs=[any_spec] * (n + 1),
        input_output_aliases={n + 1 + i: i for i in range(n + 1)},
        scratch_shapes=[pltpu.SemaphoreType.DMA((3 * (n + 1),)), pltpu.SemaphoreType.DMA((3 * (n + 1),))],
    )(*[chip[nm] for nm in names], rep_chip, *zeros)
    return dict(zip(names, outs[:n])), outs[n]


def _cross_start(chips, name):
    n = len(chips)

    def body(*refs):
        chip_refs, land_refs = refs[:n], refs[n:2 * n]
        sems = refs[2 * n:4 * n]
        x, y, c = _position()
        mine = 2 * x + y
        for i in range(n):
            for k in range(1, 4):
                px, py = (1 - x if (k >> 1) & 1 else x), (1 - y if k & 1 else y)
                pltpu.make_async_remote_copy(
                    src_ref=chip_refs[i].at[2 * px + py], dst_ref=land_refs[i].at[mine], send_sem=sems[2 * i],
                    recv_sem=sems[2 * i + 1], device_id=(px, py, c), device_id_type=MESH).start()
        refs[-1][...] = jnp.zeros((8, 128), f32)

    hbm = [pltpu.with_memory_space_constraint(a, pltpu.HBM) for a in list(chips) + [jnp.zeros(a.shape, a.dtype) for a in chips]]
    outs = pl.pallas_call(
        body, name=name,
        out_shape=[pltpu.SemaphoreType.DMA(())] * (2 * n) + [pltpu.HBM(a.shape, a.dtype) for a in hbm] + [_sds((8, 128), f32)],
        in_specs=[HBM_SPEC] * (2 * n),
        out_specs=[SEM_SPEC] * (2 * n) + [HBM_SPEC] * (2 * n) + [pl.BlockSpec(memory_space=pltpu.VMEM)],
        input_output_aliases={i: 2 * n + i for i in range(2 * n)},
        compiler_params=pltpu.CompilerParams(has_side_effects=SIDE_EFFECT),
    )(*hbm)
    return outs[:2 * n], outs[2 * n:3 * n], outs[3 * n:4 * n], outs[4 * n]


def _cross_wait(sems, chips, lands, after, name):
    n = len(chips)

    def body(*refs):
        land_refs = refs[n:2 * n]
        sem_refs = refs[2 * n:4 * n]
        x, y, c = _position()
        for i in range(n):
            moved = land_refs[i].at[pl.ds(0, 3)]
            cp = pltpu.make_async_remote_copy(
                src_ref=moved, dst_ref=moved, send_sem=sem_refs[2 * i], recv_sem=sem_refs[2 * i + 1],
                device_id=(x, y, 1 - c), device_id_type=MESH)
            cp.wait_send()
            cp.wait_recv()

    outs = pl.pallas_call(
        body, name=name, out_shape=[pltpu.HBM(a.shape, a.dtype) for a in list(chips) + list(lands)],
        in_specs=[HBM_SPEC] * (2 * n) + [SEM_SPEC] * (2 * n) + [pl.BlockSpec(memory_space=pl.ANY)],
        out_specs=[HBM_SPEC] * (2 * n), input_output_aliases={i: i for i in range(2 * n)},
        compiler_params=pltpu.CompilerParams(has_side_effects=SIDE_EFFECT),
    )(*chips, *lands, *sems, after)
    return outs[:n], outs[n:]


def _sum_adamw_layer(parts, own, mine, w, m, v, li, prev, name):
    nl, r, l = w.shape
    lp = parts.shape[2]
    tr = r if r <= 512 else _tile(r, (512, FF_ROWS, 256, 128))
    c1 = 1.0 / (1.0 - ADAM_B1 ** ADAM_STEP)
    c2 = 1.0 / (1.0 - ADAM_B2 ** ADAM_STEP)
    k = 0 if prev is None else 4

    def body(mine_ref, p_ref, o_ref, w_ref, m_ref, v_ref, *rest):
        g_ref, d_ref, nm_ref, nv_ref = rest[k:]
        mine_v = o_ref[0].astype(f32)
        g = jnp.where(mine_ref[0] == 0, mine_v, p_ref[0].astype(f32))
        for s in range(1, 4):
            g = g + jnp.where(mine_ref[0] == s, mine_v, p_ref[s].astype(f32))
        if lp != l:
            g = g[:, :l]
        m_new = ADAM_B1 * m_ref[0] + (1.0 - ADAM_B1) * g
        v_new = ADAM_B2 * v_ref[0] + (1.0 - ADAM_B2) * (g * g)
        g_ref[0] = g
        nm_ref[0] = m_new
        nv_ref[0] = v_new
        d_ref[0] = -ADAM_LR * ((m_new * c1) / (jnp.sqrt(v_new * c2) + ADAM_EPS) + ADAM_WD * w_ref[0])

    tile = pl.BlockSpec((1, tr, l), lambda i, mn: (li, i, 0))
    keep = [pl.BlockSpec(memory_space=pl.ANY)] * k
    return pl.pallas_call(
        body, name=name, out_shape=[_sds((nl, r, l), f32)] * 4,
        grid_spec=pltpu.PrefetchScalarGridSpec(
            num_scalar_prefetch=1, grid=(r // tr,),
            in_specs=[pl.BlockSpec((4, tr, lp), lambda i, mn: (0, i, 0)), pl.BlockSpec((1, tr, lp), lambda i, mn: (mn[0], i, 0)),
                      tile, tile, tile] + keep,
            out_specs=[tile] * 4),
        input_output_aliases={6 + i: i for i in range(k)},
        compiler_params=_cparams(("parallel",)))(mine, parts, own, w, m, v, *(prev or ()))


def _sum_adamw(parts, own, mine, w, m, v, name):
    r, l = w.shape
    lp = parts.shape[2]
    tr = _tile(r, (256, 128, 64, 32, 16, 8))
    c1 = 1.0 / (1.0 - ADAM_B1 ** ADAM_STEP)
    c2 = 1.0 / (1.0 - ADAM_B2 ** ADAM_STEP)

    def body(mine_ref, p_ref, o_ref, w_ref, m_ref, v_ref, g_ref, d_ref, nm_ref, nv_ref):
        mine_v = (o_ref[0] if own.ndim == 3 else o_ref[...]).astype(f32)
        g = jnp.where(mine_ref[0] == 0, mine_v, p_ref[0].astype(f32))
        for s in range(1, parts.shape[0]):
            g = g + jnp.where(mine_ref[0] == s, mine_v, p_ref[s].astype(f32))
        if lp != l:
            g = g[:, :l]
        m_new = ADAM_B1 * m_ref[...] + (1.0 - ADAM_B1) * g
        v_new = ADAM_B2 * v_ref[...] + (1.0 - ADAM_B2) * (g * g)
        g_ref[...] = g
        nm_ref[...] = m_new
        nv_ref[...] = v_new
        d_ref[...] = -ADAM_LR * ((m_new * c1) / (jnp.sqrt(v_new * c2) + ADAM_EPS) + ADAM_WD * w_ref[...])

    tile = pl.BlockSpec((tr, l), lambda i, mn: (i, 0))
    own_spec = (pl.BlockSpec((1, tr, lp), lambda i, mn: (mn[0], i, 0)) if own.ndim == 3
                else pl.BlockSpec((tr, lp), lambda i, mn: (i, 0)))
    return pl.pallas_call(
        body, name=name, out_shape=[_sds((r, l), f32)] * 4,
        grid_spec=pltpu.PrefetchScalarGridSpec(
            num_scalar_prefetch=1, grid=(r // tr,),
            in_specs=[pl.BlockSpec((parts.shape[0], tr, lp), lambda i, mn: (0, i, 0)), own_spec, tile, tile, tile],
            out_specs=[tile] * 4),
        compiler_params=_cparams(("parallel",)))(mine, parts, own, w, m, v)


def _pack(arrs, lead=None):
    if lead is None:
        flat = jnp.concatenate([a.reshape(-1).astype(f32) for a in arrs])
        n = flat.shape[0]
    else:
        flat = jnp.concatenate([a.reshape(lead, -1).astype(f32) for a in arrs], axis=1)
        n = flat.shape[1]
    tot = -(-n // 1024) * 1024
    if lead is None:
        return jnp.pad(flat, (0, tot - n)).reshape(tot // 128, 128)
    return jnp.pad(flat, ((0, 0), (0, tot - n))).reshape(lead, tot // 128, 128)


def _unpack(packed, shapes, lead=False):
    flat = packed.reshape(packed.shape[0], -1) if lead else packed.reshape(-1)
    out, off = [], 0
    for s in shapes:
        n = int(np.prod(s))
        out.append(flat[:, off:off + n].reshape((packed.shape[0],) + tuple(s)) if lead else flat[off:off + n].reshape(s))
        off += n
    return out


def _merge_shards(g, axis):
    g = jnp.moveaxis(g, 0, axis)
    s = g.shape
    return g.reshape(s[:axis] + (s[axis] * s[axis + 1],) + s[axis + 2:])


def _split_shards(full, axis):
    s = full.shape
    g = full.reshape(s[:axis] + (N_DEV, s[axis] // N_DEV) + s[axis + 1:])
    return jnp.moveaxis(g, axis, 0)


def kernel(x, norm_mix, norm_ffn, norm_final, ab_w_in, gdn_conv_w, gdn_a_log, gdn_dt_bias, gdn_norm, hgrn_lower_bounds, hgrn_norm, ab_w_out, c_w_in, c_conv_w, c_conv_b, c_gate_a_w, c_gate_a_b, c_gate_x_w, c_gate_x_b, c_lambda, c_w_out, ffn_w_up, ffn_conv_w, ffn_conv_b, ffn_w_down, loss_target, m_norm_mix, m_norm_ffn, m_norm_final, m_ab_w_in, m_gdn_conv_w, m_gdn_a_log, m_gdn_dt_bias, m_gdn_norm, m_hgrn_lower_bounds, m_hgrn_norm, m_ab_w_out, m_c_w_in, m_c_conv_w, m_c_conv_b, m_c_gate_a_w, m_c_gate_a_b, m_c_gate_x_w, m_c_gate_x_b, m_c_lambda, m_c_w_out, m_ffn_w_up, m_ffn_conv_w, m_ffn_conv_b, m_ffn_w_down, v_norm_mix, v_norm_ffn, v_norm_final, v_ab_w_in, v_gdn_conv_w, v_gdn_a_log, v_gdn_dt_bias, v_gdn_norm, v_hgrn_lower_bounds, v_hgrn_norm, v_ab_w_out, v_c_w_in, v_c_conv_w, v_c_conv_b, v_c_gate_a_w, v_c_gate_a_b, v_c_gate_x_w, v_c_gate_x_b, v_c_lambda, v_c_w_out, v_ffn_w_up, v_ffn_conv_w, v_ffn_conv_b, v_ffn_w_down):
    wl = dict(zip(WEIGHTS, (norm_mix, norm_ffn, norm_final, ab_w_in, gdn_conv_w, gdn_a_log, gdn_dt_bias, gdn_norm, hgrn_lower_bounds, hgrn_norm, ab_w_out, c_w_in, c_conv_w, c_conv_b, c_gate_a_w, c_gate_a_b, c_gate_x_w, c_gate_x_b, c_lambda, c_w_out, ffn_w_up, ffn_conv_w, ffn_conv_b, ffn_w_down)))
    ml = dict(zip(WEIGHTS, (m_norm_mix, m_norm_ffn, m_norm_final, m_ab_w_in, m_gdn_conv_w, m_gdn_a_log, m_gdn_dt_bias, m_gdn_norm, m_hgrn_lower_bounds, m_hgrn_norm, m_ab_w_out, m_c_w_in, m_c_conv_w, m_c_conv_b, m_c_gate_a_w, m_c_gate_a_b, m_c_gate_x_w, m_c_gate_x_b, m_c_lambda, m_c_w_out, m_ffn_w_up, m_ffn_conv_w, m_ffn_conv_b, m_ffn_w_down)))
    vl = dict(zip(WEIGHTS, (v_norm_mix, v_norm_ffn, v_norm_final, v_ab_w_in, v_gdn_conv_w, v_gdn_a_log, v_gdn_dt_bias, v_gdn_norm, v_hgrn_lower_bounds, v_hgrn_norm, v_ab_w_out, v_c_w_in, v_c_conv_w, v_c_conv_b, v_c_gate_a_w, v_c_gate_a_b, v_c_gate_x_w, v_c_gate_x_b, v_c_lambda, v_c_w_out, v_ffn_w_up, v_ffn_conv_w, v_ffn_conv_b, v_ffn_w_down)))

    big = [n for n in SHARDED if n in MATMUL_WEIGHTS]
    vec = [n for n in SHARDED if n not in MATMUL_WEIGHTS]
    shards = {n: wl[n].astype(bf16) for n in big}
    shards['ab_w_in'] = jnp.pad(shards['ab_w_in'], ((0, 0), (0, 0), (0, AB_SHARD_PAD - AB_SHARD)))
    shards['ffn_w_up'] = jnp.pad(jnp.swapaxes(shards['ffn_w_up'], 1, 2), ((0, 0), (0, FF_PAD - FF_SHARD), (0, 0)))
    shards['vec'] = _pack([wl[n] for n in vec])
    pos = _position()
    layer_items = [_layer_items(l) for l in range(DEPTH)]
    posv = [v.astype(jnp.int32).reshape(1) for v in (4 * pos[0] + 2 * pos[1] + pos[2], 2 * pos[0] + pos[1], pos[2])]
    first_items = layer_items[0] + [('vec', None)]
    lands = [_place_own(first_items, shards, posv, "place_own_0")]
    lands += [_place_own(layer_items[l], shards, posv, "place_own_%d" % l) for l in range(1, DEPTH)]
    mix0, ffn0 = layer_items[0][:-2], layer_items[0][-2:]
    first = _gather_now(mix0 + [('vec', None)], shards, lands[0][:len(mix0)] + lands[0][-1:])
    flight = {'shards': {n: shards[n] for n in big}}

    full = {n: wl[n] for n in REPLICATED}

    def start(items, item_lands, token, name, anchor):
        sems, thru, flight['lands'], tok = _gather_start(items, flight['shards'], item_lands, token, name)
        flight['sems'] = list(sems)
        flight['shards'].update(thru)
        full[anchor] = tok

    start(ffn0, lands[0][len(mix0):-1], first[-1], "gather_start_0", ('fwd', 0))

    for n, a in zip(vec, _unpack(first[-1], [wl[n].shape for n in vec], lead=True)):
        full[n] = _merge_shards(a, SHARD_AXIS[n])
    full['ffn_conv_w'] = _block_pad(full['ffn_conv_w'], 2, 4, FF_PAD)
    full['ffn_conv_b'] = _block_pad(full['ffn_conv_b'], 1, 4, FF_PAD)
    for n in big:
        full[n] = {}

    def arrive(items, x_in, tag):
        flight['shards'], got = _gather_wait(items, flight['sems'], flight['shards'], flight['lands'], x_in,
                                             "gather_wait_" + tag)
        return _gather_forward(items, got, "gather_forward_" + tag)

    def fetch(l, x_in, part):
        if l == 0 and part == 'mix':
            items, got = mix0, first[:len(mix0)]
        elif l == 0:
            items, got = ffn0, arrive(ffn0, x_in, "0")
            start(layer_items[1], lands[1], got[0], "gather_start_1", ('fwdf', 0))
        elif part == 'mix':
            items = layer_items[l]
            got = arrive(items, x_in, str(l))
            if l + 1 < DEPTH:
                start(layer_items[l + 1], lands[l + 1], got[0], "gather_start_%d" % (l + 1), ('fwd', l))
        else:
            return
        for (nm, li), a in zip(items, got):
            if nm == 'ab_w_in':
                a = _ab_permute(_block_unpad(a, 1, N_DEV, AB_SHARD))
            elif nm in ('ab_w_out', 'c_w_out'):
                a = a.reshape(D_MODEL, D_MODEL)
            elif nm in ('c_gate_a_w', 'c_gate_x_w'):
                a = a.reshape(HEADS, LRU_BLOCK, LRU_BLOCK)
            elif nm == 'ffn_w_down':
                a = a.reshape(D_FFP, D_MODEL)
            elif nm == 'ffn_w_up':
                a = a.reshape(2 * D_FFP, D_MODEL)
            full[nm][li] = a

    cpos = lax.axis_index("c").astype(jnp.int32).reshape(1)
    mine = (2 * lax.axis_index("x") + lax.axis_index("y")).astype(jnp.int32).reshape(1)
    pending = {}

    def exchange(key, items, g, anchor):
        fulls = {}
        for nm, _ in items:
            a = g[nm].astype(bf16)
            if nm == 'ab_w_in':
                a = _block_pad(_ab_unpermute(a), 1, N_DEV, AB_SHARD_PAD)
            fulls[nm] = a.reshape((1,) + BLOCK_LAYOUT[nm][0][1:])
        pair, _ = _pair_exchange(fulls, jnp.zeros((8, 128), f32), "_" + key)
        chips = [_chip_sum(nm, fulls[nm], pair[nm], cpos, "_" + key) for nm in fulls]
        sems, chips, crosses, tok = _cross_start(chips, "grad_cross_start_" + key)
        pending[key] = (items, sems, chips, crosses)
        full[anchor] = tok

    stash = {}

    def push(l, part, g):
        if l == 0:
            exchange("0f" if part == 'ffn' else "0", ffn0 if part == 'ffn' else mix0, g,
                     ('bwdm', 0) if part == 'ffn' else ('bwd', -1))
            return
        stash.update(g)
        if part == 'mix':
            exchange(str(l), layer_items[l], dict(stash), ('bwd', l - 1))
            stash.clear()

    loss, dx, grads = _local_step(x[0], loss_target[0], full, fetch, push)

    grads['ffn_conv_w'] = _block_unpad(grads['ffn_conv_w'], 2, 4, FF_SHARD)
    grads['ffn_conv_b'] = _block_unpad(grads['ffn_conv_b'], 1, 4, FF_SHARD)
    fulls = {'vec': _pack([_split_shards(grads[n], SHARD_AXIS[n]) for n in vec], lead=N_DEV)}
    chip, rep_chip, recv, rrep = _exchange_grads(fulls, _pack([grads[n] for n in REPLICATED]))
    res = {}
    stacked = {}
    after = full['bwd', -1]
    for key in ("3", "2", "1", "0f", "0"):
        items, sems, chips, lands = pending[key]
        chips, lands = _cross_wait(sems, chips, lands, after, "grad_cross_wait_" + key)
        for (n, li), own, parts in zip(items, chips, lands):
            if n == 'ffn_w_up':
                wmv = [jnp.swapaxes(a[n], 1, 2) for a in (wl, ml, vl)]
                rp = FF_PAD
            else:
                shp = wl[n].shape
                rp = int(np.prod(shp[1:-1]))
                wmv = [a[n].reshape(shp[0], rp, shp[-1]) for a in (wl, ml, vl)]
            stacked[n] = _sum_adamw_layer(parts.reshape(4, rp, -1), own.reshape(4, rp, -1), mine, *wmv, li,
                                          stacked.get(n), "adamw_%s_%d" % (n, li))
        if key == "0f":
            after = stacked['ffn_w_up'][0]
    for n in big:
        for kind, o in zip(("grad", "delta", "new_m", "new_v"), stacked[n]):
            res[kind, n] = jnp.swapaxes(o, 1, 2) if n == 'ffn_w_up' else o.reshape(wl[n].shape)
    for names, parts, own, tag in ((vec, recv['vec'], chip['vec'], "adamw_vectors"),
                                   (REPLICATED, rrep, rep_chip, "adamw_replicated")):
        outs = _sum_adamw(parts, own, mine, _pack([wl[n] for n in names]), _pack([ml[n] for n in names]),
                          _pack([vl[n] for n in names]), tag)
        for kind, o in zip(("grad", "delta", "new_m", "new_v"), outs):
            for n, a in zip(names, _unpack(o, [wl[n].shape for n in names])):
                res[kind, n] = a

    loss = lax.psum(loss, ("x", "y", "c"))
    return (loss, dx[None], *[res[kind, n] for kind in ("grad", "delta", "new_m", "new_v") for n in WEIGHTS])
```

```python
import functools

import numpy as np
import jax
import jax.numpy as jnp
from jax import lax
from jax.experimental import pallas as pl
from jax.experimental.pallas import tpu as pltpu

f32 = jnp.float32
bf16 = jnp.bfloat16
HI = lax.Precision.HIGHEST
MESH = pl.DeviceIdType.MESH

N_DEV = 8
D_MODEL = 1024
DEPTH = 4
EPS = 1e-6
F_FLOOR = 1e-30
HEADS = 4
HEAD_DIM = 128
GDN_WIDTH = 512
GDN_CONV = 4
GDN_CHUNK = 64
HGRN_CHUNK = 16
HGRN_STEP = 128
MIX_WIDTH = 1024
AB_COLS = 4104
AB_PAD = 4224
LRU_WIDTH = 1024
LRU_BLOCK = 256
LRU_CONV = 4
RG_C = 8.0
D_FF = 2816
FF_SHARD = 704
FF_PAD = 768
D_FFP = 4 * FF_PAD
FF_ROWS = 352
AB_SHARD, AB_SHARD_PAD = 513, 640
FFN_CONV = 3
ADAM_LR, ADAM_B1, ADAM_B2, ADAM_EPS, ADAM_WD, ADAM_STEP = 0.001, 0.9, 0.999, 1e-08, 0.01, 10
VMEM_LIMIT = 56 * 1024 * 1024
ROW_SLAB = 32

OFF_Q, OFF_K, OFF_V, OFF_Z, OFF_QB, OFF_FB, OFF_IB, OFF_GB, OFF_BA = 0, 512, 1024, 1536, 2048, 2560, 3072, 3584, 4096

WEIGHTS = ['norm_mix', 'norm_ffn', 'norm_final', 'ab_w_in', 'gdn_conv_w', 'gdn_a_log', 'gdn_dt_bias', 'gdn_norm',
           'hgrn_lower_bounds', 'hgrn_norm', 'ab_w_out', 'c_w_in', 'c_conv_w', 'c_conv_b', 'c_gate_a_w', 'c_gate_a_b',
           'c_gate_x_w', 'c_gate_x_b', 'c_lambda', 'c_w_out', 'ffn_w_up', 'ffn_conv_w', 'ffn_conv_b', 'ffn_w_down']
SHARD_AXIS = {'norm_mix': None, 'norm_ffn': None, 'norm_final': None, 'ab_w_in': 2, 'gdn_conv_w': 2, 'gdn_a_log': None,
              'gdn_dt_bias': None, 'gdn_norm': None, 'hgrn_lower_bounds': None, 'hgrn_norm': None, 'ab_w_out': 1,
              'c_w_in': 2, 'c_conv_w': 2, 'c_conv_b': 1, 'c_gate_a_w': 2, 'c_gate_a_b': 1, 'c_gate_x_w': 2,
              'c_gate_x_b': 1, 'c_lambda': 1, 'c_w_out': 1, 'ffn_w_up': 2, 'ffn_conv_w': 2, 'ffn_conv_b': None,
              'ffn_w_down': 1}
MATMUL_WEIGHTS = ('ab_w_in', 'ab_w_out', 'c_w_in', 'c_gate_a_w', 'c_gate_x_w', 'c_w_out', 'ffn_w_up', 'ffn_w_down')
SHARDED = [n for n in WEIGHTS if SHARD_AXIS[n] is not None]
REPLICATED = [n for n in WEIGHTS if SHARD_AXIS[n] is None]


def _tile(n, prefs=(512, 384, 256, 128)):
    for p in prefs:
        if n % p == 0:
            return p
    return n


def _cparams(sem=None):
    kw = dict(vmem_limit_bytes=VMEM_LIMIT)
    if sem is not None:
        kw['dimension_semantics'] = sem
    return pltpu.CompilerParams(**kw)


def _sds(shape, dtype):
    return jax.ShapeDtypeStruct(tuple(shape), dtype)


def _sigmoid(x):
    return 1.0 / (1.0 + jnp.exp(-x))


def _silu(x):
    return x * (0.5 * jnp.tanh(0.5 * x) + 0.5)


def _log1p(x):
    u = 1.0 + x
    return jnp.where(u == 1.0, x, jnp.log(u) * (x / jnp.where(u == 1.0, 1.0, u - 1.0)))


def _softplus(x):
    return jnp.maximum(x, 0.0) + _log1p(jnp.exp(-jnp.abs(x)))


def _expm1(x):
    small = jnp.abs(x) < 0.05
    xs = jnp.where(small, x, 0.0)
    series = xs * (1.0 + xs * (0.5 + xs * (1.0 / 6.0 + xs * (1.0 / 24.0 + xs * (1.0 / 120.0)))))
    return jnp.where(small, series, jnp.exp(x) - 1.0)


def _gelu(x):
    return 0.5 * x * (1.0 + jnp.tanh(0.7978845608028654 * (x + 0.044715 * x * x * x)))


def _rms(x, gain):
    return x * lax.rsqrt(jnp.mean(x * x, axis=-1, keepdims=True) + EPS) * gain


def _dot(a, b, dims=((1,), (0,)), precision=None):
    return lax.dot_general(a, b, (dims, ((), ())), precision=precision, preferred_element_type=f32)


def _bdot(a, b, dims=((1,), (0,))):
    return _dot(a.astype(bf16), b.astype(bf16), dims)


NT = ((1,), (1,))
TN = ((0,), (0,))


def _shift_down(x, k):
    if k == 0:
        return x
    row = lax.broadcasted_iota(jnp.int32, x.shape, 0)
    return jnp.where(row >= k, pltpu.roll(x, k, 0), 0.0)


def _shift_up(x, k, fill=0.0):
    if k == 0:
        return x
    n = x.shape[0]
    row = lax.broadcasted_iota(jnp.int32, x.shape, 0)
    return jnp.where(row < n - k, pltpu.roll(x, n - k, 0), fill)


def _conv_fwd(x, w_ref, width):
    acc = w_ref[width - 1:width, :] * x
    for k in range(width - 1):
        acc = acc + w_ref[k:k + 1, :] * _shift_down(x, width - 1 - k)
    return acc


def _conv_bwd(x, dout, w_ref, dw_ref, width):
    dx = w_ref[width - 1:width, :] * dout
    dw_ref[width - 1:width, :] = jnp.sum(dout * x, axis=0, keepdims=True)
    for k in range(width - 1):
        s = width - 1 - k
        dx = dx + w_ref[k:k + 1, :] * _shift_up(dout, s)
        dw_ref[k:k + 1, :] = jnp.sum(dout * _shift_down(x, s), axis=0, keepdims=True)
    return dx


MM_VMEM_BUDGET = 36 * 1024 * 1024
MM_MAX_TILE = 1024 * 1024


def _mm_tiles(m, n, k, out_bytes):
    best = None
    for tm in (1024, 512, 384, 256, 128):
        if m % tm:
            continue
        for tn in range(1536, 0, -128):
            if n % tn or tm * tn > MM_MAX_TILE:
                continue
            score = (tm * tn, min(tm, tn))
            if 2 * (tm * k * 2 + k * tn * 2 + tm * tn * out_bytes) <= MM_VMEM_BUDGET and (best is None or score > best[0]):
                best = (score, tm, tn)
    return (best[1], best[2]) if best else (_tile(m), _tile(n))


def _mm(a, b, *, ta=False, tb=False, add=None, out_dtype=f32, name):
    m, k = (a.shape[1], a.shape[0]) if ta else a.shape
    n = b.shape[0] if tb else b.shape[1]
    tm, tn = _mm_tiles(m, n, k, jnp.dtype(out_dtype).itemsize + (4 if add is not None else 0))
    dims = ((0 if ta else 1,), (1 if tb else 0,))

    def body(*refs):
        a_ref, b_ref = refs[0], refs[1]
        o_ref = refs[-1]
        r = _dot(a_ref[...], b_ref[...], dims)
        if add is not None:
            r = r + refs[2][...]
        o_ref[...] = r.astype(out_dtype)

    a_spec = pl.BlockSpec((k, tm), lambda j, i: (0, i)) if ta else pl.BlockSpec((tm, k), lambda j, i: (i, 0))
    b_spec = pl.BlockSpec((tn, k), lambda j, i: (j, 0)) if tb else pl.BlockSpec((k, tn), lambda j, i: (0, j))
    o_spec = pl.BlockSpec((tm, tn), lambda j, i: (i, j))
    ins, specs = [a, b], [a_spec, b_spec]
    if add is not None:
        ins.append(add)
        specs.append(o_spec)
    return pl.pallas_call(body, name=name, grid=(n // tn, m // tm), in_specs=specs, out_specs=o_spec,
                          out_shape=_sds((m, n), out_dtype), compiler_params=_cparams(("parallel", "parallel")))(*ins)


def _rms_fwd(x, gain, name):
    t, d = x.shape
    tr = _tile(t, (256, 128))

    def body(x_ref, g_ref, h_ref):
        h_ref[...] = _rms(x_ref[...], g_ref[...]).astype(bf16)

    return pl.pallas_call(body, name=name, grid=(t // tr,),
                          in_specs=[pl.BlockSpec((tr, d), lambda i: (i, 0)), pl.BlockSpec((1, d), lambda i: (0, 0))],
                          out_specs=pl.BlockSpec((tr, d), lambda i: (i, 0)), out_shape=_sds((t, d), bf16),
                          compiler_params=_cparams(("parallel",)))(x, gain)


def _rms_bwd(x, gain, dh, dres, name):
    t, d = x.shape
    tr = _tile(t, (256, 128))

    def body(x_ref, g_ref, dh_ref, dres_ref, dx_ref, dxb_ref, dg_ref):
        _, vjp = jax.vjp(_rms, x_ref[...], g_ref[...])
        dx, dg = vjp(dh_ref[...])
        dx = dx + dres_ref[...]
        dx_ref[...] = dx
        dxb_ref[...] = dx.astype(bf16)

        @pl.when(pl.program_id(0) == 0)
        def _():
            dg_ref[...] = jnp.zeros_like(dg_ref)

        dg_ref[...] += dg

    row = pl.BlockSpec((tr, d), lambda i: (i, 0))
    vec = pl.BlockSpec((1, d), lambda i: (0, 0))
    return pl.pallas_call(body, name=name, grid=(t // tr,), in_specs=[row, vec, row, row], out_specs=[row, row, vec],
                          out_shape=[_sds((t, d), f32), _sds((t, d), bf16), _sds((1, d), f32)],
                          compiler_params=_cparams(("arbitrary",)))(x, gain, dh, dres)


def _loss_head(x, gain, target):
    t, d = x.shape
    tr = _tile(t, (256, 128))

    def f(xv, g, tgt):
        err = _rms(xv, g) - tgt
        return 0.5 * jnp.sum(jnp.mean(err * err, axis=-1, keepdims=True), axis=0, keepdims=True)

    def body(x_ref, g_ref, t_ref, loss_ref, dx_ref, dxb_ref, dg_ref):
        loss, vjp = jax.vjp(lambda xv, g: f(xv, g, t_ref[...]), x_ref[...], g_ref[...])
        dx, dg = vjp(jnp.ones((1, 1), f32))
        dx_ref[...] = dx
        dxb_ref[...] = dx.astype(bf16)

        @pl.when(pl.program_id(0) == 0)
        def _():
            dg_ref[...] = jnp.zeros_like(dg_ref)
            loss_ref[...] = jnp.zeros_like(loss_ref)

        dg_ref[...] += dg
        loss_ref[...] += jnp.broadcast_to(loss, loss_ref.shape)

    row = pl.BlockSpec((tr, d), lambda i: (i, 0))
    vec = pl.BlockSpec((1, d), lambda i: (0, 0))
    one = pl.BlockSpec((8, 128), lambda i: (0, 0))
    return pl.pallas_call(body, name="loss_head", grid=(t // tr,), in_specs=[row, vec, row],
                          out_specs=[one, row, row, vec],
                          out_shape=[_sds((8, 128), f32), _sds((t, d), f32), _sds((t, d), bf16), _sds((1, d), f32)],
                          compiler_params=_cparams(("arbitrary",)))(x, gain, target)


def _ffn_act_fwd(u, conv_w, conv_b):
    t = u.shape[0]
    tc = FF_PAD // 2
    nb = D_FFP // tc

    def body(g_ref, v_ref, w_ref, b_ref, a_ref):
        gc = _conv_fwd(g_ref[...], w_ref, FFN_CONV) + b_ref[...]
        a_ref[...] = (_silu(gc) * v_ref[...]).astype(bf16)

    return pl.pallas_call(
        body, name="ffn_act_fwd", grid=(nb,),
        in_specs=[pl.BlockSpec((t, tc), lambda j: (0, j)), pl.BlockSpec((t, tc), lambda j: (0, j + nb)),
                  pl.BlockSpec((FFN_CONV, tc), lambda j: (0, j)), pl.BlockSpec((1, tc), lambda j: (0, j))],
        out_specs=pl.BlockSpec((t, tc), lambda j: (0, j)), out_shape=_sds((t, D_FFP), bf16),
        compiler_params=_cparams(("parallel",)))(u, u, conv_w, conv_b)


def _ffn_act_bwd(u, conv_w, conv_b, da):
    t = u.shape[0]
    tc = FF_PAD // 2
    nb = D_FFP // tc

    def act(gc, val):
        return _silu(gc) * val

    def body(g_ref, v_ref, w_ref, b_ref, da_ref, a_ref, dg_ref, dv_ref, dw_ref, db_ref, gc_ref):
        gp = g_ref[...]
        gc_ref[...] = _conv_fwd(gp, w_ref, FFN_CONV) + b_ref[...]

        def slab(i, carry):
            rows = pl.ds(pl.multiple_of(i * ROW_SLAB, ROW_SLAB), ROW_SLAB)
            a, vjp = jax.vjp(act, gc_ref[rows, :], v_ref[rows, :])
            dgc, dval = vjp(da_ref[rows, :])
            a_ref[rows, :] = a.astype(bf16)
            dv_ref[rows, :] = dval.astype(bf16)
            gc_ref[rows, :] = dgc
            return carry

        lax.fori_loop(0, t // ROW_SLAB, slab, 0, unroll=2)
        dgc = gc_ref[...]
        db_ref[...] = jnp.sum(dgc, axis=0, keepdims=True)
        dg_ref[...] = _conv_bwd(gp, dgc, w_ref, dw_ref, FFN_CONV).astype(bf16)

    col = pl.BlockSpec((t, tc), lambda j: (0, j))
    return pl.pallas_call(
        body, name="ffn_act_bwd", grid=(nb,),
        in_specs=[col, pl.BlockSpec((t, tc), lambda j: (0, j + nb)), pl.BlockSpec((FFN_CONV, tc), lambda j: (0, j)),
                  pl.BlockSpec((1, tc), lambda j: (0, j)), col],
        out_specs=[col, col, col, pl.BlockSpec((FFN_CONV, tc), lambda j: (0, j)), pl.BlockSpec((1, tc), lambda j: (0, j))],
        out_shape=[_sds((t, D_FFP), bf16), _sds((t, D_FFP), bf16), _sds((t, D_FFP), bf16), _sds((FFN_CONV, D_FFP), f32),
                   _sds((1, D_FFP), f32)],
        scratch_shapes=[pltpu.VMEM((t, tc), f32)],
        compiler_params=_cparams(("parallel",)))(u, u, conv_w, conv_b, da)


def _lru_gates(xc, ra, ia, lam):
    r = _sigmoid(ra)
    i = _sigmoid(ia)
    log_a = -RG_C * r * _softplus(-lam)
    a = jnp.exp(log_a)
    u = jnp.sqrt(jnp.maximum(-_expm1(2.0 * log_a), 0.0)) * (i * xc)
    return a, u


SCAN_BLOCK = 64


def _lin_scan(a_ref, u_ref, h_ref):
    n, c = a_ref.shape
    blk = min(SCAN_BLOCK, n)
    row = lax.broadcasted_iota(jnp.int32, (blk, c), 0)

    def step(i, h_prev):
        rows = pl.ds(pl.multiple_of(i * blk, blk), blk)
        a, u = a_ref[rows, :], u_ref[rows, :]
        s = 1
        while s < blk:
            keep = row >= s
            u = a * jnp.where(keep, pltpu.roll(u, s, 0), 0.0) + u
            a = a * jnp.where(keep, pltpu.roll(a, s, 0), 1.0)
            s *= 2
        h = u + a * h_prev
        h_ref[rows, :] = h
        return h[blk - 1:blk, :]

    lax.fori_loop(0, n // blk, step, jnp.zeros((1, c), f32))


def _rev_scan(an_ref, d_ref, g_ref):
    n, c = d_ref.shape
    blk = min(SCAN_BLOCK, n)
    row = lax.broadcasted_iota(jnp.int32, (blk, c), 0)

    def step(i, g_next):
        rows = pl.ds(pl.multiple_of((n // blk - 1 - i) * blk, blk), blk)
        a, d = an_ref[rows, :], d_ref[rows, :]
        s = 1
        while s < blk:
            keep = row < blk - s
            d = a * jnp.where(keep, pltpu.roll(d, blk - s, 0), 0.0) + d
            a = a * jnp.where(keep, pltpu.roll(a, blk - s, 0), 1.0)
            s *= 2
        g = d + a * g_next
        g_ref[rows, :] = g
        return g[0:1, :]

    lax.fori_loop(0, n // blk, step, jnp.zeros((1, c), f32))


def _col_conv_fwd(p, col_off, conv_w, conv_b, width, tc, name):
    t = p.shape[0]
    c = conv_w.shape[1]
    ob = col_off // tc

    def body(x_ref, w_ref, b_ref, o_ref):
        o_ref[...] = _conv_fwd(x_ref[...], w_ref, width) + b_ref[...]

    return pl.pallas_call(
        body, name=name, grid=(c // tc,),
        in_specs=[pl.BlockSpec((t, tc), lambda j: (0, j + ob)), pl.BlockSpec((width, tc), lambda j: (0, j)),
                  pl.BlockSpec((1, tc), lambda j: (0, j))],
        out_specs=pl.BlockSpec((t, tc), lambda j: (0, j)), out_shape=_sds((t, c), f32),
        compiler_params=_cparams(("parallel",)))(p, conv_w, conv_b)


def _col_conv_bwd(p, col_off, conv_w, dxc, width, tc, name):
    t = p.shape[0]
    c = conv_w.shape[1]
    ob = col_off // tc

    def body(x_ref, w_ref, d_ref, dx_ref, dw_ref, db_ref):
        d = d_ref[...]
        db_ref[...] = jnp.sum(d, axis=0, keepdims=True)
        dx_ref[...] = _conv_bwd(x_ref[...], d, w_ref, dw_ref, width).astype(bf16)

    col = pl.BlockSpec((t, tc), lambda j: (0, j))
    return pl.pallas_call(
        body, name=name, grid=(c // tc,),
        in_specs=[pl.BlockSpec((t, tc), lambda j: (0, j + ob)), pl.BlockSpec((width, tc), lambda j: (0, j)), col],
        out_specs=[col, pl.BlockSpec((width, tc), lambda j: (0, j)), pl.BlockSpec((1, tc), lambda j: (0, j))],
        out_shape=[_sds((t, c), bf16), _sds((width, c), f32), _sds((1, c), f32)],
        compiler_params=_cparams(("parallel",)))(p, conv_w, dxc)


def _lru_fwd(p, xc, wa, ba, wx, bx, lam):
    t = p.shape[0]
    bw = LRU_BLOCK

    def body(y_ref, xc_ref, wa_ref, ba_ref, wx_ref, bx_ref, lam_ref, out_ref, hs_ref, a_ref, u_ref):
        xc_v = xc_ref[...]
        xb = xc_v.astype(bf16)
        ra = _dot(xb, wa_ref[0]) + ba_ref[...]
        ia = _dot(xb, wx_ref[0]) + bx_ref[...]
        a_ref[...], u_ref[...] = _lru_gates(xc_v, ra, ia, lam_ref[...])
        _lin_scan(a_ref, u_ref, hs_ref)
        out_ref[...] = (hs_ref[...] * _gelu(y_ref[...])).astype(bf16)

    col = pl.BlockSpec((t, bw), lambda h: (0, h))
    vec = pl.BlockSpec((1, bw), lambda h: (0, h))
    mat = pl.BlockSpec((1, bw, bw), lambda h: (h, 0, 0))
    return pl.pallas_call(
        body, name="lru_fwd", grid=(HEADS,), in_specs=[col, col, mat, vec, mat, vec, vec], out_specs=[col, col, col],
        out_shape=[_sds((t, LRU_WIDTH), bf16), _sds((t, LRU_WIDTH), f32), _sds((t, LRU_WIDTH), f32)],
        scratch_shapes=[pltpu.VMEM((t, bw), f32)],
        compiler_params=_cparams(("parallel",)))(p, xc, wa, ba, wx, bx, lam)


def _lru_bwd_scan(p, a, hs, dout):
    t = p.shape[0]
    bw = LRU_BLOCK

    def body(y_ref, a_ref, hs_ref, do_ref, dy_ref, da_ref, du_ref, an_ref, d_ref):
        hs_v = hs_ref[...]
        do = do_ref[...]
        gate, vjp = jax.vjp(_gelu, y_ref[...])
        dy_ref[...] = vjp(do * hs_v)[0].astype(bf16)
        an_ref[...] = _shift_up(a_ref[...], 1)
        d_ref[...] = do * gate
        _rev_scan(an_ref, d_ref, du_ref)
        da_ref[...] = du_ref[...] * _shift_down(hs_v, 1)

    col = pl.BlockSpec((t, bw), lambda h: (0, h))
    return pl.pallas_call(
        body, name="lru_bwd_scan", grid=(HEADS,), in_specs=[col, col, col, col], out_specs=[col, col, col],
        out_shape=[_sds((t, LRU_WIDTH), bf16), _sds((t, LRU_WIDTH), f32), _sds((t, LRU_WIDTH), f32)],
        scratch_shapes=[pltpu.VMEM((t, bw), f32), pltpu.VMEM((t, bw), f32)],
        compiler_params=_cparams(("parallel",)))(p, a, hs, dout)


def _lru_bwd_gates(xc, da, du, wa, ba, wx, bx, lam):
    t = xc.shape[0]
    bw = LRU_BLOCK
    tr = _tile(t, (512, 256, 128))

    def body(xc_ref, da_ref, du_ref, wa_ref, ba_ref, wx_ref, bx_ref, lam_ref,
             dxc_ref, dwa_ref, dwx_ref, dba_ref, dbx_ref, dlam_ref):
        xc_v = xc_ref[...]
        xb = xc_v.astype(bf16)
        ra = _dot(xb, wa_ref[0]) + ba_ref[...]
        ia = _dot(xb, wx_ref[0]) + bx_ref[...]
        _, vjp = jax.vjp(_lru_gates, xc_v, ra, ia, lam_ref[...])
        dxc, dra, dia, dlam = vjp((da_ref[...], du_ref[...]))
        drb, dib = dra.astype(bf16), dia.astype(bf16)
        dxc_ref[...] = dxc + _dot(drb, wa_ref[0], NT) + _dot(dib, wx_ref[0], NT)

        @pl.when(pl.program_id(1) == 0)
        def _():
            dwa_ref[...] = jnp.zeros_like(dwa_ref)
            dwx_ref[...] = jnp.zeros_like(dwx_ref)
            dba_ref[...] = jnp.zeros_like(dba_ref)
            dbx_ref[...] = jnp.zeros_like(dbx_ref)
            dlam_ref[...] = jnp.zeros_like(dlam_ref)

        dwa_ref[0] += _dot(xb, drb, TN)
        dwx_ref[0] += _dot(xb, dib, TN)
        dba_ref[...] += jnp.sum(dra, axis=0, keepdims=True)
        dbx_ref[...] += jnp.sum(dia, axis=0, keepdims=True)
        dlam_ref[...] += dlam

    tile = pl.BlockSpec((tr, bw), lambda h, i: (i, h))
    vec = pl.BlockSpec((1, bw), lambda h, i: (0, h))
    mat = pl.BlockSpec((1, bw, bw), lambda h, i: (h, 0, 0))
    return pl.pallas_call(
        body, name="lru_bwd_gates", grid=(HEADS, t // tr), in_specs=[tile, tile, tile, mat, vec, mat, vec, vec],
        out_specs=[tile, mat, mat, vec, vec, vec],
        out_shape=[_sds((t, LRU_WIDTH), f32), _sds((HEADS, bw, bw), f32), _sds((HEADS, bw, bw), f32),
                   _sds((1, LRU_WIDTH), f32), _sds((1, LRU_WIDTH), f32), _sds((1, LRU_WIDTH), f32)],
        compiler_params=_cparams(("parallel", "arbitrary")))(xc, da, du, wa, ba, wx, bx, lam)


def _gdn_pre_fn(cq, ck, cv, ba, alog, dtb, h):
    q, k, v = _silu(cq), _silu(ck), _silu(cv)
    q = q * lax.rsqrt(jnp.sum(q * q, axis=-1, keepdims=True) + EPS) * (HEAD_DIM ** -0.5)
    k = k * lax.rsqrt(jnp.sum(k * k, axis=-1, keepdims=True) + EPS)
    lane = lax.broadcasted_iota(jnp.int32, (1, HEAD_DIM), 1)
    mb = (lane == h).astype(f32)
    ma = (lane == HEADS + h).astype(f32)
    beta_raw = jnp.sum(ba * mb, axis=-1, keepdims=True)
    alpha = jnp.sum(ba * ma, axis=-1, keepdims=True)
    al = jnp.sum(alog * mb, axis=-1, keepdims=True)
    db = jnp.sum(dtb * mb, axis=-1, keepdims=True)
    beta = _sigmoid(beta_raw)
    g = -jnp.exp(al) * _softplus(alpha + db)
    return q, k, v, jnp.broadcast_to(beta, q.shape), jnp.broadcast_to(g, q.shape)


def _gdn_pre_fwd(p, conv_w, alog, dtb):
    t = p.shape[0]
    hd = HEAD_DIM

    def body(pq_ref, pk_ref, pv_ref, ba_ref, wq_ref, wk_ref, wv_ref, al_ref, dt_ref, q_ref, k_ref, v_ref, b_ref, g_ref):
        h = pl.program_id(0)
        cq = _conv_fwd(pq_ref[...], wq_ref, GDN_CONV)
        ck = _conv_fwd(pk_ref[...], wk_ref, GDN_CONV)
        cv = _conv_fwd(pv_ref[...], wv_ref, GDN_CONV)
        q, k, v, be, ge = _gdn_pre_fn(cq, ck, cv, ba_ref[...], al_ref[...], dt_ref[...], h)
        q_ref[...], k_ref[...], v_ref[...], b_ref[...], g_ref[...] = q, k, v, be, ge

    def pcol(off):
        return pl.BlockSpec((t, hd), lambda h: (0, h + off // hd))

    def wcol(off):
        return pl.BlockSpec((GDN_CONV, hd), lambda h: (0, h + off // hd))

    vec = pl.BlockSpec((1, hd), lambda h: (0, 0))
    out = pl.BlockSpec((t, hd), lambda h: (0, h))
    return pl.pallas_call(
        body, name="gdn_pre_fwd", grid=(HEADS,),
        in_specs=[pcol(OFF_Q), pcol(OFF_K), pcol(OFF_V), pl.BlockSpec((t, hd), lambda h: (0, OFF_BA // hd)),
                  wcol(0), wcol(GDN_WIDTH), wcol(2 * GDN_WIDTH), vec, vec],
        out_specs=[out] * 5, out_shape=[_sds((t, GDN_WIDTH), f32)] * 5,
        compiler_params=_cparams(("parallel",)))(p, p, p, p, conv_w, conv_w, conv_w, alog, dtb)


def _gdn_pre_bwd(p, conv_w, alog, dtb, dq, dk, dv, dbe, dge):
    t = p.shape[0]
    hd = HEAD_DIM

    def body(pq_ref, pk_ref, pv_ref, ba_ref, wq_ref, wk_ref, wv_ref, al_ref, dt_ref,
             dq_ref, dk_ref, dv_ref, dbe_ref, dge_ref,
             opq_ref, opk_ref, opv_ref, dba_ref, dwq_ref, dwk_ref, dwv_ref, dal_ref, ddt_ref):
        h = pl.program_id(0)
        pq, pk, pv = pq_ref[...], pk_ref[...], pv_ref[...]
        cq = _conv_fwd(pq, wq_ref, GDN_CONV)
        ck = _conv_fwd(pk, wk_ref, GDN_CONV)
        cv = _conv_fwd(pv, wv_ref, GDN_CONV)
        _, vjp = jax.vjp(functools.partial(_gdn_pre_fn, h=h), cq, ck, cv, ba_ref[...], al_ref[...], dt_ref[...])
        dcq, dck, dcv, dba, dal, ddt = vjp((dq_ref[...], dk_ref[...], dv_ref[...], dbe_ref[...], dge_ref[...]))
        opq_ref[...] = _conv_bwd(pq, dcq, wq_ref, dwq_ref, GDN_CONV).astype(bf16)
        opk_ref[...] = _conv_bwd(pk, dck, wk_ref, dwk_ref, GDN_CONV).astype(bf16)
        opv_ref[...] = _conv_bwd(pv, dcv, wv_ref, dwv_ref, GDN_CONV).astype(bf16)

        @pl.when(h == 0)
        def _():
            dba_ref[...] = jnp.zeros_like(dba_ref)
            dal_ref[...] = jnp.zeros_like(dal_ref)
            ddt_ref[...] = jnp.zeros_like(ddt_ref)

        dba_ref[...] += dba
        dal_ref[...] += dal
        ddt_ref[...] += ddt

    def pcol(off):
        return pl.BlockSpec((t, hd), lambda h: (0, h + off // hd))

    def wcol(off):
        return pl.BlockSpec((GDN_CONV, hd), lambda h: (0, h + off // hd))

    vec = pl.BlockSpec((1, hd), lambda h: (0, 0))
    col = pl.BlockSpec((t, hd), lambda h: (0, h))
    full = pl.BlockSpec((t, hd), lambda h: (0, 0))
    wout = pl.BlockSpec((GDN_CONV, hd), lambda h: (0, h))
    return pl.pallas_call(
        body, name="gdn_pre_bwd", grid=(HEADS,),
        in_specs=[pcol(OFF_Q), pcol(OFF_K), pcol(OFF_V), pl.BlockSpec((t, hd), lambda h: (0, OFF_BA // hd)),
                  wcol(0), wcol(GDN_WIDTH), wcol(2 * GDN_WIDTH), vec, vec, col, col, col, col, col],
        out_specs=[col, col, col, full, wout, wout, wout, vec, vec],
        out_shape=[_sds((t, GDN_WIDTH), bf16)] * 3 + [_sds((t, hd), f32)] + [_sds((GDN_CONV, GDN_WIDTH), f32)] * 3
        + [_sds((1, hd), f32)] * 2,
        compiler_params=_cparams(("arbitrary",)))(p, p, p, p, conv_w, conv_w, conv_w, alog, dtb, dq, dk, dv, dbe, dge)


BNN = (((2,), (1,)), ((0,), (0,)))
BNT = (((2,), (2,)), ((0,), (0,)))
BTN = (((1,), (1,)), ((0,), (0,)))


def _hdot(a, b, dn=BNN, precision=None):
    return lax.dot_general(a, b, dn, precision=precision, preferred_element_type=f32)


def _hbdot(a, b, dn=BNN):
    return _hdot(a.astype(bf16), b.astype(bf16), dn)


def _tri_inverse(a):
    c = a.shape[-1]
    r = lax.broadcasted_iota(jnp.int32, (c, c), 0)
    col = lax.broadcasted_iota(jnp.int32, (c, c), 1)
    m = -a
    inv = jnp.where(r == col, 1.0, 0.0) + m
    s = 2
    while s < c:
        m = _hdot(m, m, precision=HI)
        inv = inv + _hdot(inv, m, precision=HI)
        s *= 2
    return inv


@jax.custom_vjp
def _saved_inverse(a, inv):
    return inv


def _saved_inverse_fwd(a, inv):
    return inv, inv


def _saved_inverse_bwd(inv, dinv):
    return -_hdot(_hdot(inv, dinv, BTN, precision=HI), inv, BNT, precision=HI), jnp.zeros_like(inv)


_saved_inverse.defvjp(_saved_inverse_fwd, _saved_inverse_bwd)


def _gdn_chunk(s, q, k, v, ge, be, tinv=None):
    nh, c, _ = q.shape
    r = lax.broadcasted_iota(jnp.int32, (c, c), 0)
    col = lax.broadcasted_iota(jnp.int32, (c, c), 1)
    causal = r >= col
    tri = jnp.broadcast_to(causal.astype(f32), (nh, c, c))
    gc = _hdot(tri, ge, precision=HI)
    gcc = gc[:, :, :c]
    gcr = jnp.swapaxes(gc, 1, 2)[:, :c, :]
    decay = jnp.where(causal, jnp.exp(jnp.where(causal, gcc - gcr, 0.0)), 0.0)
    kb = k * be
    lower = jnp.where(r > col, _hbdot(kb, k, BNT) * decay, 0.0)
    tinv = _tri_inverse(lower) if tinv is None else _saved_inverse(lower, tinv)
    egc = jnp.exp(gc)
    u = _hdot(tinv, v * be, precision=HI)
    w = _hdot(tinv, kb * egc, precision=HI)
    attn = _hbdot(q, k, BNT) * decay
    gl = gc[:, c - 1:c, :]
    v_new = u - _hbdot(w, s)
    o = _hbdot(q * egc, s) + _hbdot(attn, v_new)
    s_new = s * jnp.exp(gl) + _hbdot(k * jnp.exp(gl - gc), v_new, BTN)
    return o, s_new, tinv


def _heads_major(ref):
    return jnp.stack([ref[:, h * HEAD_DIM:(h + 1) * HEAD_DIM] for h in range(HEADS)])


def _gdn_core_fwd(q, k, v, ge, be):
    t = q.shape[0]
    c, hd = GDN_CHUNK, HEAD_DIM
    n = t // c

    def body(q_ref, k_ref, v_ref, g_ref, b_ref, o_ref, st_ref, ti_ref, s_ref):
        @pl.when(pl.program_id(0) == 0)
        def _():
            s_ref[...] = jnp.zeros_like(s_ref)

        s = s_ref[...]
        st_ref[:, 0] = s
        o, s_new, tinv = _gdn_chunk(s, *[_heads_major(r) for r in (q_ref, k_ref, v_ref, g_ref, b_ref)])
        ti_ref[0] = tinv
        for h in range(HEADS):
            o_ref[:, h * hd:(h + 1) * hd] = o[h]
        s_ref[...] = s_new

    tile = pl.BlockSpec((c, GDN_WIDTH), lambda i: (i, 0))
    return pl.pallas_call(
        body, name="gdn_core_fwd", grid=(n,), in_specs=[tile] * 5,
        out_specs=[tile, pl.BlockSpec((HEADS, 1, hd, hd), lambda i: (0, i, 0, 0)),
                   pl.BlockSpec((1, HEADS, c, c), lambda i: (i, 0, 0, 0))],
        out_shape=[_sds((t, GDN_WIDTH), f32), _sds((HEADS, n, hd, hd), f32), _sds((n, HEADS, c, c), f32)],
        scratch_shapes=[pltpu.VMEM((HEADS, hd, hd), f32)],
        compiler_params=_cparams(("arbitrary",)))(q, k, v, ge, be)


def _gdn_core_bwd(q, k, v, ge, be, states, do):
    t = q.shape[0]
    c, hd = GDN_CHUNK, HEAD_DIM
    n = t // c
    states, tinvs = states

    def body(q_ref, k_ref, v_ref, g_ref, b_ref, st_ref, ti_ref, do_ref, dq_ref, dk_ref, dv_ref, dg_ref, db_ref, ds_ref):
        @pl.when(pl.program_id(0) == 0)
        def _():
            ds_ref[...] = jnp.zeros_like(ds_ref)

        tinv = ti_ref[0]
        _, vjp = jax.vjp(lambda *a: _gdn_chunk(*a, tinv=tinv)[:2], st_ref[:, 0],
                         *[_heads_major(r) for r in (q_ref, k_ref, v_ref, g_ref, b_ref)])
        ds, *dins = vjp((_heads_major(do_ref), ds_ref[...]))
        ds_ref[...] = ds
        for d_ref, d in zip((dq_ref, dk_ref, dv_ref, dg_ref, db_ref), dins):
            for h in range(HEADS):
                d_ref[:, h * hd:(h + 1) * hd] = d[h]

    tile = pl.BlockSpec((c, GDN_WIDTH), lambda i: (n - 1 - i, 0))
    return pl.pallas_call(
        body, name="gdn_core_bwd", grid=(n,),
        in_specs=[tile] * 5 + [pl.BlockSpec((HEADS, 1, hd, hd), lambda i: (0, n - 1 - i, 0, 0)),
                               pl.BlockSpec((1, HEADS, c, c), lambda i: (n - 1 - i, 0, 0, 0)), tile],
        out_specs=[tile] * 5, out_shape=[_sds((t, GDN_WIDTH), f32)] * 5,
        scratch_shapes=[pltpu.VMEM((HEADS, hd, hd), f32)],
        compiler_params=_cparams(("arbitrary",)))(q, k, v, ge, be, states, tinvs, do)


def _post_fn(o, z, gain):
    return _rms(o, gain) * _silu(z)


def _post_fwd(o, p, z_off, gain, name):
    t = o.shape[0]
    hd = HEAD_DIM

    def body(o_ref, z_ref, g_ref, y_ref):
        y_ref[...] = _post_fn(o_ref[...], z_ref[...], g_ref[...]).astype(bf16)

    col = pl.BlockSpec((t, hd), lambda h: (0, h))
    return pl.pallas_call(
        body, name=name, grid=(HEADS,),
        in_specs=[col, pl.BlockSpec((t, hd), lambda h: (0, h + z_off // hd)), pl.BlockSpec((1, hd), lambda h: (0, 0))],
        out_specs=col, out_shape=_sds((t, HEADS * hd), bf16), compiler_params=_cparams(("parallel",)))(o, p, gain)


def _post_bwd(o, p, z_off, gain, dmix, mix_off, name):
    t = o.shape[0]
    hd = HEAD_DIM

    def body(o_ref, z_ref, g_ref, dy_ref, do_ref, dz_ref, dg_ref):
        _, vjp = jax.vjp(_post_fn, o_ref[...], z_ref[...], g_ref[...])
        do, dz, dg = vjp(dy_ref[...])
        do_ref[...] = do
        dz_ref[...] = dz.astype(bf16)

        @pl.when(pl.program_id(0) == 0)
        def _():
            dg_ref[...] = jnp.zeros_like(dg_ref)

        dg_ref[...] += dg

    col = pl.BlockSpec((t, hd), lambda h: (0, h))
    vec = pl.BlockSpec((1, hd), lambda h: (0, 0))
    return pl.pallas_call(
        body, name=name, grid=(HEADS,),
        in_specs=[col, pl.BlockSpec((t, hd), lambda h: (0, h + z_off // hd)), vec,
                  pl.BlockSpec((t, hd), lambda h: (0, h + mix_off // hd))],
        out_specs=[col, col, vec], out_shape=[_sds((t, HEADS * hd), f32), _sds((t, HEADS * hd), bf16), _sds((1, hd), f32)],
        compiler_params=_cparams(("arbitrary",)))(o, p, gain, dmix)


def _hgrn_pre_fn(qb, fb, lbw, layer):
    l0, l1 = lbw[0:1, :], lbw[1:2, :]
    m = jnp.maximum(l0, l1)
    e0, e1 = jnp.exp(l0 - m), jnp.exp(l1 - m)
    p0, p1 = e0 / (e0 + e1), e1 / (e0 + e1)
    lb = (p0 - p0) if layer == 0 else ((p0 + p1) - p0)
    f = lb + (1.0 - lb) * _sigmoid(fb)
    return _silu(qb), 1.0 - f, jnp.log(jnp.maximum(f, F_FLOOR))


def _hgrn_pre_fwd(p, lbw, layer):
    t = p.shape[0]
    tc = HEAD_DIM

    def body(qb_ref, fb_ref, lb_ref, q_ref, k_ref, lf_ref):
        q_ref[...], k_ref[...], lf_ref[...] = _hgrn_pre_fn(qb_ref[...], fb_ref[...], lb_ref[...], layer)

    col = pl.BlockSpec((t, tc), lambda j: (0, j))
    return pl.pallas_call(
        body, name="hgrn_pre_fwd", grid=(GDN_WIDTH // tc,),
        in_specs=[pl.BlockSpec((t, tc), lambda j: (0, j + OFF_QB // tc)), pl.BlockSpec((t, tc), lambda j: (0, j + OFF_FB // tc)),
                  pl.BlockSpec((2, tc), lambda j: (0, j))],
        out_specs=[col] * 3, out_shape=[_sds((t, GDN_WIDTH), f32)] * 3,
        compiler_params=_cparams(("parallel",)))(p, p, lbw)


def _hgrn_pre_bwd(p, lbw, layer, dq, dk, dlf):
    t = p.shape[0]
    tc = HEAD_DIM

    def body(qb_ref, fb_ref, lb_ref, dq_ref, dk_ref, dlf_ref, dqb_ref, dfb_ref, dlb_ref):
        _, vjp = jax.vjp(functools.partial(_hgrn_pre_fn, layer=layer), qb_ref[...], fb_ref[...], lb_ref[...])
        dqb, dfb, dlb = vjp((dq_ref[...], dk_ref[...], dlf_ref[...]))
        dqb_ref[...] = dqb.astype(bf16)
        dfb_ref[...] = dfb.astype(bf16)
        dlb_ref[...] = dlb

    col = pl.BlockSpec((t, tc), lambda j: (0, j))
    lb = pl.BlockSpec((2, tc), lambda j: (0, j))
    return pl.pallas_call(
        body, name="hgrn_pre_bwd", grid=(GDN_WIDTH // tc,),
        in_specs=[pl.BlockSpec((t, tc), lambda j: (0, j + OFF_QB // tc)), pl.BlockSpec((t, tc), lambda j: (0, j + OFF_FB // tc)),
                  lb, col, col, col],
        out_specs=[col, col, lb], out_shape=[_sds((t, GDN_WIDTH), bf16)] * 2 + [_sds((2, GDN_WIDTH), f32)],
        compiler_params=_cparams(("parallel",)))(p, p, lbw, dq, dk, dlf)


def _hgrn_step(st, q, k, lf, v):
    c = HGRN_CHUNK
    nh, rs = q.shape[0], q.shape[1]
    r2 = lax.broadcasted_iota(jnp.int32, (rs, rs), 0)
    c2 = lax.broadcasted_iota(jnp.int32, (rs, rs), 1)
    shift = c.bit_length() - 1
    same_chunk = jnp.right_shift(r2, shift) == jnp.right_shift(c2, shift)
    tri = jnp.broadcast_to(((r2 >= c2) & same_chunk).astype(f32), (nh, rs, rs))
    b_all = _hdot(tri, lf, precision=HI)
    i3 = lax.broadcasted_iota(jnp.int32, (c, c, HEAD_DIM), 0)
    j3 = lax.broadcasted_iota(jnp.int32, (c, c, HEAD_DIM), 1)
    mask = i3 >= j3
    outs = []
    for n in range(q.shape[1] // c):
        sl = slice(n * c, (n + 1) * c)
        qc, kc, vc, b = q[:, sl], k[:, sl], v[:, sl], b_all[:, sl]
        rel = jnp.where(mask, jnp.exp(jnp.where(mask, b[:, :, None, :] - b[:, None, :, :], 0.0)), 0.0)
        scores = jnp.sum(qc[:, :, None, :] * kc[:, None, :, :] * rel, axis=-1)
        bl = b[:, c - 1:c, :]
        o = _hbdot(scores, vc) + _hbdot(qc * jnp.exp(b), st, BNT)
        st = st * jnp.exp(bl) + _hbdot(vc, kc * jnp.exp(bl - b), BTN)
        outs.append(o)
    return jnp.concatenate(outs, axis=1), st


def _hgrn_core_fwd(q, k, lf, p):
    t = q.shape[0]
    hd = HEAD_DIM
    rs = min(HGRN_STEP, t)
    n = t // rs

    def body(q_ref, k_ref, lf_ref, v_ref, o_ref, st_ref, s_ref):
        @pl.when(pl.program_id(0) == 0)
        def _():
            s_ref[...] = jnp.zeros_like(s_ref)

        s = s_ref[...]
        st_ref[:, 0] = s
        o, s_new = _hgrn_step(s, *[_heads_major(r) for r in (q_ref, k_ref, lf_ref, v_ref)])
        for h in range(HEADS):
            o_ref[:, h * hd:(h + 1) * hd] = o[h]
        s_ref[...] = s_new

    tile = pl.BlockSpec((rs, GDN_WIDTH), lambda i: (i, 0))
    return pl.pallas_call(
        body, name="hgrn_core_fwd", grid=(n,),
        in_specs=[tile, tile, tile, pl.BlockSpec((rs, GDN_WIDTH), lambda i: (i, OFF_IB // GDN_WIDTH))],
        out_specs=[tile, pl.BlockSpec((HEADS, 1, hd, hd), lambda i: (0, i, 0, 0))],
        out_shape=[_sds((t, GDN_WIDTH), f32), _sds((HEADS, n, hd, hd), f32)],
        scratch_shapes=[pltpu.VMEM((HEADS, hd, hd), f32)],
        compiler_params=_cparams(("arbitrary",)))(q, k, lf, p)


def _hgrn_core_bwd(q, k, lf, p, states, do):
    t = q.shape[0]
    hd = HEAD_DIM
    rs = min(HGRN_STEP, t)
    n = t // rs

    def body(q_ref, k_ref, lf_ref, v_ref, st_ref, do_ref, dq_ref, dk_ref, dlf_ref, dv_ref, ds_ref):
        @pl.when(pl.program_id(0) == 0)
        def _():
            ds_ref[...] = jnp.zeros_like(ds_ref)

        _, vjp = jax.vjp(_hgrn_step, st_ref[:, 0], *[_heads_major(r) for r in (q_ref, k_ref, lf_ref, v_ref)])
        ds, *dins = vjp((_heads_major(do_ref), ds_ref[...]))
        ds_ref[...] = ds
        for d_ref, d in zip((dq_ref, dk_ref, dlf_ref, dv_ref), dins):
            for h in range(HEADS):
                d_ref[:, h * hd:(h + 1) * hd] = d[h].astype(d_ref.dtype)

    tile = pl.BlockSpec((rs, GDN_WIDTH), lambda i: (n - 1 - i, 0))
    return pl.pallas_call(
        body, name="hgrn_core_bwd", grid=(n,),
        in_specs=[tile, tile, tile, pl.BlockSpec((rs, GDN_WIDTH), lambda i: (n - 1 - i, OFF_IB // GDN_WIDTH)),
                  pl.BlockSpec((HEADS, 1, hd, hd), lambda i: (0, n - 1 - i, 0, 0)), tile],
        out_specs=[tile] * 4, out_shape=[_sds((t, GDN_WIDTH), f32)] * 3 + [_sds((t, GDN_WIDTH), bf16)],
        scratch_shapes=[pltpu.VMEM((HEADS, hd, hd), f32)],
        compiler_params=_cparams(("arbitrary",)))(q, k, lf, p, states, do)


def _row(v):
    return v.reshape(1, -1)


def _anchored(w, row, key):
    tok = w.get(key)
    return row if tok is None else row + tok[0, 0]


def _pad_lanes(v, n=HEAD_DIM):
    return jnp.pad(v.reshape(1, -1), ((0, 0), (0, n - v.shape[-1])))


def _ffn_fwd(x, w, l):
    h = _rms_fwd(x, _anchored(w, _row(w['norm_ffn'][l]), ('fwdf', l)), "ffn_norm")
    u = _mm(h, w['ffn_w_up'][l], tb=True, name="ffn_up")
    a = _ffn_act_fwd(u, w['ffn_conv_w'][l], _row(w['ffn_conv_b'][l]))
    y = _mm(a, w['ffn_w_down'][l], add=x, name="ffn_down")
    return y, (x, h, u)


def _ffn_bwd(saved, w, l, dy, dyb, grads):
    x, h, u = saved
    da = _mm(dyb, w['ffn_w_down'][l], tb=True, name="ffn_down_dx")
    a, dg, dv, dcw, dcb = _ffn_act_bwd(u, w['ffn_conv_w'][l], _anchored(w, _row(w['ffn_conv_b'][l]), ('bwd', l)), da)
    grads['ffn_w_down'][l] = _mm(a, dyb, ta=True, out_dtype=bf16, name="ffn_down_dw")
    du = jnp.concatenate([dg, dv], axis=1)
    grads['ffn_w_up'][l] = _mm(du, h, ta=True, out_dtype=bf16, name="ffn_up_dw")
    dh = _mm(du, w['ffn_w_up'][l], name="ffn_up_dx")
    dx, dxb, dgain = _rms_bwd(x, _row(w['norm_ffn'][l]), dh, dy, "ffn_norm_bwd")
    grads['ffn_conv_w'][l] = dcw
    grads['ffn_conv_b'][l] = dcb[0]
    grads['norm_ffn'][l] = dgain[0]
    return dx, dxb


def _odd_fwd(x, w, l, j):
    h = _rms_fwd(x, _anchored(w, _row(w['norm_mix'][l]), ('fwd', l)), "mix_norm")
    p = _mm(h, w['c_w_in'][j], name="lru_in")
    xc = _col_conv_fwd(p, LRU_WIDTH, w['c_conv_w'][j], _row(w['c_conv_b'][j]), LRU_CONV, 256, "lru_conv_fwd")
    out, hs, a = _lru_fwd(p, xc, w['c_gate_a_w'][j], _row(w['c_gate_a_b'][j]), w['c_gate_x_w'][j],
                          _row(w['c_gate_x_b'][j]), _row(w['c_lambda'][j]))
    y = _mm(out, w['c_w_out'][j], add=x, name="lru_out")
    return y, (x, h, p, xc, out, hs, a)


def _odd_bwd(saved, w, l, j, dy, dyb, grads):
    x, h, p, xc, out, hs, a = saved
    dout = _mm(dyb, w['c_w_out'][j], tb=True, name="lru_out_dx")
    grads['c_w_out'][j] = _mm(out, dyb, ta=True, out_dtype=bf16, name="lru_out_dw")
    dyb_, da, du = _lru_bwd_scan(p, a, hs, dout)
    dxc, dwa, dwx, dba, dbx, dlam = _lru_bwd_gates(xc, da, du, w['c_gate_a_w'][j], _row(w['c_gate_a_b'][j]),
                                                   w['c_gate_x_w'][j], _row(w['c_gate_x_b'][j]), _row(w['c_lambda'][j]))
    dxb_, dcw, dcb = _col_conv_bwd(p, LRU_WIDTH, w['c_conv_w'][j], dxc, LRU_CONV, 256, "lru_conv_bwd")
    dp = jnp.concatenate([dyb_, dxb_], axis=1)
    grads['c_w_in'][j] = _mm(h, dp, ta=True, out_dtype=bf16, name="lru_in_dw")
    dh = _mm(dp, w['c_w_in'][j], tb=True, name="lru_in_dx")
    dx, dxb, dgain = _rms_bwd(x, _row(w['norm_mix'][l]), dh, dy, "mix_norm_bwd")
    grads['c_gate_a_w'][j], grads['c_gate_x_w'][j] = dwa, dwx
    grads['c_gate_a_b'][j], grads['c_gate_x_b'][j], grads['c_lambda'][j] = dba[0], dbx[0], dlam[0]
    grads['c_conv_w'][j], grads['c_conv_b'][j] = dcw, dcb[0]
    grads['norm_mix'][l] = dgain[0]
    return dx, dxb


def _even_fwd(x, w, l, j):
    h = _rms_fwd(x, _anchored(w, _row(w['norm_mix'][l]), ('fwd', l)), "mix_norm")
    p = _mm(h, w['ab_w_in'][j], name="ab_in")
    alog, dtb = _pad_lanes(w['gdn_a_log'][j]), _pad_lanes(w['gdn_dt_bias'][j])
    q, k, v, be, ge = _gdn_pre_fwd(p, w['gdn_conv_w'][j], alog, dtb)
    oa, *sa = _gdn_core_fwd(q, k, v, ge, be)
    ya = _post_fwd(oa, p, OFF_Z, _row(w['gdn_norm'][j]), "gdn_post_fwd")
    qq, kk, lf = _hgrn_pre_fwd(p, w['hgrn_lower_bounds'], j)
    ob, sb = _hgrn_core_fwd(qq, kk, lf, p)
    yb = _post_fwd(ob, p, OFF_GB, _row(w['hgrn_norm'][j]), "hgrn_post_fwd")
    mix = jnp.concatenate([ya, yb], axis=1)
    y = _mm(mix, w['ab_w_out'][j], add=x, name="ab_out")
    return y, (x, h, p, q, k, v, be, ge, oa, sa, qq, kk, lf, ob, sb, mix)


def _even_bwd(saved, w, l, j, dy, dyb, grads):
    x, h, p, q, k, v, be, ge, oa, sa, qq, kk, lf, ob, sb, mix = saved
    alog, dtb = _pad_lanes(w['gdn_a_log'][j]), _pad_lanes(w['gdn_dt_bias'][j])
    dmix = _mm(dyb, w['ab_w_out'][j], tb=True, name="ab_out_dx")
    grads['ab_w_out'][j] = _mm(mix, dyb, ta=True, out_dtype=bf16, name="ab_out_dw")
    doa, dz, dgn = _post_bwd(oa, p, OFF_Z, _anchored(w, _row(w['gdn_norm'][j]), ('bwdm', l)), dmix, 0, "gdn_post_bwd")
    dob, dgb, dhn = _post_bwd(ob, p, OFF_GB, _row(w['hgrn_norm'][j]), dmix, GDN_WIDTH, "hgrn_post_bwd")
    dq, dk, dv, dge, dbe = _gdn_core_bwd(q, k, v, ge, be, sa, doa)
    dpq, dpk, dpv, dba, dwq, dwk, dwv, dal, ddt = _gdn_pre_bwd(p, w['gdn_conv_w'][j], alog, dtb, dq, dk, dv, dbe, dge)
    dqq, dkk, dlf, dib = _hgrn_core_bwd(qq, kk, lf, p, sb, dob)
    dqb, dfb, dlb = _hgrn_pre_bwd(p, w['hgrn_lower_bounds'], j, dqq, dkk, dlf)
    dp = jnp.concatenate([dpq, dpk, dpv, dz, dqb, dfb, dib, dgb, dba.astype(bf16)], axis=1)
    grads['ab_w_in'][j] = _mm(h, dp, ta=True, out_dtype=bf16, name="ab_in_dw")
    dh = _mm(dp, w['ab_w_in'][j], tb=True, name="ab_in_dx")
    dx, dxb, dgain = _rms_bwd(x, _row(w['norm_mix'][l]), dh, dy, "mix_norm_bwd")
    grads['gdn_conv_w'][j] = jnp.concatenate([dwq, dwk, dwv], axis=1)
    grads['gdn_a_log'][j], grads['gdn_dt_bias'][j] = dal[0, :HEADS], ddt[0, :HEADS]
    grads['gdn_norm'][j], grads['hgrn_norm'][j] = dgn[0], dhn[0]
    grads['hgrn_lower_bounds'].append(dlb)
    grads['norm_mix'][l] = dgain[0]
    return dx, dxb


def _ab_permute(w_in):
    pad = jnp.zeros(w_in.shape[:-1] + (AB_PAD - AB_COLS,), w_in.dtype)
    return jnp.concatenate([w_in[..., :2048], w_in[..., 2056:], w_in[..., 2048:2056], pad], axis=-1)


def _ab_unpermute(g):
    return jnp.concatenate([g[..., :2048], g[..., 4096:4104], g[..., 2048:4096]], axis=-1)


def _block_pad(a, axis, nblk, padded):
    axis = axis % a.ndim
    s = a.shape
    a = a.reshape(s[:axis] + (nblk, s[axis] // nblk) + s[axis + 1:])
    pad = [(0, 0)] * a.ndim
    pad[axis + 1] = (0, padded - s[axis] // nblk)
    return jnp.pad(a, pad).reshape(s[:axis] + (nblk * padded,) + s[axis + 1:])


def _block_unpad(a, axis, nblk, width):
    axis = axis % a.ndim
    s = a.shape
    a = a.reshape(s[:axis] + (nblk, s[axis] // nblk) + s[axis + 1:])
    a = lax.slice_in_dim(a, 0, width, axis=axis + 1)
    return a.reshape(s[:axis] + (nblk * width,) + s[axis + 1:])


def _kernel_layout(w):
    w = dict(w)
    w['ab_w_in'] = _ab_permute(w['ab_w_in'])
    w['ffn_w_up'] = jnp.swapaxes(_block_pad(w['ffn_w_up'], 2, N_DEV, FF_PAD), 1, 2)
    w['ffn_w_down'] = _block_pad(w['ffn_w_down'], 1, 4, FF_PAD)
    w['ffn_conv_w'] = _block_pad(w['ffn_conv_w'], 2, 4, FF_PAD)
    w['ffn_conv_b'] = _block_pad(w['ffn_conv_b'], 1, 4, FF_PAD)
    return w


def _natural_grads(g):
    g = dict(g)
    g['ab_w_in'] = _ab_unpermute(g['ab_w_in'])
    g['ffn_w_up'] = _block_unpad(jnp.swapaxes(g['ffn_w_up'], 1, 2), 2, N_DEV, FF_SHARD)
    g['ffn_w_down'] = _block_unpad(g['ffn_w_down'], 1, 4, FF_SHARD)
    g['ffn_conv_w'] = _block_unpad(g['ffn_conv_w'], 2, 4, FF_SHARD)
    g['ffn_conv_b'] = _block_unpad(g['ffn_conv_b'], 1, 4, FF_SHARD)
    return g


def _local_step(x, target, w, fetch=None, push=None):
    grads = {n: [None] * (DEPTH if n in ('norm_mix', 'norm_ffn') or n.startswith('ffn_') else 2)
             for n in WEIGHTS if n not in ('norm_final', 'hgrn_lower_bounds')}
    grads['hgrn_lower_bounds'] = []
    saved = []
    for l in range(DEPTH):
        j = l // 2
        if fetch is not None:
            fetch(l, x, 'mix')
        x, s_mix = (_even_fwd if l % 2 == 0 else _odd_fwd)(x, w, l, j)
        if fetch is not None:
            fetch(l, x, 'ffn')
        x, s_ffn = _ffn_fwd(x, w, l)
        saved.append((s_mix, s_ffn))
    loss, dx, dxb, dgf = _loss_head(x, _row(w['norm_final']), target)
    for l in reversed(range(DEPTH)):
        j = l // 2
        s_mix, s_ffn = saved[l]
        dx, dxb = _ffn_bwd(s_ffn, w, l, dx, dxb, grads)
        if push is not None:
            push(l, 'ffn', {nm: grads[nm].pop(li) for nm, li in _layer_items(l)[-2:]})
        dx, dxb = (_even_bwd if l % 2 == 0 else _odd_bwd)(s_mix, w, l, j, dx, dxb, grads)
        if push is not None:
            push(l, 'mix', {nm: grads[nm].pop(li) for nm, li in _layer_items(l)[:-2]})
    out = {n: jnp.stack(g) for n, g in grads.items() if n != 'hgrn_lower_bounds' and g}
    out['hgrn_lower_bounds'] = grads['hgrn_lower_bounds'][0] + grads['hgrn_lower_bounds'][1]
    out['norm_final'] = dgf[0]
    return loss[0, 0], dx, out


def _position():
    return lax.axis_index("x"), lax.axis_index("y"), lax.axis_index("c")


BLOCK_LAYOUT = {
    'ab_w_in': ((2, D_MODEL, N_DEV * AB_SHARD_PAD), (2, D_MODEL, AB_SHARD_PAD)),
    'ab_w_out': ((2, N_DEV, 128, D_MODEL), (2, 128, D_MODEL)),
    'c_w_in': ((2, D_MODEL, 2 * LRU_WIDTH), (2, D_MODEL, 256)),
    'c_w_out': ((2, N_DEV, 128, D_MODEL), (2, 128, D_MODEL)),
    'c_gate_a_w': ((2, HEADS, N_DEV, 32, LRU_BLOCK), (2, HEADS, 32, LRU_BLOCK)),
    'c_gate_x_w': ((2, HEADS, N_DEV, 32, LRU_BLOCK), (2, HEADS, 32, LRU_BLOCK)),
    'ffn_w_up': ((DEPTH, N_DEV, FF_PAD, D_MODEL), (DEPTH, FF_PAD, D_MODEL)),
    'ffn_w_down': ((DEPTH, 4, FF_PAD, D_MODEL), (DEPTH, FF_ROWS, D_MODEL)),
}


COL_WINDOW = {'ab_w_in': AB_SHARD_PAD, 'c_w_in': 256}


def _block_index(name, p):
    d = 4 * p[0] + 2 * p[1] + p[2]
    if name in COL_WINDOW:
        return (slice(None), pl.ds(pl.multiple_of(d * COL_WINDOW[name], 128), COL_WINDOW[name]))
    if name == 'ffn_w_down':
        return (2 * p[0] + p[1], pl.ds(pl.multiple_of(p[2] * FF_ROWS, 16), FF_ROWS), slice(None))
    if name in ('c_gate_a_w', 'c_gate_x_w'):
        return (slice(None), d)
    return (d,)


def _block_of(name, ref, p, layered=True):
    idx = _block_index(name, p)
    if layered and name in BLOCK_LAYOUT:
        idx = (slice(None),) + idx
    return ref.at[idx]


def _layer_items(l):
    j = l // 2
    mix = ([('ab_w_in', j), ('ab_w_out', j)] if l % 2 == 0 else
           [('c_w_in', j), ('c_w_out', j), ('c_gate_a_w', j), ('c_gate_x_w', j)])
    return mix + [('ffn_w_up', l), ('ffn_w_down', l)]


def _place_own(items, shards, posv, name):
    n = len(items)
    down = [i for i, (nm, _) in enumerate(items) if nm == 'ffn_w_down']
    in_specs, out_specs, out_shapes, operands = [], [], [], []
    for nm, li in items:
        sh = shards[nm]
        shard_shape = sh.shape if li is None else sh.shape[1:]
        z = (0,) * len(shard_shape)
        operands.append(sh)
        in_specs.append(pl.BlockSpec(shard_shape, lambda i, d, q, c, z=z: z) if li is None else
                        pl.BlockSpec((1,) + shard_shape, lambda i, d, q, c, li=li, z=z: (li,) + z))
        out_shapes.append(_sds(BLOCK_LAYOUT[nm][0][1:] if nm in BLOCK_LAYOUT else (N_DEV,) + sh.shape, sh.dtype))
        if nm in COL_WINDOW:
            out_specs.append(pl.BlockSpec(shard_shape, lambda i, d, q, c: (0, d[0])))
        elif nm == 'ffn_w_down':
            out_specs.append(pl.BlockSpec((1,) + shard_shape, lambda i, d, q, c: (q[0], c[0], 0)))
        elif nm in ('c_gate_a_w', 'c_gate_x_w'):
            out_specs.append(pl.BlockSpec((HEADS, 1) + shard_shape[1:], lambda i, d, q, c: (0, d[0], 0, 0)))
        else:
            out_specs.append(pl.BlockSpec((1,) + shard_shape, lambda i, d, q, c, z=z: (d[0],) + z))

    def body(d_ref, q_ref, c_ref, *refs):
        for i, (nm, li) in enumerate(items):
            v = refs[i][...] if li is None else refs[i][0]
            o_ref = refs[n + len(down) + i]
            if nm in COL_WINDOW:
                o_ref[...] = v
            elif nm in ('c_gate_a_w', 'c_gate_x_w'):
                o_ref[:, 0] = v
            else:
                o_ref[0] = v

    zeros = [jnp.zeros(out_shapes[i].shape, out_shapes[i].dtype) for i in down]
    return pl.pallas_call(
        body, name=name, out_shape=out_shapes,
        grid_spec=pltpu.PrefetchScalarGridSpec(
            num_scalar_prefetch=3, grid=(1,), in_specs=in_specs + [pl.BlockSpec(memory_space=pl.ANY)] * len(down),
            out_specs=out_specs),
        input_output_aliases={3 + n + k: i for k, i in enumerate(down)},
        compiler_params=_cparams(("arbitrary",)))(*posv, *operands, *zeros)


def _src_of(shard_ref, li):
    return shard_ref if li is None else shard_ref.at[li]


def _gather_now(items, shards, lands):
    n = len(items)
    srcs = sorted({nm for nm, _ in items})

    def body(*refs):
        ins = dict(zip(srcs, refs[:len(srcs)]))
        outs = refs[len(srcs) + n:len(srcs) + 2 * n]
        send_sems, recv_sems = refs[len(srcs) + 2 * n:]
        x, y, c = _position()
        me, sibling = (x, y, c), (x, y, 1 - c)
        chips = [(1 - x, y), (x, 1 - y), (1 - x, 1 - y)]

        def copy(i, k, block, to, own=False):
            nm, li = items[i]
            dst = _block_of(nm, outs[i], block, layered=False)
            return pltpu.make_async_remote_copy(
                src_ref=_src_of(ins[nm], li) if own else dst, dst_ref=dst, send_sem=send_sems.at[7 * i + k],
                recv_sem=recv_sems.at[7 * i + k], device_id=to, device_id_type=MESH)

        first = []
        for i in range(n):
            first.append(copy(i, 0, me, sibling, own=True))
            first += [copy(i, 1 + j, me, (*chip, c), own=True) for j, chip in enumerate(chips)]
        for cp in first:
            cp.start()
        passed = []
        for j, chip in enumerate(chips):
            for i in range(n):
                copy(i, 1 + j, (*chip, c), me).wait_recv()
                fwd = copy(i, 4 + j, (*chip, c), sibling)
                fwd.start()
                passed.append(fwd)
        for i in range(n):
            copy(i, 0, sibling, me).wait_recv()
        for j, chip in enumerate(chips):
            for i in range(n):
                copy(i, 4 + j, (*chip, 1 - c), me).wait_recv()
        for cp in first + passed:
            cp.wait_send()

    any_spec = pl.BlockSpec(memory_space=pl.ANY)
    return pl.pallas_call(
        body, name="gather_first_layer", out_shape=[_sds(a.shape, a.dtype) for a in lands],
        in_specs=[any_spec] * (len(srcs) + n), out_specs=[any_spec] * n,
        input_output_aliases={len(srcs) + i: i for i in range(n)},
        scratch_shapes=[pltpu.SemaphoreType.DMA((7 * n,)), pltpu.SemaphoreType.DMA((7 * n,))],
    )(*[shards[nm] for nm in srcs], *lands)


FIRST_HOP = (1, 2, 4, 6)


def _lanes(name, land_ref, pos):
    if name == 'ffn_w_down':
        return [(FIRST_HOP, land_ref.at[pl.ds(0, 2), pl.ds(0, 2 * FF_ROWS)])]
    if name in COL_WINDOW:
        return [(FIRST_HOP, land_ref.at[:, pl.ds(0, 4 * COL_WINDOW[name])])]
    if name in ('c_gate_a_w', 'c_gate_x_w'):
        return [(FIRST_HOP, land_ref.at[:, pl.ds(0, 4)])]
    return [(FIRST_HOP, land_ref.at[pl.ds(0, 4)])]


def _n_lanes(items):
    return len(items)


def _gather_forward(items, lands, name):
    n = len(items)

    def body(*refs):
        outs = refs[n:2 * n]
        send_sems, recv_sems = refs[2 * n:]
        x, y, c = _position()
        chips = [(1 - x, y), (x, 1 - y), (1 - x, 1 - y)]
        copies, arrivals = [], []
        for i, (nm, _) in enumerate(items):
            for j, chip in enumerate(chips):
                mine = _block_of(nm, outs[i], (*chip, c), layered=False)
                theirs = _block_of(nm, outs[i], (*chip, 1 - c), layered=False)
                copies.append(pltpu.make_async_remote_copy(
                    src_ref=mine, dst_ref=mine, send_sem=send_sems.at[3 * i + j], recv_sem=recv_sems.at[3 * i + j],
                    device_id=(x, y, 1 - c), device_id_type=MESH))
                arrivals.append(pltpu.make_async_remote_copy(
                    src_ref=theirs, dst_ref=theirs, send_sem=send_sems.at[3 * i + j], recv_sem=recv_sems.at[3 * i + j],
                    device_id=(x, y, 1 - c), device_id_type=MESH))
        for cp in copies:
            cp.start()
        for cp in arrivals:
            cp.wait_recv()
        for cp in copies:
            cp.wait_send()

    any_spec = pl.BlockSpec(memory_space=pl.ANY)
    return pl.pallas_call(
        body, name=name, out_shape=[_sds(a.shape, a.dtype) for a in lands],
        in_specs=[any_spec] * n, out_specs=[any_spec] * n, input_output_aliases={i: i for i in range(n)},
        scratch_shapes=[pltpu.SemaphoreType.DMA((3 * n,)), pltpu.SemaphoreType.DMA((3 * n,))],
    )(*lands)


HBM_SPEC = pl.BlockSpec(memory_space=pltpu.HBM)
SEM_SPEC = pl.BlockSpec(memory_space=pltpu.SEMAPHORE)
SIDE_EFFECT = pltpu.SideEffectType.DATAFLOW_SIDE_EFFECTING


def _gather_start(items, shards, lands, token, name):
    n = len(items)
    srcs = sorted({nm for nm, _ in items})
    ns, nl = len(srcs), _n_lanes(items)

    def body(*refs):
        ins = dict(zip(srcs, refs[:ns]))
        land_refs = refs[ns:ns + n]
        sems = refs[ns + n + 1:ns + n + 1 + 2 * nl]
        x, y, c = _position()
        me = (x, y, c)
        lane = 0
        for i, (nm, li) in enumerate(items):
            for codes, _ in _lanes(nm, land_refs[i], me):
                for k in codes:
                    peer = (1 - x if (k >> 2) & 1 else x, 1 - y if (k >> 1) & 1 else y, 1 - c if k & 1 else c)
                    pltpu.make_async_remote_copy(
                        src_ref=_src_of(ins[nm], li), dst_ref=_block_of(nm, land_refs[i], me, layered=False),
                        send_sem=sems[2 * lane], recv_sem=sems[2 * lane + 1], device_id=peer, device_id_type=MESH).start()
                lane += 1
        refs[-1][...] = jnp.zeros((8, 128), f32)

    hbm = [pltpu.with_memory_space_constraint(a, pltpu.HBM) for a in [shards[nm] for nm in srcs] + list(lands)]
    outs = pl.pallas_call(
        body, name=name,
        out_shape=[pltpu.SemaphoreType.DMA(())] * (2 * nl) + [pltpu.HBM(a.shape, a.dtype) for a in hbm] + [_sds((8, 128), f32)],
        in_specs=[HBM_SPEC] * (ns + n) + [pl.BlockSpec(memory_space=pl.ANY)],
        out_specs=[SEM_SPEC] * (2 * nl) + [HBM_SPEC] * (ns + n) + [pl.BlockSpec(memory_space=pltpu.VMEM)],
        input_output_aliases={i: 2 * nl + i for i in range(ns + n)},
        compiler_params=pltpu.CompilerParams(has_side_effects=SIDE_EFFECT),
    )(*hbm, token)
    return outs[:2 * nl], dict(zip(srcs, outs[2 * nl:2 * nl + ns])), outs[2 * nl + ns:-1], outs[-1]


def _gather_wait(items, sems, shards, lands, after, name):
    n = len(items)
    srcs = sorted(shards)
    ns, nl = len(srcs), _n_lanes(items)

    def body(*refs):
        land_refs = refs[ns:ns + n]
        sem_refs = refs[ns + n:ns + n + 2 * nl]
        x, y, c = _position()
        lane = 0
        for i, (nm, _) in enumerate(items):
            for _, moved in _lanes(nm, land_refs[i], (x, y, c)):
                cp = pltpu.make_async_remote_copy(
                    src_ref=moved, dst_ref=moved, send_sem=sem_refs[2 * lane], recv_sem=sem_refs[2 * lane + 1],
                    device_id=(x, y, 1 - c), device_id_type=MESH)
                cp.wait_send()
                cp.wait_recv()
                lane += 1

    outs = pl.pallas_call(
        body, name=name, out_shape=[pltpu.HBM(shards[nm].shape, shards[nm].dtype) for nm in srcs]
        + [pltpu.HBM(a.shape, a.dtype) for a in lands],
        in_specs=[HBM_SPEC] * (ns + n) + [SEM_SPEC] * (2 * nl) + [pl.BlockSpec(memory_space=pl.ANY)],
        out_specs=[HBM_SPEC] * (ns + n), input_output_aliases={i: i for i in range(ns + n)},
        compiler_params=pltpu.CompilerParams(has_side_effects=SIDE_EFFECT),
    )(*[shards[nm] for nm in srcs], *lands, *sems, after)
    return dict(zip(srcs, outs[:ns])), outs[ns:]


def _exchange_grads(fulls, rep):
    cpos = lax.axis_index("c").astype(jnp.int32).reshape(1)
    pair, rep_pair = _pair_exchange(fulls, rep)
    chip = {nm: _chip_sum(nm, fulls[nm], pair[nm], cpos) for nm in fulls}
    rep_chip = _add_pair(rep, rep_pair, "chip_sum_replicated")
    cross, cross_rep = _cross_exchange(chip, rep_chip)
    return chip, rep_chip, cross, cross_rep


def _pair_exchange(fulls, rep, tag=""):
    names = list(fulls)
    n = len(names)
    shard_shape = {nm: ((fulls[nm].shape[0],) + BLOCK_LAYOUT[nm][1][1:] if nm in BLOCK_LAYOUT else fulls[nm].shape[1:])
                   for nm in names}

    def body(*refs):
        ins = dict(zip(names, refs[:n]))
        rep_ref = refs[n]
        pair = dict(zip(names, refs[n + 1:2 * n + 1]))
        rpair_ref = refs[2 * n + 1]
        send_sems, recv_sems = refs[2 * n + 2:]
        x, y, c = _position()
        sibling = (x, y, 1 - c)
        remote = []
        for i, nm in enumerate(names):
            for q in range(4):
                remote.append(pltpu.make_async_remote_copy(
                    src_ref=_block_of(nm, ins[nm], (q >> 1, q & 1, 1 - c)), dst_ref=pair[nm].at[q],
                    send_sem=send_sems.at[4 * i + q], recv_sem=recv_sems.at[4 * i + q], device_id=sibling,
                    device_id_type=MESH))
        remote.append(pltpu.make_async_remote_copy(
            src_ref=rep_ref, dst_ref=rpair_ref, send_sem=send_sems.at[4 * n], recv_sem=recv_sems.at[4 * n],
            device_id=sibling, device_id_type=MESH))
        for cp in remote:
            cp.start()
        for cp in remote:
            cp.wait_recv()
        for cp in remote:
            cp.wait_send()

    any_spec = pl.BlockSpec(memory_space=pl.ANY)
    four = [_sds((4,) + tuple(shard_shape[nm]), fulls[nm].dtype) for nm in names]
    outs = pl.pallas_call(
        body, name="grad_pair_exchange" + tag, out_shape=four + [_sds(rep.shape, rep.dtype)],
        in_specs=[any_spec] * (n + 1), out_specs=[any_spec] * (n + 1),
        scratch_shapes=[pltpu.SemaphoreType.DMA((4 * n + 1,)), pltpu.SemaphoreType.DMA((4 * n + 1,))],
    )(*[fulls[nm] for nm in names], rep)
    return dict(zip(names, outs[:n])), outs[n]


def _chip_sum(name, full, pair, cpos, tag=""):
    if name in COL_WINDOW:
        width = BLOCK_LAYOUT[name][1][-1]
        rows = full.shape[0] * full.shape[1]
        tr = 512

        def body(c_ref, f_ref, p_ref, o_ref):
            o_ref[0] = (f_ref[...].astype(f32) + p_ref[0].astype(f32)).astype(o_ref.dtype)

        slot = pl.BlockSpec((1, tr, width), lambda q, i, c: (q, i, 0))
        out = pl.pallas_call(
            body, name="chip_sum_" + name + tag, out_shape=_sds((4, rows, width), full.dtype),
            grid_spec=pltpu.PrefetchScalarGridSpec(
                num_scalar_prefetch=1, grid=(4, rows // tr),
                in_specs=[pl.BlockSpec((tr, width), lambda q, i, c: (i, 2 * q + c[0])), slot], out_specs=slot),
            compiler_params=_cparams(("parallel", "parallel")))(
            cpos, full.reshape(rows, N_DEV * width), pair.reshape(4, rows, width))
        return out.reshape(pair.shape)

    if name == 'ffn_w_down':
        f4, p4 = full, pair
        fspec = pl.BlockSpec((full.shape[0], 1, FF_ROWS, D_MODEL), lambda q, c: (0, q, c[0], 0))
    else:
        shard = pair.shape[1:]
        lead = int(np.prod(shard[:-2]))
        f4 = full.reshape((lead, N_DEV) + shard[-2:])
        p4 = pair.reshape((4, lead) + shard[-2:])
        fspec = pl.BlockSpec((lead, 1) + shard[-2:], lambda q, c: (0, 2 * q + c[0], 0, 0))

    def body4(c_ref, f_ref, p_ref, o_ref):
        o_ref[0] = (f_ref[:, 0].astype(f32) + p_ref[0].astype(f32)).astype(o_ref.dtype)

    slot = pl.BlockSpec((1,) + p4.shape[1:], lambda q, c: (q, 0, 0, 0))
    out = pl.pallas_call(
        body4, name="chip_sum_" + name + tag, out_shape=_sds(p4.shape, full.dtype),
        grid_spec=pltpu.PrefetchScalarGridSpec(num_scalar_prefetch=1, grid=(4,), in_specs=[fspec, slot], out_specs=slot),
        compiler_params=_cparams(("parallel",)))(cpos, f4, p4)
    return out.reshape(pair.shape)


def _add_pair(a, b, name):
    shp = a.shape
    r, c = int(np.prod(shp[:-1])), shp[-1]
    tr = _tile(r, (512, 256, 128, 64, 32, 16, 8))

    def body(a_ref, b_ref, o_ref):
        o_ref[...] = (a_ref[...].astype(f32) + b_ref[...].astype(f32)).astype(o_ref.dtype)

    tile = pl.BlockSpec((tr, c), lambda i: (i, 0))
    return pl.pallas_call(body, name=name, grid=(r // tr,), in_specs=[tile, tile], out_specs=tile,
                          out_shape=_sds((r, c), a.dtype), compiler_params=_cparams(("parallel",)))(
        a.reshape(r, c), b.reshape(r, c)).reshape(shp)


def _cross_exchange(chip, rep_chip):
    names = list(chip)
    n = len(names)

    def body(*refs):
        ins = dict(zip(names, refs[:n]))
        rep_ref = refs[n]
        outs = dict(zip(names, refs[2 * n + 2:3 * n + 2]))
        rrep_ref = refs[3 * n + 2]
        send_sems, recv_sems = refs[3 * n + 3:]
        x, y, c = _position()
        mine = 2 * x + y
        copies = []
        for k in range(1, 4):
            px, py = (1 - x if (k >> 1) & 1 else x), (1 - y if k & 1 else y)
            for i, nm in enumerate(names + ['']):
                src = rep_ref if i == n else ins[nm].at[2 * px + py]
                dst = (rrep_ref if i == n else outs[nm]).at[mine]
                copies.append(pltpu.make_async_remote_copy(
                    src_ref=src, dst_ref=dst, send_sem=send_sems.at[3 * i + k - 1], recv_sem=recv_sems.at[3 * i + k - 1],
                    device_id=(px, py, c), device_id_type=MESH))
        for cp in copies:
            cp.start()
        for cp in copies:
            cp.wait_recv()
        for cp in copies:
            cp.wait_send()

    any_spec = pl.BlockSpec(memory_space=pl.ANY)
    shapes = [_sds(chip[nm].shape, chip[nm].dtype) for nm in names] + [_sds((4,) + rep_chip.shape, rep_chip.dtype)]
    zeros = [jnp.zeros(s.shape, s.dtype) for s in shapes]
    outs = pl.pallas_call(
        body, name="grad_cross_exchange", out_shape=shapes,
        in_specs=[any_spec] * (2 * n + 2), out_specs=[any_spec] * (n + 1),
        input_output_aliases={n + 1 + i: i for i in range(n + 1)},
        scratch_shapes=[pltpu.SemaphoreType.DMA((3 * (n + 1),)), pltpu.SemaphoreType.DMA((3 * (n + 1),))],
    )(*[chip[nm] for nm in names], rep_chip, *zeros)
    return dict(zip(names, outs[:n])), outs[n]


def _cross_start(chips, name):
    n = len(chips)

    def body(*refs):
        chip_refs, land_refs = refs[:n], refs[n:2 * n]
        sems = refs[2 * n:4 * n]
        x, y, c = _position()
        mine = 2 * x + y
        for i in range(n):
            for k in range(1, 4):
                px, py = (1 - x if (k >> 1) & 1 else x), (1 - y if k & 1 else y)
                pltpu.make_async_remote_copy(
                    src_ref=chip_refs[i].at[2 * px + py], dst_ref=land_refs[i].at[mine], send_sem=sems[2 * i],
                    recv_sem=sems[2 * i + 1], device_id=(px, py, c), device_id_type=MESH).start()
        refs[-1][...] = jnp.zeros((8, 128), f32)

    hbm = [pltpu.with_memory_space_constraint(a, pltpu.HBM) for a in list(chips) + [jnp.zeros(a.shape, a.dtype) for a in chips]]
    outs = pl.pallas_call(
        body, name=name,
        out_shape=[pltpu.SemaphoreType.DMA(())] * (2 * n) + [pltpu.HBM(a.shape, a.dtype) for a in hbm] + [_sds((8, 128), f32)],
        in_specs=[HBM_SPEC] * (2 * n),
        out_specs=[SEM_SPEC] * (2 * n) + [HBM_SPEC] * (2 * n) + [pl.BlockSpec(memory_space=pltpu.VMEM)],
        input_output_aliases={i: 2 * n + i for i in range(2 * n)},
        compiler_params=pltpu.CompilerParams(has_side_effects=SIDE_EFFECT),
    )(*hbm)
    return outs[:2 * n], outs[2 * n:3 * n], outs[3 * n:4 * n], outs[4 * n]


def _direct_start(names, fulls, name):
    n = len(names)

    def body(*refs):
        full_refs, land_refs = refs[:n], refs[n:2 * n]
        sems = refs[2 * n:4 * n]
        x, y, c = _position()
        me = 4 * x + 2 * y + c
        for i, nm in enumerate(names):
            for k in range(N_DEV):
                peer = (1 - x if (k >> 2) & 1 else x, 1 - y if (k >> 1) & 1 else y, 1 - c if k & 1 else c)
                pltpu.make_async_remote_copy(
                    src_ref=_block_of(nm, full_refs[i], peer), dst_ref=land_refs[i].at[me], send_sem=sems[2 * i],
                    recv_sem=sems[2 * i + 1], device_id=peer, device_id_type=MESH).start()
        refs[-1][...] = jnp.zeros((8, 128), f32)

    lands = [jnp.zeros((N_DEV, 1) + BLOCK_LAYOUT[nm][1][1:], a.dtype) for nm, a in zip(names, fulls)]
    hbm = [pltpu.with_memory_space_constraint(a, pltpu.HBM) for a in list(fulls) + lands]
    outs = pl.pallas_call(
        body, name=name,
        out_shape=[pltpu.SemaphoreType.DMA(())] * (2 * n) + [pltpu.HBM(a.shape, a.dtype) for a in hbm] + [_sds((8, 128), f32)],
        in_specs=[HBM_SPEC] * (2 * n),
        out_specs=[SEM_SPEC] * (2 * n) + [HBM_SPEC] * (2 * n) + [pl.BlockSpec(memory_space=pltpu.VMEM)],
        input_output_aliases={i: 2 * n + i for i in range(2 * n)},
        compiler_params=pltpu.CompilerParams(has_side_effects=SIDE_EFFECT),
    )(*hbm)
    return outs[:2 * n], outs[2 * n:3 * n], outs[3 * n:4 * n], outs[4 * n]


def _cross_wait(sems, chips, lands, after, name, nslots=3):
    n = len(chips)

    def body(*refs):
        land_refs = refs[n:2 * n]
        sem_refs = refs[2 * n:4 * n]
        x, y, c = _position()
        for i in range(n):
            moved = land_refs[i].at[pl.ds(0, nslots)]
            cp = pltpu.make_async_remote_copy(
                src_ref=moved, dst_ref=moved, send_sem=sem_refs[2 * i], recv_sem=sem_refs[2 * i + 1],
                device_id=(x, y, 1 - c), device_id_type=MESH)
            cp.wait_send()
            cp.wait_recv()

    outs = pl.pallas_call(
        body, name=name, out_shape=[pltpu.HBM(a.shape, a.dtype) for a in list(chips) + list(lands)],
        in_specs=[HBM_SPEC] * (2 * n) + [SEM_SPEC] * (2 * n) + [pl.BlockSpec(memory_space=pl.ANY)],
        out_specs=[HBM_SPEC] * (2 * n), input_output_aliases={i: i for i in range(2 * n)},
        compiler_params=pltpu.CompilerParams(has_side_effects=SIDE_EFFECT),
    )(*chips, *lands, *sems, after)
    return outs[:n], outs[n:]


def _sum_adamw_layer(parts, own, mine, w, m, v, li, prev, name):
    nl, r, l = w.shape
    lp = parts.shape[2]
    tr = r if r <= 512 else _tile(r, (512, FF_ROWS, 256, 128))
    c1 = 1.0 / (1.0 - ADAM_B1 ** ADAM_STEP)
    c2 = 1.0 / (1.0 - ADAM_B2 ** ADAM_STEP)
    k = 0 if prev is None else 4

    def body(mine_ref, p_ref, o_ref, w_ref, m_ref, v_ref, *rest):
        g_ref, d_ref, nm_ref, nv_ref = rest[k:]
        if own is None:
            g = p_ref[0].astype(f32)
            for s in range(1, parts.shape[0]):
                g = g + p_ref[s].astype(f32)
        else:
            mine_v = o_ref[0].astype(f32)
            g = jnp.where(mine_ref[0] == 0, mine_v, p_ref[0].astype(f32))
            for s in range(1, 4):
                g = g + jnp.where(mine_ref[0] == s, mine_v, p_ref[s].astype(f32))
        if lp != l:
            g = g[:, :l]
        m_new = ADAM_B1 * m_ref[0] + (1.0 - ADAM_B1) * g
        v_new = ADAM_B2 * v_ref[0] + (1.0 - ADAM_B2) * (g * g)
        g_ref[0] = g
        nm_ref[0] = m_new
        nv_ref[0] = v_new
        d_ref[0] = -ADAM_LR * ((m_new * c1) / (jnp.sqrt(v_new * c2) + ADAM_EPS) + ADAM_WD * w_ref[0])

    tile = pl.BlockSpec((1, tr, l), lambda i, mn: (li, i, 0))
    keep = [pl.BlockSpec(memory_space=pl.ANY)] * k
    return pl.pallas_call(
        body, name=name, out_shape=[_sds((nl, r, l), f32)] * 4,
        grid_spec=pltpu.PrefetchScalarGridSpec(
            num_scalar_prefetch=1, grid=(r // tr,),
            in_specs=[pl.BlockSpec((parts.shape[0], tr, lp), lambda i, mn: (0, i, 0)),
                      pl.BlockSpec((1, tr, lp), lambda i, mn: (mn[0], i, 0)), tile, tile, tile] + keep,
            out_specs=[tile] * 4),
        input_output_aliases={6 + i: i for i in range(k)},
        compiler_params=_cparams(("parallel",)))(mine, parts, parts if own is None else own, w, m, v, *(prev or ()))


def _sum_adamw(parts, own, mine, w, m, v, name):
    r, l = w.shape
    lp = parts.shape[2]
    tr = _tile(r, (256, 128, 64, 32, 16, 8))
    c1 = 1.0 / (1.0 - ADAM_B1 ** ADAM_STEP)
    c2 = 1.0 / (1.0 - ADAM_B2 ** ADAM_STEP)

    def body(mine_ref, p_ref, o_ref, w_ref, m_ref, v_ref, g_ref, d_ref, nm_ref, nv_ref):
        mine_v = (o_ref[0] if own.ndim == 3 else o_ref[...]).astype(f32)
        g = jnp.where(mine_ref[0] == 0, mine_v, p_ref[0].astype(f32))
        for s in range(1, parts.shape[0]):
            g = g + jnp.where(mine_ref[0] == s, mine_v, p_ref[s].astype(f32))
        if lp != l:
            g = g[:, :l]
        m_new = ADAM_B1 * m_ref[...] + (1.0 - ADAM_B1) * g
        v_new = ADAM_B2 * v_ref[...] + (1.0 - ADAM_B2) * (g * g)
        g_ref[...] = g
        nm_ref[...] = m_new
        nv_ref[...] = v_new
        d_ref[...] = -ADAM_LR * ((m_new * c1) / (jnp.sqrt(v_new * c2) + ADAM_EPS) + ADAM_WD * w_ref[...])

    tile = pl.BlockSpec((tr, l), lambda i, mn: (i, 0))
    own_spec = (pl.BlockSpec((1, tr, lp), lambda i, mn: (mn[0], i, 0)) if own.ndim == 3
                else pl.BlockSpec((tr, lp), lambda i, mn: (i, 0)))
    return pl.pallas_call(
        body, name=name, out_shape=[_sds((r, l), f32)] * 4,
        grid_spec=pltpu.PrefetchScalarGridSpec(
            num_scalar_prefetch=1, grid=(r // tr,),
            in_specs=[pl.BlockSpec((parts.shape[0], tr, lp), lambda i, mn: (0, i, 0)), own_spec, tile, tile, tile],
            out_specs=[tile] * 4),
        compiler_params=_cparams(("parallel",)))(mine, parts, own, w, m, v)


def _pack(arrs, lead=None):
    if lead is None:
        flat = jnp.concatenate([a.reshape(-1).astype(f32) for a in arrs])
        n = flat.shape[0]
    else:
        flat = jnp.concatenate([a.reshape(lead, -1).astype(f32) for a in arrs], axis=1)
        n = flat.shape[1]
    tot = -(-n // 1024) * 1024
    if lead is None:
        return jnp.pad(flat, (0, tot - n)).reshape(tot // 128, 128)
    return jnp.pad(flat, ((0, 0), (0, tot - n))).reshape(lead, tot // 128, 128)


def _unpack(packed, shapes, lead=False):
    flat = packed.reshape(packed.shape[0], -1) if lead else packed.reshape(-1)
    out, off = [], 0
    for s in shapes:
        n = int(np.prod(s))
        out.append(flat[:, off:off + n].reshape((packed.shape[0],) + tuple(s)) if lead else flat[off:off + n].reshape(s))
        off += n
    return out


def _merge_shards(g, axis):
    g = jnp.moveaxis(g, 0, axis)
    s = g.shape
    return g.reshape(s[:axis] + (s[axis] * s[axis + 1],) + s[axis + 2:])


def _split_shards(full, axis):
    s = full.shape
    g = full.reshape(s[:axis] + (N_DEV, s[axis] // N_DEV) + s[axis + 1:])
    return jnp.moveaxis(g, axis, 0)


def kernel(x, norm_mix, norm_ffn, norm_final, ab_w_in, gdn_conv_w, gdn_a_log, gdn_dt_bias, gdn_norm, hgrn_lower_bounds, hgrn_norm, ab_w_out, c_w_in, c_conv_w, c_conv_b, c_gate_a_w, c_gate_a_b, c_gate_x_w, c_gate_x_b, c_lambda, c_w_out, ffn_w_up, ffn_conv_w, ffn_conv_b, ffn_w_down, loss_target, m_norm_mix, m_norm_ffn, m_norm_final, m_ab_w_in, m_gdn_conv_w, m_gdn_a_log, m_gdn_dt_bias, m_gdn_norm, m_hgrn_lower_bounds, m_hgrn_norm, m_ab_w_out, m_c_w_in, m_c_conv_w, m_c_conv_b, m_c_gate_a_w, m_c_gate_a_b, m_c_gate_x_w, m_c_gate_x_b, m_c_lambda, m_c_w_out, m_ffn_w_up, m_ffn_conv_w, m_ffn_conv_b, m_ffn_w_down, v_norm_mix, v_norm_ffn, v_norm_final, v_ab_w_in, v_gdn_conv_w, v_gdn_a_log, v_gdn_dt_bias, v_gdn_norm, v_hgrn_lower_bounds, v_hgrn_norm, v_ab_w_out, v_c_w_in, v_c_conv_w, v_c_conv_b, v_c_gate_a_w, v_c_gate_a_b, v_c_gate_x_w, v_c_gate_x_b, v_c_lambda, v_c_w_out, v_ffn_w_up, v_ffn_conv_w, v_ffn_conv_b, v_ffn_w_down):
    wl = dict(zip(WEIGHTS, (norm_mix, norm_ffn, norm_final, ab_w_in, gdn_conv_w, gdn_a_log, gdn_dt_bias, gdn_norm, hgrn_lower_bounds, hgrn_norm, ab_w_out, c_w_in, c_conv_w, c_conv_b, c_gate_a_w, c_gate_a_b, c_gate_x_w, c_gate_x_b, c_lambda, c_w_out, ffn_w_up, ffn_conv_w, ffn_conv_b, ffn_w_down)))
    ml = dict(zip(WEIGHTS, (m_norm_mix, m_norm_ffn, m_norm_final, m_ab_w_in, m_gdn_conv_w, m_gdn_a_log, m_gdn_dt_bias, m_gdn_norm, m_hgrn_lower_bounds, m_hgrn_norm, m_ab_w_out, m_c_w_in, m_c_conv_w, m_c_conv_b, m_c_gate_a_w, m_c_gate_a_b, m_c_gate_x_w, m_c_gate_x_b, m_c_lambda, m_c_w_out, m_ffn_w_up, m_ffn_conv_w, m_ffn_conv_b, m_ffn_w_down)))
    vl = dict(zip(WEIGHTS, (v_norm_mix, v_norm_ffn, v_norm_final, v_ab_w_in, v_gdn_conv_w, v_gdn_a_log, v_gdn_dt_bias, v_gdn_norm, v_hgrn_lower_bounds, v_hgrn_norm, v_ab_w_out, v_c_w_in, v_c_conv_w, v_c_conv_b, v_c_gate_a_w, v_c_gate_a_b, v_c_gate_x_w, v_c_gate_x_b, v_c_lambda, v_c_w_out, v_ffn_w_up, v_ffn_conv_w, v_ffn_conv_b, v_ffn_w_down)))

    big = [n for n in SHARDED if n in MATMUL_WEIGHTS]
    vec = [n for n in SHARDED if n not in MATMUL_WEIGHTS]
    shards = {n: wl[n].astype(bf16) for n in big}
    shards['ab_w_in'] = jnp.pad(shards['ab_w_in'], ((0, 0), (0, 0), (0, AB_SHARD_PAD - AB_SHARD)))
    shards['ffn_w_up'] = jnp.pad(jnp.swapaxes(shards['ffn_w_up'], 1, 2), ((0, 0), (0, FF_PAD - FF_SHARD), (0, 0)))
    shards['vec'] = _pack([wl[n] for n in vec])
    pos = _position()
    layer_items = [_layer_items(l) for l in range(DEPTH)]
    posv = [v.astype(jnp.int32).reshape(1) for v in (4 * pos[0] + 2 * pos[1] + pos[2], 2 * pos[0] + pos[1], pos[2])]
    first_items = layer_items[0] + [('vec', None)]
    lands = [_place_own(first_items, shards, posv, "place_own_0")]
    lands += [_place_own(layer_items[l], shards, posv, "place_own_%d" % l) for l in range(1, DEPTH)]
    mix0, ffn0 = layer_items[0][:-2], layer_items[0][-2:]
    first = _gather_now(mix0 + [('vec', None)], shards, lands[0][:len(mix0)] + lands[0][-1:])
    flight = {'shards': {n: shards[n] for n in big}}

    full = {n: wl[n] for n in REPLICATED}

    def start(items, item_lands, token, name, anchor):
        sems, thru, flight['lands'], tok = _gather_start(items, flight['shards'], item_lands, token, name)
        flight['sems'] = list(sems)
        flight['shards'].update(thru)
        full[anchor] = tok

    start(ffn0, lands[0][len(mix0):-1], first[-1], "gather_start_0", ('fwd', 0))

    for n, a in zip(vec, _unpack(first[-1], [wl[n].shape for n in vec], lead=True)):
        full[n] = _merge_shards(a, SHARD_AXIS[n])
    full['ffn_conv_w'] = _block_pad(full['ffn_conv_w'], 2, 4, FF_PAD)
    full['ffn_conv_b'] = _block_pad(full['ffn_conv_b'], 1, 4, FF_PAD)
    for n in big:
        full[n] = {}

    def arrive(items, x_in, tag):
        flight['shards'], got = _gather_wait(items, flight['sems'], flight['shards'], flight['lands'], x_in,
                                             "gather_wait_" + tag)
        return _gather_forward(items, got, "gather_forward_" + tag)

    def fetch(l, x_in, part):
        if l == 0 and part == 'mix':
            items, got = mix0, first[:len(mix0)]
        elif l == 0:
            items, got = ffn0, arrive(ffn0, x_in, "0")
            start(layer_items[1], lands[1], got[0], "gather_start_1", ('fwdf', 0))
        elif part == 'mix':
            items = layer_items[l]
            got = arrive(items, x_in, str(l))
            if l + 1 < DEPTH:
                start(layer_items[l + 1], lands[l + 1], got[0], "gather_start_%d" % (l + 1), ('fwd', l))
        else:
            return
        for (nm, li), a in zip(items, got):
            if nm == 'ab_w_in':
                a = _ab_permute(_block_unpad(a, 1, N_DEV, AB_SHARD))
            elif nm in ('ab_w_out', 'c_w_out'):
                a = a.reshape(D_MODEL, D_MODEL)
            elif nm in ('c_gate_a_w', 'c_gate_x_w'):
                a = a.reshape(HEADS, LRU_BLOCK, LRU_BLOCK)
            elif nm == 'ffn_w_down':
                a = a.reshape(D_FFP, D_MODEL)
            elif nm == 'ffn_w_up':
                a = a.reshape(2 * D_FFP, D_MODEL)
            full[nm][li] = a

    cpos = lax.axis_index("c").astype(jnp.int32).reshape(1)
    mine = (2 * lax.axis_index("x") + lax.axis_index("y")).astype(jnp.int32).reshape(1)
    pending = {}

    def exchange(key, items, g, anchor):
        fulls = {}
        for nm, _ in items:
            a = g[nm].astype(bf16)
            if nm == 'ab_w_in':
                a = _block_pad(_ab_unpermute(a), 1, N_DEV, AB_SHARD_PAD)
            fulls[nm] = a.reshape((1,) + BLOCK_LAYOUT[nm][0][1:])
        sems, thru, crosses, tok = _direct_start(list(fulls), list(fulls.values()), "grad_direct_start_" + key)
        pending[key] = (items, sems, thru, crosses)
        full[anchor] = tok

    stash = {}

    def push(l, part, g):
        if l == 0:
            exchange("0f" if part == 'ffn' else "0", ffn0 if part == 'ffn' else mix0, g,
                     ('bwdm', 0) if part == 'ffn' else ('bwd', -1))
            return
        stash.update(g)
        if part == 'mix':
            exchange(str(l), layer_items[l], dict(stash), ('bwd', l - 1))
            stash.clear()

    loss, dx, grads = _local_step(x[0], loss_target[0], full, fetch, push)

    grads['ffn_conv_w'] = _block_unpad(grads['ffn_conv_w'], 2, 4, FF_SHARD)
    grads['ffn_conv_b'] = _block_unpad(grads['ffn_conv_b'], 1, 4, FF_SHARD)
    fulls = {'vec': _pack([_split_shards(grads[n], SHARD_AXIS[n]) for n in vec], lead=N_DEV)}
    chip, rep_chip, recv, rrep = _exchange_grads(fulls, _pack([grads[n] for n in REPLICATED]))
    res = {}
    stacked = {}
    after = full['bwd', -1]
    for key in ("3", "2", "1", "0f", "0"):
        items, sems, chips, lands = pending[key]
        chips, lands = _cross_wait(sems, chips, lands, after, "grad_direct_wait_" + key, nslots=N_DEV)
        for (n, li), parts in zip(items, lands):
            if n == 'ffn_w_up':
                wmv = [jnp.swapaxes(a[n], 1, 2) for a in (wl, ml, vl)]
                rp = FF_PAD
            else:
                shp = wl[n].shape
                rp = int(np.prod(shp[1:-1]))
                wmv = [a[n].reshape(shp[0], rp, shp[-1]) for a in (wl, ml, vl)]
            stacked[n] = _sum_adamw_layer(parts.reshape(N_DEV, rp, -1), None, mine, *wmv, li,
                                          stacked.get(n), "adamw_%s_%d" % (n, li))
        if key == "0f":
            after = stacked['ffn_w_up'][0]
    for n in big:
        for kind, o in zip(("grad", "delta", "new_m", "new_v"), stacked[n]):
            res[kind, n] = jnp.swapaxes(o, 1, 2) if n == 'ffn_w_up' else o.reshape(wl[n].shape)
    for names, parts, own, tag in ((vec, recv['vec'], chip['vec'], "adamw_vectors"),
                                   (REPLICATED, rrep, rep_chip, "adamw_replicated")):
        outs = _sum_adamw(parts, own, mine, _pack([wl[n] for n in names]), _pack([ml[n] for n in names]),
                          _pack([vl[n] for n in names]), tag)
        for kind, o in zip(("grad", "delta", "new_m", "new_v"), outs):
            for n, a in zip(names, _unpack(o, [wl[n].shape for n in names])):
                res[kind, n] = a

    loss = lax.psum(loss, ("x", "y", "c"))
    return (loss, dx[None], *[res[kind, n] for kind in ("grad", "delta", "new_m", "new_v") for n in WEIGHTS])
```
